```python
import math
import jax, jax.numpy as jnp
from jax import lax
import numpy as np

D_MODEL = 1024
BATCH = 8
SEQ = 2048
DEPTH = 2

GRID_W = 64
NA_HEADS = 8
NA_HEAD_DIM = 64
NA_WIN_ROWS = 8
NA_WIN_COLS = 16
NA_QCOL_BLOCK = NA_WIN_COLS
NA_KCOL_BLOCK = 2 * NA_WIN_COLS
SW_HEADS = 8
SW_KV_HEADS = 2
SW_HEAD_DIM = 64
SW_WINDOW = 128
SW_BLOCK = 128
REL_BUCKETS = 32
REL_MAX_DIST = 128
D_FF = 2816
N_BRANCHES = 2
EPS = 1e-6
NEG = -1e30

NA_WIDTH = NA_HEADS * NA_HEAD_DIM
SW_Q_WIDTH = SW_HEADS * SW_HEAD_DIM
SW_KV_WIDTH = SW_KV_HEADS * SW_HEAD_DIM
IN_WIDTH = 3 * NA_WIDTH + SW_Q_WIDTH + 2 * SW_KV_WIDTH + N_BRANCHES * D_MODEL

kernel_name = "hybrid_natten_swa_macaron_encoder"


def rms_norm(x, g):
    xf = x.astype(jnp.float32)
    y = xf * lax.rsqrt(jnp.mean(xf * xf, axis=-1, keepdims=True) + EPS)
    return (y * g.astype(jnp.float32)).astype(x.dtype)


def swiglu(x, w_gate, w_up, w_down):
    return (jax.nn.silu(x @ w_gate) * (x @ w_up)) @ w_down


def t5_bucket(rel):
    nb = REL_BUCKETS // 2
    max_exact = nb // 2
    n = np.abs(rel)
    large = max_exact + (np.log(np.maximum(n, 1) / max_exact)
                         / np.log(REL_MAX_DIST / max_exact) * (nb - max_exact)).astype(np.int32)
    large = np.minimum(large, nb - 1)
    return ((rel > 0) * nb + np.where(n < max_exact, n, large)).astype(np.int32)


def neighbourhood_attention(q, k, v, rpb):
    B, S, H, dh = q.shape
    rows = S // GRID_W
    kr = min(NA_WIN_ROWS, rows)
    ncb = GRID_W // NA_QCOL_BLOCK
    r = np.arange(rows)
    row_start = np.clip(r - kr // 2, 0, rows - kr)
    key_rows = row_start[:, None] + np.arange(kr)
    cb = np.arange(ncb)
    kcol_start = np.clip(cb * NA_QCOL_BLOCK - NA_WIN_COLS // 2, 0, GRID_W - NA_KCOL_BLOCK)
    key_cols = kcol_start[:, None] + np.arange(NA_KCOL_BLOCK)
    key_idx = (key_rows[:, None, :, None] * GRID_W + key_cols[None, :, None, :])
    key_idx = key_idx.reshape(rows, ncb, kr * NA_KCOL_BLOCK)
    kg = k[:, key_idx]
    vg = v[:, key_idx]
    qb = q.reshape(B, rows, ncb, NA_QCOL_BLOCK, H, dh)
    s = jnp.einsum('brcqhd,brckhd->bhrcqk', qb, kg).astype(jnp.float32) / math.sqrt(dh)
    q_cols = cb[:, None] * NA_QCOL_BLOCK + np.arange(NA_QCOL_BLOCK)
    q_col_start = np.clip(q_cols - NA_WIN_COLS // 2, 0, GRID_W - NA_WIN_COLS)
    kc = key_cols[:, None, :]
    col_ok = (kc >= q_col_start[..., None]) & (kc < q_col_start[..., None] + NA_WIN_COLS)
    row_idx = key_rows - r[:, None] + NA_WIN_ROWS - 1
    col_idx = np.clip(kc - q_cols[..., None] + NA_WIN_COLS - 1, 0, 2 * NA_WIN_COLS - 2)
    bias = rpb[:, row_idx[:, None, None, :, None], col_idx[None, :, :, None, :]]
    bias = bias.reshape(H, rows, ncb, NA_QCOL_BLOCK, kr * NA_KCOL_BLOCK)
    mask = np.broadcast_to(col_ok[None, :, :, None, :],
                           (rows, ncb, NA_QCOL_BLOCK, kr, NA_KCOL_BLOCK))
    mask = mask.reshape(rows, ncb, NA_QCOL_BLOCK, kr * NA_KCOL_BLOCK)
    s = jnp.where(mask, s + bias.astype(jnp.float32), NEG)
    p = jax.nn.softmax(s, axis=-1)
    o = jnp.einsum('bhrcqk,brckhd->brcqhd', p.astype(v.dtype), vg)
    return o.reshape(B, S, H * dh)


def sliding_window_gqa(q, k, v, rel_bias, sink):
    B, S, _, dh = q.shape
    nb = S // SW_BLOCK
    rep = SW_HEADS // SW_KV_HEADS
    qb = q.reshape(B, nb, SW_BLOCK, SW_KV_HEADS, rep, dh)
    pad = ((0, 0), (SW_BLOCK, SW_BLOCK), (0, 0), (0, 0))
    kp = jnp.pad(k, pad).reshape(B, nb + 2, SW_BLOCK, SW_KV_HEADS, dh)
    vp = jnp.pad(v, pad).reshape(B, nb + 2, SW_BLOCK, SW_KV_HEADS, dh)
    kb = jnp.concatenate([kp[:, :-2], kp[:, 1:-1], kp[:, 2:]], axis=2)
    vb = jnp.concatenate([vp[:, :-2], vp[:, 1:-1], vp[:, 2:]], axis=2)
    s = jnp.einsum('bnqgrd,bnkgd->bgrnqk', qb, kb).astype(jnp.float32) / math.sqrt(dh)
    a = np.arange(SW_BLOCK)[:, None]
    j = np.arange(3 * SW_BLOCK)[None, :]
    rel = j - SW_BLOCK - a
    kpos = (np.arange(nb)[:, None, None] - 1) * SW_BLOCK + j[None]
    mask = (np.abs(rel)[None] <= SW_WINDOW) & (kpos >= 0) & (kpos < S)
    bias = rel_bias.astype(jnp.float32).reshape(SW_KV_HEADS, rep, 1, SW_BLOCK, 3 * SW_BLOCK)
    s = jnp.where(mask, s + bias, NEG)
    sink_b = sink.astype(jnp.float32).reshape(SW_KV_HEADS, rep, 1, 1, 1)
    m = jnp.maximum(s.max(axis=-1, keepdims=True), sink_b)
    e = jnp.exp(s - m)
    p = e / (e.sum(axis=-1, keepdims=True) + jnp.exp(sink_b - m))
    o = jnp.einsum('bgrnqk,bnkgd->bnqgrd', p.astype(v.dtype), vb)
    return o.reshape(B, S, SW_HEADS * dh)


def _fwd_setup_inputs(seed: int = 0) -> dict:
    key = jax.random.key(seed)
    ks = jax.random.split(key, 24)
    L, D, F = DEPTH, D_MODEL, D_FF
    nrm = lambda k, shape, fan: jax.random.normal(k, shape, jnp.float32) * fan ** -0.5
    gain = lambda k, shape: 1.0 + 0.05 * jax.random.normal(k, shape, jnp.float32)
    return {
        "x": jax.random.normal(ks[0], (BATCH, SEQ, D), jnp.float32),
        "ffn1_norm": gain(ks[1], (L, D)),
        "ffn1_w_gate": nrm(ks[2], (L, D, F), D),
        "ffn1_w_up": nrm(ks[3], (L, D, F), D),
        "ffn1_w_down": nrm(ks[4], (L, F, D), F),
        "mix_norm": gain(ks[5], (L, D)),
        "w_in": nrm(ks[6], (L, D, IN_WIDTH), D),
        "b_gate": 0.01 * jax.random.normal(ks[7], (L, N_BRANCHES * D), jnp.float32),
        "na_q_norm": gain(ks[8], (L, NA_HEAD_DIM)),
        "na_k_norm": gain(ks[9], (L, NA_HEAD_DIM)),
        "na_rpb": 0.1 * jax.random.normal(ks[10], (L, NA_HEADS, 2 * NA_WIN_ROWS - 1, 2 * NA_WIN_COLS - 1), jnp.float32),
        "sw_q_norm": gain(ks[11], (L, SW_HEAD_DIM)),
        "sw_k_norm": gain(ks[12], (L, SW_HEAD_DIM)),
        "sw_sink": 0.5 * jax.random.normal(ks[13], (L, SW_HEADS), jnp.float32),
        "t5_rel_table": 0.1 * jax.random.normal(ks[14], (REL_BUCKETS, SW_HEADS), jnp.float32),
        "w_branch_na": nrm(ks[15], (L, NA_WIDTH, D), NA_WIDTH),
        "w_branch_sw": nrm(ks[16], (L, SW_Q_WIDTH, D), SW_Q_WIDTH),
        "w_out": nrm(ks[17], (L, D, D), D),
        "ffn2_norm": gain(ks[18], (L, D)),
        "ffn2_w_gate": nrm(ks[19], (L, D, F), D),
        "ffn2_w_up": nrm(ks[20], (L, D, F), D),
        "ffn2_w_down": nrm(ks[21], (L, F, D), F),
    }


def _fwd_reference(x, ffn1_norm, ffn1_w_gate, ffn1_w_up, ffn1_w_down, mix_norm, w_in, b_gate,
              na_q_norm, na_k_norm, na_rpb, sw_q_norm, sw_k_norm, sw_sink, t5_rel_table,
              w_branch_na, w_branch_sw, w_out, ffn2_norm, ffn2_w_gate, ffn2_w_up, ffn2_w_down):
    B, S, D = x.shape
    rel = np.arange(3 * SW_BLOCK)[None, :] - SW_BLOCK - np.arange(SW_BLOCK)[:, None]
    t5_bias = jnp.transpose(t5_rel_table[t5_bucket(rel)], (2, 0, 1))
    splits = np.cumsum([NA_WIDTH, NA_WIDTH, NA_WIDTH, SW_Q_WIDTH, SW_KV_WIDTH, SW_KV_WIDTH])
    for l in range(DEPTH):
        x = x + 0.5 * swiglu(rms_norm(x, ffn1_norm[l]), ffn1_w_gate[l], ffn1_w_up[l], ffn1_w_down[l])
        h = rms_norm(x, mix_norm[l])
        z = h @ w_in[l]
        qa, ka, va, qs, ks_, vs, zg = jnp.split(z, splits, axis=-1)
        qa = rms_norm(qa.reshape(B, S, NA_HEADS, NA_HEAD_DIM), na_q_norm[l])
        ka = rms_norm(ka.reshape(B, S, NA_HEADS, NA_HEAD_DIM), na_k_norm[l])
        va = va.reshape(B, S, NA_HEADS, NA_HEAD_DIM)
        o_na = neighbourhood_attention(qa, ka, va, na_rpb[l])
        qs = rms_norm(qs.reshape(B, S, SW_HEADS, SW_HEAD_DIM), sw_q_norm[l])
        ks_ = rms_norm(ks_.reshape(B, S, SW_KV_HEADS, SW_HEAD_DIM), sw_k_norm[l])
        vs = vs.reshape(B, S, SW_KV_HEADS, SW_HEAD_DIM)
        o_sw = sliding_window_gqa(qs, ks_, vs, t5_bias, sw_sink[l])
        g = jax.nn.sigmoid((zg + b_gate[l]).astype(jnp.float32)).astype(x.dtype)
        g = g.reshape(B, S, N_BRANCHES, D)
        merged = g[:, :, 0] * (o_na @ w_branch_na[l]) + g[:, :, 1] * (o_sw @ w_branch_sw[l])
        x = x + merged @ w_out[l]
        x = x + 0.5 * swiglu(rms_norm(x, ffn2_norm[l]), ffn2_w_gate[l], ffn2_w_up[l], ffn2_w_down[l])
    return x


import jax as _jax
import jax.numpy as _jnp

TWIN_FORMAT = 'train_step'
FWD_PARAMS = ['x', 'ffn1_norm', 'ffn1_w_gate', 'ffn1_w_up', 'ffn1_w_down', 'mix_norm', 'w_in', 'b_gate', 'na_q_norm', 'na_k_norm', 'na_rpb', 'sw_q_norm', 'sw_k_norm', 'sw_sink', 't5_rel_table', 'w_branch_na', 'w_branch_sw', 'w_out', 'ffn2_norm', 'ffn2_w_gate', 'ffn2_w_up', 'ffn2_w_down']
TWIN_WEIGHTS = ['ffn1_norm', 'ffn1_w_gate', 'ffn1_w_up', 'ffn1_w_down', 'mix_norm', 'w_in', 'b_gate', 'na_q_norm', 'na_k_norm', 'na_rpb', 'sw_q_norm', 'sw_k_norm', 'sw_sink', 't5_rel_table', 'w_branch_na', 'w_branch_sw', 'w_out', 'ffn2_norm', 'ffn2_w_gate', 'ffn2_w_up', 'ffn2_w_down']
TWIN_DIFF_INPUT = 'x'
TWIN_INPUTS = ['x', 'ffn1_norm', 'ffn1_w_gate', 'ffn1_w_up', 'ffn1_w_down', 'mix_norm', 'w_in', 'b_gate', 'na_q_norm', 'na_k_norm', 'na_rpb', 'sw_q_norm', 'sw_k_norm', 'sw_sink', 't5_rel_table', 'w_branch_na', 'w_branch_sw', 'w_out', 'ffn2_norm', 'ffn2_w_gate', 'ffn2_w_up', 'ffn2_w_down', 'loss_target', 'm_ffn1_norm', 'm_ffn1_w_gate', 'm_ffn1_w_up', 'm_ffn1_w_down', 'm_mix_norm', 'm_w_in', 'm_b_gate', 'm_na_q_norm', 'm_na_k_norm', 'm_na_rpb', 'm_sw_q_norm', 'm_sw_k_norm', 'm_sw_sink', 'm_t5_rel_table', 'm_w_branch_na', 'm_w_branch_sw', 'm_w_out', 'm_ffn2_norm', 'm_ffn2_w_gate', 'm_ffn2_w_up', 'm_ffn2_w_down', 'v_ffn1_norm', 'v_ffn1_w_gate', 'v_ffn1_w_up', 'v_ffn1_w_down', 'v_mix_norm', 'v_w_in', 'v_b_gate', 'v_na_q_norm', 'v_na_k_norm', 'v_na_rpb', 'v_sw_q_norm', 'v_sw_k_norm', 'v_sw_sink', 'v_t5_rel_table', 'v_w_branch_na', 'v_w_branch_sw', 'v_w_out', 'v_ffn2_norm', 'v_ffn2_w_gate', 'v_ffn2_w_up', 'v_ffn2_w_down']
TWIN_OUTPUTS = ['loss', 'grad_x', 'grad_ffn1_norm', 'grad_ffn1_w_gate', 'grad_ffn1_w_up', 'grad_ffn1_w_down', 'grad_mix_norm', 'grad_w_in', 'grad_b_gate', 'grad_na_q_norm', 'grad_na_k_norm', 'grad_na_rpb', 'grad_sw_q_norm', 'grad_sw_k_norm', 'grad_sw_sink', 'grad_t5_rel_table', 'grad_w_branch_na', 'grad_w_branch_sw', 'grad_w_out', 'grad_ffn2_norm', 'grad_ffn2_w_gate', 'grad_ffn2_w_up', 'grad_ffn2_w_down', 'delta_ffn1_norm', 'delta_ffn1_w_gate', 'delta_ffn1_w_up', 'delta_ffn1_w_down', 'delta_mix_norm', 'delta_w_in', 'delta_b_gate', 'delta_na_q_norm', 'delta_na_k_norm', 'delta_na_rpb', 'delta_sw_q_norm', 'delta_sw_k_norm', 'delta_sw_sink', 'delta_t5_rel_table', 'delta_w_branch_na', 'delta_w_branch_sw', 'delta_w_out', 'delta_ffn2_norm', 'delta_ffn2_w_gate', 'delta_ffn2_w_up', 'delta_ffn2_w_down', 'new_m_ffn1_norm', 'new_m_ffn1_w_gate', 'new_m_ffn1_w_up', 'new_m_ffn1_w_down', 'new_m_mix_norm', 'new_m_w_in', 'new_m_b_gate', 'new_m_na_q_norm', 'new_m_na_k_norm', 'new_m_na_rpb', 'new_m_sw_q_norm', 'new_m_sw_k_norm', 'new_m_sw_sink', 'new_m_t5_rel_table', 'new_m_w_branch_na', 'new_m_w_branch_sw', 'new_m_w_out', 'new_m_ffn2_norm', 'new_m_ffn2_w_gate', 'new_m_ffn2_w_up', 'new_m_ffn2_w_down', 'new_v_ffn1_norm', 'new_v_ffn1_w_gate', 'new_v_ffn1_w_up', 'new_v_ffn1_w_down', 'new_v_mix_norm', 'new_v_w_in', 'new_v_b_gate', 'new_v_na_q_norm', 'new_v_na_k_norm', 'new_v_na_rpb', 'new_v_sw_q_norm', 'new_v_sw_k_norm', 'new_v_sw_sink', 'new_v_t5_rel_table', 'new_v_w_branch_na', 'new_v_w_branch_sw', 'new_v_w_out', 'new_v_ffn2_norm', 'new_v_ffn2_w_gate', 'new_v_ffn2_w_up', 'new_v_ffn2_w_down']
TWIN_LEAF_KINDS = {'loss': 'loss', 'grad_x': 'grad_x', 'grad_ffn1_norm': 'grad_w', 'grad_ffn1_w_gate': 'grad_w', 'grad_ffn1_w_up': 'grad_w', 'grad_ffn1_w_down': 'grad_w', 'grad_mix_norm': 'grad_w', 'grad_w_in': 'grad_w', 'grad_b_gate': 'grad_w', 'grad_na_q_norm': 'grad_w', 'grad_na_k_norm': 'grad_w', 'grad_na_rpb': 'grad_w', 'grad_sw_q_norm': 'grad_w', 'grad_sw_k_norm': 'grad_w', 'grad_sw_sink': 'grad_w', 'grad_t5_rel_table': 'grad_w', 'grad_w_branch_na': 'grad_w', 'grad_w_branch_sw': 'grad_w', 'grad_w_out': 'grad_w', 'grad_ffn2_norm': 'grad_w', 'grad_ffn2_w_gate': 'grad_w', 'grad_ffn2_w_up': 'grad_w', 'grad_ffn2_w_down': 'grad_w', 'delta_ffn1_norm': 'delta_w', 'delta_ffn1_w_gate': 'delta_w', 'delta_ffn1_w_up': 'delta_w', 'delta_ffn1_w_down': 'delta_w', 'delta_mix_norm': 'delta_w', 'delta_w_in': 'delta_w', 'delta_b_gate': 'delta_w', 'delta_na_q_norm': 'delta_w', 'delta_na_k_norm': 'delta_w', 'delta_na_rpb': 'delta_w', 'delta_sw_q_norm': 'delta_w', 'delta_sw_k_norm': 'delta_w', 'delta_sw_sink': 'delta_w', 'delta_t5_rel_table': 'delta_w', 'delta_w_branch_na': 'delta_w', 'delta_w_branch_sw': 'delta_w', 'delta_w_out': 'delta_w', 'delta_ffn2_norm': 'delta_w', 'delta_ffn2_w_gate': 'delta_w', 'delta_ffn2_w_up': 'delta_w', 'delta_ffn2_w_down': 'delta_w', 'new_m_ffn1_norm': 'new_m', 'new_m_ffn1_w_gate': 'new_m', 'new_m_ffn1_w_up': 'new_m', 'new_m_ffn1_w_down': 'new_m', 'new_m_mix_norm': 'new_m', 'new_m_w_in': 'new_m', 'new_m_b_gate': 'new_m', 'new_m_na_q_norm': 'new_m', 'new_m_na_k_norm': 'new_m', 'new_m_na_rpb': 'new_m', 'new_m_sw_q_norm': 'new_m', 'new_m_sw_k_norm': 'new_m', 'new_m_sw_sink': 'new_m', 'new_m_t5_rel_table': 'new_m', 'new_m_w_branch_na': 'new_m', 'new_m_w_branch_sw': 'new_m', 'new_m_w_out': 'new_m', 'new_m_ffn2_norm': 'new_m', 'new_m_ffn2_w_gate': 'new_m', 'new_m_ffn2_w_up': 'new_m', 'new_m_ffn2_w_down': 'new_m', 'new_v_ffn1_norm': 'new_v', 'new_v_ffn1_w_gate': 'new_v', 'new_v_ffn1_w_up': 'new_v', 'new_v_ffn1_w_down': 'new_v', 'new_v_mix_norm': 'new_v', 'new_v_w_in': 'new_v', 'new_v_b_gate': 'new_v', 'new_v_na_q_norm': 'new_v', 'new_v_na_k_norm': 'new_v', 'new_v_na_rpb': 'new_v', 'new_v_sw_q_norm': 'new_v', 'new_v_sw_k_norm': 'new_v', 'new_v_sw_sink': 'new_v', 'new_v_t5_rel_table': 'new_v', 'new_v_w_branch_na': 'new_v', 'new_v_w_branch_sw': 'new_v', 'new_v_w_out': 'new_v', 'new_v_ffn2_norm': 'new_v', 'new_v_ffn2_w_gate': 'new_v', 'new_v_ffn2_w_up': 'new_v', 'new_v_ffn2_w_down': 'new_v'}


def _forward(args):
    return _fwd_reference(*[args[k] for k in FWD_PARAMS])


def _output_shape():
    out = _jax.eval_shape(lambda: _forward(_fwd_setup_inputs(0)))
    return out.shape, out.dtype

N_MICROBATCH = 1
ADAM_LR = 0.001
ADAM_B1 = 0.9
ADAM_B2 = 0.999
ADAM_EPS = 1e-08
ADAM_WD = 0.01
ADAM_STEP = 10
PER_EXAMPLE_BATCH_AXIS = {'x': 0, 'loss_target': 0}
SHARED_INPUTS = []
_WEIGHT_DTYPES = {'ffn1_norm': _jnp.float32, 'ffn1_w_gate': _jnp.float32, 'ffn1_w_up': _jnp.float32, 'ffn1_w_down': _jnp.float32, 'mix_norm': _jnp.float32, 'w_in': _jnp.float32, 'b_gate': _jnp.float32, 'na_q_norm': _jnp.float32, 'na_k_norm': _jnp.float32, 'na_rpb': _jnp.float32, 'sw_q_norm': _jnp.float32, 'sw_k_norm': _jnp.float32, 'sw_sink': _jnp.float32, 't5_rel_table': _jnp.float32, 'w_branch_na': _jnp.float32, 'w_branch_sw': _jnp.float32, 'w_out': _jnp.float32, 'ffn2_norm': _jnp.float32, 'ffn2_w_gate': _jnp.float32, 'ffn2_w_up': _jnp.float32, 'ffn2_w_down': _jnp.float32}
MOMENT_SCALE = {'ffn1_norm': 3.086419e+00, 'ffn1_w_gate': 4.613723e-02, 'ffn1_w_up': 4.969230e-02, 'ffn1_w_down': 8.042248e-02, 'mix_norm': 1.086648e-01, 'w_in': 2.680968e-02, 'b_gate': 1.709689e-02, 'na_q_norm': 7.412283e-01, 'na_k_norm': 7.409943e-01, 'na_rpb': 1.250870e-02, 'sw_q_norm': 6.266867e-01, 'sw_k_norm': 6.225145e-01, 'sw_sink': 1.787585e-02, 't5_rel_table': 4.688944e-02, 'w_branch_na': 2.847203e-02, 'w_branch_sw': 1.932267e-02, 'w_out': 3.356761e-02, 'ffn2_norm': 3.097813e+00, 'ffn2_w_gate': 4.333927e-02, 'ffn2_w_up': 4.782981e-02, 'ffn2_w_down': 7.711231e-02}


def _to_microbatches(a, axis):
    t = _jnp.moveaxis(a, axis, 0)
    t = t.reshape((N_MICROBATCH, t.shape[0] // N_MICROBATCH) + t.shape[1:])
    return _jnp.moveaxis(t, 1, axis + 1)


def setup_inputs(seed: int = 0) -> dict:
    inp = _fwd_setup_inputs(seed)
    key = _jax.random.fold_in(_jax.random.key(seed), 7919)
    shape, _ = _output_shape()
    out = dict(inp)
    out["loss_target"] = _jax.random.normal(_jax.random.fold_in(key, 0), shape, _jnp.float32)
    for i, name in enumerate(TWIN_WEIGHTS):
        w = inp[name].astype(_jnp.float32)
        if MOMENT_SCALE is None:
            s = _jnp.sqrt(_jnp.mean(_jnp.square(w)) + 1e-30)
        else:
            s = MOMENT_SCALE[name]
        km, kv = _jax.random.split(_jax.random.fold_in(key, i + 1))
        out[name] = w
        out["m_" + name] = s * _jax.random.normal(km, w.shape, _jnp.float32)
        out["v_" + name] = (s * s) * _jax.random.uniform(kv, w.shape, _jnp.float32, 0.5, 1.5)
    if N_MICROBATCH > 1:
        for name, axis in PER_EXAMPLE_BATCH_AXIS.items():
            out[name] = _to_microbatches(out[name], axis)
    return {'x': out['x'], 'ffn1_norm': out['ffn1_norm'], 'ffn1_w_gate': out['ffn1_w_gate'], 'ffn1_w_up': out['ffn1_w_up'], 'ffn1_w_down': out['ffn1_w_down'], 'mix_norm': out['mix_norm'], 'w_in': out['w_in'], 'b_gate': out['b_gate'], 'na_q_norm': out['na_q_norm'], 'na_k_norm': out['na_k_norm'], 'na_rpb': out['na_rpb'], 'sw_q_norm': out['sw_q_norm'], 'sw_k_norm': out['sw_k_norm'], 'sw_sink': out['sw_sink'], 't5_rel_table': out['t5_rel_table'], 'w_branch_na': out['w_branch_na'], 'w_branch_sw': out['w_branch_sw'], 'w_out': out['w_out'], 'ffn2_norm': out['ffn2_norm'], 'ffn2_w_gate': out['ffn2_w_gate'], 'ffn2_w_up': out['ffn2_w_up'], 'ffn2_w_down': out['ffn2_w_down'], 'loss_target': out['loss_target'], 'm_ffn1_norm': out['m_ffn1_norm'], 'm_ffn1_w_gate': out['m_ffn1_w_gate'], 'm_ffn1_w_up': out['m_ffn1_w_up'], 'm_ffn1_w_down': out['m_ffn1_w_down'], 'm_mix_norm': out['m_mix_norm'], 'm_w_in': out['m_w_in'], 'm_b_gate': out['m_b_gate'], 'm_na_q_norm': out['m_na_q_norm'], 'm_na_k_norm': out['m_na_k_norm'], 'm_na_rpb': out['m_na_rpb'], 'm_sw_q_norm': out['m_sw_q_norm'], 'm_sw_k_norm': out['m_sw_k_norm'], 'm_sw_sink': out['m_sw_sink'], 'm_t5_rel_table': out['m_t5_rel_table'], 'm_w_branch_na': out['m_w_branch_na'], 'm_w_branch_sw': out['m_w_branch_sw'], 'm_w_out': out['m_w_out'], 'm_ffn2_norm': out['m_ffn2_norm'], 'm_ffn2_w_gate': out['m_ffn2_w_gate'], 'm_ffn2_w_up': out['m_ffn2_w_up'], 'm_ffn2_w_down': out['m_ffn2_w_down'], 'v_ffn1_norm': out['v_ffn1_norm'], 'v_ffn1_w_gate': out['v_ffn1_w_gate'], 'v_ffn1_w_up': out['v_ffn1_w_up'], 'v_ffn1_w_down': out['v_ffn1_w_down'], 'v_mix_norm': out['v_mix_norm'], 'v_w_in': out['v_w_in'], 'v_b_gate': out['v_b_gate'], 'v_na_q_norm': out['v_na_q_norm'], 'v_na_k_norm': out['v_na_k_norm'], 'v_na_rpb': out['v_na_rpb'], 'v_sw_q_norm': out['v_sw_q_norm'], 'v_sw_k_norm': out['v_sw_k_norm'], 'v_sw_sink': out['v_sw_sink'], 'v_t5_rel_table': out['v_t5_rel_table'], 'v_w_branch_na': out['v_w_branch_na'], 'v_w_branch_sw': out['v_w_branch_sw'], 'v_w_out': out['v_w_out'], 'v_ffn2_norm': out['v_ffn2_norm'], 'v_ffn2_w_gate': out['v_ffn2_w_gate'], 'v_ffn2_w_up': out['v_ffn2_w_up'], 'v_ffn2_w_down': out['v_ffn2_w_down']}


def _loss(weights, diff, rest, loss_target):
    with _jax.named_scope("forward"):
        args = {**rest, TWIN_DIFF_INPUT: diff, **{k: w.astype(_WEIGHT_DTYPES[k]) for k, w in weights.items()}}
        y = _forward(args)
    with _jax.named_scope("loss_head"):
        err = _jnp.square(y.astype(_jnp.float32) - loss_target)
        return 0.5 * _jnp.sum(_jnp.mean(err, axis=-1)) if err.ndim else 0.5 * err


def _adamw(w, g, m, v):
    m = ADAM_B1 * m + (1.0 - ADAM_B1) * g
    v = ADAM_B2 * v + (1.0 - ADAM_B2) * _jnp.square(g)
    m_hat = m / (1.0 - ADAM_B1 ** ADAM_STEP)
    v_hat = v / (1.0 - ADAM_B2 ** ADAM_STEP)
    delta = -ADAM_LR * (m_hat / (_jnp.sqrt(v_hat) + ADAM_EPS) + ADAM_WD * w)
    return delta, m, v


def reference(x, ffn1_norm, ffn1_w_gate, ffn1_w_up, ffn1_w_down, mix_norm, w_in, b_gate, na_q_norm, na_k_norm, na_rpb, sw_q_norm, sw_k_norm, sw_sink, t5_rel_table, w_branch_na, w_branch_sw, w_out, ffn2_norm, ffn2_w_gate, ffn2_w_up, ffn2_w_down, loss_target, m_ffn1_norm, m_ffn1_w_gate, m_ffn1_w_up, m_ffn1_w_down, m_mix_norm, m_w_in, m_b_gate, m_na_q_norm, m_na_k_norm, m_na_rpb, m_sw_q_norm, m_sw_k_norm, m_sw_sink, m_t5_rel_table, m_w_branch_na, m_w_branch_sw, m_w_out, m_ffn2_norm, m_ffn2_w_gate, m_ffn2_w_up, m_ffn2_w_down, v_ffn1_norm, v_ffn1_w_gate, v_ffn1_w_up, v_ffn1_w_down, v_mix_norm, v_w_in, v_b_gate, v_na_q_norm, v_na_k_norm, v_na_rpb, v_sw_q_norm, v_sw_k_norm, v_sw_sink, v_t5_rel_table, v_w_branch_na, v_w_branch_sw, v_w_out, v_ffn2_norm, v_ffn2_w_gate, v_ffn2_w_up, v_ffn2_w_down):
    given = dict(x=x, ffn1_norm=ffn1_norm, ffn1_w_gate=ffn1_w_gate, ffn1_w_up=ffn1_w_up, ffn1_w_down=ffn1_w_down, mix_norm=mix_norm, w_in=w_in, b_gate=b_gate, na_q_norm=na_q_norm, na_k_norm=na_k_norm, na_rpb=na_rpb, sw_q_norm=sw_q_norm, sw_k_norm=sw_k_norm, sw_sink=sw_sink, t5_rel_table=t5_rel_table, w_branch_na=w_branch_na, w_branch_sw=w_branch_sw, w_out=w_out, ffn2_norm=ffn2_norm, ffn2_w_gate=ffn2_w_gate, ffn2_w_up=ffn2_w_up, ffn2_w_down=ffn2_w_down, loss_target=loss_target, m_ffn1_norm=m_ffn1_norm, m_ffn1_w_gate=m_ffn1_w_gate, m_ffn1_w_up=m_ffn1_w_up, m_ffn1_w_down=m_ffn1_w_down, m_mix_norm=m_mix_norm, m_w_in=m_w_in, m_b_gate=m_b_gate, m_na_q_norm=m_na_q_norm, m_na_k_norm=m_na_k_norm, m_na_rpb=m_na_rpb, m_sw_q_norm=m_sw_q_norm, m_sw_k_norm=m_sw_k_norm, m_sw_sink=m_sw_sink, m_t5_rel_table=m_t5_rel_table, m_w_branch_na=m_w_branch_na, m_w_branch_sw=m_w_branch_sw, m_w_out=m_w_out, m_ffn2_norm=m_ffn2_norm, m_ffn2_w_gate=m_ffn2_w_gate, m_ffn2_w_up=m_ffn2_w_up, m_ffn2_w_down=m_ffn2_w_down, v_ffn1_norm=v_ffn1_norm, v_ffn1_w_gate=v_ffn1_w_gate, v_ffn1_w_up=v_ffn1_w_up, v_ffn1_w_down=v_ffn1_w_down, v_mix_norm=v_mix_norm, v_w_in=v_w_in, v_b_gate=v_b_gate, v_na_q_norm=v_na_q_norm, v_na_k_norm=v_na_k_norm, v_na_rpb=v_na_rpb, v_sw_q_norm=v_sw_q_norm, v_sw_k_norm=v_sw_k_norm, v_sw_sink=v_sw_sink, v_t5_rel_table=v_t5_rel_table, v_w_branch_na=v_w_branch_na, v_w_branch_sw=v_w_branch_sw, v_w_out=v_w_out, v_ffn2_norm=v_ffn2_norm, v_ffn2_w_gate=v_ffn2_w_gate, v_ffn2_w_up=v_ffn2_w_up, v_ffn2_w_down=v_ffn2_w_down)
    weights = {n: given[n] for n in TWIN_WEIGHTS}
    shared = {n: given[n] for n in SHARED_INPUTS}
    per_example = {n: given[n] for n in ['x']}
    grad_fn = _jax.value_and_grad(_loss, argnums=(0, 1))

    def one_microbatch(ex, loss_target):
        ex = dict(ex)
        diff = ex.pop(TWIN_DIFF_INPUT)
        return grad_fn(weights, diff, {**shared, **ex}, loss_target)

    if N_MICROBATCH == 1:
        loss, (grad_w, grad_x) = one_microbatch(per_example, given["loss_target"])
    else:
        def body(carry, xs):
            loss_sum, grad_sum = carry
            l_k, (gw_k, gx_k) = one_microbatch(xs[0], xs[1])
            with _jax.named_scope("update"):
                return (loss_sum + l_k, _jax.tree.map(_jnp.add, grad_sum, gw_k)), gx_k

        init = (_jnp.zeros((), _jnp.float32), _jax.tree.map(_jnp.zeros_like, weights))
        (loss, grad_w), grad_x = _jax.lax.scan(body, init, (per_example, given["loss_target"]))
    with _jax.named_scope("update"):
        delta_w, new_m, new_v = {}, {}, {}
        for n in TWIN_WEIGHTS:
            delta_w[n], new_m[n], new_v[n] = _adamw(weights[n], grad_w[n], given["m_" + n], given["v_" + n])
    return (loss, grad_x, *[grad_w[n] for n in TWIN_WEIGHTS], *[delta_w[n] for n in TWIN_WEIGHTS],
            *[new_m[n] for n in TWIN_WEIGHTS], *[new_v[n] for n in TWIN_WEIGHTS])
```

```python
import functools
import math

import jax
import jax.numpy as jnp
import numpy as np
from jax import lax
from jax.experimental import pallas as pl
from jax.experimental.pallas import tpu as pltpu

F32 = jnp.float32
BF16 = jnp.bfloat16

SEQ = 2048
DM = 1024
DFF = 2816
DEPTH = 2
NSH = 4
FSH = DFF // NSH
GRID_W = 64
ROWS = SEQ // GRID_W
NA_HEADS = 8
HD = 64
NA_WR = 8
NA_WC = 16
NA_KEYS = NA_WR * GRID_W
SW_BLK = 128
SW_NB = SEQ // SW_BLK
SW_KEYS = 3 * SW_BLK
ATT_W = 2304
GATE_W = 2048
IN_W = ATT_W + GATE_W
EPS = 1e-6
NEG = -1e30
QK_SCALE = 1.0 / math.sqrt(HD)

ADAM_LR = 0.001
ADAM_B1 = 0.9
ADAM_B2 = 0.999
ADAM_EPS = 1e-08
ADAM_WD = 0.01
ADAM_STEP = 10

VMEM_LIMIT = 56 << 20
MESH = pl.DeviceIdType.MESH

NT = (((1,), (1,)), ((), ()))
TN = (((0,), (0,)), ((), ()))
NN = (((1,), (0,)), ((), ()))


def _dot(a, b, dims=NN):
    return lax.dot_general(a, b, dims, preferred_element_type=F32)


def _params(sem=None):
    return pltpu.CompilerParams(dimension_semantics=sem, vmem_limit_bytes=VMEM_LIMIT)


def _sds(shape, dtype):
    return jax.ShapeDtypeStruct(shape, dtype)


def mm(a, b, *, name, ta=False, tb=False, out_dtype=F32, add=None, scale=None, tm=512, tn=None, tk=None, exact=False):
    m, kd = (a.shape[1], a.shape[0]) if ta else a.shape
    n = b.shape[0] if tb else b.shape[1]
    tm, tn, tk = min(tm, m), min(tn or n, n), min(tk or kd, kd)
    nk = kd // tk
    dims = (((0 if ta else 1,), (1 if tb else 0,)), ((), ()))

    def body(*refs):
        if add is None:
            a_ref, b_ref, o_ref, acc = refs
        else:
            a_ref, b_ref, add_ref, o_ref, acc = refs
        k = pl.program_id(2)

        @pl.when(k == 0)
        def _():
            acc[...] = jnp.zeros_like(acc)

        if exact:
            acc[...] += lax.dot_general(a_ref[...], b_ref[...], dims, precision=lax.Precision.HIGHEST,
                                        preferred_element_type=F32)
        else:
            acc[...] += lax.dot_general(a_ref[...].astype(BF16), b_ref[...].astype(BF16), dims,
                                        preferred_element_type=F32)

        @pl.when(k == nk - 1)
        def _():
            r = acc[...]
            if scale is not None:
                r = r * scale
            if add is not None:
                r = r + add_ref[...]
            o_ref[...] = r.astype(out_dtype)

    a_spec = pl.BlockSpec((tk, tm), lambda i, j, k: (k, i)) if ta else pl.BlockSpec((tm, tk), lambda i, j, k: (i, k))
    b_spec = pl.BlockSpec((tn, tk), lambda i, j, k: (j, k)) if tb else pl.BlockSpec((tk, tn), lambda i, j, k: (k, j))
    o_spec = pl.BlockSpec((tm, tn), lambda i, j, k: (i, j))
    ins, specs = [a, b], [a_spec, b_spec]
    if add is not None:
        ins.append(add)
        specs.append(o_spec)
    return pl.pallas_call(
        body, name=name, out_shape=_sds((m, n), out_dtype), grid=(m // tm, n // tn, nk), in_specs=specs,
        out_specs=o_spec, scratch_shapes=[pltpu.VMEM((tm, tn), F32)],
        compiler_params=_params(("parallel", "parallel", "arbitrary")))(*ins)


def _rms(x):
    return lax.rsqrt(jnp.mean(x * x, axis=-1, keepdims=True) + EPS)


def rms_fwd(x, gain, *, name, tm=512):
    def body(x_ref, g_ref, h_ref):
        x = x_ref[...]
        h_ref[...] = (x * _rms(x) * g_ref[...]).astype(BF16)

    return pl.pallas_call(
        body, name=name, out_shape=_sds(x.shape, BF16), grid=(x.shape[0] // tm,),
        in_specs=[pl.BlockSpec((tm, DM), lambda i: (i, 0)), pl.BlockSpec((1, DM), lambda i: (0, 0))],
        out_specs=pl.BlockSpec((tm, DM), lambda i: (i, 0)), compiler_params=_params(("parallel",)))(x, gain)


def _rms_bwd_math(dh, x, gain):
    r = _rms(x)
    xh = x * r
    dgain = jnp.sum(dh * xh, axis=0, keepdims=True)
    dxn = dh * gain
    dx = r * (dxn - xh * jnp.mean(dxn * xh, axis=-1, keepdims=True))
    return dx, dgain


def rms_bwd(dh, x, gain, dres, *, name, tm=512):
    def body(dh_ref, x_ref, g_ref, dres_ref, dx_ref, dg_ref):
        @pl.when(pl.program_id(0) == 0)
        def _():
            dg_ref[...] = jnp.zeros_like(dg_ref)

        dx, dg = _rms_bwd_math(dh_ref[...], x_ref[...], g_ref[...])
        dx_ref[...] = dres_ref[...] + dx
        dg_ref[...] += dg

    tile = pl.BlockSpec((tm, DM), lambda i: (i, 0))
    vec = pl.BlockSpec((1, DM), lambda i: (0, 0))
    return pl.pallas_call(
        body, name=name, out_shape=(_sds(x.shape, F32), _sds((1, DM), F32)), grid=(x.shape[0] // tm,),
        in_specs=[tile, tile, vec, tile], out_specs=(tile, vec), compiler_params=_params(("arbitrary",)))(dh, x, gain, dres)


def ffn_fwd(x, gain, wg, wu, wd, *, name, tm=512):
    def body(x_ref, g_ref, wg_ref, wu_ref, wd_ref, y_ref, h_ref, gg_ref, uu_ref, acc):
        f = pl.program_id(1)

        @pl.when(f == 0)
        def _():
            x = x_ref[...]
            h_ref[...] = (x * _rms(x) * g_ref[...]).astype(BF16)
            acc[...] = jnp.zeros_like(acc)

        h = h_ref[...]
        gg = _dot(h, wg_ref[...])
        uu = _dot(h, wu_ref[...])
        gg_ref[...] = gg.astype(BF16)
        uu_ref[...] = uu.astype(BF16)
        act = (gg * jax.nn.sigmoid(gg) * uu).astype(BF16)
        acc[...] += _dot(act, wd_ref[...])

        @pl.when(f == NSH - 1)
        def _():
            y_ref[...] = x_ref[...] + 0.5 * acc[...]

    s = x.shape[0]
    tile = pl.BlockSpec((tm, DM), lambda i, f: (i, 0))
    w_in = pl.BlockSpec((None, DM, FSH), lambda i, f: (f, 0, 0))
    w_out = pl.BlockSpec((None, FSH, DM), lambda i, f: (f, 0, 0))
    hid = pl.BlockSpec((None, tm, FSH), lambda i, f: (f, i, 0))
    return pl.pallas_call(
        body, name=name,
        out_shape=(_sds((s, DM), F32), _sds((s, DM), BF16), _sds((NSH, s, FSH), BF16), _sds((NSH, s, FSH), BF16)),
        grid=(s // tm, NSH), in_specs=[tile, pl.BlockSpec((1, DM), lambda i, f: (0, 0)), w_in, w_in, w_out],
        out_specs=(tile, tile, hid, hid), scratch_shapes=[pltpu.VMEM((tm, DM), F32)],
        compiler_params=_params(("parallel", "arbitrary")))(x, gain, wg, wu, wd)


def ffn_bwd_tokens(dy, x, gain, gg, uu, wg, wu, wd, *, name, tm=512):
    def body(dy_ref, x_ref, g_ref, gg_ref, uu_ref, wg_ref, wu_ref, wd_ref, dx_ref, dgain_ref, act_ref, dg_ref, du_ref, dh):
        i, f = pl.program_id(0), pl.program_id(1)

        @pl.when(f == 0)
        def _():
            dh[...] = jnp.zeros_like(dh)

        @pl.when((i == 0) & (f == 0))
        def _():
            dgain_ref[...] = jnp.zeros_like(dgain_ref)

        dact = _dot((0.5 * dy_ref[...]).astype(BF16), wd_ref[...], NT)
        g = gg_ref[...].astype(F32)
        u = uu_ref[...].astype(F32)
        sg = jax.nn.sigmoid(g)
        silu = g * sg
        act_ref[...] = (silu * u).astype(BF16)
        dg = (dact * u * (sg * (1.0 + g * (1.0 - sg)))).astype(BF16)
        du = (dact * silu).astype(BF16)
        dg_ref[...] = dg
        du_ref[...] = du
        dh[...] += _dot(dg, wg_ref[...], NT) + _dot(du, wu_ref[...], NT)

        @pl.when(f == NSH - 1)
        def _():
            dx, dgain = _rms_bwd_math(dh[...], x_ref[...], g_ref[...])
            dx_ref[...] = dy_ref[...] + dx
            dgain_ref[...] += dgain

    s = x.shape[0]
    tile = pl.BlockSpec((tm, DM), lambda i, f: (i, 0))
    vec = pl.BlockSpec((1, DM), lambda i, f: (0, 0))
    w_in = pl.BlockSpec((None, DM, FSH), lambda i, f: (f, 0, 0))
    w_out = pl.BlockSpec((None, FSH, DM), lambda i, f: (f, 0, 0))
    hid = pl.BlockSpec((None, tm, FSH), lambda i, f: (f, i, 0))
    hshape = _sds((NSH, s, FSH), BF16)
    return pl.pallas_call(
        body, name=name, out_shape=(_sds((s, DM), F32), _sds((1, DM), F32), hshape, hshape, hshape),
        grid=(s // tm, NSH), in_specs=[tile, tile, vec, hid, hid, w_in, w_in, w_out],
        out_specs=(tile, vec, hid, hid, hid), scratch_shapes=[pltpu.VMEM((tm, DM), F32)],
        compiler_params=_params(("arbitrary", "arbitrary")))(dy, x, gain, gg, uu, wg, wu, wd)


def ffn_bwd_weights(h, dy, act, dg, du, *, name, tk=512):
    s = h.shape[0]
    nk = s // tk

    def body(h_ref, dy_ref, act_ref, dg_ref, du_ref, gwg_ref, gwu_ref, gwd_ref, ag, au, ad):
        k = pl.program_id(1)

        @pl.when(k == 0)
        def _():
            ag[...] = jnp.zeros_like(ag)
            au[...] = jnp.zeros_like(au)
            ad[...] = jnp.zeros_like(ad)

        h = h_ref[...]
        ag[...] += _dot(h, dg_ref[...], TN)
        au[...] += _dot(h, du_ref[...], TN)
        ad[...] += _dot(act_ref[...], dy_ref[...].astype(BF16), TN)

        @pl.when(k == nk - 1)
        def _():
            gwg_ref[...] = ag[...].astype(BF16)
            gwu_ref[...] = au[...].astype(BF16)
            gwd_ref[...] = (0.5 * ad[...]).astype(BF16)

    tile = pl.BlockSpec((tk, DM), lambda f, k: (k, 0))
    hid = pl.BlockSpec((None, tk, FSH), lambda f, k: (f, k, 0))
    w_in = pl.BlockSpec((None, DM, FSH), lambda f, k: (f, 0, 0))
    w_out = pl.BlockSpec((None, FSH, DM), lambda f, k: (f, 0, 0))
    return pl.pallas_call(
        body, name=name,
        out_shape=(_sds((NSH, DM, FSH), BF16), _sds((NSH, DM, FSH), BF16), _sds((NSH, FSH, DM), BF16)),
        grid=(NSH, nk), in_specs=[tile, tile, hid, hid, hid], out_specs=(w_in, w_in, w_out),
        scratch_shapes=[pltpu.VMEM((DM, FSH), F32), pltpu.VMEM((DM, FSH), F32), pltpu.VMEM((FSH, DM), F32)],
        compiler_params=_params(("parallel", "arbitrary")))(h, dy, act, dg, du)


def _group_mean(v, bd):
    return lax.dot_general(v, bd, NN, precision=lax.Precision.HIGHEST, preferred_element_type=F32)


def _block_diag(width):
    idx = np.arange(width) // HD
    return jnp.asarray((idx[:, None] == idx[None, :]).astype(np.float32) / HD)


def qknorm_fwd(z, gq_na, gk_na, gq_sw, gk_sw, *, name, tm=256):
    def body(zq_ref, zk_ref, zv_ref, zs_ref, zkv_ref, gqa_ref, gka_ref, gqs_ref, gks_ref, bd_ref, bd2_ref,
             qa_ref, ka_ref, va_ref, qs_ref, kv_ref):
        bd = bd_ref[...]

        def norm(x, g, bdm):
            return x * lax.rsqrt(_group_mean(x * x, bdm) + EPS) * g

        qa_ref[...] = (norm(zq_ref[...], gqa_ref[...], bd) * QK_SCALE).astype(BF16)
        ka_ref[...] = norm(zk_ref[...], gka_ref[...], bd).astype(BF16)
        va_ref[...] = zv_ref[...].astype(BF16)
        qs_ref[...] = (norm(zs_ref[...], gqs_ref[...], bd) * QK_SCALE).astype(BF16)
        kv = zkv_ref[...]
        kv_ref[:, 0:128] = norm(kv[:, 0:128], gks_ref[...], bd2_ref[...]).astype(BF16)
        kv_ref[:, 128:256] = kv[:, 128:256].astype(BF16)

    s = z.shape[0]
    col = lambda j: pl.BlockSpec((tm, 512), lambda i, j=j: (i, j))
    vec = lambda w: pl.BlockSpec((1, w), lambda i: (0, 0))
    o512 = pl.BlockSpec((tm, 512), lambda i: (i, 0))
    g512 = lambda g: jnp.tile(g.reshape(1, HD), (1, 8))
    return pl.pallas_call(
        body, name=name,
        out_shape=(_sds((s, 512), BF16),) * 4 + (_sds((s, 256), BF16),), grid=(s // tm,),
        in_specs=[col(0), col(1), col(2), col(3), pl.BlockSpec((tm, 256), lambda i: (i, 8)), vec(512), vec(512), vec(512),
                  vec(128), pl.BlockSpec((512, 512), lambda i: (0, 0)), pl.BlockSpec((128, 128), lambda i: (0, 0))],
        out_specs=(o512, o512, o512, o512, pl.BlockSpec((tm, 256), lambda i: (i, 0))),
        compiler_params=_params(("parallel",)))(
            z, z, z, z, z, g512(gq_na), g512(gk_na), g512(gq_sw), jnp.tile(gk_sw.reshape(1, HD), (1, 2)),
            _block_diag(512), _block_diag(128))


def qknorm_bwd(z, dqa, dka, dva, dqs, dkv, gq_na, gk_na, gq_sw, gk_sw, *, name, tm=256):
    def body(zq_ref, zk_ref, zs_ref, zkv_ref, dqa_ref, dka_ref, dva_ref, dqs_ref, dkv_ref, gqa_ref, gka_ref, gqs_ref,
             gks_ref, bd_ref, bd2_ref, dz_ref, dgqa_ref, dgka_ref, dgqs_ref, dgks_ref):
        @pl.when(pl.program_id(0) == 0)
        def _():
            dgqa_ref[...] = jnp.zeros_like(dgqa_ref)
            dgka_ref[...] = jnp.zeros_like(dgka_ref)
            dgqs_ref[...] = jnp.zeros_like(dgqs_ref)
            dgks_ref[...] = jnp.zeros_like(dgks_ref)

        bd = bd_ref[...]

        def bwd(x, dy, g, bdm, dg_ref):
            r = lax.rsqrt(_group_mean(x * x, bdm) + EPS)
            xh = x * r
            dg_ref[...] += jnp.sum(dy * xh, axis=0, keepdims=True)
            dxn = dy * g
            return r * (dxn - xh * _group_mean(dxn * xh, bdm))

        dz_ref[:, 0:512] = bwd(zq_ref[...], dqa_ref[...] * QK_SCALE, gqa_ref[...], bd, dgqa_ref).astype(BF16)
        dz_ref[:, 512:1024] = bwd(zk_ref[...], dka_ref[...], gka_ref[...], bd, dgka_ref).astype(BF16)
        dz_ref[:, 1024:1536] = dva_ref[...].astype(BF16)
        dz_ref[:, 1536:2048] = bwd(zs_ref[...], dqs_ref[...] * QK_SCALE, gqs_ref[...], bd, dgqs_ref).astype(BF16)
        dkv = dkv_ref[...]
        dz_ref[:, 2048:2176] = bwd(zkv_ref[:, 0:128], dkv[:, 0:128], gks_ref[...], bd2_ref[...], dgks_ref).astype(BF16)
        dz_ref[:, 2176:2304] = dkv[:, 128:256].astype(BF16)

    s = z.shape[0]
    col = lambda j: pl.BlockSpec((tm, 512), lambda i, j=j: (i, j))
    t512 = pl.BlockSpec((tm, 512), lambda i: (i, 0))
    t256 = pl.BlockSpec((tm, 256), lambda i: (i, 0))
    vec = lambda w: pl.BlockSpec((1, w), lambda i: (0, 0))
    g512 = lambda g: jnp.tile(g.reshape(1, HD), (1, 8))
    return pl.pallas_call(
        body, name=name,
        out_shape=(_sds((s, ATT_W), BF16), _sds((1, 512), F32), _sds((1, 512), F32), _sds((1, 512), F32), _sds((1, 128), F32)),
        grid=(s // tm,),
        in_specs=[col(0), col(1), col(3), pl.BlockSpec((tm, 256), lambda i: (i, 8)), t512, t512, t512, t512, t256,
                  vec(512), vec(512), vec(512), vec(128), pl.BlockSpec((512, 512), lambda i: (0, 0)),
                  pl.BlockSpec((128, 128), lambda i: (0, 0))],
        out_specs=(pl.BlockSpec((tm, ATT_W), lambda i: (i, 0)), vec(512), vec(512), vec(512), vec(128)),
        compiler_params=_params(("arbitrary",)))(
            z, z, z, z, dqa, dka, dva, dqs, dkv, g512(gq_na), g512(gk_na), g512(gq_sw),
            jnp.tile(gk_sw.reshape(1, HD), (1, 2)), _block_diag(512), _block_diag(128))


def _na_row_start(r):
    return jnp.clip(r - NA_WR // 2, 0, ROWS - NA_WR)


def na_bias_table(rpb, *, name):
    t = jnp.pad(rpb, ((0, 0), (0, 2), (0, HD - (2 * NA_WC - 1))))
    pairs = jnp.concatenate([t[:, :16], t[:, 1:17]], axis=-1).reshape(NA_HEADS, 16, 1, 128)

    def body(t_ref, o_ref):
        p = pl.program_id(0)
        q = lax.broadcasted_iota(jnp.int32, (GRID_W, 128), 0)
        kc = lax.broadcasted_iota(jnp.int32, (GRID_W, 128), 1) & (GRID_W - 1)
        cs = jnp.clip(q - NA_WC // 2, 0, GRID_W - NA_WC)
        ok = (kc >= cs) & (kc < cs + NA_WC)
        for h in range(NA_HEADS):
            for pr in range(NA_WR // 2):
                x = jnp.broadcast_to(t_ref[h, 2 * pr - p + NA_WR - 1], (GRID_W, 128))
                b = pltpu.roll(x, 128 - (NA_WC - 1), 1, stride=1, stride_axis=0)
                o_ref[h, :, 128 * pr:128 * pr + 128] = jnp.where(ok, b, NEG)

    return pl.pallas_call(
        body, name=name, out_shape=_sds((NA_WR, NA_HEADS, GRID_W, NA_KEYS), F32), grid=(NA_WR,),
        in_specs=[pl.BlockSpec((NA_HEADS, 16, 1, 128), lambda p: (0, 0, 0, 0))],
        out_specs=pl.BlockSpec((None, NA_HEADS, GRID_W, NA_KEYS), lambda p: (p, 0, 0, 0)),
        compiler_params=_params(("parallel",)))(pairs)


def _lane_halves():
    lane = lax.broadcasted_iota(jnp.int32, (1, 128), 1)
    return lane < HD


def na_fwd(q, k, v, bias, *, name):
    def body(q_ref, k_ref, v_ref, b_ref, o_ref, lse_ref):
        r = pl.program_id(0)
        off = pl.multiple_of(_na_row_start(r) * GRID_W, GRID_W)
        first = _lane_halves()
        for j in range(NA_HEADS // 2):
            lanes = slice(128 * j, 128 * j + 128)
            q2 = q_ref[:, lanes]
            k2 = k_ref[pl.ds(off, NA_KEYS), lanes]
            v2 = v_ref[pl.ds(off, NA_KEYS), lanes]
            zero = jnp.zeros_like(q2)
            o2 = jnp.zeros((GRID_W, 128), F32)
            for half in range(2):
                h = 2 * j + half
                sel = first if half == 0 else jnp.logical_not(first)
                s = _dot(jnp.where(sel, q2, zero), k2, NT)
                b = b_ref[h]
                s = jnp.where(b > 0.5 * NEG, s + b, NEG)
                m = jnp.max(s, axis=-1, keepdims=True)
                e = jnp.exp(s - m)
                l = jnp.sum(e, axis=-1, keepdims=True)
                p = (e / l).astype(BF16)
                o2 = o2 + _dot(p, jnp.where(sel, v2, jnp.zeros_like(v2)))
                lse_ref[:, h:h + 1] = m + jnp.log(l)
            o_ref[:, lanes] = o2.astype(BF16)

    s_tok = q.shape[0]
    full = pl.BlockSpec((s_tok, 512), lambda r: (0, 0))
    return pl.pallas_call(
        body, name=name, out_shape=(_sds((s_tok, 512), BF16), _sds((s_tok, NA_HEADS), F32)), grid=(ROWS,),
        in_specs=[pl.BlockSpec((GRID_W, 512), lambda r: (r, 0)), full, full,
                  pl.BlockSpec((None, NA_HEADS, GRID_W, NA_KEYS), lambda r: (r - _na_row_start(r), 0, 0, 0))],
        out_specs=(pl.BlockSpec((GRID_W, 512), lambda r: (r, 0)), pl.BlockSpec((GRID_W, NA_HEADS), lambda r: (r, 0))),
        compiler_params=_params(("parallel",)))(q, k, v, bias)


def na_bwd(q, k, v, o, do, lse, bias, *, name):
    def body(q_ref, k_ref, v_ref, o_ref, do_ref, lse_ref, b_ref, dq_ref, dk_ref, dv_ref, db_ref):
        r = pl.program_id(0)

        @pl.when(r == 0)
        def _():
            dk_ref[...] = jnp.zeros_like(dk_ref)
            dv_ref[...] = jnp.zeros_like(dv_ref)

        @pl.when((r <= NA_WR // 2) | (r > ROWS - NA_WR // 2))
        def _():
            db_ref[...] = jnp.zeros_like(db_ref)

        off = pl.multiple_of(_na_row_start(r) * GRID_W, GRID_W)
        first = _lane_halves()
        for j in range(NA_HEADS // 2):
            lanes = slice(128 * j, 128 * j + 128)
            q2 = q_ref[:, lanes]
            k2 = k_ref[pl.ds(off, NA_KEYS), lanes]
            v2 = v_ref[pl.ds(off, NA_KEYS), lanes]
            do2 = do_ref[:, lanes]
            prod = do2.astype(F32) * o_ref[:, lanes].astype(F32)
            dq2 = jnp.zeros((GRID_W, 128), F32)
            dk2 = jnp.zeros((NA_KEYS, 128), F32)
            dv2 = jnp.zeros((NA_KEYS, 128), F32)
            for half in range(2):
                h = 2 * j + half
                sel = first if half == 0 else jnp.logical_not(first)
                qh = jnp.where(sel, q2, jnp.zeros_like(q2))
                doh = jnp.where(sel, do2, jnp.zeros_like(do2))
                s = _dot(qh, k2, NT)
                b = b_ref[h]
                s = jnp.where(b > 0.5 * NEG, s + b, NEG)
                p = jnp.exp(s - lse_ref[:, h:h + 1])
                dp = _dot(doh, v2, NT)
                delta = jnp.sum(jnp.where(sel, prod, 0.0), axis=-1, keepdims=True)
                ds = p * (dp - delta)
                db_ref[h] += ds
                dsb = ds.astype(BF16)
                dq2 = dq2 + _dot(dsb, jnp.where(sel, k2, jnp.zeros_like(k2)))
                dk2 = dk2 + _dot(dsb, qh, TN)
                dv2 = dv2 + _dot(p.astype(BF16), doh, TN)
            dq_ref[:, lanes] = dq2
            dk_ref[pl.ds(off, NA_KEYS), lanes] += dk2
            dv_ref[pl.ds(off, NA_KEYS), lanes] += dv2

    s_tok = q.shape[0]
    full = pl.BlockSpec((s_tok, 512), lambda r: (0, 0))
    row = pl.BlockSpec((GRID_W, 512), lambda r: (r, 0))
    bias_spec = pl.BlockSpec((None, NA_HEADS, GRID_W, NA_KEYS), lambda r: (r - _na_row_start(r), 0, 0, 0))
    return pl.pallas_call(
        body, name=name,
        out_shape=(_sds((s_tok, 512), F32), _sds((s_tok, 512), F32), _sds((s_tok, 512), F32),
                   _sds((NA_WR, NA_HEADS, GRID_W, NA_KEYS), F32)),
        grid=(ROWS,),
        in_specs=[row, full, full, row, row, pl.BlockSpec((GRID_W, NA_HEADS), lambda r: (r, 0)), bias_spec],
        out_specs=(row, full, full, bias_spec), compiler_params=_params(("arbitrary",)))(q, k, v, o, do, lse, bias)


def t5_bucket_map():
    rel = np.arange(SW_KEYS)[None, :] - SW_BLK - np.arange(SW_BLK)[:, None]
    nb = 16
    max_exact = nb // 2
    n = np.abs(rel)
    large = max_exact + (np.log(np.maximum(n, 1) / max_exact) / np.log(128 / max_exact) * (nb - max_exact)).astype(np.int32)
    large = np.minimum(large, nb - 1)
    return ((rel > 0) * nb + np.where(n < max_exact, n, large)).astype(np.int32)


def t5_bias(table, *, name):
    rel = np.arange(-SW_BLK, SW_BLK + 1)
    nb, max_exact = 16, 8
    n = np.abs(rel)
    large = max_exact + (np.log(np.maximum(n, 1) / max_exact) / np.log(128 / max_exact) * (nb - max_exact)).astype(np.int32)
    bucket = ((rel > 0) * nb + np.where(n < max_exact, n, np.minimum(large, nb - 1))).astype(np.int32)
    u = jnp.pad(table[jnp.asarray(bucket)].T, ((0, 0), (0, SW_KEYS - bucket.shape[0]))).reshape(8, 1, SW_KEYS)

    def body(u_ref, o_ref):
        for h in range(8):
            x = jnp.broadcast_to(u_ref[h], (SW_BLK, SW_KEYS))
            o_ref[h] = pltpu.roll(x, 0, 1, stride=1, stride_axis=0)

    return pl.pallas_call(body, name=name, out_shape=_sds((8, SW_BLK, SW_KEYS), F32), compiler_params=_params())(u)


def _sw_valid(n):
    a = lax.broadcasted_iota(jnp.int32, (SW_BLK, SW_KEYS), 0)
    j = lax.broadcasted_iota(jnp.int32, (SW_BLK, SW_KEYS), 1)
    kpos = (n - 1) * SW_BLK + j
    return (jnp.abs(j - SW_BLK - a) <= SW_BLK) & (kpos >= 0) & (kpos < SEQ)


def _dup_group(x2, g, first):
    rolled = pltpu.roll(x2, HD, 1)
    return jnp.where(first, x2, rolled) if g == 0 else jnp.where(first, rolled, x2)


def sw_fwd(q, kv, t5, sink, *, name):
    def body(q_ref, kv_ref, t5_ref, sink_ref, o_ref, lse_ref):
        n = pl.program_id(0)
        off = pl.multiple_of(n * SW_BLK, SW_BLK)
        first = _lane_halves()
        valid = _sw_valid(n)
        k2 = kv_ref[pl.ds(off, SW_KEYS), 0:128]
        v2 = kv_ref[pl.ds(off, SW_KEYS), 128:256]
        for j in range(4):
            g = j // 2
            kk = _dup_group(k2, g, first)
            vv = _dup_group(v2, g, first)
            lanes = slice(128 * j, 128 * j + 128)
            q2 = q_ref[:, lanes]
            o2 = jnp.zeros((SW_BLK, 128), F32)
            for half in range(2):
                h = 2 * j + half
                sel = first if half == 0 else jnp.logical_not(first)
                s = _dot(jnp.where(sel, q2, jnp.zeros_like(q2)), kk, NT)
                s = jnp.where(valid, s + t5_ref[h], NEG)
                snk = sink_ref[:, h:h + 1]
                m = jnp.maximum(jnp.max(s, axis=-1, keepdims=True), snk)
                e = jnp.exp(s - m)
                den = jnp.sum(e, axis=-1, keepdims=True) + jnp.exp(snk - m)
                p = (e / den).astype(BF16)
                o2 = o2 + _dot(p, jnp.where(sel, vv, jnp.zeros_like(vv)))
                lse_ref[:, h:h + 1] = m + jnp.log(den)
            o_ref[:, lanes] = o2.astype(BF16)

    s_tok = q.shape[0]
    blk = pl.BlockSpec((SW_BLK, 512), lambda n: (n, 0))
    return pl.pallas_call(
        body, name=name, out_shape=(_sds((s_tok, 512), BF16), _sds((s_tok, 8), F32)), grid=(SW_NB,),
        in_specs=[blk, pl.BlockSpec(kv.shape, lambda n: (0, 0)), pl.BlockSpec((8, SW_BLK, SW_KEYS), lambda n: (0, 0, 0)),
                  pl.BlockSpec((1, 8), lambda n: (0, 0))],
        out_specs=(blk, pl.BlockSpec((SW_BLK, 8), lambda n: (n, 0))), compiler_params=_params(("parallel",)))(q, kv, t5, sink)


def sw_bwd(q, kv, o, do, lse, t5, sink, *, name):
    def body(q_ref, kv_ref, o_ref, do_ref, lse_ref, t5_ref, sink_ref, dq_ref, dkv_ref, dt5_ref, dsink_ref):
        n = pl.program_id(0)

        @pl.when(n == 0)
        def _():
            dkv_ref[...] = jnp.zeros_like(dkv_ref)
            dt5_ref[...] = jnp.zeros_like(dt5_ref)
            dsink_ref[...] = jnp.zeros_like(dsink_ref)

        off = pl.multiple_of(n * SW_BLK, SW_BLK)
        first = _lane_halves()
        valid = _sw_valid(n)
        k2 = kv_ref[pl.ds(off, SW_KEYS), 0:128]
        v2 = kv_ref[pl.ds(off, SW_KEYS), 128:256]
        dk_groups, dv_groups = [], []
        for g in range(2):
            kk = _dup_group(k2, g, first)
            vv = _dup_group(v2, g, first)
            dkk = jnp.zeros((SW_KEYS, 128), F32)
            dvv = jnp.zeros((SW_KEYS, 128), F32)
            for j in (2 * g, 2 * g + 1):
                lanes = slice(128 * j, 128 * j + 128)
                q2 = q_ref[:, lanes]
                do2 = do_ref[:, lanes]
                prod = do2.astype(F32) * o_ref[:, lanes].astype(F32)
                dq2 = jnp.zeros((SW_BLK, 128), F32)
                for half in range(2):
                    h = 2 * j + half
                    sel = first if half == 0 else jnp.logical_not(first)
                    qh = jnp.where(sel, q2, jnp.zeros_like(q2))
                    doh = jnp.where(sel, do2, jnp.zeros_like(do2))
                    s = _dot(qh, kk, NT)
                    s = jnp.where(valid, s + t5_ref[h], NEG)
                    lse = lse_ref[:, h:h + 1]
                    p = jnp.exp(s - lse)
                    dp = _dot(doh, vv, NT)
                    delta = jnp.sum(jnp.where(sel, prod, 0.0), axis=-1, keepdims=True)
                    ds = p * (dp - delta)
                    dt5_ref[h] += ds
                    dsink_ref[:, h:h + 1] += -jnp.sum(jnp.exp(sink_ref[:, h:h + 1] - lse) * delta, axis=0, keepdims=True)
                    dsb = ds.astype(BF16)
                    dq2 = dq2 + _dot(dsb, jnp.where(sel, kk, jnp.zeros_like(kk)))
                    dkk = dkk + _dot(dsb, qh, TN)
                    dvv = dvv + _dot(p.astype(BF16), doh, TN)
                dq_ref[:, lanes] = dq2
            dk_groups.append(dkk + pltpu.roll(dkk, HD, 1))
            dv_groups.append(dvv + pltpu.roll(dvv, HD, 1))
        dkv_ref[pl.ds(off, SW_KEYS), 0:128] += jnp.where(first, dk_groups[0], dk_groups[1])
        dkv_ref[pl.ds(off, SW_KEYS), 128:256] += jnp.where(first, dv_groups[0], dv_groups[1])

    s_tok = q.shape[0]
    blk = pl.BlockSpec((SW_BLK, 512), lambda n: (n, 0))
    kv_spec = pl.BlockSpec(kv.shape, lambda n: (0, 0))
    t5_spec = pl.BlockSpec((8, SW_BLK, SW_KEYS), lambda n: (0, 0, 0))
    vec = pl.BlockSpec((1, 8), lambda n: (0, 0))
    return pl.pallas_call(
        body, name=name,
        out_shape=(_sds((s_tok, 512), F32), _sds(kv.shape, F32), _sds((8, SW_BLK, SW_KEYS), F32), _sds((1, 8), F32)),
        grid=(SW_NB,), in_specs=[blk, kv_spec, blk, blk, pl.BlockSpec((SW_BLK, 8), lambda n: (n, 0)), t5_spec, vec],
        out_specs=(blk, kv_spec, t5_spec, vec), compiler_params=_params(("arbitrary",)))(q, kv, o, do, lse, t5, sink)


def gate_fwd(zg, bias, pa, ps, *, name, tm=512):
    def body(z0_ref, z1_ref, b0_ref, b1_ref, pa_ref, ps_ref, m_ref):
        g0 = jax.nn.sigmoid(z0_ref[...] + b0_ref[...])
        g1 = jax.nn.sigmoid(z1_ref[...] + b1_ref[...])
        m_ref[...] = (g0 * pa_ref[...] + g1 * ps_ref[...]).astype(BF16)

    s = zg.shape[0]
    half = lambda j: pl.BlockSpec((tm, DM), lambda i, j=j: (i, j))
    bvec = lambda j: pl.BlockSpec((1, DM), lambda i, j=j: (0, j))
    return pl.pallas_call(
        body, name=name, out_shape=_sds((s, DM), BF16), grid=(s // tm,),
        in_specs=[half(0), half(1), bvec(0), bvec(1), half(0), half(0)], out_specs=half(0),
        compiler_params=_params(("parallel",)))(zg, zg, bias, bias, pa, ps)


def gate_bwd(dm, zg, bias, pa, ps, *, name, tm=512):
    def body(dm_ref, z0_ref, z1_ref, b0_ref, b1_ref, pa_ref, ps_ref, dpa_ref, dps_ref, dz_ref, db_ref):
        @pl.when(pl.program_id(0) == 0)
        def _():
            db_ref[...] = jnp.zeros_like(db_ref)

        dm = dm_ref[...]
        g0 = jax.nn.sigmoid(z0_ref[...] + b0_ref[...])
        g1 = jax.nn.sigmoid(z1_ref[...] + b1_ref[...])
        dpa_ref[...] = (dm * g0).astype(BF16)
        dps_ref[...] = (dm * g1).astype(BF16)
        dz0 = dm * pa_ref[...] * g0 * (1.0 - g0)
        dz1 = dm * ps_ref[...] * g1 * (1.0 - g1)
        dz_ref[:, 0:DM] = dz0.astype(BF16)
        dz_ref[:, DM:2 * DM] = dz1.astype(BF16)
        db_ref[:, 0:DM] += jnp.sum(dz0, axis=0, keepdims=True)
        db_ref[:, DM:2 * DM] += jnp.sum(dz1, axis=0, keepdims=True)

    s = zg.shape[0]
    half = lambda j: pl.BlockSpec((tm, DM), lambda i, j=j: (i, j))
    bvec = lambda j: pl.BlockSpec((1, DM), lambda i, j=j: (0, j))
    return pl.pallas_call(
        body, name=name,
        out_shape=(_sds((s, DM), BF16), _sds((s, DM), BF16), _sds((s, GATE_W), BF16), _sds((1, GATE_W), F32)),
        grid=(s // tm,), in_specs=[half(0), half(0), half(1), bvec(0), bvec(1), half(0), half(0)],
        out_specs=(half(0), half(0), pl.BlockSpec((tm, GATE_W), lambda i: (i, 0)), pl.BlockSpec((1, GATE_W), lambda i: (0, 0))),
        compiler_params=_params(("arbitrary",)))(dm, zg, zg, bias, bias, pa, ps)


def loss_head(y, target, *, name, tm=512):
    def body(y_ref, t_ref, dy_ref, l_ref):
        @pl.when(pl.program_id(0) == 0)
        def _():
            l_ref[...] = jnp.zeros_like(l_ref)

        err = y_ref[...] - t_ref[...]
        dy_ref[...] = err * (1.0 / DM)
        l_ref[...] += 0.5 * jnp.sum(jnp.mean(err * err, axis=-1, keepdims=True), axis=0, keepdims=True)

    s = y.shape[0]
    tile = pl.BlockSpec((tm, DM), lambda i: (i, 0))
    return pl.pallas_call(
        body, name=name, out_shape=(_sds((s, DM), F32), _sds((1, 128), F32)), grid=(s // tm,), in_specs=[tile, tile],
        out_specs=(tile, pl.BlockSpec((1, 128), lambda i: (0, 0))), compiler_params=_params(("arbitrary",)))(y, target)


def adamw(w, g, m, v, *, name):
    def body(w_ref, g_ref, m_ref, v_ref, d_ref, nm_ref, nv_ref):
        g = g_ref[...]
        nm = ADAM_B1 * m_ref[...] + (1.0 - ADAM_B1) * g
        nv = ADAM_B2 * v_ref[...] + (1.0 - ADAM_B2) * jnp.square(g)
        m_hat = nm / (1.0 - ADAM_B1 ** ADAM_STEP)
        v_hat = nv / (1.0 - ADAM_B2 ** ADAM_STEP)
        d_ref[...] = -ADAM_LR * (m_hat / (jnp.sqrt(v_hat) + ADAM_EPS) + ADAM_WD * w_ref[...])
        nm_ref[...] = nm
        nv_ref[...] = nv

    b, k, n = w.shape
    tk = k // 4 if k % 32 == 0 else k
    spec = pl.BlockSpec((None, tk, n), lambda i, j: (i, j, 0))
    out = _sds(w.shape, F32)
    return pl.pallas_call(
        body, name=name, out_shape=(out, out, out), grid=(b, k // tk), in_specs=[spec] * 4, out_specs=(spec,) * 3,
        compiler_params=_params(("parallel", "parallel")))(w, g, m, v)


def adamw_shard(w, m, v, mine, theirs, cidx, *, name):
    _, k, n = w.shape
    nt = 2
    tk = k // 2 // nt

    def body(c_ref, w_ref, m_ref, v_ref, a0_ref, a1_ref, b0_ref, b1_ref, g_ref, d_ref, nm_ref, nv_ref):
        own = pl.program_id(1) == c_ref[0]
        g0 = jnp.where(own, a0_ref[...], b0_ref[...])
        g1 = jnp.where(own, a1_ref[...], b1_ref[...])
        g = jnp.where(pl.program_id(0) == 0, g0, g1)
        g_ref[...] = g
        nm = ADAM_B1 * m_ref[...] + (1.0 - ADAM_B1) * g
        nv = ADAM_B2 * v_ref[...] + (1.0 - ADAM_B2) * jnp.square(g)
        m_hat = nm / (1.0 - ADAM_B1 ** ADAM_STEP)
        v_hat = nv / (1.0 - ADAM_B2 ** ADAM_STEP)
        d_ref[...] = -ADAM_LR * (m_hat / (jnp.sqrt(v_hat) + ADAM_EPS) + ADAM_WD * w_ref[...])
        nm_ref[...] = nm
        nv_ref[...] = nv

    full = pl.BlockSpec((None, tk, n), lambda l, hf, t, c: (l, hf * nt + t, 0))

    def half(layer, is_mine):
        def index(l, hf, t, c):
            sel = (l == layer) & ((hf == c[0]) if is_mine else (hf != c[0]))
            return (jnp.where(sel, t, 0), 0)
        return pl.BlockSpec((tk, n), index)

    out = _sds(w.shape, F32)
    return pl.pallas_call(
        body, name=name, out_shape=(out, out, out, out),
        grid_spec=pltpu.PrefetchScalarGridSpec(
            num_scalar_prefetch=1, grid=(DEPTH, 2, nt),
            in_specs=[full, full, full, half(0, True), half(1, True), half(0, False), half(1, False)],
            out_specs=(full, full, full, full)),
        compiler_params=_params(("arbitrary", "arbitrary", "arbitrary")))(cidx, w, m, v, mine[0], mine[1], theirs[0], theirs[1])


def t5_table_grad(dt5_a, dt5_b, *, name):
    def body(a_ref, b_ref, map_ref, o_ref):
        d = a_ref[...] + b_ref[...]
        bucket = map_ref[...]
        for b in range(32):
            hit = (bucket == b)[None]
            o_ref[b] = jnp.sum(jnp.sum(jnp.where(hit, d, 0.0), axis=2), axis=1, keepdims=True)

    return pl.pallas_call(
        body, name=name, out_shape=_sds((32, 8, 1), F32), compiler_params=_params())(
            dt5_a, dt5_b, jnp.asarray(t5_bucket_map()))


def rpb_grad(dbias, *, name):
    q = np.arange(GRID_W)[:, None]
    kc = np.arange(GRID_W)[None, :]
    diag = ((kc - q + NA_WC - 1)[..., None] == np.arange(128)).reshape(GRID_W * GRID_W, 128).astype(np.float32)
    p = np.arange(NA_WR)[:, None]
    kr = np.arange(NA_WR)[None, :]
    rows = ((kr - p + NA_WR - 1).reshape(-1)[None, :] == np.arange(16)[:, None]).astype(np.float32)
    d = dbias.reshape(NA_WR, NA_HEADS, GRID_W, NA_WR, GRID_W).transpose(0, 1, 3, 2, 4).reshape(512, GRID_W * GRID_W)
    e = mm(d, jnp.asarray(diag), name=name + "_cols", exact=True, tk=1024)
    e = e.reshape(NA_WR, NA_HEADS, NA_WR, 128).transpose(0, 2, 1, 3).reshape(NA_WR * NA_WR, NA_HEADS * 128)
    out = mm(jnp.asarray(rows), e, name=name + "_rows", exact=True)
    return out.reshape(16, NA_HEADS, 128)[:2 * NA_WR - 1, :, :2 * NA_WC - 1].transpose(1, 0, 2)


BIG = ("ffn1_w_gate", "ffn1_w_up", "ffn1_w_down", "w_in", "w_branch_na", "w_branch_sw", "w_out",
       "ffn2_w_gate", "ffn2_w_up", "ffn2_w_down")
SMALL = ("ffn1_norm", "mix_norm", "b_gate", "na_q_norm", "na_k_norm", "na_rpb", "sw_q_norm", "sw_k_norm", "sw_sink",
         "ffn2_norm")


def _cols_to_full(w4):
    return w4.transpose(1, 0, 2).reshape(w4.shape[1], NSH * w4.shape[2])


def _full_to_cols(w):
    return w.reshape(w.shape[0], NSH, w.shape[1] // NSH).transpose(1, 0, 2)


def _mixer_weights(g):
    w_in = _cols_to_full(g["w_in"])
    return dict(w_att=w_in[:, :ATT_W], w_gz=w_in[:, ATT_W:], wa=_cols_to_full(g["w_branch_na"]),
                ws=_cols_to_full(g["w_branch_sw"]), wo=g["w_out"].reshape(DM, DM))


def layer_fwd(x, p, g, t5b, tag):
    w = _mixer_weights(g)
    row = lambda v: v.reshape(1, -1)
    y1, h1, gg1, uu1 = ffn_fwd(x, row(p["ffn1_norm"]), g["ffn1_w_gate"], g["ffn1_w_up"], g["ffn1_w_down"], name="ffn_fwd")
    hm = rms_fwd(y1, row(p["mix_norm"]), name="mix_norm_fwd")
    z = mm(hm, w["w_att"], name="proj_att", tn=768)
    zg = mm(hm, w["w_gz"], name="proj_gate", tn=1024)
    qa, ka, va, qs, kv = qknorm_fwd(z, p["na_q_norm"], p["na_k_norm"], p["sw_q_norm"], p["sw_k_norm"], name="qknorm_fwd")
    bias = na_bias_table(p["na_rpb"], name="na_bias_table")
    o_na, lse_na = na_fwd(qa, ka, va, bias, name="na_fwd")
    kvp = jnp.pad(kv, ((SW_BLK, SW_BLK), (0, 0)))
    sink = row(p["sw_sink"])
    o_sw, lse_sw = sw_fwd(qs, kvp, t5b, sink, name="sw_fwd")
    pa = mm(o_na, w["wa"], name="branch_na")
    ps = mm(o_sw, w["ws"], name="branch_sw")
    merged = gate_fwd(zg, row(p["b_gate"]), pa, ps, name="gate_fwd")
    y2 = mm(merged, w["wo"], add=y1, name="out_proj")
    y3, h2, gg2, uu2 = ffn_fwd(y2, row(p["ffn2_norm"]), g["ffn2_w_gate"], g["ffn2_w_up"], g["ffn2_w_down"], name="ffn_fwd")
    saved = dict(x=x, y1=y1, h1=h1, gg1=gg1, uu1=uu1, hm=hm, z=z, zg=zg, qa=qa, ka=ka, va=va, qs=qs, kvp=kvp, bias=bias,
                 o_na=o_na, lse_na=lse_na, o_sw=o_sw, lse_sw=lse_sw, pa=pa, ps=ps, merged=merged, y2=y2, h2=h2, gg2=gg2,
                 uu2=uu2, w=w, sink=sink)
    return y3, saved


def layer_bwd(dy3, sv, p, g, t5b):
    w = sv["w"]
    row = lambda v: v.reshape(1, -1)
    fold = lambda v: v.reshape(-1, HD).sum(axis=0)
    big, small = {}, {}
    dy2, small["ffn2_norm"], act, dg, du = ffn_bwd_tokens(
        dy3, sv["y2"], row(p["ffn2_norm"]), sv["gg2"], sv["uu2"], g["ffn2_w_gate"], g["ffn2_w_up"], g["ffn2_w_down"],
        name="ffn_bwd_tokens")
    big["ffn2_w_gate"], big["ffn2_w_up"], big["ffn2_w_down"] = ffn_bwd_weights(sv["h2"], dy3, act, dg, du, name="ffn_bwd_weights")
    dmerged = mm(dy2, w["wo"], tb=True, name="out_proj_dx")
    big["w_out"] = mm(sv["merged"], dy2, ta=True, out_dtype=BF16, tk=512, name="out_proj_dw").reshape(NSH, DM // NSH, DM)
    dpa, dps, dzg, small["b_gate"] = gate_bwd(dmerged, sv["zg"], row(p["b_gate"]), sv["pa"], sv["ps"], name="gate_bwd")
    big["w_branch_na"] = _full_to_cols(mm(sv["o_na"], dpa, ta=True, out_dtype=BF16, tk=512, name="branch_dw"))
    big["w_branch_sw"] = _full_to_cols(mm(sv["o_sw"], dps, ta=True, out_dtype=BF16, tk=512, name="branch_dw"))
    do_na = mm(dpa, w["wa"], tb=True, out_dtype=BF16, name="branch_dx")
    do_sw = mm(dps, w["ws"], tb=True, out_dtype=BF16, name="branch_dx")
    dqa, dka, dva, dbias = na_bwd(sv["qa"], sv["ka"], sv["va"], sv["o_na"], do_na, sv["lse_na"], sv["bias"], name="na_bwd")
    dqs, dkvp, dt5, dsink = sw_bwd(sv["qs"], sv["kvp"], sv["o_sw"], do_sw, sv["lse_sw"], t5b, sv["sink"], name="sw_bwd")
    dkv = dkvp[SW_BLK:SW_BLK + SEQ]
    dz, dgqa, dgka, dgqs, dgks = qknorm_bwd(sv["z"], dqa, dka, dva, dqs, dkv, p["na_q_norm"], p["na_k_norm"],
                                            p["sw_q_norm"], p["sw_k_norm"], name="qknorm_bwd")
    small["na_q_norm"], small["na_k_norm"], small["sw_q_norm"], small["sw_k_norm"] = fold(dgqa), fold(dgka), fold(dgqs), fold(dgks)
    small["na_rpb"] = rpb_grad(dbias, name="rpb_grad")
    small["sw_sink"] = dsink
    gw_att = mm(sv["hm"], dz, ta=True, out_dtype=BF16, tk=512, name="proj_att_dw")
    gw_gz = mm(sv["hm"], dzg, ta=True, out_dtype=BF16, tk=512, name="proj_gate_dw")
    big["w_in"] = _full_to_cols(jnp.concatenate([gw_att, gw_gz], axis=1))
    dh = mm(dz, w["w_att"], tb=True, name="proj_att_dx")
    dh = mm(dzg, w["w_gz"], tb=True, add=dh, name="proj_gate_dx")
    dy1, small["mix_norm"] = rms_bwd(dh, sv["y1"], row(p["mix_norm"]), dy2, name="mix_norm_bwd")
    dx, small["ffn1_norm"], act, dg, du = ffn_bwd_tokens(
        dy1, sv["x"], row(p["ffn1_norm"]), sv["gg1"], sv["uu1"], g["ffn1_w_gate"], g["ffn1_w_up"], g["ffn1_w_down"],
        name="ffn_bwd_tokens")
    big["ffn1_w_gate"], big["ffn1_w_up"], big["ffn1_w_down"] = ffn_bwd_weights(sv["h1"], dy1, act, dg, du, name="ffn_bwd_weights")
    return dx, big, small, dt5


def local_step(x, target, small_params, t5_table, gathered):
    t5b = t5_bias(t5_table, name="t5_bias")
    h, saved = x, []
    for l in range(DEPTH):
        p = {n: small_params[n][l] for n in SMALL}
        h, sv = layer_fwd(h, p, gathered[l], t5b, l)
        saved.append(sv)
    dy, loss_row = loss_head(h, target, name="loss_head")
    bigs, smalls, dt5s = [None] * DEPTH, [None] * DEPTH, [None] * DEPTH
    for l in reversed(range(DEPTH)):
        p = {n: small_params[n][l] for n in SMALL}
        dy, bigs[l], smalls[l], dt5s[l] = layer_bwd(dy, saved[l], p, gathered[l], t5b)
    dt5_table = t5_table_grad(dt5s[0], dt5s[1], name="t5_table_grad").reshape(32, 8)
    return loss_row, dy, bigs, smalls, dt5_table


ANY = pl.BlockSpec(memory_space=pl.ANY)


def _place():
    x, y, c = lax.axis_index("x"), lax.axis_index("y"), lax.axis_index("c")
    chips = [(1 - x, y), (x, 1 - y), (1 - x, 1 - y)]
    return x, y, c, chips


def _remote(src, dst, send_sem, recv_sem, to):
    return pltpu.make_async_remote_copy(src_ref=src, dst_ref=dst, send_sem=send_sem, recv_sem=recv_sem, device_id=to,
                                        device_id_type=MESH)


def gather_weights(shards, *, name):
    n = len(shards)

    def body(*refs):
        w = refs[:n]
        out = [refs[n + 2 * a:n + 2 * a + 2] for a in range(n)]
        send_sems, recv_sems, own_send, own_recv = refs[3 * n:]
        x, y, c, chips = _place()
        me, sibling = 2 * x + y, (x, y, 1 - c)

        own = []
        for a in range(n):
            for l in range(DEPTH):
                cp = _remote(w[a].at[l], out[a][l].at[me], own_send.at[a, l], own_recv.at[a, l], sibling)
                cp.start()
                own.append(cp)

        def on_layer(fn):
            for l in range(DEPTH):
                pl.when(c == l)(functools.partial(fn, l))

        def send_own(l):
            for a in range(n):
                for k, chip in enumerate(chips):
                    _remote(w[a].at[l], out[a][l].at[me], send_sems.at[a, k], recv_sems.at[a, k], (*chip, c)).start()

        def forward(l):
            for a in range(n):
                for k, (cx, cy) in enumerate(chips):
                    blk = out[a][l].at[2 * cx + cy]
                    _remote(blk, blk, send_sems.at[a, k], recv_sems.at[a, k], (cx, cy, c)).wait_recv()
                    _remote(blk, blk, send_sems.at[a, 3 + k], recv_sems.at[a, 3 + k], sibling).start()

        def finish(l):
            for a in range(n):
                for k, (cx, cy) in enumerate(chips):
                    other = out[a][1 - l].at[2 * cx + cy]
                    _remote(other, other, send_sems.at[a, 3 + k], recv_sems.at[a, 3 + k], sibling).wait_recv()
                for k, (cx, cy) in enumerate(chips):
                    blk = out[a][l].at[2 * cx + cy]
                    _remote(w[a].at[l], blk, send_sems.at[a, k], recv_sems.at[a, k], (cx, cy, c)).wait_send()
                    _remote(blk, blk, send_sems.at[a, 3 + k], recv_sems.at[a, 3 + k], sibling).wait_send()

        on_layer(send_own)
        on_layer(forward)
        on_layer(finish)
        for cp in own:
            cp.wait()

    out_shape = []
    for s in shards:
        out_shape += [pltpu.HBM((NSH,) + s.shape[1:], s.dtype)] * DEPTH
    res = pl.pallas_call(
        body, name=name, out_shape=tuple(out_shape), in_specs=[ANY] * n, out_specs=tuple([ANY] * (2 * n)),
        scratch_shapes=[pltpu.SemaphoreType.DMA((n, 6)), pltpu.SemaphoreType.DMA((n, 6)), pltpu.SemaphoreType.DMA((n, DEPTH)),
                        pltpu.SemaphoreType.DMA((n, DEPTH))],
    )(*shards)
    return [res[2 * a:2 * a + 2] for a in range(n)]


def pair_exchange(grads, *, name):
    n = len(grads)

    def body(*refs):
        g, buf = refs[:n], refs[n:2 * n]
        send_sems, recv_sems = refs[2 * n:]
        x, y, c, _ = _place()
        copies = []
        for a in range(n):
            half = g[a].shape[1] // 2
            cp = _remote(g[a].at[:, pl.ds((1 - c) * half, half)], buf[a], send_sems.at[a], recv_sems.at[a], (x, y, 1 - c))
            cp.start()
            copies.append(cp)
        for cp in copies:
            cp.wait()

    return pl.pallas_call(
        body, name=name, out_shape=tuple(pltpu.HBM((NSH, g.shape[1] // 2, g.shape[2]), g.dtype) for g in grads),
        in_specs=[ANY] * n, out_specs=tuple([ANY] * n),
        scratch_shapes=[pltpu.SemaphoreType.DMA((n,)), pltpu.SemaphoreType.DMA((n,))])(*grads)


def chip_exchange(sums, *, name):
    n = len(sums)

    def body(*refs):
        s, buf = refs[:n], refs[n:2 * n]
        send_sems, recv_sems = refs[2 * n:]
        x, y, c, chips = _place()
        copies = []
        for a in range(n):
            for k, (cx, cy) in enumerate(chips):
                cp = _remote(s[a].at[2 * cx + cy], buf[a].at[k], send_sems.at[a, k], recv_sems.at[a, k], (cx, cy, c))
                cp.start()
                copies.append(cp)
        for cp in copies:
            cp.wait()

    return pl.pallas_call(
        body, name=name, out_shape=tuple(pltpu.HBM((3,) + s.shape[1:], s.dtype) for s in sums),
        in_specs=[ANY] * n, out_specs=tuple([ANY] * n),
        scratch_shapes=[pltpu.SemaphoreType.DMA((n, 3)), pltpu.SemaphoreType.DMA((n, 3))])(*sums)


def pair_send(halves, *, name):
    n = len(halves[0])

    def body(*refs):
        h, got = refs[:DEPTH * n], refs[DEPTH * n:2 * DEPTH * n]
        send_sems, recv_sems = refs[2 * DEPTH * n:]
        x, y, c, _ = _place()
        copies = []
        for i in range(DEPTH * n):
            cp = _remote(h[i], got[i], send_sems.at[i], recv_sems.at[i], (x, y, 1 - c))
            cp.start()
            copies.append(cp)
        for cp in copies:
            cp.wait()

    flat = [v for l in range(DEPTH) for v in halves[l]]
    got = pl.pallas_call(
        body, name=name, out_shape=tuple(pltpu.HBM(v.shape, v.dtype) for v in flat),
        in_specs=[ANY] * (DEPTH * n), out_specs=tuple([ANY] * (DEPTH * n)),
        scratch_shapes=[pltpu.SemaphoreType.DMA((DEPTH * n,)), pltpu.SemaphoreType.DMA((DEPTH * n,))])(*flat)
    return [list(got[l * n:(l + 1) * n]) for l in range(DEPTH)]


def allreduce_small(v, *, name):
    rows = v.shape[0]

    def body(v_ref, o_ref, gath, send_sems, recv_sems):
        x, y, c, _ = _place()
        me = 4 * x + 2 * y + c
        gath[me] = v_ref[...]
        copies = []
        for k in range(1, 8):
            fx, fy, fc = (k >> 2) & 1, (k >> 1) & 1, k & 1
            peer = (jnp.where(fx, 1 - x, x), jnp.where(fy, 1 - y, y), jnp.where(fc, 1 - c, c))
            cp = _remote(v_ref, gath.at[me], send_sems.at[k - 1], recv_sems.at[k - 1], peer)
            cp.start()
            copies.append(cp)
        for cp in copies:
            cp.wait()
        acc = gath[0]
        for d in range(1, 8):
            acc = acc + gath[d]
        o_ref[...] = acc

    return pl.pallas_call(
        body, name=name, out_shape=_sds(v.shape, F32),
        in_specs=[pl.BlockSpec(memory_space=pltpu.VMEM)], out_specs=pl.BlockSpec(memory_space=pltpu.VMEM),
        scratch_shapes=[pltpu.VMEM((8, rows, 128), F32), pltpu.SemaphoreType.DMA((7,)), pltpu.SemaphoreType.DMA((7,))])(v)


def add_halves(g, buf, cidx, *, name):
    _, k, n = g.shape

    def body(c_ref, g_ref, b_ref, o_ref):
        o_ref[...] = (g_ref[...].astype(F32) + b_ref[...].astype(F32)).astype(BF16)

    blk = pl.BlockSpec((None, k // 2, n), lambda s, c: (s, 0, 0))
    return pl.pallas_call(
        body, name=name, out_shape=_sds(buf.shape, BF16),
        grid_spec=pltpu.PrefetchScalarGridSpec(
            num_scalar_prefetch=1, grid=(NSH,),
            in_specs=[pl.BlockSpec((None, k // 2, n), lambda s, c: (s, c[0], 0)), blk], out_specs=blk),
        compiler_params=_params(("parallel",)))(cidx, g, buf)


def add_chips(sums, buf, sidx, *, name):
    _, kh, n = sums.shape

    def body(s_ref, mine_ref, b_ref, o_ref):
        o_ref[...] = (mine_ref[...].astype(F32) + b_ref[0].astype(F32)) + (b_ref[1].astype(F32) + b_ref[2].astype(F32))

    return pl.pallas_call(
        body, name=name, out_shape=_sds((kh, n), F32),
        grid_spec=pltpu.PrefetchScalarGridSpec(
            num_scalar_prefetch=1, grid=(1,),
            in_specs=[pl.BlockSpec((None, kh, n), lambda i, s: (s[0], 0, 0)), pl.BlockSpec((3, kh, n), lambda i, s: (0, 0, 0))],
            out_specs=pl.BlockSpec((kh, n), lambda i, s: (0, 0))),
        compiler_params=_params(("arbitrary",)))(sidx, sums, buf)


PARAMS = ("ffn1_norm", "ffn1_w_gate", "ffn1_w_up", "ffn1_w_down", "mix_norm", "w_in", "b_gate", "na_q_norm", "na_k_norm",
          "na_rpb", "sw_q_norm", "sw_k_norm", "sw_sink", "t5_rel_table", "w_branch_na", "w_branch_sw", "w_out", "ffn2_norm",
          "ffn2_w_gate", "ffn2_w_up", "ffn2_w_down")
SMALL_ALL = tuple(n for n in PARAMS if n not in BIG)
SMALL_ROWS = 152


def _pack_small(vals):
    flat = jnp.concatenate([vals[n].reshape(-1).astype(F32) for n in SMALL_ALL] + [vals["loss"].reshape(-1)])
    return jnp.pad(flat, (0, SMALL_ROWS * 128 - flat.shape[0])).reshape(SMALL_ROWS, 128)


def _unpack_small(packed, like):
    flat, out, off = packed.reshape(-1), {}, 0
    for n in SMALL_ALL:
        size = math.prod(like[n].shape)
        out[n] = flat[off:off + size].reshape(like[n].shape)
        off += size
    out["loss"] = flat[off]
    return out


def kernel(x, ffn1_norm, ffn1_w_gate, ffn1_w_up, ffn1_w_down, mix_norm, w_in, b_gate, na_q_norm, na_k_norm, na_rpb, sw_q_norm, sw_k_norm, sw_sink, t5_rel_table, w_branch_na, w_branch_sw, w_out, ffn2_norm, ffn2_w_gate, ffn2_w_up, ffn2_w_down, loss_target, m_ffn1_norm, m_ffn1_w_gate, m_ffn1_w_up, m_ffn1_w_down, m_mix_norm, m_w_in, m_b_gate, m_na_q_norm, m_na_k_norm, m_na_rpb, m_sw_q_norm, m_sw_k_norm, m_sw_sink, m_t5_rel_table, m_w_branch_na, m_w_branch_sw, m_w_out, m_ffn2_norm, m_ffn2_w_gate, m_ffn2_w_up, m_ffn2_w_down, v_ffn1_norm, v_ffn1_w_gate, v_ffn1_w_up, v_ffn1_w_down, v_mix_norm, v_w_in, v_b_gate, v_na_q_norm, v_na_k_norm, v_na_rpb, v_sw_q_norm, v_sw_k_norm, v_sw_sink, v_t5_rel_table, v_w_branch_na, v_w_branch_sw, v_w_out, v_ffn2_norm, v_ffn2_w_gate, v_ffn2_w_up, v_ffn2_w_down):
    args = locals()
    w = {n: args[n] for n in PARAMS}
    m = {n: args["m_" + n] for n in PARAMS}
    v = {n: args["v_" + n] for n in PARAMS}
    cidx = lax.axis_index("c").astype(jnp.int32).reshape(1)
    sidx = (2 * lax.axis_index("x") + lax.axis_index("y")).astype(jnp.int32).reshape(1)

    got = gather_weights([w[n].astype(BF16) for n in BIG], name="gather_weights")
    gathered = [{n: got[a][l] for a, n in enumerate(BIG)} for l in range(DEPTH)]

    loss_row, grad_x, bigs, smalls, dt5 = local_step(x[0], loss_target[0], {n: w[n] for n in SMALL}, w["t5_rel_table"], gathered)

    halves = []
    for l in range(DEPTH):
        grads = [bigs[l][n] for n in BIG]
        from_sibling = pair_exchange(grads, name="pair_exchange")
        sums = [add_halves(g, b, cidx, name="add_halves") for g, b in zip(grads, from_sibling)]
        from_chips = chip_exchange(sums, name="chip_exchange")
        halves.append([add_chips(s, b, sidx, name="add_chips") for s, b in zip(sums, from_chips)])
    theirs = pair_send(halves, name="pair_send")

    local_small = {n: jnp.stack([smalls[l][n].reshape(w[n].shape[1:]) for l in range(DEPTH)]) for n in SMALL}
    local_small["t5_rel_table"] = dt5
    local_small["loss"] = loss_row[0, 0:1]
    total = allreduce_small(_pack_small(local_small), name="allreduce_small")
    small_grads = _unpack_small(total, w)

    grad, delta, new_m, new_v = {}, {}, {}, {}
    for a, n in enumerate(BIG):
        grad[n], delta[n], new_m[n], new_v[n] = adamw_shard(
            w[n], m[n], v[n], [halves[l][a] for l in range(DEPTH)], [theirs[l][a] for l in range(DEPTH)], cidx, name="adamw_shard")
    pack = lambda d: _pack_small({**d, "loss": jnp.zeros((1,), F32)})[None]
    ds, ms, vs = adamw(pack(w), total[None], pack(m), pack(v), name="adamw_small")
    for n in SMALL_ALL:
        grad[n] = small_grads[n]
    d_s, m_s, v_s = _unpack_small(ds[0], w), _unpack_small(ms[0], w), _unpack_small(vs[0], w)
    for n in SMALL_ALL:
        delta[n], new_m[n], new_v[n] = d_s[n], m_s[n], v_s[n]

    return (small_grads["loss"], grad_x[None], *[grad[n] for n in PARAMS], *[delta[n] for n in PARAMS],
            *[new_m[n] for n in PARAMS], *[new_v[n] for n in PARAMS])
```

```python
import functools
import math

import jax
import jax.numpy as jnp
import numpy as np
from jax import lax
from jax.experimental import pallas as pl
from jax.experimental.pallas import tpu as pltpu

F32 = jnp.float32
BF16 = jnp.bfloat16

SEQ = 2048
DM = 1024
DFF = 2816
DEPTH = 2
NSH = 4
FSH = DFF // NSH
GRID_W = 64
ROWS = SEQ // GRID_W
NA_HEADS = 8
HD = 64
NA_WR = 8
NA_WC = 16
NA_KEYS = NA_WR * GRID_W
SW_BLK = 128
SW_NB = SEQ // SW_BLK
SW_KEYS = 3 * SW_BLK
ATT_W = 2304
GATE_W = 2048
IN_W = ATT_W + GATE_W
EPS = 1e-6
NEG = -1e30
QK_SCALE = 1.0 / math.sqrt(HD)

ADAM_LR = 0.001
ADAM_B1 = 0.9
ADAM_B2 = 0.999
ADAM_EPS = 1e-08
ADAM_WD = 0.01
ADAM_STEP = 10

VMEM_LIMIT = 56 << 20
MESH = pl.DeviceIdType.MESH

NT = (((1,), (1,)), ((), ()))
TN = (((0,), (0,)), ((), ()))
NN = (((1,), (0,)), ((), ()))


def _dot(a, b, dims=NN):
    return lax.dot_general(a, b, dims, preferred_element_type=F32)


def _params(sem=None):
    return pltpu.CompilerParams(dimension_semantics=sem, vmem_limit_bytes=VMEM_LIMIT)


def _sds(shape, dtype):
    return jax.ShapeDtypeStruct(shape, dtype)


def mm(a, b, *, name, ta=False, tb=False, out_dtype=F32, add=None, scale=None, tm=512, tn=None, tk=None, exact=False):
    m, kd = (a.shape[1], a.shape[0]) if ta else a.shape
    n = b.shape[0] if tb else b.shape[1]
    tm, tn, tk = min(tm, m), min(tn or n, n), min(tk or kd, kd)
    nk = kd // tk
    dims = (((0 if ta else 1,), (1 if tb else 0,)), ((), ()))

    def body(*refs):
        if add is None:
            a_ref, b_ref, o_ref, acc = refs
        else:
            a_ref, b_ref, add_ref, o_ref, acc = refs
        k = pl.program_id(2)

        @pl.when(k == 0)
        def _():
            acc[...] = jnp.zeros_like(acc)

        if exact:
            acc[...] += lax.dot_general(a_ref[...], b_ref[...], dims, precision=lax.Precision.HIGHEST,
                                        preferred_element_type=F32)
        else:
            acc[...] += lax.dot_general(a_ref[...].astype(BF16), b_ref[...].astype(BF16), dims,
                                        preferred_element_type=F32)

        @pl.when(k == nk - 1)
        def _():
            r = acc[...]
            if scale is not None:
                r = r * scale
            if add is not None:
                r = r + add_ref[...]
            o_ref[...] = r.astype(out_dtype)

    a_spec = pl.BlockSpec((tk, tm), lambda i, j, k: (k, i)) if ta else pl.BlockSpec((tm, tk), lambda i, j, k: (i, k))
    b_spec = pl.BlockSpec((tn, tk), lambda i, j, k: (j, k)) if tb else pl.BlockSpec((tk, tn), lambda i, j, k: (k, j))
    o_spec = pl.BlockSpec((tm, tn), lambda i, j, k: (i, j))
    ins, specs = [a, b], [a_spec, b_spec]
    if add is not None:
        ins.append(add)
        specs.append(o_spec)
    return pl.pallas_call(
        body, name=name, out_shape=_sds((m, n), out_dtype), grid=(m // tm, n // tn, nk), in_specs=specs,
        out_specs=o_spec, scratch_shapes=[pltpu.VMEM((tm, tn), F32)],
        compiler_params=_params(("parallel", "parallel", "arbitrary")))(*ins)


def _rms(x):
    return lax.rsqrt(jnp.mean(x * x, axis=-1, keepdims=True) + EPS)


def rms_fwd(x, gain, *, name, tm=512):
    def body(x_ref, g_ref, h_ref):
        x = x_ref[...]
        h_ref[...] = (x * _rms(x) * g_ref[...]).astype(BF16)

    return pl.pallas_call(
        body, name=name, out_shape=_sds(x.shape, BF16), grid=(x.shape[0] // tm,),
        in_specs=[pl.BlockSpec((tm, DM), lambda i: (i, 0)), pl.BlockSpec((1, DM), lambda i: (0, 0))],
        out_specs=pl.BlockSpec((tm, DM), lambda i: (i, 0)), compiler_params=_params(("parallel",)))(x, gain)


def _rms_bwd_math(dh, x, gain):
    r = _rms(x)
    xh = x * r
    dgain = jnp.sum(dh * xh, axis=0, keepdims=True)
    dxn = dh * gain
    dx = r * (dxn - xh * jnp.mean(dxn * xh, axis=-1, keepdims=True))
    return dx, dgain


def rms_bwd(dh, x, gain, dres, *, name, tm=512):
    def body(dh_ref, x_ref, g_ref, dres_ref, dx_ref, dg_ref):
        @pl.when(pl.program_id(0) == 0)
        def _():
            dg_ref[...] = jnp.zeros_like(dg_ref)

        dx, dg = _rms_bwd_math(dh_ref[...], x_ref[...], g_ref[...])
        dx_ref[...] = dres_ref[...] + dx
        dg_ref[...] += dg

    tile = pl.BlockSpec((tm, DM), lambda i: (i, 0))
    vec = pl.BlockSpec((1, DM), lambda i: (0, 0))
    return pl.pallas_call(
        body, name=name, out_shape=(_sds(x.shape, F32), _sds((1, DM), F32)), grid=(x.shape[0] // tm,),
        in_specs=[tile, tile, vec, tile], out_specs=(tile, vec), compiler_params=_params(("arbitrary",)))(dh, x, gain, dres)


def _with_dep(ins, specs, dep):
    if dep is None:
        return ins, specs
    return ins + [dep], specs + [pl.BlockSpec(memory_space=pl.ANY)]


def ffn_fwd(x, gain, wg, wu, wd, *, name, tm=512, dep=None):
    def body(x_ref, g_ref, wg_ref, wu_ref, wd_ref, *rest):
        y_ref, h_ref, gg_ref, uu_ref, acc = rest[-5:]
        f = pl.program_id(1)

        @pl.when(f == 0)
        def _():
            x = x_ref[...]
            h_ref[...] = (x * _rms(x) * g_ref[...]).astype(BF16)
            acc[...] = jnp.zeros_like(acc)

        h = h_ref[...]
        gg = _dot(h, wg_ref[...])
        uu = _dot(h, wu_ref[...])
        gg_ref[...] = gg.astype(BF16)
        uu_ref[...] = uu.astype(BF16)
        act = (gg * jax.nn.sigmoid(gg) * uu).astype(BF16)
        acc[...] += _dot(act, wd_ref[...])

        @pl.when(f == NSH - 1)
        def _():
            y_ref[...] = x_ref[...] + 0.5 * acc[...]

    s = x.shape[0]
    tile = pl.BlockSpec((tm, DM), lambda i, f: (i, 0))
    w_in = pl.BlockSpec((None, DM, FSH), lambda i, f: (f, 0, 0))
    w_out = pl.BlockSpec((None, FSH, DM), lambda i, f: (f, 0, 0))
    hid = pl.BlockSpec((None, tm, FSH), lambda i, f: (f, i, 0))
    ins, specs = _with_dep([x, gain, wg, wu, wd], [tile, pl.BlockSpec((1, DM), lambda i, f: (0, 0)), w_in, w_in, w_out], dep)
    return pl.pallas_call(
        body, name=name,
        out_shape=(_sds((s, DM), F32), _sds((s, DM), BF16), _sds((NSH, s, FSH), BF16), _sds((NSH, s, FSH), BF16)),
        grid=(s // tm, NSH), in_specs=specs,
        out_specs=(tile, tile, hid, hid), scratch_shapes=[pltpu.VMEM((tm, DM), F32)],
        compiler_params=_params(("parallel", "arbitrary")))(*ins)


def ffn_bwd_tokens(dy, x, gain, gg, uu, wg, wu, wd, *, name, tm=512, dep=None):
    def body(dy_ref, x_ref, g_ref, gg_ref, uu_ref, wg_ref, wu_ref, wd_ref, *rest):
        dx_ref, dgain_ref, act_ref, dg_ref, du_ref, dh = rest[-6:]
        i, f = pl.program_id(0), pl.program_id(1)

        @pl.when(f == 0)
        def _():
            dh[...] = jnp.zeros_like(dh)

        @pl.when((i == 0) & (f == 0))
        def _():
            dgain_ref[...] = jnp.zeros_like(dgain_ref)

        dact = _dot((0.5 * dy_ref[...]).astype(BF16), wd_ref[...], NT)
        g = gg_ref[...].astype(F32)
        u = uu_ref[...].astype(F32)
        sg = jax.nn.sigmoid(g)
        silu = g * sg
        act_ref[...] = (silu * u).astype(BF16)
        dg = (dact * u * (sg * (1.0 + g * (1.0 - sg)))).astype(BF16)
        du = (dact * silu).astype(BF16)
        dg_ref[...] = dg
        du_ref[...] = du
        dh[...] += _dot(dg, wg_ref[...], NT) + _dot(du, wu_ref[...], NT)

        @pl.when(f == NSH - 1)
        def _():
            dx, dgain = _rms_bwd_math(dh[...], x_ref[...], g_ref[...])
            dx_ref[...] = dy_ref[...] + dx
            dgain_ref[...] += dgain

    s = x.shape[0]
    tile = pl.BlockSpec((tm, DM), lambda i, f: (i, 0))
    vec = pl.BlockSpec((1, DM), lambda i, f: (0, 0))
    w_in = pl.BlockSpec((None, DM, FSH), lambda i, f: (f, 0, 0))
    w_out = pl.BlockSpec((None, FSH, DM), lambda i, f: (f, 0, 0))
    hid = pl.BlockSpec((None, tm, FSH), lambda i, f: (f, i, 0))
    hshape = _sds((NSH, s, FSH), BF16)
    ins, specs = _with_dep([dy, x, gain, gg, uu, wg, wu, wd], [tile, tile, vec, hid, hid, w_in, w_in, w_out], dep)
    return pl.pallas_call(
        body, name=name, out_shape=(_sds((s, DM), F32), _sds((1, DM), F32), hshape, hshape, hshape),
        grid=(s // tm, NSH), in_specs=specs,
        out_specs=(tile, vec, hid, hid, hid), scratch_shapes=[pltpu.VMEM((tm, DM), F32)],
        compiler_params=_params(("arbitrary", "arbitrary")))(*ins)


def ffn_bwd_weights(h, dy, act, dg, du, *, name, tk=512):
    s = h.shape[0]
    nk = s // tk

    def body(h_ref, dy_ref, act_ref, dg_ref, du_ref, gwg_ref, gwu_ref, gwd_ref, ag, au, ad):
        k = pl.program_id(1)

        @pl.when(k == 0)
        def _():
            ag[...] = jnp.zeros_like(ag)
            au[...] = jnp.zeros_like(au)
            ad[...] = jnp.zeros_like(ad)

        h = h_ref[...]
        ag[...] += _dot(h, dg_ref[...], TN)
        au[...] += _dot(h, du_ref[...], TN)
        ad[...] += _dot(act_ref[...], dy_ref[...].astype(BF16), TN)

        @pl.when(k == nk - 1)
        def _():
            gwg_ref[...] = ag[...].astype(BF16)
            gwu_ref[...] = au[...].astype(BF16)
            gwd_ref[...] = (0.5 * ad[...]).astype(BF16)

    tile = pl.BlockSpec((tk, DM), lambda f, k: (k, 0))
    hid = pl.BlockSpec((None, tk, FSH), lambda f, k: (f, k, 0))
    w_in = pl.BlockSpec((None, DM, FSH), lambda f, k: (f, 0, 0))
    w_out = pl.BlockSpec((None, FSH, DM), lambda f, k: (f, 0, 0))
    return pl.pallas_call(
        body, name=name,
        out_shape=(_sds((NSH, DM, FSH), BF16), _sds((NSH, DM, FSH), BF16), _sds((NSH, FSH, DM), BF16)),
        grid=(NSH, nk), in_specs=[tile, tile, hid, hid, hid], out_specs=(w_in, w_in, w_out),
        scratch_shapes=[pltpu.VMEM((DM, FSH), F32), pltpu.VMEM((DM, FSH), F32), pltpu.VMEM((FSH, DM), F32)],
        compiler_params=_params(("parallel", "arbitrary")))(h, dy, act, dg, du)


def _group_mean(v, bd):
    return lax.dot_general(v, bd, NN, precision=lax.Precision.HIGHEST, preferred_element_type=F32)


def _block_diag(width):
    idx = np.arange(width) // HD
    return jnp.asarray((idx[:, None] == idx[None, :]).astype(np.float32) / HD)


def qknorm_fwd(z, gq_na, gk_na, gq_sw, gk_sw, *, name, tm=256):
    def body(zq_ref, zk_ref, zv_ref, zs_ref, zkv_ref, gqa_ref, gka_ref, gqs_ref, gks_ref, bd_ref, bd2_ref,
             qa_ref, ka_ref, va_ref, qs_ref, kv_ref):
        bd = bd_ref[...]

        def norm(x, g, bdm):
            return x * lax.rsqrt(_group_mean(x * x, bdm) + EPS) * g

        qa_ref[...] = (norm(zq_ref[...], gqa_ref[...], bd) * QK_SCALE).astype(BF16)
        ka_ref[...] = norm(zk_ref[...], gka_ref[...], bd).astype(BF16)
        va_ref[...] = zv_ref[...].astype(BF16)
        qs_ref[...] = (norm(zs_ref[...], gqs_ref[...], bd) * QK_SCALE).astype(BF16)
        kv = zkv_ref[...]
        kv_ref[:, 0:128] = norm(kv[:, 0:128], gks_ref[...], bd2_ref[...]).astype(BF16)
        kv_ref[:, 128:256] = kv[:, 128:256].astype(BF16)

    s = z.shape[0]
    col = lambda j: pl.BlockSpec((tm, 512), lambda i, j=j: (i, j))
    vec = lambda w: pl.BlockSpec((1, w), lambda i: (0, 0))
    o512 = pl.BlockSpec((tm, 512), lambda i: (i, 0))
    g512 = lambda g: jnp.tile(g.reshape(1, HD), (1, 8))
    return pl.pallas_call(
        body, name=name,
        out_shape=(_sds((s, 512), BF16),) * 4 + (_sds((s, 256), BF16),), grid=(s // tm,),
        in_specs=[col(0), col(1), col(2), col(3), pl.BlockSpec((tm, 256), lambda i: (i, 8)), vec(512), vec(512), vec(512),
                  vec(128), pl.BlockSpec((512, 512), lambda i: (0, 0)), pl.BlockSpec((128, 128), lambda i: (0, 0))],
        out_specs=(o512, o512, o512, o512, pl.BlockSpec((tm, 256), lambda i: (i, 0))),
        compiler_params=_params(("parallel",)))(
            z, z, z, z, z, g512(gq_na), g512(gk_na), g512(gq_sw), jnp.tile(gk_sw.reshape(1, HD), (1, 2)),
            _block_diag(512), _block_diag(128))


def qknorm_bwd(z, dqa, dka, dva, dqs, dkv, gq_na, gk_na, gq_sw, gk_sw, *, name, tm=256):
    def body(zq_ref, zk_ref, zs_ref, zkv_ref, dqa_ref, dka_ref, dva_ref, dqs_ref, dkv_ref, gqa_ref, gka_ref, gqs_ref,
             gks_ref, bd_ref, bd2_ref, dz_ref, dgqa_ref, dgka_ref, dgqs_ref, dgks_ref):
        @pl.when(pl.program_id(0) == 0)
        def _():
            dgqa_ref[...] = jnp.zeros_like(dgqa_ref)
            dgka_ref[...] = jnp.zeros_like(dgka_ref)
            dgqs_ref[...] = jnp.zeros_like(dgqs_ref)
            dgks_ref[...] = jnp.zeros_like(dgks_ref)

        bd = bd_ref[...]

        def bwd(x, dy, g, bdm, dg_ref):
            r = lax.rsqrt(_group_mean(x * x, bdm) + EPS)
            xh = x * r
            dg_ref[...] += jnp.sum(dy * xh, axis=0, keepdims=True)
            dxn = dy * g
            return r * (dxn - xh * _group_mean(dxn * xh, bdm))

        dz_ref[:, 0:512] = bwd(zq_ref[...], dqa_ref[...] * QK_SCALE, gqa_ref[...], bd, dgqa_ref).astype(BF16)
        dz_ref[:, 512:1024] = bwd(zk_ref[...], dka_ref[...], gka_ref[...], bd, dgka_ref).astype(BF16)
        dz_ref[:, 1024:1536] = dva_ref[...].astype(BF16)
        dz_ref[:, 1536:2048] = bwd(zs_ref[...], dqs_ref[...] * QK_SCALE, gqs_ref[...], bd, dgqs_ref).astype(BF16)
        dkv = dkv_ref[...]
        dz_ref[:, 2048:2176] = bwd(zkv_ref[:, 0:128], dkv[:, 0:128], gks_ref[...], bd2_ref[...], dgks_ref).astype(BF16)
        dz_ref[:, 2176:2304] = dkv[:, 128:256].astype(BF16)

    s = z.shape[0]
    col = lambda j: pl.BlockSpec((tm, 512), lambda i, j=j: (i, j))
    t512 = pl.BlockSpec((tm, 512), lambda i: (i, 0))
    t256 = pl.BlockSpec((tm, 256), lambda i: (i, 0))
    vec = lambda w: pl.BlockSpec((1, w), lambda i: (0, 0))
    g512 = lambda g: jnp.tile(g.reshape(1, HD), (1, 8))
    return pl.pallas_call(
        body, name=name,
        out_shape=(_sds((s, ATT_W), BF16), _sds((1, 512), F32), _sds((1, 512), F32), _sds((1, 512), F32), _sds((1, 128), F32)),
        grid=(s // tm,),
        in_specs=[col(0), col(1), col(3), pl.BlockSpec((tm, 256), lambda i: (i, 8)), t512, t512, t512, t512, t256,
                  vec(512), vec(512), vec(512), vec(128), pl.BlockSpec((512, 512), lambda i: (0, 0)),
                  pl.BlockSpec((128, 128), lambda i: (0, 0))],
        out_specs=(pl.BlockSpec((tm, ATT_W), lambda i: (i, 0)), vec(512), vec(512), vec(512), vec(128)),
        compiler_params=_params(("arbitrary",)))(
            z, z, z, z, dqa, dka, dva, dqs, dkv, g512(gq_na), g512(gk_na), g512(gq_sw),
            jnp.tile(gk_sw.reshape(1, HD), (1, 2)), _block_diag(512), _block_diag(128))


def _na_row_start(r):
    return jnp.clip(r - NA_WR // 2, 0, ROWS - NA_WR)


def na_bias_table(rpb, *, name):
    t = jnp.pad(rpb, ((0, 0), (0, 2), (0, HD - (2 * NA_WC - 1))))
    pairs = jnp.concatenate([t[:, :16], t[:, 1:17]], axis=-1).reshape(NA_HEADS, 16, 1, 128)

    def body(t_ref, o_ref):
        p = pl.program_id(0)
        q = lax.broadcasted_iota(jnp.int32, (GRID_W, 128), 0)
        kc = lax.broadcasted_iota(jnp.int32, (GRID_W, 128), 1) & (GRID_W - 1)
        cs = jnp.clip(q - NA_WC // 2, 0, GRID_W - NA_WC)
        ok = (kc >= cs) & (kc < cs + NA_WC)
        for h in range(NA_HEADS):
            for pr in range(NA_WR // 2):
                x = jnp.broadcast_to(t_ref[h, 2 * pr - p + NA_WR - 1], (GRID_W, 128))
                b = pltpu.roll(x, 128 - (NA_WC - 1), 1, stride=1, stride_axis=0)
                o_ref[h, :, 128 * pr:128 * pr + 128] = jnp.where(ok, b, NEG)

    return pl.pallas_call(
        body, name=name, out_shape=_sds((NA_WR, NA_HEADS, GRID_W, NA_KEYS), F32), grid=(NA_WR,),
        in_specs=[pl.BlockSpec((NA_HEADS, 16, 1, 128), lambda p: (0, 0, 0, 0))],
        out_specs=pl.BlockSpec((None, NA_HEADS, GRID_W, NA_KEYS), lambda p: (p, 0, 0, 0)),
        compiler_params=_params(("parallel",)))(pairs)


def _lane_halves():
    lane = lax.broadcasted_iota(jnp.int32, (1, 128), 1)
    return lane < HD


def na_fwd(q, k, v, bias, *, name):
    def body(q_ref, k_ref, v_ref, b_ref, o_ref, lse_ref):
        r = pl.program_id(0)
        off = pl.multiple_of(_na_row_start(r) * GRID_W, GRID_W)
        first = _lane_halves()
        for j in range(NA_HEADS // 2):
            lanes = slice(128 * j, 128 * j + 128)
            q2 = q_ref[:, lanes]
            k2 = k_ref[pl.ds(off, NA_KEYS), lanes]
            v2 = v_ref[pl.ds(off, NA_KEYS), lanes]
            zero = jnp.zeros_like(q2)
            o2 = jnp.zeros((GRID_W, 128), F32)
            for half in range(2):
                h = 2 * j + half
                sel = first if half == 0 else jnp.logical_not(first)
                s = _dot(jnp.where(sel, q2, zero), k2, NT)
                b = b_ref[h]
                s = jnp.where(b > 0.5 * NEG, s + b, NEG)
                m = jnp.max(s, axis=-1, keepdims=True)
                e = jnp.exp(s - m)
                l = jnp.sum(e, axis=-1, keepdims=True)
                p = (e / l).astype(BF16)
                o2 = o2 + _dot(p, jnp.where(sel, v2, jnp.zeros_like(v2)))
                lse_ref[:, h:h + 1] = m + jnp.log(l)
            o_ref[:, lanes] = o2.astype(BF16)

    s_tok = q.shape[0]
    full = pl.BlockSpec((s_tok, 512), lambda r: (0, 0))
    return pl.pallas_call(
        body, name=name, out_shape=(_sds((s_tok, 512), BF16), _sds((s_tok, NA_HEADS), F32)), grid=(ROWS,),
        in_specs=[pl.BlockSpec((GRID_W, 512), lambda r: (r, 0)), full, full,
                  pl.BlockSpec((None, NA_HEADS, GRID_W, NA_KEYS), lambda r: (r - _na_row_start(r), 0, 0, 0))],
        out_specs=(pl.BlockSpec((GRID_W, 512), lambda r: (r, 0)), pl.BlockSpec((GRID_W, NA_HEADS), lambda r: (r, 0))),
        compiler_params=_params(("parallel",)))(q, k, v, bias)


def na_bwd(q, k, v, o, do, lse, bias, *, name):
    def body(q_ref, k_ref, v_ref, o_ref, do_ref, lse_ref, b_ref, dq_ref, dk_ref, dv_ref, db_ref):
        r = pl.program_id(0)

        @pl.when(r == 0)
        def _():
            dk_ref[...] = jnp.zeros_like(dk_ref)
            dv_ref[...] = jnp.zeros_like(dv_ref)

        @pl.when((r <= NA_WR // 2) | (r > ROWS - NA_WR // 2))
        def _():
            db_ref[...] = jnp.zeros_like(db_ref)

        off = pl.multiple_of(_na_row_start(r) * GRID_W, GRID_W)
        first = _lane_halves()
        for j in range(NA_HEADS // 2):
            lanes = slice(128 * j, 128 * j + 128)
            q2 = q_ref[:, lanes]
            k2 = k_ref[pl.ds(off, NA_KEYS), lanes]
            v2 = v_ref[pl.ds(off, NA_KEYS), lanes]
            do2 = do_ref[:, lanes]
            prod = do2.astype(F32) * o_ref[:, lanes].astype(F32)
            dq2 = jnp.zeros((GRID_W, 128), F32)
            dk2 = jnp.zeros((NA_KEYS, 128), F32)
            dv2 = jnp.zeros((NA_KEYS, 128), F32)
            for half in range(2):
                h = 2 * j + half
                sel = first if half == 0 else jnp.logical_not(first)
                qh = jnp.where(sel, q2, jnp.zeros_like(q2))
                doh = jnp.where(sel, do2, jnp.zeros_like(do2))
                s = _dot(qh, k2, NT)
                b = b_ref[h]
                s = jnp.where(b > 0.5 * NEG, s + b, NEG)
                p = jnp.exp(s - lse_ref[:, h:h + 1])
                dp = _dot(doh, v2, NT)
                delta = jnp.sum(jnp.where(sel, prod, 0.0), axis=-1, keepdims=True)
                ds = p * (dp - delta)
                db_ref[h] += ds
                dsb = ds.astype(BF16)
                dq2 = dq2 + _dot(dsb, jnp.where(sel, k2, jnp.zeros_like(k2)))
                dk2 = dk2 + _dot(dsb, qh, TN)
                dv2 = dv2 + _dot(p.astype(BF16), doh, TN)
            dq_ref[:, lanes] = dq2
            dk_ref[pl.ds(off, NA_KEYS), lanes] += dk2
            dv_ref[pl.ds(off, NA_KEYS), lanes] += dv2

    s_tok = q.shape[0]
    full = pl.BlockSpec((s_tok, 512), lambda r: (0, 0))
    row = pl.BlockSpec((GRID_W, 512), lambda r: (r, 0))
    bias_spec = pl.BlockSpec((None, NA_HEADS, GRID_W, NA_KEYS), lambda r: (r - _na_row_start(r), 0, 0, 0))
    return pl.pallas_call(
        body, name=name,
        out_shape=(_sds((s_tok, 512), F32), _sds((s_tok, 512), F32), _sds((s_tok, 512), F32),
                   _sds((NA_WR, NA_HEADS, GRID_W, NA_KEYS), F32)),
        grid=(ROWS,),
        in_specs=[row, full, full, row, row, pl.BlockSpec((GRID_W, NA_HEADS), lambda r: (r, 0)), bias_spec],
        out_specs=(row, full, full, bias_spec), compiler_params=_params(("arbitrary",)))(q, k, v, o, do, lse, bias)


def t5_bucket_map():
    rel = np.arange(SW_KEYS)[None, :] - SW_BLK - np.arange(SW_BLK)[:, None]
    nb = 16
    max_exact = nb // 2
    n = np.abs(rel)
    large = max_exact + (np.log(np.maximum(n, 1) / max_exact) / np.log(128 / max_exact) * (nb - max_exact)).astype(np.int32)
    large = np.minimum(large, nb - 1)
    return ((rel > 0) * nb + np.where(n < max_exact, n, large)).astype(np.int32)


def t5_bias(table, *, name):
    rel = np.arange(-SW_BLK, SW_BLK + 1)
    nb, max_exact = 16, 8
    n = np.abs(rel)
    large = max_exact + (np.log(np.maximum(n, 1) / max_exact) / np.log(128 / max_exact) * (nb - max_exact)).astype(np.int32)
    bucket = ((rel > 0) * nb + np.where(n < max_exact, n, np.minimum(large, nb - 1))).astype(np.int32)
    u = jnp.pad(table[jnp.asarray(bucket)].T, ((0, 0), (0, SW_KEYS - bucket.shape[0]))).reshape(8, 1, SW_KEYS)

    def body(u_ref, o_ref):
        for h in range(8):
            x = jnp.broadcast_to(u_ref[h], (SW_BLK, SW_KEYS))
            o_ref[h] = pltpu.roll(x, 0, 1, stride=1, stride_axis=0)

    return pl.pallas_call(body, name=name, out_shape=_sds((8, SW_BLK, SW_KEYS), F32), compiler_params=_params())(u)


def _sw_valid(n):
    a = lax.broadcasted_iota(jnp.int32, (SW_BLK, SW_KEYS), 0)
    j = lax.broadcasted_iota(jnp.int32, (SW_BLK, SW_KEYS), 1)
    kpos = (n - 1) * SW_BLK + j
    return (jnp.abs(j - SW_BLK - a) <= SW_BLK) & (kpos >= 0) & (kpos < SEQ)


def _dup_group(x2, g, first):
    rolled = pltpu.roll(x2, HD, 1)
    return jnp.where(first, x2, rolled) if g == 0 else jnp.where(first, rolled, x2)


def sw_fwd(q, kv, t5, sink, *, name):
    def body(q_ref, kv_ref, t5_ref, sink_ref, o_ref, lse_ref):
        n = pl.program_id(0)
        off = pl.multiple_of(n * SW_BLK, SW_BLK)
        first = _lane_halves()
        valid = _sw_valid(n)
        k2 = kv_ref[pl.ds(off, SW_KEYS), 0:128]
        v2 = kv_ref[pl.ds(off, SW_KEYS), 128:256]
        for j in range(4):
            g = j // 2
            kk = _dup_group(k2, g, first)
            vv = _dup_group(v2, g, first)
            lanes = slice(128 * j, 128 * j + 128)
            q2 = q_ref[:, lanes]
            o2 = jnp.zeros((SW_BLK, 128), F32)
            for half in range(2):
                h = 2 * j + half
                sel = first if half == 0 else jnp.logical_not(first)
                s = _dot(jnp.where(sel, q2, jnp.zeros_like(q2)), kk, NT)
                s = jnp.where(valid, s + t5_ref[h], NEG)
                snk = sink_ref[:, h:h + 1]
                m = jnp.maximum(jnp.max(s, axis=-1, keepdims=True), snk)
                e = jnp.exp(s - m)
                den = jnp.sum(e, axis=-1, keepdims=True) + jnp.exp(snk - m)
                p = (e / den).astype(BF16)
                o2 = o2 + _dot(p, jnp.where(sel, vv, jnp.zeros_like(vv)))
                lse_ref[:, h:h + 1] = m + jnp.log(den)
            o_ref[:, lanes] = o2.astype(BF16)

    s_tok = q.shape[0]
    blk = pl.BlockSpec((SW_BLK, 512), lambda n: (n, 0))
    return pl.pallas_call(
        body, name=name, out_shape=(_sds((s_tok, 512), BF16), _sds((s_tok, 8), F32)), grid=(SW_NB,),
        in_specs=[blk, pl.BlockSpec(kv.shape, lambda n: (0, 0)), pl.BlockSpec((8, SW_BLK, SW_KEYS), lambda n: (0, 0, 0)),
                  pl.BlockSpec((1, 8), lambda n: (0, 0))],
        out_specs=(blk, pl.BlockSpec((SW_BLK, 8), lambda n: (n, 0))), compiler_params=_params(("parallel",)))(q, kv, t5, sink)


def sw_bwd(q, kv, o, do, lse, t5, sink, *, name):
    def body(q_ref, kv_ref, o_ref, do_ref, lse_ref, t5_ref, sink_ref, dq_ref, dkv_ref, dt5_ref, dsink_ref):
        n = pl.program_id(0)

        @pl.when(n == 0)
        def _():
            dkv_ref[...] = jnp.zeros_like(dkv_ref)
            dt5_ref[...] = jnp.zeros_like(dt5_ref)
            dsink_ref[...] = jnp.zeros_like(dsink_ref)

        off = pl.multiple_of(n * SW_BLK, SW_BLK)
        first = _lane_halves()
        valid = _sw_valid(n)
        k2 = kv_ref[pl.ds(off, SW_KEYS), 0:128]
        v2 = kv_ref[pl.ds(off, SW_KEYS), 128:256]
        dk_groups, dv_groups = [], []
        for g in range(2):
            kk = _dup_group(k2, g, first)
            vv = _dup_group(v2, g, first)
            dkk = jnp.zeros((SW_KEYS, 128), F32)
            dvv = jnp.zeros((SW_KEYS, 128), F32)
            for j in (2 * g, 2 * g + 1):
                lanes = slice(128 * j, 128 * j + 128)
                q2 = q_ref[:, lanes]
                do2 = do_ref[:, lanes]
                prod = do2.astype(F32) * o_ref[:, lanes].astype(F32)
                dq2 = jnp.zeros((SW_BLK, 128), F32)
                for half in range(2):
                    h = 2 * j + half
                    sel = first if half == 0 else jnp.logical_not(first)
                    qh = jnp.where(sel, q2, jnp.zeros_like(q2))
                    doh = jnp.where(sel, do2, jnp.zeros_like(do2))
                    s = _dot(qh, kk, NT)
                    s = jnp.where(valid, s + t5_ref[h], NEG)
                    lse = lse_ref[:, h:h + 1]
                    p = jnp.exp(s - lse)
                    dp = _dot(doh, vv, NT)
                    delta = jnp.sum(jnp.where(sel, prod, 0.0), axis=-1, keepdims=True)
                    ds = p * (dp - delta)
                    dt5_ref[h] += ds
                    dsink_ref[:, h:h + 1] += -jnp.sum(jnp.exp(sink_ref[:, h:h + 1] - lse) * delta, axis=0, keepdims=True)
                    dsb = ds.astype(BF16)
                    dq2 = dq2 + _dot(dsb, jnp.where(sel, kk, jnp.zeros_like(kk)))
                    dkk = dkk + _dot(dsb, qh, TN)
                    dvv = dvv + _dot(p.astype(BF16), doh, TN)
                dq_ref[:, lanes] = dq2
            dk_groups.append(dkk + pltpu.roll(dkk, HD, 1))
            dv_groups.append(dvv + pltpu.roll(dvv, HD, 1))
        dkv_ref[pl.ds(off, SW_KEYS), 0:128] += jnp.where(first, dk_groups[0], dk_groups[1])
        dkv_ref[pl.ds(off, SW_KEYS), 128:256] += jnp.where(first, dv_groups[0], dv_groups[1])

    s_tok = q.shape[0]
    blk = pl.BlockSpec((SW_BLK, 512), lambda n: (n, 0))
    kv_spec = pl.BlockSpec(kv.shape, lambda n: (0, 0))
    t5_spec = pl.BlockSpec((8, SW_BLK, SW_KEYS), lambda n: (0, 0, 0))
    vec = pl.BlockSpec((1, 8), lambda n: (0, 0))
    return pl.pallas_call(
        body, name=name,
        out_shape=(_sds((s_tok, 512), F32), _sds(kv.shape, F32), _sds((8, SW_BLK, SW_KEYS), F32), _sds((1, 8), F32)),
        grid=(SW_NB,), in_specs=[blk, kv_spec, blk, blk, pl.BlockSpec((SW_BLK, 8), lambda n: (n, 0)), t5_spec, vec],
        out_specs=(blk, kv_spec, t5_spec, vec), compiler_params=_params(("arbitrary",)))(q, kv, o, do, lse, t5, sink)


def gate_fwd(zg, bias, pa, ps, *, name, tm=512):
    def body(z0_ref, z1_ref, b0_ref, b1_ref, pa_ref, ps_ref, m_ref):
        g0 = jax.nn.sigmoid(z0_ref[...] + b0_ref[...])
        g1 = jax.nn.sigmoid(z1_ref[...] + b1_ref[...])
        m_ref[...] = (g0 * pa_ref[...] + g1 * ps_ref[...]).astype(BF16)

    s = zg.shape[0]
    half = lambda j: pl.BlockSpec((tm, DM), lambda i, j=j: (i, j))
    bvec = lambda j: pl.BlockSpec((1, DM), lambda i, j=j: (0, j))
    return pl.pallas_call(
        body, name=name, out_shape=_sds((s, DM), BF16), grid=(s // tm,),
        in_specs=[half(0), half(1), bvec(0), bvec(1), half(0), half(0)], out_specs=half(0),
        compiler_params=_params(("parallel",)))(zg, zg, bias, bias, pa, ps)


def gate_bwd(dm, zg, bias, pa, ps, *, name, tm=512):
    def body(dm_ref, z0_ref, z1_ref, b0_ref, b1_ref, pa_ref, ps_ref, dpa_ref, dps_ref, dz_ref, db_ref):
        @pl.when(pl.program_id(0) == 0)
        def _():
            db_ref[...] = jnp.zeros_like(db_ref)

        dm = dm_ref[...]
        g0 = jax.nn.sigmoid(z0_ref[...] + b0_ref[...])
        g1 = jax.nn.sigmoid(z1_ref[...] + b1_ref[...])
        dpa_ref[...] = (dm * g0).astype(BF16)
        dps_ref[...] = (dm * g1).astype(BF16)
        dz0 = dm * pa_ref[...] * g0 * (1.0 - g0)
        dz1 = dm * ps_ref[...] * g1 * (1.0 - g1)
        dz_ref[:, 0:DM] = dz0.astype(BF16)
        dz_ref[:, DM:2 * DM] = dz1.astype(BF16)
        db_ref[:, 0:DM] += jnp.sum(dz0, axis=0, keepdims=True)
        db_ref[:, DM:2 * DM] += jnp.sum(dz1, axis=0, keepdims=True)

    s = zg.shape[0]
    half = lambda j: pl.BlockSpec((tm, DM), lambda i, j=j: (i, j))
    bvec = lambda j: pl.BlockSpec((1, DM), lambda i, j=j: (0, j))
    return pl.pallas_call(
        body, name=name,
        out_shape=(_sds((s, DM), BF16), _sds((s, DM), BF16), _sds((s, GATE_W), BF16), _sds((1, GATE_W), F32)),
        grid=(s // tm,), in_specs=[half(0), half(0), half(1), bvec(0), bvec(1), half(0), half(0)],
        out_specs=(half(0), half(0), pl.BlockSpec((tm, GATE_W), lambda i: (i, 0)), pl.BlockSpec((1, GATE_W), lambda i: (0, 0))),
        compiler_params=_params(("arbitrary",)))(dm, zg, zg, bias, bias, pa, ps)


def loss_head(y, target, *, name, tm=512):
    def body(y_ref, t_ref, dy_ref, l_ref):
        @pl.when(pl.program_id(0) == 0)
        def _():
            l_ref[...] = jnp.zeros_like(l_ref)

        err = y_ref[...] - t_ref[...]
        dy_ref[...] = err * (1.0 / DM)
        l_ref[...] += 0.5 * jnp.sum(jnp.mean(err * err, axis=-1, keepdims=True), axis=0, keepdims=True)

    s = y.shape[0]
    tile = pl.BlockSpec((tm, DM), lambda i: (i, 0))
    return pl.pallas_call(
        body, name=name, out_shape=(_sds((s, DM), F32), _sds((1, 128), F32)), grid=(s // tm,), in_specs=[tile, tile],
        out_specs=(tile, pl.BlockSpec((1, 128), lambda i: (0, 0))), compiler_params=_params(("arbitrary",)))(y, target)


def adamw(w, g, m, v, *, name):
    def body(w_ref, g_ref, m_ref, v_ref, d_ref, nm_ref, nv_ref):
        g = g_ref[...]
        nm = ADAM_B1 * m_ref[...] + (1.0 - ADAM_B1) * g
        nv = ADAM_B2 * v_ref[...] + (1.0 - ADAM_B2) * jnp.square(g)
        m_hat = nm / (1.0 - ADAM_B1 ** ADAM_STEP)
        v_hat = nv / (1.0 - ADAM_B2 ** ADAM_STEP)
        d_ref[...] = -ADAM_LR * (m_hat / (jnp.sqrt(v_hat) + ADAM_EPS) + ADAM_WD * w_ref[...])
        nm_ref[...] = nm
        nv_ref[...] = nv

    b, k, n = w.shape
    tk = k // 4 if k % 32 == 0 else k
    spec = pl.BlockSpec((None, tk, n), lambda i, j: (i, j, 0))
    out = _sds(w.shape, F32)
    return pl.pallas_call(
        body, name=name, out_shape=(out, out, out), grid=(b, k // tk), in_specs=[spec] * 4, out_specs=(spec,) * 3,
        compiler_params=_params(("parallel", "parallel")))(w, g, m, v)


def adamw_shard(w, m, v, mine, theirs, cidx, *, name):
    _, k, n = w.shape
    nt = 2
    tk = k // 2 // nt

    def body(c_ref, w_ref, m_ref, v_ref, a0_ref, a1_ref, b0_ref, b1_ref, g_ref, d_ref, nm_ref, nv_ref):
        own = pl.program_id(1) == c_ref[0]
        g0 = jnp.where(own, a0_ref[...], b0_ref[...])
        g1 = jnp.where(own, a1_ref[...], b1_ref[...])
        g = jnp.where(pl.program_id(0) == 0, g0, g1)
        g_ref[...] = g
        nm = ADAM_B1 * m_ref[...] + (1.0 - ADAM_B1) * g
        nv = ADAM_B2 * v_ref[...] + (1.0 - ADAM_B2) * jnp.square(g)
        m_hat = nm / (1.0 - ADAM_B1 ** ADAM_STEP)
        v_hat = nv / (1.0 - ADAM_B2 ** ADAM_STEP)
        d_ref[...] = -ADAM_LR * (m_hat / (jnp.sqrt(v_hat) + ADAM_EPS) + ADAM_WD * w_ref[...])
        nm_ref[...] = nm
        nv_ref[...] = nv

    full = pl.BlockSpec((None, tk, n), lambda l, hf, t, c: (l, hf * nt + t, 0))

    def half(layer, is_mine):
        def index(l, hf, t, c):
            sel = (l == layer) & ((hf == c[0]) if is_mine else (hf != c[0]))
            return (jnp.where(sel, t, 0), 0)
        return pl.BlockSpec((tk, n), index)

    out = _sds(w.shape, F32)
    return pl.pallas_call(
        body, name=name, out_shape=(out, out, out, out),
        grid_spec=pltpu.PrefetchScalarGridSpec(
            num_scalar_prefetch=1, grid=(DEPTH, 2, nt),
            in_specs=[full, full, full, half(0, True), half(1, True), half(0, False), half(1, False)],
            out_specs=(full, full, full, full)),
        compiler_params=_params(("arbitrary", "arbitrary", "arbitrary")))(cidx, w, m, v, mine[0], mine[1], theirs[0], theirs[1])


def t5_table_grad(dt5_a, dt5_b, *, name):
    def body(a_ref, b_ref, map_ref, o_ref):
        d = a_ref[...] + b_ref[...]
        bucket = map_ref[...]
        for b in range(32):
            hit = (bucket == b)[None]
            o_ref[b] = jnp.sum(jnp.sum(jnp.where(hit, d, 0.0), axis=2), axis=1, keepdims=True)

    return pl.pallas_call(
        body, name=name, out_shape=_sds((32, 8, 1), F32), compiler_params=_params())(
            dt5_a, dt5_b, jnp.asarray(t5_bucket_map()))


def rpb_grad(dbias, *, name):
    q = np.arange(GRID_W)[:, None]
    kc = np.arange(GRID_W)[None, :]
    diag = ((kc - q + NA_WC - 1)[..., None] == np.arange(128)).reshape(GRID_W * GRID_W, 128).astype(np.float32)
    p = np.arange(NA_WR)[:, None]
    kr = np.arange(NA_WR)[None, :]
    rows = ((kr - p + NA_WR - 1).reshape(-1)[None, :] == np.arange(16)[:, None]).astype(np.float32)
    d = dbias.reshape(NA_WR, NA_HEADS, GRID_W, NA_WR, GRID_W).transpose(0, 1, 3, 2, 4).reshape(512, GRID_W * GRID_W)
    e = mm(d, jnp.asarray(diag), name=name + "_cols", exact=True, tk=1024)
    e = e.reshape(NA_WR, NA_HEADS, NA_WR, 128).transpose(0, 2, 1, 3).reshape(NA_WR * NA_WR, NA_HEADS * 128)
    out = mm(jnp.asarray(rows), e, name=name + "_rows", exact=True)
    return out.reshape(16, NA_HEADS, 128)[:2 * NA_WR - 1, :, :2 * NA_WC - 1].transpose(1, 0, 2)


BIG = ("ffn1_w_gate", "ffn1_w_up", "ffn1_w_down", "w_in", "w_branch_na", "w_branch_sw", "w_out",
       "ffn2_w_gate", "ffn2_w_up", "ffn2_w_down")
SMALL = ("ffn1_norm", "mix_norm", "b_gate", "na_q_norm", "na_k_norm", "na_rpb", "sw_q_norm", "sw_k_norm", "sw_sink",
         "ffn2_norm")


def _cols_to_full(w4):
    return w4.transpose(1, 0, 2).reshape(w4.shape[1], NSH * w4.shape[2])


def _full_to_cols(w):
    return w.reshape(w.shape[0], NSH, w.shape[1] // NSH).transpose(1, 0, 2)


def _mixer_weights(g):
    w_in = _cols_to_full(g["w_in"])
    return dict(w_att=w_in[:, :ATT_W], w_gz=w_in[:, ATT_W:], wa=_cols_to_full(g["w_branch_na"]),
                ws=_cols_to_full(g["w_branch_sw"]), wo=g["w_out"].reshape(DM, DM))


def layer_fwd(x, p, g, t5b, dep=None):
    w = _mixer_weights(g)
    row = lambda v: v.reshape(1, -1)
    y1, h1, gg1, uu1 = ffn_fwd(x, row(p["ffn1_norm"]), g["ffn1_w_gate"], g["ffn1_w_up"], g["ffn1_w_down"], name="ffn_fwd",
                               dep=dep)
    hm = rms_fwd(y1, row(p["mix_norm"]), name="mix_norm_fwd")
    z = mm(hm, w["w_att"], name="proj_att", tn=768)
    zg = mm(hm, w["w_gz"], name="proj_gate", tn=1024)
    qa, ka, va, qs, kv = qknorm_fwd(z, p["na_q_norm"], p["na_k_norm"], p["sw_q_norm"], p["sw_k_norm"], name="qknorm_fwd")
    bias = na_bias_table(p["na_rpb"], name="na_bias_table")
    o_na, lse_na = na_fwd(qa, ka, va, bias, name="na_fwd")
    kvp = jnp.pad(kv, ((SW_BLK, SW_BLK), (0, 0)))
    sink = row(p["sw_sink"])
    o_sw, lse_sw = sw_fwd(qs, kvp, t5b, sink, name="sw_fwd")
    pa = mm(o_na, w["wa"], name="branch_na")
    ps = mm(o_sw, w["ws"], name="branch_sw")
    merged = gate_fwd(zg, row(p["b_gate"]), pa, ps, name="gate_fwd")
    y2 = mm(merged, w["wo"], add=y1, name="out_proj")
    y3, h2, gg2, uu2 = ffn_fwd(y2, row(p["ffn2_norm"]), g["ffn2_w_gate"], g["ffn2_w_up"], g["ffn2_w_down"], name="ffn_fwd")
    saved = dict(x=x, y1=y1, h1=h1, gg1=gg1, uu1=uu1, hm=hm, z=z, zg=zg, qa=qa, ka=ka, va=va, qs=qs, kvp=kvp, bias=bias,
                 o_na=o_na, lse_na=lse_na, o_sw=o_sw, lse_sw=lse_sw, pa=pa, ps=ps, merged=merged, y2=y2, h2=h2, gg2=gg2,
                 uu2=uu2, w=w, sink=sink)
    return y3, saved


def layer_bwd(dy3, sv, p, g, t5b, dep=None):
    w = sv["w"]
    row = lambda v: v.reshape(1, -1)
    fold = lambda v: v.reshape(-1, HD).sum(axis=0)
    big, small = {}, {}
    dy2, small["ffn2_norm"], act, dg, du = ffn_bwd_tokens(
        dy3, sv["y2"], row(p["ffn2_norm"]), sv["gg2"], sv["uu2"], g["ffn2_w_gate"], g["ffn2_w_up"], g["ffn2_w_down"],
        name="ffn_bwd_tokens", dep=dep)
    big["ffn2_w_gate"], big["ffn2_w_up"], big["ffn2_w_down"] = ffn_bwd_weights(sv["h2"], dy3, act, dg, du, name="ffn_bwd_weights")
    dmerged = mm(dy2, w["wo"], tb=True, name="out_proj_dx")
    big["w_out"] = mm(sv["merged"], dy2, ta=True, out_dtype=BF16, tk=512, name="out_proj_dw").reshape(NSH, DM // NSH, DM)
    dpa, dps, dzg, small["b_gate"] = gate_bwd(dmerged, sv["zg"], row(p["b_gate"]), sv["pa"], sv["ps"], name="gate_bwd")
    big["w_branch_na"] = _full_to_cols(mm(sv["o_na"], dpa, ta=True, out_dtype=BF16, tk=512, name="branch_dw"))
    big["w_branch_sw"] = _full_to_cols(mm(sv["o_sw"], dps, ta=True, out_dtype=BF16, tk=512, name="branch_dw"))
    do_na = mm(dpa, w["wa"], tb=True, out_dtype=BF16, name="branch_dx")
    do_sw = mm(dps, w["ws"], tb=True, out_dtype=BF16, name="branch_dx")
    dqa, dka, dva, dbias = na_bwd(sv["qa"], sv["ka"], sv["va"], sv["o_na"], do_na, sv["lse_na"], sv["bias"], name="na_bwd")
    dqs, dkvp, dt5, dsink = sw_bwd(sv["qs"], sv["kvp"], sv["o_sw"], do_sw, sv["lse_sw"], t5b, sv["sink"], name="sw_bwd")
    dkv = dkvp[SW_BLK:SW_BLK + SEQ]
    dz, dgqa, dgka, dgqs, dgks = qknorm_bwd(sv["z"], dqa, dka, dva, dqs, dkv, p["na_q_norm"], p["na_k_norm"],
                                            p["sw_q_norm"], p["sw_k_norm"], name="qknorm_bwd")
    small["na_q_norm"], small["na_k_norm"], small["sw_q_norm"], small["sw_k_norm"] = fold(dgqa), fold(dgka), fold(dgqs), fold(dgks)
    small["na_rpb"] = rpb_grad(dbias, name="rpb_grad")
    small["sw_sink"] = dsink
    gw_att = mm(sv["hm"], dz, ta=True, out_dtype=BF16, tk=512, name="proj_att_dw")
    gw_gz = mm(sv["hm"], dzg, ta=True, out_dtype=BF16, tk=512, name="proj_gate_dw")
    big["w_in"] = _full_to_cols(jnp.concatenate([gw_att, gw_gz], axis=1))
    dh = mm(dz, w["w_att"], tb=True, name="proj_att_dx")
    dh = mm(dzg, w["w_gz"], tb=True, add=dh, name="proj_gate_dx")
    dy1, small["mix_norm"] = rms_bwd(dh, sv["y1"], row(p["mix_norm"]), dy2, name="mix_norm_bwd")
    dx, small["ffn1_norm"], act, dg, du = ffn_bwd_tokens(
        dy1, sv["x"], row(p["ffn1_norm"]), sv["gg1"], sv["uu1"], g["ffn1_w_gate"], g["ffn1_w_up"], g["ffn1_w_down"],
        name="ffn_bwd_tokens")
    big["ffn1_w_gate"], big["ffn1_w_up"], big["ffn1_w_down"] = ffn_bwd_weights(sv["h1"], dy1, act, dg, du, name="ffn_bwd_weights")
    return dx, big, small, dt5


ANY = pl.BlockSpec(memory_space=pl.ANY)


def _place():
    x, y, c = lax.axis_index("x"), lax.axis_index("y"), lax.axis_index("c")
    chips = [(1 - x, y), (x, 1 - y), (1 - x, 1 - y)]
    return x, y, c, chips


def _remote(src, dst, send_sem, recv_sem, to):
    return pltpu.make_async_remote_copy(src_ref=src, dst_ref=dst, send_sem=send_sem, recv_sem=recv_sem, device_id=to,
                                        device_id_type=MESH)


HBM = pl.BlockSpec(memory_space=pltpu.HBM)
SEM = pl.BlockSpec(memory_space=pltpu.SEMAPHORE)
ORDERED_EFFECT = pltpu.SideEffectType.DATAFLOW_SIDE_EFFECTING


def _in_hbm(v):
    return pltpu.with_memory_space_constraint(v, pltpu.HBM)


def _row_half(ref_shape_rows, c):
    half = ref_shape_rows // 2
    return pl.ds(c * half, half)


def _ici_gather_copies(w, land, send_sems, recv_sems):
    x, y, c, chips = _place()
    me = 2 * x + y
    copies = []
    for a in range(len(w)):
        rows = _row_half(w[a].shape[0], c)
        for k, chip in enumerate(chips):
            copies.append(_remote(w[a].at[rows], land[a].at[me, rows], send_sems.at[3 * a + k], recv_sems.at[3 * a + k],
                                  (*chip, c)))
    return copies


def _d2d_gather_copies(w, land, send_sems, recv_sems):
    x, y, c, chips = _place()
    me, sibling = 2 * x + y, (x, y, 1 - c)
    copies = []
    for a in range(len(w)):
        rows = _row_half(w[a].shape[0], c)
        for k, (cx, cy) in enumerate(chips):
            blk = land[a].at[2 * cx + cy, rows]
            copies.append(_remote(blk, blk, send_sems.at[4 * a + k], recv_sems.at[4 * a + k], sibling))
        copies.append(_remote(w[a], land[a].at[me], send_sems.at[4 * a + 3], recv_sems.at[4 * a + 3], sibling))
    return copies


def _d2d_gather_waits(w, land, send_sems, recv_sems):
    x, y, c, chips = _place()
    me, sibling = 2 * x + y, (x, y, 1 - c)
    waits = []
    for a in range(len(w)):
        rows = _row_half(w[a].shape[0], 1 - c)
        for k, (cx, cy) in enumerate(chips):
            blk = land[a].at[2 * cx + cy, rows]
            waits.append(_remote(blk, blk, send_sems.at[4 * a + k], recv_sems.at[4 * a + k], sibling))
        waits.append(_remote(w[a], land[a].at[me], send_sems.at[4 * a + 3], recv_sems.at[4 * a + 3], sibling))
    return waits


def gather_layer(shards, *, name):
    n = len(shards)

    def body(*refs):
        w, land = refs[:n], refs[n:2 * n]
        ici_send, ici_recv, d2d_send, d2d_recv = refs[2 * n:]
        x, y, c, chips = _place()
        ici = _ici_gather_copies(w, land, ici_send, ici_recv)
        d2d = _d2d_gather_copies(w, land, d2d_send, d2d_recv)
        for cp in ici:
            cp.start()
        for a in range(n):
            d2d[4 * a + 3].start()
        for a in range(n):
            for k in range(3):
                ici[3 * a + k].wait_recv()
                d2d[4 * a + k].start()
        for cp in _d2d_gather_waits(w, land, d2d_send, d2d_recv):
            cp.wait_recv()
        for cp in ici + d2d:
            cp.wait_send()

    return pl.pallas_call(
        body, name=name, out_shape=tuple(pltpu.HBM((NSH,) + s.shape, s.dtype) for s in shards),
        in_specs=[ANY] * n, out_specs=tuple([ANY] * n),
        scratch_shapes=[pltpu.SemaphoreType.DMA((3 * n,)), pltpu.SemaphoreType.DMA((3 * n,)), pltpu.SemaphoreType.DMA((4 * n,)),
                        pltpu.SemaphoreType.DMA((4 * n,))])(*shards)


def gather_start(shards, dep, *, name):
    n = len(shards)

    def body(*refs):
        w, land = refs[:n], refs[n:2 * n]
        send_sems, recv_sems = refs[2 * n + 1:2 * n + 3]
        token = refs[-1]
        for cp in _ici_gather_copies(w, land, send_sems, recv_sems):
            cp.start()
        token[...] = jnp.zeros_like(token)

    lands = [lax.empty((NSH,) + s.shape, s.dtype) for s in shards]
    res = pl.pallas_call(
        body, name=name,
        out_shape=(pltpu.SemaphoreType.DMA((3 * n,)), pltpu.SemaphoreType.DMA((3 * n,)))
        + tuple(pltpu.HBM(s.shape, s.dtype) for s in shards) + tuple(pltpu.HBM(l.shape, l.dtype) for l in lands)
        + (_sds((8, 128), F32),),
        in_specs=[HBM] * (2 * n) + [ANY], out_specs=(SEM, SEM) + (HBM,) * (2 * n) + (pl.BlockSpec(memory_space=pltpu.VMEM),),
        input_output_aliases={i: 2 + i for i in range(2 * n)},
        compiler_params=pltpu.CompilerParams(has_side_effects=ORDERED_EFFECT))(
            *[_in_hbm(s) for s in shards], *[_in_hbm(l) for l in lands], dep)
    return res[0], res[1], list(res[2:2 + n]), list(res[2 + n:2 + 2 * n]), res[-1]


def gather_wait(send_sems, recv_sems, shards, lands, after, *, name):
    n = len(shards)

    def body(*refs):
        w, land = refs[:n], refs[n:2 * n]
        send, recv = refs[2 * n:2 * n + 2]
        for cp in _ici_gather_copies(w, land, send, recv):
            cp.wait_send()
            cp.wait_recv()

    res = pl.pallas_call(
        body, name=name,
        out_shape=tuple(pltpu.HBM(s.shape, s.dtype) for s in shards) + tuple(pltpu.HBM(l.shape, l.dtype) for l in lands),
        in_specs=[HBM] * (2 * n) + [SEM, SEM, ANY], out_specs=(HBM,) * (2 * n),
        input_output_aliases={i: i for i in range(2 * n)},
        compiler_params=pltpu.CompilerParams(has_side_effects=ORDERED_EFFECT))(*shards, *lands, send_sems, recv_sems, after)
    return list(res[:n]), list(res[n:])


def gather_finish(shards, lands, *, name):
    n = len(shards)

    def body(*refs):
        w, land = refs[:n], refs[n:2 * n]
        send_sems, recv_sems = refs[3 * n:]
        d2d = _d2d_gather_copies(w, land, send_sems, recv_sems)
        for cp in d2d:
            cp.start()
        for cp in _d2d_gather_waits(w, land, send_sems, recv_sems):
            cp.wait_recv()
        for cp in d2d:
            cp.wait_send()

    return list(pl.pallas_call(
        body, name=name, out_shape=tuple(pltpu.HBM(l.shape, l.dtype) for l in lands),
        in_specs=[ANY] * (2 * n), out_specs=tuple([ANY] * n), input_output_aliases={n + i: i for i in range(n)},
        scratch_shapes=[pltpu.SemaphoreType.DMA((4 * n,)), pltpu.SemaphoreType.DMA((4 * n,))])(*shards, *lands))


def pair_exchange(grads, *, name):
    n = len(grads)

    def body(*refs):
        g, buf = refs[:n], refs[n:2 * n]
        send_sems, recv_sems = refs[2 * n:]
        x, y, c, _ = _place()
        copies = []
        for a in range(n):
            half = g[a].shape[1] // 2
            cp = _remote(g[a].at[:, pl.ds((1 - c) * half, half)], buf[a], send_sems.at[a], recv_sems.at[a], (x, y, 1 - c))
            cp.start()
            copies.append(cp)
        for cp in copies:
            cp.wait()

    return pl.pallas_call(
        body, name=name, out_shape=tuple(pltpu.HBM((NSH, g.shape[1] // 2, g.shape[2]), g.dtype) for g in grads),
        in_specs=[ANY] * n, out_specs=tuple([ANY] * n),
        scratch_shapes=[pltpu.SemaphoreType.DMA((n,)), pltpu.SemaphoreType.DMA((n,))])(*grads)


def _chip_exchange_copies(s, buf, send_sems, recv_sems):
    x, y, c, chips = _place()
    return [_remote(s[a].at[2 * cx + cy], buf[a].at[k], send_sems.at[3 * a + k], recv_sems.at[3 * a + k], (cx, cy, c))
            for a in range(len(s)) for k, (cx, cy) in enumerate(chips)]


def chip_exchange(sums, *, name):
    n = len(sums)

    def body(*refs):
        copies = _chip_exchange_copies(refs[:n], refs[n:2 * n], *refs[2 * n:])
        for cp in copies:
            cp.start()
        for cp in copies:
            cp.wait()

    return pl.pallas_call(
        body, name=name, out_shape=tuple(pltpu.HBM((3,) + s.shape[1:], s.dtype) for s in sums),
        in_specs=[ANY] * n, out_specs=tuple([ANY] * n),
        scratch_shapes=[pltpu.SemaphoreType.DMA((3 * n,)), pltpu.SemaphoreType.DMA((3 * n,))])(*sums)


def chip_exchange_start(sums, *, name):
    n = len(sums)

    def body(*refs):
        for cp in _chip_exchange_copies(refs[:n], refs[n:2 * n], refs[2 * n], refs[2 * n + 1]):
            cp.start()
        refs[-1][...] = jnp.zeros_like(refs[-1])

    lands = [lax.empty((3,) + s.shape[1:], s.dtype) for s in sums]
    res = pl.pallas_call(
        body, name=name,
        out_shape=(pltpu.SemaphoreType.DMA((3 * n,)), pltpu.SemaphoreType.DMA((3 * n,)))
        + tuple(pltpu.HBM(s.shape, s.dtype) for s in sums) + tuple(pltpu.HBM(l.shape, l.dtype) for l in lands)
        + (_sds((8, 128), F32),),
        in_specs=[HBM] * (2 * n), out_specs=(SEM, SEM) + (HBM,) * (2 * n) + (pl.BlockSpec(memory_space=pltpu.VMEM),),
        input_output_aliases={i: 2 + i for i in range(2 * n)},
        compiler_params=pltpu.CompilerParams(has_side_effects=ORDERED_EFFECT))(
            *[_in_hbm(s) for s in sums], *[_in_hbm(l) for l in lands])
    return res[0], res[1], list(res[2:2 + n]), list(res[2 + n:2 + 2 * n]), res[-1]


def chip_exchange_wait(send_sems, recv_sems, sums, lands, after, *, name):
    n = len(sums)

    def body(*refs):
        for cp in _chip_exchange_copies(refs[:n], refs[n:2 * n], refs[2 * n], refs[2 * n + 1]):
            cp.wait_send()
            cp.wait_recv()

    res = pl.pallas_call(
        body, name=name,
        out_shape=tuple(pltpu.HBM(s.shape, s.dtype) for s in sums) + tuple(pltpu.HBM(l.shape, l.dtype) for l in lands),
        in_specs=[HBM] * (2 * n) + [SEM, SEM, ANY], out_specs=(HBM,) * (2 * n),
        input_output_aliases={i: i for i in range(2 * n)},
        compiler_params=pltpu.CompilerParams(has_side_effects=ORDERED_EFFECT))(*sums, *lands, send_sems, recv_sems, after)
    return list(res[:n]), list(res[n:])


def pair_send(halves, *, name):
    n = len(halves[0])

    def body(*refs):
        h, got = refs[:DEPTH * n], refs[DEPTH * n:2 * DEPTH * n]
        send_sems, recv_sems = refs[2 * DEPTH * n:]
        x, y, c, _ = _place()
        copies = []
        for i in range(DEPTH * n):
            cp = _remote(h[i], got[i], send_sems.at[i], recv_sems.at[i], (x, y, 1 - c))
            cp.start()
            copies.append(cp)
        for cp in copies:
            cp.wait()

    flat = [v for l in range(DEPTH) for v in halves[l]]
    got = pl.pallas_call(
        body, name=name, out_shape=tuple(pltpu.HBM(v.shape, v.dtype) for v in flat),
        in_specs=[ANY] * (DEPTH * n), out_specs=tuple([ANY] * (DEPTH * n)),
        scratch_shapes=[pltpu.SemaphoreType.DMA((DEPTH * n,)), pltpu.SemaphoreType.DMA((DEPTH * n,))])(*flat)
    return [list(got[l * n:(l + 1) * n]) for l in range(DEPTH)]


def allreduce_small(v, *, name):
    rows = v.shape[0]

    def body(v_ref, o_ref, gath, send_sems, recv_sems):
        x, y, c, _ = _place()
        me = 4 * x + 2 * y + c
        gath[me] = v_ref[...]
        copies = []
        for k in range(1, 8):
            fx, fy, fc = (k >> 2) & 1, (k >> 1) & 1, k & 1
            peer = (jnp.where(fx, 1 - x, x), jnp.where(fy, 1 - y, y), jnp.where(fc, 1 - c, c))
            cp = _remote(v_ref, gath.at[me], send_sems.at[k - 1], recv_sems.at[k - 1], peer)
            cp.start()
            copies.append(cp)
        for cp in copies:
            cp.wait()
        acc = gath[0]
        for d in range(1, 8):
            acc = acc + gath[d]
        o_ref[...] = acc

    return pl.pallas_call(
        body, name=name, out_shape=_sds(v.shape, F32),
        in_specs=[pl.BlockSpec(memory_space=pltpu.VMEM)], out_specs=pl.BlockSpec(memory_space=pltpu.VMEM),
        scratch_shapes=[pltpu.VMEM((8, rows, 128), F32), pltpu.SemaphoreType.DMA((7,)), pltpu.SemaphoreType.DMA((7,))])(v)


def add_halves(g, buf, cidx, *, name):
    _, k, n = g.shape

    def body(c_ref, g_ref, b_ref, o_ref):
        o_ref[...] = (g_ref[...].astype(F32) + b_ref[...].astype(F32)).astype(BF16)

    blk = pl.BlockSpec((None, k // 2, n), lambda s, c: (s, 0, 0))
    return pl.pallas_call(
        body, name=name, out_shape=_sds(buf.shape, BF16),
        grid_spec=pltpu.PrefetchScalarGridSpec(
            num_scalar_prefetch=1, grid=(NSH,),
            in_specs=[pl.BlockSpec((None, k // 2, n), lambda s, c: (s, c[0], 0)), blk], out_specs=blk),
        compiler_params=_params(("parallel",)))(cidx, g, buf)


def add_chips(sums, buf, sidx, *, name):
    _, kh, n = sums.shape

    def body(s_ref, mine_ref, b_ref, o_ref):
        o_ref[...] = (mine_ref[...].astype(F32) + b_ref[0].astype(F32)) + (b_ref[1].astype(F32) + b_ref[2].astype(F32))

    return pl.pallas_call(
        body, name=name, out_shape=_sds((kh, n), F32),
        grid_spec=pltpu.PrefetchScalarGridSpec(
            num_scalar_prefetch=1, grid=(1,),
            in_specs=[pl.BlockSpec((None, kh, n), lambda i, s: (s[0], 0, 0)), pl.BlockSpec((3, kh, n), lambda i, s: (0, 0, 0))],
            out_specs=pl.BlockSpec((kh, n), lambda i, s: (0, 0))),
        compiler_params=_params(("arbitrary",)))(sidx, sums, buf)


PARAMS = ("ffn1_norm", "ffn1_w_gate", "ffn1_w_up", "ffn1_w_down", "mix_norm", "w_in", "b_gate", "na_q_norm", "na_k_norm",
          "na_rpb", "sw_q_norm", "sw_k_norm", "sw_sink", "t5_rel_table", "w_branch_na", "w_branch_sw", "w_out", "ffn2_norm",
          "ffn2_w_gate", "ffn2_w_up", "ffn2_w_down")
SMALL_ALL = tuple(n for n in PARAMS if n not in BIG)
SMALL_ROWS = 152


def _pack_small(vals):
    flat = jnp.concatenate([vals[n].reshape(-1).astype(F32) for n in SMALL_ALL] + [vals["loss"].reshape(-1)])
    return jnp.pad(flat, (0, SMALL_ROWS * 128 - flat.shape[0])).reshape(SMALL_ROWS, 128)


def _unpack_small(packed, like):
    flat, out, off = packed.reshape(-1), {}, 0
    for n in SMALL_ALL:
        size = math.prod(like[n].shape)
        out[n] = flat[off:off + size].reshape(like[n].shape)
        off += size
    out["loss"] = flat[off]
    return out


def kernel(x, ffn1_norm, ffn1_w_gate, ffn1_w_up, ffn1_w_down, mix_norm, w_in, b_gate, na_q_norm, na_k_norm, na_rpb, sw_q_norm, sw_k_norm, sw_sink, t5_rel_table, w_branch_na, w_branch_sw, w_out, ffn2_norm, ffn2_w_gate, ffn2_w_up, ffn2_w_down, loss_target, m_ffn1_norm, m_ffn1_w_gate, m_ffn1_w_up, m_ffn1_w_down, m_mix_norm, m_w_in, m_b_gate, m_na_q_norm, m_na_k_norm, m_na_rpb, m_sw_q_norm, m_sw_k_norm, m_sw_sink, m_t5_rel_table, m_w_branch_na, m_w_branch_sw, m_w_out, m_ffn2_norm, m_ffn2_w_gate, m_ffn2_w_up, m_ffn2_w_down, v_ffn1_norm, v_ffn1_w_gate, v_ffn1_w_up, v_ffn1_w_down, v_mix_norm, v_w_in, v_b_gate, v_na_q_norm, v_na_k_norm, v_na_rpb, v_sw_q_norm, v_sw_k_norm, v_sw_sink, v_t5_rel_table, v_w_branch_na, v_w_branch_sw, v_w_out, v_ffn2_norm, v_ffn2_w_gate, v_ffn2_w_up, v_ffn2_w_down):
    args = locals()
    w = {n: args[n] for n in PARAMS}
    m = {n: args["m_" + n] for n in PARAMS}
    v = {n: args["v_" + n] for n in PARAMS}
    cidx = lax.axis_index("c").astype(jnp.int32).reshape(1)
    sidx = (2 * lax.axis_index("x") + lax.axis_index("y")).astype(jnp.int32).reshape(1)

    small = [{n: w[n][l] for n in SMALL} for l in range(DEPTH)]
    shards = [[w[n][l].astype(BF16) for n in BIG] for l in range(DEPTH)]
    t5b = t5_bias(w["t5_rel_table"], name="t5_bias")

    g0 = gather_layer(shards[0], name="gather_layer")
    send_sems, recv_sems, thru, lands, token = gather_start(shards[1], g0[0], name="gather_start")
    h0, saved0 = layer_fwd(x[0], small[0], dict(zip(BIG, g0)), t5b, dep=token)
    thru, lands = gather_wait(send_sems, recv_sems, thru, lands, h0, name="gather_wait")
    g1 = gather_finish(thru, lands, name="gather_finish")
    gathered = [dict(zip(BIG, g0)), dict(zip(BIG, g1))]
    h1, saved1 = layer_fwd(h0, small[1], gathered[1], t5b)
    dy, loss_row = loss_head(h1, loss_target[0], name="loss_head")

    dy, big1, small1, dt5_1 = layer_bwd(dy, saved1, small[1], gathered[1], t5b)
    grads1 = [big1[n] for n in BIG]
    sums1 = [add_halves(g, b, cidx, name="add_halves") for g, b in zip(grads1, pair_exchange(grads1, name="pair_exchange"))]
    send_sems, recv_sems, sums1, lands, token = chip_exchange_start(sums1, name="chip_exchange_start")
    grad_x, big0, small0, dt5_0 = layer_bwd(dy, saved0, small[0], gathered[0], t5b, dep=token)
    grads0 = [big0[n] for n in BIG]
    sums1, from_chips1 = chip_exchange_wait(send_sems, recv_sems, sums1, lands, grads0[0], name="chip_exchange_wait")
    sums0 = [add_halves(g, b, cidx, name="add_halves") for g, b in zip(grads0, pair_exchange(grads0, name="pair_exchange"))]
    from_chips0 = chip_exchange(sums0, name="chip_exchange")
    halves = [[add_chips(s, b, sidx, name="add_chips") for s, b in zip(sums0, from_chips0)],
              [add_chips(s, b, sidx, name="add_chips") for s, b in zip(sums1, from_chips1)]]
    theirs = pair_send(halves, name="pair_send")
    smalls = [small0, small1]
    dt5 = t5_table_grad(dt5_0, dt5_1, name="t5_table_grad").reshape(32, 8)

    local_small = {n: jnp.stack([smalls[l][n].reshape(w[n].shape[1:]) for l in range(DEPTH)]) for n in SMALL}
    local_small["t5_rel_table"] = dt5
    local_small["loss"] = loss_row[0, 0:1]
    total = allreduce_small(_pack_small(local_small), name="allreduce_small")
    small_grads = _unpack_small(total, w)

    grad, delta, new_m, new_v = {}, {}, {}, {}
    for a, n in enumerate(BIG):
        grad[n], delta[n], new_m[n], new_v[n] = adamw_shard(
            w[n], m[n], v[n], [halves[l][a] for l in range(DEPTH)], [theirs[l][a] for l in range(DEPTH)], cidx, name="adamw_shard")
    pack = lambda d: _pack_small({**d, "loss": jnp.zeros((1,), F32)})[None]
    ds, ms, vs = adamw(pack(w), total[None], pack(m), pack(v), name="adamw_small")
    for n in SMALL_ALL:
        grad[n] = small_grads[n]
    d_s, m_s, v_s = _unpack_small(ds[0], w), _unpack_small(ms[0], w), _unpack_small(vs[0], w)
    for n in SMALL_ALL:
        delta[n], new_m[n], new_v[n] = d_s[n], m_s[n], v_s[n]

    return (small_grads["loss"], grad_x[None], *[grad[n] for n in PARAMS], *[delta[n] for n in PARAMS],
            *[new_m[n] for n in PARAMS], *[new_v[n] for n in PARAMS])
```

```python
import functools
import math

import jax
import jax.numpy as jnp
import numpy as np
from jax import lax
from jax.experimental import pallas as pl
from jax.experimental.pallas import tpu as pltpu

F32 = jnp.float32
BF16 = jnp.bfloat16

SEQ = 2048
DM = 1024
DFF = 2816
DEPTH = 2
NSH = 4
FSH = DFF // NSH
GRID_W = 64
ROWS = SEQ // GRID_W
NA_HEADS = 8
HD = 64
NA_WR = 8
NA_WC = 16
NA_KEYS = NA_WR * GRID_W
SW_BLK = 128
SW_NB = SEQ // SW_BLK
SW_KEYS = 3 * SW_BLK
ATT_W = 2304
GATE_W = 2048
IN_W = ATT_W + GATE_W
EPS = 1e-6
NEG = -1e30
QK_SCALE = 1.0 / math.sqrt(HD)

ADAM_LR = 0.001
ADAM_B1 = 0.9
ADAM_B2 = 0.999
ADAM_EPS = 1e-08
ADAM_WD = 0.01
ADAM_STEP = 10

VMEM_LIMIT = 56 << 20
MESH = pl.DeviceIdType.MESH

NT = (((1,), (1,)), ((), ()))
TN = (((0,), (0,)), ((), ()))
NN = (((1,), (0,)), ((), ()))


def _dot(a, b, dims=NN):
    return lax.dot_general(a, b, dims, preferred_element_type=F32)


def _params(sem=None):
    return pltpu.CompilerParams(dimension_semantics=sem, vmem_limit_bytes=VMEM_LIMIT)


def _sds(shape, dtype):
    return jax.ShapeDtypeStruct(shape, dtype)


def mm(a, b, *, name, ta=False, tb=False, out_dtype=F32, add=None, scale=None, tm=512, tn=None, tk=None, exact=False):
    m, kd = (a.shape[1], a.shape[0]) if ta else a.shape
    n = b.shape[0] if tb else b.shape[1]
    tm, tn, tk = min(tm, m), min(tn or n, n), min(tk or kd, kd)
    nk = kd // tk
    dims = (((0 if ta else 1,), (1 if tb else 0,)), ((), ()))

    def body(*refs):
        if add is None:
            a_ref, b_ref, o_ref, acc = refs
        else:
            a_ref, b_ref, add_ref, o_ref, acc = refs
        k = pl.program_id(2)

        @pl.when(k == 0)
        def _():
            acc[...] = jnp.zeros_like(acc)

        if exact:
            acc[...] += lax.dot_general(a_ref[...], b_ref[...], dims, precision=lax.Precision.HIGHEST,
                                        preferred_element_type=F32)
        else:
            acc[...] += lax.dot_general(a_ref[...].astype(BF16), b_ref[...].astype(BF16), dims,
                                        preferred_element_type=F32)

        @pl.when(k == nk - 1)
        def _():
            r = acc[...]
            if scale is not None:
                r = r * scale
            if add is not None:
                r = r + add_ref[...]
            o_ref[...] = r.astype(out_dtype)

    a_spec = pl.BlockSpec((tk, tm), lambda i, j, k: (k, i)) if ta else pl.BlockSpec((tm, tk), lambda i, j, k: (i, k))
    b_spec = pl.BlockSpec((tn, tk), lambda i, j, k: (j, k)) if tb else pl.BlockSpec((tk, tn), lambda i, j, k: (k, j))
    o_spec = pl.BlockSpec((tm, tn), lambda i, j, k: (i, j))
    ins, specs = [a, b], [a_spec, b_spec]
    if add is not None:
        ins.append(add)
        specs.append(o_spec)
    return pl.pallas_call(
        body, name=name, out_shape=_sds((m, n), out_dtype), grid=(m // tm, n // tn, nk), in_specs=specs,
        out_specs=o_spec, scratch_shapes=[pltpu.VMEM((tm, tn), F32)],
        compiler_params=_params(("parallel", "parallel", "arbitrary")))(*ins)


def _rms(x):
    return lax.rsqrt(jnp.mean(x * x, axis=-1, keepdims=True) + EPS)


def rms_fwd(x, gain, *, name, tm=512):
    def body(x_ref, g_ref, h_ref):
        x = x_ref[...]
        h_ref[...] = (x * _rms(x) * g_ref[...]).astype(BF16)

    return pl.pallas_call(
        body, name=name, out_shape=_sds(x.shape, BF16), grid=(x.shape[0] // tm,),
        in_specs=[pl.BlockSpec((tm, DM), lambda i: (i, 0)), pl.BlockSpec((1, DM), lambda i: (0, 0))],
        out_specs=pl.BlockSpec((tm, DM), lambda i: (i, 0)), compiler_params=_params(("parallel",)))(x, gain)


def _rms_bwd_math(dh, x, gain):
    r = _rms(x)
    xh = x * r
    dgain = jnp.sum(dh * xh, axis=0, keepdims=True)
    dxn = dh * gain
    dx = r * (dxn - xh * jnp.mean(dxn * xh, axis=-1, keepdims=True))
    return dx, dgain


def rms_bwd(dh, x, gain, dres, *, name, tm=512):
    def body(dh_ref, x_ref, g_ref, dres_ref, dx_ref, dg_ref):
        @pl.when(pl.program_id(0) == 0)
        def _():
            dg_ref[...] = jnp.zeros_like(dg_ref)

        dx, dg = _rms_bwd_math(dh_ref[...], x_ref[...], g_ref[...])
        dx_ref[...] = dres_ref[...] + dx
        dg_ref[...] += dg

    tile = pl.BlockSpec((tm, DM), lambda i: (i, 0))
    vec = pl.BlockSpec((1, DM), lambda i: (0, 0))
    return pl.pallas_call(
        body, name=name, out_shape=(_sds(x.shape, F32), _sds((1, DM), F32)), grid=(x.shape[0] // tm,),
        in_specs=[tile, tile, vec, tile], out_specs=(tile, vec), compiler_params=_params(("arbitrary",)))(dh, x, gain, dres)


def _with_dep(ins, specs, dep):
    if dep is None:
        return ins, specs
    return ins + [dep], specs + [pl.BlockSpec(memory_space=pl.ANY)]


def ffn_fwd(x, gain, wg, wu, wd, *, name, tm=512, dep=None):
    def body(x_ref, g_ref, wg_ref, wu_ref, wd_ref, *rest):
        y_ref, h_ref, gg_ref, uu_ref, acc = rest[-5:]
        f = pl.program_id(1)

        @pl.when(f == 0)
        def _():
            x = x_ref[...]
            h_ref[...] = (x * _rms(x) * g_ref[...]).astype(BF16)
            acc[...] = jnp.zeros_like(acc)

        h = h_ref[...]
        gg = _dot(h, wg_ref[...], NT)
        uu = _dot(h, wu_ref[...], NT)
        gg_ref[...] = gg.astype(BF16)
        uu_ref[...] = uu.astype(BF16)
        act = (gg * jax.nn.sigmoid(gg) * uu).astype(BF16)
        acc[...] += _dot(act, wd_ref[...])

        @pl.when(f == NSH - 1)
        def _():
            y_ref[...] = x_ref[...] + 0.5 * acc[...]

    s = x.shape[0]
    tile = pl.BlockSpec((tm, DM), lambda i, f: (i, 0))
    wsh = pl.BlockSpec((None, FSH, DM), lambda i, f: (f, 0, 0))
    hid = pl.BlockSpec((None, tm, FSH), lambda i, f: (f, i, 0))
    ins, specs = _with_dep([x, gain, wg, wu, wd], [tile, pl.BlockSpec((1, DM), lambda i, f: (0, 0)), wsh, wsh, wsh], dep)
    return pl.pallas_call(
        body, name=name,
        out_shape=(_sds((s, DM), F32), _sds((s, DM), BF16), _sds((NSH, s, FSH), BF16), _sds((NSH, s, FSH), BF16)),
        grid=(s // tm, NSH), in_specs=specs,
        out_specs=(tile, tile, hid, hid), scratch_shapes=[pltpu.VMEM((tm, DM), F32)],
        compiler_params=_params(("parallel", "arbitrary")))(*ins)


def ffn_bwd_tokens(dy, x, gain, gg, uu, wg, wu, wd, *, name, tm=512, dep=None):
    def body(dy_ref, x_ref, g_ref, gg_ref, uu_ref, wg_ref, wu_ref, wd_ref, *rest):
        dx_ref, dgain_ref, act_ref, dg_ref, du_ref, dh = rest[-6:]
        i, f = pl.program_id(0), pl.program_id(1)

        @pl.when(f == 0)
        def _():
            dh[...] = jnp.zeros_like(dh)

        @pl.when((i == 0) & (f == 0))
        def _():
            dgain_ref[...] = jnp.zeros_like(dgain_ref)

        dact = _dot((0.5 * dy_ref[...]).astype(BF16), wd_ref[...], NT)
        g = gg_ref[...].astype(F32)
        u = uu_ref[...].astype(F32)
        sg = jax.nn.sigmoid(g)
        silu = g * sg
        act_ref[...] = (silu * u).astype(BF16)
        dg = (dact * u * (sg * (1.0 + g * (1.0 - sg)))).astype(BF16)
        du = (dact * silu).astype(BF16)
        dg_ref[...] = dg
        du_ref[...] = du
        dh[...] += _dot(dg, wg_ref[...]) + _dot(du, wu_ref[...])

        @pl.when(f == NSH - 1)
        def _():
            dx, dgain = _rms_bwd_math(dh[...], x_ref[...], g_ref[...])
            dx_ref[...] = dy_ref[...] + dx
            dgain_ref[...] += dgain

    s = x.shape[0]
    tile = pl.BlockSpec((tm, DM), lambda i, f: (i, 0))
    vec = pl.BlockSpec((1, DM), lambda i, f: (0, 0))
    wsh = pl.BlockSpec((None, FSH, DM), lambda i, f: (f, 0, 0))
    hid = pl.BlockSpec((None, tm, FSH), lambda i, f: (f, i, 0))
    hshape = _sds((NSH, s, FSH), BF16)
    ins, specs = _with_dep([dy, x, gain, gg, uu, wg, wu, wd], [tile, tile, vec, hid, hid, wsh, wsh, wsh], dep)
    return pl.pallas_call(
        body, name=name, out_shape=(_sds((s, DM), F32), _sds((1, DM), F32), hshape, hshape, hshape),
        grid=(s // tm, NSH), in_specs=specs,
        out_specs=(tile, vec, hid, hid, hid), scratch_shapes=[pltpu.VMEM((tm, DM), F32)],
        compiler_params=_params(("arbitrary", "arbitrary")))(*ins)


def ffn_bwd_weights(h, dy, act, dg, du, *, name, tk=512):
    s = h.shape[0]
    nk = s // tk

    def body(h_ref, dy_ref, act_ref, dg_ref, du_ref, gwg_ref, gwu_ref, gwd_ref, ag, au, ad):
        k = pl.program_id(1)

        @pl.when(k == 0)
        def _():
            ag[...] = jnp.zeros_like(ag)
            au[...] = jnp.zeros_like(au)
            ad[...] = jnp.zeros_like(ad)

        h = h_ref[...]
        ag[...] += _dot(dg_ref[...], h, TN)
        au[...] += _dot(du_ref[...], h, TN)
        ad[...] += _dot(act_ref[...], dy_ref[...].astype(BF16), TN)

        @pl.when(k == nk - 1)
        def _():
            gwg_ref[...] = ag[...].astype(BF16)
            gwu_ref[...] = au[...].astype(BF16)
            gwd_ref[...] = (0.5 * ad[...]).astype(BF16)

    tile = pl.BlockSpec((tk, DM), lambda f, k: (k, 0))
    hid = pl.BlockSpec((None, tk, FSH), lambda f, k: (f, k, 0))
    wsh = pl.BlockSpec((None, FSH, DM), lambda f, k: (f, 0, 0))
    wshape = _sds((NSH, FSH, DM), BF16)
    return pl.pallas_call(
        body, name=name, out_shape=(wshape, wshape, wshape),
        grid=(NSH, nk), in_specs=[tile, tile, hid, hid, hid], out_specs=(wsh, wsh, wsh),
        scratch_shapes=[pltpu.VMEM((FSH, DM), F32), pltpu.VMEM((FSH, DM), F32), pltpu.VMEM((FSH, DM), F32)],
        compiler_params=_params(("parallel", "arbitrary")))(h, dy, act, dg, du)


def _group_mean(v, bd):
    return lax.dot_general(v, bd, NN, precision=lax.Precision.HIGHEST, preferred_element_type=F32)


def _block_diag(width):
    idx = np.arange(width) // HD
    return jnp.asarray((idx[:, None] == idx[None, :]).astype(np.float32) / HD)


def qknorm_fwd(z, gq_na, gk_na, gq_sw, gk_sw, *, name, tm=256):
    def body(zq_ref, zk_ref, zv_ref, zs_ref, zkv_ref, gqa_ref, gka_ref, gqs_ref, gks_ref, bd_ref, bd2_ref,
             qa_ref, ka_ref, va_ref, qs_ref, kv_ref):
        bd = bd_ref[...]

        def norm(x, g, bdm):
            return x * lax.rsqrt(_group_mean(x * x, bdm) + EPS) * g

        qa_ref[...] = (norm(zq_ref[...], gqa_ref[...], bd) * QK_SCALE).astype(BF16)
        ka_ref[...] = norm(zk_ref[...], gka_ref[...], bd).astype(BF16)
        va_ref[...] = zv_ref[...].astype(BF16)
        qs_ref[...] = (norm(zs_ref[...], gqs_ref[...], bd) * QK_SCALE).astype(BF16)
        kv = zkv_ref[...]
        kv_ref[:, 0:128] = norm(kv[:, 0:128], gks_ref[...], bd2_ref[...]).astype(BF16)
        kv_ref[:, 128:256] = kv[:, 128:256].astype(BF16)

    s = z.shape[0]
    col = lambda j: pl.BlockSpec((tm, 512), lambda i, j=j: (i, j))
    vec = lambda w: pl.BlockSpec((1, w), lambda i: (0, 0))
    o512 = pl.BlockSpec((tm, 512), lambda i: (i, 0))
    g512 = lambda g: jnp.tile(g.reshape(1, HD), (1, 8))
    return pl.pallas_call(
        body, name=name,
        out_shape=(_sds((s, 512), BF16),) * 4 + (_sds((s, 256), BF16),), grid=(s // tm,),
        in_specs=[col(0), col(1), col(2), col(3), pl.BlockSpec((tm, 256), lambda i: (i, 8)), vec(512), vec(512), vec(512),
                  vec(128), pl.BlockSpec((512, 512), lambda i: (0, 0)), pl.BlockSpec((128, 128), lambda i: (0, 0))],
        out_specs=(o512, o512, o512, o512, pl.BlockSpec((tm, 256), lambda i: (i, 0))),
        compiler_params=_params(("parallel",)))(
            z, z, z, z, z, g512(gq_na), g512(gk_na), g512(gq_sw), jnp.tile(gk_sw.reshape(1, HD), (1, 2)),
            _block_diag(512), _block_diag(128))


def qknorm_bwd(z, dqa, dka, dva, dqs, dkv, gq_na, gk_na, gq_sw, gk_sw, *, name, tm=256):
    def body(zq_ref, zk_ref, zs_ref, zkv_ref, dqa_ref, dka_ref, dva_ref, dqs_ref, dkv_ref, gqa_ref, gka_ref, gqs_ref,
             gks_ref, bd_ref, bd2_ref, dz_ref, dgqa_ref, dgka_ref, dgqs_ref, dgks_ref):
        @pl.when(pl.program_id(0) == 0)
        def _():
            dgqa_ref[...] = jnp.zeros_like(dgqa_ref)
            dgka_ref[...] = jnp.zeros_like(dgka_ref)
            dgqs_ref[...] = jnp.zeros_like(dgqs_ref)
            dgks_ref[...] = jnp.zeros_like(dgks_ref)

        bd = bd_ref[...]

        def bwd(x, dy, g, bdm, dg_ref):
            r = lax.rsqrt(_group_mean(x * x, bdm) + EPS)
            xh = x * r
            dg_ref[...] += jnp.sum(dy * xh, axis=0, keepdims=True)
            dxn = dy * g
            return r * (dxn - xh * _group_mean(dxn * xh, bdm))

        dz_ref[:, 0:512] = bwd(zq_ref[...], dqa_ref[...] * QK_SCALE, gqa_ref[...], bd, dgqa_ref).astype(BF16)
        dz_ref[:, 512:1024] = bwd(zk_ref[...], dka_ref[...], gka_ref[...], bd, dgka_ref).astype(BF16)
        dz_ref[:, 1024:1536] = dva_ref[...].astype(BF16)
        dz_ref[:, 1536:2048] = bwd(zs_ref[...], dqs_ref[...] * QK_SCALE, gqs_ref[...], bd, dgqs_ref).astype(BF16)
        dkv = dkv_ref[...]
        dz_ref[:, 2048:2176] = bwd(zkv_ref[:, 0:128], dkv[:, 0:128], gks_ref[...], bd2_ref[...], dgks_ref).astype(BF16)
        dz_ref[:, 2176:2304] = dkv[:, 128:256].astype(BF16)

    s = z.shape[0]
    col = lambda j: pl.BlockSpec((tm, 512), lambda i, j=j: (i, j))
    t512 = pl.BlockSpec((tm, 512), lambda i: (i, 0))
    t256 = pl.BlockSpec((tm, 256), lambda i: (i, 0))
    vec = lambda w: pl.BlockSpec((1, w), lambda i: (0, 0))
    g512 = lambda g: jnp.tile(g.reshape(1, HD), (1, 8))
    return pl.pallas_call(
        body, name=name,
        out_shape=(_sds((s, ATT_W), BF16), _sds((1, 512), F32), _sds((1, 512), F32), _sds((1, 512), F32), _sds((1, 128), F32)),
        grid=(s // tm,),
        in_specs=[col(0), col(1), col(3), pl.BlockSpec((tm, 256), lambda i: (i, 8)), t512, t512, t512, t512, t256,
                  vec(512), vec(512), vec(512), vec(128), pl.BlockSpec((512, 512), lambda i: (0, 0)),
                  pl.BlockSpec((128, 128), lambda i: (0, 0))],
        out_specs=(pl.BlockSpec((tm, ATT_W), lambda i: (i, 0)), vec(512), vec(512), vec(512), vec(128)),
        compiler_params=_params(("arbitrary",)))(
            z, z, z, z, dqa, dka, dva, dqs, dkv, g512(gq_na), g512(gk_na), g512(gq_sw),
            jnp.tile(gk_sw.reshape(1, HD), (1, 2)), _block_diag(512), _block_diag(128))


def _na_row_start(r):
    return jnp.clip(r - NA_WR // 2, 0, ROWS - NA_WR)


def na_bias_table(rpb, *, name):
    t = jnp.pad(rpb, ((0, 0), (0, 2), (0, HD - (2 * NA_WC - 1))))
    pairs = jnp.concatenate([t[:, :16], t[:, 1:17]], axis=-1).reshape(NA_HEADS, 16, 1, 128)

    def body(t_ref, o_ref):
        p = pl.program_id(0)
        q = lax.broadcasted_iota(jnp.int32, (GRID_W, 128), 0)
        kc = lax.broadcasted_iota(jnp.int32, (GRID_W, 128), 1) & (GRID_W - 1)
        cs = jnp.clip(q - NA_WC // 2, 0, GRID_W - NA_WC)
        ok = (kc >= cs) & (kc < cs + NA_WC)
        for h in range(NA_HEADS):
            for pr in range(NA_WR // 2):
                x = jnp.broadcast_to(t_ref[h, 2 * pr - p + NA_WR - 1], (GRID_W, 128))
                b = pltpu.roll(x, 128 - (NA_WC - 1), 1, stride=1, stride_axis=0)
                o_ref[h, :, 128 * pr:128 * pr + 128] = jnp.where(ok, b, NEG)

    return pl.pallas_call(
        body, name=name, out_shape=_sds((NA_WR, NA_HEADS, GRID_W, NA_KEYS), F32), grid=(NA_WR,),
        in_specs=[pl.BlockSpec((NA_HEADS, 16, 1, 128), lambda p: (0, 0, 0, 0))],
        out_specs=pl.BlockSpec((None, NA_HEADS, GRID_W, NA_KEYS), lambda p: (p, 0, 0, 0)),
        compiler_params=_params(("parallel",)))(pairs)


def _lane_halves():
    lane = lax.broadcasted_iota(jnp.int32, (1, 128), 1)
    return lane < HD


def na_fwd(q, k, v, bias, *, name):
    def body(q_ref, k_ref, v_ref, b_ref, o_ref, lse_ref):
        r = pl.program_id(0)
        off = pl.multiple_of(_na_row_start(r) * GRID_W, GRID_W)
        first = _lane_halves()
        for j in range(NA_HEADS // 2):
            lanes = slice(128 * j, 128 * j + 128)
            q2 = q_ref[:, lanes]
            k2 = k_ref[pl.ds(off, NA_KEYS), lanes]
            v2 = v_ref[pl.ds(off, NA_KEYS), lanes]
            zero = jnp.zeros_like(q2)
            o2 = jnp.zeros((GRID_W, 128), F32)
            for half in range(2):
                h = 2 * j + half
                sel = first if half == 0 else jnp.logical_not(first)
                s = _dot(jnp.where(sel, q2, zero), k2, NT)
                b = b_ref[h]
                s = jnp.where(b > 0.5 * NEG, s + b, NEG)
                m = jnp.max(s, axis=-1, keepdims=True)
                e = jnp.exp(s - m)
                l = jnp.sum(e, axis=-1, keepdims=True)
                p = (e / l).astype(BF16)
                o2 = o2 + _dot(p, jnp.where(sel, v2, jnp.zeros_like(v2)))
                lse_ref[:, h:h + 1] = m + jnp.log(l)
            o_ref[:, lanes] = o2.astype(BF16)

    s_tok = q.shape[0]
    full = pl.BlockSpec((s_tok, 512), lambda r: (0, 0))
    return pl.pallas_call(
        body, name=name, out_shape=(_sds((s_tok, 512), BF16), _sds((s_tok, NA_HEADS), F32)), grid=(ROWS,),
        in_specs=[pl.BlockSpec((GRID_W, 512), lambda r: (r, 0)), full, full,
                  pl.BlockSpec((None, NA_HEADS, GRID_W, NA_KEYS), lambda r: (r - _na_row_start(r), 0, 0, 0))],
        out_specs=(pl.BlockSpec((GRID_W, 512), lambda r: (r, 0)), pl.BlockSpec((GRID_W, NA_HEADS), lambda r: (r, 0))),
        compiler_params=_params(("parallel",)))(q, k, v, bias)


def na_bwd(q, k, v, o, do, lse, bias, *, name):
    def body(q_ref, k_ref, v_ref, o_ref, do_ref, lse_ref, b_ref, dq_ref, dk_ref, dv_ref, db_ref):
        r = pl.program_id(0)

        @pl.when(r == 0)
        def _():
            dk_ref[...] = jnp.zeros_like(dk_ref)
            dv_ref[...] = jnp.zeros_like(dv_ref)

        @pl.when((r <= NA_WR // 2) | (r > ROWS - NA_WR // 2))
        def _():
            db_ref[...] = jnp.zeros_like(db_ref)

        off = pl.multiple_of(_na_row_start(r) * GRID_W, GRID_W)
        first = _lane_halves()
        for j in range(NA_HEADS // 2):
            lanes = slice(128 * j, 128 * j + 128)
            q2 = q_ref[:, lanes]
            k2 = k_ref[pl.ds(off, NA_KEYS), lanes]
            v2 = v_ref[pl.ds(off, NA_KEYS), lanes]
            do2 = do_ref[:, lanes]
            prod = do2.astype(F32) * o_ref[:, lanes].astype(F32)
            dq2 = jnp.zeros((GRID_W, 128), F32)
            dk2 = jnp.zeros((NA_KEYS, 128), F32)
            dv2 = jnp.zeros((NA_KEYS, 128), F32)
            for half in range(2):
                h = 2 * j + half
                sel = first if half == 0 else jnp.logical_not(first)
                qh = jnp.where(sel, q2, jnp.zeros_like(q2))
                doh = jnp.where(sel, do2, jnp.zeros_like(do2))
                s = _dot(qh, k2, NT)
                b = b_ref[h]
                s = jnp.where(b > 0.5 * NEG, s + b, NEG)
                p = jnp.exp(s - lse_ref[:, h:h + 1])
                dp = _dot(doh, v2, NT)
                delta = jnp.sum(jnp.where(sel, prod, 0.0), axis=-1, keepdims=True)
                ds = p * (dp - delta)
                db_ref[h] += ds
                dsb = ds.astype(BF16)
                dq2 = dq2 + _dot(dsb, jnp.where(sel, k2, jnp.zeros_like(k2)))
                dk2 = dk2 + _dot(dsb, qh, TN)
                dv2 = dv2 + _dot(p.astype(BF16), doh, TN)
            dq_ref[:, lanes] = dq2
            dk_ref[pl.ds(off, NA_KEYS), lanes] += dk2
            dv_ref[pl.ds(off, NA_KEYS), lanes] += dv2

    s_tok = q.shape[0]
    full = pl.BlockSpec((s_tok, 512), lambda r: (0, 0))
    row = pl.BlockSpec((GRID_W, 512), lambda r: (r, 0))
    bias_spec = pl.BlockSpec((None, NA_HEADS, GRID_W, NA_KEYS), lambda r: (r - _na_row_start(r), 0, 0, 0))
    return pl.pallas_call(
        body, name=name,
        out_shape=(_sds((s_tok, 512), F32), _sds((s_tok, 512), F32), _sds((s_tok, 512), F32),
                   _sds((NA_WR, NA_HEADS, GRID_W, NA_KEYS), F32)),
        grid=(ROWS,),
        in_specs=[row, full, full, row, row, pl.BlockSpec((GRID_W, NA_HEADS), lambda r: (r, 0)), bias_spec],
        out_specs=(row, full, full, bias_spec), compiler_params=_params(("arbitrary",)))(q, k, v, o, do, lse, bias)


def t5_bucket_map():
    rel = np.arange(SW_KEYS)[None, :] - SW_BLK - np.arange(SW_BLK)[:, None]
    nb = 16
    max_exact = nb // 2
    n = np.abs(rel)
    large = max_exact + (np.log(np.maximum(n, 1) / max_exact) / np.log(128 / max_exact) * (nb - max_exact)).astype(np.int32)
    large = np.minimum(large, nb - 1)
    return ((rel > 0) * nb + np.where(n < max_exact, n, large)).astype(np.int32)


def t5_bias(table, *, name):
    rel = np.arange(-SW_BLK, SW_BLK + 1)
    nb, max_exact = 16, 8
    n = np.abs(rel)
    large = max_exact + (np.log(np.maximum(n, 1) / max_exact) / np.log(128 / max_exact) * (nb - max_exact)).astype(np.int32)
    bucket = ((rel > 0) * nb + np.where(n < max_exact, n, np.minimum(large, nb - 1))).astype(np.int32)
    u = jnp.pad(table[jnp.asarray(bucket)].T, ((0, 0), (0, SW_KEYS - bucket.shape[0]))).reshape(8, 1, SW_KEYS)

    def body(u_ref, o_ref):
        for h in range(8):
            x = jnp.broadcast_to(u_ref[h], (SW_BLK, SW_KEYS))
            o_ref[h] = pltpu.roll(x, 0, 1, stride=1, stride_axis=0)

    return pl.pallas_call(body, name=name, out_shape=_sds((8, SW_BLK, SW_KEYS), F32), compiler_params=_params())(u)


def _sw_valid(n):
    a = lax.broadcasted_iota(jnp.int32, (SW_BLK, SW_KEYS), 0)
    j = lax.broadcasted_iota(jnp.int32, (SW_BLK, SW_KEYS), 1)
    kpos = (n - 1) * SW_BLK + j
    return (jnp.abs(j - SW_BLK - a) <= SW_BLK) & (kpos >= 0) & (kpos < SEQ)


def _dup_group(x2, g, first):
    rolled = pltpu.roll(x2, HD, 1)
    return jnp.where(first, x2, rolled) if g == 0 else jnp.where(first, rolled, x2)


def sw_fwd(q, kv, t5, sink, *, name):
    def body(q_ref, kv_ref, t5_ref, sink_ref, o_ref, lse_ref):
        n = pl.program_id(0)
        off = pl.multiple_of(n * SW_BLK, SW_BLK)
        first = _lane_halves()
        valid = _sw_valid(n)
        k2 = kv_ref[pl.ds(off, SW_KEYS), 0:128]
        v2 = kv_ref[pl.ds(off, SW_KEYS), 128:256]
        for j in range(4):
            g = j // 2
            kk = _dup_group(k2, g, first)
            vv = _dup_group(v2, g, first)
            lanes = slice(128 * j, 128 * j + 128)
            q2 = q_ref[:, lanes]
            o2 = jnp.zeros((SW_BLK, 128), F32)
            for half in range(2):
                h = 2 * j + half
                sel = first if half == 0 else jnp.logical_not(first)
                s = _dot(jnp.where(sel, q2, jnp.zeros_like(q2)), kk, NT)
                s = jnp.where(valid, s + t5_ref[h], NEG)
                snk = sink_ref[:, h:h + 1]
                m = jnp.maximum(jnp.max(s, axis=-1, keepdims=True), snk)
                e = jnp.exp(s - m)
                den = jnp.sum(e, axis=-1, keepdims=True) + jnp.exp(snk - m)
                p = (e / den).astype(BF16)
                o2 = o2 + _dot(p, jnp.where(sel, vv, jnp.zeros_like(vv)))
                lse_ref[:, h:h + 1] = m + jnp.log(den)
            o_ref[:, lanes] = o2.astype(BF16)

    s_tok = q.shape[0]
    blk = pl.BlockSpec((SW_BLK, 512), lambda n: (n, 0))
    return pl.pallas_call(
        body, name=name, out_shape=(_sds((s_tok, 512), BF16), _sds((s_tok, 8), F32)), grid=(SW_NB,),
        in_specs=[blk, pl.BlockSpec(kv.shape, lambda n: (0, 0)), pl.BlockSpec((8, SW_BLK, SW_KEYS), lambda n: (0, 0, 0)),
                  pl.BlockSpec((1, 8), lambda n: (0, 0))],
        out_specs=(blk, pl.BlockSpec((SW_BLK, 8), lambda n: (n, 0))), compiler_params=_params(("parallel",)))(q, kv, t5, sink)


def sw_bwd(q, kv, o, do, lse, t5, sink, *, name):
    def body(q_ref, kv_ref, o_ref, do_ref, lse_ref, t5_ref, sink_ref, dq_ref, dkv_ref, dt5_ref, dsink_ref):
        n = pl.program_id(0)

        @pl.when(n == 0)
        def _():
            dkv_ref[...] = jnp.zeros_like(dkv_ref)
            dt5_ref[...] = jnp.zeros_like(dt5_ref)
            dsink_ref[...] = jnp.zeros_like(dsink_ref)

        off = pl.multiple_of(n * SW_BLK, SW_BLK)
        first = _lane_halves()
        valid = _sw_valid(n)
        k2 = kv_ref[pl.ds(off, SW_KEYS), 0:128]
        v2 = kv_ref[pl.ds(off, SW_KEYS), 128:256]
        dk_groups, dv_groups = [], []
        for g in range(2):
            kk = _dup_group(k2, g, first)
            vv = _dup_group(v2, g, first)
            dkk = jnp.zeros((SW_KEYS, 128), F32)
            dvv = jnp.zeros((SW_KEYS, 128), F32)
            for j in (2 * g, 2 * g + 1):
                lanes = slice(128 * j, 128 * j + 128)
                q2 = q_ref[:, lanes]
                do2 = do_ref[:, lanes]
                prod = do2.astype(F32) * o_ref[:, lanes].astype(F32)
                dq2 = jnp.zeros((SW_BLK, 128), F32)
                for half in range(2):
                    h = 2 * j + half
                    sel = first if half == 0 else jnp.logical_not(first)
                    qh = jnp.where(sel, q2, jnp.zeros_like(q2))
                    doh = jnp.where(sel, do2, jnp.zeros_like(do2))
                    s = _dot(qh, kk, NT)
                    s = jnp.where(valid, s + t5_ref[h], NEG)
                    lse = lse_ref[:, h:h + 1]
                    p = jnp.exp(s - lse)
                    dp = _dot(doh, vv, NT)
                    delta = jnp.sum(jnp.where(sel, prod, 0.0), axis=-1, keepdims=True)
                    ds = p * (dp - delta)
                    dt5_ref[h] += ds
                    dsink_ref[:, h:h + 1] += -jnp.sum(jnp.exp(sink_ref[:, h:h + 1] - lse) * delta, axis=0, keepdims=True)
                    dsb = ds.astype(BF16)
                    dq2 = dq2 + _dot(dsb, jnp.where(sel, kk, jnp.zeros_like(kk)))
                    dkk = dkk + _dot(dsb, qh, TN)
                    dvv = dvv + _dot(p.astype(BF16), doh, TN)
                dq_ref[:, lanes] = dq2
            dk_groups.append(dkk + pltpu.roll(dkk, HD, 1))
            dv_groups.append(dvv + pltpu.roll(dvv, HD, 1))
        dkv_ref[pl.ds(off, SW_KEYS), 0:128] += jnp.where(first, dk_groups[0], dk_groups[1])
        dkv_ref[pl.ds(off, SW_KEYS), 128:256] += jnp.where(first, dv_groups[0], dv_groups[1])

    s_tok = q.shape[0]
    blk = pl.BlockSpec((SW_BLK, 512), lambda n: (n, 0))
    kv_spec = pl.BlockSpec(kv.shape, lambda n: (0, 0))
    t5_spec = pl.BlockSpec((8, SW_BLK, SW_KEYS), lambda n: (0, 0, 0))
    vec = pl.BlockSpec((1, 8), lambda n: (0, 0))
    return pl.pallas_call(
        body, name=name,
        out_shape=(_sds((s_tok, 512), F32), _sds(kv.shape, F32), _sds((8, SW_BLK, SW_KEYS), F32), _sds((1, 8), F32)),
        grid=(SW_NB,), in_specs=[blk, kv_spec, blk, blk, pl.BlockSpec((SW_BLK, 8), lambda n: (n, 0)), t5_spec, vec],
        out_specs=(blk, kv_spec, t5_spec, vec), compiler_params=_params(("arbitrary",)))(q, kv, o, do, lse, t5, sink)


def gate_fwd(zg, bias, pa, ps, *, name, tm=512):
    def body(z0_ref, z1_ref, b0_ref, b1_ref, pa_ref, ps_ref, m_ref):
        g0 = jax.nn.sigmoid(z0_ref[...] + b0_ref[...])
        g1 = jax.nn.sigmoid(z1_ref[...] + b1_ref[...])
        m_ref[...] = (g0 * pa_ref[...] + g1 * ps_ref[...]).astype(BF16)

    s = zg.shape[0]
    half = lambda j: pl.BlockSpec((tm, DM), lambda i, j=j: (i, j))
    bvec = lambda j: pl.BlockSpec((1, DM), lambda i, j=j: (0, j))
    return pl.pallas_call(
        body, name=name, out_shape=_sds((s, DM), BF16), grid=(s // tm,),
        in_specs=[half(0), half(1), bvec(0), bvec(1), half(0), half(0)], out_specs=half(0),
        compiler_params=_params(("parallel",)))(zg, zg, bias, bias, pa, ps)


def gate_bwd(dm, zg, bias, pa, ps, *, name, tm=512):
    def body(dm_ref, z0_ref, z1_ref, b0_ref, b1_ref, pa_ref, ps_ref, dpa_ref, dps_ref, dz_ref, db_ref):
        @pl.when(pl.program_id(0) == 0)
        def _():
            db_ref[...] = jnp.zeros_like(db_ref)

        dm = dm_ref[...]
        g0 = jax.nn.sigmoid(z0_ref[...] + b0_ref[...])
        g1 = jax.nn.sigmoid(z1_ref[...] + b1_ref[...])
        dpa_ref[...] = (dm * g0).astype(BF16)
        dps_ref[...] = (dm * g1).astype(BF16)
        dz0 = dm * pa_ref[...] * g0 * (1.0 - g0)
        dz1 = dm * ps_ref[...] * g1 * (1.0 - g1)
        dz_ref[:, 0:DM] = dz0.astype(BF16)
        dz_ref[:, DM:2 * DM] = dz1.astype(BF16)
        db_ref[:, 0:DM] += jnp.sum(dz0, axis=0, keepdims=True)
        db_ref[:, DM:2 * DM] += jnp.sum(dz1, axis=0, keepdims=True)

    s = zg.shape[0]
    half = lambda j: pl.BlockSpec((tm, DM), lambda i, j=j: (i, j))
    bvec = lambda j: pl.BlockSpec((1, DM), lambda i, j=j: (0, j))
    return pl.pallas_call(
        body, name=name,
        out_shape=(_sds((s, DM), BF16), _sds((s, DM), BF16), _sds((s, GATE_W), BF16), _sds((1, GATE_W), F32)),
        grid=(s // tm,), in_specs=[half(0), half(0), half(1), bvec(0), bvec(1), half(0), half(0)],
        out_specs=(half(0), half(0), pl.BlockSpec((tm, GATE_W), lambda i: (i, 0)), pl.BlockSpec((1, GATE_W), lambda i: (0, 0))),
        compiler_params=_params(("arbitrary",)))(dm, zg, zg, bias, bias, pa, ps)


def loss_head(y, target, *, name, tm=512):
    def body(y_ref, t_ref, dy_ref, l_ref):
        @pl.when(pl.program_id(0) == 0)
        def _():
            l_ref[...] = jnp.zeros_like(l_ref)

        err = y_ref[...] - t_ref[...]
        dy_ref[...] = err * (1.0 / DM)
        l_ref[...] += 0.5 * jnp.sum(jnp.mean(err * err, axis=-1, keepdims=True), axis=0, keepdims=True)

    s = y.shape[0]
    tile = pl.BlockSpec((tm, DM), lambda i: (i, 0))
    return pl.pallas_call(
        body, name=name, out_shape=(_sds((s, DM), F32), _sds((1, 128), F32)), grid=(s // tm,), in_specs=[tile, tile],
        out_specs=(tile, pl.BlockSpec((1, 128), lambda i: (0, 0))), compiler_params=_params(("arbitrary",)))(y, target)


def adamw(w, g, m, v, *, name):
    def body(w_ref, g_ref, m_ref, v_ref, d_ref, nm_ref, nv_ref):
        g = g_ref[...]
        nm = ADAM_B1 * m_ref[...] + (1.0 - ADAM_B1) * g
        nv = ADAM_B2 * v_ref[...] + (1.0 - ADAM_B2) * jnp.square(g)
        m_hat = nm / (1.0 - ADAM_B1 ** ADAM_STEP)
        v_hat = nv / (1.0 - ADAM_B2 ** ADAM_STEP)
        d_ref[...] = -ADAM_LR * (m_hat / (jnp.sqrt(v_hat) + ADAM_EPS) + ADAM_WD * w_ref[...])
        nm_ref[...] = nm
        nv_ref[...] = nv

    b, k, n = w.shape
    tk = k // 4 if k % 32 == 0 else k
    spec = pl.BlockSpec((None, tk, n), lambda i, j: (i, j, 0))
    out = _sds(w.shape, F32)
    return pl.pallas_call(
        body, name=name, out_shape=(out, out, out), grid=(b, k // tk), in_specs=[spec] * 4, out_specs=(spec,) * 3,
        compiler_params=_params(("parallel", "parallel")))(w, g, m, v)


def adamw_layer(w, m, v, mine, theirs, cidx, layer, filled=None, *, name):
    _, k, n = w.shape
    nt = 2
    tk = k // 2 // nt

    def body(c_ref, w_ref, m_ref, v_ref, a_ref, b_ref, *rest):
        g_ref, d_ref, nm_ref, nv_ref = rest[-4:]
        g = jnp.where(pl.program_id(0) == c_ref[0], a_ref[...], b_ref[...])
        g_ref[...] = g
        nm = ADAM_B1 * m_ref[...] + (1.0 - ADAM_B1) * g
        nv = ADAM_B2 * v_ref[...] + (1.0 - ADAM_B2) * jnp.square(g)
        m_hat = nm / (1.0 - ADAM_B1 ** ADAM_STEP)
        v_hat = nv / (1.0 - ADAM_B2 ** ADAM_STEP)
        d_ref[...] = -ADAM_LR * (m_hat / (jnp.sqrt(v_hat) + ADAM_EPS) + ADAM_WD * w_ref[...])
        nm_ref[...] = nm
        nv_ref[...] = nv

    full = pl.BlockSpec((None, tk, n), lambda hf, t, c: (layer, hf * nt + t, 0))
    half_mine = pl.BlockSpec((tk, n), lambda hf, t, c: (jnp.where(hf == c[0], t, 0), 0))
    half_theirs = pl.BlockSpec((tk, n), lambda hf, t, c: (jnp.where(hf != c[0], t, 0), 0))
    out = _sds(w.shape, F32)
    ins, specs, aliases = [cidx, w, m, v, mine, theirs], [full, full, full, half_mine, half_theirs], {}
    if filled is not None:
        aliases = {len(ins) + i: i for i in range(4)}
        ins += list(filled)
        specs += [pl.BlockSpec(memory_space=pl.ANY)] * 4
    return pl.pallas_call(
        body, name=name, out_shape=(out, out, out, out),
        grid_spec=pltpu.PrefetchScalarGridSpec(
            num_scalar_prefetch=1, grid=(2, nt), in_specs=specs, out_specs=(full, full, full, full)),
        input_output_aliases=aliases,
        compiler_params=_params(("arbitrary", "arbitrary")))(*ins)


def t5_table_grad(dt5_a, dt5_b, *, name):
    def body(a_ref, b_ref, map_ref, o_ref):
        d = a_ref[...] + b_ref[...]
        bucket = map_ref[...]
        for b in range(32):
            hit = (bucket == b)[None]
            o_ref[b] = jnp.sum(jnp.sum(jnp.where(hit, d, 0.0), axis=2), axis=1, keepdims=True)

    return pl.pallas_call(
        body, name=name, out_shape=_sds((32, 8, 1), F32), compiler_params=_params())(
            dt5_a, dt5_b, jnp.asarray(t5_bucket_map()))


def rpb_grad(dbias, *, name):
    q = np.arange(GRID_W)[:, None]
    kc = np.arange(GRID_W)[None, :]
    diag = ((kc - q + NA_WC - 1)[..., None] == np.arange(128)).reshape(GRID_W * GRID_W, 128).astype(np.float32)
    p = np.arange(NA_WR)[:, None]
    kr = np.arange(NA_WR)[None, :]
    rows = ((kr - p + NA_WR - 1).reshape(-1)[None, :] == np.arange(16)[:, None]).astype(np.float32)
    d = dbias.reshape(NA_WR, NA_HEADS, GRID_W, NA_WR, GRID_W).transpose(0, 1, 3, 2, 4).reshape(512, GRID_W * GRID_W)
    e = mm(d, jnp.asarray(diag), name=name + "_cols", exact=True, tk=1024)
    e = e.reshape(NA_WR, NA_HEADS, NA_WR, 128).transpose(0, 2, 1, 3).reshape(NA_WR * NA_WR, NA_HEADS * 128)
    out = mm(jnp.asarray(rows), e, name=name + "_rows", exact=True)
    return out.reshape(16, NA_HEADS, 128)[:2 * NA_WR - 1, :, :2 * NA_WC - 1].transpose(1, 0, 2)


BIG = ("ffn1_w_gate", "ffn1_w_up", "ffn1_w_down", "w_in", "w_branch_na", "w_branch_sw", "w_out",
       "ffn2_w_gate", "ffn2_w_up", "ffn2_w_down")
SMALL = ("ffn1_norm", "mix_norm", "b_gate", "na_q_norm", "na_k_norm", "na_rpb", "sw_q_norm", "sw_k_norm", "sw_sink",
         "ffn2_norm")


def _cols_to_full(w4):
    return w4.transpose(1, 0, 2).reshape(w4.shape[1], NSH * w4.shape[2])


def _full_to_cols(w):
    return w.reshape(w.shape[0], NSH, w.shape[1] // NSH).transpose(1, 0, 2)


def _mixer_weights(g):
    w_in_t = g["w_in"].reshape(IN_W, DM)
    return dict(w_att_t=w_in_t[:ATT_W], w_gz_t=w_in_t[ATT_W:], wa=_cols_to_full(g["w_branch_na"]),
                ws=_cols_to_full(g["w_branch_sw"]), wo=g["w_out"].reshape(DM, DM))


def layer_fwd(x, p, g, t5b, dep=None):
    w = _mixer_weights(g)
    row = lambda v: v.reshape(1, -1)
    y1, h1, gg1, uu1 = ffn_fwd(x, row(p["ffn1_norm"]), g["ffn1_w_gate"], g["ffn1_w_up"], g["ffn1_w_down"], name="ffn_fwd",
                               dep=dep)
    hm = rms_fwd(y1, row(p["mix_norm"]), name="mix_norm_fwd")
    z = mm(hm, w["w_att_t"], tb=True, name="proj_att", tn=768)
    zg = mm(hm, w["w_gz_t"], tb=True, name="proj_gate", tn=1024)
    qa, ka, va, qs, kv = qknorm_fwd(z, p["na_q_norm"], p["na_k_norm"], p["sw_q_norm"], p["sw_k_norm"], name="qknorm_fwd")
    bias = na_bias_table(p["na_rpb"], name="na_bias_table")
    o_na, lse_na = na_fwd(qa, ka, va, bias, name="na_fwd")
    kvp = jnp.pad(kv, ((SW_BLK, SW_BLK), (0, 0)))
    sink = row(p["sw_sink"])
    o_sw, lse_sw = sw_fwd(qs, kvp, t5b, sink, name="sw_fwd")
    pa = mm(o_na, w["wa"], name="branch_na")
    ps = mm(o_sw, w["ws"], name="branch_sw")
    merged = gate_fwd(zg, row(p["b_gate"]), pa, ps, name="gate_fwd")
    y2 = mm(merged, w["wo"], add=y1, name="out_proj")
    y3, h2, gg2, uu2 = ffn_fwd(y2, row(p["ffn2_norm"]), g["ffn2_w_gate"], g["ffn2_w_up"], g["ffn2_w_down"], name="ffn_fwd")
    saved = dict(x=x, y1=y1, h1=h1, gg1=gg1, uu1=uu1, hm=hm, z=z, zg=zg, qa=qa, ka=ka, va=va, qs=qs, kvp=kvp, bias=bias,
                 o_na=o_na, lse_na=lse_na, o_sw=o_sw, lse_sw=lse_sw, pa=pa, ps=ps, merged=merged, y2=y2, h2=h2, gg2=gg2,
                 uu2=uu2, w=w, sink=sink)
    return y3, saved


def layer_bwd(dy3, sv, p, g, t5b, dep=None):
    w = sv["w"]
    row = lambda v: v.reshape(1, -1)
    fold = lambda v: v.reshape(-1, HD).sum(axis=0)
    big, small = {}, {}
    dy2, small["ffn2_norm"], act, dg, du = ffn_bwd_tokens(
        dy3, sv["y2"], row(p["ffn2_norm"]), sv["gg2"], sv["uu2"], g["ffn2_w_gate"], g["ffn2_w_up"], g["ffn2_w_down"],
        name="ffn_bwd_tokens", dep=dep)
    big["ffn2_w_gate"], big["ffn2_w_up"], big["ffn2_w_down"] = ffn_bwd_weights(sv["h2"], dy3, act, dg, du, name="ffn_bwd_weights")
    dmerged = mm(dy2, w["wo"], tb=True, name="out_proj_dx")
    big["w_out"] = mm(sv["merged"], dy2, ta=True, out_dtype=BF16, tk=512, name="out_proj_dw").reshape(NSH, DM // NSH, DM)
    dpa, dps, dzg, small["b_gate"] = gate_bwd(dmerged, sv["zg"], row(p["b_gate"]), sv["pa"], sv["ps"], name="gate_bwd")
    big["w_branch_na"] = _full_to_cols(mm(sv["o_na"], dpa, ta=True, out_dtype=BF16, tk=512, name="branch_dw"))
    big["w_branch_sw"] = _full_to_cols(mm(sv["o_sw"], dps, ta=True, out_dtype=BF16, tk=512, name="branch_dw"))
    do_na = mm(dpa, w["wa"], tb=True, out_dtype=BF16, name="branch_dx")
    do_sw = mm(dps, w["ws"], tb=True, out_dtype=BF16, name="branch_dx")
    dqa, dka, dva, dbias = na_bwd(sv["qa"], sv["ka"], sv["va"], sv["o_na"], do_na, sv["lse_na"], sv["bias"], name="na_bwd")
    dqs, dkvp, dt5, dsink = sw_bwd(sv["qs"], sv["kvp"], sv["o_sw"], do_sw, sv["lse_sw"], t5b, sv["sink"], name="sw_bwd")
    dkv = dkvp[SW_BLK:SW_BLK + SEQ]
    dz, dgqa, dgka, dgqs, dgks = qknorm_bwd(sv["z"], dqa, dka, dva, dqs, dkv, p["na_q_norm"], p["na_k_norm"],
                                            p["sw_q_norm"], p["sw_k_norm"], name="qknorm_bwd")
    small["na_q_norm"], small["na_k_norm"], small["sw_q_norm"], small["sw_k_norm"] = fold(dgqa), fold(dgka), fold(dgqs), fold(dgks)
    small["na_rpb"] = rpb_grad(dbias, name="rpb_grad")
    small["sw_sink"] = dsink
    gw_att_t = mm(dz, sv["hm"], ta=True, out_dtype=BF16, tk=512, tm=768, name="proj_att_dw")
    gw_gz_t = mm(dzg, sv["hm"], ta=True, out_dtype=BF16, tk=512, name="proj_gate_dw")
    big["w_in"] = jnp.concatenate([gw_att_t, gw_gz_t], axis=0).reshape(NSH, IN_W // NSH, DM)
    dh = mm(dz, w["w_att_t"], tk=768, name="proj_att_dx")
    dh = mm(dzg, w["w_gz_t"], add=dh, name="proj_gate_dx")
    dy1, small["mix_norm"] = rms_bwd(dh, sv["y1"], row(p["mix_norm"]), dy2, name="mix_norm_bwd")
    dx, small["ffn1_norm"], act, dg, du = ffn_bwd_tokens(
        dy1, sv["x"], row(p["ffn1_norm"]), sv["gg1"], sv["uu1"], g["ffn1_w_gate"], g["ffn1_w_up"], g["ffn1_w_down"],
        name="ffn_bwd_tokens")
    big["ffn1_w_gate"], big["ffn1_w_up"], big["ffn1_w_down"] = ffn_bwd_weights(sv["h1"], dy1, act, dg, du, name="ffn_bwd_weights")
    return dx, big, small, dt5


ANY = pl.BlockSpec(memory_space=pl.ANY)


def _place():
    x, y, c = lax.axis_index("x"), lax.axis_index("y"), lax.axis_index("c")
    chips = [(1 - x, y), (x, 1 - y), (1 - x, 1 - y)]
    return x, y, c, chips


def _remote(src, dst, send_sem, recv_sem, to):
    return pltpu.make_async_remote_copy(src_ref=src, dst_ref=dst, send_sem=send_sem, recv_sem=recv_sem, device_id=to,
                                        device_id_type=MESH)


HBM = pl.BlockSpec(memory_space=pltpu.HBM)
SEM = pl.BlockSpec(memory_space=pltpu.SEMAPHORE)
ORDERED_EFFECT = pltpu.SideEffectType.DATAFLOW_SIDE_EFFECTING


def _in_hbm(v):
    return pltpu.with_memory_space_constraint(v, pltpu.HBM)


def _row_half(ref_shape_rows, c):
    half = ref_shape_rows // 2
    return pl.ds(c * half, half)


def _ici_gather_copies(w, land, send_sems, recv_sems):
    x, y, c, chips = _place()
    me = 2 * x + y
    copies = []
    for a in range(len(w)):
        rows = _row_half(w[a].shape[0], c)
        for k, chip in enumerate(chips):
            copies.append(_remote(w[a].at[rows], land[a].at[me, rows], send_sems.at[3 * a + k], recv_sems.at[3 * a + k],
                                  (*chip, c)))
    return copies


def _d2d_gather_copies(w, land, send_sems, recv_sems):
    x, y, c, chips = _place()
    me, sibling = 2 * x + y, (x, y, 1 - c)
    copies = []
    for a in range(len(w)):
        rows = _row_half(w[a].shape[0], c)
        for k, (cx, cy) in enumerate(chips):
            blk = land[a].at[2 * cx + cy, rows]
            copies.append(_remote(blk, blk, send_sems.at[4 * a + k], recv_sems.at[4 * a + k], sibling))
        copies.append(_remote(w[a], land[a].at[me], send_sems.at[4 * a + 3], recv_sems.at[4 * a + 3], sibling))
    return copies


def _d2d_gather_waits(w, land, send_sems, recv_sems):
    x, y, c, chips = _place()
    me, sibling = 2 * x + y, (x, y, 1 - c)
    waits = []
    for a in range(len(w)):
        rows = _row_half(w[a].shape[0], 1 - c)
        for k, (cx, cy) in enumerate(chips):
            blk = land[a].at[2 * cx + cy, rows]
            waits.append(_remote(blk, blk, send_sems.at[4 * a + k], recv_sems.at[4 * a + k], sibling))
        waits.append(_remote(w[a], land[a].at[me], send_sems.at[4 * a + 3], recv_sems.at[4 * a + 3], sibling))
    return waits


def gather_layer(shards, *, name):
    n = len(shards)

    def body(*refs):
        w, land = refs[:n], refs[n:2 * n]
        ici_send, ici_recv, d2d_send, d2d_recv = refs[2 * n:]
        x, y, c, chips = _place()
        ici = _ici_gather_copies(w, land, ici_send, ici_recv)
        d2d = _d2d_gather_copies(w, land, d2d_send, d2d_recv)
        for cp in ici:
            cp.start()
        for a in range(n):
            d2d[4 * a + 3].start()
        for a in range(n):
            for k in range(3):
                ici[3 * a + k].wait_recv()
                d2d[4 * a + k].start()
        for cp in _d2d_gather_waits(w, land, d2d_send, d2d_recv):
            cp.wait_recv()
        for cp in ici + d2d:
            cp.wait_send()

    return pl.pallas_call(
        body, name=name, out_shape=tuple(pltpu.HBM((NSH,) + s.shape, s.dtype) for s in shards),
        in_specs=[ANY] * n, out_specs=tuple([ANY] * n),
        scratch_shapes=[pltpu.SemaphoreType.DMA((3 * n,)), pltpu.SemaphoreType.DMA((3 * n,)), pltpu.SemaphoreType.DMA((4 * n,)),
                        pltpu.SemaphoreType.DMA((4 * n,))])(*shards)


def gather_start(shards, dep, *, name):
    n = len(shards)

    def body(*refs):
        w, land = refs[:n], refs[n:2 * n]
        send_sems, recv_sems = refs[2 * n + 1:2 * n + 3]
        token = refs[-1]
        for cp in _ici_gather_copies(w, land, send_sems, recv_sems):
            cp.start()
        token[...] = jnp.zeros_like(token)

    lands = [lax.empty((NSH,) + s.shape, s.dtype) for s in shards]
    res = pl.pallas_call(
        body, name=name,
        out_shape=(pltpu.SemaphoreType.DMA((3 * n,)), pltpu.SemaphoreType.DMA((3 * n,)))
        + tuple(pltpu.HBM(s.shape, s.dtype) for s in shards) + tuple(pltpu.HBM(l.shape, l.dtype) for l in lands)
        + (_sds((8, 128), F32),),
        in_specs=[HBM] * (2 * n) + [ANY], out_specs=(SEM, SEM) + (HBM,) * (2 * n) + (pl.BlockSpec(memory_space=pltpu.VMEM),),
        input_output_aliases={i: 2 + i for i in range(2 * n)},
        compiler_params=pltpu.CompilerParams(has_side_effects=ORDERED_EFFECT))(
            *[_in_hbm(s) for s in shards], *[_in_hbm(l) for l in lands], dep)
    return res[0], res[1], list(res[2:2 + n]), list(res[2 + n:2 + 2 * n]), res[-1]


def gather_wait(send_sems, recv_sems, shards, lands, after, *, name):
    n = len(shards)

    def body(*refs):
        w, land = refs[:n], refs[n:2 * n]
        send, recv = refs[2 * n:2 * n + 2]
        for cp in _ici_gather_copies(w, land, send, recv):
            cp.wait_send()
            cp.wait_recv()

    res = pl.pallas_call(
        body, name=name,
        out_shape=tuple(pltpu.HBM(s.shape, s.dtype) for s in shards) + tuple(pltpu.HBM(l.shape, l.dtype) for l in lands),
        in_specs=[HBM] * (2 * n) + [SEM, SEM, ANY], out_specs=(HBM,) * (2 * n),
        input_output_aliases={i: i for i in range(2 * n)},
        compiler_params=pltpu.CompilerParams(has_side_effects=ORDERED_EFFECT))(*shards, *lands, send_sems, recv_sems, after)
    return list(res[:n]), list(res[n:])


def gather_finish(shards, lands, *, name):
    n = len(shards)

    def body(*refs):
        w, land = refs[:n], refs[n:2 * n]
        send_sems, recv_sems = refs[3 * n:]
        d2d = _d2d_gather_copies(w, land, send_sems, recv_sems)
        for cp in d2d:
            cp.start()
        for cp in _d2d_gather_waits(w, land, send_sems, recv_sems):
            cp.wait_recv()
        for cp in d2d:
            cp.wait_send()

    return list(pl.pallas_call(
        body, name=name, out_shape=tuple(pltpu.HBM(l.shape, l.dtype) for l in lands),
        in_specs=[ANY] * (2 * n), out_specs=tuple([ANY] * n), input_output_aliases={n + i: i for i in range(n)},
        scratch_shapes=[pltpu.SemaphoreType.DMA((4 * n,)), pltpu.SemaphoreType.DMA((4 * n,))])(*shards, *lands))


def pair_exchange(grads, *, name):
    n = len(grads)

    def body(*refs):
        g, buf = refs[:n], refs[n:2 * n]
        send_sems, recv_sems = refs[2 * n:]
        x, y, c, _ = _place()
        copies = []
        for a in range(n):
            half = g[a].shape[1] // 2
            cp = _remote(g[a].at[:, pl.ds((1 - c) * half, half)], buf[a], send_sems.at[a], recv_sems.at[a], (x, y, 1 - c))
            cp.start()
            copies.append(cp)
        for cp in copies:
            cp.wait()

    return pl.pallas_call(
        body, name=name, out_shape=tuple(pltpu.HBM((NSH, g.shape[1] // 2, g.shape[2]), g.dtype) for g in grads),
        in_specs=[ANY] * n, out_specs=tuple([ANY] * n),
        scratch_shapes=[pltpu.SemaphoreType.DMA((n,)), pltpu.SemaphoreType.DMA((n,))])(*grads)


def _chip_exchange_copies(s, buf, send_sems, recv_sems):
    x, y, c, chips = _place()
    return [_remote(s[a].at[2 * cx + cy], buf[a].at[k], send_sems.at[3 * a + k], recv_sems.at[3 * a + k], (cx, cy, c))
            for a in range(len(s)) for k, (cx, cy) in enumerate(chips)]


def chip_exchange(sums, *, name):
    n = len(sums)

    def body(*refs):
        copies = _chip_exchange_copies(refs[:n], refs[n:2 * n], *refs[2 * n:])
        for cp in copies:
            cp.start()
        for cp in copies:
            cp.wait()

    return pl.pallas_call(
        body, name=name, out_shape=tuple(pltpu.HBM((3,) + s.shape[1:], s.dtype) for s in sums),
        in_specs=[ANY] * n, out_specs=tuple([ANY] * n),
        scratch_shapes=[pltpu.SemaphoreType.DMA((3 * n,)), pltpu.SemaphoreType.DMA((3 * n,))])(*sums)


def chip_exchange_start(sums, *, name):
    n = len(sums)

    def body(*refs):
        for cp in _chip_exchange_copies(refs[:n], refs[n:2 * n], refs[2 * n], refs[2 * n + 1]):
            cp.start()
        refs[-1][...] = jnp.zeros_like(refs[-1])

    lands = [lax.empty((3,) + s.shape[1:], s.dtype) for s in sums]
    res = pl.pallas_call(
        body, name=name,
        out_shape=(pltpu.SemaphoreType.DMA((3 * n,)), pltpu.SemaphoreType.DMA((3 * n,)))
        + tuple(pltpu.HBM(s.shape, s.dtype) for s in sums) + tuple(pltpu.HBM(l.shape, l.dtype) for l in lands)
        + (_sds((8, 128), F32),),
        in_specs=[HBM] * (2 * n), out_specs=(SEM, SEM) + (HBM,) * (2 * n) + (pl.BlockSpec(memory_space=pltpu.VMEM),),
        input_output_aliases={i: 2 + i for i in range(2 * n)},
        compiler_params=pltpu.CompilerParams(has_side_effects=ORDERED_EFFECT))(
            *[_in_hbm(s) for s in sums], *[_in_hbm(l) for l in lands])
    return res[0], res[1], list(res[2:2 + n]), list(res[2 + n:2 + 2 * n]), res[-1]


def chip_exchange_wait(send_sems, recv_sems, sums, lands, after, *, name):
    n = len(sums)

    def body(*refs):
        for cp in _chip_exchange_copies(refs[:n], refs[n:2 * n], refs[2 * n], refs[2 * n + 1]):
            cp.wait_send()
            cp.wait_recv()

    res = pl.pallas_call(
        body, name=name,
        out_shape=tuple(pltpu.HBM(s.shape, s.dtype) for s in sums) + tuple(pltpu.HBM(l.shape, l.dtype) for l in lands),
        in_specs=[HBM] * (2 * n) + [SEM, SEM] + [ANY] * len(after), out_specs=(HBM,) * (2 * n),
        input_output_aliases={i: i for i in range(2 * n)},
        compiler_params=pltpu.CompilerParams(has_side_effects=ORDERED_EFFECT))(*sums, *lands, send_sems, recv_sems, *after)
    return list(res[:n]), list(res[n:])


def pair_send(halves, *, name):
    n = len(halves)

    def body(*refs):
        h, got = refs[:n], refs[n:2 * n]
        send_sems, recv_sems = refs[2 * n:]
        x, y, c, _ = _place()
        copies = []
        for i in range(n):
            cp = _remote(h[i], got[i], send_sems.at[i], recv_sems.at[i], (x, y, 1 - c))
            cp.start()
            copies.append(cp)
        for cp in copies:
            cp.wait()

    return list(pl.pallas_call(
        body, name=name, out_shape=tuple(pltpu.HBM(v.shape, v.dtype) for v in halves),
        in_specs=[ANY] * n, out_specs=tuple([ANY] * n),
        scratch_shapes=[pltpu.SemaphoreType.DMA((n,)), pltpu.SemaphoreType.DMA((n,))])(*halves))


def allreduce_small(v, *, name):
    rows = v.shape[0]

    def body(v_ref, o_ref, gath, send_sems, recv_sems):
        x, y, c, _ = _place()
        me = 4 * x + 2 * y + c
        gath[me] = v_ref[...]
        copies = []
        for k in range(1, 8):
            fx, fy, fc = (k >> 2) & 1, (k >> 1) & 1, k & 1
            peer = (jnp.where(fx, 1 - x, x), jnp.where(fy, 1 - y, y), jnp.where(fc, 1 - c, c))
            cp = _remote(v_ref, gath.at[me], send_sems.at[k - 1], recv_sems.at[k - 1], peer)
            cp.start()
            copies.append(cp)
        for cp in copies:
            cp.wait()
        acc = gath[0]
        for d in range(1, 8):
            acc = acc + gath[d]
        o_ref[...] = acc

    return pl.pallas_call(
        body, name=name, out_shape=_sds(v.shape, F32),
        in_specs=[pl.BlockSpec(memory_space=pltpu.VMEM)], out_specs=pl.BlockSpec(memory_space=pltpu.VMEM),
        scratch_shapes=[pltpu.VMEM((8, rows, 128), F32), pltpu.SemaphoreType.DMA((7,)), pltpu.SemaphoreType.DMA((7,))])(v)


def add_halves(g, buf, cidx, *, name):
    _, k, n = g.shape

    def body(c_ref, g_ref, b_ref, o_ref):
        o_ref[...] = (g_ref[...].astype(F32) + b_ref[...].astype(F32)).astype(BF16)

    blk = pl.BlockSpec((None, k // 2, n), lambda s, c: (s, 0, 0))
    return pl.pallas_call(
        body, name=name, out_shape=_sds(buf.shape, BF16),
        grid_spec=pltpu.PrefetchScalarGridSpec(
            num_scalar_prefetch=1, grid=(NSH,),
            in_specs=[pl.BlockSpec((None, k // 2, n), lambda s, c: (s, c[0], 0)), blk], out_specs=blk),
        compiler_params=_params(("parallel",)))(cidx, g, buf)


def add_chips(sums, buf, sidx, *, name, dep=None):
    _, kh, n = sums.shape

    def body(s_ref, mine_ref, b_ref, *rest):
        o_ref = rest[-1]
        o_ref[...] = (mine_ref[...].astype(F32) + b_ref[0].astype(F32)) + (b_ref[1].astype(F32) + b_ref[2].astype(F32))

    ins, specs = _with_dep(
        [sums, buf], [pl.BlockSpec((None, kh, n), lambda i, s: (s[0], 0, 0)), pl.BlockSpec((3, kh, n), lambda i, s: (0, 0, 0))],
        dep)
    return pl.pallas_call(
        body, name=name, out_shape=_sds((kh, n), F32),
        grid_spec=pltpu.PrefetchScalarGridSpec(
            num_scalar_prefetch=1, grid=(1,), in_specs=specs, out_specs=pl.BlockSpec((kh, n), lambda i, s: (0, 0))),
        compiler_params=_params(("arbitrary",)))(sidx, *ins)


PARAMS = ("ffn1_norm", "ffn1_w_gate", "ffn1_w_up", "ffn1_w_down", "mix_norm", "w_in", "b_gate", "na_q_norm", "na_k_norm",
          "na_rpb", "sw_q_norm", "sw_k_norm", "sw_sink", "t5_rel_table", "w_branch_na", "w_branch_sw", "w_out", "ffn2_norm",
          "ffn2_w_gate", "ffn2_w_up", "ffn2_w_down")
SMALL_ALL = tuple(n for n in PARAMS if n not in BIG)
TRANSPOSED = ("ffn1_w_gate", "ffn1_w_up", "w_in", "ffn2_w_gate", "ffn2_w_up")
SMALL_ROWS = 152


def _pack_small(vals):
    flat = jnp.concatenate([vals[n].reshape(-1).astype(F32) for n in SMALL_ALL] + [vals["loss"].reshape(-1)])
    return jnp.pad(flat, (0, SMALL_ROWS * 128 - flat.shape[0])).reshape(SMALL_ROWS, 128)


def _unpack_small(packed, like):
    flat, out, off = packed.reshape(-1), {}, 0
    for n in SMALL_ALL:
        size = math.prod(like[n].shape)
        out[n] = flat[off:off + size].reshape(like[n].shape)
        off += size
    out["loss"] = flat[off]
    return out


def kernel(x, ffn1_norm, ffn1_w_gate, ffn1_w_up, ffn1_w_down, mix_norm, w_in, b_gate, na_q_norm, na_k_norm, na_rpb, sw_q_norm, sw_k_norm, sw_sink, t5_rel_table, w_branch_na, w_branch_sw, w_out, ffn2_norm, ffn2_w_gate, ffn2_w_up, ffn2_w_down, loss_target, m_ffn1_norm, m_ffn1_w_gate, m_ffn1_w_up, m_ffn1_w_down, m_mix_norm, m_w_in, m_b_gate, m_na_q_norm, m_na_k_norm, m_na_rpb, m_sw_q_norm, m_sw_k_norm, m_sw_sink, m_t5_rel_table, m_w_branch_na, m_w_branch_sw, m_w_out, m_ffn2_norm, m_ffn2_w_gate, m_ffn2_w_up, m_ffn2_w_down, v_ffn1_norm, v_ffn1_w_gate, v_ffn1_w_up, v_ffn1_w_down, v_mix_norm, v_w_in, v_b_gate, v_na_q_norm, v_na_k_norm, v_na_rpb, v_sw_q_norm, v_sw_k_norm, v_sw_sink, v_t5_rel_table, v_w_branch_na, v_w_branch_sw, v_w_out, v_ffn2_norm, v_ffn2_w_gate, v_ffn2_w_up, v_ffn2_w_down):
    args = locals()
    tr = lambda n, a: jnp.transpose(a, (0, 2, 1)) if n in TRANSPOSED else a
    w = {n: tr(n, args[n]) for n in PARAMS}
    m = {n: tr(n, args["m_" + n]) for n in PARAMS}
    v = {n: tr(n, args["v_" + n]) for n in PARAMS}
    cidx = lax.axis_index("c").astype(jnp.int32).reshape(1)
    sidx = (2 * lax.axis_index("x") + lax.axis_index("y")).astype(jnp.int32).reshape(1)

    small = [{n: w[n][l] for n in SMALL} for l in range(DEPTH)]
    shards = [[w[n][l].astype(BF16) for n in BIG] for l in range(DEPTH)]
    t5b = t5_bias(w["t5_rel_table"], name="t5_bias")

    g0 = gather_layer(shards[0], name="gather_layer")
    send_sems, recv_sems, thru, lands, token = gather_start(shards[1], g0[0], name="gather_start")
    h0, saved0 = layer_fwd(x[0], small[0], dict(zip(BIG, g0)), t5b, dep=token)
    thru, lands = gather_wait(send_sems, recv_sems, thru, lands, h0, name="gather_wait")
    g1 = gather_finish(thru, lands, name="gather_finish")
    gathered = [dict(zip(BIG, g0)), dict(zip(BIG, g1))]
    h1, saved1 = layer_fwd(h0, small[1], gathered[1], t5b)
    dy, loss_row = loss_head(h1, loss_target[0], name="loss_head")

    dy, big1, small1, dt5_1 = layer_bwd(dy, saved1, small[1], gathered[1], t5b)
    grads1 = [big1[n] for n in BIG]
    sums1 = [add_halves(g, b, cidx, name="add_halves") for g, b in zip(grads1, pair_exchange(grads1, name="pair_exchange"))]
    send_sems, recv_sems, sums1, lands, token = chip_exchange_start(sums1, name="chip_exchange_start")
    grad_x, big0, small0, dt5_0 = layer_bwd(dy, saved0, small[0], gathered[0], t5b, dep=token)
    grads0 = [big0[n] for n in BIG]
    sums1, from_chips1 = chip_exchange_wait(send_sems, recv_sems, sums1, lands, [grads0[0]], name="chip_exchange_wait")
    sums0 = [add_halves(g, b, cidx, name="add_halves") for g, b in zip(grads0, pair_exchange(grads0, name="pair_exchange"))]
    send_sems, recv_sems, sums0, lands, token = chip_exchange_start(sums0, name="chip_exchange_start")
    halves1 = [add_chips(s, b, sidx, name="add_chips", dep=token) for s, b in zip(sums1, from_chips1)]
    theirs1 = pair_send(halves1, name="pair_send")
    done1 = [adamw_layer(w[n], m[n], v[n], halves1[a], theirs1[a], cidx, 1, name="adamw_layer") for a, n in enumerate(BIG)]

    smalls = [small0, small1]
    dt5 = t5_table_grad(dt5_0, dt5_1, name="t5_table_grad").reshape(32, 8)
    local_small = {n: jnp.stack([smalls[l][n].reshape(w[n].shape[1:]) for l in range(DEPTH)]) for n in SMALL}
    local_small["t5_rel_table"] = dt5
    local_small["loss"] = loss_row[0, 0:1]
    total = allreduce_small(_pack_small(local_small), name="allreduce_small")
    small_grads = _unpack_small(total, w)
    pack = lambda d: _pack_small({**d, "loss": jnp.zeros((1,), F32)})[None]
    ds, ms, vs = adamw(pack(w), total[None], pack(m), pack(v), name="adamw_small")

    sums0, from_chips0 = chip_exchange_wait(send_sems, recv_sems, sums0, lands, [ds, done1[-1][0]],
                                            name="chip_exchange_wait")
    halves0 = [add_chips(s, b, sidx, name="add_chips") for s, b in zip(sums0, from_chips0)]
    theirs0 = pair_send(halves0, name="pair_send")
    grad, delta, new_m, new_v = {}, {}, {}, {}
    for a, n in enumerate(BIG):
        grad[n], delta[n], new_m[n], new_v[n] = adamw_layer(
            w[n], m[n], v[n], halves0[a], theirs0[a], cidx, 0, filled=done1[a], name="adamw_layer")
    for n in SMALL_ALL:
        grad[n] = small_grads[n]
    d_s, m_s, v_s = _unpack_small(ds[0], w), _unpack_small(ms[0], w), _unpack_small(vs[0], w)
    for n in SMALL_ALL:
        delta[n], new_m[n], new_v[n] = d_s[n], m_s[n], v_s[n]

    return (small_grads["loss"], grad_x[None], *[tr(n, grad[n]) for n in PARAMS], *[tr(n, delta[n]) for n in PARAMS],
            *[tr(n, new_m[n]) for n in PARAMS], *[tr(n, new_v[n]) for n in PARAMS])
```

```python
import functools
import math

import jax
import jax.numpy as jnp
import numpy as np
from jax import lax
from jax.experimental import pallas as pl
from jax.experimental.pallas import tpu as pltpu

F32 = jnp.float32
BF16 = jnp.bfloat16

SEQ = 2048
DM = 1024
DFF = 2816
DEPTH = 2
NSH = 4
FSH = DFF // NSH
GRID_W = 64
ROWS = SEQ // GRID_W
NA_HEADS = 8
HD = 64
NA_WR = 8
NA_WC = 16
NA_KEYS = NA_WR * GRID_W
SW_BLK = 128
SW_NB = SEQ // SW_BLK
SW_KEYS = 3 * SW_BLK
ATT_W = 2304
GATE_W = 2048
IN_W = ATT_W + GATE_W
EPS = 1e-6
NEG = -1e30
QK_SCALE = 1.0 / math.sqrt(HD)

ADAM_LR = 0.001
ADAM_B1 = 0.9
ADAM_B2 = 0.999
ADAM_EPS = 1e-08
ADAM_WD = 0.01
ADAM_STEP = 10

VMEM_LIMIT = 56 << 20
MESH = pl.DeviceIdType.MESH

NT = (((1,), (1,)), ((), ()))
TN = (((0,), (0,)), ((), ()))
NN = (((1,), (0,)), ((), ()))


def _dot(a, b, dims=NN):
    return lax.dot_general(a, b, dims, preferred_element_type=F32)


def _params(sem=None):
    return pltpu.CompilerParams(dimension_semantics=sem, vmem_limit_bytes=VMEM_LIMIT)


def _sds(shape, dtype):
    return jax.ShapeDtypeStruct(shape, dtype)


def mm(a, b, *, name, ta=False, tb=False, out_dtype=F32, add=None, scale=None, tm=512, tn=None, tk=None, exact=False,
       dep=None):
    m, kd = (a.shape[1], a.shape[0]) if ta else a.shape
    n = b.shape[0] if tb else b.shape[1]
    tm, tn, tk = min(tm, m), min(tn or n, n), min(tk or kd, kd)
    nk = kd // tk
    dims = (((0 if ta else 1,), (1 if tb else 0,)), ((), ()))

    def body(*refs):
        a_ref, b_ref = refs[:2]
        add_ref = refs[2] if add is not None else None
        o_ref, acc = refs[-2:]
        k = pl.program_id(2)

        @pl.when(k == 0)
        def _():
            acc[...] = jnp.zeros_like(acc)

        if exact:
            acc[...] += lax.dot_general(a_ref[...], b_ref[...], dims, precision=lax.Precision.HIGHEST,
                                        preferred_element_type=F32)
        else:
            acc[...] += lax.dot_general(a_ref[...].astype(BF16), b_ref[...].astype(BF16), dims,
                                        preferred_element_type=F32)

        @pl.when(k == nk - 1)
        def _():
            r = acc[...]
            if scale is not None:
                r = r * scale
            if add is not None:
                r = r + add_ref[...]
            o_ref[...] = r.astype(out_dtype)

    a_spec = pl.BlockSpec((tk, tm), lambda i, j, k: (k, i)) if ta else pl.BlockSpec((tm, tk), lambda i, j, k: (i, k))
    b_spec = pl.BlockSpec((tn, tk), lambda i, j, k: (j, k)) if tb else pl.BlockSpec((tk, tn), lambda i, j, k: (k, j))
    o_spec = pl.BlockSpec((tm, tn), lambda i, j, k: (i, j))
    ins, specs = [a, b], [a_spec, b_spec]
    if add is not None:
        ins.append(add)
        specs.append(o_spec)
    if dep is not None:
        ins.append(dep)
        specs.append(pl.BlockSpec(memory_space=pl.ANY))
    return pl.pallas_call(
        body, name=name, out_shape=_sds((m, n), out_dtype), grid=(m // tm, n // tn, nk), in_specs=specs,
        out_specs=o_spec, scratch_shapes=[pltpu.VMEM((tm, tn), F32)],
        compiler_params=_params(("parallel", "parallel", "arbitrary")))(*ins)


def _rms(x):
    return lax.rsqrt(jnp.mean(x * x, axis=-1, keepdims=True) + EPS)


def rms_fwd(x, gain, *, name, tm=512):
    def body(x_ref, g_ref, h_ref):
        x = x_ref[...]
        h_ref[...] = (x * _rms(x) * g_ref[...]).astype(BF16)

    return pl.pallas_call(
        body, name=name, out_shape=_sds(x.shape, BF16), grid=(x.shape[0] // tm,),
        in_specs=[pl.BlockSpec((tm, DM), lambda i: (i, 0)), pl.BlockSpec((1, DM), lambda i: (0, 0))],
        out_specs=pl.BlockSpec((tm, DM), lambda i: (i, 0)), compiler_params=_params(("parallel",)))(x, gain)


def _rms_bwd_math(dh, x, gain):
    r = _rms(x)
    xh = x * r
    dgain = jnp.sum(dh * xh, axis=0, keepdims=True)
    dxn = dh * gain
    dx = r * (dxn - xh * jnp.mean(dxn * xh, axis=-1, keepdims=True))
    return dx, dgain


def rms_bwd(dh, x, gain, dres, *, name, tm=512):
    def body(dh_ref, x_ref, g_ref, dres_ref, dx_ref, dg_ref):
        @pl.when(pl.program_id(0) == 0)
        def _():
            dg_ref[...] = jnp.zeros_like(dg_ref)

        dx, dg = _rms_bwd_math(dh_ref[...], x_ref[...], g_ref[...])
        dx_ref[...] = dres_ref[...] + dx
        dg_ref[...] += dg

    tile = pl.BlockSpec((tm, DM), lambda i: (i, 0))
    vec = pl.BlockSpec((1, DM), lambda i: (0, 0))
    return pl.pallas_call(
        body, name=name, out_shape=(_sds(x.shape, F32), _sds((1, DM), F32)), grid=(x.shape[0] // tm,),
        in_specs=[tile, tile, vec, tile], out_specs=(tile, vec), compiler_params=_params(("arbitrary",)))(dh, x, gain, dres)


def _with_dep(ins, specs, dep):
    if dep is None:
        return ins, specs
    return ins + [dep], specs + [pl.BlockSpec(memory_space=pl.ANY)]


def ffn_fwd(x, gain, wg, wu, wd, *, name, tm=512, dep=None):
    def body(x_ref, g_ref, wg_ref, wu_ref, wd_ref, *rest):
        y_ref, h_ref, gg_ref, uu_ref, acc = rest[-5:]
        f = pl.program_id(1)

        @pl.when(f == 0)
        def _():
            x = x_ref[...]
            h_ref[...] = (x * _rms(x) * g_ref[...]).astype(BF16)
            acc[...] = jnp.zeros_like(acc)

        h = h_ref[...]
        gg = _dot(h, wg_ref[...], NT)
        uu = _dot(h, wu_ref[...], NT)
        gg_ref[...] = gg.astype(BF16)
        uu_ref[...] = uu.astype(BF16)
        act = (gg * jax.nn.sigmoid(gg) * uu).astype(BF16)
        acc[...] += _dot(act, wd_ref[...])

        @pl.when(f == NSH - 1)
        def _():
            y_ref[...] = x_ref[...] + 0.5 * acc[...]

    s = x.shape[0]
    tile = pl.BlockSpec((tm, DM), lambda i, f: (i, 0))
    wsh = pl.BlockSpec((None, FSH, DM), lambda i, f: (f, 0, 0))
    hid = pl.BlockSpec((None, tm, FSH), lambda i, f: (f, i, 0))
    ins, specs = _with_dep([x, gain, wg, wu, wd], [tile, pl.BlockSpec((1, DM), lambda i, f: (0, 0)), wsh, wsh, wsh], dep)
    return pl.pallas_call(
        body, name=name,
        out_shape=(_sds((s, DM), F32), _sds((s, DM), BF16), _sds((NSH, s, FSH), BF16), _sds((NSH, s, FSH), BF16)),
        grid=(s // tm, NSH), in_specs=specs,
        out_specs=(tile, tile, hid, hid), scratch_shapes=[pltpu.VMEM((tm, DM), F32)],
        compiler_params=_params(("parallel", "arbitrary")))(*ins)


def ffn_bwd_tokens(dy, x, gain, gg, uu, wg, wu, wd, *, name, tm=512, dep=None):
    def body(dy_ref, x_ref, g_ref, gg_ref, uu_ref, wg_ref, wu_ref, wd_ref, *rest):
        dx_ref, dgain_ref, act_ref, dg_ref, du_ref, dh = rest[-6:]
        i, f = pl.program_id(0), pl.program_id(1)

        @pl.when(f == 0)
        def _():
            dh[...] = jnp.zeros_like(dh)

        @pl.when((i == 0) & (f == 0))
        def _():
            dgain_ref[...] = jnp.zeros_like(dgain_ref)

        dact = _dot((0.5 * dy_ref[...]).astype(BF16), wd_ref[...], NT)
        g = gg_ref[...].astype(F32)
        u = uu_ref[...].astype(F32)
        sg = jax.nn.sigmoid(g)
        silu = g * sg
        act_ref[...] = (silu * u).astype(BF16)
        dg = (dact * u * (sg * (1.0 + g * (1.0 - sg)))).astype(BF16)
        du = (dact * silu).astype(BF16)
        dg_ref[...] = dg
        du_ref[...] = du
        dh[...] += _dot(dg, wg_ref[...]) + _dot(du, wu_ref[...])

        @pl.when(f == NSH - 1)
        def _():
            dx, dgain = _rms_bwd_math(dh[...], x_ref[...], g_ref[...])
            dx_ref[...] = dy_ref[...] + dx
            dgain_ref[...] += dgain

    s = x.shape[0]
    tile = pl.BlockSpec((tm, DM), lambda i, f: (i, 0))
    vec = pl.BlockSpec((1, DM), lambda i, f: (0, 0))
    wsh = pl.BlockSpec((None, FSH, DM), lambda i, f: (f, 0, 0))
    hid = pl.BlockSpec((None, tm, FSH), lambda i, f: (f, i, 0))
    hshape = _sds((NSH, s, FSH), BF16)
    ins, specs = _with_dep([dy, x, gain, gg, uu, wg, wu, wd], [tile, tile, vec, hid, hid, wsh, wsh, wsh], dep)
    return pl.pallas_call(
        body, name=name, out_shape=(_sds((s, DM), F32), _sds((1, DM), F32), hshape, hshape, hshape),
        grid=(s // tm, NSH), in_specs=specs,
        out_specs=(tile, vec, hid, hid, hid), scratch_shapes=[pltpu.VMEM((tm, DM), F32)],
        compiler_params=_params(("arbitrary", "arbitrary")))(*ins)


def ffn_bwd_weights(h, dy, act, dg, du, *, name, tk=512):
    s = h.shape[0]
    nk = s // tk

    def body(h_ref, dy_ref, act_ref, dg_ref, du_ref, gwg_ref, gwu_ref, gwd_ref, ag, au, ad):
        k = pl.program_id(1)

        @pl.when(k == 0)
        def _():
            ag[...] = jnp.zeros_like(ag)
            au[...] = jnp.zeros_like(au)
            ad[...] = jnp.zeros_like(ad)

        h = h_ref[...]
        ag[...] += _dot(dg_ref[...], h, TN)
        au[...] += _dot(du_ref[...], h, TN)
        ad[...] += _dot(act_ref[...], dy_ref[...].astype(BF16), TN)

        @pl.when(k == nk - 1)
        def _():
            gwg_ref[...] = ag[...].astype(BF16)
            gwu_ref[...] = au[...].astype(BF16)
            gwd_ref[...] = (0.5 * ad[...]).astype(BF16)

    tile = pl.BlockSpec((tk, DM), lambda f, k: (k, 0))
    hid = pl.BlockSpec((None, tk, FSH), lambda f, k: (f, k, 0))
    wsh = pl.BlockSpec((None, FSH, DM), lambda f, k: (f, 0, 0))
    wshape = _sds((NSH, FSH, DM), BF16)
    return pl.pallas_call(
        body, name=name, out_shape=(wshape, wshape, wshape),
        grid=(NSH, nk), in_specs=[tile, tile, hid, hid, hid], out_specs=(wsh, wsh, wsh),
        scratch_shapes=[pltpu.VMEM((FSH, DM), F32), pltpu.VMEM((FSH, DM), F32), pltpu.VMEM((FSH, DM), F32)],
        compiler_params=_params(("parallel", "arbitrary")))(h, dy, act, dg, du)


def _group_mean(v, bd):
    return lax.dot_general(v, bd, NN, precision=lax.Precision.HIGHEST, preferred_element_type=F32)


def _block_diag(width):
    idx = np.arange(width) // HD
    return jnp.asarray((idx[:, None] == idx[None, :]).astype(np.float32) / HD)


def qknorm_fwd(z, gq_na, gk_na, gq_sw, gk_sw, *, name, tm=256):
    def body(zq_ref, zk_ref, zv_ref, zs_ref, zkv_ref, gqa_ref, gka_ref, gqs_ref, gks_ref, bd_ref, bd2_ref,
             qa_ref, ka_ref, va_ref, qs_ref, kv_ref):
        bd = bd_ref[...]

        def norm(x, g, bdm):
            return x * lax.rsqrt(_group_mean(x * x, bdm) + EPS) * g

        qa_ref[...] = (norm(zq_ref[...], gqa_ref[...], bd) * QK_SCALE).astype(BF16)
        ka_ref[...] = norm(zk_ref[...], gka_ref[...], bd).astype(BF16)
        va_ref[...] = zv_ref[...].astype(BF16)
        qs_ref[...] = (norm(zs_ref[...], gqs_ref[...], bd) * QK_SCALE).astype(BF16)
        kv = zkv_ref[...]
        kv_ref[:, 0:128] = norm(kv[:, 0:128], gks_ref[...], bd2_ref[...]).astype(BF16)
        kv_ref[:, 128:256] = kv[:, 128:256].astype(BF16)

    s = z.shape[0]
    col = lambda j: pl.BlockSpec((tm, 512), lambda i, j=j: (i, j))
    vec = lambda w: pl.BlockSpec((1, w), lambda i: (0, 0))
    o512 = pl.BlockSpec((tm, 512), lambda i: (i, 0))
    g512 = lambda g: jnp.tile(g.reshape(1, HD), (1, 8))
    return pl.pallas_call(
        body, name=name,
        out_shape=(_sds((s, 512), BF16),) * 4 + (_sds((s, 256), BF16),), grid=(s // tm,),
        in_specs=[col(0), col(1), col(2), col(3), pl.BlockSpec((tm, 256), lambda i: (i, 8)), vec(512), vec(512), vec(512),
                  vec(128), pl.BlockSpec((512, 512), lambda i: (0, 0)), pl.BlockSpec((128, 128), lambda i: (0, 0))],
        out_specs=(o512, o512, o512, o512, pl.BlockSpec((tm, 256), lambda i: (i, 0))),
        compiler_params=_params(("parallel",)))(
            z, z, z, z, z, g512(gq_na), g512(gk_na), g512(gq_sw), jnp.tile(gk_sw.reshape(1, HD), (1, 2)),
            _block_diag(512), _block_diag(128))


def qknorm_bwd(z, dqa, dka, dva, dqs, dkv, gq_na, gk_na, gq_sw, gk_sw, *, name, tm=256):
    def body(zq_ref, zk_ref, zs_ref, zkv_ref, dqa_ref, dka_ref, dva_ref, dqs_ref, dkv_ref, gqa_ref, gka_ref, gqs_ref,
             gks_ref, bd_ref, bd2_ref, dz_ref, dgqa_ref, dgka_ref, dgqs_ref, dgks_ref):
        @pl.when(pl.program_id(0) == 0)
        def _():
            dgqa_ref[...] = jnp.zeros_like(dgqa_ref)
            dgka_ref[...] = jnp.zeros_like(dgka_ref)
            dgqs_ref[...] = jnp.zeros_like(dgqs_ref)
            dgks_ref[...] = jnp.zeros_like(dgks_ref)

        bd = bd_ref[...]

        def bwd(x, dy, g, bdm, dg_ref):
            r = lax.rsqrt(_group_mean(x * x, bdm) + EPS)
            xh = x * r
            dg_ref[...] += jnp.sum(dy * xh, axis=0, keepdims=True)
            dxn = dy * g
            return r * (dxn - xh * _group_mean(dxn * xh, bdm))

        dz_ref[:, 0:512] = bwd(zq_ref[...], dqa_ref[...] * QK_SCALE, gqa_ref[...], bd, dgqa_ref).astype(BF16)
        dz_ref[:, 512:1024] = bwd(zk_ref[...], dka_ref[...], gka_ref[...], bd, dgka_ref).astype(BF16)
        dz_ref[:, 1024:1536] = dva_ref[...].astype(BF16)
        dz_ref[:, 1536:2048] = bwd(zs_ref[...], dqs_ref[...] * QK_SCALE, gqs_ref[...], bd, dgqs_ref).astype(BF16)
        dkv = dkv_ref[...]
        dz_ref[:, 2048:2176] = bwd(zkv_ref[:, 0:128], dkv[:, 0:128], gks_ref[...], bd2_ref[...], dgks_ref).astype(BF16)
        dz_ref[:, 2176:2304] = dkv[:, 128:256].astype(BF16)

    s = z.shape[0]
    col = lambda j: pl.BlockSpec((tm, 512), lambda i, j=j: (i, j))
    t512 = pl.BlockSpec((tm, 512), lambda i: (i, 0))
    t256 = pl.BlockSpec((tm, 256), lambda i: (i, 0))
    vec = lambda w: pl.BlockSpec((1, w), lambda i: (0, 0))
    g512 = lambda g: jnp.tile(g.reshape(1, HD), (1, 8))
    return pl.pallas_call(
        body, name=name,
        out_shape=(_sds((s, ATT_W), BF16), _sds((1, 512), F32), _sds((1, 512), F32), _sds((1, 512), F32), _sds((1, 128), F32)),
        grid=(s // tm,),
        in_specs=[col(0), col(1), col(3), pl.BlockSpec((tm, 256), lambda i: (i, 8)), t512, t512, t512, t512, t256,
                  vec(512), vec(512), vec(512), vec(128), pl.BlockSpec((512, 512), lambda i: (0, 0)),
                  pl.BlockSpec((128, 128), lambda i: (0, 0))],
        out_specs=(pl.BlockSpec((tm, ATT_W), lambda i: (i, 0)), vec(512), vec(512), vec(512), vec(128)),
        compiler_params=_params(("arbitrary",)))(
            z, z, z, z, dqa, dka, dva, dqs, dkv, g512(gq_na), g512(gk_na), g512(gq_sw),
            jnp.tile(gk_sw.reshape(1, HD), (1, 2)), _block_diag(512), _block_diag(128))


def _na_row_start(r):
    return jnp.clip(r - NA_WR // 2, 0, ROWS - NA_WR)


def na_bias_table(rpb, *, name):
    t = jnp.pad(rpb, ((0, 0), (0, 2), (0, HD - (2 * NA_WC - 1))))
    pairs = jnp.concatenate([t[:, :16], t[:, 1:17]], axis=-1).reshape(NA_HEADS, 16, 1, 128)

    def body(t_ref, o_ref):
        p = pl.program_id(0)
        q = lax.broadcasted_iota(jnp.int32, (GRID_W, 128), 0)
        kc = lax.broadcasted_iota(jnp.int32, (GRID_W, 128), 1) & (GRID_W - 1)
        cs = jnp.clip(q - NA_WC // 2, 0, GRID_W - NA_WC)
        ok = (kc >= cs) & (kc < cs + NA_WC)
        for h in range(NA_HEADS):
            for pr in range(NA_WR // 2):
                x = jnp.broadcast_to(t_ref[h, 2 * pr - p + NA_WR - 1], (GRID_W, 128))
                b = pltpu.roll(x, 128 - (NA_WC - 1), 1, stride=1, stride_axis=0)
                o_ref[h, :, 128 * pr:128 * pr + 128] = jnp.where(ok, b, NEG)

    return pl.pallas_call(
        body, name=name, out_shape=_sds((NA_WR, NA_HEADS, GRID_W, NA_KEYS), F32), grid=(NA_WR,),
        in_specs=[pl.BlockSpec((NA_HEADS, 16, 1, 128), lambda p: (0, 0, 0, 0))],
        out_specs=pl.BlockSpec((None, NA_HEADS, GRID_W, NA_KEYS), lambda p: (p, 0, 0, 0)),
        compiler_params=_params(("parallel",)))(pairs)


def _lane_halves():
    lane = lax.broadcasted_iota(jnp.int32, (1, 128), 1)
    return lane < HD


def na_fwd(q, k, v, bias, *, name):
    def body(q_ref, k_ref, v_ref, b_ref, o_ref, lse_ref):
        r = pl.program_id(0)
        off = pl.multiple_of(_na_row_start(r) * GRID_W, GRID_W)
        first = _lane_halves()
        for j in range(NA_HEADS // 2):
            lanes = slice(128 * j, 128 * j + 128)
            q2 = q_ref[:, lanes]
            k2 = k_ref[pl.ds(off, NA_KEYS), lanes]
            v2 = v_ref[pl.ds(off, NA_KEYS), lanes]
            zero = jnp.zeros_like(q2)
            o2 = jnp.zeros((GRID_W, 128), F32)
            for half in range(2):
                h = 2 * j + half
                sel = first if half == 0 else jnp.logical_not(first)
                s = _dot(jnp.where(sel, q2, zero), k2, NT)
                b = b_ref[h]
                s = jnp.where(b > 0.5 * NEG, s + b, NEG)
                m = jnp.max(s, axis=-1, keepdims=True)
                e = jnp.exp(s - m)
                l = jnp.sum(e, axis=-1, keepdims=True)
                p = (e / l).astype(BF16)
                o2 = o2 + _dot(p, jnp.where(sel, v2, jnp.zeros_like(v2)))
                lse_ref[:, h:h + 1] = m + jnp.log(l)
            o_ref[:, lanes] = o2.astype(BF16)

    s_tok = q.shape[0]
    full = pl.BlockSpec((s_tok, 512), lambda r: (0, 0))
    return pl.pallas_call(
        body, name=name, out_shape=(_sds((s_tok, 512), BF16), _sds((s_tok, NA_HEADS), F32)), grid=(ROWS,),
        in_specs=[pl.BlockSpec((GRID_W, 512), lambda r: (r, 0)), full, full,
                  pl.BlockSpec((None, NA_HEADS, GRID_W, NA_KEYS), lambda r: (r - _na_row_start(r), 0, 0, 0))],
        out_specs=(pl.BlockSpec((GRID_W, 512), lambda r: (r, 0)), pl.BlockSpec((GRID_W, NA_HEADS), lambda r: (r, 0))),
        compiler_params=_params(("parallel",)))(q, k, v, bias)


def na_bwd(q, k, v, o, do, lse, bias, *, name):
    def body(q_ref, k_ref, v_ref, o_ref, do_ref, lse_ref, b_ref, dq_ref, dk_ref, dv_ref, db_ref):
        r = pl.program_id(0)

        @pl.when(r == 0)
        def _():
            dk_ref[...] = jnp.zeros_like(dk_ref)
            dv_ref[...] = jnp.zeros_like(dv_ref)

        @pl.when((r <= NA_WR // 2) | (r > ROWS - NA_WR // 2))
        def _():
            db_ref[...] = jnp.zeros_like(db_ref)

        off = pl.multiple_of(_na_row_start(r) * GRID_W, GRID_W)
        first = _lane_halves()
        for j in range(NA_HEADS // 2):
            lanes = slice(128 * j, 128 * j + 128)
            q2 = q_ref[:, lanes]
            k2 = k_ref[pl.ds(off, NA_KEYS), lanes]
            v2 = v_ref[pl.ds(off, NA_KEYS), lanes]
            do2 = do_ref[:, lanes]
            prod = do2.astype(F32) * o_ref[:, lanes].astype(F32)
            dq2 = jnp.zeros((GRID_W, 128), F32)
            dk2 = jnp.zeros((NA_KEYS, 128), F32)
            dv2 = jnp.zeros((NA_KEYS, 128), F32)
            for half in range(2):
                h = 2 * j + half
                sel = first if half == 0 else jnp.logical_not(first)
                qh = jnp.where(sel, q2, jnp.zeros_like(q2))
                doh = jnp.where(sel, do2, jnp.zeros_like(do2))
                s = _dot(qh, k2, NT)
                b = b_ref[h]
                s = jnp.where(b > 0.5 * NEG, s + b, NEG)
                p = jnp.exp(s - lse_ref[:, h:h + 1])
                dp = _dot(doh, v2, NT)
                delta = jnp.sum(jnp.where(sel, prod, 0.0), axis=-1, keepdims=True)
                ds = p * (dp - delta)
                db_ref[h] += ds
                dsb = ds.astype(BF16)
                dq2 = dq2 + _dot(dsb, jnp.where(sel, k2, jnp.zeros_like(k2)))
                dk2 = dk2 + _dot(dsb, qh, TN)
                dv2 = dv2 + _dot(p.astype(BF16), doh, TN)
            dq_ref[:, lanes] = dq2
            dk_ref[pl.ds(off, NA_KEYS), lanes] += dk2
            dv_ref[pl.ds(off, NA_KEYS), lanes] += dv2

    s_tok = q.shape[0]
    full = pl.BlockSpec((s_tok, 512), lambda r: (0, 0))
    row = pl.BlockSpec((GRID_W, 512), lambda r: (r, 0))
    bias_spec = pl.BlockSpec((None, NA_HEADS, GRID_W, NA_KEYS), lambda r: (r - _na_row_start(r), 0, 0, 0))
    return pl.pallas_call(
        body, name=name,
        out_shape=(_sds((s_tok, 512), F32), _sds((s_tok, 512), F32), _sds((s_tok, 512), F32),
                   _sds((NA_WR, NA_HEADS, GRID_W, NA_KEYS), F32)),
        grid=(ROWS,),
        in_specs=[row, full, full, row, row, pl.BlockSpec((GRID_W, NA_HEADS), lambda r: (r, 0)), bias_spec],
        out_specs=(row, full, full, bias_spec), compiler_params=_params(("arbitrary",)))(q, k, v, o, do, lse, bias)


def t5_bucket_map():
    rel = np.arange(SW_KEYS)[None, :] - SW_BLK - np.arange(SW_BLK)[:, None]
    nb = 16
    max_exact = nb // 2
    n = np.abs(rel)
    large = max_exact + (np.log(np.maximum(n, 1) / max_exact) / np.log(128 / max_exact) * (nb - max_exact)).astype(np.int32)
    large = np.minimum(large, nb - 1)
    return ((rel > 0) * nb + np.where(n < max_exact, n, large)).astype(np.int32)


def t5_bias(table, *, name):
    rel = np.arange(-SW_BLK, SW_BLK + 1)
    nb, max_exact = 16, 8
    n = np.abs(rel)
    large = max_exact + (np.log(np.maximum(n, 1) / max_exact) / np.log(128 / max_exact) * (nb - max_exact)).astype(np.int32)
    bucket = ((rel > 0) * nb + np.where(n < max_exact, n, np.minimum(large, nb - 1))).astype(np.int32)
    u = jnp.pad(table[jnp.asarray(bucket)].T, ((0, 0), (0, SW_KEYS - bucket.shape[0]))).reshape(8, 1, SW_KEYS)

    def body(u_ref, o_ref):
        for h in range(8):
            x = jnp.broadcast_to(u_ref[h], (SW_BLK, SW_KEYS))
            o_ref[h] = pltpu.roll(x, 0, 1, stride=1, stride_axis=0)

    return pl.pallas_call(body, name=name, out_shape=_sds((8, SW_BLK, SW_KEYS), F32), compiler_params=_params())(u)


def _sw_valid(n):
    a = lax.broadcasted_iota(jnp.int32, (SW_BLK, SW_KEYS), 0)
    j = lax.broadcasted_iota(jnp.int32, (SW_BLK, SW_KEYS), 1)
    kpos = (n - 1) * SW_BLK + j
    return (jnp.abs(j - SW_BLK - a) <= SW_BLK) & (kpos >= 0) & (kpos < SEQ)


def _dup_group(x2, g, first):
    rolled = pltpu.roll(x2, HD, 1)
    return jnp.where(first, x2, rolled) if g == 0 else jnp.where(first, rolled, x2)


def sw_fwd(q, kv, t5, sink, *, name):
    def body(q_ref, kv_ref, t5_ref, sink_ref, o_ref, lse_ref):
        n = pl.program_id(0)
        off = pl.multiple_of(n * SW_BLK, SW_BLK)
        first = _lane_halves()
        valid = _sw_valid(n)
        k2 = kv_ref[pl.ds(off, SW_KEYS), 0:128]
        v2 = kv_ref[pl.ds(off, SW_KEYS), 128:256]
        for j in range(4):
            g = j // 2
            kk = _dup_group(k2, g, first)
            vv = _dup_group(v2, g, first)
            lanes = slice(128 * j, 128 * j + 128)
            q2 = q_ref[:, lanes]
            o2 = jnp.zeros((SW_BLK, 128), F32)
            for half in range(2):
                h = 2 * j + half
                sel = first if half == 0 else jnp.logical_not(first)
                s = _dot(jnp.where(sel, q2, jnp.zeros_like(q2)), kk, NT)
                s = jnp.where(valid, s + t5_ref[h], NEG)
                snk = sink_ref[:, h:h + 1]
                m = jnp.maximum(jnp.max(s, axis=-1, keepdims=True), snk)
                e = jnp.exp(s - m)
                den = jnp.sum(e, axis=-1, keepdims=True) + jnp.exp(snk - m)
                p = (e / den).astype(BF16)
                o2 = o2 + _dot(p, jnp.where(sel, vv, jnp.zeros_like(vv)))
                lse_ref[:, h:h + 1] = m + jnp.log(den)
            o_ref[:, lanes] = o2.astype(BF16)

    s_tok = q.shape[0]
    blk = pl.BlockSpec((SW_BLK, 512), lambda n: (n, 0))
    return pl.pallas_call(
        body, name=name, out_shape=(_sds((s_tok, 512), BF16), _sds((s_tok, 8), F32)), grid=(SW_NB,),
        in_specs=[blk, pl.BlockSpec(kv.shape, lambda n: (0, 0)), pl.BlockSpec((8, SW_BLK, SW_KEYS), lambda n: (0, 0, 0)),
                  pl.BlockSpec((1, 8), lambda n: (0, 0))],
        out_specs=(blk, pl.BlockSpec((SW_BLK, 8), lambda n: (n, 0))), compiler_params=_params(("parallel",)))(q, kv, t5, sink)


def sw_bwd(q, kv, o, do, lse, t5, sink, *, name):
    def body(q_ref, kv_ref, o_ref, do_ref, lse_ref, t5_ref, sink_ref, dq_ref, dkv_ref, dt5_ref, dsink_ref):
        n = pl.program_id(0)

        @pl.when(n == 0)
        def _():
            dkv_ref[...] = jnp.zeros_like(dkv_ref)
            dt5_ref[...] = jnp.zeros_like(dt5_ref)
            dsink_ref[...] = jnp.zeros_like(dsink_ref)

        off = pl.multiple_of(n * SW_BLK, SW_BLK)
        first = _lane_halves()
        valid = _sw_valid(n)
        k2 = kv_ref[pl.ds(off, SW_KEYS), 0:128]
        v2 = kv_ref[pl.ds(off, SW_KEYS), 128:256]
        dk_groups, dv_groups = [], []
        for g in range(2):
            kk = _dup_group(k2, g, first)
            vv = _dup_group(v2, g, first)
            dkk = jnp.zeros((SW_KEYS, 128), F32)
            dvv = jnp.zeros((SW_KEYS, 128), F32)
            for j in (2 * g, 2 * g + 1):
                lanes = slice(128 * j, 128 * j + 128)
                q2 = q_ref[:, lanes]
                do2 = do_ref[:, lanes]
                prod = do2.astype(F32) * o_ref[:, lanes].astype(F32)
                dq2 = jnp.zeros((SW_BLK, 128), F32)
                for half in range(2):
                    h = 2 * j + half
                    sel = first if half == 0 else jnp.logical_not(first)
                    qh = jnp.where(sel, q2, jnp.zeros_like(q2))
                    doh = jnp.where(sel, do2, jnp.zeros_like(do2))
                    s = _dot(qh, kk, NT)
                    s = jnp.where(valid, s + t5_ref[h], NEG)
                    lse = lse_ref[:, h:h + 1]
                    p = jnp.exp(s - lse)
                    dp = _dot(doh, vv, NT)
                    delta = jnp.sum(jnp.where(sel, prod, 0.0), axis=-1, keepdims=True)
                    ds = p * (dp - delta)
                    dt5_ref[h] += ds
                    dsink_ref[:, h:h + 1] += -jnp.sum(jnp.exp(sink_ref[:, h:h + 1] - lse) * delta, axis=0, keepdims=True)
                    dsb = ds.astype(BF16)
                    dq2 = dq2 + _dot(dsb, jnp.where(sel, kk, jnp.zeros_like(kk)))
                    dkk = dkk + _dot(dsb, qh, TN)
                    dvv = dvv + _dot(p.astype(BF16), doh, TN)
                dq_ref[:, lanes] = dq2
            dk_groups.append(dkk + pltpu.roll(dkk, HD, 1))
            dv_groups.append(dvv + pltpu.roll(dvv, HD, 1))
        dkv_ref[pl.ds(off, SW_KEYS), 0:128] += jnp.where(first, dk_groups[0], dk_groups[1])
        dkv_ref[pl.ds(off, SW_KEYS), 128:256] += jnp.where(first, dv_groups[0], dv_groups[1])

    s_tok = q.shape[0]
    blk = pl.BlockSpec((SW_BLK, 512), lambda n: (n, 0))
    kv_spec = pl.BlockSpec(kv.shape, lambda n: (0, 0))
    t5_spec = pl.BlockSpec((8, SW_BLK, SW_KEYS), lambda n: (0, 0, 0))
    vec = pl.BlockSpec((1, 8), lambda n: (0, 0))
    return pl.pallas_call(
        body, name=name,
        out_shape=(_sds((s_tok, 512), F32), _sds(kv.shape, F32), _sds((8, SW_BLK, SW_KEYS), F32), _sds((1, 8), F32)),
        grid=(SW_NB,), in_specs=[blk, kv_spec, blk, blk, pl.BlockSpec((SW_BLK, 8), lambda n: (n, 0)), t5_spec, vec],
        out_specs=(blk, kv_spec, t5_spec, vec), compiler_params=_params(("arbitrary",)))(q, kv, o, do, lse, t5, sink)


def gate_fwd(zg, bias, pa, ps, *, name, tm=512):
    def body(z0_ref, z1_ref, b0_ref, b1_ref, pa_ref, ps_ref, m_ref):
        g0 = jax.nn.sigmoid(z0_ref[...] + b0_ref[...])
        g1 = jax.nn.sigmoid(z1_ref[...] + b1_ref[...])
        m_ref[...] = (g0 * pa_ref[...] + g1 * ps_ref[...]).astype(BF16)

    s = zg.shape[0]
    half = lambda j: pl.BlockSpec((tm, DM), lambda i, j=j: (i, j))
    bvec = lambda j: pl.BlockSpec((1, DM), lambda i, j=j: (0, j))
    return pl.pallas_call(
        body, name=name, out_shape=_sds((s, DM), BF16), grid=(s // tm,),
        in_specs=[half(0), half(1), bvec(0), bvec(1), half(0), half(0)], out_specs=half(0),
        compiler_params=_params(("parallel",)))(zg, zg, bias, bias, pa, ps)


def gate_bwd(dm, zg, bias, pa, ps, *, name, tm=512):
    def body(dm_ref, z0_ref, z1_ref, b0_ref, b1_ref, pa_ref, ps_ref, dpa_ref, dps_ref, dz_ref, db_ref):
        @pl.when(pl.program_id(0) == 0)
        def _():
            db_ref[...] = jnp.zeros_like(db_ref)

        dm = dm_ref[...]
        g0 = jax.nn.sigmoid(z0_ref[...] + b0_ref[...])
        g1 = jax.nn.sigmoid(z1_ref[...] + b1_ref[...])
        dpa_ref[...] = (dm * g0).astype(BF16)
        dps_ref[...] = (dm * g1).astype(BF16)
        dz0 = dm * pa_ref[...] * g0 * (1.0 - g0)
        dz1 = dm * ps_ref[...] * g1 * (1.0 - g1)
        dz_ref[:, 0:DM] = dz0.astype(BF16)
        dz_ref[:, DM:2 * DM] = dz1.astype(BF16)
        db_ref[:, 0:DM] += jnp.sum(dz0, axis=0, keepdims=True)
        db_ref[:, DM:2 * DM] += jnp.sum(dz1, axis=0, keepdims=True)

    s = zg.shape[0]
    half = lambda j: pl.BlockSpec((tm, DM), lambda i, j=j: (i, j))
    bvec = lambda j: pl.BlockSpec((1, DM), lambda i, j=j: (0, j))
    return pl.pallas_call(
        body, name=name,
        out_shape=(_sds((s, DM), BF16), _sds((s, DM), BF16), _sds((s, GATE_W), BF16), _sds((1, GATE_W), F32)),
        grid=(s // tm,), in_specs=[half(0), half(0), half(1), bvec(0), bvec(1), half(0), half(0)],
        out_specs=(half(0), half(0), pl.BlockSpec((tm, GATE_W), lambda i: (i, 0)), pl.BlockSpec((1, GATE_W), lambda i: (0, 0))),
        compiler_params=_params(("arbitrary",)))(dm, zg, zg, bias, bias, pa, ps)


def loss_head(y, target, *, name, tm=512):
    def body(y_ref, t_ref, dy_ref, l_ref):
        @pl.when(pl.program_id(0) == 0)
        def _():
            l_ref[...] = jnp.zeros_like(l_ref)

        err = y_ref[...] - t_ref[...]
        dy_ref[...] = err * (1.0 / DM)
        l_ref[...] += 0.5 * jnp.sum(jnp.mean(err * err, axis=-1, keepdims=True), axis=0, keepdims=True)

    s = y.shape[0]
    tile = pl.BlockSpec((tm, DM), lambda i: (i, 0))
    return pl.pallas_call(
        body, name=name, out_shape=(_sds((s, DM), F32), _sds((1, 128), F32)), grid=(s // tm,), in_specs=[tile, tile],
        out_specs=(tile, pl.BlockSpec((1, 128), lambda i: (0, 0))), compiler_params=_params(("arbitrary",)))(y, target)


def adamw(w, g, m, v, *, name):
    def body(w_ref, g_ref, m_ref, v_ref, d_ref, nm_ref, nv_ref):
        g = g_ref[...]
        nm = ADAM_B1 * m_ref[...] + (1.0 - ADAM_B1) * g
        nv = ADAM_B2 * v_ref[...] + (1.0 - ADAM_B2) * jnp.square(g)
        m_hat = nm / (1.0 - ADAM_B1 ** ADAM_STEP)
        v_hat = nv / (1.0 - ADAM_B2 ** ADAM_STEP)
        d_ref[...] = -ADAM_LR * (m_hat / (jnp.sqrt(v_hat) + ADAM_EPS) + ADAM_WD * w_ref[...])
        nm_ref[...] = nm
        nv_ref[...] = nv

    b, k, n = w.shape
    tk = k // 4 if k % 32 == 0 else k
    spec = pl.BlockSpec((None, tk, n), lambda i, j: (i, j, 0))
    out = _sds(w.shape, F32)
    return pl.pallas_call(
        body, name=name, out_shape=(out, out, out), grid=(b, k // tk), in_specs=[spec] * 4, out_specs=(spec,) * 3,
        compiler_params=_params(("parallel", "parallel")))(w, g, m, v)


def adamw_layer(w, m, v, mine, theirs, cidx, layer, filled=None, *, name):
    _, k, n = w.shape
    nt = 2
    tk = k // 2 // nt

    def body(c_ref, w_ref, m_ref, v_ref, a_ref, b_ref, *rest):
        g_ref, d_ref, nm_ref, nv_ref = rest[-4:]
        g = jnp.where(pl.program_id(0) == c_ref[0], a_ref[...], b_ref[...])
        g_ref[...] = g
        nm = ADAM_B1 * m_ref[...] + (1.0 - ADAM_B1) * g
        nv = ADAM_B2 * v_ref[...] + (1.0 - ADAM_B2) * jnp.square(g)
        m_hat = nm / (1.0 - ADAM_B1 ** ADAM_STEP)
        v_hat = nv / (1.0 - ADAM_B2 ** ADAM_STEP)
        d_ref[...] = -ADAM_LR * (m_hat / (jnp.sqrt(v_hat) + ADAM_EPS) + ADAM_WD * w_ref[...])
        nm_ref[...] = nm
        nv_ref[...] = nv

    full = pl.BlockSpec((None, tk, n), lambda hf, t, c: (layer, hf * nt + t, 0))
    half_mine = pl.BlockSpec((tk, n), lambda hf, t, c: (jnp.where(hf == c[0], t, 0), 0))
    half_theirs = pl.BlockSpec((tk, n), lambda hf, t, c: (jnp.where(hf != c[0], t, 0), 0))
    out = _sds(w.shape, F32)
    ins, specs, aliases = [cidx, w, m, v, mine, theirs], [full, full, full, half_mine, half_theirs], {}
    if filled is not None:
        aliases = {len(ins) + i: i for i in range(4)}
        ins += list(filled)
        specs += [pl.BlockSpec(memory_space=pl.ANY)] * 4
    return pl.pallas_call(
        body, name=name, out_shape=(out, out, out, out),
        grid_spec=pltpu.PrefetchScalarGridSpec(
            num_scalar_prefetch=1, grid=(2, nt), in_specs=specs, out_specs=(full, full, full, full)),
        input_output_aliases=aliases,
        compiler_params=_params(("arbitrary", "arbitrary")))(*ins)


def t5_table_grad(dt5_a, dt5_b, *, name):
    def body(a_ref, b_ref, map_ref, o_ref):
        d = a_ref[...] + b_ref[...]
        bucket = map_ref[...]
        for b in range(32):
            hit = (bucket == b)[None]
            o_ref[b] = jnp.sum(jnp.sum(jnp.where(hit, d, 0.0), axis=2), axis=1, keepdims=True)

    return pl.pallas_call(
        body, name=name, out_shape=_sds((32, 8, 1), F32), compiler_params=_params())(
            dt5_a, dt5_b, jnp.asarray(t5_bucket_map()))


def rpb_grad(dbias, *, name):
    q = np.arange(GRID_W)[:, None]
    kc = np.arange(GRID_W)[None, :]
    diag = ((kc - q + NA_WC - 1)[..., None] == np.arange(128)).reshape(GRID_W * GRID_W, 128).astype(np.float32)
    p = np.arange(NA_WR)[:, None]
    kr = np.arange(NA_WR)[None, :]
    rows = ((kr - p + NA_WR - 1).reshape(-1)[None, :] == np.arange(16)[:, None]).astype(np.float32)
    d = dbias.reshape(NA_WR, NA_HEADS, GRID_W, NA_WR, GRID_W).transpose(0, 1, 3, 2, 4).reshape(512, GRID_W * GRID_W)
    e = mm(d, jnp.asarray(diag), name=name + "_cols", exact=True, tk=1024)
    e = e.reshape(NA_WR, NA_HEADS, NA_WR, 128).transpose(0, 2, 1, 3).reshape(NA_WR * NA_WR, NA_HEADS * 128)
    out = mm(jnp.asarray(rows), e, name=name + "_rows", exact=True)
    return out.reshape(16, NA_HEADS, 128)[:2 * NA_WR - 1, :, :2 * NA_WC - 1].transpose(1, 0, 2)


BIG = ("ffn1_w_gate", "ffn1_w_up", "ffn1_w_down", "w_in", "w_branch_na", "w_branch_sw", "w_out",
       "ffn2_w_gate", "ffn2_w_up", "ffn2_w_down")
SMALL = ("ffn1_norm", "mix_norm", "b_gate", "na_q_norm", "na_k_norm", "na_rpb", "sw_q_norm", "sw_k_norm", "sw_sink",
         "ffn2_norm")


def _cols_to_full(w4):
    return w4.transpose(1, 0, 2).reshape(w4.shape[1], NSH * w4.shape[2])


def _full_to_cols(w):
    return w.reshape(w.shape[0], NSH, w.shape[1] // NSH).transpose(1, 0, 2)


def _mixer_weights(g):
    w_in_t = g["w_in"].reshape(IN_W, DM)
    return dict(w_att_t=w_in_t[:ATT_W], w_gz_t=w_in_t[ATT_W:], wa=_cols_to_full(g["w_branch_na"]),
                ws=_cols_to_full(g["w_branch_sw"]), wo=g["w_out"].reshape(DM, DM))


GROUPS = {"ffn1": ("ffn1_w_gate", "ffn1_w_up", "ffn1_w_down"), "mix": ("w_in", "w_branch_na", "w_branch_sw", "w_out"),
          "ffn2": ("ffn2_w_gate", "ffn2_w_up", "ffn2_w_down")}


def layer_fwd(x, p, weights, t5b):
    row = lambda v: v.reshape(1, -1)
    g1 = weights("ffn1", x)
    y1, h1, gg1, uu1 = ffn_fwd(x, row(p["ffn1_norm"]), g1["ffn1_w_gate"], g1["ffn1_w_up"], g1["ffn1_w_down"], name="ffn_fwd")
    w = _mixer_weights(weights("mix", y1))
    hm = rms_fwd(y1, row(p["mix_norm"]), name="mix_norm_fwd")
    z = mm(hm, w["w_att_t"], tb=True, name="proj_att", tn=768)
    zg = mm(hm, w["w_gz_t"], tb=True, name="proj_gate", tn=1024)
    qa, ka, va, qs, kv = qknorm_fwd(z, p["na_q_norm"], p["na_k_norm"], p["sw_q_norm"], p["sw_k_norm"], name="qknorm_fwd")
    bias = na_bias_table(p["na_rpb"], name="na_bias_table")
    o_na, lse_na = na_fwd(qa, ka, va, bias, name="na_fwd")
    kvp = jnp.pad(kv, ((SW_BLK, SW_BLK), (0, 0)))
    sink = row(p["sw_sink"])
    o_sw, lse_sw = sw_fwd(qs, kvp, t5b, sink, name="sw_fwd")
    pa = mm(o_na, w["wa"], name="branch_na")
    ps = mm(o_sw, w["ws"], name="branch_sw")
    merged = gate_fwd(zg, row(p["b_gate"]), pa, ps, name="gate_fwd")
    y2 = mm(merged, w["wo"], add=y1, name="out_proj")
    g2 = weights("ffn2", y2)
    y3, h2, gg2, uu2 = ffn_fwd(y2, row(p["ffn2_norm"]), g2["ffn2_w_gate"], g2["ffn2_w_up"], g2["ffn2_w_down"], name="ffn_fwd")
    saved = dict(x=x, y1=y1, h1=h1, gg1=gg1, uu1=uu1, hm=hm, z=z, zg=zg, qa=qa, ka=ka, va=va, qs=qs, kvp=kvp, bias=bias,
                 o_na=o_na, lse_na=lse_na, o_sw=o_sw, lse_sw=lse_sw, pa=pa, ps=ps, merged=merged, y2=y2, h2=h2, gg2=gg2,
                 uu2=uu2, w=w, sink=sink, g1=g1, g2=g2)
    return y3, saved


def layer_bwd(dy3, sv, p, t5b, emit, dep=None):
    w, g1, g2 = sv["w"], sv["g1"], sv["g2"]
    row = lambda v: v.reshape(1, -1)
    fold = lambda v: v.reshape(-1, HD).sum(axis=0)
    small = {}
    dy2, small["ffn2_norm"], act, dg, du = ffn_bwd_tokens(
        dy3, sv["y2"], row(p["ffn2_norm"]), sv["gg2"], sv["uu2"], g2["ffn2_w_gate"], g2["ffn2_w_up"], g2["ffn2_w_down"],
        name="ffn_bwd_tokens", dep=dep)
    token = emit("ffn2", ffn_bwd_weights(sv["h2"], dy3, act, dg, du, name="ffn_bwd_weights"))
    dmerged = mm(dy2, w["wo"], tb=True, name="out_proj_dx", dep=token)
    gw_out = mm(sv["merged"], dy2, ta=True, out_dtype=BF16, tk=512, name="out_proj_dw").reshape(NSH, DM // NSH, DM)
    dpa, dps, dzg, small["b_gate"] = gate_bwd(dmerged, sv["zg"], row(p["b_gate"]), sv["pa"], sv["ps"], name="gate_bwd")
    gw_na = _full_to_cols(mm(sv["o_na"], dpa, ta=True, out_dtype=BF16, tk=512, name="branch_dw"))
    gw_sw = _full_to_cols(mm(sv["o_sw"], dps, ta=True, out_dtype=BF16, tk=512, name="branch_dw"))
    do_na = mm(dpa, w["wa"], tb=True, out_dtype=BF16, name="branch_dx")
    do_sw = mm(dps, w["ws"], tb=True, out_dtype=BF16, name="branch_dx")
    dqa, dka, dva, dbias = na_bwd(sv["qa"], sv["ka"], sv["va"], sv["o_na"], do_na, sv["lse_na"], sv["bias"], name="na_bwd")
    dqs, dkvp, dt5, dsink = sw_bwd(sv["qs"], sv["kvp"], sv["o_sw"], do_sw, sv["lse_sw"], t5b, sv["sink"], name="sw_bwd")
    dkv = dkvp[SW_BLK:SW_BLK + SEQ]
    dz, dgqa, dgka, dgqs, dgks = qknorm_bwd(sv["z"], dqa, dka, dva, dqs, dkv, p["na_q_norm"], p["na_k_norm"],
                                            p["sw_q_norm"], p["sw_k_norm"], name="qknorm_bwd")
    small["na_q_norm"], small["na_k_norm"], small["sw_q_norm"], small["sw_k_norm"] = fold(dgqa), fold(dgka), fold(dgqs), fold(dgks)
    small["na_rpb"] = rpb_grad(dbias, name="rpb_grad")
    small["sw_sink"] = dsink
    gw_att_t = mm(dz, sv["hm"], ta=True, out_dtype=BF16, tk=512, tm=768, name="proj_att_dw")
    gw_gz_t = mm(dzg, sv["hm"], ta=True, out_dtype=BF16, tk=512, name="proj_gate_dw")
    gw_in = jnp.concatenate([gw_att_t, gw_gz_t], axis=0).reshape(NSH, IN_W // NSH, DM)
    token = emit("mix", (gw_in, gw_na, gw_sw, gw_out))
    dh = mm(dz, w["w_att_t"], tk=768, name="proj_att_dx", dep=token)
    dh = mm(dzg, w["w_gz_t"], add=dh, name="proj_gate_dx")
    dy1, small["mix_norm"] = rms_bwd(dh, sv["y1"], row(p["mix_norm"]), dy2, name="mix_norm_bwd")
    dx, small["ffn1_norm"], act, dg, du = ffn_bwd_tokens(
        dy1, sv["x"], row(p["ffn1_norm"]), sv["gg1"], sv["uu1"], g1["ffn1_w_gate"], g1["ffn1_w_up"], g1["ffn1_w_down"],
        name="ffn_bwd_tokens")
    emit("ffn1", ffn_bwd_weights(sv["h1"], dy1, act, dg, du, name="ffn_bwd_weights"))
    return dx, small, dt5


ANY = pl.BlockSpec(memory_space=pl.ANY)


def _place():
    x, y, c = lax.axis_index("x"), lax.axis_index("y"), lax.axis_index("c")
    chips = [(1 - x, y), (x, 1 - y), (1 - x, 1 - y)]
    return x, y, c, chips


def _remote(src, dst, send_sem, recv_sem, to):
    return pltpu.make_async_remote_copy(src_ref=src, dst_ref=dst, send_sem=send_sem, recv_sem=recv_sem, device_id=to,
                                        device_id_type=MESH)


HBM = pl.BlockSpec(memory_space=pltpu.HBM)
SEM = pl.BlockSpec(memory_space=pltpu.SEMAPHORE)
ORDERED_EFFECT = pltpu.SideEffectType.DATAFLOW_SIDE_EFFECTING


def _in_hbm(v):
    return pltpu.with_memory_space_constraint(v, pltpu.HBM)


def _row_half(ref_shape_rows, c):
    half = ref_shape_rows // 2
    return pl.ds(c * half, half)


def _ici_gather_copies(w, land, send_sems, recv_sems):
    x, y, c, chips = _place()
    me = 2 * x + y
    copies = []
    for a in range(len(w)):
        rows = _row_half(w[a].shape[0], c)
        for k, chip in enumerate(chips):
            copies.append(_remote(w[a].at[rows], land[a].at[me, rows], send_sems.at[3 * a + k], recv_sems.at[3 * a + k],
                                  (*chip, c)))
    return copies


def _d2d_gather_copies(w, land, send_sems, recv_sems):
    x, y, c, chips = _place()
    me, sibling = 2 * x + y, (x, y, 1 - c)
    copies = []
    for a in range(len(w)):
        rows = _row_half(w[a].shape[0], c)
        for k, (cx, cy) in enumerate(chips):
            blk = land[a].at[2 * cx + cy, rows]
            copies.append(_remote(blk, blk, send_sems.at[4 * a + k], recv_sems.at[4 * a + k], sibling))
        copies.append(_remote(w[a], land[a].at[me], send_sems.at[4 * a + 3], recv_sems.at[4 * a + 3], sibling))
    return copies


def _d2d_gather_waits(w, land, send_sems, recv_sems):
    x, y, c, chips = _place()
    me, sibling = 2 * x + y, (x, y, 1 - c)
    waits = []
    for a in range(len(w)):
        rows = _row_half(w[a].shape[0], 1 - c)
        for k, (cx, cy) in enumerate(chips):
            blk = land[a].at[2 * cx + cy, rows]
            waits.append(_remote(blk, blk, send_sems.at[4 * a + k], recv_sems.at[4 * a + k], sibling))
        waits.append(_remote(w[a], land[a].at[me], send_sems.at[4 * a + 3], recv_sems.at[4 * a + 3], sibling))
    return waits


def gather_start(groups, *, name):
    sizes = [len(g) for g in groups]
    shards = [s for g in groups for s in g]
    n, ng = len(shards), len(groups)

    def body(*refs):
        w, land, sems = refs[:n], refs[n:2 * n], refs[2 * n:2 * n + 2 * ng]
        off = 0
        for gi, size in enumerate(sizes):
            for cp in _ici_gather_copies(w[off:off + size], land[off:off + size], sems[2 * gi], sems[2 * gi + 1]):
                cp.start()
            off += size

    lands = [lax.empty((NSH,) + s.shape, s.dtype) for s in shards]
    sem_shapes = tuple(pltpu.SemaphoreType.DMA((3 * size,)) for size in sizes for _ in range(2))
    res = pl.pallas_call(
        body, name=name,
        out_shape=sem_shapes + tuple(pltpu.HBM(s.shape, s.dtype) for s in shards) + tuple(pltpu.HBM(l.shape, l.dtype) for l in lands),
        in_specs=[HBM] * (2 * n), out_specs=(SEM,) * (2 * ng) + (HBM,) * (2 * n),
        input_output_aliases={i: 2 * ng + i for i in range(2 * n)},
        compiler_params=pltpu.CompilerParams(has_side_effects=ORDERED_EFFECT))(
            *[_in_hbm(s) for s in shards], *[_in_hbm(l) for l in lands])
    out, off = [], 0
    for gi, size in enumerate(sizes):
        out.append((res[2 * gi], res[2 * gi + 1], list(res[2 * ng + off:2 * ng + off + size]),
                    list(res[2 * ng + n + off:2 * ng + n + off + size])))
        off += size
    return out


def gather_wait(send_sems, recv_sems, shards, lands, after, *, name):
    n = len(shards)

    def body(*refs):
        w, land = refs[:n], refs[n:2 * n]
        send, recv = refs[2 * n:2 * n + 2]
        for cp in _ici_gather_copies(w, land, send, recv):
            cp.wait_send()
            cp.wait_recv()

    res = pl.pallas_call(
        body, name=name,
        out_shape=tuple(pltpu.HBM(s.shape, s.dtype) for s in shards) + tuple(pltpu.HBM(l.shape, l.dtype) for l in lands),
        in_specs=[HBM] * (2 * n) + [SEM, SEM, ANY], out_specs=(HBM,) * (2 * n),
        input_output_aliases={i: i for i in range(2 * n)},
        compiler_params=pltpu.CompilerParams(has_side_effects=ORDERED_EFFECT))(*shards, *lands, send_sems, recv_sems, after)
    return list(res[:n]), list(res[n:])


def gather_finish(shards, lands, *, name):
    n = len(shards)

    def body(*refs):
        w, land = refs[:n], refs[n:2 * n]
        send_sems, recv_sems = refs[3 * n:]
        d2d = _d2d_gather_copies(w, land, send_sems, recv_sems)
        for cp in d2d:
            cp.start()
        for cp in _d2d_gather_waits(w, land, send_sems, recv_sems):
            cp.wait_recv()
        for cp in d2d:
            cp.wait_send()

    return list(pl.pallas_call(
        body, name=name, out_shape=tuple(pltpu.HBM(l.shape, l.dtype) for l in lands),
        in_specs=[ANY] * (2 * n), out_specs=tuple([ANY] * n), input_output_aliases={n + i: i for i in range(n)},
        scratch_shapes=[pltpu.SemaphoreType.DMA((4 * n,)), pltpu.SemaphoreType.DMA((4 * n,))])(*shards, *lands))


def pair_exchange(grads, *, name):
    n = len(grads)

    def body(*refs):
        g, buf = refs[:n], refs[n:2 * n]
        send_sems, recv_sems = refs[2 * n:]
        x, y, c, _ = _place()
        copies = []
        for a in range(n):
            half = g[a].shape[1] // 2
            cp = _remote(g[a].at[:, pl.ds((1 - c) * half, half)], buf[a], send_sems.at[a], recv_sems.at[a], (x, y, 1 - c))
            cp.start()
            copies.append(cp)
        for cp in copies:
            cp.wait()

    return pl.pallas_call(
        body, name=name, out_shape=tuple(pltpu.HBM((NSH, g.shape[1] // 2, g.shape[2]), g.dtype) for g in grads),
        in_specs=[ANY] * n, out_specs=tuple([ANY] * n),
        scratch_shapes=[pltpu.SemaphoreType.DMA((n,)), pltpu.SemaphoreType.DMA((n,))])(*grads)


def _chip_exchange_copies(s, buf, send_sems, recv_sems):
    x, y, c, chips = _place()
    return [_remote(s[a].at[2 * cx + cy], buf[a].at[k], send_sems.at[3 * a + k], recv_sems.at[3 * a + k], (cx, cy, c))
            for a in range(len(s)) for k, (cx, cy) in enumerate(chips)]


def chip_exchange(sums, *, name):
    n = len(sums)

    def body(*refs):
        copies = _chip_exchange_copies(refs[:n], refs[n:2 * n], *refs[2 * n:])
        for cp in copies:
            cp.start()
        for cp in copies:
            cp.wait()

    return pl.pallas_call(
        body, name=name, out_shape=tuple(pltpu.HBM((3,) + s.shape[1:], s.dtype) for s in sums),
        in_specs=[ANY] * n, out_specs=tuple([ANY] * n),
        scratch_shapes=[pltpu.SemaphoreType.DMA((3 * n,)), pltpu.SemaphoreType.DMA((3 * n,))])(*sums)


def chip_exchange_start(sums, *, name):
    n = len(sums)

    def body(*refs):
        for cp in _chip_exchange_copies(refs[:n], refs[n:2 * n], refs[2 * n], refs[2 * n + 1]):
            cp.start()
        refs[-1][...] = jnp.zeros_like(refs[-1])

    lands = [lax.empty((3,) + s.shape[1:], s.dtype) for s in sums]
    res = pl.pallas_call(
        body, name=name,
        out_shape=(pltpu.SemaphoreType.DMA((3 * n,)), pltpu.SemaphoreType.DMA((3 * n,)))
        + tuple(pltpu.HBM(s.shape, s.dtype) for s in sums) + tuple(pltpu.HBM(l.shape, l.dtype) for l in lands)
        + (_sds((8, 128), F32),),
        in_specs=[HBM] * (2 * n), out_specs=(SEM, SEM) + (HBM,) * (2 * n) + (pl.BlockSpec(memory_space=pltpu.VMEM),),
        input_output_aliases={i: 2 + i for i in range(2 * n)},
        compiler_params=pltpu.CompilerParams(has_side_effects=ORDERED_EFFECT))(
            *[_in_hbm(s) for s in sums], *[_in_hbm(l) for l in lands])
    return res[0], res[1], list(res[2:2 + n]), list(res[2 + n:2 + 2 * n]), res[-1]


def chip_exchange_wait(send_sems, recv_sems, sums, lands, after, *, name):
    n = len(sums)

    def body(*refs):
        for cp in _chip_exchange_copies(refs[:n], refs[n:2 * n], refs[2 * n], refs[2 * n + 1]):
            cp.wait_send()
            cp.wait_recv()

    res = pl.pallas_call(
        body, name=name,
        out_shape=tuple(pltpu.HBM(s.shape, s.dtype) for s in sums) + tuple(pltpu.HBM(l.shape, l.dtype) for l in lands),
        in_specs=[HBM] * (2 * n) + [SEM, SEM] + [ANY] * len(after), out_specs=(HBM,) * (2 * n),
        input_output_aliases={i: i for i in range(2 * n)},
        compiler_params=pltpu.CompilerParams(has_side_effects=ORDERED_EFFECT))(*sums, *lands, send_sems, recv_sems, *after)
    return list(res[:n]), list(res[n:])


def pair_send(halves, *, name):
    n = len(halves)

    def body(*refs):
        h, got = refs[:n], refs[n:2 * n]
        send_sems, recv_sems = refs[2 * n:]
        x, y, c, _ = _place()
        copies = []
        for i in range(n):
            cp = _remote(h[i], got[i], send_sems.at[i], recv_sems.at[i], (x, y, 1 - c))
            cp.start()
            copies.append(cp)
        for cp in copies:
            cp.wait()

    return list(pl.pallas_call(
        body, name=name, out_shape=tuple(pltpu.HBM(v.shape, v.dtype) for v in halves),
        in_specs=[ANY] * n, out_specs=tuple([ANY] * n),
        scratch_shapes=[pltpu.SemaphoreType.DMA((n,)), pltpu.SemaphoreType.DMA((n,))])(*halves))


def allreduce_small(v, *, name):
    rows = v.shape[0]

    def body(v_ref, o_ref, gath, send_sems, recv_sems):
        x, y, c, _ = _place()
        me = 4 * x + 2 * y + c
        gath[me] = v_ref[...]
        copies = []
        for k in range(1, 8):
            fx, fy, fc = (k >> 2) & 1, (k >> 1) & 1, k & 1
            peer = (jnp.where(fx, 1 - x, x), jnp.where(fy, 1 - y, y), jnp.where(fc, 1 - c, c))
            cp = _remote(v_ref, gath.at[me], send_sems.at[k - 1], recv_sems.at[k - 1], peer)
            cp.start()
            copies.append(cp)
        for cp in copies:
            cp.wait()
        acc = gath[0]
        for d in range(1, 8):
            acc = acc + gath[d]
        o_ref[...] = acc

    return pl.pallas_call(
        body, name=name, out_shape=_sds(v.shape, F32),
        in_specs=[pl.BlockSpec(memory_space=pltpu.VMEM)], out_specs=pl.BlockSpec(memory_space=pltpu.VMEM),
        scratch_shapes=[pltpu.VMEM((8, rows, 128), F32), pltpu.SemaphoreType.DMA((7,)), pltpu.SemaphoreType.DMA((7,))])(v)


def add_halves(g, buf, cidx, *, name):
    _, k, n = g.shape

    def body(c_ref, g_ref, b_ref, o_ref):
        o_ref[...] = (g_ref[...].astype(F32) + b_ref[...].astype(F32)).astype(BF16)

    blk = pl.BlockSpec((None, k // 2, n), lambda s, c: (s, 0, 0))
    return pl.pallas_call(
        body, name=name, out_shape=_sds(buf.shape, BF16),
        grid_spec=pltpu.PrefetchScalarGridSpec(
            num_scalar_prefetch=1, grid=(NSH,),
            in_specs=[pl.BlockSpec((None, k // 2, n), lambda s, c: (s, c[0], 0)), blk], out_specs=blk),
        compiler_params=_params(("parallel",)))(cidx, g, buf)


def add_chips(sums, buf, sidx, *, name, dep=None):
    _, kh, n = sums.shape

    def body(s_ref, mine_ref, b_ref, *rest):
        o_ref = rest[-1]
        o_ref[...] = (mine_ref[...].astype(F32) + b_ref[0].astype(F32)) + (b_ref[1].astype(F32) + b_ref[2].astype(F32))

    ins, specs = _with_dep(
        [sums, buf], [pl.BlockSpec((None, kh, n), lambda i, s: (s[0], 0, 0)), pl.BlockSpec((3, kh, n), lambda i, s: (0, 0, 0))],
        dep)
    return pl.pallas_call(
        body, name=name, out_shape=_sds((kh, n), F32),
        grid_spec=pltpu.PrefetchScalarGridSpec(
            num_scalar_prefetch=1, grid=(1,), in_specs=specs, out_specs=pl.BlockSpec((kh, n), lambda i, s: (0, 0))),
        compiler_params=_params(("arbitrary",)))(sidx, *ins)


PARAMS = ("ffn1_norm", "ffn1_w_gate", "ffn1_w_up", "ffn1_w_down", "mix_norm", "w_in", "b_gate", "na_q_norm", "na_k_norm",
          "na_rpb", "sw_q_norm", "sw_k_norm", "sw_sink", "t5_rel_table", "w_branch_na", "w_branch_sw", "w_out", "ffn2_norm",
          "ffn2_w_gate", "ffn2_w_up", "ffn2_w_down")
SMALL_ALL = tuple(n for n in PARAMS if n not in BIG)
TRANSPOSED = ("ffn1_w_gate", "ffn1_w_up", "w_in", "ffn2_w_gate", "ffn2_w_up")
SMALL_ROWS = 152


def _pack_small(vals):
    flat = jnp.concatenate([vals[n].reshape(-1).astype(F32) for n in SMALL_ALL] + [vals["loss"].reshape(-1)])
    return jnp.pad(flat, (0, SMALL_ROWS * 128 - flat.shape[0])).reshape(SMALL_ROWS, 128)


def _unpack_small(packed, like):
    flat, out, off = packed.reshape(-1), {}, 0
    for n in SMALL_ALL:
        size = math.prod(like[n].shape)
        out[n] = flat[off:off + size].reshape(like[n].shape)
        off += size
    out["loss"] = flat[off]
    return out


def kernel(x, ffn1_norm, ffn1_w_gate, ffn1_w_up, ffn1_w_down, mix_norm, w_in, b_gate, na_q_norm, na_k_norm, na_rpb, sw_q_norm, sw_k_norm, sw_sink, t5_rel_table, w_branch_na, w_branch_sw, w_out, ffn2_norm, ffn2_w_gate, ffn2_w_up, ffn2_w_down, loss_target, m_ffn1_norm, m_ffn1_w_gate, m_ffn1_w_up, m_ffn1_w_down, m_mix_norm, m_w_in, m_b_gate, m_na_q_norm, m_na_k_norm, m_na_rpb, m_sw_q_norm, m_sw_k_norm, m_sw_sink, m_t5_rel_table, m_w_branch_na, m_w_branch_sw, m_w_out, m_ffn2_norm, m_ffn2_w_gate, m_ffn2_w_up, m_ffn2_w_down, v_ffn1_norm, v_ffn1_w_gate, v_ffn1_w_up, v_ffn1_w_down, v_mix_norm, v_w_in, v_b_gate, v_na_q_norm, v_na_k_norm, v_na_rpb, v_sw_q_norm, v_sw_k_norm, v_sw_sink, v_t5_rel_table, v_w_branch_na, v_w_branch_sw, v_w_out, v_ffn2_norm, v_ffn2_w_gate, v_ffn2_w_up, v_ffn2_w_down):
    args = locals()
    tr = lambda n, a: jnp.transpose(a, (0, 2, 1)) if n in TRANSPOSED else a
    w = {n: tr(n, args[n]) for n in PARAMS}
    m = {n: tr(n, args["m_" + n]) for n in PARAMS}
    v = {n: tr(n, args["v_" + n]) for n in PARAMS}
    cidx = lax.axis_index("c").astype(jnp.int32).reshape(1)
    sidx = (2 * lax.axis_index("x") + lax.axis_index("y")).astype(jnp.int32).reshape(1)

    small = [{n: w[n][l] for n in SMALL} for l in range(DEPTH)]
    t5b = t5_bias(w["t5_rel_table"], name="t5_bias")
    order = ("ffn1", "mix", "ffn2")

    keys = [(l, g) for l in range(DEPTH) for g in order]
    in_flight = dict(zip(keys, gather_start([[w[n][l].astype(BF16) for n in GROUPS[g]] for l, g in keys], name="gather_start")))

    def weights_of(l):
        def get(group, after):
            send_sems, recv_sems, thru, lands = in_flight[(l, group)]
            thru, lands = gather_wait(send_sems, recv_sems, thru, lands, after, name="gather_wait")
            return dict(zip(GROUPS[group], gather_finish(thru, lands, name="gather_finish")))
        return get

    h0, saved0 = layer_fwd(x[0], small[0], weights_of(0), t5b)
    h1, saved1 = layer_fwd(h0, small[1], weights_of(1), t5b)
    dy, loss_row = loss_head(h1, loss_target[0], name="loss_head")

    crossing, tokens = {}, []

    def reduce_of(l):
        def emit(group, grads):
            grads = list(grads)
            sums = [add_halves(g, b, cidx, name="add_halves") for g, b in zip(grads, pair_exchange(grads, name="pair_exchange"))]
            send_sems, recv_sems, sums, lands, token = chip_exchange_start(sums, name="chip_exchange_start")
            crossing[(l, group)] = (send_sems, recv_sems, sums, lands)
            tokens.append(token)
            return token
        return emit

    def arrived(l, after):
        halves = {}
        for group in order:
            send_sems, recv_sems, sums, lands = crossing[(l, group)]
            sums, got = chip_exchange_wait(send_sems, recv_sems, sums, lands, after, name="chip_exchange_wait")
            for n, s, b in zip(GROUPS[group], sums, got):
                halves[n] = add_chips(s, b, sidx, name="add_chips")
        return [halves[n] for n in BIG]

    dy, small1, dt5_1 = layer_bwd(dy, saved1, small[1], t5b, reduce_of(1))
    grad_x, small0, dt5_0 = layer_bwd(dy, saved0, small[0], t5b, reduce_of(0), dep=tokens[-1])
    halves1 = arrived(1, [tokens[-1]])
    theirs1 = pair_send(halves1, name="pair_send")
    done1 = [adamw_layer(w[n], m[n], v[n], halves1[a], theirs1[a], cidx, 1, name="adamw_layer") for a, n in enumerate(BIG)]

    smalls = [small0, small1]
    dt5 = t5_table_grad(dt5_0, dt5_1, name="t5_table_grad").reshape(32, 8)
    local_small = {n: jnp.stack([smalls[l][n].reshape(w[n].shape[1:]) for l in range(DEPTH)]) for n in SMALL}
    local_small["t5_rel_table"] = dt5
    local_small["loss"] = loss_row[0, 0:1]
    total = allreduce_small(_pack_small(local_small), name="allreduce_small")
    small_grads = _unpack_small(total, w)
    pack = lambda d: _pack_small({**d, "loss": jnp.zeros((1,), F32)})[None]
    ds, ms, vs = adamw(pack(w), total[None], pack(m), pack(v), name="adamw_small")

    halves0 = arrived(0, [ds, done1[-1][0]])
    theirs0 = pair_send(halves0, name="pair_send")
    grad, delta, new_m, new_v = {}, {}, {}, {}
    for a, n in enumerate(BIG):
        grad[n], delta[n], new_m[n], new_v[n] = adamw_layer(
            w[n], m[n], v[n], halves0[a], theirs0[a], cidx, 0, filled=done1[a], name="adamw_layer")
    for n in SMALL_ALL:
        grad[n] = small_grads[n]
    d_s, m_s, v_s = _unpack_small(ds[0], w), _unpack_small(ms[0], w), _unpack_small(vs[0], w)
    for n in SMALL_ALL:
        delta[n], new_m[n], new_v[n] = d_s[n], m_s[n], v_s[n]

    return (small_grads["loss"], grad_x[None], *[tr(n, grad[n]) for n in PARAMS], *[tr(n, delta[n]) for n in PARAMS],
            *[tr(n, new_m[n]) for n in PARAMS], *[tr(n, new_v[n]) for n in PARAMS])
```

```python
import functools
import math

import jax
import jax.numpy as jnp
import numpy as np
from jax import lax
from jax.experimental import pallas as pl
from jax.experimental.pallas import tpu as pltpu

F32 = jnp.float32
BF16 = jnp.bfloat16

SEQ = 2048
DM = 1024
DFF = 2816
DEPTH = 2
NSH = 4
FSH = DFF // NSH
GRID_W = 64
ROWS = SEQ // GRID_W
NA_HEADS = 8
HD = 64
NA_WR = 8
NA_WC = 16
NA_KEYS = NA_WR * GRID_W
SW_BLK = 128
SW_NB = SEQ // SW_BLK
SW_KEYS = 3 * SW_BLK
ATT_W = 2304
GATE_W = 2048
IN_W = ATT_W + GATE_W
EPS = 1e-6
NEG = -1e30
QK_SCALE = 1.0 / math.sqrt(HD)

ADAM_LR = 0.001
ADAM_B1 = 0.9
ADAM_B2 = 0.999
ADAM_EPS = 1e-08
ADAM_WD = 0.01
ADAM_STEP = 10

VMEM_LIMIT = 56 << 20
MESH = pl.DeviceIdType.MESH

NT = (((1,), (1,)), ((), ()))
TN = (((0,), (0,)), ((), ()))
NN = (((1,), (0,)), ((), ()))


def _dot(a, b, dims=NN):
    return lax.dot_general(a, b, dims, preferred_element_type=F32)


def _params(sem=None):
    return pltpu.CompilerParams(dimension_semantics=sem, vmem_limit_bytes=VMEM_LIMIT)


def _sds(shape, dtype):
    return jax.ShapeDtypeStruct(shape, dtype)


def mm(a, b, *, name, ta=False, tb=False, out_dtype=F32, add=None, scale=None, tm=512, tn=None, tk=None, exact=False,
       dep=None):
    m, kd = (a.shape[1], a.shape[0]) if ta else a.shape
    n = b.shape[0] if tb else b.shape[1]
    tm, tn, tk = min(tm, m), min(tn or n, n), min(tk or kd, kd)
    nk = kd // tk
    dims = (((0 if ta else 1,), (1 if tb else 0,)), ((), ()))

    def body(*refs):
        a_ref, b_ref = refs[:2]
        add_ref = refs[2] if add is not None else None
        o_ref, acc = refs[-2:]
        k = pl.program_id(2)

        @pl.when(k == 0)
        def _():
            acc[...] = jnp.zeros_like(acc)

        if exact:
            acc[...] += lax.dot_general(a_ref[...], b_ref[...], dims, precision=lax.Precision.HIGHEST,
                                        preferred_element_type=F32)
        else:
            acc[...] += lax.dot_general(a_ref[...].astype(BF16), b_ref[...].astype(BF16), dims,
                                        preferred_element_type=F32)

        @pl.when(k == nk - 1)
        def _():
            r = acc[...]
            if scale is not None:
                r = r * scale
            if add is not None:
                r = r + add_ref[...]
            o_ref[...] = r.astype(out_dtype)

    a_spec = pl.BlockSpec((tk, tm), lambda i, j, k: (k, i)) if ta else pl.BlockSpec((tm, tk), lambda i, j, k: (i, k))
    b_spec = pl.BlockSpec((tn, tk), lambda i, j, k: (j, k)) if tb else pl.BlockSpec((tk, tn), lambda i, j, k: (k, j))
    o_spec = pl.BlockSpec((tm, tn), lambda i, j, k: (i, j))
    ins, specs = [a, b], [a_spec, b_spec]
    if add is not None:
        ins.append(add)
        specs.append(o_spec)
    if dep is not None:
        ins.append(dep)
        specs.append(pl.BlockSpec(memory_space=pl.ANY))
    return pl.pallas_call(
        body, name=name, out_shape=_sds((m, n), out_dtype), grid=(m // tm, n // tn, nk), in_specs=specs,
        out_specs=o_spec, scratch_shapes=[pltpu.VMEM((tm, tn), F32)],
        compiler_params=_params(("parallel", "parallel", "arbitrary")))(*ins)


def _rms(x):
    return lax.rsqrt(jnp.mean(x * x, axis=-1, keepdims=True) + EPS)


def rms_fwd(x, gain, *, name, tm=512):
    def body(x_ref, g_ref, h_ref):
        x = x_ref[...]
        h_ref[...] = (x * _rms(x) * g_ref[...]).astype(BF16)

    return pl.pallas_call(
        body, name=name, out_shape=_sds(x.shape, BF16), grid=(x.shape[0] // tm,),
        in_specs=[pl.BlockSpec((tm, DM), lambda i: (i, 0)), pl.BlockSpec((1, DM), lambda i: (0, 0))],
        out_specs=pl.BlockSpec((tm, DM), lambda i: (i, 0)), compiler_params=_params(("parallel",)))(x, gain)


def _rms_bwd_math(dh, x, gain):
    r = _rms(x)
    xh = x * r
    dgain = jnp.sum(dh * xh, axis=0, keepdims=True)
    dxn = dh * gain
    dx = r * (dxn - xh * jnp.mean(dxn * xh, axis=-1, keepdims=True))
    return dx, dgain


def rms_bwd(dh, x, gain, dres, *, name, tm=512):
    def body(dh_ref, x_ref, g_ref, dres_ref, dx_ref, dg_ref):
        @pl.when(pl.program_id(0) == 0)
        def _():
            dg_ref[...] = jnp.zeros_like(dg_ref)

        dx, dg = _rms_bwd_math(dh_ref[...], x_ref[...], g_ref[...])
        dx_ref[...] = dres_ref[...] + dx
        dg_ref[...] += dg

    tile = pl.BlockSpec((tm, DM), lambda i: (i, 0))
    vec = pl.BlockSpec((1, DM), lambda i: (0, 0))
    return pl.pallas_call(
        body, name=name, out_shape=(_sds(x.shape, F32), _sds((1, DM), F32)), grid=(x.shape[0] // tm,),
        in_specs=[tile, tile, vec, tile], out_specs=(tile, vec), compiler_params=_params(("arbitrary",)))(dh, x, gain, dres)


def _with_dep(ins, specs, dep):
    if dep is None:
        return ins, specs
    return ins + [dep], specs + [pl.BlockSpec(memory_space=pl.ANY)]


def ffn_fwd(x, gain, wg, wu, wd, *, name, tm=512, dep=None):
    def body(x_ref, g_ref, wg_ref, wu_ref, wd_ref, *rest):
        y_ref, h_ref, gg_ref, uu_ref, acc = rest[-5:]
        f = pl.program_id(1)

        @pl.when(f == 0)
        def _():
            x = x_ref[...]
            h_ref[...] = (x * _rms(x) * g_ref[...]).astype(BF16)
            acc[...] = jnp.zeros_like(acc)

        h = h_ref[...]
        gg = _dot(h, wg_ref[...], NT)
        uu = _dot(h, wu_ref[...], NT)
        gg_ref[...] = gg.astype(BF16)
        uu_ref[...] = uu.astype(BF16)
        act = (gg * jax.nn.sigmoid(gg) * uu).astype(BF16)
        acc[...] += _dot(act, wd_ref[...])

        @pl.when(f == NSH - 1)
        def _():
            y_ref[...] = x_ref[...] + 0.5 * acc[...]

    s = x.shape[0]
    tile = pl.BlockSpec((tm, DM), lambda i, f: (i, 0))
    wsh = pl.BlockSpec((None, FSH, DM), lambda i, f: (f, 0, 0))
    hid = pl.BlockSpec((None, tm, FSH), lambda i, f: (f, i, 0))
    ins, specs = _with_dep([x, gain, wg, wu, wd], [tile, pl.BlockSpec((1, DM), lambda i, f: (0, 0)), wsh, wsh, wsh], dep)
    return pl.pallas_call(
        body, name=name,
        out_shape=(_sds((s, DM), F32), _sds((s, DM), BF16), _sds((NSH, s, FSH), BF16), _sds((NSH, s, FSH), BF16)),
        grid=(s // tm, NSH), in_specs=specs,
        out_specs=(tile, tile, hid, hid), scratch_shapes=[pltpu.VMEM((tm, DM), F32)],
        compiler_params=_params(("parallel", "arbitrary")))(*ins)


def ffn_bwd_tokens(dy, x, gain, gg, uu, wg, wu, wd, *, name, tm=512, dep=None):
    def body(dy_ref, x_ref, g_ref, gg_ref, uu_ref, wg_ref, wu_ref, wd_ref, *rest):
        dx_ref, dgain_ref, act_ref, dg_ref, du_ref, dh = rest[-6:]
        i, f = pl.program_id(0), pl.program_id(1)

        @pl.when(f == 0)
        def _():
            dh[...] = jnp.zeros_like(dh)

        @pl.when((i == 0) & (f == 0))
        def _():
            dgain_ref[...] = jnp.zeros_like(dgain_ref)

        dact = _dot((0.5 * dy_ref[...]).astype(BF16), wd_ref[...], NT)
        g = gg_ref[...].astype(F32)
        u = uu_ref[...].astype(F32)
        sg = jax.nn.sigmoid(g)
        silu = g * sg
        act_ref[...] = (silu * u).astype(BF16)
        dg = (dact * u * (sg * (1.0 + g * (1.0 - sg)))).astype(BF16)
        du = (dact * silu).astype(BF16)
        dg_ref[...] = dg
        du_ref[...] = du
        dh[...] += _dot(dg, wg_ref[...]) + _dot(du, wu_ref[...])

        @pl.when(f == NSH - 1)
        def _():
            dx, dgain = _rms_bwd_math(dh[...], x_ref[...], g_ref[...])
            dx_ref[...] = dy_ref[...] + dx
            dgain_ref[...] += dgain

    s = x.shape[0]
    tile = pl.BlockSpec((tm, DM), lambda i, f: (i, 0))
    vec = pl.BlockSpec((1, DM), lambda i, f: (0, 0))
    wsh = pl.BlockSpec((None, FSH, DM), lambda i, f: (f, 0, 0))
    hid = pl.BlockSpec((None, tm, FSH), lambda i, f: (f, i, 0))
    hshape = _sds((NSH, s, FSH), BF16)
    ins, specs = _with_dep([dy, x, gain, gg, uu, wg, wu, wd], [tile, tile, vec, hid, hid, wsh, wsh, wsh], dep)
    return pl.pallas_call(
        body, name=name, out_shape=(_sds((s, DM), F32), _sds((1, DM), F32), hshape, hshape, hshape),
        grid=(s // tm, NSH), in_specs=specs,
        out_specs=(tile, vec, hid, hid, hid), scratch_shapes=[pltpu.VMEM((tm, DM), F32)],
        compiler_params=_params(("arbitrary", "arbitrary")))(*ins)


def ffn_bwd_weights(h, dy, act, dg, du, *, name, tk=1024):
    s = h.shape[0]
    nk = s // tk

    def body(h_ref, dy_ref, act_ref, dg_ref, du_ref, gwg_ref, gwu_ref, gwd_ref, ag, au, ad):
        k = pl.program_id(1)

        @pl.when(k == 0)
        def _():
            ag[...] = jnp.zeros_like(ag)
            au[...] = jnp.zeros_like(au)
            ad[...] = jnp.zeros_like(ad)

        h = h_ref[...]
        ag[...] += _dot(dg_ref[...], h, TN)
        au[...] += _dot(du_ref[...], h, TN)
        ad[...] += _dot(act_ref[...], dy_ref[...].astype(BF16), TN)

        @pl.when(k == nk - 1)
        def _():
            gwg_ref[...] = ag[...].astype(BF16)
            gwu_ref[...] = au[...].astype(BF16)
            gwd_ref[...] = (0.5 * ad[...]).astype(BF16)

    tile = pl.BlockSpec((tk, DM), lambda f, k: (k, 0))
    hid = pl.BlockSpec((None, tk, FSH), lambda f, k: (f, k, 0))
    wsh = pl.BlockSpec((None, FSH, DM), lambda f, k: (f, 0, 0))
    wshape = _sds((NSH, FSH, DM), BF16)
    return pl.pallas_call(
        body, name=name, out_shape=(wshape, wshape, wshape),
        grid=(NSH, nk), in_specs=[tile, tile, hid, hid, hid], out_specs=(wsh, wsh, wsh),
        scratch_shapes=[pltpu.VMEM((FSH, DM), F32), pltpu.VMEM((FSH, DM), F32), pltpu.VMEM((FSH, DM), F32)],
        compiler_params=_params(("parallel", "arbitrary")))(h, dy, act, dg, du)


def _group_mean(v, bd):
    hi = v.astype(BF16)
    lo = (v - hi.astype(F32)).astype(BF16)
    return _dot(hi, bd) + _dot(lo, bd)


def _block_diag(width):
    idx = np.arange(width) // HD
    return jnp.asarray((idx[:, None] == idx[None, :]).astype(np.float32) / HD, dtype=BF16)


def qknorm_fwd(z, gq_na, gk_na, gq_sw, gk_sw, *, name, tm=256):
    def body(zq_ref, zk_ref, zv_ref, zs_ref, zkv_ref, gqa_ref, gka_ref, gqs_ref, gks_ref, bd_ref, bd2_ref,
             qa_ref, ka_ref, va_ref, qs_ref, kv_ref):
        bd = bd_ref[...]

        def norm(x, g, bdm):
            return x * lax.rsqrt(_group_mean(x * x, bdm) + EPS) * g

        qa_ref[...] = (norm(zq_ref[...], gqa_ref[...], bd) * QK_SCALE).astype(BF16)
        ka_ref[...] = norm(zk_ref[...], gka_ref[...], bd).astype(BF16)
        va_ref[...] = zv_ref[...].astype(BF16)
        qs_ref[...] = (norm(zs_ref[...], gqs_ref[...], bd) * QK_SCALE).astype(BF16)
        kv = zkv_ref[...]
        kv_ref[:, 0:128] = norm(kv[:, 0:128], gks_ref[...], bd2_ref[...]).astype(BF16)
        kv_ref[:, 128:256] = kv[:, 128:256].astype(BF16)

    s = z.shape[0]
    col = lambda j: pl.BlockSpec((tm, 512), lambda i, j=j: (i, j))
    vec = lambda w: pl.BlockSpec((1, w), lambda i: (0, 0))
    o512 = pl.BlockSpec((tm, 512), lambda i: (i, 0))
    g512 = lambda g: jnp.tile(g.reshape(1, HD), (1, 8))
    return pl.pallas_call(
        body, name=name,
        out_shape=(_sds((s, 512), BF16),) * 4 + (_sds((s, 256), BF16),), grid=(s // tm,),
        in_specs=[col(0), col(1), col(2), col(3), pl.BlockSpec((tm, 256), lambda i: (i, 8)), vec(512), vec(512), vec(512),
                  vec(128), pl.BlockSpec((512, 512), lambda i: (0, 0)), pl.BlockSpec((128, 128), lambda i: (0, 0))],
        out_specs=(o512, o512, o512, o512, pl.BlockSpec((tm, 256), lambda i: (i, 0))),
        compiler_params=_params(("parallel",)))(
            z, z, z, z, z, g512(gq_na), g512(gk_na), g512(gq_sw), jnp.tile(gk_sw.reshape(1, HD), (1, 2)),
            _block_diag(512), _block_diag(128))


def qknorm_bwd(z, dqa, dka, dva, dqs, dkv, gq_na, gk_na, gq_sw, gk_sw, *, name, tm=256):
    def body(zq_ref, zk_ref, zs_ref, zkv_ref, dqa_ref, dka_ref, dva_ref, dqs_ref, dkv_ref, gqa_ref, gka_ref, gqs_ref,
             gks_ref, bd_ref, bd2_ref, dz_ref, dgqa_ref, dgka_ref, dgqs_ref, dgks_ref):
        @pl.when(pl.program_id(0) == 0)
        def _():
            dgqa_ref[...] = jnp.zeros_like(dgqa_ref)
            dgka_ref[...] = jnp.zeros_like(dgka_ref)
            dgqs_ref[...] = jnp.zeros_like(dgqs_ref)
            dgks_ref[...] = jnp.zeros_like(dgks_ref)

        bd = bd_ref[...]

        def bwd(x, dy, g, bdm, dg_ref):
            r = lax.rsqrt(_group_mean(x * x, bdm) + EPS)
            xh = x * r
            dg_ref[...] += jnp.sum(dy * xh, axis=0, keepdims=True)
            dxn = dy * g
            return r * (dxn - xh * _group_mean(dxn * xh, bdm))

        dz_ref[:, 0:512] = bwd(zq_ref[...], dqa_ref[...] * QK_SCALE, gqa_ref[...], bd, dgqa_ref).astype(BF16)
        dz_ref[:, 512:1024] = bwd(zk_ref[...], dka_ref[...], gka_ref[...], bd, dgka_ref).astype(BF16)
        dz_ref[:, 1024:1536] = dva_ref[...].astype(BF16)
        dz_ref[:, 1536:2048] = bwd(zs_ref[...], dqs_ref[...] * QK_SCALE, gqs_ref[...], bd, dgqs_ref).astype(BF16)
        dkv = dkv_ref[...]
        dz_ref[:, 2048:2176] = bwd(zkv_ref[:, 0:128], dkv[:, 0:128], gks_ref[...], bd2_ref[...], dgks_ref).astype(BF16)
        dz_ref[:, 2176:2304] = dkv[:, 128:256].astype(BF16)

    s = z.shape[0]
    col = lambda j: pl.BlockSpec((tm, 512), lambda i, j=j: (i, j))
    t512 = pl.BlockSpec((tm, 512), lambda i: (i, 0))
    t256 = pl.BlockSpec((tm, 256), lambda i: (i, 0))
    vec = lambda w: pl.BlockSpec((1, w), lambda i: (0, 0))
    g512 = lambda g: jnp.tile(g.reshape(1, HD), (1, 8))
    return pl.pallas_call(
        body, name=name,
        out_shape=(_sds((s, ATT_W), BF16), _sds((1, 512), F32), _sds((1, 512), F32), _sds((1, 512), F32), _sds((1, 128), F32)),
        grid=(s // tm,),
        in_specs=[col(0), col(1), col(3), pl.BlockSpec((tm, 256), lambda i: (i, 8)), t512, t512, t512, t512, t256,
                  vec(512), vec(512), vec(512), vec(128), pl.BlockSpec((512, 512), lambda i: (0, 0)),
                  pl.BlockSpec((128, 128), lambda i: (0, 0))],
        out_specs=(pl.BlockSpec((tm, ATT_W), lambda i: (i, 0)), vec(512), vec(512), vec(512), vec(128)),
        compiler_params=_params(("arbitrary",)))(
            z, z, z, z, dqa, dka, dva, dqs, dkv, g512(gq_na), g512(gk_na), g512(gq_sw),
            jnp.tile(gk_sw.reshape(1, HD), (1, 2)), _block_diag(512), _block_diag(128))


def _na_row_start(r):
    return jnp.clip(r - NA_WR // 2, 0, ROWS - NA_WR)


def na_bias_table(rpb, *, name):
    t = jnp.pad(rpb, ((0, 0), (0, 2), (0, HD - (2 * NA_WC - 1))))
    pairs = jnp.concatenate([t[:, :16], t[:, 1:17]], axis=-1).reshape(NA_HEADS, 16, 1, 128)

    def body(t_ref, o_ref):
        p = pl.program_id(0)
        q = lax.broadcasted_iota(jnp.int32, (GRID_W, 128), 0)
        kc = lax.broadcasted_iota(jnp.int32, (GRID_W, 128), 1) & (GRID_W - 1)
        cs = jnp.clip(q - NA_WC // 2, 0, GRID_W - NA_WC)
        ok = (kc >= cs) & (kc < cs + NA_WC)
        for h in range(NA_HEADS):
            for pr in range(NA_WR // 2):
                x = jnp.broadcast_to(t_ref[h, 2 * pr - p + NA_WR - 1], (GRID_W, 128))
                b = pltpu.roll(x, 128 - (NA_WC - 1), 1, stride=1, stride_axis=0)
                o_ref[h, :, 128 * pr:128 * pr + 128] = jnp.where(ok, b, NEG)

    return pl.pallas_call(
        body, name=name, out_shape=_sds((NA_WR, NA_HEADS, GRID_W, NA_KEYS), F32), grid=(NA_WR,),
        in_specs=[pl.BlockSpec((NA_HEADS, 16, 1, 128), lambda p: (0, 0, 0, 0))],
        out_specs=pl.BlockSpec((None, NA_HEADS, GRID_W, NA_KEYS), lambda p: (p, 0, 0, 0)),
        compiler_params=_params(("parallel",)))(pairs)


def _lane_halves():
    lane = lax.broadcasted_iota(jnp.int32, (1, 128), 1)
    return lane < HD


def na_fwd(q, k, v, bias, *, name):
    def body(q_ref, k_ref, v_ref, b_ref, o_ref, lse_ref):
        r = pl.program_id(0)
        off = pl.multiple_of(_na_row_start(r) * GRID_W, GRID_W)
        first = _lane_halves()
        sels = [first, jnp.logical_not(first)]
        lanes = [slice(128 * j, 128 * j + 128) for j in range(NA_HEADS // 2)]
        q2s = [q_ref[:, l] for l in lanes]
        k2s = [k_ref[pl.ds(off, NA_KEYS), l] for l in lanes]
        v2s = [v_ref[pl.ds(off, NA_KEYS), l] for l in lanes]
        scores = []
        for h in range(NA_HEADS):
            j, half = divmod(h, 2)
            scores.append(_dot(jnp.where(sels[half], q2s[j], jnp.zeros_like(q2s[j])), k2s[j], NT))
        probs, lses = [], []
        for h in range(NA_HEADS):
            b = b_ref[h]
            s = jnp.where(b > 0.5 * NEG, scores[h] + b, NEG)
            m = jnp.max(s, axis=-1, keepdims=True)
            e = jnp.exp(s - m)
            l = jnp.sum(e, axis=-1, keepdims=True)
            probs.append((e / l).astype(BF16))
            lses.append(m + jnp.log(l))
        for j in range(NA_HEADS // 2):
            zero = jnp.zeros_like(v2s[j])
            o2 = (_dot(probs[2 * j], jnp.where(sels[0], v2s[j], zero))
                  + _dot(probs[2 * j + 1], jnp.where(sels[1], v2s[j], zero)))
            o_ref[:, lanes[j]] = o2.astype(BF16)
        lse_ref[...] = jnp.concatenate(lses, axis=1)

    s_tok = q.shape[0]
    full = pl.BlockSpec((s_tok, 512), lambda r: (0, 0))
    return pl.pallas_call(
        body, name=name, out_shape=(_sds((s_tok, 512), BF16), _sds((s_tok, NA_HEADS), F32)), grid=(ROWS,),
        in_specs=[pl.BlockSpec((GRID_W, 512), lambda r: (r, 0)), full, full,
                  pl.BlockSpec((None, NA_HEADS, GRID_W, NA_KEYS), lambda r: (r - _na_row_start(r), 0, 0, 0))],
        out_specs=(pl.BlockSpec((GRID_W, 512), lambda r: (r, 0)), pl.BlockSpec((GRID_W, NA_HEADS), lambda r: (r, 0))),
        compiler_params=_params(("parallel",)))(q, k, v, bias)


def na_bwd(q, k, v, o, do, lse, bias, *, name):
    def body(q_ref, k_ref, v_ref, o_ref, do_ref, lse_ref, b_ref, dq_ref, dk_ref, dv_ref, db_ref):
        r = pl.program_id(0)

        @pl.when(r == 0)
        def _():
            dk_ref[...] = jnp.zeros_like(dk_ref)
            dv_ref[...] = jnp.zeros_like(dv_ref)

        @pl.when((r <= NA_WR // 2) | (r > ROWS - NA_WR // 2))
        def _():
            db_ref[...] = jnp.zeros_like(db_ref)

        off = pl.multiple_of(_na_row_start(r) * GRID_W, GRID_W)
        first = _lane_halves()
        sels = [first, jnp.logical_not(first)]
        lanes = [slice(128 * j, 128 * j + 128) for j in range(NA_HEADS // 2)]
        q2s = [q_ref[:, l] for l in lanes]
        k2s = [k_ref[pl.ds(off, NA_KEYS), l] for l in lanes]
        v2s = [v_ref[pl.ds(off, NA_KEYS), l] for l in lanes]
        do2s = [do_ref[:, l] for l in lanes]
        prods = [do2s[j].astype(F32) * o_ref[:, lanes[j]].astype(F32) for j in range(NA_HEADS // 2)]
        lse = lse_ref[...]
        qhs, dohs, scores, dps = [], [], [], []
        for h in range(NA_HEADS):
            j, half = divmod(h, 2)
            qhs.append(jnp.where(sels[half], q2s[j], jnp.zeros_like(q2s[j])))
            dohs.append(jnp.where(sels[half], do2s[j], jnp.zeros_like(do2s[j])))
            scores.append(_dot(qhs[h], k2s[j], NT))
            dps.append(_dot(dohs[h], v2s[j], NT))
        pbs, dsbs = [], []
        for h in range(NA_HEADS):
            j, half = divmod(h, 2)
            b = b_ref[h]
            s = jnp.where(b > 0.5 * NEG, scores[h] + b, NEG)
            p = jnp.exp(s - lse[:, h:h + 1])
            delta = jnp.sum(jnp.where(sels[half], prods[j], 0.0), axis=-1, keepdims=True)
            ds = p * (dps[h] - delta)
            db_ref[h] += ds
            pbs.append(p.astype(BF16))
            dsbs.append(ds.astype(BF16))
        for j in range(NA_HEADS // 2):
            a, b = 2 * j, 2 * j + 1
            zero = jnp.zeros_like(k2s[j])
            dq_ref[:, lanes[j]] = (_dot(dsbs[a], jnp.where(sels[0], k2s[j], zero))
                                   + _dot(dsbs[b], jnp.where(sels[1], k2s[j], zero)))
            dk_ref[pl.ds(off, NA_KEYS), lanes[j]] += _dot(dsbs[a], qhs[a], TN) + _dot(dsbs[b], qhs[b], TN)
            dv_ref[pl.ds(off, NA_KEYS), lanes[j]] += _dot(pbs[a], dohs[a], TN) + _dot(pbs[b], dohs[b], TN)

    s_tok = q.shape[0]
    full = pl.BlockSpec((s_tok, 512), lambda r: (0, 0))
    row = pl.BlockSpec((GRID_W, 512), lambda r: (r, 0))
    bias_spec = pl.BlockSpec((None, NA_HEADS, GRID_W, NA_KEYS), lambda r: (r - _na_row_start(r), 0, 0, 0))
    return pl.pallas_call(
        body, name=name,
        out_shape=(_sds((s_tok, 512), F32), _sds((s_tok, 512), F32), _sds((s_tok, 512), F32),
                   _sds((NA_WR, NA_HEADS, GRID_W, NA_KEYS), F32)),
        grid=(ROWS,),
        in_specs=[row, full, full, row, row, pl.BlockSpec((GRID_W, NA_HEADS), lambda r: (r, 0)), bias_spec],
        out_specs=(row, full, full, bias_spec), compiler_params=_params(("arbitrary",)))(q, k, v, o, do, lse, bias)


def t5_bucket_map():
    rel = np.arange(SW_KEYS)[None, :] - SW_BLK - np.arange(SW_BLK)[:, None]
    nb = 16
    max_exact = nb // 2
    n = np.abs(rel)
    large = max_exact + (np.log(np.maximum(n, 1) / max_exact) / np.log(128 / max_exact) * (nb - max_exact)).astype(np.int32)
    large = np.minimum(large, nb - 1)
    return ((rel > 0) * nb + np.where(n < max_exact, n, large)).astype(np.int32)


def t5_bias(table, *, name):
    rel = np.arange(-SW_BLK, SW_BLK + 1)
    nb, max_exact = 16, 8
    n = np.abs(rel)
    large = max_exact + (np.log(np.maximum(n, 1) / max_exact) / np.log(128 / max_exact) * (nb - max_exact)).astype(np.int32)
    bucket = ((rel > 0) * nb + np.where(n < max_exact, n, np.minimum(large, nb - 1))).astype(np.int32)
    u = jnp.pad(table[jnp.asarray(bucket)].T, ((0, 0), (0, SW_KEYS - bucket.shape[0]))).reshape(8, 1, SW_KEYS)

    def body(u_ref, o_ref):
        for h in range(8):
            x = jnp.broadcast_to(u_ref[h], (SW_BLK, SW_KEYS))
            o_ref[h] = pltpu.roll(x, 0, 1, stride=1, stride_axis=0)

    return pl.pallas_call(body, name=name, out_shape=_sds((8, SW_BLK, SW_KEYS), F32), compiler_params=_params())(u)


def _sw_valid(n):
    a = lax.broadcasted_iota(jnp.int32, (SW_BLK, SW_KEYS), 0)
    j = lax.broadcasted_iota(jnp.int32, (SW_BLK, SW_KEYS), 1)
    kpos = (n - 1) * SW_BLK + j
    return (jnp.abs(j - SW_BLK - a) <= SW_BLK) & (kpos >= 0) & (kpos < SEQ)


def _dup_group(x2, g, first):
    rolled = pltpu.roll(x2, HD, 1)
    return jnp.where(first, x2, rolled) if g == 0 else jnp.where(first, rolled, x2)


def sw_fwd(q, kv, t5, sink, *, name):
    def body(q_ref, kv_ref, t5_ref, sink_ref, o_ref, lse_ref):
        n = pl.program_id(0)
        off = pl.multiple_of(n * SW_BLK, SW_BLK)
        first = _lane_halves()
        valid = _sw_valid(n)
        k2 = kv_ref[pl.ds(off, SW_KEYS), 0:128]
        v2 = kv_ref[pl.ds(off, SW_KEYS), 128:256]
        for j in range(4):
            g = j // 2
            kk = _dup_group(k2, g, first)
            vv = _dup_group(v2, g, first)
            lanes = slice(128 * j, 128 * j + 128)
            q2 = q_ref[:, lanes]
            o2 = jnp.zeros((SW_BLK, 128), F32)
            for half in range(2):
                h = 2 * j + half
                sel = first if half == 0 else jnp.logical_not(first)
                s = _dot(jnp.where(sel, q2, jnp.zeros_like(q2)), kk, NT)
                s = jnp.where(valid, s + t5_ref[h], NEG)
                snk = sink_ref[:, h:h + 1]
                m = jnp.maximum(jnp.max(s, axis=-1, keepdims=True), snk)
                e = jnp.exp(s - m)
                den = jnp.sum(e, axis=-1, keepdims=True) + jnp.exp(snk - m)
                p = (e / den).astype(BF16)
                o2 = o2 + _dot(p, jnp.where(sel, vv, jnp.zeros_like(vv)))
                lse_ref[:, h:h + 1] = m + jnp.log(den)
            o_ref[:, lanes] = o2.astype(BF16)

    s_tok = q.shape[0]
    blk = pl.BlockSpec((SW_BLK, 512), lambda n: (n, 0))
    return pl.pallas_call(
        body, name=name, out_shape=(_sds((s_tok, 512), BF16), _sds((s_tok, 8), F32)), grid=(SW_NB,),
        in_specs=[blk, pl.BlockSpec(kv.shape, lambda n: (0, 0)), pl.BlockSpec((8, SW_BLK, SW_KEYS), lambda n: (0, 0, 0)),
                  pl.BlockSpec((1, 8), lambda n: (0, 0))],
        out_specs=(blk, pl.BlockSpec((SW_BLK, 8), lambda n: (n, 0))), compiler_params=_params(("parallel",)))(q, kv, t5, sink)


def sw_bwd(q, kv, o, do, lse, t5, sink, *, name):
    def body(q_ref, kv_ref, o_ref, do_ref, lse_ref, t5_ref, sink_ref, dq_ref, dkv_ref, dt5_ref, dsink_ref):
        n = pl.program_id(0)

        @pl.when(n == 0)
        def _():
            dkv_ref[...] = jnp.zeros_like(dkv_ref)
            dt5_ref[...] = jnp.zeros_like(dt5_ref)
            dsink_ref[...] = jnp.zeros_like(dsink_ref)

        off = pl.multiple_of(n * SW_BLK, SW_BLK)
        first = _lane_halves()
        sels = [first, jnp.logical_not(first)]
        valid = _sw_valid(n)
        k2 = kv_ref[pl.ds(off, SW_KEYS), 0:128]
        v2 = kv_ref[pl.ds(off, SW_KEYS), 128:256]
        kk = [_dup_group(k2, g, first) for g in range(2)]
        vv = [_dup_group(v2, g, first) for g in range(2)]
        lanes = [slice(128 * j, 128 * j + 128) for j in range(4)]
        q2s = [q_ref[:, l] for l in lanes]
        do2s = [do_ref[:, l] for l in lanes]
        prods = [do2s[j].astype(F32) * o_ref[:, lanes[j]].astype(F32) for j in range(4)]
        lse = lse_ref[...]
        qhs, dohs, scores, dps = [], [], [], []
        for h in range(8):
            j, half = divmod(h, 2)
            qhs.append(jnp.where(sels[half], q2s[j], jnp.zeros_like(q2s[j])))
            dohs.append(jnp.where(sels[half], do2s[j], jnp.zeros_like(do2s[j])))
            scores.append(_dot(qhs[h], kk[j // 2], NT))
            dps.append(_dot(dohs[h], vv[j // 2], NT))
        pbs, dsbs, dss, dsinks = [], [], [], []
        for h in range(8):
            j, half = divmod(h, 2)
            s = jnp.where(valid, scores[h] + t5_ref[h], NEG)
            lse_h = lse[:, h:h + 1]
            p = jnp.exp(s - lse_h)
            delta = jnp.sum(jnp.where(sels[half], prods[j], 0.0), axis=-1, keepdims=True)
            ds = p * (dps[h] - delta)
            dss.append(ds)
            dsinks.append(-jnp.sum(jnp.exp(sink_ref[:, h:h + 1] - lse_h) * delta, axis=0, keepdims=True))
            pbs.append(p.astype(BF16))
            dsbs.append(ds.astype(BF16))
        dt5_ref[...] += jnp.stack(dss)
        dsink_ref[...] += jnp.concatenate(dsinks, axis=1)
        dqs = []
        for j in range(4):
            a, b = 2 * j, 2 * j + 1
            zero = jnp.zeros_like(kk[j // 2])
            dqs.append(_dot(dsbs[a], jnp.where(sels[0], kk[j // 2], zero)) + _dot(dsbs[b], jnp.where(sels[1], kk[j // 2], zero)))
        dq_ref[...] = jnp.concatenate(dqs, axis=1)
        dk_groups, dv_groups = [], []
        for g in range(2):
            dkk = sum(_dot(dsbs[h], qhs[h], TN) for h in range(4 * g, 4 * g + 4))
            dvv = sum(_dot(pbs[h], dohs[h], TN) for h in range(4 * g, 4 * g + 4))
            dk_groups.append(dkk + pltpu.roll(dkk, HD, 1))
            dv_groups.append(dvv + pltpu.roll(dvv, HD, 1))
        dkv_ref[pl.ds(off, SW_KEYS), :] += jnp.concatenate(
            [jnp.where(first, dk_groups[0], dk_groups[1]), jnp.where(first, dv_groups[0], dv_groups[1])], axis=1)

    s_tok = q.shape[0]
    blk = pl.BlockSpec((SW_BLK, 512), lambda n: (n, 0))
    kv_spec = pl.BlockSpec(kv.shape, lambda n: (0, 0))
    t5_spec = pl.BlockSpec((8, SW_BLK, SW_KEYS), lambda n: (0, 0, 0))
    vec = pl.BlockSpec((1, 8), lambda n: (0, 0))
    return pl.pallas_call(
        body, name=name,
        out_shape=(_sds((s_tok, 512), F32), _sds(kv.shape, F32), _sds((8, SW_BLK, SW_KEYS), F32), _sds((1, 8), F32)),
        grid=(SW_NB,), in_specs=[blk, kv_spec, blk, blk, pl.BlockSpec((SW_BLK, 8), lambda n: (n, 0)), t5_spec, vec],
        out_specs=(blk, kv_spec, t5_spec, vec), compiler_params=_params(("arbitrary",)))(q, kv, o, do, lse, t5, sink)


def gate_fwd(zg, bias, pa, ps, *, name, tm=512):
    def body(z0_ref, z1_ref, b0_ref, b1_ref, pa_ref, ps_ref, m_ref):
        g0 = jax.nn.sigmoid(z0_ref[...] + b0_ref[...])
        g1 = jax.nn.sigmoid(z1_ref[...] + b1_ref[...])
        m_ref[...] = (g0 * pa_ref[...] + g1 * ps_ref[...]).astype(BF16)

    s = zg.shape[0]
    half = lambda j: pl.BlockSpec((tm, DM), lambda i, j=j: (i, j))
    bvec = lambda j: pl.BlockSpec((1, DM), lambda i, j=j: (0, j))
    return pl.pallas_call(
        body, name=name, out_shape=_sds((s, DM), BF16), grid=(s // tm,),
        in_specs=[half(0), half(1), bvec(0), bvec(1), half(0), half(0)], out_specs=half(0),
        compiler_params=_params(("parallel",)))(zg, zg, bias, bias, pa, ps)


def gate_bwd(dm, zg, bias, pa, ps, *, name, tm=512):
    def body(dm_ref, z0_ref, z1_ref, b0_ref, b1_ref, pa_ref, ps_ref, dpa_ref, dps_ref, dz_ref, db_ref):
        @pl.when(pl.program_id(0) == 0)
        def _():
            db_ref[...] = jnp.zeros_like(db_ref)

        dm = dm_ref[...]
        g0 = jax.nn.sigmoid(z0_ref[...] + b0_ref[...])
        g1 = jax.nn.sigmoid(z1_ref[...] + b1_ref[...])
        dpa_ref[...] = (dm * g0).astype(BF16)
        dps_ref[...] = (dm * g1).astype(BF16)
        dz0 = dm * pa_ref[...] * g0 * (1.0 - g0)
        dz1 = dm * ps_ref[...] * g1 * (1.0 - g1)
        dz_ref[:, 0:DM] = dz0.astype(BF16)
        dz_ref[:, DM:2 * DM] = dz1.astype(BF16)
        db_ref[:, 0:DM] += jnp.sum(dz0, axis=0, keepdims=True)
        db_ref[:, DM:2 * DM] += jnp.sum(dz1, axis=0, keepdims=True)

    s = zg.shape[0]
    half = lambda j: pl.BlockSpec((tm, DM), lambda i, j=j: (i, j))
    bvec = lambda j: pl.BlockSpec((1, DM), lambda i, j=j: (0, j))
    return pl.pallas_call(
        body, name=name,
        out_shape=(_sds((s, DM), BF16), _sds((s, DM), BF16), _sds((s, GATE_W), BF16), _sds((1, GATE_W), F32)),
        grid=(s // tm,), in_specs=[half(0), half(0), half(1), bvec(0), bvec(1), half(0), half(0)],
        out_specs=(half(0), half(0), pl.BlockSpec((tm, GATE_W), lambda i: (i, 0)), pl.BlockSpec((1, GATE_W), lambda i: (0, 0))),
        compiler_params=_params(("arbitrary",)))(dm, zg, zg, bias, bias, pa, ps)


def loss_head(y, target, *, name, tm=512):
    def body(y_ref, t_ref, dy_ref, l_ref):
        @pl.when(pl.program_id(0) == 0)
        def _():
            l_ref[...] = jnp.zeros_like(l_ref)

        err = y_ref[...] - t_ref[...]
        dy_ref[...] = err * (1.0 / DM)
        l_ref[...] += 0.5 * jnp.sum(jnp.mean(err * err, axis=-1, keepdims=True), axis=0, keepdims=True)

    s = y.shape[0]
    tile = pl.BlockSpec((tm, DM), lambda i: (i, 0))
    return pl.pallas_call(
        body, name=name, out_shape=(_sds((s, DM), F32), _sds((1, 128), F32)), grid=(s // tm,), in_specs=[tile, tile],
        out_specs=(tile, pl.BlockSpec((1, 128), lambda i: (0, 0))), compiler_params=_params(("arbitrary",)))(y, target)


def adamw(w, g, m, v, *, name):
    def body(w_ref, g_ref, m_ref, v_ref, d_ref, nm_ref, nv_ref):
        g = g_ref[...]
        nm = ADAM_B1 * m_ref[...] + (1.0 - ADAM_B1) * g
        nv = ADAM_B2 * v_ref[...] + (1.0 - ADAM_B2) * jnp.square(g)
        m_hat = nm / (1.0 - ADAM_B1 ** ADAM_STEP)
        v_hat = nv / (1.0 - ADAM_B2 ** ADAM_STEP)
        d_ref[...] = -ADAM_LR * (m_hat / (jnp.sqrt(v_hat) + ADAM_EPS) + ADAM_WD * w_ref[...])
        nm_ref[...] = nm
        nv_ref[...] = nv

    b, k, n = w.shape
    tk = k // 4 if k % 32 == 0 else k
    spec = pl.BlockSpec((None, tk, n), lambda i, j: (i, j, 0))
    out = _sds(w.shape, F32)
    return pl.pallas_call(
        body, name=name, out_shape=(out, out, out), grid=(b, k // tk), in_specs=[spec] * 4, out_specs=(spec,) * 3,
        compiler_params=_params(("parallel", "parallel")))(w, g, m, v)


def adamw_layer(w, m, v, mine, theirs, cidx, layer, filled=None, *, name):
    _, k, n = w.shape
    nt = 2
    tk = k // 2 // nt

    def body(c_ref, w_ref, m_ref, v_ref, a_ref, b_ref, *rest):
        g_ref, d_ref, nm_ref, nv_ref = rest[-4:]
        g = jnp.where(pl.program_id(0) == c_ref[0], a_ref[...], b_ref[...])
        g_ref[...] = g
        nm = ADAM_B1 * m_ref[...] + (1.0 - ADAM_B1) * g
        nv = ADAM_B2 * v_ref[...] + (1.0 - ADAM_B2) * jnp.square(g)
        m_hat = nm / (1.0 - ADAM_B1 ** ADAM_STEP)
        v_hat = nv / (1.0 - ADAM_B2 ** ADAM_STEP)
        d_ref[...] = -ADAM_LR * (m_hat / (jnp.sqrt(v_hat) + ADAM_EPS) + ADAM_WD * w_ref[...])
        nm_ref[...] = nm
        nv_ref[...] = nv

    full = pl.BlockSpec((None, tk, n), lambda hf, t, c: (layer, hf * nt + t, 0))
    half_mine = pl.BlockSpec((tk, n), lambda hf, t, c: (jnp.where(hf == c[0], t, 0), 0))
    half_theirs = pl.BlockSpec((tk, n), lambda hf, t, c: (jnp.where(hf != c[0], t, 0), 0))
    out = _sds(w.shape, F32)
    ins, specs, aliases = [cidx, w, m, v, mine, theirs], [full, full, full, half_mine, half_theirs], {}
    if filled is not None:
        aliases = {len(ins) + i: i for i in range(4)}
        ins += list(filled)
        specs += [pl.BlockSpec(memory_space=pl.ANY)] * 4
    return pl.pallas_call(
        body, name=name, out_shape=(out, out, out, out),
        grid_spec=pltpu.PrefetchScalarGridSpec(
            num_scalar_prefetch=1, grid=(2, nt), in_specs=specs, out_specs=(full, full, full, full)),
        input_output_aliases=aliases,
        compiler_params=_params(("arbitrary", "arbitrary")))(*ins)


def t5_table_grad(dt5_a, dt5_b, *, name):
    def body(a_ref, b_ref, map_ref, o_ref):
        d = a_ref[...] + b_ref[...]
        bucket = map_ref[...]
        for b in range(32):
            hit = (bucket == b)[None]
            o_ref[b] = jnp.sum(jnp.sum(jnp.where(hit, d, 0.0), axis=2), axis=1, keepdims=True)

    return pl.pallas_call(
        body, name=name, out_shape=_sds((32, 8, 1), F32), compiler_params=_params())(
            dt5_a, dt5_b, jnp.asarray(t5_bucket_map()))


def rpb_grad(dbias, *, name):
    def body(d_ref, rev_ref, o_ref):
        rev = rev_ref[...]
        for h in range(NA_HEADS):
            for pr in range(NA_WR // 2):
                d = d_ref[h, :, 128 * pr:128 * pr + 128]
                hi = d.astype(BF16)
                lo = (d - hi.astype(F32)).astype(BF16)
                flipped = _dot(rev, hi) + _dot(rev, lo)
                o_ref[h, pr] = jnp.sum(pltpu.roll(flipped, 0, 1, stride=1, stride_axis=0), axis=0, keepdims=True)

    anti = jnp.asarray(np.eye(GRID_W, dtype=np.float32)[::-1], dtype=BF16)
    e = pl.pallas_call(
        body, name=name, out_shape=_sds((NA_WR, NA_HEADS, NA_WR // 2, 1, 128), F32), grid=(NA_WR,),
        in_specs=[pl.BlockSpec((None, NA_HEADS, GRID_W, NA_KEYS), lambda p: (p, 0, 0, 0)),
                  pl.BlockSpec((GRID_W, GRID_W), lambda p: (0, 0))],
        out_specs=pl.BlockSpec((None, NA_HEADS, NA_WR // 2, 1, 128), lambda p: (p, 0, 0, 0, 0)),
        compiler_params=_params(("parallel",)))(dbias, anti)
    nci, nri = 2 * NA_WC - 1, 2 * NA_WR - 1
    e = e.reshape(NA_WR, NA_HEADS, NA_WR // 2, 128).transpose(0, 2, 1, 3).reshape(NA_WR * NA_WR // 2, NA_HEADS, 128)
    parts = jnp.concatenate([e[..., 48:48 + nci], jnp.concatenate([e[..., 112:128], e[..., 0:nci - 16]], axis=-1)], axis=0)
    p, pr = np.arange(NA_WR)[:, None], np.arange(NA_WR // 2)[None, :]
    ri = np.concatenate([(2 * pr - p + NA_WR - 1).reshape(-1), (2 * pr - p + NA_WR).reshape(-1)])
    pick = jnp.asarray((ri[None, :] == np.arange(16)[:, None]).astype(np.float32))
    out = mm(pick, parts.reshape(2 * NA_WR * NA_WR // 2, NA_HEADS * nci), name=name + "_rows", exact=True)
    return out.reshape(16, NA_HEADS, nci)[:nri].transpose(1, 0, 2)


BIG = ("ffn1_w_gate", "ffn1_w_up", "ffn1_w_down", "w_in", "w_branch_na", "w_branch_sw", "w_out",
       "ffn2_w_gate", "ffn2_w_up", "ffn2_w_down")
SMALL = ("ffn1_norm", "mix_norm", "b_gate", "na_q_norm", "na_k_norm", "na_rpb", "sw_q_norm", "sw_k_norm", "sw_sink",
         "ffn2_norm")


def _cols_to_full(w4):
    return w4.transpose(1, 0, 2).reshape(w4.shape[1], NSH * w4.shape[2])


def _full_to_cols(w):
    return w.reshape(w.shape[0], NSH, w.shape[1] // NSH).transpose(1, 0, 2)


def _mixer_weights(g):
    w_in_t = g["w_in"].reshape(IN_W, DM)
    return dict(w_att_t=w_in_t[:ATT_W], w_gz_t=w_in_t[ATT_W:], wa=_cols_to_full(g["w_branch_na"]),
                ws=_cols_to_full(g["w_branch_sw"]), wo=g["w_out"].reshape(DM, DM))


GROUPS = {"ffn1": ("ffn1_w_gate", "ffn1_w_up", "ffn1_w_down"), "mix": ("w_in", "w_branch_na", "w_branch_sw", "w_out"),
          "ffn2": ("ffn2_w_gate", "ffn2_w_up", "ffn2_w_down")}


def layer_fwd(x, p, weights, t5b):
    row = lambda v: v.reshape(1, -1)
    g1 = weights("ffn1", x)
    y1, h1, gg1, uu1 = ffn_fwd(x, row(p["ffn1_norm"]), g1["ffn1_w_gate"], g1["ffn1_w_up"], g1["ffn1_w_down"], name="ffn_fwd")
    w = _mixer_weights(weights("mix", y1))
    hm = rms_fwd(y1, row(p["mix_norm"]), name="mix_norm_fwd")
    z = mm(hm, w["w_att_t"], tb=True, name="proj_att", tn=768)
    zg = mm(hm, w["w_gz_t"], tb=True, name="proj_gate", tn=1024)
    qa, ka, va, qs, kv = qknorm_fwd(z, p["na_q_norm"], p["na_k_norm"], p["sw_q_norm"], p["sw_k_norm"], name="qknorm_fwd")
    bias = na_bias_table(p["na_rpb"], name="na_bias_table")
    o_na, lse_na = na_fwd(qa, ka, va, bias, name="na_fwd")
    kvp = jnp.pad(kv, ((SW_BLK, SW_BLK), (0, 0)))
    sink = row(p["sw_sink"])
    o_sw, lse_sw = sw_fwd(qs, kvp, t5b, sink, name="sw_fwd")
    pa = mm(o_na, w["wa"], name="branch_na")
    ps = mm(o_sw, w["ws"], name="branch_sw")
    merged = gate_fwd(zg, row(p["b_gate"]), pa, ps, name="gate_fwd")
    y2 = mm(merged, w["wo"], add=y1, name="out_proj")
    g2 = weights("ffn2", y2)
    y3, h2, gg2, uu2 = ffn_fwd(y2, row(p["ffn2_norm"]), g2["ffn2_w_gate"], g2["ffn2_w_up"], g2["ffn2_w_down"], name="ffn_fwd")
    saved = dict(x=x, y1=y1, h1=h1, gg1=gg1, uu1=uu1, hm=hm, z=z, zg=zg, qa=qa, ka=ka, va=va, qs=qs, kvp=kvp, bias=bias,
                 o_na=o_na, lse_na=lse_na, o_sw=o_sw, lse_sw=lse_sw, pa=pa, ps=ps, merged=merged, y2=y2, h2=h2, gg2=gg2,
                 uu2=uu2, w=w, sink=sink, g1=g1, g2=g2)
    return y3, saved


def layer_bwd(dy3, sv, p, t5b, emit, dep=None):
    w, g1, g2 = sv["w"], sv["g1"], sv["g2"]
    row = lambda v: v.reshape(1, -1)
    fold = lambda v: v.reshape(-1, HD).sum(axis=0)
    small = {}
    dy2, small["ffn2_norm"], act, dg, du = ffn_bwd_tokens(
        dy3, sv["y2"], row(p["ffn2_norm"]), sv["gg2"], sv["uu2"], g2["ffn2_w_gate"], g2["ffn2_w_up"], g2["ffn2_w_down"],
        name="ffn_bwd_tokens", dep=dep)
    token = emit("ffn2", ffn_bwd_weights(sv["h2"], dy3, act, dg, du, name="ffn_bwd_weights"))
    dmerged = mm(dy2, w["wo"], tb=True, name="out_proj_dx", dep=token)
    gw_out = mm(sv["merged"], dy2, ta=True, out_dtype=BF16, tk=512, name="out_proj_dw").reshape(NSH, DM // NSH, DM)
    dpa, dps, dzg, small["b_gate"] = gate_bwd(dmerged, sv["zg"], row(p["b_gate"]), sv["pa"], sv["ps"], name="gate_bwd")
    gw_na = _full_to_cols(mm(sv["o_na"], dpa, ta=True, out_dtype=BF16, tk=512, name="branch_dw"))
    gw_sw = _full_to_cols(mm(sv["o_sw"], dps, ta=True, out_dtype=BF16, tk=512, name="branch_dw"))
    do_na = mm(dpa, w["wa"], tb=True, out_dtype=BF16, name="branch_dx")
    do_sw = mm(dps, w["ws"], tb=True, out_dtype=BF16, name="branch_dx")
    dqa, dka, dva, dbias = na_bwd(sv["qa"], sv["ka"], sv["va"], sv["o_na"], do_na, sv["lse_na"], sv["bias"], name="na_bwd")
    dqs, dkvp, dt5, dsink = sw_bwd(sv["qs"], sv["kvp"], sv["o_sw"], do_sw, sv["lse_sw"], t5b, sv["sink"], name="sw_bwd")
    dkv = dkvp[SW_BLK:SW_BLK + SEQ]
    dz, dgqa, dgka, dgqs, dgks = qknorm_bwd(sv["z"], dqa, dka, dva, dqs, dkv, p["na_q_norm"], p["na_k_norm"],
                                            p["sw_q_norm"], p["sw_k_norm"], name="qknorm_bwd")
    small["na_q_norm"], small["na_k_norm"], small["sw_q_norm"], small["sw_k_norm"] = fold(dgqa), fold(dgka), fold(dgqs), fold(dgks)
    small["na_rpb"] = rpb_grad(dbias, name="rpb_grad")
    small["sw_sink"] = dsink
    gw_att_t = mm(dz, sv["hm"], ta=True, out_dtype=BF16, tk=512, tm=768, name="proj_att_dw")
    gw_gz_t = mm(dzg, sv["hm"], ta=True, out_dtype=BF16, tk=512, name="proj_gate_dw")
    gw_in = jnp.concatenate([gw_att_t, gw_gz_t], axis=0).reshape(NSH, IN_W // NSH, DM)
    token = emit("mix", (gw_in, gw_na, gw_sw, gw_out))
    dh = mm(dz, w["w_att_t"], tk=768, name="proj_att_dx", dep=token)
    dh = mm(dzg, w["w_gz_t"], add=dh, name="proj_gate_dx")
    dy1, small["mix_norm"] = rms_bwd(dh, sv["y1"], row(p["mix_norm"]), dy2, name="mix_norm_bwd")
    dx, small["ffn1_norm"], act, dg, du = ffn_bwd_tokens(
        dy1, sv["x"], row(p["ffn1_norm"]), sv["gg1"], sv["uu1"], g1["ffn1_w_gate"], g1["ffn1_w_up"], g1["ffn1_w_down"],
        name="ffn_bwd_tokens")
    emit("ffn1", ffn_bwd_weights(sv["h1"], dy1, act, dg, du, name="ffn_bwd_weights"))
    return dx, small, dt5


ANY = pl.BlockSpec(memory_space=pl.ANY)


def _place():
    x, y, c = lax.axis_index("x"), lax.axis_index("y"), lax.axis_index("c")
    chips = [(1 - x, y), (x, 1 - y), (1 - x, 1 - y)]
    return x, y, c, chips


def _remote(src, dst, send_sem, recv_sem, to):
    return pltpu.make_async_remote_copy(src_ref=src, dst_ref=dst, send_sem=send_sem, recv_sem=recv_sem, device_id=to,
                                        device_id_type=MESH)


HBM = pl.BlockSpec(memory_space=pltpu.HBM)
SEM = pl.BlockSpec(memory_space=pltpu.SEMAPHORE)
ORDERED_EFFECT = pltpu.SideEffectType.DATAFLOW_SIDE_EFFECTING


def _in_hbm(v):
    return pltpu.with_memory_space_constraint(v, pltpu.HBM)


def _row_half(ref_shape_rows, c):
    half = ref_shape_rows // 2
    return pl.ds(c * half, half)


def _ici_gather_copies(w, land, send_sems, recv_sems):
    x, y, c, chips = _place()
    me = 2 * x + y
    copies = []
    for a in range(len(w)):
        rows = _row_half(w[a].shape[0], c)
        for k, chip in enumerate(chips):
            copies.append(_remote(w[a].at[rows], land[a].at[me, rows], send_sems.at[3 * a + k], recv_sems.at[3 * a + k],
                                  (*chip, c)))
    return copies


def _d2d_gather_copies(w, land, send_sems, recv_sems):
    x, y, c, chips = _place()
    me, sibling = 2 * x + y, (x, y, 1 - c)
    copies = []
    for a in range(len(w)):
        rows = _row_half(w[a].shape[0], c)
        for k, (cx, cy) in enumerate(chips):
            blk = land[a].at[2 * cx + cy, rows]
            copies.append(_remote(blk, blk, send_sems.at[4 * a + k], recv_sems.at[4 * a + k], sibling))
        copies.append(_remote(w[a], land[a].at[me], send_sems.at[4 * a + 3], recv_sems.at[4 * a + 3], sibling))
    return copies


def _d2d_gather_waits(w, land, send_sems, recv_sems):
    x, y, c, chips = _place()
    me, sibling = 2 * x + y, (x, y, 1 - c)
    waits = []
    for a in range(len(w)):
        rows = _row_half(w[a].shape[0], 1 - c)
        for k, (cx, cy) in enumerate(chips):
            blk = land[a].at[2 * cx + cy, rows]
            waits.append(_remote(blk, blk, send_sems.at[4 * a + k], recv_sems.at[4 * a + k], sibling))
        waits.append(_remote(w[a], land[a].at[me], send_sems.at[4 * a + 3], recv_sems.at[4 * a + 3], sibling))
    return waits


def gather_start(groups, *, name):
    sizes = [len(g) for g in groups]
    shards = [s for g in groups for s in g]
    n, ng = len(shards), len(groups)

    def body(*refs):
        w, land, sems = refs[:n], refs[n:2 * n], refs[2 * n:2 * n + 2 * ng]
        off = 0
        for gi, size in enumerate(sizes):
            for cp in _ici_gather_copies(w[off:off + size], land[off:off + size], sems[2 * gi], sems[2 * gi + 1]):
                cp.start()
            off += size

    lands = [lax.empty((NSH,) + s.shape, s.dtype) for s in shards]
    sem_shapes = tuple(pltpu.SemaphoreType.DMA((3 * size,)) for size in sizes for _ in range(2))
    res = pl.pallas_call(
        body, name=name,
        out_shape=sem_shapes + tuple(pltpu.HBM(s.shape, s.dtype) for s in shards) + tuple(pltpu.HBM(l.shape, l.dtype) for l in lands),
        in_specs=[HBM] * (2 * n), out_specs=(SEM,) * (2 * ng) + (HBM,) * (2 * n),
        input_output_aliases={i: 2 * ng + i for i in range(2 * n)},
        compiler_params=pltpu.CompilerParams(has_side_effects=ORDERED_EFFECT))(
            *[_in_hbm(s) for s in shards], *[_in_hbm(l) for l in lands])
    out, off = [], 0
    for gi, size in enumerate(sizes):
        out.append((res[2 * gi], res[2 * gi + 1], list(res[2 * ng + off:2 * ng + off + size]),
                    list(res[2 * ng + n + off:2 * ng + n + off + size])))
        off += size
    return out


def gather_wait(send_sems, recv_sems, shards, lands, after, *, name):
    n = len(shards)

    def body(*refs):
        w, land = refs[:n], refs[n:2 * n]
        send, recv = refs[2 * n:2 * n + 2]
        for cp in _ici_gather_copies(w, land, send, recv):
            cp.wait_send()
            cp.wait_recv()

    res = pl.pallas_call(
        body, name=name,
        out_shape=tuple(pltpu.HBM(s.shape, s.dtype) for s in shards) + tuple(pltpu.HBM(l.shape, l.dtype) for l in lands),
        in_specs=[HBM] * (2 * n) + [SEM, SEM, ANY], out_specs=(HBM,) * (2 * n),
        input_output_aliases={i: i for i in range(2 * n)},
        compiler_params=pltpu.CompilerParams(has_side_effects=ORDERED_EFFECT))(*shards, *lands, send_sems, recv_sems, after)
    return list(res[:n]), list(res[n:])


def gather_finish(shards, lands, *, name):
    n = len(shards)

    def body(*refs):
        w, land = refs[:n], refs[n:2 * n]
        send_sems, recv_sems = refs[3 * n:]
        d2d = _d2d_gather_copies(w, land, send_sems, recv_sems)
        for cp in d2d:
            cp.start()
        for cp in _d2d_gather_waits(w, land, send_sems, recv_sems):
            cp.wait_recv()
        for cp in d2d:
            cp.wait_send()

    return list(pl.pallas_call(
        body, name=name, out_shape=tuple(pltpu.HBM(l.shape, l.dtype) for l in lands),
        in_specs=[ANY] * (2 * n), out_specs=tuple([ANY] * n), input_output_aliases={n + i: i for i in range(n)},
        scratch_shapes=[pltpu.SemaphoreType.DMA((4 * n,)), pltpu.SemaphoreType.DMA((4 * n,))])(*shards, *lands))


def pair_exchange(grads, *, name):
    n = len(grads)

    def body(*refs):
        g, buf = refs[:n], refs[n:2 * n]
        send_sems, recv_sems = refs[2 * n:]
        x, y, c, _ = _place()
        copies = []
        for a in range(n):
            half = g[a].shape[1] // 2
            cp = _remote(g[a].at[:, pl.ds((1 - c) * half, half)], buf[a], send_sems.at[a], recv_sems.at[a], (x, y, 1 - c))
            cp.start()
            copies.append(cp)
        for cp in copies:
            cp.wait()

    return pl.pallas_call(
        body, name=name, out_shape=tuple(pltpu.HBM((NSH, g.shape[1] // 2, g.shape[2]), g.dtype) for g in grads),
        in_specs=[ANY] * n, out_specs=tuple([ANY] * n),
        scratch_shapes=[pltpu.SemaphoreType.DMA((n,)), pltpu.SemaphoreType.DMA((n,))])(*grads)


def _chip_exchange_copies(s, buf, send_sems, recv_sems):
    x, y, c, chips = _place()
    return [_remote(s[a].at[2 * cx + cy], buf[a].at[k], send_sems.at[3 * a + k], recv_sems.at[3 * a + k], (cx, cy, c))
            for a in range(len(s)) for k, (cx, cy) in enumerate(chips)]


def chip_exchange(sums, *, name):
    n = len(sums)

    def body(*refs):
        copies = _chip_exchange_copies(refs[:n], refs[n:2 * n], *refs[2 * n:])
        for cp in copies:
            cp.start()
        for cp in copies:
            cp.wait()

    return pl.pallas_call(
        body, name=name, out_shape=tuple(pltpu.HBM((3,) + s.shape[1:], s.dtype) for s in sums),
        in_specs=[ANY] * n, out_specs=tuple([ANY] * n),
        scratch_shapes=[pltpu.SemaphoreType.DMA((3 * n,)), pltpu.SemaphoreType.DMA((3 * n,))])(*sums)


def chip_exchange_start(sums, *, name):
    n = len(sums)

    def body(*refs):
        for cp in _chip_exchange_copies(refs[:n], refs[n:2 * n], refs[2 * n], refs[2 * n + 1]):
            cp.start()
        refs[-1][...] = jnp.zeros_like(refs[-1])

    lands = [lax.empty((3,) + s.shape[1:], s.dtype) for s in sums]
    res = pl.pallas_call(
        body, name=name,
        out_shape=(pltpu.SemaphoreType.DMA((3 * n,)), pltpu.SemaphoreType.DMA((3 * n,)))
        + tuple(pltpu.HBM(s.shape, s.dtype) for s in sums) + tuple(pltpu.HBM(l.shape, l.dtype) for l in lands)
        + (_sds((8, 128), F32),),
        in_specs=[HBM] * (2 * n), out_specs=(SEM, SEM) + (HBM,) * (2 * n) + (pl.BlockSpec(memory_space=pltpu.VMEM),),
        input_output_aliases={i: 2 + i for i in range(2 * n)},
        compiler_params=pltpu.CompilerParams(has_side_effects=ORDERED_EFFECT))(
            *[_in_hbm(s) for s in sums], *[_in_hbm(l) for l in lands])
    return res[0], res[1], list(res[2:2 + n]), list(res[2 + n:2 + 2 * n]), res[-1]


def chip_exchange_wait(send_sems, recv_sems, sums, lands, after, *, name):
    n = len(sums)

    def body(*refs):
        for cp in _chip_exchange_copies(refs[:n], refs[n:2 * n], refs[2 * n], refs[2 * n + 1]):
            cp.wait_send()
            cp.wait_recv()

    res = pl.pallas_call(
        body, name=name,
        out_shape=tuple(pltpu.HBM(s.shape, s.dtype) for s in sums) + tuple(pltpu.HBM(l.shape, l.dtype) for l in lands),
        in_specs=[HBM] * (2 * n) + [SEM, SEM] + [ANY] * len(after), out_specs=(HBM,) * (2 * n),
        input_output_aliases={i: i for i in range(2 * n)},
        compiler_params=pltpu.CompilerParams(has_side_effects=ORDERED_EFFECT))(*sums, *lands, send_sems, recv_sems, *after)
    return list(res[:n]), list(res[n:])


def pair_send(halves, *, name):
    n = len(halves)

    def body(*refs):
        h, got = refs[:n], refs[n:2 * n]
        send_sems, recv_sems = refs[2 * n:]
        x, y, c, _ = _place()
        copies = []
        for i in range(n):
            cp = _remote(h[i], got[i], send_sems.at[i], recv_sems.at[i], (x, y, 1 - c))
            cp.start()
            copies.append(cp)
        for cp in copies:
            cp.wait()

    return list(pl.pallas_call(
        body, name=name, out_shape=tuple(pltpu.HBM(v.shape, v.dtype) for v in halves),
        in_specs=[ANY] * n, out_specs=tuple([ANY] * n),
        scratch_shapes=[pltpu.SemaphoreType.DMA((n,)), pltpu.SemaphoreType.DMA((n,))])(*halves))


def allreduce_small(v, *, name):
    rows = v.shape[0]

    def body(v_ref, o_ref, gath, send_sems, recv_sems):
        x, y, c, _ = _place()
        me = 4 * x + 2 * y + c
        gath[me] = v_ref[...]
        copies = []
        for k in range(1, 8):
            fx, fy, fc = (k >> 2) & 1, (k >> 1) & 1, k & 1
            peer = (jnp.where(fx, 1 - x, x), jnp.where(fy, 1 - y, y), jnp.where(fc, 1 - c, c))
            cp = _remote(v_ref, gath.at[me], send_sems.at[k - 1], recv_sems.at[k - 1], peer)
            cp.start()
            copies.append(cp)
        for cp in copies:
            cp.wait()
        acc = gath[0]
        for d in range(1, 8):
            acc = acc + gath[d]
        o_ref[...] = acc

    return pl.pallas_call(
        body, name=name, out_shape=_sds(v.shape, F32),
        in_specs=[pl.BlockSpec(memory_space=pltpu.VMEM)], out_specs=pl.BlockSpec(memory_space=pltpu.VMEM),
        scratch_shapes=[pltpu.VMEM((8, rows, 128), F32), pltpu.SemaphoreType.DMA((7,)), pltpu.SemaphoreType.DMA((7,))])(v)


def add_halves(g, buf, cidx, *, name):
    _, k, n = g.shape

    def body(c_ref, g_ref, b_ref, o_ref):
        o_ref[...] = (g_ref[...].astype(F32) + b_ref[...].astype(F32)).astype(BF16)

    blk = pl.BlockSpec((None, k // 2, n), lambda s, c: (s, 0, 0))
    return pl.pallas_call(
        body, name=name, out_shape=_sds(buf.shape, BF16),
        grid_spec=pltpu.PrefetchScalarGridSpec(
            num_scalar_prefetch=1, grid=(NSH,),
            in_specs=[pl.BlockSpec((None, k // 2, n), lambda s, c: (s, c[0], 0)), blk], out_specs=blk),
        compiler_params=_params(("parallel",)))(cidx, g, buf)


def add_chips(sums, buf, sidx, *, name, dep=None):
    _, kh, n = sums.shape

    def body(s_ref, mine_ref, b_ref, *rest):
        o_ref = rest[-1]
        o_ref[...] = (mine_ref[...].astype(F32) + b_ref[0].astype(F32)) + (b_ref[1].astype(F32) + b_ref[2].astype(F32))

    ins, specs = _with_dep(
        [sums, buf], [pl.BlockSpec((None, kh, n), lambda i, s: (s[0], 0, 0)), pl.BlockSpec((3, kh, n), lambda i, s: (0, 0, 0))],
        dep)
    return pl.pallas_call(
        body, name=name, out_shape=_sds((kh, n), F32),
        grid_spec=pltpu.PrefetchScalarGridSpec(
            num_scalar_prefetch=1, grid=(1,), in_specs=specs, out_specs=pl.BlockSpec((kh, n), lambda i, s: (0, 0))),
        compiler_params=_params(("arbitrary",)))(sidx, *ins)


PARAMS = ("ffn1_norm", "ffn1_w_gate", "ffn1_w_up", "ffn1_w_down", "mix_norm", "w_in", "b_gate", "na_q_norm", "na_k_norm",
          "na_rpb", "sw_q_norm", "sw_k_norm", "sw_sink", "t5_rel_table", "w_branch_na", "w_branch_sw", "w_out", "ffn2_norm",
          "ffn2_w_gate", "ffn2_w_up", "ffn2_w_down")
SMALL_ALL = tuple(n for n in PARAMS if n not in BIG)
TRANSPOSED = ("ffn1_w_gate", "ffn1_w_up", "w_in", "ffn2_w_gate", "ffn2_w_up")
SMALL_ROWS = 152


def _pack_small(vals):
    flat = jnp.concatenate([vals[n].reshape(-1).astype(F32) for n in SMALL_ALL] + [vals["loss"].reshape(-1)])
    return jnp.pad(flat, (0, SMALL_ROWS * 128 - flat.shape[0])).reshape(SMALL_ROWS, 128)


def _unpack_small(packed, like):
    flat, out, off = packed.reshape(-1), {}, 0
    for n in SMALL_ALL:
        size = math.prod(like[n].shape)
        out[n] = flat[off:off + size].reshape(like[n].shape)
        off += size
    out["loss"] = flat[off]
    return out


def kernel(x, ffn1_norm, ffn1_w_gate, ffn1_w_up, ffn1_w_down, mix_norm, w_in, b_gate, na_q_norm, na_k_norm, na_rpb, sw_q_norm, sw_k_norm, sw_sink, t5_rel_table, w_branch_na, w_branch_sw, w_out, ffn2_norm, ffn2_w_gate, ffn2_w_up, ffn2_w_down, loss_target, m_ffn1_norm, m_ffn1_w_gate, m_ffn1_w_up, m_ffn1_w_down, m_mix_norm, m_w_in, m_b_gate, m_na_q_norm, m_na_k_norm, m_na_rpb, m_sw_q_norm, m_sw_k_norm, m_sw_sink, m_t5_rel_table, m_w_branch_na, m_w_branch_sw, m_w_out, m_ffn2_norm, m_ffn2_w_gate, m_ffn2_w_up, m_ffn2_w_down, v_ffn1_norm, v_ffn1_w_gate, v_ffn1_w_up, v_ffn1_w_down, v_mix_norm, v_w_in, v_b_gate, v_na_q_norm, v_na_k_norm, v_na_rpb, v_sw_q_norm, v_sw_k_norm, v_sw_sink, v_t5_rel_table, v_w_branch_na, v_w_branch_sw, v_w_out, v_ffn2_norm, v_ffn2_w_gate, v_ffn2_w_up, v_ffn2_w_down):
    args = locals()
    tr = lambda n, a: jnp.transpose(a, (0, 2, 1)) if n in TRANSPOSED else a
    w = {n: tr(n, args[n]) for n in PARAMS}
    m = {n: tr(n, args["m_" + n]) for n in PARAMS}
    v = {n: tr(n, args["v_" + n]) for n in PARAMS}
    cidx = lax.axis_index("c").astype(jnp.int32).reshape(1)
    sidx = (2 * lax.axis_index("x") + lax.axis_index("y")).astype(jnp.int32).reshape(1)

    small = [{n: w[n][l] for n in SMALL} for l in range(DEPTH)]
    t5b = t5_bias(w["t5_rel_table"], name="t5_bias")
    order = ("ffn1", "mix", "ffn2")

    keys = [(l, g) for l in range(DEPTH) for g in order]
    in_flight = dict(zip(keys, gather_start([[w[n][l].astype(BF16) for n in GROUPS[g]] for l, g in keys], name="gather_start")))

    def weights_of(l):
        def get(group, after):
            send_sems, recv_sems, thru, lands = in_flight[(l, group)]
            thru, lands = gather_wait(send_sems, recv_sems, thru, lands, after, name="gather_wait")
            return dict(zip(GROUPS[group], gather_finish(thru, lands, name="gather_finish")))
        return get

    h0, saved0 = layer_fwd(x[0], small[0], weights_of(0), t5b)
    h1, saved1 = layer_fwd(h0, small[1], weights_of(1), t5b)
    dy, loss_row = loss_head(h1, loss_target[0], name="loss_head")

    crossing, tokens = {}, []

    def reduce_of(l):
        def emit(group, grads):
            grads = list(grads)
            sums = [add_halves(g, b, cidx, name="add_halves") for g, b in zip(grads, pair_exchange(grads, name="pair_exchange"))]
            send_sems, recv_sems, sums, lands, token = chip_exchange_start(sums, name="chip_exchange_start")
            crossing[(l, group)] = (send_sems, recv_sems, sums, lands)
            tokens.append(token)
            return token
        return emit

    def arrived(l, after):
        halves = {}
        for group in order:
            send_sems, recv_sems, sums, lands = crossing[(l, group)]
            sums, got = chip_exchange_wait(send_sems, recv_sems, sums, lands, after, name="chip_exchange_wait")
            for n, s, b in zip(GROUPS[group], sums, got):
                halves[n] = add_chips(s, b, sidx, name="add_chips")
        return [halves[n] for n in BIG]

    dy, small1, dt5_1 = layer_bwd(dy, saved1, small[1], t5b, reduce_of(1))
    grad_x, small0, dt5_0 = layer_bwd(dy, saved0, small[0], t5b, reduce_of(0), dep=tokens[-1])
    halves1 = arrived(1, [tokens[-1]])
    theirs1 = pair_send(halves1, name="pair_send")
    done1 = [adamw_layer(w[n], m[n], v[n], halves1[a], theirs1[a], cidx, 1, name="adamw_layer") for a, n in enumerate(BIG)]

    smalls = [small0, small1]
    dt5 = t5_table_grad(dt5_0, dt5_1, name="t5_table_grad").reshape(32, 8)
    local_small = {n: jnp.stack([smalls[l][n].reshape(w[n].shape[1:]) for l in range(DEPTH)]) for n in SMALL}
    local_small["t5_rel_table"] = dt5
    local_small["loss"] = loss_row[0, 0:1]
    total = allreduce_small(_pack_small(local_small), name="allreduce_small")
    small_grads = _unpack_small(total, w)
    pack = lambda d: _pack_small({**d, "loss": jnp.zeros((1,), F32)})[None]
    ds, ms, vs = adamw(pack(w), total[None], pack(m), pack(v), name="adamw_small")

    halves0 = arrived(0, [ds, done1[-1][0]])
    theirs0 = pair_send(halves0, name="pair_send")
    grad, delta, new_m, new_v = {}, {}, {}, {}
    for a, n in enumerate(BIG):
        grad[n], delta[n], new_m[n], new_v[n] = adamw_layer(
            w[n], m[n], v[n], halves0[a], theirs0[a], cidx, 0, filled=done1[a], name="adamw_layer")
    for n in SMALL_ALL:
        grad[n] = small_grads[n]
    d_s, m_s, v_s = _unpack_small(ds[0], w), _unpack_small(ms[0], w), _unpack_small(vs[0], w)
    for n in SMALL_ALL:
        delta[n], new_m[n], new_v[n] = d_s[n], m_s[n], v_s[n]

    return (small_grads["loss"], grad_x[None], *[tr(n, grad[n]) for n in PARAMS], *[tr(n, delta[n]) for n in PARAMS],
            *[tr(n, new_m[n]) for n in PARAMS], *[tr(n, new_v[n]) for n in PARAMS])
```

```python
import functools
import math

import jax
import jax.numpy as jnp
import numpy as np
from jax import lax
from jax.experimental import pallas as pl
from jax.experimental.pallas import tpu as pltpu

F32 = jnp.float32
BF16 = jnp.bfloat16

SEQ = 2048
DM = 1024
DFF = 2816
DEPTH = 2
NSH = 4
FSH = DFF // NSH
GRID_W = 64
ROWS = SEQ // GRID_W
NA_HEADS = 8
HD = 64
NA_WR = 8
NA_WC = 16
NA_KEYS = NA_WR * GRID_W
SW_BLK = 128
SW_NB = SEQ // SW_BLK
SW_KEYS = 3 * SW_BLK
ATT_W = 2304
GATE_W = 2048
IN_W = ATT_W + GATE_W
EPS = 1e-6
NEG = -1e30
QK_SCALE = 1.0 / math.sqrt(HD)

ADAM_LR = 0.001
ADAM_B1 = 0.9
ADAM_B2 = 0.999
ADAM_EPS = 1e-08
ADAM_WD = 0.01
ADAM_STEP = 10

VMEM_LIMIT = 56 << 20
MESH = pl.DeviceIdType.MESH

NT = (((1,), (1,)), ((), ()))
TN = (((0,), (0,)), ((), ()))
NN = (((1,), (0,)), ((), ()))


def _dot(a, b, dims=NN):
    return lax.dot_general(a, b, dims, preferred_element_type=F32)


def _params(sem=None):
    return pltpu.CompilerParams(dimension_semantics=sem, vmem_limit_bytes=VMEM_LIMIT)


def _sds(shape, dtype):
    return jax.ShapeDtypeStruct(shape, dtype)


def mm(a, b, *, name, ta=False, tb=False, out_dtype=F32, add=None, scale=None, tm=512, tn=None, tk=None, exact=False,
       dep=None):
    m, kd = (a.shape[1], a.shape[0]) if ta else a.shape
    n = b.shape[0] if tb else b.shape[1]
    tm, tn, tk = min(tm, m), min(tn or n, n), min(tk or kd, kd)
    nk = kd // tk
    dims = (((0 if ta else 1,), (1 if tb else 0,)), ((), ()))

    def body(*refs):
        a_ref, b_ref = refs[:2]
        add_ref = refs[2] if add is not None else None
        o_ref, acc = refs[-2:]
        k = pl.program_id(2)

        @pl.when(k == 0)
        def _():
            acc[...] = jnp.zeros_like(acc)

        if exact:
            acc[...] += lax.dot_general(a_ref[...], b_ref[...], dims, precision=lax.Precision.HIGHEST,
                                        preferred_element_type=F32)
        else:
            acc[...] += lax.dot_general(a_ref[...].astype(BF16), b_ref[...].astype(BF16), dims,
                                        preferred_element_type=F32)

        @pl.when(k == nk - 1)
        def _():
            r = acc[...]
            if scale is not None:
                r = r * scale
            if add is not None:
                r = r + add_ref[...]
            o_ref[...] = r.astype(out_dtype)

    a_spec = pl.BlockSpec((tk, tm), lambda i, j, k: (k, i)) if ta else pl.BlockSpec((tm, tk), lambda i, j, k: (i, k))
    b_spec = pl.BlockSpec((tn, tk), lambda i, j, k: (j, k)) if tb else pl.BlockSpec((tk, tn), lambda i, j, k: (k, j))
    o_spec = pl.BlockSpec((tm, tn), lambda i, j, k: (i, j))
    ins, specs = [a, b], [a_spec, b_spec]
    if add is not None:
        ins.append(add)
        specs.append(o_spec)
    if dep is not None:
        ins.append(dep)
        specs.append(pl.BlockSpec(memory_space=pl.ANY))
    return pl.pallas_call(
        body, name=name, out_shape=_sds((m, n), out_dtype), grid=(m // tm, n // tn, nk), in_specs=specs,
        out_specs=o_spec, scratch_shapes=[pltpu.VMEM((tm, tn), F32)],
        compiler_params=_params(("parallel", "parallel", "arbitrary")))(*ins)


def _rms(x):
    return lax.rsqrt(jnp.mean(x * x, axis=-1, keepdims=True) + EPS)


def rms_fwd(x, gain, *, name, tm=512):
    def body(x_ref, g_ref, h_ref):
        x = x_ref[...]
        h_ref[...] = (x * _rms(x) * g_ref[...]).astype(BF16)

    return pl.pallas_call(
        body, name=name, out_shape=_sds(x.shape, BF16), grid=(x.shape[0] // tm,),
        in_specs=[pl.BlockSpec((tm, DM), lambda i: (i, 0)), pl.BlockSpec((1, DM), lambda i: (0, 0))],
        out_specs=pl.BlockSpec((tm, DM), lambda i: (i, 0)), compiler_params=_params(("parallel",)))(x, gain)


def _rms_bwd_math(dh, x, gain):
    r = _rms(x)
    xh = x * r
    dgain = jnp.sum(dh * xh, axis=0, keepdims=True)
    dxn = dh * gain
    dx = r * (dxn - xh * jnp.mean(dxn * xh, axis=-1, keepdims=True))
    return dx, dgain


def rms_bwd(dh, x, gain, dres, *, name, tm=512):
    def body(dh_ref, x_ref, g_ref, dres_ref, dx_ref, dg_ref):
        @pl.when(pl.program_id(0) == 0)
        def _():
            dg_ref[...] = jnp.zeros_like(dg_ref)

        dx, dg = _rms_bwd_math(dh_ref[...], x_ref[...], g_ref[...])
        dx_ref[...] = dres_ref[...] + dx
        dg_ref[...] += dg

    tile = pl.BlockSpec((tm, DM), lambda i: (i, 0))
    vec = pl.BlockSpec((1, DM), lambda i: (0, 0))
    return pl.pallas_call(
        body, name=name, out_shape=(_sds(x.shape, F32), _sds((1, DM), F32)), grid=(x.shape[0] // tm,),
        in_specs=[tile, tile, vec, tile], out_specs=(tile, vec), compiler_params=_params(("arbitrary",)))(dh, x, gain, dres)


def _with_dep(ins, specs, dep):
    if dep is None:
        return ins, specs
    return ins + [dep], specs + [pl.BlockSpec(memory_space=pl.ANY)]


def _resident_weight():
    return pl.BlockSpec((DFF, DM), lambda i: (0, 0), pipeline_mode=pl.Buffered(1))


def ffn_fwd(x, gain, wg, wu, wd, *, name, tm=512):
    def body(x_ref, g_ref, wg_ref, wu_ref, wd_ref, y_ref, h_ref, gg_ref, uu_ref):
        x = x_ref[...]
        h = (x * _rms(x) * g_ref[...]).astype(BF16)
        h_ref[...] = h
        gg = _dot(h, wg_ref[...], NT)
        uu = _dot(h, wu_ref[...], NT)
        gg_ref[...] = gg.astype(BF16)
        uu_ref[...] = uu.astype(BF16)
        act = (gg * jax.nn.sigmoid(gg) * uu).astype(BF16)
        y_ref[...] = x + 0.5 * _dot(act, wd_ref[...])

    s = x.shape[0]
    tile = pl.BlockSpec((tm, DM), lambda i: (i, 0))
    hid = pl.BlockSpec((tm, DFF), lambda i: (i, 0))
    w = _resident_weight()
    return pl.pallas_call(
        body, name=name,
        out_shape=(_sds((s, DM), F32), _sds((s, DM), BF16), _sds((s, DFF), BF16), _sds((s, DFF), BF16)),
        grid=(s // tm,), in_specs=[tile, pl.BlockSpec((1, DM), lambda i: (0, 0)), w, w, w],
        out_specs=(tile, tile, hid, hid), compiler_params=_params(("parallel",)))(x, gain, wg, wu, wd)


def ffn_bwd_tokens(dy, x, gain, gg, uu, wg, wu, wd, *, name, tm=256, dep=None):
    def body(dy_ref, x_ref, g_ref, gg_ref, uu_ref, wg_ref, wu_ref, wd_ref, *rest):
        dx_ref, dgain_ref, act_ref, dg_ref, du_ref = rest[-5:]

        @pl.when(pl.program_id(0) == 0)
        def _():
            dgain_ref[...] = jnp.zeros_like(dgain_ref)

        dy = dy_ref[...]
        dact = _dot((0.5 * dy).astype(BF16), wd_ref[...], NT)
        g = gg_ref[...].astype(F32)
        u = uu_ref[...].astype(F32)
        sg = jax.nn.sigmoid(g)
        silu = g * sg
        act_ref[...] = (silu * u).astype(BF16)
        dg = (dact * u * (sg * (1.0 + g * (1.0 - sg)))).astype(BF16)
        du = (dact * silu).astype(BF16)
        dg_ref[...] = dg
        du_ref[...] = du
        dx, dgain = _rms_bwd_math(_dot(dg, wg_ref[...]) + _dot(du, wu_ref[...]), x_ref[...], g_ref[...])
        dx_ref[...] = dy + dx
        dgain_ref[...] += dgain

    s = x.shape[0]
    tile = pl.BlockSpec((tm, DM), lambda i: (i, 0))
    vec = pl.BlockSpec((1, DM), lambda i: (0, 0))
    hid = pl.BlockSpec((tm, DFF), lambda i: (i, 0))
    hshape = _sds((s, DFF), BF16)
    w = _resident_weight()
    ins, specs = _with_dep([dy, x, gain, gg, uu, wg, wu, wd], [tile, tile, vec, hid, hid, w, w, w], dep)
    return pl.pallas_call(
        body, name=name, out_shape=(_sds((s, DM), F32), _sds((1, DM), F32), hshape, hshape, hshape),
        grid=(s // tm,), in_specs=specs, out_specs=(tile, vec, hid, hid, hid),
        compiler_params=_params(("arbitrary",)))(*ins)


def ffn_bwd_weights(h, dy, act, dg, du, *, name, tf=256):
    def body(h_ref, dy_ref, act_ref, dg_ref, du_ref, gwg_ref, gwu_ref, gwd_ref):
        h = h_ref[...]
        gwg_ref[...] = _dot(dg_ref[...], h, TN).astype(BF16)
        gwu_ref[...] = _dot(du_ref[...], h, TN).astype(BF16)
        gwd_ref[...] = (0.5 * _dot(act_ref[...], dy_ref[...], TN)).astype(BF16)

    s = h.shape[0]
    full = pl.BlockSpec((s, DM), lambda f: (0, 0))
    hid = pl.BlockSpec((s, tf), lambda f: (0, f))
    wt = pl.BlockSpec((tf, DM), lambda f: (f, 0))
    wshape = _sds((DFF, DM), BF16)
    return pl.pallas_call(
        body, name=name, out_shape=(wshape, wshape, wshape), grid=(DFF // tf,), in_specs=[full, full, hid, hid, hid],
        out_specs=(wt, wt, wt), compiler_params=_params(("parallel",)))(h, dy, act, dg, du)


def _group_mean(v, bd):
    hi = v.astype(BF16)
    lo = (v - hi.astype(F32)).astype(BF16)
    return _dot(hi, bd) + _dot(lo, bd)


def _block_diag(width):
    idx = np.arange(width) // HD
    return jnp.asarray((idx[:, None] == idx[None, :]).astype(np.float32) / HD, dtype=BF16)


def qknorm_fwd(z, gq_na, gk_na, gq_sw, gk_sw, *, name, tm=256):
    def body(zq_ref, zk_ref, zv_ref, zs_ref, zkv_ref, gqa_ref, gka_ref, gqs_ref, gks_ref, bd_ref, bd2_ref,
             qa_ref, ka_ref, va_ref, qs_ref, kv_ref):
        bd = bd_ref[...]

        def norm(x, g, bdm):
            return x * lax.rsqrt(_group_mean(x * x, bdm) + EPS) * g

        qa_ref[...] = (norm(zq_ref[...], gqa_ref[...], bd) * QK_SCALE).astype(BF16)
        ka_ref[...] = norm(zk_ref[...], gka_ref[...], bd).astype(BF16)
        va_ref[...] = zv_ref[...].astype(BF16)
        qs_ref[...] = (norm(zs_ref[...], gqs_ref[...], bd) * QK_SCALE).astype(BF16)
        kv = zkv_ref[...]
        kv_ref[:, 0:128] = norm(kv[:, 0:128], gks_ref[...], bd2_ref[...]).astype(BF16)
        kv_ref[:, 128:256] = kv[:, 128:256].astype(BF16)

    s = z.shape[0]
    col = lambda j: pl.BlockSpec((tm, 512), lambda i, j=j: (i, j))
    vec = lambda w: pl.BlockSpec((1, w), lambda i: (0, 0))
    o512 = pl.BlockSpec((tm, 512), lambda i: (i, 0))
    g512 = lambda g: jnp.tile(g.reshape(1, HD), (1, 8))
    return pl.pallas_call(
        body, name=name,
        out_shape=(_sds((s, 512), BF16),) * 4 + (_sds((s, 256), BF16),), grid=(s // tm,),
        in_specs=[col(0), col(1), col(2), col(3), pl.BlockSpec((tm, 256), lambda i: (i, 8)), vec(512), vec(512), vec(512),
                  vec(128), pl.BlockSpec((512, 512), lambda i: (0, 0)), pl.BlockSpec((128, 128), lambda i: (0, 0))],
        out_specs=(o512, o512, o512, o512, pl.BlockSpec((tm, 256), lambda i: (i, 0))),
        compiler_params=_params(("parallel",)))(
            z, z, z, z, z, g512(gq_na), g512(gk_na), g512(gq_sw), jnp.tile(gk_sw.reshape(1, HD), (1, 2)),
            _block_diag(512), _block_diag(128))


def qknorm_bwd(z, dqa, dka, dva, dqs, dkv, gq_na, gk_na, gq_sw, gk_sw, *, name, tm=256):
    def body(zq_ref, zk_ref, zs_ref, zkv_ref, dqa_ref, dka_ref, dva_ref, dqs_ref, dkv_ref, gqa_ref, gka_ref, gqs_ref,
             gks_ref, bd_ref, bd2_ref, dz_ref, dgqa_ref, dgka_ref, dgqs_ref, dgks_ref):
        @pl.when(pl.program_id(0) == 0)
        def _():
            dgqa_ref[...] = jnp.zeros_like(dgqa_ref)
            dgka_ref[...] = jnp.zeros_like(dgka_ref)
            dgqs_ref[...] = jnp.zeros_like(dgqs_ref)
            dgks_ref[...] = jnp.zeros_like(dgks_ref)

        bd = bd_ref[...]

        def bwd(x, dy, g, bdm, dg_ref):
            r = lax.rsqrt(_group_mean(x * x, bdm) + EPS)
            xh = x * r
            dg_ref[...] += jnp.sum(dy * xh, axis=0, keepdims=True)
            dxn = dy * g
            return r * (dxn - xh * _group_mean(dxn * xh, bdm))

        dz_ref[:, 0:512] = bwd(zq_ref[...], dqa_ref[...] * QK_SCALE, gqa_ref[...], bd, dgqa_ref).astype(BF16)
        dz_ref[:, 512:1024] = bwd(zk_ref[...], dka_ref[...], gka_ref[...], bd, dgka_ref).astype(BF16)
        dz_ref[:, 1024:1536] = dva_ref[...].astype(BF16)
        dz_ref[:, 1536:2048] = bwd(zs_ref[...], dqs_ref[...] * QK_SCALE, gqs_ref[...], bd, dgqs_ref).astype(BF16)
        dkv = dkv_ref[...]
        dz_ref[:, 2048:2176] = bwd(zkv_ref[:, 0:128], dkv[:, 0:128], gks_ref[...], bd2_ref[...], dgks_ref).astype(BF16)
        dz_ref[:, 2176:2304] = dkv[:, 128:256].astype(BF16)

    s = z.shape[0]
    col = lambda j: pl.BlockSpec((tm, 512), lambda i, j=j: (i, j))
    t512 = pl.BlockSpec((tm, 512), lambda i: (i, 0))
    t256 = pl.BlockSpec((tm, 256), lambda i: (i, 0))
    vec = lambda w: pl.BlockSpec((1, w), lambda i: (0, 0))
    g512 = lambda g: jnp.tile(g.reshape(1, HD), (1, 8))
    return pl.pallas_call(
        body, name=name,
        out_shape=(_sds((s, ATT_W), BF16), _sds((1, 512), F32), _sds((1, 512), F32), _sds((1, 512), F32), _sds((1, 128), F32)),
        grid=(s // tm,),
        in_specs=[col(0), col(1), col(3), pl.BlockSpec((tm, 256), lambda i: (i, 8)), t512, t512, t512, t512, t256,
                  vec(512), vec(512), vec(512), vec(128), pl.BlockSpec((512, 512), lambda i: (0, 0)),
                  pl.BlockSpec((128, 128), lambda i: (0, 0))],
        out_specs=(pl.BlockSpec((tm, ATT_W), lambda i: (i, 0)), vec(512), vec(512), vec(512), vec(128)),
        compiler_params=_params(("arbitrary",)))(
            z, z, z, z, dqa, dka, dva, dqs, dkv, g512(gq_na), g512(gk_na), g512(gq_sw),
            jnp.tile(gk_sw.reshape(1, HD), (1, 2)), _block_diag(512), _block_diag(128))


def _na_row_start(r):
    return jnp.clip(r - NA_WR // 2, 0, ROWS - NA_WR)


def na_bias_table(rpb, *, name):
    t = jnp.pad(rpb, ((0, 0), (0, 2), (0, HD - (2 * NA_WC - 1))))
    pairs = jnp.concatenate([t[:, :16], t[:, 1:17]], axis=-1).reshape(NA_HEADS, 16, 1, 128)

    def body(t_ref, o_ref):
        p = pl.program_id(0)
        q = lax.broadcasted_iota(jnp.int32, (GRID_W, 128), 0)
        kc = lax.broadcasted_iota(jnp.int32, (GRID_W, 128), 1) & (GRID_W - 1)
        cs = jnp.clip(q - NA_WC // 2, 0, GRID_W - NA_WC)
        ok = (kc >= cs) & (kc < cs + NA_WC)
        for h in range(NA_HEADS):
            for pr in range(NA_WR // 2):
                x = jnp.broadcast_to(t_ref[h, 2 * pr - p + NA_WR - 1], (GRID_W, 128))
                b = pltpu.roll(x, 128 - (NA_WC - 1), 1, stride=1, stride_axis=0)
                o_ref[h, :, 128 * pr:128 * pr + 128] = jnp.where(ok, b, NEG)

    return pl.pallas_call(
        body, name=name, out_shape=_sds((NA_WR, NA_HEADS, GRID_W, NA_KEYS), F32), grid=(NA_WR,),
        in_specs=[pl.BlockSpec((NA_HEADS, 16, 1, 128), lambda p: (0, 0, 0, 0))],
        out_specs=pl.BlockSpec((None, NA_HEADS, GRID_W, NA_KEYS), lambda p: (p, 0, 0, 0)),
        compiler_params=_params(("parallel",)))(pairs)


def _lane_halves():
    lane = lax.broadcasted_iota(jnp.int32, (1, 128), 1)
    return lane < HD


def na_fwd(q, k, v, bias, *, name):
    def body(q_ref, k_ref, v_ref, b_ref, o_ref, lse_ref):
        r = pl.program_id(0)
        off = pl.multiple_of(_na_row_start(r) * GRID_W, GRID_W)
        first = _lane_halves()
        sels = [first, jnp.logical_not(first)]
        lanes = [slice(128 * j, 128 * j + 128) for j in range(NA_HEADS // 2)]
        q2s = [q_ref[:, l] for l in lanes]
        k2s = [k_ref[pl.ds(off, NA_KEYS), l] for l in lanes]
        v2s = [v_ref[pl.ds(off, NA_KEYS), l] for l in lanes]
        scores = []
        for h in range(NA_HEADS):
            j, half = divmod(h, 2)
            scores.append(_dot(jnp.where(sels[half], q2s[j], jnp.zeros_like(q2s[j])), k2s[j], NT))
        probs, lses = [], []
        for h in range(NA_HEADS):
            b = b_ref[h]
            s = jnp.where(b > 0.5 * NEG, scores[h] + b, NEG)
            m = jnp.max(s, axis=-1, keepdims=True)
            e = jnp.exp(s - m)
            l = jnp.sum(e, axis=-1, keepdims=True)
            probs.append((e / l).astype(BF16))
            lses.append(m + jnp.log(l))
        for j in range(NA_HEADS // 2):
            zero = jnp.zeros_like(v2s[j])
            o2 = (_dot(probs[2 * j], jnp.where(sels[0], v2s[j], zero))
                  + _dot(probs[2 * j + 1], jnp.where(sels[1], v2s[j], zero)))
            o_ref[:, lanes[j]] = o2.astype(BF16)
        lse_ref[...] = jnp.concatenate(lses, axis=1)

    s_tok = q.shape[0]
    full = pl.BlockSpec((s_tok, 512), lambda r: (0, 0))
    return pl.pallas_call(
        body, name=name, out_shape=(_sds((s_tok, 512), BF16), _sds((s_tok, NA_HEADS), F32)), grid=(ROWS,),
        in_specs=[pl.BlockSpec((GRID_W, 512), lambda r: (r, 0)), full, full,
                  pl.BlockSpec((None, NA_HEADS, GRID_W, NA_KEYS), lambda r: (r - _na_row_start(r), 0, 0, 0))],
        out_specs=(pl.BlockSpec((GRID_W, 512), lambda r: (r, 0)), pl.BlockSpec((GRID_W, NA_HEADS), lambda r: (r, 0))),
        compiler_params=_params(("parallel",)))(q, k, v, bias)


def na_bwd(q, k, v, o, do, lse, bias, *, name):
    def body(q_ref, k_ref, v_ref, o_ref, do_ref, lse_ref, b_ref, dq_ref, dk_ref, dv_ref, db_ref):
        r = pl.program_id(0)

        @pl.when(r == 0)
        def _():
            dk_ref[...] = jnp.zeros_like(dk_ref)
            dv_ref[...] = jnp.zeros_like(dv_ref)

        @pl.when((r <= NA_WR // 2) | (r > ROWS - NA_WR // 2))
        def _():
            db_ref[...] = jnp.zeros_like(db_ref)

        off = pl.multiple_of(_na_row_start(r) * GRID_W, GRID_W)
        first = _lane_halves()
        sels = [first, jnp.logical_not(first)]
        lanes = [slice(128 * j, 128 * j + 128) for j in range(NA_HEADS // 2)]
        q2s = [q_ref[:, l] for l in lanes]
        k2s = [k_ref[pl.ds(off, NA_KEYS), l] for l in lanes]
        v2s = [v_ref[pl.ds(off, NA_KEYS), l] for l in lanes]
        do2s = [do_ref[:, l] for l in lanes]
        prods = [do2s[j].astype(F32) * o_ref[:, lanes[j]].astype(F32) for j in range(NA_HEADS // 2)]
        lse = lse_ref[...]
        qhs, dohs, scores, dps = [], [], [], []
        for h in range(NA_HEADS):
            j, half = divmod(h, 2)
            qhs.append(jnp.where(sels[half], q2s[j], jnp.zeros_like(q2s[j])))
            dohs.append(jnp.where(sels[half], do2s[j], jnp.zeros_like(do2s[j])))
            scores.append(_dot(qhs[h], k2s[j], NT))
            dps.append(_dot(dohs[h], v2s[j], NT))
        pbs, dsbs = [], []
        for h in range(NA_HEADS):
            j, half = divmod(h, 2)
            b = b_ref[h]
            s = jnp.where(b > 0.5 * NEG, scores[h] + b, NEG)
            p = jnp.exp(s - lse[:, h:h + 1])
            delta = jnp.sum(jnp.where(sels[half], prods[j], 0.0), axis=-1, keepdims=True)
            ds = p * (dps[h] - delta)
            db_ref[h] += ds
            pbs.append(p.astype(BF16))
            dsbs.append(ds.astype(BF16))
        for j in range(NA_HEADS // 2):
            a, b = 2 * j, 2 * j + 1
            zero = jnp.zeros_like(k2s[j])
            dq_ref[:, lanes[j]] = (_dot(dsbs[a], jnp.where(sels[0], k2s[j], zero))
                                   + _dot(dsbs[b], jnp.where(sels[1], k2s[j], zero)))
            dk_ref[pl.ds(off, NA_KEYS), lanes[j]] += _dot(dsbs[a], qhs[a], TN) + _dot(dsbs[b], qhs[b], TN)
            dv_ref[pl.ds(off, NA_KEYS), lanes[j]] += _dot(pbs[a], dohs[a], TN) + _dot(pbs[b], dohs[b], TN)

    s_tok = q.shape[0]
    full = pl.BlockSpec((s_tok, 512), lambda r: (0, 0))
    row = pl.BlockSpec((GRID_W, 512), lambda r: (r, 0))
    bias_spec = pl.BlockSpec((None, NA_HEADS, GRID_W, NA_KEYS), lambda r: (r - _na_row_start(r), 0, 0, 0))
    return pl.pallas_call(
        body, name=name,
        out_shape=(_sds((s_tok, 512), F32), _sds((s_tok, 512), F32), _sds((s_tok, 512), F32),
                   _sds((NA_WR, NA_HEADS, GRID_W, NA_KEYS), F32)),
        grid=(ROWS,),
        in_specs=[row, full, full, row, row, pl.BlockSpec((GRID_W, NA_HEADS), lambda r: (r, 0)), bias_spec],
        out_specs=(row, full, full, bias_spec), compiler_params=_params(("arbitrary",)))(q, k, v, o, do, lse, bias)


def t5_bucket_map():
    rel = np.arange(SW_KEYS)[None, :] - SW_BLK - np.arange(SW_BLK)[:, None]
    nb = 16
    max_exact = nb // 2
    n = np.abs(rel)
    large = max_exact + (np.log(np.maximum(n, 1) / max_exact) / np.log(128 / max_exact) * (nb - max_exact)).astype(np.int32)
    large = np.minimum(large, nb - 1)
    return ((rel > 0) * nb + np.where(n < max_exact, n, large)).astype(np.int32)


def t5_bias(table, *, name):
    rel = np.arange(-SW_BLK, SW_BLK + 1)
    nb, max_exact = 16, 8
    n = np.abs(rel)
    large = max_exact + (np.log(np.maximum(n, 1) / max_exact) / np.log(128 / max_exact) * (nb - max_exact)).astype(np.int32)
    bucket = ((rel > 0) * nb + np.where(n < max_exact, n, np.minimum(large, nb - 1))).astype(np.int32)
    u = jnp.pad(table[jnp.asarray(bucket)].T, ((0, 0), (0, SW_KEYS - bucket.shape[0]))).reshape(8, 1, SW_KEYS)

    def body(u_ref, o_ref):
        for h in range(8):
            x = jnp.broadcast_to(u_ref[h], (SW_BLK, SW_KEYS))
            o_ref[h] = pltpu.roll(x, 0, 1, stride=1, stride_axis=0)

    return pl.pallas_call(body, name=name, out_shape=_sds((8, SW_BLK, SW_KEYS), F32), compiler_params=_params())(u)


def _sw_valid(n):
    a = lax.broadcasted_iota(jnp.int32, (SW_BLK, SW_KEYS), 0)
    j = lax.broadcasted_iota(jnp.int32, (SW_BLK, SW_KEYS), 1)
    kpos = (n - 1) * SW_BLK + j
    return (jnp.abs(j - SW_BLK - a) <= SW_BLK) & (kpos >= 0) & (kpos < SEQ)


def _dup_group(x2, g, first):
    rolled = pltpu.roll(x2, HD, 1)
    return jnp.where(first, x2, rolled) if g == 0 else jnp.where(first, rolled, x2)


def sw_fwd(q, kv, t5, sink, *, name):
    def body(q_ref, kv_ref, t5_ref, sink_ref, o_ref, lse_ref):
        n = pl.program_id(0)
        off = pl.multiple_of(n * SW_BLK, SW_BLK)
        first = _lane_halves()
        valid = _sw_valid(n)
        k2 = kv_ref[pl.ds(off, SW_KEYS), 0:128]
        v2 = kv_ref[pl.ds(off, SW_KEYS), 128:256]
        for j in range(4):
            g = j // 2
            kk = _dup_group(k2, g, first)
            vv = _dup_group(v2, g, first)
            lanes = slice(128 * j, 128 * j + 128)
            q2 = q_ref[:, lanes]
            o2 = jnp.zeros((SW_BLK, 128), F32)
            for half in range(2):
                h = 2 * j + half
                sel = first if half == 0 else jnp.logical_not(first)
                s = _dot(jnp.where(sel, q2, jnp.zeros_like(q2)), kk, NT)
                s = jnp.where(valid, s + t5_ref[h], NEG)
                snk = sink_ref[:, h:h + 1]
                m = jnp.maximum(jnp.max(s, axis=-1, keepdims=True), snk)
                e = jnp.exp(s - m)
                den = jnp.sum(e, axis=-1, keepdims=True) + jnp.exp(snk - m)
                p = (e / den).astype(BF16)
                o2 = o2 + _dot(p, jnp.where(sel, vv, jnp.zeros_like(vv)))
                lse_ref[:, h:h + 1] = m + jnp.log(den)
            o_ref[:, lanes] = o2.astype(BF16)

    s_tok = q.shape[0]
    blk = pl.BlockSpec((SW_BLK, 512), lambda n: (n, 0))
    return pl.pallas_call(
        body, name=name, out_shape=(_sds((s_tok, 512), BF16), _sds((s_tok, 8), F32)), grid=(SW_NB,),
        in_specs=[blk, pl.BlockSpec(kv.shape, lambda n: (0, 0)), pl.BlockSpec((8, SW_BLK, SW_KEYS), lambda n: (0, 0, 0)),
                  pl.BlockSpec((1, 8), lambda n: (0, 0))],
        out_specs=(blk, pl.BlockSpec((SW_BLK, 8), lambda n: (n, 0))), compiler_params=_params(("parallel",)))(q, kv, t5, sink)


def sw_bwd(q, kv, o, do, lse, t5, sink, *, name):
    def body(q_ref, kv_ref, o_ref, do_ref, lse_ref, t5_ref, sink_ref, dq_ref, dkv_ref, dt5_ref, dsink_ref):
        n = pl.program_id(0)

        @pl.when(n == 0)
        def _():
            dkv_ref[...] = jnp.zeros_like(dkv_ref)
            dt5_ref[...] = jnp.zeros_like(dt5_ref)
            dsink_ref[...] = jnp.zeros_like(dsink_ref)

        off = pl.multiple_of(n * SW_BLK, SW_BLK)
        first = _lane_halves()
        sels = [first, jnp.logical_not(first)]
        valid = _sw_valid(n)
        k2 = kv_ref[pl.ds(off, SW_KEYS), 0:128]
        v2 = kv_ref[pl.ds(off, SW_KEYS), 128:256]
        kk = [_dup_group(k2, g, first) for g in range(2)]
        vv = [_dup_group(v2, g, first) for g in range(2)]
        lanes = [slice(128 * j, 128 * j + 128) for j in range(4)]
        q2s = [q_ref[:, l] for l in lanes]
        do2s = [do_ref[:, l] for l in lanes]
        prods = [do2s[j].astype(F32) * o_ref[:, lanes[j]].astype(F32) for j in range(4)]
        lse = lse_ref[...]
        qhs, dohs, scores, dps = [], [], [], []
        for h in range(8):
            j, half = divmod(h, 2)
            qhs.append(jnp.where(sels[half], q2s[j], jnp.zeros_like(q2s[j])))
            dohs.append(jnp.where(sels[half], do2s[j], jnp.zeros_like(do2s[j])))
            scores.append(_dot(qhs[h], kk[j // 2], NT))
            dps.append(_dot(dohs[h], vv[j // 2], NT))
        pbs, dsbs, dss, dsinks = [], [], [], []
        for h in range(8):
            j, half = divmod(h, 2)
            s = jnp.where(valid, scores[h] + t5_ref[h], NEG)
            lse_h = lse[:, h:h + 1]
            p = jnp.exp(s - lse_h)
            delta = jnp.sum(jnp.where(sels[half], prods[j], 0.0), axis=-1, keepdims=True)
            ds = p * (dps[h] - delta)
            dss.append(ds)
            dsinks.append(-jnp.sum(jnp.exp(sink_ref[:, h:h + 1] - lse_h) * delta, axis=0, keepdims=True))
            pbs.append(p.astype(BF16))
            dsbs.append(ds.astype(BF16))
        dt5_ref[...] += jnp.stack(dss)
        dsink_ref[...] += jnp.concatenate(dsinks, axis=1)
        dqs = []
        for j in range(4):
            a, b = 2 * j, 2 * j + 1
            zero = jnp.zeros_like(kk[j // 2])
            dqs.append(_dot(dsbs[a], jnp.where(sels[0], kk[j // 2], zero)) + _dot(dsbs[b], jnp.where(sels[1], kk[j // 2], zero)))
        dq_ref[...] = jnp.concatenate(dqs, axis=1)
        dk_groups, dv_groups = [], []
        for g in range(2):
            dkk = sum(_dot(dsbs[h], qhs[h], TN) for h in range(4 * g, 4 * g + 4))
            dvv = sum(_dot(pbs[h], dohs[h], TN) for h in range(4 * g, 4 * g + 4))
            dk_groups.append(dkk + pltpu.roll(dkk, HD, 1))
            dv_groups.append(dvv + pltpu.roll(dvv, HD, 1))
        dkv_ref[pl.ds(off, SW_KEYS), :] += jnp.concatenate(
            [jnp.where(first, dk_groups[0], dk_groups[1]), jnp.where(first, dv_groups[0], dv_groups[1])], axis=1)

    s_tok = q.shape[0]
    blk = pl.BlockSpec((SW_BLK, 512), lambda n: (n, 0))
    kv_spec = pl.BlockSpec(kv.shape, lambda n: (0, 0))
    t5_spec = pl.BlockSpec((8, SW_BLK, SW_KEYS), lambda n: (0, 0, 0))
    vec = pl.BlockSpec((1, 8), lambda n: (0, 0))
    return pl.pallas_call(
        body, name=name,
        out_shape=(_sds((s_tok, 512), F32), _sds(kv.shape, F32), _sds((8, SW_BLK, SW_KEYS), F32), _sds((1, 8), F32)),
        grid=(SW_NB,), in_specs=[blk, kv_spec, blk, blk, pl.BlockSpec((SW_BLK, 8), lambda n: (n, 0)), t5_spec, vec],
        out_specs=(blk, kv_spec, t5_spec, vec), compiler_params=_params(("arbitrary",)))(q, kv, o, do, lse, t5, sink)


def gate_fwd(zg, bias, pa, ps, *, name, tm=512):
    def body(z0_ref, z1_ref, b0_ref, b1_ref, pa_ref, ps_ref, m_ref):
        g0 = jax.nn.sigmoid(z0_ref[...] + b0_ref[...])
        g1 = jax.nn.sigmoid(z1_ref[...] + b1_ref[...])
        m_ref[...] = (g0 * pa_ref[...] + g1 * ps_ref[...]).astype(BF16)

    s = zg.shape[0]
    half = lambda j: pl.BlockSpec((tm, DM), lambda i, j=j: (i, j))
    bvec = lambda j: pl.BlockSpec((1, DM), lambda i, j=j: (0, j))
    return pl.pallas_call(
        body, name=name, out_shape=_sds((s, DM), BF16), grid=(s // tm,),
        in_specs=[half(0), half(1), bvec(0), bvec(1), half(0), half(0)], out_specs=half(0),
        compiler_params=_params(("parallel",)))(zg, zg, bias, bias, pa, ps)


def gate_bwd(dm, zg, bias, pa, ps, *, name, tm=512):
    def body(dm_ref, z0_ref, z1_ref, b0_ref, b1_ref, pa_ref, ps_ref, dpa_ref, dps_ref, dz_ref, db_ref):
        @pl.when(pl.program_id(0) == 0)
        def _():
            db_ref[...] = jnp.zeros_like(db_ref)

        dm = dm_ref[...]
        g0 = jax.nn.sigmoid(z0_ref[...] + b0_ref[...])
        g1 = jax.nn.sigmoid(z1_ref[...] + b1_ref[...])
        dpa_ref[...] = (dm * g0).astype(BF16)
        dps_ref[...] = (dm * g1).astype(BF16)
        dz0 = dm * pa_ref[...] * g0 * (1.0 - g0)
        dz1 = dm * ps_ref[...] * g1 * (1.0 - g1)
        dz_ref[:, 0:DM] = dz0.astype(BF16)
        dz_ref[:, DM:2 * DM] = dz1.astype(BF16)
        db_ref[:, 0:DM] += jnp.sum(dz0, axis=0, keepdims=True)
        db_ref[:, DM:2 * DM] += jnp.sum(dz1, axis=0, keepdims=True)

    s = zg.shape[0]
    half = lambda j: pl.BlockSpec((tm, DM), lambda i, j=j: (i, j))
    bvec = lambda j: pl.BlockSpec((1, DM), lambda i, j=j: (0, j))
    return pl.pallas_call(
        body, name=name,
        out_shape=(_sds((s, DM), BF16), _sds((s, DM), BF16), _sds((s, GATE_W), BF16), _sds((1, GATE_W), F32)),
        grid=(s // tm,), in_specs=[half(0), half(0), half(1), bvec(0), bvec(1), half(0), half(0)],
        out_specs=(half(0), half(0), pl.BlockSpec((tm, GATE_W), lambda i: (i, 0)), pl.BlockSpec((1, GATE_W), lambda i: (0, 0))),
        compiler_params=_params(("arbitrary",)))(dm, zg, zg, bias, bias, pa, ps)


def loss_head(y, target, *, name, tm=512):
    def body(y_ref, t_ref, dy_ref, l_ref):
        @pl.when(pl.program_id(0) == 0)
        def _():
            l_ref[...] = jnp.zeros_like(l_ref)

        err = y_ref[...] - t_ref[...]
        dy_ref[...] = err * (1.0 / DM)
        l_ref[...] += 0.5 * jnp.sum(jnp.mean(err * err, axis=-1, keepdims=True), axis=0, keepdims=True)

    s = y.shape[0]
    tile = pl.BlockSpec((tm, DM), lambda i: (i, 0))
    return pl.pallas_call(
        body, name=name, out_shape=(_sds((s, DM), F32), _sds((1, 128), F32)), grid=(s // tm,), in_specs=[tile, tile],
        out_specs=(tile, pl.BlockSpec((1, 128), lambda i: (0, 0))), compiler_params=_params(("arbitrary",)))(y, target)


def adamw(w, g, m, v, *, name):
    def body(w_ref, g_ref, m_ref, v_ref, d_ref, nm_ref, nv_ref):
        g = g_ref[...]
        nm = ADAM_B1 * m_ref[...] + (1.0 - ADAM_B1) * g
        nv = ADAM_B2 * v_ref[...] + (1.0 - ADAM_B2) * jnp.square(g)
        m_hat = nm / (1.0 - ADAM_B1 ** ADAM_STEP)
        v_hat = nv / (1.0 - ADAM_B2 ** ADAM_STEP)
        d_ref[...] = -ADAM_LR * (m_hat / (jnp.sqrt(v_hat) + ADAM_EPS) + ADAM_WD * w_ref[...])
        nm_ref[...] = nm
        nv_ref[...] = nv

    b, k, n = w.shape
    tk = k // 4 if k % 32 == 0 else k
    spec = pl.BlockSpec((None, tk, n), lambda i, j: (i, j, 0))
    out = _sds(w.shape, F32)
    return pl.pallas_call(
        body, name=name, out_shape=(out, out, out), grid=(b, k // tk), in_specs=[spec] * 4, out_specs=(spec,) * 3,
        compiler_params=_params(("parallel", "parallel")))(w, g, m, v)


def adamw_layer(w, m, v, mine, theirs, cidx, layer, filled=None, *, name):
    _, k, n = w.shape
    nt = 2
    tk = k // 2 // nt

    def body(c_ref, w_ref, m_ref, v_ref, a_ref, b_ref, *rest):
        g_ref, d_ref, nm_ref, nv_ref = rest[-4:]
        g = jnp.where(pl.program_id(0) == c_ref[0], a_ref[...], b_ref[...])
        g_ref[...] = g
        nm = ADAM_B1 * m_ref[...] + (1.0 - ADAM_B1) * g
        nv = ADAM_B2 * v_ref[...] + (1.0 - ADAM_B2) * jnp.square(g)
        m_hat = nm / (1.0 - ADAM_B1 ** ADAM_STEP)
        v_hat = nv / (1.0 - ADAM_B2 ** ADAM_STEP)
        d_ref[...] = -ADAM_LR * (m_hat / (jnp.sqrt(v_hat) + ADAM_EPS) + ADAM_WD * w_ref[...])
        nm_ref[...] = nm
        nv_ref[...] = nv

    full = pl.BlockSpec((None, tk, n), lambda hf, t, c: (layer, hf * nt + t, 0))
    half_mine = pl.BlockSpec((tk, n), lambda hf, t, c: (jnp.where(hf == c[0], t, 0), 0))
    half_theirs = pl.BlockSpec((tk, n), lambda hf, t, c: (jnp.where(hf != c[0], t, 0), 0))
    out = _sds(w.shape, F32)
    ins, specs, aliases = [cidx, w, m, v, mine, theirs], [full, full, full, half_mine, half_theirs], {}
    if filled is not None:
        aliases = {len(ins) + i: i for i in range(4)}
        ins += list(filled)
        specs += [pl.BlockSpec(memory_space=pl.ANY)] * 4
    return pl.pallas_call(
        body, name=name, out_shape=(out, out, out, out),
        grid_spec=pltpu.PrefetchScalarGridSpec(
            num_scalar_prefetch=1, grid=(2, nt), in_specs=specs, out_specs=(full, full, full, full)),
        input_output_aliases=aliases,
        compiler_params=_params(("arbitrary", "arbitrary")))(*ins)


def t5_table_grad(dt5_a, dt5_b, *, name):
    def body(a_ref, b_ref, map_ref, o_ref):
        d = a_ref[...] + b_ref[...]
        bucket = map_ref[...]
        for b in range(32):
            hit = (bucket == b)[None]
            o_ref[b] = jnp.sum(jnp.sum(jnp.where(hit, d, 0.0), axis=2), axis=1, keepdims=True)

    return pl.pallas_call(
        body, name=name, out_shape=_sds((32, 8, 1), F32), compiler_params=_params())(
            dt5_a, dt5_b, jnp.asarray(t5_bucket_map()))


def rpb_grad(dbias, *, name):
    def body(d_ref, rev_ref, o_ref):
        rev = rev_ref[...]
        for h in range(NA_HEADS):
            for pr in range(NA_WR // 2):
                d = d_ref[h, :, 128 * pr:128 * pr + 128]
                hi = d.astype(BF16)
                lo = (d - hi.astype(F32)).astype(BF16)
                flipped = _dot(rev, hi) + _dot(rev, lo)
                o_ref[h, pr] = jnp.sum(pltpu.roll(flipped, 0, 1, stride=1, stride_axis=0), axis=0, keepdims=True)

    anti = jnp.asarray(np.eye(GRID_W, dtype=np.float32)[::-1], dtype=BF16)
    e = pl.pallas_call(
        body, name=name, out_shape=_sds((NA_WR, NA_HEADS, NA_WR // 2, 1, 128), F32), grid=(NA_WR,),
        in_specs=[pl.BlockSpec((None, NA_HEADS, GRID_W, NA_KEYS), lambda p: (p, 0, 0, 0)),
                  pl.BlockSpec((GRID_W, GRID_W), lambda p: (0, 0))],
        out_specs=pl.BlockSpec((None, NA_HEADS, NA_WR // 2, 1, 128), lambda p: (p, 0, 0, 0, 0)),
        compiler_params=_params(("parallel",)))(dbias, anti)
    nci, nri = 2 * NA_WC - 1, 2 * NA_WR - 1
    e = e.reshape(NA_WR, NA_HEADS, NA_WR // 2, 128).transpose(0, 2, 1, 3).reshape(NA_WR * NA_WR // 2, NA_HEADS, 128)
    parts = jnp.concatenate([e[..., 48:48 + nci], jnp.concatenate([e[..., 112:128], e[..., 0:nci - 16]], axis=-1)], axis=0)
    p, pr = np.arange(NA_WR)[:, None], np.arange(NA_WR // 2)[None, :]
    ri = np.concatenate([(2 * pr - p + NA_WR - 1).reshape(-1), (2 * pr - p + NA_WR).reshape(-1)])
    pick = jnp.asarray((ri[None, :] == np.arange(16)[:, None]).astype(np.float32))
    out = mm(pick, parts.reshape(2 * NA_WR * NA_WR // 2, NA_HEADS * nci), name=name + "_rows", exact=True)
    return out.reshape(16, NA_HEADS, nci)[:nri].transpose(1, 0, 2)


BIG = ("ffn1_w_gate", "ffn1_w_up", "ffn1_w_down", "w_in", "w_branch_na", "w_branch_sw", "w_out",
       "ffn2_w_gate", "ffn2_w_up", "ffn2_w_down")
SMALL = ("ffn1_norm", "mix_norm", "b_gate", "na_q_norm", "na_k_norm", "na_rpb", "sw_q_norm", "sw_k_norm", "sw_sink",
         "ffn2_norm")


def _cols_to_full(w4):
    return w4.transpose(1, 0, 2).reshape(w4.shape[1], NSH * w4.shape[2])


def _full_to_cols(w):
    return w.reshape(w.shape[0], NSH, w.shape[1] // NSH).transpose(1, 0, 2)


def _mixer_weights(g):
    w_in_t = g["w_in"].reshape(IN_W, DM)
    return dict(w_att_t=w_in_t[:ATT_W], w_gz_t=w_in_t[ATT_W:], wa=_cols_to_full(g["w_branch_na"]),
                ws=_cols_to_full(g["w_branch_sw"]), wo=g["w_out"].reshape(DM, DM))


GROUPS = {"ffn1": ("ffn1_w_gate", "ffn1_w_up", "ffn1_w_down"), "mix": ("w_in", "w_branch_na", "w_branch_sw", "w_out"),
          "ffn2": ("ffn2_w_gate", "ffn2_w_up", "ffn2_w_down")}


def layer_fwd(x, p, weights, t5b):
    row = lambda v: v.reshape(1, -1)
    stacked = lambda g: {n: a.reshape(DFF, DM) for n, a in g.items()}
    g1 = stacked(weights("ffn1", x))
    y1, h1, gg1, uu1 = ffn_fwd(x, row(p["ffn1_norm"]), g1["ffn1_w_gate"], g1["ffn1_w_up"], g1["ffn1_w_down"], name="ffn_fwd")
    w = _mixer_weights(weights("mix", y1))
    hm = rms_fwd(y1, row(p["mix_norm"]), name="mix_norm_fwd")
    z = mm(hm, w["w_att_t"], tb=True, name="proj_att", tm=SEQ, tn=768)
    zg = mm(hm, w["w_gz_t"], tb=True, name="proj_gate", tm=SEQ, tn=512)
    qa, ka, va, qs, kv = qknorm_fwd(z, p["na_q_norm"], p["na_k_norm"], p["sw_q_norm"], p["sw_k_norm"], name="qknorm_fwd")
    bias = na_bias_table(p["na_rpb"], name="na_bias_table")
    o_na, lse_na = na_fwd(qa, ka, va, bias, name="na_fwd")
    kvp = jnp.pad(kv, ((SW_BLK, SW_BLK), (0, 0)))
    sink = row(p["sw_sink"])
    o_sw, lse_sw = sw_fwd(qs, kvp, t5b, sink, name="sw_fwd")
    pa = mm(o_na, w["wa"], name="branch_na", tm=1024)
    ps = mm(o_sw, w["ws"], name="branch_sw", tm=1024)
    merged = gate_fwd(zg, row(p["b_gate"]), pa, ps, name="gate_fwd")
    y2 = mm(merged, w["wo"], add=y1, name="out_proj", tm=1024)
    g2 = stacked(weights("ffn2", y2))
    y3, h2, gg2, uu2 = ffn_fwd(y2, row(p["ffn2_norm"]), g2["ffn2_w_gate"], g2["ffn2_w_up"], g2["ffn2_w_down"], name="ffn_fwd")
    saved = dict(x=x, y1=y1, h1=h1, gg1=gg1, uu1=uu1, hm=hm, z=z, zg=zg, qa=qa, ka=ka, va=va, qs=qs, kvp=kvp, bias=bias,
                 o_na=o_na, lse_na=lse_na, o_sw=o_sw, lse_sw=lse_sw, pa=pa, ps=ps, merged=merged, y2=y2, h2=h2, gg2=gg2,
                 uu2=uu2, w=w, sink=sink, g1=g1, g2=g2)
    return y3, saved


def layer_bwd(dy3, sv, p, t5b, emit, dep=None):
    w, g1, g2 = sv["w"], sv["g1"], sv["g2"]
    row = lambda v: v.reshape(1, -1)
    fold = lambda v: v.reshape(-1, HD).sum(axis=0)
    small = {}
    dy2, small["ffn2_norm"], act, dg, du = ffn_bwd_tokens(
        dy3, sv["y2"], row(p["ffn2_norm"]), sv["gg2"], sv["uu2"], g2["ffn2_w_gate"], g2["ffn2_w_up"], g2["ffn2_w_down"],
        name="ffn_bwd_tokens", dep=dep)
    shards = lambda gs: [g.reshape(NSH, FSH, DM) for g in gs]
    token = emit("ffn2", shards(ffn_bwd_weights(sv["h2"], dy3.astype(BF16), act, dg, du, name="ffn_bwd_weights")))
    dmerged = mm(dy2, w["wo"], tb=True, name="out_proj_dx", tm=1024, dep=token)
    gw_out = mm(sv["merged"], dy2, ta=True, out_dtype=BF16, name="out_proj_dw").reshape(NSH, DM // NSH, DM)
    dpa, dps, dzg, small["b_gate"] = gate_bwd(dmerged, sv["zg"], row(p["b_gate"]), sv["pa"], sv["ps"], name="gate_bwd")
    gw_na = _full_to_cols(mm(sv["o_na"], dpa, ta=True, out_dtype=BF16, name="branch_dw"))
    gw_sw = _full_to_cols(mm(sv["o_sw"], dps, ta=True, out_dtype=BF16, name="branch_dw"))
    do_na = mm(dpa, w["wa"], tb=True, out_dtype=BF16, tm=SEQ, name="branch_dx")
    do_sw = mm(dps, w["ws"], tb=True, out_dtype=BF16, tm=SEQ, name="branch_dx")
    dqa, dka, dva, dbias = na_bwd(sv["qa"], sv["ka"], sv["va"], sv["o_na"], do_na, sv["lse_na"], sv["bias"], name="na_bwd")
    dqs, dkvp, dt5, dsink = sw_bwd(sv["qs"], sv["kvp"], sv["o_sw"], do_sw, sv["lse_sw"], t5b, sv["sink"], name="sw_bwd")
    dkv = dkvp[SW_BLK:SW_BLK + SEQ]
    dz, dgqa, dgka, dgqs, dgks = qknorm_bwd(sv["z"], dqa, dka, dva, dqs, dkv, p["na_q_norm"], p["na_k_norm"],
                                            p["sw_q_norm"], p["sw_k_norm"], name="qknorm_bwd")
    small["na_q_norm"], small["na_k_norm"], small["sw_q_norm"], small["sw_k_norm"] = fold(dgqa), fold(dgka), fold(dgqs), fold(dgks)
    small["na_rpb"] = rpb_grad(dbias, name="rpb_grad")
    small["sw_sink"] = dsink
    gw_att_t = mm(dz, sv["hm"], ta=True, out_dtype=BF16, tm=768, name="proj_att_dw")
    gw_gz_t = mm(dzg, sv["hm"], ta=True, out_dtype=BF16, tm=1024, name="proj_gate_dw")
    gw_in = jnp.concatenate([gw_att_t, gw_gz_t], axis=0).reshape(NSH, IN_W // NSH, DM)
    token = emit("mix", (gw_in, gw_na, gw_sw, gw_out))
    dh = mm(dz, w["w_att_t"], tm=1024, name="proj_att_dx", dep=token)
    dh = mm(dzg, w["w_gz_t"], add=dh, tm=1024, name="proj_gate_dx")
    dy1, small["mix_norm"] = rms_bwd(dh, sv["y1"], row(p["mix_norm"]), dy2, name="mix_norm_bwd")
    dx, small["ffn1_norm"], act, dg, du = ffn_bwd_tokens(
        dy1, sv["x"], row(p["ffn1_norm"]), sv["gg1"], sv["uu1"], g1["ffn1_w_gate"], g1["ffn1_w_up"], g1["ffn1_w_down"],
        name="ffn_bwd_tokens")
    emit("ffn1", shards(ffn_bwd_weights(sv["h1"], dy1.astype(BF16), act, dg, du, name="ffn_bwd_weights")))
    return dx, small, dt5


ANY = pl.BlockSpec(memory_space=pl.ANY)


def _place():
    x, y, c = lax.axis_index("x"), lax.axis_index("y"), lax.axis_index("c")
    chips = [(1 - x, y), (x, 1 - y), (1 - x, 1 - y)]
    return x, y, c, chips


def _remote(src, dst, send_sem, recv_sem, to):
    return pltpu.make_async_remote_copy(src_ref=src, dst_ref=dst, send_sem=send_sem, recv_sem=recv_sem, device_id=to,
                                        device_id_type=MESH)


HBM = pl.BlockSpec(memory_space=pltpu.HBM)
SEM = pl.BlockSpec(memory_space=pltpu.SEMAPHORE)
ORDERED_EFFECT = pltpu.SideEffectType.DATAFLOW_SIDE_EFFECTING


def _in_hbm(v):
    return pltpu.with_memory_space_constraint(v, pltpu.HBM)


def _row_half(ref_shape_rows, c):
    half = ref_shape_rows // 2
    return pl.ds(c * half, half)


def _ici_gather_copies(w, land, send_sems, recv_sems):
    x, y, c, chips = _place()
    me = 2 * x + y
    copies = []
    for a in range(len(w)):
        rows = _row_half(w[a].shape[0], c)
        for k, chip in enumerate(chips):
            copies.append(_remote(w[a].at[rows], land[a].at[me, rows], send_sems.at[3 * a + k], recv_sems.at[3 * a + k],
                                  (*chip, c)))
    return copies


def _d2d_gather_copies(w, land, send_sems, recv_sems):
    x, y, c, chips = _place()
    me, sibling = 2 * x + y, (x, y, 1 - c)
    copies = []
    for a in range(len(w)):
        rows = _row_half(w[a].shape[0], c)
        for k, (cx, cy) in enumerate(chips):
            blk = land[a].at[2 * cx + cy, rows]
            copies.append(_remote(blk, blk, send_sems.at[4 * a + k], recv_sems.at[4 * a + k], sibling))
        copies.append(_remote(w[a], land[a].at[me], send_sems.at[4 * a + 3], recv_sems.at[4 * a + 3], sibling))
    return copies


def _d2d_gather_waits(w, land, send_sems, recv_sems):
    x, y, c, chips = _place()
    me, sibling = 2 * x + y, (x, y, 1 - c)
    waits = []
    for a in range(len(w)):
        rows = _row_half(w[a].shape[0], 1 - c)
        for k, (cx, cy) in enumerate(chips):
            blk = land[a].at[2 * cx + cy, rows]
            waits.append(_remote(blk, blk, send_sems.at[4 * a + k], recv_sems.at[4 * a + k], sibling))
        waits.append(_remote(w[a], land[a].at[me], send_sems.at[4 * a + 3], recv_sems.at[4 * a + 3], sibling))
    return waits


def gather_start(groups, *, name):
    sizes = [len(g) for g in groups]
    shards = [s for g in groups for s in g]
    n, ng = len(shards), len(groups)

    def body(*refs):
        w, land, sems = refs[:n], refs[n:2 * n], refs[2 * n:2 * n + 2 * ng]
        off = 0
        for gi, size in enumerate(sizes):
            for cp in _ici_gather_copies(w[off:off + size], land[off:off + size], sems[2 * gi], sems[2 * gi + 1]):
                cp.start()
            off += size

    lands = [lax.empty((NSH,) + s.shape, s.dtype) for s in shards]
    sem_shapes = tuple(pltpu.SemaphoreType.DMA((3 * size,)) for size in sizes for _ in range(2))
    res = pl.pallas_call(
        body, name=name,
        out_shape=sem_shapes + tuple(pltpu.HBM(s.shape, s.dtype) for s in shards) + tuple(pltpu.HBM(l.shape, l.dtype) for l in lands),
        in_specs=[HBM] * (2 * n), out_specs=(SEM,) * (2 * ng) + (HBM,) * (2 * n),
        input_output_aliases={i: 2 * ng + i for i in range(2 * n)},
        compiler_params=pltpu.CompilerParams(has_side_effects=ORDERED_EFFECT))(
            *[_in_hbm(s) for s in shards], *[_in_hbm(l) for l in lands])
    out, off = [], 0
    for gi, size in enumerate(sizes):
        out.append((res[2 * gi], res[2 * gi + 1], list(res[2 * ng + off:2 * ng + off + size]),
                    list(res[2 * ng + n + off:2 * ng + n + off + size])))
        off += size
    return out


def gather_wait(send_sems, recv_sems, shards, lands, after, *, name):
    n = len(shards)

    def body(*refs):
        w, land = refs[:n], refs[n:2 * n]
        send, recv = refs[2 * n:2 * n + 2]
        for cp in _ici_gather_copies(w, land, send, recv):
            cp.wait_send()
            cp.wait_recv()

    res = pl.pallas_call(
        body, name=name,
        out_shape=tuple(pltpu.HBM(s.shape, s.dtype) for s in shards) + tuple(pltpu.HBM(l.shape, l.dtype) for l in lands),
        in_specs=[HBM] * (2 * n) + [SEM, SEM, ANY], out_specs=(HBM,) * (2 * n),
        input_output_aliases={i: i for i in range(2 * n)},
        compiler_params=pltpu.CompilerParams(has_side_effects=ORDERED_EFFECT))(*shards, *lands, send_sems, recv_sems, after)
    return list(res[:n]), list(res[n:])


def gather_finish(shards, lands, *, name):
    n = len(shards)

    def body(*refs):
        w, land = refs[:n], refs[n:2 * n]
        send_sems, recv_sems = refs[3 * n:]
        d2d = _d2d_gather_copies(w, land, send_sems, recv_sems)
        for cp in d2d:
            cp.start()
        for cp in _d2d_gather_waits(w, land, send_sems, recv_sems):
            cp.wait_recv()
        for cp in d2d:
            cp.wait_send()

    return list(pl.pallas_call(
        body, name=name, out_shape=tuple(pltpu.HBM(l.shape, l.dtype) for l in lands),
        in_specs=[ANY] * (2 * n), out_specs=tuple([ANY] * n), input_output_aliases={n + i: i for i in range(n)},
        scratch_shapes=[pltpu.SemaphoreType.DMA((4 * n,)), pltpu.SemaphoreType.DMA((4 * n,))])(*shards, *lands))


def pair_exchange(grads, *, name):
    n = len(grads)

    def body(*refs):
        g, buf = refs[:n], refs[n:2 * n]
        send_sems, recv_sems = refs[2 * n:]
        x, y, c, _ = _place()
        copies = []
        for a in range(n):
            half = g[a].shape[1] // 2
            cp = _remote(g[a].at[:, pl.ds((1 - c) * half, half)], buf[a], send_sems.at[a], recv_sems.at[a], (x, y, 1 - c))
            cp.start()
            copies.append(cp)
        for cp in copies:
            cp.wait()

    return pl.pallas_call(
        body, name=name, out_shape=tuple(pltpu.HBM((NSH, g.shape[1] // 2, g.shape[2]), g.dtype) for g in grads),
        in_specs=[ANY] * n, out_specs=tuple([ANY] * n),
        scratch_shapes=[pltpu.SemaphoreType.DMA((n,)), pltpu.SemaphoreType.DMA((n,))])(*grads)


def _chip_exchange_copies(s, buf, send_sems, recv_sems):
    x, y, c, chips = _place()
    return [_remote(s[a].at[2 * cx + cy], buf[a].at[k], send_sems.at[3 * a + k], recv_sems.at[3 * a + k], (cx, cy, c))
            for a in range(len(s)) for k, (cx, cy) in enumerate(chips)]


def chip_exchange(sums, *, name):
    n = len(sums)

    def body(*refs):
        copies = _chip_exchange_copies(refs[:n], refs[n:2 * n], *refs[2 * n:])
        for cp in copies:
            cp.start()
        for cp in copies:
            cp.wait()

    return pl.pallas_call(
        body, name=name, out_shape=tuple(pltpu.HBM((3,) + s.shape[1:], s.dtype) for s in sums),
        in_specs=[ANY] * n, out_specs=tuple([ANY] * n),
        scratch_shapes=[pltpu.SemaphoreType.DMA((3 * n,)), pltpu.SemaphoreType.DMA((3 * n,))])(*sums)


def chip_exchange_start(sums, *, name):
    n = len(sums)

    def body(*refs):
        for cp in _chip_exchange_copies(refs[:n], refs[n:2 * n], refs[2 * n], refs[2 * n + 1]):
            cp.start()
        refs[-1][...] = jnp.zeros_like(refs[-1])

    lands = [lax.empty((3,) + s.shape[1:], s.dtype) for s in sums]
    res = pl.pallas_call(
        body, name=name,
        out_shape=(pltpu.SemaphoreType.DMA((3 * n,)), pltpu.SemaphoreType.DMA((3 * n,)))
        + tuple(pltpu.HBM(s.shape, s.dtype) for s in sums) + tuple(pltpu.HBM(l.shape, l.dtype) for l in lands)
        + (_sds((8, 128), F32),),
        in_specs=[HBM] * (2 * n), out_specs=(SEM, SEM) + (HBM,) * (2 * n) + (pl.BlockSpec(memory_space=pltpu.VMEM),),
        input_output_aliases={i: 2 + i for i in range(2 * n)},
        compiler_params=pltpu.CompilerParams(has_side_effects=ORDERED_EFFECT))(
            *[_in_hbm(s) for s in sums], *[_in_hbm(l) for l in lands])
    return res[0], res[1], list(res[2:2 + n]), list(res[2 + n:2 + 2 * n]), res[-1]


def chip_exchange_wait(send_sems, recv_sems, sums, lands, after, *, name):
    n = len(sums)

    def body(*refs):
        for cp in _chip_exchange_copies(refs[:n], refs[n:2 * n], refs[2 * n], refs[2 * n + 1]):
            cp.wait_send()
            cp.wait_recv()

    res = pl.pallas_call(
        body, name=name,
        out_shape=tuple(pltpu.HBM(s.shape, s.dtype) for s in sums) + tuple(pltpu.HBM(l.shape, l.dtype) for l in lands),
        in_specs=[HBM] * (2 * n) + [SEM, SEM] + [ANY] * len(after), out_specs=(HBM,) * (2 * n),
        input_output_aliases={i: i for i in range(2 * n)},
        compiler_params=pltpu.CompilerParams(has_side_effects=ORDERED_EFFECT))(*sums, *lands, send_sems, recv_sems, *after)
    return list(res[:n]), list(res[n:])


def pair_send(halves, *, name):
    n = len(halves)

    def body(*refs):
        h, got = refs[:n], refs[n:2 * n]
        send_sems, recv_sems = refs[2 * n:]
        x, y, c, _ = _place()
        copies = []
        for i in range(n):
            cp = _remote(h[i], got[i], send_sems.at[i], recv_sems.at[i], (x, y, 1 - c))
            cp.start()
            copies.append(cp)
        for cp in copies:
            cp.wait()

    return list(pl.pallas_call(
        body, name=name, out_shape=tuple(pltpu.HBM(v.shape, v.dtype) for v in halves),
        in_specs=[ANY] * n, out_specs=tuple([ANY] * n),
        scratch_shapes=[pltpu.SemaphoreType.DMA((n,)), pltpu.SemaphoreType.DMA((n,))])(*halves))


def allreduce_small(v, *, name):
    rows = v.shape[0]

    def body(v_ref, o_ref, gath, send_sems, recv_sems):
        x, y, c, _ = _place()
        me = 4 * x + 2 * y + c
        gath[me] = v_ref[...]
        copies = []
        for k in range(1, 8):
            fx, fy, fc = (k >> 2) & 1, (k >> 1) & 1, k & 1
            peer = (jnp.where(fx, 1 - x, x), jnp.where(fy, 1 - y, y), jnp.where(fc, 1 - c, c))
            cp = _remote(v_ref, gath.at[me], send_sems.at[k - 1], recv_sems.at[k - 1], peer)
            cp.start()
            copies.append(cp)
        for cp in copies:
            cp.wait()
        acc = gath[0]
        for d in range(1, 8):
            acc = acc + gath[d]
        o_ref[...] = acc

    return pl.pallas_call(
        body, name=name, out_shape=_sds(v.shape, F32),
        in_specs=[pl.BlockSpec(memory_space=pltpu.VMEM)], out_specs=pl.BlockSpec(memory_space=pltpu.VMEM),
        scratch_shapes=[pltpu.VMEM((8, rows, 128), F32), pltpu.SemaphoreType.DMA((7,)), pltpu.SemaphoreType.DMA((7,))])(v)


def add_halves(g, buf, cidx, *, name):
    _, k, n = g.shape

    def body(c_ref, g_ref, b_ref, o_ref):
        o_ref[...] = (g_ref[...].astype(F32) + b_ref[...].astype(F32)).astype(BF16)

    blk = pl.BlockSpec((None, k // 2, n), lambda s, c: (s, 0, 0))
    return pl.pallas_call(
        body, name=name, out_shape=_sds(buf.shape, BF16),
        grid_spec=pltpu.PrefetchScalarGridSpec(
            num_scalar_prefetch=1, grid=(NSH,),
            in_specs=[pl.BlockSpec((None, k // 2, n), lambda s, c: (s, c[0], 0)), blk], out_specs=blk),
        compiler_params=_params(("parallel",)))(cidx, g, buf)


def add_chips(sums, buf, sidx, *, name, dep=None):
    _, kh, n = sums.shape

    def body(s_ref, mine_ref, b_ref, *rest):
        o_ref = rest[-1]
        o_ref[...] = (mine_ref[...].astype(F32) + b_ref[0].astype(F32)) + (b_ref[1].astype(F32) + b_ref[2].astype(F32))

    ins, specs = _with_dep(
        [sums, buf], [pl.BlockSpec((None, kh, n), lambda i, s: (s[0], 0, 0)), pl.BlockSpec((3, kh, n), lambda i, s: (0, 0, 0))],
        dep)
    return pl.pallas_call(
        body, name=name, out_shape=_sds((kh, n), F32),
        grid_spec=pltpu.PrefetchScalarGridSpec(
            num_scalar_prefetch=1, grid=(1,), in_specs=specs, out_specs=pl.BlockSpec((kh, n), lambda i, s: (0, 0))),
        compiler_params=_params(("arbitrary",)))(sidx, *ins)


PARAMS = ("ffn1_norm", "ffn1_w_gate", "ffn1_w_up", "ffn1_w_down", "mix_norm", "w_in", "b_gate", "na_q_norm", "na_k_norm",
          "na_rpb", "sw_q_norm", "sw_k_norm", "sw_sink", "t5_rel_table", "w_branch_na", "w_branch_sw", "w_out", "ffn2_norm",
          "ffn2_w_gate", "ffn2_w_up", "ffn2_w_down")
SMALL_ALL = tuple(n for n in PARAMS if n not in BIG)
TRANSPOSED = ("ffn1_w_gate", "ffn1_w_up", "w_in", "ffn2_w_gate", "ffn2_w_up")
SMALL_ROWS = 152


def _pack_small(vals):
    flat = jnp.concatenate([vals[n].reshape(-1).astype(F32) for n in SMALL_ALL] + [vals["loss"].reshape(-1)])
    return jnp.pad(flat, (0, SMALL_ROWS * 128 - flat.shape[0])).reshape(SMALL_ROWS, 128)


def _unpack_small(packed, like):
    flat, out, off = packed.reshape(-1), {}, 0
    for n in SMALL_ALL:
        size = math.prod(like[n].shape)
        out[n] = flat[off:off + size].reshape(like[n].shape)
        off += size
    out["loss"] = flat[off]
    return out


def kernel(x, ffn1_norm, ffn1_w_gate, ffn1_w_up, ffn1_w_down, mix_norm, w_in, b_gate, na_q_norm, na_k_norm, na_rpb, sw_q_norm, sw_k_norm, sw_sink, t5_rel_table, w_branch_na, w_branch_sw, w_out, ffn2_norm, ffn2_w_gate, ffn2_w_up, ffn2_w_down, loss_target, m_ffn1_norm, m_ffn1_w_gate, m_ffn1_w_up, m_ffn1_w_down, m_mix_norm, m_w_in, m_b_gate, m_na_q_norm, m_na_k_norm, m_na_rpb, m_sw_q_norm, m_sw_k_norm, m_sw_sink, m_t5_rel_table, m_w_branch_na, m_w_branch_sw, m_w_out, m_ffn2_norm, m_ffn2_w_gate, m_ffn2_w_up, m_ffn2_w_down, v_ffn1_norm, v_ffn1_w_gate, v_ffn1_w_up, v_ffn1_w_down, v_mix_norm, v_w_in, v_b_gate, v_na_q_norm, v_na_k_norm, v_na_rpb, v_sw_q_norm, v_sw_k_norm, v_sw_sink, v_t5_rel_table, v_w_branch_na, v_w_branch_sw, v_w_out, v_ffn2_norm, v_ffn2_w_gate, v_ffn2_w_up, v_ffn2_w_down):
    args = locals()
    tr = lambda n, a: jnp.transpose(a, (0, 2, 1)) if n in TRANSPOSED else a
    w = {n: tr(n, args[n]) for n in PARAMS}
    m = {n: tr(n, args["m_" + n]) for n in PARAMS}
    v = {n: tr(n, args["v_" + n]) for n in PARAMS}
    cidx = lax.axis_index("c").astype(jnp.int32).reshape(1)
    sidx = (2 * lax.axis_index("x") + lax.axis_index("y")).astype(jnp.int32).reshape(1)

    small = [{n: w[n][l] for n in SMALL} for l in range(DEPTH)]
    t5b = t5_bias(w["t5_rel_table"], name="t5_bias")
    order = ("ffn1", "mix", "ffn2")

    keys = [(l, g) for l in range(DEPTH) for g in order]
    in_flight = dict(zip(keys, gather_start([[w[n][l].astype(BF16) for n in GROUPS[g]] for l, g in keys], name="gather_start")))

    def weights_of(l):
        def get(group, after):
            send_sems, recv_sems, thru, lands = in_flight[(l, group)]
            thru, lands = gather_wait(send_sems, recv_sems, thru, lands, after, name="gather_wait")
            return dict(zip(GROUPS[group], gather_finish(thru, lands, name="gather_finish")))
        return get

    h0, saved0 = layer_fwd(x[0], small[0], weights_of(0), t5b)
    h1, saved1 = layer_fwd(h0, small[1], weights_of(1), t5b)
    dy, loss_row = loss_head(h1, loss_target[0], name="loss_head")

    crossing, tokens = {}, []

    def reduce_of(l):
        def emit(group, grads):
            grads = list(grads)
            sums = [add_halves(g, b, cidx, name="add_halves") for g, b in zip(grads, pair_exchange(grads, name="pair_exchange"))]
            send_sems, recv_sems, sums, lands, token = chip_exchange_start(sums, name="chip_exchange_start")
            crossing[(l, group)] = (send_sems, recv_sems, sums, lands)
            tokens.append(token)
            return token
        return emit

    def arrived(l, after):
        halves = {}
        for group in order:
            send_sems, recv_sems, sums, lands = crossing[(l, group)]
            sums, got = chip_exchange_wait(send_sems, recv_sems, sums, lands, after, name="chip_exchange_wait")
            for n, s, b in zip(GROUPS[group], sums, got):
                halves[n] = add_chips(s, b, sidx, name="add_chips")
        return [halves[n] for n in BIG]

    dy, small1, dt5_1 = layer_bwd(dy, saved1, small[1], t5b, reduce_of(1))
    grad_x, small0, dt5_0 = layer_bwd(dy, saved0, small[0], t5b, reduce_of(0), dep=tokens[-1])
    halves1 = arrived(1, [tokens[-1]])
    theirs1 = pair_send(halves1, name="pair_send")
    done1 = [adamw_layer(w[n], m[n], v[n], halves1[a], theirs1[a], cidx, 1, name="adamw_layer") for a, n in enumerate(BIG)]

    smalls = [small0, small1]
    dt5 = t5_table_grad(dt5_0, dt5_1, name="t5_table_grad").reshape(32, 8)
    local_small = {n: jnp.stack([smalls[l][n].reshape(w[n].shape[1:]) for l in range(DEPTH)]) for n in SMALL}
    local_small["t5_rel_table"] = dt5
    local_small["loss"] = loss_row[0, 0:1]
    total = allreduce_small(_pack_small(local_small), name="allreduce_small")
    small_grads = _unpack_small(total, w)
    pack = lambda d: _pack_small({**d, "loss": jnp.zeros((1,), F32)})[None]
    ds, ms, vs = adamw(pack(w), total[None], pack(m), pack(v), name="adamw_small")

    halves0 = arrived(0, [ds, done1[-1][0]])
    theirs0 = pair_send(halves0, name="pair_send")
    grad, delta, new_m, new_v = {}, {}, {}, {}
    for a, n in enumerate(BIG):
        grad[n], delta[n], new_m[n], new_v[n] = adamw_layer(
            w[n], m[n], v[n], halves0[a], theirs0[a], cidx, 0, filled=done1[a], name="adamw_layer")
    for n in SMALL_ALL:
        grad[n] = small_grads[n]
    d_s, m_s, v_s = _unpack_small(ds[0], w), _unpack_small(ms[0], w), _unpack_small(vs[0], w)
    for n in SMALL_ALL:
        delta[n], new_m[n], new_v[n] = d_s[n], m_s[n], v_s[n]

    return (small_grads["loss"], grad_x[None], *[tr(n, grad[n]) for n in PARAMS], *[tr(n, delta[n]) for n in PARAMS],
            *[tr(n, new_m[n]) for n in PARAMS], *[tr(n, new_v[n]) for n in PARAMS])
```

```python
import functools
import math

import jax
import jax.numpy as jnp
import numpy as np
from jax import lax
from jax.experimental import pallas as pl
from jax.experimental.pallas import tpu as pltpu

F32 = jnp.float32
BF16 = jnp.bfloat16

SEQ = 2048
DM = 1024
DFF = 2816
DEPTH = 2
NSH = 4
FSH = DFF // NSH
GRID_W = 64
ROWS = SEQ // GRID_W
NA_HEADS = 8
HD = 64
NA_WR = 8
NA_WC = 16
NA_KEYS = NA_WR * GRID_W
SW_BLK = 128
SW_NB = SEQ // SW_BLK
SW_KEYS = 3 * SW_BLK
ATT_W = 2304
GATE_W = 2048
IN_W = ATT_W + GATE_W
EPS = 1e-6
NEG = -1e30
QK_SCALE = 1.0 / math.sqrt(HD)

ADAM_LR = 0.001
ADAM_B1 = 0.9
ADAM_B2 = 0.999
ADAM_EPS = 1e-08
ADAM_WD = 0.01
ADAM_STEP = 10

VMEM_LIMIT = 56 << 20
MESH = pl.DeviceIdType.MESH

NT = (((1,), (1,)), ((), ()))
TN = (((0,), (0,)), ((), ()))
NN = (((1,), (0,)), ((), ()))


def _dot(a, b, dims=NN):
    return lax.dot_general(a, b, dims, preferred_element_type=F32)


def _params(sem=None):
    return pltpu.CompilerParams(dimension_semantics=sem, vmem_limit_bytes=VMEM_LIMIT)


def _sds(shape, dtype):
    return jax.ShapeDtypeStruct(shape, dtype)


def mm(a, b, *, name, ta=False, tb=False, out_dtype=F32, add=None, scale=None, tm=512, tn=None, tk=None, exact=False,
       dep=None):
    m, kd = (a.shape[1], a.shape[0]) if ta else a.shape
    n = b.shape[0] if tb else b.shape[1]
    tm, tn, tk = min(tm, m), min(tn or n, n), min(tk or kd, kd)
    nk = kd // tk
    dims = (((0 if ta else 1,), (1 if tb else 0,)), ((), ()))

    def body(*refs):
        a_ref, b_ref = refs[:2]
        add_ref = refs[2] if add is not None else None
        o_ref, acc = refs[-2:]
        k = pl.program_id(2)

        @pl.when(k == 0)
        def _():
            acc[...] = jnp.zeros_like(acc)

        if exact:
            acc[...] += lax.dot_general(a_ref[...], b_ref[...], dims, precision=lax.Precision.HIGHEST,
                                        preferred_element_type=F32)
        else:
            acc[...] += lax.dot_general(a_ref[...].astype(BF16), b_ref[...].astype(BF16), dims,
                                        preferred_element_type=F32)

        @pl.when(k == nk - 1)
        def _():
            r = acc[...]
            if scale is not None:
                r = r * scale
            if add is not None:
                r = r + add_ref[...]
            o_ref[...] = r.astype(out_dtype)

    a_spec = pl.BlockSpec((tk, tm), lambda i, j, k: (k, i)) if ta else pl.BlockSpec((tm, tk), lambda i, j, k: (i, k))
    b_spec = pl.BlockSpec((tn, tk), lambda i, j, k: (j, k)) if tb else pl.BlockSpec((tk, tn), lambda i, j, k: (k, j))
    o_spec = pl.BlockSpec((tm, tn), lambda i, j, k: (i, j))
    ins, specs = [a, b], [a_spec, b_spec]
    if add is not None:
        ins.append(add)
        specs.append(o_spec)
    if dep is not None:
        ins.append(dep)
        specs.append(pl.BlockSpec(memory_space=pl.ANY))
    return pl.pallas_call(
        body, name=name, out_shape=_sds((m, n), out_dtype), grid=(m // tm, n // tn, nk), in_specs=specs,
        out_specs=o_spec, scratch_shapes=[pltpu.VMEM((tm, tn), F32)],
        compiler_params=_params(("parallel", "parallel", "arbitrary")))(*ins)


def _rms(x):
    return lax.rsqrt(jnp.mean(x * x, axis=-1, keepdims=True) + EPS)


def rms_fwd(x, gain, *, name, tm=512):
    def body(x_ref, g_ref, h_ref):
        x = x_ref[...]
        h_ref[...] = (x * _rms(x) * g_ref[...]).astype(BF16)

    return pl.pallas_call(
        body, name=name, out_shape=_sds(x.shape, BF16), grid=(x.shape[0] // tm,),
        in_specs=[pl.BlockSpec((tm, DM), lambda i: (i, 0)), pl.BlockSpec((1, DM), lambda i: (0, 0))],
        out_specs=pl.BlockSpec((tm, DM), lambda i: (i, 0)), compiler_params=_params(("parallel",)))(x, gain)


def _rms_bwd_math(dh, x, gain):
    r = _rms(x)
    xh = x * r
    dgain = jnp.sum(dh * xh, axis=0, keepdims=True)
    dxn = dh * gain
    dx = r * (dxn - xh * jnp.mean(dxn * xh, axis=-1, keepdims=True))
    return dx, dgain


def rms_bwd(dh, x, gain, dres, *, name, tm=512):
    def body(dh_ref, x_ref, g_ref, dres_ref, dx_ref, dg_ref):
        @pl.when(pl.program_id(0) == 0)
        def _():
            dg_ref[...] = jnp.zeros_like(dg_ref)

        dx, dg = _rms_bwd_math(dh_ref[...], x_ref[...], g_ref[...])
        dx_ref[...] = dres_ref[...] + dx
        dg_ref[...] += dg

    tile = pl.BlockSpec((tm, DM), lambda i: (i, 0))
    vec = pl.BlockSpec((1, DM), lambda i: (0, 0))
    return pl.pallas_call(
        body, name=name, out_shape=(_sds(x.shape, F32), _sds((1, DM), F32)), grid=(x.shape[0] // tm,),
        in_specs=[tile, tile, vec, tile], out_specs=(tile, vec), compiler_params=_params(("arbitrary",)))(dh, x, gain, dres)


def _with_dep(ins, specs, dep):
    if dep is None:
        return ins, specs
    return ins + [dep], specs + [pl.BlockSpec(memory_space=pl.ANY)]


def _resident_weight():
    return pl.BlockSpec((DFF, DM), lambda i: (0, 0), pipeline_mode=pl.Buffered(1))


def ffn_fwd(x, gain, wg, wu, wd, *, name, tm=512):
    def body(x_ref, g_ref, wg_ref, wu_ref, wd_ref, y_ref, h_ref, gg_ref, uu_ref):
        x = x_ref[...]
        h = (x * _rms(x) * g_ref[...]).astype(BF16)
        h_ref[...] = h
        gg = _dot(h, wg_ref[...], NT)
        uu = _dot(h, wu_ref[...], NT)
        gg_ref[...] = gg.astype(BF16)
        uu_ref[...] = uu.astype(BF16)
        act = (gg * jax.nn.sigmoid(gg) * uu).astype(BF16)
        y_ref[...] = x + 0.5 * _dot(act, wd_ref[...])

    s = x.shape[0]
    tile = pl.BlockSpec((tm, DM), lambda i: (i, 0))
    hid = pl.BlockSpec((tm, DFF), lambda i: (i, 0))
    w = _resident_weight()
    return pl.pallas_call(
        body, name=name,
        out_shape=(_sds((s, DM), F32), _sds((s, DM), BF16), _sds((s, DFF), BF16), _sds((s, DFF), BF16)),
        grid=(s // tm,), in_specs=[tile, pl.BlockSpec((1, DM), lambda i: (0, 0)), w, w, w],
        out_specs=(tile, tile, hid, hid), compiler_params=_params(("parallel",)))(x, gain, wg, wu, wd)


def ffn_bwd_tokens(dy, x, gain, gg, uu, wg, wu, wd, *, name, tm=256, dep=None):
    def body(dy_ref, x_ref, g_ref, gg_ref, uu_ref, wg_ref, wu_ref, wd_ref, *rest):
        dx_ref, dgain_ref, act_ref, dg_ref, du_ref = rest[-5:]

        @pl.when(pl.program_id(0) == 0)
        def _():
            dgain_ref[...] = jnp.zeros_like(dgain_ref)

        dy = dy_ref[...]
        dact = _dot((0.5 * dy).astype(BF16), wd_ref[...], NT)
        g = gg_ref[...].astype(F32)
        u = uu_ref[...].astype(F32)
        sg = jax.nn.sigmoid(g)
        silu = g * sg
        act_ref[...] = (silu * u).astype(BF16)
        dg = (dact * u * (sg * (1.0 + g * (1.0 - sg)))).astype(BF16)
        du = (dact * silu).astype(BF16)
        dg_ref[...] = dg
        du_ref[...] = du
        dx, dgain = _rms_bwd_math(_dot(dg, wg_ref[...]) + _dot(du, wu_ref[...]), x_ref[...], g_ref[...])
        dx_ref[...] = dy + dx
        dgain_ref[...] += dgain

    s = x.shape[0]
    tile = pl.BlockSpec((tm, DM), lambda i: (i, 0))
    vec = pl.BlockSpec((1, DM), lambda i: (0, 0))
    hid = pl.BlockSpec((tm, DFF), lambda i: (i, 0))
    hshape = _sds((s, DFF), BF16)
    w = _resident_weight()
    ins, specs = _with_dep([dy, x, gain, gg, uu, wg, wu, wd], [tile, tile, vec, hid, hid, w, w, w], dep)
    return pl.pallas_call(
        body, name=name, out_shape=(_sds((s, DM), F32), _sds((1, DM), F32), hshape, hshape, hshape),
        grid=(s // tm,), in_specs=specs, out_specs=(tile, vec, hid, hid, hid),
        compiler_params=_params(("arbitrary",)))(*ins)


def ffn_bwd_weights(h, dy, act, dg, du, *, name, tf=256):
    def body(h_ref, dy_ref, act_ref, dg_ref, du_ref, gwg_ref, gwu_ref, gwd_ref):
        h = h_ref[...]
        gwg_ref[...] = _dot(dg_ref[...], h, TN).astype(BF16)
        gwu_ref[...] = _dot(du_ref[...], h, TN).astype(BF16)
        gwd_ref[...] = (0.5 * _dot(act_ref[...], dy_ref[...], TN)).astype(BF16)

    s = h.shape[0]
    full = pl.BlockSpec((s, DM), lambda f: (0, 0))
    hid = pl.BlockSpec((s, tf), lambda f: (0, f))
    wt = pl.BlockSpec((tf, DM), lambda f: (f, 0))
    wshape = _sds((DFF, DM), BF16)
    return pl.pallas_call(
        body, name=name, out_shape=(wshape, wshape, wshape), grid=(DFF // tf,), in_specs=[full, full, hid, hid, hid],
        out_specs=(wt, wt, wt), compiler_params=_params(("parallel",)))(h, dy, act, dg, du)


def _group_mean(v, bd):
    hi = v.astype(BF16)
    lo = (v - hi.astype(F32)).astype(BF16)
    return _dot(hi, bd) + _dot(lo, bd)


def _block_diag(width):
    idx = np.arange(width) // HD
    return jnp.asarray((idx[:, None] == idx[None, :]).astype(np.float32) / HD, dtype=BF16)


def qknorm_fwd(z, gq_na, gk_na, gq_sw, gk_sw, *, name, tm=256):
    def body(zq_ref, zk_ref, zv_ref, zs_ref, zkv_ref, gqa_ref, gka_ref, gqs_ref, gks_ref, bd_ref, bd2_ref,
             qa_ref, ka_ref, va_ref, qs_ref, kv_ref):
        bd = bd_ref[...]

        def norm(x, g, bdm):
            return x * lax.rsqrt(_group_mean(x * x, bdm) + EPS) * g

        qa_ref[...] = (norm(zq_ref[...], gqa_ref[...], bd) * QK_SCALE).astype(BF16)
        ka_ref[...] = norm(zk_ref[...], gka_ref[...], bd).astype(BF16)
        va_ref[...] = zv_ref[...].astype(BF16)
        qs_ref[...] = (norm(zs_ref[...], gqs_ref[...], bd) * QK_SCALE).astype(BF16)
        kv = zkv_ref[...]
        kv_ref[:, 0:128] = norm(kv[:, 0:128], gks_ref[...], bd2_ref[...]).astype(BF16)
        kv_ref[:, 128:256] = kv[:, 128:256].astype(BF16)

    s = z.shape[0]
    col = lambda j: pl.BlockSpec((tm, 512), lambda i, j=j: (i, j))
    vec = lambda w: pl.BlockSpec((1, w), lambda i: (0, 0))
    o512 = pl.BlockSpec((tm, 512), lambda i: (i, 0))
    g512 = lambda g: jnp.tile(g.reshape(1, HD), (1, 8))
    return pl.pallas_call(
        body, name=name,
        out_shape=(_sds((s, 512), BF16),) * 4 + (_sds((s, 256), BF16),), grid=(s // tm,),
        in_specs=[col(0), col(1), col(2), col(3), pl.BlockSpec((tm, 256), lambda i: (i, 8)), vec(512), vec(512), vec(512),
                  vec(128), pl.BlockSpec((512, 512), lambda i: (0, 0)), pl.BlockSpec((128, 128), lambda i: (0, 0))],
        out_specs=(o512, o512, o512, o512, pl.BlockSpec((tm, 256), lambda i: (i, 0))),
        compiler_params=_params(("parallel",)))(
            z, z, z, z, z, g512(gq_na), g512(gk_na), g512(gq_sw), jnp.tile(gk_sw.reshape(1, HD), (1, 2)),
            _block_diag(512), _block_diag(128))


def qknorm_bwd(z, dqa, dka, dva, dqs, dkv, gq_na, gk_na, gq_sw, gk_sw, *, name, tm=256):
    def body(zq_ref, zk_ref, zs_ref, zkv_ref, dqa_ref, dka_ref, dva_ref, dqs_ref, dkv_ref, gqa_ref, gka_ref, gqs_ref,
             gks_ref, bd_ref, bd2_ref, dz_ref, dgqa_ref, dgka_ref, dgqs_ref, dgks_ref):
        @pl.when(pl.program_id(0) == 0)
        def _():
            dgqa_ref[...] = jnp.zeros_like(dgqa_ref)
            dgka_ref[...] = jnp.zeros_like(dgka_ref)
            dgqs_ref[...] = jnp.zeros_like(dgqs_ref)
            dgks_ref[...] = jnp.zeros_like(dgks_ref)

        bd = bd_ref[...]

        def bwd(x, dy, g, bdm, dg_ref):
            r = lax.rsqrt(_group_mean(x * x, bdm) + EPS)
            xh = x * r
            dg_ref[...] += jnp.sum(dy * xh, axis=0, keepdims=True)
            dxn = dy * g
            return r * (dxn - xh * _group_mean(dxn * xh, bdm))

        dz_ref[:, 0:512] = bwd(zq_ref[...], dqa_ref[...] * QK_SCALE, gqa_ref[...], bd, dgqa_ref).astype(BF16)
        dz_ref[:, 512:1024] = bwd(zk_ref[...], dka_ref[...], gka_ref[...], bd, dgka_ref).astype(BF16)
        dz_ref[:, 1024:1536] = dva_ref[...].astype(BF16)
        dz_ref[:, 1536:2048] = bwd(zs_ref[...], dqs_ref[...] * QK_SCALE, gqs_ref[...], bd, dgqs_ref).astype(BF16)
        dkv = dkv_ref[...]
        dz_ref[:, 2048:2176] = bwd(zkv_ref[:, 0:128], dkv[:, 0:128], gks_ref[...], bd2_ref[...], dgks_ref).astype(BF16)
        dz_ref[:, 2176:2304] = dkv[:, 128:256].astype(BF16)

    s = z.shape[0]
    col = lambda j: pl.BlockSpec((tm, 512), lambda i, j=j: (i, j))
    t512 = pl.BlockSpec((tm, 512), lambda i: (i, 0))
    t256 = pl.BlockSpec((tm, 256), lambda i: (i, 0))
    vec = lambda w: pl.BlockSpec((1, w), lambda i: (0, 0))
    g512 = lambda g: jnp.tile(g.reshape(1, HD), (1, 8))
    return pl.pallas_call(
        body, name=name,
        out_shape=(_sds((s, ATT_W), BF16), _sds((1, 512), F32), _sds((1, 512), F32), _sds((1, 512), F32), _sds((1, 128), F32)),
        grid=(s // tm,),
        in_specs=[col(0), col(1), col(3), pl.BlockSpec((tm, 256), lambda i: (i, 8)), t512, t512, t512, t512, t256,
                  vec(512), vec(512), vec(512), vec(128), pl.BlockSpec((512, 512), lambda i: (0, 0)),
                  pl.BlockSpec((128, 128), lambda i: (0, 0))],
        out_specs=(pl.BlockSpec((tm, ATT_W), lambda i: (i, 0)), vec(512), vec(512), vec(512), vec(128)),
        compiler_params=_params(("arbitrary",)))(
            z, z, z, z, dqa, dka, dva, dqs, dkv, g512(gq_na), g512(gk_na), g512(gq_sw),
            jnp.tile(gk_sw.reshape(1, HD), (1, 2)), _block_diag(512), _block_diag(128))


def _na_row_start(r):
    return jnp.clip(r - NA_WR // 2, 0, ROWS - NA_WR)


def na_bias_table(rpb, *, name):
    t = jnp.pad(rpb, ((0, 0), (0, 2), (0, HD - (2 * NA_WC - 1))))
    pairs = jnp.concatenate([t[:, :16], t[:, 1:17]], axis=-1).reshape(NA_HEADS, 16, 1, 128)

    def body(t_ref, o_ref):
        p = pl.program_id(0)
        q = lax.broadcasted_iota(jnp.int32, (GRID_W, 128), 0)
        kc = lax.broadcasted_iota(jnp.int32, (GRID_W, 128), 1) & (GRID_W - 1)
        cs = jnp.clip(q - NA_WC // 2, 0, GRID_W - NA_WC)
        ok = (kc >= cs) & (kc < cs + NA_WC)
        for h in range(NA_HEADS):
            for pr in range(NA_WR // 2):
                x = jnp.broadcast_to(t_ref[h, 2 * pr - p + NA_WR - 1], (GRID_W, 128))
                b = pltpu.roll(x, 128 - (NA_WC - 1), 1, stride=1, stride_axis=0)
                o_ref[h, :, 128 * pr:128 * pr + 128] = jnp.where(ok, b, NEG)

    return pl.pallas_call(
        body, name=name, out_shape=_sds((NA_WR, NA_HEADS, GRID_W, NA_KEYS), F32), grid=(NA_WR,),
        in_specs=[pl.BlockSpec((NA_HEADS, 16, 1, 128), lambda p: (0, 0, 0, 0))],
        out_specs=pl.BlockSpec((None, NA_HEADS, GRID_W, NA_KEYS), lambda p: (p, 0, 0, 0)),
        compiler_params=_params(("parallel",)))(pairs)


def _lane_halves():
    lane = lax.broadcasted_iota(jnp.int32, (1, 128), 1)
    return lane < HD


def na_fwd(q, k, v, bias, *, name):
    def body(q_ref, k_ref, v_ref, b_ref, o_ref, lse_ref):
        r = pl.program_id(0)
        off = pl.multiple_of(_na_row_start(r) * GRID_W, GRID_W)
        first = _lane_halves()
        sels = [first, jnp.logical_not(first)]
        lanes = [slice(128 * j, 128 * j + 128) for j in range(NA_HEADS // 2)]
        q2s = [q_ref[:, l] for l in lanes]
        k2s = [k_ref[pl.ds(off, NA_KEYS), l] for l in lanes]
        v2s = [v_ref[pl.ds(off, NA_KEYS), l] for l in lanes]
        scores = []
        for h in range(NA_HEADS):
            j, half = divmod(h, 2)
            scores.append(_dot(jnp.where(sels[half], q2s[j], jnp.zeros_like(q2s[j])), k2s[j], NT))
        probs, lses = [], []
        for h in range(NA_HEADS):
            b = b_ref[h]
            s = jnp.where(b > 0.5 * NEG, scores[h] + b, NEG)
            m = jnp.max(s, axis=-1, keepdims=True)
            e = jnp.exp(s - m)
            l = jnp.sum(e, axis=-1, keepdims=True)
            probs.append((e / l).astype(BF16))
            lses.append(m + jnp.log(l))
        for j in range(NA_HEADS // 2):
            zero = jnp.zeros_like(v2s[j])
            o2 = (_dot(probs[2 * j], jnp.where(sels[0], v2s[j], zero))
                  + _dot(probs[2 * j + 1], jnp.where(sels[1], v2s[j], zero)))
            o_ref[:, lanes[j]] = o2.astype(BF16)
        lse_ref[...] = jnp.concatenate(lses, axis=1)

    s_tok = q.shape[0]
    full = pl.BlockSpec((s_tok, 512), lambda r: (0, 0))
    return pl.pallas_call(
        body, name=name, out_shape=(_sds((s_tok, 512), BF16), _sds((s_tok, NA_HEADS), F32)), grid=(ROWS,),
        in_specs=[pl.BlockSpec((GRID_W, 512), lambda r: (r, 0)), full, full,
                  pl.BlockSpec((None, NA_HEADS, GRID_W, NA_KEYS), lambda r: (r - _na_row_start(r), 0, 0, 0))],
        out_specs=(pl.BlockSpec((GRID_W, 512), lambda r: (r, 0)), pl.BlockSpec((GRID_W, NA_HEADS), lambda r: (r, 0))),
        compiler_params=_params(("parallel",)))(q, k, v, bias)


def na_bwd(q, k, v, o, do, lse, bias, *, name):
    def body(q_ref, k_ref, v_ref, o_ref, do_ref, lse_ref, b_ref, dq_ref, dk_ref, dv_ref, db_ref):
        r = pl.program_id(0)

        @pl.when(r == 0)
        def _():
            dk_ref[...] = jnp.zeros_like(dk_ref)
            dv_ref[...] = jnp.zeros_like(dv_ref)

        @pl.when((r <= NA_WR // 2) | (r > ROWS - NA_WR // 2))
        def _():
            db_ref[...] = jnp.zeros_like(db_ref)

        off = pl.multiple_of(_na_row_start(r) * GRID_W, GRID_W)
        first = _lane_halves()
        sels = [first, jnp.logical_not(first)]
        lanes = [slice(128 * j, 128 * j + 128) for j in range(NA_HEADS // 2)]
        q2s = [q_ref[:, l] for l in lanes]
        k2s = [k_ref[pl.ds(off, NA_KEYS), l] for l in lanes]
        v2s = [v_ref[pl.ds(off, NA_KEYS), l] for l in lanes]
        do2s = [do_ref[:, l] for l in lanes]
        prods = [do2s[j].astype(F32) * o_ref[:, lanes[j]].astype(F32) for j in range(NA_HEADS // 2)]
        lse = lse_ref[...]
        qhs, dohs, scores, dps = [], [], [], []
        for h in range(NA_HEADS):
            j, half = divmod(h, 2)
            qhs.append(jnp.where(sels[half], q2s[j], jnp.zeros_like(q2s[j])))
            dohs.append(jnp.where(sels[half], do2s[j], jnp.zeros_like(do2s[j])))
            scores.append(_dot(qhs[h], k2s[j], NT))
            dps.append(_dot(dohs[h], v2s[j], NT))
        pbs, dsbs = [], []
        for h in range(NA_HEADS):
            j, half = divmod(h, 2)
            b = b_ref[h]
            s = jnp.where(b > 0.5 * NEG, scores[h] + b, NEG)
            p = jnp.exp(s - lse[:, h:h + 1])
            delta = jnp.sum(jnp.where(sels[half], prods[j], 0.0), axis=-1, keepdims=True)
            ds = p * (dps[h] - delta)
            db_ref[h] += ds
            pbs.append(p.astype(BF16))
            dsbs.append(ds.astype(BF16))
        for j in range(NA_HEADS // 2):
            a, b = 2 * j, 2 * j + 1
            zero = jnp.zeros_like(k2s[j])
            dq_ref[:, lanes[j]] = (_dot(dsbs[a], jnp.where(sels[0], k2s[j], zero))
                                   + _dot(dsbs[b], jnp.where(sels[1], k2s[j], zero)))
            dk_ref[pl.ds(off, NA_KEYS), lanes[j]] += _dot(dsbs[a], qhs[a], TN) + _dot(dsbs[b], qhs[b], TN)
            dv_ref[pl.ds(off, NA_KEYS), lanes[j]] += _dot(pbs[a], dohs[a], TN) + _dot(pbs[b], dohs[b], TN)

    s_tok = q.shape[0]
    full = pl.BlockSpec((s_tok, 512), lambda r: (0, 0))
    row = pl.BlockSpec((GRID_W, 512), lambda r: (r, 0))
    bias_spec = pl.BlockSpec((None, NA_HEADS, GRID_W, NA_KEYS), lambda r: (r - _na_row_start(r), 0, 0, 0))
    return pl.pallas_call(
        body, name=name,
        out_shape=(_sds((s_tok, 512), F32), _sds((s_tok, 512), F32), _sds((s_tok, 512), F32),
                   _sds((NA_WR, NA_HEADS, GRID_W, NA_KEYS), F32)),
        grid=(ROWS,),
        in_specs=[row, full, full, row, row, pl.BlockSpec((GRID_W, NA_HEADS), lambda r: (r, 0)), bias_spec],
        out_specs=(row, full, full, bias_spec), compiler_params=_params(("arbitrary",)))(q, k, v, o, do, lse, bias)


def t5_bucket_map():
    rel = np.arange(SW_KEYS)[None, :] - SW_BLK - np.arange(SW_BLK)[:, None]
    nb = 16
    max_exact = nb // 2
    n = np.abs(rel)
    large = max_exact + (np.log(np.maximum(n, 1) / max_exact) / np.log(128 / max_exact) * (nb - max_exact)).astype(np.int32)
    large = np.minimum(large, nb - 1)
    return ((rel > 0) * nb + np.where(n < max_exact, n, large)).astype(np.int32)


def t5_bias(table, *, name):
    rel = np.arange(-SW_BLK, SW_BLK + 1)
    nb, max_exact = 16, 8
    n = np.abs(rel)
    large = max_exact + (np.log(np.maximum(n, 1) / max_exact) / np.log(128 / max_exact) * (nb - max_exact)).astype(np.int32)
    bucket = ((rel > 0) * nb + np.where(n < max_exact, n, np.minimum(large, nb - 1))).astype(np.int32)
    u = jnp.pad(table[jnp.asarray(bucket)].T, ((0, 0), (0, SW_KEYS - bucket.shape[0]))).reshape(8, 1, SW_KEYS)

    def body(u_ref, o_ref):
        for h in range(8):
            x = jnp.broadcast_to(u_ref[h], (SW_BLK, SW_KEYS))
            o_ref[h] = pltpu.roll(x, 0, 1, stride=1, stride_axis=0)

    return pl.pallas_call(body, name=name, out_shape=_sds((8, SW_BLK, SW_KEYS), F32), compiler_params=_params())(u)


def _sw_valid(n):
    a = lax.broadcasted_iota(jnp.int32, (SW_BLK, SW_KEYS), 0)
    j = lax.broadcasted_iota(jnp.int32, (SW_BLK, SW_KEYS), 1)
    kpos = (n - 1) * SW_BLK + j
    return (jnp.abs(j - SW_BLK - a) <= SW_BLK) & (kpos >= 0) & (kpos < SEQ)


def _dup_group(x2, g, first):
    rolled = pltpu.roll(x2, HD, 1)
    return jnp.where(first, x2, rolled) if g == 0 else jnp.where(first, rolled, x2)


def sw_fwd(q, kv, t5, sink, *, name):
    def body(q_ref, kv_ref, t5_ref, sink_ref, o_ref, lse_ref):
        n = pl.program_id(0)
        off = pl.multiple_of(n * SW_BLK, SW_BLK)
        first = _lane_halves()
        sels = [first, jnp.logical_not(first)]
        valid = _sw_valid(n)
        k2 = kv_ref[pl.ds(off, SW_KEYS), 0:128]
        v2 = kv_ref[pl.ds(off, SW_KEYS), 128:256]
        kk = [_dup_group(k2, g, first) for g in range(2)]
        vv = [_dup_group(v2, g, first) for g in range(2)]
        q2s = [q_ref[:, 128 * j:128 * j + 128] for j in range(4)]
        scores = []
        for h in range(8):
            j, half = divmod(h, 2)
            scores.append(_dot(jnp.where(sels[half], q2s[j], jnp.zeros_like(q2s[j])), kk[j // 2], NT))
        probs, lses = [], []
        for h in range(8):
            s = jnp.where(valid, scores[h] + t5_ref[h], NEG)
            snk = sink_ref[h]
            m = jnp.maximum(jnp.max(s, axis=-1, keepdims=True), snk)
            e = jnp.exp(s - m)
            den = jnp.sum(e, axis=-1, keepdims=True) + jnp.exp(snk - m)
            probs.append((e / den).astype(BF16))
            lses.append(m + jnp.log(den))
        outs = []
        for j in range(4):
            vg = vv[j // 2]
            zero = jnp.zeros_like(vg)
            outs.append(_dot(probs[2 * j], jnp.where(sels[0], vg, zero)) + _dot(probs[2 * j + 1], jnp.where(sels[1], vg, zero)))
        o_ref[...] = jnp.concatenate(outs, axis=1).astype(BF16)
        lse_ref[...] = jnp.concatenate(lses, axis=1)

    s_tok = q.shape[0]
    blk = pl.BlockSpec((SW_BLK, 512), lambda n: (n, 0))
    return pl.pallas_call(
        body, name=name, out_shape=(_sds((s_tok, 512), BF16), _sds((s_tok, 8), F32)), grid=(SW_NB,),
        in_specs=[blk, pl.BlockSpec(kv.shape, lambda n: (0, 0)), pl.BlockSpec((8, SW_BLK, SW_KEYS), lambda n: (0, 0, 0)),
                  pl.BlockSpec(memory_space=pltpu.SMEM)],
        out_specs=(blk, pl.BlockSpec((SW_BLK, 8), lambda n: (n, 0))), compiler_params=_params(("parallel",)))(q, kv, t5, sink)


def sw_bwd(q, kv, o, do, lse, t5, sink, *, name):
    def body(q_ref, kv_ref, o_ref, do_ref, lse_ref, t5_ref, sink_ref, dq_ref, dkv_ref, dt5_ref, dsink_ref):
        n = pl.program_id(0)

        @pl.when(n == 0)
        def _():
            dkv_ref[...] = jnp.zeros_like(dkv_ref)
            dt5_ref[...] = jnp.zeros_like(dt5_ref)
            dsink_ref[...] = jnp.zeros_like(dsink_ref)

        off = pl.multiple_of(n * SW_BLK, SW_BLK)
        first = _lane_halves()
        sels = [first, jnp.logical_not(first)]
        valid = _sw_valid(n)
        k2 = kv_ref[pl.ds(off, SW_KEYS), 0:128]
        v2 = kv_ref[pl.ds(off, SW_KEYS), 128:256]
        kk = [_dup_group(k2, g, first) for g in range(2)]
        vv = [_dup_group(v2, g, first) for g in range(2)]
        lanes = [slice(128 * j, 128 * j + 128) for j in range(4)]
        q2s = [q_ref[:, l] for l in lanes]
        do2s = [do_ref[:, l] for l in lanes]
        prods = [do2s[j].astype(F32) * o_ref[:, lanes[j]].astype(F32) for j in range(4)]
        lse = lse_ref[...]
        qhs, dohs, scores, dps = [], [], [], []
        for h in range(8):
            j, half = divmod(h, 2)
            qhs.append(jnp.where(sels[half], q2s[j], jnp.zeros_like(q2s[j])))
            dohs.append(jnp.where(sels[half], do2s[j], jnp.zeros_like(do2s[j])))
            scores.append(_dot(qhs[h], kk[j // 2], NT))
            dps.append(_dot(dohs[h], vv[j // 2], NT))
        pbs, dsbs, dss, dsinks = [], [], [], []
        for h in range(8):
            j, half = divmod(h, 2)
            s = jnp.where(valid, scores[h] + t5_ref[h], NEG)
            lse_h = lse[:, h:h + 1]
            p = jnp.exp(s - lse_h)
            delta = jnp.sum(jnp.where(sels[half], prods[j], 0.0), axis=-1, keepdims=True)
            ds = p * (dps[h] - delta)
            dss.append(ds)
            dsinks.append(-jnp.sum(jnp.exp(sink_ref[h] - lse_h) * delta, axis=0, keepdims=True))
            pbs.append(p.astype(BF16))
            dsbs.append(ds.astype(BF16))
        dt5_ref[...] += jnp.stack(dss)
        dsink_ref[...] += jnp.concatenate(dsinks, axis=1)
        dqs = []
        for j in range(4):
            a, b = 2 * j, 2 * j + 1
            zero = jnp.zeros_like(kk[j // 2])
            dqs.append(_dot(dsbs[a], jnp.where(sels[0], kk[j // 2], zero)) + _dot(dsbs[b], jnp.where(sels[1], kk[j // 2], zero)))
        dq_ref[...] = jnp.concatenate(dqs, axis=1)
        dk_groups, dv_groups = [], []
        for g in range(2):
            dkk = sum(_dot(dsbs[h], qhs[h], TN) for h in range(4 * g, 4 * g + 4))
            dvv = sum(_dot(pbs[h], dohs[h], TN) for h in range(4 * g, 4 * g + 4))
            dk_groups.append(dkk + pltpu.roll(dkk, HD, 1))
            dv_groups.append(dvv + pltpu.roll(dvv, HD, 1))
        dkv_ref[pl.ds(off, SW_KEYS), :] += jnp.concatenate(
            [jnp.where(first, dk_groups[0], dk_groups[1]), jnp.where(first, dv_groups[0], dv_groups[1])], axis=1)

    s_tok = q.shape[0]
    blk = pl.BlockSpec((SW_BLK, 512), lambda n: (n, 0))
    kv_spec = pl.BlockSpec(kv.shape, lambda n: (0, 0))
    t5_spec = pl.BlockSpec((8, SW_BLK, SW_KEYS), lambda n: (0, 0, 0))
    vec = pl.BlockSpec((1, 8), lambda n: (0, 0))
    return pl.pallas_call(
        body, name=name,
        out_shape=(_sds((s_tok, 512), F32), _sds(kv.shape, F32), _sds((8, SW_BLK, SW_KEYS), F32), _sds((1, 8), F32)),
        grid=(SW_NB,), in_specs=[blk, kv_spec, blk, blk, pl.BlockSpec((SW_BLK, 8), lambda n: (n, 0)), t5_spec,
                                 pl.BlockSpec(memory_space=pltpu.SMEM)],
        out_specs=(blk, kv_spec, t5_spec, vec), compiler_params=_params(("arbitrary",)))(q, kv, o, do, lse, t5, sink)


def gate_fwd(zg, bias, pa, ps, *, name, tm=512):
    def body(z0_ref, z1_ref, b0_ref, b1_ref, pa_ref, ps_ref, m_ref):
        g0 = jax.nn.sigmoid(z0_ref[...] + b0_ref[...])
        g1 = jax.nn.sigmoid(z1_ref[...] + b1_ref[...])
        m_ref[...] = (g0 * pa_ref[...] + g1 * ps_ref[...]).astype(BF16)

    s = zg.shape[0]
    half = lambda j: pl.BlockSpec((tm, DM), lambda i, j=j: (i, j))
    bvec = lambda j: pl.BlockSpec((1, DM), lambda i, j=j: (0, j))
    return pl.pallas_call(
        body, name=name, out_shape=_sds((s, DM), BF16), grid=(s // tm,),
        in_specs=[half(0), half(1), bvec(0), bvec(1), half(0), half(0)], out_specs=half(0),
        compiler_params=_params(("parallel",)))(zg, zg, bias, bias, pa, ps)


def gate_bwd(dm, zg, bias, pa, ps, *, name, tm=512):
    def body(dm_ref, z0_ref, z1_ref, b0_ref, b1_ref, pa_ref, ps_ref, dpa_ref, dps_ref, dz_ref, db_ref):
        @pl.when(pl.program_id(0) == 0)
        def _():
            db_ref[...] = jnp.zeros_like(db_ref)

        dm = dm_ref[...]
        g0 = jax.nn.sigmoid(z0_ref[...] + b0_ref[...])
        g1 = jax.nn.sigmoid(z1_ref[...] + b1_ref[...])
        dpa_ref[...] = (dm * g0).astype(BF16)
        dps_ref[...] = (dm * g1).astype(BF16)
        dz0 = dm * pa_ref[...] * g0 * (1.0 - g0)
        dz1 = dm * ps_ref[...] * g1 * (1.0 - g1)
        dz_ref[:, 0:DM] = dz0.astype(BF16)
        dz_ref[:, DM:2 * DM] = dz1.astype(BF16)
        db_ref[:, 0:DM] += jnp.sum(dz0, axis=0, keepdims=True)
        db_ref[:, DM:2 * DM] += jnp.sum(dz1, axis=0, keepdims=True)

    s = zg.shape[0]
    half = lambda j: pl.BlockSpec((tm, DM), lambda i, j=j: (i, j))
    bvec = lambda j: pl.BlockSpec((1, DM), lambda i, j=j: (0, j))
    return pl.pallas_call(
        body, name=name,
        out_shape=(_sds((s, DM), BF16), _sds((s, DM), BF16), _sds((s, GATE_W), BF16), _sds((1, GATE_W), F32)),
        grid=(s // tm,), in_specs=[half(0), half(0), half(1), bvec(0), bvec(1), half(0), half(0)],
        out_specs=(half(0), half(0), pl.BlockSpec((tm, GATE_W), lambda i: (i, 0)), pl.BlockSpec((1, GATE_W), lambda i: (0, 0))),
        compiler_params=_params(("arbitrary",)))(dm, zg, zg, bias, bias, pa, ps)


def loss_head(y, target, *, name, tm=512):
    def body(y_ref, t_ref, dy_ref, l_ref):
        @pl.when(pl.program_id(0) == 0)
        def _():
            l_ref[...] = jnp.zeros_like(l_ref)

        err = y_ref[...] - t_ref[...]
        dy_ref[...] = err * (1.0 / DM)
        l_ref[...] += 0.5 * jnp.sum(jnp.mean(err * err, axis=-1, keepdims=True), axis=0, keepdims=True)

    s = y.shape[0]
    tile = pl.BlockSpec((tm, DM), lambda i: (i, 0))
    return pl.pallas_call(
        body, name=name, out_shape=(_sds((s, DM), F32), _sds((1, 128), F32)), grid=(s // tm,), in_specs=[tile, tile],
        out_specs=(tile, pl.BlockSpec((1, 128), lambda i: (0, 0))), compiler_params=_params(("arbitrary",)))(y, target)


def adamw(w, g, m, v, *, name):
    def body(w_ref, g_ref, m_ref, v_ref, d_ref, nm_ref, nv_ref):
        g = g_ref[...]
        nm = ADAM_B1 * m_ref[...] + (1.0 - ADAM_B1) * g
        nv = ADAM_B2 * v_ref[...] + (1.0 - ADAM_B2) * jnp.square(g)
        m_hat = nm / (1.0 - ADAM_B1 ** ADAM_STEP)
        v_hat = nv / (1.0 - ADAM_B2 ** ADAM_STEP)
        d_ref[...] = -ADAM_LR * (m_hat / (jnp.sqrt(v_hat) + ADAM_EPS) + ADAM_WD * w_ref[...])
        nm_ref[...] = nm
        nv_ref[...] = nv

    b, k, n = w.shape
    tk = k // 4 if k % 32 == 0 else k
    spec = pl.BlockSpec((None, tk, n), lambda i, j: (i, j, 0))
    out = _sds(w.shape, F32)
    return pl.pallas_call(
        body, name=name, out_shape=(out, out, out), grid=(b, k // tk), in_specs=[spec] * 4, out_specs=(spec,) * 3,
        compiler_params=_params(("parallel", "parallel")))(w, g, m, v)


def adamw_layer(w, m, v, mine, theirs, cidx, layer, filled=None, *, name):
    _, k, n = w.shape
    nt = 2
    tk = k // 2 // nt

    def body(c_ref, w_ref, m_ref, v_ref, a_ref, b_ref, *rest):
        g_ref, d_ref, nm_ref, nv_ref = rest[-4:]
        g = jnp.where(pl.program_id(0) == c_ref[0], a_ref[...], b_ref[...])
        g_ref[...] = g
        nm = ADAM_B1 * m_ref[...] + (1.0 - ADAM_B1) * g
        nv = ADAM_B2 * v_ref[...] + (1.0 - ADAM_B2) * jnp.square(g)
        m_hat = nm / (1.0 - ADAM_B1 ** ADAM_STEP)
        v_hat = nv / (1.0 - ADAM_B2 ** ADAM_STEP)
        d_ref[...] = -ADAM_LR * (m_hat / (jnp.sqrt(v_hat) + ADAM_EPS) + ADAM_WD * w_ref[...])
        nm_ref[...] = nm
        nv_ref[...] = nv

    full = pl.BlockSpec((None, tk, n), lambda hf, t, c: (layer, hf * nt + t, 0))
    half_mine = pl.BlockSpec((tk, n), lambda hf, t, c: (jnp.where(hf == c[0], t, 0), 0))
    half_theirs = pl.BlockSpec((tk, n), lambda hf, t, c: (jnp.where(hf != c[0], t, 0), 0))
    out = _sds(w.shape, F32)
    ins, specs, aliases = [cidx, w, m, v, mine, theirs], [full, full, full, half_mine, half_theirs], {}
    if filled is not None:
        aliases = {len(ins) + i: i for i in range(4)}
        ins += list(filled)
        specs += [pl.BlockSpec(memory_space=pl.ANY)] * 4
    return pl.pallas_call(
        body, name=name, out_shape=(out, out, out, out),
        grid_spec=pltpu.PrefetchScalarGridSpec(
            num_scalar_prefetch=1, grid=(2, nt), in_specs=specs, out_specs=(full, full, full, full)),
        input_output_aliases=aliases,
        compiler_params=_params(("arbitrary", "arbitrary")))(*ins)


def t5_table_grad(dt5_a, dt5_b, *, name):
    def body(a_ref, b_ref, map_ref, o_ref):
        d = a_ref[...] + b_ref[...]
        bucket = map_ref[...]
        for b in range(32):
            hit = (bucket == b)[None]
            o_ref[b] = jnp.sum(jnp.sum(jnp.where(hit, d, 0.0), axis=2), axis=1, keepdims=True)

    return pl.pallas_call(
        body, name=name, out_shape=_sds((32, 8, 1), F32), compiler_params=_params())(
            dt5_a, dt5_b, jnp.asarray(t5_bucket_map()))


def rpb_grad(dbias, *, name):
    def body(d_ref, rev_ref, o_ref):
        rev = rev_ref[...]
        for h in range(NA_HEADS):
            for pr in range(NA_WR // 2):
                d = d_ref[h, :, 128 * pr:128 * pr + 128]
                hi = d.astype(BF16)
                lo = (d - hi.astype(F32)).astype(BF16)
                flipped = _dot(rev, hi) + _dot(rev, lo)
                o_ref[h, pr] = jnp.sum(pltpu.roll(flipped, 0, 1, stride=1, stride_axis=0), axis=0, keepdims=True)

    anti = jnp.asarray(np.eye(GRID_W, dtype=np.float32)[::-1], dtype=BF16)
    e = pl.pallas_call(
        body, name=name, out_shape=_sds((NA_WR, NA_HEADS, NA_WR // 2, 1, 128), F32), grid=(NA_WR,),
        in_specs=[pl.BlockSpec((None, NA_HEADS, GRID_W, NA_KEYS), lambda p: (p, 0, 0, 0)),
                  pl.BlockSpec((GRID_W, GRID_W), lambda p: (0, 0))],
        out_specs=pl.BlockSpec((None, NA_HEADS, NA_WR // 2, 1, 128), lambda p: (p, 0, 0, 0, 0)),
        compiler_params=_params(("parallel",)))(dbias, anti)
    nci, nri = 2 * NA_WC - 1, 2 * NA_WR - 1
    e = e.reshape(NA_WR, NA_HEADS, NA_WR // 2, 128).transpose(0, 2, 1, 3).reshape(NA_WR * NA_WR // 2, NA_HEADS, 128)
    parts = jnp.concatenate([e[..., 48:48 + nci], jnp.concatenate([e[..., 112:128], e[..., 0:nci - 16]], axis=-1)], axis=0)
    p, pr = np.arange(NA_WR)[:, None], np.arange(NA_WR // 2)[None, :]
    ri = np.concatenate([(2 * pr - p + NA_WR - 1).reshape(-1), (2 * pr - p + NA_WR).reshape(-1)])
    pick = jnp.asarray((ri[None, :] == np.arange(16)[:, None]).astype(np.float32))
    out = mm(pick, parts.reshape(2 * NA_WR * NA_WR // 2, NA_HEADS * nci), name=name + "_rows", exact=True)
    return out.reshape(16, NA_HEADS, nci)[:nri].transpose(1, 0, 2)


BIG = ("ffn1_w_gate", "ffn1_w_up", "ffn1_w_down", "w_in", "w_branch_na", "w_branch_sw", "w_out",
       "ffn2_w_gate", "ffn2_w_up", "ffn2_w_down")
SMALL = ("ffn1_norm", "mix_norm", "b_gate", "na_q_norm", "na_k_norm", "na_rpb", "sw_q_norm", "sw_k_norm", "sw_sink",
         "ffn2_norm")


def _cols_to_full(w4):
    return w4.transpose(1, 0, 2).reshape(w4.shape[1], NSH * w4.shape[2])


def _full_to_cols(w):
    return w.reshape(w.shape[0], NSH, w.shape[1] // NSH).transpose(1, 0, 2)


def _mixer_weights(g):
    w_in_t = g["w_in"].reshape(IN_W, DM)
    return dict(w_att_t=w_in_t[:ATT_W], w_gz_t=w_in_t[ATT_W:], wa=_cols_to_full(g["w_branch_na"]),
                ws=_cols_to_full(g["w_branch_sw"]), wo=g["w_out"].reshape(DM, DM))


GROUPS = {"ffn1": ("ffn1_w_gate", "ffn1_w_up", "ffn1_w_down"), "mix": ("w_in", "w_branch_na", "w_branch_sw", "w_out"),
          "ffn2": ("ffn2_w_gate", "ffn2_w_up", "ffn2_w_down")}


def layer_fwd(x, p, weights, t5b):
    row = lambda v: v.reshape(1, -1)
    stacked = lambda g: {n: a.reshape(DFF, DM) for n, a in g.items()}
    g1 = stacked(weights("ffn1", x))
    y1, h1, gg1, uu1 = ffn_fwd(x, row(p["ffn1_norm"]), g1["ffn1_w_gate"], g1["ffn1_w_up"], g1["ffn1_w_down"], name="ffn_fwd")
    w = _mixer_weights(weights("mix", y1))
    hm = rms_fwd(y1, row(p["mix_norm"]), name="mix_norm_fwd")
    z = mm(hm, w["w_att_t"], tb=True, name="proj_att", tm=SEQ, tn=768)
    zg = mm(hm, w["w_gz_t"], tb=True, name="proj_gate", tm=SEQ, tn=512)
    qa, ka, va, qs, kv = qknorm_fwd(z, p["na_q_norm"], p["na_k_norm"], p["sw_q_norm"], p["sw_k_norm"], name="qknorm_fwd")
    bias = na_bias_table(p["na_rpb"], name="na_bias_table")
    o_na, lse_na = na_fwd(qa, ka, va, bias, name="na_fwd")
    kvp = jnp.pad(kv, ((SW_BLK, SW_BLK), (0, 0)))
    sink = p["sw_sink"]
    o_sw, lse_sw = sw_fwd(qs, kvp, t5b, sink, name="sw_fwd")
    pa = mm(o_na, w["wa"], name="branch_na", tm=1024)
    ps = mm(o_sw, w["ws"], name="branch_sw", tm=1024)
    merged = gate_fwd(zg, row(p["b_gate"]), pa, ps, name="gate_fwd")
    y2 = mm(merged, w["wo"], add=y1, name="out_proj", tm=1024)
    g2 = stacked(weights("ffn2", y2))
    y3, h2, gg2, uu2 = ffn_fwd(y2, row(p["ffn2_norm"]), g2["ffn2_w_gate"], g2["ffn2_w_up"], g2["ffn2_w_down"], name="ffn_fwd")
    saved = dict(x=x, y1=y1, h1=h1, gg1=gg1, uu1=uu1, hm=hm, z=z, zg=zg, qa=qa, ka=ka, va=va, qs=qs, kvp=kvp, bias=bias,
                 o_na=o_na, lse_na=lse_na, o_sw=o_sw, lse_sw=lse_sw, pa=pa, ps=ps, merged=merged, y2=y2, h2=h2, gg2=gg2,
                 uu2=uu2, w=w, sink=sink, g1=g1, g2=g2)
    return y3, saved


def layer_bwd(dy3, sv, p, t5b, emit, dep=None):
    w, g1, g2 = sv["w"], sv["g1"], sv["g2"]
    row = lambda v: v.reshape(1, -1)
    fold = lambda v: v.reshape(-1, HD).sum(axis=0)
    small = {}
    dy2, small["ffn2_norm"], act, dg, du = ffn_bwd_tokens(
        dy3, sv["y2"], row(p["ffn2_norm"]), sv["gg2"], sv["uu2"], g2["ffn2_w_gate"], g2["ffn2_w_up"], g2["ffn2_w_down"],
        name="ffn_bwd_tokens", dep=dep)
    shards = lambda gs: [g.reshape(NSH, FSH, DM) for g in gs]
    token = emit("ffn2", shards(ffn_bwd_weights(sv["h2"], dy3.astype(BF16), act, dg, du, name="ffn_bwd_weights")))
    dmerged = mm(dy2, w["wo"], tb=True, name="out_proj_dx", tm=1024, dep=token)
    gw_out = mm(sv["merged"], dy2, ta=True, out_dtype=BF16, name="out_proj_dw").reshape(NSH, DM // NSH, DM)
    dpa, dps, dzg, small["b_gate"] = gate_bwd(dmerged, sv["zg"], row(p["b_gate"]), sv["pa"], sv["ps"], name="gate_bwd")
    gw_na = _full_to_cols(mm(sv["o_na"], dpa, ta=True, out_dtype=BF16, name="branch_dw"))
    gw_sw = _full_to_cols(mm(sv["o_sw"], dps, ta=True, out_dtype=BF16, name="branch_dw"))
    do_na = mm(dpa, w["wa"], tb=True, out_dtype=BF16, tm=SEQ, name="branch_dx")
    do_sw = mm(dps, w["ws"], tb=True, out_dtype=BF16, tm=SEQ, name="branch_dx")
    dqa, dka, dva, dbias = na_bwd(sv["qa"], sv["ka"], sv["va"], sv["o_na"], do_na, sv["lse_na"], sv["bias"], name="na_bwd")
    dqs, dkvp, dt5, dsink = sw_bwd(sv["qs"], sv["kvp"], sv["o_sw"], do_sw, sv["lse_sw"], t5b, sv["sink"], name="sw_bwd")
    dkv = dkvp[SW_BLK:SW_BLK + SEQ]
    dz, dgqa, dgka, dgqs, dgks = qknorm_bwd(sv["z"], dqa, dka, dva, dqs, dkv, p["na_q_norm"], p["na_k_norm"],
                                            p["sw_q_norm"], p["sw_k_norm"], name="qknorm_bwd")
    small["na_q_norm"], small["na_k_norm"], small["sw_q_norm"], small["sw_k_norm"] = fold(dgqa), fold(dgka), fold(dgqs), fold(dgks)
    small["na_rpb"] = rpb_grad(dbias, name="rpb_grad")
    small["sw_sink"] = dsink
    gw_att_t = mm(dz, sv["hm"], ta=True, out_dtype=BF16, tm=768, name="proj_att_dw")
    gw_gz_t = mm(dzg, sv["hm"], ta=True, out_dtype=BF16, tm=1024, name="proj_gate_dw")
    gw_in = jnp.concatenate([gw_att_t, gw_gz_t], axis=0).reshape(NSH, IN_W // NSH, DM)
    token = emit("mix", (gw_in, gw_na, gw_sw, gw_out))
    dh = mm(dz, w["w_att_t"], tm=1024, name="proj_att_dx", dep=token)
    dh = mm(dzg, w["w_gz_t"], add=dh, tm=1024, name="proj_gate_dx")
    dy1, small["mix_norm"] = rms_bwd(dh, sv["y1"], row(p["mix_norm"]), dy2, name="mix_norm_bwd")
    dx, small["ffn1_norm"], act, dg, du = ffn_bwd_tokens(
        dy1, sv["x"], row(p["ffn1_norm"]), sv["gg1"], sv["uu1"], g1["ffn1_w_gate"], g1["ffn1_w_up"], g1["ffn1_w_down"],
        name="ffn_bwd_tokens")
    emit("ffn1", shards(ffn_bwd_weights(sv["h1"], dy1.astype(BF16), act, dg, du, name="ffn_bwd_weights")))
    return dx, small, dt5


ANY = pl.BlockSpec(memory_space=pl.ANY)


def _place():
    x, y, c = lax.axis_index("x"), lax.axis_index("y"), lax.axis_index("c")
    chips = [(1 - x, y), (x, 1 - y), (1 - x, 1 - y)]
    return x, y, c, chips


def _remote(src, dst, send_sem, recv_sem, to):
    return pltpu.make_async_remote_copy(src_ref=src, dst_ref=dst, send_sem=send_sem, recv_sem=recv_sem, device_id=to,
                                        device_id_type=MESH)


HBM = pl.BlockSpec(memory_space=pltpu.HBM)
SEM = pl.BlockSpec(memory_space=pltpu.SEMAPHORE)
ORDERED_EFFECT = pltpu.SideEffectType.DATAFLOW_SIDE_EFFECTING


def _in_hbm(v):
    return pltpu.with_memory_space_constraint(v, pltpu.HBM)


def _row_half(ref_shape_rows, c):
    half = ref_shape_rows // 2
    return pl.ds(c * half, half)


def _ici_gather_copies(w, land, send_sems, recv_sems):
    x, y, c, chips = _place()
    me = 2 * x + y
    copies = []
    for a in range(len(w)):
        rows = _row_half(w[a].shape[0], c)
        for k, chip in enumerate(chips):
            copies.append(_remote(w[a].at[rows], land[a].at[me, rows], send_sems.at[4 * a + k], recv_sems.at[4 * a + k],
                                  (*chip, c)))
        copies.append(_remote(w[a], land[a].at[me], send_sems.at[4 * a + 3], recv_sems.at[4 * a + 3], (x, y, 1 - c)))
    return copies


def _d2d_gather_copies(w, land, send_sems, recv_sems):
    x, y, c, chips = _place()
    copies = []
    for a in range(len(w)):
        rows = _row_half(w[a].shape[0], c)
        for k, (cx, cy) in enumerate(chips):
            blk = land[a].at[2 * cx + cy, rows]
            copies.append(_remote(blk, blk, send_sems.at[3 * a + k], recv_sems.at[3 * a + k], (x, y, 1 - c)))
    return copies


def _d2d_gather_waits(w, land, send_sems, recv_sems):
    x, y, c, chips = _place()
    waits = []
    for a in range(len(w)):
        rows = _row_half(w[a].shape[0], 1 - c)
        for k, (cx, cy) in enumerate(chips):
            blk = land[a].at[2 * cx + cy, rows]
            waits.append(_remote(blk, blk, send_sems.at[3 * a + k], recv_sems.at[3 * a + k], (x, y, 1 - c)))
    return waits


def gather_start(groups, *, name):
    sizes = [len(g) for g in groups]
    shards = [s for g in groups for s in g]
    n, ng = len(shards), len(groups)

    def body(*refs):
        w, land, sems = refs[:n], refs[n:2 * n], refs[2 * n:2 * n + 2 * ng]
        off = 0
        for gi, size in enumerate(sizes):
            for cp in _ici_gather_copies(w[off:off + size], land[off:off + size], sems[2 * gi], sems[2 * gi + 1]):
                cp.start()
            off += size

    lands = [lax.empty((NSH,) + s.shape, s.dtype) for s in shards]
    sem_shapes = tuple(pltpu.SemaphoreType.DMA((4 * size,)) for size in sizes for _ in range(2))
    res = pl.pallas_call(
        body, name=name,
        out_shape=sem_shapes + tuple(pltpu.HBM(s.shape, s.dtype) for s in shards) + tuple(pltpu.HBM(l.shape, l.dtype) for l in lands),
        in_specs=[HBM] * (2 * n), out_specs=(SEM,) * (2 * ng) + (HBM,) * (2 * n),
        input_output_aliases={i: 2 * ng + i for i in range(2 * n)},
        compiler_params=pltpu.CompilerParams(has_side_effects=ORDERED_EFFECT))(
            *[_in_hbm(s) for s in shards], *[_in_hbm(l) for l in lands])
    out, off = [], 0
    for gi, size in enumerate(sizes):
        out.append((res[2 * gi], res[2 * gi + 1], list(res[2 * ng + off:2 * ng + off + size]),
                    list(res[2 * ng + n + off:2 * ng + n + off + size])))
        off += size
    return out


def gather_wait(send_sems, recv_sems, shards, lands, after, *, name):
    n = len(shards)

    def body(*refs):
        w, land = refs[:n], refs[n:2 * n]
        send, recv = refs[2 * n:2 * n + 2]
        for cp in _ici_gather_copies(w, land, send, recv):
            cp.wait_send()
            cp.wait_recv()

    res = pl.pallas_call(
        body, name=name,
        out_shape=tuple(pltpu.HBM(s.shape, s.dtype) for s in shards) + tuple(pltpu.HBM(l.shape, l.dtype) for l in lands),
        in_specs=[HBM] * (2 * n) + [SEM, SEM, ANY], out_specs=(HBM,) * (2 * n),
        input_output_aliases={i: i for i in range(2 * n)},
        compiler_params=pltpu.CompilerParams(has_side_effects=ORDERED_EFFECT))(*shards, *lands, send_sems, recv_sems, after)
    return list(res[:n]), list(res[n:])


def gather_finish(shards, lands, *, name):
    n = len(shards)

    def body(*refs):
        w, land = refs[:n], refs[n:2 * n]
        send_sems, recv_sems = refs[3 * n:]
        d2d = _d2d_gather_copies(w, land, send_sems, recv_sems)
        for cp in d2d:
            cp.start()
        for cp in _d2d_gather_waits(w, land, send_sems, recv_sems):
            cp.wait_recv()
        for cp in d2d:
            cp.wait_send()

    return list(pl.pallas_call(
        body, name=name, out_shape=tuple(pltpu.HBM(l.shape, l.dtype) for l in lands),
        in_specs=[ANY] * (2 * n), out_specs=tuple([ANY] * n), input_output_aliases={n + i: i for i in range(n)},
        scratch_shapes=[pltpu.SemaphoreType.DMA((3 * n,)), pltpu.SemaphoreType.DMA((3 * n,))])(*shards, *lands))


def pair_exchange(grads, *, name):
    n = len(grads)

    def body(*refs):
        g, buf = refs[:n], refs[n:2 * n]
        send_sems, recv_sems = refs[2 * n:]
        x, y, c, _ = _place()
        copies = []
        for a in range(n):
            half = g[a].shape[1] // 2
            cp = _remote(g[a].at[:, pl.ds((1 - c) * half, half)], buf[a], send_sems.at[a], recv_sems.at[a], (x, y, 1 - c))
            cp.start()
            copies.append(cp)
        for cp in copies:
            cp.wait()

    return pl.pallas_call(
        body, name=name, out_shape=tuple(pltpu.HBM((NSH, g.shape[1] // 2, g.shape[2]), g.dtype) for g in grads),
        in_specs=[ANY] * n, out_specs=tuple([ANY] * n),
        scratch_shapes=[pltpu.SemaphoreType.DMA((n,)), pltpu.SemaphoreType.DMA((n,))])(*grads)


def _chip_exchange_copies(s, buf, send_sems, recv_sems):
    x, y, c, chips = _place()
    return [_remote(s[a].at[2 * cx + cy], buf[a].at[k], send_sems.at[3 * a + k], recv_sems.at[3 * a + k], (cx, cy, c))
            for a in range(len(s)) for k, (cx, cy) in enumerate(chips)]


def chip_exchange(sums, *, name):
    n = len(sums)

    def body(*refs):
        copies = _chip_exchange_copies(refs[:n], refs[n:2 * n], *refs[2 * n:])
        for cp in copies:
            cp.start()
        for cp in copies:
            cp.wait()

    return pl.pallas_call(
        body, name=name, out_shape=tuple(pltpu.HBM((3,) + s.shape[1:], s.dtype) for s in sums),
        in_specs=[ANY] * n, out_specs=tuple([ANY] * n),
        scratch_shapes=[pltpu.SemaphoreType.DMA((3 * n,)), pltpu.SemaphoreType.DMA((3 * n,))])(*sums)


def chip_exchange_start(sums, *, name):
    n = len(sums)

    def body(*refs):
        for cp in _chip_exchange_copies(refs[:n], refs[n:2 * n], refs[2 * n], refs[2 * n + 1]):
            cp.start()
        refs[-1][...] = jnp.zeros_like(refs[-1])

    lands = [lax.empty((3,) + s.shape[1:], s.dtype) for s in sums]
    res = pl.pallas_call(
        body, name=name,
        out_shape=(pltpu.SemaphoreType.DMA((3 * n,)), pltpu.SemaphoreType.DMA((3 * n,)))
        + tuple(pltpu.HBM(s.shape, s.dtype) for s in sums) + tuple(pltpu.HBM(l.shape, l.dtype) for l in lands)
        + (_sds((8, 128), F32),),
        in_specs=[HBM] * (2 * n), out_specs=(SEM, SEM) + (HBM,) * (2 * n) + (pl.BlockSpec(memory_space=pltpu.VMEM),),
        input_output_aliases={i: 2 + i for i in range(2 * n)},
        compiler_params=pltpu.CompilerParams(has_side_effects=ORDERED_EFFECT))(
            *[_in_hbm(s) for s in sums], *[_in_hbm(l) for l in lands])
    return res[0], res[1], list(res[2:2 + n]), list(res[2 + n:2 + 2 * n]), res[-1]


def chip_exchange_wait(send_sems, recv_sems, sums, lands, after, *, name):
    n = len(sums)

    def body(*refs):
        for cp in _chip_exchange_copies(refs[:n], refs[n:2 * n], refs[2 * n], refs[2 * n + 1]):
            cp.wait_send()
            cp.wait_recv()

    res = pl.pallas_call(
        body, name=name,
        out_shape=tuple(pltpu.HBM(s.shape, s.dtype) for s in sums) + tuple(pltpu.HBM(l.shape, l.dtype) for l in lands),
        in_specs=[HBM] * (2 * n) + [SEM, SEM] + [ANY] * len(after), out_specs=(HBM,) * (2 * n),
        input_output_aliases={i: i for i in range(2 * n)},
        compiler_params=pltpu.CompilerParams(has_side_effects=ORDERED_EFFECT))(*sums, *lands, send_sems, recv_sems, *after)
    return list(res[:n]), list(res[n:])


def pair_send(halves, *, name):
    n = len(halves)

    def body(*refs):
        h, got = refs[:n], refs[n:2 * n]
        send_sems, recv_sems = refs[2 * n:]
        x, y, c, _ = _place()
        copies = []
        for i in range(n):
            cp = _remote(h[i], got[i], send_sems.at[i], recv_sems.at[i], (x, y, 1 - c))
            cp.start()
            copies.append(cp)
        for cp in copies:
            cp.wait()

    return list(pl.pallas_call(
        body, name=name, out_shape=tuple(pltpu.HBM(v.shape, v.dtype) for v in halves),
        in_specs=[ANY] * n, out_specs=tuple([ANY] * n),
        scratch_shapes=[pltpu.SemaphoreType.DMA((n,)), pltpu.SemaphoreType.DMA((n,))])(*halves))


def allreduce_small(v, *, name):
    rows = v.shape[0]

    def body(v_ref, o_ref, gath, send_sems, recv_sems):
        x, y, c, _ = _place()
        me = 4 * x + 2 * y + c
        gath[me] = v_ref[...]
        copies = []
        for k in range(1, 8):
            fx, fy, fc = (k >> 2) & 1, (k >> 1) & 1, k & 1
            peer = (jnp.where(fx, 1 - x, x), jnp.where(fy, 1 - y, y), jnp.where(fc, 1 - c, c))
            cp = _remote(v_ref, gath.at[me], send_sems.at[k - 1], recv_sems.at[k - 1], peer)
            cp.start()
            copies.append(cp)
        for cp in copies:
            cp.wait()
        acc = gath[0]
        for d in range(1, 8):
            acc = acc + gath[d]
        o_ref[...] = acc

    return pl.pallas_call(
        body, name=name, out_shape=_sds(v.shape, F32),
        in_specs=[pl.BlockSpec(memory_space=pltpu.VMEM)], out_specs=pl.BlockSpec(memory_space=pltpu.VMEM),
        scratch_shapes=[pltpu.VMEM((8, rows, 128), F32), pltpu.SemaphoreType.DMA((7,)), pltpu.SemaphoreType.DMA((7,))])(v)


def add_halves(g, buf, cidx, *, name):
    _, k, n = g.shape

    def body(c_ref, g_ref, b_ref, o_ref):
        o_ref[...] = (g_ref[...].astype(F32) + b_ref[...].astype(F32)).astype(BF16)

    blk = pl.BlockSpec((None, k // 2, n), lambda s, c: (s, 0, 0))
    return pl.pallas_call(
        body, name=name, out_shape=_sds(buf.shape, BF16),
        grid_spec=pltpu.PrefetchScalarGridSpec(
            num_scalar_prefetch=1, grid=(NSH,),
            in_specs=[pl.BlockSpec((None, k // 2, n), lambda s, c: (s, c[0], 0)), blk], out_specs=blk),
        compiler_params=_params(("parallel",)))(cidx, g, buf)


def add_chips(sums, buf, sidx, *, name, dep=None):
    _, kh, n = sums.shape

    def body(s_ref, mine_ref, b_ref, *rest):
        o_ref = rest[-1]
        o_ref[...] = (mine_ref[...].astype(F32) + b_ref[0].astype(F32)) + (b_ref[1].astype(F32) + b_ref[2].astype(F32))

    ins, specs = _with_dep(
        [sums, buf], [pl.BlockSpec((None, kh, n), lambda i, s: (s[0], 0, 0)), pl.BlockSpec((3, kh, n), lambda i, s: (0, 0, 0))],
        dep)
    return pl.pallas_call(
        body, name=name, out_shape=_sds((kh, n), F32),
        grid_spec=pltpu.PrefetchScalarGridSpec(
            num_scalar_prefetch=1, grid=(1,), in_specs=specs, out_specs=pl.BlockSpec((kh, n), lambda i, s: (0, 0))),
        compiler_params=_params(("arbitrary",)))(sidx, *ins)


PARAMS = ("ffn1_norm", "ffn1_w_gate", "ffn1_w_up", "ffn1_w_down", "mix_norm", "w_in", "b_gate", "na_q_norm", "na_k_norm",
          "na_rpb", "sw_q_norm", "sw_k_norm", "sw_sink", "t5_rel_table", "w_branch_na", "w_branch_sw", "w_out", "ffn2_norm",
          "ffn2_w_gate", "ffn2_w_up", "ffn2_w_down")
SMALL_ALL = tuple(n for n in PARAMS if n not in BIG)
TRANSPOSED = ("ffn1_w_gate", "ffn1_w_up", "w_in", "ffn2_w_gate", "ffn2_w_up")
SMALL_ROWS = 152


def _pack_small(vals):
    flat = jnp.concatenate([vals[n].reshape(-1).astype(F32) for n in SMALL_ALL] + [vals["loss"].reshape(-1)])
    return jnp.pad(flat, (0, SMALL_ROWS * 128 - flat.shape[0])).reshape(SMALL_ROWS, 128)


def _unpack_small(packed, like):
    flat, out, off = packed.reshape(-1), {}, 0
    for n in SMALL_ALL:
        size = math.prod(like[n].shape)
        out[n] = flat[off:off + size].reshape(like[n].shape)
        off += size
    out["loss"] = flat[off]
    return out


def kernel(x, ffn1_norm, ffn1_w_gate, ffn1_w_up, ffn1_w_down, mix_norm, w_in, b_gate, na_q_norm, na_k_norm, na_rpb, sw_q_norm, sw_k_norm, sw_sink, t5_rel_table, w_branch_na, w_branch_sw, w_out, ffn2_norm, ffn2_w_gate, ffn2_w_up, ffn2_w_down, loss_target, m_ffn1_norm, m_ffn1_w_gate, m_ffn1_w_up, m_ffn1_w_down, m_mix_norm, m_w_in, m_b_gate, m_na_q_norm, m_na_k_norm, m_na_rpb, m_sw_q_norm, m_sw_k_norm, m_sw_sink, m_t5_rel_table, m_w_branch_na, m_w_branch_sw, m_w_out, m_ffn2_norm, m_ffn2_w_gate, m_ffn2_w_up, m_ffn2_w_down, v_ffn1_norm, v_ffn1_w_gate, v_ffn1_w_up, v_ffn1_w_down, v_mix_norm, v_w_in, v_b_gate, v_na_q_norm, v_na_k_norm, v_na_rpb, v_sw_q_norm, v_sw_k_norm, v_sw_sink, v_t5_rel_table, v_w_branch_na, v_w_branch_sw, v_w_out, v_ffn2_norm, v_ffn2_w_gate, v_ffn2_w_up, v_ffn2_w_down):
    args = locals()
    tr = lambda n, a: jnp.transpose(a, (0, 2, 1)) if n in TRANSPOSED else a
    w = {n: tr(n, args[n]) for n in PARAMS}
    m = {n: tr(n, args["m_" + n]) for n in PARAMS}
    v = {n: tr(n, args["v_" + n]) for n in PARAMS}
    cidx = lax.axis_index("c").astype(jnp.int32).reshape(1)
    sidx = (2 * lax.axis_index("x") + lax.axis_index("y")).astype(jnp.int32).reshape(1)

    small = [{n: w[n][l] for n in SMALL} for l in range(DEPTH)]
    t5b = t5_bias(w["t5_rel_table"], name="t5_bias")
    order = ("ffn1", "mix", "ffn2")

    keys = [(l, g) for l in range(DEPTH) for g in order]
    in_flight = dict(zip(keys, gather_start([[w[n][l].astype(BF16) for n in GROUPS[g]] for l, g in keys], name="gather_start")))

    def weights_of(l):
        def get(group, after):
            send_sems, recv_sems, thru, lands = in_flight[(l, group)]
            thru, lands = gather_wait(send_sems, recv_sems, thru, lands, after, name="gather_wait")
            return dict(zip(GROUPS[group], gather_finish(thru, lands, name="gather_finish")))
        return get

    h0, saved0 = layer_fwd(x[0], small[0], weights_of(0), t5b)
    h1, saved1 = layer_fwd(h0, small[1], weights_of(1), t5b)
    dy, loss_row = loss_head(h1, loss_target[0], name="loss_head")

    crossing, tokens = {}, []

    def reduce_of(l):
        def emit(group, grads):
            grads = list(grads)
            sums = [add_halves(g, b, cidx, name="add_halves") for g, b in zip(grads, pair_exchange(grads, name="pair_exchange"))]
            send_sems, recv_sems, sums, lands, token = chip_exchange_start(sums, name="chip_exchange_start")
            crossing[(l, group)] = (send_sems, recv_sems, sums, lands)
            tokens.append(token)
            return token
        return emit

    def arrived(l, after):
        halves = {}
        for group in order:
            send_sems, recv_sems, sums, lands = crossing[(l, group)]
            sums, got = chip_exchange_wait(send_sems, recv_sems, sums, lands, after, name="chip_exchange_wait")
            for n, s, b in zip(GROUPS[group], sums, got):
                halves[n] = add_chips(s, b, sidx, name="add_chips")
        return [halves[n] for n in BIG]

    dy, small1, dt5_1 = layer_bwd(dy, saved1, small[1], t5b, reduce_of(1))
    grad_x, small0, dt5_0 = layer_bwd(dy, saved0, small[0], t5b, reduce_of(0), dep=tokens[-1])
    halves1 = arrived(1, [tokens[-1]])
    theirs1 = pair_send(halves1, name="pair_send")
    done1 = [adamw_layer(w[n], m[n], v[n], halves1[a], theirs1[a], cidx, 1, name="adamw_layer") for a, n in enumerate(BIG)]

    smalls = [small0, small1]
    dt5 = t5_table_grad(dt5_0, dt5_1, name="t5_table_grad").reshape(32, 8)
    local_small = {n: jnp.stack([smalls[l][n].reshape(w[n].shape[1:]) for l in range(DEPTH)]) for n in SMALL}
    local_small["t5_rel_table"] = dt5
    local_small["loss"] = loss_row[0, 0:1]
    total = allreduce_small(_pack_small(local_small), name="allreduce_small")
    small_grads = _unpack_small(total, w)
    pack = lambda d: _pack_small({**d, "loss": jnp.zeros((1,), F32)})[None]
    ds, ms, vs = adamw(pack(w), total[None], pack(m), pack(v), name="adamw_small")

    halves0 = arrived(0, [ds, done1[-1][0]])
    theirs0 = pair_send(halves0, name="pair_send")
    grad, delta, new_m, new_v = {}, {}, {}, {}
    for a, n in enumerate(BIG):
        grad[n], delta[n], new_m[n], new_v[n] = adamw_layer(
            w[n], m[n], v[n], halves0[a], theirs0[a], cidx, 0, filled=done1[a], name="adamw_layer")
    for n in SMALL_ALL:
        grad[n] = small_grads[n]
    d_s, m_s, v_s = _unpack_small(ds[0], w), _unpack_small(ms[0], w), _unpack_small(vs[0], w)
    for n in SMALL_ALL:
        delta[n], new_m[n], new_v[n] = d_s[n], m_s[n], v_s[n]

    return (small_grads["loss"], grad_x[None], *[tr(n, grad[n]) for n in PARAMS], *[tr(n, delta[n]) for n in PARAMS],
            *[tr(n, new_m[n]) for n in PARAMS], *[tr(n, new_v[n]) for n in PARAMS])
```

```python
import functools
import math

import jax
import jax.numpy as jnp
import numpy as np
from jax import lax
from jax.experimental import pallas as pl
from jax.experimental.pallas import tpu as pltpu

F32 = jnp.float32
BF16 = jnp.bfloat16

SEQ = 2048
DM = 1024
DFF = 2816
DEPTH = 2
NSH = 4
FSH = DFF // NSH
GRID_W = 64
ROWS = SEQ // GRID_W
NA_HEADS = 8
HD = 64
NA_WR = 8
NA_WC = 16
NA_KEYS = NA_WR * GRID_W
SW_BLK = 128
SW_NB = SEQ // SW_BLK
SW_KEYS = 3 * SW_BLK
ATT_W = 2304
GATE_W = 2048
IN_W = ATT_W + GATE_W
EPS = 1e-6
NEG = -1e30
QK_SCALE = 1.0 / math.sqrt(HD)

ADAM_LR = 0.001
ADAM_B1 = 0.9
ADAM_B2 = 0.999
ADAM_EPS = 1e-08
ADAM_WD = 0.01
ADAM_STEP = 10

VMEM_LIMIT = 56 << 20
MESH = pl.DeviceIdType.MESH

NT = (((1,), (1,)), ((), ()))
TN = (((0,), (0,)), ((), ()))
NN = (((1,), (0,)), ((), ()))


def _dot(a, b, dims=NN):
    return lax.dot_general(a, b, dims, preferred_element_type=F32)


def _params(sem=None):
    return pltpu.CompilerParams(dimension_semantics=sem, vmem_limit_bytes=VMEM_LIMIT)


def _sds(shape, dtype):
    return jax.ShapeDtypeStruct(shape, dtype)


def mm(a, b, *, name, ta=False, tb=False, out_dtype=F32, add=None, scale=None, tm=512, tn=None, tk=None, exact=False,
       dep=None):
    m, kd = (a.shape[1], a.shape[0]) if ta else a.shape
    n = b.shape[0] if tb else b.shape[1]
    tm, tn, tk = min(tm, m), min(tn or n, n), min(tk or kd, kd)
    nk = kd // tk
    dims = (((0 if ta else 1,), (1 if tb else 0,)), ((), ()))

    def body(*refs):
        a_ref, b_ref = refs[:2]
        add_ref = refs[2] if add is not None else None
        o_ref, acc = refs[-2:]
        k = pl.program_id(2)

        @pl.when(k == 0)
        def _():
            acc[...] = jnp.zeros_like(acc)

        if exact:
            acc[...] += lax.dot_general(a_ref[...], b_ref[...], dims, precision=lax.Precision.HIGHEST,
                                        preferred_element_type=F32)
        else:
            acc[...] += lax.dot_general(a_ref[...].astype(BF16), b_ref[...].astype(BF16), dims,
                                        preferred_element_type=F32)

        @pl.when(k == nk - 1)
        def _():
            r = acc[...]
            if scale is not None:
                r = r * scale
            if add is not None:
                r = r + add_ref[...]
            o_ref[...] = r.astype(out_dtype)

    a_spec = pl.BlockSpec((tk, tm), lambda i, j, k: (k, i)) if ta else pl.BlockSpec((tm, tk), lambda i, j, k: (i, k))
    b_spec = pl.BlockSpec((tn, tk), lambda i, j, k: (j, k)) if tb else pl.BlockSpec((tk, tn), lambda i, j, k: (k, j))
    o_spec = pl.BlockSpec((tm, tn), lambda i, j, k: (i, j))
    ins, specs = [a, b], [a_spec, b_spec]
    if add is not None:
        ins.append(add)
        specs.append(o_spec)
    if dep is not None:
        ins.append(dep)
        specs.append(pl.BlockSpec(memory_space=pl.ANY))
    return pl.pallas_call(
        body, name=name, out_shape=_sds((m, n), out_dtype), grid=(m // tm, n // tn, nk), in_specs=specs,
        out_specs=o_spec, scratch_shapes=[pltpu.VMEM((tm, tn), F32)],
        compiler_params=_params(("parallel", "parallel", "arbitrary")))(*ins)


def _rms(x):
    return lax.rsqrt(jnp.mean(x * x, axis=-1, keepdims=True) + EPS)


def rms_fwd(x, gain, *, name, tm=512):
    def body(x_ref, g_ref, h_ref):
        x = x_ref[...]
        h_ref[...] = (x * _rms(x) * g_ref[...]).astype(BF16)

    return pl.pallas_call(
        body, name=name, out_shape=_sds(x.shape, BF16), grid=(x.shape[0] // tm,),
        in_specs=[pl.BlockSpec((tm, DM), lambda i: (i, 0)), pl.BlockSpec((1, DM), lambda i: (0, 0))],
        out_specs=pl.BlockSpec((tm, DM), lambda i: (i, 0)), compiler_params=_params(("parallel",)))(x, gain)


def _rms_bwd_math(dh, x, gain):
    r = _rms(x)
    xh = x * r
    dgain = jnp.sum(dh * xh, axis=0, keepdims=True)
    dxn = dh * gain
    dx = r * (dxn - xh * jnp.mean(dxn * xh, axis=-1, keepdims=True))
    return dx, dgain


def rms_bwd(dh, x, gain, dres, *, name, tm=512):
    def body(dh_ref, x_ref, g_ref, dres_ref, dx_ref, dg_ref):
        @pl.when(pl.program_id(0) == 0)
        def _():
            dg_ref[...] = jnp.zeros_like(dg_ref)

        dx, dg = _rms_bwd_math(dh_ref[...], x_ref[...], g_ref[...])
        dx_ref[...] = dres_ref[...] + dx
        dg_ref[...] += dg

    tile = pl.BlockSpec((tm, DM), lambda i: (i, 0))
    vec = pl.BlockSpec((1, DM), lambda i: (0, 0))
    return pl.pallas_call(
        body, name=name, out_shape=(_sds(x.shape, F32), _sds((1, DM), F32)), grid=(x.shape[0] // tm,),
        in_specs=[tile, tile, vec, tile], out_specs=(tile, vec), compiler_params=_params(("arbitrary",)))(dh, x, gain, dres)


def _with_dep(ins, specs, dep):
    if dep is None:
        return ins, specs
    return ins + [dep], specs + [pl.BlockSpec(memory_space=pl.ANY)]


def _resident_weight():
    return pl.BlockSpec((DFF, DM), lambda i: (0, 0), pipeline_mode=pl.Buffered(1))


def ffn_fwd(x, gain, wg, wu, wd, *, name, tm=512):
    def body(x_ref, g_ref, wg_ref, wu_ref, wd_ref, y_ref, h_ref, gg_ref, uu_ref):
        x = x_ref[...]
        h = (x * _rms(x) * g_ref[...]).astype(BF16)
        h_ref[...] = h
        gg = _dot(h, wg_ref[...], NT)
        uu = _dot(h, wu_ref[...], NT)
        gg_ref[...] = gg.astype(BF16)
        uu_ref[...] = uu.astype(BF16)
        act = (gg * jax.nn.sigmoid(gg) * uu).astype(BF16)
        y_ref[...] = x + 0.5 * _dot(act, wd_ref[...])

    s = x.shape[0]
    tile = pl.BlockSpec((tm, DM), lambda i: (i, 0))
    hid = pl.BlockSpec((tm, DFF), lambda i: (i, 0))
    w = _resident_weight()
    return pl.pallas_call(
        body, name=name,
        out_shape=(_sds((s, DM), F32), _sds((s, DM), BF16), _sds((s, DFF), BF16), _sds((s, DFF), BF16)),
        grid=(s // tm,), in_specs=[tile, pl.BlockSpec((1, DM), lambda i: (0, 0)), w, w, w],
        out_specs=(tile, tile, hid, hid), compiler_params=_params(("parallel",)))(x, gain, wg, wu, wd)


def ffn_bwd_tokens(dy, x, gain, gg, uu, wg, wu, wd, *, name, tm=256, dep=None):
    def body(dy_ref, x_ref, g_ref, gg_ref, uu_ref, wg_ref, wu_ref, wd_ref, *rest):
        dx_ref, dgain_ref, act_ref, dg_ref, du_ref = rest[-5:]

        @pl.when(pl.program_id(0) == 0)
        def _():
            dgain_ref[...] = jnp.zeros_like(dgain_ref)

        dy = dy_ref[...]
        dact = _dot((0.5 * dy).astype(BF16), wd_ref[...], NT)
        g = gg_ref[...].astype(F32)
        u = uu_ref[...].astype(F32)
        sg = jax.nn.sigmoid(g)
        silu = g * sg
        act_ref[...] = (silu * u).astype(BF16)
        dg = (dact * u * (sg * (1.0 + g * (1.0 - sg)))).astype(BF16)
        du = (dact * silu).astype(BF16)
        dg_ref[...] = dg
        du_ref[...] = du
        dx, dgain = _rms_bwd_math(_dot(dg, wg_ref[...]) + _dot(du, wu_ref[...]), x_ref[...], g_ref[...])
        dx_ref[...] = dy + dx
        dgain_ref[...] += dgain

    s = x.shape[0]
    tile = pl.BlockSpec((tm, DM), lambda i: (i, 0))
    vec = pl.BlockSpec((1, DM), lambda i: (0, 0))
    hid = pl.BlockSpec((tm, DFF), lambda i: (i, 0))
    hshape = _sds((s, DFF), BF16)
    w = _resident_weight()
    ins, specs = _with_dep([dy, x, gain, gg, uu, wg, wu, wd], [tile, tile, vec, hid, hid, w, w, w], dep)
    return pl.pallas_call(
        body, name=name, out_shape=(_sds((s, DM), F32), _sds((1, DM), F32), hshape, hshape, hshape),
        grid=(s // tm,), in_specs=specs, out_specs=(tile, vec, hid, hid, hid),
        compiler_params=_params(("arbitrary",)))(*ins)


def ffn_bwd_weights(h, dy, act, dg, du, *, name, tf=256):
    def body(h_ref, dy_ref, act_ref, dg_ref, du_ref, gwg_ref, gwu_ref, gwd_ref):
        h = h_ref[...]
        gwg_ref[...] = _dot(dg_ref[...], h, TN).astype(BF16)
        gwu_ref[...] = _dot(du_ref[...], h, TN).astype(BF16)
        gwd_ref[...] = (0.5 * _dot(act_ref[...], dy_ref[...], TN)).astype(BF16)

    s = h.shape[0]
    full = pl.BlockSpec((s, DM), lambda f: (0, 0))
    hid = pl.BlockSpec((s, tf), lambda f: (0, f))
    wt = pl.BlockSpec((tf, DM), lambda f: (f, 0))
    wshape = _sds((DFF, DM), BF16)
    return pl.pallas_call(
        body, name=name, out_shape=(wshape, wshape, wshape), grid=(DFF // tf,), in_specs=[full, full, hid, hid, hid],
        out_specs=(wt, wt, wt), compiler_params=_params(("parallel",)))(h, dy, act, dg, du)


def _group_mean(v, bd):
    hi = v.astype(BF16)
    lo = (v - hi.astype(F32)).astype(BF16)
    return _dot(hi, bd) + _dot(lo, bd)


def _block_diag(width):
    idx = np.arange(width) // HD
    return jnp.asarray((idx[:, None] == idx[None, :]).astype(np.float32) / HD, dtype=BF16)


def qknorm_fwd(z, gq_na, gk_na, gq_sw, gk_sw, *, name, tm=256):
    def body(zq_ref, zk_ref, zv_ref, zs_ref, zkv_ref, gqa_ref, gka_ref, gqs_ref, gks_ref, bd_ref, bd2_ref,
             qa_ref, ka_ref, va_ref, qs_ref, kv_ref):
        bd = bd_ref[...]

        def norm(x, g, bdm):
            return x * lax.rsqrt(_group_mean(x * x, bdm) + EPS) * g

        qa_ref[...] = (norm(zq_ref[...], gqa_ref[...], bd) * QK_SCALE).astype(BF16)
        ka_ref[...] = norm(zk_ref[...], gka_ref[...], bd).astype(BF16)
        va_ref[...] = zv_ref[...].astype(BF16)
        qs_ref[...] = (norm(zs_ref[...], gqs_ref[...], bd) * QK_SCALE).astype(BF16)
        kv = zkv_ref[...]
        kv_ref[:, 0:128] = norm(kv[:, 0:128], gks_ref[...], bd2_ref[...]).astype(BF16)
        kv_ref[:, 128:256] = kv[:, 128:256].astype(BF16)

    s = z.shape[0]
    col = lambda j: pl.BlockSpec((tm, 512), lambda i, j=j: (i, j))
    vec = lambda w: pl.BlockSpec((1, w), lambda i: (0, 0))
    o512 = pl.BlockSpec((tm, 512), lambda i: (i, 0))
    g512 = lambda g: jnp.tile(g.reshape(1, HD), (1, 8))
    return pl.pallas_call(
        body, name=name,
        out_shape=(_sds((s, 512), BF16),) * 4 + (_sds((s, 256), BF16),), grid=(s // tm,),
        in_specs=[col(0), col(1), col(2), col(3), pl.BlockSpec((tm, 256), lambda i: (i, 8)), vec(512), vec(512), vec(512),
                  vec(128), pl.BlockSpec((512, 512), lambda i: (0, 0)), pl.BlockSpec((128, 128), lambda i: (0, 0))],
        out_specs=(o512, o512, o512, o512, pl.BlockSpec((tm, 256), lambda i: (i, 0))),
        compiler_params=_params(("parallel",)))(
            z, z, z, z, z, g512(gq_na), g512(gk_na), g512(gq_sw), jnp.tile(gk_sw.reshape(1, HD), (1, 2)),
            _block_diag(512), _block_diag(128))


def qknorm_bwd(z, dqa, dka, dva, dqs, dkv, gq_na, gk_na, gq_sw, gk_sw, *, name, tm=256):
    def body(zq_ref, zk_ref, zs_ref, zkv_ref, dqa_ref, dka_ref, dva_ref, dqs_ref, dkv_ref, gqa_ref, gka_ref, gqs_ref,
             gks_ref, bd_ref, bd2_ref, dz_ref, dgqa_ref, dgka_ref, dgqs_ref, dgks_ref):
        @pl.when(pl.program_id(0) == 0)
        def _():
            dgqa_ref[...] = jnp.zeros_like(dgqa_ref)
            dgka_ref[...] = jnp.zeros_like(dgka_ref)
            dgqs_ref[...] = jnp.zeros_like(dgqs_ref)
            dgks_ref[...] = jnp.zeros_like(dgks_ref)

        bd = bd_ref[...]

        def bwd(x, dy, g, bdm, dg_ref):
            r = lax.rsqrt(_group_mean(x * x, bdm) + EPS)
            xh = x * r
            dg_ref[...] += jnp.sum(dy * xh, axis=0, keepdims=True)
            dxn = dy * g
            return r * (dxn - xh * _group_mean(dxn * xh, bdm))

        dz_ref[:, 0:512] = bwd(zq_ref[...], dqa_ref[...] * QK_SCALE, gqa_ref[...], bd, dgqa_ref).astype(BF16)
        dz_ref[:, 512:1024] = bwd(zk_ref[...], dka_ref[...], gka_ref[...], bd, dgka_ref).astype(BF16)
        dz_ref[:, 1024:1536] = dva_ref[...].astype(BF16)
        dz_ref[:, 1536:2048] = bwd(zs_ref[...], dqs_ref[...] * QK_SCALE, gqs_ref[...], bd, dgqs_ref).astype(BF16)
        dkv = dkv_ref[...]
        dz_ref[:, 2048:2176] = bwd(zkv_ref[:, 0:128], dkv[:, 0:128], gks_ref[...], bd2_ref[...], dgks_ref).astype(BF16)
        dz_ref[:, 2176:2304] = dkv[:, 128:256].astype(BF16)

    s = z.shape[0]
    col = lambda j: pl.BlockSpec((tm, 512), lambda i, j=j: (i, j))
    t512 = pl.BlockSpec((tm, 512), lambda i: (i, 0))
    t256 = pl.BlockSpec((tm, 256), lambda i: (i, 0))
    vec = lambda w: pl.BlockSpec((1, w), lambda i: (0, 0))
    g512 = lambda g: jnp.tile(g.reshape(1, HD), (1, 8))
    return pl.pallas_call(
        body, name=name,
        out_shape=(_sds((s, ATT_W), BF16), _sds((1, 512), F32), _sds((1, 512), F32), _sds((1, 512), F32), _sds((1, 128), F32)),
        grid=(s // tm,),
        in_specs=[col(0), col(1), col(3), pl.BlockSpec((tm, 256), lambda i: (i, 8)), t512, t512, t512, t512, t256,
                  vec(512), vec(512), vec(512), vec(128), pl.BlockSpec((512, 512), lambda i: (0, 0)),
                  pl.BlockSpec((128, 128), lambda i: (0, 0))],
        out_specs=(pl.BlockSpec((tm, ATT_W), lambda i: (i, 0)), vec(512), vec(512), vec(512), vec(128)),
        compiler_params=_params(("arbitrary",)))(
            z, z, z, z, dqa, dka, dva, dqs, dkv, g512(gq_na), g512(gk_na), g512(gq_sw),
            jnp.tile(gk_sw.reshape(1, HD), (1, 2)), _block_diag(512), _block_diag(128))


def _na_row_start(r):
    return jnp.clip(r - NA_WR // 2, 0, ROWS - NA_WR)


def na_bias_table(rpb, *, name):
    t = jnp.pad(rpb, ((0, 0), (0, 2), (0, HD - (2 * NA_WC - 1))))
    pairs = jnp.concatenate([t[:, :16], t[:, 1:17]], axis=-1).reshape(NA_HEADS, 16, 1, 128)

    def body(t_ref, o_ref):
        p = pl.program_id(0)
        q = lax.broadcasted_iota(jnp.int32, (GRID_W, 128), 0)
        kc = lax.broadcasted_iota(jnp.int32, (GRID_W, 128), 1) & (GRID_W - 1)
        cs = jnp.clip(q - NA_WC // 2, 0, GRID_W - NA_WC)
        ok = (kc >= cs) & (kc < cs + NA_WC)
        for h in range(NA_HEADS):
            for pr in range(NA_WR // 2):
                x = jnp.broadcast_to(t_ref[h, 2 * pr - p + NA_WR - 1], (GRID_W, 128))
                b = pltpu.roll(x, 128 - (NA_WC - 1), 1, stride=1, stride_axis=0)
                o_ref[h, :, 128 * pr:128 * pr + 128] = jnp.where(ok, b, NEG)

    return pl.pallas_call(
        body, name=name, out_shape=_sds((NA_WR, NA_HEADS, GRID_W, NA_KEYS), F32), grid=(NA_WR,),
        in_specs=[pl.BlockSpec((NA_HEADS, 16, 1, 128), lambda p: (0, 0, 0, 0))],
        out_specs=pl.BlockSpec((None, NA_HEADS, GRID_W, NA_KEYS), lambda p: (p, 0, 0, 0)),
        compiler_params=_params(("parallel",)))(pairs)


def _lane_halves():
    lane = lax.broadcasted_iota(jnp.int32, (1, 128), 1)
    return lane < HD


def na_fwd(q, k, v, bias, *, name):
    def body(q_ref, k_ref, v_ref, b_ref, o_ref, lse_ref):
        r = pl.program_id(0)
        off = pl.multiple_of(_na_row_start(r) * GRID_W, GRID_W)
        first = _lane_halves()
        sels = [first, jnp.logical_not(first)]
        lanes = [slice(128 * j, 128 * j + 128) for j in range(NA_HEADS // 2)]
        q2s = [q_ref[:, l] for l in lanes]
        k2s = [k_ref[pl.ds(off, NA_KEYS), l] for l in lanes]
        v2s = [v_ref[pl.ds(off, NA_KEYS), l] for l in lanes]
        scores = []
        for h in range(NA_HEADS):
            j, half = divmod(h, 2)
            scores.append(_dot(jnp.where(sels[half], q2s[j], jnp.zeros_like(q2s[j])), k2s[j], NT))
        probs, lses = [], []
        for h in range(NA_HEADS):
            b = b_ref[h]
            s = jnp.where(b > 0.5 * NEG, scores[h] + b, NEG)
            m = jnp.max(s, axis=-1, keepdims=True)
            e = jnp.exp(s - m)
            l = jnp.sum(e, axis=-1, keepdims=True)
            probs.append((e / l).astype(BF16))
            lses.append(m + jnp.log(l))
        for j in range(NA_HEADS // 2):
            zero = jnp.zeros_like(v2s[j])
            o2 = (_dot(probs[2 * j], jnp.where(sels[0], v2s[j], zero))
                  + _dot(probs[2 * j + 1], jnp.where(sels[1], v2s[j], zero)))
            o_ref[:, lanes[j]] = o2.astype(BF16)
        lse_ref[...] = jnp.concatenate(lses, axis=1)

    s_tok = q.shape[0]
    full = pl.BlockSpec((s_tok, 512), lambda r: (0, 0))
    return pl.pallas_call(
        body, name=name, out_shape=(_sds((s_tok, 512), BF16), _sds((s_tok, NA_HEADS), F32)), grid=(ROWS,),
        in_specs=[pl.BlockSpec((GRID_W, 512), lambda r: (r, 0)), full, full,
                  pl.BlockSpec((None, NA_HEADS, GRID_W, NA_KEYS), lambda r: (r - _na_row_start(r), 0, 0, 0))],
        out_specs=(pl.BlockSpec((GRID_W, 512), lambda r: (r, 0)), pl.BlockSpec((GRID_W, NA_HEADS), lambda r: (r, 0))),
        compiler_params=_params(("parallel",)))(q, k, v, bias)


def na_bwd(q, k, v, o, do, lse, bias, *, name):
    def body(q_ref, k_ref, v_ref, o_ref, do_ref, lse_ref, b_ref, dq_ref, dk_ref, dv_ref, db_ref):
        r = pl.program_id(0)

        @pl.when(r == 0)
        def _():
            dk_ref[...] = jnp.zeros_like(dk_ref)
            dv_ref[...] = jnp.zeros_like(dv_ref)

        @pl.when((r <= NA_WR // 2) | (r > ROWS - NA_WR // 2))
        def _():
            db_ref[...] = jnp.zeros_like(db_ref)

        off = pl.multiple_of(_na_row_start(r) * GRID_W, GRID_W)
        first = _lane_halves()
        sels = [first, jnp.logical_not(first)]
        lanes = [slice(128 * j, 128 * j + 128) for j in range(NA_HEADS // 2)]
        q2s = [q_ref[:, l] for l in lanes]
        k2s = [k_ref[pl.ds(off, NA_KEYS), l] for l in lanes]
        v2s = [v_ref[pl.ds(off, NA_KEYS), l] for l in lanes]
        do2s = [do_ref[:, l] for l in lanes]
        prods = [do2s[j].astype(F32) * o_ref[:, lanes[j]].astype(F32) for j in range(NA_HEADS // 2)]
        lse = lse_ref[...]
        qhs, dohs, scores, dps = [], [], [], []
        for h in range(NA_HEADS):
            j, half = divmod(h, 2)
            qhs.append(jnp.where(sels[half], q2s[j], jnp.zeros_like(q2s[j])))
            dohs.append(jnp.where(sels[half], do2s[j], jnp.zeros_like(do2s[j])))
            scores.append(_dot(qhs[h], k2s[j], NT))
            dps.append(_dot(dohs[h], v2s[j], NT))
        pbs, dsbs = [], []
        for h in range(NA_HEADS):
            j, half = divmod(h, 2)
            b = b_ref[h]
            s = jnp.where(b > 0.5 * NEG, scores[h] + b, NEG)
            p = jnp.exp(s - lse[:, h:h + 1])
            delta = jnp.sum(jnp.where(sels[half], prods[j], 0.0), axis=-1, keepdims=True)
            ds = p * (dps[h] - delta)
            db_ref[h] += ds
            pbs.append(p.astype(BF16))
            dsbs.append(ds.astype(BF16))
        for j in range(NA_HEADS // 2):
            a, b = 2 * j, 2 * j + 1
            zero = jnp.zeros_like(k2s[j])
            dq_ref[:, lanes[j]] = (_dot(dsbs[a], jnp.where(sels[0], k2s[j], zero))
                                   + _dot(dsbs[b], jnp.where(sels[1], k2s[j], zero)))
            dk_ref[pl.ds(off, NA_KEYS), lanes[j]] += _dot(dsbs[a], qhs[a], TN) + _dot(dsbs[b], qhs[b], TN)
            dv_ref[pl.ds(off, NA_KEYS), lanes[j]] += _dot(pbs[a], dohs[a], TN) + _dot(pbs[b], dohs[b], TN)

    s_tok = q.shape[0]
    full = pl.BlockSpec((s_tok, 512), lambda r: (0, 0))
    row = pl.BlockSpec((GRID_W, 512), lambda r: (r, 0))
    bias_spec = pl.BlockSpec((None, NA_HEADS, GRID_W, NA_KEYS), lambda r: (r - _na_row_start(r), 0, 0, 0))
    return pl.pallas_call(
        body, name=name,
        out_shape=(_sds((s_tok, 512), F32), _sds((s_tok, 512), F32), _sds((s_tok, 512), F32),
                   _sds((NA_WR, NA_HEADS, GRID_W, NA_KEYS), F32)),
        grid=(ROWS,),
        in_specs=[row, full, full, row, row, pl.BlockSpec((GRID_W, NA_HEADS), lambda r: (r, 0)), bias_spec],
        out_specs=(row, full, full, bias_spec), compiler_params=_params(("arbitrary",)))(q, k, v, o, do, lse, bias)


def t5_bucket_map():
    rel = np.arange(SW_KEYS)[None, :] - SW_BLK - np.arange(SW_BLK)[:, None]
    nb = 16
    max_exact = nb // 2
    n = np.abs(rel)
    large = max_exact + (np.log(np.maximum(n, 1) / max_exact) / np.log(128 / max_exact) * (nb - max_exact)).astype(np.int32)
    large = np.minimum(large, nb - 1)
    return ((rel > 0) * nb + np.where(n < max_exact, n, large)).astype(np.int32)


def t5_bias(table, *, name):
    rel = np.arange(-SW_BLK, SW_BLK + 1)
    nb, max_exact = 16, 8
    n = np.abs(rel)
    large = max_exact + (np.log(np.maximum(n, 1) / max_exact) / np.log(128 / max_exact) * (nb - max_exact)).astype(np.int32)
    bucket = ((rel > 0) * nb + np.where(n < max_exact, n, np.minimum(large, nb - 1))).astype(np.int32)
    u = jnp.pad(table[jnp.asarray(bucket)].T, ((0, 0), (0, SW_KEYS - bucket.shape[0]))).reshape(8, 1, SW_KEYS)

    def body(u_ref, o_ref):
        for h in range(8):
            x = jnp.broadcast_to(u_ref[h], (SW_BLK, SW_KEYS))
            o_ref[h] = pltpu.roll(x, 0, 1, stride=1, stride_axis=0)

    return pl.pallas_call(body, name=name, out_shape=_sds((8, SW_BLK, SW_KEYS), F32), compiler_params=_params())(u)


def _sw_valid(n):
    a = lax.broadcasted_iota(jnp.int32, (SW_BLK, SW_KEYS), 0)
    j = lax.broadcasted_iota(jnp.int32, (SW_BLK, SW_KEYS), 1)
    kpos = (n - 1) * SW_BLK + j
    return (jnp.abs(j - SW_BLK - a) <= SW_BLK) & (kpos >= 0) & (kpos < SEQ)


def _dup_group(x2, g, first):
    rolled = pltpu.roll(x2, HD, 1)
    return jnp.where(first, x2, rolled) if g == 0 else jnp.where(first, rolled, x2)


def sw_fwd(q, kv, t5, sink, *, name):
    def body(q_ref, kv_ref, t5_ref, sink_ref, o_ref, lse_ref):
        n = pl.program_id(0)
        off = pl.multiple_of(n * SW_BLK, SW_BLK)
        first = _lane_halves()
        sels = [first, jnp.logical_not(first)]
        valid = _sw_valid(n)
        k2 = kv_ref[pl.ds(off, SW_KEYS), 0:128]
        v2 = kv_ref[pl.ds(off, SW_KEYS), 128:256]
        kk = [_dup_group(k2, g, first) for g in range(2)]
        vv = [_dup_group(v2, g, first) for g in range(2)]
        q2s = [q_ref[:, 128 * j:128 * j + 128] for j in range(4)]
        scores = []
        for h in range(8):
            j, half = divmod(h, 2)
            scores.append(_dot(jnp.where(sels[half], q2s[j], jnp.zeros_like(q2s[j])), kk[j // 2], NT))
        probs, lses = [], []
        for h in range(8):
            s = jnp.where(valid, scores[h] + t5_ref[h], NEG)
            snk = sink_ref[h]
            m = jnp.maximum(jnp.max(s, axis=-1, keepdims=True), snk)
            e = jnp.exp(s - m)
            den = jnp.sum(e, axis=-1, keepdims=True) + jnp.exp(snk - m)
            probs.append((e / den).astype(BF16))
            lses.append(m + jnp.log(den))
        outs = []
        for j in range(4):
            vg = vv[j // 2]
            zero = jnp.zeros_like(vg)
            outs.append(_dot(probs[2 * j], jnp.where(sels[0], vg, zero)) + _dot(probs[2 * j + 1], jnp.where(sels[1], vg, zero)))
        o_ref[...] = jnp.concatenate(outs, axis=1).astype(BF16)
        lse_ref[...] = jnp.concatenate(lses, axis=1)

    s_tok = q.shape[0]
    blk = pl.BlockSpec((SW_BLK, 512), lambda n: (n, 0))
    return pl.pallas_call(
        body, name=name, out_shape=(_sds((s_tok, 512), BF16), _sds((s_tok, 8), F32)), grid=(SW_NB,),
        in_specs=[blk, pl.BlockSpec(kv.shape, lambda n: (0, 0)), pl.BlockSpec((8, SW_BLK, SW_KEYS), lambda n: (0, 0, 0)),
                  pl.BlockSpec(memory_space=pltpu.SMEM)],
        out_specs=(blk, pl.BlockSpec((SW_BLK, 8), lambda n: (n, 0))), compiler_params=_params(("parallel",)))(q, kv, t5, sink)


def sw_bwd(q, kv, o, do, lse, t5, sink, *, name):
    def body(q_ref, kv_ref, o_ref, do_ref, lse_ref, t5_ref, sink_ref, dq_ref, dkv_ref, dt5_ref, dsink_ref):
        n = pl.program_id(0)

        @pl.when(n == 0)
        def _():
            dkv_ref[...] = jnp.zeros_like(dkv_ref)
            dt5_ref[...] = jnp.zeros_like(dt5_ref)
            dsink_ref[...] = jnp.zeros_like(dsink_ref)

        off = pl.multiple_of(n * SW_BLK, SW_BLK)
        first = _lane_halves()
        sels = [first, jnp.logical_not(first)]
        valid = _sw_valid(n)
        k2 = kv_ref[pl.ds(off, SW_KEYS), 0:128]
        v2 = kv_ref[pl.ds(off, SW_KEYS), 128:256]
        kk = [_dup_group(k2, g, first) for g in range(2)]
        vv = [_dup_group(v2, g, first) for g in range(2)]
        lanes = [slice(128 * j, 128 * j + 128) for j in range(4)]
        q2s = [q_ref[:, l] for l in lanes]
        do2s = [do_ref[:, l] for l in lanes]
        prods = [do2s[j].astype(F32) * o_ref[:, lanes[j]].astype(F32) for j in range(4)]
        lse = lse_ref[...]
        qhs, dohs, scores, dps = [], [], [], []
        for h in range(8):
            j, half = divmod(h, 2)
            qhs.append(jnp.where(sels[half], q2s[j], jnp.zeros_like(q2s[j])))
            dohs.append(jnp.where(sels[half], do2s[j], jnp.zeros_like(do2s[j])))
            scores.append(_dot(qhs[h], kk[j // 2], NT))
            dps.append(_dot(dohs[h], vv[j // 2], NT))
        pbs, dsbs, dss, dsinks = [], [], [], []
        for h in range(8):
            j, half = divmod(h, 2)
            s = jnp.where(valid, scores[h] + t5_ref[h], NEG)
            lse_h = lse[:, h:h + 1]
            p = jnp.exp(s - lse_h)
            delta = jnp.sum(jnp.where(sels[half], prods[j], 0.0), axis=-1, keepdims=True)
            ds = p * (dps[h] - delta)
            dss.append(ds)
            dsinks.append(-jnp.sum(jnp.exp(sink_ref[h] - lse_h) * delta, axis=0, keepdims=True))
            pbs.append(p.astype(BF16))
            dsbs.append(ds.astype(BF16))
        dt5_ref[...] += jnp.stack(dss)
        dsink_ref[...] += jnp.concatenate(dsinks, axis=1)
        dqs = []
        for j in range(4):
            a, b = 2 * j, 2 * j + 1
            zero = jnp.zeros_like(kk[j // 2])
            dqs.append(_dot(dsbs[a], jnp.where(sels[0], kk[j // 2], zero)) + _dot(dsbs[b], jnp.where(sels[1], kk[j // 2], zero)))
        dq_ref[...] = jnp.concatenate(dqs, axis=1)
        dk_groups, dv_groups = [], []
        for g in range(2):
            dkk = sum(_dot(dsbs[h], qhs[h], TN) for h in range(4 * g, 4 * g + 4))
            dvv = sum(_dot(pbs[h], dohs[h], TN) for h in range(4 * g, 4 * g + 4))
            dk_groups.append(dkk + pltpu.roll(dkk, HD, 1))
            dv_groups.append(dvv + pltpu.roll(dvv, HD, 1))
        dkv_ref[pl.ds(off, SW_KEYS), :] += jnp.concatenate(
            [jnp.where(first, dk_groups[0], dk_groups[1]), jnp.where(first, dv_groups[0], dv_groups[1])], axis=1)

    s_tok = q.shape[0]
    blk = pl.BlockSpec((SW_BLK, 512), lambda n: (n, 0))
    kv_spec = pl.BlockSpec(kv.shape, lambda n: (0, 0))
    t5_spec = pl.BlockSpec((8, SW_BLK, SW_KEYS), lambda n: (0, 0, 0))
    vec = pl.BlockSpec((1, 8), lambda n: (0, 0))
    return pl.pallas_call(
        body, name=name,
        out_shape=(_sds((s_tok, 512), F32), _sds(kv.shape, F32), _sds((8, SW_BLK, SW_KEYS), F32), _sds((1, 8), F32)),
        grid=(SW_NB,), in_specs=[blk, kv_spec, blk, blk, pl.BlockSpec((SW_BLK, 8), lambda n: (n, 0)), t5_spec,
                                 pl.BlockSpec(memory_space=pltpu.SMEM)],
        out_specs=(blk, kv_spec, t5_spec, vec), compiler_params=_params(("arbitrary",)))(q, kv, o, do, lse, t5, sink)


def gate_fwd(zg, bias, pa, ps, *, name, tm=512):
    def body(z0_ref, z1_ref, b0_ref, b1_ref, pa_ref, ps_ref, m_ref):
        g0 = jax.nn.sigmoid(z0_ref[...] + b0_ref[...])
        g1 = jax.nn.sigmoid(z1_ref[...] + b1_ref[...])
        m_ref[...] = (g0 * pa_ref[...] + g1 * ps_ref[...]).astype(BF16)

    s = zg.shape[0]
    half = lambda j: pl.BlockSpec((tm, DM), lambda i, j=j: (i, j))
    bvec = lambda j: pl.BlockSpec((1, DM), lambda i, j=j: (0, j))
    return pl.pallas_call(
        body, name=name, out_shape=_sds((s, DM), BF16), grid=(s // tm,),
        in_specs=[half(0), half(1), bvec(0), bvec(1), half(0), half(0)], out_specs=half(0),
        compiler_params=_params(("parallel",)))(zg, zg, bias, bias, pa, ps)


def gate_bwd(dm, zg, bias, pa, ps, *, name, tm=512):
    def body(dm_ref, z0_ref, z1_ref, b0_ref, b1_ref, pa_ref, ps_ref, dpa_ref, dps_ref, dz_ref, db_ref):
        @pl.when(pl.program_id(0) == 0)
        def _():
            db_ref[...] = jnp.zeros_like(db_ref)

        dm = dm_ref[...]
        g0 = jax.nn.sigmoid(z0_ref[...] + b0_ref[...])
        g1 = jax.nn.sigmoid(z1_ref[...] + b1_ref[...])
        dpa_ref[...] = (dm * g0).astype(BF16)
        dps_ref[...] = (dm * g1).astype(BF16)
        dz0 = dm * pa_ref[...] * g0 * (1.0 - g0)
        dz1 = dm * ps_ref[...] * g1 * (1.0 - g1)
        dz_ref[:, 0:DM] = dz0.astype(BF16)
        dz_ref[:, DM:2 * DM] = dz1.astype(BF16)
        db_ref[:, 0:DM] += jnp.sum(dz0, axis=0, keepdims=True)
        db_ref[:, DM:2 * DM] += jnp.sum(dz1, axis=0, keepdims=True)

    s = zg.shape[0]
    half = lambda j: pl.BlockSpec((tm, DM), lambda i, j=j: (i, j))
    bvec = lambda j: pl.BlockSpec((1, DM), lambda i, j=j: (0, j))
    return pl.pallas_call(
        body, name=name,
        out_shape=(_sds((s, DM), BF16), _sds((s, DM), BF16), _sds((s, GATE_W), BF16), _sds((1, GATE_W), F32)),
        grid=(s // tm,), in_specs=[half(0), half(0), half(1), bvec(0), bvec(1), half(0), half(0)],
        out_specs=(half(0), half(0), pl.BlockSpec((tm, GATE_W), lambda i: (i, 0)), pl.BlockSpec((1, GATE_W), lambda i: (0, 0))),
        compiler_params=_params(("arbitrary",)))(dm, zg, zg, bias, bias, pa, ps)


def loss_head(y, target, *, name, tm=512):
    def body(y_ref, t_ref, dy_ref, l_ref):
        @pl.when(pl.program_id(0) == 0)
        def _():
            l_ref[...] = jnp.zeros_like(l_ref)

        err = y_ref[...] - t_ref[...]
        dy_ref[...] = err * (1.0 / DM)
        l_ref[...] += 0.5 * jnp.sum(jnp.mean(err * err, axis=-1, keepdims=True), axis=0, keepdims=True)

    s = y.shape[0]
    tile = pl.BlockSpec((tm, DM), lambda i: (i, 0))
    return pl.pallas_call(
        body, name=name, out_shape=(_sds((s, DM), F32), _sds((1, 128), F32)), grid=(s // tm,), in_specs=[tile, tile],
        out_specs=(tile, pl.BlockSpec((1, 128), lambda i: (0, 0))), compiler_params=_params(("arbitrary",)))(y, target)


def adamw(w, g, m, v, *, name):
    def body(w_ref, g_ref, m_ref, v_ref, d_ref, nm_ref, nv_ref):
        g = g_ref[...]
        nm = ADAM_B1 * m_ref[...] + (1.0 - ADAM_B1) * g
        nv = ADAM_B2 * v_ref[...] + (1.0 - ADAM_B2) * jnp.square(g)
        m_hat = nm / (1.0 - ADAM_B1 ** ADAM_STEP)
        v_hat = nv / (1.0 - ADAM_B2 ** ADAM_STEP)
        d_ref[...] = -ADAM_LR * (m_hat / (jnp.sqrt(v_hat) + ADAM_EPS) + ADAM_WD * w_ref[...])
        nm_ref[...] = nm
        nv_ref[...] = nv

    b, k, n = w.shape
    tk = k // 4 if k % 32 == 0 else k
    spec = pl.BlockSpec((None, tk, n), lambda i, j: (i, j, 0))
    out = _sds(w.shape, F32)
    return pl.pallas_call(
        body, name=name, out_shape=(out, out, out), grid=(b, k // tk), in_specs=[spec] * 4, out_specs=(spec,) * 3,
        compiler_params=_params(("parallel", "parallel")))(w, g, m, v)


def adamw_layer(ws, ms, vs, mines, theirs, cidx, layer, filled=None, *, name):
    cnt = len(ws)
    _, k, n = ws[0].shape
    nt = 2
    tk = k // 2 // nt

    def body(c_ref, *refs):
        own = pl.program_id(0) == c_ref[0]
        outs = refs[-4 * cnt:]
        for i in range(cnt):
            w_ref, m_ref, v_ref, a_ref, b_ref = refs[5 * i:5 * i + 5]
            g_ref, d_ref, nm_ref, nv_ref = outs[4 * i:4 * i + 4]
            g = jnp.where(own, a_ref[...], b_ref[...])
            g_ref[...] = g
            nm = ADAM_B1 * m_ref[...] + (1.0 - ADAM_B1) * g
            nv = ADAM_B2 * v_ref[...] + (1.0 - ADAM_B2) * jnp.square(g)
            m_hat = nm / (1.0 - ADAM_B1 ** ADAM_STEP)
            v_hat = nv / (1.0 - ADAM_B2 ** ADAM_STEP)
            d_ref[...] = -ADAM_LR * (m_hat / (jnp.sqrt(v_hat) + ADAM_EPS) + ADAM_WD * w_ref[...])
            nm_ref[...] = nm
            nv_ref[...] = nv

    full = pl.BlockSpec((None, tk, n), lambda hf, t, c: (layer, hf * nt + t, 0))
    half_mine = pl.BlockSpec((tk, n), lambda hf, t, c: (jnp.where(hf == c[0], t, 0), 0))
    half_theirs = pl.BlockSpec((tk, n), lambda hf, t, c: (jnp.where(hf != c[0], t, 0), 0))
    out = _sds(ws[0].shape, F32)
    ins, specs, aliases = [cidx], [], {}
    for i in range(cnt):
        ins += [ws[i], ms[i], vs[i], mines[i], theirs[i]]
        specs += [full, full, full, half_mine, half_theirs]
    if filled is not None:
        aliases = {len(ins) + j: j for j in range(4 * cnt)}
        ins += [a for f in filled for a in f]
        specs += [pl.BlockSpec(memory_space=pl.ANY)] * (4 * cnt)
    res = pl.pallas_call(
        body, name=name, out_shape=(out,) * (4 * cnt),
        grid_spec=pltpu.PrefetchScalarGridSpec(
            num_scalar_prefetch=1, grid=(2, nt), in_specs=specs, out_specs=(full,) * (4 * cnt)),
        input_output_aliases=aliases,
        compiler_params=_params(("arbitrary", "arbitrary")))(*ins)
    return [tuple(res[4 * i:4 * i + 4]) for i in range(cnt)]


def t5_table_grad(dt5_a, dt5_b, *, name):
    def body(a_ref, b_ref, map_ref, o_ref):
        d = a_ref[...] + b_ref[...]
        bucket = map_ref[...]
        for b in range(32):
            hit = (bucket == b)[None]
            o_ref[b] = jnp.sum(jnp.sum(jnp.where(hit, d, 0.0), axis=2), axis=1, keepdims=True)

    return pl.pallas_call(
        body, name=name, out_shape=_sds((32, 8, 1), F32), compiler_params=_params())(
            dt5_a, dt5_b, jnp.asarray(t5_bucket_map()))


def rpb_grad(dbias, *, name):
    def body(d_ref, rev_ref, o_ref):
        rev = rev_ref[...]
        for h in range(NA_HEADS):
            for pr in range(NA_WR // 2):
                d = d_ref[h, :, 128 * pr:128 * pr + 128]
                hi = d.astype(BF16)
                lo = (d - hi.astype(F32)).astype(BF16)
                flipped = _dot(rev, hi) + _dot(rev, lo)
                o_ref[h, pr] = jnp.sum(pltpu.roll(flipped, 0, 1, stride=1, stride_axis=0), axis=0, keepdims=True)

    anti = jnp.asarray(np.eye(GRID_W, dtype=np.float32)[::-1], dtype=BF16)
    e = pl.pallas_call(
        body, name=name, out_shape=_sds((NA_WR, NA_HEADS, NA_WR // 2, 1, 128), F32), grid=(NA_WR,),
        in_specs=[pl.BlockSpec((None, NA_HEADS, GRID_W, NA_KEYS), lambda p: (p, 0, 0, 0)),
                  pl.BlockSpec((GRID_W, GRID_W), lambda p: (0, 0))],
        out_specs=pl.BlockSpec((None, NA_HEADS, NA_WR // 2, 1, 128), lambda p: (p, 0, 0, 0, 0)),
        compiler_params=_params(("parallel",)))(dbias, anti)
    nci, nri = 2 * NA_WC - 1, 2 * NA_WR - 1
    e = e.reshape(NA_WR, NA_HEADS, NA_WR // 2, 128).transpose(0, 2, 1, 3).reshape(NA_WR * NA_WR // 2, NA_HEADS, 128)
    parts = jnp.concatenate([e[..., 48:48 + nci], jnp.concatenate([e[..., 112:128], e[..., 0:nci - 16]], axis=-1)], axis=0)
    p, pr = np.arange(NA_WR)[:, None], np.arange(NA_WR // 2)[None, :]
    ri = np.concatenate([(2 * pr - p + NA_WR - 1).reshape(-1), (2 * pr - p + NA_WR).reshape(-1)])
    pick = jnp.asarray((ri[None, :] == np.arange(16)[:, None]).astype(np.float32))
    out = mm(pick, parts.reshape(2 * NA_WR * NA_WR // 2, NA_HEADS * nci), name=name + "_rows", exact=True)
    return out.reshape(16, NA_HEADS, nci)[:nri].transpose(1, 0, 2)


BIG = ("ffn1_w_gate", "ffn1_w_up", "ffn1_w_down", "w_in", "w_branch_na", "w_branch_sw", "w_out",
       "ffn2_w_gate", "ffn2_w_up", "ffn2_w_down")
SMALL = ("ffn1_norm", "mix_norm", "b_gate", "na_q_norm", "na_k_norm", "na_rpb", "sw_q_norm", "sw_k_norm", "sw_sink",
         "ffn2_norm")


def _cols_to_full(w4):
    return w4.transpose(1, 0, 2).reshape(w4.shape[1], NSH * w4.shape[2])


def _full_to_cols(w):
    return w.reshape(w.shape[0], NSH, w.shape[1] // NSH).transpose(1, 0, 2)


def _mixer_weights(g):
    w_in_t = g["w_in"].reshape(IN_W, DM)
    return dict(w_att_t=w_in_t[:ATT_W], w_gz_t=w_in_t[ATT_W:], wa=_cols_to_full(g["w_branch_na"]),
                ws=_cols_to_full(g["w_branch_sw"]), wo=g["w_out"].reshape(DM, DM))


GROUPS = {"ffn1": ("ffn1_w_gate", "ffn1_w_up", "ffn1_w_down"), "mix": ("w_in", "w_branch_na", "w_branch_sw", "w_out"),
          "ffn2": ("ffn2_w_gate", "ffn2_w_up", "ffn2_w_down")}


def layer_fwd(x, p, weights, t5b):
    row = lambda v: v.reshape(1, -1)
    stacked = lambda g: {n: a.reshape(DFF, DM) for n, a in g.items()}
    g1 = stacked(weights("ffn1", x))
    y1, h1, gg1, uu1 = ffn_fwd(x, row(p["ffn1_norm"]), g1["ffn1_w_gate"], g1["ffn1_w_up"], g1["ffn1_w_down"], name="ffn_fwd")
    w = _mixer_weights(weights("mix", y1))
    hm = rms_fwd(y1, row(p["mix_norm"]), name="mix_norm_fwd")
    z = mm(hm, w["w_att_t"], tb=True, name="proj_att", tm=SEQ, tn=768)
    zg = mm(hm, w["w_gz_t"], tb=True, name="proj_gate", tm=SEQ, tn=512)
    qa, ka, va, qs, kv = qknorm_fwd(z, p["na_q_norm"], p["na_k_norm"], p["sw_q_norm"], p["sw_k_norm"], name="qknorm_fwd")
    bias = na_bias_table(p["na_rpb"], name="na_bias_table")
    o_na, lse_na = na_fwd(qa, ka, va, bias, name="na_fwd")
    kvp = jnp.pad(kv, ((SW_BLK, SW_BLK), (0, 0)))
    sink = p["sw_sink"]
    o_sw, lse_sw = sw_fwd(qs, kvp, t5b, sink, name="sw_fwd")
    pa = mm(o_na, w["wa"], name="branch_na", tm=1024)
    ps = mm(o_sw, w["ws"], name="branch_sw", tm=1024)
    merged = gate_fwd(zg, row(p["b_gate"]), pa, ps, name="gate_fwd")
    y2 = mm(merged, w["wo"], add=y1, name="out_proj", tm=1024)
    g2 = stacked(weights("ffn2", y2))
    y3, h2, gg2, uu2 = ffn_fwd(y2, row(p["ffn2_norm"]), g2["ffn2_w_gate"], g2["ffn2_w_up"], g2["ffn2_w_down"], name="ffn_fwd")
    saved = dict(x=x, y1=y1, h1=h1, gg1=gg1, uu1=uu1, hm=hm, z=z, zg=zg, qa=qa, ka=ka, va=va, qs=qs, kvp=kvp, bias=bias,
                 o_na=o_na, lse_na=lse_na, o_sw=o_sw, lse_sw=lse_sw, pa=pa, ps=ps, merged=merged, y2=y2, h2=h2, gg2=gg2,
                 uu2=uu2, w=w, sink=sink, g1=g1, g2=g2)
    return y3, saved


def layer_bwd(dy3, sv, p, t5b, emit, dep=None):
    w, g1, g2 = sv["w"], sv["g1"], sv["g2"]
    row = lambda v: v.reshape(1, -1)
    fold = lambda v: v.reshape(-1, HD).sum(axis=0)
    small = {}
    dy2, small["ffn2_norm"], act, dg, du = ffn_bwd_tokens(
        dy3, sv["y2"], row(p["ffn2_norm"]), sv["gg2"], sv["uu2"], g2["ffn2_w_gate"], g2["ffn2_w_up"], g2["ffn2_w_down"],
        name="ffn_bwd_tokens", dep=dep)
    shards = lambda gs: [g.reshape(NSH, FSH, DM) for g in gs]
    token = emit("ffn2", shards(ffn_bwd_weights(sv["h2"], dy3.astype(BF16), act, dg, du, name="ffn_bwd_weights")))
    dmerged = mm(dy2, w["wo"], tb=True, name="out_proj_dx", tm=1024, dep=token)
    gw_out = mm(sv["merged"], dy2, ta=True, out_dtype=BF16, name="out_proj_dw").reshape(NSH, DM // NSH, DM)
    dpa, dps, dzg, small["b_gate"] = gate_bwd(dmerged, sv["zg"], row(p["b_gate"]), sv["pa"], sv["ps"], name="gate_bwd")
    gw_na = _full_to_cols(mm(sv["o_na"], dpa, ta=True, out_dtype=BF16, name="branch_dw"))
    gw_sw = _full_to_cols(mm(sv["o_sw"], dps, ta=True, out_dtype=BF16, name="branch_dw"))
    do_na = mm(dpa, w["wa"], tb=True, out_dtype=BF16, tm=SEQ, name="branch_dx")
    do_sw = mm(dps, w["ws"], tb=True, out_dtype=BF16, tm=SEQ, name="branch_dx")
    dqa, dka, dva, dbias = na_bwd(sv["qa"], sv["ka"], sv["va"], sv["o_na"], do_na, sv["lse_na"], sv["bias"], name="na_bwd")
    dqs, dkvp, dt5, dsink = sw_bwd(sv["qs"], sv["kvp"], sv["o_sw"], do_sw, sv["lse_sw"], t5b, sv["sink"], name="sw_bwd")
    dkv = dkvp[SW_BLK:SW_BLK + SEQ]
    dz, dgqa, dgka, dgqs, dgks = qknorm_bwd(sv["z"], dqa, dka, dva, dqs, dkv, p["na_q_norm"], p["na_k_norm"],
                                            p["sw_q_norm"], p["sw_k_norm"], name="qknorm_bwd")
    small["na_q_norm"], small["na_k_norm"], small["sw_q_norm"], small["sw_k_norm"] = fold(dgqa), fold(dgka), fold(dgqs), fold(dgks)
    small["na_rpb"] = rpb_grad(dbias, name="rpb_grad")
    small["sw_sink"] = dsink
    gw_att_t = mm(dz, sv["hm"], ta=True, out_dtype=BF16, tm=768, name="proj_att_dw")
    gw_gz_t = mm(dzg, sv["hm"], ta=True, out_dtype=BF16, tm=1024, name="proj_gate_dw")
    gw_in = jnp.concatenate([gw_att_t, gw_gz_t], axis=0).reshape(NSH, IN_W // NSH, DM)
    token = emit("mix", (gw_in, gw_na, gw_sw, gw_out))
    dh = mm(dz, w["w_att_t"], tm=1024, name="proj_att_dx", dep=token)
    dh = mm(dzg, w["w_gz_t"], add=dh, tm=1024, name="proj_gate_dx")
    dy1, small["mix_norm"] = rms_bwd(dh, sv["y1"], row(p["mix_norm"]), dy2, name="mix_norm_bwd")
    dx, small["ffn1_norm"], act, dg, du = ffn_bwd_tokens(
        dy1, sv["x"], row(p["ffn1_norm"]), sv["gg1"], sv["uu1"], g1["ffn1_w_gate"], g1["ffn1_w_up"], g1["ffn1_w_down"],
        name="ffn_bwd_tokens")
    emit("ffn1", shards(ffn_bwd_weights(sv["h1"], dy1.astype(BF16), act, dg, du, name="ffn_bwd_weights")))
    return dx, small, dt5


ANY = pl.BlockSpec(memory_space=pl.ANY)


def _place():
    x, y, c = lax.axis_index("x"), lax.axis_index("y"), lax.axis_index("c")
    chips = [(1 - x, y), (x, 1 - y), (1 - x, 1 - y)]
    return x, y, c, chips


def _remote(src, dst, send_sem, recv_sem, to):
    return pltpu.make_async_remote_copy(src_ref=src, dst_ref=dst, send_sem=send_sem, recv_sem=recv_sem, device_id=to,
                                        device_id_type=MESH)


HBM = pl.BlockSpec(memory_space=pltpu.HBM)
SEM = pl.BlockSpec(memory_space=pltpu.SEMAPHORE)
ORDERED_EFFECT = pltpu.SideEffectType.DATAFLOW_SIDE_EFFECTING


def _in_hbm(v):
    return pltpu.with_memory_space_constraint(v, pltpu.HBM)


def _row_half(ref_shape_rows, c):
    half = ref_shape_rows // 2
    return pl.ds(c * half, half)


def _ici_gather_copies(w, land, send_sems, recv_sems):
    x, y, c, chips = _place()
    me = 2 * x + y
    copies = []
    for a in range(len(w)):
        rows = _row_half(w[a].shape[0], c)
        for k, chip in enumerate(chips):
            copies.append(_remote(w[a].at[rows], land[a].at[me, rows], send_sems.at[4 * a + k], recv_sems.at[4 * a + k],
                                  (*chip, c)))
        copies.append(_remote(w[a], land[a].at[me], send_sems.at[4 * a + 3], recv_sems.at[4 * a + 3], (x, y, 1 - c)))
    return copies


def _d2d_gather_copies(w, land, send_sems, recv_sems):
    x, y, c, chips = _place()
    copies = []
    for a in range(len(w)):
        rows = _row_half(w[a].shape[0], c)
        for k, (cx, cy) in enumerate(chips):
            blk = land[a].at[2 * cx + cy, rows]
            copies.append(_remote(blk, blk, send_sems.at[3 * a + k], recv_sems.at[3 * a + k], (x, y, 1 - c)))
    return copies


def _d2d_gather_waits(w, land, send_sems, recv_sems):
    x, y, c, chips = _place()
    waits = []
    for a in range(len(w)):
        rows = _row_half(w[a].shape[0], 1 - c)
        for k, (cx, cy) in enumerate(chips):
            blk = land[a].at[2 * cx + cy, rows]
            waits.append(_remote(blk, blk, send_sems.at[3 * a + k], recv_sems.at[3 * a + k], (x, y, 1 - c)))
    return waits


def gather_start(groups, *, name):
    sizes = [len(g) for g in groups]
    shards = [s for g in groups for s in g]
    n, ng = len(shards), len(groups)

    def body(*refs):
        w, land, sems = refs[:n], refs[n:2 * n], refs[2 * n:2 * n + 2 * ng]
        off = 0
        for gi, size in enumerate(sizes):
            for cp in _ici_gather_copies(w[off:off + size], land[off:off + size], sems[2 * gi], sems[2 * gi + 1]):
                cp.start()
            off += size

    lands = [lax.empty((NSH,) + s.shape, s.dtype) for s in shards]
    sem_shapes = tuple(pltpu.SemaphoreType.DMA((4 * size,)) for size in sizes for _ in range(2))
    res = pl.pallas_call(
        body, name=name,
        out_shape=sem_shapes + tuple(pltpu.HBM(s.shape, s.dtype) for s in shards) + tuple(pltpu.HBM(l.shape, l.dtype) for l in lands),
        in_specs=[HBM] * (2 * n), out_specs=(SEM,) * (2 * ng) + (HBM,) * (2 * n),
        input_output_aliases={i: 2 * ng + i for i in range(2 * n)},
        compiler_params=pltpu.CompilerParams(has_side_effects=ORDERED_EFFECT))(
            *[_in_hbm(s) for s in shards], *[_in_hbm(l) for l in lands])
    out, off = [], 0
    for gi, size in enumerate(sizes):
        out.append((res[2 * gi], res[2 * gi + 1], list(res[2 * ng + off:2 * ng + off + size]),
                    list(res[2 * ng + n + off:2 * ng + n + off + size])))
        off += size
    return out


def gather_wait(send_sems, recv_sems, shards, lands, after, *, name):
    n = len(shards)

    def body(*refs):
        w, land = refs[:n], refs[n:2 * n]
        send, recv = refs[2 * n:2 * n + 2]
        for cp in _ici_gather_copies(w, land, send, recv):
            cp.wait_send()
            cp.wait_recv()

    res = pl.pallas_call(
        body, name=name,
        out_shape=tuple(pltpu.HBM(s.shape, s.dtype) for s in shards) + tuple(pltpu.HBM(l.shape, l.dtype) for l in lands),
        in_specs=[HBM] * (2 * n) + [SEM, SEM, ANY], out_specs=(HBM,) * (2 * n),
        input_output_aliases={i: i for i in range(2 * n)},
        compiler_params=pltpu.CompilerParams(has_side_effects=ORDERED_EFFECT))(*shards, *lands, send_sems, recv_sems, after)
    return list(res[:n]), list(res[n:])


def gather_finish(shards, lands, *, name):
    n = len(shards)

    def body(*refs):
        w, land = refs[:n], refs[n:2 * n]
        send_sems, recv_sems = refs[3 * n:]
        d2d = _d2d_gather_copies(w, land, send_sems, recv_sems)
        for cp in d2d:
            cp.start()
        for cp in _d2d_gather_waits(w, land, send_sems, recv_sems):
            cp.wait_recv()
        for cp in d2d:
            cp.wait_send()

    return list(pl.pallas_call(
        body, name=name, out_shape=tuple(pltpu.HBM(l.shape, l.dtype) for l in lands),
        in_specs=[ANY] * (2 * n), out_specs=tuple([ANY] * n), input_output_aliases={n + i: i for i in range(n)},
        scratch_shapes=[pltpu.SemaphoreType.DMA((3 * n,)), pltpu.SemaphoreType.DMA((3 * n,))])(*shards, *lands))


def pair_exchange(grads, *, name):
    n = len(grads)

    def body(*refs):
        g, buf = refs[:n], refs[n:2 * n]
        send_sems, recv_sems = refs[2 * n:]
        x, y, c, _ = _place()
        copies = []
        for a in range(n):
            half = g[a].shape[1] // 2
            cp = _remote(g[a].at[:, pl.ds((1 - c) * half, half)], buf[a], send_sems.at[a], recv_sems.at[a], (x, y, 1 - c))
            cp.start()
            copies.append(cp)
        for cp in copies:
            cp.wait()

    return pl.pallas_call(
        body, name=name, out_shape=tuple(pltpu.HBM((NSH, g.shape[1] // 2, g.shape[2]), g.dtype) for g in grads),
        in_specs=[ANY] * n, out_specs=tuple([ANY] * n),
        scratch_shapes=[pltpu.SemaphoreType.DMA((n,)), pltpu.SemaphoreType.DMA((n,))])(*grads)


def _chip_exchange_copies(s, buf, send_sems, recv_sems):
    x, y, c, chips = _place()
    return [_remote(s[a].at[2 * cx + cy], buf[a].at[k], send_sems.at[3 * a + k], recv_sems.at[3 * a + k], (cx, cy, c))
            for a in range(len(s)) for k, (cx, cy) in enumerate(chips)]


def chip_exchange(sums, *, name):
    n = len(sums)

    def body(*refs):
        copies = _chip_exchange_copies(refs[:n], refs[n:2 * n], *refs[2 * n:])
        for cp in copies:
            cp.start()
        for cp in copies:
            cp.wait()

    return pl.pallas_call(
        body, name=name, out_shape=tuple(pltpu.HBM((3,) + s.shape[1:], s.dtype) for s in sums),
        in_specs=[ANY] * n, out_specs=tuple([ANY] * n),
        scratch_shapes=[pltpu.SemaphoreType.DMA((3 * n,)), pltpu.SemaphoreType.DMA((3 * n,))])(*sums)


def chip_exchange_start(sums, *, name):
    n = len(sums)

    def body(*refs):
        for cp in _chip_exchange_copies(refs[:n], refs[n:2 * n], refs[2 * n], refs[2 * n + 1]):
            cp.start()
        refs[-1][...] = jnp.zeros_like(refs[-1])

    lands = [lax.empty((3,) + s.shape[1:], s.dtype) for s in sums]
    res = pl.pallas_call(
        body, name=name,
        out_shape=(pltpu.SemaphoreType.DMA((3 * n,)), pltpu.SemaphoreType.DMA((3 * n,)))
        + tuple(pltpu.HBM(s.shape, s.dtype) for s in sums) + tuple(pltpu.HBM(l.shape, l.dtype) for l in lands)
        + (_sds((8, 128), F32),),
        in_specs=[HBM] * (2 * n), out_specs=(SEM, SEM) + (HBM,) * (2 * n) + (pl.BlockSpec(memory_space=pltpu.VMEM),),
        input_output_aliases={i: 2 + i for i in range(2 * n)},
        compiler_params=pltpu.CompilerParams(has_side_effects=ORDERED_EFFECT))(
            *[_in_hbm(s) for s in sums], *[_in_hbm(l) for l in lands])
    return res[0], res[1], list(res[2:2 + n]), list(res[2 + n:2 + 2 * n]), res[-1]


def chip_exchange_wait(send_sems, recv_sems, sums, lands, after, *, name):
    n = len(sums)

    def body(*refs):
        for cp in _chip_exchange_copies(refs[:n], refs[n:2 * n], refs[2 * n], refs[2 * n + 1]):
            cp.wait_send()
            cp.wait_recv()

    res = pl.pallas_call(
        body, name=name,
        out_shape=tuple(pltpu.HBM(s.shape, s.dtype) for s in sums) + tuple(pltpu.HBM(l.shape, l.dtype) for l in lands),
        in_specs=[HBM] * (2 * n) + [SEM, SEM] + [ANY] * len(after), out_specs=(HBM,) * (2 * n),
        input_output_aliases={i: i for i in range(2 * n)},
        compiler_params=pltpu.CompilerParams(has_side_effects=ORDERED_EFFECT))(*sums, *lands, send_sems, recv_sems, *after)
    return list(res[:n]), list(res[n:])


def pair_send(halves, *, name):
    n = len(halves)

    def body(*refs):
        h, got = refs[:n], refs[n:2 * n]
        send_sems, recv_sems = refs[2 * n:]
        x, y, c, _ = _place()
        copies = []
        for i in range(n):
            cp = _remote(h[i], got[i], send_sems.at[i], recv_sems.at[i], (x, y, 1 - c))
            cp.start()
            copies.append(cp)
        for cp in copies:
            cp.wait()

    return list(pl.pallas_call(
        body, name=name, out_shape=tuple(pltpu.HBM(v.shape, v.dtype) for v in halves),
        in_specs=[ANY] * n, out_specs=tuple([ANY] * n),
        scratch_shapes=[pltpu.SemaphoreType.DMA((n,)), pltpu.SemaphoreType.DMA((n,))])(*halves))


def allreduce_small(v, *, name):
    rows = v.shape[0]

    def body(v_ref, o_ref, gath, send_sems, recv_sems):
        x, y, c, _ = _place()
        me = 4 * x + 2 * y + c
        gath[me] = v_ref[...]
        copies = []
        for k in range(1, 8):
            fx, fy, fc = (k >> 2) & 1, (k >> 1) & 1, k & 1
            peer = (jnp.where(fx, 1 - x, x), jnp.where(fy, 1 - y, y), jnp.where(fc, 1 - c, c))
            cp = _remote(v_ref, gath.at[me], send_sems.at[k - 1], recv_sems.at[k - 1], peer)
            cp.start()
            copies.append(cp)
        for cp in copies:
            cp.wait()
        acc = gath[0]
        for d in range(1, 8):
            acc = acc + gath[d]
        o_ref[...] = acc

    return pl.pallas_call(
        body, name=name, out_shape=_sds(v.shape, F32),
        in_specs=[pl.BlockSpec(memory_space=pltpu.VMEM)], out_specs=pl.BlockSpec(memory_space=pltpu.VMEM),
        scratch_shapes=[pltpu.VMEM((8, rows, 128), F32), pltpu.SemaphoreType.DMA((7,)), pltpu.SemaphoreType.DMA((7,))])(v)


def _same_shape_runs(arrays):
    runs = {}
    for i, a in enumerate(arrays):
        runs.setdefault(a.shape, []).append(i)
    return list(runs.values())


def _per_shape(fn, *lists):
    out = [None] * len(lists[0])
    for idx in _same_shape_runs(lists[0]):
        for i, r in zip(idx, fn(*[[l[i] for i in idx] for l in lists])):
            out[i] = r
    return out


def add_halves(gs, bufs, cidx, *, name):
    cnt = len(gs)
    _, k, n = gs[0].shape

    def body(c_ref, *refs):
        g, b, o = refs[:cnt], refs[cnt:2 * cnt], refs[2 * cnt:]
        for i in range(cnt):
            o[i][...] = (g[i][...].astype(F32) + b[i][...].astype(F32)).astype(BF16)

    blk = pl.BlockSpec((None, k // 2, n), lambda s, c: (s, 0, 0))
    mine = pl.BlockSpec((None, k // 2, n), lambda s, c: (s, c[0], 0))
    return list(pl.pallas_call(
        body, name=name, out_shape=tuple(_sds(b.shape, BF16) for b in bufs),
        grid_spec=pltpu.PrefetchScalarGridSpec(
            num_scalar_prefetch=1, grid=(NSH,), in_specs=[mine] * cnt + [blk] * cnt, out_specs=tuple([blk] * cnt)),
        compiler_params=_params(("parallel",)))(cidx, *gs, *bufs))


def add_chips(sums, bufs, sidx, *, name):
    cnt = len(sums)
    _, kh, n = sums[0].shape

    def body(s_ref, *refs):
        mine, b, o = refs[:cnt], refs[cnt:2 * cnt], refs[2 * cnt:]
        for i in range(cnt):
            o[i][...] = ((mine[i][...].astype(F32) + b[i][0].astype(F32)) + (b[i][1].astype(F32) + b[i][2].astype(F32)))

    own = pl.BlockSpec((None, kh, n), lambda i, s: (s[0], 0, 0))
    got = pl.BlockSpec((3, kh, n), lambda i, s: (0, 0, 0))
    out = pl.BlockSpec((kh, n), lambda i, s: (0, 0))
    return list(pl.pallas_call(
        body, name=name, out_shape=tuple(_sds((kh, n), F32) for _ in sums),
        grid_spec=pltpu.PrefetchScalarGridSpec(
            num_scalar_prefetch=1, grid=(1,), in_specs=[own] * cnt + [got] * cnt, out_specs=tuple([out] * cnt)),
        compiler_params=_params(("arbitrary",)))(sidx, *sums, *bufs))


PARAMS = ("ffn1_norm", "ffn1_w_gate", "ffn1_w_up", "ffn1_w_down", "mix_norm", "w_in", "b_gate", "na_q_norm", "na_k_norm",
          "na_rpb", "sw_q_norm", "sw_k_norm", "sw_sink", "t5_rel_table", "w_branch_na", "w_branch_sw", "w_out", "ffn2_norm",
          "ffn2_w_gate", "ffn2_w_up", "ffn2_w_down")
SMALL_ALL = tuple(n for n in PARAMS if n not in BIG)
TRANSPOSED = ("ffn1_w_gate", "ffn1_w_up", "w_in", "ffn2_w_gate", "ffn2_w_up")
SMALL_ROWS = 152


def _pack_small(vals):
    flat = jnp.concatenate([vals[n].reshape(-1).astype(F32) for n in SMALL_ALL] + [vals["loss"].reshape(-1)])
    return jnp.pad(flat, (0, SMALL_ROWS * 128 - flat.shape[0])).reshape(SMALL_ROWS, 128)


def _unpack_small(packed, like):
    flat, out, off = packed.reshape(-1), {}, 0
    for n in SMALL_ALL:
        size = math.prod(like[n].shape)
        out[n] = flat[off:off + size].reshape(like[n].shape)
        off += size
    out["loss"] = flat[off]
    return out


def kernel(x, ffn1_norm, ffn1_w_gate, ffn1_w_up, ffn1_w_down, mix_norm, w_in, b_gate, na_q_norm, na_k_norm, na_rpb, sw_q_norm, sw_k_norm, sw_sink, t5_rel_table, w_branch_na, w_branch_sw, w_out, ffn2_norm, ffn2_w_gate, ffn2_w_up, ffn2_w_down, loss_target, m_ffn1_norm, m_ffn1_w_gate, m_ffn1_w_up, m_ffn1_w_down, m_mix_norm, m_w_in, m_b_gate, m_na_q_norm, m_na_k_norm, m_na_rpb, m_sw_q_norm, m_sw_k_norm, m_sw_sink, m_t5_rel_table, m_w_branch_na, m_w_branch_sw, m_w_out, m_ffn2_norm, m_ffn2_w_gate, m_ffn2_w_up, m_ffn2_w_down, v_ffn1_norm, v_ffn1_w_gate, v_ffn1_w_up, v_ffn1_w_down, v_mix_norm, v_w_in, v_b_gate, v_na_q_norm, v_na_k_norm, v_na_rpb, v_sw_q_norm, v_sw_k_norm, v_sw_sink, v_t5_rel_table, v_w_branch_na, v_w_branch_sw, v_w_out, v_ffn2_norm, v_ffn2_w_gate, v_ffn2_w_up, v_ffn2_w_down):
    args = locals()
    tr = lambda n, a: jnp.transpose(a, (0, 2, 1)) if n in TRANSPOSED else a
    w = {n: tr(n, args[n]) for n in PARAMS}
    m = {n: tr(n, args["m_" + n]) for n in PARAMS}
    v = {n: tr(n, args["v_" + n]) for n in PARAMS}
    cidx = lax.axis_index("c").astype(jnp.int32).reshape(1)
    sidx = (2 * lax.axis_index("x") + lax.axis_index("y")).astype(jnp.int32).reshape(1)

    small = [{n: w[n][l] for n in SMALL} for l in range(DEPTH)]
    t5b = t5_bias(w["t5_rel_table"], name="t5_bias")
    order = ("ffn1", "mix", "ffn2")

    keys = [(l, g) for l in range(DEPTH) for g in order]
    in_flight = dict(zip(keys, gather_start([[w[n][l].astype(BF16) for n in GROUPS[g]] for l, g in keys], name="gather_start")))

    def weights_of(l):
        def get(group, after):
            send_sems, recv_sems, thru, lands = in_flight[(l, group)]
            thru, lands = gather_wait(send_sems, recv_sems, thru, lands, after, name="gather_wait")
            return dict(zip(GROUPS[group], gather_finish(thru, lands, name="gather_finish")))
        return get

    h0, saved0 = layer_fwd(x[0], small[0], weights_of(0), t5b)
    h1, saved1 = layer_fwd(h0, small[1], weights_of(1), t5b)
    dy, loss_row = loss_head(h1, loss_target[0], name="loss_head")

    crossing, tokens = {}, []

    def reduce_of(l):
        def emit(group, grads):
            grads = list(grads)
            sums = _per_shape(lambda gs, bs: add_halves(gs, bs, cidx, name="add_halves"), grads,
                              pair_exchange(grads, name="pair_exchange"))
            send_sems, recv_sems, sums, lands, token = chip_exchange_start(sums, name="chip_exchange_start")
            crossing[(l, group)] = (send_sems, recv_sems, sums, lands)
            tokens.append(token)
            return token
        return emit

    def arrived(l, after):
        halves = {}
        for group in order:
            send_sems, recv_sems, sums, lands = crossing[(l, group)]
            sums, got = chip_exchange_wait(send_sems, recv_sems, sums, lands, after, name="chip_exchange_wait")
            halves.update(zip(GROUPS[group], _per_shape(lambda ss, bs: add_chips(ss, bs, sidx, name="add_chips"), sums, got)))
        return [halves[n] for n in BIG]

    def update(layer, mine, theirs, filled=None):
        out = [None] * len(BIG)
        for group in order:
            idx = [BIG.index(n) for n in GROUPS[group]]
            pick = lambda seq: [seq[i] for i in idx]
            res = _per_shape(
                lambda ws, ms, vs, a, b, *f: adamw_layer(ws, ms, vs, a, b, cidx, layer, list(f[0]) if f else None, name="adamw_layer"),
                *([[w[n] for n in GROUPS[group]], [m[n] for n in GROUPS[group]], [v[n] for n in GROUPS[group]], pick(mine),
                   pick(theirs)] + ([pick(filled)] if filled is not None else [])))
            for i, r in zip(idx, res):
                out[i] = r
        return out

    dy, small1, dt5_1 = layer_bwd(dy, saved1, small[1], t5b, reduce_of(1))
    grad_x, small0, dt5_0 = layer_bwd(dy, saved0, small[0], t5b, reduce_of(0), dep=tokens[-1])
    halves1 = arrived(1, [tokens[-1]])
    theirs1 = pair_send(halves1, name="pair_send")
    done1 = update(1, halves1, theirs1)

    smalls = [small0, small1]
    dt5 = t5_table_grad(dt5_0, dt5_1, name="t5_table_grad").reshape(32, 8)
    local_small = {n: jnp.stack([smalls[l][n].reshape(w[n].shape[1:]) for l in range(DEPTH)]) for n in SMALL}
    local_small["t5_rel_table"] = dt5
    local_small["loss"] = loss_row[0, 0:1]
    total = allreduce_small(_pack_small(local_small), name="allreduce_small")
    small_grads = _unpack_small(total, w)
    pack = lambda d: _pack_small({**d, "loss": jnp.zeros((1,), F32)})[None]
    ds, ms, vs = adamw(pack(w), total[None], pack(m), pack(v), name="adamw_small")

    halves0 = arrived(0, [ds, done1[-1][0]])
    theirs0 = pair_send(halves0, name="pair_send")
    grad, delta, new_m, new_v = {}, {}, {}, {}
    for n, done in zip(BIG, update(0, halves0, theirs0, filled=done1)):
        grad[n], delta[n], new_m[n], new_v[n] = done
    for n in SMALL_ALL:
        grad[n] = small_grads[n]
    d_s, m_s, v_s = _unpack_small(ds[0], w), _unpack_small(ms[0], w), _unpack_small(vs[0], w)
    for n in SMALL_ALL:
        delta[n], new_m[n], new_v[n] = d_s[n], m_s[n], v_s[n]

    return (small_grads["loss"], grad_x[None], *[tr(n, grad[n]) for n in PARAMS], *[tr(n, delta[n]) for n in PARAMS],
            *[tr(n, new_m[n]) for n in PARAMS], *[tr(n, new_v[n]) for n in PARAMS])
```

```python
import functools
import math

import jax
import jax.numpy as jnp
import numpy as np
from jax import lax
from jax.experimental import pallas as pl
from jax.experimental.pallas import tpu as pltpu

F32 = jnp.float32
BF16 = jnp.bfloat16

SEQ = 2048
DM = 1024
DFF = 2816
DEPTH = 2
NSH = 4
FSH = DFF // NSH
GRID_W = 64
ROWS = SEQ // GRID_W
NA_HEADS = 8
HD = 64
NA_WR = 8
NA_WC = 16
NA_KEYS = NA_WR * GRID_W
SW_BLK = 128
SW_NB = SEQ // SW_BLK
SW_KEYS = 3 * SW_BLK
ATT_W = 2304
GATE_W = 2048
IN_W = ATT_W + GATE_W
EPS = 1e-6
NEG = -1e30
QK_SCALE = 1.0 / math.sqrt(HD)

ADAM_LR = 0.001
ADAM_B1 = 0.9
ADAM_B2 = 0.999
ADAM_EPS = 1e-08
ADAM_WD = 0.01
ADAM_STEP = 10

VMEM_LIMIT = 56 << 20
MESH = pl.DeviceIdType.MESH

NT = (((1,), (1,)), ((), ()))
TN = (((0,), (0,)), ((), ()))
NN = (((1,), (0,)), ((), ()))


def _dot(a, b, dims=NN):
    return lax.dot_general(a, b, dims, preferred_element_type=F32)


def _params(sem=None):
    return pltpu.CompilerParams(dimension_semantics=sem, vmem_limit_bytes=VMEM_LIMIT)


def _sds(shape, dtype):
    return jax.ShapeDtypeStruct(shape, dtype)


def mm(a, b, *, name, ta=False, tb=False, out_dtype=F32, add=None, scale=None, tm=512, tn=None, tk=None, exact=False,
       dep=None):
    m, kd = (a.shape[1], a.shape[0]) if ta else a.shape
    n = b.shape[0] if tb else b.shape[1]
    tm, tn, tk = min(tm, m), min(tn or n, n), min(tk or kd, kd)
    nk = kd // tk
    dims = (((0 if ta else 1,), (1 if tb else 0,)), ((), ()))

    def body(*refs):
        a_ref, b_ref = refs[:2]
        add_ref = refs[2] if add is not None else None
        o_ref, acc = refs[-2:]
        k = pl.program_id(2)

        @pl.when(k == 0)
        def _():
            acc[...] = jnp.zeros_like(acc)

        if exact:
            acc[...] += lax.dot_general(a_ref[...], b_ref[...], dims, precision=lax.Precision.HIGHEST,
                                        preferred_element_type=F32)
        else:
            acc[...] += lax.dot_general(a_ref[...].astype(BF16), b_ref[...].astype(BF16), dims,
                                        preferred_element_type=F32)

        @pl.when(k == nk - 1)
        def _():
            r = acc[...]
            if scale is not None:
                r = r * scale
            if add is not None:
                r = r + add_ref[...]
            o_ref[...] = r.astype(out_dtype)

    a_spec = pl.BlockSpec((tk, tm), lambda i, j, k: (k, i)) if ta else pl.BlockSpec((tm, tk), lambda i, j, k: (i, k))
    b_spec = pl.BlockSpec((tn, tk), lambda i, j, k: (j, k)) if tb else pl.BlockSpec((tk, tn), lambda i, j, k: (k, j))
    o_spec = pl.BlockSpec((tm, tn), lambda i, j, k: (i, j))
    ins, specs = [a, b], [a_spec, b_spec]
    if add is not None:
        ins.append(add)
        specs.append(o_spec)
    if dep is not None:
        ins.append(dep)
        specs.append(pl.BlockSpec(memory_space=pl.ANY))
    return pl.pallas_call(
        body, name=name, out_shape=_sds((m, n), out_dtype), grid=(m // tm, n // tn, nk), in_specs=specs,
        out_specs=o_spec, scratch_shapes=[pltpu.VMEM((tm, tn), F32)],
        compiler_params=_params(("parallel", "parallel", "arbitrary")))(*ins)


def _rms(x):
    return lax.rsqrt(jnp.mean(x * x, axis=-1, keepdims=True) + EPS)


def rms_fwd(x, gain, *, name, tm=512):
    def body(x_ref, g_ref, h_ref):
        x = x_ref[...]
        h_ref[...] = (x * _rms(x) * g_ref[...]).astype(BF16)

    return pl.pallas_call(
        body, name=name, out_shape=_sds(x.shape, BF16), grid=(x.shape[0] // tm,),
        in_specs=[pl.BlockSpec((tm, DM), lambda i: (i, 0)), pl.BlockSpec((1, DM), lambda i: (0, 0))],
        out_specs=pl.BlockSpec((tm, DM), lambda i: (i, 0)), compiler_params=_params(("parallel",)))(x, gain)


def _rms_bwd_math(dh, x, gain):
    r = _rms(x)
    xh = x * r
    dgain = jnp.sum(dh * xh, axis=0, keepdims=True)
    dxn = dh * gain
    dx = r * (dxn - xh * jnp.mean(dxn * xh, axis=-1, keepdims=True))
    return dx, dgain


def rms_bwd(dh, x, gain, dres, *, name, tm=512):
    def body(dh_ref, x_ref, g_ref, dres_ref, dx_ref, dg_ref):
        @pl.when(pl.program_id(0) == 0)
        def _():
            dg_ref[...] = jnp.zeros_like(dg_ref)

        dx, dg = _rms_bwd_math(dh_ref[...], x_ref[...], g_ref[...])
        dx_ref[...] = dres_ref[...] + dx
        dg_ref[...] += dg

    tile = pl.BlockSpec((tm, DM), lambda i: (i, 0))
    vec = pl.BlockSpec((1, DM), lambda i: (0, 0))
    return pl.pallas_call(
        body, name=name, out_shape=(_sds(x.shape, F32), _sds((1, DM), F32)), grid=(x.shape[0] // tm,),
        in_specs=[tile, tile, vec, tile], out_specs=(tile, vec), compiler_params=_params(("arbitrary",)))(dh, x, gain, dres)


def _with_dep(ins, specs, dep):
    if dep is None:
        return ins, specs
    return ins + [dep], specs + [pl.BlockSpec(memory_space=pl.ANY)]


def _resident_weight():
    return pl.BlockSpec((DFF, DM), lambda i: (0, 0), pipeline_mode=pl.Buffered(1))


def ffn_fwd(x, gain, wg, wu, wd, *, name, tm=512):
    def body(x_ref, g_ref, wg_ref, wu_ref, wd_ref, y_ref, h_ref, gg_ref, uu_ref):
        x = x_ref[...]
        h = (x * _rms(x) * g_ref[...]).astype(BF16)
        h_ref[...] = h
        gg = _dot(h, wg_ref[...], NT)
        uu = _dot(h, wu_ref[...], NT)
        gg_ref[...] = gg.astype(BF16)
        uu_ref[...] = uu.astype(BF16)
        act = (gg * jax.nn.sigmoid(gg) * uu).astype(BF16)
        y_ref[...] = x + 0.5 * _dot(act, wd_ref[...])

    s = x.shape[0]
    tile = pl.BlockSpec((tm, DM), lambda i: (i, 0))
    hid = pl.BlockSpec((tm, DFF), lambda i: (i, 0))
    w = _resident_weight()
    return pl.pallas_call(
        body, name=name,
        out_shape=(_sds((s, DM), F32), _sds((s, DM), BF16), _sds((s, DFF), BF16), _sds((s, DFF), BF16)),
        grid=(s // tm,), in_specs=[tile, pl.BlockSpec((1, DM), lambda i: (0, 0)), w, w, w],
        out_specs=(tile, tile, hid, hid), compiler_params=_params(("parallel",)))(x, gain, wg, wu, wd)


def ffn_bwd_tokens(dy, x, gain, gg, uu, wg, wu, wd, *, name, tm=256, dep=None):
    def body(dy_ref, x_ref, g_ref, gg_ref, uu_ref, wg_ref, wu_ref, wd_ref, *rest):
        dx_ref, dgain_ref, act_ref, dg_ref, du_ref = rest[-5:]

        @pl.when(pl.program_id(0) == 0)
        def _():
            dgain_ref[...] = jnp.zeros_like(dgain_ref)

        dy = dy_ref[...]
        dact = _dot((0.5 * dy).astype(BF16), wd_ref[...], NT)
        g = gg_ref[...].astype(F32)
        u = uu_ref[...].astype(F32)
        sg = jax.nn.sigmoid(g)
        silu = g * sg
        act_ref[...] = (silu * u).astype(BF16)
        dg = (dact * u * (sg * (1.0 + g * (1.0 - sg)))).astype(BF16)
        du = (dact * silu).astype(BF16)
        dg_ref[...] = dg
        du_ref[...] = du
        dx, dgain = _rms_bwd_math(_dot(dg, wg_ref[...]) + _dot(du, wu_ref[...]), x_ref[...], g_ref[...])
        dx_ref[...] = dy + dx
        dgain_ref[...] += dgain

    s = x.shape[0]
    tile = pl.BlockSpec((tm, DM), lambda i: (i, 0))
    vec = pl.BlockSpec((1, DM), lambda i: (0, 0))
    hid = pl.BlockSpec((tm, DFF), lambda i: (i, 0))
    hshape = _sds((s, DFF), BF16)
    w = _resident_weight()
    ins, specs = _with_dep([dy, x, gain, gg, uu, wg, wu, wd], [tile, tile, vec, hid, hid, w, w, w], dep)
    return pl.pallas_call(
        body, name=name, out_shape=(_sds((s, DM), F32), _sds((1, DM), F32), hshape, hshape, hshape),
        grid=(s // tm,), in_specs=specs, out_specs=(tile, vec, hid, hid, hid),
        compiler_params=_params(("arbitrary",)))(*ins)


def ffn_bwd_weights(h, dy, act, dg, du, *, name, tf=256):
    def body(h_ref, dy_ref, act_ref, dg_ref, du_ref, gwg_ref, gwu_ref, gwd_ref):
        h = h_ref[...]
        gwg_ref[...] = _dot(dg_ref[...], h, TN).astype(BF16)
        gwu_ref[...] = _dot(du_ref[...], h, TN).astype(BF16)
        gwd_ref[...] = (0.5 * _dot(act_ref[...], dy_ref[...], TN)).astype(BF16)

    s = h.shape[0]
    full = pl.BlockSpec((s, DM), lambda f: (0, 0))
    hid = pl.BlockSpec((s, tf), lambda f: (0, f))
    wt = pl.BlockSpec((tf, DM), lambda f: (f, 0))
    wshape = _sds((DFF, DM), BF16)
    return pl.pallas_call(
        body, name=name, out_shape=(wshape, wshape, wshape), grid=(DFF // tf,), in_specs=[full, full, hid, hid, hid],
        out_specs=(wt, wt, wt), compiler_params=_params(("parallel",)))(h, dy, act, dg, du)


def _group_mean(v, bd):
    hi = v.astype(BF16)
    lo = (v - hi.astype(F32)).astype(BF16)
    return _dot(hi, bd) + _dot(lo, bd)


def _block_diag(width):
    idx = np.arange(width) // HD
    return jnp.asarray((idx[:, None] == idx[None, :]).astype(np.float32) / HD, dtype=BF16)


def qknorm_fwd(z, gq_na, gk_na, gq_sw, gk_sw, *, name, tm=256):
    def body(zq_ref, zk_ref, zv_ref, zs_ref, zkv_ref, gqa_ref, gka_ref, gqs_ref, gks_ref, bd_ref, bd2_ref,
             qa_ref, ka_ref, va_ref, qs_ref, kv_ref):
        bd = bd_ref[...]

        def norm(x, g, bdm):
            return x * lax.rsqrt(_group_mean(x * x, bdm) + EPS) * g

        qa_ref[...] = (norm(zq_ref[...], gqa_ref[...], bd) * QK_SCALE).astype(BF16)
        ka_ref[...] = norm(zk_ref[...], gka_ref[...], bd).astype(BF16)
        va_ref[...] = zv_ref[...].astype(BF16)
        qs_ref[...] = (norm(zs_ref[...], gqs_ref[...], bd) * QK_SCALE).astype(BF16)
        kv = zkv_ref[...]
        kv_ref[:, 0:128] = norm(kv[:, 0:128], gks_ref[...], bd2_ref[...]).astype(BF16)
        kv_ref[:, 128:256] = kv[:, 128:256].astype(BF16)

    s = z.shape[0]
    col = lambda j: pl.BlockSpec((tm, 512), lambda i, j=j: (i, j))
    vec = lambda w: pl.BlockSpec((1, w), lambda i: (0, 0))
    o512 = pl.BlockSpec((tm, 512), lambda i: (i, 0))
    g512 = lambda g: jnp.tile(g.reshape(1, HD), (1, 8))
    return pl.pallas_call(
        body, name=name,
        out_shape=(_sds((s, 512), BF16),) * 4 + (_sds((s, 256), BF16),), grid=(s // tm,),
        in_specs=[col(0), col(1), col(2), col(3), pl.BlockSpec((tm, 256), lambda i: (i, 8)), vec(512), vec(512), vec(512),
                  vec(128), pl.BlockSpec((512, 512), lambda i: (0, 0)), pl.BlockSpec((128, 128), lambda i: (0, 0))],
        out_specs=(o512, o512, o512, o512, pl.BlockSpec((tm, 256), lambda i: (i, 0))),
        compiler_params=_params(("parallel",)))(
            z, z, z, z, z, g512(gq_na), g512(gk_na), g512(gq_sw), jnp.tile(gk_sw.reshape(1, HD), (1, 2)),
            _block_diag(512), _block_diag(128))


def qknorm_bwd(z, dqa, dka, dva, dqs, dkv, gq_na, gk_na, gq_sw, gk_sw, *, name, tm=256):
    def body(zq_ref, zk_ref, zs_ref, zkv_ref, dqa_ref, dka_ref, dva_ref, dqs_ref, dkv_ref, gqa_ref, gka_ref, gqs_ref,
             gks_ref, bd_ref, bd2_ref, dz_ref, dgqa_ref, dgka_ref, dgqs_ref, dgks_ref):
        @pl.when(pl.program_id(0) == 0)
        def _():
            dgqa_ref[...] = jnp.zeros_like(dgqa_ref)
            dgka_ref[...] = jnp.zeros_like(dgka_ref)
            dgqs_ref[...] = jnp.zeros_like(dgqs_ref)
            dgks_ref[...] = jnp.zeros_like(dgks_ref)

        bd = bd_ref[...]

        def bwd(x, dy, g, bdm, dg_ref):
            r = lax.rsqrt(_group_mean(x * x, bdm) + EPS)
            xh = x * r
            dg_ref[...] += jnp.sum(dy * xh, axis=0, keepdims=True)
            dxn = dy * g
            return r * (dxn - xh * _group_mean(dxn * xh, bdm))

        dz_ref[:, 0:512] = bwd(zq_ref[...], dqa_ref[...] * QK_SCALE, gqa_ref[...], bd, dgqa_ref).astype(BF16)
        dz_ref[:, 512:1024] = bwd(zk_ref[...], dka_ref[...], gka_ref[...], bd, dgka_ref).astype(BF16)
        dz_ref[:, 1024:1536] = dva_ref[...].astype(BF16)
        dz_ref[:, 1536:2048] = bwd(zs_ref[...], dqs_ref[...] * QK_SCALE, gqs_ref[...], bd, dgqs_ref).astype(BF16)
        dkv = dkv_ref[...]
        dz_ref[:, 2048:2176] = bwd(zkv_ref[:, 0:128], dkv[:, 0:128], gks_ref[...], bd2_ref[...], dgks_ref).astype(BF16)
        dz_ref[:, 2176:2304] = dkv[:, 128:256].astype(BF16)

    s = z.shape[0]
    col = lambda j: pl.BlockSpec((tm, 512), lambda i, j=j: (i, j))
    t512 = pl.BlockSpec((tm, 512), lambda i: (i, 0))
    t256 = pl.BlockSpec((tm, 256), lambda i: (i, 0))
    vec = lambda w: pl.BlockSpec((1, w), lambda i: (0, 0))
    g512 = lambda g: jnp.tile(g.reshape(1, HD), (1, 8))
    return pl.pallas_call(
        body, name=name,
        out_shape=(_sds((s, ATT_W), BF16), _sds((1, 512), F32), _sds((1, 512), F32), _sds((1, 512), F32), _sds((1, 128), F32)),
        grid=(s // tm,),
        in_specs=[col(0), col(1), col(3), pl.BlockSpec((tm, 256), lambda i: (i, 8)), t512, t512, t512, t512, t256,
                  vec(512), vec(512), vec(512), vec(128), pl.BlockSpec((512, 512), lambda i: (0, 0)),
                  pl.BlockSpec((128, 128), lambda i: (0, 0))],
        out_specs=(pl.BlockSpec((tm, ATT_W), lambda i: (i, 0)), vec(512), vec(512), vec(512), vec(128)),
        compiler_params=_params(("arbitrary",)))(
            z, z, z, z, dqa, dka, dva, dqs, dkv, g512(gq_na), g512(gk_na), g512(gq_sw),
            jnp.tile(gk_sw.reshape(1, HD), (1, 2)), _block_diag(512), _block_diag(128))


def _na_row_start(r):
    return jnp.clip(r - NA_WR // 2, 0, ROWS - NA_WR)


def na_bias_table(rpb, *, name):
    t = jnp.pad(rpb, ((0, 0), (0, 2), (0, HD - (2 * NA_WC - 1))))
    pairs = jnp.concatenate([t[:, :16], t[:, 1:17]], axis=-1).reshape(NA_HEADS, 16, 1, 128)

    def body(t_ref, o_ref):
        p = pl.program_id(0)
        q = lax.broadcasted_iota(jnp.int32, (GRID_W, 128), 0)
        kc = lax.broadcasted_iota(jnp.int32, (GRID_W, 128), 1) & (GRID_W - 1)
        cs = jnp.clip(q - NA_WC // 2, 0, GRID_W - NA_WC)
        ok = (kc >= cs) & (kc < cs + NA_WC)
        for h in range(NA_HEADS):
            for pr in range(NA_WR // 2):
                x = jnp.broadcast_to(t_ref[h, 2 * pr - p + NA_WR - 1], (GRID_W, 128))
                b = pltpu.roll(x, 128 - (NA_WC - 1), 1, stride=1, stride_axis=0)
                o_ref[h, :, 128 * pr:128 * pr + 128] = jnp.where(ok, b, NEG)

    return pl.pallas_call(
        body, name=name, out_shape=_sds((NA_WR, NA_HEADS, GRID_W, NA_KEYS), F32), grid=(NA_WR,),
        in_specs=[pl.BlockSpec((NA_HEADS, 16, 1, 128), lambda p: (0, 0, 0, 0))],
        out_specs=pl.BlockSpec((None, NA_HEADS, GRID_W, NA_KEYS), lambda p: (p, 0, 0, 0)),
        compiler_params=_params(("parallel",)))(pairs)


def _lane_halves():
    lane = lax.broadcasted_iota(jnp.int32, (1, 128), 1)
    return lane < HD


def na_fwd(q, k, v, bias, *, name):
    def body(q_ref, k_ref, v_ref, b_ref, o_ref, lse_ref):
        r = pl.program_id(0)
        off = pl.multiple_of(_na_row_start(r) * GRID_W, GRID_W)
        first = _lane_halves()
        sels = [first, jnp.logical_not(first)]
        lanes = [slice(128 * j, 128 * j + 128) for j in range(NA_HEADS // 2)]
        q2s = [q_ref[:, l] for l in lanes]
        k2s = [k_ref[pl.ds(off, NA_KEYS), l] for l in lanes]
        v2s = [v_ref[pl.ds(off, NA_KEYS), l] for l in lanes]
        scores = []
        for h in range(NA_HEADS):
            j, half = divmod(h, 2)
            scores.append(_dot(jnp.where(sels[half], q2s[j], jnp.zeros_like(q2s[j])), k2s[j], NT))
        probs, lses = [], []
        for h in range(NA_HEADS):
            b = b_ref[h]
            s = jnp.where(b > 0.5 * NEG, scores[h] + b, NEG)
            m = jnp.max(s, axis=-1, keepdims=True)
            e = jnp.exp(s - m)
            l = jnp.sum(e, axis=-1, keepdims=True)
            probs.append((e / l).astype(BF16))
            lses.append(m + jnp.log(l))
        for j in range(NA_HEADS // 2):
            zero = jnp.zeros_like(v2s[j])
            o2 = (_dot(probs[2 * j], jnp.where(sels[0], v2s[j], zero))
                  + _dot(probs[2 * j + 1], jnp.where(sels[1], v2s[j], zero)))
            o_ref[:, lanes[j]] = o2.astype(BF16)
        lse_ref[...] = jnp.concatenate(lses, axis=1)

    s_tok = q.shape[0]
    full = pl.BlockSpec((s_tok, 512), lambda r: (0, 0))
    return pl.pallas_call(
        body, name=name, out_shape=(_sds((s_tok, 512), BF16), _sds((s_tok, NA_HEADS), F32)), grid=(ROWS,),
        in_specs=[pl.BlockSpec((GRID_W, 512), lambda r: (r, 0)), full, full,
                  pl.BlockSpec((None, NA_HEADS, GRID_W, NA_KEYS), lambda r: (r - _na_row_start(r), 0, 0, 0))],
        out_specs=(pl.BlockSpec((GRID_W, 512), lambda r: (r, 0)), pl.BlockSpec((GRID_W, NA_HEADS), lambda r: (r, 0))),
        compiler_params=_params(("parallel",)))(q, k, v, bias)


def na_bwd(q, k, v, o, do, lse, bias, *, name):
    def body(q_ref, k_ref, v_ref, o_ref, do_ref, lse_ref, b_ref, dq_ref, dk_ref, dv_ref, db_ref):
        r = pl.program_id(0)

        @pl.when(r == 0)
        def _():
            dk_ref[...] = jnp.zeros_like(dk_ref)
            dv_ref[...] = jnp.zeros_like(dv_ref)

        @pl.when((r <= NA_WR // 2) | (r > ROWS - NA_WR // 2))
        def _():
            db_ref[...] = jnp.zeros_like(db_ref)

        off = pl.multiple_of(_na_row_start(r) * GRID_W, GRID_W)
        first = _lane_halves()
        sels = [first, jnp.logical_not(first)]
        lanes = [slice(128 * j, 128 * j + 128) for j in range(NA_HEADS // 2)]
        q2s = [q_ref[:, l] for l in lanes]
        k2s = [k_ref[pl.ds(off, NA_KEYS), l] for l in lanes]
        v2s = [v_ref[pl.ds(off, NA_KEYS), l] for l in lanes]
        do2s = [do_ref[:, l] for l in lanes]
        prods = [do2s[j].astype(F32) * o_ref[:, lanes[j]].astype(F32) for j in range(NA_HEADS // 2)]
        lse = lse_ref[...]
        qhs, dohs, scores, dps = [], [], [], []
        for h in range(NA_HEADS):
            j, half = divmod(h, 2)
            qhs.append(jnp.where(sels[half], q2s[j], jnp.zeros_like(q2s[j])))
            dohs.append(jnp.where(sels[half], do2s[j], jnp.zeros_like(do2s[j])))
            scores.append(_dot(qhs[h], k2s[j], NT))
            dps.append(_dot(dohs[h], v2s[j], NT))
        pbs, dsbs = [], []
        for h in range(NA_HEADS):
            j, half = divmod(h, 2)
            b = b_ref[h]
            s = jnp.where(b > 0.5 * NEG, scores[h] + b, NEG)
            p = jnp.exp(s - lse[:, h:h + 1])
            delta = jnp.sum(jnp.where(sels[half], prods[j], 0.0), axis=-1, keepdims=True)
            ds = p * (dps[h] - delta)
            db_ref[h] += ds
            pbs.append(p.astype(BF16))
            dsbs.append(ds.astype(BF16))
        for j in range(NA_HEADS // 2):
            a, b = 2 * j, 2 * j + 1
            zero = jnp.zeros_like(k2s[j])
            dq_ref[:, lanes[j]] = (_dot(dsbs[a], jnp.where(sels[0], k2s[j], zero))
                                   + _dot(dsbs[b], jnp.where(sels[1], k2s[j], zero)))
            dk_ref[pl.ds(off, NA_KEYS), lanes[j]] += _dot(dsbs[a], qhs[a], TN) + _dot(dsbs[b], qhs[b], TN)
            dv_ref[pl.ds(off, NA_KEYS), lanes[j]] += _dot(pbs[a], dohs[a], TN) + _dot(pbs[b], dohs[b], TN)

    s_tok = q.shape[0]
    full = pl.BlockSpec((s_tok, 512), lambda r: (0, 0))
    row = pl.BlockSpec((GRID_W, 512), lambda r: (r, 0))
    bias_spec = pl.BlockSpec((None, NA_HEADS, GRID_W, NA_KEYS), lambda r: (r - _na_row_start(r), 0, 0, 0))
    return pl.pallas_call(
        body, name=name,
        out_shape=(_sds((s_tok, 512), F32), _sds((s_tok, 512), F32), _sds((s_tok, 512), F32),
                   _sds((NA_WR, NA_HEADS, GRID_W, NA_KEYS), F32)),
        grid=(ROWS,),
        in_specs=[row, full, full, row, row, pl.BlockSpec((GRID_W, NA_HEADS), lambda r: (r, 0)), bias_spec],
        out_specs=(row, full, full, bias_spec), compiler_params=_params(("arbitrary",)))(q, k, v, o, do, lse, bias)


def t5_bucket_map():
    rel = np.arange(SW_KEYS)[None, :] - SW_BLK - np.arange(SW_BLK)[:, None]
    nb = 16
    max_exact = nb // 2
    n = np.abs(rel)
    large = max_exact + (np.log(np.maximum(n, 1) / max_exact) / np.log(128 / max_exact) * (nb - max_exact)).astype(np.int32)
    large = np.minimum(large, nb - 1)
    return ((rel > 0) * nb + np.where(n < max_exact, n, large)).astype(np.int32)


def t5_bias(table, *, name):
    rel = np.arange(-SW_BLK, SW_BLK + 1)
    nb, max_exact = 16, 8
    n = np.abs(rel)
    large = max_exact + (np.log(np.maximum(n, 1) / max_exact) / np.log(128 / max_exact) * (nb - max_exact)).astype(np.int32)
    bucket = ((rel > 0) * nb + np.where(n < max_exact, n, np.minimum(large, nb - 1))).astype(np.int32)
    u = jnp.pad(table[jnp.asarray(bucket)].T, ((0, 0), (0, SW_KEYS - bucket.shape[0]))).reshape(8, 1, SW_KEYS)

    def body(u_ref, o_ref):
        for h in range(8):
            x = jnp.broadcast_to(u_ref[h], (SW_BLK, SW_KEYS))
            o_ref[h] = pltpu.roll(x, 0, 1, stride=1, stride_axis=0)

    return pl.pallas_call(body, name=name, out_shape=_sds((8, SW_BLK, SW_KEYS), F32), compiler_params=_params())(u)


def _sw_valid(n):
    a = lax.broadcasted_iota(jnp.int32, (SW_BLK, SW_KEYS), 0)
    j = lax.broadcasted_iota(jnp.int32, (SW_BLK, SW_KEYS), 1)
    kpos = (n - 1) * SW_BLK + j
    return (jnp.abs(j - SW_BLK - a) <= SW_BLK) & (kpos >= 0) & (kpos < SEQ)


def _dup_group(x2, g, first):
    rolled = pltpu.roll(x2, HD, 1)
    return jnp.where(first, x2, rolled) if g == 0 else jnp.where(first, rolled, x2)


def sw_fwd(q, kv, t5, sink, *, name):
    def body(q_ref, kv_ref, t5_ref, sink_ref, o_ref, lse_ref):
        n = pl.program_id(0)
        off = pl.multiple_of(n * SW_BLK, SW_BLK)
        first = _lane_halves()
        sels = [first, jnp.logical_not(first)]
        valid = _sw_valid(n)
        k2 = kv_ref[pl.ds(off, SW_KEYS), 0:128]
        v2 = kv_ref[pl.ds(off, SW_KEYS), 128:256]
        kk = [_dup_group(k2, g, first) for g in range(2)]
        vv = [_dup_group(v2, g, first) for g in range(2)]
        q2s = [q_ref[:, 128 * j:128 * j + 128] for j in range(4)]
        scores = []
        for h in range(8):
            j, half = divmod(h, 2)
            scores.append(_dot(jnp.where(sels[half], q2s[j], jnp.zeros_like(q2s[j])), kk[j // 2], NT))
        probs, lses = [], []
        for h in range(8):
            s = jnp.where(valid, scores[h] + t5_ref[h], NEG)
            snk = sink_ref[h]
            m = jnp.maximum(jnp.max(s, axis=-1, keepdims=True), snk)
            e = jnp.exp(s - m)
            den = jnp.sum(e, axis=-1, keepdims=True) + jnp.exp(snk - m)
            probs.append((e / den).astype(BF16))
            lses.append(m + jnp.log(den))
        outs = []
        for j in range(4):
            vg = vv[j // 2]
            zero = jnp.zeros_like(vg)
            outs.append(_dot(probs[2 * j], jnp.where(sels[0], vg, zero)) + _dot(probs[2 * j + 1], jnp.where(sels[1], vg, zero)))
        o_ref[...] = jnp.concatenate(outs, axis=1).astype(BF16)
        lse_ref[...] = jnp.concatenate(lses, axis=1)

    s_tok = q.shape[0]
    blk = pl.BlockSpec((SW_BLK, 512), lambda n: (n, 0))
    return pl.pallas_call(
        body, name=name, out_shape=(_sds((s_tok, 512), BF16), _sds((s_tok, 8), F32)), grid=(SW_NB,),
        in_specs=[blk, pl.BlockSpec(kv.shape, lambda n: (0, 0)), pl.BlockSpec((8, SW_BLK, SW_KEYS), lambda n: (0, 0, 0)),
                  pl.BlockSpec(memory_space=pltpu.SMEM)],
        out_specs=(blk, pl.BlockSpec((SW_BLK, 8), lambda n: (n, 0))), compiler_params=_params(("parallel",)))(q, kv, t5, sink)


def sw_bwd(q, kv, o, do, lse, t5, sink, *, name):
    def body(q_ref, kv_ref, o_ref, do_ref, lse_ref, t5_ref, sink_ref, dq_ref, dkv_ref, dt5_ref, dsink_ref):
        n = pl.program_id(0)

        @pl.when(n == 0)
        def _():
            dkv_ref[...] = jnp.zeros_like(dkv_ref)
            dt5_ref[...] = jnp.zeros_like(dt5_ref)
            dsink_ref[...] = jnp.zeros_like(dsink_ref)

        off = pl.multiple_of(n * SW_BLK, SW_BLK)
        first = _lane_halves()
        sels = [first, jnp.logical_not(first)]
        valid = _sw_valid(n)
        k2 = kv_ref[pl.ds(off, SW_KEYS), 0:128]
        v2 = kv_ref[pl.ds(off, SW_KEYS), 128:256]
        kk = [_dup_group(k2, g, first) for g in range(2)]
        vv = [_dup_group(v2, g, first) for g in range(2)]
        lanes = [slice(128 * j, 128 * j + 128) for j in range(4)]
        q2s = [q_ref[:, l] for l in lanes]
        do2s = [do_ref[:, l] for l in lanes]
        prods = [do2s[j].astype(F32) * o_ref[:, lanes[j]].astype(F32) for j in range(4)]
        lse = lse_ref[...]
        qhs, dohs, scores, dps = [], [], [], []
        for h in range(8):
            j, half = divmod(h, 2)
            qhs.append(jnp.where(sels[half], q2s[j], jnp.zeros_like(q2s[j])))
            dohs.append(jnp.where(sels[half], do2s[j], jnp.zeros_like(do2s[j])))
            scores.append(_dot(qhs[h], kk[j // 2], NT))
            dps.append(_dot(dohs[h], vv[j // 2], NT))
        pbs, dsbs, dss, dsinks = [], [], [], []
        for h in range(8):
            j, half = divmod(h, 2)
            s = jnp.where(valid, scores[h] + t5_ref[h], NEG)
            lse_h = lse[:, h:h + 1]
            p = jnp.exp(s - lse_h)
            delta = jnp.sum(jnp.where(sels[half], prods[j], 0.0), axis=-1, keepdims=True)
            ds = p * (dps[h] - delta)
            dss.append(ds)
            dsinks.append(-jnp.sum(jnp.exp(sink_ref[h] - lse_h) * delta, axis=0, keepdims=True))
            pbs.append(p.astype(BF16))
            dsbs.append(ds.astype(BF16))
        dt5_ref[...] += jnp.stack(dss)
        dsink_ref[...] += jnp.concatenate(dsinks, axis=1)
        dqs = []
        for j in range(4):
            a, b = 2 * j, 2 * j + 1
            zero = jnp.zeros_like(kk[j // 2])
            dqs.append(_dot(dsbs[a], jnp.where(sels[0], kk[j // 2], zero)) + _dot(dsbs[b], jnp.where(sels[1], kk[j // 2], zero)))
        dq_ref[...] = jnp.concatenate(dqs, axis=1)
        dk_groups, dv_groups = [], []
        for g in range(2):
            dkk = sum(_dot(dsbs[h], qhs[h], TN) for h in range(4 * g, 4 * g + 4))
            dvv = sum(_dot(pbs[h], dohs[h], TN) for h in range(4 * g, 4 * g + 4))
            dk_groups.append(dkk + pltpu.roll(dkk, HD, 1))
            dv_groups.append(dvv + pltpu.roll(dvv, HD, 1))
        dkv_ref[pl.ds(off, SW_KEYS), :] += jnp.concatenate(
            [jnp.where(first, dk_groups[0], dk_groups[1]), jnp.where(first, dv_groups[0], dv_groups[1])], axis=1)

    s_tok = q.shape[0]
    blk = pl.BlockSpec((SW_BLK, 512), lambda n: (n, 0))
    kv_spec = pl.BlockSpec(kv.shape, lambda n: (0, 0))
    t5_spec = pl.BlockSpec((8, SW_BLK, SW_KEYS), lambda n: (0, 0, 0))
    vec = pl.BlockSpec((1, 8), lambda n: (0, 0))
    return pl.pallas_call(
        body, name=name,
        out_shape=(_sds((s_tok, 512), F32), _sds(kv.shape, F32), _sds((8, SW_BLK, SW_KEYS), F32), _sds((1, 8), F32)),
        grid=(SW_NB,), in_specs=[blk, kv_spec, blk, blk, pl.BlockSpec((SW_BLK, 8), lambda n: (n, 0)), t5_spec,
                                 pl.BlockSpec(memory_space=pltpu.SMEM)],
        out_specs=(blk, kv_spec, t5_spec, vec), compiler_params=_params(("arbitrary",)))(q, kv, o, do, lse, t5, sink)


def gate_fwd(zg, bias, pa, ps, *, name, tm=512):
    def body(z0_ref, z1_ref, b0_ref, b1_ref, pa_ref, ps_ref, m_ref):
        g0 = jax.nn.sigmoid(z0_ref[...] + b0_ref[...])
        g1 = jax.nn.sigmoid(z1_ref[...] + b1_ref[...])
        m_ref[...] = (g0 * pa_ref[...] + g1 * ps_ref[...]).astype(BF16)

    s = zg.shape[0]
    half = lambda j: pl.BlockSpec((tm, DM), lambda i, j=j: (i, j))
    bvec = lambda j: pl.BlockSpec((1, DM), lambda i, j=j: (0, j))
    return pl.pallas_call(
        body, name=name, out_shape=_sds((s, DM), BF16), grid=(s // tm,),
        in_specs=[half(0), half(1), bvec(0), bvec(1), half(0), half(0)], out_specs=half(0),
        compiler_params=_params(("parallel",)))(zg, zg, bias, bias, pa, ps)


def gate_bwd(dm, zg, bias, pa, ps, *, name, tm=512):
    def body(dm_ref, z0_ref, z1_ref, b0_ref, b1_ref, pa_ref, ps_ref, dpa_ref, dps_ref, dz_ref, db_ref):
        @pl.when(pl.program_id(0) == 0)
        def _():
            db_ref[...] = jnp.zeros_like(db_ref)

        dm = dm_ref[...]
        g0 = jax.nn.sigmoid(z0_ref[...] + b0_ref[...])
        g1 = jax.nn.sigmoid(z1_ref[...] + b1_ref[...])
        dpa_ref[...] = (dm * g0).astype(BF16)
        dps_ref[...] = (dm * g1).astype(BF16)
        dz0 = dm * pa_ref[...] * g0 * (1.0 - g0)
        dz1 = dm * ps_ref[...] * g1 * (1.0 - g1)
        dz_ref[:, 0:DM] = dz0.astype(BF16)
        dz_ref[:, DM:2 * DM] = dz1.astype(BF16)
        db_ref[:, 0:DM] += jnp.sum(dz0, axis=0, keepdims=True)
        db_ref[:, DM:2 * DM] += jnp.sum(dz1, axis=0, keepdims=True)

    s = zg.shape[0]
    half = lambda j: pl.BlockSpec((tm, DM), lambda i, j=j: (i, j))
    bvec = lambda j: pl.BlockSpec((1, DM), lambda i, j=j: (0, j))
    return pl.pallas_call(
        body, name=name,
        out_shape=(_sds((s, DM), BF16), _sds((s, DM), BF16), _sds((s, GATE_W), BF16), _sds((1, GATE_W), F32)),
        grid=(s // tm,), in_specs=[half(0), half(0), half(1), bvec(0), bvec(1), half(0), half(0)],
        out_specs=(half(0), half(0), pl.BlockSpec((tm, GATE_W), lambda i: (i, 0)), pl.BlockSpec((1, GATE_W), lambda i: (0, 0))),
        compiler_params=_params(("arbitrary",)))(dm, zg, zg, bias, bias, pa, ps)


def loss_head(y, target, *, name, tm=512):
    def body(y_ref, t_ref, dy_ref, l_ref):
        @pl.when(pl.program_id(0) == 0)
        def _():
            l_ref[...] = jnp.zeros_like(l_ref)

        err = y_ref[...] - t_ref[...]
        dy_ref[...] = err * (1.0 / DM)
        l_ref[...] += 0.5 * jnp.sum(jnp.mean(err * err, axis=-1, keepdims=True), axis=0, keepdims=True)

    s = y.shape[0]
    tile = pl.BlockSpec((tm, DM), lambda i: (i, 0))
    return pl.pallas_call(
        body, name=name, out_shape=(_sds((s, DM), F32), _sds((1, 128), F32)), grid=(s // tm,), in_specs=[tile, tile],
        out_specs=(tile, pl.BlockSpec((1, 128), lambda i: (0, 0))), compiler_params=_params(("arbitrary",)))(y, target)


def adamw(w, g, m, v, *, name):
    def body(w_ref, g_ref, m_ref, v_ref, d_ref, nm_ref, nv_ref):
        g = g_ref[...]
        nm = ADAM_B1 * m_ref[...] + (1.0 - ADAM_B1) * g
        nv = ADAM_B2 * v_ref[...] + (1.0 - ADAM_B2) * jnp.square(g)
        m_hat = nm / (1.0 - ADAM_B1 ** ADAM_STEP)
        v_hat = nv / (1.0 - ADAM_B2 ** ADAM_STEP)
        d_ref[...] = -ADAM_LR * (m_hat / (jnp.sqrt(v_hat) + ADAM_EPS) + ADAM_WD * w_ref[...])
        nm_ref[...] = nm
        nv_ref[...] = nv

    b, k, n = w.shape
    tk = k // 4 if k % 32 == 0 else k
    spec = pl.BlockSpec((None, tk, n), lambda i, j: (i, j, 0))
    out = _sds(w.shape, F32)
    return pl.pallas_call(
        body, name=name, out_shape=(out, out, out), grid=(b, k // tk), in_specs=[spec] * 4, out_specs=(spec,) * 3,
        compiler_params=_params(("parallel", "parallel")))(w, g, m, v)


def adamw_layer(ws, ms, vs, mines, theirs, cidx, layer, filled=None, *, name):
    cnt = len(ws)
    _, k, n = ws[0].shape
    nt = 2
    tk = k // 2 // nt

    def body(c_ref, *refs):
        own = pl.program_id(0) == c_ref[0]
        outs = refs[-4 * cnt:]
        for i in range(cnt):
            w_ref, m_ref, v_ref, a_ref, b_ref = refs[5 * i:5 * i + 5]
            g_ref, d_ref, nm_ref, nv_ref = outs[4 * i:4 * i + 4]
            g = jnp.where(own, a_ref[...], b_ref[...])
            g_ref[...] = g
            nm = ADAM_B1 * m_ref[...] + (1.0 - ADAM_B1) * g
            nv = ADAM_B2 * v_ref[...] + (1.0 - ADAM_B2) * jnp.square(g)
            m_hat = nm / (1.0 - ADAM_B1 ** ADAM_STEP)
            v_hat = nv / (1.0 - ADAM_B2 ** ADAM_STEP)
            d_ref[...] = -ADAM_LR * (m_hat / (jnp.sqrt(v_hat) + ADAM_EPS) + ADAM_WD * w_ref[...])
            nm_ref[...] = nm
            nv_ref[...] = nv

    full = pl.BlockSpec((None, tk, n), lambda hf, t, c: (layer, hf * nt + t, 0))
    half_mine = pl.BlockSpec((tk, n), lambda hf, t, c: (jnp.where(hf == c[0], t, 0), 0))
    half_theirs = pl.BlockSpec((tk, n), lambda hf, t, c: (jnp.where(hf != c[0], t, 0), 0))
    out = _sds(ws[0].shape, F32)
    ins, specs, aliases = [cidx], [], {}
    for i in range(cnt):
        ins += [ws[i], ms[i], vs[i], mines[i], theirs[i]]
        specs += [full, full, full, half_mine, half_theirs]
    if filled is not None:
        aliases = {len(ins) + j: j for j in range(4 * cnt)}
        ins += [a for f in filled for a in f]
        specs += [pl.BlockSpec(memory_space=pl.ANY)] * (4 * cnt)
    res = pl.pallas_call(
        body, name=name, out_shape=(out,) * (4 * cnt),
        grid_spec=pltpu.PrefetchScalarGridSpec(
            num_scalar_prefetch=1, grid=(2, nt), in_specs=specs, out_specs=(full,) * (4 * cnt)),
        input_output_aliases=aliases,
        compiler_params=_params(("arbitrary", "arbitrary")))(*ins)
    return [tuple(res[4 * i:4 * i + 4]) for i in range(cnt)]


def t5_table_grad(dt5_a, dt5_b, *, name):
    def body(a_ref, b_ref, map_ref, o_ref):
        d = a_ref[...] + b_ref[...]
        bucket = map_ref[...]
        for b in range(32):
            hit = (bucket == b)[None]
            o_ref[b] = jnp.sum(jnp.sum(jnp.where(hit, d, 0.0), axis=2), axis=1, keepdims=True)

    return pl.pallas_call(
        body, name=name, out_shape=_sds((32, 8, 1), F32), compiler_params=_params())(
            dt5_a, dt5_b, jnp.asarray(t5_bucket_map()))


def rpb_grad(dbias, *, name):
    def body(d_ref, rev_ref, o_ref):
        rev = rev_ref[...]
        for h in range(NA_HEADS):
            for pr in range(NA_WR // 2):
                d = d_ref[h, :, 128 * pr:128 * pr + 128]
                hi = d.astype(BF16)
                lo = (d - hi.astype(F32)).astype(BF16)
                flipped = _dot(rev, hi) + _dot(rev, lo)
                o_ref[h, pr] = jnp.sum(pltpu.roll(flipped, 0, 1, stride=1, stride_axis=0), axis=0, keepdims=True)

    anti = jnp.asarray(np.eye(GRID_W, dtype=np.float32)[::-1], dtype=BF16)
    e = pl.pallas_call(
        body, name=name, out_shape=_sds((NA_WR, NA_HEADS, NA_WR // 2, 1, 128), F32), grid=(NA_WR,),
        in_specs=[pl.BlockSpec((None, NA_HEADS, GRID_W, NA_KEYS), lambda p: (p, 0, 0, 0)),
                  pl.BlockSpec((GRID_W, GRID_W), lambda p: (0, 0))],
        out_specs=pl.BlockSpec((None, NA_HEADS, NA_WR // 2, 1, 128), lambda p: (p, 0, 0, 0, 0)),
        compiler_params=_params(("parallel",)))(dbias, anti)
    nci, nri = 2 * NA_WC - 1, 2 * NA_WR - 1
    e = e.reshape(NA_WR, NA_HEADS, NA_WR // 2, 128).transpose(0, 2, 1, 3).reshape(NA_WR * NA_WR // 2, NA_HEADS, 128)
    parts = jnp.concatenate([e[..., 48:48 + nci], jnp.concatenate([e[..., 112:128], e[..., 0:nci - 16]], axis=-1)], axis=0)
    p, pr = np.arange(NA_WR)[:, None], np.arange(NA_WR // 2)[None, :]
    ri = np.concatenate([(2 * pr - p + NA_WR - 1).reshape(-1), (2 * pr - p + NA_WR).reshape(-1)])
    pick = jnp.asarray((ri[None, :] == np.arange(16)[:, None]).astype(np.float32))
    out = mm(pick, parts.reshape(2 * NA_WR * NA_WR // 2, NA_HEADS * nci), name=name + "_rows", exact=True)
    return out.reshape(16, NA_HEADS, nci)[:nri].transpose(1, 0, 2)


BIG = ("ffn1_w_gate", "ffn1_w_up", "ffn1_w_down", "w_in", "w_branch_na", "w_branch_sw", "w_out",
       "ffn2_w_gate", "ffn2_w_up", "ffn2_w_down")
SMALL = ("ffn1_norm", "mix_norm", "b_gate", "na_q_norm", "na_k_norm", "na_rpb", "sw_q_norm", "sw_k_norm", "sw_sink",
         "ffn2_norm")


def _cols_to_full(w4):
    return w4.transpose(1, 0, 2).reshape(w4.shape[1], NSH * w4.shape[2])


def _full_to_cols(w):
    return w.reshape(w.shape[0], NSH, w.shape[1] // NSH).transpose(1, 0, 2)


def _mixer_weights(g):
    w_in_t = g["w_in"].reshape(IN_W, DM)
    return dict(w_att_t=w_in_t[:ATT_W], w_gz_t=w_in_t[ATT_W:], wa=_cols_to_full(g["w_branch_na"]),
                ws=_cols_to_full(g["w_branch_sw"]), wo=g["w_out"].reshape(DM, DM))


GROUPS = {"ffn1": ("ffn1_w_gate", "ffn1_w_up", "ffn1_w_down"), "mix": ("w_in", "w_branch_na", "w_branch_sw", "w_out"),
          "ffn2": ("ffn2_w_gate", "ffn2_w_up", "ffn2_w_down")}


def layer_fwd(x, p, weights, t5b):
    row = lambda v: v.reshape(1, -1)
    stacked = lambda g: {n: a.reshape(DFF, DM) for n, a in g.items()}
    g1 = stacked(weights("ffn1", x))
    y1, h1, gg1, uu1 = ffn_fwd(x, row(p["ffn1_norm"]), g1["ffn1_w_gate"], g1["ffn1_w_up"], g1["ffn1_w_down"], name="ffn_fwd")
    w = _mixer_weights(weights("mix", y1))
    hm = rms_fwd(y1, row(p["mix_norm"]), name="mix_norm_fwd")
    z = mm(hm, w["w_att_t"], tb=True, name="proj_att", tm=SEQ, tn=768)
    zg = mm(hm, w["w_gz_t"], tb=True, name="proj_gate", tm=SEQ, tn=512)
    qa, ka, va, qs, kv = qknorm_fwd(z, p["na_q_norm"], p["na_k_norm"], p["sw_q_norm"], p["sw_k_norm"], name="qknorm_fwd")
    bias = p["na_bias"]
    o_na, lse_na = na_fwd(qa, ka, va, bias, name="na_fwd")
    kvp = jnp.pad(kv, ((SW_BLK, SW_BLK), (0, 0)))
    sink = p["sw_sink"]
    o_sw, lse_sw = sw_fwd(qs, kvp, t5b, sink, name="sw_fwd")
    pa = mm(o_na, w["wa"], name="branch_na", tm=1024)
    ps = mm(o_sw, w["ws"], name="branch_sw", tm=1024)
    merged = gate_fwd(zg, row(p["b_gate"]), pa, ps, name="gate_fwd")
    y2 = mm(merged, w["wo"], add=y1, name="out_proj", tm=1024)
    g2 = stacked(weights("ffn2", y2))
    y3, h2, gg2, uu2 = ffn_fwd(y2, row(p["ffn2_norm"]), g2["ffn2_w_gate"], g2["ffn2_w_up"], g2["ffn2_w_down"], name="ffn_fwd")
    saved = dict(x=x, y1=y1, h1=h1, gg1=gg1, uu1=uu1, hm=hm, z=z, zg=zg, qa=qa, ka=ka, va=va, qs=qs, kvp=kvp, bias=bias,
                 o_na=o_na, lse_na=lse_na, o_sw=o_sw, lse_sw=lse_sw, pa=pa, ps=ps, merged=merged, y2=y2, h2=h2, gg2=gg2,
                 uu2=uu2, w=w, sink=sink, g1=g1, g2=g2)
    return y3, saved


def layer_bwd(dy3, sv, p, t5b, emit, dep=None):
    w, g1, g2 = sv["w"], sv["g1"], sv["g2"]
    row = lambda v: v.reshape(1, -1)
    fold = lambda v: v.reshape(-1, HD).sum(axis=0)
    small = {}
    dy2, small["ffn2_norm"], act, dg, du = ffn_bwd_tokens(
        dy3, sv["y2"], row(p["ffn2_norm"]), sv["gg2"], sv["uu2"], g2["ffn2_w_gate"], g2["ffn2_w_up"], g2["ffn2_w_down"],
        name="ffn_bwd_tokens", dep=dep)
    shards = lambda gs: [g.reshape(NSH, FSH, DM) for g in gs]
    token = emit("ffn2", shards(ffn_bwd_weights(sv["h2"], dy3.astype(BF16), act, dg, du, name="ffn_bwd_weights")))
    dmerged = mm(dy2, w["wo"], tb=True, name="out_proj_dx", tm=1024, dep=token)
    gw_out = mm(sv["merged"], dy2, ta=True, out_dtype=BF16, name="out_proj_dw").reshape(NSH, DM // NSH, DM)
    dpa, dps, dzg, small["b_gate"] = gate_bwd(dmerged, sv["zg"], row(p["b_gate"]), sv["pa"], sv["ps"], name="gate_bwd")
    gw_na = _full_to_cols(mm(sv["o_na"], dpa, ta=True, out_dtype=BF16, name="branch_dw"))
    gw_sw = _full_to_cols(mm(sv["o_sw"], dps, ta=True, out_dtype=BF16, name="branch_dw"))
    do_na = mm(dpa, w["wa"], tb=True, out_dtype=BF16, tm=SEQ, name="branch_dx")
    do_sw = mm(dps, w["ws"], tb=True, out_dtype=BF16, tm=SEQ, name="branch_dx")
    dqa, dka, dva, dbias = na_bwd(sv["qa"], sv["ka"], sv["va"], sv["o_na"], do_na, sv["lse_na"], sv["bias"], name="na_bwd")
    dqs, dkvp, dt5, dsink = sw_bwd(sv["qs"], sv["kvp"], sv["o_sw"], do_sw, sv["lse_sw"], t5b, sv["sink"], name="sw_bwd")
    dkv = dkvp[SW_BLK:SW_BLK + SEQ]
    dz, dgqa, dgka, dgqs, dgks = qknorm_bwd(sv["z"], dqa, dka, dva, dqs, dkv, p["na_q_norm"], p["na_k_norm"],
                                            p["sw_q_norm"], p["sw_k_norm"], name="qknorm_bwd")
    small["na_q_norm"], small["na_k_norm"], small["sw_q_norm"], small["sw_k_norm"] = fold(dgqa), fold(dgka), fold(dgqs), fold(dgks)
    small["na_rpb"] = rpb_grad(dbias, name="rpb_grad")
    small["sw_sink"] = dsink
    gw_att_t = mm(dz, sv["hm"], ta=True, out_dtype=BF16, tm=768, name="proj_att_dw")
    gw_gz_t = mm(dzg, sv["hm"], ta=True, out_dtype=BF16, tm=1024, name="proj_gate_dw")
    gw_in = jnp.concatenate([gw_att_t, gw_gz_t], axis=0).reshape(NSH, IN_W // NSH, DM)
    token = emit("mix", (gw_in, gw_na, gw_sw, gw_out))
    dh = mm(dz, w["w_att_t"], tm=1024, name="proj_att_dx", dep=token)
    dh = mm(dzg, w["w_gz_t"], add=dh, tm=1024, name="proj_gate_dx")
    dy1, small["mix_norm"] = rms_bwd(dh, sv["y1"], row(p["mix_norm"]), dy2, name="mix_norm_bwd")
    dx, small["ffn1_norm"], act, dg, du = ffn_bwd_tokens(
        dy1, sv["x"], row(p["ffn1_norm"]), sv["gg1"], sv["uu1"], g1["ffn1_w_gate"], g1["ffn1_w_up"], g1["ffn1_w_down"],
        name="ffn_bwd_tokens")
    emit("ffn1", shards(ffn_bwd_weights(sv["h1"], dy1.astype(BF16), act, dg, du, name="ffn_bwd_weights")))
    return dx, small, dt5


ANY = pl.BlockSpec(memory_space=pl.ANY)


def _place():
    x, y, c = lax.axis_index("x"), lax.axis_index("y"), lax.axis_index("c")
    chips = [(1 - x, y), (x, 1 - y), (1 - x, 1 - y)]
    return x, y, c, chips


def _remote(src, dst, send_sem, recv_sem, to):
    return pltpu.make_async_remote_copy(src_ref=src, dst_ref=dst, send_sem=send_sem, recv_sem=recv_sem, device_id=to,
                                        device_id_type=MESH)


HBM = pl.BlockSpec(memory_space=pltpu.HBM)
SEM = pl.BlockSpec(memory_space=pltpu.SEMAPHORE)
ORDERED_EFFECT = pltpu.SideEffectType.DATAFLOW_SIDE_EFFECTING


def _in_hbm(v):
    return pltpu.with_memory_space_constraint(v, pltpu.HBM)


def _row_half(ref_shape_rows, c):
    half = ref_shape_rows // 2
    return pl.ds(c * half, half)


def _ici_gather_copies(w, land, send_sems, recv_sems):
    x, y, c, chips = _place()
    me = 2 * x + y
    copies = []
    for a in range(len(w)):
        rows = _row_half(w[a].shape[0], c)
        for k, chip in enumerate(chips):
            copies.append(_remote(w[a].at[rows], land[a].at[me, rows], send_sems.at[4 * a + k], recv_sems.at[4 * a + k],
                                  (*chip, c)))
        copies.append(_remote(w[a], land[a].at[me], send_sems.at[4 * a + 3], recv_sems.at[4 * a + 3], (x, y, 1 - c)))
    return copies


def _d2d_gather_copies(w, land, send_sems, recv_sems):
    x, y, c, chips = _place()
    copies = []
    for a in range(len(w)):
        rows = _row_half(w[a].shape[0], c)
        for k, (cx, cy) in enumerate(chips):
            blk = land[a].at[2 * cx + cy, rows]
            copies.append(_remote(blk, blk, send_sems.at[3 * a + k], recv_sems.at[3 * a + k], (x, y, 1 - c)))
    return copies


def _d2d_gather_waits(w, land, send_sems, recv_sems):
    x, y, c, chips = _place()
    waits = []
    for a in range(len(w)):
        rows = _row_half(w[a].shape[0], 1 - c)
        for k, (cx, cy) in enumerate(chips):
            blk = land[a].at[2 * cx + cy, rows]
            waits.append(_remote(blk, blk, send_sems.at[3 * a + k], recv_sems.at[3 * a + k], (x, y, 1 - c)))
    return waits


def gather_start(groups, dep=None, *, name):
    sizes = [len(g) for g in groups]
    shards = [s for g in groups for s in g]
    n, ng = len(shards), len(groups)
    extra = [] if dep is None else [dep]

    def body(*refs):
        first_out = 2 * n + len(extra)
        w, land, sems = refs[:n], refs[n:2 * n], refs[first_out:first_out + 2 * ng]
        off = 0
        for gi, size in enumerate(sizes):
            for cp in _ici_gather_copies(w[off:off + size], land[off:off + size], sems[2 * gi], sems[2 * gi + 1]):
                cp.start()
            off += size

    lands = [lax.empty((NSH,) + s.shape, s.dtype) for s in shards]
    sem_shapes = tuple(pltpu.SemaphoreType.DMA((4 * size,)) for size in sizes for _ in range(2))
    res = pl.pallas_call(
        body, name=name,
        out_shape=sem_shapes + tuple(pltpu.HBM(s.shape, s.dtype) for s in shards) + tuple(pltpu.HBM(l.shape, l.dtype) for l in lands),
        in_specs=[HBM] * (2 * n) + [ANY] * len(extra), out_specs=(SEM,) * (2 * ng) + (HBM,) * (2 * n),
        input_output_aliases={i: 2 * ng + i for i in range(2 * n)},
        compiler_params=pltpu.CompilerParams(has_side_effects=ORDERED_EFFECT))(
            *[_in_hbm(s) for s in shards], *[_in_hbm(l) for l in lands], *extra)
    out, off = [], 0
    for gi, size in enumerate(sizes):
        out.append((res[2 * gi], res[2 * gi + 1], list(res[2 * ng + off:2 * ng + off + size]),
                    list(res[2 * ng + n + off:2 * ng + n + off + size])))
        off += size
    return out


def gather_wait(send_sems, recv_sems, shards, lands, after, *, name):
    n = len(shards)

    def body(*refs):
        w, land = refs[:n], refs[n:2 * n]
        send, recv = refs[2 * n:2 * n + 2]
        for cp in _ici_gather_copies(w, land, send, recv):
            cp.wait_send()
            cp.wait_recv()

    res = pl.pallas_call(
        body, name=name,
        out_shape=tuple(pltpu.HBM(s.shape, s.dtype) for s in shards) + tuple(pltpu.HBM(l.shape, l.dtype) for l in lands),
        in_specs=[HBM] * (2 * n) + [SEM, SEM] + [ANY] * len(after), out_specs=(HBM,) * (2 * n),
        input_output_aliases={i: i for i in range(2 * n)},
        compiler_params=pltpu.CompilerParams(has_side_effects=ORDERED_EFFECT))(*shards, *lands, send_sems, recv_sems, *after)
    return list(res[:n]), list(res[n:])


def gather_finish(shards, lands, *, name):
    n = len(shards)

    def body(*refs):
        w, land = refs[:n], refs[n:2 * n]
        send_sems, recv_sems = refs[3 * n:]
        d2d = _d2d_gather_copies(w, land, send_sems, recv_sems)
        for cp in d2d:
            cp.start()
        for cp in _d2d_gather_waits(w, land, send_sems, recv_sems):
            cp.wait_recv()
        for cp in d2d:
            cp.wait_send()

    return list(pl.pallas_call(
        body, name=name, out_shape=tuple(pltpu.HBM(l.shape, l.dtype) for l in lands),
        in_specs=[ANY] * (2 * n), out_specs=tuple([ANY] * n), input_output_aliases={n + i: i for i in range(n)},
        scratch_shapes=[pltpu.SemaphoreType.DMA((3 * n,)), pltpu.SemaphoreType.DMA((3 * n,))])(*shards, *lands))


def _pair_exchange_copies(g, buf, send_sems, recv_sems):
    x, y, c, _ = _place()
    copies = []
    for a in range(len(g)):
        half = g[a].shape[1] // 2
        copies.append(_remote(g[a].at[:, pl.ds((1 - c) * half, half)], buf[a], send_sems.at[a], recv_sems.at[a], (x, y, 1 - c)))
    return copies


def pair_exchange_start(grads, dep=None, *, name):
    n = len(grads)
    extra = [] if dep is None else [dep]

    def body(*refs):
        sems = refs[2 * n + len(extra):]
        for cp in _pair_exchange_copies(refs[:n], refs[n:2 * n], sems[0], sems[1]):
            cp.start()
        refs[-1][...] = jnp.zeros_like(refs[-1])

    lands = [lax.empty((NSH, g.shape[1] // 2, g.shape[2]), g.dtype) for g in grads]
    res = pl.pallas_call(
        body, name=name,
        out_shape=(pltpu.SemaphoreType.DMA((n,)), pltpu.SemaphoreType.DMA((n,)))
        + tuple(pltpu.HBM(g.shape, g.dtype) for g in grads) + tuple(pltpu.HBM(l.shape, l.dtype) for l in lands)
        + (_sds((8, 128), F32),),
        in_specs=[HBM] * (2 * n) + [ANY] * len(extra),
        out_specs=(SEM, SEM) + (HBM,) * (2 * n) + (pl.BlockSpec(memory_space=pltpu.VMEM),),
        input_output_aliases={i: 2 + i for i in range(2 * n)},
        compiler_params=pltpu.CompilerParams(has_side_effects=ORDERED_EFFECT))(
            *[_in_hbm(g) for g in grads], *[_in_hbm(l) for l in lands], *extra)
    return res[0], res[1], list(res[2:2 + n]), list(res[2 + n:2 + 2 * n]), res[-1]


def pair_exchange_wait(send_sems, recv_sems, grads, lands, after, *, name):
    n = len(grads)

    def body(*refs):
        for cp in _pair_exchange_copies(refs[:n], refs[n:2 * n], refs[2 * n], refs[2 * n + 1]):
            cp.wait_send()
            cp.wait_recv()

    res = pl.pallas_call(
        body, name=name,
        out_shape=tuple(pltpu.HBM(g.shape, g.dtype) for g in grads) + tuple(pltpu.HBM(l.shape, l.dtype) for l in lands),
        in_specs=[HBM] * (2 * n) + [SEM, SEM] + [ANY] * len(after), out_specs=(HBM,) * (2 * n),
        input_output_aliases={i: i for i in range(2 * n)},
        compiler_params=pltpu.CompilerParams(has_side_effects=ORDERED_EFFECT))(*grads, *lands, send_sems, recv_sems, *after)
    return list(res[:n]), list(res[n:])


def _chip_exchange_copies(s, buf, send_sems, recv_sems):
    x, y, c, chips = _place()
    return [_remote(s[a].at[2 * cx + cy], buf[a].at[k], send_sems.at[3 * a + k], recv_sems.at[3 * a + k], (cx, cy, c))
            for a in range(len(s)) for k, (cx, cy) in enumerate(chips)]


def chip_exchange(sums, *, name):
    n = len(sums)

    def body(*refs):
        copies = _chip_exchange_copies(refs[:n], refs[n:2 * n], *refs[2 * n:])
        for cp in copies:
            cp.start()
        for cp in copies:
            cp.wait()

    return pl.pallas_call(
        body, name=name, out_shape=tuple(pltpu.HBM((3,) + s.shape[1:], s.dtype) for s in sums),
        in_specs=[ANY] * n, out_specs=tuple([ANY] * n),
        scratch_shapes=[pltpu.SemaphoreType.DMA((3 * n,)), pltpu.SemaphoreType.DMA((3 * n,))])(*sums)


def chip_exchange_start(sums, *, name):
    n = len(sums)

    def body(*refs):
        for cp in _chip_exchange_copies(refs[:n], refs[n:2 * n], refs[2 * n], refs[2 * n + 1]):
            cp.start()
        refs[-1][...] = jnp.zeros_like(refs[-1])

    lands = [lax.empty((3,) + s.shape[1:], s.dtype) for s in sums]
    res = pl.pallas_call(
        body, name=name,
        out_shape=(pltpu.SemaphoreType.DMA((3 * n,)), pltpu.SemaphoreType.DMA((3 * n,)))
        + tuple(pltpu.HBM(s.shape, s.dtype) for s in sums) + tuple(pltpu.HBM(l.shape, l.dtype) for l in lands)
        + (_sds((8, 128), F32),),
        in_specs=[HBM] * (2 * n), out_specs=(SEM, SEM) + (HBM,) * (2 * n) + (pl.BlockSpec(memory_space=pltpu.VMEM),),
        input_output_aliases={i: 2 + i for i in range(2 * n)},
        compiler_params=pltpu.CompilerParams(has_side_effects=ORDERED_EFFECT))(
            *[_in_hbm(s) for s in sums], *[_in_hbm(l) for l in lands])
    return res[0], res[1], list(res[2:2 + n]), list(res[2 + n:2 + 2 * n]), res[-1]


def chip_exchange_wait(send_sems, recv_sems, sums, lands, after, *, name):
    n = len(sums)

    def body(*refs):
        for cp in _chip_exchange_copies(refs[:n], refs[n:2 * n], refs[2 * n], refs[2 * n + 1]):
            cp.wait_send()
            cp.wait_recv()

    res = pl.pallas_call(
        body, name=name,
        out_shape=tuple(pltpu.HBM(s.shape, s.dtype) for s in sums) + tuple(pltpu.HBM(l.shape, l.dtype) for l in lands),
        in_specs=[HBM] * (2 * n) + [SEM, SEM] + [ANY] * len(after), out_specs=(HBM,) * (2 * n),
        input_output_aliases={i: i for i in range(2 * n)},
        compiler_params=pltpu.CompilerParams(has_side_effects=ORDERED_EFFECT))(*sums, *lands, send_sems, recv_sems, *after)
    return list(res[:n]), list(res[n:])


def pair_send(halves, *, name):
    n = len(halves)

    def body(*refs):
        h, got = refs[:n], refs[n:2 * n]
        send_sems, recv_sems = refs[2 * n:]
        x, y, c, _ = _place()
        copies = []
        for i in range(n):
            cp = _remote(h[i], got[i], send_sems.at[i], recv_sems.at[i], (x, y, 1 - c))
            cp.start()
            copies.append(cp)
        for cp in copies:
            cp.wait()

    return list(pl.pallas_call(
        body, name=name, out_shape=tuple(pltpu.HBM(v.shape, v.dtype) for v in halves),
        in_specs=[ANY] * n, out_specs=tuple([ANY] * n),
        scratch_shapes=[pltpu.SemaphoreType.DMA((n,)), pltpu.SemaphoreType.DMA((n,))])(*halves))


def allreduce_small(v, *, name):
    rows = v.shape[0]

    def body(v_ref, o_ref, gath, send_sems, recv_sems):
        x, y, c, _ = _place()
        me = 4 * x + 2 * y + c
        gath[me] = v_ref[...]
        copies = []
        for k in range(1, 8):
            fx, fy, fc = (k >> 2) & 1, (k >> 1) & 1, k & 1
            peer = (jnp.where(fx, 1 - x, x), jnp.where(fy, 1 - y, y), jnp.where(fc, 1 - c, c))
            cp = _remote(v_ref, gath.at[me], send_sems.at[k - 1], recv_sems.at[k - 1], peer)
            cp.start()
            copies.append(cp)
        for cp in copies:
            cp.wait()
        acc = gath[0]
        for d in range(1, 8):
            acc = acc + gath[d]
        o_ref[...] = acc

    return pl.pallas_call(
        body, name=name, out_shape=_sds(v.shape, F32),
        in_specs=[pl.BlockSpec(memory_space=pltpu.VMEM)], out_specs=pl.BlockSpec(memory_space=pltpu.VMEM),
        scratch_shapes=[pltpu.VMEM((8, rows, 128), F32), pltpu.SemaphoreType.DMA((7,)), pltpu.SemaphoreType.DMA((7,))])(v)


def _same_shape_runs(arrays):
    runs = {}
    for i, a in enumerate(arrays):
        runs.setdefault(a.shape, []).append(i)
    return list(runs.values())


def _per_shape(fn, *lists):
    out = [None] * len(lists[0])
    for idx in _same_shape_runs(lists[0]):
        for i, r in zip(idx, fn(*[[l[i] for i in idx] for l in lists])):
            out[i] = r
    return out


def add_halves(gs, bufs, cidx, *, name):
    cnt = len(gs)
    _, k, n = gs[0].shape

    def body(c_ref, *refs):
        g, b, o = refs[:cnt], refs[cnt:2 * cnt], refs[2 * cnt:]
        for i in range(cnt):
            o[i][...] = (g[i][...].astype(F32) + b[i][...].astype(F32)).astype(BF16)

    blk = pl.BlockSpec((None, k // 2, n), lambda s, c: (s, 0, 0))
    mine = pl.BlockSpec((None, k // 2, n), lambda s, c: (s, c[0], 0))
    return list(pl.pallas_call(
        body, name=name, out_shape=tuple(_sds(b.shape, BF16) for b in bufs),
        grid_spec=pltpu.PrefetchScalarGridSpec(
            num_scalar_prefetch=1, grid=(NSH,), in_specs=[mine] * cnt + [blk] * cnt, out_specs=tuple([blk] * cnt)),
        compiler_params=_params(("parallel",)))(cidx, *gs, *bufs))


def add_chips(sums, bufs, sidx, *, name):
    cnt = len(sums)
    _, kh, n = sums[0].shape

    def body(s_ref, *refs):
        mine, b, o = refs[:cnt], refs[cnt:2 * cnt], refs[2 * cnt:]
        for i in range(cnt):
            o[i][...] = ((mine[i][...].astype(F32) + b[i][0].astype(F32)) + (b[i][1].astype(F32) + b[i][2].astype(F32)))

    own = pl.BlockSpec((None, kh, n), lambda i, s: (s[0], 0, 0))
    got = pl.BlockSpec((3, kh, n), lambda i, s: (0, 0, 0))
    out = pl.BlockSpec((kh, n), lambda i, s: (0, 0))
    return list(pl.pallas_call(
        body, name=name, out_shape=tuple(_sds((kh, n), F32) for _ in sums),
        grid_spec=pltpu.PrefetchScalarGridSpec(
            num_scalar_prefetch=1, grid=(1,), in_specs=[own] * cnt + [got] * cnt, out_specs=tuple([out] * cnt)),
        compiler_params=_params(("arbitrary",)))(sidx, *sums, *bufs))


PARAMS = ("ffn1_norm", "ffn1_w_gate", "ffn1_w_up", "ffn1_w_down", "mix_norm", "w_in", "b_gate", "na_q_norm", "na_k_norm",
          "na_rpb", "sw_q_norm", "sw_k_norm", "sw_sink", "t5_rel_table", "w_branch_na", "w_branch_sw", "w_out", "ffn2_norm",
          "ffn2_w_gate", "ffn2_w_up", "ffn2_w_down")
SMALL_ALL = tuple(n for n in PARAMS if n not in BIG)
TRANSPOSED = ("ffn1_w_gate", "ffn1_w_up", "w_in", "ffn2_w_gate", "ffn2_w_up")
SMALL_ROWS = 152


def _pack_small(vals):
    flat = jnp.concatenate([vals[n].reshape(-1).astype(F32) for n in SMALL_ALL] + [vals["loss"].reshape(-1)])
    return jnp.pad(flat, (0, SMALL_ROWS * 128 - flat.shape[0])).reshape(SMALL_ROWS, 128)


def _unpack_small(packed, like):
    flat, out, off = packed.reshape(-1), {}, 0
    for n in SMALL_ALL:
        size = math.prod(like[n].shape)
        out[n] = flat[off:off + size].reshape(like[n].shape)
        off += size
    out["loss"] = flat[off]
    return out


def kernel(x, ffn1_norm, ffn1_w_gate, ffn1_w_up, ffn1_w_down, mix_norm, w_in, b_gate, na_q_norm, na_k_norm, na_rpb, sw_q_norm, sw_k_norm, sw_sink, t5_rel_table, w_branch_na, w_branch_sw, w_out, ffn2_norm, ffn2_w_gate, ffn2_w_up, ffn2_w_down, loss_target, m_ffn1_norm, m_ffn1_w_gate, m_ffn1_w_up, m_ffn1_w_down, m_mix_norm, m_w_in, m_b_gate, m_na_q_norm, m_na_k_norm, m_na_rpb, m_sw_q_norm, m_sw_k_norm, m_sw_sink, m_t5_rel_table, m_w_branch_na, m_w_branch_sw, m_w_out, m_ffn2_norm, m_ffn2_w_gate, m_ffn2_w_up, m_ffn2_w_down, v_ffn1_norm, v_ffn1_w_gate, v_ffn1_w_up, v_ffn1_w_down, v_mix_norm, v_w_in, v_b_gate, v_na_q_norm, v_na_k_norm, v_na_rpb, v_sw_q_norm, v_sw_k_norm, v_sw_sink, v_t5_rel_table, v_w_branch_na, v_w_branch_sw, v_w_out, v_ffn2_norm, v_ffn2_w_gate, v_ffn2_w_up, v_ffn2_w_down):
    args = locals()
    tr = lambda n, a: jnp.transpose(a, (0, 2, 1)) if n in TRANSPOSED else a
    w = {n: tr(n, args[n]) for n in PARAMS}
    m = {n: tr(n, args["m_" + n]) for n in PARAMS}
    v = {n: tr(n, args["v_" + n]) for n in PARAMS}
    cidx = lax.axis_index("c").astype(jnp.int32).reshape(1)
    sidx = (2 * lax.axis_index("x") + lax.axis_index("y")).astype(jnp.int32).reshape(1)

    small = [{n: w[n][l] for n in SMALL} for l in range(DEPTH)]
    order = ("ffn1", "mix", "ffn2")

    keys = [(l, g) for l in range(DEPTH) for g in order]
    local = lambda l, g: [w[n][l].astype(BF16) for n in GROUPS[g]]
    first = gather_start([local(*keys[0])], name="gather_start")
    rest = gather_start([local(*key) for key in keys[1:]], first[0][2][0], name="gather_start")
    in_flight = dict(zip(keys, first + rest))
    t5b = t5_bias(w["t5_rel_table"], name="t5_bias")
    for l in range(DEPTH):
        small[l]["na_bias"] = na_bias_table(small[l]["na_rpb"], name="na_bias_table")
    early = [t5b] + [small[l]["na_bias"] for l in range(DEPTH)] + [rest[0][2][0]]

    def weights_of(l):
        def get(group, after):
            send_sems, recv_sems, thru, lands = in_flight[(l, group)]
            after = [after] + (early if (l, group) == keys[0] else [])
            thru, lands = gather_wait(send_sems, recv_sems, thru, lands, after, name="gather_wait")
            return dict(zip(GROUPS[group], gather_finish(thru, lands, name="gather_finish")))
        return get

    h0, saved0 = layer_fwd(x[0], small[0], weights_of(0), t5b)
    h1, saved1 = layer_fwd(h0, small[1], weights_of(1), t5b)
    dy, loss_row = loss_head(h1, loss_target[0], name="loss_head")

    crossing, tokens, pending = {}, [], []

    def ship(after):
        key, send_sems, recv_sems, grads, lands = pending.pop()
        grads, from_sibling = pair_exchange_wait(send_sems, recv_sems, grads, lands, after, name="pair_exchange_wait")
        sums = _per_shape(lambda gs, bs: add_halves(gs, bs, cidx, name="add_halves"), grads, from_sibling)
        send_sems, recv_sems, sums, lands, token = chip_exchange_start(sums, name="chip_exchange_start")
        crossing[key] = (send_sems, recv_sems, sums, lands)
        return token

    def reduce_of(l):
        def emit(group, grads):
            grads = list(grads)
            shipped = ship([grads[0]]) if pending else None
            send_sems, recv_sems, grads, lands, token = pair_exchange_start(grads, shipped, name="pair_exchange_start")
            pending.append(((l, group), send_sems, recv_sems, grads, lands))
            tokens.append(token)
            return token
        return emit

    def arrived(l, after):
        halves = {}
        for group in order:
            send_sems, recv_sems, sums, lands = crossing[(l, group)]
            sums, got = chip_exchange_wait(send_sems, recv_sems, sums, lands, after, name="chip_exchange_wait")
            halves.update(zip(GROUPS[group], _per_shape(lambda ss, bs: add_chips(ss, bs, sidx, name="add_chips"), sums, got)))
        return [halves[n] for n in BIG]

    def update(layer, mine, theirs, filled=None):
        out = [None] * len(BIG)
        for group in order:
            idx = [BIG.index(n) for n in GROUPS[group]]
            pick = lambda seq: [seq[i] for i in idx]
            res = _per_shape(
                lambda ws, ms, vs, a, b, *f: adamw_layer(ws, ms, vs, a, b, cidx, layer, list(f[0]) if f else None, name="adamw_layer"),
                *([[w[n] for n in GROUPS[group]], [m[n] for n in GROUPS[group]], [v[n] for n in GROUPS[group]], pick(mine),
                   pick(theirs)] + ([pick(filled)] if filled is not None else [])))
            for i, r in zip(idx, res):
                out[i] = r
        return out

    dy, small1, dt5_1 = layer_bwd(dy, saved1, small[1], t5b, reduce_of(1))
    grad_x, small0, dt5_0 = layer_bwd(dy, saved0, small[0], t5b, reduce_of(0), dep=tokens[-1])
    halves1 = arrived(1, [ship([grad_x])])
    theirs1 = pair_send(halves1, name="pair_send")
    done1 = update(1, halves1, theirs1)

    smalls = [small0, small1]
    dt5 = t5_table_grad(dt5_0, dt5_1, name="t5_table_grad").reshape(32, 8)
    local_small = {n: jnp.stack([smalls[l][n].reshape(w[n].shape[1:]) for l in range(DEPTH)]) for n in SMALL}
    local_small["t5_rel_table"] = dt5
    local_small["loss"] = loss_row[0, 0:1]
    total = allreduce_small(_pack_small(local_small), name="allreduce_small")
    small_grads = _unpack_small(total, w)
    pack = lambda d: _pack_small({**d, "loss": jnp.zeros((1,), F32)})[None]
    ds, ms, vs = adamw(pack(w), total[None], pack(m), pack(v), name="adamw_small")

    halves0 = arrived(0, [ds, done1[-1][0]])
    theirs0 = pair_send(halves0, name="pair_send")
    grad, delta, new_m, new_v = {}, {}, {}, {}
    for n, done in zip(BIG, update(0, halves0, theirs0, filled=done1)):
        grad[n], delta[n], new_m[n], new_v[n] = done
    for n in SMALL_ALL:
        grad[n] = small_grads[n]
    d_s, m_s, v_s = _unpack_small(ds[0], w), _unpack_small(ms[0], w), _unpack_small(vs[0], w)
    for n in SMALL_ALL:
        delta[n], new_m[n], new_v[n] = d_s[n], m_s[n], v_s[n]

    return (small_grads["loss"], grad_x[None], *[tr(n, grad[n]) for n in PARAMS], *[tr(n, delta[n]) for n in PARAMS],
            *[tr(n, new_m[n]) for n in PARAMS], *[tr(n, new_v[n]) for n in PARAMS])
```

```python
import functools
import math

import jax
import jax.numpy as jnp
import numpy as np
from jax import lax
from jax.experimental import pallas as pl
from jax.experimental.pallas import tpu as pltpu

F32 = jnp.float32
BF16 = jnp.bfloat16

SEQ = 2048
DM = 1024
DFF = 2816
DEPTH = 2
NSH = 4
FSH = DFF // NSH
GRID_W = 64
ROWS = SEQ // GRID_W
NA_HEADS = 8
HD = 64
NA_WR = 8
NA_WC = 16
NA_KEYS = NA_WR * GRID_W
SW_BLK = 128
SW_NB = SEQ // SW_BLK
SW_KEYS = 3 * SW_BLK
ATT_W = 2304
GATE_W = 2048
IN_W = ATT_W + GATE_W
EPS = 1e-6
NEG = -1e30
QK_SCALE = 1.0 / math.sqrt(HD)

ADAM_LR = 0.001
ADAM_B1 = 0.9
ADAM_B2 = 0.999
ADAM_EPS = 1e-08
ADAM_WD = 0.01
ADAM_STEP = 10

VMEM_LIMIT = 56 << 20
MESH = pl.DeviceIdType.MESH

NT = (((1,), (1,)), ((), ()))
TN = (((0,), (0,)), ((), ()))
NN = (((1,), (0,)), ((), ()))


def _dot(a, b, dims=NN):
    return lax.dot_general(a, b, dims, preferred_element_type=F32)


def _params(sem=None):
    return pltpu.CompilerParams(dimension_semantics=sem, vmem_limit_bytes=VMEM_LIMIT)


def _sds(shape, dtype):
    return jax.ShapeDtypeStruct(shape, dtype)


def mm(a, b, *, name, ta=False, tb=False, out_dtype=F32, add=None, scale=None, tm=512, tn=None, tk=None, exact=False,
       dep=None):
    m, kd = (a.shape[1], a.shape[0]) if ta else a.shape
    n = b.shape[0] if tb else b.shape[1]
    tm, tn, tk = min(tm, m), min(tn or n, n), min(tk or kd, kd)
    nk = kd // tk
    dims = (((0 if ta else 1,), (1 if tb else 0,)), ((), ()))

    def body(*refs):
        a_ref, b_ref = refs[:2]
        add_ref = refs[2] if add is not None else None
        o_ref, acc = refs[-2:]
        k = pl.program_id(2)

        @pl.when(k == 0)
        def _():
            acc[...] = jnp.zeros_like(acc)

        if exact:
            acc[...] += lax.dot_general(a_ref[...], b_ref[...], dims, precision=lax.Precision.HIGHEST,
                                        preferred_element_type=F32)
        else:
            acc[...] += lax.dot_general(a_ref[...].astype(BF16), b_ref[...].astype(BF16), dims,
                                        preferred_element_type=F32)

        @pl.when(k == nk - 1)
        def _():
            r = acc[...]
            if scale is not None:
                r = r * scale
            if add is not None:
                r = r + add_ref[...]
            o_ref[...] = r.astype(out_dtype)

    a_spec = pl.BlockSpec((tk, tm), lambda i, j, k: (k, i)) if ta else pl.BlockSpec((tm, tk), lambda i, j, k: (i, k))
    b_spec = pl.BlockSpec((tn, tk), lambda i, j, k: (j, k)) if tb else pl.BlockSpec((tk, tn), lambda i, j, k: (k, j))
    o_spec = pl.BlockSpec((tm, tn), lambda i, j, k: (i, j))
    ins, specs = [a, b], [a_spec, b_spec]
    if add is not None:
        ins.append(add)
        specs.append(o_spec)
    if dep is not None:
        ins.append(dep)
        specs.append(pl.BlockSpec(memory_space=pl.ANY))
    return pl.pallas_call(
        body, name=name, out_shape=_sds((m, n), out_dtype), grid=(m // tm, n // tn, nk), in_specs=specs,
        out_specs=o_spec, scratch_shapes=[pltpu.VMEM((tm, tn), F32)],
        compiler_params=_params(("parallel", "parallel", "arbitrary")))(*ins)


def _rms(x):
    return lax.rsqrt(jnp.mean(x * x, axis=-1, keepdims=True) + EPS)


def rms_fwd(x, gain, *, name, tm=512):
    def body(x_ref, g_ref, h_ref):
        x = x_ref[...]
        h_ref[...] = (x * _rms(x) * g_ref[...]).astype(BF16)

    return pl.pallas_call(
        body, name=name, out_shape=_sds(x.shape, BF16), grid=(x.shape[0] // tm,),
        in_specs=[pl.BlockSpec((tm, DM), lambda i: (i, 0)), pl.BlockSpec((1, DM), lambda i: (0, 0))],
        out_specs=pl.BlockSpec((tm, DM), lambda i: (i, 0)), compiler_params=_params(("parallel",)))(x, gain)


def _rms_bwd_math(dh, x, gain):
    r = _rms(x)
    xh = x * r
    dgain = jnp.sum(dh * xh, axis=0, keepdims=True)
    dxn = dh * gain
    dx = r * (dxn - xh * jnp.mean(dxn * xh, axis=-1, keepdims=True))
    return dx, dgain


def rms_bwd(dh, x, gain, dres, *, name, tm=512):
    def body(dh_ref, x_ref, g_ref, dres_ref, dx_ref, dxb_ref, dg_ref):
        @pl.when(pl.program_id(0) == 0)
        def _():
            dg_ref[...] = jnp.zeros_like(dg_ref)

        dx, dg = _rms_bwd_math(dh_ref[...], x_ref[...], g_ref[...])
        dx = dres_ref[...] + dx
        dx_ref[...] = dx
        dxb_ref[...] = dx.astype(BF16)
        dg_ref[...] += dg

    tile = pl.BlockSpec((tm, DM), lambda i: (i, 0))
    vec = pl.BlockSpec((1, DM), lambda i: (0, 0))
    return pl.pallas_call(
        body, name=name, out_shape=(_sds(x.shape, F32), _sds(x.shape, BF16), _sds((1, DM), F32)), grid=(x.shape[0] // tm,),
        in_specs=[tile, tile, vec, tile], out_specs=(tile, tile, vec), compiler_params=_params(("arbitrary",)))(dh, x, gain, dres)


def _with_dep(ins, specs, dep):
    if dep is None:
        return ins, specs
    return ins + [dep], specs + [pl.BlockSpec(memory_space=pl.ANY)]


def _resident_weight():
    return pl.BlockSpec((DFF, DM), lambda i: (0, 0), pipeline_mode=pl.Buffered(1))


def ffn_fwd(x, gain, wg, wu, wd, *, name, tm=512):
    def body(x_ref, g_ref, wg_ref, wu_ref, wd_ref, y_ref, h_ref, gg_ref, uu_ref):
        x = x_ref[...]
        h = (x * _rms(x) * g_ref[...]).astype(BF16)
        h_ref[...] = h
        gg = _dot(h, wg_ref[...], NT)
        uu = _dot(h, wu_ref[...], NT)
        gg_ref[...] = gg.astype(BF16)
        uu_ref[...] = uu.astype(BF16)
        act = (gg * jax.nn.sigmoid(gg) * uu).astype(BF16)
        y_ref[...] = x + 0.5 * _dot(act, wd_ref[...])

    s = x.shape[0]
    tile = pl.BlockSpec((tm, DM), lambda i: (i, 0))
    hid = pl.BlockSpec((tm, DFF), lambda i: (i, 0))
    w = _resident_weight()
    return pl.pallas_call(
        body, name=name,
        out_shape=(_sds((s, DM), F32), _sds((s, DM), BF16), _sds((s, DFF), BF16), _sds((s, DFF), BF16)),
        grid=(s // tm,), in_specs=[tile, pl.BlockSpec((1, DM), lambda i: (0, 0)), w, w, w],
        out_specs=(tile, tile, hid, hid), compiler_params=_params(("parallel",)))(x, gain, wg, wu, wd)


def ffn_bwd_tokens(dy, x, gain, gg, uu, wg, wu, wd, *, name, tm=256, dep=None):
    def body(dy_ref, x_ref, g_ref, gg_ref, uu_ref, wg_ref, wu_ref, wd_ref, *rest):
        dx_ref, dxb_ref, dgain_ref, act_ref, dg_ref, du_ref = rest[-6:]

        @pl.when(pl.program_id(0) == 0)
        def _():
            dgain_ref[...] = jnp.zeros_like(dgain_ref)

        dy = dy_ref[...]
        dact = _dot((0.5 * dy).astype(BF16), wd_ref[...], NT)
        g = gg_ref[...].astype(F32)
        u = uu_ref[...].astype(F32)
        sg = jax.nn.sigmoid(g)
        silu = g * sg
        act_ref[...] = (silu * u).astype(BF16)
        dg = (dact * u * (sg * (1.0 + g * (1.0 - sg)))).astype(BF16)
        du = (dact * silu).astype(BF16)
        dg_ref[...] = dg
        du_ref[...] = du
        dx, dgain = _rms_bwd_math(_dot(dg, wg_ref[...]) + _dot(du, wu_ref[...]), x_ref[...], g_ref[...])
        dx = dy + dx
        dx_ref[...] = dx
        dxb_ref[...] = dx.astype(BF16)
        dgain_ref[...] += dgain

    s = x.shape[0]
    tile = pl.BlockSpec((tm, DM), lambda i: (i, 0))
    vec = pl.BlockSpec((1, DM), lambda i: (0, 0))
    hid = pl.BlockSpec((tm, DFF), lambda i: (i, 0))
    hshape = _sds((s, DFF), BF16)
    w = _resident_weight()
    ins, specs = _with_dep([dy, x, gain, gg, uu, wg, wu, wd], [tile, tile, vec, hid, hid, w, w, w], dep)
    return pl.pallas_call(
        body, name=name, out_shape=(_sds((s, DM), F32), _sds((s, DM), BF16), _sds((1, DM), F32), hshape, hshape, hshape),
        grid=(s // tm,), in_specs=specs, out_specs=(tile, tile, vec, hid, hid, hid),
        compiler_params=_params(("arbitrary",)))(*ins)


def ffn_bwd_weights(h, dy, act, dg, du, *, name, tf=256):
    def body(h_ref, dy_ref, act_ref, dg_ref, du_ref, gwg_ref, gwu_ref, gwd_ref):
        h = h_ref[...]
        gwg_ref[...] = _dot(dg_ref[...], h, TN).astype(BF16)
        gwu_ref[...] = _dot(du_ref[...], h, TN).astype(BF16)
        gwd_ref[...] = (0.5 * _dot(act_ref[...], dy_ref[...], TN)).astype(BF16)

    s = h.shape[0]
    full = pl.BlockSpec((s, DM), lambda f: (0, 0))
    hid = pl.BlockSpec((s, tf), lambda f: (0, f))
    wt = pl.BlockSpec((tf, DM), lambda f: (f, 0))
    wshape = _sds((DFF, DM), BF16)
    return pl.pallas_call(
        body, name=name, out_shape=(wshape, wshape, wshape), grid=(DFF // tf,), in_specs=[full, full, hid, hid, hid],
        out_specs=(wt, wt, wt), compiler_params=_params(("parallel",)))(h, dy, act, dg, du)


def _group_mean(v, bd):
    hi = v.astype(BF16)
    lo = (v - hi.astype(F32)).astype(BF16)
    return _dot(hi, bd) + _dot(lo, bd)


def _block_diag(width):
    idx = np.arange(width) // HD
    return jnp.asarray((idx[:, None] == idx[None, :]).astype(np.float32) / HD, dtype=BF16)


def qknorm_fwd(z, gq_na, gk_na, gq_sw, gk_sw, *, name, tm=256):
    def body(zq_ref, zk_ref, zv_ref, zs_ref, zkv_ref, gqa_ref, gka_ref, gqs_ref, gks_ref, bd_ref, bd2_ref,
             qa_ref, ka_ref, va_ref, qs_ref, kv_ref):
        bd = bd_ref[...]

        def norm(x, g, bdm):
            return x * lax.rsqrt(_group_mean(x * x, bdm) + EPS) * g

        qa_ref[...] = (norm(zq_ref[...], gqa_ref[...], bd) * QK_SCALE).astype(BF16)
        ka_ref[...] = norm(zk_ref[...], gka_ref[...], bd).astype(BF16)
        va_ref[...] = zv_ref[...].astype(BF16)
        qs_ref[...] = (norm(zs_ref[...], gqs_ref[...], bd) * QK_SCALE).astype(BF16)
        kv = zkv_ref[...]
        kv_ref[:, 0:128] = norm(kv[:, 0:128], gks_ref[...], bd2_ref[...]).astype(BF16)
        kv_ref[:, 128:256] = kv[:, 128:256].astype(BF16)

    s = z.shape[0]
    col = lambda j: pl.BlockSpec((tm, 512), lambda i, j=j: (i, j))
    vec = lambda w: pl.BlockSpec((1, w), lambda i: (0, 0))
    o512 = pl.BlockSpec((tm, 512), lambda i: (i, 0))
    g512 = lambda g: jnp.tile(g.reshape(1, HD), (1, 8))
    return pl.pallas_call(
        body, name=name,
        out_shape=(_sds((s, 512), BF16),) * 4 + (_sds((s, 256), BF16),), grid=(s // tm,),
        in_specs=[col(0), col(1), col(2), col(3), pl.BlockSpec((tm, 256), lambda i: (i, 8)), vec(512), vec(512), vec(512),
                  vec(128), pl.BlockSpec((512, 512), lambda i: (0, 0)), pl.BlockSpec((128, 128), lambda i: (0, 0))],
        out_specs=(o512, o512, o512, o512, pl.BlockSpec((tm, 256), lambda i: (i, 0))),
        compiler_params=_params(("parallel",)))(
            z, z, z, z, z, g512(gq_na), g512(gk_na), g512(gq_sw), jnp.tile(gk_sw.reshape(1, HD), (1, 2)),
            _block_diag(512), _block_diag(128))


def qknorm_bwd(z, dqa, dka, dva, dqs, dkv, gq_na, gk_na, gq_sw, gk_sw, *, name, tm=256):
    def body(zq_ref, zk_ref, zs_ref, zkv_ref, dqa_ref, dka_ref, dva_ref, dqs_ref, dkv_ref, gqa_ref, gka_ref, gqs_ref,
             gks_ref, bd_ref, bd2_ref, dz_ref, dgqa_ref, dgka_ref, dgqs_ref, dgks_ref):
        @pl.when(pl.program_id(0) == 0)
        def _():
            dgqa_ref[...] = jnp.zeros_like(dgqa_ref)
            dgka_ref[...] = jnp.zeros_like(dgka_ref)
            dgqs_ref[...] = jnp.zeros_like(dgqs_ref)
            dgks_ref[...] = jnp.zeros_like(dgks_ref)

        bd = bd_ref[...]

        def bwd(x, dy, g, bdm, dg_ref):
            r = lax.rsqrt(_group_mean(x * x, bdm) + EPS)
            xh = x * r
            dg_ref[...] += jnp.sum(dy * xh, axis=0, keepdims=True)
            dxn = dy * g
            return r * (dxn - xh * _group_mean(dxn * xh, bdm))

        dz_ref[:, 0:512] = bwd(zq_ref[...], dqa_ref[...] * QK_SCALE, gqa_ref[...], bd, dgqa_ref).astype(BF16)
        dz_ref[:, 512:1024] = bwd(zk_ref[...], dka_ref[...], gka_ref[...], bd, dgka_ref).astype(BF16)
        dz_ref[:, 1024:1536] = dva_ref[...].astype(BF16)
        dz_ref[:, 1536:2048] = bwd(zs_ref[...], dqs_ref[...] * QK_SCALE, gqs_ref[...], bd, dgqs_ref).astype(BF16)
        dkv = dkv_ref[...]
        dz_ref[:, 2048:2176] = bwd(zkv_ref[:, 0:128], dkv[:, 0:128], gks_ref[...], bd2_ref[...], dgks_ref).astype(BF16)
        dz_ref[:, 2176:2304] = dkv[:, 128:256].astype(BF16)

    s = z.shape[0]
    col = lambda j: pl.BlockSpec((tm, 512), lambda i, j=j: (i, j))
    t512 = pl.BlockSpec((tm, 512), lambda i: (i, 0))
    t256 = pl.BlockSpec((tm, 256), lambda i: (i, 0))
    vec = lambda w: pl.BlockSpec((1, w), lambda i: (0, 0))
    g512 = lambda g: jnp.tile(g.reshape(1, HD), (1, 8))
    return pl.pallas_call(
        body, name=name,
        out_shape=(_sds((s, ATT_W), BF16), _sds((1, 512), F32), _sds((1, 512), F32), _sds((1, 512), F32), _sds((1, 128), F32)),
        grid=(s // tm,),
        in_specs=[col(0), col(1), col(3), pl.BlockSpec((tm, 256), lambda i: (i, 8)), t512, t512, t512, t512, t256,
                  vec(512), vec(512), vec(512), vec(128), pl.BlockSpec((512, 512), lambda i: (0, 0)),
                  pl.BlockSpec((128, 128), lambda i: (0, 0))],
        out_specs=(pl.BlockSpec((tm, ATT_W), lambda i: (i, 0)), vec(512), vec(512), vec(512), vec(128)),
        compiler_params=_params(("arbitrary",)))(
            z, z, z, z, dqa, dka, dva, dqs, dkv, g512(gq_na), g512(gk_na), g512(gq_sw),
            jnp.tile(gk_sw.reshape(1, HD), (1, 2)), _block_diag(512), _block_diag(128))


def _na_row_start(r):
    return jnp.clip(r - NA_WR // 2, 0, ROWS - NA_WR)


def na_bias_table(rpb, *, name):
    t = jnp.pad(rpb, ((0, 0), (0, 2), (0, HD - (2 * NA_WC - 1))))
    pairs = jnp.concatenate([t[:, :16], t[:, 1:17]], axis=-1).reshape(NA_HEADS, 16, 1, 128)

    def body(t_ref, o_ref):
        p = pl.program_id(0)
        q = lax.broadcasted_iota(jnp.int32, (GRID_W, 128), 0)
        kc = lax.broadcasted_iota(jnp.int32, (GRID_W, 128), 1) & (GRID_W - 1)
        cs = jnp.clip(q - NA_WC // 2, 0, GRID_W - NA_WC)
        ok = (kc >= cs) & (kc < cs + NA_WC)
        for h in range(NA_HEADS):
            for pr in range(NA_WR // 2):
                x = jnp.broadcast_to(t_ref[h, 2 * pr - p + NA_WR - 1], (GRID_W, 128))
                b = pltpu.roll(x, 128 - (NA_WC - 1), 1, stride=1, stride_axis=0)
                o_ref[h, :, 128 * pr:128 * pr + 128] = jnp.where(ok, b, NEG)

    return pl.pallas_call(
        body, name=name, out_shape=_sds((NA_WR, NA_HEADS, GRID_W, NA_KEYS), F32), grid=(NA_WR,),
        in_specs=[pl.BlockSpec((NA_HEADS, 16, 1, 128), lambda p: (0, 0, 0, 0))],
        out_specs=pl.BlockSpec((None, NA_HEADS, GRID_W, NA_KEYS), lambda p: (p, 0, 0, 0)),
        compiler_params=_params(("parallel",)))(pairs)


def _lane_halves():
    lane = lax.broadcasted_iota(jnp.int32, (1, 128), 1)
    return lane < HD


def na_fwd(q, k, v, bias, *, name):
    def body(q_ref, k_ref, v_ref, b_ref, o_ref, lse_ref):
        r = pl.program_id(0)
        off = pl.multiple_of(_na_row_start(r) * GRID_W, GRID_W)
        first = _lane_halves()
        sels = [first, jnp.logical_not(first)]
        lanes = [slice(128 * j, 128 * j + 128) for j in range(NA_HEADS // 2)]
        q2s = [q_ref[:, l] for l in lanes]
        k2s = [k_ref[pl.ds(off, NA_KEYS), l] for l in lanes]
        v2s = [v_ref[pl.ds(off, NA_KEYS), l] for l in lanes]
        scores = []
        for h in range(NA_HEADS):
            j, half = divmod(h, 2)
            scores.append(_dot(jnp.where(sels[half], q2s[j], jnp.zeros_like(q2s[j])), k2s[j], NT))
        probs, lses = [], []
        for h in range(NA_HEADS):
            b = b_ref[h]
            s = jnp.where(b > 0.5 * NEG, scores[h] + b, NEG)
            m = jnp.max(s, axis=-1, keepdims=True)
            e = jnp.exp(s - m)
            l = jnp.sum(e, axis=-1, keepdims=True)
            probs.append((e / l).astype(BF16))
            lses.append(m + jnp.log(l))
        for j in range(NA_HEADS // 2):
            zero = jnp.zeros_like(v2s[j])
            o2 = (_dot(probs[2 * j], jnp.where(sels[0], v2s[j], zero))
                  + _dot(probs[2 * j + 1], jnp.where(sels[1], v2s[j], zero)))
            o_ref[:, lanes[j]] = o2.astype(BF16)
        lse_ref[...] = jnp.concatenate(lses, axis=1)

    s_tok = q.shape[0]
    full = pl.BlockSpec((s_tok, 512), lambda r: (0, 0))
    return pl.pallas_call(
        body, name=name, out_shape=(_sds((s_tok, 512), BF16), _sds((s_tok, NA_HEADS), F32)), grid=(ROWS,),
        in_specs=[pl.BlockSpec((GRID_W, 512), lambda r: (r, 0)), full, full,
                  pl.BlockSpec((None, NA_HEADS, GRID_W, NA_KEYS), lambda r: (r - _na_row_start(r), 0, 0, 0))],
        out_specs=(pl.BlockSpec((GRID_W, 512), lambda r: (r, 0)), pl.BlockSpec((GRID_W, NA_HEADS), lambda r: (r, 0))),
        compiler_params=_params(("parallel",)))(q, k, v, bias)


def na_bwd(q, k, v, o, do, lse, bias, *, name):
    def body(q_ref, k_ref, v_ref, o_ref, do_ref, lse_ref, b_ref, dq_ref, dk_ref, dv_ref, db_ref):
        r = pl.program_id(0)

        @pl.when(r == 0)
        def _():
            dk_ref[...] = jnp.zeros_like(dk_ref)
            dv_ref[...] = jnp.zeros_like(dv_ref)

        @pl.when((r <= NA_WR // 2) | (r > ROWS - NA_WR // 2))
        def _():
            db_ref[...] = jnp.zeros_like(db_ref)

        off = pl.multiple_of(_na_row_start(r) * GRID_W, GRID_W)
        first = _lane_halves()
        sels = [first, jnp.logical_not(first)]
        lanes = [slice(128 * j, 128 * j + 128) for j in range(NA_HEADS // 2)]
        q2s = [q_ref[:, l] for l in lanes]
        k2s = [k_ref[pl.ds(off, NA_KEYS), l] for l in lanes]
        v2s = [v_ref[pl.ds(off, NA_KEYS), l] for l in lanes]
        do2s = [do_ref[:, l] for l in lanes]
        prods = [do2s[j].astype(F32) * o_ref[:, lanes[j]].astype(F32) for j in range(NA_HEADS // 2)]
        lse = lse_ref[...]
        qhs, dohs, scores, dps = [], [], [], []
        for h in range(NA_HEADS):
            j, half = divmod(h, 2)
            qhs.append(jnp.where(sels[half], q2s[j], jnp.zeros_like(q2s[j])))
            dohs.append(jnp.where(sels[half], do2s[j], jnp.zeros_like(do2s[j])))
            scores.append(_dot(qhs[h], k2s[j], NT))
            dps.append(_dot(dohs[h], v2s[j], NT))
        pbs, dsbs = [], []
        for h in range(NA_HEADS):
            j, half = divmod(h, 2)
            b = b_ref[h]
            s = jnp.where(b > 0.5 * NEG, scores[h] + b, NEG)
            p = jnp.exp(s - lse[:, h:h + 1])
            delta = jnp.sum(jnp.where(sels[half], prods[j], 0.0), axis=-1, keepdims=True)
            ds = p * (dps[h] - delta)
            db_ref[h] += ds
            pbs.append(p.astype(BF16))
            dsbs.append(ds.astype(BF16))
        for j in range(NA_HEADS // 2):
            a, b = 2 * j, 2 * j + 1
            zero = jnp.zeros_like(k2s[j])
            dq_ref[:, lanes[j]] = (_dot(dsbs[a], jnp.where(sels[0], k2s[j], zero))
                                   + _dot(dsbs[b], jnp.where(sels[1], k2s[j], zero)))
            dk_ref[pl.ds(off, NA_KEYS), lanes[j]] += _dot(dsbs[a], qhs[a], TN) + _dot(dsbs[b], qhs[b], TN)
            dv_ref[pl.ds(off, NA_KEYS), lanes[j]] += _dot(pbs[a], dohs[a], TN) + _dot(pbs[b], dohs[b], TN)

    s_tok = q.shape[0]
    full = pl.BlockSpec((s_tok, 512), lambda r: (0, 0))
    row = pl.BlockSpec((GRID_W, 512), lambda r: (r, 0))
    bias_spec = pl.BlockSpec((None, NA_HEADS, GRID_W, NA_KEYS), lambda r: (r - _na_row_start(r), 0, 0, 0))
    return pl.pallas_call(
        body, name=name,
        out_shape=(_sds((s_tok, 512), F32), _sds((s_tok, 512), F32), _sds((s_tok, 512), F32),
                   _sds((NA_WR, NA_HEADS, GRID_W, NA_KEYS), F32)),
        grid=(ROWS,),
        in_specs=[row, full, full, row, row, pl.BlockSpec((GRID_W, NA_HEADS), lambda r: (r, 0)), bias_spec],
        out_specs=(row, full, full, bias_spec), compiler_params=_params(("arbitrary",)))(q, k, v, o, do, lse, bias)


def t5_bucket_map():
    rel = np.arange(SW_KEYS)[None, :] - SW_BLK - np.arange(SW_BLK)[:, None]
    nb = 16
    max_exact = nb // 2
    n = np.abs(rel)
    large = max_exact + (np.log(np.maximum(n, 1) / max_exact) / np.log(128 / max_exact) * (nb - max_exact)).astype(np.int32)
    large = np.minimum(large, nb - 1)
    return ((rel > 0) * nb + np.where(n < max_exact, n, large)).astype(np.int32)


def t5_bias(table, *, name):
    rel = np.arange(-SW_BLK, SW_BLK + 1)
    nb, max_exact = 16, 8
    n = np.abs(rel)
    large = max_exact + (np.log(np.maximum(n, 1) / max_exact) / np.log(128 / max_exact) * (nb - max_exact)).astype(np.int32)
    bucket = ((rel > 0) * nb + np.where(n < max_exact, n, np.minimum(large, nb - 1))).astype(np.int32)
    u = jnp.pad(table[jnp.asarray(bucket)].T, ((0, 0), (0, SW_KEYS - bucket.shape[0]))).reshape(8, 1, SW_KEYS)

    def body(u_ref, o_ref):
        for h in range(8):
            x = jnp.broadcast_to(u_ref[h], (SW_BLK, SW_KEYS))
            o_ref[h] = pltpu.roll(x, 0, 1, stride=1, stride_axis=0)

    return pl.pallas_call(body, name=name, out_shape=_sds((8, SW_BLK, SW_KEYS), F32), compiler_params=_params())(u)


def _sw_valid(n):
    a = lax.broadcasted_iota(jnp.int32, (SW_BLK, SW_KEYS), 0)
    j = lax.broadcasted_iota(jnp.int32, (SW_BLK, SW_KEYS), 1)
    kpos = (n - 1) * SW_BLK + j
    return (jnp.abs(j - SW_BLK - a) <= SW_BLK) & (kpos >= 0) & (kpos < SEQ)


def _dup_group(x2, g, first):
    rolled = pltpu.roll(x2, HD, 1)
    return jnp.where(first, x2, rolled) if g == 0 else jnp.where(first, rolled, x2)


def sw_fwd(q, kv, t5, sink, *, name):
    def body(q_ref, kv_ref, t5_ref, sink_ref, o_ref, lse_ref):
        n = pl.program_id(0)
        off = pl.multiple_of(n * SW_BLK, SW_BLK)
        first = _lane_halves()
        sels = [first, jnp.logical_not(first)]
        valid = _sw_valid(n)
        k2 = kv_ref[pl.ds(off, SW_KEYS), 0:128]
        v2 = kv_ref[pl.ds(off, SW_KEYS), 128:256]
        kk = [_dup_group(k2, g, first) for g in range(2)]
        vv = [_dup_group(v2, g, first) for g in range(2)]
        q2s = [q_ref[:, 128 * j:128 * j + 128] for j in range(4)]
        scores = []
        for h in range(8):
            j, half = divmod(h, 2)
            scores.append(_dot(jnp.where(sels[half], q2s[j], jnp.zeros_like(q2s[j])), kk[j // 2], NT))
        probs, lses = [], []
        for h in range(8):
            s = jnp.where(valid, scores[h] + t5_ref[h], NEG)
            snk = sink_ref[h]
            m = jnp.maximum(jnp.max(s, axis=-1, keepdims=True), snk)
            e = jnp.exp(s - m)
            den = jnp.sum(e, axis=-1, keepdims=True) + jnp.exp(snk - m)
            probs.append((e / den).astype(BF16))
            lses.append(m + jnp.log(den))
        outs = []
        for j in range(4):
            vg = vv[j // 2]
            zero = jnp.zeros_like(vg)
            outs.append(_dot(probs[2 * j], jnp.where(sels[0], vg, zero)) + _dot(probs[2 * j + 1], jnp.where(sels[1], vg, zero)))
        o_ref[...] = jnp.concatenate(outs, axis=1).astype(BF16)
        lse_ref[...] = jnp.concatenate(lses, axis=1)

    s_tok = q.shape[0]
    blk = pl.BlockSpec((SW_BLK, 512), lambda n: (n, 0))
    return pl.pallas_call(
        body, name=name, out_shape=(_sds((s_tok, 512), BF16), _sds((s_tok, 8), F32)), grid=(SW_NB,),
        in_specs=[blk, pl.BlockSpec(kv.shape, lambda n: (0, 0)), pl.BlockSpec((8, SW_BLK, SW_KEYS), lambda n: (0, 0, 0)),
                  pl.BlockSpec(memory_space=pltpu.SMEM)],
        out_specs=(blk, pl.BlockSpec((SW_BLK, 8), lambda n: (n, 0))), compiler_params=_params(("parallel",)))(q, kv, t5, sink)


def sw_bwd(q, kv, o, do, lse, t5, sink, *, name):
    def body(q_ref, kv_ref, o_ref, do_ref, lse_ref, t5_ref, sink_ref, dq_ref, dkv_ref, dt5_ref, dsink_ref):
        n = pl.program_id(0)

        @pl.when(n == 0)
        def _():
            dkv_ref[...] = jnp.zeros_like(dkv_ref)
            dt5_ref[...] = jnp.zeros_like(dt5_ref)
            dsink_ref[...] = jnp.zeros_like(dsink_ref)

        off = pl.multiple_of(n * SW_BLK, SW_BLK)
        first = _lane_halves()
        sels = [first, jnp.logical_not(first)]
        valid = _sw_valid(n)
        k2 = kv_ref[pl.ds(off, SW_KEYS), 0:128]
        v2 = kv_ref[pl.ds(off, SW_KEYS), 128:256]
        kk = [_dup_group(k2, g, first) for g in range(2)]
        vv = [_dup_group(v2, g, first) for g in range(2)]
        lanes = [slice(128 * j, 128 * j + 128) for j in range(4)]
        q2s = [q_ref[:, l] for l in lanes]
        do2s = [do_ref[:, l] for l in lanes]
        prods = [do2s[j].astype(F32) * o_ref[:, lanes[j]].astype(F32) for j in range(4)]
        lse = lse_ref[...]
        qhs, dohs, scores, dps = [], [], [], []
        for h in range(8):
            j, half = divmod(h, 2)
            qhs.append(jnp.where(sels[half], q2s[j], jnp.zeros_like(q2s[j])))
            dohs.append(jnp.where(sels[half], do2s[j], jnp.zeros_like(do2s[j])))
            scores.append(_dot(qhs[h], kk[j // 2], NT))
            dps.append(_dot(dohs[h], vv[j // 2], NT))
        pbs, dsbs, dss, dsinks = [], [], [], []
        for h in range(8):
            j, half = divmod(h, 2)
            s = jnp.where(valid, scores[h] + t5_ref[h], NEG)
            lse_h = lse[:, h:h + 1]
            p = jnp.exp(s - lse_h)
            delta = jnp.sum(jnp.where(sels[half], prods[j], 0.0), axis=-1, keepdims=True)
            ds = p * (dps[h] - delta)
            dss.append(ds)
            dsinks.append(-jnp.sum(jnp.exp(sink_ref[h] - lse_h) * delta, axis=0, keepdims=True))
            pbs.append(p.astype(BF16))
            dsbs.append(ds.astype(BF16))
        dt5_ref[...] += jnp.stack(dss)
        dsink_ref[...] += jnp.concatenate(dsinks, axis=1)
        dqs = []
        for j in range(4):
            a, b = 2 * j, 2 * j + 1
            zero = jnp.zeros_like(kk[j // 2])
            dqs.append(_dot(dsbs[a], jnp.where(sels[0], kk[j // 2], zero)) + _dot(dsbs[b], jnp.where(sels[1], kk[j // 2], zero)))
        dq_ref[...] = jnp.concatenate(dqs, axis=1)
        dk_groups, dv_groups = [], []
        for g in range(2):
            dkk = sum(_dot(dsbs[h], qhs[h], TN) for h in range(4 * g, 4 * g + 4))
            dvv = sum(_dot(pbs[h], dohs[h], TN) for h in range(4 * g, 4 * g + 4))
            dk_groups.append(dkk + pltpu.roll(dkk, HD, 1))
            dv_groups.append(dvv + pltpu.roll(dvv, HD, 1))
        dkv_ref[pl.ds(off, SW_KEYS), :] += jnp.concatenate(
            [jnp.where(first, dk_groups[0], dk_groups[1]), jnp.where(first, dv_groups[0], dv_groups[1])], axis=1)

    s_tok = q.shape[0]
    blk = pl.BlockSpec((SW_BLK, 512), lambda n: (n, 0))
    kv_spec = pl.BlockSpec(kv.shape, lambda n: (0, 0))
    t5_spec = pl.BlockSpec((8, SW_BLK, SW_KEYS), lambda n: (0, 0, 0))
    vec = pl.BlockSpec((1, 8), lambda n: (0, 0))
    return pl.pallas_call(
        body, name=name,
        out_shape=(_sds((s_tok, 512), F32), _sds(kv.shape, F32), _sds((8, SW_BLK, SW_KEYS), F32), _sds((1, 8), F32)),
        grid=(SW_NB,), in_specs=[blk, kv_spec, blk, blk, pl.BlockSpec((SW_BLK, 8), lambda n: (n, 0)), t5_spec,
                                 pl.BlockSpec(memory_space=pltpu.SMEM)],
        out_specs=(blk, kv_spec, t5_spec, vec), compiler_params=_params(("arbitrary",)))(q, kv, o, do, lse, t5, sink)


def gate_fwd(zg, bias, pa, ps, *, name, tm=512):
    def body(z0_ref, z1_ref, b0_ref, b1_ref, pa_ref, ps_ref, m_ref):
        g0 = jax.nn.sigmoid(z0_ref[...] + b0_ref[...])
        g1 = jax.nn.sigmoid(z1_ref[...] + b1_ref[...])
        m_ref[...] = (g0 * pa_ref[...] + g1 * ps_ref[...]).astype(BF16)

    s = zg.shape[0]
    half = lambda j: pl.BlockSpec((tm, DM), lambda i, j=j: (i, j))
    bvec = lambda j: pl.BlockSpec((1, DM), lambda i, j=j: (0, j))
    return pl.pallas_call(
        body, name=name, out_shape=_sds((s, DM), BF16), grid=(s // tm,),
        in_specs=[half(0), half(1), bvec(0), bvec(1), half(0), half(0)], out_specs=half(0),
        compiler_params=_params(("parallel",)))(zg, zg, bias, bias, pa, ps)


def gate_bwd(dm, zg, bias, pa, ps, *, name, tm=512):
    def body(dm_ref, z0_ref, z1_ref, b0_ref, b1_ref, pa_ref, ps_ref, dpa_ref, dps_ref, dz_ref, db_ref):
        @pl.when(pl.program_id(0) == 0)
        def _():
            db_ref[...] = jnp.zeros_like(db_ref)

        dm = dm_ref[...]
        g0 = jax.nn.sigmoid(z0_ref[...] + b0_ref[...])
        g1 = jax.nn.sigmoid(z1_ref[...] + b1_ref[...])
        dpa_ref[...] = (dm * g0).astype(BF16)
        dps_ref[...] = (dm * g1).astype(BF16)
        dz0 = dm * pa_ref[...] * g0 * (1.0 - g0)
        dz1 = dm * ps_ref[...] * g1 * (1.0 - g1)
        dz_ref[:, 0:DM] = dz0.astype(BF16)
        dz_ref[:, DM:2 * DM] = dz1.astype(BF16)
        db_ref[:, 0:DM] += jnp.sum(dz0, axis=0, keepdims=True)
        db_ref[:, DM:2 * DM] += jnp.sum(dz1, axis=0, keepdims=True)

    s = zg.shape[0]
    half = lambda j: pl.BlockSpec((tm, DM), lambda i, j=j: (i, j))
    bvec = lambda j: pl.BlockSpec((1, DM), lambda i, j=j: (0, j))
    return pl.pallas_call(
        body, name=name,
        out_shape=(_sds((s, DM), BF16), _sds((s, DM), BF16), _sds((s, GATE_W), BF16), _sds((1, GATE_W), F32)),
        grid=(s // tm,), in_specs=[half(0), half(0), half(1), bvec(0), bvec(1), half(0), half(0)],
        out_specs=(half(0), half(0), pl.BlockSpec((tm, GATE_W), lambda i: (i, 0)), pl.BlockSpec((1, GATE_W), lambda i: (0, 0))),
        compiler_params=_params(("arbitrary",)))(dm, zg, zg, bias, bias, pa, ps)


def loss_head(y, target, *, name, tm=512):
    def body(y_ref, t_ref, dy_ref, dyb_ref, l_ref):
        @pl.when(pl.program_id(0) == 0)
        def _():
            l_ref[...] = jnp.zeros_like(l_ref)

        err = y_ref[...] - t_ref[...]
        dy = err * (1.0 / DM)
        dy_ref[...] = dy
        dyb_ref[...] = dy.astype(BF16)
        l_ref[...] += 0.5 * jnp.sum(jnp.mean(err * err, axis=-1, keepdims=True), axis=0, keepdims=True)

    s = y.shape[0]
    tile = pl.BlockSpec((tm, DM), lambda i: (i, 0))
    return pl.pallas_call(
        body, name=name, out_shape=(_sds((s, DM), F32), _sds((s, DM), BF16), _sds((1, 128), F32)), grid=(s // tm,),
        in_specs=[tile, tile], out_specs=(tile, tile, pl.BlockSpec((1, 128), lambda i: (0, 0))),
        compiler_params=_params(("arbitrary",)))(y, target)


def adamw(w, g, m, v, *, name):
    def body(w_ref, g_ref, m_ref, v_ref, d_ref, nm_ref, nv_ref):
        g = g_ref[...]
        nm = ADAM_B1 * m_ref[...] + (1.0 - ADAM_B1) * g
        nv = ADAM_B2 * v_ref[...] + (1.0 - ADAM_B2) * jnp.square(g)
        m_hat = nm / (1.0 - ADAM_B1 ** ADAM_STEP)
        v_hat = nv / (1.0 - ADAM_B2 ** ADAM_STEP)
        d_ref[...] = -ADAM_LR * (m_hat / (jnp.sqrt(v_hat) + ADAM_EPS) + ADAM_WD * w_ref[...])
        nm_ref[...] = nm
        nv_ref[...] = nv

    b, k, n = w.shape
    tk = k // 4 if k % 32 == 0 else k
    spec = pl.BlockSpec((None, tk, n), lambda i, j: (i, j, 0))
    out = _sds(w.shape, F32)
    return pl.pallas_call(
        body, name=name, out_shape=(out, out, out), grid=(b, k // tk), in_specs=[spec] * 4, out_specs=(spec,) * 3,
        compiler_params=_params(("parallel", "parallel")))(w, g, m, v)


def adamw_layer(ws, ms, vs, mines, theirs, cidx, layer, filled=None, *, name):
    cnt = len(ws)
    _, k, n = ws[0].shape
    nt = 2
    tk = k // 2 // nt

    def body(c_ref, *refs):
        own = pl.program_id(0) == c_ref[0]
        outs = refs[-4 * cnt:]
        for i in range(cnt):
            w_ref, m_ref, v_ref, a_ref, b_ref = refs[5 * i:5 * i + 5]
            g_ref, d_ref, nm_ref, nv_ref = outs[4 * i:4 * i + 4]
            g = jnp.where(own, a_ref[...], b_ref[...])
            g_ref[...] = g
            nm = ADAM_B1 * m_ref[...] + (1.0 - ADAM_B1) * g
            nv = ADAM_B2 * v_ref[...] + (1.0 - ADAM_B2) * jnp.square(g)
            m_hat = nm / (1.0 - ADAM_B1 ** ADAM_STEP)
            v_hat = nv / (1.0 - ADAM_B2 ** ADAM_STEP)
            d_ref[...] = -ADAM_LR * (m_hat / (jnp.sqrt(v_hat) + ADAM_EPS) + ADAM_WD * w_ref[...])
            nm_ref[...] = nm
            nv_ref[...] = nv

    full = pl.BlockSpec((None, tk, n), lambda hf, t, c: (layer, hf * nt + t, 0))
    half_mine = pl.BlockSpec((tk, n), lambda hf, t, c: (jnp.where(hf == c[0], t, 0), 0))
    half_theirs = pl.BlockSpec((tk, n), lambda hf, t, c: (jnp.where(hf != c[0], t, 0), 0))
    out = _sds(ws[0].shape, F32)
    ins, specs, aliases = [cidx], [], {}
    for i in range(cnt):
        ins += [ws[i], ms[i], vs[i], mines[i], theirs[i]]
        specs += [full, full, full, half_mine, half_theirs]
    if filled is not None:
        aliases = {len(ins) + j: j for j in range(4 * cnt)}
        ins += [a for f in filled for a in f]
        specs += [pl.BlockSpec(memory_space=pl.ANY)] * (4 * cnt)
    res = pl.pallas_call(
        body, name=name, out_shape=(out,) * (4 * cnt),
        grid_spec=pltpu.PrefetchScalarGridSpec(
            num_scalar_prefetch=1, grid=(2, nt), in_specs=specs, out_specs=(full,) * (4 * cnt)),
        input_output_aliases=aliases,
        compiler_params=_params(("arbitrary", "arbitrary")))(*ins)
    return [tuple(res[4 * i:4 * i + 4]) for i in range(cnt)]


def t5_table_grad(dt5_a, dt5_b, *, name):
    def body(a_ref, b_ref, map_ref, o_ref):
        d = a_ref[...] + b_ref[...]
        bucket = map_ref[...]
        for b in range(32):
            hit = (bucket == b)[None]
            o_ref[b] = jnp.sum(jnp.sum(jnp.where(hit, d, 0.0), axis=2), axis=1, keepdims=True)

    return pl.pallas_call(
        body, name=name, out_shape=_sds((32, 8, 1), F32), compiler_params=_params())(
            dt5_a, dt5_b, jnp.asarray(t5_bucket_map()))


def rpb_grad(dbias, *, name):
    def body(d_ref, rev_ref, o_ref):
        rev = rev_ref[...]
        for h in range(NA_HEADS):
            for pr in range(NA_WR // 2):
                d = d_ref[h, :, 128 * pr:128 * pr + 128]
                hi = d.astype(BF16)
                lo = (d - hi.astype(F32)).astype(BF16)
                flipped = _dot(rev, hi) + _dot(rev, lo)
                o_ref[h, pr] = jnp.sum(pltpu.roll(flipped, 0, 1, stride=1, stride_axis=0), axis=0, keepdims=True)

    anti = jnp.asarray(np.eye(GRID_W, dtype=np.float32)[::-1], dtype=BF16)
    e = pl.pallas_call(
        body, name=name, out_shape=_sds((NA_WR, NA_HEADS, NA_WR // 2, 1, 128), F32), grid=(NA_WR,),
        in_specs=[pl.BlockSpec((None, NA_HEADS, GRID_W, NA_KEYS), lambda p: (p, 0, 0, 0)),
                  pl.BlockSpec((GRID_W, GRID_W), lambda p: (0, 0))],
        out_specs=pl.BlockSpec((None, NA_HEADS, NA_WR // 2, 1, 128), lambda p: (p, 0, 0, 0, 0)),
        compiler_params=_params(("parallel",)))(dbias, anti)
    nci, nri = 2 * NA_WC - 1, 2 * NA_WR - 1
    e = e.reshape(NA_WR, NA_HEADS, NA_WR // 2, 128).transpose(0, 2, 1, 3).reshape(NA_WR * NA_WR // 2, NA_HEADS, 128)
    parts = jnp.concatenate([e[..., 48:48 + nci], jnp.concatenate([e[..., 112:128], e[..., 0:nci - 16]], axis=-1)], axis=0)
    p, pr = np.arange(NA_WR)[:, None], np.arange(NA_WR // 2)[None, :]
    ri = np.concatenate([(2 * pr - p + NA_WR - 1).reshape(-1), (2 * pr - p + NA_WR).reshape(-1)])
    pick = jnp.asarray((ri[None, :] == np.arange(16)[:, None]).astype(np.float32))
    out = mm(pick, parts.reshape(2 * NA_WR * NA_WR // 2, NA_HEADS * nci), name=name + "_rows", exact=True)
    return out.reshape(16, NA_HEADS, nci)[:nri].transpose(1, 0, 2)


BIG = ("ffn1_w_gate", "ffn1_w_up", "ffn1_w_down", "w_in", "w_branch_na", "w_branch_sw", "w_out",
       "ffn2_w_gate", "ffn2_w_up", "ffn2_w_down")
SMALL = ("ffn1_norm", "mix_norm", "b_gate", "na_q_norm", "na_k_norm", "na_rpb", "sw_q_norm", "sw_k_norm", "sw_sink",
         "ffn2_norm")


def _cols_to_full(w4):
    return w4.transpose(1, 0, 2).reshape(w4.shape[1], NSH * w4.shape[2])


def _full_to_cols(w):
    return w.reshape(w.shape[0], NSH, w.shape[1] // NSH).transpose(1, 0, 2)


def _mixer_weights(g):
    w_in_t = g["w_in"].reshape(IN_W, DM)
    return dict(w_att_t=w_in_t[:ATT_W], w_gz_t=w_in_t[ATT_W:], wa=_cols_to_full(g["w_branch_na"]),
                ws=_cols_to_full(g["w_branch_sw"]), wo=g["w_out"].reshape(DM, DM))


GROUPS = {"ffn1": ("ffn1_w_gate", "ffn1_w_up", "ffn1_w_down"), "mix": ("w_in", "w_branch_na", "w_branch_sw", "w_out"),
          "ffn2": ("ffn2_w_gate", "ffn2_w_up", "ffn2_w_down")}


def layer_fwd(x, p, weights, t5b):
    row = lambda v: v.reshape(1, -1)
    stacked = lambda g: {n: a.reshape(DFF, DM) for n, a in g.items()}
    g1 = stacked(weights("ffn1", x))
    y1, h1, gg1, uu1 = ffn_fwd(x, row(p["ffn1_norm"]), g1["ffn1_w_gate"], g1["ffn1_w_up"], g1["ffn1_w_down"], name="ffn_fwd")
    w = _mixer_weights(weights("mix", y1))
    hm = rms_fwd(y1, row(p["mix_norm"]), name="mix_norm_fwd")
    z = mm(hm, w["w_att_t"], tb=True, name="proj_att", tm=SEQ, tn=768)
    zg = mm(hm, w["w_gz_t"], tb=True, name="proj_gate", tm=SEQ, tn=512)
    qa, ka, va, qs, kv = qknorm_fwd(z, p["na_q_norm"], p["na_k_norm"], p["sw_q_norm"], p["sw_k_norm"], name="qknorm_fwd")
    bias = p["na_bias"]
    o_na, lse_na = na_fwd(qa, ka, va, bias, name="na_fwd")
    kvp = jnp.pad(kv, ((SW_BLK, SW_BLK), (0, 0)))
    sink = p["sw_sink"]
    o_sw, lse_sw = sw_fwd(qs, kvp, t5b, sink, name="sw_fwd")
    pa = mm(o_na, w["wa"], name="branch_na", tm=1024)
    ps = mm(o_sw, w["ws"], name="branch_sw", tm=1024)
    merged = gate_fwd(zg, row(p["b_gate"]), pa, ps, name="gate_fwd")
    y2 = mm(merged, w["wo"], add=y1, name="out_proj", tm=1024)
    g2 = stacked(weights("ffn2", y2))
    y3, h2, gg2, uu2 = ffn_fwd(y2, row(p["ffn2_norm"]), g2["ffn2_w_gate"], g2["ffn2_w_up"], g2["ffn2_w_down"], name="ffn_fwd")
    saved = dict(x=x, y1=y1, h1=h1, gg1=gg1, uu1=uu1, hm=hm, z=z, zg=zg, qa=qa, ka=ka, va=va, qs=qs, kvp=kvp, bias=bias,
                 o_na=o_na, lse_na=lse_na, o_sw=o_sw, lse_sw=lse_sw, pa=pa, ps=ps, merged=merged, y2=y2, h2=h2, gg2=gg2,
                 uu2=uu2, w=w, sink=sink, g1=g1, g2=g2)
    return y3, saved


def layer_bwd(dy3, dy3_bf, sv, p, t5b, emit, dep=None):
    w, g1, g2 = sv["w"], sv["g1"], sv["g2"]
    row = lambda v: v.reshape(1, -1)
    fold = lambda v: v.reshape(-1, HD).sum(axis=0)
    small = {}
    dy2, _, small["ffn2_norm"], act, dg, du = ffn_bwd_tokens(
        dy3, sv["y2"], row(p["ffn2_norm"]), sv["gg2"], sv["uu2"], g2["ffn2_w_gate"], g2["ffn2_w_up"], g2["ffn2_w_down"],
        name="ffn_bwd_tokens", dep=dep)
    shards = lambda gs: [g.reshape(NSH, FSH, DM) for g in gs]
    token = emit("ffn2", shards(ffn_bwd_weights(sv["h2"], dy3_bf, act, dg, du, name="ffn_bwd_weights")))
    dmerged = mm(dy2, w["wo"], tb=True, name="out_proj_dx", tm=1024, dep=token)
    gw_out = mm(sv["merged"], dy2, ta=True, out_dtype=BF16, name="out_proj_dw").reshape(NSH, DM // NSH, DM)
    dpa, dps, dzg, small["b_gate"] = gate_bwd(dmerged, sv["zg"], row(p["b_gate"]), sv["pa"], sv["ps"], name="gate_bwd")
    gw_na = _full_to_cols(mm(sv["o_na"], dpa, ta=True, out_dtype=BF16, name="branch_dw"))
    gw_sw = _full_to_cols(mm(sv["o_sw"], dps, ta=True, out_dtype=BF16, name="branch_dw"))
    do_na = mm(dpa, w["wa"], tb=True, out_dtype=BF16, tm=SEQ, name="branch_dx")
    do_sw = mm(dps, w["ws"], tb=True, out_dtype=BF16, tm=SEQ, name="branch_dx")
    dqa, dka, dva, dbias = na_bwd(sv["qa"], sv["ka"], sv["va"], sv["o_na"], do_na, sv["lse_na"], sv["bias"], name="na_bwd")
    dqs, dkvp, dt5, dsink = sw_bwd(sv["qs"], sv["kvp"], sv["o_sw"], do_sw, sv["lse_sw"], t5b, sv["sink"], name="sw_bwd")
    dkv = dkvp[SW_BLK:SW_BLK + SEQ]
    dz, dgqa, dgka, dgqs, dgks = qknorm_bwd(sv["z"], dqa, dka, dva, dqs, dkv, p["na_q_norm"], p["na_k_norm"],
                                            p["sw_q_norm"], p["sw_k_norm"], name="qknorm_bwd")
    small["na_q_norm"], small["na_k_norm"], small["sw_q_norm"], small["sw_k_norm"] = fold(dgqa), fold(dgka), fold(dgqs), fold(dgks)
    small["na_rpb"] = rpb_grad(dbias, name="rpb_grad")
    small["sw_sink"] = dsink
    gw_att_t = mm(dz, sv["hm"], ta=True, out_dtype=BF16, tm=768, name="proj_att_dw")
    gw_gz_t = mm(dzg, sv["hm"], ta=True, out_dtype=BF16, tm=1024, name="proj_gate_dw")
    gw_in = jnp.concatenate([gw_att_t, gw_gz_t], axis=0).reshape(NSH, IN_W // NSH, DM)
    token = emit("mix", (gw_in, gw_na, gw_sw, gw_out))
    dh = mm(dz, w["w_att_t"], tm=1024, name="proj_att_dx", dep=token)
    dh = mm(dzg, w["w_gz_t"], add=dh, tm=1024, name="proj_gate_dx")
    dy1, dy1_bf, small["mix_norm"] = rms_bwd(dh, sv["y1"], row(p["mix_norm"]), dy2, name="mix_norm_bwd")
    dx, dx_bf, small["ffn1_norm"], act, dg, du = ffn_bwd_tokens(
        dy1, sv["x"], row(p["ffn1_norm"]), sv["gg1"], sv["uu1"], g1["ffn1_w_gate"], g1["ffn1_w_up"], g1["ffn1_w_down"],
        name="ffn_bwd_tokens")
    emit("ffn1", shards(ffn_bwd_weights(sv["h1"], dy1_bf, act, dg, du, name="ffn_bwd_weights")))
    return dx, dx_bf, small, dt5


ANY = pl.BlockSpec(memory_space=pl.ANY)


def _place():
    x, y, c = lax.axis_index("x"), lax.axis_index("y"), lax.axis_index("c")
    chips = [(1 - x, y), (x, 1 - y), (1 - x, 1 - y)]
    return x, y, c, chips


def _remote(src, dst, send_sem, recv_sem, to):
    return pltpu.make_async_remote_copy(src_ref=src, dst_ref=dst, send_sem=send_sem, recv_sem=recv_sem, device_id=to,
                                        device_id_type=MESH)


HBM = pl.BlockSpec(memory_space=pltpu.HBM)
SEM = pl.BlockSpec(memory_space=pltpu.SEMAPHORE)
ORDERED_EFFECT = pltpu.SideEffectType.DATAFLOW_SIDE_EFFECTING


def _in_hbm(v):
    return pltpu.with_memory_space_constraint(v, pltpu.HBM)


def _row_half(ref_shape_rows, c):
    half = ref_shape_rows // 2
    return pl.ds(c * half, half)


def _ici_gather_copies(w, land, send_sems, recv_sems):
    x, y, c, chips = _place()
    me = 2 * x + y
    copies = []
    for a in range(len(w)):
        rows = _row_half(w[a].shape[0], c)
        for k, chip in enumerate(chips):
            copies.append(_remote(w[a].at[rows], land[a].at[me, rows], send_sems.at[4 * a + k], recv_sems.at[4 * a + k],
                                  (*chip, c)))
        copies.append(_remote(w[a], land[a].at[me], send_sems.at[4 * a + 3], recv_sems.at[4 * a + 3], (x, y, 1 - c)))
    return copies


def _d2d_gather_copies(w, land, send_sems, recv_sems):
    x, y, c, chips = _place()
    copies = []
    for a in range(len(w)):
        rows = _row_half(w[a].shape[0], c)
        for k, (cx, cy) in enumerate(chips):
            blk = land[a].at[2 * cx + cy, rows]
            copies.append(_remote(blk, blk, send_sems.at[3 * a + k], recv_sems.at[3 * a + k], (x, y, 1 - c)))
    return copies


def _d2d_gather_waits(w, land, send_sems, recv_sems):
    x, y, c, chips = _place()
    waits = []
    for a in range(len(w)):
        rows = _row_half(w[a].shape[0], 1 - c)
        for k, (cx, cy) in enumerate(chips):
            blk = land[a].at[2 * cx + cy, rows]
            waits.append(_remote(blk, blk, send_sems.at[3 * a + k], recv_sems.at[3 * a + k], (x, y, 1 - c)))
    return waits


def gather_start(groups, dep=None, *, name):
    sizes = [len(g) for g in groups]
    shards = [s for g in groups for s in g]
    n, ng = len(shards), len(groups)
    extra = [] if dep is None else [dep]

    def body(*refs):
        first_out = 2 * n + len(extra)
        w, land, sems = refs[:n], refs[n:2 * n], refs[first_out:first_out + 2 * ng]
        off = 0
        for gi, size in enumerate(sizes):
            for cp in _ici_gather_copies(w[off:off + size], land[off:off + size], sems[2 * gi], sems[2 * gi + 1]):
                cp.start()
            off += size

    lands = [lax.empty((NSH,) + s.shape, s.dtype) for s in shards]
    sem_shapes = tuple(pltpu.SemaphoreType.DMA((4 * size,)) for size in sizes for _ in range(2))
    res = pl.pallas_call(
        body, name=name,
        out_shape=sem_shapes + tuple(pltpu.HBM(s.shape, s.dtype) for s in shards) + tuple(pltpu.HBM(l.shape, l.dtype) for l in lands),
        in_specs=[HBM] * (2 * n) + [ANY] * len(extra), out_specs=(SEM,) * (2 * ng) + (HBM,) * (2 * n),
        input_output_aliases={i: 2 * ng + i for i in range(2 * n)},
        compiler_params=pltpu.CompilerParams(has_side_effects=ORDERED_EFFECT))(
            *[_in_hbm(s) for s in shards], *[_in_hbm(l) for l in lands], *extra)
    out, off = [], 0
    for gi, size in enumerate(sizes):
        out.append((res[2 * gi], res[2 * gi + 1], list(res[2 * ng + off:2 * ng + off + size]),
                    list(res[2 * ng + n + off:2 * ng + n + off + size])))
        off += size
    return out


def gather_wait(send_sems, recv_sems, shards, lands, after, *, name):
    n = len(shards)

    def body(*refs):
        w, land = refs[:n], refs[n:2 * n]
        send, recv = refs[2 * n:2 * n + 2]
        for cp in _ici_gather_copies(w, land, send, recv):
            cp.wait_send()
            cp.wait_recv()

    res = pl.pallas_call(
        body, name=name,
        out_shape=tuple(pltpu.HBM(s.shape, s.dtype) for s in shards) + tuple(pltpu.HBM(l.shape, l.dtype) for l in lands),
        in_specs=[HBM] * (2 * n) + [SEM, SEM] + [ANY] * len(after), out_specs=(HBM,) * (2 * n),
        input_output_aliases={i: i for i in range(2 * n)},
        compiler_params=pltpu.CompilerParams(has_side_effects=ORDERED_EFFECT))(*shards, *lands, send_sems, recv_sems, *after)
    return list(res[:n]), list(res[n:])


def gather_finish(shards, lands, *, name):
    n = len(shards)

    def body(*refs):
        w, land = refs[:n], refs[n:2 * n]
        send_sems, recv_sems = refs[3 * n:]
        d2d = _d2d_gather_copies(w, land, send_sems, recv_sems)
        for cp in d2d:
            cp.start()
        for cp in _d2d_gather_waits(w, land, send_sems, recv_sems):
            cp.wait_recv()
        for cp in d2d:
            cp.wait_send()

    return list(pl.pallas_call(
        body, name=name, out_shape=tuple(pltpu.HBM(l.shape, l.dtype) for l in lands),
        in_specs=[ANY] * (2 * n), out_specs=tuple([ANY] * n), input_output_aliases={n + i: i for i in range(n)},
        scratch_shapes=[pltpu.SemaphoreType.DMA((3 * n,)), pltpu.SemaphoreType.DMA((3 * n,))])(*shards, *lands))


def _pair_exchange_copies(g, buf, send_sems, recv_sems):
    x, y, c, _ = _place()
    copies = []
    for a in range(len(g)):
        half = g[a].shape[1] // 2
        copies.append(_remote(g[a].at[:, pl.ds((1 - c) * half, half)], buf[a], send_sems.at[a], recv_sems.at[a], (x, y, 1 - c)))
    return copies


def pair_exchange_start(grads, dep=None, *, name):
    n = len(grads)
    extra = [] if dep is None else [dep]

    def body(*refs):
        sems = refs[2 * n + len(extra):]
        for cp in _pair_exchange_copies(refs[:n], refs[n:2 * n], sems[0], sems[1]):
            cp.start()
        refs[-1][...] = jnp.zeros_like(refs[-1])

    lands = [lax.empty((NSH, g.shape[1] // 2, g.shape[2]), g.dtype) for g in grads]
    res = pl.pallas_call(
        body, name=name,
        out_shape=(pltpu.SemaphoreType.DMA((n,)), pltpu.SemaphoreType.DMA((n,)))
        + tuple(pltpu.HBM(g.shape, g.dtype) for g in grads) + tuple(pltpu.HBM(l.shape, l.dtype) for l in lands)
        + (_sds((8, 128), F32),),
        in_specs=[HBM] * (2 * n) + [ANY] * len(extra),
        out_specs=(SEM, SEM) + (HBM,) * (2 * n) + (pl.BlockSpec(memory_space=pltpu.VMEM),),
        input_output_aliases={i: 2 + i for i in range(2 * n)},
        compiler_params=pltpu.CompilerParams(has_side_effects=ORDERED_EFFECT))(
            *[_in_hbm(g) for g in grads], *[_in_hbm(l) for l in lands], *extra)
    return res[0], res[1], list(res[2:2 + n]), list(res[2 + n:2 + 2 * n]), res[-1]


def pair_exchange_wait(send_sems, recv_sems, grads, lands, after, *, name):
    n = len(grads)

    def body(*refs):
        for cp in _pair_exchange_copies(refs[:n], refs[n:2 * n], refs[2 * n], refs[2 * n + 1]):
            cp.wait_send()
            cp.wait_recv()

    res = pl.pallas_call(
        body, name=name,
        out_shape=tuple(pltpu.HBM(g.shape, g.dtype) for g in grads) + tuple(pltpu.HBM(l.shape, l.dtype) for l in lands),
        in_specs=[HBM] * (2 * n) + [SEM, SEM] + [ANY] * len(after), out_specs=(HBM,) * (2 * n),
        input_output_aliases={i: i for i in range(2 * n)},
        compiler_params=pltpu.CompilerParams(has_side_effects=ORDERED_EFFECT))(*grads, *lands, send_sems, recv_sems, *after)
    return list(res[:n]), list(res[n:])


def _chip_exchange_copies(s, buf, send_sems, recv_sems):
    x, y, c, chips = _place()
    return [_remote(s[a].at[2 * cx + cy], buf[a].at[k], send_sems.at[3 * a + k], recv_sems.at[3 * a + k], (cx, cy, c))
            for a in range(len(s)) for k, (cx, cy) in enumerate(chips)]


def chip_exchange(sums, *, name):
    n = len(sums)

    def body(*refs):
        copies = _chip_exchange_copies(refs[:n], refs[n:2 * n], *refs[2 * n:])
        for cp in copies:
            cp.start()
        for cp in copies:
            cp.wait()

    return pl.pallas_call(
        body, name=name, out_shape=tuple(pltpu.HBM((3,) + s.shape[1:], s.dtype) for s in sums),
        in_specs=[ANY] * n, out_specs=tuple([ANY] * n),
        scratch_shapes=[pltpu.SemaphoreType.DMA((3 * n,)), pltpu.SemaphoreType.DMA((3 * n,))])(*sums)


def chip_exchange_start(sums, *, name):
    n = len(sums)

    def body(*refs):
        for cp in _chip_exchange_copies(refs[:n], refs[n:2 * n], refs[2 * n], refs[2 * n + 1]):
            cp.start()
        refs[-1][...] = jnp.zeros_like(refs[-1])

    lands = [lax.empty((3,) + s.shape[1:], s.dtype) for s in sums]
    res = pl.pallas_call(
        body, name=name,
        out_shape=(pltpu.SemaphoreType.DMA((3 * n,)), pltpu.SemaphoreType.DMA((3 * n,)))
        + tuple(pltpu.HBM(s.shape, s.dtype) for s in sums) + tuple(pltpu.HBM(l.shape, l.dtype) for l in lands)
        + (_sds((8, 128), F32),),
        in_specs=[HBM] * (2 * n), out_specs=(SEM, SEM) + (HBM,) * (2 * n) + (pl.BlockSpec(memory_space=pltpu.VMEM),),
        input_output_aliases={i: 2 + i for i in range(2 * n)},
        compiler_params=pltpu.CompilerParams(has_side_effects=ORDERED_EFFECT))(
            *[_in_hbm(s) for s in sums], *[_in_hbm(l) for l in lands])
    return res[0], res[1], list(res[2:2 + n]), list(res[2 + n:2 + 2 * n]), res[-1]


def chip_exchange_wait(send_sems, recv_sems, sums, lands, after, *, name):
    n = len(sums)

    def body(*refs):
        for cp in _chip_exchange_copies(refs[:n], refs[n:2 * n], refs[2 * n], refs[2 * n + 1]):
            cp.wait_send()
            cp.wait_recv()

    res = pl.pallas_call(
        body, name=name,
        out_shape=tuple(pltpu.HBM(s.shape, s.dtype) for s in sums) + tuple(pltpu.HBM(l.shape, l.dtype) for l in lands),
        in_specs=[HBM] * (2 * n) + [SEM, SEM] + [ANY] * len(after), out_specs=(HBM,) * (2 * n),
        input_output_aliases={i: i for i in range(2 * n)},
        compiler_params=pltpu.CompilerParams(has_side_effects=ORDERED_EFFECT))(*sums, *lands, send_sems, recv_sems, *after)
    return list(res[:n]), list(res[n:])


def _pair_send_copies(h, got, send_sems, recv_sems):
    x, y, c, _ = _place()
    return [_remote(h[i], got[i], send_sems.at[i], recv_sems.at[i], (x, y, 1 - c)) for i in range(len(h))]


def pair_send_start(halves, *, name):
    n = len(halves)

    def body(*refs):
        for cp in _pair_send_copies(refs[:n], refs[n:2 * n], refs[2 * n], refs[2 * n + 1]):
            cp.start()
        refs[-1][...] = jnp.zeros_like(refs[-1])

    lands = [lax.empty(h.shape, h.dtype) for h in halves]
    res = pl.pallas_call(
        body, name=name,
        out_shape=(pltpu.SemaphoreType.DMA((n,)), pltpu.SemaphoreType.DMA((n,)))
        + tuple(pltpu.HBM(h.shape, h.dtype) for h in halves) * 2 + (_sds((8, 128), F32),),
        in_specs=[HBM] * (2 * n), out_specs=(SEM, SEM) + (HBM,) * (2 * n) + (pl.BlockSpec(memory_space=pltpu.VMEM),),
        input_output_aliases={i: 2 + i for i in range(2 * n)},
        compiler_params=pltpu.CompilerParams(has_side_effects=ORDERED_EFFECT))(
            *[_in_hbm(h) for h in halves], *[_in_hbm(l) for l in lands])
    return res[0], res[1], list(res[2:2 + n]), list(res[2 + n:2 + 2 * n]), res[-1]


def pair_send_wait(send_sems, recv_sems, halves, lands, after, *, name):
    n = len(halves)

    def body(*refs):
        for cp in _pair_send_copies(refs[:n], refs[n:2 * n], refs[2 * n], refs[2 * n + 1]):
            cp.wait_send()
            cp.wait_recv()

    res = pl.pallas_call(
        body, name=name, out_shape=tuple(pltpu.HBM(h.shape, h.dtype) for h in halves) * 2,
        in_specs=[HBM] * (2 * n) + [SEM, SEM] + [ANY] * len(after), out_specs=(HBM,) * (2 * n),
        input_output_aliases={i: i for i in range(2 * n)},
        compiler_params=pltpu.CompilerParams(has_side_effects=ORDERED_EFFECT))(*halves, *lands, send_sems, recv_sems, *after)
    return list(res[:n]), list(res[n:])


def allreduce_small(v, *, name):
    rows = v.shape[0]

    def body(v_ref, o_ref, gath, send_sems, recv_sems):
        x, y, c, _ = _place()
        me = 4 * x + 2 * y + c
        gath[me] = v_ref[...]
        copies = []
        for k in range(1, 8):
            fx, fy, fc = (k >> 2) & 1, (k >> 1) & 1, k & 1
            peer = (jnp.where(fx, 1 - x, x), jnp.where(fy, 1 - y, y), jnp.where(fc, 1 - c, c))
            cp = _remote(v_ref, gath.at[me], send_sems.at[k - 1], recv_sems.at[k - 1], peer)
            cp.start()
            copies.append(cp)
        for cp in copies:
            cp.wait()
        acc = gath[0]
        for d in range(1, 8):
            acc = acc + gath[d]
        o_ref[...] = acc

    return pl.pallas_call(
        body, name=name, out_shape=_sds(v.shape, F32),
        in_specs=[pl.BlockSpec(memory_space=pltpu.VMEM)], out_specs=pl.BlockSpec(memory_space=pltpu.VMEM),
        scratch_shapes=[pltpu.VMEM((8, rows, 128), F32), pltpu.SemaphoreType.DMA((7,)), pltpu.SemaphoreType.DMA((7,))])(v)


def _same_shape_runs(arrays):
    runs = {}
    for i, a in enumerate(arrays):
        runs.setdefault(a.shape, []).append(i)
    return list(runs.values())


def _per_shape(fn, *lists):
    out = [None] * len(lists[0])
    for idx in _same_shape_runs(lists[0]):
        for i, r in zip(idx, fn(*[[l[i] for i in idx] for l in lists])):
            out[i] = r
    return out


def add_halves(gs, bufs, cidx, *, name):
    cnt = len(gs)
    _, k, n = gs[0].shape

    def body(c_ref, *refs):
        g, b, o = refs[:cnt], refs[cnt:2 * cnt], refs[2 * cnt:]
        for i in range(cnt):
            o[i][...] = (g[i][...].astype(F32) + b[i][...].astype(F32)).astype(BF16)

    blk = pl.BlockSpec((None, k // 2, n), lambda s, c: (s, 0, 0))
    mine = pl.BlockSpec((None, k // 2, n), lambda s, c: (s, c[0], 0))
    return list(pl.pallas_call(
        body, name=name, out_shape=tuple(_sds(b.shape, BF16) for b in bufs),
        grid_spec=pltpu.PrefetchScalarGridSpec(
            num_scalar_prefetch=1, grid=(NSH,), in_specs=[mine] * cnt + [blk] * cnt, out_specs=tuple([blk] * cnt)),
        compiler_params=_params(("parallel",)))(cidx, *gs, *bufs))


def add_chips(sums, bufs, sidx, *, name):
    cnt = len(sums)
    _, kh, n = sums[0].shape

    def body(s_ref, *refs):
        mine, b, o = refs[:cnt], refs[cnt:2 * cnt], refs[2 * cnt:]
        for i in range(cnt):
            o[i][...] = ((mine[i][...].astype(F32) + b[i][0].astype(F32)) + (b[i][1].astype(F32) + b[i][2].astype(F32)))

    own = pl.BlockSpec((None, kh, n), lambda i, s: (s[0], 0, 0))
    got = pl.BlockSpec((3, kh, n), lambda i, s: (0, 0, 0))
    out = pl.BlockSpec((kh, n), lambda i, s: (0, 0))
    return list(pl.pallas_call(
        body, name=name, out_shape=tuple(_sds((kh, n), F32) for _ in sums),
        grid_spec=pltpu.PrefetchScalarGridSpec(
            num_scalar_prefetch=1, grid=(1,), in_specs=[own] * cnt + [got] * cnt, out_specs=tuple([out] * cnt)),
        compiler_params=_params(("arbitrary",)))(sidx, *sums, *bufs))


PARAMS = ("ffn1_norm", "ffn1_w_gate", "ffn1_w_up", "ffn1_w_down", "mix_norm", "w_in", "b_gate", "na_q_norm", "na_k_norm",
          "na_rpb", "sw_q_norm", "sw_k_norm", "sw_sink", "t5_rel_table", "w_branch_na", "w_branch_sw", "w_out", "ffn2_norm",
          "ffn2_w_gate", "ffn2_w_up", "ffn2_w_down")
SMALL_ALL = tuple(n for n in PARAMS if n not in BIG)
TRANSPOSED = ("ffn1_w_gate", "ffn1_w_up", "w_in", "ffn2_w_gate", "ffn2_w_up")
SMALL_ROWS = 152


def _pack_small(vals):
    flat = jnp.concatenate([vals[n].reshape(-1).astype(F32) for n in SMALL_ALL] + [vals["loss"].reshape(-1)])
    return jnp.pad(flat, (0, SMALL_ROWS * 128 - flat.shape[0])).reshape(SMALL_ROWS, 128)


def _unpack_small(packed, like):
    flat, out, off = packed.reshape(-1), {}, 0
    for n in SMALL_ALL:
        size = math.prod(like[n].shape)
        out[n] = flat[off:off + size].reshape(like[n].shape)
        off += size
    out["loss"] = flat[off]
    return out


def kernel(x, ffn1_norm, ffn1_w_gate, ffn1_w_up, ffn1_w_down, mix_norm, w_in, b_gate, na_q_norm, na_k_norm, na_rpb, sw_q_norm, sw_k_norm, sw_sink, t5_rel_table, w_branch_na, w_branch_sw, w_out, ffn2_norm, ffn2_w_gate, ffn2_w_up, ffn2_w_down, loss_target, m_ffn1_norm, m_ffn1_w_gate, m_ffn1_w_up, m_ffn1_w_down, m_mix_norm, m_w_in, m_b_gate, m_na_q_norm, m_na_k_norm, m_na_rpb, m_sw_q_norm, m_sw_k_norm, m_sw_sink, m_t5_rel_table, m_w_branch_na, m_w_branch_sw, m_w_out, m_ffn2_norm, m_ffn2_w_gate, m_ffn2_w_up, m_ffn2_w_down, v_ffn1_norm, v_ffn1_w_gate, v_ffn1_w_up, v_ffn1_w_down, v_mix_norm, v_w_in, v_b_gate, v_na_q_norm, v_na_k_norm, v_na_rpb, v_sw_q_norm, v_sw_k_norm, v_sw_sink, v_t5_rel_table, v_w_branch_na, v_w_branch_sw, v_w_out, v_ffn2_norm, v_ffn2_w_gate, v_ffn2_w_up, v_ffn2_w_down):
    args = locals()
    tr = lambda n, a: jnp.transpose(a, (0, 2, 1)) if n in TRANSPOSED else a
    w = {n: tr(n, args[n]) for n in PARAMS}
    m = {n: tr(n, args["m_" + n]) for n in PARAMS}
    v = {n: tr(n, args["v_" + n]) for n in PARAMS}
    cidx = lax.axis_index("c").astype(jnp.int32).reshape(1)
    sidx = (2 * lax.axis_index("x") + lax.axis_index("y")).astype(jnp.int32).reshape(1)

    small = [{n: w[n][l] for n in SMALL} for l in range(DEPTH)]
    order = ("ffn1", "mix", "ffn2")

    keys = [(l, g) for l in range(DEPTH) for g in order]
    local = lambda l, g: [w[n][l].astype(BF16) for n in GROUPS[g]]
    first = gather_start([local(*keys[0])], name="gather_start")
    rest = gather_start([local(*key) for key in keys[1:]], first[0][2][0], name="gather_start")
    in_flight = dict(zip(keys, first + rest))
    t5b = t5_bias(w["t5_rel_table"], name="t5_bias")
    for l in range(DEPTH):
        small[l]["na_bias"] = na_bias_table(small[l]["na_rpb"], name="na_bias_table")
    early = [t5b] + [small[l]["na_bias"] for l in range(DEPTH)] + [rest[0][2][0]]

    def weights_of(l):
        def get(group, after):
            send_sems, recv_sems, thru, lands = in_flight[(l, group)]
            after = [after] + (early if (l, group) == keys[0] else [])
            thru, lands = gather_wait(send_sems, recv_sems, thru, lands, after, name="gather_wait")
            return dict(zip(GROUPS[group], gather_finish(thru, lands, name="gather_finish")))
        return get

    h0, saved0 = layer_fwd(x[0], small[0], weights_of(0), t5b)
    h1, saved1 = layer_fwd(h0, small[1], weights_of(1), t5b)
    dy, dy_bf, loss_row = loss_head(h1, loss_target[0], name="loss_head")

    crossing, tokens, pending = {}, [], []

    def ship(after):
        key, send_sems, recv_sems, grads, lands = pending.pop()
        grads, from_sibling = pair_exchange_wait(send_sems, recv_sems, grads, lands, after, name="pair_exchange_wait")
        sums = _per_shape(lambda gs, bs: add_halves(gs, bs, cidx, name="add_halves"), grads, from_sibling)
        send_sems, recv_sems, sums, lands, token = chip_exchange_start(sums, name="chip_exchange_start")
        crossing[key] = (send_sems, recv_sems, sums, lands)
        return token

    def reduce_of(l):
        def emit(group, grads):
            grads = list(grads)
            shipped = ship([grads[0]]) if pending else None
            send_sems, recv_sems, grads, lands, token = pair_exchange_start(grads, shipped, name="pair_exchange_start")
            pending.append(((l, group), send_sems, recv_sems, grads, lands))
            tokens.append(token)
            return token
        return emit

    def finish(layer, after, filled=None):
        sent = {}
        for group in order:
            send_sems, recv_sems, sums, lands = crossing[(layer, group)]
            sums, got = chip_exchange_wait(send_sems, recv_sems, sums, lands, after, name="chip_exchange_wait")
            halves = _per_shape(lambda ss, bs: add_chips(ss, bs, sidx, name="add_chips"), sums, got)
            sent[group] = pair_send_start(halves, name="pair_send_start")
            after = [sent[group][4]]
        out = {}
        for group in order:
            send_sems, recv_sems, halves, lands, _ = sent[group]
            halves, theirs = pair_send_wait(send_sems, recv_sems, halves, lands, after, name="pair_send_wait")
            names = GROUPS[group]
            res = _per_shape(
                lambda ws, ms, vs, a, b, *f: adamw_layer(ws, ms, vs, a, b, cidx, layer, list(f[0]) if f else None, name="adamw_layer"),
                *([[w[n] for n in names], [m[n] for n in names], [v[n] for n in names], halves, theirs]
                  + ([[filled[n] for n in names]] if filled is not None else [])))
            out.update(zip(names, res))
            after = [res[-1][0]]
        return out

    dy, dy_bf, small1, dt5_1 = layer_bwd(dy, dy_bf, saved1, small[1], t5b, reduce_of(1))
    grad_x, _, small0, dt5_0 = layer_bwd(dy, dy_bf, saved0, small[0], t5b, reduce_of(0), dep=tokens[-1])
    done1 = finish(1, [ship([grad_x])])

    smalls = [small0, small1]
    dt5 = t5_table_grad(dt5_0, dt5_1, name="t5_table_grad").reshape(32, 8)
    local_small = {n: jnp.stack([smalls[l][n].reshape(w[n].shape[1:]) for l in range(DEPTH)]) for n in SMALL}
    local_small["t5_rel_table"] = dt5
    local_small["loss"] = loss_row[0, 0:1]
    total = allreduce_small(_pack_small(local_small), name="allreduce_small")
    small_grads = _unpack_small(total, w)
    pack = lambda d: _pack_small({**d, "loss": jnp.zeros((1,), F32)})[None]
    ds, ms, vs = adamw(pack(w), total[None], pack(m), pack(v), name="adamw_small")

    grad, delta, new_m, new_v = {}, {}, {}, {}
    for n, done in finish(0, [ds, done1[BIG[-1]][0]], filled=done1).items():
        grad[n], delta[n], new_m[n], new_v[n] = done
    for n in SMALL_ALL:
        grad[n] = small_grads[n]
    d_s, m_s, v_s = _unpack_small(ds[0], w), _unpack_small(ms[0], w), _unpack_small(vs[0], w)
    for n in SMALL_ALL:
        delta[n], new_m[n], new_v[n] = d_s[n], m_s[n], v_s[n]

    return (small_grads["loss"], grad_x[None], *[tr(n, grad[n]) for n in PARAMS], *[tr(n, delta[n]) for n in PARAMS],
            *[tr(n, new_m[n]) for n in PARAMS], *[tr(n, new_v[n]) for n in PARAMS])
```

```python
import functools
import math

import jax
import jax.numpy as jnp
import numpy as np
from jax import lax
from jax.experimental import pallas as pl
from jax.experimental.pallas import tpu as pltpu

F32 = jnp.float32
BF16 = jnp.bfloat16

SEQ = 2048
DM = 1024
DFF = 2816
DEPTH = 2
NSH = 4
FSH = DFF // NSH
GRID_W = 64
ROWS = SEQ // GRID_W
NA_HEADS = 8
HD = 64
NA_WR = 8
NA_WC = 16
NA_KEYS = NA_WR * GRID_W
SW_BLK = 128
SW_NB = SEQ // SW_BLK
SW_KEYS = 3 * SW_BLK
ATT_W = 2304
GATE_W = 2048
IN_W = ATT_W + GATE_W
EPS = 1e-6
NEG = -1e30
QK_SCALE = 1.0 / math.sqrt(HD)

ADAM_LR = 0.001
ADAM_B1 = 0.9
ADAM_B2 = 0.999
ADAM_EPS = 1e-08
ADAM_WD = 0.01
ADAM_STEP = 10

VMEM_LIMIT = 56 << 20
MESH = pl.DeviceIdType.MESH

NT = (((1,), (1,)), ((), ()))
TN = (((0,), (0,)), ((), ()))
NN = (((1,), (0,)), ((), ()))


def _dot(a, b, dims=NN):
    return lax.dot_general(a, b, dims, preferred_element_type=F32)


def _params(sem=None):
    return pltpu.CompilerParams(dimension_semantics=sem, vmem_limit_bytes=VMEM_LIMIT)


def _sds(shape, dtype):
    return jax.ShapeDtypeStruct(shape, dtype)


def mm(a, b, *, name, ta=False, tb=False, out_dtype=F32, add=None, scale=None, tm=512, tn=None, tk=None, exact=False,
       dep=None):
    m, kd = (a.shape[1], a.shape[0]) if ta else a.shape
    n = b.shape[0] if tb else b.shape[1]
    tm, tn, tk = min(tm, m), min(tn or n, n), min(tk or kd, kd)
    nk = kd // tk
    dims = (((0 if ta else 1,), (1 if tb else 0,)), ((), ()))

    def body(*refs):
        a_ref, b_ref = refs[:2]
        add_ref = refs[2] if add is not None else None
        o_ref, acc = refs[-2:]
        k = pl.program_id(2)

        @pl.when(k == 0)
        def _():
            acc[...] = jnp.zeros_like(acc)

        if exact:
            acc[...] += lax.dot_general(a_ref[...], b_ref[...], dims, precision=lax.Precision.HIGHEST,
                                        preferred_element_type=F32)
        else:
            acc[...] += lax.dot_general(a_ref[...].astype(BF16), b_ref[...].astype(BF16), dims,
                                        preferred_element_type=F32)

        @pl.when(k == nk - 1)
        def _():
            r = acc[...]
            if scale is not None:
                r = r * scale
            if add is not None:
                r = r + add_ref[...]
            o_ref[...] = r.astype(out_dtype)

    a_spec = pl.BlockSpec((tk, tm), lambda i, j, k: (k, i)) if ta else pl.BlockSpec((tm, tk), lambda i, j, k: (i, k))
    b_spec = pl.BlockSpec((tn, tk), lambda i, j, k: (j, k)) if tb else pl.BlockSpec((tk, tn), lambda i, j, k: (k, j))
    o_spec = pl.BlockSpec((tm, tn), lambda i, j, k: (i, j))
    ins, specs = [a, b], [a_spec, b_spec]
    if add is not None:
        ins.append(add)
        specs.append(o_spec)
    if dep is not None:
        ins.append(dep)
        specs.append(pl.BlockSpec(memory_space=pl.ANY))
    return pl.pallas_call(
        body, name=name, out_shape=_sds((m, n), out_dtype), grid=(m // tm, n // tn, nk), in_specs=specs,
        out_specs=o_spec, scratch_shapes=[pltpu.VMEM((tm, tn), F32)],
        compiler_params=_params(("parallel", "parallel", "arbitrary")))(*ins)


def _rms(x):
    return lax.rsqrt(jnp.mean(x * x, axis=-1, keepdims=True) + EPS)


def rms_fwd(x, gain, *, name, tm=512):
    def body(x_ref, g_ref, h_ref):
        x = x_ref[...]
        h_ref[...] = (x * _rms(x) * g_ref[...]).astype(BF16)

    return pl.pallas_call(
        body, name=name, out_shape=_sds(x.shape, BF16), grid=(x.shape[0] // tm,),
        in_specs=[pl.BlockSpec((tm, DM), lambda i: (i, 0)), pl.BlockSpec((1, DM), lambda i: (0, 0))],
        out_specs=pl.BlockSpec((tm, DM), lambda i: (i, 0)), compiler_params=_params(("parallel",)))(x, gain)


def _rms_bwd_math(dh, x, gain):
    r = _rms(x)
    xh = x * r
    dgain = jnp.sum(dh * xh, axis=0, keepdims=True)
    dxn = dh * gain
    dx = r * (dxn - xh * jnp.mean(dxn * xh, axis=-1, keepdims=True))
    return dx, dgain


def rms_bwd(dh, x, gain, dres, *, name, tm=512):
    def body(dh_ref, x_ref, g_ref, dres_ref, dx_ref, dxb_ref, dg_ref):
        @pl.when(pl.program_id(0) == 0)
        def _():
            dg_ref[...] = jnp.zeros_like(dg_ref)

        dx, dg = _rms_bwd_math(dh_ref[...], x_ref[...], g_ref[...])
        dx = dres_ref[...] + dx
        dx_ref[...] = dx
        dxb_ref[...] = dx.astype(BF16)
        dg_ref[...] += dg

    tile = pl.BlockSpec((tm, DM), lambda i: (i, 0))
    vec = pl.BlockSpec((1, DM), lambda i: (0, 0))
    return pl.pallas_call(
        body, name=name, out_shape=(_sds(x.shape, F32), _sds(x.shape, BF16), _sds((1, DM), F32)), grid=(x.shape[0] // tm,),
        in_specs=[tile, tile, vec, tile], out_specs=(tile, tile, vec), compiler_params=_params(("arbitrary",)))(dh, x, gain, dres)


def _with_dep(ins, specs, dep):
    if dep is None:
        return ins, specs
    return ins + [dep], specs + [pl.BlockSpec(memory_space=pl.ANY)]


def _resident_weight():
    return pl.BlockSpec((DFF, DM), lambda i: (0, 0), pipeline_mode=pl.Buffered(1))


def ffn_fwd(x, gain, wg, wu, wd, *, name, tm=512):
    def body(x_ref, g_ref, wg_ref, wu_ref, wd_ref, y_ref, h_ref, gg_ref, uu_ref):
        x = x_ref[...]
        h = (x * _rms(x) * g_ref[...]).astype(BF16)
        h_ref[...] = h
        gg = _dot(h, wg_ref[...], NT)
        uu = _dot(h, wu_ref[...], NT)
        gg_ref[...] = gg.astype(BF16)
        uu_ref[...] = uu.astype(BF16)
        act = (gg * jax.nn.sigmoid(gg) * uu).astype(BF16)
        y_ref[...] = x + 0.5 * _dot(act, wd_ref[...])

    s = x.shape[0]
    tile = pl.BlockSpec((tm, DM), lambda i: (i, 0))
    hid = pl.BlockSpec((tm, DFF), lambda i: (i, 0))
    w = _resident_weight()
    return pl.pallas_call(
        body, name=name,
        out_shape=(_sds((s, DM), F32), _sds((s, DM), BF16), _sds((s, DFF), BF16), _sds((s, DFF), BF16)),
        grid=(s // tm,), in_specs=[tile, pl.BlockSpec((1, DM), lambda i: (0, 0)), w, w, w],
        out_specs=(tile, tile, hid, hid), compiler_params=_params(("parallel",)))(x, gain, wg, wu, wd)


def ffn_bwd_tokens(dy, x, gain, gg, uu, wg, wu, wd, *, name, tm=256, dep=None):
    def body(dy_ref, x_ref, g_ref, gg_ref, uu_ref, wg_ref, wu_ref, wd_ref, *rest):
        dx_ref, dxb_ref, dgain_ref, act_ref, dg_ref, du_ref = rest[-6:]

        @pl.when(pl.program_id(0) == 0)
        def _():
            dgain_ref[...] = jnp.zeros_like(dgain_ref)

        dy = dy_ref[...]
        dact = _dot((0.5 * dy).astype(BF16), wd_ref[...], NT)
        g = gg_ref[...].astype(F32)
        u = uu_ref[...].astype(F32)
        sg = jax.nn.sigmoid(g)
        silu = g * sg
        act_ref[...] = (silu * u).astype(BF16)
        dg = (dact * u * (sg * (1.0 + g * (1.0 - sg)))).astype(BF16)
        du = (dact * silu).astype(BF16)
        dg_ref[...] = dg
        du_ref[...] = du
        dx, dgain = _rms_bwd_math(_dot(dg, wg_ref[...]) + _dot(du, wu_ref[...]), x_ref[...], g_ref[...])
        dx = dy + dx
        dx_ref[...] = dx
        dxb_ref[...] = dx.astype(BF16)
        dgain_ref[...] += dgain

    s = x.shape[0]
    tile = pl.BlockSpec((tm, DM), lambda i: (i, 0))
    vec = pl.BlockSpec((1, DM), lambda i: (0, 0))
    hid = pl.BlockSpec((tm, DFF), lambda i: (i, 0))
    hshape = _sds((s, DFF), BF16)
    w = _resident_weight()
    ins, specs = _with_dep([dy, x, gain, gg, uu, wg, wu, wd], [tile, tile, vec, hid, hid, w, w, w], dep)
    return pl.pallas_call(
        body, name=name, out_shape=(_sds((s, DM), F32), _sds((s, DM), BF16), _sds((1, DM), F32), hshape, hshape, hshape),
        grid=(s // tm,), in_specs=specs, out_specs=(tile, tile, vec, hid, hid, hid),
        compiler_params=_params(("arbitrary",)))(*ins)


def ffn_bwd_weights(h, dy, act, dg, du, *, name, tf=256):
    def body(h_ref, dy_ref, act_ref, dg_ref, du_ref, gwg_ref, gwu_ref, gwd_ref):
        h = h_ref[...]
        gwg_ref[...] = _dot(dg_ref[...], h, TN).astype(BF16)
        gwu_ref[...] = _dot(du_ref[...], h, TN).astype(BF16)
        gwd_ref[...] = (0.5 * _dot(act_ref[...], dy_ref[...], TN)).astype(BF16)

    s = h.shape[0]
    full = pl.BlockSpec((s, DM), lambda f: (0, 0))
    hid = pl.BlockSpec((s, tf), lambda f: (0, f))
    wt = pl.BlockSpec((tf, DM), lambda f: (f, 0))
    wshape = _sds((DFF, DM), BF16)
    return pl.pallas_call(
        body, name=name, out_shape=(wshape, wshape, wshape), grid=(DFF // tf,), in_specs=[full, full, hid, hid, hid],
        out_specs=(wt, wt, wt), compiler_params=_params(("parallel",)))(h, dy, act, dg, du)


def _group_mean(v, bd):
    hi = v.astype(BF16)
    lo = (v - hi.astype(F32)).astype(BF16)
    return _dot(hi, bd) + _dot(lo, bd)


def _block_diag(width):
    idx = np.arange(width) // HD
    return jnp.asarray((idx[:, None] == idx[None, :]).astype(np.float32) / HD, dtype=BF16)


def qknorm_fwd(z, gq_na, gk_na, gq_sw, gk_sw, *, name, tm=256):
    def body(zq_ref, zk_ref, zv_ref, zs_ref, zkv_ref, gqa_ref, gka_ref, gqs_ref, gks_ref, bd_ref, bd2_ref,
             qa_ref, ka_ref, va_ref, qs_ref, kv_ref):
        bd = bd_ref[...]

        def norm(x, g, bdm):
            x = x.astype(F32)
            return x * lax.rsqrt(_group_mean(x * x, bdm) + EPS) * g

        qa_ref[...] = (norm(zq_ref[...], gqa_ref[...], bd) * QK_SCALE).astype(BF16)
        ka_ref[...] = norm(zk_ref[...], gka_ref[...], bd).astype(BF16)
        va_ref[...] = zv_ref[...].astype(BF16)
        qs_ref[...] = (norm(zs_ref[...], gqs_ref[...], bd) * QK_SCALE).astype(BF16)
        kv = zkv_ref[...]
        kv_ref[:, 0:128] = norm(kv[:, 0:128], gks_ref[...], bd2_ref[...]).astype(BF16)
        kv_ref[:, 128:256] = kv[:, 128:256].astype(BF16)

    s = z.shape[0]
    col = lambda j: pl.BlockSpec((tm, 512), lambda i, j=j: (i, j))
    vec = lambda w: pl.BlockSpec((1, w), lambda i: (0, 0))
    o512 = pl.BlockSpec((tm, 512), lambda i: (i, 0))
    g512 = lambda g: jnp.tile(g.reshape(1, HD), (1, 8))
    return pl.pallas_call(
        body, name=name,
        out_shape=(_sds((s, 512), BF16),) * 4 + (_sds((s, 256), BF16),), grid=(s // tm,),
        in_specs=[col(0), col(1), col(2), col(3), pl.BlockSpec((tm, 256), lambda i: (i, 8)), vec(512), vec(512), vec(512),
                  vec(128), pl.BlockSpec((512, 512), lambda i: (0, 0)), pl.BlockSpec((128, 128), lambda i: (0, 0))],
        out_specs=(o512, o512, o512, o512, pl.BlockSpec((tm, 256), lambda i: (i, 0))),
        compiler_params=_params(("parallel",)))(
            z, z, z, z, z, g512(gq_na), g512(gk_na), g512(gq_sw), jnp.tile(gk_sw.reshape(1, HD), (1, 2)),
            _block_diag(512), _block_diag(128))


def qknorm_bwd(z, dqa, dka, dva, dqs, dkv, gq_na, gk_na, gq_sw, gk_sw, *, name, tm=256):
    def body(zq_ref, zk_ref, zs_ref, zkv_ref, dqa_ref, dka_ref, dva_ref, dqs_ref, dkv_ref, gqa_ref, gka_ref, gqs_ref,
             gks_ref, bd_ref, bd2_ref, dz_ref, dgqa_ref, dgka_ref, dgqs_ref, dgks_ref):
        @pl.when(pl.program_id(0) == 0)
        def _():
            dgqa_ref[...] = jnp.zeros_like(dgqa_ref)
            dgka_ref[...] = jnp.zeros_like(dgka_ref)
            dgqs_ref[...] = jnp.zeros_like(dgqs_ref)
            dgks_ref[...] = jnp.zeros_like(dgks_ref)

        bd = bd_ref[...]

        def bwd(x, dy, g, bdm, dg_ref):
            x = x.astype(F32)
            r = lax.rsqrt(_group_mean(x * x, bdm) + EPS)
            xh = x * r
            dg_ref[...] += jnp.sum(dy * xh, axis=0, keepdims=True)
            dxn = dy * g
            return r * (dxn - xh * _group_mean(dxn * xh, bdm))

        dz_ref[:, 0:512] = bwd(zq_ref[...], dqa_ref[...] * QK_SCALE, gqa_ref[...], bd, dgqa_ref).astype(BF16)
        dz_ref[:, 512:1024] = bwd(zk_ref[...], dka_ref[...], gka_ref[...], bd, dgka_ref).astype(BF16)
        dz_ref[:, 1024:1536] = dva_ref[...].astype(BF16)
        dz_ref[:, 1536:2048] = bwd(zs_ref[...], dqs_ref[...] * QK_SCALE, gqs_ref[...], bd, dgqs_ref).astype(BF16)
        dkv = dkv_ref[...]
        dz_ref[:, 2048:2176] = bwd(zkv_ref[:, 0:128], dkv[:, 0:128], gks_ref[...], bd2_ref[...], dgks_ref).astype(BF16)
        dz_ref[:, 2176:2304] = dkv[:, 128:256].astype(BF16)

    s = z.shape[0]
    col = lambda j: pl.BlockSpec((tm, 512), lambda i, j=j: (i, j))
    t512 = pl.BlockSpec((tm, 512), lambda i: (i, 0))
    t256 = pl.BlockSpec((tm, 256), lambda i: (i, 0))
    vec = lambda w: pl.BlockSpec((1, w), lambda i: (0, 0))
    g512 = lambda g: jnp.tile(g.reshape(1, HD), (1, 8))
    return pl.pallas_call(
        body, name=name,
        out_shape=(_sds((s, ATT_W), BF16), _sds((1, 512), F32), _sds((1, 512), F32), _sds((1, 512), F32), _sds((1, 128), F32)),
        grid=(s // tm,),
        in_specs=[col(0), col(1), col(3), pl.BlockSpec((tm, 256), lambda i: (i, 8)), t512, t512, t512, t512, t256,
                  vec(512), vec(512), vec(512), vec(128), pl.BlockSpec((512, 512), lambda i: (0, 0)),
                  pl.BlockSpec((128, 128), lambda i: (0, 0))],
        out_specs=(pl.BlockSpec((tm, ATT_W), lambda i: (i, 0)), vec(512), vec(512), vec(512), vec(128)),
        compiler_params=_params(("arbitrary",)))(
            z, z, z, z, dqa, dka, dva, dqs, dkv, g512(gq_na), g512(gk_na), g512(gq_sw),
            jnp.tile(gk_sw.reshape(1, HD), (1, 2)), _block_diag(512), _block_diag(128))


def _na_row_start(r):
    return jnp.clip(r - NA_WR // 2, 0, ROWS - NA_WR)


def na_bias_table(rpb, *, name):
    t = jnp.pad(rpb, ((0, 0), (0, 2), (0, HD - (2 * NA_WC - 1))))
    pairs = jnp.concatenate([t[:, :16], t[:, 1:17]], axis=-1).reshape(NA_HEADS, 16, 1, 128)

    def body(t_ref, o_ref):
        p = pl.program_id(0)
        q = lax.broadcasted_iota(jnp.int32, (GRID_W, 128), 0)
        kc = lax.broadcasted_iota(jnp.int32, (GRID_W, 128), 1) & (GRID_W - 1)
        cs = jnp.clip(q - NA_WC // 2, 0, GRID_W - NA_WC)
        ok = (kc >= cs) & (kc < cs + NA_WC)
        for h in range(NA_HEADS):
            for pr in range(NA_WR // 2):
                x = jnp.broadcast_to(t_ref[h, 2 * pr - p + NA_WR - 1], (GRID_W, 128))
                b = pltpu.roll(x, 128 - (NA_WC - 1), 1, stride=1, stride_axis=0)
                o_ref[h, :, 128 * pr:128 * pr + 128] = jnp.where(ok, b, NEG)

    return pl.pallas_call(
        body, name=name, out_shape=_sds((NA_WR, NA_HEADS, GRID_W, NA_KEYS), F32), grid=(NA_WR,),
        in_specs=[pl.BlockSpec((NA_HEADS, 16, 1, 128), lambda p: (0, 0, 0, 0))],
        out_specs=pl.BlockSpec((None, NA_HEADS, GRID_W, NA_KEYS), lambda p: (p, 0, 0, 0)),
        compiler_params=_params(("parallel",)))(pairs)


def _lane_halves():
    lane = lax.broadcasted_iota(jnp.int32, (1, 128), 1)
    return lane < HD


def na_fwd(q, k, v, bias, *, name):
    def body(q_ref, k_ref, v_ref, b_ref, o_ref, lse_ref):
        r = pl.program_id(0)
        off = pl.multiple_of(_na_row_start(r) * GRID_W, GRID_W)
        first = _lane_halves()
        sels = [first, jnp.logical_not(first)]
        lanes = [slice(128 * j, 128 * j + 128) for j in range(NA_HEADS // 2)]
        q2s = [q_ref[:, l] for l in lanes]
        k2s = [k_ref[pl.ds(off, NA_KEYS), l] for l in lanes]
        v2s = [v_ref[pl.ds(off, NA_KEYS), l] for l in lanes]
        scores = []
        for h in range(NA_HEADS):
            j, half = divmod(h, 2)
            scores.append(_dot(jnp.where(sels[half], q2s[j], jnp.zeros_like(q2s[j])), k2s[j], NT))
        probs, lses = [], []
        for h in range(NA_HEADS):
            b = b_ref[h]
            s = jnp.where(b > 0.5 * NEG, scores[h] + b, NEG)
            m = jnp.max(s, axis=-1, keepdims=True)
            e = jnp.exp(s - m)
            l = jnp.sum(e, axis=-1, keepdims=True)
            probs.append((e / l).astype(BF16))
            lses.append(m + jnp.log(l))
        for j in range(NA_HEADS // 2):
            zero = jnp.zeros_like(v2s[j])
            o2 = (_dot(probs[2 * j], jnp.where(sels[0], v2s[j], zero))
                  + _dot(probs[2 * j + 1], jnp.where(sels[1], v2s[j], zero)))
            o_ref[:, lanes[j]] = o2.astype(BF16)
        lse_ref[...] = jnp.concatenate(lses, axis=1)

    s_tok = q.shape[0]
    full = pl.BlockSpec((s_tok, 512), lambda r: (0, 0))
    return pl.pallas_call(
        body, name=name, out_shape=(_sds((s_tok, 512), BF16), _sds((s_tok, NA_HEADS), F32)), grid=(ROWS,),
        in_specs=[pl.BlockSpec((GRID_W, 512), lambda r: (r, 0)), full, full,
                  pl.BlockSpec((None, NA_HEADS, GRID_W, NA_KEYS), lambda r: (r - _na_row_start(r), 0, 0, 0))],
        out_specs=(pl.BlockSpec((GRID_W, 512), lambda r: (r, 0)), pl.BlockSpec((GRID_W, NA_HEADS), lambda r: (r, 0))),
        compiler_params=_params(("parallel",)))(q, k, v, bias)


def na_bwd(q, k, v, o, do, lse, bias, *, name):
    def body(q_ref, k_ref, v_ref, o_ref, do_ref, lse_ref, b_ref, dq_ref, dk_ref, dv_ref, db_ref):
        r = pl.program_id(0)

        @pl.when(r == 0)
        def _():
            dk_ref[...] = jnp.zeros_like(dk_ref)
            dv_ref[...] = jnp.zeros_like(dv_ref)

        @pl.when((r <= NA_WR // 2) | (r > ROWS - NA_WR // 2))
        def _():
            db_ref[...] = jnp.zeros_like(db_ref)

        off = pl.multiple_of(_na_row_start(r) * GRID_W, GRID_W)
        first = _lane_halves()
        sels = [first, jnp.logical_not(first)]
        lanes = [slice(128 * j, 128 * j + 128) for j in range(NA_HEADS // 2)]
        q2s = [q_ref[:, l] for l in lanes]
        k2s = [k_ref[pl.ds(off, NA_KEYS), l] for l in lanes]
        v2s = [v_ref[pl.ds(off, NA_KEYS), l] for l in lanes]
        do2s = [do_ref[:, l] for l in lanes]
        prods = [do2s[j].astype(F32) * o_ref[:, lanes[j]].astype(F32) for j in range(NA_HEADS // 2)]
        lse = lse_ref[...]
        qhs, dohs, scores, dps = [], [], [], []
        for h in range(NA_HEADS):
            j, half = divmod(h, 2)
            qhs.append(jnp.where(sels[half], q2s[j], jnp.zeros_like(q2s[j])))
            dohs.append(jnp.where(sels[half], do2s[j], jnp.zeros_like(do2s[j])))
            scores.append(_dot(qhs[h], k2s[j], NT))
            dps.append(_dot(dohs[h], v2s[j], NT))
        pbs, dsbs = [], []
        for h in range(NA_HEADS):
            j, half = divmod(h, 2)
            b = b_ref[h]
            s = jnp.where(b > 0.5 * NEG, scores[h] + b, NEG)
            p = jnp.exp(s - lse[:, h:h + 1])
            delta = jnp.sum(jnp.where(sels[half], prods[j], 0.0), axis=-1, keepdims=True)
            ds = p * (dps[h] - delta)
            db_ref[h] += ds
            pbs.append(p.astype(BF16))
            dsbs.append(ds.astype(BF16))
        for j in range(NA_HEADS // 2):
            a, b = 2 * j, 2 * j + 1
            zero = jnp.zeros_like(k2s[j])
            dq_ref[:, lanes[j]] = (_dot(dsbs[a], jnp.where(sels[0], k2s[j], zero))
                                   + _dot(dsbs[b], jnp.where(sels[1], k2s[j], zero)))
            dk_ref[pl.ds(off, NA_KEYS), lanes[j]] += _dot(dsbs[a], qhs[a], TN) + _dot(dsbs[b], qhs[b], TN)
            dv_ref[pl.ds(off, NA_KEYS), lanes[j]] += _dot(pbs[a], dohs[a], TN) + _dot(pbs[b], dohs[b], TN)

    s_tok = q.shape[0]
    full = pl.BlockSpec((s_tok, 512), lambda r: (0, 0))
    row = pl.BlockSpec((GRID_W, 512), lambda r: (r, 0))
    bias_spec = pl.BlockSpec((None, NA_HEADS, GRID_W, NA_KEYS), lambda r: (r - _na_row_start(r), 0, 0, 0))
    return pl.pallas_call(
        body, name=name,
        out_shape=(_sds((s_tok, 512), F32), _sds((s_tok, 512), F32), _sds((s_tok, 512), F32),
                   _sds((NA_WR, NA_HEADS, GRID_W, NA_KEYS), F32)),
        grid=(ROWS,),
        in_specs=[row, full, full, row, row, pl.BlockSpec((GRID_W, NA_HEADS), lambda r: (r, 0)), bias_spec],
        out_specs=(row, full, full, bias_spec), compiler_params=_params(("arbitrary",)))(q, k, v, o, do, lse, bias)


def t5_bucket_map():
    rel = np.arange(SW_KEYS)[None, :] - SW_BLK - np.arange(SW_BLK)[:, None]
    nb = 16
    max_exact = nb // 2
    n = np.abs(rel)
    large = max_exact + (np.log(np.maximum(n, 1) / max_exact) / np.log(128 / max_exact) * (nb - max_exact)).astype(np.int32)
    large = np.minimum(large, nb - 1)
    return ((rel > 0) * nb + np.where(n < max_exact, n, large)).astype(np.int32)


def t5_bias(table, *, name):
    rel = np.arange(-SW_BLK, SW_BLK + 1)
    nb, max_exact = 16, 8
    n = np.abs(rel)
    large = max_exact + (np.log(np.maximum(n, 1) / max_exact) / np.log(128 / max_exact) * (nb - max_exact)).astype(np.int32)
    bucket = ((rel > 0) * nb + np.where(n < max_exact, n, np.minimum(large, nb - 1))).astype(np.int32)
    u = jnp.pad(table[jnp.asarray(bucket)].T, ((0, 0), (0, SW_KEYS - bucket.shape[0]))).reshape(8, 1, SW_KEYS)

    def body(u_ref, o_ref):
        for h in range(8):
            x = jnp.broadcast_to(u_ref[h], (SW_BLK, SW_KEYS))
            o_ref[h] = pltpu.roll(x, 0, 1, stride=1, stride_axis=0)

    return pl.pallas_call(body, name=name, out_shape=_sds((8, SW_BLK, SW_KEYS), F32), compiler_params=_params())(u)


def _sw_valid(n):
    a = lax.broadcasted_iota(jnp.int32, (SW_BLK, SW_KEYS), 0)
    j = lax.broadcasted_iota(jnp.int32, (SW_BLK, SW_KEYS), 1)
    kpos = (n - 1) * SW_BLK + j
    return (jnp.abs(j - SW_BLK - a) <= SW_BLK) & (kpos >= 0) & (kpos < SEQ)


def _dup_group(x2, g, first):
    rolled = pltpu.roll(x2, HD, 1)
    return jnp.where(first, x2, rolled) if g == 0 else jnp.where(first, rolled, x2)


def sw_fwd(q, kv, t5, sink, *, name):
    def body(q_ref, kv_ref, t5_ref, sink_ref, o_ref, lse_ref):
        n = pl.program_id(0)
        off = pl.multiple_of(n * SW_BLK, SW_BLK)
        first = _lane_halves()
        sels = [first, jnp.logical_not(first)]
        valid = _sw_valid(n)
        k2 = kv_ref[pl.ds(off, SW_KEYS), 0:128]
        v2 = kv_ref[pl.ds(off, SW_KEYS), 128:256]
        kk = [_dup_group(k2, g, first) for g in range(2)]
        vv = [_dup_group(v2, g, first) for g in range(2)]
        q2s = [q_ref[:, 128 * j:128 * j + 128] for j in range(4)]
        scores = []
        for h in range(8):
            j, half = divmod(h, 2)
            scores.append(_dot(jnp.where(sels[half], q2s[j], jnp.zeros_like(q2s[j])), kk[j // 2], NT))
        probs, lses = [], []
        for h in range(8):
            s = jnp.where(valid, scores[h] + t5_ref[h], NEG)
            snk = sink_ref[h]
            m = jnp.maximum(jnp.max(s, axis=-1, keepdims=True), snk)
            e = jnp.exp(s - m)
            den = jnp.sum(e, axis=-1, keepdims=True) + jnp.exp(snk - m)
            probs.append((e / den).astype(BF16))
            lses.append(m + jnp.log(den))
        outs = []
        for j in range(4):
            vg = vv[j // 2]
            zero = jnp.zeros_like(vg)
            outs.append(_dot(probs[2 * j], jnp.where(sels[0], vg, zero)) + _dot(probs[2 * j + 1], jnp.where(sels[1], vg, zero)))
        o_ref[...] = jnp.concatenate(outs, axis=1).astype(BF16)
        lse_ref[...] = jnp.concatenate(lses, axis=1)

    s_tok = q.shape[0]
    blk = pl.BlockSpec((SW_BLK, 512), lambda n: (n, 0))
    return pl.pallas_call(
        body, name=name, out_shape=(_sds((s_tok, 512), BF16), _sds((s_tok, 8), F32)), grid=(SW_NB,),
        in_specs=[blk, pl.BlockSpec(kv.shape, lambda n: (0, 0)), pl.BlockSpec((8, SW_BLK, SW_KEYS), lambda n: (0, 0, 0)),
                  pl.BlockSpec(memory_space=pltpu.SMEM)],
        out_specs=(blk, pl.BlockSpec((SW_BLK, 8), lambda n: (n, 0))), compiler_params=_params(("parallel",)))(q, kv, t5, sink)


def sw_bwd(q, kv, o, do, lse, t5, sink, *, name):
    def body(q_ref, kv_ref, o_ref, do_ref, lse_ref, t5_ref, sink_ref, dq_ref, dkv_ref, dt5_ref, dsink_ref):
        n = pl.program_id(0)

        @pl.when(n == 0)
        def _():
            dkv_ref[...] = jnp.zeros_like(dkv_ref)
            dt5_ref[...] = jnp.zeros_like(dt5_ref)
            dsink_ref[...] = jnp.zeros_like(dsink_ref)

        off = pl.multiple_of(n * SW_BLK, SW_BLK)
        first = _lane_halves()
        sels = [first, jnp.logical_not(first)]
        valid = _sw_valid(n)
        k2 = kv_ref[pl.ds(off, SW_KEYS), 0:128]
        v2 = kv_ref[pl.ds(off, SW_KEYS), 128:256]
        kk = [_dup_group(k2, g, first) for g in range(2)]
        vv = [_dup_group(v2, g, first) for g in range(2)]
        lanes = [slice(128 * j, 128 * j + 128) for j in range(4)]
        q2s = [q_ref[:, l] for l in lanes]
        do2s = [do_ref[:, l] for l in lanes]
        prods = [do2s[j].astype(F32) * o_ref[:, lanes[j]].astype(F32) for j in range(4)]
        lse = lse_ref[...]
        qhs, dohs, scores, dps = [], [], [], []
        for h in range(8):
            j, half = divmod(h, 2)
            qhs.append(jnp.where(sels[half], q2s[j], jnp.zeros_like(q2s[j])))
            dohs.append(jnp.where(sels[half], do2s[j], jnp.zeros_like(do2s[j])))
            scores.append(_dot(qhs[h], kk[j // 2], NT))
            dps.append(_dot(dohs[h], vv[j // 2], NT))
        pbs, dsbs, dss, dsinks = [], [], [], []
        for h in range(8):
            j, half = divmod(h, 2)
            s = jnp.where(valid, scores[h] + t5_ref[h], NEG)
            lse_h = lse[:, h:h + 1]
            p = jnp.exp(s - lse_h)
            delta = jnp.sum(jnp.where(sels[half], prods[j], 0.0), axis=-1, keepdims=True)
            ds = p * (dps[h] - delta)
            dss.append(ds)
            dsinks.append(-jnp.sum(jnp.exp(sink_ref[h] - lse_h) * delta, axis=0, keepdims=True))
            pbs.append(p.astype(BF16))
            dsbs.append(ds.astype(BF16))
        dt5_ref[...] += jnp.stack(dss)
        dsink_ref[...] += jnp.concatenate(dsinks, axis=1)
        dqs = []
        for j in range(4):
            a, b = 2 * j, 2 * j + 1
            zero = jnp.zeros_like(kk[j // 2])
            dqs.append(_dot(dsbs[a], jnp.where(sels[0], kk[j // 2], zero)) + _dot(dsbs[b], jnp.where(sels[1], kk[j // 2], zero)))
        dq_ref[...] = jnp.concatenate(dqs, axis=1)
        dk_groups, dv_groups = [], []
        for g in range(2):
            dkk = sum(_dot(dsbs[h], qhs[h], TN) for h in range(4 * g, 4 * g + 4))
            dvv = sum(_dot(pbs[h], dohs[h], TN) for h in range(4 * g, 4 * g + 4))
            dk_groups.append(dkk + pltpu.roll(dkk, HD, 1))
            dv_groups.append(dvv + pltpu.roll(dvv, HD, 1))
        dkv_ref[pl.ds(off, SW_KEYS), :] += jnp.concatenate(
            [jnp.where(first, dk_groups[0], dk_groups[1]), jnp.where(first, dv_groups[0], dv_groups[1])], axis=1)

    s_tok = q.shape[0]
    blk = pl.BlockSpec((SW_BLK, 512), lambda n: (n, 0))
    kv_spec = pl.BlockSpec(kv.shape, lambda n: (0, 0))
    t5_spec = pl.BlockSpec((8, SW_BLK, SW_KEYS), lambda n: (0, 0, 0))
    vec = pl.BlockSpec((1, 8), lambda n: (0, 0))
    return pl.pallas_call(
        body, name=name,
        out_shape=(_sds((s_tok, 512), F32), _sds(kv.shape, F32), _sds((8, SW_BLK, SW_KEYS), F32), _sds((1, 8), F32)),
        grid=(SW_NB,), in_specs=[blk, kv_spec, blk, blk, pl.BlockSpec((SW_BLK, 8), lambda n: (n, 0)), t5_spec,
                                 pl.BlockSpec(memory_space=pltpu.SMEM)],
        out_specs=(blk, kv_spec, t5_spec, vec), compiler_params=_params(("arbitrary",)))(q, kv, o, do, lse, t5, sink)


def gate_fwd(zg, bias, pa, ps, *, name, tm=512):
    def body(z0_ref, z1_ref, b0_ref, b1_ref, pa_ref, ps_ref, m_ref):
        g0 = jax.nn.sigmoid(z0_ref[...] + b0_ref[...])
        g1 = jax.nn.sigmoid(z1_ref[...] + b1_ref[...])
        m_ref[...] = (g0 * pa_ref[...] + g1 * ps_ref[...]).astype(BF16)

    s = zg.shape[0]
    half = lambda j: pl.BlockSpec((tm, DM), lambda i, j=j: (i, j))
    bvec = lambda j: pl.BlockSpec((1, DM), lambda i, j=j: (0, j))
    return pl.pallas_call(
        body, name=name, out_shape=_sds((s, DM), BF16), grid=(s // tm,),
        in_specs=[half(0), half(1), bvec(0), bvec(1), half(0), half(0)], out_specs=half(0),
        compiler_params=_params(("parallel",)))(zg, zg, bias, bias, pa, ps)


def gate_bwd(dm, zg, bias, pa, ps, *, name, tm=512):
    def body(dm_ref, z0_ref, z1_ref, b0_ref, b1_ref, pa_ref, ps_ref, dpa_ref, dps_ref, dz_ref, db_ref):
        @pl.when(pl.program_id(0) == 0)
        def _():
            db_ref[...] = jnp.zeros_like(db_ref)

        dm = dm_ref[...]
        g0 = jax.nn.sigmoid(z0_ref[...] + b0_ref[...])
        g1 = jax.nn.sigmoid(z1_ref[...] + b1_ref[...])
        dpa_ref[...] = (dm * g0).astype(BF16)
        dps_ref[...] = (dm * g1).astype(BF16)
        dz0 = dm * pa_ref[...] * g0 * (1.0 - g0)
        dz1 = dm * ps_ref[...] * g1 * (1.0 - g1)
        dz_ref[:, 0:DM] = dz0.astype(BF16)
        dz_ref[:, DM:2 * DM] = dz1.astype(BF16)
        db_ref[:, 0:DM] += jnp.sum(dz0, axis=0, keepdims=True)
        db_ref[:, DM:2 * DM] += jnp.sum(dz1, axis=0, keepdims=True)

    s = zg.shape[0]
    half = lambda j: pl.BlockSpec((tm, DM), lambda i, j=j: (i, j))
    bvec = lambda j: pl.BlockSpec((1, DM), lambda i, j=j: (0, j))
    return pl.pallas_call(
        body, name=name,
        out_shape=(_sds((s, DM), BF16), _sds((s, DM), BF16), _sds((s, GATE_W), BF16), _sds((1, GATE_W), F32)),
        grid=(s // tm,), in_specs=[half(0), half(0), half(1), bvec(0), bvec(1), half(0), half(0)],
        out_specs=(half(0), half(0), pl.BlockSpec((tm, GATE_W), lambda i: (i, 0)), pl.BlockSpec((1, GATE_W), lambda i: (0, 0))),
        compiler_params=_params(("arbitrary",)))(dm, zg, zg, bias, bias, pa, ps)


def loss_head(y, target, *, name, tm=512):
    def body(y_ref, t_ref, dy_ref, dyb_ref, l_ref):
        @pl.when(pl.program_id(0) == 0)
        def _():
            l_ref[...] = jnp.zeros_like(l_ref)

        err = y_ref[...] - t_ref[...]
        dy = err * (1.0 / DM)
        dy_ref[...] = dy
        dyb_ref[...] = dy.astype(BF16)
        l_ref[...] += 0.5 * jnp.sum(jnp.mean(err * err, axis=-1, keepdims=True), axis=0, keepdims=True)

    s = y.shape[0]
    tile = pl.BlockSpec((tm, DM), lambda i: (i, 0))
    return pl.pallas_call(
        body, name=name, out_shape=(_sds((s, DM), F32), _sds((s, DM), BF16), _sds((1, 128), F32)), grid=(s // tm,),
        in_specs=[tile, tile], out_specs=(tile, tile, pl.BlockSpec((1, 128), lambda i: (0, 0))),
        compiler_params=_params(("arbitrary",)))(y, target)


def adamw(w, g, m, v, *, name):
    def body(w_ref, g_ref, m_ref, v_ref, d_ref, nm_ref, nv_ref):
        g = g_ref[...]
        nm = ADAM_B1 * m_ref[...] + (1.0 - ADAM_B1) * g
        nv = ADAM_B2 * v_ref[...] + (1.0 - ADAM_B2) * jnp.square(g)
        m_hat = nm / (1.0 - ADAM_B1 ** ADAM_STEP)
        v_hat = nv / (1.0 - ADAM_B2 ** ADAM_STEP)
        d_ref[...] = -ADAM_LR * (m_hat / (jnp.sqrt(v_hat) + ADAM_EPS) + ADAM_WD * w_ref[...])
        nm_ref[...] = nm
        nv_ref[...] = nv

    b, k, n = w.shape
    tk = k // 4 if k % 32 == 0 else k
    spec = pl.BlockSpec((None, tk, n), lambda i, j: (i, j, 0))
    out = _sds(w.shape, F32)
    return pl.pallas_call(
        body, name=name, out_shape=(out, out, out), grid=(b, k // tk), in_specs=[spec] * 4, out_specs=(spec,) * 3,
        compiler_params=_params(("parallel", "parallel")))(w, g, m, v)


def adamw_layer(ws, ms, vs, mines, theirs, cidx, layer, filled=None, *, name):
    cnt = len(ws)
    _, k, n = ws[0].shape
    nt = 2
    tk = k // 2 // nt

    def body(c_ref, *refs):
        own = pl.program_id(0) == c_ref[0]
        outs = refs[-4 * cnt:]
        for i in range(cnt):
            w_ref, m_ref, v_ref, a_ref, b_ref = refs[5 * i:5 * i + 5]
            g_ref, d_ref, nm_ref, nv_ref = outs[4 * i:4 * i + 4]
            g = jnp.where(own, a_ref[...], b_ref[...])
            g_ref[...] = g
            nm = ADAM_B1 * m_ref[...] + (1.0 - ADAM_B1) * g
            nv = ADAM_B2 * v_ref[...] + (1.0 - ADAM_B2) * jnp.square(g)
            m_hat = nm / (1.0 - ADAM_B1 ** ADAM_STEP)
            v_hat = nv / (1.0 - ADAM_B2 ** ADAM_STEP)
            d_ref[...] = -ADAM_LR * (m_hat / (jnp.sqrt(v_hat) + ADAM_EPS) + ADAM_WD * w_ref[...])
            nm_ref[...] = nm
            nv_ref[...] = nv

    full = pl.BlockSpec((None, tk, n), lambda hf, t, c: (layer, hf * nt + t, 0))
    half_mine = pl.BlockSpec((tk, n), lambda hf, t, c: (jnp.where(hf == c[0], t, 0), 0))
    half_theirs = pl.BlockSpec((tk, n), lambda hf, t, c: (jnp.where(hf != c[0], t, 0), 0))
    out = _sds(ws[0].shape, F32)
    ins, specs, aliases = [cidx], [], {}
    for i in range(cnt):
        ins += [ws[i], ms[i], vs[i], mines[i], theirs[i]]
        specs += [full, full, full, half_mine, half_theirs]
    if filled is not None:
        aliases = {len(ins) + j: j for j in range(4 * cnt)}
        ins += [a for f in filled for a in f]
        specs += [pl.BlockSpec(memory_space=pl.ANY)] * (4 * cnt)
    res = pl.pallas_call(
        body, name=name, out_shape=(out,) * (4 * cnt),
        grid_spec=pltpu.PrefetchScalarGridSpec(
            num_scalar_prefetch=1, grid=(2, nt), in_specs=specs, out_specs=(full,) * (4 * cnt)),
        input_output_aliases=aliases,
        compiler_params=_params(("arbitrary", "arbitrary")))(*ins)
    return [tuple(res[4 * i:4 * i + 4]) for i in range(cnt)]


def t5_table_grad(dt5_a, dt5_b, *, name):
    def body(a_ref, b_ref, map_ref, o_ref):
        d = a_ref[...] + b_ref[...]
        bucket = map_ref[...]
        for b in range(32):
            hit = (bucket == b)[None]
            o_ref[b] = jnp.sum(jnp.sum(jnp.where(hit, d, 0.0), axis=2), axis=1, keepdims=True)

    return pl.pallas_call(
        body, name=name, out_shape=_sds((32, 8, 1), F32), compiler_params=_params())(
            dt5_a, dt5_b, jnp.asarray(t5_bucket_map()))


def rpb_grad(dbias, *, name):
    def body(d_ref, rev_ref, o_ref):
        rev = rev_ref[...]
        for h in range(NA_HEADS):
            for pr in range(NA_WR // 2):
                d = d_ref[h, :, 128 * pr:128 * pr + 128]
                hi = d.astype(BF16)
                lo = (d - hi.astype(F32)).astype(BF16)
                flipped = _dot(rev, hi) + _dot(rev, lo)
                o_ref[h, pr] = jnp.sum(pltpu.roll(flipped, 0, 1, stride=1, stride_axis=0), axis=0, keepdims=True)

    anti = jnp.asarray(np.eye(GRID_W, dtype=np.float32)[::-1], dtype=BF16)
    e = pl.pallas_call(
        body, name=name, out_shape=_sds((NA_WR, NA_HEADS, NA_WR // 2, 1, 128), F32), grid=(NA_WR,),
        in_specs=[pl.BlockSpec((None, NA_HEADS, GRID_W, NA_KEYS), lambda p: (p, 0, 0, 0)),
                  pl.BlockSpec((GRID_W, GRID_W), lambda p: (0, 0))],
        out_specs=pl.BlockSpec((None, NA_HEADS, NA_WR // 2, 1, 128), lambda p: (p, 0, 0, 0, 0)),
        compiler_params=_params(("parallel",)))(dbias, anti)
    nci, nri = 2 * NA_WC - 1, 2 * NA_WR - 1
    e = e.reshape(NA_WR, NA_HEADS, NA_WR // 2, 128).transpose(0, 2, 1, 3).reshape(NA_WR * NA_WR // 2, NA_HEADS, 128)
    parts = jnp.concatenate([e[..., 48:48 + nci], jnp.concatenate([e[..., 112:128], e[..., 0:nci - 16]], axis=-1)], axis=0)
    p, pr = np.arange(NA_WR)[:, None], np.arange(NA_WR // 2)[None, :]
    ri = np.concatenate([(2 * pr - p + NA_WR - 1).reshape(-1), (2 * pr - p + NA_WR).reshape(-1)])
    pick = jnp.asarray((ri[None, :] == np.arange(16)[:, None]).astype(np.float32))
    out = mm(pick, parts.reshape(2 * NA_WR * NA_WR // 2, NA_HEADS * nci), name=name + "_rows", exact=True)
    return out.reshape(16, NA_HEADS, nci)[:nri].transpose(1, 0, 2)


BIG = ("ffn1_w_gate", "ffn1_w_up", "ffn1_w_down", "w_in", "w_branch_na", "w_branch_sw", "w_out",
       "ffn2_w_gate", "ffn2_w_up", "ffn2_w_down")
SMALL = ("ffn1_norm", "mix_norm", "b_gate", "na_q_norm", "na_k_norm", "na_rpb", "sw_q_norm", "sw_k_norm", "sw_sink",
         "ffn2_norm")


def _cols_to_full(w4):
    return w4.transpose(1, 0, 2).reshape(w4.shape[1], NSH * w4.shape[2])


def _full_to_cols(w):
    return w.reshape(w.shape[0], NSH, w.shape[1] // NSH).transpose(1, 0, 2)


def _mixer_weights(g):
    w_in_t = g["w_in"].reshape(IN_W, DM)
    return dict(w_att_t=w_in_t[:ATT_W], w_gz_t=w_in_t[ATT_W:], wa=_cols_to_full(g["w_branch_na"]),
                ws=_cols_to_full(g["w_branch_sw"]), wo=g["w_out"].reshape(DM, DM))


GROUPS = {"ffn1": ("ffn1_w_gate", "ffn1_w_up", "ffn1_w_down"), "mix": ("w_in", "w_branch_na", "w_branch_sw", "w_out"),
          "ffn2": ("ffn2_w_gate", "ffn2_w_up", "ffn2_w_down")}


def layer_fwd(x, p, weights, t5b):
    row = lambda v: v.reshape(1, -1)
    stacked = lambda g: {n: a.reshape(DFF, DM) for n, a in g.items()}
    g1 = stacked(weights("ffn1", x))
    y1, h1, gg1, uu1 = ffn_fwd(x, row(p["ffn1_norm"]), g1["ffn1_w_gate"], g1["ffn1_w_up"], g1["ffn1_w_down"], name="ffn_fwd")
    w = _mixer_weights(weights("mix", y1))
    hm = rms_fwd(y1, row(p["mix_norm"]), name="mix_norm_fwd")
    z = mm(hm, w["w_att_t"], tb=True, out_dtype=BF16, name="proj_att", tm=SEQ, tn=768)
    zg = mm(hm, w["w_gz_t"], tb=True, out_dtype=BF16, name="proj_gate", tm=SEQ, tn=512)
    qa, ka, va, qs, kv = qknorm_fwd(z, p["na_q_norm"], p["na_k_norm"], p["sw_q_norm"], p["sw_k_norm"], name="qknorm_fwd")
    bias = p["na_bias"]
    o_na, lse_na = na_fwd(qa, ka, va, bias, name="na_fwd")
    kvp = jnp.pad(kv, ((SW_BLK, SW_BLK), (0, 0)))
    sink = p["sw_sink"]
    o_sw, lse_sw = sw_fwd(qs, kvp, t5b, sink, name="sw_fwd")
    pa = mm(o_na, w["wa"], out_dtype=BF16, name="branch_na", tm=1024)
    ps = mm(o_sw, w["ws"], out_dtype=BF16, name="branch_sw", tm=1024)
    merged = gate_fwd(zg, row(p["b_gate"]), pa, ps, name="gate_fwd")
    y2 = mm(merged, w["wo"], add=y1, name="out_proj", tm=1024)
    g2 = stacked(weights("ffn2", y2))
    y3, h2, gg2, uu2 = ffn_fwd(y2, row(p["ffn2_norm"]), g2["ffn2_w_gate"], g2["ffn2_w_up"], g2["ffn2_w_down"], name="ffn_fwd")
    saved = dict(x=x, y1=y1, h1=h1, gg1=gg1, uu1=uu1, hm=hm, z=z, zg=zg, qa=qa, ka=ka, va=va, qs=qs, kvp=kvp, bias=bias,
                 o_na=o_na, lse_na=lse_na, o_sw=o_sw, lse_sw=lse_sw, pa=pa, ps=ps, merged=merged, y2=y2, h2=h2, gg2=gg2,
                 uu2=uu2, w=w, sink=sink, g1=g1, g2=g2)
    return y3, saved


def layer_bwd(dy3, dy3_bf, sv, p, t5b, emit, dep=None):
    w, g1, g2 = sv["w"], sv["g1"], sv["g2"]
    row = lambda v: v.reshape(1, -1)
    fold = lambda v: v.reshape(-1, HD).sum(axis=0)
    small = {}
    dy2, _, small["ffn2_norm"], act, dg, du = ffn_bwd_tokens(
        dy3, sv["y2"], row(p["ffn2_norm"]), sv["gg2"], sv["uu2"], g2["ffn2_w_gate"], g2["ffn2_w_up"], g2["ffn2_w_down"],
        name="ffn_bwd_tokens", dep=dep)
    shards = lambda gs: [g.reshape(NSH, FSH, DM) for g in gs]
    token = emit("ffn2", shards(ffn_bwd_weights(sv["h2"], dy3_bf, act, dg, du, name="ffn_bwd_weights")))
    dmerged = mm(dy2, w["wo"], tb=True, name="out_proj_dx", tm=1024, dep=token)
    gw_out = mm(sv["merged"], dy2, ta=True, out_dtype=BF16, name="out_proj_dw").reshape(NSH, DM // NSH, DM)
    dpa, dps, dzg, small["b_gate"] = gate_bwd(dmerged, sv["zg"], row(p["b_gate"]), sv["pa"], sv["ps"], name="gate_bwd")
    gw_na = _full_to_cols(mm(sv["o_na"], dpa, ta=True, out_dtype=BF16, name="branch_dw"))
    gw_sw = _full_to_cols(mm(sv["o_sw"], dps, ta=True, out_dtype=BF16, name="branch_dw"))
    do_na = mm(dpa, w["wa"], tb=True, out_dtype=BF16, tm=SEQ, name="branch_dx")
    do_sw = mm(dps, w["ws"], tb=True, out_dtype=BF16, tm=SEQ, name="branch_dx")
    dqa, dka, dva, dbias = na_bwd(sv["qa"], sv["ka"], sv["va"], sv["o_na"], do_na, sv["lse_na"], sv["bias"], name="na_bwd")
    dqs, dkvp, dt5, dsink = sw_bwd(sv["qs"], sv["kvp"], sv["o_sw"], do_sw, sv["lse_sw"], t5b, sv["sink"], name="sw_bwd")
    dkv = dkvp[SW_BLK:SW_BLK + SEQ]
    dz, dgqa, dgka, dgqs, dgks = qknorm_bwd(sv["z"], dqa, dka, dva, dqs, dkv, p["na_q_norm"], p["na_k_norm"],
                                            p["sw_q_norm"], p["sw_k_norm"], name="qknorm_bwd")
    small["na_q_norm"], small["na_k_norm"], small["sw_q_norm"], small["sw_k_norm"] = fold(dgqa), fold(dgka), fold(dgqs), fold(dgks)
    small["na_rpb"] = rpb_grad(dbias, name="rpb_grad")
    small["sw_sink"] = dsink
    gw_att_t = mm(dz, sv["hm"], ta=True, out_dtype=BF16, tm=768, name="proj_att_dw")
    gw_gz_t = mm(dzg, sv["hm"], ta=True, out_dtype=BF16, tm=1024, name="proj_gate_dw")
    gw_in = jnp.concatenate([gw_att_t, gw_gz_t], axis=0).reshape(NSH, IN_W // NSH, DM)
    token = emit("mix", (gw_in, gw_na, gw_sw, gw_out))
    dh = mm(dz, w["w_att_t"], tm=1024, name="proj_att_dx", dep=token)
    dh = mm(dzg, w["w_gz_t"], add=dh, tm=1024, name="proj_gate_dx")
    dy1, dy1_bf, small["mix_norm"] = rms_bwd(dh, sv["y1"], row(p["mix_norm"]), dy2, name="mix_norm_bwd")
    dx, dx_bf, small["ffn1_norm"], act, dg, du = ffn_bwd_tokens(
        dy1, sv["x"], row(p["ffn1_norm"]), sv["gg1"], sv["uu1"], g1["ffn1_w_gate"], g1["ffn1_w_up"], g1["ffn1_w_down"],
        name="ffn_bwd_tokens")
    emit("ffn1", shards(ffn_bwd_weights(sv["h1"], dy1_bf, act, dg, du, name="ffn_bwd_weights")))
    return dx, dx_bf, small, dt5


ANY = pl.BlockSpec(memory_space=pl.ANY)


def _place():
    x, y, c = lax.axis_index("x"), lax.axis_index("y"), lax.axis_index("c")
    chips = [(1 - x, y), (x, 1 - y), (1 - x, 1 - y)]
    return x, y, c, chips


def _remote(src, dst, send_sem, recv_sem, to):
    return pltpu.make_async_remote_copy(src_ref=src, dst_ref=dst, send_sem=send_sem, recv_sem=recv_sem, device_id=to,
                                        device_id_type=MESH)


HBM = pl.BlockSpec(memory_space=pltpu.HBM)
SEM = pl.BlockSpec(memory_space=pltpu.SEMAPHORE)
ORDERED_EFFECT = pltpu.SideEffectType.DATAFLOW_SIDE_EFFECTING


def _in_hbm(v):
    return pltpu.with_memory_space_constraint(v, pltpu.HBM)


def _row_half(ref_shape_rows, c):
    half = ref_shape_rows // 2
    return pl.ds(c * half, half)


def _ici_gather_copies(w, land, send_sems, recv_sems):
    x, y, c, chips = _place()
    me = 2 * x + y
    copies = []
    for a in range(len(w)):
        rows = _row_half(w[a].shape[0], c)
        for k, chip in enumerate(chips):
            copies.append(_remote(w[a].at[rows], land[a].at[me, rows], send_sems.at[4 * a + k], recv_sems.at[4 * a + k],
                                  (*chip, c)))
        copies.append(_remote(w[a], land[a].at[me], send_sems.at[4 * a + 3], recv_sems.at[4 * a + 3], (x, y, 1 - c)))
    return copies


def _d2d_gather_copies(w, land, send_sems, recv_sems):
    x, y, c, chips = _place()
    copies = []
    for a in range(len(w)):
        rows = _row_half(w[a].shape[0], c)
        for k, (cx, cy) in enumerate(chips):
            blk = land[a].at[2 * cx + cy, rows]
            copies.append(_remote(blk, blk, send_sems.at[3 * a + k], recv_sems.at[3 * a + k], (x, y, 1 - c)))
    return copies


def _d2d_gather_waits(w, land, send_sems, recv_sems):
    x, y, c, chips = _place()
    waits = []
    for a in range(len(w)):
        rows = _row_half(w[a].shape[0], 1 - c)
        for k, (cx, cy) in enumerate(chips):
            blk = land[a].at[2 * cx + cy, rows]
            waits.append(_remote(blk, blk, send_sems.at[3 * a + k], recv_sems.at[3 * a + k], (x, y, 1 - c)))
    return waits


def gather_start(groups, dep=None, *, name):
    sizes = [len(g) for g in groups]
    shards = [s for g in groups for s in g]
    n, ng = len(shards), len(groups)
    extra = [] if dep is None else [dep]

    def body(*refs):
        first_out = 2 * n + len(extra)
        w, land, sems = refs[:n], refs[n:2 * n], refs[first_out:first_out + 2 * ng]
        off = 0
        for gi, size in enumerate(sizes):
            for cp in _ici_gather_copies(w[off:off + size], land[off:off + size], sems[2 * gi], sems[2 * gi + 1]):
                cp.start()
            off += size

    lands = [lax.empty((NSH,) + s.shape, s.dtype) for s in shards]
    sem_shapes = tuple(pltpu.SemaphoreType.DMA((4 * size,)) for size in sizes for _ in range(2))
    res = pl.pallas_call(
        body, name=name,
        out_shape=sem_shapes + tuple(pltpu.HBM(s.shape, s.dtype) for s in shards) + tuple(pltpu.HBM(l.shape, l.dtype) for l in lands),
        in_specs=[HBM] * (2 * n) + [ANY] * len(extra), out_specs=(SEM,) * (2 * ng) + (HBM,) * (2 * n),
        input_output_aliases={i: 2 * ng + i for i in range(2 * n)},
        compiler_params=pltpu.CompilerParams(has_side_effects=ORDERED_EFFECT))(
            *[_in_hbm(s) for s in shards], *[_in_hbm(l) for l in lands], *extra)
    out, off = [], 0
    for gi, size in enumerate(sizes):
        out.append((res[2 * gi], res[2 * gi + 1], list(res[2 * ng + off:2 * ng + off + size]),
                    list(res[2 * ng + n + off:2 * ng + n + off + size])))
        off += size
    return out


def gather_wait(send_sems, recv_sems, shards, lands, after, *, name):
    n = len(shards)

    def body(*refs):
        w, land = refs[:n], refs[n:2 * n]
        send, recv = refs[2 * n:2 * n + 2]
        for cp in _ici_gather_copies(w, land, send, recv):
            cp.wait_send()
            cp.wait_recv()

    res = pl.pallas_call(
        body, name=name,
        out_shape=tuple(pltpu.HBM(s.shape, s.dtype) for s in shards) + tuple(pltpu.HBM(l.shape, l.dtype) for l in lands),
        in_specs=[HBM] * (2 * n) + [SEM, SEM] + [ANY] * len(after), out_specs=(HBM,) * (2 * n),
        input_output_aliases={i: i for i in range(2 * n)},
        compiler_params=pltpu.CompilerParams(has_side_effects=ORDERED_EFFECT))(*shards, *lands, send_sems, recv_sems, *after)
    return list(res[:n]), list(res[n:])


def gather_finish(shards, lands, *, name):
    n = len(shards)

    def body(*refs):
        w, land = refs[:n], refs[n:2 * n]
        send_sems, recv_sems = refs[3 * n:]
        d2d = _d2d_gather_copies(w, land, send_sems, recv_sems)
        for cp in d2d:
            cp.start()
        for cp in _d2d_gather_waits(w, land, send_sems, recv_sems):
            cp.wait_recv()
        for cp in d2d:
            cp.wait_send()

    return list(pl.pallas_call(
        body, name=name, out_shape=tuple(pltpu.HBM(l.shape, l.dtype) for l in lands),
        in_specs=[ANY] * (2 * n), out_specs=tuple([ANY] * n), input_output_aliases={n + i: i for i in range(n)},
        scratch_shapes=[pltpu.SemaphoreType.DMA((3 * n,)), pltpu.SemaphoreType.DMA((3 * n,))])(*shards, *lands))


def _pair_exchange_copies(g, buf, send_sems, recv_sems):
    x, y, c, _ = _place()
    copies = []
    for a in range(len(g)):
        half = g[a].shape[1] // 2
        copies.append(_remote(g[a].at[:, pl.ds((1 - c) * half, half)], buf[a], send_sems.at[a], recv_sems.at[a], (x, y, 1 - c)))
    return copies


def pair_exchange_start(grads, dep=None, *, name):
    n = len(grads)
    extra = [] if dep is None else [dep]

    def body(*refs):
        sems = refs[2 * n + len(extra):]
        for cp in _pair_exchange_copies(refs[:n], refs[n:2 * n], sems[0], sems[1]):
            cp.start()
        refs[-1][...] = jnp.zeros_like(refs[-1])

    lands = [lax.empty((NSH, g.shape[1] // 2, g.shape[2]), g.dtype) for g in grads]
    res = pl.pallas_call(
        body, name=name,
        out_shape=(pltpu.SemaphoreType.DMA((n,)), pltpu.SemaphoreType.DMA((n,)))
        + tuple(pltpu.HBM(g.shape, g.dtype) for g in grads) + tuple(pltpu.HBM(l.shape, l.dtype) for l in lands)
        + (_sds((8, 128), F32),),
        in_specs=[HBM] * (2 * n) + [ANY] * len(extra),
        out_specs=(SEM, SEM) + (HBM,) * (2 * n) + (pl.BlockSpec(memory_space=pltpu.VMEM),),
        input_output_aliases={i: 2 + i for i in range(2 * n)},
        compiler_params=pltpu.CompilerParams(has_side_effects=ORDERED_EFFECT))(
            *[_in_hbm(g) for g in grads], *[_in_hbm(l) for l in lands], *extra)
    return res[0], res[1], list(res[2:2 + n]), list(res[2 + n:2 + 2 * n]), res[-1]


def pair_exchange_wait(send_sems, recv_sems, grads, lands, after, *, name):
    n = len(grads)

    def body(*refs):
        for cp in _pair_exchange_copies(refs[:n], refs[n:2 * n], refs[2 * n], refs[2 * n + 1]):
            cp.wait_send()
            cp.wait_recv()

    res = pl.pallas_call(
        body, name=name,
        out_shape=tuple(pltpu.HBM(g.shape, g.dtype) for g in grads) + tuple(pltpu.HBM(l.shape, l.dtype) for l in lands),
        in_specs=[HBM] * (2 * n) + [SEM, SEM] + [ANY] * len(after), out_specs=(HBM,) * (2 * n),
        input_output_aliases={i: i for i in range(2 * n)},
        compiler_params=pltpu.CompilerParams(has_side_effects=ORDERED_EFFECT))(*grads, *lands, send_sems, recv_sems, *after)
    return list(res[:n]), list(res[n:])


def _chip_exchange_copies(s, buf, send_sems, recv_sems):
    x, y, c, chips = _place()
    return [_remote(s[a].at[2 * cx + cy], buf[a].at[k], send_sems.at[3 * a + k], recv_sems.at[3 * a + k], (cx, cy, c))
            for a in range(len(s)) for k, (cx, cy) in enumerate(chips)]


def chip_exchange(sums, *, name):
    n = len(sums)

    def body(*refs):
        copies = _chip_exchange_copies(refs[:n], refs[n:2 * n], *refs[2 * n:])
        for cp in copies:
            cp.start()
        for cp in copies:
            cp.wait()

    return pl.pallas_call(
        body, name=name, out_shape=tuple(pltpu.HBM((3,) + s.shape[1:], s.dtype) for s in sums),
        in_specs=[ANY] * n, out_specs=tuple([ANY] * n),
        scratch_shapes=[pltpu.SemaphoreType.DMA((3 * n,)), pltpu.SemaphoreType.DMA((3 * n,))])(*sums)


def chip_exchange_start(sums, *, name):
    n = len(sums)

    def body(*refs):
        for cp in _chip_exchange_copies(refs[:n], refs[n:2 * n], refs[2 * n], refs[2 * n + 1]):
            cp.start()
        refs[-1][...] = jnp.zeros_like(refs[-1])

    lands = [lax.empty((3,) + s.shape[1:], s.dtype) for s in sums]
    res = pl.pallas_call(
        body, name=name,
        out_shape=(pltpu.SemaphoreType.DMA((3 * n,)), pltpu.SemaphoreType.DMA((3 * n,)))
        + tuple(pltpu.HBM(s.shape, s.dtype) for s in sums) + tuple(pltpu.HBM(l.shape, l.dtype) for l in lands)
        + (_sds((8, 128), F32),),
        in_specs=[HBM] * (2 * n), out_specs=(SEM, SEM) + (HBM,) * (2 * n) + (pl.BlockSpec(memory_space=pltpu.VMEM),),
        input_output_aliases={i: 2 + i for i in range(2 * n)},
        compiler_params=pltpu.CompilerParams(has_side_effects=ORDERED_EFFECT))(
            *[_in_hbm(s) for s in sums], *[_in_hbm(l) for l in lands])
    return res[0], res[1], list(res[2:2 + n]), list(res[2 + n:2 + 2 * n]), res[-1]


def chip_exchange_wait(send_sems, recv_sems, sums, lands, after, *, name):
    n = len(sums)

    def body(*refs):
        for cp in _chip_exchange_copies(refs[:n], refs[n:2 * n], refs[2 * n], refs[2 * n + 1]):
            cp.wait_send()
            cp.wait_recv()

    res = pl.pallas_call(
        body, name=name,
        out_shape=tuple(pltpu.HBM(s.shape, s.dtype) for s in sums) + tuple(pltpu.HBM(l.shape, l.dtype) for l in lands),
        in_specs=[HBM] * (2 * n) + [SEM, SEM] + [ANY] * len(after), out_specs=(HBM,) * (2 * n),
        input_output_aliases={i: i for i in range(2 * n)},
        compiler_params=pltpu.CompilerParams(has_side_effects=ORDERED_EFFECT))(*sums, *lands, send_sems, recv_sems, *after)
    return list(res[:n]), list(res[n:])


def _pair_send_copies(h, got, send_sems, recv_sems):
    x, y, c, _ = _place()
    return [_remote(h[i], got[i], send_sems.at[i], recv_sems.at[i], (x, y, 1 - c)) for i in range(len(h))]


def pair_send_start(halves, *, name):
    n = len(halves)

    def body(*refs):
        for cp in _pair_send_copies(refs[:n], refs[n:2 * n], refs[2 * n], refs[2 * n + 1]):
            cp.start()
        refs[-1][...] = jnp.zeros_like(refs[-1])

    lands = [lax.empty(h.shape, h.dtype) for h in halves]
    res = pl.pallas_call(
        body, name=name,
        out_shape=(pltpu.SemaphoreType.DMA((n,)), pltpu.SemaphoreType.DMA((n,)))
        + tuple(pltpu.HBM(h.shape, h.dtype) for h in halves) * 2 + (_sds((8, 128), F32),),
        in_specs=[HBM] * (2 * n), out_specs=(SEM, SEM) + (HBM,) * (2 * n) + (pl.BlockSpec(memory_space=pltpu.VMEM),),
        input_output_aliases={i: 2 + i for i in range(2 * n)},
        compiler_params=pltpu.CompilerParams(has_side_effects=ORDERED_EFFECT))(
            *[_in_hbm(h) for h in halves], *[_in_hbm(l) for l in lands])
    return res[0], res[1], list(res[2:2 + n]), list(res[2 + n:2 + 2 * n]), res[-1]


def pair_send_wait(send_sems, recv_sems, halves, lands, after, *, name):
    n = len(halves)

    def body(*refs):
        for cp in _pair_send_copies(refs[:n], refs[n:2 * n], refs[2 * n], refs[2 * n + 1]):
            cp.wait_send()
            cp.wait_recv()

    res = pl.pallas_call(
        body, name=name, out_shape=tuple(pltpu.HBM(h.shape, h.dtype) for h in halves) * 2,
        in_specs=[HBM] * (2 * n) + [SEM, SEM] + [ANY] * len(after), out_specs=(HBM,) * (2 * n),
        input_output_aliases={i: i for i in range(2 * n)},
        compiler_params=pltpu.CompilerParams(has_side_effects=ORDERED_EFFECT))(*halves, *lands, send_sems, recv_sems, *after)
    return list(res[:n]), list(res[n:])


def allreduce_small(v, *, name):
    rows = v.shape[0]

    def body(v_ref, o_ref, gath, send_sems, recv_sems):
        x, y, c, _ = _place()
        me = 4 * x + 2 * y + c
        gath[me] = v_ref[...]
        copies = []
        for k in range(1, 8):
            fx, fy, fc = (k >> 2) & 1, (k >> 1) & 1, k & 1
            peer = (jnp.where(fx, 1 - x, x), jnp.where(fy, 1 - y, y), jnp.where(fc, 1 - c, c))
            cp = _remote(v_ref, gath.at[me], send_sems.at[k - 1], recv_sems.at[k - 1], peer)
            cp.start()
            copies.append(cp)
        for cp in copies:
            cp.wait()
        acc = gath[0]
        for d in range(1, 8):
            acc = acc + gath[d]
        o_ref[...] = acc

    return pl.pallas_call(
        body, name=name, out_shape=_sds(v.shape, F32),
        in_specs=[pl.BlockSpec(memory_space=pltpu.VMEM)], out_specs=pl.BlockSpec(memory_space=pltpu.VMEM),
        scratch_shapes=[pltpu.VMEM((8, rows, 128), F32), pltpu.SemaphoreType.DMA((7,)), pltpu.SemaphoreType.DMA((7,))])(v)


def _same_shape_runs(arrays):
    runs = {}
    for i, a in enumerate(arrays):
        runs.setdefault(a.shape, []).append(i)
    return list(runs.values())


def _per_shape(fn, *lists):
    out = [None] * len(lists[0])
    for idx in _same_shape_runs(lists[0]):
        for i, r in zip(idx, fn(*[[l[i] for i in idx] for l in lists])):
            out[i] = r
    return out


def add_halves(gs, bufs, cidx, *, name):
    cnt = len(gs)
    _, k, n = gs[0].shape

    def body(c_ref, *refs):
        g, b, o = refs[:cnt], refs[cnt:2 * cnt], refs[2 * cnt:]
        for i in range(cnt):
            o[i][...] = (g[i][...].astype(F32) + b[i][...].astype(F32)).astype(BF16)

    blk = pl.BlockSpec((None, k // 2, n), lambda s, c: (s, 0, 0))
    mine = pl.BlockSpec((None, k // 2, n), lambda s, c: (s, c[0], 0))
    return list(pl.pallas_call(
        body, name=name, out_shape=tuple(_sds(b.shape, BF16) for b in bufs),
        grid_spec=pltpu.PrefetchScalarGridSpec(
            num_scalar_prefetch=1, grid=(NSH,), in_specs=[mine] * cnt + [blk] * cnt, out_specs=tuple([blk] * cnt)),
        compiler_params=_params(("parallel",)))(cidx, *gs, *bufs))


def add_chips(sums, bufs, sidx, *, name):
    cnt = len(sums)
    _, kh, n = sums[0].shape

    def body(s_ref, *refs):
        mine, b, o = refs[:cnt], refs[cnt:2 * cnt], refs[2 * cnt:]
        for i in range(cnt):
            o[i][...] = ((mine[i][...].astype(F32) + b[i][0].astype(F32)) + (b[i][1].astype(F32) + b[i][2].astype(F32)))

    own = pl.BlockSpec((None, kh, n), lambda i, s: (s[0], 0, 0))
    got = pl.BlockSpec((3, kh, n), lambda i, s: (0, 0, 0))
    out = pl.BlockSpec((kh, n), lambda i, s: (0, 0))
    return list(pl.pallas_call(
        body, name=name, out_shape=tuple(_sds((kh, n), F32) for _ in sums),
        grid_spec=pltpu.PrefetchScalarGridSpec(
            num_scalar_prefetch=1, grid=(1,), in_specs=[own] * cnt + [got] * cnt, out_specs=tuple([out] * cnt)),
        compiler_params=_params(("arbitrary",)))(sidx, *sums, *bufs))


PARAMS = ("ffn1_norm", "ffn1_w_gate", "ffn1_w_up", "ffn1_w_down", "mix_norm", "w_in", "b_gate", "na_q_norm", "na_k_norm",
          "na_rpb", "sw_q_norm", "sw_k_norm", "sw_sink", "t5_rel_table", "w_branch_na", "w_branch_sw", "w_out", "ffn2_norm",
          "ffn2_w_gate", "ffn2_w_up", "ffn2_w_down")
SMALL_ALL = tuple(n for n in PARAMS if n not in BIG)
TRANSPOSED = ("ffn1_w_gate", "ffn1_w_up", "w_in", "ffn2_w_gate", "ffn2_w_up")
SMALL_ROWS = 152


def _pack_small(vals):
    flat = jnp.concatenate([vals[n].reshape(-1).astype(F32) for n in SMALL_ALL] + [vals["loss"].reshape(-1)])
    return jnp.pad(flat, (0, SMALL_ROWS * 128 - flat.shape[0])).reshape(SMALL_ROWS, 128)


def _unpack_small(packed, like):
    flat, out, off = packed.reshape(-1), {}, 0
    for n in SMALL_ALL:
        size = math.prod(like[n].shape)
        out[n] = flat[off:off + size].reshape(like[n].shape)
        off += size
    out["loss"] = flat[off]
    return out


def kernel(x, ffn1_norm, ffn1_w_gate, ffn1_w_up, ffn1_w_down, mix_norm, w_in, b_gate, na_q_norm, na_k_norm, na_rpb, sw_q_norm, sw_k_norm, sw_sink, t5_rel_table, w_branch_na, w_branch_sw, w_out, ffn2_norm, ffn2_w_gate, ffn2_w_up, ffn2_w_down, loss_target, m_ffn1_norm, m_ffn1_w_gate, m_ffn1_w_up, m_ffn1_w_down, m_mix_norm, m_w_in, m_b_gate, m_na_q_norm, m_na_k_norm, m_na_rpb, m_sw_q_norm, m_sw_k_norm, m_sw_sink, m_t5_rel_table, m_w_branch_na, m_w_branch_sw, m_w_out, m_ffn2_norm, m_ffn2_w_gate, m_ffn2_w_up, m_ffn2_w_down, v_ffn1_norm, v_ffn1_w_gate, v_ffn1_w_up, v_ffn1_w_down, v_mix_norm, v_w_in, v_b_gate, v_na_q_norm, v_na_k_norm, v_na_rpb, v_sw_q_norm, v_sw_k_norm, v_sw_sink, v_t5_rel_table, v_w_branch_na, v_w_branch_sw, v_w_out, v_ffn2_norm, v_ffn2_w_gate, v_ffn2_w_up, v_ffn2_w_down):
    args = locals()
    tr = lambda n, a: jnp.transpose(a, (0, 2, 1)) if n in TRANSPOSED else a
    w = {n: tr(n, args[n]) for n in PARAMS}
    m = {n: tr(n, args["m_" + n]) for n in PARAMS}
    v = {n: tr(n, args["v_" + n]) for n in PARAMS}
    cidx = lax.axis_index("c").astype(jnp.int32).reshape(1)
    sidx = (2 * lax.axis_index("x") + lax.axis_index("y")).astype(jnp.int32).reshape(1)

    small = [{n: w[n][l] for n in SMALL} for l in range(DEPTH)]
    order = ("ffn1", "mix", "ffn2")

    keys = [(l, g) for l in range(DEPTH) for g in order]
    local = lambda l, g: [w[n][l].astype(BF16) for n in GROUPS[g]]
    first = gather_start([local(*keys[0])], name="gather_start")
    rest = gather_start([local(*key) for key in keys[1:]], first[0][2][0], name="gather_start")
    in_flight = dict(zip(keys, first + rest))
    t5b = t5_bias(w["t5_rel_table"], name="t5_bias")
    for l in range(DEPTH):
        small[l]["na_bias"] = na_bias_table(small[l]["na_rpb"], name="na_bias_table")
    early = [t5b] + [small[l]["na_bias"] for l in range(DEPTH)] + [rest[0][2][0]]

    def weights_of(l):
        def get(group, after):
            send_sems, recv_sems, thru, lands = in_flight[(l, group)]
            after = [after] + (early if (l, group) == keys[0] else [])
            thru, lands = gather_wait(send_sems, recv_sems, thru, lands, after, name="gather_wait")
            return dict(zip(GROUPS[group], gather_finish(thru, lands, name="gather_finish")))
        return get

    h0, saved0 = layer_fwd(x[0], small[0], weights_of(0), t5b)
    h1, saved1 = layer_fwd(h0, small[1], weights_of(1), t5b)
    dy, dy_bf, loss_row = loss_head(h1, loss_target[0], name="loss_head")

    crossing, tokens, pending = {}, [], []

    def ship(after):
        key, send_sems, recv_sems, grads, lands = pending.pop()
        grads, from_sibling = pair_exchange_wait(send_sems, recv_sems, grads, lands, after, name="pair_exchange_wait")
        sums = _per_shape(lambda gs, bs: add_halves(gs, bs, cidx, name="add_halves"), grads, from_sibling)
        send_sems, recv_sems, sums, lands, token = chip_exchange_start(sums, name="chip_exchange_start")
        crossing[key] = (send_sems, recv_sems, sums, lands)
        return token

    def reduce_of(l):
        def emit(group, grads):
            grads = list(grads)
            shipped = ship([grads[0]]) if pending else None
            send_sems, recv_sems, grads, lands, token = pair_exchange_start(grads, shipped, name="pair_exchange_start")
            pending.append(((l, group), send_sems, recv_sems, grads, lands))
            tokens.append(token)
            return token
        return emit

    def finish(layer, after, filled=None):
        sent = {}
        for group in order:
            send_sems, recv_sems, sums, lands = crossing[(layer, group)]
            sums, got = chip_exchange_wait(send_sems, recv_sems, sums, lands, after, name="chip_exchange_wait")
            halves = _per_shape(lambda ss, bs: add_chips(ss, bs, sidx, name="add_chips"), sums, got)
            sent[group] = pair_send_start(halves, name="pair_send_start")
            after = [sent[group][4]]
        out = {}
        for group in order:
            send_sems, recv_sems, halves, lands, _ = sent[group]
            halves, theirs = pair_send_wait(send_sems, recv_sems, halves, lands, after, name="pair_send_wait")
            names = GROUPS[group]
            res = _per_shape(
                lambda ws, ms, vs, a, b, *f: adamw_layer(ws, ms, vs, a, b, cidx, layer, list(f[0]) if f else None, name="adamw_layer"),
                *([[w[n] for n in names], [m[n] for n in names], [v[n] for n in names], halves, theirs]
                  + ([[filled[n] for n in names]] if filled is not None else [])))
            out.update(zip(names, res))
            after = [res[-1][0]]
        return out

    dy, dy_bf, small1, dt5_1 = layer_bwd(dy, dy_bf, saved1, small[1], t5b, reduce_of(1))
    grad_x, _, small0, dt5_0 = layer_bwd(dy, dy_bf, saved0, small[0], t5b, reduce_of(0), dep=tokens[-1])
    done1 = finish(1, [ship([grad_x])])

    smalls = [small0, small1]
    dt5 = t5_table_grad(dt5_0, dt5_1, name="t5_table_grad").reshape(32, 8)
    local_small = {n: jnp.stack([smalls[l][n].reshape(w[n].shape[1:]) for l in range(DEPTH)]) for n in SMALL}
    local_small["t5_rel_table"] = dt5
    local_small["loss"] = loss_row[0, 0:1]
    total = allreduce_small(_pack_small(local_small), name="allreduce_small")
    small_grads = _unpack_small(total, w)
    pack = lambda d: _pack_small({**d, "loss": jnp.zeros((1,), F32)})[None]
    ds, ms, vs = adamw(pack(w), total[None], pack(m), pack(v), name="adamw_small")

    grad, delta, new_m, new_v = {}, {}, {}, {}
    for n, done in finish(0, [ds, done1[BIG[-1]][0]], filled=done1).items():
        grad[n], delta[n], new_m[n], new_v[n] = done
    for n in SMALL_ALL:
        grad[n] = small_grads[n]
    d_s, m_s, v_s = _unpack_small(ds[0], w), _unpack_small(ms[0], w), _unpack_small(vs[0], w)
    for n in SMALL_ALL:
        delta[n], new_m[n], new_v[n] = d_s[n], m_s[n], v_s[n]

    return (small_grads["loss"], grad_x[None], *[tr(n, grad[n]) for n in PARAMS], *[tr(n, delta[n]) for n in PARAMS],
            *[tr(n, new_m[n]) for n in PARAMS], *[tr(n, new_v[n]) for n in PARAMS])
```

```python
import functools
import math

import jax
import jax.numpy as jnp
import numpy as np
from jax import lax
from jax.experimental import pallas as pl
from jax.experimental.pallas import tpu as pltpu

F32 = jnp.float32
BF16 = jnp.bfloat16

SEQ = 2048
DM = 1024
DFF = 2816
DEPTH = 2
NSH = 4
FSH = DFF // NSH
GRID_W = 64
ROWS = SEQ // GRID_W
NA_HEADS = 8
HD = 64
NA_WR = 8
NA_WC = 16
NA_KEYS = NA_WR * GRID_W
SW_BLK = 128
SW_NB = SEQ // SW_BLK
SW_KEYS = 3 * SW_BLK
ATT_W = 2304
GATE_W = 2048
IN_W = ATT_W + GATE_W
EPS = 1e-6
NEG = -1e30
QK_SCALE = 1.0 / math.sqrt(HD)

ADAM_LR = 0.001
ADAM_B1 = 0.9
ADAM_B2 = 0.999
ADAM_EPS = 1e-08
ADAM_WD = 0.01
ADAM_STEP = 10

VMEM_LIMIT = 56 << 20
MESH = pl.DeviceIdType.MESH

NT = (((1,), (1,)), ((), ()))
TN = (((0,), (0,)), ((), ()))
NN = (((1,), (0,)), ((), ()))


def _dot(a, b, dims=NN):
    return lax.dot_general(a, b, dims, preferred_element_type=F32)


def _params(sem=None):
    return pltpu.CompilerParams(dimension_semantics=sem, vmem_limit_bytes=VMEM_LIMIT)


def _sds(shape, dtype):
    return jax.ShapeDtypeStruct(shape, dtype)


def mm(a, b, *, name, ta=False, tb=False, out_dtype=F32, add=None, scale=None, tm=512, tn=None, tk=None, exact=False,
       dep=None):
    m, kd = (a.shape[1], a.shape[0]) if ta else a.shape
    n = b.shape[0] if tb else b.shape[1]
    tm, tn, tk = min(tm, m), min(tn or n, n), min(tk or kd, kd)
    nk = kd // tk
    dims = (((0 if ta else 1,), (1 if tb else 0,)), ((), ()))

    def body(*refs):
        a_ref, b_ref = refs[:2]
        add_ref = refs[2] if add is not None else None
        o_ref, acc = refs[-2:]
        k = pl.program_id(2)

        @pl.when(k == 0)
        def _():
            acc[...] = jnp.zeros_like(acc)

        if exact:
            acc[...] += lax.dot_general(a_ref[...], b_ref[...], dims, precision=lax.Precision.HIGHEST,
                                        preferred_element_type=F32)
        else:
            acc[...] += lax.dot_general(a_ref[...].astype(BF16), b_ref[...].astype(BF16), dims,
                                        preferred_element_type=F32)

        @pl.when(k == nk - 1)
        def _():
            r = acc[...]
            if scale is not None:
                r = r * scale
            if add is not None:
                r = r + add_ref[...]
            o_ref[...] = r.astype(out_dtype)

    a_spec = pl.BlockSpec((tk, tm), lambda i, j, k: (k, i)) if ta else pl.BlockSpec((tm, tk), lambda i, j, k: (i, k))
    b_spec = pl.BlockSpec((tn, tk), lambda i, j, k: (j, k)) if tb else pl.BlockSpec((tk, tn), lambda i, j, k: (k, j))
    o_spec = pl.BlockSpec((tm, tn), lambda i, j, k: (i, j))
    ins, specs = [a, b], [a_spec, b_spec]
    if add is not None:
        ins.append(add)
        specs.append(o_spec)
    if dep is not None:
        ins.append(dep)
        specs.append(pl.BlockSpec(memory_space=pl.ANY))
    return pl.pallas_call(
        body, name=name, out_shape=_sds((m, n), out_dtype), grid=(m // tm, n // tn, nk), in_specs=specs,
        out_specs=o_spec, scratch_shapes=[pltpu.VMEM((tm, tn), F32)],
        compiler_params=_params(("parallel", "parallel", "arbitrary")))(*ins)


def _rms(x):
    return lax.rsqrt(jnp.mean(x * x, axis=-1, keepdims=True) + EPS)


def rms_fwd(x, gain, *, name, tm=512):
    def body(x_ref, g_ref, h_ref):
        x = x_ref[...]
        h_ref[...] = (x * _rms(x) * g_ref[...]).astype(BF16)

    return pl.pallas_call(
        body, name=name, out_shape=_sds(x.shape, BF16), grid=(x.shape[0] // tm,),
        in_specs=[pl.BlockSpec((tm, DM), lambda i: (i, 0)), pl.BlockSpec((1, DM), lambda i: (0, 0))],
        out_specs=pl.BlockSpec((tm, DM), lambda i: (i, 0)), compiler_params=_params(("parallel",)))(x, gain)


def _rms_bwd_math(dh, x, gain):
    r = _rms(x)
    xh = x * r
    dgain = jnp.sum(dh * xh, axis=0, keepdims=True)
    dxn = dh * gain
    dx = r * (dxn - xh * jnp.mean(dxn * xh, axis=-1, keepdims=True))
    return dx, dgain


def rms_bwd(dh, x, gain, dres, *, name, tm=512):
    def body(dh_ref, x_ref, g_ref, dres_ref, dx_ref, dxb_ref, dg_ref):
        @pl.when(pl.program_id(0) == 0)
        def _():
            dg_ref[...] = jnp.zeros_like(dg_ref)

        dx, dg = _rms_bwd_math(dh_ref[...], x_ref[...], g_ref[...])
        dx = dres_ref[...] + dx
        dx_ref[...] = dx
        dxb_ref[...] = dx.astype(BF16)
        dg_ref[...] += dg

    tile = pl.BlockSpec((tm, DM), lambda i: (i, 0))
    vec = pl.BlockSpec((1, DM), lambda i: (0, 0))
    return pl.pallas_call(
        body, name=name, out_shape=(_sds(x.shape, F32), _sds(x.shape, BF16), _sds((1, DM), F32)), grid=(x.shape[0] // tm,),
        in_specs=[tile, tile, vec, tile], out_specs=(tile, tile, vec), compiler_params=_params(("arbitrary",)))(dh, x, gain, dres)


def _with_dep(ins, specs, dep):
    if dep is None:
        return ins, specs
    return ins + [dep], specs + [pl.BlockSpec(memory_space=pl.ANY)]


def _resident_weight():
    return pl.BlockSpec((DFF, DM), lambda i: (0, 0), pipeline_mode=pl.Buffered(1))


def ffn_fwd(x, gain, wg, wu, wd, *, name, tm=512):
    def body(x_ref, g_ref, wg_ref, wu_ref, wd_ref, y_ref, h_ref, gg_ref, uu_ref):
        x = x_ref[...]
        h = (x * _rms(x) * g_ref[...]).astype(BF16)
        h_ref[...] = h
        gg = _dot(h, wg_ref[...], NT)
        uu = _dot(h, wu_ref[...], NT)
        gg_ref[...] = gg.astype(BF16)
        uu_ref[...] = uu.astype(BF16)
        act = (gg * jax.nn.sigmoid(gg) * uu).astype(BF16)
        y_ref[...] = x + 0.5 * _dot(act, wd_ref[...])

    s = x.shape[0]
    tile = pl.BlockSpec((tm, DM), lambda i: (i, 0))
    hid = pl.BlockSpec((tm, DFF), lambda i: (i, 0))
    w = _resident_weight()
    return pl.pallas_call(
        body, name=name,
        out_shape=(_sds((s, DM), F32), _sds((s, DM), BF16), _sds((s, DFF), BF16), _sds((s, DFF), BF16)),
        grid=(s // tm,), in_specs=[tile, pl.BlockSpec((1, DM), lambda i: (0, 0)), w, w, w],
        out_specs=(tile, tile, hid, hid), compiler_params=_params(("parallel",)))(x, gain, wg, wu, wd)


def ffn_bwd_tokens(dy, x, gain, gg, uu, wg, wu, wd, *, name, tm=256, dep=None):
    def body(dy_ref, x_ref, g_ref, gg_ref, uu_ref, wg_ref, wu_ref, wd_ref, *rest):
        dx_ref, dxb_ref, dgain_ref, act_ref, dg_ref, du_ref = rest[-6:]

        @pl.when(pl.program_id(0) == 0)
        def _():
            dgain_ref[...] = jnp.zeros_like(dgain_ref)

        dy = dy_ref[...]
        dact = _dot((0.5 * dy).astype(BF16), wd_ref[...], NT)
        g = gg_ref[...].astype(F32)
        u = uu_ref[...].astype(F32)
        sg = jax.nn.sigmoid(g)
        silu = g * sg
        act_ref[...] = (silu * u).astype(BF16)
        dg = (dact * u * (sg * (1.0 + g * (1.0 - sg)))).astype(BF16)
        du = (dact * silu).astype(BF16)
        dg_ref[...] = dg
        du_ref[...] = du
        dx, dgain = _rms_bwd_math(_dot(dg, wg_ref[...]) + _dot(du, wu_ref[...]), x_ref[...], g_ref[...])
        dx = dy + dx
        dx_ref[...] = dx
        dxb_ref[...] = dx.astype(BF16)
        dgain_ref[...] += dgain

    s = x.shape[0]
    tile = pl.BlockSpec((tm, DM), lambda i: (i, 0))
    vec = pl.BlockSpec((1, DM), lambda i: (0, 0))
    hid = pl.BlockSpec((tm, DFF), lambda i: (i, 0))
    hshape = _sds((s, DFF), BF16)
    w = _resident_weight()
    ins, specs = _with_dep([dy, x, gain, gg, uu, wg, wu, wd], [tile, tile, vec, hid, hid, w, w, w], dep)
    return pl.pallas_call(
        body, name=name, out_shape=(_sds((s, DM), F32), _sds((s, DM), BF16), _sds((1, DM), F32), hshape, hshape, hshape),
        grid=(s // tm,), in_specs=specs, out_specs=(tile, tile, vec, hid, hid, hid),
        compiler_params=_params(("arbitrary",)))(*ins)


def ffn_bwd_weights(h, dy, act, dg, du, *, name, tf=256):
    def body(h_ref, dy_ref, act_ref, dg_ref, du_ref, gwg_ref, gwu_ref, gwd_ref):
        h = h_ref[...]
        gwg_ref[...] = _dot(dg_ref[...], h, TN).astype(BF16)
        gwu_ref[...] = _dot(du_ref[...], h, TN).astype(BF16)
        gwd_ref[...] = (0.5 * _dot(act_ref[...], dy_ref[...], TN)).astype(BF16)

    s = h.shape[0]
    full = pl.BlockSpec((s, DM), lambda f: (0, 0))
    hid = pl.BlockSpec((s, tf), lambda f: (0, f))
    wt = pl.BlockSpec((tf, DM), lambda f: (f, 0))
    wshape = _sds((DFF, DM), BF16)
    return pl.pallas_call(
        body, name=name, out_shape=(wshape, wshape, wshape), grid=(DFF // tf,), in_specs=[full, full, hid, hid, hid],
        out_specs=(wt, wt, wt), compiler_params=_params(("parallel",)))(h, dy, act, dg, du)


def _group_mean(v, bd):
    hi = v.astype(BF16)
    lo = (v - hi.astype(F32)).astype(BF16)
    return _dot(hi, bd) + _dot(lo, bd)


def _block_diag(width):
    idx = np.arange(width) // HD
    return jnp.asarray((idx[:, None] == idx[None, :]).astype(np.float32) / HD, dtype=BF16)


def qknorm_fwd(z, gq_na, gk_na, gq_sw, gk_sw, *, name, tm=256):
    def body(zq_ref, zk_ref, zv_ref, zs_ref, zkv_ref, gqa_ref, gka_ref, gqs_ref, gks_ref, bd_ref, bd2_ref,
             qa_ref, ka_ref, va_ref, qs_ref, kv_ref):
        bd = bd_ref[...]

        def norm(x, g, bdm):
            x = x.astype(F32)
            return x * lax.rsqrt(_group_mean(x * x, bdm) + EPS) * g

        qa_ref[...] = (norm(zq_ref[...], gqa_ref[...], bd) * QK_SCALE).astype(BF16)
        ka_ref[...] = norm(zk_ref[...], gka_ref[...], bd).astype(BF16)
        va_ref[...] = zv_ref[...].astype(BF16)
        qs_ref[...] = (norm(zs_ref[...], gqs_ref[...], bd) * QK_SCALE).astype(BF16)
        kv = zkv_ref[...]
        kv_ref[:, 0:128] = norm(kv[:, 0:128], gks_ref[...], bd2_ref[...]).astype(BF16)
        kv_ref[:, 128:256] = kv[:, 128:256].astype(BF16)

    s = z.shape[0]
    col = lambda j: pl.BlockSpec((tm, 512), lambda i, j=j: (i, j))
    vec = lambda w: pl.BlockSpec((1, w), lambda i: (0, 0))
    o512 = pl.BlockSpec((tm, 512), lambda i: (i, 0))
    g512 = lambda g: jnp.tile(g.reshape(1, HD), (1, 8))
    return pl.pallas_call(
        body, name=name,
        out_shape=(_sds((s, 512), BF16),) * 4 + (_sds((s, 256), BF16),), grid=(s // tm,),
        in_specs=[col(0), col(1), col(2), col(3), pl.BlockSpec((tm, 256), lambda i: (i, 8)), vec(512), vec(512), vec(512),
                  vec(128), pl.BlockSpec((512, 512), lambda i: (0, 0)), pl.BlockSpec((128, 128), lambda i: (0, 0))],
        out_specs=(o512, o512, o512, o512, pl.BlockSpec((tm, 256), lambda i: (i, 0))),
        compiler_params=_params(("parallel",)))(
            z, z, z, z, z, g512(gq_na), g512(gk_na), g512(gq_sw), jnp.tile(gk_sw.reshape(1, HD), (1, 2)),
            _block_diag(512), _block_diag(128))


def qknorm_bwd(z, dqa, dka, dva, dqs, dkv, gq_na, gk_na, gq_sw, gk_sw, *, name, tm=256):
    def body(zq_ref, zk_ref, zs_ref, zkv_ref, dqa_ref, dka_ref, dva_ref, dqs_ref, dkv_ref, gqa_ref, gka_ref, gqs_ref,
             gks_ref, bd_ref, bd2_ref, dz_ref, dgqa_ref, dgka_ref, dgqs_ref, dgks_ref):
        @pl.when(pl.program_id(0) == 0)
        def _():
            dgqa_ref[...] = jnp.zeros_like(dgqa_ref)
            dgka_ref[...] = jnp.zeros_like(dgka_ref)
            dgqs_ref[...] = jnp.zeros_like(dgqs_ref)
            dgks_ref[...] = jnp.zeros_like(dgks_ref)

        bd = bd_ref[...]

        def bwd(x, dy, g, bdm, dg_ref):
            x = x.astype(F32)
            r = lax.rsqrt(_group_mean(x * x, bdm) + EPS)
            xh = x * r
            dg_ref[...] += jnp.sum(dy * xh, axis=0, keepdims=True)
            dxn = dy * g
            return r * (dxn - xh * _group_mean(dxn * xh, bdm))

        dz_ref[:, 0:512] = bwd(zq_ref[...], dqa_ref[...] * QK_SCALE, gqa_ref[...], bd, dgqa_ref).astype(BF16)
        dz_ref[:, 512:1024] = bwd(zk_ref[...], dka_ref[...], gka_ref[...], bd, dgka_ref).astype(BF16)
        dz_ref[:, 1024:1536] = dva_ref[...].astype(BF16)
        dz_ref[:, 1536:2048] = bwd(zs_ref[...], dqs_ref[...] * QK_SCALE, gqs_ref[...], bd, dgqs_ref).astype(BF16)
        dkv = dkv_ref[...]
        dz_ref[:, 2048:2176] = bwd(zkv_ref[:, 0:128], dkv[:, 0:128], gks_ref[...], bd2_ref[...], dgks_ref).astype(BF16)
        dz_ref[:, 2176:2304] = dkv[:, 128:256].astype(BF16)

    s = z.shape[0]
    col = lambda j: pl.BlockSpec((tm, 512), lambda i, j=j: (i, j))
    t512 = pl.BlockSpec((tm, 512), lambda i: (i, 0))
    t256 = pl.BlockSpec((tm, 256), lambda i: (i, 0))
    vec = lambda w: pl.BlockSpec((1, w), lambda i: (0, 0))
    g512 = lambda g: jnp.tile(g.reshape(1, HD), (1, 8))
    return pl.pallas_call(
        body, name=name,
        out_shape=(_sds((s, ATT_W), BF16), _sds((1, 512), F32), _sds((1, 512), F32), _sds((1, 512), F32), _sds((1, 128), F32)),
        grid=(s // tm,),
        in_specs=[col(0), col(1), col(3), pl.BlockSpec((tm, 256), lambda i: (i, 8)), t512, t512, t512, t512, t256,
                  vec(512), vec(512), vec(512), vec(128), pl.BlockSpec((512, 512), lambda i: (0, 0)),
                  pl.BlockSpec((128, 128), lambda i: (0, 0))],
        out_specs=(pl.BlockSpec((tm, ATT_W), lambda i: (i, 0)), vec(512), vec(512), vec(512), vec(128)),
        compiler_params=_params(("arbitrary",)))(
            z, z, z, z, dqa, dka, dva, dqs, dkv, g512(gq_na), g512(gk_na), g512(gq_sw),
            jnp.tile(gk_sw.reshape(1, HD), (1, 2)), _block_diag(512), _block_diag(128))


def _na_row_start(r):
    return jnp.clip(r - NA_WR // 2, 0, ROWS - NA_WR)


def na_bias_table(rpb, *, name):
    t = jnp.pad(rpb, ((0, 0), (0, 2), (0, HD - (2 * NA_WC - 1))))
    pairs = jnp.concatenate([t[:, :16], t[:, 1:17]], axis=-1).reshape(NA_HEADS, 16, 1, 128)

    def body(t_ref, o_ref):
        p = pl.program_id(0)
        q = lax.broadcasted_iota(jnp.int32, (GRID_W, 128), 0)
        kc = lax.broadcasted_iota(jnp.int32, (GRID_W, 128), 1) & (GRID_W - 1)
        cs = jnp.clip(q - NA_WC // 2, 0, GRID_W - NA_WC)
        ok = (kc >= cs) & (kc < cs + NA_WC)
        for h in range(NA_HEADS):
            for pr in range(NA_WR // 2):
                x = jnp.broadcast_to(t_ref[h, 2 * pr - p + NA_WR - 1], (GRID_W, 128))
                b = pltpu.roll(x, 128 - (NA_WC - 1), 1, stride=1, stride_axis=0)
                o_ref[h, :, 128 * pr:128 * pr + 128] = jnp.where(ok, b, NEG)

    return pl.pallas_call(
        body, name=name, out_shape=_sds((NA_WR, NA_HEADS, GRID_W, NA_KEYS), F32), grid=(NA_WR,),
        in_specs=[pl.BlockSpec((NA_HEADS, 16, 1, 128), lambda p: (0, 0, 0, 0))],
        out_specs=pl.BlockSpec((None, NA_HEADS, GRID_W, NA_KEYS), lambda p: (p, 0, 0, 0)),
        compiler_params=_params(("parallel",)))(pairs)


def _lane_halves():
    lane = lax.broadcasted_iota(jnp.int32, (1, 128), 1)
    return lane < HD


def na_fwd(q, k, v, bias, *, name):
    def body(q_ref, k_ref, v_ref, b_ref, o_ref, lse_ref):
        r = pl.program_id(0)
        off = pl.multiple_of(_na_row_start(r) * GRID_W, GRID_W)
        first = _lane_halves()
        sels = [first, jnp.logical_not(first)]
        lanes = [slice(128 * j, 128 * j + 128) for j in range(NA_HEADS // 2)]
        q2s = [q_ref[:, l] for l in lanes]
        k2s = [k_ref[pl.ds(off, NA_KEYS), l] for l in lanes]
        v2s = [v_ref[pl.ds(off, NA_KEYS), l] for l in lanes]
        scores = []
        for h in range(NA_HEADS):
            j, half = divmod(h, 2)
            scores.append(_dot(jnp.where(sels[half], q2s[j], jnp.zeros_like(q2s[j])), k2s[j], NT))
        probs, lses = [], []
        for h in range(NA_HEADS):
            b = b_ref[h]
            s = jnp.where(b > 0.5 * NEG, scores[h] + b, NEG)
            m = jnp.max(s, axis=-1, keepdims=True)
            e = jnp.exp(s - m)
            l = jnp.sum(e, axis=-1, keepdims=True)
            probs.append((e / l).astype(BF16))
            lses.append(m + jnp.log(l))
        for j in range(NA_HEADS // 2):
            zero = jnp.zeros_like(v2s[j])
            o2 = (_dot(probs[2 * j], jnp.where(sels[0], v2s[j], zero))
                  + _dot(probs[2 * j + 1], jnp.where(sels[1], v2s[j], zero)))
            o_ref[:, lanes[j]] = o2.astype(BF16)
        lse_ref[...] = jnp.concatenate(lses, axis=1)

    s_tok = q.shape[0]
    full = pl.BlockSpec((s_tok, 512), lambda r: (0, 0))
    return pl.pallas_call(
        body, name=name, out_shape=(_sds((s_tok, 512), BF16), _sds((s_tok, NA_HEADS), F32)), grid=(ROWS,),
        in_specs=[pl.BlockSpec((GRID_W, 512), lambda r: (r, 0)), full, full,
                  pl.BlockSpec((None, NA_HEADS, GRID_W, NA_KEYS), lambda r: (r - _na_row_start(r), 0, 0, 0))],
        out_specs=(pl.BlockSpec((GRID_W, 512), lambda r: (r, 0)), pl.BlockSpec((GRID_W, NA_HEADS), lambda r: (r, 0))),
        compiler_params=_params(("parallel",)))(q, k, v, bias)


def na_bwd(q, k, v, o, do, lse, bias, *, name):
    def body(q_ref, k_ref, v_ref, o_ref, do_ref, lse_ref, b_ref, dq_ref, dk_ref, dv_ref, db_ref):
        r = pl.program_id(0)

        @pl.when(r == 0)
        def _():
            dk_ref[...] = jnp.zeros_like(dk_ref)
            dv_ref[...] = jnp.zeros_like(dv_ref)

        @pl.when((r <= NA_WR // 2) | (r > ROWS - NA_WR // 2))
        def _():
            db_ref[...] = jnp.zeros_like(db_ref)

        off = pl.multiple_of(_na_row_start(r) * GRID_W, GRID_W)
        first = _lane_halves()
        sels = [first, jnp.logical_not(first)]
        lanes = [slice(128 * j, 128 * j + 128) for j in range(NA_HEADS // 2)]
        q2s = [q_ref[:, l] for l in lanes]
        k2s = [k_ref[pl.ds(off, NA_KEYS), l] for l in lanes]
        v2s = [v_ref[pl.ds(off, NA_KEYS), l] for l in lanes]
        do2s = [do_ref[:, l] for l in lanes]
        prods = [do2s[j].astype(F32) * o_ref[:, lanes[j]].astype(F32) for j in range(NA_HEADS // 2)]
        lse = lse_ref[...]
        qhs, dohs, scores, dps = [], [], [], []
        for h in range(NA_HEADS):
            j, half = divmod(h, 2)
            qhs.append(jnp.where(sels[half], q2s[j], jnp.zeros_like(q2s[j])))
            dohs.append(jnp.where(sels[half], do2s[j], jnp.zeros_like(do2s[j])))
            scores.append(_dot(qhs[h], k2s[j], NT))
            dps.append(_dot(dohs[h], v2s[j], NT))
        pbs, dsbs = [], []
        for h in range(NA_HEADS):
            j, half = divmod(h, 2)
            b = b_ref[h]
            s = jnp.where(b > 0.5 * NEG, scores[h] + b, NEG)
            p = jnp.exp(s - lse[:, h:h + 1])
            delta = jnp.sum(jnp.where(sels[half], prods[j], 0.0), axis=-1, keepdims=True)
            ds = p * (dps[h] - delta)
            db_ref[h] += ds
            pbs.append(p.astype(BF16))
            dsbs.append(ds.astype(BF16))
        for j in range(NA_HEADS // 2):
            a, b = 2 * j, 2 * j + 1
            zero = jnp.zeros_like(k2s[j])
            dq_ref[:, lanes[j]] = (_dot(dsbs[a], jnp.where(sels[0], k2s[j], zero))
                                   + _dot(dsbs[b], jnp.where(sels[1], k2s[j], zero)))
            dk_ref[pl.ds(off, NA_KEYS), lanes[j]] += _dot(dsbs[a], qhs[a], TN) + _dot(dsbs[b], qhs[b], TN)
            dv_ref[pl.ds(off, NA_KEYS), lanes[j]] += _dot(pbs[a], dohs[a], TN) + _dot(pbs[b], dohs[b], TN)

    s_tok = q.shape[0]
    full = pl.BlockSpec((s_tok, 512), lambda r: (0, 0))
    row = pl.BlockSpec((GRID_W, 512), lambda r: (r, 0))
    bias_spec = pl.BlockSpec((None, NA_HEADS, GRID_W, NA_KEYS), lambda r: (r - _na_row_start(r), 0, 0, 0))
    return pl.pallas_call(
        body, name=name,
        out_shape=(_sds((s_tok, 512), F32), _sds((s_tok, 512), F32), _sds((s_tok, 512), F32),
                   _sds((NA_WR, NA_HEADS, GRID_W, NA_KEYS), F32)),
        grid=(ROWS,),
        in_specs=[row, full, full, row, row, pl.BlockSpec((GRID_W, NA_HEADS), lambda r: (r, 0)), bias_spec],
        out_specs=(row, full, full, bias_spec), compiler_params=_params(("arbitrary",)))(q, k, v, o, do, lse, bias)


def t5_bucket_map():
    rel = np.arange(SW_KEYS)[None, :] - SW_BLK - np.arange(SW_BLK)[:, None]
    nb = 16
    max_exact = nb // 2
    n = np.abs(rel)
    large = max_exact + (np.log(np.maximum(n, 1) / max_exact) / np.log(128 / max_exact) * (nb - max_exact)).astype(np.int32)
    large = np.minimum(large, nb - 1)
    return ((rel > 0) * nb + np.where(n < max_exact, n, large)).astype(np.int32)


def t5_bias(table, *, name):
    rel = np.arange(-SW_BLK, SW_BLK + 1)
    nb, max_exact = 16, 8
    n = np.abs(rel)
    large = max_exact + (np.log(np.maximum(n, 1) / max_exact) / np.log(128 / max_exact) * (nb - max_exact)).astype(np.int32)
    bucket = ((rel > 0) * nb + np.where(n < max_exact, n, np.minimum(large, nb - 1))).astype(np.int32)
    u = jnp.pad(table[jnp.asarray(bucket)].T, ((0, 0), (0, SW_KEYS - bucket.shape[0]))).reshape(8, 1, SW_KEYS)

    def body(u_ref, o_ref):
        for h in range(8):
            x = jnp.broadcast_to(u_ref[h], (SW_BLK, SW_KEYS))
            o_ref[h] = pltpu.roll(x, 0, 1, stride=1, stride_axis=0)

    return pl.pallas_call(body, name=name, out_shape=_sds((8, SW_BLK, SW_KEYS), F32), compiler_params=_params())(u)


def _sw_valid(n):
    a = lax.broadcasted_iota(jnp.int32, (SW_BLK, SW_KEYS), 0)
    j = lax.broadcasted_iota(jnp.int32, (SW_BLK, SW_KEYS), 1)
    kpos = (n - 1) * SW_BLK + j
    return (jnp.abs(j - SW_BLK - a) <= SW_BLK) & (kpos >= 0) & (kpos < SEQ)


def _dup_group(x2, g, first):
    rolled = pltpu.roll(x2, HD, 1)
    return jnp.where(first, x2, rolled) if g == 0 else jnp.where(first, rolled, x2)


def sw_fwd(q, kv, t5, sink, *, name):
    def body(q_ref, kv_ref, t5_ref, sink_ref, o_ref, lse_ref):
        n = pl.program_id(0)
        off = pl.multiple_of(n * SW_BLK, SW_BLK)
        first = _lane_halves()
        sels = [first, jnp.logical_not(first)]
        valid = _sw_valid(n)
        k2 = kv_ref[pl.ds(off, SW_KEYS), 0:128]
        v2 = kv_ref[pl.ds(off, SW_KEYS), 128:256]
        kk = [_dup_group(k2, g, first) for g in range(2)]
        vv = [_dup_group(v2, g, first) for g in range(2)]
        q2s = [q_ref[:, 128 * j:128 * j + 128] for j in range(4)]
        scores = []
        for h in range(8):
            j, half = divmod(h, 2)
            scores.append(_dot(jnp.where(sels[half], q2s[j], jnp.zeros_like(q2s[j])), kk[j // 2], NT))
        probs, lses = [], []
        for h in range(8):
            s = jnp.where(valid, scores[h] + t5_ref[h], NEG)
            snk = sink_ref[h]
            m = jnp.maximum(jnp.max(s, axis=-1, keepdims=True), snk)
            e = jnp.exp(s - m)
            den = jnp.sum(e, axis=-1, keepdims=True) + jnp.exp(snk - m)
            probs.append((e / den).astype(BF16))
            lses.append(m + jnp.log(den))
        outs = []
        for j in range(4):
            vg = vv[j // 2]
            zero = jnp.zeros_like(vg)
            outs.append(_dot(probs[2 * j], jnp.where(sels[0], vg, zero)) + _dot(probs[2 * j + 1], jnp.where(sels[1], vg, zero)))
        o_ref[...] = jnp.concatenate(outs, axis=1).astype(BF16)
        lse_ref[...] = jnp.concatenate(lses, axis=1)

    s_tok = q.shape[0]
    blk = pl.BlockSpec((SW_BLK, 512), lambda n: (n, 0))
    return pl.pallas_call(
        body, name=name, out_shape=(_sds((s_tok, 512), BF16), _sds((s_tok, 8), F32)), grid=(SW_NB,),
        in_specs=[blk, pl.BlockSpec(kv.shape, lambda n: (0, 0)), pl.BlockSpec((8, SW_BLK, SW_KEYS), lambda n: (0, 0, 0)),
                  pl.BlockSpec(memory_space=pltpu.SMEM)],
        out_specs=(blk, pl.BlockSpec((SW_BLK, 8), lambda n: (n, 0))), compiler_params=_params(("parallel",)))(q, kv, t5, sink)


def sw_bwd(q, kv, o, do, lse, t5, sink, *, name):
    def body(q_ref, kv_ref, o_ref, do_ref, lse_ref, t5_ref, sink_ref, dq_ref, dkv_ref, dt5_ref, dsink_ref):
        n = pl.program_id(0)

        @pl.when(n == 0)
        def _():
            dkv_ref[...] = jnp.zeros_like(dkv_ref)
            dt5_ref[...] = jnp.zeros_like(dt5_ref)
            dsink_ref[...] = jnp.zeros_like(dsink_ref)

        off = pl.multiple_of(n * SW_BLK, SW_BLK)
        first = _lane_halves()
        sels = [first, jnp.logical_not(first)]
        valid = _sw_valid(n)
        k2 = kv_ref[pl.ds(off, SW_KEYS), 0:128]
        v2 = kv_ref[pl.ds(off, SW_KEYS), 128:256]
        kk = [_dup_group(k2, g, first) for g in range(2)]
        vv = [_dup_group(v2, g, first) for g in range(2)]
        lanes = [slice(128 * j, 128 * j + 128) for j in range(4)]
        q2s = [q_ref[:, l] for l in lanes]
        do2s = [do_ref[:, l] for l in lanes]
        prods = [do2s[j].astype(F32) * o_ref[:, lanes[j]].astype(F32) for j in range(4)]
        lse = lse_ref[...]
        qhs, dohs, scores, dps = [], [], [], []
        for h in range(8):
            j, half = divmod(h, 2)
            qhs.append(jnp.where(sels[half], q2s[j], jnp.zeros_like(q2s[j])))
            dohs.append(jnp.where(sels[half], do2s[j], jnp.zeros_like(do2s[j])))
            scores.append(_dot(qhs[h], kk[j // 2], NT))
            dps.append(_dot(dohs[h], vv[j // 2], NT))
        pbs, dsbs, dss, dsinks = [], [], [], []
        for h in range(8):
            j, half = divmod(h, 2)
            s = jnp.where(valid, scores[h] + t5_ref[h], NEG)
            lse_h = lse[:, h:h + 1]
            p = jnp.exp(s - lse_h)
            delta = jnp.sum(jnp.where(sels[half], prods[j], 0.0), axis=-1, keepdims=True)
            ds = p * (dps[h] - delta)
            dss.append(ds)
            dsinks.append(-jnp.sum(jnp.exp(sink_ref[h] - lse_h) * delta, axis=0, keepdims=True))
            pbs.append(p.astype(BF16))
            dsbs.append(ds.astype(BF16))
        dt5_ref[...] += jnp.stack(dss)
        dsink_ref[...] += jnp.concatenate(dsinks, axis=1)
        dqs = []
        for j in range(4):
            a, b = 2 * j, 2 * j + 1
            zero = jnp.zeros_like(kk[j // 2])
            dqs.append(_dot(dsbs[a], jnp.where(sels[0], kk[j // 2], zero)) + _dot(dsbs[b], jnp.where(sels[1], kk[j // 2], zero)))
        dq_ref[...] = jnp.concatenate(dqs, axis=1)
        dk_groups, dv_groups = [], []
        for g in range(2):
            dkk = sum(_dot(dsbs[h], qhs[h], TN) for h in range(4 * g, 4 * g + 4))
            dvv = sum(_dot(pbs[h], dohs[h], TN) for h in range(4 * g, 4 * g + 4))
            dk_groups.append(dkk + pltpu.roll(dkk, HD, 1))
            dv_groups.append(dvv + pltpu.roll(dvv, HD, 1))
        dkv_ref[pl.ds(off, SW_KEYS), :] += jnp.concatenate(
            [jnp.where(first, dk_groups[0], dk_groups[1]), jnp.where(first, dv_groups[0], dv_groups[1])], axis=1)

    s_tok = q.shape[0]
    blk = pl.BlockSpec((SW_BLK, 512), lambda n: (n, 0))
    kv_spec = pl.BlockSpec(kv.shape, lambda n: (0, 0))
    t5_spec = pl.BlockSpec((8, SW_BLK, SW_KEYS), lambda n: (0, 0, 0))
    vec = pl.BlockSpec((1, 8), lambda n: (0, 0))
    return pl.pallas_call(
        body, name=name,
        out_shape=(_sds((s_tok, 512), F32), _sds(kv.shape, F32), _sds((8, SW_BLK, SW_KEYS), F32), _sds((1, 8), F32)),
        grid=(SW_NB,), in_specs=[blk, kv_spec, blk, blk, pl.BlockSpec((SW_BLK, 8), lambda n: (n, 0)), t5_spec,
                                 pl.BlockSpec(memory_space=pltpu.SMEM)],
        out_specs=(blk, kv_spec, t5_spec, vec), compiler_params=_params(("arbitrary",)))(q, kv, o, do, lse, t5, sink)


def gate_fwd(zg, bias, pa, ps, *, name, tm=512):
    def body(z0_ref, z1_ref, b0_ref, b1_ref, pa_ref, ps_ref, m_ref):
        g0 = jax.nn.sigmoid(z0_ref[...] + b0_ref[...])
        g1 = jax.nn.sigmoid(z1_ref[...] + b1_ref[...])
        m_ref[...] = (g0 * pa_ref[...] + g1 * ps_ref[...]).astype(BF16)

    s = zg.shape[0]
    half = lambda j: pl.BlockSpec((tm, DM), lambda i, j=j: (i, j))
    bvec = lambda j: pl.BlockSpec((1, DM), lambda i, j=j: (0, j))
    return pl.pallas_call(
        body, name=name, out_shape=_sds((s, DM), BF16), grid=(s // tm,),
        in_specs=[half(0), half(1), bvec(0), bvec(1), half(0), half(0)], out_specs=half(0),
        compiler_params=_params(("parallel",)))(zg, zg, bias, bias, pa, ps)


def gate_bwd(dm, zg, bias, pa, ps, *, name, tm=512):
    def body(dm_ref, z0_ref, z1_ref, b0_ref, b1_ref, pa_ref, ps_ref, dpa_ref, dps_ref, dz_ref, db_ref):
        @pl.when(pl.program_id(0) == 0)
        def _():
            db_ref[...] = jnp.zeros_like(db_ref)

        dm = dm_ref[...]
        g0 = jax.nn.sigmoid(z0_ref[...] + b0_ref[...])
        g1 = jax.nn.sigmoid(z1_ref[...] + b1_ref[...])
        dpa_ref[...] = (dm * g0).astype(BF16)
        dps_ref[...] = (dm * g1).astype(BF16)
        dz0 = dm * pa_ref[...] * g0 * (1.0 - g0)
        dz1 = dm * ps_ref[...] * g1 * (1.0 - g1)
        dz_ref[:, 0:DM] = dz0.astype(BF16)
        dz_ref[:, DM:2 * DM] = dz1.astype(BF16)
        db_ref[:, 0:DM] += jnp.sum(dz0, axis=0, keepdims=True)
        db_ref[:, DM:2 * DM] += jnp.sum(dz1, axis=0, keepdims=True)

    s = zg.shape[0]
    half = lambda j: pl.BlockSpec((tm, DM), lambda i, j=j: (i, j))
    bvec = lambda j: pl.BlockSpec((1, DM), lambda i, j=j: (0, j))
    return pl.pallas_call(
        body, name=name,
        out_shape=(_sds((s, DM), BF16), _sds((s, DM), BF16), _sds((s, GATE_W), BF16), _sds((1, GATE_W), F32)),
        grid=(s // tm,), in_specs=[half(0), half(0), half(1), bvec(0), bvec(1), half(0), half(0)],
        out_specs=(half(0), half(0), pl.BlockSpec((tm, GATE_W), lambda i: (i, 0)), pl.BlockSpec((1, GATE_W), lambda i: (0, 0))),
        compiler_params=_params(("arbitrary",)))(dm, zg, zg, bias, bias, pa, ps)


def loss_head(y, target, *, name, tm=512):
    def body(y_ref, t_ref, dy_ref, dyb_ref, l_ref):
        @pl.when(pl.program_id(0) == 0)
        def _():
            l_ref[...] = jnp.zeros_like(l_ref)

        err = y_ref[...] - t_ref[...]
        dy = err * (1.0 / DM)
        dy_ref[...] = dy
        dyb_ref[...] = dy.astype(BF16)
        l_ref[...] += 0.5 * jnp.sum(jnp.mean(err * err, axis=-1, keepdims=True), axis=0, keepdims=True)

    s = y.shape[0]
    tile = pl.BlockSpec((tm, DM), lambda i: (i, 0))
    return pl.pallas_call(
        body, name=name, out_shape=(_sds((s, DM), F32), _sds((s, DM), BF16), _sds((1, 128), F32)), grid=(s // tm,),
        in_specs=[tile, tile], out_specs=(tile, tile, pl.BlockSpec((1, 128), lambda i: (0, 0))),
        compiler_params=_params(("arbitrary",)))(y, target)


def adamw_small(ws, gs, ms, vs, *, name):
    cnt = len(ws)

    def body(*refs):
        ins, outs = refs[:4 * cnt], refs[4 * cnt:]
        for i in range(cnt):
            w_ref, g_ref, m_ref, v_ref = ins[4 * i:4 * i + 4]
            d_ref, nm_ref, nv_ref = outs[3 * i:3 * i + 3]
            g = g_ref[...]
            nm = ADAM_B1 * m_ref[...] + (1.0 - ADAM_B1) * g
            nv = ADAM_B2 * v_ref[...] + (1.0 - ADAM_B2) * jnp.square(g)
            m_hat = nm / (1.0 - ADAM_B1 ** ADAM_STEP)
            v_hat = nv / (1.0 - ADAM_B2 ** ADAM_STEP)
            d_ref[...] = -ADAM_LR * (m_hat / (jnp.sqrt(v_hat) + ADAM_EPS) + ADAM_WD * w_ref[...])
            nm_ref[...] = nm
            nv_ref[...] = nv

    flat = [a for i in range(cnt) for a in (ws[i], gs[i], ms[i], vs[i])]
    res = pl.pallas_call(
        body, name=name, out_shape=tuple(_sds(ws[i].shape, F32) for i in range(cnt) for _ in range(3)),
        compiler_params=_params())(*flat)
    return [tuple(res[3 * i:3 * i + 3]) for i in range(cnt)]


def adamw_layer(ws, ms, vs, mines, theirs, cidx, layer, filled=None, *, name):
    cnt = len(ws)
    _, k, n = ws[0].shape
    nt = 2
    tk = k // 2 // nt

    def body(c_ref, *refs):
        own = pl.program_id(0) == c_ref[0]
        outs = refs[-4 * cnt:]
        for i in range(cnt):
            w_ref, m_ref, v_ref, a_ref, b_ref = refs[5 * i:5 * i + 5]
            g_ref, d_ref, nm_ref, nv_ref = outs[4 * i:4 * i + 4]
            g = jnp.where(own, a_ref[...], b_ref[...])
            g_ref[...] = g
            nm = ADAM_B1 * m_ref[...] + (1.0 - ADAM_B1) * g
            nv = ADAM_B2 * v_ref[...] + (1.0 - ADAM_B2) * jnp.square(g)
            m_hat = nm / (1.0 - ADAM_B1 ** ADAM_STEP)
            v_hat = nv / (1.0 - ADAM_B2 ** ADAM_STEP)
            d_ref[...] = -ADAM_LR * (m_hat / (jnp.sqrt(v_hat) + ADAM_EPS) + ADAM_WD * w_ref[...])
            nm_ref[...] = nm
            nv_ref[...] = nv

    full = pl.BlockSpec((None, tk, n), lambda hf, t, c: (layer, hf * nt + t, 0))
    half_mine = pl.BlockSpec((tk, n), lambda hf, t, c: (jnp.where(hf == c[0], t, 0), 0))
    half_theirs = pl.BlockSpec((tk, n), lambda hf, t, c: (jnp.where(hf != c[0], t, 0), 0))
    out = _sds(ws[0].shape, F32)
    ins, specs, aliases = [cidx], [], {}
    for i in range(cnt):
        ins += [ws[i], ms[i], vs[i], mines[i], theirs[i]]
        specs += [full, full, full, half_mine, half_theirs]
    if filled is not None:
        aliases = {len(ins) + j: j for j in range(4 * cnt)}
        ins += [a for f in filled for a in f]
        specs += [pl.BlockSpec(memory_space=pl.ANY)] * (4 * cnt)
    res = pl.pallas_call(
        body, name=name, out_shape=(out,) * (4 * cnt),
        grid_spec=pltpu.PrefetchScalarGridSpec(
            num_scalar_prefetch=1, grid=(2, nt), in_specs=specs, out_specs=(full,) * (4 * cnt)),
        input_output_aliases=aliases,
        compiler_params=_params(("arbitrary", "arbitrary")))(*ins)
    return [tuple(res[4 * i:4 * i + 4]) for i in range(cnt)]


def t5_table_grad(dt5_a, dt5_b, *, name):
    def body(a_ref, b_ref, map_ref, o_ref):
        d = a_ref[...] + b_ref[...]
        bucket = map_ref[...]
        for b in range(32):
            hit = (bucket == b)[None]
            o_ref[b] = jnp.sum(jnp.sum(jnp.where(hit, d, 0.0), axis=2), axis=1, keepdims=True)

    return pl.pallas_call(
        body, name=name, out_shape=_sds((32, 8, 1), F32), compiler_params=_params())(
            dt5_a, dt5_b, jnp.asarray(t5_bucket_map()))


def rpb_grad(dbias, *, name):
    def body(d_ref, rev_ref, o_ref):
        rev = rev_ref[...]
        for h in range(NA_HEADS):
            for pr in range(NA_WR // 2):
                d = d_ref[h, :, 128 * pr:128 * pr + 128]
                hi = d.astype(BF16)
                lo = (d - hi.astype(F32)).astype(BF16)
                flipped = _dot(rev, hi) + _dot(rev, lo)
                o_ref[h, pr] = jnp.sum(pltpu.roll(flipped, 0, 1, stride=1, stride_axis=0), axis=0, keepdims=True)

    anti = jnp.asarray(np.eye(GRID_W, dtype=np.float32)[::-1], dtype=BF16)
    e = pl.pallas_call(
        body, name=name, out_shape=_sds((NA_WR, NA_HEADS, NA_WR // 2, 1, 128), F32), grid=(NA_WR,),
        in_specs=[pl.BlockSpec((None, NA_HEADS, GRID_W, NA_KEYS), lambda p: (p, 0, 0, 0)),
                  pl.BlockSpec((GRID_W, GRID_W), lambda p: (0, 0))],
        out_specs=pl.BlockSpec((None, NA_HEADS, NA_WR // 2, 1, 128), lambda p: (p, 0, 0, 0, 0)),
        compiler_params=_params(("parallel",)))(dbias, anti)
    nci, nri = 2 * NA_WC - 1, 2 * NA_WR - 1
    e = e.reshape(NA_WR, NA_HEADS, NA_WR // 2, 128).transpose(0, 2, 1, 3).reshape(NA_WR * NA_WR // 2, NA_HEADS, 128)
    parts = jnp.concatenate([e[..., 48:48 + nci], jnp.concatenate([e[..., 112:128], e[..., 0:nci - 16]], axis=-1)], axis=0)
    p, pr = np.arange(NA_WR)[:, None], np.arange(NA_WR // 2)[None, :]
    ri = np.concatenate([(2 * pr - p + NA_WR - 1).reshape(-1), (2 * pr - p + NA_WR).reshape(-1)])
    pick = jnp.asarray((ri[None, :] == np.arange(16)[:, None]).astype(np.float32))
    out = mm(pick, parts.reshape(2 * NA_WR * NA_WR // 2, NA_HEADS * nci), name=name + "_rows", exact=True)
    return out.reshape(16, NA_HEADS, nci)[:nri].transpose(1, 0, 2)


BIG = ("ffn1_w_gate", "ffn1_w_up", "ffn1_w_down", "w_in", "w_branch_na", "w_branch_sw", "w_out",
       "ffn2_w_gate", "ffn2_w_up", "ffn2_w_down")
SMALL = ("ffn1_norm", "mix_norm", "b_gate", "na_q_norm", "na_k_norm", "na_rpb", "sw_q_norm", "sw_k_norm", "sw_sink",
         "ffn2_norm")


def _cols_to_full(w4):
    return w4.transpose(1, 0, 2).reshape(w4.shape[1], NSH * w4.shape[2])


def _full_to_cols(w):
    return w.reshape(w.shape[0], NSH, w.shape[1] // NSH).transpose(1, 0, 2)


def _mixer_weights(g):
    w_in_t = g["w_in"].reshape(IN_W, DM)
    return dict(w_att_t=w_in_t[:ATT_W], w_gz_t=w_in_t[ATT_W:], wa=_cols_to_full(g["w_branch_na"]),
                ws=_cols_to_full(g["w_branch_sw"]), wo=g["w_out"].reshape(DM, DM))


GROUPS = {"ffn1": ("ffn1_w_gate", "ffn1_w_up", "ffn1_w_down"), "mix": ("w_in", "w_branch_na", "w_branch_sw", "w_out"),
          "ffn2": ("ffn2_w_gate", "ffn2_w_up", "ffn2_w_down")}


def layer_fwd(x, p, weights, t5b):
    row = lambda v: v.reshape(1, -1)
    stacked = lambda g: {n: a.reshape(DFF, DM) for n, a in g.items()}
    g1 = stacked(weights("ffn1", x))
    y1, h1, gg1, uu1 = ffn_fwd(x, row(p["ffn1_norm"]), g1["ffn1_w_gate"], g1["ffn1_w_up"], g1["ffn1_w_down"], name="ffn_fwd")
    w = _mixer_weights(weights("mix", y1))
    hm = rms_fwd(y1, row(p["mix_norm"]), name="mix_norm_fwd")
    z = mm(hm, w["w_att_t"], tb=True, out_dtype=BF16, name="proj_att", tm=SEQ, tn=768)
    zg = mm(hm, w["w_gz_t"], tb=True, out_dtype=BF16, name="proj_gate", tm=SEQ, tn=512)
    qa, ka, va, qs, kv = qknorm_fwd(z, p["na_q_norm"], p["na_k_norm"], p["sw_q_norm"], p["sw_k_norm"], name="qknorm_fwd")
    bias = p["na_bias"]
    o_na, lse_na = na_fwd(qa, ka, va, bias, name="na_fwd")
    kvp = jnp.pad(kv, ((SW_BLK, SW_BLK), (0, 0)))
    sink = p["sw_sink"]
    o_sw, lse_sw = sw_fwd(qs, kvp, t5b, sink, name="sw_fwd")
    pa = mm(o_na, w["wa"], out_dtype=BF16, name="branch_na", tm=1024)
    ps = mm(o_sw, w["ws"], out_dtype=BF16, name="branch_sw", tm=1024)
    merged = gate_fwd(zg, row(p["b_gate"]), pa, ps, name="gate_fwd")
    y2 = mm(merged, w["wo"], add=y1, name="out_proj", tm=1024)
    g2 = stacked(weights("ffn2", y2))
    y3, h2, gg2, uu2 = ffn_fwd(y2, row(p["ffn2_norm"]), g2["ffn2_w_gate"], g2["ffn2_w_up"], g2["ffn2_w_down"], name="ffn_fwd")
    saved = dict(x=x, y1=y1, h1=h1, gg1=gg1, uu1=uu1, hm=hm, z=z, zg=zg, qa=qa, ka=ka, va=va, qs=qs, kvp=kvp, bias=bias,
                 o_na=o_na, lse_na=lse_na, o_sw=o_sw, lse_sw=lse_sw, pa=pa, ps=ps, merged=merged, y2=y2, h2=h2, gg2=gg2,
                 uu2=uu2, w=w, sink=sink, g1=g1, g2=g2)
    return y3, saved


def layer_bwd(dy3, dy3_bf, sv, p, t5b, emit, dep=None):
    w, g1, g2 = sv["w"], sv["g1"], sv["g2"]
    row = lambda v: v.reshape(1, -1)
    fold = lambda v: v.reshape(-1, HD).sum(axis=0)
    small = {}
    dy2, _, small["ffn2_norm"], act, dg, du = ffn_bwd_tokens(
        dy3, sv["y2"], row(p["ffn2_norm"]), sv["gg2"], sv["uu2"], g2["ffn2_w_gate"], g2["ffn2_w_up"], g2["ffn2_w_down"],
        name="ffn_bwd_tokens", dep=dep)
    shards = lambda gs: [g.reshape(NSH, FSH, DM) for g in gs]
    token = emit("ffn2", shards(ffn_bwd_weights(sv["h2"], dy3_bf, act, dg, du, name="ffn_bwd_weights")))
    dmerged = mm(dy2, w["wo"], tb=True, name="out_proj_dx", tm=1024, dep=token)
    gw_out = mm(sv["merged"], dy2, ta=True, out_dtype=BF16, name="out_proj_dw").reshape(NSH, DM // NSH, DM)
    dpa, dps, dzg, small["b_gate"] = gate_bwd(dmerged, sv["zg"], row(p["b_gate"]), sv["pa"], sv["ps"], name="gate_bwd")
    gw_na = _full_to_cols(mm(sv["o_na"], dpa, ta=True, out_dtype=BF16, name="branch_dw"))
    gw_sw = _full_to_cols(mm(sv["o_sw"], dps, ta=True, out_dtype=BF16, name="branch_dw"))
    do_na = mm(dpa, w["wa"], tb=True, out_dtype=BF16, tm=SEQ, name="branch_dx")
    do_sw = mm(dps, w["ws"], tb=True, out_dtype=BF16, tm=SEQ, name="branch_dx")
    dqa, dka, dva, dbias = na_bwd(sv["qa"], sv["ka"], sv["va"], sv["o_na"], do_na, sv["lse_na"], sv["bias"], name="na_bwd")
    dqs, dkvp, dt5, dsink = sw_bwd(sv["qs"], sv["kvp"], sv["o_sw"], do_sw, sv["lse_sw"], t5b, sv["sink"], name="sw_bwd")
    dkv = dkvp[SW_BLK:SW_BLK + SEQ]
    dz, dgqa, dgka, dgqs, dgks = qknorm_bwd(sv["z"], dqa, dka, dva, dqs, dkv, p["na_q_norm"], p["na_k_norm"],
                                            p["sw_q_norm"], p["sw_k_norm"], name="qknorm_bwd")
    small["na_q_norm"], small["na_k_norm"], small["sw_q_norm"], small["sw_k_norm"] = fold(dgqa), fold(dgka), fold(dgqs), fold(dgks)
    small["na_rpb"] = rpb_grad(dbias, name="rpb_grad")
    small["sw_sink"] = dsink
    gw_att_t = mm(dz, sv["hm"], ta=True, out_dtype=BF16, tm=768, name="proj_att_dw")
    gw_gz_t = mm(dzg, sv["hm"], ta=True, out_dtype=BF16, tm=1024, name="proj_gate_dw")
    gw_in = jnp.concatenate([gw_att_t, gw_gz_t], axis=0).reshape(NSH, IN_W // NSH, DM)
    token = emit("mix", (gw_in, gw_na, gw_sw, gw_out))
    dh = mm(dz, w["w_att_t"], tm=1024, name="proj_att_dx", dep=token)
    dh = mm(dzg, w["w_gz_t"], add=dh, tm=1024, name="proj_gate_dx")
    dy1, dy1_bf, small["mix_norm"] = rms_bwd(dh, sv["y1"], row(p["mix_norm"]), dy2, name="mix_norm_bwd")
    dx, dx_bf, small["ffn1_norm"], act, dg, du = ffn_bwd_tokens(
        dy1, sv["x"], row(p["ffn1_norm"]), sv["gg1"], sv["uu1"], g1["ffn1_w_gate"], g1["ffn1_w_up"], g1["ffn1_w_down"],
        name="ffn_bwd_tokens")
    emit("ffn1", shards(ffn_bwd_weights(sv["h1"], dy1_bf, act, dg, du, name="ffn_bwd_weights")))
    return dx, dx_bf, small, dt5


ANY = pl.BlockSpec(memory_space=pl.ANY)


def _place():
    x, y, c = lax.axis_index("x"), lax.axis_index("y"), lax.axis_index("c")
    chips = [(1 - x, y), (x, 1 - y), (1 - x, 1 - y)]
    return x, y, c, chips


def _remote(src, dst, send_sem, recv_sem, to):
    return pltpu.make_async_remote_copy(src_ref=src, dst_ref=dst, send_sem=send_sem, recv_sem=recv_sem, device_id=to,
                                        device_id_type=MESH)


HBM = pl.BlockSpec(memory_space=pltpu.HBM)
SEM = pl.BlockSpec(memory_space=pltpu.SEMAPHORE)
ORDERED_EFFECT = pltpu.SideEffectType.DATAFLOW_SIDE_EFFECTING


def _in_hbm(v):
    return pltpu.with_memory_space_constraint(v, pltpu.HBM)


def _row_half(ref_shape_rows, c):
    half = ref_shape_rows // 2
    return pl.ds(c * half, half)


def _ici_gather_copies(w, land, send_sems, recv_sems):
    x, y, c, chips = _place()
    me = 2 * x + y
    copies = []
    for a in range(len(w)):
        rows = _row_half(w[a].shape[0], c)
        for k, chip in enumerate(chips):
            copies.append(_remote(w[a].at[rows], land[a].at[me, rows], send_sems.at[4 * a + k], recv_sems.at[4 * a + k],
                                  (*chip, c)))
        copies.append(_remote(w[a], land[a].at[me], send_sems.at[4 * a + 3], recv_sems.at[4 * a + 3], (x, y, 1 - c)))
    return copies


def _d2d_gather_copies(w, land, send_sems, recv_sems):
    x, y, c, chips = _place()
    copies = []
    for a in range(len(w)):
        rows = _row_half(w[a].shape[0], c)
        for k, (cx, cy) in enumerate(chips):
            blk = land[a].at[2 * cx + cy, rows]
            copies.append(_remote(blk, blk, send_sems.at[3 * a + k], recv_sems.at[3 * a + k], (x, y, 1 - c)))
    return copies


def _d2d_gather_waits(w, land, send_sems, recv_sems):
    x, y, c, chips = _place()
    waits = []
    for a in range(len(w)):
        rows = _row_half(w[a].shape[0], 1 - c)
        for k, (cx, cy) in enumerate(chips):
            blk = land[a].at[2 * cx + cy, rows]
            waits.append(_remote(blk, blk, send_sems.at[3 * a + k], recv_sems.at[3 * a + k], (x, y, 1 - c)))
    return waits


def gather_start(groups, dep=None, *, name):
    sizes = [len(g) for g in groups]
    shards = [s for g in groups for s in g]
    n, ng = len(shards), len(groups)
    extra = [] if dep is None else [dep]

    def body(*refs):
        first_out = 2 * n + len(extra)
        w, land, sems = refs[:n], refs[n:2 * n], refs[first_out:first_out + 2 * ng]
        off = 0
        for gi, size in enumerate(sizes):
            for cp in _ici_gather_copies(w[off:off + size], land[off:off + size], sems[2 * gi], sems[2 * gi + 1]):
                cp.start()
            off += size

    lands = [lax.empty((NSH,) + s.shape, s.dtype) for s in shards]
    sem_shapes = tuple(pltpu.SemaphoreType.DMA((4 * size,)) for size in sizes for _ in range(2))
    res = pl.pallas_call(
        body, name=name,
        out_shape=sem_shapes + tuple(pltpu.HBM(s.shape, s.dtype) for s in shards) + tuple(pltpu.HBM(l.shape, l.dtype) for l in lands),
        in_specs=[HBM] * (2 * n) + [ANY] * len(extra), out_specs=(SEM,) * (2 * ng) + (HBM,) * (2 * n),
        input_output_aliases={i: 2 * ng + i for i in range(2 * n)},
        compiler_params=pltpu.CompilerParams(has_side_effects=ORDERED_EFFECT))(
            *[_in_hbm(s) for s in shards], *[_in_hbm(l) for l in lands], *extra)
    out, off = [], 0
    for gi, size in enumerate(sizes):
        out.append((res[2 * gi], res[2 * gi + 1], list(res[2 * ng + off:2 * ng + off + size]),
                    list(res[2 * ng + n + off:2 * ng + n + off + size])))
        off += size
    return out


def gather_wait(send_sems, recv_sems, shards, lands, after, *, name):
    n = len(shards)

    def body(*refs):
        w, land = refs[:n], refs[n:2 * n]
        send, recv = refs[2 * n:2 * n + 2]
        for cp in _ici_gather_copies(w, land, send, recv):
            cp.wait_send()
            cp.wait_recv()

    res = pl.pallas_call(
        body, name=name,
        out_shape=tuple(pltpu.HBM(s.shape, s.dtype) for s in shards) + tuple(pltpu.HBM(l.shape, l.dtype) for l in lands),
        in_specs=[HBM] * (2 * n) + [SEM, SEM] + [ANY] * len(after), out_specs=(HBM,) * (2 * n),
        input_output_aliases={i: i for i in range(2 * n)},
        compiler_params=pltpu.CompilerParams(has_side_effects=ORDERED_EFFECT))(*shards, *lands, send_sems, recv_sems, *after)
    return list(res[:n]), list(res[n:])


def gather_finish(shards, lands, *, name):
    n = len(shards)

    def body(*refs):
        w, land = refs[:n], refs[n:2 * n]
        send_sems, recv_sems = refs[3 * n:]
        d2d = _d2d_gather_copies(w, land, send_sems, recv_sems)
        for cp in d2d:
            cp.start()
        for cp in _d2d_gather_waits(w, land, send_sems, recv_sems):
            cp.wait_recv()
        for cp in d2d:
            cp.wait_send()

    return list(pl.pallas_call(
        body, name=name, out_shape=tuple(pltpu.HBM(l.shape, l.dtype) for l in lands),
        in_specs=[ANY] * (2 * n), out_specs=tuple([ANY] * n), input_output_aliases={n + i: i for i in range(n)},
        scratch_shapes=[pltpu.SemaphoreType.DMA((3 * n,)), pltpu.SemaphoreType.DMA((3 * n,))])(*shards, *lands))


def _pair_exchange_copies(g, buf, send_sems, recv_sems):
    x, y, c, _ = _place()
    copies = []
    for a in range(len(g)):
        half = g[a].shape[1] // 2
        copies.append(_remote(g[a].at[:, pl.ds((1 - c) * half, half)], buf[a], send_sems.at[a], recv_sems.at[a], (x, y, 1 - c)))
    return copies


def pair_exchange_start(grads, dep=None, *, name):
    n = len(grads)
    extra = [] if dep is None else [dep]

    def body(*refs):
        sems = refs[2 * n + len(extra):]
        for cp in _pair_exchange_copies(refs[:n], refs[n:2 * n], sems[0], sems[1]):
            cp.start()
        refs[-1][...] = jnp.zeros_like(refs[-1])

    lands = [lax.empty((NSH, g.shape[1] // 2, g.shape[2]), g.dtype) for g in grads]
    res = pl.pallas_call(
        body, name=name,
        out_shape=(pltpu.SemaphoreType.DMA((n,)), pltpu.SemaphoreType.DMA((n,)))
        + tuple(pltpu.HBM(g.shape, g.dtype) for g in grads) + tuple(pltpu.HBM(l.shape, l.dtype) for l in lands)
        + (_sds((8, 128), F32),),
        in_specs=[HBM] * (2 * n) + [ANY] * len(extra),
        out_specs=(SEM, SEM) + (HBM,) * (2 * n) + (pl.BlockSpec(memory_space=pltpu.VMEM),),
        input_output_aliases={i: 2 + i for i in range(2 * n)},
        compiler_params=pltpu.CompilerParams(has_side_effects=ORDERED_EFFECT))(
            *[_in_hbm(g) for g in grads], *[_in_hbm(l) for l in lands], *extra)
    return res[0], res[1], list(res[2:2 + n]), list(res[2 + n:2 + 2 * n]), res[-1]


def pair_exchange_wait(send_sems, recv_sems, grads, lands, after, *, name):
    n = len(grads)

    def body(*refs):
        for cp in _pair_exchange_copies(refs[:n], refs[n:2 * n], refs[2 * n], refs[2 * n + 1]):
            cp.wait_send()
            cp.wait_recv()

    res = pl.pallas_call(
        body, name=name,
        out_shape=tuple(pltpu.HBM(g.shape, g.dtype) for g in grads) + tuple(pltpu.HBM(l.shape, l.dtype) for l in lands),
        in_specs=[HBM] * (2 * n) + [SEM, SEM] + [ANY] * len(after), out_specs=(HBM,) * (2 * n),
        input_output_aliases={i: i for i in range(2 * n)},
        compiler_params=pltpu.CompilerParams(has_side_effects=ORDERED_EFFECT))(*grads, *lands, send_sems, recv_sems, *after)
    return list(res[:n]), list(res[n:])


def _chip_exchange_copies(s, buf, send_sems, recv_sems):
    x, y, c, chips = _place()
    return [_remote(s[a].at[2 * cx + cy], buf[a].at[k], send_sems.at[3 * a + k], recv_sems.at[3 * a + k], (cx, cy, c))
            for a in range(len(s)) for k, (cx, cy) in enumerate(chips)]


def chip_exchange(sums, *, name):
    n = len(sums)

    def body(*refs):
        copies = _chip_exchange_copies(refs[:n], refs[n:2 * n], *refs[2 * n:])
        for cp in copies:
            cp.start()
        for cp in copies:
            cp.wait()

    return pl.pallas_call(
        body, name=name, out_shape=tuple(pltpu.HBM((3,) + s.shape[1:], s.dtype) for s in sums),
        in_specs=[ANY] * n, out_specs=tuple([ANY] * n),
        scratch_shapes=[pltpu.SemaphoreType.DMA((3 * n,)), pltpu.SemaphoreType.DMA((3 * n,))])(*sums)


def chip_exchange_start(sums, *, name):
    n = len(sums)

    def body(*refs):
        for cp in _chip_exchange_copies(refs[:n], refs[n:2 * n], refs[2 * n], refs[2 * n + 1]):
            cp.start()
        refs[-1][...] = jnp.zeros_like(refs[-1])

    lands = [lax.empty((3,) + s.shape[1:], s.dtype) for s in sums]
    res = pl.pallas_call(
        body, name=name,
        out_shape=(pltpu.SemaphoreType.DMA((3 * n,)), pltpu.SemaphoreType.DMA((3 * n,)))
        + tuple(pltpu.HBM(s.shape, s.dtype) for s in sums) + tuple(pltpu.HBM(l.shape, l.dtype) for l in lands)
        + (_sds((8, 128), F32),),
        in_specs=[HBM] * (2 * n), out_specs=(SEM, SEM) + (HBM,) * (2 * n) + (pl.BlockSpec(memory_space=pltpu.VMEM),),
        input_output_aliases={i: 2 + i for i in range(2 * n)},
        compiler_params=pltpu.CompilerParams(has_side_effects=ORDERED_EFFECT))(
            *[_in_hbm(s) for s in sums], *[_in_hbm(l) for l in lands])
    return res[0], res[1], list(res[2:2 + n]), list(res[2 + n:2 + 2 * n]), res[-1]


def chip_exchange_wait(send_sems, recv_sems, sums, lands, after, *, name):
    n = len(sums)

    def body(*refs):
        for cp in _chip_exchange_copies(refs[:n], refs[n:2 * n], refs[2 * n], refs[2 * n + 1]):
            cp.wait_send()
            cp.wait_recv()

    res = pl.pallas_call(
        body, name=name,
        out_shape=tuple(pltpu.HBM(s.shape, s.dtype) for s in sums) + tuple(pltpu.HBM(l.shape, l.dtype) for l in lands),
        in_specs=[HBM] * (2 * n) + [SEM, SEM] + [ANY] * len(after), out_specs=(HBM,) * (2 * n),
        input_output_aliases={i: i for i in range(2 * n)},
        compiler_params=pltpu.CompilerParams(has_side_effects=ORDERED_EFFECT))(*sums, *lands, send_sems, recv_sems, *after)
    return list(res[:n]), list(res[n:])


def _pair_send_copies(h, got, send_sems, recv_sems):
    x, y, c, _ = _place()
    return [_remote(h[i], got[i], send_sems.at[i], recv_sems.at[i], (x, y, 1 - c)) for i in range(len(h))]


def pair_send_start(halves, *, name):
    n = len(halves)

    def body(*refs):
        for cp in _pair_send_copies(refs[:n], refs[n:2 * n], refs[2 * n], refs[2 * n + 1]):
            cp.start()
        refs[-1][...] = jnp.zeros_like(refs[-1])

    lands = [lax.empty(h.shape, h.dtype) for h in halves]
    res = pl.pallas_call(
        body, name=name,
        out_shape=(pltpu.SemaphoreType.DMA((n,)), pltpu.SemaphoreType.DMA((n,)))
        + tuple(pltpu.HBM(h.shape, h.dtype) for h in halves) * 2 + (_sds((8, 128), F32),),
        in_specs=[HBM] * (2 * n), out_specs=(SEM, SEM) + (HBM,) * (2 * n) + (pl.BlockSpec(memory_space=pltpu.VMEM),),
        input_output_aliases={i: 2 + i for i in range(2 * n)},
        compiler_params=pltpu.CompilerParams(has_side_effects=ORDERED_EFFECT))(
            *[_in_hbm(h) for h in halves], *[_in_hbm(l) for l in lands])
    return res[0], res[1], list(res[2:2 + n]), list(res[2 + n:2 + 2 * n]), res[-1]


def pair_send_wait(send_sems, recv_sems, halves, lands, after, *, name):
    n = len(halves)

    def body(*refs):
        for cp in _pair_send_copies(refs[:n], refs[n:2 * n], refs[2 * n], refs[2 * n + 1]):
            cp.wait_send()
            cp.wait_recv()

    res = pl.pallas_call(
        body, name=name, out_shape=tuple(pltpu.HBM(h.shape, h.dtype) for h in halves) * 2,
        in_specs=[HBM] * (2 * n) + [SEM, SEM] + [ANY] * len(after), out_specs=(HBM,) * (2 * n),
        input_output_aliases={i: i for i in range(2 * n)},
        compiler_params=pltpu.CompilerParams(has_side_effects=ORDERED_EFFECT))(*halves, *lands, send_sems, recv_sems, *after)
    return list(res[:n]), list(res[n:])


def allreduce_small(v, *, name):
    rows = v.shape[0]

    def body(v_ref, o_ref, gath, send_sems, recv_sems):
        x, y, c, _ = _place()
        me = 4 * x + 2 * y + c
        gath[me] = v_ref[...]
        copies = []
        for k in range(1, 8):
            fx, fy, fc = (k >> 2) & 1, (k >> 1) & 1, k & 1
            peer = (jnp.where(fx, 1 - x, x), jnp.where(fy, 1 - y, y), jnp.where(fc, 1 - c, c))
            cp = _remote(v_ref, gath.at[me], send_sems.at[k - 1], recv_sems.at[k - 1], peer)
            cp.start()
            copies.append(cp)
        for cp in copies:
            cp.wait()
        acc = gath[0]
        for d in range(1, 8):
            acc = acc + gath[d]
        o_ref[...] = acc

    return pl.pallas_call(
        body, name=name, out_shape=_sds(v.shape, F32),
        in_specs=[pl.BlockSpec(memory_space=pltpu.VMEM)], out_specs=pl.BlockSpec(memory_space=pltpu.VMEM),
        scratch_shapes=[pltpu.VMEM((8, rows, 128), F32), pltpu.SemaphoreType.DMA((7,)), pltpu.SemaphoreType.DMA((7,))])(v)


def _same_shape_runs(arrays):
    runs = {}
    for i, a in enumerate(arrays):
        runs.setdefault(a.shape, []).append(i)
    return list(runs.values())


def _per_shape(fn, *lists):
    out = [None] * len(lists[0])
    for idx in _same_shape_runs(lists[0]):
        for i, r in zip(idx, fn(*[[l[i] for i in idx] for l in lists])):
            out[i] = r
    return out


def add_halves(gs, bufs, cidx, *, name):
    cnt = len(gs)
    _, k, n = gs[0].shape

    def body(c_ref, *refs):
        g, b, o = refs[:cnt], refs[cnt:2 * cnt], refs[2 * cnt:]
        for i in range(cnt):
            o[i][...] = (g[i][...].astype(F32) + b[i][...].astype(F32)).astype(BF16)

    blk = pl.BlockSpec((None, k // 2, n), lambda s, c: (s, 0, 0))
    mine = pl.BlockSpec((None, k // 2, n), lambda s, c: (s, c[0], 0))
    return list(pl.pallas_call(
        body, name=name, out_shape=tuple(_sds(b.shape, BF16) for b in bufs),
        grid_spec=pltpu.PrefetchScalarGridSpec(
            num_scalar_prefetch=1, grid=(NSH,), in_specs=[mine] * cnt + [blk] * cnt, out_specs=tuple([blk] * cnt)),
        compiler_params=_params(("parallel",)))(cidx, *gs, *bufs))


def add_chips(sums, bufs, sidx, *, name):
    cnt = len(sums)
    _, kh, n = sums[0].shape

    def body(s_ref, *refs):
        mine, b, o = refs[:cnt], refs[cnt:2 * cnt], refs[2 * cnt:]
        for i in range(cnt):
            o[i][...] = ((mine[i][...].astype(F32) + b[i][0].astype(F32)) + (b[i][1].astype(F32) + b[i][2].astype(F32)))

    own = pl.BlockSpec((None, kh, n), lambda i, s: (s[0], 0, 0))
    got = pl.BlockSpec((3, kh, n), lambda i, s: (0, 0, 0))
    out = pl.BlockSpec((kh, n), lambda i, s: (0, 0))
    return list(pl.pallas_call(
        body, name=name, out_shape=tuple(_sds((kh, n), F32) for _ in sums),
        grid_spec=pltpu.PrefetchScalarGridSpec(
            num_scalar_prefetch=1, grid=(1,), in_specs=[own] * cnt + [got] * cnt, out_specs=tuple([out] * cnt)),
        compiler_params=_params(("arbitrary",)))(sidx, *sums, *bufs))


PARAMS = ("ffn1_norm", "ffn1_w_gate", "ffn1_w_up", "ffn1_w_down", "mix_norm", "w_in", "b_gate", "na_q_norm", "na_k_norm",
          "na_rpb", "sw_q_norm", "sw_k_norm", "sw_sink", "t5_rel_table", "w_branch_na", "w_branch_sw", "w_out", "ffn2_norm",
          "ffn2_w_gate", "ffn2_w_up", "ffn2_w_down")
SMALL_ALL = tuple(n for n in PARAMS if n not in BIG)
TRANSPOSED = ("ffn1_w_gate", "ffn1_w_up", "w_in", "ffn2_w_gate", "ffn2_w_up")
SMALL_ROWS = 152


def _pack_small(vals):
    flat = jnp.concatenate([vals[n].reshape(-1).astype(F32) for n in SMALL_ALL] + [vals["loss"].reshape(-1)])
    return jnp.pad(flat, (0, SMALL_ROWS * 128 - flat.shape[0])).reshape(SMALL_ROWS, 128)


def _unpack_small(packed, like):
    flat, out, off = packed.reshape(-1), {}, 0
    for n in SMALL_ALL:
        size = math.prod(like[n].shape)
        out[n] = flat[off:off + size].reshape(like[n].shape)
        off += size
    out["loss"] = flat[off]
    return out


def kernel(x, ffn1_norm, ffn1_w_gate, ffn1_w_up, ffn1_w_down, mix_norm, w_in, b_gate, na_q_norm, na_k_norm, na_rpb, sw_q_norm, sw_k_norm, sw_sink, t5_rel_table, w_branch_na, w_branch_sw, w_out, ffn2_norm, ffn2_w_gate, ffn2_w_up, ffn2_w_down, loss_target, m_ffn1_norm, m_ffn1_w_gate, m_ffn1_w_up, m_ffn1_w_down, m_mix_norm, m_w_in, m_b_gate, m_na_q_norm, m_na_k_norm, m_na_rpb, m_sw_q_norm, m_sw_k_norm, m_sw_sink, m_t5_rel_table, m_w_branch_na, m_w_branch_sw, m_w_out, m_ffn2_norm, m_ffn2_w_gate, m_ffn2_w_up, m_ffn2_w_down, v_ffn1_norm, v_ffn1_w_gate, v_ffn1_w_up, v_ffn1_w_down, v_mix_norm, v_w_in, v_b_gate, v_na_q_norm, v_na_k_norm, v_na_rpb, v_sw_q_norm, v_sw_k_norm, v_sw_sink, v_t5_rel_table, v_w_branch_na, v_w_branch_sw, v_w_out, v_ffn2_norm, v_ffn2_w_gate, v_ffn2_w_up, v_ffn2_w_down):
    args = locals()
    tr = lambda n, a: jnp.transpose(a, (0, 2, 1)) if n in TRANSPOSED else a
    w = {n: tr(n, args[n]) for n in PARAMS}
    m = {n: tr(n, args["m_" + n]) for n in PARAMS}
    v = {n: tr(n, args["v_" + n]) for n in PARAMS}
    cidx = lax.axis_index("c").astype(jnp.int32).reshape(1)
    sidx = (2 * lax.axis_index("x") + lax.axis_index("y")).astype(jnp.int32).reshape(1)

    small = [{n: w[n][l] for n in SMALL} for l in range(DEPTH)]
    order = ("ffn1", "mix", "ffn2")

    keys = [(l, g) for l in range(DEPTH) for g in order]
    local = lambda l, g: [w[n][l].astype(BF16) for n in GROUPS[g]]
    first = gather_start([local(*keys[0])], name="gather_start")
    rest = gather_start([local(*key) for key in keys[1:]], first[0][2][0], name="gather_start")
    in_flight = dict(zip(keys, first + rest))
    t5b = t5_bias(w["t5_rel_table"], name="t5_bias")
    for l in range(DEPTH):
        small[l]["na_bias"] = na_bias_table(small[l]["na_rpb"], name="na_bias_table")
    early = [t5b] + [small[l]["na_bias"] for l in range(DEPTH)] + [rest[0][2][0]]

    def weights_of(l):
        def get(group, after):
            send_sems, recv_sems, thru, lands = in_flight[(l, group)]
            after = [after] + (early if (l, group) == keys[0] else [])
            thru, lands = gather_wait(send_sems, recv_sems, thru, lands, after, name="gather_wait")
            return dict(zip(GROUPS[group], gather_finish(thru, lands, name="gather_finish")))
        return get

    h0, saved0 = layer_fwd(x[0], small[0], weights_of(0), t5b)
    h1, saved1 = layer_fwd(h0, small[1], weights_of(1), t5b)
    dy, dy_bf, loss_row = loss_head(h1, loss_target[0], name="loss_head")

    crossing, tokens, pending = {}, [], []

    def ship(after):
        key, send_sems, recv_sems, grads, lands = pending.pop()
        grads, from_sibling = pair_exchange_wait(send_sems, recv_sems, grads, lands, after, name="pair_exchange_wait")
        sums = _per_shape(lambda gs, bs: add_halves(gs, bs, cidx, name="add_halves"), grads, from_sibling)
        send_sems, recv_sems, sums, lands, token = chip_exchange_start(sums, name="chip_exchange_start")
        crossing[key] = (send_sems, recv_sems, sums, lands)
        return token

    def reduce_of(l):
        def emit(group, grads):
            grads = list(grads)
            shipped = ship([grads[0]]) if pending else None
            send_sems, recv_sems, grads, lands, token = pair_exchange_start(grads, shipped, name="pair_exchange_start")
            pending.append(((l, group), send_sems, recv_sems, grads, lands))
            tokens.append(token)
            return token
        return emit

    def finish(layer, after, filled=None):
        sent = {}
        for group in order:
            send_sems, recv_sems, sums, lands = crossing[(layer, group)]
            sums, got = chip_exchange_wait(send_sems, recv_sems, sums, lands, after, name="chip_exchange_wait")
            halves = _per_shape(lambda ss, bs: add_chips(ss, bs, sidx, name="add_chips"), sums, got)
            sent[group] = pair_send_start(halves, name="pair_send_start")
            after = [sent[group][4]]
        out = {}
        for group in order:
            send_sems, recv_sems, halves, lands, _ = sent[group]
            halves, theirs = pair_send_wait(send_sems, recv_sems, halves, lands, after, name="pair_send_wait")
            names = GROUPS[group]
            res = _per_shape(
                lambda ws, ms, vs, a, b, *f: adamw_layer(ws, ms, vs, a, b, cidx, layer, list(f[0]) if f else None, name="adamw_layer"),
                *([[w[n] for n in names], [m[n] for n in names], [v[n] for n in names], halves, theirs]
                  + ([[filled[n] for n in names]] if filled is not None else [])))
            out.update(zip(names, res))
            after = [res[-1][0]]
        return out

    dy, dy_bf, small1, dt5_1 = layer_bwd(dy, dy_bf, saved1, small[1], t5b, reduce_of(1))
    grad_x, _, small0, dt5_0 = layer_bwd(dy, dy_bf, saved0, small[0], t5b, reduce_of(0), dep=tokens[-1])
    done1 = finish(1, [ship([grad_x])])

    smalls = [small0, small1]
    dt5 = t5_table_grad(dt5_0, dt5_1, name="t5_table_grad").reshape(32, 8)
    local_small = {n: jnp.stack([smalls[l][n].reshape(w[n].shape[1:]) for l in range(DEPTH)]) for n in SMALL}
    local_small["t5_rel_table"] = dt5
    local_small["loss"] = loss_row[0, 0:1]
    total = allreduce_small(_pack_small(local_small), name="allreduce_small")
    small_grads = _unpack_small(total, w)
    small_done = adamw_small([w[n] for n in SMALL_ALL], [small_grads[n] for n in SMALL_ALL], [m[n] for n in SMALL_ALL],
                             [v[n] for n in SMALL_ALL], name="adamw_small")

    grad, delta, new_m, new_v = {}, {}, {}, {}
    for n, done in finish(0, [small_done[0][0], done1[BIG[-1]][0]], filled=done1).items():
        grad[n], delta[n], new_m[n], new_v[n] = done
    for n, done in zip(SMALL_ALL, small_done):
        grad[n] = small_grads[n]
        delta[n], new_m[n], new_v[n] = done

    return (small_grads["loss"], grad_x[None], *[tr(n, grad[n]) for n in PARAMS], *[tr(n, delta[n]) for n in PARAMS],
            *[tr(n, new_m[n]) for n in PARAMS], *[tr(n, new_v[n]) for n in PARAMS])
```

```python
import functools
import math

import jax
import jax.numpy as jnp
import numpy as np
from jax import lax
from jax.experimental import pallas as pl
from jax.experimental.pallas import tpu as pltpu

F32 = jnp.float32
BF16 = jnp.bfloat16

SEQ = 2048
DM = 1024
DFF = 2816
DEPTH = 2
NSH = 4
FSH = DFF // NSH
GRID_W = 64
ROWS = SEQ // GRID_W
NA_HEADS = 8
HD = 64
NA_WR = 8
NA_WC = 16
NA_KEYS = NA_WR * GRID_W
SW_BLK = 128
SW_NB = SEQ // SW_BLK
SW_KEYS = 3 * SW_BLK
ATT_W = 2304
GATE_W = 2048
IN_W = ATT_W + GATE_W
EPS = 1e-6
NEG = -1e30
QK_SCALE = 1.0 / math.sqrt(HD)

ADAM_LR = 0.001
ADAM_B1 = 0.9
ADAM_B2 = 0.999
ADAM_EPS = 1e-08
ADAM_WD = 0.01
ADAM_STEP = 10

VMEM_LIMIT = 56 << 20
MESH = pl.DeviceIdType.MESH

NT = (((1,), (1,)), ((), ()))
TN = (((0,), (0,)), ((), ()))
NN = (((1,), (0,)), ((), ()))


def _dot(a, b, dims=NN):
    return lax.dot_general(a, b, dims, preferred_element_type=F32)


def _params(sem=None):
    return pltpu.CompilerParams(dimension_semantics=sem, vmem_limit_bytes=VMEM_LIMIT)


def _sds(shape, dtype):
    return jax.ShapeDtypeStruct(shape, dtype)


def mm(a, b, *, name, ta=False, tb=False, out_dtype=F32, add=None, scale=None, tm=512, tn=None, tk=None, exact=False,
       dep=None):
    m, kd = (a.shape[1], a.shape[0]) if ta else a.shape
    n = b.shape[0] if tb else b.shape[1]
    tm, tn, tk = min(tm, m), min(tn or n, n), min(tk or kd, kd)
    nk = kd // tk
    dims = (((0 if ta else 1,), (1 if tb else 0,)), ((), ()))

    def body(*refs):
        a_ref, b_ref = refs[:2]
        add_ref = refs[2] if add is not None else None
        o_ref, acc = refs[-2:]
        k = pl.program_id(2)

        @pl.when(k == 0)
        def _():
            acc[...] = jnp.zeros_like(acc)

        if exact:
            acc[...] += lax.dot_general(a_ref[...], b_ref[...], dims, precision=lax.Precision.HIGHEST,
                                        preferred_element_type=F32)
        else:
            acc[...] += lax.dot_general(a_ref[...].astype(BF16), b_ref[...].astype(BF16), dims,
                                        preferred_element_type=F32)

        @pl.when(k == nk - 1)
        def _():
            r = acc[...]
            if scale is not None:
                r = r * scale
            if add is not None:
                r = r + add_ref[...]
            o_ref[...] = r.astype(out_dtype)

    a_spec = pl.BlockSpec((tk, tm), lambda i, j, k: (k, i)) if ta else pl.BlockSpec((tm, tk), lambda i, j, k: (i, k))
    b_spec = pl.BlockSpec((tn, tk), lambda i, j, k: (j, k)) if tb else pl.BlockSpec((tk, tn), lambda i, j, k: (k, j))
    o_spec = pl.BlockSpec((tm, tn), lambda i, j, k: (i, j))
    ins, specs = [a, b], [a_spec, b_spec]
    if add is not None:
        ins.append(add)
        specs.append(o_spec)
    if dep is not None:
        ins.append(dep)
        specs.append(pl.BlockSpec(memory_space=pl.ANY))
    return pl.pallas_call(
        body, name=name, out_shape=_sds((m, n), out_dtype), grid=(m // tm, n // tn, nk), in_specs=specs,
        out_specs=o_spec, scratch_shapes=[pltpu.VMEM((tm, tn), F32)],
        compiler_params=_params(("parallel", "parallel", "arbitrary")))(*ins)


def _rms(x):
    return lax.rsqrt(jnp.mean(x * x, axis=-1, keepdims=True) + EPS)


def rms_fwd(x, gain, *, name, tm=512):
    def body(x_ref, g_ref, h_ref):
        x = x_ref[...]
        h_ref[...] = (x * _rms(x) * g_ref[...]).astype(BF16)

    return pl.pallas_call(
        body, name=name, out_shape=_sds(x.shape, BF16), grid=(x.shape[0] // tm,),
        in_specs=[pl.BlockSpec((tm, DM), lambda i: (i, 0)), pl.BlockSpec((1, DM), lambda i: (0, 0))],
        out_specs=pl.BlockSpec((tm, DM), lambda i: (i, 0)), compiler_params=_params(("parallel",)))(x, gain)


def _rms_bwd_math(dh, x, gain):
    r = _rms(x)
    xh = x * r
    dgain = jnp.sum(dh * xh, axis=0, keepdims=True)
    dxn = dh * gain
    dx = r * (dxn - xh * jnp.mean(dxn * xh, axis=-1, keepdims=True))
    return dx, dgain


def rms_bwd(dh, x, gain, dres, *, name, tm=512):
    def body(dh_ref, x_ref, g_ref, dres_ref, dx_ref, dxb_ref, dg_ref):
        @pl.when(pl.program_id(0) == 0)
        def _():
            dg_ref[...] = jnp.zeros_like(dg_ref)

        dx, dg = _rms_bwd_math(dh_ref[...], x_ref[...], g_ref[...])
        dx = dres_ref[...] + dx
        dx_ref[...] = dx
        dxb_ref[...] = dx.astype(BF16)
        dg_ref[...] += dg

    tile = pl.BlockSpec((tm, DM), lambda i: (i, 0))
    vec = pl.BlockSpec((1, DM), lambda i: (0, 0))
    return pl.pallas_call(
        body, name=name, out_shape=(_sds(x.shape, F32), _sds(x.shape, BF16), _sds((1, DM), F32)), grid=(x.shape[0] // tm,),
        in_specs=[tile, tile, vec, tile], out_specs=(tile, tile, vec), compiler_params=_params(("arbitrary",)))(dh, x, gain, dres)


def _with_dep(ins, specs, dep):
    if dep is None:
        return ins, specs
    return ins + [dep], specs + [pl.BlockSpec(memory_space=pl.ANY)]


def _resident_weights(count):
    return [pltpu.VMEM((DFF, DM), BF16) for _ in range(count)] + [pltpu.SemaphoreType.DMA((count,))]


def _weights_at_first_step(hbm, vmem, sems):
    first = pl.program_id(0) == 0
    copies = [pltpu.make_async_copy(hbm[i], vmem[i], sems.at[i]) for i in range(len(hbm))]

    @pl.when(first)
    def _():
        for cp in copies:
            cp.start()

    def arrive(i):
        pl.when(first)(copies[i].wait)

    return arrive


def ffn_fwd(x, gain, wg, wu, wd, *, name, tm=512):
    def body(x_ref, g_ref, wg_hbm, wu_hbm, wd_hbm, y_ref, h_ref, gg_ref, uu_ref, wg_ref, wu_ref, wd_ref, sems):
        arrive = _weights_at_first_step([wg_hbm, wu_hbm, wd_hbm], [wg_ref, wu_ref, wd_ref], sems)
        x = x_ref[...]
        h = (x * _rms(x) * g_ref[...]).astype(BF16)
        h_ref[...] = h
        arrive(0)
        gg = _dot(h, wg_ref[...], NT)
        arrive(1)
        uu = _dot(h, wu_ref[...], NT)
        gg_ref[...] = gg.astype(BF16)
        uu_ref[...] = uu.astype(BF16)
        act = (gg * jax.nn.sigmoid(gg) * uu).astype(BF16)
        arrive(2)
        y_ref[...] = x + 0.5 * _dot(act, wd_ref[...])

    s = x.shape[0]
    tile = pl.BlockSpec((tm, DM), lambda i: (i, 0))
    hid = pl.BlockSpec((tm, DFF), lambda i: (i, 0))
    w = pl.BlockSpec(memory_space=pl.ANY)
    return pl.pallas_call(
        body, name=name,
        out_shape=(_sds((s, DM), F32), _sds((s, DM), BF16), _sds((s, DFF), BF16), _sds((s, DFF), BF16)),
        grid=(s // tm,), in_specs=[tile, pl.BlockSpec((1, DM), lambda i: (0, 0)), w, w, w],
        out_specs=(tile, tile, hid, hid), scratch_shapes=_resident_weights(3),
        compiler_params=_params(("arbitrary",)))(x, gain, wg, wu, wd)


def ffn_bwd_tokens(dy, x, gain, gg, uu, wg, wu, wd, *, name, tm=256, dep=None):
    def body(dy_ref, x_ref, g_ref, gg_ref, uu_ref, wg_hbm, wu_hbm, wd_hbm, *rest):
        dx_ref, dxb_ref, dgain_ref, act_ref, dg_ref, du_ref, wd_ref, wg_ref, wu_ref, sems = rest[-10:]
        arrive = _weights_at_first_step([wd_hbm, wg_hbm, wu_hbm], [wd_ref, wg_ref, wu_ref], sems)

        @pl.when(pl.program_id(0) == 0)
        def _():
            dgain_ref[...] = jnp.zeros_like(dgain_ref)

        dy = dy_ref[...]
        arrive(0)
        dact = _dot((0.5 * dy).astype(BF16), wd_ref[...], NT)
        g = gg_ref[...].astype(F32)
        u = uu_ref[...].astype(F32)
        sg = jax.nn.sigmoid(g)
        silu = g * sg
        act_ref[...] = (silu * u).astype(BF16)
        dg = (dact * u * (sg * (1.0 + g * (1.0 - sg)))).astype(BF16)
        du = (dact * silu).astype(BF16)
        dg_ref[...] = dg
        du_ref[...] = du
        arrive(1)
        arrive(2)
        dx, dgain = _rms_bwd_math(_dot(dg, wg_ref[...]) + _dot(du, wu_ref[...]), x_ref[...], g_ref[...])
        dx = dy + dx
        dx_ref[...] = dx
        dxb_ref[...] = dx.astype(BF16)
        dgain_ref[...] += dgain

    s = x.shape[0]
    tile = pl.BlockSpec((tm, DM), lambda i: (i, 0))
    vec = pl.BlockSpec((1, DM), lambda i: (0, 0))
    hid = pl.BlockSpec((tm, DFF), lambda i: (i, 0))
    hshape = _sds((s, DFF), BF16)
    w = pl.BlockSpec(memory_space=pl.ANY)
    ins, specs = _with_dep([dy, x, gain, gg, uu, wg, wu, wd], [tile, tile, vec, hid, hid, w, w, w], dep)
    return pl.pallas_call(
        body, name=name, out_shape=(_sds((s, DM), F32), _sds((s, DM), BF16), _sds((1, DM), F32), hshape, hshape, hshape),
        grid=(s // tm,), in_specs=specs, out_specs=(tile, tile, vec, hid, hid, hid), scratch_shapes=_resident_weights(3),
        compiler_params=_params(("arbitrary",)))(*ins)


def ffn_bwd_weights(h, dy, act, dg, du, *, name, tf=256):
    def body(h_ref, dy_ref, act_ref, dg_ref, du_ref, gwg_ref, gwu_ref, gwd_ref):
        h = h_ref[...]
        gwg_ref[...] = _dot(dg_ref[...], h, TN).astype(BF16)
        gwu_ref[...] = _dot(du_ref[...], h, TN).astype(BF16)
        gwd_ref[...] = (0.5 * _dot(act_ref[...], dy_ref[...], TN)).astype(BF16)

    s = h.shape[0]
    full = pl.BlockSpec((s, DM), lambda f: (0, 0))
    hid = pl.BlockSpec((s, tf), lambda f: (0, f))
    wt = pl.BlockSpec((tf, DM), lambda f: (f, 0))
    wshape = _sds((DFF, DM), BF16)
    return pl.pallas_call(
        body, name=name, out_shape=(wshape, wshape, wshape), grid=(DFF // tf,), in_specs=[full, full, hid, hid, hid],
        out_specs=(wt, wt, wt), compiler_params=_params(("parallel",)))(h, dy, act, dg, du)


def _group_mean(v, bd):
    hi = v.astype(BF16)
    lo = (v - hi.astype(F32)).astype(BF16)
    return _dot(hi, bd) + _dot(lo, bd)


def _block_diag(width):
    idx = np.arange(width) // HD
    return jnp.asarray((idx[:, None] == idx[None, :]).astype(np.float32) / HD, dtype=BF16)


def qknorm_fwd(z, gq_na, gk_na, gq_sw, gk_sw, *, name, tm=256):
    def body(zq_ref, zk_ref, zv_ref, zs_ref, zkv_ref, gqa_ref, gka_ref, gqs_ref, gks_ref, bd_ref, bd2_ref,
             qa_ref, ka_ref, va_ref, qs_ref, kv_ref):
        bd = bd_ref[...]

        def norm(x, g, bdm):
            x = x.astype(F32)
            return x * lax.rsqrt(_group_mean(x * x, bdm) + EPS) * g

        qa_ref[...] = (norm(zq_ref[...], gqa_ref[...], bd) * QK_SCALE).astype(BF16)
        ka_ref[...] = norm(zk_ref[...], gka_ref[...], bd).astype(BF16)
        va_ref[...] = zv_ref[...].astype(BF16)
        qs_ref[...] = (norm(zs_ref[...], gqs_ref[...], bd) * QK_SCALE).astype(BF16)
        kv = zkv_ref[...]
        kv_ref[:, 0:128] = norm(kv[:, 0:128], gks_ref[...], bd2_ref[...]).astype(BF16)
        kv_ref[:, 128:256] = kv[:, 128:256].astype(BF16)

    s = z.shape[0]
    col = lambda j: pl.BlockSpec((tm, 512), lambda i, j=j: (i, j))
    vec = lambda w: pl.BlockSpec((1, w), lambda i: (0, 0))
    o512 = pl.BlockSpec((tm, 512), lambda i: (i, 0))
    g512 = lambda g: jnp.tile(g.reshape(1, HD), (1, 8))
    return pl.pallas_call(
        body, name=name,
        out_shape=(_sds((s, 512), BF16),) * 4 + (_sds((s, 256), BF16),), grid=(s // tm,),
        in_specs=[col(0), col(1), col(2), col(3), pl.BlockSpec((tm, 256), lambda i: (i, 8)), vec(512), vec(512), vec(512),
                  vec(128), pl.BlockSpec((512, 512), lambda i: (0, 0)), pl.BlockSpec((128, 128), lambda i: (0, 0))],
        out_specs=(o512, o512, o512, o512, pl.BlockSpec((tm, 256), lambda i: (i, 0))),
        compiler_params=_params(("parallel",)))(
            z, z, z, z, z, g512(gq_na), g512(gk_na), g512(gq_sw), jnp.tile(gk_sw.reshape(1, HD), (1, 2)),
            _block_diag(512), _block_diag(128))


def qknorm_bwd(z, dqa, dka, dva, dqs, dkv, gq_na, gk_na, gq_sw, gk_sw, *, name, tm=256):
    def body(zq_ref, zk_ref, zs_ref, zkv_ref, dqa_ref, dka_ref, dva_ref, dqs_ref, dkv_ref, gqa_ref, gka_ref, gqs_ref,
             gks_ref, bd_ref, bd2_ref, dz_ref, dgqa_ref, dgka_ref, dgqs_ref, dgks_ref):
        @pl.when(pl.program_id(0) == 0)
        def _():
            dgqa_ref[...] = jnp.zeros_like(dgqa_ref)
            dgka_ref[...] = jnp.zeros_like(dgka_ref)
            dgqs_ref[...] = jnp.zeros_like(dgqs_ref)
            dgks_ref[...] = jnp.zeros_like(dgks_ref)

        bd = bd_ref[...]

        def bwd(x, dy, g, bdm, dg_ref):
            x = x.astype(F32)
            r = lax.rsqrt(_group_mean(x * x, bdm) + EPS)
            xh = x * r
            dg_ref[...] += jnp.sum(dy * xh, axis=0, keepdims=True)
            dxn = dy * g
            return r * (dxn - xh * _group_mean(dxn * xh, bdm))

        dz_ref[:, 0:512] = bwd(zq_ref[...], dqa_ref[...] * QK_SCALE, gqa_ref[...], bd, dgqa_ref).astype(BF16)
        dz_ref[:, 512:1024] = bwd(zk_ref[...], dka_ref[...], gka_ref[...], bd, dgka_ref).astype(BF16)
        dz_ref[:, 1024:1536] = dva_ref[...].astype(BF16)
        dz_ref[:, 1536:2048] = bwd(zs_ref[...], dqs_ref[...] * QK_SCALE, gqs_ref[...], bd, dgqs_ref).astype(BF16)
        dkv = dkv_ref[...]
        dz_ref[:, 2048:2176] = bwd(zkv_ref[:, 0:128], dkv[:, 0:128], gks_ref[...], bd2_ref[...], dgks_ref).astype(BF16)
        dz_ref[:, 2176:2304] = dkv[:, 128:256].astype(BF16)

    s = z.shape[0]
    col = lambda j: pl.BlockSpec((tm, 512), lambda i, j=j: (i, j))
    t512 = pl.BlockSpec((tm, 512), lambda i: (i, 0))
    t256 = pl.BlockSpec((tm, 256), lambda i: (i, 0))
    vec = lambda w: pl.BlockSpec((1, w), lambda i: (0, 0))
    g512 = lambda g: jnp.tile(g.reshape(1, HD), (1, 8))
    return pl.pallas_call(
        body, name=name,
        out_shape=(_sds((s, ATT_W), BF16), _sds((1, 512), F32), _sds((1, 512), F32), _sds((1, 512), F32), _sds((1, 128), F32)),
        grid=(s // tm,),
        in_specs=[col(0), col(1), col(3), pl.BlockSpec((tm, 256), lambda i: (i, 8)), t512, t512, t512, t512, t256,
                  vec(512), vec(512), vec(512), vec(128), pl.BlockSpec((512, 512), lambda i: (0, 0)),
                  pl.BlockSpec((128, 128), lambda i: (0, 0))],
        out_specs=(pl.BlockSpec((tm, ATT_W), lambda i: (i, 0)), vec(512), vec(512), vec(512), vec(128)),
        compiler_params=_params(("arbitrary",)))(
            z, z, z, z, dqa, dka, dva, dqs, dkv, g512(gq_na), g512(gk_na), g512(gq_sw),
            jnp.tile(gk_sw.reshape(1, HD), (1, 2)), _block_diag(512), _block_diag(128))


def _na_row_start(r):
    return jnp.clip(r - NA_WR // 2, 0, ROWS - NA_WR)


def na_bias_table(rpb, *, name):
    t = jnp.pad(rpb, ((0, 0), (0, 2), (0, HD - (2 * NA_WC - 1))))
    pairs = jnp.concatenate([t[:, :16], t[:, 1:17]], axis=-1).reshape(NA_HEADS, 16, 1, 128)

    def body(t_ref, o_ref):
        p = pl.program_id(0)
        q = lax.broadcasted_iota(jnp.int32, (GRID_W, 128), 0)
        kc = lax.broadcasted_iota(jnp.int32, (GRID_W, 128), 1) & (GRID_W - 1)
        cs = jnp.clip(q - NA_WC // 2, 0, GRID_W - NA_WC)
        ok = (kc >= cs) & (kc < cs + NA_WC)
        for h in range(NA_HEADS):
            for pr in range(NA_WR // 2):
                x = jnp.broadcast_to(t_ref[h, 2 * pr - p + NA_WR - 1], (GRID_W, 128))
                b = pltpu.roll(x, 128 - (NA_WC - 1), 1, stride=1, stride_axis=0)
                o_ref[h, :, 128 * pr:128 * pr + 128] = jnp.where(ok, b, NEG)

    return pl.pallas_call(
        body, name=name, out_shape=_sds((NA_WR, NA_HEADS, GRID_W, NA_KEYS), F32), grid=(NA_WR,),
        in_specs=[pl.BlockSpec((NA_HEADS, 16, 1, 128), lambda p: (0, 0, 0, 0))],
        out_specs=pl.BlockSpec((None, NA_HEADS, GRID_W, NA_KEYS), lambda p: (p, 0, 0, 0)),
        compiler_params=_params(("parallel",)))(pairs)


def _lane_halves():
    lane = lax.broadcasted_iota(jnp.int32, (1, 128), 1)
    return lane < HD


def na_fwd(q, k, v, bias, *, name):
    def body(q_ref, k_ref, v_ref, b_ref, o_ref, lse_ref):
        r = pl.program_id(0)
        off = pl.multiple_of(_na_row_start(r) * GRID_W, GRID_W)
        first = _lane_halves()
        sels = [first, jnp.logical_not(first)]
        lanes = [slice(128 * j, 128 * j + 128) for j in range(NA_HEADS // 2)]
        q2s = [q_ref[:, l] for l in lanes]
        k2s = [k_ref[pl.ds(off, NA_KEYS), l] for l in lanes]
        v2s = [v_ref[pl.ds(off, NA_KEYS), l] for l in lanes]
        scores = []
        for h in range(NA_HEADS):
            j, half = divmod(h, 2)
            scores.append(_dot(jnp.where(sels[half], q2s[j], jnp.zeros_like(q2s[j])), k2s[j], NT))
        probs, lses = [], []
        for h in range(NA_HEADS):
            b = b_ref[h]
            s = jnp.where(b > 0.5 * NEG, scores[h] + b, NEG)
            m = jnp.max(s, axis=-1, keepdims=True)
            e = jnp.exp(s - m)
            l = jnp.sum(e, axis=-1, keepdims=True)
            probs.append((e / l).astype(BF16))
            lses.append(m + jnp.log(l))
        for j in range(NA_HEADS // 2):
            zero = jnp.zeros_like(v2s[j])
            o2 = (_dot(probs[2 * j], jnp.where(sels[0], v2s[j], zero))
                  + _dot(probs[2 * j + 1], jnp.where(sels[1], v2s[j], zero)))
            o_ref[:, lanes[j]] = o2.astype(BF16)
        lse_ref[...] = jnp.concatenate(lses, axis=1)

    s_tok = q.shape[0]
    full = pl.BlockSpec((s_tok, 512), lambda r: (0, 0))
    return pl.pallas_call(
        body, name=name, out_shape=(_sds((s_tok, 512), BF16), _sds((s_tok, NA_HEADS), F32)), grid=(ROWS,),
        in_specs=[pl.BlockSpec((GRID_W, 512), lambda r: (r, 0)), full, full,
                  pl.BlockSpec((None, NA_HEADS, GRID_W, NA_KEYS), lambda r: (r - _na_row_start(r), 0, 0, 0))],
        out_specs=(pl.BlockSpec((GRID_W, 512), lambda r: (r, 0)), pl.BlockSpec((GRID_W, NA_HEADS), lambda r: (r, 0))),
        compiler_params=_params(("parallel",)))(q, k, v, bias)


def na_bwd(q, k, v, o, do, lse, bias, *, name):
    def body(q_ref, k_ref, v_ref, o_ref, do_ref, lse_ref, b_ref, dq_ref, dk_ref, dv_ref, db_ref):
        r = pl.program_id(0)

        @pl.when(r == 0)
        def _():
            dk_ref[...] = jnp.zeros_like(dk_ref)
            dv_ref[...] = jnp.zeros_like(dv_ref)

        @pl.when((r <= NA_WR // 2) | (r > ROWS - NA_WR // 2))
        def _():
            db_ref[...] = jnp.zeros_like(db_ref)

        off = pl.multiple_of(_na_row_start(r) * GRID_W, GRID_W)
        first = _lane_halves()
        sels = [first, jnp.logical_not(first)]
        lanes = [slice(128 * j, 128 * j + 128) for j in range(NA_HEADS // 2)]
        q2s = [q_ref[:, l] for l in lanes]
        k2s = [k_ref[pl.ds(off, NA_KEYS), l] for l in lanes]
        v2s = [v_ref[pl.ds(off, NA_KEYS), l] for l in lanes]
        do2s = [do_ref[:, l] for l in lanes]
        prods = [do2s[j].astype(F32) * o_ref[:, lanes[j]].astype(F32) for j in range(NA_HEADS // 2)]
        lse = lse_ref[...]
        qhs, dohs, scores, dps = [], [], [], []
        for h in range(NA_HEADS):
            j, half = divmod(h, 2)
            qhs.append(jnp.where(sels[half], q2s[j], jnp.zeros_like(q2s[j])))
            dohs.append(jnp.where(sels[half], do2s[j], jnp.zeros_like(do2s[j])))
            scores.append(_dot(qhs[h], k2s[j], NT))
            dps.append(_dot(dohs[h], v2s[j], NT))
        pbs, dsbs = [], []
        for h in range(NA_HEADS):
            j, half = divmod(h, 2)
            b = b_ref[h]
            s = jnp.where(b > 0.5 * NEG, scores[h] + b, NEG)
            p = jnp.exp(s - lse[:, h:h + 1])
            delta = jnp.sum(jnp.where(sels[half], prods[j], 0.0), axis=-1, keepdims=True)
            ds = p * (dps[h] - delta)
            db_ref[h] += ds
            pbs.append(p.astype(BF16))
            dsbs.append(ds.astype(BF16))
        for j in range(NA_HEADS // 2):
            a, b = 2 * j, 2 * j + 1
            zero = jnp.zeros_like(k2s[j])
            dq_ref[:, lanes[j]] = (_dot(dsbs[a], jnp.where(sels[0], k2s[j], zero))
                                   + _dot(dsbs[b], jnp.where(sels[1], k2s[j], zero)))
            dk_ref[pl.ds(off, NA_KEYS), lanes[j]] += _dot(dsbs[a], qhs[a], TN) + _dot(dsbs[b], qhs[b], TN)
            dv_ref[pl.ds(off, NA_KEYS), lanes[j]] += _dot(pbs[a], dohs[a], TN) + _dot(pbs[b], dohs[b], TN)

    s_tok = q.shape[0]
    full = pl.BlockSpec((s_tok, 512), lambda r: (0, 0))
    row = pl.BlockSpec((GRID_W, 512), lambda r: (r, 0))
    bias_spec = pl.BlockSpec((None, NA_HEADS, GRID_W, NA_KEYS), lambda r: (r - _na_row_start(r), 0, 0, 0))
    return pl.pallas_call(
        body, name=name,
        out_shape=(_sds((s_tok, 512), F32), _sds((s_tok, 512), F32), _sds((s_tok, 512), F32),
                   _sds((NA_WR, NA_HEADS, GRID_W, NA_KEYS), F32)),
        grid=(ROWS,),
        in_specs=[row, full, full, row, row, pl.BlockSpec((GRID_W, NA_HEADS), lambda r: (r, 0)), bias_spec],
        out_specs=(row, full, full, bias_spec), compiler_params=_params(("arbitrary",)))(q, k, v, o, do, lse, bias)


def t5_bucket_map():
    rel = np.arange(SW_KEYS)[None, :] - SW_BLK - np.arange(SW_BLK)[:, None]
    nb = 16
    max_exact = nb // 2
    n = np.abs(rel)
    large = max_exact + (np.log(np.maximum(n, 1) / max_exact) / np.log(128 / max_exact) * (nb - max_exact)).astype(np.int32)
    large = np.minimum(large, nb - 1)
    return ((rel > 0) * nb + np.where(n < max_exact, n, large)).astype(np.int32)


def t5_bias(table, *, name):
    rel = np.arange(-SW_BLK, SW_BLK + 1)
    nb, max_exact = 16, 8
    n = np.abs(rel)
    large = max_exact + (np.log(np.maximum(n, 1) / max_exact) / np.log(128 / max_exact) * (nb - max_exact)).astype(np.int32)
    bucket = ((rel > 0) * nb + np.where(n < max_exact, n, np.minimum(large, nb - 1))).astype(np.int32)
    u = jnp.pad(table[jnp.asarray(bucket)].T, ((0, 0), (0, SW_KEYS - bucket.shape[0]))).reshape(8, 1, SW_KEYS)

    def body(u_ref, o_ref):
        for h in range(8):
            x = jnp.broadcast_to(u_ref[h], (SW_BLK, SW_KEYS))
            o_ref[h] = pltpu.roll(x, 0, 1, stride=1, stride_axis=0)

    return pl.pallas_call(body, name=name, out_shape=_sds((8, SW_BLK, SW_KEYS), F32), compiler_params=_params())(u)


def _sw_valid(n):
    a = lax.broadcasted_iota(jnp.int32, (SW_BLK, SW_KEYS), 0)
    j = lax.broadcasted_iota(jnp.int32, (SW_BLK, SW_KEYS), 1)
    kpos = (n - 1) * SW_BLK + j
    return (jnp.abs(j - SW_BLK - a) <= SW_BLK) & (kpos >= 0) & (kpos < SEQ)


def _dup_group(x2, g, first):
    rolled = pltpu.roll(x2, HD, 1)
    return jnp.where(first, x2, rolled) if g == 0 else jnp.where(first, rolled, x2)


def sw_fwd(q, kv, t5, sink, *, name):
    def body(q_ref, kv_ref, t5_ref, sink_ref, o_ref, lse_ref):
        n = pl.program_id(0)
        off = pl.multiple_of(n * SW_BLK, SW_BLK)
        first = _lane_halves()
        sels = [first, jnp.logical_not(first)]
        valid = _sw_valid(n)
        k2 = kv_ref[pl.ds(off, SW_KEYS), 0:128]
        v2 = kv_ref[pl.ds(off, SW_KEYS), 128:256]
        kk = [_dup_group(k2, g, first) for g in range(2)]
        vv = [_dup_group(v2, g, first) for g in range(2)]
        q2s = [q_ref[:, 128 * j:128 * j + 128] for j in range(4)]
        scores = []
        for h in range(8):
            j, half = divmod(h, 2)
            scores.append(_dot(jnp.where(sels[half], q2s[j], jnp.zeros_like(q2s[j])), kk[j // 2], NT))
        probs, lses = [], []
        for h in range(8):
            s = jnp.where(valid, scores[h] + t5_ref[h], NEG)
            snk = sink_ref[h]
            m = jnp.maximum(jnp.max(s, axis=-1, keepdims=True), snk)
            e = jnp.exp(s - m)
            den = jnp.sum(e, axis=-1, keepdims=True) + jnp.exp(snk - m)
            probs.append((e / den).astype(BF16))
            lses.append(m + jnp.log(den))
        outs = []
        for j in range(4):
            vg = vv[j // 2]
            zero = jnp.zeros_like(vg)
            outs.append(_dot(probs[2 * j], jnp.where(sels[0], vg, zero)) + _dot(probs[2 * j + 1], jnp.where(sels[1], vg, zero)))
        o_ref[...] = jnp.concatenate(outs, axis=1).astype(BF16)
        lse_ref[...] = jnp.concatenate(lses, axis=1)

    s_tok = q.shape[0]
    blk = pl.BlockSpec((SW_BLK, 512), lambda n: (n, 0))
    return pl.pallas_call(
        body, name=name, out_shape=(_sds((s_tok, 512), BF16), _sds((s_tok, 8), F32)), grid=(SW_NB,),
        in_specs=[blk, pl.BlockSpec(kv.shape, lambda n: (0, 0)), pl.BlockSpec((8, SW_BLK, SW_KEYS), lambda n: (0, 0, 0)),
                  pl.BlockSpec(memory_space=pltpu.SMEM)],
        out_specs=(blk, pl.BlockSpec((SW_BLK, 8), lambda n: (n, 0))), compiler_params=_params(("parallel",)))(q, kv, t5, sink)


def sw_bwd(q, kv, o, do, lse, t5, sink, *, name):
    def body(q_ref, kv_ref, o_ref, do_ref, lse_ref, t5_ref, sink_ref, dq_ref, dkv_ref, dt5_ref, dsink_ref):
        n = pl.program_id(0)

        @pl.when(n == 0)
        def _():
            dkv_ref[...] = jnp.zeros_like(dkv_ref)
            dt5_ref[...] = jnp.zeros_like(dt5_ref)
            dsink_ref[...] = jnp.zeros_like(dsink_ref)

        off = pl.multiple_of(n * SW_BLK, SW_BLK)
        first = _lane_halves()
        sels = [first, jnp.logical_not(first)]
        valid = _sw_valid(n)
        k2 = kv_ref[pl.ds(off, SW_KEYS), 0:128]
        v2 = kv_ref[pl.ds(off, SW_KEYS), 128:256]
        kk = [_dup_group(k2, g, first) for g in range(2)]
        vv = [_dup_group(v2, g, first) for g in range(2)]
        lanes = [slice(128 * j, 128 * j + 128) for j in range(4)]
        q2s = [q_ref[:, l] for l in lanes]
        do2s = [do_ref[:, l] for l in lanes]
        prods = [do2s[j].astype(F32) * o_ref[:, lanes[j]].astype(F32) for j in range(4)]
        lse = lse_ref[...]
        qhs, dohs, scores, dps = [], [], [], []
        for h in range(8):
            j, half = divmod(h, 2)
            qhs.append(jnp.where(sels[half], q2s[j], jnp.zeros_like(q2s[j])))
            dohs.append(jnp.where(sels[half], do2s[j], jnp.zeros_like(do2s[j])))
            scores.append(_dot(qhs[h], kk[j // 2], NT))
            dps.append(_dot(dohs[h], vv[j // 2], NT))
        pbs, dsbs, dss, dsinks = [], [], [], []
        for h in range(8):
            j, half = divmod(h, 2)
            s = jnp.where(valid, scores[h] + t5_ref[h], NEG)
            lse_h = lse[:, h:h + 1]
            p = jnp.exp(s - lse_h)
            delta = jnp.sum(jnp.where(sels[half], prods[j], 0.0), axis=-1, keepdims=True)
            ds = p * (dps[h] - delta)
            dss.append(ds)
            dsinks.append(-jnp.sum(jnp.exp(sink_ref[h] - lse_h) * delta, axis=0, keepdims=True))
            pbs.append(p.astype(BF16))
            dsbs.append(ds.astype(BF16))
        dt5_ref[...] += jnp.stack(dss)
        dsink_ref[...] += jnp.concatenate(dsinks, axis=1)
        dqs = []
        for j in range(4):
            a, b = 2 * j, 2 * j + 1
            zero = jnp.zeros_like(kk[j // 2])
            dqs.append(_dot(dsbs[a], jnp.where(sels[0], kk[j // 2], zero)) + _dot(dsbs[b], jnp.where(sels[1], kk[j // 2], zero)))
        dq_ref[...] = jnp.concatenate(dqs, axis=1)
        dk_groups, dv_groups = [], []
        for g in range(2):
            dkk = sum(_dot(dsbs[h], qhs[h], TN) for h in range(4 * g, 4 * g + 4))
            dvv = sum(_dot(pbs[h], dohs[h], TN) for h in range(4 * g, 4 * g + 4))
            dk_groups.append(dkk + pltpu.roll(dkk, HD, 1))
            dv_groups.append(dvv + pltpu.roll(dvv, HD, 1))
        dkv_ref[pl.ds(off, SW_KEYS), :] += jnp.concatenate(
            [jnp.where(first, dk_groups[0], dk_groups[1]), jnp.where(first, dv_groups[0], dv_groups[1])], axis=1)

    s_tok = q.shape[0]
    blk = pl.BlockSpec((SW_BLK, 512), lambda n: (n, 0))
    kv_spec = pl.BlockSpec(kv.shape, lambda n: (0, 0))
    t5_spec = pl.BlockSpec((8, SW_BLK, SW_KEYS), lambda n: (0, 0, 0))
    vec = pl.BlockSpec((1, 8), lambda n: (0, 0))
    return pl.pallas_call(
        body, name=name,
        out_shape=(_sds((s_tok, 512), F32), _sds(kv.shape, F32), _sds((8, SW_BLK, SW_KEYS), F32), _sds((1, 8), F32)),
        grid=(SW_NB,), in_specs=[blk, kv_spec, blk, blk, pl.BlockSpec((SW_BLK, 8), lambda n: (n, 0)), t5_spec,
                                 pl.BlockSpec(memory_space=pltpu.SMEM)],
        out_specs=(blk, kv_spec, t5_spec, vec), compiler_params=_params(("arbitrary",)))(q, kv, o, do, lse, t5, sink)


def gate_fwd(zg, bias, pa, ps, *, name, tm=512):
    def body(z0_ref, z1_ref, b0_ref, b1_ref, pa_ref, ps_ref, m_ref):
        g0 = jax.nn.sigmoid(z0_ref[...] + b0_ref[...])
        g1 = jax.nn.sigmoid(z1_ref[...] + b1_ref[...])
        m_ref[...] = (g0 * pa_ref[...] + g1 * ps_ref[...]).astype(BF16)

    s = zg.shape[0]
    half = lambda j: pl.BlockSpec((tm, DM), lambda i, j=j: (i, j))
    bvec = lambda j: pl.BlockSpec((1, DM), lambda i, j=j: (0, j))
    return pl.pallas_call(
        body, name=name, out_shape=_sds((s, DM), BF16), grid=(s // tm,),
        in_specs=[half(0), half(1), bvec(0), bvec(1), half(0), half(0)], out_specs=half(0),
        compiler_params=_params(("parallel",)))(zg, zg, bias, bias, pa, ps)


def gate_bwd(dm, zg, bias, pa, ps, *, name, tm=512):
    def body(dm_ref, z0_ref, z1_ref, b0_ref, b1_ref, pa_ref, ps_ref, dpa_ref, dps_ref, dz_ref, db_ref):
        @pl.when(pl.program_id(0) == 0)
        def _():
            db_ref[...] = jnp.zeros_like(db_ref)

        dm = dm_ref[...]
        g0 = jax.nn.sigmoid(z0_ref[...] + b0_ref[...])
        g1 = jax.nn.sigmoid(z1_ref[...] + b1_ref[...])
        dpa_ref[...] = (dm * g0).astype(BF16)
        dps_ref[...] = (dm * g1).astype(BF16)
        dz0 = dm * pa_ref[...] * g0 * (1.0 - g0)
        dz1 = dm * ps_ref[...] * g1 * (1.0 - g1)
        dz_ref[:, 0:DM] = dz0.astype(BF16)
        dz_ref[:, DM:2 * DM] = dz1.astype(BF16)
        db_ref[:, 0:DM] += jnp.sum(dz0, axis=0, keepdims=True)
        db_ref[:, DM:2 * DM] += jnp.sum(dz1, axis=0, keepdims=True)

    s = zg.shape[0]
    half = lambda j: pl.BlockSpec((tm, DM), lambda i, j=j: (i, j))
    bvec = lambda j: pl.BlockSpec((1, DM), lambda i, j=j: (0, j))
    return pl.pallas_call(
        body, name=name,
        out_shape=(_sds((s, DM), BF16), _sds((s, DM), BF16), _sds((s, GATE_W), BF16), _sds((1, GATE_W), F32)),
        grid=(s // tm,), in_specs=[half(0), half(0), half(1), bvec(0), bvec(1), half(0), half(0)],
        out_specs=(half(0), half(0), pl.BlockSpec((tm, GATE_W), lambda i: (i, 0)), pl.BlockSpec((1, GATE_W), lambda i: (0, 0))),
        compiler_params=_params(("arbitrary",)))(dm, zg, zg, bias, bias, pa, ps)


def loss_head(y, target, *, name, tm=512):
    def body(y_ref, t_ref, dy_ref, dyb_ref, l_ref):
        @pl.when(pl.program_id(0) == 0)
        def _():
            l_ref[...] = jnp.zeros_like(l_ref)

        err = y_ref[...] - t_ref[...]
        dy = err * (1.0 / DM)
        dy_ref[...] = dy
        dyb_ref[...] = dy.astype(BF16)
        l_ref[...] += 0.5 * jnp.sum(jnp.mean(err * err, axis=-1, keepdims=True), axis=0, keepdims=True)

    s = y.shape[0]
    tile = pl.BlockSpec((tm, DM), lambda i: (i, 0))
    return pl.pallas_call(
        body, name=name, out_shape=(_sds((s, DM), F32), _sds((s, DM), BF16), _sds((1, 128), F32)), grid=(s // tm,),
        in_specs=[tile, tile], out_specs=(tile, tile, pl.BlockSpec((1, 128), lambda i: (0, 0))),
        compiler_params=_params(("arbitrary",)))(y, target)


def adamw_small(ws, gs, ms, vs, *, name):
    cnt = len(ws)

    def body(*refs):
        ins, outs = refs[:4 * cnt], refs[4 * cnt:]
        for i in range(cnt):
            w_ref, g_ref, m_ref, v_ref = ins[4 * i:4 * i + 4]
            d_ref, nm_ref, nv_ref = outs[3 * i:3 * i + 3]
            g = g_ref[...]
            nm = ADAM_B1 * m_ref[...] + (1.0 - ADAM_B1) * g
            nv = ADAM_B2 * v_ref[...] + (1.0 - ADAM_B2) * jnp.square(g)
            m_hat = nm / (1.0 - ADAM_B1 ** ADAM_STEP)
            v_hat = nv / (1.0 - ADAM_B2 ** ADAM_STEP)
            d_ref[...] = -ADAM_LR * (m_hat / (jnp.sqrt(v_hat) + ADAM_EPS) + ADAM_WD * w_ref[...])
            nm_ref[...] = nm
            nv_ref[...] = nv

    flat = [a for i in range(cnt) for a in (ws[i], gs[i], ms[i], vs[i])]
    res = pl.pallas_call(
        body, name=name, out_shape=tuple(_sds(ws[i].shape, F32) for i in range(cnt) for _ in range(3)),
        compiler_params=_params())(*flat)
    return [tuple(res[3 * i:3 * i + 3]) for i in range(cnt)]


def adamw_layer(ws, ms, vs, mines, theirs, cidx, layer, filled=None, *, name):
    cnt = len(ws)
    _, k, n = ws[0].shape
    nt = 2
    tk = k // 2 // nt

    def body(c_ref, *refs):
        own = pl.program_id(0) == c_ref[0]
        outs = refs[-4 * cnt:]
        for i in range(cnt):
            w_ref, m_ref, v_ref, a_ref, b_ref = refs[5 * i:5 * i + 5]
            g_ref, d_ref, nm_ref, nv_ref = outs[4 * i:4 * i + 4]
            g = jnp.where(own, a_ref[...], b_ref[...])
            g_ref[...] = g
            nm = ADAM_B1 * m_ref[...] + (1.0 - ADAM_B1) * g
            nv = ADAM_B2 * v_ref[...] + (1.0 - ADAM_B2) * jnp.square(g)
            m_hat = nm / (1.0 - ADAM_B1 ** ADAM_STEP)
            v_hat = nv / (1.0 - ADAM_B2 ** ADAM_STEP)
            d_ref[...] = -ADAM_LR * (m_hat / (jnp.sqrt(v_hat) + ADAM_EPS) + ADAM_WD * w_ref[...])
            nm_ref[...] = nm
            nv_ref[...] = nv

    full = pl.BlockSpec((None, tk, n), lambda hf, t, c: (layer, hf * nt + t, 0))
    half_mine = pl.BlockSpec((tk, n), lambda hf, t, c: (jnp.where(hf == c[0], t, 0), 0))
    half_theirs = pl.BlockSpec((tk, n), lambda hf, t, c: (jnp.where(hf != c[0], t, 0), 0))
    out = _sds(ws[0].shape, F32)
    ins, specs, aliases = [cidx], [], {}
    for i in range(cnt):
        ins += [ws[i], ms[i], vs[i], mines[i], theirs[i]]
        specs += [full, full, full, half_mine, half_theirs]
    if filled is not None:
        aliases = {len(ins) + j: j for j in range(4 * cnt)}
        ins += [a for f in filled for a in f]
        specs += [pl.BlockSpec(memory_space=pl.ANY)] * (4 * cnt)
    res = pl.pallas_call(
        body, name=name, out_shape=(out,) * (4 * cnt),
        grid_spec=pltpu.PrefetchScalarGridSpec(
            num_scalar_prefetch=1, grid=(2, nt), in_specs=specs, out_specs=(full,) * (4 * cnt)),
        input_output_aliases=aliases,
        compiler_params=_params(("arbitrary", "arbitrary")))(*ins)
    return [tuple(res[4 * i:4 * i + 4]) for i in range(cnt)]


def t5_table_grad(dt5_a, dt5_b, *, name):
    def body(a_ref, b_ref, map_ref, o_ref):
        d = a_ref[...] + b_ref[...]
        bucket = map_ref[...]
        for b in range(32):
            hit = (bucket == b)[None]
            o_ref[b] = jnp.sum(jnp.sum(jnp.where(hit, d, 0.0), axis=2), axis=1, keepdims=True)

    return pl.pallas_call(
        body, name=name, out_shape=_sds((32, 8, 1), F32), compiler_params=_params())(
            dt5_a, dt5_b, jnp.asarray(t5_bucket_map()))


def rpb_grad(dbias, *, name):
    def body(d_ref, rev_ref, o_ref):
        rev = rev_ref[...]
        for h in range(NA_HEADS):
            for pr in range(NA_WR // 2):
                d = d_ref[h, :, 128 * pr:128 * pr + 128]
                hi = d.astype(BF16)
                lo = (d - hi.astype(F32)).astype(BF16)
                flipped = _dot(rev, hi) + _dot(rev, lo)
                o_ref[h, pr] = jnp.sum(pltpu.roll(flipped, 0, 1, stride=1, stride_axis=0), axis=0, keepdims=True)

    anti = jnp.asarray(np.eye(GRID_W, dtype=np.float32)[::-1], dtype=BF16)
    e = pl.pallas_call(
        body, name=name, out_shape=_sds((NA_WR, NA_HEADS, NA_WR // 2, 1, 128), F32), grid=(NA_WR,),
        in_specs=[pl.BlockSpec((None, NA_HEADS, GRID_W, NA_KEYS), lambda p: (p, 0, 0, 0)),
                  pl.BlockSpec((GRID_W, GRID_W), lambda p: (0, 0))],
        out_specs=pl.BlockSpec((None, NA_HEADS, NA_WR // 2, 1, 128), lambda p: (p, 0, 0, 0, 0)),
        compiler_params=_params(("parallel",)))(dbias, anti)
    nci, nri = 2 * NA_WC - 1, 2 * NA_WR - 1
    e = e.reshape(NA_WR, NA_HEADS, NA_WR // 2, 128).transpose(0, 2, 1, 3).reshape(NA_WR * NA_WR // 2, NA_HEADS, 128)
    parts = jnp.concatenate([e[..., 48:48 + nci], jnp.concatenate([e[..., 112:128], e[..., 0:nci - 16]], axis=-1)], axis=0)
    p, pr = np.arange(NA_WR)[:, None], np.arange(NA_WR // 2)[None, :]
    ri = np.concatenate([(2 * pr - p + NA_WR - 1).reshape(-1), (2 * pr - p + NA_WR).reshape(-1)])
    pick = jnp.asarray((ri[None, :] == np.arange(16)[:, None]).astype(np.float32))
    out = mm(pick, parts.reshape(2 * NA_WR * NA_WR // 2, NA_HEADS * nci), name=name + "_rows", exact=True)
    return out.reshape(16, NA_HEADS, nci)[:nri].transpose(1, 0, 2)


BIG = ("ffn1_w_gate", "ffn1_w_up", "ffn1_w_down", "w_in", "w_branch_na", "w_branch_sw", "w_out",
       "ffn2_w_gate", "ffn2_w_up", "ffn2_w_down")
SMALL = ("ffn1_norm", "mix_norm", "b_gate", "na_q_norm", "na_k_norm", "na_rpb", "sw_q_norm", "sw_k_norm", "sw_sink",
         "ffn2_norm")


def _cols_to_full(w4):
    return w4.transpose(1, 0, 2).reshape(w4.shape[1], NSH * w4.shape[2])


def _full_to_cols(w):
    return w.reshape(w.shape[0], NSH, w.shape[1] // NSH).transpose(1, 0, 2)


def _mixer_weights(g):
    w_in_t = g["w_in"].reshape(IN_W, DM)
    return dict(w_att_t=w_in_t[:ATT_W], w_gz_t=w_in_t[ATT_W:], wa=_cols_to_full(g["w_branch_na"]),
                ws=_cols_to_full(g["w_branch_sw"]), wo=g["w_out"].reshape(DM, DM))


GROUPS = {"ffn1": ("ffn1_w_gate", "ffn1_w_up", "ffn1_w_down"), "mix": ("w_in", "w_branch_na", "w_branch_sw", "w_out"),
          "ffn2": ("ffn2_w_gate", "ffn2_w_up", "ffn2_w_down")}


def layer_fwd(x, p, weights, t5b):
    row = lambda v: v.reshape(1, -1)
    stacked = lambda g: {n: a.reshape(DFF, DM) for n, a in g.items()}
    g1 = stacked(weights("ffn1", x))
    y1, h1, gg1, uu1 = ffn_fwd(x, row(p["ffn1_norm"]), g1["ffn1_w_gate"], g1["ffn1_w_up"], g1["ffn1_w_down"], name="ffn_fwd")
    w = _mixer_weights(weights("mix", y1))
    hm = rms_fwd(y1, row(p["mix_norm"]), name="mix_norm_fwd")
    z = mm(hm, w["w_att_t"], tb=True, out_dtype=BF16, name="proj_att", tm=SEQ, tn=768)
    zg = mm(hm, w["w_gz_t"], tb=True, out_dtype=BF16, name="proj_gate", tm=SEQ, tn=512)
    qa, ka, va, qs, kv = qknorm_fwd(z, p["na_q_norm"], p["na_k_norm"], p["sw_q_norm"], p["sw_k_norm"], name="qknorm_fwd")
    bias = p["na_bias"]
    o_na, lse_na = na_fwd(qa, ka, va, bias, name="na_fwd")
    kvp = jnp.pad(kv, ((SW_BLK, SW_BLK), (0, 0)))
    sink = p["sw_sink"]
    o_sw, lse_sw = sw_fwd(qs, kvp, t5b, sink, name="sw_fwd")
    pa = mm(o_na, w["wa"], out_dtype=BF16, name="branch_na", tm=1024)
    ps = mm(o_sw, w["ws"], out_dtype=BF16, name="branch_sw", tm=1024)
    merged = gate_fwd(zg, row(p["b_gate"]), pa, ps, name="gate_fwd")
    y2 = mm(merged, w["wo"], add=y1, name="out_proj", tm=1024)
    g2 = stacked(weights("ffn2", y2))
    y3, h2, gg2, uu2 = ffn_fwd(y2, row(p["ffn2_norm"]), g2["ffn2_w_gate"], g2["ffn2_w_up"], g2["ffn2_w_down"], name="ffn_fwd")
    saved = dict(x=x, y1=y1, h1=h1, gg1=gg1, uu1=uu1, hm=hm, z=z, zg=zg, qa=qa, ka=ka, va=va, qs=qs, kvp=kvp, bias=bias,
                 o_na=o_na, lse_na=lse_na, o_sw=o_sw, lse_sw=lse_sw, pa=pa, ps=ps, merged=merged, y2=y2, h2=h2, gg2=gg2,
                 uu2=uu2, w=w, sink=sink, g1=g1, g2=g2)
    return y3, saved


def layer_bwd(dy3, dy3_bf, sv, p, t5b, emit, dep=None):
    w, g1, g2 = sv["w"], sv["g1"], sv["g2"]
    row = lambda v: v.reshape(1, -1)
    fold = lambda v: v.reshape(-1, HD).sum(axis=0)
    small = {}
    dy2, _, small["ffn2_norm"], act, dg, du = ffn_bwd_tokens(
        dy3, sv["y2"], row(p["ffn2_norm"]), sv["gg2"], sv["uu2"], g2["ffn2_w_gate"], g2["ffn2_w_up"], g2["ffn2_w_down"],
        name="ffn_bwd_tokens", dep=dep)
    shards = lambda gs: [g.reshape(NSH, FSH, DM) for g in gs]
    token = emit("ffn2", shards(ffn_bwd_weights(sv["h2"], dy3_bf, act, dg, du, name="ffn_bwd_weights")))
    dmerged = mm(dy2, w["wo"], tb=True, name="out_proj_dx", tm=1024, dep=token)
    gw_out = mm(sv["merged"], dy2, ta=True, out_dtype=BF16, name="out_proj_dw").reshape(NSH, DM // NSH, DM)
    dpa, dps, dzg, small["b_gate"] = gate_bwd(dmerged, sv["zg"], row(p["b_gate"]), sv["pa"], sv["ps"], name="gate_bwd")
    gw_na = _full_to_cols(mm(sv["o_na"], dpa, ta=True, out_dtype=BF16, name="branch_dw"))
    gw_sw = _full_to_cols(mm(sv["o_sw"], dps, ta=True, out_dtype=BF16, name="branch_dw"))
    do_na = mm(dpa, w["wa"], tb=True, out_dtype=BF16, tm=SEQ, name="branch_dx")
    do_sw = mm(dps, w["ws"], tb=True, out_dtype=BF16, tm=SEQ, name="branch_dx")
    dqa, dka, dva, dbias = na_bwd(sv["qa"], sv["ka"], sv["va"], sv["o_na"], do_na, sv["lse_na"], sv["bias"], name="na_bwd")
    dqs, dkvp, dt5, dsink = sw_bwd(sv["qs"], sv["kvp"], sv["o_sw"], do_sw, sv["lse_sw"], t5b, sv["sink"], name="sw_bwd")
    dkv = dkvp[SW_BLK:SW_BLK + SEQ]
    dz, dgqa, dgka, dgqs, dgks = qknorm_bwd(sv["z"], dqa, dka, dva, dqs, dkv, p["na_q_norm"], p["na_k_norm"],
                                            p["sw_q_norm"], p["sw_k_norm"], name="qknorm_bwd")
    small["na_q_norm"], small["na_k_norm"], small["sw_q_norm"], small["sw_k_norm"] = fold(dgqa), fold(dgka), fold(dgqs), fold(dgks)
    small["na_rpb"] = rpb_grad(dbias, name="rpb_grad")
    small["sw_sink"] = dsink
    gw_att_t = mm(dz, sv["hm"], ta=True, out_dtype=BF16, tm=768, name="proj_att_dw")
    gw_gz_t = mm(dzg, sv["hm"], ta=True, out_dtype=BF16, tm=1024, name="proj_gate_dw")
    gw_in = jnp.concatenate([gw_att_t, gw_gz_t], axis=0).reshape(NSH, IN_W // NSH, DM)
    token = emit("mix", (gw_in, gw_na, gw_sw, gw_out))
    dh = mm(dz, w["w_att_t"], tm=1024, name="proj_att_dx", dep=token)
    dh = mm(dzg, w["w_gz_t"], add=dh, tm=1024, name="proj_gate_dx")
    dy1, dy1_bf, small["mix_norm"] = rms_bwd(dh, sv["y1"], row(p["mix_norm"]), dy2, name="mix_norm_bwd")
    dx, dx_bf, small["ffn1_norm"], act, dg, du = ffn_bwd_tokens(
        dy1, sv["x"], row(p["ffn1_norm"]), sv["gg1"], sv["uu1"], g1["ffn1_w_gate"], g1["ffn1_w_up"], g1["ffn1_w_down"],
        name="ffn_bwd_tokens")
    emit("ffn1", shards(ffn_bwd_weights(sv["h1"], dy1_bf, act, dg, du, name="ffn_bwd_weights")))
    return dx, dx_bf, small, dt5


ANY = pl.BlockSpec(memory_space=pl.ANY)


def _place():
    x, y, c = lax.axis_index("x"), lax.axis_index("y"), lax.axis_index("c")
    chips = [(1 - x, y), (x, 1 - y), (1 - x, 1 - y)]
    return x, y, c, chips


def _remote(src, dst, send_sem, recv_sem, to):
    return pltpu.make_async_remote_copy(src_ref=src, dst_ref=dst, send_sem=send_sem, recv_sem=recv_sem, device_id=to,
                                        device_id_type=MESH)


HBM = pl.BlockSpec(memory_space=pltpu.HBM)
SEM = pl.BlockSpec(memory_space=pltpu.SEMAPHORE)
ORDERED_EFFECT = pltpu.SideEffectType.DATAFLOW_SIDE_EFFECTING


def _in_hbm(v):
    return pltpu.with_memory_space_constraint(v, pltpu.HBM)


def _row_half(ref_shape_rows, c):
    half = ref_shape_rows // 2
    return pl.ds(c * half, half)


def _ici_gather_copies(w, land, send_sems, recv_sems):
    x, y, c, chips = _place()
    me = 2 * x + y
    copies = []
    for a in range(len(w)):
        rows = _row_half(w[a].shape[0], c)
        for k, chip in enumerate(chips):
            copies.append(_remote(w[a].at[rows], land[a].at[me, rows], send_sems.at[4 * a + k], recv_sems.at[4 * a + k],
                                  (*chip, c)))
        copies.append(_remote(w[a], land[a].at[me], send_sems.at[4 * a + 3], recv_sems.at[4 * a + 3], (x, y, 1 - c)))
    return copies


def _d2d_gather_copies(w, land, send_sems, recv_sems):
    x, y, c, chips = _place()
    copies = []
    for a in range(len(w)):
        rows = _row_half(w[a].shape[0], c)
        for k, (cx, cy) in enumerate(chips):
            blk = land[a].at[2 * cx + cy, rows]
            copies.append(_remote(blk, blk, send_sems.at[3 * a + k], recv_sems.at[3 * a + k], (x, y, 1 - c)))
    return copies


def _d2d_gather_waits(w, land, send_sems, recv_sems):
    x, y, c, chips = _place()
    waits = []
    for a in range(len(w)):
        rows = _row_half(w[a].shape[0], 1 - c)
        for k, (cx, cy) in enumerate(chips):
            blk = land[a].at[2 * cx + cy, rows]
            waits.append(_remote(blk, blk, send_sems.at[3 * a + k], recv_sems.at[3 * a + k], (x, y, 1 - c)))
    return waits


def gather_start(groups, dep=None, *, name):
    sizes = [len(g) for g in groups]
    shards = [s for g in groups for s in g]
    n, ng = len(shards), len(groups)
    extra = [] if dep is None else [dep]

    def body(*refs):
        first_out = 2 * n + len(extra)
        w, land, sems = refs[:n], refs[n:2 * n], refs[first_out:first_out + 2 * ng]
        off = 0
        for gi, size in enumerate(sizes):
            for cp in _ici_gather_copies(w[off:off + size], land[off:off + size], sems[2 * gi], sems[2 * gi + 1]):
                cp.start()
            off += size

    lands = [lax.empty((NSH,) + s.shape, s.dtype) for s in shards]
    sem_shapes = tuple(pltpu.SemaphoreType.DMA((4 * size,)) for size in sizes for _ in range(2))
    res = pl.pallas_call(
        body, name=name,
        out_shape=sem_shapes + tuple(pltpu.HBM(s.shape, s.dtype) for s in shards) + tuple(pltpu.HBM(l.shape, l.dtype) for l in lands),
        in_specs=[HBM] * (2 * n) + [ANY] * len(extra), out_specs=(SEM,) * (2 * ng) + (HBM,) * (2 * n),
        input_output_aliases={i: 2 * ng + i for i in range(2 * n)},
        compiler_params=pltpu.CompilerParams(has_side_effects=ORDERED_EFFECT))(
            *[_in_hbm(s) for s in shards], *[_in_hbm(l) for l in lands], *extra)
    out, off = [], 0
    for gi, size in enumerate(sizes):
        out.append((res[2 * gi], res[2 * gi + 1], list(res[2 * ng + off:2 * ng + off + size]),
                    list(res[2 * ng + n + off:2 * ng + n + off + size])))
        off += size
    return out


def gather_wait(send_sems, recv_sems, shards, lands, after, *, name):
    n = len(shards)

    def body(*refs):
        w, land = refs[:n], refs[n:2 * n]
        send, recv = refs[2 * n:2 * n + 2]
        for cp in _ici_gather_copies(w, land, send, recv):
            cp.wait_send()
            cp.wait_recv()

    res = pl.pallas_call(
        body, name=name,
        out_shape=tuple(pltpu.HBM(s.shape, s.dtype) for s in shards) + tuple(pltpu.HBM(l.shape, l.dtype) for l in lands),
        in_specs=[HBM] * (2 * n) + [SEM, SEM] + [ANY] * len(after), out_specs=(HBM,) * (2 * n),
        input_output_aliases={i: i for i in range(2 * n)},
        compiler_params=pltpu.CompilerParams(has_side_effects=ORDERED_EFFECT))(*shards, *lands, send_sems, recv_sems, *after)
    return list(res[:n]), list(res[n:])


def gather_finish(shards, lands, *, name):
    n = len(shards)

    def body(*refs):
        w, land = refs[:n], refs[n:2 * n]
        send_sems, recv_sems = refs[3 * n:]
        d2d = _d2d_gather_copies(w, land, send_sems, recv_sems)
        for cp in d2d:
            cp.start()
        for cp in _d2d_gather_waits(w, land, send_sems, recv_sems):
            cp.wait_recv()
        for cp in d2d:
            cp.wait_send()

    return list(pl.pallas_call(
        body, name=name, out_shape=tuple(pltpu.HBM(l.shape, l.dtype) for l in lands),
        in_specs=[ANY] * (2 * n), out_specs=tuple([ANY] * n), input_output_aliases={n + i: i for i in range(n)},
        scratch_shapes=[pltpu.SemaphoreType.DMA((3 * n,)), pltpu.SemaphoreType.DMA((3 * n,))])(*shards, *lands))


def _pair_exchange_copies(g, buf, send_sems, recv_sems):
    x, y, c, _ = _place()
    copies = []
    for a in range(len(g)):
        half = g[a].shape[1] // 2
        copies.append(_remote(g[a].at[:, pl.ds((1 - c) * half, half)], buf[a], send_sems.at[a], recv_sems.at[a], (x, y, 1 - c)))
    return copies


def pair_exchange_start(grads, dep=None, *, name):
    n = len(grads)
    extra = [] if dep is None else [dep]

    def body(*refs):
        sems = refs[2 * n + len(extra):]
        for cp in _pair_exchange_copies(refs[:n], refs[n:2 * n], sems[0], sems[1]):
            cp.start()
        refs[-1][...] = jnp.zeros_like(refs[-1])

    lands = [lax.empty((NSH, g.shape[1] // 2, g.shape[2]), g.dtype) for g in grads]
    res = pl.pallas_call(
        body, name=name,
        out_shape=(pltpu.SemaphoreType.DMA((n,)), pltpu.SemaphoreType.DMA((n,)))
        + tuple(pltpu.HBM(g.shape, g.dtype) for g in grads) + tuple(pltpu.HBM(l.shape, l.dtype) for l in lands)
        + (_sds((8, 128), F32),),
        in_specs=[HBM] * (2 * n) + [ANY] * len(extra),
        out_specs=(SEM, SEM) + (HBM,) * (2 * n) + (pl.BlockSpec(memory_space=pltpu.VMEM),),
        input_output_aliases={i: 2 + i for i in range(2 * n)},
        compiler_params=pltpu.CompilerParams(has_side_effects=ORDERED_EFFECT))(
            *[_in_hbm(g) for g in grads], *[_in_hbm(l) for l in lands], *extra)
    return res[0], res[1], list(res[2:2 + n]), list(res[2 + n:2 + 2 * n]), res[-1]


def pair_exchange_wait(send_sems, recv_sems, grads, lands, after, *, name):
    n = len(grads)

    def body(*refs):
        for cp in _pair_exchange_copies(refs[:n], refs[n:2 * n], refs[2 * n], refs[2 * n + 1]):
            cp.wait_send()
            cp.wait_recv()

    res = pl.pallas_call(
        body, name=name,
        out_shape=tuple(pltpu.HBM(g.shape, g.dtype) for g in grads) + tuple(pltpu.HBM(l.shape, l.dtype) for l in lands),
        in_specs=[HBM] * (2 * n) + [SEM, SEM] + [ANY] * len(after), out_specs=(HBM,) * (2 * n),
        input_output_aliases={i: i for i in range(2 * n)},
        compiler_params=pltpu.CompilerParams(has_side_effects=ORDERED_EFFECT))(*grads, *lands, send_sems, recv_sems, *after)
    return list(res[:n]), list(res[n:])


def _chip_exchange_copies(s, buf, send_sems, recv_sems):
    x, y, c, chips = _place()
    return [_remote(s[a].at[2 * cx + cy], buf[a].at[k], send_sems.at[3 * a + k], recv_sems.at[3 * a + k], (cx, cy, c))
            for a in range(len(s)) for k, (cx, cy) in enumerate(chips)]


def chip_exchange(sums, *, name):
    n = len(sums)

    def body(*refs):
        copies = _chip_exchange_copies(refs[:n], refs[n:2 * n], *refs[2 * n:])
        for cp in copies:
            cp.start()
        for cp in copies:
            cp.wait()

    return pl.pallas_call(
        body, name=name, out_shape=tuple(pltpu.HBM((3,) + s.shape[1:], s.dtype) for s in sums),
        in_specs=[ANY] * n, out_specs=tuple([ANY] * n),
        scratch_shapes=[pltpu.SemaphoreType.DMA((3 * n,)), pltpu.SemaphoreType.DMA((3 * n,))])(*sums)


def chip_exchange_start(sums, *, name):
    n = len(sums)

    def body(*refs):
        for cp in _chip_exchange_copies(refs[:n], refs[n:2 * n], refs[2 * n], refs[2 * n + 1]):
            cp.start()
        refs[-1][...] = jnp.zeros_like(refs[-1])

    lands = [lax.empty((3,) + s.shape[1:], s.dtype) for s in sums]
    res = pl.pallas_call(
        body, name=name,
        out_shape=(pltpu.SemaphoreType.DMA((3 * n,)), pltpu.SemaphoreType.DMA((3 * n,)))
        + tuple(pltpu.HBM(s.shape, s.dtype) for s in sums) + tuple(pltpu.HBM(l.shape, l.dtype) for l in lands)
        + (_sds((8, 128), F32),),
        in_specs=[HBM] * (2 * n), out_specs=(SEM, SEM) + (HBM,) * (2 * n) + (pl.BlockSpec(memory_space=pltpu.VMEM),),
        input_output_aliases={i: 2 + i for i in range(2 * n)},
        compiler_params=pltpu.CompilerParams(has_side_effects=ORDERED_EFFECT))(
            *[_in_hbm(s) for s in sums], *[_in_hbm(l) for l in lands])
    return res[0], res[1], list(res[2:2 + n]), list(res[2 + n:2 + 2 * n]), res[-1]


def chip_exchange_wait(send_sems, recv_sems, sums, lands, after, *, name):
    n = len(sums)

    def body(*refs):
        for cp in _chip_exchange_copies(refs[:n], refs[n:2 * n], refs[2 * n], refs[2 * n + 1]):
            cp.wait_send()
            cp.wait_recv()

    res = pl.pallas_call(
        body, name=name,
        out_shape=tuple(pltpu.HBM(s.shape, s.dtype) for s in sums) + tuple(pltpu.HBM(l.shape, l.dtype) for l in lands),
        in_specs=[HBM] * (2 * n) + [SEM, SEM] + [ANY] * len(after), out_specs=(HBM,) * (2 * n),
        input_output_aliases={i: i for i in range(2 * n)},
        compiler_params=pltpu.CompilerParams(has_side_effects=ORDERED_EFFECT))(*sums, *lands, send_sems, recv_sems, *after)
    return list(res[:n]), list(res[n:])


def _pair_send_copies(h, got, send_sems, recv_sems):
    x, y, c, _ = _place()
    return [_remote(h[i], got[i], send_sems.at[i], recv_sems.at[i], (x, y, 1 - c)) for i in range(len(h))]


def pair_send_start(halves, *, name):
    n = len(halves)

    def body(*refs):
        for cp in _pair_send_copies(refs[:n], refs[n:2 * n], refs[2 * n], refs[2 * n + 1]):
            cp.start()
        refs[-1][...] = jnp.zeros_like(refs[-1])

    lands = [lax.empty(h.shape, h.dtype) for h in halves]
    res = pl.pallas_call(
        body, name=name,
        out_shape=(pltpu.SemaphoreType.DMA((n,)), pltpu.SemaphoreType.DMA((n,)))
        + tuple(pltpu.HBM(h.shape, h.dtype) for h in halves) * 2 + (_sds((8, 128), F32),),
        in_specs=[HBM] * (2 * n), out_specs=(SEM, SEM) + (HBM,) * (2 * n) + (pl.BlockSpec(memory_space=pltpu.VMEM),),
        input_output_aliases={i: 2 + i for i in range(2 * n)},
        compiler_params=pltpu.CompilerParams(has_side_effects=ORDERED_EFFECT))(
            *[_in_hbm(h) for h in halves], *[_in_hbm(l) for l in lands])
    return res[0], res[1], list(res[2:2 + n]), list(res[2 + n:2 + 2 * n]), res[-1]


def pair_send_wait(send_sems, recv_sems, halves, lands, after, *, name):
    n = len(halves)

    def body(*refs):
        for cp in _pair_send_copies(refs[:n], refs[n:2 * n], refs[2 * n], refs[2 * n + 1]):
            cp.wait_send()
            cp.wait_recv()

    res = pl.pallas_call(
        body, name=name, out_shape=tuple(pltpu.HBM(h.shape, h.dtype) for h in halves) * 2,
        in_specs=[HBM] * (2 * n) + [SEM, SEM] + [ANY] * len(after), out_specs=(HBM,) * (2 * n),
        input_output_aliases={i: i for i in range(2 * n)},
        compiler_params=pltpu.CompilerParams(has_side_effects=ORDERED_EFFECT))(*halves, *lands, send_sems, recv_sems, *after)
    return list(res[:n]), list(res[n:])


def allreduce_small(v, *, name):
    rows = v.shape[0]

    def body(v_ref, o_ref, gath, send_sems, recv_sems):
        x, y, c, _ = _place()
        me = 4 * x + 2 * y + c
        gath[me] = v_ref[...]
        copies = []
        for k in range(1, 8):
            fx, fy, fc = (k >> 2) & 1, (k >> 1) & 1, k & 1
            peer = (jnp.where(fx, 1 - x, x), jnp.where(fy, 1 - y, y), jnp.where(fc, 1 - c, c))
            cp = _remote(v_ref, gath.at[me], send_sems.at[k - 1], recv_sems.at[k - 1], peer)
            cp.start()
            copies.append(cp)
        for cp in copies:
            cp.wait()
        acc = gath[0]
        for d in range(1, 8):
            acc = acc + gath[d]
        o_ref[...] = acc

    return pl.pallas_call(
        body, name=name, out_shape=_sds(v.shape, F32),
        in_specs=[pl.BlockSpec(memory_space=pltpu.VMEM)], out_specs=pl.BlockSpec(memory_space=pltpu.VMEM),
        scratch_shapes=[pltpu.VMEM((8, rows, 128), F32), pltpu.SemaphoreType.DMA((7,)), pltpu.SemaphoreType.DMA((7,))])(v)


def _same_shape_runs(arrays):
    runs = {}
    for i, a in enumerate(arrays):
        runs.setdefault(a.shape, []).append(i)
    return list(runs.values())


def _per_shape(fn, *lists):
    out = [None] * len(lists[0])
    for idx in _same_shape_runs(lists[0]):
        for i, r in zip(idx, fn(*[[l[i] for i in idx] for l in lists])):
            out[i] = r
    return out


def add_halves(gs, bufs, cidx, *, name):
    cnt = len(gs)
    _, k, n = gs[0].shape

    def body(c_ref, *refs):
        g, b, o = refs[:cnt], refs[cnt:2 * cnt], refs[2 * cnt:]
        for i in range(cnt):
            o[i][...] = (g[i][...].astype(F32) + b[i][...].astype(F32)).astype(BF16)

    blk = pl.BlockSpec((None, k // 2, n), lambda s, c: (s, 0, 0))
    mine = pl.BlockSpec((None, k // 2, n), lambda s, c: (s, c[0], 0))
    return list(pl.pallas_call(
        body, name=name, out_shape=tuple(_sds(b.shape, BF16) for b in bufs),
        grid_spec=pltpu.PrefetchScalarGridSpec(
            num_scalar_prefetch=1, grid=(NSH,), in_specs=[mine] * cnt + [blk] * cnt, out_specs=tuple([blk] * cnt)),
        compiler_params=_params(("parallel",)))(cidx, *gs, *bufs))


def add_chips(sums, bufs, sidx, *, name):
    cnt = len(sums)
    _, kh, n = sums[0].shape

    def body(s_ref, *refs):
        mine, b, o = refs[:cnt], refs[cnt:2 * cnt], refs[2 * cnt:]
        for i in range(cnt):
            o[i][...] = ((mine[i][...].astype(F32) + b[i][0].astype(F32)) + (b[i][1].astype(F32) + b[i][2].astype(F32)))

    own = pl.BlockSpec((None, kh, n), lambda i, s: (s[0], 0, 0))
    got = pl.BlockSpec((3, kh, n), lambda i, s: (0, 0, 0))
    out = pl.BlockSpec((kh, n), lambda i, s: (0, 0))
    return list(pl.pallas_call(
        body, name=name, out_shape=tuple(_sds((kh, n), F32) for _ in sums),
        grid_spec=pltpu.PrefetchScalarGridSpec(
            num_scalar_prefetch=1, grid=(1,), in_specs=[own] * cnt + [got] * cnt, out_specs=tuple([out] * cnt)),
        compiler_params=_params(("arbitrary",)))(sidx, *sums, *bufs))


PARAMS = ("ffn1_norm", "ffn1_w_gate", "ffn1_w_up", "ffn1_w_down", "mix_norm", "w_in", "b_gate", "na_q_norm", "na_k_norm",
          "na_rpb", "sw_q_norm", "sw_k_norm", "sw_sink", "t5_rel_table", "w_branch_na", "w_branch_sw", "w_out", "ffn2_norm",
          "ffn2_w_gate", "ffn2_w_up", "ffn2_w_down")
SMALL_ALL = tuple(n for n in PARAMS if n not in BIG)
TRANSPOSED = ("ffn1_w_gate", "ffn1_w_up", "w_in", "ffn2_w_gate", "ffn2_w_up")
SMALL_ROWS = 152


def _pack_small(vals):
    flat = jnp.concatenate([vals[n].reshape(-1).astype(F32) for n in SMALL_ALL] + [vals["loss"].reshape(-1)])
    return jnp.pad(flat, (0, SMALL_ROWS * 128 - flat.shape[0])).reshape(SMALL_ROWS, 128)


def _unpack_small(packed, like):
    flat, out, off = packed.reshape(-1), {}, 0
    for n in SMALL_ALL:
        size = math.prod(like[n].shape)
        out[n] = flat[off:off + size].reshape(like[n].shape)
        off += size
    out["loss"] = flat[off]
    return out


def kernel(x, ffn1_norm, ffn1_w_gate, ffn1_w_up, ffn1_w_down, mix_norm, w_in, b_gate, na_q_norm, na_k_norm, na_rpb, sw_q_norm, sw_k_norm, sw_sink, t5_rel_table, w_branch_na, w_branch_sw, w_out, ffn2_norm, ffn2_w_gate, ffn2_w_up, ffn2_w_down, loss_target, m_ffn1_norm, m_ffn1_w_gate, m_ffn1_w_up, m_ffn1_w_down, m_mix_norm, m_w_in, m_b_gate, m_na_q_norm, m_na_k_norm, m_na_rpb, m_sw_q_norm, m_sw_k_norm, m_sw_sink, m_t5_rel_table, m_w_branch_na, m_w_branch_sw, m_w_out, m_ffn2_norm, m_ffn2_w_gate, m_ffn2_w_up, m_ffn2_w_down, v_ffn1_norm, v_ffn1_w_gate, v_ffn1_w_up, v_ffn1_w_down, v_mix_norm, v_w_in, v_b_gate, v_na_q_norm, v_na_k_norm, v_na_rpb, v_sw_q_norm, v_sw_k_norm, v_sw_sink, v_t5_rel_table, v_w_branch_na, v_w_branch_sw, v_w_out, v_ffn2_norm, v_ffn2_w_gate, v_ffn2_w_up, v_ffn2_w_down):
    args = locals()
    tr = lambda n, a: jnp.transpose(a, (0, 2, 1)) if n in TRANSPOSED else a
    w = {n: tr(n, args[n]) for n in PARAMS}
    m = {n: tr(n, args["m_" + n]) for n in PARAMS}
    v = {n: tr(n, args["v_" + n]) for n in PARAMS}
    cidx = lax.axis_index("c").astype(jnp.int32).reshape(1)
    sidx = (2 * lax.axis_index("x") + lax.axis_index("y")).astype(jnp.int32).reshape(1)

    small = [{n: w[n][l] for n in SMALL} for l in range(DEPTH)]
    order = ("ffn1", "mix", "ffn2")

    keys = [(l, g) for l in range(DEPTH) for g in order]
    local = lambda l, g: [w[n][l].astype(BF16) for n in GROUPS[g]]
    first = gather_start([local(*keys[0])], name="gather_start")
    rest = gather_start([local(*key) for key in keys[1:]], first[0][2][0], name="gather_start")
    in_flight = dict(zip(keys, first + rest))
    t5b = t5_bias(w["t5_rel_table"], name="t5_bias")
    for l in range(DEPTH):
        small[l]["na_bias"] = na_bias_table(small[l]["na_rpb"], name="na_bias_table")
    early = [t5b] + [small[l]["na_bias"] for l in range(DEPTH)] + [rest[0][2][0]]

    def weights_of(l):
        def get(group, after):
            send_sems, recv_sems, thru, lands = in_flight[(l, group)]
            after = [after] + (early if (l, group) == keys[0] else [])
            thru, lands = gather_wait(send_sems, recv_sems, thru, lands, after, name="gather_wait")
            return dict(zip(GROUPS[group], gather_finish(thru, lands, name="gather_finish")))
        return get

    h0, saved0 = layer_fwd(x[0], small[0], weights_of(0), t5b)
    h1, saved1 = layer_fwd(h0, small[1], weights_of(1), t5b)
    dy, dy_bf, loss_row = loss_head(h1, loss_target[0], name="loss_head")

    crossing, tokens, pending = {}, [], []

    def ship(after):
        key, send_sems, recv_sems, grads, lands = pending.pop()
        grads, from_sibling = pair_exchange_wait(send_sems, recv_sems, grads, lands, after, name="pair_exchange_wait")
        sums = _per_shape(lambda gs, bs: add_halves(gs, bs, cidx, name="add_halves"), grads, from_sibling)
        send_sems, recv_sems, sums, lands, token = chip_exchange_start(sums, name="chip_exchange_start")
        crossing[key] = (send_sems, recv_sems, sums, lands)
        return token

    def reduce_of(l):
        def emit(group, grads):
            grads = list(grads)
            shipped = ship([grads[0]]) if pending else None
            send_sems, recv_sems, grads, lands, token = pair_exchange_start(grads, shipped, name="pair_exchange_start")
            pending.append(((l, group), send_sems, recv_sems, grads, lands))
            tokens.append(token)
            return token
        return emit

    def finish(layer, after, filled=None):
        sent = {}
        for group in order:
            send_sems, recv_sems, sums, lands = crossing[(layer, group)]
            sums, got = chip_exchange_wait(send_sems, recv_sems, sums, lands, after, name="chip_exchange_wait")
            halves = _per_shape(lambda ss, bs: add_chips(ss, bs, sidx, name="add_chips"), sums, got)
            sent[group] = pair_send_start(halves, name="pair_send_start")
            after = [sent[group][4]]
        out = {}
        for group in order:
            send_sems, recv_sems, halves, lands, _ = sent[group]
            halves, theirs = pair_send_wait(send_sems, recv_sems, halves, lands, after, name="pair_send_wait")
            names = GROUPS[group]
            res = _per_shape(
                lambda ws, ms, vs, a, b, *f: adamw_layer(ws, ms, vs, a, b, cidx, layer, list(f[0]) if f else None, name="adamw_layer"),
                *([[w[n] for n in names], [m[n] for n in names], [v[n] for n in names], halves, theirs]
                  + ([[filled[n] for n in names]] if filled is not None else [])))
            out.update(zip(names, res))
            after = [res[-1][0]]
        return out

    dy, dy_bf, small1, dt5_1 = layer_bwd(dy, dy_bf, saved1, small[1], t5b, reduce_of(1))
    grad_x, _, small0, dt5_0 = layer_bwd(dy, dy_bf, saved0, small[0], t5b, reduce_of(0), dep=tokens[-1])
    done1 = finish(1, [ship([grad_x])])

    smalls = [small0, small1]
    dt5 = t5_table_grad(dt5_0, dt5_1, name="t5_table_grad").reshape(32, 8)
    local_small = {n: jnp.stack([smalls[l][n].reshape(w[n].shape[1:]) for l in range(DEPTH)]) for n in SMALL}
    local_small["t5_rel_table"] = dt5
    local_small["loss"] = loss_row[0, 0:1]
    total = allreduce_small(_pack_small(local_small), name="allreduce_small")
    small_grads = _unpack_small(total, w)
    small_done = adamw_small([w[n] for n in SMALL_ALL], [small_grads[n] for n in SMALL_ALL], [m[n] for n in SMALL_ALL],
                             [v[n] for n in SMALL_ALL], name="adamw_small")

    grad, delta, new_m, new_v = {}, {}, {}, {}
    for n, done in finish(0, [small_done[0][0], done1[BIG[-1]][0]], filled=done1).items():
        grad[n], delta[n], new_m[n], new_v[n] = done
    for n, done in zip(SMALL_ALL, small_done):
        grad[n] = small_grads[n]
        delta[n], new_m[n], new_v[n] = done

    return (small_grads["loss"], grad_x[None], *[tr(n, grad[n]) for n in PARAMS], *[tr(n, delta[n]) for n in PARAMS],
            *[tr(n, new_m[n]) for n in PARAMS], *[tr(n, new_v[n]) for n in PARAMS])
```

```python
import functools
import math

import jax
import jax.numpy as jnp
import numpy as np
from jax import lax
from jax.experimental import pallas as pl
from jax.experimental.pallas import tpu as pltpu

F32 = jnp.float32
BF16 = jnp.bfloat16

SEQ = 2048
DM = 1024
DFF = 2816
DEPTH = 2
NSH = 4
FSH = DFF // NSH
GRID_W = 64
ROWS = SEQ // GRID_W
NA_HEADS = 8
HD = 64
NA_WR = 8
NA_WC = 16
NA_KEYS = NA_WR * GRID_W
SW_BLK = 128
SW_NB = SEQ // SW_BLK
SW_KEYS = 3 * SW_BLK
ATT_W = 2304
GATE_W = 2048
IN_W = ATT_W + GATE_W
EPS = 1e-6
NEG = -1e30
QK_SCALE = 1.0 / math.sqrt(HD)

ADAM_LR = 0.001
ADAM_B1 = 0.9
ADAM_B2 = 0.999
ADAM_EPS = 1e-08
ADAM_WD = 0.01
ADAM_STEP = 10

VMEM_LIMIT = 56 << 20
MESH = pl.DeviceIdType.MESH

NT = (((1,), (1,)), ((), ()))
TN = (((0,), (0,)), ((), ()))
NN = (((1,), (0,)), ((), ()))


def _dot(a, b, dims=NN):
    return lax.dot_general(a, b, dims, preferred_element_type=F32)


def _params(sem=None):
    return pltpu.CompilerParams(dimension_semantics=sem, vmem_limit_bytes=VMEM_LIMIT)


def _sds(shape, dtype):
    return jax.ShapeDtypeStruct(shape, dtype)


def mm(a, b, *, name, ta=False, tb=False, out_dtype=F32, add=None, scale=None, tm=512, tn=None, tk=None, exact=False,
       dep=None, b_rows=None):
    m, kd = (a.shape[1], a.shape[0]) if ta else a.shape
    if b_rows is None:
        n = b.shape[0] if tb else b.shape[1]
    else:
        n = b_rows[1] if tb else b.shape[1]
        assert tb or (b_rows[1] == kd and (tk or kd) == kd)
    tm, tn, tk = min(tm, m), min(tn or n, n), min(tk or kd, kd)
    nk = kd // tk
    dims = (((0 if ta else 1,), (1 if tb else 0,)), ((), ()))

    def body(*refs):
        a_ref, b_ref = refs[:2]
        add_ref = refs[2] if add is not None else None
        o_ref = refs[-1] if nk == 1 else refs[-2]
        if b_rows is None:
            bv = b_ref[...]
        elif tb:
            bv = b_ref[pl.ds(pl.multiple_of(b_rows[0] + pl.program_id(1) * tn, 16), tn), :]
        else:
            bv = b_ref[b_rows[0]:b_rows[0] + b_rows[1], :]
        if exact:
            part = lax.dot_general(a_ref[...], bv, dims, precision=lax.Precision.HIGHEST, preferred_element_type=F32)
        else:
            part = lax.dot_general(a_ref[...].astype(BF16), bv.astype(BF16), dims, preferred_element_type=F32)

        def finish(r):
            if scale is not None:
                r = r * scale
            if add is not None:
                r = r + add_ref[...]
            o_ref[...] = r.astype(out_dtype)

        if nk == 1:
            finish(part)
        else:
            acc, k = refs[-1], pl.program_id(2)

            @pl.when(k == 0)
            def _():
                acc[...] = part

            @pl.when(k != 0)
            def _():
                acc[...] += part

            pl.when(k == nk - 1)(lambda: finish(acc[...]))

    a_spec = pl.BlockSpec((tk, tm), lambda i, j, k: (k, i)) if ta else pl.BlockSpec((tm, tk), lambda i, j, k: (i, k))
    if b_rows is not None:
        b_spec = pl.BlockSpec(b.shape, lambda i, j, k: (0, 0), pipeline_mode=pl.Buffered(1))
    else:
        b_spec = pl.BlockSpec((tn, tk), lambda i, j, k: (j, k)) if tb else pl.BlockSpec((tk, tn), lambda i, j, k: (k, j))
    o_spec = pl.BlockSpec((tm, tn), lambda i, j, k: (i, j))
    ins, specs = [a, b], [a_spec, b_spec]
    if add is not None:
        ins.append(add)
        specs.append(o_spec)
    if dep is not None:
        ins.append(dep)
        specs.append(pl.BlockSpec(memory_space=pl.ANY))
    return pl.pallas_call(
        body, name=name, out_shape=_sds((m, n), out_dtype), grid=(m // tm, n // tn, nk), in_specs=specs,
        out_specs=o_spec, scratch_shapes=[] if nk == 1 else [pltpu.VMEM((tm, tn), F32)],
        compiler_params=_params(("parallel", "parallel", "arbitrary")))(*ins)


def _rms(x):
    return lax.rsqrt(jnp.mean(x * x, axis=-1, keepdims=True) + EPS)


def rms_fwd(x, gain, *, name, tm=512):
    def body(x_ref, g_ref, h_ref):
        x = x_ref[...]
        h_ref[...] = (x * _rms(x) * g_ref[...]).astype(BF16)

    return pl.pallas_call(
        body, name=name, out_shape=_sds(x.shape, BF16), grid=(x.shape[0] // tm,),
        in_specs=[pl.BlockSpec((tm, DM), lambda i: (i, 0)), pl.BlockSpec((1, DM), lambda i: (0, 0))],
        out_specs=pl.BlockSpec((tm, DM), lambda i: (i, 0)), compiler_params=_params(("parallel",)))(x, gain)


def _rms_bwd_math(dh, x, gain):
    r = _rms(x)
    xh = x * r
    dgain = jnp.sum(dh * xh, axis=0, keepdims=True)
    dxn = dh * gain
    dx = r * (dxn - xh * jnp.mean(dxn * xh, axis=-1, keepdims=True))
    return dx, dgain


def rms_bwd(dh, x, gain, dres, *, name, tm=512):
    def body(dh_ref, x_ref, g_ref, dres_ref, dx_ref, dxb_ref, dg_ref):
        @pl.when(pl.program_id(0) == 0)
        def _():
            dg_ref[...] = jnp.zeros_like(dg_ref)

        dx, dg = _rms_bwd_math(dh_ref[...], x_ref[...], g_ref[...])
        dx = dres_ref[...] + dx
        dx_ref[...] = dx
        dxb_ref[...] = dx.astype(BF16)
        dg_ref[...] += dg

    tile = pl.BlockSpec((tm, DM), lambda i: (i, 0))
    vec = pl.BlockSpec((1, DM), lambda i: (0, 0))
    return pl.pallas_call(
        body, name=name, out_shape=(_sds(x.shape, F32), _sds(x.shape, BF16), _sds((1, DM), F32)), grid=(x.shape[0] // tm,),
        in_specs=[tile, tile, vec, tile], out_specs=(tile, tile, vec), compiler_params=_params(("arbitrary",)))(dh, x, gain, dres)


def _with_dep(ins, specs, dep):
    if dep is None:
        return ins, specs
    return ins + [dep], specs + [pl.BlockSpec(memory_space=pl.ANY)]


def _resident_weight():
    return pl.BlockSpec((DFF, DM), lambda i: (0, 0), pipeline_mode=pl.Buffered(1))


def ffn_fwd(x, gain, wg, wu, wd, *, name, tm=512):
    def body(x_ref, g_ref, wg_ref, wu_ref, wd_ref, y_ref, h_ref, gg_ref, uu_ref):
        x = x_ref[...]
        h = (x * _rms(x) * g_ref[...]).astype(BF16)
        h_ref[...] = h
        gg = _dot(h, wg_ref[...], NT)
        uu = _dot(h, wu_ref[...], NT)
        gg_ref[...] = gg.astype(BF16)
        uu_ref[...] = uu.astype(BF16)
        act = (gg * jax.nn.sigmoid(gg) * uu).astype(BF16)
        y_ref[...] = x + 0.5 * _dot(act, wd_ref[...])

    s = x.shape[0]
    tile = pl.BlockSpec((tm, DM), lambda i: (i, 0))
    hid = pl.BlockSpec((tm, DFF), lambda i: (i, 0))
    w = _resident_weight()
    return pl.pallas_call(
        body, name=name,
        out_shape=(_sds((s, DM), F32), _sds((s, DM), BF16), _sds((s, DFF), BF16), _sds((s, DFF), BF16)),
        grid=(s // tm,), in_specs=[tile, pl.BlockSpec((1, DM), lambda i: (0, 0)), w, w, w],
        out_specs=(tile, tile, hid, hid), compiler_params=_params(("parallel",)))(x, gain, wg, wu, wd)


def ffn_bwd_tokens(dy, x, gain, gg, uu, wg, wu, wd, *, name, tm=256, dep=None):
    def body(dy_ref, x_ref, g_ref, gg_ref, uu_ref, wg_ref, wu_ref, wd_ref, *rest):
        dx_ref, dxb_ref, dgain_ref, act_ref, dg_ref, du_ref = rest[-6:]

        @pl.when(pl.program_id(0) == 0)
        def _():
            dgain_ref[...] = jnp.zeros_like(dgain_ref)

        dy = dy_ref[...]
        dact = _dot((0.5 * dy).astype(BF16), wd_ref[...], NT)
        g = gg_ref[...].astype(F32)
        u = uu_ref[...].astype(F32)
        sg = jax.nn.sigmoid(g)
        silu = g * sg
        act_ref[...] = (silu * u).astype(BF16)
        dg = (dact * u * (sg * (1.0 + g * (1.0 - sg)))).astype(BF16)
        du = (dact * silu).astype(BF16)
        dg_ref[...] = dg
        du_ref[...] = du
        dx, dgain = _rms_bwd_math(_dot(dg, wg_ref[...]) + _dot(du, wu_ref[...]), x_ref[...], g_ref[...])
        dx = dy + dx
        dx_ref[...] = dx
        dxb_ref[...] = dx.astype(BF16)
        dgain_ref[...] += dgain

    s = x.shape[0]
    tile = pl.BlockSpec((tm, DM), lambda i: (i, 0))
    vec = pl.BlockSpec((1, DM), lambda i: (0, 0))
    hid = pl.BlockSpec((tm, DFF), lambda i: (i, 0))
    hshape = _sds((s, DFF), BF16)
    w = _resident_weight()
    ins, specs = _with_dep([dy, x, gain, gg, uu, wg, wu, wd], [tile, tile, vec, hid, hid, w, w, w], dep)
    return pl.pallas_call(
        body, name=name, out_shape=(_sds((s, DM), F32), _sds((s, DM), BF16), _sds((1, DM), F32), hshape, hshape, hshape),
        grid=(s // tm,), in_specs=specs, out_specs=(tile, tile, vec, hid, hid, hid),
        compiler_params=_params(("arbitrary",)))(*ins)


def ffn_bwd_weights(h, dy, act, dg, du, *, name, tf=256):
    def body(h_ref, dy_ref, act_ref, dg_ref, du_ref, gwg_ref, gwu_ref, gwd_ref):
        h = h_ref[...]
        gwg_ref[...] = _dot(dg_ref[...], h, TN).astype(BF16)
        gwu_ref[...] = _dot(du_ref[...], h, TN).astype(BF16)
        gwd_ref[...] = (0.5 * _dot(act_ref[...], dy_ref[...], TN)).astype(BF16)

    s = h.shape[0]
    full = pl.BlockSpec((s, DM), lambda f: (0, 0))
    hid = pl.BlockSpec((s, tf), lambda f: (0, f))
    wt = pl.BlockSpec((tf, DM), lambda f: (f, 0))
    wshape = _sds((DFF, DM), BF16)
    return pl.pallas_call(
        body, name=name, out_shape=(wshape, wshape, wshape), grid=(DFF // tf,), in_specs=[full, full, hid, hid, hid],
        out_specs=(wt, wt, wt), compiler_params=_params(("parallel",)))(h, dy, act, dg, du)


def _group_mean(v, bd):
    hi = v.astype(BF16)
    lo = (v - hi.astype(F32)).astype(BF16)
    return _dot(hi, bd) + _dot(lo, bd)


def _block_diag(width):
    idx = np.arange(width) // HD
    return jnp.asarray((idx[:, None] == idx[None, :]).astype(np.float32) / HD, dtype=BF16)


def qknorm_fwd(z, gq_na, gk_na, gq_sw, gk_sw, *, name, tm=256):
    def body(zq_ref, zk_ref, zv_ref, zs_ref, zkv_ref, gqa_ref, gka_ref, gqs_ref, gks_ref, bd_ref, bd2_ref,
             qa_ref, ka_ref, va_ref, qs_ref, kv_ref):
        bd = bd_ref[...]

        def norm(x, g, bdm):
            x = x.astype(F32)
            return x * lax.rsqrt(_group_mean(x * x, bdm) + EPS) * g

        qa_ref[...] = (norm(zq_ref[...], gqa_ref[...], bd) * QK_SCALE).astype(BF16)
        ka_ref[...] = norm(zk_ref[...], gka_ref[...], bd).astype(BF16)
        va_ref[...] = zv_ref[...].astype(BF16)
        qs_ref[...] = (norm(zs_ref[...], gqs_ref[...], bd) * QK_SCALE).astype(BF16)
        kv = zkv_ref[...]
        kv_ref[:, 0:128] = norm(kv[:, 0:128], gks_ref[...], bd2_ref[...]).astype(BF16)
        kv_ref[:, 128:256] = kv[:, 128:256].astype(BF16)

    s = z.shape[0]
    col = lambda j: pl.BlockSpec((tm, 512), lambda i, j=j: (i, j))
    vec = lambda w: pl.BlockSpec((1, w), lambda i: (0, 0))
    o512 = pl.BlockSpec((tm, 512), lambda i: (i, 0))
    g512 = lambda g: jnp.tile(g.reshape(1, HD), (1, 8))
    return pl.pallas_call(
        body, name=name,
        out_shape=(_sds((s, 512), BF16),) * 4 + (_sds((s, 256), BF16),), grid=(s // tm,),
        in_specs=[col(0), col(1), col(2), col(3), pl.BlockSpec((tm, 256), lambda i: (i, 8)), vec(512), vec(512), vec(512),
                  vec(128), pl.BlockSpec((512, 512), lambda i: (0, 0)), pl.BlockSpec((128, 128), lambda i: (0, 0))],
        out_specs=(o512, o512, o512, o512, pl.BlockSpec((tm, 256), lambda i: (i, 0))),
        compiler_params=_params(("parallel",)))(
            z, z, z, z, z, g512(gq_na), g512(gk_na), g512(gq_sw), jnp.tile(gk_sw.reshape(1, HD), (1, 2)),
            _block_diag(512), _block_diag(128))


def qknorm_bwd(z, dqa, dka, dva, dqs, dkv, gq_na, gk_na, gq_sw, gk_sw, *, name, tm=256):
    def body(zq_ref, zk_ref, zs_ref, zkv_ref, dqa_ref, dka_ref, dva_ref, dqs_ref, dkv_ref, gqa_ref, gka_ref, gqs_ref,
             gks_ref, bd_ref, bd2_ref, dz_ref, dgqa_ref, dgka_ref, dgqs_ref, dgks_ref):
        @pl.when(pl.program_id(0) == 0)
        def _():
            dgqa_ref[...] = jnp.zeros_like(dgqa_ref)
            dgka_ref[...] = jnp.zeros_like(dgka_ref)
            dgqs_ref[...] = jnp.zeros_like(dgqs_ref)
            dgks_ref[...] = jnp.zeros_like(dgks_ref)

        bd = bd_ref[...]

        def bwd(x, dy, g, bdm, dg_ref):
            x = x.astype(F32)
            r = lax.rsqrt(_group_mean(x * x, bdm) + EPS)
            xh = x * r
            dg_ref[...] += jnp.sum(dy * xh, axis=0, keepdims=True)
            dxn = dy * g
            return r * (dxn - xh * _group_mean(dxn * xh, bdm))

        dz_ref[:, 0:512] = bwd(zq_ref[...], dqa_ref[...] * QK_SCALE, gqa_ref[...], bd, dgqa_ref).astype(BF16)
        dz_ref[:, 512:1024] = bwd(zk_ref[...], dka_ref[...], gka_ref[...], bd, dgka_ref).astype(BF16)
        dz_ref[:, 1024:1536] = dva_ref[...].astype(BF16)
        dz_ref[:, 1536:2048] = bwd(zs_ref[...], dqs_ref[...] * QK_SCALE, gqs_ref[...], bd, dgqs_ref).astype(BF16)
        dkv = dkv_ref[...]
        dz_ref[:, 2048:2176] = bwd(zkv_ref[:, 0:128], dkv[:, 0:128], gks_ref[...], bd2_ref[...], dgks_ref).astype(BF16)
        dz_ref[:, 2176:2304] = dkv[:, 128:256].astype(BF16)

    s = z.shape[0]
    col = lambda j: pl.BlockSpec((tm, 512), lambda i, j=j: (i, j))
    t512 = pl.BlockSpec((tm, 512), lambda i: (i, 0))
    t256 = pl.BlockSpec((tm, 256), lambda i: (i, 0))
    vec = lambda w: pl.BlockSpec((1, w), lambda i: (0, 0))
    g512 = lambda g: jnp.tile(g.reshape(1, HD), (1, 8))
    return pl.pallas_call(
        body, name=name,
        out_shape=(_sds((s, ATT_W), BF16), _sds((1, 512), F32), _sds((1, 512), F32), _sds((1, 512), F32), _sds((1, 128), F32)),
        grid=(s // tm,),
        in_specs=[col(0), col(1), col(3), pl.BlockSpec((tm, 256), lambda i: (i, 8)), t512, t512, t512, t512, t256,
                  vec(512), vec(512), vec(512), vec(128), pl.BlockSpec((512, 512), lambda i: (0, 0)),
                  pl.BlockSpec((128, 128), lambda i: (0, 0))],
        out_specs=(pl.BlockSpec((tm, ATT_W), lambda i: (i, 0)), vec(512), vec(512), vec(512), vec(128)),
        compiler_params=_params(("arbitrary",)))(
            z, z, z, z, dqa, dka, dva, dqs, dkv, g512(gq_na), g512(gk_na), g512(gq_sw),
            jnp.tile(gk_sw.reshape(1, HD), (1, 2)), _block_diag(512), _block_diag(128))


def _na_row_start(r):
    return jnp.clip(r - NA_WR // 2, 0, ROWS - NA_WR)


def na_bias_table(rpb, *, name):
    t = jnp.pad(rpb, ((0, 0), (0, 2), (0, HD - (2 * NA_WC - 1))))
    pairs = jnp.concatenate([t[:, :16], t[:, 1:17]], axis=-1).reshape(NA_HEADS, 16, 1, 128)

    def body(t_ref, o_ref):
        p = pl.program_id(0)
        q = lax.broadcasted_iota(jnp.int32, (GRID_W, 128), 0)
        kc = lax.broadcasted_iota(jnp.int32, (GRID_W, 128), 1) & (GRID_W - 1)
        cs = jnp.clip(q - NA_WC // 2, 0, GRID_W - NA_WC)
        ok = (kc >= cs) & (kc < cs + NA_WC)
        for h in range(NA_HEADS):
            for pr in range(NA_WR // 2):
                x = jnp.broadcast_to(t_ref[h, 2 * pr - p + NA_WR - 1], (GRID_W, 128))
                b = pltpu.roll(x, 128 - (NA_WC - 1), 1, stride=1, stride_axis=0)
                o_ref[h, :, 128 * pr:128 * pr + 128] = jnp.where(ok, b, NEG)

    return pl.pallas_call(
        body, name=name, out_shape=_sds((NA_WR, NA_HEADS, GRID_W, NA_KEYS), F32), grid=(NA_WR,),
        in_specs=[pl.BlockSpec((NA_HEADS, 16, 1, 128), lambda p: (0, 0, 0, 0))],
        out_specs=pl.BlockSpec((None, NA_HEADS, GRID_W, NA_KEYS), lambda p: (p, 0, 0, 0)),
        compiler_params=_params(("parallel",)))(pairs)


def _lane_halves():
    lane = lax.broadcasted_iota(jnp.int32, (1, 128), 1)
    return lane < HD


def na_fwd(q, k, v, bias, *, name):
    def body(q_ref, k_ref, v_ref, b_ref, o_ref, lse_ref):
        r = pl.program_id(0)
        off = pl.multiple_of(_na_row_start(r) * GRID_W, GRID_W)
        first = _lane_halves()
        sels = [first, jnp.logical_not(first)]
        lanes = [slice(128 * j, 128 * j + 128) for j in range(NA_HEADS // 2)]
        q2s = [q_ref[:, l] for l in lanes]
        k2s = [k_ref[pl.ds(off, NA_KEYS), l] for l in lanes]
        v2s = [v_ref[pl.ds(off, NA_KEYS), l] for l in lanes]
        scores = []
        for h in range(NA_HEADS):
            j, half = divmod(h, 2)
            scores.append(_dot(jnp.where(sels[half], q2s[j], jnp.zeros_like(q2s[j])), k2s[j], NT))
        probs, lses = [], []
        for h in range(NA_HEADS):
            b = b_ref[h]
            s = jnp.where(b > 0.5 * NEG, scores[h] + b, NEG)
            m = jnp.max(s, axis=-1, keepdims=True)
            e = jnp.exp(s - m)
            l = jnp.sum(e, axis=-1, keepdims=True)
            probs.append((e / l).astype(BF16))
            lses.append(m + jnp.log(l))
        for j in range(NA_HEADS // 2):
            zero = jnp.zeros_like(v2s[j])
            o2 = (_dot(probs[2 * j], jnp.where(sels[0], v2s[j], zero))
                  + _dot(probs[2 * j + 1], jnp.where(sels[1], v2s[j], zero)))
            o_ref[:, lanes[j]] = o2.astype(BF16)
        lse_ref[...] = jnp.concatenate(lses, axis=1)

    s_tok = q.shape[0]
    full = pl.BlockSpec((s_tok, 512), lambda r: (0, 0))
    return pl.pallas_call(
        body, name=name, out_shape=(_sds((s_tok, 512), BF16), _sds((s_tok, NA_HEADS), F32)), grid=(ROWS,),
        in_specs=[pl.BlockSpec((GRID_W, 512), lambda r: (r, 0)), full, full,
                  pl.BlockSpec((None, NA_HEADS, GRID_W, NA_KEYS), lambda r: (r - _na_row_start(r), 0, 0, 0))],
        out_specs=(pl.BlockSpec((GRID_W, 512), lambda r: (r, 0)), pl.BlockSpec((GRID_W, NA_HEADS), lambda r: (r, 0))),
        compiler_params=_params(("parallel",)))(q, k, v, bias)


def na_bwd(q, k, v, o, do, lse, bias, *, name):
    def body(q_ref, k_ref, v_ref, o_ref, do_ref, lse_ref, b_ref, dq_ref, dk_ref, dv_ref, db_ref):
        r = pl.program_id(0)

        @pl.when(r == 0)
        def _():
            dk_ref[...] = jnp.zeros_like(dk_ref)
            dv_ref[...] = jnp.zeros_like(dv_ref)

        @pl.when((r <= NA_WR // 2) | (r > ROWS - NA_WR // 2))
        def _():
            db_ref[...] = jnp.zeros_like(db_ref)

        off = pl.multiple_of(_na_row_start(r) * GRID_W, GRID_W)
        first = _lane_halves()
        sels = [first, jnp.logical_not(first)]
        lanes = [slice(128 * j, 128 * j + 128) for j in range(NA_HEADS // 2)]
        q2s = [q_ref[:, l] for l in lanes]
        k2s = [k_ref[pl.ds(off, NA_KEYS), l] for l in lanes]
        v2s = [v_ref[pl.ds(off, NA_KEYS), l] for l in lanes]
        do2s = [do_ref[:, l] for l in lanes]
        prods = [do2s[j].astype(F32) * o_ref[:, lanes[j]].astype(F32) for j in range(NA_HEADS // 2)]
        lse = lse_ref[...]
        qhs, dohs, scores, dps = [], [], [], []
        for h in range(NA_HEADS):
            j, half = divmod(h, 2)
            qhs.append(jnp.where(sels[half], q2s[j], jnp.zeros_like(q2s[j])))
            dohs.append(jnp.where(sels[half], do2s[j], jnp.zeros_like(do2s[j])))
            scores.append(_dot(qhs[h], k2s[j], NT))
            dps.append(_dot(dohs[h], v2s[j], NT))
        pbs, dsbs = [], []
        for h in range(NA_HEADS):
            j, half = divmod(h, 2)
            b = b_ref[h]
            s = jnp.where(b > 0.5 * NEG, scores[h] + b, NEG)
            p = jnp.exp(s - lse[:, h:h + 1])
            delta = jnp.sum(jnp.where(sels[half], prods[j], 0.0), axis=-1, keepdims=True)
            ds = p * (dps[h] - delta)
            db_ref[h] += ds
            pbs.append(p.astype(BF16))
            dsbs.append(ds.astype(BF16))
        for j in range(NA_HEADS // 2):
            a, b = 2 * j, 2 * j + 1
            zero = jnp.zeros_like(k2s[j])
            dq_ref[:, lanes[j]] = (_dot(dsbs[a], jnp.where(sels[0], k2s[j], zero))
                                   + _dot(dsbs[b], jnp.where(sels[1], k2s[j], zero)))
            dk_ref[pl.ds(off, NA_KEYS), lanes[j]] += _dot(dsbs[a], qhs[a], TN) + _dot(dsbs[b], qhs[b], TN)
            dv_ref[pl.ds(off, NA_KEYS), lanes[j]] += _dot(pbs[a], dohs[a], TN) + _dot(pbs[b], dohs[b], TN)

    s_tok = q.shape[0]
    full = pl.BlockSpec((s_tok, 512), lambda r: (0, 0))
    row = pl.BlockSpec((GRID_W, 512), lambda r: (r, 0))
    bias_spec = pl.BlockSpec((None, NA_HEADS, GRID_W, NA_KEYS), lambda r: (r - _na_row_start(r), 0, 0, 0))
    return pl.pallas_call(
        body, name=name,
        out_shape=(_sds((s_tok, 512), F32), _sds((s_tok, 512), F32), _sds((s_tok, 512), F32),
                   _sds((NA_WR, NA_HEADS, GRID_W, NA_KEYS), F32)),
        grid=(ROWS,),
        in_specs=[row, full, full, row, row, pl.BlockSpec((GRID_W, NA_HEADS), lambda r: (r, 0)), bias_spec],
        out_specs=(row, full, full, bias_spec), compiler_params=_params(("arbitrary",)))(q, k, v, o, do, lse, bias)


def t5_bucket_map():
    rel = np.arange(SW_KEYS)[None, :] - SW_BLK - np.arange(SW_BLK)[:, None]
    nb = 16
    max_exact = nb // 2
    n = np.abs(rel)
    large = max_exact + (np.log(np.maximum(n, 1) / max_exact) / np.log(128 / max_exact) * (nb - max_exact)).astype(np.int32)
    large = np.minimum(large, nb - 1)
    return ((rel > 0) * nb + np.where(n < max_exact, n, large)).astype(np.int32)


def t5_bias(table, *, name):
    rel = np.arange(-SW_BLK, SW_BLK + 1)
    nb, max_exact = 16, 8
    n = np.abs(rel)
    large = max_exact + (np.log(np.maximum(n, 1) / max_exact) / np.log(128 / max_exact) * (nb - max_exact)).astype(np.int32)
    bucket = ((rel > 0) * nb + np.where(n < max_exact, n, np.minimum(large, nb - 1))).astype(np.int32)
    u = jnp.pad(table[jnp.asarray(bucket)].T, ((0, 0), (0, SW_KEYS - bucket.shape[0]))).reshape(8, 1, SW_KEYS)

    def body(u_ref, o_ref):
        for h in range(8):
            x = jnp.broadcast_to(u_ref[h], (SW_BLK, SW_KEYS))
            o_ref[h] = pltpu.roll(x, 0, 1, stride=1, stride_axis=0)

    return pl.pallas_call(body, name=name, out_shape=_sds((8, SW_BLK, SW_KEYS), F32), compiler_params=_params())(u)


def _sw_valid(n):
    a = lax.broadcasted_iota(jnp.int32, (SW_BLK, SW_KEYS), 0)
    j = lax.broadcasted_iota(jnp.int32, (SW_BLK, SW_KEYS), 1)
    kpos = (n - 1) * SW_BLK + j
    return (jnp.abs(j - SW_BLK - a) <= SW_BLK) & (kpos >= 0) & (kpos < SEQ)


def _dup_group(x2, g, first):
    rolled = pltpu.roll(x2, HD, 1)
    return jnp.where(first, x2, rolled) if g == 0 else jnp.where(first, rolled, x2)


def sw_fwd(q, kv, t5, sink, *, name):
    def body(q_ref, kv_ref, t5_ref, sink_ref, o_ref, lse_ref):
        n = pl.program_id(0)
        off = pl.multiple_of(n * SW_BLK, SW_BLK)
        first = _lane_halves()
        sels = [first, jnp.logical_not(first)]
        valid = _sw_valid(n)
        k2 = kv_ref[pl.ds(off, SW_KEYS), 0:128]
        v2 = kv_ref[pl.ds(off, SW_KEYS), 128:256]
        kk = [_dup_group(k2, g, first) for g in range(2)]
        vv = [_dup_group(v2, g, first) for g in range(2)]
        q2s = [q_ref[:, 128 * j:128 * j + 128] for j in range(4)]
        scores = []
        for h in range(8):
            j, half = divmod(h, 2)
            scores.append(_dot(jnp.where(sels[half], q2s[j], jnp.zeros_like(q2s[j])), kk[j // 2], NT))
        probs, lses = [], []
        for h in range(8):
            s = jnp.where(valid, scores[h] + t5_ref[h], NEG)
            snk = sink_ref[h]
            m = jnp.maximum(jnp.max(s, axis=-1, keepdims=True), snk)
            e = jnp.exp(s - m)
            den = jnp.sum(e, axis=-1, keepdims=True) + jnp.exp(snk - m)
            probs.append((e / den).astype(BF16))
            lses.append(m + jnp.log(den))
        outs = []
        for j in range(4):
            vg = vv[j // 2]
            zero = jnp.zeros_like(vg)
            outs.append(_dot(probs[2 * j], jnp.where(sels[0], vg, zero)) + _dot(probs[2 * j + 1], jnp.where(sels[1], vg, zero)))
        o_ref[...] = jnp.concatenate(outs, axis=1).astype(BF16)
        lse_ref[...] = jnp.concatenate(lses, axis=1)

    s_tok = q.shape[0]
    blk = pl.BlockSpec((SW_BLK, 512), lambda n: (n, 0))
    return pl.pallas_call(
        body, name=name, out_shape=(_sds((s_tok, 512), BF16), _sds((s_tok, 8), F32)), grid=(SW_NB,),
        in_specs=[blk, pl.BlockSpec(kv.shape, lambda n: (0, 0)), pl.BlockSpec((8, SW_BLK, SW_KEYS), lambda n: (0, 0, 0)),
                  pl.BlockSpec(memory_space=pltpu.SMEM)],
        out_specs=(blk, pl.BlockSpec((SW_BLK, 8), lambda n: (n, 0))), compiler_params=_params(("parallel",)))(q, kv, t5, sink)


def sw_bwd(q, kv, o, do, lse, t5, sink, *, name):
    def body(q_ref, kv_ref, o_ref, do_ref, lse_ref, t5_ref, sink_ref, dq_ref, dkv_ref, dt5_ref, dsink_ref):
        n = pl.program_id(0)

        @pl.when(n == 0)
        def _():
            dkv_ref[...] = jnp.zeros_like(dkv_ref)
            dt5_ref[...] = jnp.zeros_like(dt5_ref)
            dsink_ref[...] = jnp.zeros_like(dsink_ref)

        off = pl.multiple_of(n * SW_BLK, SW_BLK)
        first = _lane_halves()
        sels = [first, jnp.logical_not(first)]
        valid = _sw_valid(n)
        k2 = kv_ref[pl.ds(off, SW_KEYS), 0:128]
        v2 = kv_ref[pl.ds(off, SW_KEYS), 128:256]
        kk = [_dup_group(k2, g, first) for g in range(2)]
        vv = [_dup_group(v2, g, first) for g in range(2)]
        lanes = [slice(128 * j, 128 * j + 128) for j in range(4)]
        q2s = [q_ref[:, l] for l in lanes]
        do2s = [do_ref[:, l] for l in lanes]
        prods = [do2s[j].astype(F32) * o_ref[:, lanes[j]].astype(F32) for j in range(4)]
        lse = lse_ref[...]
        qhs, dohs, scores, dps = [], [], [], []
        for h in range(8):
            j, half = divmod(h, 2)
            qhs.append(jnp.where(sels[half], q2s[j], jnp.zeros_like(q2s[j])))
            dohs.append(jnp.where(sels[half], do2s[j], jnp.zeros_like(do2s[j])))
            scores.append(_dot(qhs[h], kk[j // 2], NT))
            dps.append(_dot(dohs[h], vv[j // 2], NT))
        pbs, dsbs, dss, dsinks = [], [], [], []
        for h in range(8):
            j, half = divmod(h, 2)
            s = jnp.where(valid, scores[h] + t5_ref[h], NEG)
            lse_h = lse[:, h:h + 1]
            p = jnp.exp(s - lse_h)
            delta = jnp.sum(jnp.where(sels[half], prods[j], 0.0), axis=-1, keepdims=True)
            ds = p * (dps[h] - delta)
            dss.append(ds)
            dsinks.append(-jnp.sum(jnp.exp(sink_ref[h] - lse_h) * delta, axis=0, keepdims=True))
            pbs.append(p.astype(BF16))
            dsbs.append(ds.astype(BF16))
        dt5_ref[...] += jnp.stack(dss)
        dsink_ref[...] += jnp.concatenate(dsinks, axis=1)
        dqs = []
        for j in range(4):
            a, b = 2 * j, 2 * j + 1
            zero = jnp.zeros_like(kk[j // 2])
            dqs.append(_dot(dsbs[a], jnp.where(sels[0], kk[j // 2], zero)) + _dot(dsbs[b], jnp.where(sels[1], kk[j // 2], zero)))
        dq_ref[...] = jnp.concatenate(dqs, axis=1)
        dk_groups, dv_groups = [], []
        for g in range(2):
            dkk = sum(_dot(dsbs[h], qhs[h], TN) for h in range(4 * g, 4 * g + 4))
            dvv = sum(_dot(pbs[h], dohs[h], TN) for h in range(4 * g, 4 * g + 4))
            dk_groups.append(dkk + pltpu.roll(dkk, HD, 1))
            dv_groups.append(dvv + pltpu.roll(dvv, HD, 1))
        dkv_ref[pl.ds(off, SW_KEYS), :] += jnp.concatenate(
            [jnp.where(first, dk_groups[0], dk_groups[1]), jnp.where(first, dv_groups[0], dv_groups[1])], axis=1)

    s_tok = q.shape[0]
    blk = pl.BlockSpec((SW_BLK, 512), lambda n: (n, 0))
    kv_spec = pl.BlockSpec(kv.shape, lambda n: (0, 0))
    t5_spec = pl.BlockSpec((8, SW_BLK, SW_KEYS), lambda n: (0, 0, 0))
    vec = pl.BlockSpec((1, 8), lambda n: (0, 0))
    return pl.pallas_call(
        body, name=name,
        out_shape=(_sds((s_tok, 512), F32), _sds(kv.shape, F32), _sds((8, SW_BLK, SW_KEYS), F32), _sds((1, 8), F32)),
        grid=(SW_NB,), in_specs=[blk, kv_spec, blk, blk, pl.BlockSpec((SW_BLK, 8), lambda n: (n, 0)), t5_spec,
                                 pl.BlockSpec(memory_space=pltpu.SMEM)],
        out_specs=(blk, kv_spec, t5_spec, vec), compiler_params=_params(("arbitrary",)))(q, kv, o, do, lse, t5, sink)


def gate_fwd(zg, bias, pa, ps, *, name, tm=512):
    def body(z0_ref, z1_ref, b0_ref, b1_ref, pa_ref, ps_ref, m_ref):
        g0 = jax.nn.sigmoid(z0_ref[...] + b0_ref[...])
        g1 = jax.nn.sigmoid(z1_ref[...] + b1_ref[...])
        m_ref[...] = (g0 * pa_ref[...] + g1 * ps_ref[...]).astype(BF16)

    s = zg.shape[0]
    half = lambda j: pl.BlockSpec((tm, DM), lambda i, j=j: (i, j))
    bvec = lambda j: pl.BlockSpec((1, DM), lambda i, j=j: (0, j))
    return pl.pallas_call(
        body, name=name, out_shape=_sds((s, DM), BF16), grid=(s // tm,),
        in_specs=[half(0), half(1), bvec(0), bvec(1), half(0), half(0)], out_specs=half(0),
        compiler_params=_params(("parallel",)))(zg, zg, bias, bias, pa, ps)


def gate_bwd(dm, zg, bias, pa, ps, *, name, tm=512):
    def body(dm_ref, z0_ref, z1_ref, b0_ref, b1_ref, pa_ref, ps_ref, dpa_ref, dps_ref, dz_ref, db_ref):
        @pl.when(pl.program_id(0) == 0)
        def _():
            db_ref[...] = jnp.zeros_like(db_ref)

        dm = dm_ref[...]
        g0 = jax.nn.sigmoid(z0_ref[...] + b0_ref[...])
        g1 = jax.nn.sigmoid(z1_ref[...] + b1_ref[...])
        dpa_ref[...] = (dm * g0).astype(BF16)
        dps_ref[...] = (dm * g1).astype(BF16)
        dz0 = dm * pa_ref[...] * g0 * (1.0 - g0)
        dz1 = dm * ps_ref[...] * g1 * (1.0 - g1)
        dz_ref[:, 0:DM] = dz0.astype(BF16)
        dz_ref[:, DM:2 * DM] = dz1.astype(BF16)
        db_ref[:, 0:DM] += jnp.sum(dz0, axis=0, keepdims=True)
        db_ref[:, DM:2 * DM] += jnp.sum(dz1, axis=0, keepdims=True)

    s = zg.shape[0]
    half = lambda j: pl.BlockSpec((tm, DM), lambda i, j=j: (i, j))
    bvec = lambda j: pl.BlockSpec((1, DM), lambda i, j=j: (0, j))
    return pl.pallas_call(
        body, name=name,
        out_shape=(_sds((s, DM), BF16), _sds((s, DM), BF16), _sds((s, GATE_W), BF16), _sds((1, GATE_W), F32)),
        grid=(s // tm,), in_specs=[half(0), half(0), half(1), bvec(0), bvec(1), half(0), half(0)],
        out_specs=(half(0), half(0), pl.BlockSpec((tm, GATE_W), lambda i: (i, 0)), pl.BlockSpec((1, GATE_W), lambda i: (0, 0))),
        compiler_params=_params(("arbitrary",)))(dm, zg, zg, bias, bias, pa, ps)


def loss_head(y, target, *, name, tm=512):
    def body(y_ref, t_ref, dy_ref, dyb_ref, l_ref):
        @pl.when(pl.program_id(0) == 0)
        def _():
            l_ref[...] = jnp.zeros_like(l_ref)

        err = y_ref[...] - t_ref[...]
        dy = err * (1.0 / DM)
        dy_ref[...] = dy
        dyb_ref[...] = dy.astype(BF16)
        l_ref[...] += 0.5 * jnp.sum(jnp.mean(err * err, axis=-1, keepdims=True), axis=0, keepdims=True)

    s = y.shape[0]
    tile = pl.BlockSpec((tm, DM), lambda i: (i, 0))
    return pl.pallas_call(
        body, name=name, out_shape=(_sds((s, DM), F32), _sds((s, DM), BF16), _sds((1, 128), F32)), grid=(s // tm,),
        in_specs=[tile, tile], out_specs=(tile, tile, pl.BlockSpec((1, 128), lambda i: (0, 0))),
        compiler_params=_params(("arbitrary",)))(y, target)


def adamw_small(ws, gs, ms, vs, *, name):
    cnt = len(ws)

    def body(*refs):
        ins, outs = refs[:4 * cnt], refs[4 * cnt:]
        for i in range(cnt):
            w_ref, g_ref, m_ref, v_ref = ins[4 * i:4 * i + 4]
            d_ref, nm_ref, nv_ref = outs[3 * i:3 * i + 3]
            g = g_ref[...]
            nm = ADAM_B1 * m_ref[...] + (1.0 - ADAM_B1) * g
            nv = ADAM_B2 * v_ref[...] + (1.0 - ADAM_B2) * jnp.square(g)
            m_hat = nm / (1.0 - ADAM_B1 ** ADAM_STEP)
            v_hat = nv / (1.0 - ADAM_B2 ** ADAM_STEP)
            d_ref[...] = -ADAM_LR * (m_hat / (jnp.sqrt(v_hat) + ADAM_EPS) + ADAM_WD * w_ref[...])
            nm_ref[...] = nm
            nv_ref[...] = nv

    flat = [a for i in range(cnt) for a in (ws[i], gs[i], ms[i], vs[i])]
    res = pl.pallas_call(
        body, name=name, out_shape=tuple(_sds(ws[i].shape, F32) for i in range(cnt) for _ in range(3)),
        compiler_params=_params())(*flat)
    return [tuple(res[3 * i:3 * i + 3]) for i in range(cnt)]


def adamw_layer(ws, ms, vs, mines, theirs, cidx, layer, filled=None, *, name):
    cnt = len(ws)
    _, k, n = ws[0].shape
    nt = 2
    tk = k // 2 // nt

    def body(c_ref, *refs):
        own = pl.program_id(0) == c_ref[0]
        outs = refs[-4 * cnt:]
        for i in range(cnt):
            w_ref, m_ref, v_ref, a_ref, b_ref = refs[5 * i:5 * i + 5]
            g_ref, d_ref, nm_ref, nv_ref = outs[4 * i:4 * i + 4]
            g = jnp.where(own, a_ref[...], b_ref[...])
            g_ref[...] = g
            nm = ADAM_B1 * m_ref[...] + (1.0 - ADAM_B1) * g
            nv = ADAM_B2 * v_ref[...] + (1.0 - ADAM_B2) * jnp.square(g)
            m_hat = nm / (1.0 - ADAM_B1 ** ADAM_STEP)
            v_hat = nv / (1.0 - ADAM_B2 ** ADAM_STEP)
            d_ref[...] = -ADAM_LR * (m_hat / (jnp.sqrt(v_hat) + ADAM_EPS) + ADAM_WD * w_ref[...])
            nm_ref[...] = nm
            nv_ref[...] = nv

    full = pl.BlockSpec((None, tk, n), lambda hf, t, c: (layer, hf * nt + t, 0))
    half_mine = pl.BlockSpec((tk, n), lambda hf, t, c: (jnp.where(hf == c[0], t, 0), 0))
    half_theirs = pl.BlockSpec((tk, n), lambda hf, t, c: (jnp.where(hf != c[0], t, 0), 0))
    out = _sds(ws[0].shape, F32)
    ins, specs, aliases = [cidx], [], {}
    for i in range(cnt):
        ins += [ws[i], ms[i], vs[i], mines[i], theirs[i]]
        specs += [full, full, full, half_mine, half_theirs]
    if filled is not None:
        aliases = {len(ins) + j: j for j in range(4 * cnt)}
        ins += [a for f in filled for a in f]
        specs += [pl.BlockSpec(memory_space=pl.ANY)] * (4 * cnt)
    res = pl.pallas_call(
        body, name=name, out_shape=(out,) * (4 * cnt),
        grid_spec=pltpu.PrefetchScalarGridSpec(
            num_scalar_prefetch=1, grid=(2, nt), in_specs=specs, out_specs=(full,) * (4 * cnt)),
        input_output_aliases=aliases,
        compiler_params=_params(("arbitrary", "arbitrary")))(*ins)
    return [tuple(res[4 * i:4 * i + 4]) for i in range(cnt)]


def t5_table_grad(dt5_a, dt5_b, *, name):
    def body(a_ref, b_ref, map_ref, o_ref):
        d = a_ref[...] + b_ref[...]
        bucket = map_ref[...]
        for b in range(32):
            hit = (bucket == b)[None]
            o_ref[b] = jnp.sum(jnp.sum(jnp.where(hit, d, 0.0), axis=2), axis=1, keepdims=True)

    return pl.pallas_call(
        body, name=name, out_shape=_sds((32, 8, 1), F32), compiler_params=_params())(
            dt5_a, dt5_b, jnp.asarray(t5_bucket_map()))


def rpb_grad(dbias, *, name):
    def body(d_ref, rev_ref, o_ref):
        rev = rev_ref[...]
        for h in range(NA_HEADS):
            for pr in range(NA_WR // 2):
                d = d_ref[h, :, 128 * pr:128 * pr + 128]
                hi = d.astype(BF16)
                lo = (d - hi.astype(F32)).astype(BF16)
                flipped = _dot(rev, hi) + _dot(rev, lo)
                o_ref[h, pr] = jnp.sum(pltpu.roll(flipped, 0, 1, stride=1, stride_axis=0), axis=0, keepdims=True)

    anti = jnp.asarray(np.eye(GRID_W, dtype=np.float32)[::-1], dtype=BF16)
    e = pl.pallas_call(
        body, name=name, out_shape=_sds((NA_WR, NA_HEADS, NA_WR // 2, 1, 128), F32), grid=(NA_WR,),
        in_specs=[pl.BlockSpec((None, NA_HEADS, GRID_W, NA_KEYS), lambda p: (p, 0, 0, 0)),
                  pl.BlockSpec((GRID_W, GRID_W), lambda p: (0, 0))],
        out_specs=pl.BlockSpec((None, NA_HEADS, NA_WR // 2, 1, 128), lambda p: (p, 0, 0, 0, 0)),
        compiler_params=_params(("parallel",)))(dbias, anti)
    nci, nri = 2 * NA_WC - 1, 2 * NA_WR - 1
    e = e.reshape(NA_WR, NA_HEADS, NA_WR // 2, 128).transpose(0, 2, 1, 3).reshape(NA_WR * NA_WR // 2, NA_HEADS, 128)
    parts = jnp.concatenate([e[..., 48:48 + nci], jnp.concatenate([e[..., 112:128], e[..., 0:nci - 16]], axis=-1)], axis=0)
    p, pr = np.arange(NA_WR)[:, None], np.arange(NA_WR // 2)[None, :]
    ri = np.concatenate([(2 * pr - p + NA_WR - 1).reshape(-1), (2 * pr - p + NA_WR).reshape(-1)])
    pick = jnp.asarray((ri[None, :] == np.arange(16)[:, None]).astype(np.float32))
    out = mm(pick, parts.reshape(2 * NA_WR * NA_WR // 2, NA_HEADS * nci), name=name + "_rows", exact=True)
    return out.reshape(16, NA_HEADS, nci)[:nri].transpose(1, 0, 2)


BIG = ("ffn1_w_gate", "ffn1_w_up", "ffn1_w_down", "w_in", "w_branch_na", "w_branch_sw", "w_out",
       "ffn2_w_gate", "ffn2_w_up", "ffn2_w_down")
SMALL = ("ffn1_norm", "mix_norm", "b_gate", "na_q_norm", "na_k_norm", "na_rpb", "sw_q_norm", "sw_k_norm", "sw_sink",
         "ffn2_norm")


def _cols_to_full(w4):
    return w4.transpose(1, 0, 2).reshape(w4.shape[1], NSH * w4.shape[2])


def _full_to_cols(w):
    return w.reshape(w.shape[0], NSH, w.shape[1] // NSH).transpose(1, 0, 2)


def _mixer_weights(g):
    w_in_t = g["w_in"].reshape(IN_W, DM)
    return dict(w_in_t=w_in_t, wa=_cols_to_full(g["w_branch_na"]),
                ws=_cols_to_full(g["w_branch_sw"]), wo=g["w_out"].reshape(DM, DM))


GROUPS = {"ffn1": ("ffn1_w_gate", "ffn1_w_up", "ffn1_w_down"), "mix": ("w_in", "w_branch_na", "w_branch_sw", "w_out"),
          "ffn2": ("ffn2_w_gate", "ffn2_w_up", "ffn2_w_down")}


def layer_fwd(x, p, weights, t5b):
    row = lambda v: v.reshape(1, -1)
    stacked = lambda g: {n: a.reshape(DFF, DM) for n, a in g.items()}
    g1 = stacked(weights("ffn1", x))
    y1, h1, gg1, uu1 = ffn_fwd(x, row(p["ffn1_norm"]), g1["ffn1_w_gate"], g1["ffn1_w_up"], g1["ffn1_w_down"], name="ffn_fwd")
    w = _mixer_weights(weights("mix", y1))
    hm = rms_fwd(y1, row(p["mix_norm"]), name="mix_norm_fwd")
    z = mm(hm, w["w_in_t"], tb=True, b_rows=(0, ATT_W), out_dtype=BF16, name="proj_att", tm=SEQ, tn=768)
    zg = mm(hm, w["w_in_t"], tb=True, b_rows=(ATT_W, GATE_W), out_dtype=BF16, name="proj_gate", tm=SEQ, tn=512)
    qa, ka, va, qs, kv = qknorm_fwd(z, p["na_q_norm"], p["na_k_norm"], p["sw_q_norm"], p["sw_k_norm"], name="qknorm_fwd")
    bias = p["na_bias"]
    o_na, lse_na = na_fwd(qa, ka, va, bias, name="na_fwd")
    kvp = jnp.pad(kv, ((SW_BLK, SW_BLK), (0, 0)))
    sink = p["sw_sink"]
    o_sw, lse_sw = sw_fwd(qs, kvp, t5b, sink, name="sw_fwd")
    pa = mm(o_na, w["wa"], out_dtype=BF16, name="branch_na", tm=1024)
    ps = mm(o_sw, w["ws"], out_dtype=BF16, name="branch_sw", tm=1024)
    merged = gate_fwd(zg, row(p["b_gate"]), pa, ps, name="gate_fwd")
    y2 = mm(merged, w["wo"], add=y1, name="out_proj", tm=1024)
    g2 = stacked(weights("ffn2", y2))
    y3, h2, gg2, uu2 = ffn_fwd(y2, row(p["ffn2_norm"]), g2["ffn2_w_gate"], g2["ffn2_w_up"], g2["ffn2_w_down"], name="ffn_fwd")
    saved = dict(x=x, y1=y1, h1=h1, gg1=gg1, uu1=uu1, hm=hm, z=z, zg=zg, qa=qa, ka=ka, va=va, qs=qs, kvp=kvp, bias=bias,
                 o_na=o_na, lse_na=lse_na, o_sw=o_sw, lse_sw=lse_sw, pa=pa, ps=ps, merged=merged, y2=y2, h2=h2, gg2=gg2,
                 uu2=uu2, w=w, sink=sink, g1=g1, g2=g2)
    return y3, saved


def layer_bwd(dy3, dy3_bf, sv, p, t5b, emit, dep=None):
    w, g1, g2 = sv["w"], sv["g1"], sv["g2"]
    row = lambda v: v.reshape(1, -1)
    fold = lambda v: v.reshape(-1, HD).sum(axis=0)
    small = {}
    dy2, _, small["ffn2_norm"], act, dg, du = ffn_bwd_tokens(
        dy3, sv["y2"], row(p["ffn2_norm"]), sv["gg2"], sv["uu2"], g2["ffn2_w_gate"], g2["ffn2_w_up"], g2["ffn2_w_down"],
        name="ffn_bwd_tokens", dep=dep)
    shards = lambda gs: [g.reshape(NSH, FSH, DM) for g in gs]
    token = emit("ffn2", shards(ffn_bwd_weights(sv["h2"], dy3_bf, act, dg, du, name="ffn_bwd_weights")))
    dmerged = mm(dy2, w["wo"], tb=True, name="out_proj_dx", tm=1024, dep=token)
    gw_out = mm(sv["merged"], dy2, ta=True, out_dtype=BF16, name="out_proj_dw").reshape(NSH, DM // NSH, DM)
    dpa, dps, dzg, small["b_gate"] = gate_bwd(dmerged, sv["zg"], row(p["b_gate"]), sv["pa"], sv["ps"], name="gate_bwd")
    gw_na = _full_to_cols(mm(sv["o_na"], dpa, ta=True, out_dtype=BF16, name="branch_dw"))
    gw_sw = _full_to_cols(mm(sv["o_sw"], dps, ta=True, out_dtype=BF16, name="branch_dw"))
    do_na = mm(dpa, w["wa"], tb=True, out_dtype=BF16, tm=SEQ, name="branch_dx")
    do_sw = mm(dps, w["ws"], tb=True, out_dtype=BF16, tm=SEQ, name="branch_dx")
    dqa, dka, dva, dbias = na_bwd(sv["qa"], sv["ka"], sv["va"], sv["o_na"], do_na, sv["lse_na"], sv["bias"], name="na_bwd")
    dqs, dkvp, dt5, dsink = sw_bwd(sv["qs"], sv["kvp"], sv["o_sw"], do_sw, sv["lse_sw"], t5b, sv["sink"], name="sw_bwd")
    dkv = dkvp[SW_BLK:SW_BLK + SEQ]
    dz, dgqa, dgka, dgqs, dgks = qknorm_bwd(sv["z"], dqa, dka, dva, dqs, dkv, p["na_q_norm"], p["na_k_norm"],
                                            p["sw_q_norm"], p["sw_k_norm"], name="qknorm_bwd")
    small["na_q_norm"], small["na_k_norm"], small["sw_q_norm"], small["sw_k_norm"] = fold(dgqa), fold(dgka), fold(dgqs), fold(dgks)
    small["na_rpb"] = rpb_grad(dbias, name="rpb_grad")
    small["sw_sink"] = dsink
    gw_att_t = mm(dz, sv["hm"], ta=True, out_dtype=BF16, tm=768, name="proj_att_dw")
    gw_gz_t = mm(dzg, sv["hm"], ta=True, out_dtype=BF16, tm=1024, name="proj_gate_dw")
    gw_in = jnp.concatenate([gw_att_t, gw_gz_t], axis=0).reshape(NSH, IN_W // NSH, DM)
    token = emit("mix", (gw_in, gw_na, gw_sw, gw_out))
    dh = mm(dz, w["w_in_t"], b_rows=(0, ATT_W), tm=1024, name="proj_att_dx", dep=token)
    dh = mm(dzg, w["w_in_t"], b_rows=(ATT_W, GATE_W), add=dh, tm=1024, name="proj_gate_dx")
    dy1, dy1_bf, small["mix_norm"] = rms_bwd(dh, sv["y1"], row(p["mix_norm"]), dy2, name="mix_norm_bwd")
    dx, dx_bf, small["ffn1_norm"], act, dg, du = ffn_bwd_tokens(
        dy1, sv["x"], row(p["ffn1_norm"]), sv["gg1"], sv["uu1"], g1["ffn1_w_gate"], g1["ffn1_w_up"], g1["ffn1_w_down"],
        name="ffn_bwd_tokens")
    emit("ffn1", shards(ffn_bwd_weights(sv["h1"], dy1_bf, act, dg, du, name="ffn_bwd_weights")))
    return dx, dx_bf, small, dt5


ANY = pl.BlockSpec(memory_space=pl.ANY)


def _place():
    x, y, c = lax.axis_index("x"), lax.axis_index("y"), lax.axis_index("c")
    chips = [(1 - x, y), (x, 1 - y), (1 - x, 1 - y)]
    return x, y, c, chips


def _remote(src, dst, send_sem, recv_sem, to):
    return pltpu.make_async_remote_copy(src_ref=src, dst_ref=dst, send_sem=send_sem, recv_sem=recv_sem, device_id=to,
                                        device_id_type=MESH)


HBM = pl.BlockSpec(memory_space=pltpu.HBM)
SEM = pl.BlockSpec(memory_space=pltpu.SEMAPHORE)
ORDERED_EFFECT = pltpu.SideEffectType.DATAFLOW_SIDE_EFFECTING


def _in_hbm(v):
    return pltpu.with_memory_space_constraint(v, pltpu.HBM)


def _row_half(ref_shape_rows, c):
    half = ref_shape_rows // 2
    return pl.ds(c * half, half)


def _ici_gather_copies(w, land, send_sems, recv_sems):
    x, y, c, chips = _place()
    me = 2 * x + y
    copies = []
    for a in range(len(w)):
        rows = _row_half(w[a].shape[0], c)
        for k, chip in enumerate(chips):
            copies.append(_remote(w[a].at[rows], land[a].at[me, rows], send_sems.at[4 * a + k], recv_sems.at[4 * a + k],
                                  (*chip, c)))
        copies.append(_remote(w[a], land[a].at[me], send_sems.at[4 * a + 3], recv_sems.at[4 * a + 3], (x, y, 1 - c)))
    return copies


def _d2d_gather_copies(w, land, send_sems, recv_sems):
    x, y, c, chips = _place()
    copies = []
    for a in range(len(w)):
        rows = _row_half(w[a].shape[0], c)
        for k, (cx, cy) in enumerate(chips):
            blk = land[a].at[2 * cx + cy, rows]
            copies.append(_remote(blk, blk, send_sems.at[3 * a + k], recv_sems.at[3 * a + k], (x, y, 1 - c)))
    return copies


def _d2d_gather_waits(w, land, send_sems, recv_sems):
    x, y, c, chips = _place()
    waits = []
    for a in range(len(w)):
        rows = _row_half(w[a].shape[0], 1 - c)
        for k, (cx, cy) in enumerate(chips):
            blk = land[a].at[2 * cx + cy, rows]
            waits.append(_remote(blk, blk, send_sems.at[3 * a + k], recv_sems.at[3 * a + k], (x, y, 1 - c)))
    return waits


def gather_start(groups, dep=None, *, name):
    sizes = [len(g) for g in groups]
    shards = [s for g in groups for s in g]
    n, ng = len(shards), len(groups)
    extra = [] if dep is None else [dep]

    def body(*refs):
        first_out = 2 * n + len(extra)
        w, land, sems = refs[:n], refs[n:2 * n], refs[first_out:first_out + 2 * ng]
        off = 0
        for gi, size in enumerate(sizes):
            for cp in _ici_gather_copies(w[off:off + size], land[off:off + size], sems[2 * gi], sems[2 * gi + 1]):
                cp.start()
            off += size

    lands = [lax.empty((NSH,) + s.shape, s.dtype) for s in shards]
    sem_shapes = tuple(pltpu.SemaphoreType.DMA((4 * size,)) for size in sizes for _ in range(2))
    res = pl.pallas_call(
        body, name=name,
        out_shape=sem_shapes + tuple(pltpu.HBM(s.shape, s.dtype) for s in shards) + tuple(pltpu.HBM(l.shape, l.dtype) for l in lands),
        in_specs=[HBM] * (2 * n) + [ANY] * len(extra), out_specs=(SEM,) * (2 * ng) + (HBM,) * (2 * n),
        input_output_aliases={i: 2 * ng + i for i in range(2 * n)},
        compiler_params=pltpu.CompilerParams(has_side_effects=ORDERED_EFFECT))(
            *[_in_hbm(s) for s in shards], *[_in_hbm(l) for l in lands], *extra)
    out, off = [], 0
    for gi, size in enumerate(sizes):
        out.append((res[2 * gi], res[2 * gi + 1], list(res[2 * ng + off:2 * ng + off + size]),
                    list(res[2 * ng + n + off:2 * ng + n + off + size])))
        off += size
    return out


def gather_wait(send_sems, recv_sems, shards, lands, after, *, name):
    n = len(shards)

    def body(*refs):
        w, land = refs[:n], refs[n:2 * n]
        send, recv = refs[2 * n:2 * n + 2]
        for cp in _ici_gather_copies(w, land, send, recv):
            cp.wait_send()
            cp.wait_recv()

    res = pl.pallas_call(
        body, name=name,
        out_shape=tuple(pltpu.HBM(s.shape, s.dtype) for s in shards) + tuple(pltpu.HBM(l.shape, l.dtype) for l in lands),
        in_specs=[HBM] * (2 * n) + [SEM, SEM] + [ANY] * len(after), out_specs=(HBM,) * (2 * n),
        input_output_aliases={i: i for i in range(2 * n)},
        compiler_params=pltpu.CompilerParams(has_side_effects=ORDERED_EFFECT))(*shards, *lands, send_sems, recv_sems, *after)
    return list(res[:n]), list(res[n:])


def gather_finish(shards, lands, *, name):
    n = len(shards)

    def body(*refs):
        w, land = refs[:n], refs[n:2 * n]
        send_sems, recv_sems = refs[3 * n:]
        d2d = _d2d_gather_copies(w, land, send_sems, recv_sems)
        for cp in d2d:
            cp.start()
        for cp in _d2d_gather_waits(w, land, send_sems, recv_sems):
            cp.wait_recv()
        for cp in d2d:
            cp.wait_send()

    return list(pl.pallas_call(
        body, name=name, out_shape=tuple(pltpu.HBM(l.shape, l.dtype) for l in lands),
        in_specs=[ANY] * (2 * n), out_specs=tuple([ANY] * n), input_output_aliases={n + i: i for i in range(n)},
        scratch_shapes=[pltpu.SemaphoreType.DMA((3 * n,)), pltpu.SemaphoreType.DMA((3 * n,))])(*shards, *lands))


def _pair_exchange_copies(g, buf, send_sems, recv_sems):
    x, y, c, _ = _place()
    copies = []
    for a in range(len(g)):
        half = g[a].shape[1] // 2
        copies.append(_remote(g[a].at[:, pl.ds((1 - c) * half, half)], buf[a], send_sems.at[a], recv_sems.at[a], (x, y, 1 - c)))
    return copies


def pair_exchange_start(grads, dep=None, *, name):
    n = len(grads)
    extra = [] if dep is None else [dep]

    def body(*refs):
        sems = refs[2 * n + len(extra):]
        for cp in _pair_exchange_copies(refs[:n], refs[n:2 * n], sems[0], sems[1]):
            cp.start()
        refs[-1][...] = jnp.zeros_like(refs[-1])

    lands = [lax.empty((NSH, g.shape[1] // 2, g.shape[2]), g.dtype) for g in grads]
    res = pl.pallas_call(
        body, name=name,
        out_shape=(pltpu.SemaphoreType.DMA((n,)), pltpu.SemaphoreType.DMA((n,)))
        + tuple(pltpu.HBM(g.shape, g.dtype) for g in grads) + tuple(pltpu.HBM(l.shape, l.dtype) for l in lands)
        + (_sds((8, 128), F32),),
        in_specs=[HBM] * (2 * n) + [ANY] * len(extra),
        out_specs=(SEM, SEM) + (HBM,) * (2 * n) + (pl.BlockSpec(memory_space=pltpu.VMEM),),
        input_output_aliases={i: 2 + i for i in range(2 * n)},
        compiler_params=pltpu.CompilerParams(has_side_effects=ORDERED_EFFECT))(
            *[_in_hbm(g) for g in grads], *[_in_hbm(l) for l in lands], *extra)
    return res[0], res[1], list(res[2:2 + n]), list(res[2 + n:2 + 2 * n]), res[-1]


def pair_exchange_wait(send_sems, recv_sems, grads, lands, after, *, name):
    n = len(grads)

    def body(*refs):
        for cp in _pair_exchange_copies(refs[:n], refs[n:2 * n], refs[2 * n], refs[2 * n + 1]):
            cp.wait_send()
            cp.wait_recv()

    res = pl.pallas_call(
        body, name=name,
        out_shape=tuple(pltpu.HBM(g.shape, g.dtype) for g in grads) + tuple(pltpu.HBM(l.shape, l.dtype) for l in lands),
        in_specs=[HBM] * (2 * n) + [SEM, SEM] + [ANY] * len(after), out_specs=(HBM,) * (2 * n),
        input_output_aliases={i: i for i in range(2 * n)},
        compiler_params=pltpu.CompilerParams(has_side_effects=ORDERED_EFFECT))(*grads, *lands, send_sems, recv_sems, *after)
    return list(res[:n]), list(res[n:])


def _chip_exchange_copies(s, buf, send_sems, recv_sems):
    x, y, c, chips = _place()
    return [_remote(s[a].at[2 * cx + cy], buf[a].at[k], send_sems.at[3 * a + k], recv_sems.at[3 * a + k], (cx, cy, c))
            for a in range(len(s)) for k, (cx, cy) in enumerate(chips)]


def chip_exchange(sums, *, name):
    n = len(sums)

    def body(*refs):
        copies = _chip_exchange_copies(refs[:n], refs[n:2 * n], *refs[2 * n:])
        for cp in copies:
            cp.start()
        for cp in copies:
            cp.wait()

    return pl.pallas_call(
        body, name=name, out_shape=tuple(pltpu.HBM((3,) + s.shape[1:], s.dtype) for s in sums),
        in_specs=[ANY] * n, out_specs=tuple([ANY] * n),
        scratch_shapes=[pltpu.SemaphoreType.DMA((3 * n,)), pltpu.SemaphoreType.DMA((3 * n,))])(*sums)


def chip_exchange_start(sums, *, name):
    n = len(sums)

    def body(*refs):
        for cp in _chip_exchange_copies(refs[:n], refs[n:2 * n], refs[2 * n], refs[2 * n + 1]):
            cp.start()
        refs[-1][...] = jnp.zeros_like(refs[-1])

    lands = [lax.empty((3,) + s.shape[1:], s.dtype) for s in sums]
    res = pl.pallas_call(
        body, name=name,
        out_shape=(pltpu.SemaphoreType.DMA((3 * n,)), pltpu.SemaphoreType.DMA((3 * n,)))
        + tuple(pltpu.HBM(s.shape, s.dtype) for s in sums) + tuple(pltpu.HBM(l.shape, l.dtype) for l in lands)
        + (_sds((8, 128), F32),),
        in_specs=[HBM] * (2 * n), out_specs=(SEM, SEM) + (HBM,) * (2 * n) + (pl.BlockSpec(memory_space=pltpu.VMEM),),
        input_output_aliases={i: 2 + i for i in range(2 * n)},
        compiler_params=pltpu.CompilerParams(has_side_effects=ORDERED_EFFECT))(
            *[_in_hbm(s) for s in sums], *[_in_hbm(l) for l in lands])
    return res[0], res[1], list(res[2:2 + n]), list(res[2 + n:2 + 2 * n]), res[-1]


def chip_exchange_wait(send_sems, recv_sems, sums, lands, after, *, name):
    n = len(sums)

    def body(*refs):
        for cp in _chip_exchange_copies(refs[:n], refs[n:2 * n], refs[2 * n], refs[2 * n + 1]):
            cp.wait_send()
            cp.wait_recv()

    res = pl.pallas_call(
        body, name=name,
        out_shape=tuple(pltpu.HBM(s.shape, s.dtype) for s in sums) + tuple(pltpu.HBM(l.shape, l.dtype) for l in lands),
        in_specs=[HBM] * (2 * n) + [SEM, SEM] + [ANY] * len(after), out_specs=(HBM,) * (2 * n),
        input_output_aliases={i: i for i in range(2 * n)},
        compiler_params=pltpu.CompilerParams(has_side_effects=ORDERED_EFFECT))(*sums, *lands, send_sems, recv_sems, *after)
    return list(res[:n]), list(res[n:])


def _pair_send_copies(h, got, send_sems, recv_sems):
    x, y, c, _ = _place()
    return [_remote(h[i], got[i], send_sems.at[i], recv_sems.at[i], (x, y, 1 - c)) for i in range(len(h))]


def pair_send_start(halves, *, name):
    n = len(halves)

    def body(*refs):
        for cp in _pair_send_copies(refs[:n], refs[n:2 * n], refs[2 * n], refs[2 * n + 1]):
            cp.start()
        refs[-1][...] = jnp.zeros_like(refs[-1])

    lands = [lax.empty(h.shape, h.dtype) for h in halves]
    res = pl.pallas_call(
        body, name=name,
        out_shape=(pltpu.SemaphoreType.DMA((n,)), pltpu.SemaphoreType.DMA((n,)))
        + tuple(pltpu.HBM(h.shape, h.dtype) for h in halves) * 2 + (_sds((8, 128), F32),),
        in_specs=[HBM] * (2 * n), out_specs=(SEM, SEM) + (HBM,) * (2 * n) + (pl.BlockSpec(memory_space=pltpu.VMEM),),
        input_output_aliases={i: 2 + i for i in range(2 * n)},
        compiler_params=pltpu.CompilerParams(has_side_effects=ORDERED_EFFECT))(
            *[_in_hbm(h) for h in halves], *[_in_hbm(l) for l in lands])
    return res[0], res[1], list(res[2:2 + n]), list(res[2 + n:2 + 2 * n]), res[-1]


def pair_send_wait(send_sems, recv_sems, halves, lands, after, *, name):
    n = len(halves)

    def body(*refs):
        for cp in _pair_send_copies(refs[:n], refs[n:2 * n], refs[2 * n], refs[2 * n + 1]):
            cp.wait_send()
            cp.wait_recv()

    res = pl.pallas_call(
        body, name=name, out_shape=tuple(pltpu.HBM(h.shape, h.dtype) for h in halves) * 2,
        in_specs=[HBM] * (2 * n) + [SEM, SEM] + [ANY] * len(after), out_specs=(HBM,) * (2 * n),
        input_output_aliases={i: i for i in range(2 * n)},
        compiler_params=pltpu.CompilerParams(has_side_effects=ORDERED_EFFECT))(*halves, *lands, send_sems, recv_sems, *after)
    return list(res[:n]), list(res[n:])


def allreduce_small(v, *, name):
    rows = v.shape[0]

    def body(v_ref, o_ref, gath, send_sems, recv_sems):
        x, y, c, _ = _place()
        me = 4 * x + 2 * y + c
        gath[me] = v_ref[...]
        copies = []
        for k in range(1, 8):
            fx, fy, fc = (k >> 2) & 1, (k >> 1) & 1, k & 1
            peer = (jnp.where(fx, 1 - x, x), jnp.where(fy, 1 - y, y), jnp.where(fc, 1 - c, c))
            cp = _remote(v_ref, gath.at[me], send_sems.at[k - 1], recv_sems.at[k - 1], peer)
            cp.start()
            copies.append(cp)
        for cp in copies:
            cp.wait()
        acc = gath[0]
        for d in range(1, 8):
            acc = acc + gath[d]
        o_ref[...] = acc

    return pl.pallas_call(
        body, name=name, out_shape=_sds(v.shape, F32),
        in_specs=[pl.BlockSpec(memory_space=pltpu.VMEM)], out_specs=pl.BlockSpec(memory_space=pltpu.VMEM),
        scratch_shapes=[pltpu.VMEM((8, rows, 128), F32), pltpu.SemaphoreType.DMA((7,)), pltpu.SemaphoreType.DMA((7,))])(v)


def _same_shape_runs(arrays):
    runs = {}
    for i, a in enumerate(arrays):
        runs.setdefault(a.shape, []).append(i)
    return list(runs.values())


def _per_shape(fn, *lists):
    out = [None] * len(lists[0])
    for idx in _same_shape_runs(lists[0]):
        for i, r in zip(idx, fn(*[[l[i] for i in idx] for l in lists])):
            out[i] = r
    return out


def add_halves(gs, bufs, cidx, *, name):
    cnt = len(gs)
    _, k, n = gs[0].shape

    def body(c_ref, *refs):
        g, b, o = refs[:cnt], refs[cnt:2 * cnt], refs[2 * cnt:]
        for i in range(cnt):
            o[i][...] = (g[i][...].astype(F32) + b[i][...].astype(F32)).astype(BF16)

    blk = pl.BlockSpec((None, k // 2, n), lambda s, c: (s, 0, 0))
    mine = pl.BlockSpec((None, k // 2, n), lambda s, c: (s, c[0], 0))
    return list(pl.pallas_call(
        body, name=name, out_shape=tuple(_sds(b.shape, BF16) for b in bufs),
        grid_spec=pltpu.PrefetchScalarGridSpec(
            num_scalar_prefetch=1, grid=(NSH,), in_specs=[mine] * cnt + [blk] * cnt, out_specs=tuple([blk] * cnt)),
        compiler_params=_params(("parallel",)))(cidx, *gs, *bufs))


def add_chips(sums, bufs, sidx, *, name):
    cnt = len(sums)
    _, kh, n = sums[0].shape

    def body(s_ref, *refs):
        mine, b, o = refs[:cnt], refs[cnt:2 * cnt], refs[2 * cnt:]
        for i in range(cnt):
            o[i][...] = ((mine[i][...].astype(F32) + b[i][0].astype(F32)) + (b[i][1].astype(F32) + b[i][2].astype(F32)))

    own = pl.BlockSpec((None, kh, n), lambda i, s: (s[0], 0, 0))
    got = pl.BlockSpec((3, kh, n), lambda i, s: (0, 0, 0))
    out = pl.BlockSpec((kh, n), lambda i, s: (0, 0))
    return list(pl.pallas_call(
        body, name=name, out_shape=tuple(_sds((kh, n), F32) for _ in sums),
        grid_spec=pltpu.PrefetchScalarGridSpec(
            num_scalar_prefetch=1, grid=(1,), in_specs=[own] * cnt + [got] * cnt, out_specs=tuple([out] * cnt)),
        compiler_params=_params(("arbitrary",)))(sidx, *sums, *bufs))


PARAMS = ("ffn1_norm", "ffn1_w_gate", "ffn1_w_up", "ffn1_w_down", "mix_norm", "w_in", "b_gate", "na_q_norm", "na_k_norm",
          "na_rpb", "sw_q_norm", "sw_k_norm", "sw_sink", "t5_rel_table", "w_branch_na", "w_branch_sw", "w_out", "ffn2_norm",
          "ffn2_w_gate", "ffn2_w_up", "ffn2_w_down")
SMALL_ALL = tuple(n for n in PARAMS if n not in BIG)
TRANSPOSED = ("ffn1_w_gate", "ffn1_w_up", "w_in", "ffn2_w_gate", "ffn2_w_up")
SMALL_ROWS = 152


def _pack_small(vals):
    flat = jnp.concatenate([vals[n].reshape(-1).astype(F32) for n in SMALL_ALL] + [vals["loss"].reshape(-1)])
    return jnp.pad(flat, (0, SMALL_ROWS * 128 - flat.shape[0])).reshape(SMALL_ROWS, 128)


def _unpack_small(packed, like):
    flat, out, off = packed.reshape(-1), {}, 0
    for n in SMALL_ALL:
        size = math.prod(like[n].shape)
        out[n] = flat[off:off + size].reshape(like[n].shape)
        off += size
    out["loss"] = flat[off]
    return out


def kernel(x, ffn1_norm, ffn1_w_gate, ffn1_w_up, ffn1_w_down, mix_norm, w_in, b_gate, na_q_norm, na_k_norm, na_rpb, sw_q_norm, sw_k_norm, sw_sink, t5_rel_table, w_branch_na, w_branch_sw, w_out, ffn2_norm, ffn2_w_gate, ffn2_w_up, ffn2_w_down, loss_target, m_ffn1_norm, m_ffn1_w_gate, m_ffn1_w_up, m_ffn1_w_down, m_mix_norm, m_w_in, m_b_gate, m_na_q_norm, m_na_k_norm, m_na_rpb, m_sw_q_norm, m_sw_k_norm, m_sw_sink, m_t5_rel_table, m_w_branch_na, m_w_branch_sw, m_w_out, m_ffn2_norm, m_ffn2_w_gate, m_ffn2_w_up, m_ffn2_w_down, v_ffn1_norm, v_ffn1_w_gate, v_ffn1_w_up, v_ffn1_w_down, v_mix_norm, v_w_in, v_b_gate, v_na_q_norm, v_na_k_norm, v_na_rpb, v_sw_q_norm, v_sw_k_norm, v_sw_sink, v_t5_rel_table, v_w_branch_na, v_w_branch_sw, v_w_out, v_ffn2_norm, v_ffn2_w_gate, v_ffn2_w_up, v_ffn2_w_down):
    args = locals()
    tr = lambda n, a: jnp.transpose(a, (0, 2, 1)) if n in TRANSPOSED else a
    w = {n: tr(n, args[n]) for n in PARAMS}
    m = {n: tr(n, args["m_" + n]) for n in PARAMS}
    v = {n: tr(n, args["v_" + n]) for n in PARAMS}
    cidx = lax.axis_index("c").astype(jnp.int32).reshape(1)
    sidx = (2 * lax.axis_index("x") + lax.axis_index("y")).astype(jnp.int32).reshape(1)

    small = [{n: w[n][l] for n in SMALL} for l in range(DEPTH)]
    order = ("ffn1", "mix", "ffn2")

    keys = [(l, g) for l in range(DEPTH) for g in order]
    local = lambda l, g: [w[n][l].astype(BF16) for n in GROUPS[g]]
    first = gather_start([local(*keys[0])], name="gather_start")
    rest = gather_start([local(*key) for key in keys[1:]], first[0][2][0], name="gather_start")
    in_flight = dict(zip(keys, first + rest))
    t5b = t5_bias(w["t5_rel_table"], name="t5_bias")
    for l in range(DEPTH):
        small[l]["na_bias"] = na_bias_table(small[l]["na_rpb"], name="na_bias_table")
    early = [t5b] + [small[l]["na_bias"] for l in range(DEPTH)] + [rest[0][2][0]]

    def weights_of(l):
        def get(group, after):
            send_sems, recv_sems, thru, lands = in_flight[(l, group)]
            after = [after] + (early if (l, group) == keys[0] else [])
            thru, lands = gather_wait(send_sems, recv_sems, thru, lands, after, name="gather_wait")
            return dict(zip(GROUPS[group], gather_finish(thru, lands, name="gather_finish")))
        return get

    h0, saved0 = layer_fwd(x[0], small[0], weights_of(0), t5b)
    h1, saved1 = layer_fwd(h0, small[1], weights_of(1), t5b)
    dy, dy_bf, loss_row = loss_head(h1, loss_target[0], name="loss_head")

    crossing, tokens, pending = {}, [], []

    def ship(after):
        key, send_sems, recv_sems, grads, lands = pending.pop()
        grads, from_sibling = pair_exchange_wait(send_sems, recv_sems, grads, lands, after, name="pair_exchange_wait")
        sums = _per_shape(lambda gs, bs: add_halves(gs, bs, cidx, name="add_halves"), grads, from_sibling)
        send_sems, recv_sems, sums, lands, token = chip_exchange_start(sums, name="chip_exchange_start")
        crossing[key] = (send_sems, recv_sems, sums, lands)
        return token

    def reduce_of(l):
        def emit(group, grads):
            grads = list(grads)
            shipped = ship([grads[0]]) if pending else None
            send_sems, recv_sems, grads, lands, token = pair_exchange_start(grads, shipped, name="pair_exchange_start")
            pending.append(((l, group), send_sems, recv_sems, grads, lands))
            tokens.append(token)
            return token
        return emit

    def finish(layer, after, filled=None):
        sent = {}
        for group in order:
            send_sems, recv_sems, sums, lands = crossing[(layer, group)]
            sums, got = chip_exchange_wait(send_sems, recv_sems, sums, lands, after, name="chip_exchange_wait")
            halves = _per_shape(lambda ss, bs: add_chips(ss, bs, sidx, name="add_chips"), sums, got)
            sent[group] = pair_send_start(halves, name="pair_send_start")
            after = [sent[group][4]]
        out = {}
        for group in order:
            send_sems, recv_sems, halves, lands, _ = sent[group]
            halves, theirs = pair_send_wait(send_sems, recv_sems, halves, lands, after, name="pair_send_wait")
            names = GROUPS[group]
            res = _per_shape(
                lambda ws, ms, vs, a, b, *f: adamw_layer(ws, ms, vs, a, b, cidx, layer, list(f[0]) if f else None, name="adamw_layer"),
                *([[w[n] for n in names], [m[n] for n in names], [v[n] for n in names], halves, theirs]
                  + ([[filled[n] for n in names]] if filled is not None else [])))
            out.update(zip(names, res))
            after = [res[-1][0]]
        return out

    dy, dy_bf, small1, dt5_1 = layer_bwd(dy, dy_bf, saved1, small[1], t5b, reduce_of(1))
    grad_x, _, small0, dt5_0 = layer_bwd(dy, dy_bf, saved0, small[0], t5b, reduce_of(0), dep=tokens[-1])
    done1 = finish(1, [ship([grad_x])])

    smalls = [small0, small1]
    dt5 = t5_table_grad(dt5_0, dt5_1, name="t5_table_grad").reshape(32, 8)
    local_small = {n: jnp.stack([smalls[l][n].reshape(w[n].shape[1:]) for l in range(DEPTH)]) for n in SMALL}
    local_small["t5_rel_table"] = dt5
    local_small["loss"] = loss_row[0, 0:1]
    total = allreduce_small(_pack_small(local_small), name="allreduce_small")
    small_grads = _unpack_small(total, w)
    small_done = adamw_small([w[n] for n in SMALL_ALL], [small_grads[n] for n in SMALL_ALL], [m[n] for n in SMALL_ALL],
                             [v[n] for n in SMALL_ALL], name="adamw_small")

    grad, delta, new_m, new_v = {}, {}, {}, {}
    for n, done in finish(0, [small_done[0][0], done1[BIG[-1]][0]], filled=done1).items():
        grad[n], delta[n], new_m[n], new_v[n] = done
    for n, done in zip(SMALL_ALL, small_done):
        grad[n] = small_grads[n]
        delta[n], new_m[n], new_v[n] = done

    return (small_grads["loss"], grad_x[None], *[tr(n, grad[n]) for n in PARAMS], *[tr(n, delta[n]) for n in PARAMS],
            *[tr(n, new_m[n]) for n in PARAMS], *[tr(n, new_v[n]) for n in PARAMS])
```

```python
import math

import jax
import jax.numpy as jnp
import numpy as np
from jax import lax
from jax.experimental import pallas as pl
from jax.experimental.pallas import tpu as pltpu

F32 = jnp.float32
BF16 = jnp.bfloat16

SEQ = 2048
DM = 1024
DFF = 2816
DEPTH = 2
NSH = 4
FSH = DFF // NSH
GRID_W = 64
ROWS = SEQ // GRID_W
NA_HEADS = 8
HD = 64
NA_WR = 8
NA_WC = 16
NA_KEYS = NA_WR * GRID_W
SW_BLK = 128
SW_NB = SEQ // SW_BLK
SW_KEYS = 3 * SW_BLK
ATT_W = 2304
GATE_W = 2048
IN_W = ATT_W + GATE_W
EPS = 1e-6
NEG = -1e30
QK_SCALE = 1.0 / math.sqrt(HD)

ADAM_LR = 0.001
ADAM_B1 = 0.9
ADAM_B2 = 0.999
ADAM_EPS = 1e-08
ADAM_WD = 0.01
ADAM_STEP = 10

VMEM_LIMIT = 56 << 20
MESH = pl.DeviceIdType.MESH

NT = (((1,), (1,)), ((), ()))
TN = (((0,), (0,)), ((), ()))
NN = (((1,), (0,)), ((), ()))


def _dot(a, b, dims=NN):
    return lax.dot_general(a, b, dims, preferred_element_type=F32)


def _params(sem=None):
    return pltpu.CompilerParams(dimension_semantics=sem, vmem_limit_bytes=VMEM_LIMIT)


def _sds(shape, dtype):
    return jax.ShapeDtypeStruct(shape, dtype)


def mm(a, b, *, name, ta=False, tb=False, out_dtype=F32, add=None, scale=None, tm=512, tn=None, tk=None, exact=False,
       dep=None, b_rows=None):
    m, kd = (a.shape[1], a.shape[0]) if ta else a.shape
    if b_rows is None:
        n = b.shape[0] if tb else b.shape[1]
    else:
        n = b_rows[1] if tb else b.shape[1]
        assert tb or (b_rows[1] == kd and (tk or kd) == kd)
    tm, tn, tk = min(tm, m), min(tn or n, n), min(tk or kd, kd)
    nk = kd // tk
    dims = (((0 if ta else 1,), (1 if tb else 0,)), ((), ()))

    def body(*refs):
        a_ref, b_ref = refs[:2]
        add_ref = refs[2] if add is not None else None
        o_ref = refs[-1] if nk == 1 else refs[-2]
        if b_rows is None:
            bv = b_ref[...]
        elif tb:
            bv = b_ref[pl.ds(pl.multiple_of(b_rows[0] + pl.program_id(1) * tn, 16), tn), :]
        else:
            bv = b_ref[b_rows[0]:b_rows[0] + b_rows[1], :]
        if exact:
            part = lax.dot_general(a_ref[...], bv, dims, precision=lax.Precision.HIGHEST, preferred_element_type=F32)
        else:
            part = lax.dot_general(a_ref[...].astype(BF16), bv.astype(BF16), dims, preferred_element_type=F32)

        def finish(r):
            if scale is not None:
                r = r * scale
            if add is not None:
                r = r + add_ref[...]
            o_ref[...] = r.astype(out_dtype)

        if nk == 1:
            finish(part)
        else:
            acc, k = refs[-1], pl.program_id(2)

            @pl.when(k == 0)
            def _():
                acc[...] = part

            @pl.when(k != 0)
            def _():
                acc[...] += part

            pl.when(k == nk - 1)(lambda: finish(acc[...]))

    a_spec = pl.BlockSpec((tk, tm), lambda i, j, k: (k, i)) if ta else pl.BlockSpec((tm, tk), lambda i, j, k: (i, k))
    if b_rows is not None:
        b_spec = pl.BlockSpec(b.shape, lambda i, j, k: (0, 0), pipeline_mode=pl.Buffered(1))
    else:
        b_spec = pl.BlockSpec((tn, tk), lambda i, j, k: (j, k)) if tb else pl.BlockSpec((tk, tn), lambda i, j, k: (k, j))
    o_spec = pl.BlockSpec((tm, tn), lambda i, j, k: (i, j))
    ins, specs = [a, b], [a_spec, b_spec]
    if add is not None:
        ins.append(add)
        specs.append(o_spec)
    if dep is not None:
        ins.append(dep)
        specs.append(pl.BlockSpec(memory_space=pl.ANY))
    return pl.pallas_call(
        body, name=name, out_shape=_sds((m, n), out_dtype), grid=(m // tm, n // tn, nk), in_specs=specs,
        out_specs=o_spec, scratch_shapes=[] if nk == 1 else [pltpu.VMEM((tm, tn), F32)],
        compiler_params=_params(("parallel", "parallel", "arbitrary")))(*ins)


def _rms(x):
    return lax.rsqrt(jnp.mean(x * x, axis=-1, keepdims=True) + EPS)


def rms_fwd(x, gain, *, name, tm=512):
    def body(x_ref, g_ref, h_ref):
        x = x_ref[...]
        h_ref[...] = (x * _rms(x) * g_ref[...]).astype(BF16)

    return pl.pallas_call(
        body, name=name, out_shape=_sds(x.shape, BF16), grid=(x.shape[0] // tm,),
        in_specs=[pl.BlockSpec((tm, DM), lambda i: (i, 0)), pl.BlockSpec((1, DM), lambda i: (0, 0))],
        out_specs=pl.BlockSpec((tm, DM), lambda i: (i, 0)), compiler_params=_params(("parallel",)))(x, gain)


def _rms_bwd_math(dh, x, gain):
    r = _rms(x)
    xh = x * r
    dgain = jnp.sum(dh * xh, axis=0, keepdims=True)
    dxn = dh * gain
    dx = r * (dxn - xh * jnp.mean(dxn * xh, axis=-1, keepdims=True))
    return dx, dgain


def rms_bwd(dh, x, gain, dres, *, name, tm=512):
    def body(dh_ref, x_ref, g_ref, dres_ref, dx_ref, dxb_ref, dg_ref):
        @pl.when(pl.program_id(0) == 0)
        def _():
            dg_ref[...] = jnp.zeros_like(dg_ref)

        dx, dg = _rms_bwd_math(dh_ref[...], x_ref[...], g_ref[...])
        dx = dres_ref[...] + dx
        dx_ref[...] = dx
        dxb_ref[...] = dx.astype(BF16)
        dg_ref[...] += dg

    tile = pl.BlockSpec((tm, DM), lambda i: (i, 0))
    vec = pl.BlockSpec((1, DM), lambda i: (0, 0))
    return pl.pallas_call(
        body, name=name, out_shape=(_sds(x.shape, F32), _sds(x.shape, BF16), _sds((1, DM), F32)), grid=(x.shape[0] // tm,),
        in_specs=[tile, tile, vec, tile], out_specs=(tile, tile, vec), compiler_params=_params(("arbitrary",)))(dh, x, gain, dres)


def _with_dep(ins, specs, dep):
    if dep is None:
        return ins, specs
    return ins + [dep], specs + [pl.BlockSpec(memory_space=pl.ANY)]


def _resident_weight():
    return pl.BlockSpec((DFF, DM), lambda i: (0, 0), pipeline_mode=pl.Buffered(1))


def ffn_fwd(x, gain, wg, wu, wd, *, name, tm=512):
    def body(x_ref, g_ref, wg_ref, wu_ref, wd_ref, y_ref, h_ref, gg_ref, uu_ref):
        x = x_ref[...]
        h = (x * _rms(x) * g_ref[...]).astype(BF16)
        h_ref[...] = h
        gg = _dot(h, wg_ref[...], NT)
        uu = _dot(h, wu_ref[...], NT)
        gg_ref[...] = gg.astype(BF16)
        uu_ref[...] = uu.astype(BF16)
        act = (gg * jax.nn.sigmoid(gg) * uu).astype(BF16)
        y_ref[...] = x + 0.5 * _dot(act, wd_ref[...])

    s = x.shape[0]
    tile = pl.BlockSpec((tm, DM), lambda i: (i, 0))
    hid = pl.BlockSpec((tm, DFF), lambda i: (i, 0))
    w = _resident_weight()
    return pl.pallas_call(
        body, name=name,
        out_shape=(_sds((s, DM), F32), _sds((s, DM), BF16), _sds((s, DFF), BF16), _sds((s, DFF), BF16)),
        grid=(s // tm,), in_specs=[tile, pl.BlockSpec((1, DM), lambda i: (0, 0)), w, w, w],
        out_specs=(tile, tile, hid, hid), compiler_params=_params(("parallel",)))(x, gain, wg, wu, wd)


def ffn_bwd_tokens(dy, x, gain, gg, uu, wg, wu, wd, *, name, tm=256, dep=None):
    def body(dy_ref, x_ref, g_ref, gg_ref, uu_ref, wg_ref, wu_ref, wd_ref, *rest):
        dx_ref, dxb_ref, dgain_ref, act_ref, dg_ref, du_ref = rest[-6:]

        @pl.when(pl.program_id(0) == 0)
        def _():
            dgain_ref[...] = jnp.zeros_like(dgain_ref)

        dy = dy_ref[...]
        dact = _dot((0.5 * dy).astype(BF16), wd_ref[...], NT)
        g = gg_ref[...].astype(F32)
        u = uu_ref[...].astype(F32)
        sg = jax.nn.sigmoid(g)
        silu = g * sg
        act_ref[...] = (silu * u).astype(BF16)
        dg = (dact * u * (sg * (1.0 + g * (1.0 - sg)))).astype(BF16)
        du = (dact * silu).astype(BF16)
        dg_ref[...] = dg
        du_ref[...] = du
        dx, dgain = _rms_bwd_math(_dot(dg, wg_ref[...]) + _dot(du, wu_ref[...]), x_ref[...], g_ref[...])
        dx = dy + dx
        dx_ref[...] = dx
        dxb_ref[...] = dx.astype(BF16)
        dgain_ref[...] += dgain

    s = x.shape[0]
    tile = pl.BlockSpec((tm, DM), lambda i: (i, 0))
    vec = pl.BlockSpec((1, DM), lambda i: (0, 0))
    hid = pl.BlockSpec((tm, DFF), lambda i: (i, 0))
    hshape = _sds((s, DFF), BF16)
    w = _resident_weight()
    ins, specs = _with_dep([dy, x, gain, gg, uu, wg, wu, wd], [tile, tile, vec, hid, hid, w, w, w], dep)
    return pl.pallas_call(
        body, name=name, out_shape=(_sds((s, DM), F32), _sds((s, DM), BF16), _sds((1, DM), F32), hshape, hshape, hshape),
        grid=(s // tm,), in_specs=specs, out_specs=(tile, tile, vec, hid, hid, hid),
        compiler_params=_params(("arbitrary",)))(*ins)


def ffn_bwd_weights(h, dy, act, dg, du, *, name, tf=256):
    def body(h_ref, dy_ref, act_ref, dg_ref, du_ref, gwg_ref, gwu_ref, gwd_ref):
        h = h_ref[...]
        gwg_ref[...] = _dot(dg_ref[...], h, TN).astype(BF16)
        gwu_ref[...] = _dot(du_ref[...], h, TN).astype(BF16)
        gwd_ref[...] = (0.5 * _dot(act_ref[...], dy_ref[...], TN)).astype(BF16)

    s = h.shape[0]
    full = pl.BlockSpec((s, DM), lambda f: (0, 0))
    hid = pl.BlockSpec((s, tf), lambda f: (0, f))
    wt = pl.BlockSpec((tf, DM), lambda f: (f, 0))
    wshape = _sds((DFF, DM), BF16)
    return pl.pallas_call(
        body, name=name, out_shape=(wshape, wshape, wshape), grid=(DFF // tf,), in_specs=[full, full, hid, hid, hid],
        out_specs=(wt, wt, wt), compiler_params=_params(("parallel",)))(h, dy, act, dg, du)


def _group_mean(v, bd):
    hi = v.astype(BF16)
    lo = (v - hi.astype(F32)).astype(BF16)
    return _dot(hi, bd) + _dot(lo, bd)


def _block_diag(width):
    idx = np.arange(width) // HD
    return jnp.asarray((idx[:, None] == idx[None, :]).astype(np.float32) / HD, dtype=BF16)


def qknorm_fwd(z, gq_na, gk_na, gq_sw, gk_sw, *, name, tm=256):
    def body(zq_ref, zk_ref, zv_ref, zs_ref, zkv_ref, gqa_ref, gka_ref, gqs_ref, gks_ref, bd_ref, bd2_ref,
             qa_ref, ka_ref, va_ref, qs_ref, kv_ref):
        bd = bd_ref[...]

        def norm(x, g, bdm):
            x = x.astype(F32)
            return x * lax.rsqrt(_group_mean(x * x, bdm) + EPS) * g

        qa_ref[...] = (norm(zq_ref[...], gqa_ref[...], bd) * QK_SCALE).astype(BF16)
        ka_ref[...] = norm(zk_ref[...], gka_ref[...], bd).astype(BF16)
        va_ref[...] = zv_ref[...].astype(BF16)
        qs_ref[...] = (norm(zs_ref[...], gqs_ref[...], bd) * QK_SCALE).astype(BF16)
        kv = zkv_ref[...]
        kv_ref[:, 0:128] = norm(kv[:, 0:128], gks_ref[...], bd2_ref[...]).astype(BF16)
        kv_ref[:, 128:256] = kv[:, 128:256].astype(BF16)

    s = z.shape[0]
    col = lambda j: pl.BlockSpec((tm, 512), lambda i, j=j: (i, j))
    vec = lambda w: pl.BlockSpec((1, w), lambda i: (0, 0))
    o512 = pl.BlockSpec((tm, 512), lambda i: (i, 0))
    g512 = lambda g: jnp.tile(g.reshape(1, HD), (1, 8))
    return pl.pallas_call(
        body, name=name,
        out_shape=(_sds((s, 512), BF16),) * 4 + (_sds((s, 256), BF16),), grid=(s // tm,),
        in_specs=[col(0), col(1), col(2), col(3), pl.BlockSpec((tm, 256), lambda i: (i, 8)), vec(512), vec(512), vec(512),
                  vec(128), pl.BlockSpec((512, 512), lambda i: (0, 0)), pl.BlockSpec((128, 128), lambda i: (0, 0))],
        out_specs=(o512, o512, o512, o512, pl.BlockSpec((tm, 256), lambda i: (i, 0))),
        compiler_params=_params(("parallel",)))(
            z, z, z, z, z, g512(gq_na), g512(gk_na), g512(gq_sw), jnp.tile(gk_sw.reshape(1, HD), (1, 2)),
            _block_diag(512), _block_diag(128))


def qknorm_bwd(z, dqa, dka, dva, dqs, dkv, gq_na, gk_na, gq_sw, gk_sw, *, name, tm=256):
    def body(zq_ref, zk_ref, zs_ref, zkv_ref, dqa_ref, dka_ref, dva_ref, dqs_ref, dkv_ref, gqa_ref, gka_ref, gqs_ref,
             gks_ref, bd_ref, bd2_ref, dz_ref, dgqa_ref, dgka_ref, dgqs_ref, dgks_ref):
        @pl.when(pl.program_id(0) == 0)
        def _():
            dgqa_ref[...] = jnp.zeros_like(dgqa_ref)
            dgka_ref[...] = jnp.zeros_like(dgka_ref)
            dgqs_ref[...] = jnp.zeros_like(dgqs_ref)
            dgks_ref[...] = jnp.zeros_like(dgks_ref)

        bd = bd_ref[...]

        def bwd(x, dy, g, bdm, dg_ref):
            x = x.astype(F32)
            r = lax.rsqrt(_group_mean(x * x, bdm) + EPS)
            xh = x * r
            dg_ref[...] += jnp.sum(dy * xh, axis=0, keepdims=True)
            dxn = dy * g
            return r * (dxn - xh * _group_mean(dxn * xh, bdm))

        dz_ref[:, 0:512] = bwd(zq_ref[...], dqa_ref[...] * QK_SCALE, gqa_ref[...], bd, dgqa_ref).astype(BF16)
        dz_ref[:, 512:1024] = bwd(zk_ref[...], dka_ref[...], gka_ref[...], bd, dgka_ref).astype(BF16)
        dz_ref[:, 1024:1536] = dva_ref[...].astype(BF16)
        dz_ref[:, 1536:2048] = bwd(zs_ref[...], dqs_ref[...] * QK_SCALE, gqs_ref[...], bd, dgqs_ref).astype(BF16)
        dkv = dkv_ref[...]
        dz_ref[:, 2048:2176] = bwd(zkv_ref[:, 0:128], dkv[:, 0:128], gks_ref[...], bd2_ref[...], dgks_ref).astype(BF16)
        dz_ref[:, 2176:2304] = dkv[:, 128:256].astype(BF16)

    s = z.shape[0]
    col = lambda j: pl.BlockSpec((tm, 512), lambda i, j=j: (i, j))
    t512 = pl.BlockSpec((tm, 512), lambda i: (i, 0))
    t256 = pl.BlockSpec((tm, 256), lambda i: (i, 0))
    vec = lambda w: pl.BlockSpec((1, w), lambda i: (0, 0))
    g512 = lambda g: jnp.tile(g.reshape(1, HD), (1, 8))
    return pl.pallas_call(
        body, name=name,
        out_shape=(_sds((s, ATT_W), BF16), _sds((1, 512), F32), _sds((1, 512), F32), _sds((1, 512), F32), _sds((1, 128), F32)),
        grid=(s // tm,),
        in_specs=[col(0), col(1), col(3), pl.BlockSpec((tm, 256), lambda i: (i, 8)), t512, t512, t512, t512, t256,
                  vec(512), vec(512), vec(512), vec(128), pl.BlockSpec((512, 512), lambda i: (0, 0)),
                  pl.BlockSpec((128, 128), lambda i: (0, 0))],
        out_specs=(pl.BlockSpec((tm, ATT_W), lambda i: (i, 0)), vec(512), vec(512), vec(512), vec(128)),
        compiler_params=_params(("arbitrary",)))(
            z, z, z, z, dqa, dka, dva, dqs, dkv, g512(gq_na), g512(gk_na), g512(gq_sw),
            jnp.tile(gk_sw.reshape(1, HD), (1, 2)), _block_diag(512), _block_diag(128))


def _na_row_start(r):
    return jnp.clip(r - NA_WR // 2, 0, ROWS - NA_WR)


def na_bias_table(rpb, *, name):
    t = jnp.pad(rpb, ((0, 0), (0, 2), (0, HD - (2 * NA_WC - 1))))
    pairs = jnp.concatenate([t[:, :16], t[:, 1:17]], axis=-1).reshape(NA_HEADS, 16, 1, 128)

    def body(t_ref, o_ref):
        p = pl.program_id(0)
        q = lax.broadcasted_iota(jnp.int32, (GRID_W, 128), 0)
        kc = lax.broadcasted_iota(jnp.int32, (GRID_W, 128), 1) & (GRID_W - 1)
        cs = jnp.clip(q - NA_WC // 2, 0, GRID_W - NA_WC)
        ok = (kc >= cs) & (kc < cs + NA_WC)
        for h in range(NA_HEADS):
            for pr in range(NA_WR // 2):
                x = jnp.broadcast_to(t_ref[h, 2 * pr - p + NA_WR - 1], (GRID_W, 128))
                b = pltpu.roll(x, 128 - (NA_WC - 1), 1, stride=1, stride_axis=0)
                o_ref[h, :, 128 * pr:128 * pr + 128] = jnp.where(ok, b, NEG)

    return pl.pallas_call(
        body, name=name, out_shape=_sds((NA_WR, NA_HEADS, GRID_W, NA_KEYS), F32), grid=(NA_WR,),
        in_specs=[pl.BlockSpec((NA_HEADS, 16, 1, 128), lambda p: (0, 0, 0, 0))],
        out_specs=pl.BlockSpec((None, NA_HEADS, GRID_W, NA_KEYS), lambda p: (p, 0, 0, 0)),
        compiler_params=_params(("parallel",)))(pairs)


def _lane_halves():
    lane = lax.broadcasted_iota(jnp.int32, (1, 128), 1)
    return lane < HD


def na_fwd(q, k, v, bias, *, name):
    def body(q_ref, k_ref, v_ref, b_ref, o_ref, lse_ref):
        r = pl.program_id(0)
        off = pl.multiple_of(_na_row_start(r) * GRID_W, GRID_W)
        first = _lane_halves()
        sels = [first, jnp.logical_not(first)]
        lanes = [slice(128 * j, 128 * j + 128) for j in range(NA_HEADS // 2)]
        q2s = [q_ref[:, l] for l in lanes]
        k2s = [k_ref[pl.ds(off, NA_KEYS), l] for l in lanes]
        v2s = [v_ref[pl.ds(off, NA_KEYS), l] for l in lanes]
        scores = []
        for h in range(NA_HEADS):
            j, half = divmod(h, 2)
            scores.append(_dot(jnp.where(sels[half], q2s[j], jnp.zeros_like(q2s[j])), k2s[j], NT))
        probs, lses = [], []
        for h in range(NA_HEADS):
            b = b_ref[h]
            s = jnp.where(b > 0.5 * NEG, scores[h] + b, NEG)
            m = jnp.max(s, axis=-1, keepdims=True)
            e = jnp.exp(s - m)
            l = jnp.sum(e, axis=-1, keepdims=True)
            probs.append((e / l).astype(BF16))
            lses.append(m + jnp.log(l))
        for j in range(NA_HEADS // 2):
            zero = jnp.zeros_like(v2s[j])
            o2 = (_dot(probs[2 * j], jnp.where(sels[0], v2s[j], zero))
                  + _dot(probs[2 * j + 1], jnp.where(sels[1], v2s[j], zero)))
            o_ref[:, lanes[j]] = o2.astype(BF16)
        lse_ref[...] = jnp.concatenate(lses, axis=1)

    s_tok = q.shape[0]
    full = pl.BlockSpec((s_tok, 512), lambda r: (0, 0))
    return pl.pallas_call(
        body, name=name, out_shape=(_sds((s_tok, 512), BF16), _sds((s_tok, NA_HEADS), F32)), grid=(ROWS,),
        in_specs=[pl.BlockSpec((GRID_W, 512), lambda r: (r, 0)), full, full,
                  pl.BlockSpec((None, NA_HEADS, GRID_W, NA_KEYS), lambda r: (r - _na_row_start(r), 0, 0, 0))],
        out_specs=(pl.BlockSpec((GRID_W, 512), lambda r: (r, 0)), pl.BlockSpec((GRID_W, NA_HEADS), lambda r: (r, 0))),
        compiler_params=_params(("parallel",)))(q, k, v, bias)


def na_bwd(q, k, v, o, do, lse, bias, *, name):
    def body(q_ref, k_ref, v_ref, o_ref, do_ref, lse_ref, b_ref, dq_ref, dk_ref, dv_ref, db_ref):
        r = pl.program_id(0)

        @pl.when(r == 0)
        def _():
            dk_ref[...] = jnp.zeros_like(dk_ref)
            dv_ref[...] = jnp.zeros_like(dv_ref)

        @pl.when((r <= NA_WR // 2) | (r > ROWS - NA_WR // 2))
        def _():
            db_ref[...] = jnp.zeros_like(db_ref)

        off = pl.multiple_of(_na_row_start(r) * GRID_W, GRID_W)
        first = _lane_halves()
        sels = [first, jnp.logical_not(first)]
        lanes = [slice(128 * j, 128 * j + 128) for j in range(NA_HEADS // 2)]
        q2s = [q_ref[:, l] for l in lanes]
        k2s = [k_ref[pl.ds(off, NA_KEYS), l] for l in lanes]
        v2s = [v_ref[pl.ds(off, NA_KEYS), l] for l in lanes]
        do2s = [do_ref[:, l] for l in lanes]
        prods = [do2s[j].astype(F32) * o_ref[:, lanes[j]].astype(F32) for j in range(NA_HEADS // 2)]
        lse = lse_ref[...]
        qhs, dohs, scores, dps = [], [], [], []
        for h in range(NA_HEADS):
            j, half = divmod(h, 2)
            qhs.append(jnp.where(sels[half], q2s[j], jnp.zeros_like(q2s[j])))
            dohs.append(jnp.where(sels[half], do2s[j], jnp.zeros_like(do2s[j])))
            scores.append(_dot(qhs[h], k2s[j], NT))
            dps.append(_dot(dohs[h], v2s[j], NT))
        pbs, dsbs = [], []
        for h in range(NA_HEADS):
            j, half = divmod(h, 2)
            b = b_ref[h]
            s = jnp.where(b > 0.5 * NEG, scores[h] + b, NEG)
            p = jnp.exp(s - lse[:, h:h + 1])
            delta = jnp.sum(jnp.where(sels[half], prods[j], 0.0), axis=-1, keepdims=True)
            ds = p * (dps[h] - delta)
            db_ref[h] += ds
            pbs.append(p.astype(BF16))
            dsbs.append(ds.astype(BF16))
        for j in range(NA_HEADS // 2):
            a, b = 2 * j, 2 * j + 1
            zero = jnp.zeros_like(k2s[j])
            dq_ref[:, lanes[j]] = (_dot(dsbs[a], jnp.where(sels[0], k2s[j], zero))
                                   + _dot(dsbs[b], jnp.where(sels[1], k2s[j], zero)))
            dk_ref[pl.ds(off, NA_KEYS), lanes[j]] += _dot(dsbs[a], qhs[a], TN) + _dot(dsbs[b], qhs[b], TN)
            dv_ref[pl.ds(off, NA_KEYS), lanes[j]] += _dot(pbs[a], dohs[a], TN) + _dot(pbs[b], dohs[b], TN)

    s_tok = q.shape[0]
    full = pl.BlockSpec((s_tok, 512), lambda r: (0, 0))
    row = pl.BlockSpec((GRID_W, 512), lambda r: (r, 0))
    bias_spec = pl.BlockSpec((None, NA_HEADS, GRID_W, NA_KEYS), lambda r: (r - _na_row_start(r), 0, 0, 0))
    return pl.pallas_call(
        body, name=name,
        out_shape=(_sds((s_tok, 512), F32), _sds((s_tok, 512), F32), _sds((s_tok, 512), F32),
                   _sds((NA_WR, NA_HEADS, GRID_W, NA_KEYS), F32)),
        grid=(ROWS,),
        in_specs=[row, full, full, row, row, pl.BlockSpec((GRID_W, NA_HEADS), lambda r: (r, 0)), bias_spec],
        out_specs=(row, full, full, bias_spec), compiler_params=_params(("arbitrary",)))(q, k, v, o, do, lse, bias)


def t5_bucket_map():
    rel = np.arange(SW_KEYS)[None, :] - SW_BLK - np.arange(SW_BLK)[:, None]
    nb = 16
    max_exact = nb // 2
    n = np.abs(rel)
    large = max_exact + (np.log(np.maximum(n, 1) / max_exact) / np.log(128 / max_exact) * (nb - max_exact)).astype(np.int32)
    large = np.minimum(large, nb - 1)
    return ((rel > 0) * nb + np.where(n < max_exact, n, large)).astype(np.int32)


def t5_bias(table, *, name):
    rel = np.arange(-SW_BLK, SW_BLK + 1)
    nb, max_exact = 16, 8
    n = np.abs(rel)
    large = max_exact + (np.log(np.maximum(n, 1) / max_exact) / np.log(128 / max_exact) * (nb - max_exact)).astype(np.int32)
    bucket = ((rel > 0) * nb + np.where(n < max_exact, n, np.minimum(large, nb - 1))).astype(np.int32)
    u = jnp.pad(table[jnp.asarray(bucket)].T, ((0, 0), (0, SW_KEYS - bucket.shape[0]))).reshape(8, 1, SW_KEYS)

    def body(u_ref, o_ref):
        for h in range(8):
            x = jnp.broadcast_to(u_ref[h], (SW_BLK, SW_KEYS))
            o_ref[h] = pltpu.roll(x, 0, 1, stride=1, stride_axis=0)

    return pl.pallas_call(body, name=name, out_shape=_sds((8, SW_BLK, SW_KEYS), F32), compiler_params=_params())(u)


def _sw_valid(n):
    a = lax.broadcasted_iota(jnp.int32, (SW_BLK, SW_KEYS), 0)
    j = lax.broadcasted_iota(jnp.int32, (SW_BLK, SW_KEYS), 1)
    kpos = (n - 1) * SW_BLK + j
    return (jnp.abs(j - SW_BLK - a) <= SW_BLK) & (kpos >= 0) & (kpos < SEQ)


def _dup_group(x2, g, first):
    rolled = pltpu.roll(x2, HD, 1)
    return jnp.where(first, x2, rolled) if g == 0 else jnp.where(first, rolled, x2)


def sw_fwd(q, kv, t5, sink, *, name):
    def body(q_ref, kv_ref, t5_ref, sink_ref, o_ref, lse_ref):
        n = pl.program_id(0)
        off = pl.multiple_of(n * SW_BLK, SW_BLK)
        first = _lane_halves()
        sels = [first, jnp.logical_not(first)]
        valid = _sw_valid(n)
        k2 = kv_ref[pl.ds(off, SW_KEYS), 0:128]
        v2 = kv_ref[pl.ds(off, SW_KEYS), 128:256]
        kk = [_dup_group(k2, g, first) for g in range(2)]
        vv = [_dup_group(v2, g, first) for g in range(2)]
        q2s = [q_ref[:, 128 * j:128 * j + 128] for j in range(4)]
        scores = []
        for h in range(8):
            j, half = divmod(h, 2)
            scores.append(_dot(jnp.where(sels[half], q2s[j], jnp.zeros_like(q2s[j])), kk[j // 2], NT))
        probs, lses = [], []
        for h in range(8):
            s = jnp.where(valid, scores[h] + t5_ref[h], NEG)
            snk = sink_ref[h]
            m = jnp.maximum(jnp.max(s, axis=-1, keepdims=True), snk)
            e = jnp.exp(s - m)
            den = jnp.sum(e, axis=-1, keepdims=True) + jnp.exp(snk - m)
            probs.append((e / den).astype(BF16))
            lses.append(m + jnp.log(den))
        outs = []
        for j in range(4):
            vg = vv[j // 2]
            zero = jnp.zeros_like(vg)
            outs.append(_dot(probs[2 * j], jnp.where(sels[0], vg, zero)) + _dot(probs[2 * j + 1], jnp.where(sels[1], vg, zero)))
        o_ref[...] = jnp.concatenate(outs, axis=1).astype(BF16)
        lse_ref[...] = jnp.concatenate(lses, axis=1)

    s_tok = q.shape[0]
    blk = pl.BlockSpec((SW_BLK, 512), lambda n: (n, 0))
    return pl.pallas_call(
        body, name=name, out_shape=(_sds((s_tok, 512), BF16), _sds((s_tok, 8), F32)), grid=(SW_NB,),
        in_specs=[blk, pl.BlockSpec(kv.shape, lambda n: (0, 0)), pl.BlockSpec((8, SW_BLK, SW_KEYS), lambda n: (0, 0, 0)),
                  pl.BlockSpec(memory_space=pltpu.SMEM)],
        out_specs=(blk, pl.BlockSpec((SW_BLK, 8), lambda n: (n, 0))), compiler_params=_params(("parallel",)))(q, kv, t5, sink)


def sw_bwd(q, kv, o, do, lse, t5, sink, *, name):
    def body(q_ref, kv_ref, o_ref, do_ref, lse_ref, t5_ref, sink_ref, dq_ref, dkv_ref, dt5_ref, dsink_ref):
        n = pl.program_id(0)

        @pl.when(n == 0)
        def _():
            dkv_ref[...] = jnp.zeros_like(dkv_ref)
            dt5_ref[...] = jnp.zeros_like(dt5_ref)
            dsink_ref[...] = jnp.zeros_like(dsink_ref)

        off = pl.multiple_of(n * SW_BLK, SW_BLK)
        first = _lane_halves()
        sels = [first, jnp.logical_not(first)]
        valid = _sw_valid(n)
        k2 = kv_ref[pl.ds(off, SW_KEYS), 0:128]
        v2 = kv_ref[pl.ds(off, SW_KEYS), 128:256]
        kk = [_dup_group(k2, g, first) for g in range(2)]
        vv = [_dup_group(v2, g, first) for g in range(2)]
        lanes = [slice(128 * j, 128 * j + 128) for j in range(4)]
        q2s = [q_ref[:, l] for l in lanes]
        do2s = [do_ref[:, l] for l in lanes]
        prods = [do2s[j].astype(F32) * o_ref[:, lanes[j]].astype(F32) for j in range(4)]
        lse = lse_ref[...]
        qhs, dohs, scores, dps = [], [], [], []
        for h in range(8):
            j, half = divmod(h, 2)
            qhs.append(jnp.where(sels[half], q2s[j], jnp.zeros_like(q2s[j])))
            dohs.append(jnp.where(sels[half], do2s[j], jnp.zeros_like(do2s[j])))
            scores.append(_dot(qhs[h], kk[j // 2], NT))
            dps.append(_dot(dohs[h], vv[j // 2], NT))
        pbs, dsbs, dss, dsinks = [], [], [], []
        for h in range(8):
            j, half = divmod(h, 2)
            s = jnp.where(valid, scores[h] + t5_ref[h], NEG)
            lse_h = lse[:, h:h + 1]
            p = jnp.exp(s - lse_h)
            delta = jnp.sum(jnp.where(sels[half], prods[j], 0.0), axis=-1, keepdims=True)
            ds = p * (dps[h] - delta)
            dss.append(ds)
            dsinks.append(-jnp.sum(jnp.exp(sink_ref[h] - lse_h) * delta, axis=0, keepdims=True))
            pbs.append(p.astype(BF16))
            dsbs.append(ds.astype(BF16))
        dt5_ref[...] += jnp.stack(dss)
        dsink_ref[...] += jnp.concatenate(dsinks, axis=1)
        dqs = []
        for j in range(4):
            a, b = 2 * j, 2 * j + 1
            zero = jnp.zeros_like(kk[j // 2])
            dqs.append(_dot(dsbs[a], jnp.where(sels[0], kk[j // 2], zero)) + _dot(dsbs[b], jnp.where(sels[1], kk[j // 2], zero)))
        dq_ref[...] = jnp.concatenate(dqs, axis=1)
        dk_groups, dv_groups = [], []
        for g in range(2):
            dkk = sum(_dot(dsbs[h], qhs[h], TN) for h in range(4 * g, 4 * g + 4))
            dvv = sum(_dot(pbs[h], dohs[h], TN) for h in range(4 * g, 4 * g + 4))
            dk_groups.append(dkk + pltpu.roll(dkk, HD, 1))
            dv_groups.append(dvv + pltpu.roll(dvv, HD, 1))
        dkv_ref[pl.ds(off, SW_KEYS), :] += jnp.concatenate(
            [jnp.where(first, dk_groups[0], dk_groups[1]), jnp.where(first, dv_groups[0], dv_groups[1])], axis=1)

    s_tok = q.shape[0]
    blk = pl.BlockSpec((SW_BLK, 512), lambda n: (n, 0))
    kv_spec = pl.BlockSpec(kv.shape, lambda n: (0, 0))
    t5_spec = pl.BlockSpec((8, SW_BLK, SW_KEYS), lambda n: (0, 0, 0))
    vec = pl.BlockSpec((1, 8), lambda n: (0, 0))
    return pl.pallas_call(
        body, name=name,
        out_shape=(_sds((s_tok, 512), F32), _sds(kv.shape, F32), _sds((8, SW_BLK, SW_KEYS), F32), _sds((1, 8), F32)),
        grid=(SW_NB,), in_specs=[blk, kv_spec, blk, blk, pl.BlockSpec((SW_BLK, 8), lambda n: (n, 0)), t5_spec,
                                 pl.BlockSpec(memory_space=pltpu.SMEM)],
        out_specs=(blk, kv_spec, t5_spec, vec), compiler_params=_params(("arbitrary",)))(q, kv, o, do, lse, t5, sink)


def gate_fwd(zg, bias, pa, ps, *, name, tm=512):
    def body(z0_ref, z1_ref, b0_ref, b1_ref, pa_ref, ps_ref, m_ref):
        g0 = jax.nn.sigmoid(z0_ref[...] + b0_ref[...])
        g1 = jax.nn.sigmoid(z1_ref[...] + b1_ref[...])
        m_ref[...] = (g0 * pa_ref[...] + g1 * ps_ref[...]).astype(BF16)

    s = zg.shape[0]
    half = lambda j: pl.BlockSpec((tm, DM), lambda i, j=j: (i, j))
    bvec = lambda j: pl.BlockSpec((1, DM), lambda i, j=j: (0, j))
    return pl.pallas_call(
        body, name=name, out_shape=_sds((s, DM), BF16), grid=(s // tm,),
        in_specs=[half(0), half(1), bvec(0), bvec(1), half(0), half(0)], out_specs=half(0),
        compiler_params=_params(("parallel",)))(zg, zg, bias, bias, pa, ps)


def gate_bwd(dm, zg, bias, pa, ps, *, name, tm=512):
    def body(dm_ref, z0_ref, z1_ref, b0_ref, b1_ref, pa_ref, ps_ref, dpa_ref, dps_ref, dz_ref, db_ref):
        @pl.when(pl.program_id(0) == 0)
        def _():
            db_ref[...] = jnp.zeros_like(db_ref)

        dm = dm_ref[...]
        g0 = jax.nn.sigmoid(z0_ref[...] + b0_ref[...])
        g1 = jax.nn.sigmoid(z1_ref[...] + b1_ref[...])
        dpa_ref[...] = (dm * g0).astype(BF16)
        dps_ref[...] = (dm * g1).astype(BF16)
        dz0 = dm * pa_ref[...] * g0 * (1.0 - g0)
        dz1 = dm * ps_ref[...] * g1 * (1.0 - g1)
        dz_ref[:, 0:DM] = dz0.astype(BF16)
        dz_ref[:, DM:2 * DM] = dz1.astype(BF16)
        db_ref[:, 0:DM] += jnp.sum(dz0, axis=0, keepdims=True)
        db_ref[:, DM:2 * DM] += jnp.sum(dz1, axis=0, keepdims=True)

    s = zg.shape[0]
    half = lambda j: pl.BlockSpec((tm, DM), lambda i, j=j: (i, j))
    bvec = lambda j: pl.BlockSpec((1, DM), lambda i, j=j: (0, j))
    return pl.pallas_call(
        body, name=name,
        out_shape=(_sds((s, DM), BF16), _sds((s, DM), BF16), _sds((s, GATE_W), BF16), _sds((1, GATE_W), F32)),
        grid=(s // tm,), in_specs=[half(0), half(0), half(1), bvec(0), bvec(1), half(0), half(0)],
        out_specs=(half(0), half(0), pl.BlockSpec((tm, GATE_W), lambda i: (i, 0)), pl.BlockSpec((1, GATE_W), lambda i: (0, 0))),
        compiler_params=_params(("arbitrary",)))(dm, zg, zg, bias, bias, pa, ps)


def loss_head(y, target, *, name, tm=512):
    def body(y_ref, t_ref, dy_ref, dyb_ref, l_ref):
        @pl.when(pl.program_id(0) == 0)
        def _():
            l_ref[...] = jnp.zeros_like(l_ref)

        err = y_ref[...] - t_ref[...]
        dy = err * (1.0 / DM)
        dy_ref[...] = dy
        dyb_ref[...] = dy.astype(BF16)
        l_ref[...] += 0.5 * jnp.sum(jnp.mean(err * err, axis=-1, keepdims=True), axis=0, keepdims=True)

    s = y.shape[0]
    tile = pl.BlockSpec((tm, DM), lambda i: (i, 0))
    return pl.pallas_call(
        body, name=name, out_shape=(_sds((s, DM), F32), _sds((s, DM), BF16), _sds((1, 128), F32)), grid=(s // tm,),
        in_specs=[tile, tile], out_specs=(tile, tile, pl.BlockSpec((1, 128), lambda i: (0, 0))),
        compiler_params=_params(("arbitrary",)))(y, target)


def adamw_small(ws, gs, ms, vs, *, name):
    cnt = len(ws)

    def body(*refs):
        ins, outs = refs[:4 * cnt], refs[4 * cnt:]
        for i in range(cnt):
            w_ref, g_ref, m_ref, v_ref = ins[4 * i:4 * i + 4]
            d_ref, nm_ref, nv_ref = outs[3 * i:3 * i + 3]
            g = g_ref[...]
            nm = ADAM_B1 * m_ref[...] + (1.0 - ADAM_B1) * g
            nv = ADAM_B2 * v_ref[...] + (1.0 - ADAM_B2) * jnp.square(g)
            m_hat = nm / (1.0 - ADAM_B1 ** ADAM_STEP)
            v_hat = nv / (1.0 - ADAM_B2 ** ADAM_STEP)
            d_ref[...] = -ADAM_LR * (m_hat / (jnp.sqrt(v_hat) + ADAM_EPS) + ADAM_WD * w_ref[...])
            nm_ref[...] = nm
            nv_ref[...] = nv

    flat = [a for i in range(cnt) for a in (ws[i], gs[i], ms[i], vs[i])]
    res = pl.pallas_call(
        body, name=name, out_shape=tuple(_sds(ws[i].shape, F32) for i in range(cnt) for _ in range(3)),
        compiler_params=_params())(*flat)
    return [tuple(res[3 * i:3 * i + 3]) for i in range(cnt)]


def adamw_layer(ws, ms, vs, mines, theirs, cidx, layer, filled=None, *, name):
    cnt = len(ws)
    _, k, n = ws[0].shape
    nt = 2
    tk = k // 2 // nt

    def body(c_ref, *refs):
        own = pl.program_id(0) == c_ref[0]
        outs = refs[-4 * cnt:]
        for i in range(cnt):
            w_ref, m_ref, v_ref, a_ref, b_ref = refs[5 * i:5 * i + 5]
            g_ref, d_ref, nm_ref, nv_ref = outs[4 * i:4 * i + 4]
            g = jnp.where(own, a_ref[...], b_ref[...])
            g_ref[...] = g
            nm = ADAM_B1 * m_ref[...] + (1.0 - ADAM_B1) * g
            nv = ADAM_B2 * v_ref[...] + (1.0 - ADAM_B2) * jnp.square(g)
            m_hat = nm / (1.0 - ADAM_B1 ** ADAM_STEP)
            v_hat = nv / (1.0 - ADAM_B2 ** ADAM_STEP)
            d_ref[...] = -ADAM_LR * (m_hat / (jnp.sqrt(v_hat) + ADAM_EPS) + ADAM_WD * w_ref[...])
            nm_ref[...] = nm
            nv_ref[...] = nv

    full = pl.BlockSpec((None, tk, n), lambda hf, t, c: (layer, hf * nt + t, 0))
    half_mine = pl.BlockSpec((tk, n), lambda hf, t, c: (jnp.where(hf == c[0], t, 0), 0))
    half_theirs = pl.BlockSpec((tk, n), lambda hf, t, c: (jnp.where(hf != c[0], t, 0), 0))
    out = _sds(ws[0].shape, F32)
    ins, specs, aliases = [cidx], [], {}
    for i in range(cnt):
        ins += [ws[i], ms[i], vs[i], mines[i], theirs[i]]
        specs += [full, full, full, half_mine, half_theirs]
    if filled is not None:
        aliases = {len(ins) + j: j for j in range(4 * cnt)}
        ins += [a for f in filled for a in f]
        specs += [pl.BlockSpec(memory_space=pl.ANY)] * (4 * cnt)
    res = pl.pallas_call(
        body, name=name, out_shape=(out,) * (4 * cnt),
        grid_spec=pltpu.PrefetchScalarGridSpec(
            num_scalar_prefetch=1, grid=(2, nt), in_specs=specs, out_specs=(full,) * (4 * cnt)),
        input_output_aliases=aliases,
        compiler_params=_params(("arbitrary", "arbitrary")))(*ins)
    return [tuple(res[4 * i:4 * i + 4]) for i in range(cnt)]


def t5_table_grad(dt5_a, dt5_b, *, name):
    def body(a_ref, b_ref, map_ref, o_ref):
        d = a_ref[...] + b_ref[...]
        bucket = map_ref[...]
        for b in range(32):
            hit = (bucket == b)[None]
            o_ref[b] = jnp.sum(jnp.sum(jnp.where(hit, d, 0.0), axis=2), axis=1, keepdims=True)

    return pl.pallas_call(
        body, name=name, out_shape=_sds((32, 8, 1), F32), compiler_params=_params())(
            dt5_a, dt5_b, jnp.asarray(t5_bucket_map()))


def rpb_grad(dbias, *, name):
    def body(d_ref, rev_ref, o_ref):
        rev = rev_ref[...]
        for h in range(NA_HEADS):
            for pr in range(NA_WR // 2):
                d = d_ref[h, :, 128 * pr:128 * pr + 128]
                hi = d.astype(BF16)
                lo = (d - hi.astype(F32)).astype(BF16)
                flipped = _dot(rev, hi) + _dot(rev, lo)
                o_ref[h, pr] = jnp.sum(pltpu.roll(flipped, 0, 1, stride=1, stride_axis=0), axis=0, keepdims=True)

    anti = jnp.asarray(np.eye(GRID_W, dtype=np.float32)[::-1], dtype=BF16)
    e = pl.pallas_call(
        body, name=name, out_shape=_sds((NA_WR, NA_HEADS, NA_WR // 2, 1, 128), F32), grid=(NA_WR,),
        in_specs=[pl.BlockSpec((None, NA_HEADS, GRID_W, NA_KEYS), lambda p: (p, 0, 0, 0)),
                  pl.BlockSpec((GRID_W, GRID_W), lambda p: (0, 0))],
        out_specs=pl.BlockSpec((None, NA_HEADS, NA_WR // 2, 1, 128), lambda p: (p, 0, 0, 0, 0)),
        compiler_params=_params(("parallel",)))(dbias, anti)
    nci, nri = 2 * NA_WC - 1, 2 * NA_WR - 1
    e = e.reshape(NA_WR, NA_HEADS, NA_WR // 2, 128).transpose(0, 2, 1, 3).reshape(NA_WR * NA_WR // 2, NA_HEADS, 128)
    parts = jnp.concatenate([e[..., 48:48 + nci], jnp.concatenate([e[..., 112:128], e[..., 0:nci - 16]], axis=-1)], axis=0)
    p, pr = np.arange(NA_WR)[:, None], np.arange(NA_WR // 2)[None, :]
    ri = np.concatenate([(2 * pr - p + NA_WR - 1).reshape(-1), (2 * pr - p + NA_WR).reshape(-1)])
    pick = jnp.asarray((ri[None, :] == np.arange(16)[:, None]).astype(np.float32))
    out = mm(pick, parts.reshape(2 * NA_WR * NA_WR // 2, NA_HEADS * nci), name=name + "_rows", exact=True)
    return out.reshape(16, NA_HEADS, nci)[:nri].transpose(1, 0, 2)


BIG = ("ffn1_w_gate", "ffn1_w_up", "ffn1_w_down", "w_in", "w_branch_na", "w_branch_sw", "w_out",
       "ffn2_w_gate", "ffn2_w_up", "ffn2_w_down")
SMALL = ("ffn1_norm", "mix_norm", "b_gate", "na_q_norm", "na_k_norm", "na_rpb", "sw_q_norm", "sw_k_norm", "sw_sink",
         "ffn2_norm")


def _cols_to_full(w4):
    return w4.transpose(1, 0, 2).reshape(w4.shape[1], NSH * w4.shape[2])


def _full_to_cols(w):
    return w.reshape(w.shape[0], NSH, w.shape[1] // NSH).transpose(1, 0, 2)


def _mixer_weights(g):
    w_in_t = g["w_in"].reshape(IN_W, DM)
    return dict(w_in_t=w_in_t, wa=_cols_to_full(g["w_branch_na"]),
                ws=_cols_to_full(g["w_branch_sw"]), wo=g["w_out"].reshape(DM, DM))


GROUPS = {"ffn1": ("ffn1_w_gate", "ffn1_w_up", "ffn1_w_down"), "mix": ("w_in", "w_branch_na", "w_branch_sw", "w_out"),
          "ffn2": ("ffn2_w_gate", "ffn2_w_up", "ffn2_w_down")}


def layer_fwd(x, p, weights, t5b):
    row = lambda v: v.reshape(1, -1)
    stacked = lambda g: {n: a.reshape(DFF, DM) for n, a in g.items()}
    g1 = stacked(weights("ffn1", x))
    y1, h1, gg1, uu1 = ffn_fwd(x, row(p["ffn1_norm"]), g1["ffn1_w_gate"], g1["ffn1_w_up"], g1["ffn1_w_down"], name="ffn_fwd")
    w = _mixer_weights(weights("mix", y1))
    hm = rms_fwd(y1, row(p["mix_norm"]), name="mix_norm_fwd")
    z = mm(hm, w["w_in_t"], tb=True, b_rows=(0, ATT_W), out_dtype=BF16, name="proj_att", tm=SEQ, tn=768)
    zg = mm(hm, w["w_in_t"], tb=True, b_rows=(ATT_W, GATE_W), out_dtype=BF16, name="proj_gate", tm=SEQ, tn=512)
    qa, ka, va, qs, kv = qknorm_fwd(z, p["na_q_norm"], p["na_k_norm"], p["sw_q_norm"], p["sw_k_norm"], name="qknorm_fwd")
    bias = p["na_bias"]
    o_na, lse_na = na_fwd(qa, ka, va, bias, name="na_fwd")
    kvp = jnp.pad(kv, ((SW_BLK, SW_BLK), (0, 0)))
    sink = p["sw_sink"]
    o_sw, lse_sw = sw_fwd(qs, kvp, t5b, sink, name="sw_fwd")
    pa = mm(o_na, w["wa"], out_dtype=BF16, name="branch_na", tm=1024)
    ps = mm(o_sw, w["ws"], out_dtype=BF16, name="branch_sw", tm=1024)
    merged = gate_fwd(zg, row(p["b_gate"]), pa, ps, name="gate_fwd")
    y2 = mm(merged, w["wo"], add=y1, name="out_proj", tm=1024)
    g2 = stacked(weights("ffn2", y2))
    y3, h2, gg2, uu2 = ffn_fwd(y2, row(p["ffn2_norm"]), g2["ffn2_w_gate"], g2["ffn2_w_up"], g2["ffn2_w_down"], name="ffn_fwd")
    saved = dict(x=x, y1=y1, h1=h1, gg1=gg1, uu1=uu1, hm=hm, z=z, zg=zg, qa=qa, ka=ka, va=va, qs=qs, kvp=kvp, bias=bias,
                 o_na=o_na, lse_na=lse_na, o_sw=o_sw, lse_sw=lse_sw, pa=pa, ps=ps, merged=merged, y2=y2, h2=h2, gg2=gg2,
                 uu2=uu2, w=w, sink=sink, g1=g1, g2=g2)
    return y3, saved


def layer_bwd(dy3, dy3_bf, sv, p, t5b, emit, dep=None):
    w, g1, g2 = sv["w"], sv["g1"], sv["g2"]
    row = lambda v: v.reshape(1, -1)
    fold = lambda v: v.reshape(-1, HD).sum(axis=0)
    small = {}
    dy2, _, small["ffn2_norm"], act, dg, du = ffn_bwd_tokens(
        dy3, sv["y2"], row(p["ffn2_norm"]), sv["gg2"], sv["uu2"], g2["ffn2_w_gate"], g2["ffn2_w_up"], g2["ffn2_w_down"],
        name="ffn_bwd_tokens", dep=dep)
    shards = lambda gs: [g.reshape(NSH, FSH, DM) for g in gs]
    token = emit("ffn2", shards(ffn_bwd_weights(sv["h2"], dy3_bf, act, dg, du, name="ffn_bwd_weights")))
    dmerged = mm(dy2, w["wo"], tb=True, name="out_proj_dx", tm=1024, dep=token)
    gw_out = mm(sv["merged"], dy2, ta=True, out_dtype=BF16, name="out_proj_dw").reshape(NSH, DM // NSH, DM)
    dpa, dps, dzg, small["b_gate"] = gate_bwd(dmerged, sv["zg"], row(p["b_gate"]), sv["pa"], sv["ps"], name="gate_bwd")
    gw_na = _full_to_cols(mm(sv["o_na"], dpa, ta=True, out_dtype=BF16, name="branch_dw"))
    gw_sw = _full_to_cols(mm(sv["o_sw"], dps, ta=True, out_dtype=BF16, name="branch_dw"))
    do_na = mm(dpa, w["wa"], tb=True, out_dtype=BF16, tm=SEQ, name="branch_dx")
    do_sw = mm(dps, w["ws"], tb=True, out_dtype=BF16, tm=SEQ, name="branch_dx")
    dqa, dka, dva, dbias = na_bwd(sv["qa"], sv["ka"], sv["va"], sv["o_na"], do_na, sv["lse_na"], sv["bias"], name="na_bwd")
    dqs, dkvp, dt5, dsink = sw_bwd(sv["qs"], sv["kvp"], sv["o_sw"], do_sw, sv["lse_sw"], t5b, sv["sink"], name="sw_bwd")
    dkv = dkvp[SW_BLK:SW_BLK + SEQ]
    dz, dgqa, dgka, dgqs, dgks = qknorm_bwd(sv["z"], dqa, dka, dva, dqs, dkv, p["na_q_norm"], p["na_k_norm"],
                                            p["sw_q_norm"], p["sw_k_norm"], name="qknorm_bwd")
    small["na_q_norm"], small["na_k_norm"], small["sw_q_norm"], small["sw_k_norm"] = fold(dgqa), fold(dgka), fold(dgqs), fold(dgks)
    small["na_rpb"] = rpb_grad(dbias, name="rpb_grad")
    small["sw_sink"] = dsink
    gw_att_t = mm(dz, sv["hm"], ta=True, out_dtype=BF16, tm=768, name="proj_att_dw")
    gw_gz_t = mm(dzg, sv["hm"], ta=True, out_dtype=BF16, tm=1024, name="proj_gate_dw")
    gw_in = jnp.concatenate([gw_att_t, gw_gz_t], axis=0).reshape(NSH, IN_W // NSH, DM)
    token = emit("mix", (gw_in, gw_na, gw_sw, gw_out))
    dh = mm(dz, w["w_in_t"], b_rows=(0, ATT_W), tm=1024, name="proj_att_dx", dep=token)
    dh = mm(dzg, w["w_in_t"], b_rows=(ATT_W, GATE_W), add=dh, tm=1024, name="proj_gate_dx")
    dy1, dy1_bf, small["mix_norm"] = rms_bwd(dh, sv["y1"], row(p["mix_norm"]), dy2, name="mix_norm_bwd")
    dx, dx_bf, small["ffn1_norm"], act, dg, du = ffn_bwd_tokens(
        dy1, sv["x"], row(p["ffn1_norm"]), sv["gg1"], sv["uu1"], g1["ffn1_w_gate"], g1["ffn1_w_up"], g1["ffn1_w_down"],
        name="ffn_bwd_tokens")
    emit("ffn1", shards(ffn_bwd_weights(sv["h1"], dy1_bf, act, dg, du, name="ffn_bwd_weights")))
    return dx, dx_bf, small, dt5


ANY = pl.BlockSpec(memory_space=pl.ANY)


def _place():
    x, y, c = lax.axis_index("x"), lax.axis_index("y"), lax.axis_index("c")
    chips = [(1 - x, y), (x, 1 - y), (1 - x, 1 - y)]
    return x, y, c, chips


def _remote(src, dst, send_sem, recv_sem, to):
    return pltpu.make_async_remote_copy(src_ref=src, dst_ref=dst, send_sem=send_sem, recv_sem=recv_sem, device_id=to,
                                        device_id_type=MESH)


HBM = pl.BlockSpec(memory_space=pltpu.HBM)
SEM = pl.BlockSpec(memory_space=pltpu.SEMAPHORE)
ORDERED_EFFECT = pltpu.SideEffectType.DATAFLOW_SIDE_EFFECTING


def _in_hbm(v):
    return pltpu.with_memory_space_constraint(v, pltpu.HBM)


def _row_half(ref_shape_rows, c):
    half = ref_shape_rows // 2
    return pl.ds(c * half, half)


def _ici_gather_copies(w, land, send_sems, recv_sems):
    x, y, c, chips = _place()
    me = 2 * x + y
    copies = []
    for a in range(len(w)):
        rows = _row_half(w[a].shape[0], c)
        for k, chip in enumerate(chips):
            copies.append(_remote(w[a].at[rows], land[a].at[me, rows], send_sems.at[4 * a + k], recv_sems.at[4 * a + k],
                                  (*chip, c)))
        copies.append(_remote(w[a], land[a].at[me], send_sems.at[4 * a + 3], recv_sems.at[4 * a + 3], (x, y, 1 - c)))
    return copies


def _d2d_gather_copies(w, land, send_sems, recv_sems):
    x, y, c, chips = _place()
    copies = []
    for a in range(len(w)):
        rows = _row_half(w[a].shape[0], c)
        for k, (cx, cy) in enumerate(chips):
            blk = land[a].at[2 * cx + cy, rows]
            copies.append(_remote(blk, blk, send_sems.at[3 * a + k], recv_sems.at[3 * a + k], (x, y, 1 - c)))
    return copies


def _d2d_gather_waits(w, land, send_sems, recv_sems):
    x, y, c, chips = _place()
    waits = []
    for a in range(len(w)):
        rows = _row_half(w[a].shape[0], 1 - c)
        for k, (cx, cy) in enumerate(chips):
            blk = land[a].at[2 * cx + cy, rows]
            waits.append(_remote(blk, blk, send_sems.at[3 * a + k], recv_sems.at[3 * a + k], (x, y, 1 - c)))
    return waits


def gather_start(groups, dep=None, *, name):
    sizes = [len(g) for g in groups]
    shards = [s for g in groups for s in g]
    n, ng = len(shards), len(groups)
    extra = [] if dep is None else [dep]

    def body(*refs):
        first_out = 2 * n + len(extra)
        w, land, sems = refs[:n], refs[n:2 * n], refs[first_out:first_out + 2 * ng]
        off = 0
        for gi, size in enumerate(sizes):
            for cp in _ici_gather_copies(w[off:off + size], land[off:off + size], sems[2 * gi], sems[2 * gi + 1]):
                cp.start()
            off += size

    lands = [lax.empty((NSH,) + s.shape, s.dtype) for s in shards]
    sem_shapes = tuple(pltpu.SemaphoreType.DMA((4 * size,)) for size in sizes for _ in range(2))
    res = pl.pallas_call(
        body, name=name,
        out_shape=sem_shapes + tuple(pltpu.HBM(s.shape, s.dtype) for s in shards) + tuple(pltpu.HBM(l.shape, l.dtype) for l in lands),
        in_specs=[HBM] * (2 * n) + [ANY] * len(extra), out_specs=(SEM,) * (2 * ng) + (HBM,) * (2 * n),
        input_output_aliases={i: 2 * ng + i for i in range(2 * n)},
        compiler_params=pltpu.CompilerParams(has_side_effects=ORDERED_EFFECT))(
            *[_in_hbm(s) for s in shards], *[_in_hbm(l) for l in lands], *extra)
    out, off = [], 0
    for gi, size in enumerate(sizes):
        out.append((res[2 * gi], res[2 * gi + 1], list(res[2 * ng + off:2 * ng + off + size]),
                    list(res[2 * ng + n + off:2 * ng + n + off + size])))
        off += size
    return out


def gather_wait(send_sems, recv_sems, shards, lands, after, *, name):
    n = len(shards)

    def body(*refs):
        w, land = refs[:n], refs[n:2 * n]
        send, recv = refs[2 * n:2 * n + 2]
        for cp in _ici_gather_copies(w, land, send, recv):
            cp.wait_send()
            cp.wait_recv()

    res = pl.pallas_call(
        body, name=name,
        out_shape=tuple(pltpu.HBM(s.shape, s.dtype) for s in shards) + tuple(pltpu.HBM(l.shape, l.dtype) for l in lands),
        in_specs=[HBM] * (2 * n) + [SEM, SEM] + [ANY] * len(after), out_specs=(HBM,) * (2 * n),
        input_output_aliases={i: i for i in range(2 * n)},
        compiler_params=pltpu.CompilerParams(has_side_effects=ORDERED_EFFECT))(*shards, *lands, send_sems, recv_sems, *after)
    return list(res[:n]), list(res[n:])


def gather_finish(shards, lands, *, name):
    n = len(shards)

    def body(*refs):
        w, land = refs[:n], refs[n:2 * n]
        send_sems, recv_sems = refs[3 * n:]
        d2d = _d2d_gather_copies(w, land, send_sems, recv_sems)
        for cp in d2d:
            cp.start()
        for cp in _d2d_gather_waits(w, land, send_sems, recv_sems):
            cp.wait_recv()
        for cp in d2d:
            cp.wait_send()

    return list(pl.pallas_call(
        body, name=name, out_shape=tuple(pltpu.HBM(l.shape, l.dtype) for l in lands),
        in_specs=[ANY] * (2 * n), out_specs=tuple([ANY] * n), input_output_aliases={n + i: i for i in range(n)},
        scratch_shapes=[pltpu.SemaphoreType.DMA((3 * n,)), pltpu.SemaphoreType.DMA((3 * n,))])(*shards, *lands))


def _pair_exchange_copies(g, buf, send_sems, recv_sems):
    x, y, c, _ = _place()
    copies = []
    for a in range(len(g)):
        half = g[a].shape[1] // 2
        copies.append(_remote(g[a].at[:, pl.ds((1 - c) * half, half)], buf[a], send_sems.at[a], recv_sems.at[a], (x, y, 1 - c)))
    return copies


def pair_exchange_start(grads, dep=None, *, name):
    n = len(grads)
    extra = [] if dep is None else [dep]

    def body(*refs):
        sems = refs[2 * n + len(extra):]
        for cp in _pair_exchange_copies(refs[:n], refs[n:2 * n], sems[0], sems[1]):
            cp.start()
        refs[-1][...] = jnp.zeros_like(refs[-1])

    lands = [lax.empty((NSH, g.shape[1] // 2, g.shape[2]), g.dtype) for g in grads]
    res = pl.pallas_call(
        body, name=name,
        out_shape=(pltpu.SemaphoreType.DMA((n,)), pltpu.SemaphoreType.DMA((n,)))
        + tuple(pltpu.HBM(g.shape, g.dtype) for g in grads) + tuple(pltpu.HBM(l.shape, l.dtype) for l in lands)
        + (_sds((8, 128), F32),),
        in_specs=[HBM] * (2 * n) + [ANY] * len(extra),
        out_specs=(SEM, SEM) + (HBM,) * (2 * n) + (pl.BlockSpec(memory_space=pltpu.VMEM),),
        input_output_aliases={i: 2 + i for i in range(2 * n)},
        compiler_params=pltpu.CompilerParams(has_side_effects=ORDERED_EFFECT))(
            *[_in_hbm(g) for g in grads], *[_in_hbm(l) for l in lands], *extra)
    return res[0], res[1], list(res[2:2 + n]), list(res[2 + n:2 + 2 * n]), res[-1]


def pair_exchange_wait(send_sems, recv_sems, grads, lands, after, *, name):
    n = len(grads)

    def body(*refs):
        for cp in _pair_exchange_copies(refs[:n], refs[n:2 * n], refs[2 * n], refs[2 * n + 1]):
            cp.wait_send()
            cp.wait_recv()

    res = pl.pallas_call(
        body, name=name,
        out_shape=tuple(pltpu.HBM(g.shape, g.dtype) for g in grads) + tuple(pltpu.HBM(l.shape, l.dtype) for l in lands),
        in_specs=[HBM] * (2 * n) + [SEM, SEM] + [ANY] * len(after), out_specs=(HBM,) * (2 * n),
        input_output_aliases={i: i for i in range(2 * n)},
        compiler_params=pltpu.CompilerParams(has_side_effects=ORDERED_EFFECT))(*grads, *lands, send_sems, recv_sems, *after)
    return list(res[:n]), list(res[n:])


def _chip_exchange_copies(s, buf, send_sems, recv_sems):
    x, y, c, chips = _place()
    return [_remote(s[a].at[2 * cx + cy], buf[a].at[k], send_sems.at[3 * a + k], recv_sems.at[3 * a + k], (cx, cy, c))
            for a in range(len(s)) for k, (cx, cy) in enumerate(chips)]


def exchange_start(sums, grads, *, name):
    n1, n2 = len(sums), len(grads)

    def body(*refs):
        first_out = 2 * (n1 + n2)
        chip = _chip_exchange_copies(refs[:n1], refs[n1:2 * n1], refs[first_out], refs[first_out + 1])
        pair = _pair_exchange_copies(refs[2 * n1:2 * n1 + n2], refs[2 * n1 + n2:first_out], refs[first_out + 2],
                                     refs[first_out + 3])
        for cp in chip + pair:
            cp.start()
        refs[-1][...] = jnp.zeros_like(refs[-1])

    chip_lands = [lax.empty((3,) + s.shape[1:], s.dtype) for s in sums]
    pair_lands = [lax.empty((NSH, g.shape[1] // 2, g.shape[2]), g.dtype) for g in grads]
    arrays = list(sums) + chip_lands + list(grads) + pair_lands
    res = pl.pallas_call(
        body, name=name,
        out_shape=(pltpu.SemaphoreType.DMA((3 * n1,)), pltpu.SemaphoreType.DMA((3 * n1,)), pltpu.SemaphoreType.DMA((n2,)),
                   pltpu.SemaphoreType.DMA((n2,)))
        + tuple(pltpu.HBM(a.shape, a.dtype) for a in arrays) + (_sds((8, 128), F32),),
        in_specs=[HBM] * len(arrays), out_specs=(SEM,) * 4 + (HBM,) * len(arrays) + (pl.BlockSpec(memory_space=pltpu.VMEM),),
        input_output_aliases={i: 4 + i for i in range(len(arrays))},
        compiler_params=pltpu.CompilerParams(has_side_effects=ORDERED_EFFECT))(*[_in_hbm(a) for a in arrays])
    thru = list(res[4:4 + len(arrays)])
    chip = (res[0], res[1], thru[:n1], thru[n1:2 * n1])
    pair = (res[2], res[3], thru[2 * n1:2 * n1 + n2], thru[2 * n1 + n2:])
    return chip, pair, res[-1]


def chip_exchange_start(sums, *, name):
    n = len(sums)

    def body(*refs):
        for cp in _chip_exchange_copies(refs[:n], refs[n:2 * n], refs[2 * n], refs[2 * n + 1]):
            cp.start()
        refs[-1][...] = jnp.zeros_like(refs[-1])

    lands = [lax.empty((3,) + s.shape[1:], s.dtype) for s in sums]
    res = pl.pallas_call(
        body, name=name,
        out_shape=(pltpu.SemaphoreType.DMA((3 * n,)), pltpu.SemaphoreType.DMA((3 * n,)))
        + tuple(pltpu.HBM(s.shape, s.dtype) for s in sums) + tuple(pltpu.HBM(l.shape, l.dtype) for l in lands)
        + (_sds((8, 128), F32),),
        in_specs=[HBM] * (2 * n), out_specs=(SEM, SEM) + (HBM,) * (2 * n) + (pl.BlockSpec(memory_space=pltpu.VMEM),),
        input_output_aliases={i: 2 + i for i in range(2 * n)},
        compiler_params=pltpu.CompilerParams(has_side_effects=ORDERED_EFFECT))(
            *[_in_hbm(s) for s in sums], *[_in_hbm(l) for l in lands])
    return res[0], res[1], list(res[2:2 + n]), list(res[2 + n:2 + 2 * n]), res[-1]


def chip_exchange_wait(send_sems, recv_sems, sums, lands, after, *, name):
    n = len(sums)

    def body(*refs):
        for cp in _chip_exchange_copies(refs[:n], refs[n:2 * n], refs[2 * n], refs[2 * n + 1]):
            cp.wait_send()
            cp.wait_recv()

    res = pl.pallas_call(
        body, name=name,
        out_shape=tuple(pltpu.HBM(s.shape, s.dtype) for s in sums) + tuple(pltpu.HBM(l.shape, l.dtype) for l in lands),
        in_specs=[HBM] * (2 * n) + [SEM, SEM] + [ANY] * len(after), out_specs=(HBM,) * (2 * n),
        input_output_aliases={i: i for i in range(2 * n)},
        compiler_params=pltpu.CompilerParams(has_side_effects=ORDERED_EFFECT))(*sums, *lands, send_sems, recv_sems, *after)
    return list(res[:n]), list(res[n:])


def _pair_send_copies(h, got, send_sems, recv_sems):
    x, y, c, _ = _place()
    return [_remote(h[i], got[i], send_sems.at[i], recv_sems.at[i], (x, y, 1 - c)) for i in range(len(h))]


def pair_send_start(halves, *, name):
    n = len(halves)

    def body(*refs):
        for cp in _pair_send_copies(refs[:n], refs[n:2 * n], refs[2 * n], refs[2 * n + 1]):
            cp.start()
        refs[-1][...] = jnp.zeros_like(refs[-1])

    lands = [lax.empty(h.shape, h.dtype) for h in halves]
    res = pl.pallas_call(
        body, name=name,
        out_shape=(pltpu.SemaphoreType.DMA((n,)), pltpu.SemaphoreType.DMA((n,)))
        + tuple(pltpu.HBM(h.shape, h.dtype) for h in halves) * 2 + (_sds((8, 128), F32),),
        in_specs=[HBM] * (2 * n), out_specs=(SEM, SEM) + (HBM,) * (2 * n) + (pl.BlockSpec(memory_space=pltpu.VMEM),),
        input_output_aliases={i: 2 + i for i in range(2 * n)},
        compiler_params=pltpu.CompilerParams(has_side_effects=ORDERED_EFFECT))(
            *[_in_hbm(h) for h in halves], *[_in_hbm(l) for l in lands])
    return res[0], res[1], list(res[2:2 + n]), list(res[2 + n:2 + 2 * n]), res[-1]


def pair_send_wait(send_sems, recv_sems, halves, lands, after, *, name):
    n = len(halves)

    def body(*refs):
        for cp in _pair_send_copies(refs[:n], refs[n:2 * n], refs[2 * n], refs[2 * n + 1]):
            cp.wait_send()
            cp.wait_recv()

    res = pl.pallas_call(
        body, name=name, out_shape=tuple(pltpu.HBM(h.shape, h.dtype) for h in halves) * 2,
        in_specs=[HBM] * (2 * n) + [SEM, SEM] + [ANY] * len(after), out_specs=(HBM,) * (2 * n),
        input_output_aliases={i: i for i in range(2 * n)},
        compiler_params=pltpu.CompilerParams(has_side_effects=ORDERED_EFFECT))(*halves, *lands, send_sems, recv_sems, *after)
    return list(res[:n]), list(res[n:])


def allreduce_small(v, *, name):
    rows = v.shape[0]

    def body(v_ref, o_ref, gath, send_sems, recv_sems):
        x, y, c, _ = _place()
        me = 4 * x + 2 * y + c
        gath[me] = v_ref[...]
        copies = []
        for k in range(1, 8):
            fx, fy, fc = (k >> 2) & 1, (k >> 1) & 1, k & 1
            peer = (jnp.where(fx, 1 - x, x), jnp.where(fy, 1 - y, y), jnp.where(fc, 1 - c, c))
            cp = _remote(v_ref, gath.at[me], send_sems.at[k - 1], recv_sems.at[k - 1], peer)
            cp.start()
            copies.append(cp)
        for cp in copies:
            cp.wait()
        acc = gath[0]
        for d in range(1, 8):
            acc = acc + gath[d]
        o_ref[...] = acc

    return pl.pallas_call(
        body, name=name, out_shape=_sds(v.shape, F32),
        in_specs=[pl.BlockSpec(memory_space=pltpu.VMEM)], out_specs=pl.BlockSpec(memory_space=pltpu.VMEM),
        scratch_shapes=[pltpu.VMEM((8, rows, 128), F32), pltpu.SemaphoreType.DMA((7,)), pltpu.SemaphoreType.DMA((7,))])(v)


def _same_shape_runs(arrays):
    runs = {}
    for i, a in enumerate(arrays):
        runs.setdefault(a.shape, []).append(i)
    return list(runs.values())


def _per_shape(fn, *lists):
    out = [None] * len(lists[0])
    for idx in _same_shape_runs(lists[0]):
        for i, r in zip(idx, fn(*[[l[i] for i in idx] for l in lists])):
            out[i] = r
    return out


def add_halves(gs, bufs, cidx, *, name):
    cnt = len(gs)
    _, k, n = gs[0].shape

    def body(c_ref, *refs):
        g, b, o = refs[:cnt], refs[cnt:2 * cnt], refs[2 * cnt:]
        for i in range(cnt):
            o[i][...] = (g[i][...].astype(F32) + b[i][...].astype(F32)).astype(BF16)

    blk = pl.BlockSpec((None, k // 2, n), lambda s, c: (s, 0, 0))
    mine = pl.BlockSpec((None, k // 2, n), lambda s, c: (s, c[0], 0))
    return list(pl.pallas_call(
        body, name=name, out_shape=tuple(_sds(b.shape, BF16) for b in bufs),
        grid_spec=pltpu.PrefetchScalarGridSpec(
            num_scalar_prefetch=1, grid=(NSH,), in_specs=[mine] * cnt + [blk] * cnt, out_specs=tuple([blk] * cnt)),
        compiler_params=_params(("parallel",)))(cidx, *gs, *bufs))


def add_chips(sums, bufs, sidx, *, name):
    cnt = len(sums)
    _, kh, n = sums[0].shape

    def body(s_ref, *refs):
        mine, b, o = refs[:cnt], refs[cnt:2 * cnt], refs[2 * cnt:]
        for i in range(cnt):
            o[i][...] = ((mine[i][...].astype(F32) + b[i][0].astype(F32)) + (b[i][1].astype(F32) + b[i][2].astype(F32)))

    own = pl.BlockSpec((None, kh, n), lambda i, s: (s[0], 0, 0))
    got = pl.BlockSpec((3, kh, n), lambda i, s: (0, 0, 0))
    out = pl.BlockSpec((kh, n), lambda i, s: (0, 0))
    return list(pl.pallas_call(
        body, name=name, out_shape=tuple(_sds((kh, n), F32) for _ in sums),
        grid_spec=pltpu.PrefetchScalarGridSpec(
            num_scalar_prefetch=1, grid=(1,), in_specs=[own] * cnt + [got] * cnt, out_specs=tuple([out] * cnt)),
        compiler_params=_params(("arbitrary",)))(sidx, *sums, *bufs))


PARAMS = ("ffn1_norm", "ffn1_w_gate", "ffn1_w_up", "ffn1_w_down", "mix_norm", "w_in", "b_gate", "na_q_norm", "na_k_norm",
          "na_rpb", "sw_q_norm", "sw_k_norm", "sw_sink", "t5_rel_table", "w_branch_na", "w_branch_sw", "w_out", "ffn2_norm",
          "ffn2_w_gate", "ffn2_w_up", "ffn2_w_down")
SMALL_ALL = tuple(n for n in PARAMS if n not in BIG)
TRANSPOSED = ("ffn1_w_gate", "ffn1_w_up", "w_in", "ffn2_w_gate", "ffn2_w_up")
SMALL_ROWS = 152


def _pack_small(vals):
    flat = jnp.concatenate([vals[n].reshape(-1).astype(F32) for n in SMALL_ALL] + [vals["loss"].reshape(-1)])
    return jnp.pad(flat, (0, SMALL_ROWS * 128 - flat.shape[0])).reshape(SMALL_ROWS, 128)


def _unpack_small(packed, like):
    flat, out, off = packed.reshape(-1), {}, 0
    for n in SMALL_ALL:
        size = math.prod(like[n].shape)
        out[n] = flat[off:off + size].reshape(like[n].shape)
        off += size
    out["loss"] = flat[off]
    return out


def kernel(x, ffn1_norm, ffn1_w_gate, ffn1_w_up, ffn1_w_down, mix_norm, w_in, b_gate, na_q_norm, na_k_norm, na_rpb, sw_q_norm, sw_k_norm, sw_sink, t5_rel_table, w_branch_na, w_branch_sw, w_out, ffn2_norm, ffn2_w_gate, ffn2_w_up, ffn2_w_down, loss_target, m_ffn1_norm, m_ffn1_w_gate, m_ffn1_w_up, m_ffn1_w_down, m_mix_norm, m_w_in, m_b_gate, m_na_q_norm, m_na_k_norm, m_na_rpb, m_sw_q_norm, m_sw_k_norm, m_sw_sink, m_t5_rel_table, m_w_branch_na, m_w_branch_sw, m_w_out, m_ffn2_norm, m_ffn2_w_gate, m_ffn2_w_up, m_ffn2_w_down, v_ffn1_norm, v_ffn1_w_gate, v_ffn1_w_up, v_ffn1_w_down, v_mix_norm, v_w_in, v_b_gate, v_na_q_norm, v_na_k_norm, v_na_rpb, v_sw_q_norm, v_sw_k_norm, v_sw_sink, v_t5_rel_table, v_w_branch_na, v_w_branch_sw, v_w_out, v_ffn2_norm, v_ffn2_w_gate, v_ffn2_w_up, v_ffn2_w_down):
    args = locals()
    tr = lambda n, a: jnp.transpose(a, (0, 2, 1)) if n in TRANSPOSED else a
    w = {n: tr(n, args[n]) for n in PARAMS}
    m = {n: tr(n, args["m_" + n]) for n in PARAMS}
    v = {n: tr(n, args["v_" + n]) for n in PARAMS}
    cidx = lax.axis_index("c").astype(jnp.int32).reshape(1)
    sidx = (2 * lax.axis_index("x") + lax.axis_index("y")).astype(jnp.int32).reshape(1)

    small = [{n: w[n][l] for n in SMALL} for l in range(DEPTH)]
    order = ("ffn1", "mix", "ffn2")

    keys = [(l, g) for l in range(DEPTH) for g in order]
    local = lambda l, g: [w[n][l].astype(BF16) for n in GROUPS[g]]
    first = gather_start([local(*keys[0])], name="gather_start")
    rest = gather_start([local(*key) for key in keys[1:]], first[0][2][0], name="gather_start")
    in_flight = dict(zip(keys, first + rest))
    t5b = t5_bias(w["t5_rel_table"], name="t5_bias")
    for l in range(DEPTH):
        small[l]["na_bias"] = na_bias_table(small[l]["na_rpb"], name="na_bias_table")
    early = [t5b] + [small[l]["na_bias"] for l in range(DEPTH)] + [rest[0][2][0]]

    def weights_of(l):
        def get(group, after):
            send_sems, recv_sems, thru, lands = in_flight[(l, group)]
            after = [after] + (early if (l, group) == keys[0] else [])
            thru, lands = gather_wait(send_sems, recv_sems, thru, lands, after, name="gather_wait")
            return dict(zip(GROUPS[group], gather_finish(thru, lands, name="gather_finish")))
        return get

    h0, saved0 = layer_fwd(x[0], small[0], weights_of(0), t5b)
    h1, saved1 = layer_fwd(h0, small[1], weights_of(1), t5b)
    dy, dy_bf, loss_row = loss_head(h1, loss_target[0], name="loss_head")

    crossing, tokens, pending = {}, [], []

    def ship(after, then=None):
        key, send_sems, recv_sems, grads, lands = pending.pop()
        grads, from_sibling = pair_exchange_wait(send_sems, recv_sems, grads, lands, after, name="pair_exchange_wait")
        sums = _per_shape(lambda gs, bs: add_halves(gs, bs, cidx, name="add_halves"), grads, from_sibling)
        if then is None:
            send_sems, recv_sems, sums, lands, token = chip_exchange_start(sums, name="chip_exchange_start")
            crossing[key] = (send_sems, recv_sems, sums, lands)
            return token
        crossing[key], pair, token = exchange_start(sums, then[1], name="exchange_start")
        pending.append((then[0],) + pair)
        return token

    def reduce_of(l):
        def emit(group, grads):
            grads = list(grads)
            if pending:
                token = ship([grads[0]], then=((l, group), grads))
            else:
                send_sems, recv_sems, grads, lands, token = pair_exchange_start(grads, name="pair_exchange_start")
                pending.append(((l, group), send_sems, recv_sems, grads, lands))
            tokens.append(token)
            return token
        return emit

    def finish(layer, after, filled=None):
        sent = {}
        for group in order:
            send_sems, recv_sems, sums, lands = crossing[(layer, group)]
            sums, got = chip_exchange_wait(send_sems, recv_sems, sums, lands, after, name="chip_exchange_wait")
            halves = _per_shape(lambda ss, bs: add_chips(ss, bs, sidx, name="add_chips"), sums, got)
            sent[group] = pair_send_start(halves, name="pair_send_start")
            after = [sent[group][4]]
        out = {}
        for group in order:
            send_sems, recv_sems, halves, lands, _ = sent[group]
            halves, theirs = pair_send_wait(send_sems, recv_sems, halves, lands, after, name="pair_send_wait")
            names = GROUPS[group]
            res = _per_shape(
                lambda ws, ms, vs, a, b, *f: adamw_layer(ws, ms, vs, a, b, cidx, layer, list(f[0]) if f else None, name="adamw_layer"),
                *([[w[n] for n in names], [m[n] for n in names], [v[n] for n in names], halves, theirs]
                  + ([[filled[n] for n in names]] if filled is not None else [])))
            out.update(zip(names, res))
            after = [res[-1][0]]
        return out

    dy, dy_bf, small1, dt5_1 = layer_bwd(dy, dy_bf, saved1, small[1], t5b, reduce_of(1))
    grad_x, _, small0, dt5_0 = layer_bwd(dy, dy_bf, saved0, small[0], t5b, reduce_of(0), dep=tokens[-1])
    done1 = finish(1, [ship([grad_x])])

    smalls = [small0, small1]
    dt5 = t5_table_grad(dt5_0, dt5_1, name="t5_table_grad").reshape(32, 8)
    local_small = {n: jnp.stack([smalls[l][n].reshape(w[n].shape[1:]) for l in range(DEPTH)]) for n in SMALL}
    local_small["t5_rel_table"] = dt5
    local_small["loss"] = loss_row[0, 0:1]
    total = allreduce_small(_pack_small(local_small), name="allreduce_small")
    small_grads = _unpack_small(total, w)
    small_done = adamw_small([w[n] for n in SMALL_ALL], [small_grads[n] for n in SMALL_ALL], [m[n] for n in SMALL_ALL],
                             [v[n] for n in SMALL_ALL], name="adamw_small")

    grad, delta, new_m, new_v = {}, {}, {}, {}
    for n, done in finish(0, [small_done[0][0], done1[BIG[-1]][0]], filled=done1).items():
        grad[n], delta[n], new_m[n], new_v[n] = done
    for n, done in zip(SMALL_ALL, small_done):
        grad[n] = small_grads[n]
        delta[n], new_m[n], new_v[n] = done

    return (small_grads["loss"], grad_x[None], *[tr(n, grad[n]) for n in PARAMS], *[tr(n, delta[n]) for n in PARAMS],
            *[tr(n, new_m[n]) for n in PARAMS], *[tr(n, new_v[n]) for n in PARAMS])
```

```python
import math

import jax
import jax.numpy as jnp
import numpy as np
from jax import lax
from jax.experimental import pallas as pl
from jax.experimental.pallas import tpu as pltpu

F32 = jnp.float32
BF16 = jnp.bfloat16

SEQ = 2048
DM = 1024
DFF = 2816
DEPTH = 2
NSH = 4
FSH = DFF // NSH
GRID_W = 64
ROWS = SEQ // GRID_W
NA_HEADS = 8
HD = 64
NA_WR = 8
NA_WC = 16
NA_KEYS = NA_WR * GRID_W
SW_BLK = 128
SW_NB = SEQ // SW_BLK
SW_KEYS = 3 * SW_BLK
ATT_W = 2304
GATE_W = 2048
IN_W = ATT_W + GATE_W
EPS = 1e-6
NEG = -1e30
QK_SCALE = 1.0 / math.sqrt(HD)

ADAM_LR = 0.001
ADAM_B1 = 0.9
ADAM_B2 = 0.999
ADAM_EPS = 1e-08
ADAM_WD = 0.01
ADAM_STEP = 10

VMEM_LIMIT = 56 << 20
MESH = pl.DeviceIdType.MESH

NT = (((1,), (1,)), ((), ()))
TN = (((0,), (0,)), ((), ()))
NN = (((1,), (0,)), ((), ()))


def _dot(a, b, dims=NN):
    return lax.dot_general(a, b, dims, preferred_element_type=F32)


def _params(sem=None):
    return pltpu.CompilerParams(dimension_semantics=sem, vmem_limit_bytes=VMEM_LIMIT)


def _sds(shape, dtype):
    return jax.ShapeDtypeStruct(shape, dtype)


def mm(a, b, *, name, ta=False, tb=False, out_dtype=F32, add=None, scale=None, tm=512, tn=None, tk=None, exact=False,
       dep=None, b_rows=None):
    m, kd = (a.shape[1], a.shape[0]) if ta else a.shape
    if b_rows is None:
        n = b.shape[0] if tb else b.shape[1]
    else:
        n = b_rows[1] if tb else b.shape[1]
        assert tb or (b_rows[1] == kd and (tk or kd) == kd)
    tm, tn, tk = min(tm, m), min(tn or n, n), min(tk or kd, kd)
    nk = kd // tk
    dims = (((0 if ta else 1,), (1 if tb else 0,)), ((), ()))

    def body(*refs):
        a_ref, b_ref = refs[:2]
        add_ref = refs[2] if add is not None else None
        o_ref = refs[-1] if nk == 1 else refs[-2]
        if b_rows is None:
            bv = b_ref[...]
        elif tb:
            bv = b_ref[pl.ds(pl.multiple_of(b_rows[0] + pl.program_id(1) * tn, 16), tn), :]
        else:
            bv = b_ref[b_rows[0]:b_rows[0] + b_rows[1], :]
        if exact:
            part = lax.dot_general(a_ref[...], bv, dims, precision=lax.Precision.HIGHEST, preferred_element_type=F32)
        else:
            part = lax.dot_general(a_ref[...].astype(BF16), bv.astype(BF16), dims, preferred_element_type=F32)

        def finish(r):
            if scale is not None:
                r = r * scale
            if add is not None:
                r = r + add_ref[...]
            o_ref[...] = r.astype(out_dtype)

        if nk == 1:
            finish(part)
        else:
            acc, k = refs[-1], pl.program_id(2)

            @pl.when(k == 0)
            def _():
                acc[...] = part

            @pl.when(k != 0)
            def _():
                acc[...] += part

            pl.when(k == nk - 1)(lambda: finish(acc[...]))

    a_spec = pl.BlockSpec((tk, tm), lambda i, j, k: (k, i)) if ta else pl.BlockSpec((tm, tk), lambda i, j, k: (i, k))
    if b_rows is not None:
        b_spec = pl.BlockSpec(b.shape, lambda i, j, k: (0, 0), pipeline_mode=pl.Buffered(1))
    else:
        b_spec = pl.BlockSpec((tn, tk), lambda i, j, k: (j, k)) if tb else pl.BlockSpec((tk, tn), lambda i, j, k: (k, j))
    o_spec = pl.BlockSpec((tm, tn), lambda i, j, k: (i, j))
    ins, specs = [a, b], [a_spec, b_spec]
    if add is not None:
        ins.append(add)
        specs.append(o_spec)
    if dep is not None:
        ins.append(dep)
        specs.append(pl.BlockSpec(memory_space=pl.ANY))
    return pl.pallas_call(
        body, name=name, out_shape=_sds((m, n), out_dtype), grid=(m // tm, n // tn, nk), in_specs=specs,
        out_specs=o_spec, scratch_shapes=[] if nk == 1 else [pltpu.VMEM((tm, tn), F32)],
        compiler_params=_params(("parallel", "parallel", "arbitrary")))(*ins)


def _rms(x):
    return lax.rsqrt(jnp.mean(x * x, axis=-1, keepdims=True) + EPS)


def rms_fwd(x, gain, *, name, tm=512):
    def body(x_ref, g_ref, h_ref):
        x = x_ref[...]
        h_ref[...] = (x * _rms(x) * g_ref[...]).astype(BF16)

    return pl.pallas_call(
        body, name=name, out_shape=_sds(x.shape, BF16), grid=(x.shape[0] // tm,),
        in_specs=[pl.BlockSpec((tm, DM), lambda i: (i, 0)), pl.BlockSpec((1, DM), lambda i: (0, 0))],
        out_specs=pl.BlockSpec((tm, DM), lambda i: (i, 0)), compiler_params=_params(("parallel",)))(x, gain)


def _rms_bwd_math(dh, x, gain):
    r = _rms(x)
    xh = x * r
    dgain = jnp.sum(dh * xh, axis=0, keepdims=True)
    dxn = dh * gain
    dx = r * (dxn - xh * jnp.mean(dxn * xh, axis=-1, keepdims=True))
    return dx, dgain


def mixer_input_bwd(dz, dzg, w_in_t, x, gain, dres, *, name, tm=512, dep=None):
    def body(dz_ref, dzg_ref, w_ref, x_ref, g_ref, dres_ref, *rest):
        dx_ref, dxb_ref, dg_ref = rest[-3:]

        @pl.when(pl.program_id(0) == 0)
        def _():
            dg_ref[...] = jnp.zeros_like(dg_ref)

        dh = _dot(dz_ref[...], w_ref[0:ATT_W, :]) + _dot(dzg_ref[...], w_ref[ATT_W:IN_W, :])
        dx, dg = _rms_bwd_math(dh, x_ref[...], g_ref[...])
        dx = dres_ref[...] + dx
        dx_ref[...] = dx
        dxb_ref[...] = dx.astype(BF16)
        dg_ref[...] += dg

    s = x.shape[0]
    tile = pl.BlockSpec((tm, DM), lambda i: (i, 0))
    vec = pl.BlockSpec((1, DM), lambda i: (0, 0))
    ins, specs = _with_dep(
        [dz, dzg, w_in_t, x, gain, dres],
        [pl.BlockSpec((tm, ATT_W), lambda i: (i, 0)), pl.BlockSpec((tm, GATE_W), lambda i: (i, 0)),
         pl.BlockSpec((IN_W, DM), lambda i: (0, 0), pipeline_mode=pl.Buffered(1)), tile, vec, tile], dep)
    return pl.pallas_call(
        body, name=name, out_shape=(_sds((s, DM), F32), _sds((s, DM), BF16), _sds((1, DM), F32)), grid=(s // tm,),
        in_specs=specs, out_specs=(tile, tile, vec), compiler_params=_params(("arbitrary",)))(*ins)


def _with_dep(ins, specs, dep):
    if dep is None:
        return ins, specs
    return ins + [dep], specs + [pl.BlockSpec(memory_space=pl.ANY)]


def _resident_weight():
    return pl.BlockSpec((DFF, DM), lambda i: (0, 0), pipeline_mode=pl.Buffered(1))


def ffn_fwd(x, gain, wg, wu, wd, *, name, tm=512):
    def body(x_ref, g_ref, wg_ref, wu_ref, wd_ref, y_ref, h_ref, gg_ref, uu_ref):
        x = x_ref[...]
        h = (x * _rms(x) * g_ref[...]).astype(BF16)
        h_ref[...] = h
        gg = _dot(h, wg_ref[...], NT)
        uu = _dot(h, wu_ref[...], NT)
        gg_ref[...] = gg.astype(BF16)
        uu_ref[...] = uu.astype(BF16)
        act = (gg * jax.nn.sigmoid(gg) * uu).astype(BF16)
        y_ref[...] = x + 0.5 * _dot(act, wd_ref[...])

    s = x.shape[0]
    tile = pl.BlockSpec((tm, DM), lambda i: (i, 0))
    hid = pl.BlockSpec((tm, DFF), lambda i: (i, 0))
    w = _resident_weight()
    return pl.pallas_call(
        body, name=name,
        out_shape=(_sds((s, DM), F32), _sds((s, DM), BF16), _sds((s, DFF), BF16), _sds((s, DFF), BF16)),
        grid=(s // tm,), in_specs=[tile, pl.BlockSpec((1, DM), lambda i: (0, 0)), w, w, w],
        out_specs=(tile, tile, hid, hid), compiler_params=_params(("parallel",)))(x, gain, wg, wu, wd)


def ffn_bwd_tokens(dy, x, gain, gg, uu, wg, wu, wd, *, name, tm=256, dep=None):
    def body(dy_ref, x_ref, g_ref, gg_ref, uu_ref, wg_ref, wu_ref, wd_ref, *rest):
        dx_ref, dxb_ref, dgain_ref, act_ref, dg_ref, du_ref = rest[-6:]

        @pl.when(pl.program_id(0) == 0)
        def _():
            dgain_ref[...] = jnp.zeros_like(dgain_ref)

        dy = dy_ref[...]
        dact = _dot((0.5 * dy).astype(BF16), wd_ref[...], NT)
        g = gg_ref[...].astype(F32)
        u = uu_ref[...].astype(F32)
        sg = jax.nn.sigmoid(g)
        silu = g * sg
        act_ref[...] = (silu * u).astype(BF16)
        dg = (dact * u * (sg * (1.0 + g * (1.0 - sg)))).astype(BF16)
        du = (dact * silu).astype(BF16)
        dg_ref[...] = dg
        du_ref[...] = du
        dx, dgain = _rms_bwd_math(_dot(dg, wg_ref[...]) + _dot(du, wu_ref[...]), x_ref[...], g_ref[...])
        dx = dy + dx
        dx_ref[...] = dx
        dxb_ref[...] = dx.astype(BF16)
        dgain_ref[...] += dgain

    s = x.shape[0]
    tile = pl.BlockSpec((tm, DM), lambda i: (i, 0))
    vec = pl.BlockSpec((1, DM), lambda i: (0, 0))
    hid = pl.BlockSpec((tm, DFF), lambda i: (i, 0))
    hshape = _sds((s, DFF), BF16)
    w = _resident_weight()
    ins, specs = _with_dep([dy, x, gain, gg, uu, wg, wu, wd], [tile, tile, vec, hid, hid, w, w, w], dep)
    return pl.pallas_call(
        body, name=name, out_shape=(_sds((s, DM), F32), _sds((s, DM), BF16), _sds((1, DM), F32), hshape, hshape, hshape),
        grid=(s // tm,), in_specs=specs, out_specs=(tile, tile, vec, hid, hid, hid),
        compiler_params=_params(("arbitrary",)))(*ins)


def ffn_bwd_weights(h, dy, act, dg, du, *, name, tf=256):
    def body(h_ref, dy_ref, act_ref, dg_ref, du_ref, gwg_ref, gwu_ref, gwd_ref):
        h = h_ref[...]
        gwg_ref[...] = _dot(dg_ref[...], h, TN).astype(BF16)
        gwu_ref[...] = _dot(du_ref[...], h, TN).astype(BF16)
        gwd_ref[...] = (0.5 * _dot(act_ref[...], dy_ref[...], TN)).astype(BF16)

    s = h.shape[0]
    full = pl.BlockSpec((s, DM), lambda f: (0, 0))
    hid = pl.BlockSpec((s, tf), lambda f: (0, f))
    wt = pl.BlockSpec((tf, DM), lambda f: (f, 0))
    wshape = _sds((DFF, DM), BF16)
    return pl.pallas_call(
        body, name=name, out_shape=(wshape, wshape, wshape), grid=(DFF // tf,), in_specs=[full, full, hid, hid, hid],
        out_specs=(wt, wt, wt), compiler_params=_params(("parallel",)))(h, dy, act, dg, du)


def _group_mean(v, bd):
    hi = v.astype(BF16)
    lo = (v - hi.astype(F32)).astype(BF16)
    return _dot(hi, bd) + _dot(lo, bd)


def _block_diag(width):
    idx = np.arange(width) // HD
    return jnp.asarray((idx[:, None] == idx[None, :]).astype(np.float32) / HD, dtype=BF16)


def qknorm_fwd(z, gq_na, gk_na, gq_sw, gk_sw, *, name, tm=256):
    def body(zq_ref, zk_ref, zv_ref, zs_ref, zkv_ref, gqa_ref, gka_ref, gqs_ref, gks_ref, bd_ref, bd2_ref,
             qa_ref, ka_ref, va_ref, qs_ref, kv_ref):
        bd = bd_ref[...]

        def norm(x, g, bdm):
            x = x.astype(F32)
            return x * lax.rsqrt(_group_mean(x * x, bdm) + EPS) * g

        qa_ref[...] = (norm(zq_ref[...], gqa_ref[...], bd) * QK_SCALE).astype(BF16)
        ka_ref[...] = norm(zk_ref[...], gka_ref[...], bd).astype(BF16)
        va_ref[...] = zv_ref[...].astype(BF16)
        qs_ref[...] = (norm(zs_ref[...], gqs_ref[...], bd) * QK_SCALE).astype(BF16)
        kv = zkv_ref[...]
        kv_ref[:, 0:128] = norm(kv[:, 0:128], gks_ref[...], bd2_ref[...]).astype(BF16)
        kv_ref[:, 128:256] = kv[:, 128:256].astype(BF16)

    s = z.shape[0]
    col = lambda j: pl.BlockSpec((tm, 512), lambda i, j=j: (i, j))
    vec = lambda w: pl.BlockSpec((1, w), lambda i: (0, 0))
    o512 = pl.BlockSpec((tm, 512), lambda i: (i, 0))
    g512 = lambda g: jnp.tile(g.reshape(1, HD), (1, 8))
    return pl.pallas_call(
        body, name=name,
        out_shape=(_sds((s, 512), BF16),) * 4 + (_sds((s, 256), BF16),), grid=(s // tm,),
        in_specs=[col(0), col(1), col(2), col(3), pl.BlockSpec((tm, 256), lambda i: (i, 8)), vec(512), vec(512), vec(512),
                  vec(128), pl.BlockSpec((512, 512), lambda i: (0, 0)), pl.BlockSpec((128, 128), lambda i: (0, 0))],
        out_specs=(o512, o512, o512, o512, pl.BlockSpec((tm, 256), lambda i: (i, 0))),
        compiler_params=_params(("parallel",)))(
            z, z, z, z, z, g512(gq_na), g512(gk_na), g512(gq_sw), jnp.tile(gk_sw.reshape(1, HD), (1, 2)),
            _block_diag(512), _block_diag(128))


def qknorm_bwd(z, dqa, dka, dva, dqs, dkv, gq_na, gk_na, gq_sw, gk_sw, *, name, tm=256):
    def body(zq_ref, zk_ref, zs_ref, zkv_ref, dqa_ref, dka_ref, dva_ref, dqs_ref, dkv_ref, gqa_ref, gka_ref, gqs_ref,
             gks_ref, bd_ref, bd2_ref, dz_ref, dgqa_ref, dgka_ref, dgqs_ref, dgks_ref):
        @pl.when(pl.program_id(0) == 0)
        def _():
            dgqa_ref[...] = jnp.zeros_like(dgqa_ref)
            dgka_ref[...] = jnp.zeros_like(dgka_ref)
            dgqs_ref[...] = jnp.zeros_like(dgqs_ref)
            dgks_ref[...] = jnp.zeros_like(dgks_ref)

        bd = bd_ref[...]

        def bwd(x, dy, g, bdm, dg_ref):
            x = x.astype(F32)
            r = lax.rsqrt(_group_mean(x * x, bdm) + EPS)
            xh = x * r
            dg_ref[...] += jnp.sum(dy * xh, axis=0, keepdims=True)
            dxn = dy * g
            return r * (dxn - xh * _group_mean(dxn * xh, bdm))

        dz_ref[:, 0:512] = bwd(zq_ref[...], dqa_ref[...] * QK_SCALE, gqa_ref[...], bd, dgqa_ref).astype(BF16)
        dz_ref[:, 512:1024] = bwd(zk_ref[...], dka_ref[...], gka_ref[...], bd, dgka_ref).astype(BF16)
        dz_ref[:, 1024:1536] = dva_ref[...].astype(BF16)
        dz_ref[:, 1536:2048] = bwd(zs_ref[...], dqs_ref[...] * QK_SCALE, gqs_ref[...], bd, dgqs_ref).astype(BF16)
        dkv = dkv_ref[...]
        dz_ref[:, 2048:2176] = bwd(zkv_ref[:, 0:128], dkv[:, 0:128], gks_ref[...], bd2_ref[...], dgks_ref).astype(BF16)
        dz_ref[:, 2176:2304] = dkv[:, 128:256].astype(BF16)

    s = z.shape[0]
    col = lambda j: pl.BlockSpec((tm, 512), lambda i, j=j: (i, j))
    t512 = pl.BlockSpec((tm, 512), lambda i: (i, 0))
    t256 = pl.BlockSpec((tm, 256), lambda i: (i, 0))
    vec = lambda w: pl.BlockSpec((1, w), lambda i: (0, 0))
    g512 = lambda g: jnp.tile(g.reshape(1, HD), (1, 8))
    return pl.pallas_call(
        body, name=name,
        out_shape=(_sds((s, ATT_W), BF16), _sds((1, 512), F32), _sds((1, 512), F32), _sds((1, 512), F32), _sds((1, 128), F32)),
        grid=(s // tm,),
        in_specs=[col(0), col(1), col(3), pl.BlockSpec((tm, 256), lambda i: (i, 8)), t512, t512, t512, t512, t256,
                  vec(512), vec(512), vec(512), vec(128), pl.BlockSpec((512, 512), lambda i: (0, 0)),
                  pl.BlockSpec((128, 128), lambda i: (0, 0))],
        out_specs=(pl.BlockSpec((tm, ATT_W), lambda i: (i, 0)), vec(512), vec(512), vec(512), vec(128)),
        compiler_params=_params(("arbitrary",)))(
            z, z, z, z, dqa, dka, dva, dqs, dkv, g512(gq_na), g512(gk_na), g512(gq_sw),
            jnp.tile(gk_sw.reshape(1, HD), (1, 2)), _block_diag(512), _block_diag(128))


def _na_row_start(r):
    return jnp.clip(r - NA_WR // 2, 0, ROWS - NA_WR)


def na_bias_table(rpb, *, name):
    t = jnp.pad(rpb, ((0, 0), (0, 2), (0, HD - (2 * NA_WC - 1))))
    pairs = jnp.concatenate([t[:, :16], t[:, 1:17]], axis=-1).reshape(NA_HEADS, 16, 1, 128)

    def body(t_ref, o_ref):
        p = pl.program_id(0)
        q = lax.broadcasted_iota(jnp.int32, (GRID_W, 128), 0)
        kc = lax.broadcasted_iota(jnp.int32, (GRID_W, 128), 1) & (GRID_W - 1)
        cs = jnp.clip(q - NA_WC // 2, 0, GRID_W - NA_WC)
        ok = (kc >= cs) & (kc < cs + NA_WC)
        for h in range(NA_HEADS):
            for pr in range(NA_WR // 2):
                x = jnp.broadcast_to(t_ref[h, 2 * pr - p + NA_WR - 1], (GRID_W, 128))
                b = pltpu.roll(x, 128 - (NA_WC - 1), 1, stride=1, stride_axis=0)
                o_ref[h, :, 128 * pr:128 * pr + 128] = jnp.where(ok, b, NEG)

    return pl.pallas_call(
        body, name=name, out_shape=_sds((NA_WR, NA_HEADS, GRID_W, NA_KEYS), F32), grid=(NA_WR,),
        in_specs=[pl.BlockSpec((NA_HEADS, 16, 1, 128), lambda p: (0, 0, 0, 0))],
        out_specs=pl.BlockSpec((None, NA_HEADS, GRID_W, NA_KEYS), lambda p: (p, 0, 0, 0)),
        compiler_params=_params(("parallel",)))(pairs)


def _lane_halves():
    lane = lax.broadcasted_iota(jnp.int32, (1, 128), 1)
    return lane < HD


def na_fwd(q, k, v, bias, *, name):
    def body(q_ref, k_ref, v_ref, b_ref, o_ref, lse_ref):
        r = pl.program_id(0)
        off = pl.multiple_of(_na_row_start(r) * GRID_W, GRID_W)
        first = _lane_halves()
        sels = [first, jnp.logical_not(first)]
        lanes = [slice(128 * j, 128 * j + 128) for j in range(NA_HEADS // 2)]
        q2s = [q_ref[:, l] for l in lanes]
        k2s = [k_ref[pl.ds(off, NA_KEYS), l] for l in lanes]
        v2s = [v_ref[pl.ds(off, NA_KEYS), l] for l in lanes]
        scores = []
        for h in range(NA_HEADS):
            j, half = divmod(h, 2)
            scores.append(_dot(jnp.where(sels[half], q2s[j], jnp.zeros_like(q2s[j])), k2s[j], NT))
        probs, lses = [], []
        for h in range(NA_HEADS):
            b = b_ref[h]
            s = jnp.where(b > 0.5 * NEG, scores[h] + b, NEG)
            m = jnp.max(s, axis=-1, keepdims=True)
            e = jnp.exp(s - m)
            l = jnp.sum(e, axis=-1, keepdims=True)
            probs.append((e / l).astype(BF16))
            lses.append(m + jnp.log(l))
        for j in range(NA_HEADS // 2):
            zero = jnp.zeros_like(v2s[j])
            o2 = (_dot(probs[2 * j], jnp.where(sels[0], v2s[j], zero))
                  + _dot(probs[2 * j + 1], jnp.where(sels[1], v2s[j], zero)))
            o_ref[:, lanes[j]] = o2.astype(BF16)
        lse_ref[...] = jnp.concatenate(lses, axis=1)

    s_tok = q.shape[0]
    full = pl.BlockSpec((s_tok, 512), lambda r: (0, 0))
    return pl.pallas_call(
        body, name=name, out_shape=(_sds((s_tok, 512), BF16), _sds((s_tok, NA_HEADS), F32)), grid=(ROWS,),
        in_specs=[pl.BlockSpec((GRID_W, 512), lambda r: (r, 0)), full, full,
                  pl.BlockSpec((None, NA_HEADS, GRID_W, NA_KEYS), lambda r: (r - _na_row_start(r), 0, 0, 0))],
        out_specs=(pl.BlockSpec((GRID_W, 512), lambda r: (r, 0)), pl.BlockSpec((GRID_W, NA_HEADS), lambda r: (r, 0))),
        compiler_params=_params(("parallel",)))(q, k, v, bias)


def na_bwd(q, k, v, o, do, lse, bias, *, name):
    def body(q_ref, k_ref, v_ref, o_ref, do_ref, lse_ref, b_ref, dq_ref, dk_ref, dv_ref, db_ref):
        r = pl.program_id(0)

        @pl.when(r == 0)
        def _():
            dk_ref[...] = jnp.zeros_like(dk_ref)
            dv_ref[...] = jnp.zeros_like(dv_ref)

        @pl.when((r <= NA_WR // 2) | (r > ROWS - NA_WR // 2))
        def _():
            db_ref[...] = jnp.zeros_like(db_ref)

        off = pl.multiple_of(_na_row_start(r) * GRID_W, GRID_W)
        first = _lane_halves()
        sels = [first, jnp.logical_not(first)]
        lanes = [slice(128 * j, 128 * j + 128) for j in range(NA_HEADS // 2)]
        q2s = [q_ref[:, l] for l in lanes]
        k2s = [k_ref[pl.ds(off, NA_KEYS), l] for l in lanes]
        v2s = [v_ref[pl.ds(off, NA_KEYS), l] for l in lanes]
        do2s = [do_ref[:, l] for l in lanes]
        prods = [do2s[j].astype(F32) * o_ref[:, lanes[j]].astype(F32) for j in range(NA_HEADS // 2)]
        lse = lse_ref[...]
        qhs, dohs, scores, dps = [], [], [], []
        for h in range(NA_HEADS):
            j, half = divmod(h, 2)
            qhs.append(jnp.where(sels[half], q2s[j], jnp.zeros_like(q2s[j])))
            dohs.append(jnp.where(sels[half], do2s[j], jnp.zeros_like(do2s[j])))
            scores.append(_dot(qhs[h], k2s[j], NT))
            dps.append(_dot(dohs[h], v2s[j], NT))
        pbs, dsbs = [], []
        for h in range(NA_HEADS):
            j, half = divmod(h, 2)
            b = b_ref[h]
            s = jnp.where(b > 0.5 * NEG, scores[h] + b, NEG)
            p = jnp.exp(s - lse[:, h:h + 1])
            delta = jnp.sum(jnp.where(sels[half], prods[j], 0.0), axis=-1, keepdims=True)
            ds = p * (dps[h] - delta)
            db_ref[h] += ds
            pbs.append(p.astype(BF16))
            dsbs.append(ds.astype(BF16))
        for j in range(NA_HEADS // 2):
            a, b = 2 * j, 2 * j + 1
            zero = jnp.zeros_like(k2s[j])
            dq_ref[:, lanes[j]] = (_dot(dsbs[a], jnp.where(sels[0], k2s[j], zero))
                                   + _dot(dsbs[b], jnp.where(sels[1], k2s[j], zero)))
            dk_ref[pl.ds(off, NA_KEYS), lanes[j]] += _dot(dsbs[a], qhs[a], TN) + _dot(dsbs[b], qhs[b], TN)
            dv_ref[pl.ds(off, NA_KEYS), lanes[j]] += _dot(pbs[a], dohs[a], TN) + _dot(pbs[b], dohs[b], TN)

    s_tok = q.shape[0]
    full = pl.BlockSpec((s_tok, 512), lambda r: (0, 0))
    row = pl.BlockSpec((GRID_W, 512), lambda r: (r, 0))
    bias_spec = pl.BlockSpec((None, NA_HEADS, GRID_W, NA_KEYS), lambda r: (r - _na_row_start(r), 0, 0, 0))
    return pl.pallas_call(
        body, name=name,
        out_shape=(_sds((s_tok, 512), F32), _sds((s_tok, 512), F32), _sds((s_tok, 512), F32),
                   _sds((NA_WR, NA_HEADS, GRID_W, NA_KEYS), F32)),
        grid=(ROWS,),
        in_specs=[row, full, full, row, row, pl.BlockSpec((GRID_W, NA_HEADS), lambda r: (r, 0)), bias_spec],
        out_specs=(row, full, full, bias_spec), compiler_params=_params(("arbitrary",)))(q, k, v, o, do, lse, bias)


def t5_bucket_map():
    rel = np.arange(SW_KEYS)[None, :] - SW_BLK - np.arange(SW_BLK)[:, None]
    nb = 16
    max_exact = nb // 2
    n = np.abs(rel)
    large = max_exact + (np.log(np.maximum(n, 1) / max_exact) / np.log(128 / max_exact) * (nb - max_exact)).astype(np.int32)
    large = np.minimum(large, nb - 1)
    return ((rel > 0) * nb + np.where(n < max_exact, n, large)).astype(np.int32)


def t5_bias(table, *, name):
    rel = np.arange(-SW_BLK, SW_BLK + 1)
    nb, max_exact = 16, 8
    n = np.abs(rel)
    large = max_exact + (np.log(np.maximum(n, 1) / max_exact) / np.log(128 / max_exact) * (nb - max_exact)).astype(np.int32)
    bucket = ((rel > 0) * nb + np.where(n < max_exact, n, np.minimum(large, nb - 1))).astype(np.int32)
    u = jnp.pad(table[jnp.asarray(bucket)].T, ((0, 0), (0, SW_KEYS - bucket.shape[0]))).reshape(8, 1, SW_KEYS)

    def body(u_ref, o_ref):
        for h in range(8):
            x = jnp.broadcast_to(u_ref[h], (SW_BLK, SW_KEYS))
            o_ref[h] = pltpu.roll(x, 0, 1, stride=1, stride_axis=0)

    return pl.pallas_call(body, name=name, out_shape=_sds((8, SW_BLK, SW_KEYS), F32), compiler_params=_params())(u)


def _sw_valid(n):
    a = lax.broadcasted_iota(jnp.int32, (SW_BLK, SW_KEYS), 0)
    j = lax.broadcasted_iota(jnp.int32, (SW_BLK, SW_KEYS), 1)
    kpos = (n - 1) * SW_BLK + j
    return (jnp.abs(j - SW_BLK - a) <= SW_BLK) & (kpos >= 0) & (kpos < SEQ)


def _dup_group(x2, g, first):
    rolled = pltpu.roll(x2, HD, 1)
    return jnp.where(first, x2, rolled) if g == 0 else jnp.where(first, rolled, x2)


def sw_fwd(q, kv, t5, sink, *, name):
    def body(q_ref, kv_ref, t5_ref, sink_ref, o_ref, lse_ref):
        n = pl.program_id(0)
        off = pl.multiple_of(n * SW_BLK, SW_BLK)
        first = _lane_halves()
        sels = [first, jnp.logical_not(first)]
        valid = _sw_valid(n)
        k2 = kv_ref[pl.ds(off, SW_KEYS), 0:128]
        v2 = kv_ref[pl.ds(off, SW_KEYS), 128:256]
        kk = [_dup_group(k2, g, first) for g in range(2)]
        vv = [_dup_group(v2, g, first) for g in range(2)]
        q2s = [q_ref[:, 128 * j:128 * j + 128] for j in range(4)]
        scores = []
        for h in range(8):
            j, half = divmod(h, 2)
            scores.append(_dot(jnp.where(sels[half], q2s[j], jnp.zeros_like(q2s[j])), kk[j // 2], NT))
        probs, lses = [], []
        for h in range(8):
            s = jnp.where(valid, scores[h] + t5_ref[h], NEG)
            snk = sink_ref[h]
            m = jnp.maximum(jnp.max(s, axis=-1, keepdims=True), snk)
            e = jnp.exp(s - m)
            den = jnp.sum(e, axis=-1, keepdims=True) + jnp.exp(snk - m)
            probs.append((e / den).astype(BF16))
            lses.append(m + jnp.log(den))
        outs = []
        for j in range(4):
            vg = vv[j // 2]
            zero = jnp.zeros_like(vg)
            outs.append(_dot(probs[2 * j], jnp.where(sels[0], vg, zero)) + _dot(probs[2 * j + 1], jnp.where(sels[1], vg, zero)))
        o_ref[...] = jnp.concatenate(outs, axis=1).astype(BF16)
        lse_ref[...] = jnp.concatenate(lses, axis=1)

    s_tok = q.shape[0]
    blk = pl.BlockSpec((SW_BLK, 512), lambda n: (n, 0))
    return pl.pallas_call(
        body, name=name, out_shape=(_sds((s_tok, 512), BF16), _sds((s_tok, 8), F32)), grid=(SW_NB,),
        in_specs=[blk, pl.BlockSpec(kv.shape, lambda n: (0, 0)), pl.BlockSpec((8, SW_BLK, SW_KEYS), lambda n: (0, 0, 0)),
                  pl.BlockSpec(memory_space=pltpu.SMEM)],
        out_specs=(blk, pl.BlockSpec((SW_BLK, 8), lambda n: (n, 0))), compiler_params=_params(("parallel",)))(q, kv, t5, sink)


def sw_bwd(q, kv, o, do, lse, t5, sink, *, name):
    def body(q_ref, kv_ref, o_ref, do_ref, lse_ref, t5_ref, sink_ref, dq_ref, dkv_ref, dt5_ref, dsink_ref):
        n = pl.program_id(0)

        @pl.when(n == 0)
        def _():
            dkv_ref[...] = jnp.zeros_like(dkv_ref)
            dt5_ref[...] = jnp.zeros_like(dt5_ref)
            dsink_ref[...] = jnp.zeros_like(dsink_ref)

        off = pl.multiple_of(n * SW_BLK, SW_BLK)
        first = _lane_halves()
        sels = [first, jnp.logical_not(first)]
        valid = _sw_valid(n)
        k2 = kv_ref[pl.ds(off, SW_KEYS), 0:128]
        v2 = kv_ref[pl.ds(off, SW_KEYS), 128:256]
        kk = [_dup_group(k2, g, first) for g in range(2)]
        vv = [_dup_group(v2, g, first) for g in range(2)]
        lanes = [slice(128 * j, 128 * j + 128) for j in range(4)]
        q2s = [q_ref[:, l] for l in lanes]
        do2s = [do_ref[:, l] for l in lanes]
        prods = [do2s[j].astype(F32) * o_ref[:, lanes[j]].astype(F32) for j in range(4)]
        lse = lse_ref[...]
        qhs, dohs, scores, dps = [], [], [], []
        for h in range(8):
            j, half = divmod(h, 2)
            qhs.append(jnp.where(sels[half], q2s[j], jnp.zeros_like(q2s[j])))
            dohs.append(jnp.where(sels[half], do2s[j], jnp.zeros_like(do2s[j])))
            scores.append(_dot(qhs[h], kk[j // 2], NT))
            dps.append(_dot(dohs[h], vv[j // 2], NT))
        pbs, dsbs, dss, dsinks = [], [], [], []
        for h in range(8):
            j, half = divmod(h, 2)
            s = jnp.where(valid, scores[h] + t5_ref[h], NEG)
            lse_h = lse[:, h:h + 1]
            p = jnp.exp(s - lse_h)
            delta = jnp.sum(jnp.where(sels[half], prods[j], 0.0), axis=-1, keepdims=True)
            ds = p * (dps[h] - delta)
            dss.append(ds)
            dsinks.append(-jnp.sum(jnp.exp(sink_ref[h] - lse_h) * delta, axis=0, keepdims=True))
            pbs.append(p.astype(BF16))
            dsbs.append(ds.astype(BF16))
        dt5_ref[...] += jnp.stack(dss)
        dsink_ref[...] += jnp.concatenate(dsinks, axis=1)
        dqs = []
        for j in range(4):
            a, b = 2 * j, 2 * j + 1
            zero = jnp.zeros_like(kk[j // 2])
            dqs.append(_dot(dsbs[a], jnp.where(sels[0], kk[j // 2], zero)) + _dot(dsbs[b], jnp.where(sels[1], kk[j // 2], zero)))
        dq_ref[...] = jnp.concatenate(dqs, axis=1)
        dk_groups, dv_groups = [], []
        for g in range(2):
            dkk = sum(_dot(dsbs[h], qhs[h], TN) for h in range(4 * g, 4 * g + 4))
            dvv = sum(_dot(pbs[h], dohs[h], TN) for h in range(4 * g, 4 * g + 4))
            dk_groups.append(dkk + pltpu.roll(dkk, HD, 1))
            dv_groups.append(dvv + pltpu.roll(dvv, HD, 1))
        dkv_ref[pl.ds(off, SW_KEYS), :] += jnp.concatenate(
            [jnp.where(first, dk_groups[0], dk_groups[1]), jnp.where(first, dv_groups[0], dv_groups[1])], axis=1)

    s_tok = q.shape[0]
    blk = pl.BlockSpec((SW_BLK, 512), lambda n: (n, 0))
    kv_spec = pl.BlockSpec(kv.shape, lambda n: (0, 0))
    t5_spec = pl.BlockSpec((8, SW_BLK, SW_KEYS), lambda n: (0, 0, 0))
    vec = pl.BlockSpec((1, 8), lambda n: (0, 0))
    return pl.pallas_call(
        body, name=name,
        out_shape=(_sds((s_tok, 512), F32), _sds(kv.shape, F32), _sds((8, SW_BLK, SW_KEYS), F32), _sds((1, 8), F32)),
        grid=(SW_NB,), in_specs=[blk, kv_spec, blk, blk, pl.BlockSpec((SW_BLK, 8), lambda n: (n, 0)), t5_spec,
                                 pl.BlockSpec(memory_space=pltpu.SMEM)],
        out_specs=(blk, kv_spec, t5_spec, vec), compiler_params=_params(("arbitrary",)))(q, kv, o, do, lse, t5, sink)


def gate_fwd(zg, bias, pa, ps, *, name, tm=512):
    def body(z0_ref, z1_ref, b0_ref, b1_ref, pa_ref, ps_ref, m_ref):
        g0 = jax.nn.sigmoid(z0_ref[...] + b0_ref[...])
        g1 = jax.nn.sigmoid(z1_ref[...] + b1_ref[...])
        m_ref[...] = (g0 * pa_ref[...] + g1 * ps_ref[...]).astype(BF16)

    s = zg.shape[0]
    half = lambda j: pl.BlockSpec((tm, DM), lambda i, j=j: (i, j))
    bvec = lambda j: pl.BlockSpec((1, DM), lambda i, j=j: (0, j))
    return pl.pallas_call(
        body, name=name, out_shape=_sds((s, DM), BF16), grid=(s // tm,),
        in_specs=[half(0), half(1), bvec(0), bvec(1), half(0), half(0)], out_specs=half(0),
        compiler_params=_params(("parallel",)))(zg, zg, bias, bias, pa, ps)


def gate_bwd(dm, zg, bias, pa, ps, *, name, tm=512):
    def body(dm_ref, z0_ref, z1_ref, b0_ref, b1_ref, pa_ref, ps_ref, dpa_ref, dps_ref, dz_ref, db_ref):
        @pl.when(pl.program_id(0) == 0)
        def _():
            db_ref[...] = jnp.zeros_like(db_ref)

        dm = dm_ref[...]
        g0 = jax.nn.sigmoid(z0_ref[...] + b0_ref[...])
        g1 = jax.nn.sigmoid(z1_ref[...] + b1_ref[...])
        dpa_ref[...] = (dm * g0).astype(BF16)
        dps_ref[...] = (dm * g1).astype(BF16)
        dz0 = dm * pa_ref[...] * g0 * (1.0 - g0)
        dz1 = dm * ps_ref[...] * g1 * (1.0 - g1)
        dz_ref[:, 0:DM] = dz0.astype(BF16)
        dz_ref[:, DM:2 * DM] = dz1.astype(BF16)
        db_ref[:, 0:DM] += jnp.sum(dz0, axis=0, keepdims=True)
        db_ref[:, DM:2 * DM] += jnp.sum(dz1, axis=0, keepdims=True)

    s = zg.shape[0]
    half = lambda j: pl.BlockSpec((tm, DM), lambda i, j=j: (i, j))
    bvec = lambda j: pl.BlockSpec((1, DM), lambda i, j=j: (0, j))
    return pl.pallas_call(
        body, name=name,
        out_shape=(_sds((s, DM), BF16), _sds((s, DM), BF16), _sds((s, GATE_W), BF16), _sds((1, GATE_W), F32)),
        grid=(s // tm,), in_specs=[half(0), half(0), half(1), bvec(0), bvec(1), half(0), half(0)],
        out_specs=(half(0), half(0), pl.BlockSpec((tm, GATE_W), lambda i: (i, 0)), pl.BlockSpec((1, GATE_W), lambda i: (0, 0))),
        compiler_params=_params(("arbitrary",)))(dm, zg, zg, bias, bias, pa, ps)


def loss_head(y, target, *, name, tm=512):
    def body(y_ref, t_ref, dy_ref, dyb_ref, l_ref):
        @pl.when(pl.program_id(0) == 0)
        def _():
            l_ref[...] = jnp.zeros_like(l_ref)

        err = y_ref[...] - t_ref[...]
        dy = err * (1.0 / DM)
        dy_ref[...] = dy
        dyb_ref[...] = dy.astype(BF16)
        l_ref[...] += 0.5 * jnp.sum(jnp.mean(err * err, axis=-1, keepdims=True), axis=0, keepdims=True)

    s = y.shape[0]
    tile = pl.BlockSpec((tm, DM), lambda i: (i, 0))
    return pl.pallas_call(
        body, name=name, out_shape=(_sds((s, DM), F32), _sds((s, DM), BF16), _sds((1, 128), F32)), grid=(s // tm,),
        in_specs=[tile, tile], out_specs=(tile, tile, pl.BlockSpec((1, 128), lambda i: (0, 0))),
        compiler_params=_params(("arbitrary",)))(y, target)


def adamw_small(ws, gs, ms, vs, *, name):
    cnt = len(ws)

    def body(*refs):
        ins, outs = refs[:4 * cnt], refs[4 * cnt:]
        for i in range(cnt):
            w_ref, g_ref, m_ref, v_ref = ins[4 * i:4 * i + 4]
            d_ref, nm_ref, nv_ref = outs[3 * i:3 * i + 3]
            g = g_ref[...]
            nm = ADAM_B1 * m_ref[...] + (1.0 - ADAM_B1) * g
            nv = ADAM_B2 * v_ref[...] + (1.0 - ADAM_B2) * jnp.square(g)
            m_hat = nm / (1.0 - ADAM_B1 ** ADAM_STEP)
            v_hat = nv / (1.0 - ADAM_B2 ** ADAM_STEP)
            d_ref[...] = -ADAM_LR * (m_hat / (jnp.sqrt(v_hat) + ADAM_EPS) + ADAM_WD * w_ref[...])
            nm_ref[...] = nm
            nv_ref[...] = nv

    flat = [a for i in range(cnt) for a in (ws[i], gs[i], ms[i], vs[i])]
    res = pl.pallas_call(
        body, name=name, out_shape=tuple(_sds(ws[i].shape, F32) for i in range(cnt) for _ in range(3)),
        compiler_params=_params())(*flat)
    return [tuple(res[3 * i:3 * i + 3]) for i in range(cnt)]


def adamw_layer(ws, ms, vs, mines, theirs, cidx, layer, filled=None, *, name):
    cnt = len(ws)
    _, k, n = ws[0].shape
    nt = 2
    tk = k // 2 // nt

    def body(c_ref, *refs):
        own = pl.program_id(0) == c_ref[0]
        outs = refs[-4 * cnt:]
        for i in range(cnt):
            w_ref, m_ref, v_ref, a_ref, b_ref = refs[5 * i:5 * i + 5]
            g_ref, d_ref, nm_ref, nv_ref = outs[4 * i:4 * i + 4]
            g = jnp.where(own, a_ref[...], b_ref[...])
            g_ref[...] = g
            nm = ADAM_B1 * m_ref[...] + (1.0 - ADAM_B1) * g
            nv = ADAM_B2 * v_ref[...] + (1.0 - ADAM_B2) * jnp.square(g)
            m_hat = nm / (1.0 - ADAM_B1 ** ADAM_STEP)
            v_hat = nv / (1.0 - ADAM_B2 ** ADAM_STEP)
            d_ref[...] = -ADAM_LR * (m_hat / (jnp.sqrt(v_hat) + ADAM_EPS) + ADAM_WD * w_ref[...])
            nm_ref[...] = nm
            nv_ref[...] = nv

    full = pl.BlockSpec((None, tk, n), lambda hf, t, c: (layer, hf * nt + t, 0))
    half_mine = pl.BlockSpec((tk, n), lambda hf, t, c: (jnp.where(hf == c[0], t, 0), 0))
    half_theirs = pl.BlockSpec((tk, n), lambda hf, t, c: (jnp.where(hf != c[0], t, 0), 0))
    out = _sds(ws[0].shape, F32)
    ins, specs, aliases = [cidx], [], {}
    for i in range(cnt):
        ins += [ws[i], ms[i], vs[i], mines[i], theirs[i]]
        specs += [full, full, full, half_mine, half_theirs]
    if filled is not None:
        aliases = {len(ins) + j: j for j in range(4 * cnt)}
        ins += [a for f in filled for a in f]
        specs += [pl.BlockSpec(memory_space=pl.ANY)] * (4 * cnt)
    res = pl.pallas_call(
        body, name=name, out_shape=(out,) * (4 * cnt),
        grid_spec=pltpu.PrefetchScalarGridSpec(
            num_scalar_prefetch=1, grid=(2, nt), in_specs=specs, out_specs=(full,) * (4 * cnt)),
        input_output_aliases=aliases,
        compiler_params=_params(("arbitrary", "arbitrary")))(*ins)
    return [tuple(res[4 * i:4 * i + 4]) for i in range(cnt)]


def t5_table_grad(dt5_a, dt5_b, *, name):
    def body(a_ref, b_ref, map_ref, o_ref):
        d = a_ref[...] + b_ref[...]
        bucket = map_ref[...]
        for b in range(32):
            hit = (bucket == b)[None]
            o_ref[b] = jnp.sum(jnp.sum(jnp.where(hit, d, 0.0), axis=2), axis=1, keepdims=True)

    return pl.pallas_call(
        body, name=name, out_shape=_sds((32, 8, 1), F32), compiler_params=_params())(
            dt5_a, dt5_b, jnp.asarray(t5_bucket_map()))


def rpb_grad(dbias, *, name):
    def body(d_ref, rev_ref, o_ref):
        rev = rev_ref[...]
        for h in range(NA_HEADS):
            for pr in range(NA_WR // 2):
                d = d_ref[h, :, 128 * pr:128 * pr + 128]
                hi = d.astype(BF16)
                lo = (d - hi.astype(F32)).astype(BF16)
                flipped = _dot(rev, hi) + _dot(rev, lo)
                o_ref[h, pr] = jnp.sum(pltpu.roll(flipped, 0, 1, stride=1, stride_axis=0), axis=0, keepdims=True)

    anti = jnp.asarray(np.eye(GRID_W, dtype=np.float32)[::-1], dtype=BF16)
    e = pl.pallas_call(
        body, name=name, out_shape=_sds((NA_WR, NA_HEADS, NA_WR // 2, 1, 128), F32), grid=(NA_WR,),
        in_specs=[pl.BlockSpec((None, NA_HEADS, GRID_W, NA_KEYS), lambda p: (p, 0, 0, 0)),
                  pl.BlockSpec((GRID_W, GRID_W), lambda p: (0, 0))],
        out_specs=pl.BlockSpec((None, NA_HEADS, NA_WR // 2, 1, 128), lambda p: (p, 0, 0, 0, 0)),
        compiler_params=_params(("parallel",)))(dbias, anti)
    nci, nri = 2 * NA_WC - 1, 2 * NA_WR - 1
    e = e.reshape(NA_WR, NA_HEADS, NA_WR // 2, 128).transpose(0, 2, 1, 3).reshape(NA_WR * NA_WR // 2, NA_HEADS, 128)
    parts = jnp.concatenate([e[..., 48:48 + nci], jnp.concatenate([e[..., 112:128], e[..., 0:nci - 16]], axis=-1)], axis=0)
    p, pr = np.arange(NA_WR)[:, None], np.arange(NA_WR // 2)[None, :]
    ri = np.concatenate([(2 * pr - p + NA_WR - 1).reshape(-1), (2 * pr - p + NA_WR).reshape(-1)])
    pick = jnp.asarray((ri[None, :] == np.arange(16)[:, None]).astype(np.float32))
    out = mm(pick, parts.reshape(2 * NA_WR * NA_WR // 2, NA_HEADS * nci), name=name + "_rows", exact=True)
    return out.reshape(16, NA_HEADS, nci)[:nri].transpose(1, 0, 2)


BIG = ("ffn1_w_gate", "ffn1_w_up", "ffn1_w_down", "w_in", "w_branch_na", "w_branch_sw", "w_out",
       "ffn2_w_gate", "ffn2_w_up", "ffn2_w_down")
SMALL = ("ffn1_norm", "mix_norm", "b_gate", "na_q_norm", "na_k_norm", "na_rpb", "sw_q_norm", "sw_k_norm", "sw_sink",
         "ffn2_norm")


def _cols_to_full(w4):
    return w4.transpose(1, 0, 2).reshape(w4.shape[1], NSH * w4.shape[2])


def _full_to_cols(w):
    return w.reshape(w.shape[0], NSH, w.shape[1] // NSH).transpose(1, 0, 2)


def _mixer_weights(g):
    w_in_t = g["w_in"].reshape(IN_W, DM)
    return dict(w_in_t=w_in_t, wa=_cols_to_full(g["w_branch_na"]),
                ws=_cols_to_full(g["w_branch_sw"]), wo=g["w_out"].reshape(DM, DM))


GROUPS = {"ffn1": ("ffn1_w_gate", "ffn1_w_up", "ffn1_w_down"), "mix": ("w_in", "w_branch_na", "w_branch_sw", "w_out"),
          "ffn2": ("ffn2_w_gate", "ffn2_w_up", "ffn2_w_down")}


def layer_fwd(x, p, weights, t5b):
    row = lambda v: v.reshape(1, -1)
    stacked = lambda g: {n: a.reshape(DFF, DM) for n, a in g.items()}
    g1 = stacked(weights("ffn1", x))
    y1, h1, gg1, uu1 = ffn_fwd(x, row(p["ffn1_norm"]), g1["ffn1_w_gate"], g1["ffn1_w_up"], g1["ffn1_w_down"], name="ffn_fwd")
    w = _mixer_weights(weights("mix", y1))
    hm = rms_fwd(y1, row(p["mix_norm"]), name="mix_norm_fwd")
    z = mm(hm, w["w_in_t"], tb=True, b_rows=(0, ATT_W), out_dtype=BF16, name="proj_att", tm=SEQ, tn=768)
    zg = mm(hm, w["w_in_t"], tb=True, b_rows=(ATT_W, GATE_W), out_dtype=BF16, name="proj_gate", tm=SEQ, tn=512)
    qa, ka, va, qs, kv = qknorm_fwd(z, p["na_q_norm"], p["na_k_norm"], p["sw_q_norm"], p["sw_k_norm"], name="qknorm_fwd")
    bias = p["na_bias"]
    o_na, lse_na = na_fwd(qa, ka, va, bias, name="na_fwd")
    kvp = jnp.pad(kv, ((SW_BLK, SW_BLK), (0, 0)))
    sink = p["sw_sink"]
    o_sw, lse_sw = sw_fwd(qs, kvp, t5b, sink, name="sw_fwd")
    pa = mm(o_na, w["wa"], out_dtype=BF16, name="branch_na", tm=1024)
    ps = mm(o_sw, w["ws"], out_dtype=BF16, name="branch_sw", tm=1024)
    merged = gate_fwd(zg, row(p["b_gate"]), pa, ps, name="gate_fwd")
    y2 = mm(merged, w["wo"], add=y1, name="out_proj", tm=1024)
    g2 = stacked(weights("ffn2", y2))
    y3, h2, gg2, uu2 = ffn_fwd(y2, row(p["ffn2_norm"]), g2["ffn2_w_gate"], g2["ffn2_w_up"], g2["ffn2_w_down"], name="ffn_fwd")
    saved = dict(x=x, y1=y1, h1=h1, gg1=gg1, uu1=uu1, hm=hm, z=z, zg=zg, qa=qa, ka=ka, va=va, qs=qs, kvp=kvp, bias=bias,
                 o_na=o_na, lse_na=lse_na, o_sw=o_sw, lse_sw=lse_sw, pa=pa, ps=ps, merged=merged, y2=y2, h2=h2, gg2=gg2,
                 uu2=uu2, w=w, sink=sink, g1=g1, g2=g2)
    return y3, saved


def layer_bwd(dy3, dy3_bf, sv, p, t5b, emit, dep=None):
    w, g1, g2 = sv["w"], sv["g1"], sv["g2"]
    row = lambda v: v.reshape(1, -1)
    fold = lambda v: v.reshape(-1, HD).sum(axis=0)
    small = {}
    dy2, _, small["ffn2_norm"], act, dg, du = ffn_bwd_tokens(
        dy3, sv["y2"], row(p["ffn2_norm"]), sv["gg2"], sv["uu2"], g2["ffn2_w_gate"], g2["ffn2_w_up"], g2["ffn2_w_down"],
        name="ffn_bwd_tokens", dep=dep)
    shards = lambda gs: [g.reshape(NSH, FSH, DM) for g in gs]
    token = emit("ffn2", shards(ffn_bwd_weights(sv["h2"], dy3_bf, act, dg, du, name="ffn_bwd_weights")))
    dmerged = mm(dy2, w["wo"], tb=True, name="out_proj_dx", tm=1024, dep=token)
    gw_out = mm(sv["merged"], dy2, ta=True, out_dtype=BF16, name="out_proj_dw").reshape(NSH, DM // NSH, DM)
    dpa, dps, dzg, small["b_gate"] = gate_bwd(dmerged, sv["zg"], row(p["b_gate"]), sv["pa"], sv["ps"], name="gate_bwd")
    gw_na = _full_to_cols(mm(sv["o_na"], dpa, ta=True, out_dtype=BF16, name="branch_dw"))
    gw_sw = _full_to_cols(mm(sv["o_sw"], dps, ta=True, out_dtype=BF16, name="branch_dw"))
    do_na = mm(dpa, w["wa"], tb=True, out_dtype=BF16, tm=SEQ, name="branch_dx")
    do_sw = mm(dps, w["ws"], tb=True, out_dtype=BF16, tm=SEQ, name="branch_dx")
    dqa, dka, dva, dbias = na_bwd(sv["qa"], sv["ka"], sv["va"], sv["o_na"], do_na, sv["lse_na"], sv["bias"], name="na_bwd")
    dqs, dkvp, dt5, dsink = sw_bwd(sv["qs"], sv["kvp"], sv["o_sw"], do_sw, sv["lse_sw"], t5b, sv["sink"], name="sw_bwd")
    dkv = dkvp[SW_BLK:SW_BLK + SEQ]
    dz, dgqa, dgka, dgqs, dgks = qknorm_bwd(sv["z"], dqa, dka, dva, dqs, dkv, p["na_q_norm"], p["na_k_norm"],
                                            p["sw_q_norm"], p["sw_k_norm"], name="qknorm_bwd")
    small["na_q_norm"], small["na_k_norm"], small["sw_q_norm"], small["sw_k_norm"] = fold(dgqa), fold(dgka), fold(dgqs), fold(dgks)
    small["na_rpb"] = rpb_grad(dbias, name="rpb_grad")
    small["sw_sink"] = dsink
    gw_att_t = mm(dz, sv["hm"], ta=True, out_dtype=BF16, tm=768, name="proj_att_dw")
    gw_gz_t = mm(dzg, sv["hm"], ta=True, out_dtype=BF16, tm=1024, name="proj_gate_dw")
    gw_in = jnp.concatenate([gw_att_t, gw_gz_t], axis=0).reshape(NSH, IN_W // NSH, DM)
    token = emit("mix", (gw_in, gw_na, gw_sw, gw_out))
    dy1, dy1_bf, small["mix_norm"] = mixer_input_bwd(dz, dzg, w["w_in_t"], sv["y1"], row(p["mix_norm"]), dy2,
                                                     name="mixer_input_bwd", dep=token)
    dx, dx_bf, small["ffn1_norm"], act, dg, du = ffn_bwd_tokens(
        dy1, sv["x"], row(p["ffn1_norm"]), sv["gg1"], sv["uu1"], g1["ffn1_w_gate"], g1["ffn1_w_up"], g1["ffn1_w_down"],
        name="ffn_bwd_tokens")
    emit("ffn1", shards(ffn_bwd_weights(sv["h1"], dy1_bf, act, dg, du, name="ffn_bwd_weights")))
    return dx, dx_bf, small, dt5


ANY = pl.BlockSpec(memory_space=pl.ANY)


def _place():
    x, y, c = lax.axis_index("x"), lax.axis_index("y"), lax.axis_index("c")
    chips = [(1 - x, y), (x, 1 - y), (1 - x, 1 - y)]
    return x, y, c, chips


def _remote(src, dst, send_sem, recv_sem, to):
    return pltpu.make_async_remote_copy(src_ref=src, dst_ref=dst, send_sem=send_sem, recv_sem=recv_sem, device_id=to,
                                        device_id_type=MESH)


HBM = pl.BlockSpec(memory_space=pltpu.HBM)
SEM = pl.BlockSpec(memory_space=pltpu.SEMAPHORE)
ORDERED_EFFECT = pltpu.SideEffectType.DATAFLOW_SIDE_EFFECTING


def _in_hbm(v):
    return pltpu.with_memory_space_constraint(v, pltpu.HBM)


def _row_half(ref_shape_rows, c):
    half = ref_shape_rows // 2
    return pl.ds(c * half, half)


def _ici_gather_copies(w, land, send_sems, recv_sems):
    x, y, c, chips = _place()
    me = 2 * x + y
    copies = []
    for a in range(len(w)):
        rows = _row_half(w[a].shape[0], c)
        for k, chip in enumerate(chips):
            copies.append(_remote(w[a].at[rows], land[a].at[me, rows], send_sems.at[4 * a + k], recv_sems.at[4 * a + k],
                                  (*chip, c)))
        copies.append(_remote(w[a], land[a].at[me], send_sems.at[4 * a + 3], recv_sems.at[4 * a + 3], (x, y, 1 - c)))
    return copies


def _d2d_gather_copies(w, land, send_sems, recv_sems):
    x, y, c, chips = _place()
    copies = []
    for a in range(len(w)):
        rows = _row_half(w[a].shape[0], c)
        for k, (cx, cy) in enumerate(chips):
            blk = land[a].at[2 * cx + cy, rows]
            copies.append(_remote(blk, blk, send_sems.at[3 * a + k], recv_sems.at[3 * a + k], (x, y, 1 - c)))
    return copies


def _d2d_gather_waits(w, land, send_sems, recv_sems):
    x, y, c, chips = _place()
    waits = []
    for a in range(len(w)):
        rows = _row_half(w[a].shape[0], 1 - c)
        for k, (cx, cy) in enumerate(chips):
            blk = land[a].at[2 * cx + cy, rows]
            waits.append(_remote(blk, blk, send_sems.at[3 * a + k], recv_sems.at[3 * a + k], (x, y, 1 - c)))
    return waits


def gather_start(groups, dep=None, *, name):
    sizes = [len(g) for g in groups]
    shards = [s for g in groups for s in g]
    n, ng = len(shards), len(groups)
    extra = [] if dep is None else [dep]

    def body(*refs):
        first_out = 2 * n + len(extra)
        w, land, sems = refs[:n], refs[n:2 * n], refs[first_out:first_out + 2 * ng]
        off = 0
        for gi, size in enumerate(sizes):
            for cp in _ici_gather_copies(w[off:off + size], land[off:off + size], sems[2 * gi], sems[2 * gi + 1]):
                cp.start()
            off += size

    lands = [lax.empty((NSH,) + s.shape, s.dtype) for s in shards]
    sem_shapes = tuple(pltpu.SemaphoreType.DMA((4 * size,)) for size in sizes for _ in range(2))
    res = pl.pallas_call(
        body, name=name,
        out_shape=sem_shapes + tuple(pltpu.HBM(s.shape, s.dtype) for s in shards) + tuple(pltpu.HBM(l.shape, l.dtype) for l in lands),
        in_specs=[HBM] * (2 * n) + [ANY] * len(extra), out_specs=(SEM,) * (2 * ng) + (HBM,) * (2 * n),
        input_output_aliases={i: 2 * ng + i for i in range(2 * n)},
        compiler_params=pltpu.CompilerParams(has_side_effects=ORDERED_EFFECT))(
            *[_in_hbm(s) for s in shards], *[_in_hbm(l) for l in lands], *extra)
    out, off = [], 0
    for gi, size in enumerate(sizes):
        out.append((res[2 * gi], res[2 * gi + 1], list(res[2 * ng + off:2 * ng + off + size]),
                    list(res[2 * ng + n + off:2 * ng + n + off + size])))
        off += size
    return out


def gather_wait(send_sems, recv_sems, shards, lands, after, *, name):
    n = len(shards)

    def body(*refs):
        w, land = refs[:n], refs[n:2 * n]
        send, recv = refs[2 * n:2 * n + 2]
        for cp in _ici_gather_copies(w, land, send, recv):
            cp.wait_send()
            cp.wait_recv()

    res = pl.pallas_call(
        body, name=name,
        out_shape=tuple(pltpu.HBM(s.shape, s.dtype) for s in shards) + tuple(pltpu.HBM(l.shape, l.dtype) for l in lands),
        in_specs=[HBM] * (2 * n) + [SEM, SEM] + [ANY] * len(after), out_specs=(HBM,) * (2 * n),
        input_output_aliases={i: i for i in range(2 * n)},
        compiler_params=pltpu.CompilerParams(has_side_effects=ORDERED_EFFECT))(*shards, *lands, send_sems, recv_sems, *after)
    return list(res[:n]), list(res[n:])


def gather_finish(shards, lands, *, name):
    n = len(shards)

    def body(*refs):
        w, land = refs[:n], refs[n:2 * n]
        send_sems, recv_sems = refs[3 * n:]
        d2d = _d2d_gather_copies(w, land, send_sems, recv_sems)
        for cp in d2d:
            cp.start()
        for cp in _d2d_gather_waits(w, land, send_sems, recv_sems):
            cp.wait_recv()
        for cp in d2d:
            cp.wait_send()

    return list(pl.pallas_call(
        body, name=name, out_shape=tuple(pltpu.HBM(l.shape, l.dtype) for l in lands),
        in_specs=[ANY] * (2 * n), out_specs=tuple([ANY] * n), input_output_aliases={n + i: i for i in range(n)},
        scratch_shapes=[pltpu.SemaphoreType.DMA((3 * n,)), pltpu.SemaphoreType.DMA((3 * n,))])(*shards, *lands))


def _pair_exchange_copies(g, buf, send_sems, recv_sems):
    x, y, c, _ = _place()
    copies = []
    for a in range(len(g)):
        half = g[a].shape[1] // 2
        copies.append(_remote(g[a].at[:, pl.ds((1 - c) * half, half)], buf[a], send_sems.at[a], recv_sems.at[a], (x, y, 1 - c)))
    return copies


def pair_exchange_start(grads, dep=None, *, name):
    n = len(grads)
    extra = [] if dep is None else [dep]

    def body(*refs):
        sems = refs[2 * n + len(extra):]
        for cp in _pair_exchange_copies(refs[:n], refs[n:2 * n], sems[0], sems[1]):
            cp.start()
        refs[-1][...] = jnp.zeros_like(refs[-1])

    lands = [lax.empty((NSH, g.shape[1] // 2, g.shape[2]), g.dtype) for g in grads]
    res = pl.pallas_call(
        body, name=name,
        out_shape=(pltpu.SemaphoreType.DMA((n,)), pltpu.SemaphoreType.DMA((n,)))
        + tuple(pltpu.HBM(g.shape, g.dtype) for g in grads) + tuple(pltpu.HBM(l.shape, l.dtype) for l in lands)
        + (_sds((8, 128), F32),),
        in_specs=[HBM] * (2 * n) + [ANY] * len(extra),
        out_specs=(SEM, SEM) + (HBM,) * (2 * n) + (pl.BlockSpec(memory_space=pltpu.VMEM),),
        input_output_aliases={i: 2 + i for i in range(2 * n)},
        compiler_params=pltpu.CompilerParams(has_side_effects=ORDERED_EFFECT))(
            *[_in_hbm(g) for g in grads], *[_in_hbm(l) for l in lands], *extra)
    return res[0], res[1], list(res[2:2 + n]), list(res[2 + n:2 + 2 * n]), res[-1]


def pair_exchange_wait(send_sems, recv_sems, grads, lands, after, *, name):
    n = len(grads)

    def body(*refs):
        for cp in _pair_exchange_copies(refs[:n], refs[n:2 * n], refs[2 * n], refs[2 * n + 1]):
            cp.wait_send()
            cp.wait_recv()

    res = pl.pallas_call(
        body, name=name,
        out_shape=tuple(pltpu.HBM(g.shape, g.dtype) for g in grads) + tuple(pltpu.HBM(l.shape, l.dtype) for l in lands),
        in_specs=[HBM] * (2 * n) + [SEM, SEM] + [ANY] * len(after), out_specs=(HBM,) * (2 * n),
        input_output_aliases={i: i for i in range(2 * n)},
        compiler_params=pltpu.CompilerParams(has_side_effects=ORDERED_EFFECT))(*grads, *lands, send_sems, recv_sems, *after)
    return list(res[:n]), list(res[n:])


def _chip_exchange_copies(s, buf, send_sems, recv_sems):
    x, y, c, chips = _place()
    return [_remote(s[a].at[2 * cx + cy], buf[a].at[k], send_sems.at[3 * a + k], recv_sems.at[3 * a + k], (cx, cy, c))
            for a in range(len(s)) for k, (cx, cy) in enumerate(chips)]


def exchange_start(sums, grads, *, name):
    n1, n2 = len(sums), len(grads)

    def body(*refs):
        first_out = 2 * (n1 + n2)
        chip = _chip_exchange_copies(refs[:n1], refs[n1:2 * n1], refs[first_out], refs[first_out + 1])
        pair = _pair_exchange_copies(refs[2 * n1:2 * n1 + n2], refs[2 * n1 + n2:first_out], refs[first_out + 2],
                                     refs[first_out + 3])
        for cp in chip + pair:
            cp.start()
        refs[-1][...] = jnp.zeros_like(refs[-1])

    chip_lands = [lax.empty((3,) + s.shape[1:], s.dtype) for s in sums]
    pair_lands = [lax.empty((NSH, g.shape[1] // 2, g.shape[2]), g.dtype) for g in grads]
    arrays = list(sums) + chip_lands + list(grads) + pair_lands
    res = pl.pallas_call(
        body, name=name,
        out_shape=(pltpu.SemaphoreType.DMA((3 * n1,)), pltpu.SemaphoreType.DMA((3 * n1,)), pltpu.SemaphoreType.DMA((n2,)),
                   pltpu.SemaphoreType.DMA((n2,)))
        + tuple(pltpu.HBM(a.shape, a.dtype) for a in arrays) + (_sds((8, 128), F32),),
        in_specs=[HBM] * len(arrays), out_specs=(SEM,) * 4 + (HBM,) * len(arrays) + (pl.BlockSpec(memory_space=pltpu.VMEM),),
        input_output_aliases={i: 4 + i for i in range(len(arrays))},
        compiler_params=pltpu.CompilerParams(has_side_effects=ORDERED_EFFECT))(*[_in_hbm(a) for a in arrays])
    thru = list(res[4:4 + len(arrays)])
    chip = (res[0], res[1], thru[:n1], thru[n1:2 * n1])
    pair = (res[2], res[3], thru[2 * n1:2 * n1 + n2], thru[2 * n1 + n2:])
    return chip, pair, res[-1]


def chip_exchange_start(sums, *, name):
    n = len(sums)

    def body(*refs):
        for cp in _chip_exchange_copies(refs[:n], refs[n:2 * n], refs[2 * n], refs[2 * n + 1]):
            cp.start()
        refs[-1][...] = jnp.zeros_like(refs[-1])

    lands = [lax.empty((3,) + s.shape[1:], s.dtype) for s in sums]
    res = pl.pallas_call(
        body, name=name,
        out_shape=(pltpu.SemaphoreType.DMA((3 * n,)), pltpu.SemaphoreType.DMA((3 * n,)))
        + tuple(pltpu.HBM(s.shape, s.dtype) for s in sums) + tuple(pltpu.HBM(l.shape, l.dtype) for l in lands)
        + (_sds((8, 128), F32),),
        in_specs=[HBM] * (2 * n), out_specs=(SEM, SEM) + (HBM,) * (2 * n) + (pl.BlockSpec(memory_space=pltpu.VMEM),),
        input_output_aliases={i: 2 + i for i in range(2 * n)},
        compiler_params=pltpu.CompilerParams(has_side_effects=ORDERED_EFFECT))(
            *[_in_hbm(s) for s in sums], *[_in_hbm(l) for l in lands])
    return res[0], res[1], list(res[2:2 + n]), list(res[2 + n:2 + 2 * n]), res[-1]


def chip_exchange_wait(send_sems, recv_sems, sums, lands, after, *, name):
    n = len(sums)

    def body(*refs):
        for cp in _chip_exchange_copies(refs[:n], refs[n:2 * n], refs[2 * n], refs[2 * n + 1]):
            cp.wait_send()
            cp.wait_recv()

    res = pl.pallas_call(
        body, name=name,
        out_shape=tuple(pltpu.HBM(s.shape, s.dtype) for s in sums) + tuple(pltpu.HBM(l.shape, l.dtype) for l in lands),
        in_specs=[HBM] * (2 * n) + [SEM, SEM] + [ANY] * len(after), out_specs=(HBM,) * (2 * n),
        input_output_aliases={i: i for i in range(2 * n)},
        compiler_params=pltpu.CompilerParams(has_side_effects=ORDERED_EFFECT))(*sums, *lands, send_sems, recv_sems, *after)
    return list(res[:n]), list(res[n:])


def _pair_send_copies(h, got, send_sems, recv_sems):
    x, y, c, _ = _place()
    return [_remote(h[i], got[i], send_sems.at[i], recv_sems.at[i], (x, y, 1 - c)) for i in range(len(h))]


def pair_send_start(halves, *, name):
    n = len(halves)

    def body(*refs):
        for cp in _pair_send_copies(refs[:n], refs[n:2 * n], refs[2 * n], refs[2 * n + 1]):
            cp.start()
        refs[-1][...] = jnp.zeros_like(refs[-1])

    lands = [lax.empty(h.shape, h.dtype) for h in halves]
    res = pl.pallas_call(
        body, name=name,
        out_shape=(pltpu.SemaphoreType.DMA((n,)), pltpu.SemaphoreType.DMA((n,)))
        + tuple(pltpu.HBM(h.shape, h.dtype) for h in halves) * 2 + (_sds((8, 128), F32),),
        in_specs=[HBM] * (2 * n), out_specs=(SEM, SEM) + (HBM,) * (2 * n) + (pl.BlockSpec(memory_space=pltpu.VMEM),),
        input_output_aliases={i: 2 + i for i in range(2 * n)},
        compiler_params=pltpu.CompilerParams(has_side_effects=ORDERED_EFFECT))(
            *[_in_hbm(h) for h in halves], *[_in_hbm(l) for l in lands])
    return res[0], res[1], list(res[2:2 + n]), list(res[2 + n:2 + 2 * n]), res[-1]


def pair_send_wait(send_sems, recv_sems, halves, lands, after, *, name):
    n = len(halves)

    def body(*refs):
        for cp in _pair_send_copies(refs[:n], refs[n:2 * n], refs[2 * n], refs[2 * n + 1]):
            cp.wait_send()
            cp.wait_recv()

    res = pl.pallas_call(
        body, name=name, out_shape=tuple(pltpu.HBM(h.shape, h.dtype) for h in halves) * 2,
        in_specs=[HBM] * (2 * n) + [SEM, SEM] + [ANY] * len(after), out_specs=(HBM,) * (2 * n),
        input_output_aliases={i: i for i in range(2 * n)},
        compiler_params=pltpu.CompilerParams(has_side_effects=ORDERED_EFFECT))(*halves, *lands, send_sems, recv_sems, *after)
    return list(res[:n]), list(res[n:])


def allreduce_small(v, *, name):
    rows = v.shape[0]

    def body(v_ref, o_ref, gath, send_sems, recv_sems):
        x, y, c, _ = _place()
        me = 4 * x + 2 * y + c
        gath[me] = v_ref[...]
        copies = []
        for k in range(1, 8):
            fx, fy, fc = (k >> 2) & 1, (k >> 1) & 1, k & 1
            peer = (jnp.where(fx, 1 - x, x), jnp.where(fy, 1 - y, y), jnp.where(fc, 1 - c, c))
            cp = _remote(v_ref, gath.at[me], send_sems.at[k - 1], recv_sems.at[k - 1], peer)
            cp.start()
            copies.append(cp)
        for cp in copies:
            cp.wait()
        acc = gath[0]
        for d in range(1, 8):
            acc = acc + gath[d]
        o_ref[...] = acc

    return pl.pallas_call(
        body, name=name, out_shape=_sds(v.shape, F32),
        in_specs=[pl.BlockSpec(memory_space=pltpu.VMEM)], out_specs=pl.BlockSpec(memory_space=pltpu.VMEM),
        scratch_shapes=[pltpu.VMEM((8, rows, 128), F32), pltpu.SemaphoreType.DMA((7,)), pltpu.SemaphoreType.DMA((7,))])(v)


def _same_shape_runs(arrays):
    runs = {}
    for i, a in enumerate(arrays):
        runs.setdefault(a.shape, []).append(i)
    return list(runs.values())


def _per_shape(fn, *lists):
    out = [None] * len(lists[0])
    for idx in _same_shape_runs(lists[0]):
        for i, r in zip(idx, fn(*[[l[i] for i in idx] for l in lists])):
            out[i] = r
    return out


def add_halves(gs, bufs, cidx, *, name):
    cnt = len(gs)
    _, k, n = gs[0].shape

    def body(c_ref, *refs):
        g, b, o = refs[:cnt], refs[cnt:2 * cnt], refs[2 * cnt:]
        for i in range(cnt):
            o[i][...] = (g[i][...].astype(F32) + b[i][...].astype(F32)).astype(BF16)

    blk = pl.BlockSpec((None, k // 2, n), lambda s, c: (s, 0, 0))
    mine = pl.BlockSpec((None, k // 2, n), lambda s, c: (s, c[0], 0))
    return list(pl.pallas_call(
        body, name=name, out_shape=tuple(_sds(b.shape, BF16) for b in bufs),
        grid_spec=pltpu.PrefetchScalarGridSpec(
            num_scalar_prefetch=1, grid=(NSH,), in_specs=[mine] * cnt + [blk] * cnt, out_specs=tuple([blk] * cnt)),
        compiler_params=_params(("parallel",)))(cidx, *gs, *bufs))


def add_chips(sums, bufs, sidx, *, name):
    cnt = len(sums)
    _, kh, n = sums[0].shape

    def body(s_ref, *refs):
        mine, b, o = refs[:cnt], refs[cnt:2 * cnt], refs[2 * cnt:]
        for i in range(cnt):
            o[i][...] = ((mine[i][...].astype(F32) + b[i][0].astype(F32)) + (b[i][1].astype(F32) + b[i][2].astype(F32)))

    own = pl.BlockSpec((None, kh, n), lambda i, s: (s[0], 0, 0))
    got = pl.BlockSpec((3, kh, n), lambda i, s: (0, 0, 0))
    out = pl.BlockSpec((kh, n), lambda i, s: (0, 0))
    return list(pl.pallas_call(
        body, name=name, out_shape=tuple(_sds((kh, n), F32) for _ in sums),
        grid_spec=pltpu.PrefetchScalarGridSpec(
            num_scalar_prefetch=1, grid=(1,), in_specs=[own] * cnt + [got] * cnt, out_specs=tuple([out] * cnt)),
        compiler_params=_params(("arbitrary",)))(sidx, *sums, *bufs))


PARAMS = ("ffn1_norm", "ffn1_w_gate", "ffn1_w_up", "ffn1_w_down", "mix_norm", "w_in", "b_gate", "na_q_norm", "na_k_norm",
          "na_rpb", "sw_q_norm", "sw_k_norm", "sw_sink", "t5_rel_table", "w_branch_na", "w_branch_sw", "w_out", "ffn2_norm",
          "ffn2_w_gate", "ffn2_w_up", "ffn2_w_down")
SMALL_ALL = tuple(n for n in PARAMS if n not in BIG)
TRANSPOSED = ("ffn1_w_gate", "ffn1_w_up", "w_in", "ffn2_w_gate", "ffn2_w_up")
SMALL_ROWS = 152


def _pack_small(vals):
    flat = jnp.concatenate([vals[n].reshape(-1).astype(F32) for n in SMALL_ALL] + [vals["loss"].reshape(-1)])
    return jnp.pad(flat, (0, SMALL_ROWS * 128 - flat.shape[0])).reshape(SMALL_ROWS, 128)


def _unpack_small(packed, like):
    flat, out, off = packed.reshape(-1), {}, 0
    for n in SMALL_ALL:
        size = math.prod(like[n].shape)
        out[n] = flat[off:off + size].reshape(like[n].shape)
        off += size
    out["loss"] = flat[off]
    return out


def kernel(x, ffn1_norm, ffn1_w_gate, ffn1_w_up, ffn1_w_down, mix_norm, w_in, b_gate, na_q_norm, na_k_norm, na_rpb, sw_q_norm, sw_k_norm, sw_sink, t5_rel_table, w_branch_na, w_branch_sw, w_out, ffn2_norm, ffn2_w_gate, ffn2_w_up, ffn2_w_down, loss_target, m_ffn1_norm, m_ffn1_w_gate, m_ffn1_w_up, m_ffn1_w_down, m_mix_norm, m_w_in, m_b_gate, m_na_q_norm, m_na_k_norm, m_na_rpb, m_sw_q_norm, m_sw_k_norm, m_sw_sink, m_t5_rel_table, m_w_branch_na, m_w_branch_sw, m_w_out, m_ffn2_norm, m_ffn2_w_gate, m_ffn2_w_up, m_ffn2_w_down, v_ffn1_norm, v_ffn1_w_gate, v_ffn1_w_up, v_ffn1_w_down, v_mix_norm, v_w_in, v_b_gate, v_na_q_norm, v_na_k_norm, v_na_rpb, v_sw_q_norm, v_sw_k_norm, v_sw_sink, v_t5_rel_table, v_w_branch_na, v_w_branch_sw, v_w_out, v_ffn2_norm, v_ffn2_w_gate, v_ffn2_w_up, v_ffn2_w_down):
    args = locals()
    tr = lambda n, a: jnp.transpose(a, (0, 2, 1)) if n in TRANSPOSED else a
    w = {n: tr(n, args[n]) for n in PARAMS}
    m = {n: tr(n, args["m_" + n]) for n in PARAMS}
    v = {n: tr(n, args["v_" + n]) for n in PARAMS}
    cidx = lax.axis_index("c").astype(jnp.int32).reshape(1)
    sidx = (2 * lax.axis_index("x") + lax.axis_index("y")).astype(jnp.int32).reshape(1)

    small = [{n: w[n][l] for n in SMALL} for l in range(DEPTH)]
    order = ("ffn1", "mix", "ffn2")

    keys = [(l, g) for l in range(DEPTH) for g in order]
    local = lambda l, g: [w[n][l].astype(BF16) for n in GROUPS[g]]
    first = gather_start([local(*keys[0])], name="gather_start")
    rest = gather_start([local(*key) for key in keys[1:]], first[0][2][0], name="gather_start")
    in_flight = dict(zip(keys, first + rest))
    t5b = t5_bias(w["t5_rel_table"], name="t5_bias")
    for l in range(DEPTH):
        small[l]["na_bias"] = na_bias_table(small[l]["na_rpb"], name="na_bias_table")
    early = [t5b] + [small[l]["na_bias"] for l in range(DEPTH)] + [rest[0][2][0]]

    def weights_of(l):
        def get(group, after):
            send_sems, recv_sems, thru, lands = in_flight[(l, group)]
            after = [after] + (early if (l, group) == keys[0] else [])
            thru, lands = gather_wait(send_sems, recv_sems, thru, lands, after, name="gather_wait")
            return dict(zip(GROUPS[group], gather_finish(thru, lands, name="gather_finish")))
        return get

    h0, saved0 = layer_fwd(x[0], small[0], weights_of(0), t5b)
    h1, saved1 = layer_fwd(h0, small[1], weights_of(1), t5b)
    dy, dy_bf, loss_row = loss_head(h1, loss_target[0], name="loss_head")

    crossing, tokens, pending = {}, [], []

    def ship(after, then=None):
        key, send_sems, recv_sems, grads, lands = pending.pop()
        grads, from_sibling = pair_exchange_wait(send_sems, recv_sems, grads, lands, after, name="pair_exchange_wait")
        sums = _per_shape(lambda gs, bs: add_halves(gs, bs, cidx, name="add_halves"), grads, from_sibling)
        if then is None:
            send_sems, recv_sems, sums, lands, token = chip_exchange_start(sums, name="chip_exchange_start")
            crossing[key] = (send_sems, recv_sems, sums, lands)
            return token
        crossing[key], pair, token = exchange_start(sums, then[1], name="exchange_start")
        pending.append((then[0],) + pair)
        return token

    def reduce_of(l):
        def emit(group, grads):
            grads = list(grads)
            if pending:
                token = ship([grads[0]], then=((l, group), grads))
            else:
                send_sems, recv_sems, grads, lands, token = pair_exchange_start(grads, name="pair_exchange_start")
                pending.append(((l, group), send_sems, recv_sems, grads, lands))
            tokens.append(token)
            return token
        return emit

    def finish(layer, after, filled=None):
        sent = {}
        for group in order:
            send_sems, recv_sems, sums, lands = crossing[(layer, group)]
            sums, got = chip_exchange_wait(send_sems, recv_sems, sums, lands, after, name="chip_exchange_wait")
            halves = _per_shape(lambda ss, bs: add_chips(ss, bs, sidx, name="add_chips"), sums, got)
            sent[group] = pair_send_start(halves, name="pair_send_start")
            after = [sent[group][4]]
        out = {}
        for group in order:
            send_sems, recv_sems, halves, lands, _ = sent[group]
            halves, theirs = pair_send_wait(send_sems, recv_sems, halves, lands, after, name="pair_send_wait")
            names = GROUPS[group]
            res = _per_shape(
                lambda ws, ms, vs, a, b, *f: adamw_layer(ws, ms, vs, a, b, cidx, layer, list(f[0]) if f else None, name="adamw_layer"),
                *([[w[n] for n in names], [m[n] for n in names], [v[n] for n in names], halves, theirs]
                  + ([[filled[n] for n in names]] if filled is not None else [])))
            out.update(zip(names, res))
            after = [res[-1][0]]
        return out

    dy, dy_bf, small1, dt5_1 = layer_bwd(dy, dy_bf, saved1, small[1], t5b, reduce_of(1))
    grad_x, _, small0, dt5_0 = layer_bwd(dy, dy_bf, saved0, small[0], t5b, reduce_of(0), dep=tokens[-1])
    done1 = finish(1, [ship([grad_x])])

    smalls = [small0, small1]
    dt5 = t5_table_grad(dt5_0, dt5_1, name="t5_table_grad").reshape(32, 8)
    local_small = {n: jnp.stack([smalls[l][n].reshape(w[n].shape[1:]) for l in range(DEPTH)]) for n in SMALL}
    local_small["t5_rel_table"] = dt5
    local_small["loss"] = loss_row[0, 0:1]
    total = allreduce_small(_pack_small(local_small), name="allreduce_small")
    small_grads = _unpack_small(total, w)
    small_done = adamw_small([w[n] for n in SMALL_ALL], [small_grads[n] for n in SMALL_ALL], [m[n] for n in SMALL_ALL],
                             [v[n] for n in SMALL_ALL], name="adamw_small")

    grad, delta, new_m, new_v = {}, {}, {}, {}
    for n, done in finish(0, [small_done[0][0], done1[BIG[-1]][0]], filled=done1).items():
        grad[n], delta[n], new_m[n], new_v[n] = done
    for n, done in zip(SMALL_ALL, small_done):
        grad[n] = small_grads[n]
        delta[n], new_m[n], new_v[n] = done

    return (small_grads["loss"], grad_x[None], *[tr(n, grad[n]) for n in PARAMS], *[tr(n, delta[n]) for n in PARAMS],
            *[tr(n, new_m[n]) for n in PARAMS], *[tr(n, new_v[n]) for n in PARAMS])
```

```python
import math

import jax
import jax.numpy as jnp
import numpy as np
from jax import lax
from jax.experimental import pallas as pl
from jax.experimental.pallas import tpu as pltpu

F32 = jnp.float32
BF16 = jnp.bfloat16

SEQ = 2048
DM = 1024
DFF = 2816
DEPTH = 2
NSH = 4
FSH = DFF // NSH
GRID_W = 64
ROWS = SEQ // GRID_W
NA_HEADS = 8
HD = 64
NA_WR = 8
NA_WC = 16
NA_KEYS = NA_WR * GRID_W
SW_BLK = 128
SW_NB = SEQ // SW_BLK
SW_KEYS = 3 * SW_BLK
ATT_W = 2304
GATE_W = 2048
IN_W = ATT_W + GATE_W
EPS = 1e-6
NEG = -1e30
QK_SCALE = 1.0 / math.sqrt(HD)

ADAM_LR = 0.001
ADAM_B1 = 0.9
ADAM_B2 = 0.999
ADAM_EPS = 1e-08
ADAM_WD = 0.01
ADAM_STEP = 10

VMEM_LIMIT = 56 << 20
MESH = pl.DeviceIdType.MESH

NT = (((1,), (1,)), ((), ()))
TN = (((0,), (0,)), ((), ()))
NN = (((1,), (0,)), ((), ()))


def _dot(a, b, dims=NN):
    return lax.dot_general(a, b, dims, preferred_element_type=F32)


def _params(sem=None):
    return pltpu.CompilerParams(dimension_semantics=sem, vmem_limit_bytes=VMEM_LIMIT)


def _sds(shape, dtype):
    return jax.ShapeDtypeStruct(shape, dtype)


def mm(a, b, *, name, ta=False, tb=False, out_dtype=F32, add=None, scale=None, tm=512, tn=None, tk=None, exact=False,
       dep=None, b_rows=None):
    m, kd = (a.shape[1], a.shape[0]) if ta else a.shape
    if b_rows is None:
        n = b.shape[0] if tb else b.shape[1]
    else:
        n = b_rows[1] if tb else b.shape[1]
        assert tb or (b_rows[1] == kd and (tk or kd) == kd)
    tm, tn, tk = min(tm, m), min(tn or n, n), min(tk or kd, kd)
    nk = kd // tk
    dims = (((0 if ta else 1,), (1 if tb else 0,)), ((), ()))

    def body(*refs):
        a_ref, b_ref = refs[:2]
        add_ref = refs[2] if add is not None else None
        o_ref = refs[-1] if nk == 1 else refs[-2]
        if b_rows is None:
            bv = b_ref[...]
        elif tb:
            bv = b_ref[pl.ds(pl.multiple_of(b_rows[0] + pl.program_id(1) * tn, 16), tn), :]
        else:
            bv = b_ref[b_rows[0]:b_rows[0] + b_rows[1], :]
        if exact:
            part = lax.dot_general(a_ref[...], bv, dims, precision=lax.Precision.HIGHEST, preferred_element_type=F32)
        else:
            part = lax.dot_general(a_ref[...].astype(BF16), bv.astype(BF16), dims, preferred_element_type=F32)

        def finish(r):
            if scale is not None:
                r = r * scale
            if add is not None:
                r = r + add_ref[...]
            o_ref[...] = r.astype(out_dtype)

        if nk == 1:
            finish(part)
        else:
            acc, k = refs[-1], pl.program_id(2)

            @pl.when(k == 0)
            def _():
                acc[...] = part

            @pl.when(k != 0)
            def _():
                acc[...] += part

            pl.when(k == nk - 1)(lambda: finish(acc[...]))

    a_spec = pl.BlockSpec((tk, tm), lambda i, j, k: (k, i)) if ta else pl.BlockSpec((tm, tk), lambda i, j, k: (i, k))
    if b_rows is not None:
        b_spec = pl.BlockSpec(b.shape, lambda i, j, k: (0, 0), pipeline_mode=pl.Buffered(1))
    else:
        b_spec = pl.BlockSpec((tn, tk), lambda i, j, k: (j, k)) if tb else pl.BlockSpec((tk, tn), lambda i, j, k: (k, j))
    o_spec = pl.BlockSpec((tm, tn), lambda i, j, k: (i, j))
    ins, specs = [a, b], [a_spec, b_spec]
    if add is not None:
        ins.append(add)
        specs.append(o_spec)
    if dep is not None:
        ins.append(dep)
        specs.append(pl.BlockSpec(memory_space=pl.ANY))
    return pl.pallas_call(
        body, name=name, out_shape=_sds((m, n), out_dtype), grid=(m // tm, n // tn, nk), in_specs=specs,
        out_specs=o_spec, scratch_shapes=[] if nk == 1 else [pltpu.VMEM((tm, tn), F32)],
        compiler_params=_params(("parallel", "parallel", "arbitrary")))(*ins)


def _rms(x):
    return lax.rsqrt(jnp.mean(x * x, axis=-1, keepdims=True) + EPS)


def mixer_input_fwd(x, gain, w_in_t, *, name, tm=512):
    def body(x_ref, g_ref, w_ref, h_ref, z_ref, zg_ref):
        x = x_ref[...]
        h = (x * _rms(x) * g_ref[...]).astype(BF16)
        h_ref[...] = h
        z_ref[...] = _dot(h, w_ref[0:ATT_W, :], NT).astype(BF16)
        zg_ref[...] = _dot(h, w_ref[ATT_W:IN_W, :], NT).astype(BF16)

    s = x.shape[0]
    tile = pl.BlockSpec((tm, DM), lambda i: (i, 0))
    return pl.pallas_call(
        body, name=name, out_shape=(_sds((s, DM), BF16), _sds((s, ATT_W), BF16), _sds((s, GATE_W), BF16)), grid=(s // tm,),
        in_specs=[tile, pl.BlockSpec((1, DM), lambda i: (0, 0)),
                  pl.BlockSpec((IN_W, DM), lambda i: (0, 0), pipeline_mode=pl.Buffered(1))],
        out_specs=(tile, pl.BlockSpec((tm, ATT_W), lambda i: (i, 0)), pl.BlockSpec((tm, GATE_W), lambda i: (i, 0))),
        compiler_params=_params(("parallel",)))(x, gain, w_in_t)


def _rms_bwd_math(dh, x, gain):
    r = _rms(x)
    xh = x * r
    dgain = jnp.sum(dh * xh, axis=0, keepdims=True)
    dxn = dh * gain
    dx = r * (dxn - xh * jnp.mean(dxn * xh, axis=-1, keepdims=True))
    return dx, dgain


def mixer_input_bwd(dz, dzg, w_in_t, x, gain, dres, *, name, tm=512, dep=None):
    def body(dz_ref, dzg_ref, w_ref, x_ref, g_ref, dres_ref, *rest):
        dx_ref, dxb_ref, dg_ref = rest[-3:]

        @pl.when(pl.program_id(0) == 0)
        def _():
            dg_ref[...] = jnp.zeros_like(dg_ref)

        dh = _dot(dz_ref[...], w_ref[0:ATT_W, :]) + _dot(dzg_ref[...], w_ref[ATT_W:IN_W, :])
        dx, dg = _rms_bwd_math(dh, x_ref[...], g_ref[...])
        dx = dres_ref[...] + dx
        dx_ref[...] = dx
        dxb_ref[...] = dx.astype(BF16)
        dg_ref[...] += dg

    s = x.shape[0]
    tile = pl.BlockSpec((tm, DM), lambda i: (i, 0))
    vec = pl.BlockSpec((1, DM), lambda i: (0, 0))
    ins, specs = _with_dep(
        [dz, dzg, w_in_t, x, gain, dres],
        [pl.BlockSpec((tm, ATT_W), lambda i: (i, 0)), pl.BlockSpec((tm, GATE_W), lambda i: (i, 0)),
         pl.BlockSpec((IN_W, DM), lambda i: (0, 0), pipeline_mode=pl.Buffered(1)), tile, vec, tile], dep)
    return pl.pallas_call(
        body, name=name, out_shape=(_sds((s, DM), F32), _sds((s, DM), BF16), _sds((1, DM), F32)), grid=(s // tm,),
        in_specs=specs, out_specs=(tile, tile, vec), compiler_params=_params(("arbitrary",)))(*ins)


def _with_dep(ins, specs, dep):
    if dep is None:
        return ins, specs
    return ins + [dep], specs + [pl.BlockSpec(memory_space=pl.ANY)]


def _resident_weight():
    return pl.BlockSpec((DFF, DM), lambda i: (0, 0), pipeline_mode=pl.Buffered(1))


def ffn_fwd(x, gain, wg, wu, wd, *, name, tm=512):
    def body(x_ref, g_ref, wg_ref, wu_ref, wd_ref, y_ref, h_ref, gg_ref, uu_ref):
        x = x_ref[...]
        h = (x * _rms(x) * g_ref[...]).astype(BF16)
        h_ref[...] = h
        gg = _dot(h, wg_ref[...], NT)
        uu = _dot(h, wu_ref[...], NT)
        gg_ref[...] = gg.astype(BF16)
        uu_ref[...] = uu.astype(BF16)
        act = (gg * jax.nn.sigmoid(gg) * uu).astype(BF16)
        y_ref[...] = x + 0.5 * _dot(act, wd_ref[...])

    s = x.shape[0]
    tile = pl.BlockSpec((tm, DM), lambda i: (i, 0))
    hid = pl.BlockSpec((tm, DFF), lambda i: (i, 0))
    w = _resident_weight()
    return pl.pallas_call(
        body, name=name,
        out_shape=(_sds((s, DM), F32), _sds((s, DM), BF16), _sds((s, DFF), BF16), _sds((s, DFF), BF16)),
        grid=(s // tm,), in_specs=[tile, pl.BlockSpec((1, DM), lambda i: (0, 0)), w, w, w],
        out_specs=(tile, tile, hid, hid), compiler_params=_params(("parallel",)))(x, gain, wg, wu, wd)


def ffn_bwd_tokens(dy, x, gain, gg, uu, wg, wu, wd, *, name, tm=256, dep=None):
    def body(dy_ref, x_ref, g_ref, gg_ref, uu_ref, wg_ref, wu_ref, wd_ref, *rest):
        dx_ref, dxb_ref, dgain_ref, act_ref, dg_ref, du_ref = rest[-6:]

        @pl.when(pl.program_id(0) == 0)
        def _():
            dgain_ref[...] = jnp.zeros_like(dgain_ref)

        dy = dy_ref[...]
        dact = _dot((0.5 * dy).astype(BF16), wd_ref[...], NT)
        g = gg_ref[...].astype(F32)
        u = uu_ref[...].astype(F32)
        sg = jax.nn.sigmoid(g)
        silu = g * sg
        act_ref[...] = (silu * u).astype(BF16)
        dg = (dact * u * (sg * (1.0 + g * (1.0 - sg)))).astype(BF16)
        du = (dact * silu).astype(BF16)
        dg_ref[...] = dg
        du_ref[...] = du
        dx, dgain = _rms_bwd_math(_dot(dg, wg_ref[...]) + _dot(du, wu_ref[...]), x_ref[...], g_ref[...])
        dx = dy + dx
        dx_ref[...] = dx
        dxb_ref[...] = dx.astype(BF16)
        dgain_ref[...] += dgain

    s = x.shape[0]
    tile = pl.BlockSpec((tm, DM), lambda i: (i, 0))
    vec = pl.BlockSpec((1, DM), lambda i: (0, 0))
    hid = pl.BlockSpec((tm, DFF), lambda i: (i, 0))
    hshape = _sds((s, DFF), BF16)
    w = _resident_weight()
    ins, specs = _with_dep([dy, x, gain, gg, uu, wg, wu, wd], [tile, tile, vec, hid, hid, w, w, w], dep)
    return pl.pallas_call(
        body, name=name, out_shape=(_sds((s, DM), F32), _sds((s, DM), BF16), _sds((1, DM), F32), hshape, hshape, hshape),
        grid=(s // tm,), in_specs=specs, out_specs=(tile, tile, vec, hid, hid, hid),
        compiler_params=_params(("arbitrary",)))(*ins)


def ffn_bwd_weights(h, dy, act, dg, du, *, name, tf=256):
    def body(h_ref, dy_ref, act_ref, dg_ref, du_ref, gwg_ref, gwu_ref, gwd_ref):
        h = h_ref[...]
        gwg_ref[...] = _dot(dg_ref[...], h, TN).astype(BF16)
        gwu_ref[...] = _dot(du_ref[...], h, TN).astype(BF16)
        gwd_ref[...] = (0.5 * _dot(act_ref[...], dy_ref[...], TN)).astype(BF16)

    s = h.shape[0]
    full = pl.BlockSpec((s, DM), lambda f: (0, 0))
    hid = pl.BlockSpec((s, tf), lambda f: (0, f))
    wt = pl.BlockSpec((tf, DM), lambda f: (f, 0))
    wshape = _sds((DFF, DM), BF16)
    return pl.pallas_call(
        body, name=name, out_shape=(wshape, wshape, wshape), grid=(DFF // tf,), in_specs=[full, full, hid, hid, hid],
        out_specs=(wt, wt, wt), compiler_params=_params(("parallel",)))(h, dy, act, dg, du)


def _group_mean(v, bd):
    hi = v.astype(BF16)
    lo = (v - hi.astype(F32)).astype(BF16)
    return _dot(hi, bd) + _dot(lo, bd)


def _block_diag(width):
    idx = np.arange(width) // HD
    return jnp.asarray((idx[:, None] == idx[None, :]).astype(np.float32) / HD, dtype=BF16)


def qknorm_fwd(z, gq_na, gk_na, gq_sw, gk_sw, *, name, tm=256):
    def body(zq_ref, zk_ref, zv_ref, zs_ref, zkv_ref, gqa_ref, gka_ref, gqs_ref, gks_ref, bd_ref, bd2_ref,
             qa_ref, ka_ref, va_ref, qs_ref, kv_ref):
        bd = bd_ref[...]

        def norm(x, g, bdm):
            x = x.astype(F32)
            return x * lax.rsqrt(_group_mean(x * x, bdm) + EPS) * g

        qa_ref[...] = (norm(zq_ref[...], gqa_ref[...], bd) * QK_SCALE).astype(BF16)
        ka_ref[...] = norm(zk_ref[...], gka_ref[...], bd).astype(BF16)
        va_ref[...] = zv_ref[...].astype(BF16)
        qs_ref[...] = (norm(zs_ref[...], gqs_ref[...], bd) * QK_SCALE).astype(BF16)
        kv = zkv_ref[...]
        kv_ref[:, 0:128] = norm(kv[:, 0:128], gks_ref[...], bd2_ref[...]).astype(BF16)
        kv_ref[:, 128:256] = kv[:, 128:256].astype(BF16)

    s = z.shape[0]
    col = lambda j: pl.BlockSpec((tm, 512), lambda i, j=j: (i, j))
    vec = lambda w: pl.BlockSpec((1, w), lambda i: (0, 0))
    o512 = pl.BlockSpec((tm, 512), lambda i: (i, 0))
    g512 = lambda g: jnp.tile(g.reshape(1, HD), (1, 8))
    return pl.pallas_call(
        body, name=name,
        out_shape=(_sds((s, 512), BF16),) * 4 + (_sds((s, 256), BF16),), grid=(s // tm,),
        in_specs=[col(0), col(1), col(2), col(3), pl.BlockSpec((tm, 256), lambda i: (i, 8)), vec(512), vec(512), vec(512),
                  vec(128), pl.BlockSpec((512, 512), lambda i: (0, 0)), pl.BlockSpec((128, 128), lambda i: (0, 0))],
        out_specs=(o512, o512, o512, o512, pl.BlockSpec((tm, 256), lambda i: (i, 0))),
        compiler_params=_params(("parallel",)))(
            z, z, z, z, z, g512(gq_na), g512(gk_na), g512(gq_sw), jnp.tile(gk_sw.reshape(1, HD), (1, 2)),
            _block_diag(512), _block_diag(128))


def qknorm_bwd(z, dqa, dka, dva, dqs, dkv, gq_na, gk_na, gq_sw, gk_sw, *, name, tm=256):
    def body(zq_ref, zk_ref, zs_ref, zkv_ref, dqa_ref, dka_ref, dva_ref, dqs_ref, dkv_ref, gqa_ref, gka_ref, gqs_ref,
             gks_ref, bd_ref, bd2_ref, dz_ref, dgqa_ref, dgka_ref, dgqs_ref, dgks_ref):
        @pl.when(pl.program_id(0) == 0)
        def _():
            dgqa_ref[...] = jnp.zeros_like(dgqa_ref)
            dgka_ref[...] = jnp.zeros_like(dgka_ref)
            dgqs_ref[...] = jnp.zeros_like(dgqs_ref)
            dgks_ref[...] = jnp.zeros_like(dgks_ref)

        bd = bd_ref[...]

        def bwd(x, dy, g, bdm, dg_ref):
            x = x.astype(F32)
            r = lax.rsqrt(_group_mean(x * x, bdm) + EPS)
            xh = x * r
            dg_ref[...] += jnp.sum(dy * xh, axis=0, keepdims=True)
            dxn = dy * g
            return r * (dxn - xh * _group_mean(dxn * xh, bdm))

        dz_ref[:, 0:512] = bwd(zq_ref[...], dqa_ref[...] * QK_SCALE, gqa_ref[...], bd, dgqa_ref).astype(BF16)
        dz_ref[:, 512:1024] = bwd(zk_ref[...], dka_ref[...], gka_ref[...], bd, dgka_ref).astype(BF16)
        dz_ref[:, 1024:1536] = dva_ref[...].astype(BF16)
        dz_ref[:, 1536:2048] = bwd(zs_ref[...], dqs_ref[...] * QK_SCALE, gqs_ref[...], bd, dgqs_ref).astype(BF16)
        dkv = dkv_ref[...]
        dz_ref[:, 2048:2176] = bwd(zkv_ref[:, 0:128], dkv[:, 0:128], gks_ref[...], bd2_ref[...], dgks_ref).astype(BF16)
        dz_ref[:, 2176:2304] = dkv[:, 128:256].astype(BF16)

    s = z.shape[0]
    col = lambda j: pl.BlockSpec((tm, 512), lambda i, j=j: (i, j))
    t512 = pl.BlockSpec((tm, 512), lambda i: (i, 0))
    t256 = pl.BlockSpec((tm, 256), lambda i: (i, 0))
    vec = lambda w: pl.BlockSpec((1, w), lambda i: (0, 0))
    g512 = lambda g: jnp.tile(g.reshape(1, HD), (1, 8))
    return pl.pallas_call(
        body, name=name,
        out_shape=(_sds((s, ATT_W), BF16), _sds((1, 512), F32), _sds((1, 512), F32), _sds((1, 512), F32), _sds((1, 128), F32)),
        grid=(s // tm,),
        in_specs=[col(0), col(1), col(3), pl.BlockSpec((tm, 256), lambda i: (i, 8)), t512, t512, t512, t512, t256,
                  vec(512), vec(512), vec(512), vec(128), pl.BlockSpec((512, 512), lambda i: (0, 0)),
                  pl.BlockSpec((128, 128), lambda i: (0, 0))],
        out_specs=(pl.BlockSpec((tm, ATT_W), lambda i: (i, 0)), vec(512), vec(512), vec(512), vec(128)),
        compiler_params=_params(("arbitrary",)))(
            z, z, z, z, dqa, dka, dva, dqs, dkv, g512(gq_na), g512(gk_na), g512(gq_sw),
            jnp.tile(gk_sw.reshape(1, HD), (1, 2)), _block_diag(512), _block_diag(128))


def _na_row_start(r):
    return jnp.clip(r - NA_WR // 2, 0, ROWS - NA_WR)


def na_bias_table(rpb, *, name):
    t = jnp.pad(rpb, ((0, 0), (0, 2), (0, HD - (2 * NA_WC - 1))))
    pairs = jnp.concatenate([t[:, :16], t[:, 1:17]], axis=-1).reshape(NA_HEADS, 16, 1, 128)

    def body(t_ref, o_ref):
        p = pl.program_id(0)
        q = lax.broadcasted_iota(jnp.int32, (GRID_W, 128), 0)
        kc = lax.broadcasted_iota(jnp.int32, (GRID_W, 128), 1) & (GRID_W - 1)
        cs = jnp.clip(q - NA_WC // 2, 0, GRID_W - NA_WC)
        ok = (kc >= cs) & (kc < cs + NA_WC)
        for h in range(NA_HEADS):
            for pr in range(NA_WR // 2):
                x = jnp.broadcast_to(t_ref[h, 2 * pr - p + NA_WR - 1], (GRID_W, 128))
                b = pltpu.roll(x, 128 - (NA_WC - 1), 1, stride=1, stride_axis=0)
                o_ref[h, :, 128 * pr:128 * pr + 128] = jnp.where(ok, b, NEG)

    return pl.pallas_call(
        body, name=name, out_shape=_sds((NA_WR, NA_HEADS, GRID_W, NA_KEYS), F32), grid=(NA_WR,),
        in_specs=[pl.BlockSpec((NA_HEADS, 16, 1, 128), lambda p: (0, 0, 0, 0))],
        out_specs=pl.BlockSpec((None, NA_HEADS, GRID_W, NA_KEYS), lambda p: (p, 0, 0, 0)),
        compiler_params=_params(("parallel",)))(pairs)


def _lane_halves():
    lane = lax.broadcasted_iota(jnp.int32, (1, 128), 1)
    return lane < HD


def na_fwd(q, k, v, bias, *, name):
    def body(q_ref, k_ref, v_ref, b_ref, o_ref, lse_ref):
        r = pl.program_id(0)
        off = pl.multiple_of(_na_row_start(r) * GRID_W, GRID_W)
        first = _lane_halves()
        sels = [first, jnp.logical_not(first)]
        lanes = [slice(128 * j, 128 * j + 128) for j in range(NA_HEADS // 2)]
        q2s = [q_ref[:, l] for l in lanes]
        k2s = [k_ref[pl.ds(off, NA_KEYS), l] for l in lanes]
        v2s = [v_ref[pl.ds(off, NA_KEYS), l] for l in lanes]
        scores = []
        for h in range(NA_HEADS):
            j, half = divmod(h, 2)
            scores.append(_dot(jnp.where(sels[half], q2s[j], jnp.zeros_like(q2s[j])), k2s[j], NT))
        probs, lses = [], []
        for h in range(NA_HEADS):
            b = b_ref[h]
            s = jnp.where(b > 0.5 * NEG, scores[h] + b, NEG)
            m = jnp.max(s, axis=-1, keepdims=True)
            e = jnp.exp(s - m)
            l = jnp.sum(e, axis=-1, keepdims=True)
            probs.append((e / l).astype(BF16))
            lses.append(m + jnp.log(l))
        for j in range(NA_HEADS // 2):
            zero = jnp.zeros_like(v2s[j])
            o2 = (_dot(probs[2 * j], jnp.where(sels[0], v2s[j], zero))
                  + _dot(probs[2 * j + 1], jnp.where(sels[1], v2s[j], zero)))
            o_ref[:, lanes[j]] = o2.astype(BF16)
        lse_ref[...] = jnp.concatenate(lses, axis=1)

    s_tok = q.shape[0]
    full = pl.BlockSpec((s_tok, 512), lambda r: (0, 0))
    return pl.pallas_call(
        body, name=name, out_shape=(_sds((s_tok, 512), BF16), _sds((s_tok, NA_HEADS), F32)), grid=(ROWS,),
        in_specs=[pl.BlockSpec((GRID_W, 512), lambda r: (r, 0)), full, full,
                  pl.BlockSpec((None, NA_HEADS, GRID_W, NA_KEYS), lambda r: (r - _na_row_start(r), 0, 0, 0))],
        out_specs=(pl.BlockSpec((GRID_W, 512), lambda r: (r, 0)), pl.BlockSpec((GRID_W, NA_HEADS), lambda r: (r, 0))),
        compiler_params=_params(("parallel",)))(q, k, v, bias)


def na_bwd(q, k, v, o, do, lse, bias, *, name):
    def body(q_ref, k_ref, v_ref, o_ref, do_ref, lse_ref, b_ref, dq_ref, dk_ref, dv_ref, db_ref):
        r = pl.program_id(0)

        @pl.when(r == 0)
        def _():
            dk_ref[...] = jnp.zeros_like(dk_ref)
            dv_ref[...] = jnp.zeros_like(dv_ref)

        @pl.when((r <= NA_WR // 2) | (r > ROWS - NA_WR // 2))
        def _():
            db_ref[...] = jnp.zeros_like(db_ref)

        off = pl.multiple_of(_na_row_start(r) * GRID_W, GRID_W)
        first = _lane_halves()
        sels = [first, jnp.logical_not(first)]
        lanes = [slice(128 * j, 128 * j + 128) for j in range(NA_HEADS // 2)]
        q2s = [q_ref[:, l] for l in lanes]
        k2s = [k_ref[pl.ds(off, NA_KEYS), l] for l in lanes]
        v2s = [v_ref[pl.ds(off, NA_KEYS), l] for l in lanes]
        do2s = [do_ref[:, l] for l in lanes]
        prods = [do2s[j].astype(F32) * o_ref[:, lanes[j]].astype(F32) for j in range(NA_HEADS // 2)]
        lse = lse_ref[...]
        qhs, dohs, scores, dps = [], [], [], []
        for h in range(NA_HEADS):
            j, half = divmod(h, 2)
            qhs.append(jnp.where(sels[half], q2s[j], jnp.zeros_like(q2s[j])))
            dohs.append(jnp.where(sels[half], do2s[j], jnp.zeros_like(do2s[j])))
            scores.append(_dot(qhs[h], k2s[j], NT))
            dps.append(_dot(dohs[h], v2s[j], NT))
        pbs, dsbs = [], []
        for h in range(NA_HEADS):
            j, half = divmod(h, 2)
            b = b_ref[h]
            s = jnp.where(b > 0.5 * NEG, scores[h] + b, NEG)
            p = jnp.exp(s - lse[:, h:h + 1])
            delta = jnp.sum(jnp.where(sels[half], prods[j], 0.0), axis=-1, keepdims=True)
            ds = p * (dps[h] - delta)
            db_ref[h] += ds
            pbs.append(p.astype(BF16))
            dsbs.append(ds.astype(BF16))
        for j in range(NA_HEADS // 2):
            a, b = 2 * j, 2 * j + 1
            zero = jnp.zeros_like(k2s[j])
            dq_ref[:, lanes[j]] = (_dot(dsbs[a], jnp.where(sels[0], k2s[j], zero))
                                   + _dot(dsbs[b], jnp.where(sels[1], k2s[j], zero)))
            dk_ref[pl.ds(off, NA_KEYS), lanes[j]] += _dot(dsbs[a], qhs[a], TN) + _dot(dsbs[b], qhs[b], TN)
            dv_ref[pl.ds(off, NA_KEYS), lanes[j]] += _dot(pbs[a], dohs[a], TN) + _dot(pbs[b], dohs[b], TN)

    s_tok = q.shape[0]
    full = pl.BlockSpec((s_tok, 512), lambda r: (0, 0))
    row = pl.BlockSpec((GRID_W, 512), lambda r: (r, 0))
    bias_spec = pl.BlockSpec((None, NA_HEADS, GRID_W, NA_KEYS), lambda r: (r - _na_row_start(r), 0, 0, 0))
    return pl.pallas_call(
        body, name=name,
        out_shape=(_sds((s_tok, 512), F32), _sds((s_tok, 512), F32), _sds((s_tok, 512), F32),
                   _sds((NA_WR, NA_HEADS, GRID_W, NA_KEYS), F32)),
        grid=(ROWS,),
        in_specs=[row, full, full, row, row, pl.BlockSpec((GRID_W, NA_HEADS), lambda r: (r, 0)), bias_spec],
        out_specs=(row, full, full, bias_spec), compiler_params=_params(("arbitrary",)))(q, k, v, o, do, lse, bias)


def t5_bucket_map():
    rel = np.arange(SW_KEYS)[None, :] - SW_BLK - np.arange(SW_BLK)[:, None]
    nb = 16
    max_exact = nb // 2
    n = np.abs(rel)
    large = max_exact + (np.log(np.maximum(n, 1) / max_exact) / np.log(128 / max_exact) * (nb - max_exact)).astype(np.int32)
    large = np.minimum(large, nb - 1)
    return ((rel > 0) * nb + np.where(n < max_exact, n, large)).astype(np.int32)


def t5_bias(table, *, name):
    rel = np.arange(-SW_BLK, SW_BLK + 1)
    nb, max_exact = 16, 8
    n = np.abs(rel)
    large = max_exact + (np.log(np.maximum(n, 1) / max_exact) / np.log(128 / max_exact) * (nb - max_exact)).astype(np.int32)
    bucket = ((rel > 0) * nb + np.where(n < max_exact, n, np.minimum(large, nb - 1))).astype(np.int32)
    u = jnp.pad(table[jnp.asarray(bucket)].T, ((0, 0), (0, SW_KEYS - bucket.shape[0]))).reshape(8, 1, SW_KEYS)

    def body(u_ref, o_ref):
        for h in range(8):
            x = jnp.broadcast_to(u_ref[h], (SW_BLK, SW_KEYS))
            o_ref[h] = pltpu.roll(x, 0, 1, stride=1, stride_axis=0)

    return pl.pallas_call(body, name=name, out_shape=_sds((8, SW_BLK, SW_KEYS), F32), compiler_params=_params())(u)


def _sw_valid(n):
    a = lax.broadcasted_iota(jnp.int32, (SW_BLK, SW_KEYS), 0)
    j = lax.broadcasted_iota(jnp.int32, (SW_BLK, SW_KEYS), 1)
    kpos = (n - 1) * SW_BLK + j
    return (jnp.abs(j - SW_BLK - a) <= SW_BLK) & (kpos >= 0) & (kpos < SEQ)


def _dup_group(x2, g, first):
    rolled = pltpu.roll(x2, HD, 1)
    return jnp.where(first, x2, rolled) if g == 0 else jnp.where(first, rolled, x2)


def sw_fwd(q, kv, t5, sink, *, name):
    def body(q_ref, kv_ref, t5_ref, sink_ref, o_ref, lse_ref):
        n = pl.program_id(0)
        off = pl.multiple_of(n * SW_BLK, SW_BLK)
        first = _lane_halves()
        sels = [first, jnp.logical_not(first)]
        valid = _sw_valid(n)
        k2 = kv_ref[pl.ds(off, SW_KEYS), 0:128]
        v2 = kv_ref[pl.ds(off, SW_KEYS), 128:256]
        kk = [_dup_group(k2, g, first) for g in range(2)]
        vv = [_dup_group(v2, g, first) for g in range(2)]
        q2s = [q_ref[:, 128 * j:128 * j + 128] for j in range(4)]
        scores = []
        for h in range(8):
            j, half = divmod(h, 2)
            scores.append(_dot(jnp.where(sels[half], q2s[j], jnp.zeros_like(q2s[j])), kk[j // 2], NT))
        probs, lses = [], []
        for h in range(8):
            s = jnp.where(valid, scores[h] + t5_ref[h], NEG)
            snk = sink_ref[h]
            m = jnp.maximum(jnp.max(s, axis=-1, keepdims=True), snk)
            e = jnp.exp(s - m)
            den = jnp.sum(e, axis=-1, keepdims=True) + jnp.exp(snk - m)
            probs.append((e / den).astype(BF16))
            lses.append(m + jnp.log(den))
        outs = []
        for j in range(4):
            vg = vv[j // 2]
            zero = jnp.zeros_like(vg)
            outs.append(_dot(probs[2 * j], jnp.where(sels[0], vg, zero)) + _dot(probs[2 * j + 1], jnp.where(sels[1], vg, zero)))
        o_ref[...] = jnp.concatenate(outs, axis=1).astype(BF16)
        lse_ref[...] = jnp.concatenate(lses, axis=1)

    s_tok = q.shape[0]
    blk = pl.BlockSpec((SW_BLK, 512), lambda n: (n, 0))
    return pl.pallas_call(
        body, name=name, out_shape=(_sds((s_tok, 512), BF16), _sds((s_tok, 8), F32)), grid=(SW_NB,),
        in_specs=[blk, pl.BlockSpec(kv.shape, lambda n: (0, 0)), pl.BlockSpec((8, SW_BLK, SW_KEYS), lambda n: (0, 0, 0)),
                  pl.BlockSpec(memory_space=pltpu.SMEM)],
        out_specs=(blk, pl.BlockSpec((SW_BLK, 8), lambda n: (n, 0))), compiler_params=_params(("parallel",)))(q, kv, t5, sink)


def sw_bwd(q, kv, o, do, lse, t5, sink, *, name):
    def body(q_ref, kv_ref, o_ref, do_ref, lse_ref, t5_ref, sink_ref, dq_ref, dkv_ref, dt5_ref, dsink_ref):
        n = pl.program_id(0)

        @pl.when(n == 0)
        def _():
            dkv_ref[...] = jnp.zeros_like(dkv_ref)
            dt5_ref[...] = jnp.zeros_like(dt5_ref)
            dsink_ref[...] = jnp.zeros_like(dsink_ref)

        off = pl.multiple_of(n * SW_BLK, SW_BLK)
        first = _lane_halves()
        sels = [first, jnp.logical_not(first)]
        valid = _sw_valid(n)
        k2 = kv_ref[pl.ds(off, SW_KEYS), 0:128]
        v2 = kv_ref[pl.ds(off, SW_KEYS), 128:256]
        kk = [_dup_group(k2, g, first) for g in range(2)]
        vv = [_dup_group(v2, g, first) for g in range(2)]
        lanes = [slice(128 * j, 128 * j + 128) for j in range(4)]
        q2s = [q_ref[:, l] for l in lanes]
        do2s = [do_ref[:, l] for l in lanes]
        prods = [do2s[j].astype(F32) * o_ref[:, lanes[j]].astype(F32) for j in range(4)]
        lse = lse_ref[...]
        qhs, dohs, scores, dps = [], [], [], []
        for h in range(8):
            j, half = divmod(h, 2)
            qhs.append(jnp.where(sels[half], q2s[j], jnp.zeros_like(q2s[j])))
            dohs.append(jnp.where(sels[half], do2s[j], jnp.zeros_like(do2s[j])))
            scores.append(_dot(qhs[h], kk[j // 2], NT))
            dps.append(_dot(dohs[h], vv[j // 2], NT))
        pbs, dsbs, dss, dsinks = [], [], [], []
        for h in range(8):
            j, half = divmod(h, 2)
            s = jnp.where(valid, scores[h] + t5_ref[h], NEG)
            lse_h = lse[:, h:h + 1]
            p = jnp.exp(s - lse_h)
            delta = jnp.sum(jnp.where(sels[half], prods[j], 0.0), axis=-1, keepdims=True)
            ds = p * (dps[h] - delta)
            dss.append(ds)
            dsinks.append(-jnp.sum(jnp.exp(sink_ref[h] - lse_h) * delta, axis=0, keepdims=True))
            pbs.append(p.astype(BF16))
            dsbs.append(ds.astype(BF16))
        dt5_ref[...] += jnp.stack(dss)
        dsink_ref[...] += jnp.concatenate(dsinks, axis=1)
        dqs = []
        for j in range(4):
            a, b = 2 * j, 2 * j + 1
            zero = jnp.zeros_like(kk[j // 2])
            dqs.append(_dot(dsbs[a], jnp.where(sels[0], kk[j // 2], zero)) + _dot(dsbs[b], jnp.where(sels[1], kk[j // 2], zero)))
        dq_ref[...] = jnp.concatenate(dqs, axis=1)
        dk_groups, dv_groups = [], []
        for g in range(2):
            dkk = sum(_dot(dsbs[h], qhs[h], TN) for h in range(4 * g, 4 * g + 4))
            dvv = sum(_dot(pbs[h], dohs[h], TN) for h in range(4 * g, 4 * g + 4))
            dk_groups.append(dkk + pltpu.roll(dkk, HD, 1))
            dv_groups.append(dvv + pltpu.roll(dvv, HD, 1))
        dkv_ref[pl.ds(off, SW_KEYS), :] += jnp.concatenate(
            [jnp.where(first, dk_groups[0], dk_groups[1]), jnp.where(first, dv_groups[0], dv_groups[1])], axis=1)

    s_tok = q.shape[0]
    blk = pl.BlockSpec((SW_BLK, 512), lambda n: (n, 0))
    kv_spec = pl.BlockSpec(kv.shape, lambda n: (0, 0))
    t5_spec = pl.BlockSpec((8, SW_BLK, SW_KEYS), lambda n: (0, 0, 0))
    vec = pl.BlockSpec((1, 8), lambda n: (0, 0))
    return pl.pallas_call(
        body, name=name,
        out_shape=(_sds((s_tok, 512), F32), _sds(kv.shape, F32), _sds((8, SW_BLK, SW_KEYS), F32), _sds((1, 8), F32)),
        grid=(SW_NB,), in_specs=[blk, kv_spec, blk, blk, pl.BlockSpec((SW_BLK, 8), lambda n: (n, 0)), t5_spec,
                                 pl.BlockSpec(memory_space=pltpu.SMEM)],
        out_specs=(blk, kv_spec, t5_spec, vec), compiler_params=_params(("arbitrary",)))(q, kv, o, do, lse, t5, sink)


def mixer_output_fwd(o_na, o_sw, zg, bias, wa, ws, wo, res, *, name, tm=512):
    def body(ona_ref, osw_ref, z0_ref, z1_ref, b0_ref, b1_ref, wa_ref, ws_ref, wo_ref, res_ref, y_ref, pa_ref, ps_ref, m_ref):
        pa = _dot(ona_ref[...], wa_ref[...]).astype(BF16)
        ps = _dot(osw_ref[...], ws_ref[...]).astype(BF16)
        pa_ref[...] = pa
        ps_ref[...] = ps
        g0 = jax.nn.sigmoid(z0_ref[...] + b0_ref[...])
        g1 = jax.nn.sigmoid(z1_ref[...] + b1_ref[...])
        merged = (g0 * pa + g1 * ps).astype(BF16)
        m_ref[...] = merged
        y_ref[...] = res_ref[...] + _dot(merged, wo_ref[...])

    s = zg.shape[0]
    half = lambda j: pl.BlockSpec((tm, DM), lambda i, j=j: (i, j))
    bvec = lambda j: pl.BlockSpec((1, DM), lambda i, j=j: (0, j))
    att = pl.BlockSpec((tm, 512), lambda i: (i, 0))
    whole = lambda a: pl.BlockSpec(a.shape, lambda i: (0, 0), pipeline_mode=pl.Buffered(1))
    act = _sds((s, DM), BF16)
    return pl.pallas_call(
        body, name=name, out_shape=(_sds((s, DM), F32), act, act, act), grid=(s // tm,),
        in_specs=[att, att, half(0), half(1), bvec(0), bvec(1), whole(wa), whole(ws), whole(wo), half(0)],
        out_specs=(half(0),) * 4, compiler_params=_params(("parallel",)))(o_na, o_sw, zg, zg, bias, bias, wa, ws, wo, res)


def mixer_output_bwd(dy, zg, bias, pa, ps, wa, ws, wo, *, name, tm=512, dep=None):
    def body(dy_ref, z0_ref, z1_ref, b0_ref, b1_ref, pa_ref, ps_ref, wa_ref, ws_ref, wo_ref, *rest):
        dpa_ref, dps_ref, dz_ref, db_ref, dona_ref, dosw_ref = rest[-6:]

        @pl.when(pl.program_id(0) == 0)
        def _():
            db_ref[...] = jnp.zeros_like(db_ref)

        dm = _dot(dy_ref[...].astype(BF16), wo_ref[...], NT)
        g0 = jax.nn.sigmoid(z0_ref[...] + b0_ref[...])
        g1 = jax.nn.sigmoid(z1_ref[...] + b1_ref[...])
        dpa = (dm * g0).astype(BF16)
        dps = (dm * g1).astype(BF16)
        dpa_ref[...] = dpa
        dps_ref[...] = dps
        dz0 = dm * pa_ref[...] * g0 * (1.0 - g0)
        dz1 = dm * ps_ref[...] * g1 * (1.0 - g1)
        dz_ref[:, 0:DM] = dz0.astype(BF16)
        dz_ref[:, DM:2 * DM] = dz1.astype(BF16)
        db_ref[:, 0:DM] += jnp.sum(dz0, axis=0, keepdims=True)
        db_ref[:, DM:2 * DM] += jnp.sum(dz1, axis=0, keepdims=True)
        dona_ref[...] = _dot(dpa, wa_ref[...], NT).astype(BF16)
        dosw_ref[...] = _dot(dps, ws_ref[...], NT).astype(BF16)

    s = zg.shape[0]
    half = lambda j: pl.BlockSpec((tm, DM), lambda i, j=j: (i, j))
    bvec = lambda j: pl.BlockSpec((1, DM), lambda i, j=j: (0, j))
    att = pl.BlockSpec((tm, 512), lambda i: (i, 0))
    whole = lambda a: pl.BlockSpec(a.shape, lambda i: (0, 0), pipeline_mode=pl.Buffered(1))
    ins, specs = _with_dep([dy, zg, zg, bias, bias, pa, ps, wa, ws, wo],
                           [half(0), half(0), half(1), bvec(0), bvec(1), half(0), half(0), whole(wa), whole(ws), whole(wo)], dep)
    return pl.pallas_call(
        body, name=name,
        out_shape=(_sds((s, DM), BF16), _sds((s, DM), BF16), _sds((s, GATE_W), BF16), _sds((1, GATE_W), F32),
                   _sds((s, 512), BF16), _sds((s, 512), BF16)),
        grid=(s // tm,), in_specs=specs,
        out_specs=(half(0), half(0), pl.BlockSpec((tm, GATE_W), lambda i: (i, 0)), pl.BlockSpec((1, GATE_W), lambda i: (0, 0)),
                   att, att),
        compiler_params=_params(("arbitrary",)))(*ins)


def loss_head(y, target, *, name, tm=512):
    def body(y_ref, t_ref, dy_ref, dyb_ref, l_ref):
        @pl.when(pl.program_id(0) == 0)
        def _():
            l_ref[...] = jnp.zeros_like(l_ref)

        err = y_ref[...] - t_ref[...]
        dy = err * (1.0 / DM)
        dy_ref[...] = dy
        dyb_ref[...] = dy.astype(BF16)
        l_ref[...] += 0.5 * jnp.sum(jnp.mean(err * err, axis=-1, keepdims=True), axis=0, keepdims=True)

    s = y.shape[0]
    tile = pl.BlockSpec((tm, DM), lambda i: (i, 0))
    return pl.pallas_call(
        body, name=name, out_shape=(_sds((s, DM), F32), _sds((s, DM), BF16), _sds((1, 128), F32)), grid=(s // tm,),
        in_specs=[tile, tile], out_specs=(tile, tile, pl.BlockSpec((1, 128), lambda i: (0, 0))),
        compiler_params=_params(("arbitrary",)))(y, target)


def adamw_small(ws, gs, ms, vs, *, name):
    cnt = len(ws)

    def body(*refs):
        ins, outs = refs[:4 * cnt], refs[4 * cnt:]
        for i in range(cnt):
            w_ref, g_ref, m_ref, v_ref = ins[4 * i:4 * i + 4]
            d_ref, nm_ref, nv_ref = outs[3 * i:3 * i + 3]
            g = g_ref[...]
            nm = ADAM_B1 * m_ref[...] + (1.0 - ADAM_B1) * g
            nv = ADAM_B2 * v_ref[...] + (1.0 - ADAM_B2) * jnp.square(g)
            m_hat = nm / (1.0 - ADAM_B1 ** ADAM_STEP)
            v_hat = nv / (1.0 - ADAM_B2 ** ADAM_STEP)
            d_ref[...] = -ADAM_LR * (m_hat / (jnp.sqrt(v_hat) + ADAM_EPS) + ADAM_WD * w_ref[...])
            nm_ref[...] = nm
            nv_ref[...] = nv

    flat = [a for i in range(cnt) for a in (ws[i], gs[i], ms[i], vs[i])]
    res = pl.pallas_call(
        body, name=name, out_shape=tuple(_sds(ws[i].shape, F32) for i in range(cnt) for _ in range(3)),
        compiler_params=_params())(*flat)
    return [tuple(res[3 * i:3 * i + 3]) for i in range(cnt)]


def adamw_layer(ws, ms, vs, mines, theirs, cidx, layer, filled=None, *, name):
    cnt = len(ws)
    _, k, n = ws[0].shape
    nt = 2
    tk = k // 2 // nt

    def body(c_ref, *refs):
        own = pl.program_id(0) == c_ref[0]
        outs = refs[-4 * cnt:]
        for i in range(cnt):
            w_ref, m_ref, v_ref, a_ref, b_ref = refs[5 * i:5 * i + 5]
            g_ref, d_ref, nm_ref, nv_ref = outs[4 * i:4 * i + 4]
            g = jnp.where(own, a_ref[...], b_ref[...])
            g_ref[...] = g
            nm = ADAM_B1 * m_ref[...] + (1.0 - ADAM_B1) * g
            nv = ADAM_B2 * v_ref[...] + (1.0 - ADAM_B2) * jnp.square(g)
            m_hat = nm / (1.0 - ADAM_B1 ** ADAM_STEP)
            v_hat = nv / (1.0 - ADAM_B2 ** ADAM_STEP)
            d_ref[...] = -ADAM_LR * (m_hat / (jnp.sqrt(v_hat) + ADAM_EPS) + ADAM_WD * w_ref[...])
            nm_ref[...] = nm
            nv_ref[...] = nv

    full = pl.BlockSpec((None, tk, n), lambda hf, t, c: (layer, hf * nt + t, 0))
    half_mine = pl.BlockSpec((tk, n), lambda hf, t, c: (jnp.where(hf == c[0], t, 0), 0))
    half_theirs = pl.BlockSpec((tk, n), lambda hf, t, c: (jnp.where(hf != c[0], t, 0), 0))
    out = _sds(ws[0].shape, F32)
    ins, specs, aliases = [cidx], [], {}
    for i in range(cnt):
        ins += [ws[i], ms[i], vs[i], mines[i], theirs[i]]
        specs += [full, full, full, half_mine, half_theirs]
    if filled is not None:
        aliases = {len(ins) + j: j for j in range(4 * cnt)}
        ins += [a for f in filled for a in f]
        specs += [pl.BlockSpec(memory_space=pl.ANY)] * (4 * cnt)
    res = pl.pallas_call(
        body, name=name, out_shape=(out,) * (4 * cnt),
        grid_spec=pltpu.PrefetchScalarGridSpec(
            num_scalar_prefetch=1, grid=(2, nt), in_specs=specs, out_specs=(full,) * (4 * cnt)),
        input_output_aliases=aliases,
        compiler_params=_params(("arbitrary", "arbitrary")))(*ins)
    return [tuple(res[4 * i:4 * i + 4]) for i in range(cnt)]


def t5_table_grad(dt5_a, dt5_b, *, name):
    def body(a_ref, b_ref, map_ref, o_ref):
        d = a_ref[...] + b_ref[...]
        bucket = map_ref[...]
        for b in range(32):
            hit = (bucket == b)[None]
            o_ref[b] = jnp.sum(jnp.sum(jnp.where(hit, d, 0.0), axis=2), axis=1, keepdims=True)

    return pl.pallas_call(
        body, name=name, out_shape=_sds((32, 8, 1), F32), compiler_params=_params())(
            dt5_a, dt5_b, jnp.asarray(t5_bucket_map()))


def rpb_grad(dbias, *, name):
    def body(d_ref, rev_ref, o_ref):
        rev = rev_ref[...]
        for h in range(NA_HEADS):
            for pr in range(NA_WR // 2):
                d = d_ref[h, :, 128 * pr:128 * pr + 128]
                hi = d.astype(BF16)
                lo = (d - hi.astype(F32)).astype(BF16)
                flipped = _dot(rev, hi) + _dot(rev, lo)
                o_ref[h, pr] = jnp.sum(pltpu.roll(flipped, 0, 1, stride=1, stride_axis=0), axis=0, keepdims=True)

    anti = jnp.asarray(np.eye(GRID_W, dtype=np.float32)[::-1], dtype=BF16)
    e = pl.pallas_call(
        body, name=name, out_shape=_sds((NA_WR, NA_HEADS, NA_WR // 2, 1, 128), F32), grid=(NA_WR,),
        in_specs=[pl.BlockSpec((None, NA_HEADS, GRID_W, NA_KEYS), lambda p: (p, 0, 0, 0)),
                  pl.BlockSpec((GRID_W, GRID_W), lambda p: (0, 0))],
        out_specs=pl.BlockSpec((None, NA_HEADS, NA_WR // 2, 1, 128), lambda p: (p, 0, 0, 0, 0)),
        compiler_params=_params(("parallel",)))(dbias, anti)
    nci, nri = 2 * NA_WC - 1, 2 * NA_WR - 1
    e = e.reshape(NA_WR, NA_HEADS, NA_WR // 2, 128).transpose(0, 2, 1, 3).reshape(NA_WR * NA_WR // 2, NA_HEADS, 128)
    parts = jnp.concatenate([e[..., 48:48 + nci], jnp.concatenate([e[..., 112:128], e[..., 0:nci - 16]], axis=-1)], axis=0)
    p, pr = np.arange(NA_WR)[:, None], np.arange(NA_WR // 2)[None, :]
    ri = np.concatenate([(2 * pr - p + NA_WR - 1).reshape(-1), (2 * pr - p + NA_WR).reshape(-1)])
    pick = jnp.asarray((ri[None, :] == np.arange(16)[:, None]).astype(np.float32))
    out = mm(pick, parts.reshape(2 * NA_WR * NA_WR // 2, NA_HEADS * nci), name=name + "_rows", exact=True)
    return out.reshape(16, NA_HEADS, nci)[:nri].transpose(1, 0, 2)


BIG = ("ffn1_w_gate", "ffn1_w_up", "ffn1_w_down", "w_in", "w_branch_na", "w_branch_sw", "w_out",
       "ffn2_w_gate", "ffn2_w_up", "ffn2_w_down")
SMALL = ("ffn1_norm", "mix_norm", "b_gate", "na_q_norm", "na_k_norm", "na_rpb", "sw_q_norm", "sw_k_norm", "sw_sink",
         "ffn2_norm")


def _cols_to_full(w4):
    return w4.transpose(1, 0, 2).reshape(w4.shape[1], NSH * w4.shape[2])


def _full_to_cols(w):
    return w.reshape(w.shape[0], NSH, w.shape[1] // NSH).transpose(1, 0, 2)


def _mixer_weights(g):
    w_in_t = g["w_in"].reshape(IN_W, DM)
    return dict(w_in_t=w_in_t, wa=_cols_to_full(g["w_branch_na"]),
                ws=_cols_to_full(g["w_branch_sw"]), wo=g["w_out"].reshape(DM, DM))


GROUPS = {"ffn1": ("ffn1_w_gate", "ffn1_w_up", "ffn1_w_down"), "mix": ("w_in", "w_branch_na", "w_branch_sw", "w_out"),
          "ffn2": ("ffn2_w_gate", "ffn2_w_up", "ffn2_w_down")}


def layer_fwd(x, p, weights, t5b):
    row = lambda v: v.reshape(1, -1)
    stacked = lambda g: {n: a.reshape(DFF, DM) for n, a in g.items()}
    g1 = stacked(weights("ffn1", x))
    y1, h1, gg1, uu1 = ffn_fwd(x, row(p["ffn1_norm"]), g1["ffn1_w_gate"], g1["ffn1_w_up"], g1["ffn1_w_down"], name="ffn_fwd")
    w = _mixer_weights(weights("mix", y1))
    hm, z, zg = mixer_input_fwd(y1, row(p["mix_norm"]), w["w_in_t"], name="mixer_input_fwd")
    qa, ka, va, qs, kv = qknorm_fwd(z, p["na_q_norm"], p["na_k_norm"], p["sw_q_norm"], p["sw_k_norm"], name="qknorm_fwd")
    bias = p["na_bias"]
    o_na, lse_na = na_fwd(qa, ka, va, bias, name="na_fwd")
    kvp = jnp.pad(kv, ((SW_BLK, SW_BLK), (0, 0)))
    sink = p["sw_sink"]
    o_sw, lse_sw = sw_fwd(qs, kvp, t5b, sink, name="sw_fwd")
    y2, pa, ps, merged = mixer_output_fwd(o_na, o_sw, zg, row(p["b_gate"]), w["wa"], w["ws"], w["wo"], y1,
                                          name="mixer_output_fwd")
    g2 = stacked(weights("ffn2", y2))
    y3, h2, gg2, uu2 = ffn_fwd(y2, row(p["ffn2_norm"]), g2["ffn2_w_gate"], g2["ffn2_w_up"], g2["ffn2_w_down"], name="ffn_fwd")
    saved = dict(x=x, y1=y1, h1=h1, gg1=gg1, uu1=uu1, hm=hm, z=z, zg=zg, qa=qa, ka=ka, va=va, qs=qs, kvp=kvp, bias=bias,
                 o_na=o_na, lse_na=lse_na, o_sw=o_sw, lse_sw=lse_sw, pa=pa, ps=ps, merged=merged, y2=y2, h2=h2, gg2=gg2,
                 uu2=uu2, w=w, sink=sink, g1=g1, g2=g2)
    return y3, saved


def layer_bwd(dy3, dy3_bf, sv, p, t5b, emit, dep=None):
    w, g1, g2 = sv["w"], sv["g1"], sv["g2"]
    row = lambda v: v.reshape(1, -1)
    fold = lambda v: v.reshape(-1, HD).sum(axis=0)
    small = {}
    dy2, _, small["ffn2_norm"], act, dg, du = ffn_bwd_tokens(
        dy3, sv["y2"], row(p["ffn2_norm"]), sv["gg2"], sv["uu2"], g2["ffn2_w_gate"], g2["ffn2_w_up"], g2["ffn2_w_down"],
        name="ffn_bwd_tokens", dep=dep)
    shards = lambda gs: [g.reshape(NSH, FSH, DM) for g in gs]
    token = emit("ffn2", shards(ffn_bwd_weights(sv["h2"], dy3_bf, act, dg, du, name="ffn_bwd_weights")))
    dpa, dps, dzg, small["b_gate"], do_na, do_sw = mixer_output_bwd(
        dy2, sv["zg"], row(p["b_gate"]), sv["pa"], sv["ps"], w["wa"], w["ws"], w["wo"], name="mixer_output_bwd", dep=token)
    gw_out = mm(sv["merged"], dy2, ta=True, out_dtype=BF16, name="out_proj_dw").reshape(NSH, DM // NSH, DM)
    gw_na = _full_to_cols(mm(sv["o_na"], dpa, ta=True, out_dtype=BF16, name="branch_dw"))
    gw_sw = _full_to_cols(mm(sv["o_sw"], dps, ta=True, out_dtype=BF16, name="branch_dw"))
    dqa, dka, dva, dbias = na_bwd(sv["qa"], sv["ka"], sv["va"], sv["o_na"], do_na, sv["lse_na"], sv["bias"], name="na_bwd")
    dqs, dkvp, dt5, dsink = sw_bwd(sv["qs"], sv["kvp"], sv["o_sw"], do_sw, sv["lse_sw"], t5b, sv["sink"], name="sw_bwd")
    dkv = dkvp[SW_BLK:SW_BLK + SEQ]
    dz, dgqa, dgka, dgqs, dgks = qknorm_bwd(sv["z"], dqa, dka, dva, dqs, dkv, p["na_q_norm"], p["na_k_norm"],
                                            p["sw_q_norm"], p["sw_k_norm"], name="qknorm_bwd")
    small["na_q_norm"], small["na_k_norm"], small["sw_q_norm"], small["sw_k_norm"] = fold(dgqa), fold(dgka), fold(dgqs), fold(dgks)
    small["na_rpb"] = rpb_grad(dbias, name="rpb_grad")
    small["sw_sink"] = dsink
    gw_att_t = mm(dz, sv["hm"], ta=True, out_dtype=BF16, tm=768, name="proj_att_dw")
    gw_gz_t = mm(dzg, sv["hm"], ta=True, out_dtype=BF16, tm=1024, name="proj_gate_dw")
    gw_in = jnp.concatenate([gw_att_t, gw_gz_t], axis=0).reshape(NSH, IN_W // NSH, DM)
    token = emit("mix", (gw_in, gw_na, gw_sw, gw_out))
    dy1, dy1_bf, small["mix_norm"] = mixer_input_bwd(dz, dzg, w["w_in_t"], sv["y1"], row(p["mix_norm"]), dy2,
                                                     name="mixer_input_bwd", dep=token)
    dx, dx_bf, small["ffn1_norm"], act, dg, du = ffn_bwd_tokens(
        dy1, sv["x"], row(p["ffn1_norm"]), sv["gg1"], sv["uu1"], g1["ffn1_w_gate"], g1["ffn1_w_up"], g1["ffn1_w_down"],
        name="ffn_bwd_tokens")
    emit("ffn1", shards(ffn_bwd_weights(sv["h1"], dy1_bf, act, dg, du, name="ffn_bwd_weights")))
    return dx, dx_bf, small, dt5


ANY = pl.BlockSpec(memory_space=pl.ANY)


def _place():
    x, y, c = lax.axis_index("x"), lax.axis_index("y"), lax.axis_index("c")
    chips = [(1 - x, y), (x, 1 - y), (1 - x, 1 - y)]
    return x, y, c, chips


def _remote(src, dst, send_sem, recv_sem, to):
    return pltpu.make_async_remote_copy(src_ref=src, dst_ref=dst, send_sem=send_sem, recv_sem=recv_sem, device_id=to,
                                        device_id_type=MESH)


HBM = pl.BlockSpec(memory_space=pltpu.HBM)
SEM = pl.BlockSpec(memory_space=pltpu.SEMAPHORE)
ORDERED_EFFECT = pltpu.SideEffectType.DATAFLOW_SIDE_EFFECTING


def _in_hbm(v):
    return pltpu.with_memory_space_constraint(v, pltpu.HBM)


def _row_half(ref_shape_rows, c):
    half = ref_shape_rows // 2
    return pl.ds(c * half, half)


def _ici_gather_copies(w, land, send_sems, recv_sems):
    x, y, c, chips = _place()
    me = 2 * x + y
    copies = []
    for a in range(len(w)):
        rows = _row_half(w[a].shape[0], c)
        for k, chip in enumerate(chips):
            copies.append(_remote(w[a].at[rows], land[a].at[me, rows], send_sems.at[4 * a + k], recv_sems.at[4 * a + k],
                                  (*chip, c)))
        copies.append(_remote(w[a], land[a].at[me], send_sems.at[4 * a + 3], recv_sems.at[4 * a + 3], (x, y, 1 - c)))
    return copies


def _d2d_gather_copies(w, land, send_sems, recv_sems):
    x, y, c, chips = _place()
    copies = []
    for a in range(len(w)):
        rows = _row_half(w[a].shape[0], c)
        for k, (cx, cy) in enumerate(chips):
            blk = land[a].at[2 * cx + cy, rows]
            copies.append(_remote(blk, blk, send_sems.at[3 * a + k], recv_sems.at[3 * a + k], (x, y, 1 - c)))
    return copies


def _d2d_gather_waits(w, land, send_sems, recv_sems):
    x, y, c, chips = _place()
    waits = []
    for a in range(len(w)):
        rows = _row_half(w[a].shape[0], 1 - c)
        for k, (cx, cy) in enumerate(chips):
            blk = land[a].at[2 * cx + cy, rows]
            waits.append(_remote(blk, blk, send_sems.at[3 * a + k], recv_sems.at[3 * a + k], (x, y, 1 - c)))
    return waits


def gather_start(groups, dep=None, *, name):
    sizes = [len(g) for g in groups]
    shards = [s for g in groups for s in g]
    n, ng = len(shards), len(groups)
    extra = [] if dep is None else [dep]

    def body(*refs):
        first_out = 2 * n + len(extra)
        w, land, sems = refs[:n], refs[n:2 * n], refs[first_out:first_out + 2 * ng]
        off = 0
        for gi, size in enumerate(sizes):
            for cp in _ici_gather_copies(w[off:off + size], land[off:off + size], sems[2 * gi], sems[2 * gi + 1]):
                cp.start()
            off += size

    lands = [lax.empty((NSH,) + s.shape, s.dtype) for s in shards]
    sem_shapes = tuple(pltpu.SemaphoreType.DMA((4 * size,)) for size in sizes for _ in range(2))
    res = pl.pallas_call(
        body, name=name,
        out_shape=sem_shapes + tuple(pltpu.HBM(s.shape, s.dtype) for s in shards) + tuple(pltpu.HBM(l.shape, l.dtype) for l in lands),
        in_specs=[HBM] * (2 * n) + [ANY] * len(extra), out_specs=(SEM,) * (2 * ng) + (HBM,) * (2 * n),
        input_output_aliases={i: 2 * ng + i for i in range(2 * n)},
        compiler_params=pltpu.CompilerParams(has_side_effects=ORDERED_EFFECT))(
            *[_in_hbm(s) for s in shards], *[_in_hbm(l) for l in lands], *extra)
    out, off = [], 0
    for gi, size in enumerate(sizes):
        out.append((res[2 * gi], res[2 * gi + 1], list(res[2 * ng + off:2 * ng + off + size]),
                    list(res[2 * ng + n + off:2 * ng + n + off + size])))
        off += size
    return out


def gather_wait(send_sems, recv_sems, shards, lands, after, *, name):
    n = len(shards)

    def body(*refs):
        w, land = refs[:n], refs[n:2 * n]
        send, recv = refs[2 * n:2 * n + 2]
        for cp in _ici_gather_copies(w, land, send, recv):
            cp.wait_send()
            cp.wait_recv()

    res = pl.pallas_call(
        body, name=name,
        out_shape=tuple(pltpu.HBM(s.shape, s.dtype) for s in shards) + tuple(pltpu.HBM(l.shape, l.dtype) for l in lands),
        in_specs=[HBM] * (2 * n) + [SEM, SEM] + [ANY] * len(after), out_specs=(HBM,) * (2 * n),
        input_output_aliases={i: i for i in range(2 * n)},
        compiler_params=pltpu.CompilerParams(has_side_effects=ORDERED_EFFECT))(*shards, *lands, send_sems, recv_sems, *after)
    return list(res[:n]), list(res[n:])


def gather_finish(shards, lands, *, name):
    n = len(shards)

    def body(*refs):
        w, land = refs[:n], refs[n:2 * n]
        send_sems, recv_sems = refs[3 * n:]
        d2d = _d2d_gather_copies(w, land, send_sems, recv_sems)
        for cp in d2d:
            cp.start()
        for cp in _d2d_gather_waits(w, land, send_sems, recv_sems):
            cp.wait_recv()
        for cp in d2d:
            cp.wait_send()

    return list(pl.pallas_call(
        body, name=name, out_shape=tuple(pltpu.HBM(l.shape, l.dtype) for l in lands),
        in_specs=[ANY] * (2 * n), out_specs=tuple([ANY] * n), input_output_aliases={n + i: i for i in range(n)},
        scratch_shapes=[pltpu.SemaphoreType.DMA((3 * n,)), pltpu.SemaphoreType.DMA((3 * n,))])(*shards, *lands))


def _pair_exchange_copies(g, buf, send_sems, recv_sems):
    x, y, c, _ = _place()
    copies = []
    for a in range(len(g)):
        half = g[a].shape[1] // 2
        copies.append(_remote(g[a].at[:, pl.ds((1 - c) * half, half)], buf[a], send_sems.at[a], recv_sems.at[a], (x, y, 1 - c)))
    return copies


def pair_exchange_start(grads, dep=None, *, name):
    n = len(grads)
    extra = [] if dep is None else [dep]

    def body(*refs):
        sems = refs[2 * n + len(extra):]
        for cp in _pair_exchange_copies(refs[:n], refs[n:2 * n], sems[0], sems[1]):
            cp.start()
        refs[-1][...] = jnp.zeros_like(refs[-1])

    lands = [lax.empty((NSH, g.shape[1] // 2, g.shape[2]), g.dtype) for g in grads]
    res = pl.pallas_call(
        body, name=name,
        out_shape=(pltpu.SemaphoreType.DMA((n,)), pltpu.SemaphoreType.DMA((n,)))
        + tuple(pltpu.HBM(g.shape, g.dtype) for g in grads) + tuple(pltpu.HBM(l.shape, l.dtype) for l in lands)
        + (_sds((8, 128), F32),),
        in_specs=[HBM] * (2 * n) + [ANY] * len(extra),
        out_specs=(SEM, SEM) + (HBM,) * (2 * n) + (pl.BlockSpec(memory_space=pltpu.VMEM),),
        input_output_aliases={i: 2 + i for i in range(2 * n)},
        compiler_params=pltpu.CompilerParams(has_side_effects=ORDERED_EFFECT))(
            *[_in_hbm(g) for g in grads], *[_in_hbm(l) for l in lands], *extra)
    return res[0], res[1], list(res[2:2 + n]), list(res[2 + n:2 + 2 * n]), res[-1]


def pair_exchange_wait(send_sems, recv_sems, grads, lands, after, *, name):
    n = len(grads)

    def body(*refs):
        for cp in _pair_exchange_copies(refs[:n], refs[n:2 * n], refs[2 * n], refs[2 * n + 1]):
            cp.wait_send()
            cp.wait_recv()

    res = pl.pallas_call(
        body, name=name,
        out_shape=tuple(pltpu.HBM(g.shape, g.dtype) for g in grads) + tuple(pltpu.HBM(l.shape, l.dtype) for l in lands),
        in_specs=[HBM] * (2 * n) + [SEM, SEM] + [ANY] * len(after), out_specs=(HBM,) * (2 * n),
        input_output_aliases={i: i for i in range(2 * n)},
        compiler_params=pltpu.CompilerParams(has_side_effects=ORDERED_EFFECT))(*grads, *lands, send_sems, recv_sems, *after)
    return list(res[:n]), list(res[n:])


def _chip_exchange_copies(s, buf, send_sems, recv_sems):
    x, y, c, chips = _place()
    return [_remote(s[a].at[2 * cx + cy], buf[a].at[k], send_sems.at[3 * a + k], recv_sems.at[3 * a + k], (cx, cy, c))
            for a in range(len(s)) for k, (cx, cy) in enumerate(chips)]


def exchange_start(sums, grads, *, name):
    n1, n2 = len(sums), len(grads)

    def body(*refs):
        first_out = 2 * (n1 + n2)
        chip = _chip_exchange_copies(refs[:n1], refs[n1:2 * n1], refs[first_out], refs[first_out + 1])
        pair = _pair_exchange_copies(refs[2 * n1:2 * n1 + n2], refs[2 * n1 + n2:first_out], refs[first_out + 2],
                                     refs[first_out + 3])
        for cp in chip + pair:
            cp.start()
        refs[-1][...] = jnp.zeros_like(refs[-1])

    chip_lands = [lax.empty((3,) + s.shape[1:], s.dtype) for s in sums]
    pair_lands = [lax.empty((NSH, g.shape[1] // 2, g.shape[2]), g.dtype) for g in grads]
    arrays = list(sums) + chip_lands + list(grads) + pair_lands
    res = pl.pallas_call(
        body, name=name,
        out_shape=(pltpu.SemaphoreType.DMA((3 * n1,)), pltpu.SemaphoreType.DMA((3 * n1,)), pltpu.SemaphoreType.DMA((n2,)),
                   pltpu.SemaphoreType.DMA((n2,)))
        + tuple(pltpu.HBM(a.shape, a.dtype) for a in arrays) + (_sds((8, 128), F32),),
        in_specs=[HBM] * len(arrays), out_specs=(SEM,) * 4 + (HBM,) * len(arrays) + (pl.BlockSpec(memory_space=pltpu.VMEM),),
        input_output_aliases={i: 4 + i for i in range(len(arrays))},
        compiler_params=pltpu.CompilerParams(has_side_effects=ORDERED_EFFECT))(*[_in_hbm(a) for a in arrays])
    thru = list(res[4:4 + len(arrays)])
    chip = (res[0], res[1], thru[:n1], thru[n1:2 * n1])
    pair = (res[2], res[3], thru[2 * n1:2 * n1 + n2], thru[2 * n1 + n2:])
    return chip, pair, res[-1]


def chip_exchange_start(sums, *, name):
    n = len(sums)

    def body(*refs):
        for cp in _chip_exchange_copies(refs[:n], refs[n:2 * n], refs[2 * n], refs[2 * n + 1]):
            cp.start()
        refs[-1][...] = jnp.zeros_like(refs[-1])

    lands = [lax.empty((3,) + s.shape[1:], s.dtype) for s in sums]
    res = pl.pallas_call(
        body, name=name,
        out_shape=(pltpu.SemaphoreType.DMA((3 * n,)), pltpu.SemaphoreType.DMA((3 * n,)))
        + tuple(pltpu.HBM(s.shape, s.dtype) for s in sums) + tuple(pltpu.HBM(l.shape, l.dtype) for l in lands)
        + (_sds((8, 128), F32),),
        in_specs=[HBM] * (2 * n), out_specs=(SEM, SEM) + (HBM,) * (2 * n) + (pl.BlockSpec(memory_space=pltpu.VMEM),),
        input_output_aliases={i: 2 + i for i in range(2 * n)},
        compiler_params=pltpu.CompilerParams(has_side_effects=ORDERED_EFFECT))(
            *[_in_hbm(s) for s in sums], *[_in_hbm(l) for l in lands])
    return res[0], res[1], list(res[2:2 + n]), list(res[2 + n:2 + 2 * n]), res[-1]


def chip_exchange_wait(send_sems, recv_sems, sums, lands, after, *, name):
    n = len(sums)

    def body(*refs):
        for cp in _chip_exchange_copies(refs[:n], refs[n:2 * n], refs[2 * n], refs[2 * n + 1]):
            cp.wait_send()
            cp.wait_recv()

    res = pl.pallas_call(
        body, name=name,
        out_shape=tuple(pltpu.HBM(s.shape, s.dtype) for s in sums) + tuple(pltpu.HBM(l.shape, l.dtype) for l in lands),
        in_specs=[HBM] * (2 * n) + [SEM, SEM] + [ANY] * len(after), out_specs=(HBM,) * (2 * n),
        input_output_aliases={i: i for i in range(2 * n)},
        compiler_params=pltpu.CompilerParams(has_side_effects=ORDERED_EFFECT))(*sums, *lands, send_sems, recv_sems, *after)
    return list(res[:n]), list(res[n:])


def _pair_send_copies(h, got, send_sems, recv_sems):
    x, y, c, _ = _place()
    return [_remote(h[i], got[i], send_sems.at[i], recv_sems.at[i], (x, y, 1 - c)) for i in range(len(h))]


def pair_send_start(halves, *, name):
    n = len(halves)

    def body(*refs):
        for cp in _pair_send_copies(refs[:n], refs[n:2 * n], refs[2 * n], refs[2 * n + 1]):
            cp.start()
        refs[-1][...] = jnp.zeros_like(refs[-1])

    lands = [lax.empty(h.shape, h.dtype) for h in halves]
    res = pl.pallas_call(
        body, name=name,
        out_shape=(pltpu.SemaphoreType.DMA((n,)), pltpu.SemaphoreType.DMA((n,)))
        + tuple(pltpu.HBM(h.shape, h.dtype) for h in halves) * 2 + (_sds((8, 128), F32),),
        in_specs=[HBM] * (2 * n), out_specs=(SEM, SEM) + (HBM,) * (2 * n) + (pl.BlockSpec(memory_space=pltpu.VMEM),),
        input_output_aliases={i: 2 + i for i in range(2 * n)},
        compiler_params=pltpu.CompilerParams(has_side_effects=ORDERED_EFFECT))(
            *[_in_hbm(h) for h in halves], *[_in_hbm(l) for l in lands])
    return res[0], res[1], list(res[2:2 + n]), list(res[2 + n:2 + 2 * n]), res[-1]


def pair_send_wait(send_sems, recv_sems, halves, lands, after, *, name):
    n = len(halves)

    def body(*refs):
        for cp in _pair_send_copies(refs[:n], refs[n:2 * n], refs[2 * n], refs[2 * n + 1]):
            cp.wait_send()
            cp.wait_recv()

    res = pl.pallas_call(
        body, name=name, out_shape=tuple(pltpu.HBM(h.shape, h.dtype) for h in halves) * 2,
        in_specs=[HBM] * (2 * n) + [SEM, SEM] + [ANY] * len(after), out_specs=(HBM,) * (2 * n),
        input_output_aliases={i: i for i in range(2 * n)},
        compiler_params=pltpu.CompilerParams(has_side_effects=ORDERED_EFFECT))(*halves, *lands, send_sems, recv_sems, *after)
    return list(res[:n]), list(res[n:])


def allreduce_small(v, *, name):
    rows = v.shape[0]

    def body(v_ref, o_ref, gath, send_sems, recv_sems):
        x, y, c, _ = _place()
        me = 4 * x + 2 * y + c
        gath[me] = v_ref[...]
        copies = []
        for k in range(1, 8):
            fx, fy, fc = (k >> 2) & 1, (k >> 1) & 1, k & 1
            peer = (jnp.where(fx, 1 - x, x), jnp.where(fy, 1 - y, y), jnp.where(fc, 1 - c, c))
            cp = _remote(v_ref, gath.at[me], send_sems.at[k - 1], recv_sems.at[k - 1], peer)
            cp.start()
            copies.append(cp)
        for cp in copies:
            cp.wait()
        acc = gath[0]
        for d in range(1, 8):
            acc = acc + gath[d]
        o_ref[...] = acc

    return pl.pallas_call(
        body, name=name, out_shape=_sds(v.shape, F32),
        in_specs=[pl.BlockSpec(memory_space=pltpu.VMEM)], out_specs=pl.BlockSpec(memory_space=pltpu.VMEM),
        scratch_shapes=[pltpu.VMEM((8, rows, 128), F32), pltpu.SemaphoreType.DMA((7,)), pltpu.SemaphoreType.DMA((7,))])(v)


def _same_shape_runs(arrays):
    runs = {}
    for i, a in enumerate(arrays):
        runs.setdefault(a.shape, []).append(i)
    return list(runs.values())


def _per_shape(fn, *lists):
    out = [None] * len(lists[0])
    for idx in _same_shape_runs(lists[0]):
        for i, r in zip(idx, fn(*[[l[i] for i in idx] for l in lists])):
            out[i] = r
    return out


def add_halves(gs, bufs, cidx, *, name):
    cnt = len(gs)
    _, k, n = gs[0].shape

    def body(c_ref, *refs):
        g, b, o = refs[:cnt], refs[cnt:2 * cnt], refs[2 * cnt:]
        for i in range(cnt):
            o[i][...] = (g[i][...].astype(F32) + b[i][...].astype(F32)).astype(BF16)

    blk = pl.BlockSpec((None, k // 2, n), lambda s, c: (s, 0, 0))
    mine = pl.BlockSpec((None, k // 2, n), lambda s, c: (s, c[0], 0))
    return list(pl.pallas_call(
        body, name=name, out_shape=tuple(_sds(b.shape, BF16) for b in bufs),
        grid_spec=pltpu.PrefetchScalarGridSpec(
            num_scalar_prefetch=1, grid=(NSH,), in_specs=[mine] * cnt + [blk] * cnt, out_specs=tuple([blk] * cnt)),
        compiler_params=_params(("parallel",)))(cidx, *gs, *bufs))


def add_chips(sums, bufs, sidx, *, name):
    cnt = len(sums)
    _, kh, n = sums[0].shape

    def body(s_ref, *refs):
        mine, b, o = refs[:cnt], refs[cnt:2 * cnt], refs[2 * cnt:]
        for i in range(cnt):
            o[i][...] = ((mine[i][...].astype(F32) + b[i][0].astype(F32)) + (b[i][1].astype(F32) + b[i][2].astype(F32)))

    own = pl.BlockSpec((None, kh, n), lambda i, s: (s[0], 0, 0))
    got = pl.BlockSpec((3, kh, n), lambda i, s: (0, 0, 0))
    out = pl.BlockSpec((kh, n), lambda i, s: (0, 0))
    return list(pl.pallas_call(
        body, name=name, out_shape=tuple(_sds((kh, n), F32) for _ in sums),
        grid_spec=pltpu.PrefetchScalarGridSpec(
            num_scalar_prefetch=1, grid=(1,), in_specs=[own] * cnt + [got] * cnt, out_specs=tuple([out] * cnt)),
        compiler_params=_params(("arbitrary",)))(sidx, *sums, *bufs))


PARAMS = ("ffn1_norm", "ffn1_w_gate", "ffn1_w_up", "ffn1_w_down", "mix_norm", "w_in", "b_gate", "na_q_norm", "na_k_norm",
          "na_rpb", "sw_q_norm", "sw_k_norm", "sw_sink", "t5_rel_table", "w_branch_na", "w_branch_sw", "w_out", "ffn2_norm",
          "ffn2_w_gate", "ffn2_w_up", "ffn2_w_down")
SMALL_ALL = tuple(n for n in PARAMS if n not in BIG)
TRANSPOSED = ("ffn1_w_gate", "ffn1_w_up", "w_in", "ffn2_w_gate", "ffn2_w_up")
SMALL_ROWS = 152


def _pack_small(vals):
    flat = jnp.concatenate([vals[n].reshape(-1).astype(F32) for n in SMALL_ALL] + [vals["loss"].reshape(-1)])
    return jnp.pad(flat, (0, SMALL_ROWS * 128 - flat.shape[0])).reshape(SMALL_ROWS, 128)


def _unpack_small(packed, like):
    flat, out, off = packed.reshape(-1), {}, 0
    for n in SMALL_ALL:
        size = math.prod(like[n].shape)
        out[n] = flat[off:off + size].reshape(like[n].shape)
        off += size
    out["loss"] = flat[off]
    return out


def kernel(x, ffn1_norm, ffn1_w_gate, ffn1_w_up, ffn1_w_down, mix_norm, w_in, b_gate, na_q_norm, na_k_norm, na_rpb, sw_q_norm, sw_k_norm, sw_sink, t5_rel_table, w_branch_na, w_branch_sw, w_out, ffn2_norm, ffn2_w_gate, ffn2_w_up, ffn2_w_down, loss_target, m_ffn1_norm, m_ffn1_w_gate, m_ffn1_w_up, m_ffn1_w_down, m_mix_norm, m_w_in, m_b_gate, m_na_q_norm, m_na_k_norm, m_na_rpb, m_sw_q_norm, m_sw_k_norm, m_sw_sink, m_t5_rel_table, m_w_branch_na, m_w_branch_sw, m_w_out, m_ffn2_norm, m_ffn2_w_gate, m_ffn2_w_up, m_ffn2_w_down, v_ffn1_norm, v_ffn1_w_gate, v_ffn1_w_up, v_ffn1_w_down, v_mix_norm, v_w_in, v_b_gate, v_na_q_norm, v_na_k_norm, v_na_rpb, v_sw_q_norm, v_sw_k_norm, v_sw_sink, v_t5_rel_table, v_w_branch_na, v_w_branch_sw, v_w_out, v_ffn2_norm, v_ffn2_w_gate, v_ffn2_w_up, v_ffn2_w_down):
    args = locals()
    tr = lambda n, a: jnp.transpose(a, (0, 2, 1)) if n in TRANSPOSED else a
    w = {n: tr(n, args[n]) for n in PARAMS}
    m = {n: tr(n, args["m_" + n]) for n in PARAMS}
    v = {n: tr(n, args["v_" + n]) for n in PARAMS}
    cidx = lax.axis_index("c").astype(jnp.int32).reshape(1)
    sidx = (2 * lax.axis_index("x") + lax.axis_index("y")).astype(jnp.int32).reshape(1)

    small = [{n: w[n][l] for n in SMALL} for l in range(DEPTH)]
    order = ("ffn1", "mix", "ffn2")

    keys = [(l, g) for l in range(DEPTH) for g in order]
    local = lambda l, g: [w[n][l].astype(BF16) for n in GROUPS[g]]
    first = gather_start([local(*keys[0])], name="gather_start")
    rest = gather_start([local(*key) for key in keys[1:]], first[0][2][0], name="gather_start")
    in_flight = dict(zip(keys, first + rest))
    t5b = t5_bias(w["t5_rel_table"], name="t5_bias")
    for l in range(DEPTH):
        small[l]["na_bias"] = na_bias_table(small[l]["na_rpb"], name="na_bias_table")
    early = [t5b] + [small[l]["na_bias"] for l in range(DEPTH)] + [rest[0][2][0]]

    def weights_of(l):
        def get(group, after):
            send_sems, recv_sems, thru, lands = in_flight[(l, group)]
            after = [after] + (early if (l, group) == keys[0] else [])
            thru, lands = gather_wait(send_sems, recv_sems, thru, lands, after, name="gather_wait")
            return dict(zip(GROUPS[group], gather_finish(thru, lands, name="gather_finish")))
        return get

    h0, saved0 = layer_fwd(x[0], small[0], weights_of(0), t5b)
    h1, saved1 = layer_fwd(h0, small[1], weights_of(1), t5b)
    dy, dy_bf, loss_row = loss_head(h1, loss_target[0], name="loss_head")

    crossing, tokens, pending = {}, [], []

    def ship(after, then=None):
        key, send_sems, recv_sems, grads, lands = pending.pop()
        grads, from_sibling = pair_exchange_wait(send_sems, recv_sems, grads, lands, after, name="pair_exchange_wait")
        sums = _per_shape(lambda gs, bs: add_halves(gs, bs, cidx, name="add_halves"), grads, from_sibling)
        if then is None:
            send_sems, recv_sems, sums, lands, token = chip_exchange_start(sums, name="chip_exchange_start")
            crossing[key] = (send_sems, recv_sems, sums, lands)
            return token
        crossing[key], pair, token = exchange_start(sums, then[1], name="exchange_start")
        pending.append((then[0],) + pair)
        return token

    def reduce_of(l):
        def emit(group, grads):
            grads = list(grads)
            if pending:
                token = ship([grads[0]], then=((l, group), grads))
            else:
                send_sems, recv_sems, grads, lands, token = pair_exchange_start(grads, name="pair_exchange_start")
                pending.append(((l, group), send_sems, recv_sems, grads, lands))
            tokens.append(token)
            return token
        return emit

    def finish(layer, after, filled=None):
        sent = {}
        for group in order:
            send_sems, recv_sems, sums, lands = crossing[(layer, group)]
            sums, got = chip_exchange_wait(send_sems, recv_sems, sums, lands, after, name="chip_exchange_wait")
            halves = _per_shape(lambda ss, bs: add_chips(ss, bs, sidx, name="add_chips"), sums, got)
            sent[group] = pair_send_start(halves, name="pair_send_start")
            after = [sent[group][4]]
        out = {}
        for group in order:
            send_sems, recv_sems, halves, lands, _ = sent[group]
            halves, theirs = pair_send_wait(send_sems, recv_sems, halves, lands, after, name="pair_send_wait")
            names = GROUPS[group]
            res = _per_shape(
                lambda ws, ms, vs, a, b, *f: adamw_layer(ws, ms, vs, a, b, cidx, layer, list(f[0]) if f else None, name="adamw_layer"),
                *([[w[n] for n in names], [m[n] for n in names], [v[n] for n in names], halves, theirs]
                  + ([[filled[n] for n in names]] if filled is not None else [])))
            out.update(zip(names, res))
            after = [res[-1][0]]
        return out

    dy, dy_bf, small1, dt5_1 = layer_bwd(dy, dy_bf, saved1, small[1], t5b, reduce_of(1))
    grad_x, _, small0, dt5_0 = layer_bwd(dy, dy_bf, saved0, small[0], t5b, reduce_of(0), dep=tokens[-1])
    done1 = finish(1, [ship([grad_x])])

    smalls = [small0, small1]
    dt5 = t5_table_grad(dt5_0, dt5_1, name="t5_table_grad").reshape(32, 8)
    local_small = {n: jnp.stack([smalls[l][n].reshape(w[n].shape[1:]) for l in range(DEPTH)]) for n in SMALL}
    local_small["t5_rel_table"] = dt5
    local_small["loss"] = loss_row[0, 0:1]
    total = allreduce_small(_pack_small(local_small), name="allreduce_small")
    small_grads = _unpack_small(total, w)
    small_done = adamw_small([w[n] for n in SMALL_ALL], [small_grads[n] for n in SMALL_ALL], [m[n] for n in SMALL_ALL],
                             [v[n] for n in SMALL_ALL], name="adamw_small")

    grad, delta, new_m, new_v = {}, {}, {}, {}
    for n, done in finish(0, [small_done[0][0], done1[BIG[-1]][0]], filled=done1).items():
        grad[n], delta[n], new_m[n], new_v[n] = done
    for n, done in zip(SMALL_ALL, small_done):
        grad[n] = small_grads[n]
        delta[n], new_m[n], new_v[n] = done

    return (small_grads["loss"], grad_x[None], *[tr(n, grad[n]) for n in PARAMS], *[tr(n, delta[n]) for n in PARAMS],
            *[tr(n, new_m[n]) for n in PARAMS], *[tr(n, new_v[n]) for n in PARAMS])
```

```python
import math

import jax
import jax.numpy as jnp
import numpy as np
from jax import lax
from jax.experimental import pallas as pl
from jax.experimental.pallas import tpu as pltpu

F32 = jnp.float32
BF16 = jnp.bfloat16

SEQ = 2048
DM = 1024
DFF = 2816
DEPTH = 2
NSH = 4
FSH = DFF // NSH
GRID_W = 64
ROWS = SEQ // GRID_W
NA_HEADS = 8
HD = 64
NA_WR = 8
NA_WC = 16
NA_KEYS = NA_WR * GRID_W
SW_BLK = 128
SW_NB = SEQ // SW_BLK
SW_KEYS = 3 * SW_BLK
ATT_W = 2304
GATE_W = 2048
IN_W = ATT_W + GATE_W
EPS = 1e-6
NEG = -1e30
QK_SCALE = 1.0 / math.sqrt(HD)

ADAM_LR = 0.001
ADAM_B1 = 0.9
ADAM_B2 = 0.999
ADAM_EPS = 1e-08
ADAM_WD = 0.01
ADAM_STEP = 10

VMEM_LIMIT = 56 << 20
MESH = pl.DeviceIdType.MESH

NT = (((1,), (1,)), ((), ()))
TN = (((0,), (0,)), ((), ()))
NN = (((1,), (0,)), ((), ()))


def _dot(a, b, dims=NN):
    return lax.dot_general(a, b, dims, preferred_element_type=F32)


def _params(sem=None):
    return pltpu.CompilerParams(dimension_semantics=sem, vmem_limit_bytes=VMEM_LIMIT)


def _sds(shape, dtype):
    return jax.ShapeDtypeStruct(shape, dtype)


def mm(a, b, *, name, ta=False, tb=False, out_dtype=F32, add=None, scale=None, tm=512, tn=None, tk=None, exact=False,
       dep=None, b_rows=None):
    m, kd = (a.shape[1], a.shape[0]) if ta else a.shape
    if b_rows is None:
        n = b.shape[0] if tb else b.shape[1]
    else:
        n = b_rows[1] if tb else b.shape[1]
        assert tb or (b_rows[1] == kd and (tk or kd) == kd)
    tm, tn, tk = min(tm, m), min(tn or n, n), min(tk or kd, kd)
    nk = kd // tk
    dims = (((0 if ta else 1,), (1 if tb else 0,)), ((), ()))

    def body(*refs):
        a_ref, b_ref = refs[:2]
        add_ref = refs[2] if add is not None else None
        o_ref = refs[-1] if nk == 1 else refs[-2]
        if b_rows is None:
            bv = b_ref[...]
        elif tb:
            bv = b_ref[pl.ds(pl.multiple_of(b_rows[0] + pl.program_id(1) * tn, 16), tn), :]
        else:
            bv = b_ref[b_rows[0]:b_rows[0] + b_rows[1], :]
        if exact:
            part = lax.dot_general(a_ref[...], bv, dims, precision=lax.Precision.HIGHEST, preferred_element_type=F32)
        else:
            part = lax.dot_general(a_ref[...].astype(BF16), bv.astype(BF16), dims, preferred_element_type=F32)

        def finish(r):
            if scale is not None:
                r = r * scale
            if add is not None:
                r = r + add_ref[...]
            o_ref[...] = r.astype(out_dtype)

        if nk == 1:
            finish(part)
        else:
            acc, k = refs[-1], pl.program_id(2)

            @pl.when(k == 0)
            def _():
                acc[...] = part

            @pl.when(k != 0)
            def _():
                acc[...] += part

            pl.when(k == nk - 1)(lambda: finish(acc[...]))

    a_spec = pl.BlockSpec((tk, tm), lambda i, j, k: (k, i)) if ta else pl.BlockSpec((tm, tk), lambda i, j, k: (i, k))
    if b_rows is not None:
        b_spec = pl.BlockSpec(b.shape, lambda i, j, k: (0, 0), pipeline_mode=pl.Buffered(1))
    else:
        b_spec = pl.BlockSpec((tn, tk), lambda i, j, k: (j, k)) if tb else pl.BlockSpec((tk, tn), lambda i, j, k: (k, j))
    o_spec = pl.BlockSpec((tm, tn), lambda i, j, k: (i, j))
    ins, specs = [a, b], [a_spec, b_spec]
    if add is not None:
        ins.append(add)
        specs.append(o_spec)
    if dep is not None:
        ins.append(dep)
        specs.append(pl.BlockSpec(memory_space=pl.ANY))
    return pl.pallas_call(
        body, name=name, out_shape=_sds((m, n), out_dtype), grid=(m // tm, n // tn, nk), in_specs=specs,
        out_specs=o_spec, scratch_shapes=[] if nk == 1 else [pltpu.VMEM((tm, tn), F32)],
        compiler_params=_params(("parallel", "parallel", "arbitrary")))(*ins)


def _rms(x):
    return lax.rsqrt(jnp.mean(x * x, axis=-1, keepdims=True) + EPS)


def mixer_input_fwd(x, gain, w_in_t, gq_na, gk_na, gq_sw, gk_sw, *, name, tm=512):
    def body(x_ref, g_ref, w_ref, gqa_ref, gka_ref, gqs_ref, gks_ref, bd_ref, bd2_ref,
             h_ref, z_ref, zg_ref, qa_ref, ka_ref, va_ref, qs_ref, kv_ref):
        x = x_ref[...]
        h = (x * _rms(x) * g_ref[...]).astype(BF16)
        h_ref[...] = h
        z = _dot(h, w_ref[0:ATT_W, :], NT).astype(BF16)
        z_ref[...] = z
        zg_ref[...] = _dot(h, w_ref[ATT_W:IN_W, :], NT).astype(BF16)
        bd = bd_ref[...]

        def norm(v, g, bdm):
            v = v.astype(F32)
            return v * lax.rsqrt(_group_mean(v * v, bdm) + EPS) * g

        qa_ref[...] = (norm(z[:, 0:512], gqa_ref[...], bd) * QK_SCALE).astype(BF16)
        ka_ref[...] = norm(z[:, 512:1024], gka_ref[...], bd).astype(BF16)
        va_ref[...] = z[:, 1024:1536]
        qs_ref[...] = (norm(z[:, 1536:2048], gqs_ref[...], bd) * QK_SCALE).astype(BF16)
        kv_ref[:, 0:128] = norm(z[:, 2048:2176], gks_ref[...], bd2_ref[...]).astype(BF16)
        kv_ref[:, 128:256] = z[:, 2176:2304]

    s = x.shape[0]
    tile = pl.BlockSpec((tm, DM), lambda i: (i, 0))
    vec = lambda w: pl.BlockSpec((1, w), lambda i: (0, 0))
    att = pl.BlockSpec((tm, 512), lambda i: (i, 0))
    g512 = lambda g: jnp.tile(g.reshape(1, HD), (1, 8))
    q = _sds((s, 512), BF16)
    return pl.pallas_call(
        body, name=name,
        out_shape=(_sds((s, DM), BF16), _sds((s, ATT_W), BF16), _sds((s, GATE_W), BF16), q, q, q, q, _sds((s, 256), BF16)),
        grid=(s // tm,),
        in_specs=[tile, vec(DM), pl.BlockSpec((IN_W, DM), lambda i: (0, 0), pipeline_mode=pl.Buffered(1)),
                  vec(512), vec(512), vec(512), vec(128), pl.BlockSpec((512, 512), lambda i: (0, 0)),
                  pl.BlockSpec((128, 128), lambda i: (0, 0))],
        out_specs=(tile, pl.BlockSpec((tm, ATT_W), lambda i: (i, 0)), pl.BlockSpec((tm, GATE_W), lambda i: (i, 0)),
                   att, att, att, att, pl.BlockSpec((tm, 256), lambda i: (i, 0))),
        compiler_params=_params(("parallel",)))(
            x, gain, w_in_t, g512(gq_na), g512(gk_na), g512(gq_sw), jnp.tile(gk_sw.reshape(1, HD), (1, 2)),
            _block_diag(512), _block_diag(128))


def _rms_bwd_math(dh, x, gain):
    r = _rms(x)
    xh = x * r
    dgain = jnp.sum(dh * xh, axis=0, keepdims=True)
    dxn = dh * gain
    dx = r * (dxn - xh * jnp.mean(dxn * xh, axis=-1, keepdims=True))
    return dx, dgain


def mixer_input_bwd(dz, dzg, w_in_t, x, gain, dres, *, name, tm=512, dep=None):
    def body(dz_ref, dzg_ref, w_ref, x_ref, g_ref, dres_ref, *rest):
        dx_ref, dxb_ref, dg_ref = rest[-3:]

        @pl.when(pl.program_id(0) == 0)
        def _():
            dg_ref[...] = jnp.zeros_like(dg_ref)

        dh = _dot(dz_ref[...], w_ref[0:ATT_W, :]) + _dot(dzg_ref[...], w_ref[ATT_W:IN_W, :])
        dx, dg = _rms_bwd_math(dh, x_ref[...], g_ref[...])
        dx = dres_ref[...] + dx
        dx_ref[...] = dx
        dxb_ref[...] = dx.astype(BF16)
        dg_ref[...] += dg

    s = x.shape[0]
    tile = pl.BlockSpec((tm, DM), lambda i: (i, 0))
    vec = pl.BlockSpec((1, DM), lambda i: (0, 0))
    ins, specs = _with_dep(
        [dz, dzg, w_in_t, x, gain, dres],
        [pl.BlockSpec((tm, ATT_W), lambda i: (i, 0)), pl.BlockSpec((tm, GATE_W), lambda i: (i, 0)),
         pl.BlockSpec((IN_W, DM), lambda i: (0, 0), pipeline_mode=pl.Buffered(1)), tile, vec, tile], dep)
    return pl.pallas_call(
        body, name=name, out_shape=(_sds((s, DM), F32), _sds((s, DM), BF16), _sds((1, DM), F32)), grid=(s // tm,),
        in_specs=specs, out_specs=(tile, tile, vec), compiler_params=_params(("arbitrary",)))(*ins)


def _with_dep(ins, specs, dep):
    if dep is None:
        return ins, specs
    return ins + [dep], specs + [pl.BlockSpec(memory_space=pl.ANY)]


def _resident_weight():
    return pl.BlockSpec((DFF, DM), lambda i: (0, 0), pipeline_mode=pl.Buffered(1))


def ffn_fwd(x, gain, wg, wu, wd, target=None, *, name, tm=512):
    def body(x_ref, g_ref, wg_ref, wu_ref, wd_ref, *rest):
        h_ref, gg_ref, uu_ref = rest[-3:]
        x = x_ref[...]
        h = (x * _rms(x) * g_ref[...]).astype(BF16)
        h_ref[...] = h
        gg = _dot(h, wg_ref[...], NT)
        uu = _dot(h, wu_ref[...], NT)
        gg_ref[...] = gg.astype(BF16)
        uu_ref[...] = uu.astype(BF16)
        act = (gg * jax.nn.sigmoid(gg) * uu).astype(BF16)
        y = x + 0.5 * _dot(act, wd_ref[...])
        if target is None:
            rest[0][...] = y
            return
        t_ref, dy_ref, dyb_ref, l_ref = rest[:4]

        @pl.when(pl.program_id(0) == 0)
        def _():
            l_ref[...] = jnp.zeros_like(l_ref)

        err = y - t_ref[...]
        dy = err * (1.0 / DM)
        dy_ref[...] = dy
        dyb_ref[...] = dy.astype(BF16)
        l_ref[...] += 0.5 * jnp.sum(jnp.mean(err * err, axis=-1, keepdims=True), axis=0, keepdims=True)

    s = x.shape[0]
    tile = pl.BlockSpec((tm, DM), lambda i: (i, 0))
    hid = pl.BlockSpec((tm, DFF), lambda i: (i, 0))
    w = _resident_weight()
    saved_shapes = (_sds((s, DM), BF16), _sds((s, DFF), BF16), _sds((s, DFF), BF16))
    ins, specs = [x, gain, wg, wu, wd], [tile, pl.BlockSpec((1, DM), lambda i: (0, 0)), w, w, w]
    if target is None:
        head_shapes, head_specs = (_sds((s, DM), F32),), (tile,)
    else:
        ins, specs = ins + [target], specs + [tile]
        head_shapes = (_sds((s, DM), F32), _sds((s, DM), BF16), _sds((1, 128), F32))
        head_specs = (tile, tile, pl.BlockSpec((1, 128), lambda i: (0, 0)))
    return pl.pallas_call(
        body, name=name, out_shape=head_shapes + saved_shapes, grid=(s // tm,), in_specs=specs,
        out_specs=head_specs + (tile, hid, hid),
        compiler_params=_params(("parallel",) if target is None else ("arbitrary",)))(*ins)


def ffn_bwd_tokens(dy, x, gain, gg, uu, wg, wu, wd, *, name, tm=256, dep=None):
    def body(dy_ref, x_ref, g_ref, gg_ref, uu_ref, wg_ref, wu_ref, wd_ref, *rest):
        dx_ref, dxb_ref, dgain_ref, act_ref, dg_ref, du_ref = rest[-6:]

        @pl.when(pl.program_id(0) == 0)
        def _():
            dgain_ref[...] = jnp.zeros_like(dgain_ref)

        dy = dy_ref[...]
        dact = _dot((0.5 * dy).astype(BF16), wd_ref[...], NT)
        g = gg_ref[...].astype(F32)
        u = uu_ref[...].astype(F32)
        sg = jax.nn.sigmoid(g)
        silu = g * sg
        act_ref[...] = (silu * u).astype(BF16)
        dg = (dact * u * (sg * (1.0 + g * (1.0 - sg)))).astype(BF16)
        du = (dact * silu).astype(BF16)
        dg_ref[...] = dg
        du_ref[...] = du
        dx, dgain = _rms_bwd_math(_dot(dg, wg_ref[...]) + _dot(du, wu_ref[...]), x_ref[...], g_ref[...])
        dx = dy + dx
        dx_ref[...] = dx
        dxb_ref[...] = dx.astype(BF16)
        dgain_ref[...] += dgain

    s = x.shape[0]
    tile = pl.BlockSpec((tm, DM), lambda i: (i, 0))
    vec = pl.BlockSpec((1, DM), lambda i: (0, 0))
    hid = pl.BlockSpec((tm, DFF), lambda i: (i, 0))
    hshape = _sds((s, DFF), BF16)
    w = _resident_weight()
    ins, specs = _with_dep([dy, x, gain, gg, uu, wg, wu, wd], [tile, tile, vec, hid, hid, w, w, w], dep)
    return pl.pallas_call(
        body, name=name, out_shape=(_sds((s, DM), F32), _sds((s, DM), BF16), _sds((1, DM), F32), hshape, hshape, hshape),
        grid=(s // tm,), in_specs=specs, out_specs=(tile, tile, vec, hid, hid, hid),
        compiler_params=_params(("arbitrary",)))(*ins)


def ffn_bwd_weights(h, dy, act, dg, du, *, name, tf=256):
    def body(h_ref, dy_ref, act_ref, dg_ref, du_ref, gwg_ref, gwu_ref, gwd_ref):
        h = h_ref[...]
        gwg_ref[...] = _dot(dg_ref[...], h, TN).astype(BF16)
        gwu_ref[...] = _dot(du_ref[...], h, TN).astype(BF16)
        gwd_ref[...] = (0.5 * _dot(act_ref[...], dy_ref[...], TN)).astype(BF16)

    s = h.shape[0]
    full = pl.BlockSpec((s, DM), lambda f: (0, 0))
    hid = pl.BlockSpec((s, tf), lambda f: (0, f))
    wt = pl.BlockSpec((tf, DM), lambda f: (f, 0))
    wshape = _sds((DFF, DM), BF16)
    return pl.pallas_call(
        body, name=name, out_shape=(wshape, wshape, wshape), grid=(DFF // tf,), in_specs=[full, full, hid, hid, hid],
        out_specs=(wt, wt, wt), compiler_params=_params(("parallel",)))(h, dy, act, dg, du)


def _group_mean(v, bd):
    hi = v.astype(BF16)
    lo = (v - hi.astype(F32)).astype(BF16)
    return _dot(hi, bd) + _dot(lo, bd)


def _block_diag(width):
    idx = np.arange(width) // HD
    return jnp.asarray((idx[:, None] == idx[None, :]).astype(np.float32) / HD, dtype=BF16)


def qknorm_bwd(z, dqa, dka, dva, dqs, dkv, gq_na, gk_na, gq_sw, gk_sw, *, name, tm=256):
    def body(zq_ref, zk_ref, zs_ref, zkv_ref, dqa_ref, dka_ref, dva_ref, dqs_ref, dkv_ref, gqa_ref, gka_ref, gqs_ref,
             gks_ref, bd_ref, bd2_ref, dz_ref, dgqa_ref, dgka_ref, dgqs_ref, dgks_ref):
        @pl.when(pl.program_id(0) == 0)
        def _():
            dgqa_ref[...] = jnp.zeros_like(dgqa_ref)
            dgka_ref[...] = jnp.zeros_like(dgka_ref)
            dgqs_ref[...] = jnp.zeros_like(dgqs_ref)
            dgks_ref[...] = jnp.zeros_like(dgks_ref)

        bd = bd_ref[...]

        def bwd(x, dy, g, bdm, dg_ref):
            x = x.astype(F32)
            r = lax.rsqrt(_group_mean(x * x, bdm) + EPS)
            xh = x * r
            dg_ref[...] += jnp.sum(dy * xh, axis=0, keepdims=True)
            dxn = dy * g
            return r * (dxn - xh * _group_mean(dxn * xh, bdm))

        dz_ref[:, 0:512] = bwd(zq_ref[...], dqa_ref[...] * QK_SCALE, gqa_ref[...], bd, dgqa_ref).astype(BF16)
        dz_ref[:, 512:1024] = bwd(zk_ref[...], dka_ref[...], gka_ref[...], bd, dgka_ref).astype(BF16)
        dz_ref[:, 1024:1536] = dva_ref[...].astype(BF16)
        dz_ref[:, 1536:2048] = bwd(zs_ref[...], dqs_ref[...] * QK_SCALE, gqs_ref[...], bd, dgqs_ref).astype(BF16)
        dkv = dkv_ref[...]
        dz_ref[:, 2048:2176] = bwd(zkv_ref[:, 0:128], dkv[:, 0:128], gks_ref[...], bd2_ref[...], dgks_ref).astype(BF16)
        dz_ref[:, 2176:2304] = dkv[:, 128:256].astype(BF16)

    s = z.shape[0]
    col = lambda j: pl.BlockSpec((tm, 512), lambda i, j=j: (i, j))
    t512 = pl.BlockSpec((tm, 512), lambda i: (i, 0))
    t256 = pl.BlockSpec((tm, 256), lambda i: (i, 0))
    vec = lambda w: pl.BlockSpec((1, w), lambda i: (0, 0))
    g512 = lambda g: jnp.tile(g.reshape(1, HD), (1, 8))
    return pl.pallas_call(
        body, name=name,
        out_shape=(_sds((s, ATT_W), BF16), _sds((1, 512), F32), _sds((1, 512), F32), _sds((1, 512), F32), _sds((1, 128), F32)),
        grid=(s // tm,),
        in_specs=[col(0), col(1), col(3), pl.BlockSpec((tm, 256), lambda i: (i, 8)), t512, t512, t512, t512, t256,
                  vec(512), vec(512), vec(512), vec(128), pl.BlockSpec((512, 512), lambda i: (0, 0)),
                  pl.BlockSpec((128, 128), lambda i: (0, 0))],
        out_specs=(pl.BlockSpec((tm, ATT_W), lambda i: (i, 0)), vec(512), vec(512), vec(512), vec(128)),
        compiler_params=_params(("arbitrary",)))(
            z, z, z, z, dqa, dka, dva, dqs, dkv, g512(gq_na), g512(gk_na), g512(gq_sw),
            jnp.tile(gk_sw.reshape(1, HD), (1, 2)), _block_diag(512), _block_diag(128))


def _na_row_start(r):
    return jnp.clip(r - NA_WR // 2, 0, ROWS - NA_WR)


def na_bias_table(rpb, *, name):
    t = jnp.pad(rpb, ((0, 0), (0, 2), (0, HD - (2 * NA_WC - 1))))
    pairs = jnp.concatenate([t[:, :16], t[:, 1:17]], axis=-1).reshape(NA_HEADS, 16, 1, 128)

    def body(t_ref, o_ref):
        p = pl.program_id(0)
        q = lax.broadcasted_iota(jnp.int32, (GRID_W, 128), 0)
        kc = lax.broadcasted_iota(jnp.int32, (GRID_W, 128), 1) & (GRID_W - 1)
        cs = jnp.clip(q - NA_WC // 2, 0, GRID_W - NA_WC)
        ok = (kc >= cs) & (kc < cs + NA_WC)
        for h in range(NA_HEADS):
            for pr in range(NA_WR // 2):
                x = jnp.broadcast_to(t_ref[h, 2 * pr - p + NA_WR - 1], (GRID_W, 128))
                b = pltpu.roll(x, 128 - (NA_WC - 1), 1, stride=1, stride_axis=0)
                o_ref[h, :, 128 * pr:128 * pr + 128] = jnp.where(ok, b, NEG)

    return pl.pallas_call(
        body, name=name, out_shape=_sds((NA_WR, NA_HEADS, GRID_W, NA_KEYS), F32), grid=(NA_WR,),
        in_specs=[pl.BlockSpec((NA_HEADS, 16, 1, 128), lambda p: (0, 0, 0, 0))],
        out_specs=pl.BlockSpec((None, NA_HEADS, GRID_W, NA_KEYS), lambda p: (p, 0, 0, 0)),
        compiler_params=_params(("parallel",)))(pairs)


def _lane_halves():
    lane = lax.broadcasted_iota(jnp.int32, (1, 128), 1)
    return lane < HD


def na_fwd(q, k, v, bias, *, name):
    def body(q_ref, k_ref, v_ref, b_ref, o_ref, lse_ref):
        r = pl.program_id(0)
        off = pl.multiple_of(_na_row_start(r) * GRID_W, GRID_W)
        first = _lane_halves()
        sels = [first, jnp.logical_not(first)]
        lanes = [slice(128 * j, 128 * j + 128) for j in range(NA_HEADS // 2)]
        q2s = [q_ref[:, l] for l in lanes]
        k2s = [k_ref[pl.ds(off, NA_KEYS), l] for l in lanes]
        v2s = [v_ref[pl.ds(off, NA_KEYS), l] for l in lanes]
        scores = []
        for h in range(NA_HEADS):
            j, half = divmod(h, 2)
            scores.append(_dot(jnp.where(sels[half], q2s[j], jnp.zeros_like(q2s[j])), k2s[j], NT))
        probs, lses = [], []
        for h in range(NA_HEADS):
            b = b_ref[h]
            s = jnp.where(b > 0.5 * NEG, scores[h] + b, NEG)
            m = jnp.max(s, axis=-1, keepdims=True)
            e = jnp.exp(s - m)
            l = jnp.sum(e, axis=-1, keepdims=True)
            probs.append((e / l).astype(BF16))
            lses.append(m + jnp.log(l))
        for j in range(NA_HEADS // 2):
            zero = jnp.zeros_like(v2s[j])
            o2 = (_dot(probs[2 * j], jnp.where(sels[0], v2s[j], zero))
                  + _dot(probs[2 * j + 1], jnp.where(sels[1], v2s[j], zero)))
            o_ref[:, lanes[j]] = o2.astype(BF16)
        lse_ref[...] = jnp.concatenate(lses, axis=1)

    s_tok = q.shape[0]
    full = pl.BlockSpec((s_tok, 512), lambda r: (0, 0))
    return pl.pallas_call(
        body, name=name, out_shape=(_sds((s_tok, 512), BF16), _sds((s_tok, NA_HEADS), F32)), grid=(ROWS,),
        in_specs=[pl.BlockSpec((GRID_W, 512), lambda r: (r, 0)), full, full,
                  pl.BlockSpec((None, NA_HEADS, GRID_W, NA_KEYS), lambda r: (r - _na_row_start(r), 0, 0, 0))],
        out_specs=(pl.BlockSpec((GRID_W, 512), lambda r: (r, 0)), pl.BlockSpec((GRID_W, NA_HEADS), lambda r: (r, 0))),
        compiler_params=_params(("parallel",)))(q, k, v, bias)


def na_bwd(q, k, v, o, do, lse, bias, *, name):
    def body(q_ref, k_ref, v_ref, o_ref, do_ref, lse_ref, b_ref, dq_ref, dk_ref, dv_ref, db_ref):
        r = pl.program_id(0)

        @pl.when(r == 0)
        def _():
            dk_ref[...] = jnp.zeros_like(dk_ref)
            dv_ref[...] = jnp.zeros_like(dv_ref)

        @pl.when((r <= NA_WR // 2) | (r > ROWS - NA_WR // 2))
        def _():
            db_ref[...] = jnp.zeros_like(db_ref)

        off = pl.multiple_of(_na_row_start(r) * GRID_W, GRID_W)
        first = _lane_halves()
        sels = [first, jnp.logical_not(first)]
        lanes = [slice(128 * j, 128 * j + 128) for j in range(NA_HEADS // 2)]
        q2s = [q_ref[:, l] for l in lanes]
        k2s = [k_ref[pl.ds(off, NA_KEYS), l] for l in lanes]
        v2s = [v_ref[pl.ds(off, NA_KEYS), l] for l in lanes]
        do2s = [do_ref[:, l] for l in lanes]
        prods = [do2s[j].astype(F32) * o_ref[:, lanes[j]].astype(F32) for j in range(NA_HEADS // 2)]
        lse = lse_ref[...]
        qhs, dohs, scores, dps = [], [], [], []
        for h in range(NA_HEADS):
            j, half = divmod(h, 2)
            qhs.append(jnp.where(sels[half], q2s[j], jnp.zeros_like(q2s[j])))
            dohs.append(jnp.where(sels[half], do2s[j], jnp.zeros_like(do2s[j])))
            scores.append(_dot(qhs[h], k2s[j], NT))
            dps.append(_dot(dohs[h], v2s[j], NT))
        pbs, dsbs = [], []
        for h in range(NA_HEADS):
            j, half = divmod(h, 2)
            b = b_ref[h]
            s = jnp.where(b > 0.5 * NEG, scores[h] + b, NEG)
            p = jnp.exp(s - lse[:, h:h + 1])
            delta = jnp.sum(jnp.where(sels[half], prods[j], 0.0), axis=-1, keepdims=True)
            ds = p * (dps[h] - delta)
            db_ref[h] += ds
            pbs.append(p.astype(BF16))
            dsbs.append(ds.astype(BF16))
        for j in range(NA_HEADS // 2):
            a, b = 2 * j, 2 * j + 1
            zero = jnp.zeros_like(k2s[j])
            dq_ref[:, lanes[j]] = (_dot(dsbs[a], jnp.where(sels[0], k2s[j], zero))
                                   + _dot(dsbs[b], jnp.where(sels[1], k2s[j], zero)))
            dk_ref[pl.ds(off, NA_KEYS), lanes[j]] += _dot(dsbs[a], qhs[a], TN) + _dot(dsbs[b], qhs[b], TN)
            dv_ref[pl.ds(off, NA_KEYS), lanes[j]] += _dot(pbs[a], dohs[a], TN) + _dot(pbs[b], dohs[b], TN)

    s_tok = q.shape[0]
    full = pl.BlockSpec((s_tok, 512), lambda r: (0, 0))
    row = pl.BlockSpec((GRID_W, 512), lambda r: (r, 0))
    bias_spec = pl.BlockSpec((None, NA_HEADS, GRID_W, NA_KEYS), lambda r: (r - _na_row_start(r), 0, 0, 0))
    return pl.pallas_call(
        body, name=name,
        out_shape=(_sds((s_tok, 512), F32), _sds((s_tok, 512), F32), _sds((s_tok, 512), F32),
                   _sds((NA_WR, NA_HEADS, GRID_W, NA_KEYS), F32)),
        grid=(ROWS,),
        in_specs=[row, full, full, row, row, pl.BlockSpec((GRID_W, NA_HEADS), lambda r: (r, 0)), bias_spec],
        out_specs=(row, full, full, bias_spec), compiler_params=_params(("arbitrary",)))(q, k, v, o, do, lse, bias)


def t5_bucket_map():
    rel = np.arange(SW_KEYS)[None, :] - SW_BLK - np.arange(SW_BLK)[:, None]
    nb = 16
    max_exact = nb // 2
    n = np.abs(rel)
    large = max_exact + (np.log(np.maximum(n, 1) / max_exact) / np.log(128 / max_exact) * (nb - max_exact)).astype(np.int32)
    large = np.minimum(large, nb - 1)
    return ((rel > 0) * nb + np.where(n < max_exact, n, large)).astype(np.int32)


def t5_bias(table, *, name):
    rel = np.arange(-SW_BLK, SW_BLK + 1)
    nb, max_exact = 16, 8
    n = np.abs(rel)
    large = max_exact + (np.log(np.maximum(n, 1) / max_exact) / np.log(128 / max_exact) * (nb - max_exact)).astype(np.int32)
    bucket = ((rel > 0) * nb + np.where(n < max_exact, n, np.minimum(large, nb - 1))).astype(np.int32)
    u = jnp.pad(table[jnp.asarray(bucket)].T, ((0, 0), (0, SW_KEYS - bucket.shape[0]))).reshape(8, 1, SW_KEYS)

    def body(u_ref, o_ref):
        for h in range(8):
            x = jnp.broadcast_to(u_ref[h], (SW_BLK, SW_KEYS))
            o_ref[h] = pltpu.roll(x, 0, 1, stride=1, stride_axis=0)

    return pl.pallas_call(body, name=name, out_shape=_sds((8, SW_BLK, SW_KEYS), F32), compiler_params=_params())(u)


def _sw_valid(n):
    a = lax.broadcasted_iota(jnp.int32, (SW_BLK, SW_KEYS), 0)
    j = lax.broadcasted_iota(jnp.int32, (SW_BLK, SW_KEYS), 1)
    kpos = (n - 1) * SW_BLK + j
    return (jnp.abs(j - SW_BLK - a) <= SW_BLK) & (kpos >= 0) & (kpos < SEQ)


def _dup_group(x2, g, first):
    rolled = pltpu.roll(x2, HD, 1)
    return jnp.where(first, x2, rolled) if g == 0 else jnp.where(first, rolled, x2)


def sw_fwd(q, kv, t5, sink, *, name):
    def body(q_ref, kv_ref, t5_ref, sink_ref, o_ref, lse_ref):
        n = pl.program_id(0)
        off = pl.multiple_of(n * SW_BLK, SW_BLK)
        first = _lane_halves()
        sels = [first, jnp.logical_not(first)]
        valid = _sw_valid(n)
        k2 = kv_ref[pl.ds(off, SW_KEYS), 0:128]
        v2 = kv_ref[pl.ds(off, SW_KEYS), 128:256]
        kk = [_dup_group(k2, g, first) for g in range(2)]
        vv = [_dup_group(v2, g, first) for g in range(2)]
        q2s = [q_ref[:, 128 * j:128 * j + 128] for j in range(4)]
        scores = []
        for h in range(8):
            j, half = divmod(h, 2)
            scores.append(_dot(jnp.where(sels[half], q2s[j], jnp.zeros_like(q2s[j])), kk[j // 2], NT))
        probs, lses = [], []
        for h in range(8):
            s = jnp.where(valid, scores[h] + t5_ref[h], NEG)
            snk = sink_ref[h]
            m = jnp.maximum(jnp.max(s, axis=-1, keepdims=True), snk)
            e = jnp.exp(s - m)
            den = jnp.sum(e, axis=-1, keepdims=True) + jnp.exp(snk - m)
            probs.append((e / den).astype(BF16))
            lses.append(m + jnp.log(den))
        outs = []
        for j in range(4):
            vg = vv[j // 2]
            zero = jnp.zeros_like(vg)
            outs.append(_dot(probs[2 * j], jnp.where(sels[0], vg, zero)) + _dot(probs[2 * j + 1], jnp.where(sels[1], vg, zero)))
        o_ref[...] = jnp.concatenate(outs, axis=1).astype(BF16)
        lse_ref[...] = jnp.concatenate(lses, axis=1)

    s_tok = q.shape[0]
    blk = pl.BlockSpec((SW_BLK, 512), lambda n: (n, 0))
    return pl.pallas_call(
        body, name=name, out_shape=(_sds((s_tok, 512), BF16), _sds((s_tok, 8), F32)), grid=(SW_NB,),
        in_specs=[blk, pl.BlockSpec(kv.shape, lambda n: (0, 0)), pl.BlockSpec((8, SW_BLK, SW_KEYS), lambda n: (0, 0, 0)),
                  pl.BlockSpec(memory_space=pltpu.SMEM)],
        out_specs=(blk, pl.BlockSpec((SW_BLK, 8), lambda n: (n, 0))), compiler_params=_params(("parallel",)))(q, kv, t5, sink)


def sw_bwd(q, kv, o, do, lse, t5, sink, *, name):
    def body(q_ref, kv_ref, o_ref, do_ref, lse_ref, t5_ref, sink_ref, dq_ref, dkv_ref, dt5_ref, dsink_ref):
        n = pl.program_id(0)

        @pl.when(n == 0)
        def _():
            dkv_ref[...] = jnp.zeros_like(dkv_ref)
            dt5_ref[...] = jnp.zeros_like(dt5_ref)
            dsink_ref[...] = jnp.zeros_like(dsink_ref)

        off = pl.multiple_of(n * SW_BLK, SW_BLK)
        first = _lane_halves()
        sels = [first, jnp.logical_not(first)]
        valid = _sw_valid(n)
        k2 = kv_ref[pl.ds(off, SW_KEYS), 0:128]
        v2 = kv_ref[pl.ds(off, SW_KEYS), 128:256]
        kk = [_dup_group(k2, g, first) for g in range(2)]
        vv = [_dup_group(v2, g, first) for g in range(2)]
        lanes = [slice(128 * j, 128 * j + 128) for j in range(4)]
        q2s = [q_ref[:, l] for l in lanes]
        do2s = [do_ref[:, l] for l in lanes]
        prods = [do2s[j].astype(F32) * o_ref[:, lanes[j]].astype(F32) for j in range(4)]
        lse = lse_ref[...]
        qhs, dohs, scores, dps = [], [], [], []
        for h in range(8):
            j, half = divmod(h, 2)
            qhs.append(jnp.where(sels[half], q2s[j], jnp.zeros_like(q2s[j])))
            dohs.append(jnp.where(sels[half], do2s[j], jnp.zeros_like(do2s[j])))
            scores.append(_dot(qhs[h], kk[j // 2], NT))
            dps.append(_dot(dohs[h], vv[j // 2], NT))
        pbs, dsbs, dss, dsinks = [], [], [], []
        for h in range(8):
            j, half = divmod(h, 2)
            s = jnp.where(valid, scores[h] + t5_ref[h], NEG)
            lse_h = lse[:, h:h + 1]
            p = jnp.exp(s - lse_h)
            delta = jnp.sum(jnp.where(sels[half], prods[j], 0.0), axis=-1, keepdims=True)
            ds = p * (dps[h] - delta)
            dss.append(ds)
            dsinks.append(-jnp.sum(jnp.exp(sink_ref[h] - lse_h) * delta, axis=0, keepdims=True))
            pbs.append(p.astype(BF16))
            dsbs.append(ds.astype(BF16))
        dt5_ref[...] += jnp.stack(dss)
        dsink_ref[...] += jnp.concatenate(dsinks, axis=1)
        dqs = []
        for j in range(4):
            a, b = 2 * j, 2 * j + 1
            zero = jnp.zeros_like(kk[j // 2])
            dqs.append(_dot(dsbs[a], jnp.where(sels[0], kk[j // 2], zero)) + _dot(dsbs[b], jnp.where(sels[1], kk[j // 2], zero)))
        dq_ref[...] = jnp.concatenate(dqs, axis=1)
        dk_groups, dv_groups = [], []
        for g in range(2):
            dkk = sum(_dot(dsbs[h], qhs[h], TN) for h in range(4 * g, 4 * g + 4))
            dvv = sum(_dot(pbs[h], dohs[h], TN) for h in range(4 * g, 4 * g + 4))
            dk_groups.append(dkk + pltpu.roll(dkk, HD, 1))
            dv_groups.append(dvv + pltpu.roll(dvv, HD, 1))
        dkv_ref[pl.ds(off, SW_KEYS), :] += jnp.concatenate(
            [jnp.where(first, dk_groups[0], dk_groups[1]), jnp.where(first, dv_groups[0], dv_groups[1])], axis=1)

    s_tok = q.shape[0]
    blk = pl.BlockSpec((SW_BLK, 512), lambda n: (n, 0))
    kv_spec = pl.BlockSpec(kv.shape, lambda n: (0, 0))
    t5_spec = pl.BlockSpec((8, SW_BLK, SW_KEYS), lambda n: (0, 0, 0))
    vec = pl.BlockSpec((1, 8), lambda n: (0, 0))
    return pl.pallas_call(
        body, name=name,
        out_shape=(_sds((s_tok, 512), F32), _sds(kv.shape, F32), _sds((8, SW_BLK, SW_KEYS), F32), _sds((1, 8), F32)),
        grid=(SW_NB,), in_specs=[blk, kv_spec, blk, blk, pl.BlockSpec((SW_BLK, 8), lambda n: (n, 0)), t5_spec,
                                 pl.BlockSpec(memory_space=pltpu.SMEM)],
        out_specs=(blk, kv_spec, t5_spec, vec), compiler_params=_params(("arbitrary",)))(q, kv, o, do, lse, t5, sink)


def mixer_output_fwd(o_na, o_sw, zg, bias, wa, ws, wo, res, *, name, tm=512):
    def body(ona_ref, osw_ref, z0_ref, z1_ref, b0_ref, b1_ref, wa_ref, ws_ref, wo_ref, res_ref, y_ref, pa_ref, ps_ref, m_ref):
        pa = _dot(ona_ref[...], wa_ref[...]).astype(BF16)
        ps = _dot(osw_ref[...], ws_ref[...]).astype(BF16)
        pa_ref[...] = pa
        ps_ref[...] = ps
        g0 = jax.nn.sigmoid(z0_ref[...] + b0_ref[...])
        g1 = jax.nn.sigmoid(z1_ref[...] + b1_ref[...])
        merged = (g0 * pa + g1 * ps).astype(BF16)
        m_ref[...] = merged
        y_ref[...] = res_ref[...] + _dot(merged, wo_ref[...])

    s = zg.shape[0]
    half = lambda j: pl.BlockSpec((tm, DM), lambda i, j=j: (i, j))
    bvec = lambda j: pl.BlockSpec((1, DM), lambda i, j=j: (0, j))
    att = pl.BlockSpec((tm, 512), lambda i: (i, 0))
    whole = lambda a: pl.BlockSpec(a.shape, lambda i: (0, 0), pipeline_mode=pl.Buffered(1))
    act = _sds((s, DM), BF16)
    return pl.pallas_call(
        body, name=name, out_shape=(_sds((s, DM), F32), act, act, act), grid=(s // tm,),
        in_specs=[att, att, half(0), half(1), bvec(0), bvec(1), whole(wa), whole(ws), whole(wo), half(0)],
        out_specs=(half(0),) * 4, compiler_params=_params(("parallel",)))(o_na, o_sw, zg, zg, bias, bias, wa, ws, wo, res)


def mixer_output_bwd(dy, zg, bias, pa, ps, wa, ws, wo, *, name, tm=512, dep=None):
    def body(dy_ref, z0_ref, z1_ref, b0_ref, b1_ref, pa_ref, ps_ref, wa_ref, ws_ref, wo_ref, *rest):
        dpa_ref, dps_ref, dz_ref, db_ref, dona_ref, dosw_ref = rest[-6:]

        @pl.when(pl.program_id(0) == 0)
        def _():
            db_ref[...] = jnp.zeros_like(db_ref)

        dm = _dot(dy_ref[...].astype(BF16), wo_ref[...], NT)
        g0 = jax.nn.sigmoid(z0_ref[...] + b0_ref[...])
        g1 = jax.nn.sigmoid(z1_ref[...] + b1_ref[...])
        dpa = (dm * g0).astype(BF16)
        dps = (dm * g1).astype(BF16)
        dpa_ref[...] = dpa
        dps_ref[...] = dps
        dz0 = dm * pa_ref[...] * g0 * (1.0 - g0)
        dz1 = dm * ps_ref[...] * g1 * (1.0 - g1)
        dz_ref[:, 0:DM] = dz0.astype(BF16)
        dz_ref[:, DM:2 * DM] = dz1.astype(BF16)
        db_ref[:, 0:DM] += jnp.sum(dz0, axis=0, keepdims=True)
        db_ref[:, DM:2 * DM] += jnp.sum(dz1, axis=0, keepdims=True)
        dona_ref[...] = _dot(dpa, wa_ref[...], NT).astype(BF16)
        dosw_ref[...] = _dot(dps, ws_ref[...], NT).astype(BF16)

    s = zg.shape[0]
    half = lambda j: pl.BlockSpec((tm, DM), lambda i, j=j: (i, j))
    bvec = lambda j: pl.BlockSpec((1, DM), lambda i, j=j: (0, j))
    att = pl.BlockSpec((tm, 512), lambda i: (i, 0))
    whole = lambda a: pl.BlockSpec(a.shape, lambda i: (0, 0), pipeline_mode=pl.Buffered(1))
    ins, specs = _with_dep([dy, zg, zg, bias, bias, pa, ps, wa, ws, wo],
                           [half(0), half(0), half(1), bvec(0), bvec(1), half(0), half(0), whole(wa), whole(ws), whole(wo)], dep)
    return pl.pallas_call(
        body, name=name,
        out_shape=(_sds((s, DM), BF16), _sds((s, DM), BF16), _sds((s, GATE_W), BF16), _sds((1, GATE_W), F32),
                   _sds((s, 512), BF16), _sds((s, 512), BF16)),
        grid=(s // tm,), in_specs=specs,
        out_specs=(half(0), half(0), pl.BlockSpec((tm, GATE_W), lambda i: (i, 0)), pl.BlockSpec((1, GATE_W), lambda i: (0, 0)),
                   att, att),
        compiler_params=_params(("arbitrary",)))(*ins)


def adamw_small(ws, gs, ms, vs, *, name):
    cnt = len(ws)

    def body(*refs):
        ins, outs = refs[:4 * cnt], refs[4 * cnt:]
        for i in range(cnt):
            w_ref, g_ref, m_ref, v_ref = ins[4 * i:4 * i + 4]
            d_ref, nm_ref, nv_ref = outs[3 * i:3 * i + 3]
            g = g_ref[...]
            nm = ADAM_B1 * m_ref[...] + (1.0 - ADAM_B1) * g
            nv = ADAM_B2 * v_ref[...] + (1.0 - ADAM_B2) * jnp.square(g)
            m_hat = nm / (1.0 - ADAM_B1 ** ADAM_STEP)
            v_hat = nv / (1.0 - ADAM_B2 ** ADAM_STEP)
            d_ref[...] = -ADAM_LR * (m_hat / (jnp.sqrt(v_hat) + ADAM_EPS) + ADAM_WD * w_ref[...])
            nm_ref[...] = nm
            nv_ref[...] = nv

    flat = [a for i in range(cnt) for a in (ws[i], gs[i], ms[i], vs[i])]
    res = pl.pallas_call(
        body, name=name, out_shape=tuple(_sds(ws[i].shape, F32) for i in range(cnt) for _ in range(3)),
        compiler_params=_params())(*flat)
    return [tuple(res[3 * i:3 * i + 3]) for i in range(cnt)]


def adamw_layer(ws, ms, vs, mines, theirs, cidx, layer, filled=None, *, name):
    cnt = len(ws)
    _, k, n = ws[0].shape
    nt = 2
    tk = k // 2 // nt

    def body(c_ref, *refs):
        own = pl.program_id(0) == c_ref[0]
        outs = refs[-4 * cnt:]
        for i in range(cnt):
            w_ref, m_ref, v_ref, a_ref, b_ref = refs[5 * i:5 * i + 5]
            g_ref, d_ref, nm_ref, nv_ref = outs[4 * i:4 * i + 4]
            g = jnp.where(own, a_ref[...], b_ref[...])
            g_ref[...] = g
            nm = ADAM_B1 * m_ref[...] + (1.0 - ADAM_B1) * g
            nv = ADAM_B2 * v_ref[...] + (1.0 - ADAM_B2) * jnp.square(g)
            m_hat = nm / (1.0 - ADAM_B1 ** ADAM_STEP)
            v_hat = nv / (1.0 - ADAM_B2 ** ADAM_STEP)
            d_ref[...] = -ADAM_LR * (m_hat / (jnp.sqrt(v_hat) + ADAM_EPS) + ADAM_WD * w_ref[...])
            nm_ref[...] = nm
            nv_ref[...] = nv

    full = pl.BlockSpec((None, tk, n), lambda hf, t, c: (layer, hf * nt + t, 0))
    half_mine = pl.BlockSpec((tk, n), lambda hf, t, c: (jnp.where(hf == c[0], t, 0), 0))
    half_theirs = pl.BlockSpec((tk, n), lambda hf, t, c: (jnp.where(hf != c[0], t, 0), 0))
    out = _sds(ws[0].shape, F32)
    ins, specs, aliases = [cidx], [], {}
    for i in range(cnt):
        ins += [ws[i], ms[i], vs[i], mines[i], theirs[i]]
        specs += [full, full, full, half_mine, half_theirs]
    if filled is not None:
        aliases = {len(ins) + j: j for j in range(4 * cnt)}
        ins += [a for f in filled for a in f]
        specs += [pl.BlockSpec(memory_space=pl.ANY)] * (4 * cnt)
    res = pl.pallas_call(
        body, name=name, out_shape=(out,) * (4 * cnt),
        grid_spec=pltpu.PrefetchScalarGridSpec(
            num_scalar_prefetch=1, grid=(2, nt), in_specs=specs, out_specs=(full,) * (4 * cnt)),
        input_output_aliases=aliases,
        compiler_params=_params(("arbitrary", "arbitrary")))(*ins)
    return [tuple(res[4 * i:4 * i + 4]) for i in range(cnt)]


def t5_table_grad(dt5_a, dt5_b, *, name):
    def body(a_ref, b_ref, map_ref, o_ref):
        d = a_ref[...] + b_ref[...]
        bucket = map_ref[...]
        for b in range(32):
            hit = (bucket == b)[None]
            o_ref[b] = jnp.sum(jnp.sum(jnp.where(hit, d, 0.0), axis=2), axis=1, keepdims=True)

    return pl.pallas_call(
        body, name=name, out_shape=_sds((32, 8, 1), F32), compiler_params=_params())(
            dt5_a, dt5_b, jnp.asarray(t5_bucket_map()))


def rpb_grad(dbias, *, name):
    def body(d_ref, rev_ref, o_ref):
        rev = rev_ref[...]
        for h in range(NA_HEADS):
            for pr in range(NA_WR // 2):
                d = d_ref[h, :, 128 * pr:128 * pr + 128]
                hi = d.astype(BF16)
                lo = (d - hi.astype(F32)).astype(BF16)
                flipped = _dot(rev, hi) + _dot(rev, lo)
                o_ref[h, pr] = jnp.sum(pltpu.roll(flipped, 0, 1, stride=1, stride_axis=0), axis=0, keepdims=True)

    anti = jnp.asarray(np.eye(GRID_W, dtype=np.float32)[::-1], dtype=BF16)
    e = pl.pallas_call(
        body, name=name, out_shape=_sds((NA_WR, NA_HEADS, NA_WR // 2, 1, 128), F32), grid=(NA_WR,),
        in_specs=[pl.BlockSpec((None, NA_HEADS, GRID_W, NA_KEYS), lambda p: (p, 0, 0, 0)),
                  pl.BlockSpec((GRID_W, GRID_W), lambda p: (0, 0))],
        out_specs=pl.BlockSpec((None, NA_HEADS, NA_WR // 2, 1, 128), lambda p: (p, 0, 0, 0, 0)),
        compiler_params=_params(("parallel",)))(dbias, anti)
    nci, nri = 2 * NA_WC - 1, 2 * NA_WR - 1
    e = e.reshape(NA_WR, NA_HEADS, NA_WR // 2, 128).transpose(0, 2, 1, 3).reshape(NA_WR * NA_WR // 2, NA_HEADS, 128)
    parts = jnp.concatenate([e[..., 48:48 + nci], jnp.concatenate([e[..., 112:128], e[..., 0:nci - 16]], axis=-1)], axis=0)
    p, pr = np.arange(NA_WR)[:, None], np.arange(NA_WR // 2)[None, :]
    ri = np.concatenate([(2 * pr - p + NA_WR - 1).reshape(-1), (2 * pr - p + NA_WR).reshape(-1)])
    pick = jnp.asarray((ri[None, :] == np.arange(16)[:, None]).astype(np.float32))
    out = mm(pick, parts.reshape(2 * NA_WR * NA_WR // 2, NA_HEADS * nci), name=name + "_rows", exact=True)
    return out.reshape(16, NA_HEADS, nci)[:nri].transpose(1, 0, 2)


BIG = ("ffn1_w_gate", "ffn1_w_up", "ffn1_w_down", "w_in", "w_branch_na", "w_branch_sw", "w_out",
       "ffn2_w_gate", "ffn2_w_up", "ffn2_w_down")
SMALL = ("ffn1_norm", "mix_norm", "b_gate", "na_q_norm", "na_k_norm", "na_rpb", "sw_q_norm", "sw_k_norm", "sw_sink",
         "ffn2_norm")


def _cols_to_full(w4):
    return w4.transpose(1, 0, 2).reshape(w4.shape[1], NSH * w4.shape[2])


def _full_to_cols(w):
    return w.reshape(w.shape[0], NSH, w.shape[1] // NSH).transpose(1, 0, 2)


def _mixer_weights(g):
    w_in_t = g["w_in"].reshape(IN_W, DM)
    return dict(w_in_t=w_in_t, wa=_cols_to_full(g["w_branch_na"]),
                ws=_cols_to_full(g["w_branch_sw"]), wo=g["w_out"].reshape(DM, DM))


GROUPS = {"ffn1": ("ffn1_w_gate", "ffn1_w_up", "ffn1_w_down"), "mix": ("w_in", "w_branch_na", "w_branch_sw", "w_out"),
          "ffn2": ("ffn2_w_gate", "ffn2_w_up", "ffn2_w_down")}


def layer_fwd(x, p, weights, t5b, target=None):
    row = lambda v: v.reshape(1, -1)
    stacked = lambda g: {n: a.reshape(DFF, DM) for n, a in g.items()}
    g1 = stacked(weights("ffn1", x))
    y1, h1, gg1, uu1 = ffn_fwd(x, row(p["ffn1_norm"]), g1["ffn1_w_gate"], g1["ffn1_w_up"], g1["ffn1_w_down"], name="ffn_fwd")
    w = _mixer_weights(weights("mix", y1))
    hm, z, zg, qa, ka, va, qs, kv = mixer_input_fwd(y1, row(p["mix_norm"]), w["w_in_t"], p["na_q_norm"], p["na_k_norm"],
                                                    p["sw_q_norm"], p["sw_k_norm"], name="mixer_input_fwd")
    bias = p["na_bias"]
    o_na, lse_na = na_fwd(qa, ka, va, bias, name="na_fwd")
    kvp = jnp.pad(kv, ((SW_BLK, SW_BLK), (0, 0)))
    sink = p["sw_sink"]
    o_sw, lse_sw = sw_fwd(qs, kvp, t5b, sink, name="sw_fwd")
    y2, pa, ps, merged = mixer_output_fwd(o_na, o_sw, zg, row(p["b_gate"]), w["wa"], w["ws"], w["wo"], y1,
                                          name="mixer_output_fwd")
    g2 = stacked(weights("ffn2", y2))
    *y3, h2, gg2, uu2 = ffn_fwd(y2, row(p["ffn2_norm"]), g2["ffn2_w_gate"], g2["ffn2_w_up"], g2["ffn2_w_down"], target,
                                name="ffn_fwd")
    y3 = y3[0] if target is None else tuple(y3)
    saved = dict(x=x, y1=y1, h1=h1, gg1=gg1, uu1=uu1, hm=hm, z=z, zg=zg, qa=qa, ka=ka, va=va, qs=qs, kvp=kvp, bias=bias,
                 o_na=o_na, lse_na=lse_na, o_sw=o_sw, lse_sw=lse_sw, pa=pa, ps=ps, merged=merged, y2=y2, h2=h2, gg2=gg2,
                 uu2=uu2, w=w, sink=sink, g1=g1, g2=g2)
    return y3, saved


def layer_bwd(dy3, dy3_bf, sv, p, t5b, emit, dep=None):
    w, g1, g2 = sv["w"], sv["g1"], sv["g2"]
    row = lambda v: v.reshape(1, -1)
    fold = lambda v: v.reshape(-1, HD).sum(axis=0)
    small = {}
    dy2, _, small["ffn2_norm"], act, dg, du = ffn_bwd_tokens(
        dy3, sv["y2"], row(p["ffn2_norm"]), sv["gg2"], sv["uu2"], g2["ffn2_w_gate"], g2["ffn2_w_up"], g2["ffn2_w_down"],
        name="ffn_bwd_tokens", dep=dep)
    shards = lambda gs: [g.reshape(NSH, FSH, DM) for g in gs]
    token = emit("ffn2", shards(ffn_bwd_weights(sv["h2"], dy3_bf, act, dg, du, name="ffn_bwd_weights")))
    dpa, dps, dzg, small["b_gate"], do_na, do_sw = mixer_output_bwd(
        dy2, sv["zg"], row(p["b_gate"]), sv["pa"], sv["ps"], w["wa"], w["ws"], w["wo"], name="mixer_output_bwd", dep=token)
    gw_out = mm(sv["merged"], dy2, ta=True, out_dtype=BF16, name="out_proj_dw").reshape(NSH, DM // NSH, DM)
    gw_na = _full_to_cols(mm(sv["o_na"], dpa, ta=True, out_dtype=BF16, name="branch_dw"))
    gw_sw = _full_to_cols(mm(sv["o_sw"], dps, ta=True, out_dtype=BF16, name="branch_dw"))
    dqa, dka, dva, dbias = na_bwd(sv["qa"], sv["ka"], sv["va"], sv["o_na"], do_na, sv["lse_na"], sv["bias"], name="na_bwd")
    dqs, dkvp, dt5, dsink = sw_bwd(sv["qs"], sv["kvp"], sv["o_sw"], do_sw, sv["lse_sw"], t5b, sv["sink"], name="sw_bwd")
    dkv = dkvp[SW_BLK:SW_BLK + SEQ]
    dz, dgqa, dgka, dgqs, dgks = qknorm_bwd(sv["z"], dqa, dka, dva, dqs, dkv, p["na_q_norm"], p["na_k_norm"],
                                            p["sw_q_norm"], p["sw_k_norm"], name="qknorm_bwd")
    small["na_q_norm"], small["na_k_norm"], small["sw_q_norm"], small["sw_k_norm"] = fold(dgqa), fold(dgka), fold(dgqs), fold(dgks)
    small["na_rpb"] = rpb_grad(dbias, name="rpb_grad")
    small["sw_sink"] = dsink
    gw_att_t = mm(dz, sv["hm"], ta=True, out_dtype=BF16, tm=768, name="proj_att_dw")
    gw_gz_t = mm(dzg, sv["hm"], ta=True, out_dtype=BF16, tm=1024, name="proj_gate_dw")
    gw_in = jnp.concatenate([gw_att_t, gw_gz_t], axis=0).reshape(NSH, IN_W // NSH, DM)
    token = emit("mix", (gw_in, gw_na, gw_sw, gw_out))
    dy1, dy1_bf, small["mix_norm"] = mixer_input_bwd(dz, dzg, w["w_in_t"], sv["y1"], row(p["mix_norm"]), dy2,
                                                     name="mixer_input_bwd", dep=token)
    dx, dx_bf, small["ffn1_norm"], act, dg, du = ffn_bwd_tokens(
        dy1, sv["x"], row(p["ffn1_norm"]), sv["gg1"], sv["uu1"], g1["ffn1_w_gate"], g1["ffn1_w_up"], g1["ffn1_w_down"],
        name="ffn_bwd_tokens")
    emit("ffn1", shards(ffn_bwd_weights(sv["h1"], dy1_bf, act, dg, du, name="ffn_bwd_weights")))
    return dx, dx_bf, small, dt5


ANY = pl.BlockSpec(memory_space=pl.ANY)


def _place():
    x, y, c = lax.axis_index("x"), lax.axis_index("y"), lax.axis_index("c")
    chips = [(1 - x, y), (x, 1 - y), (1 - x, 1 - y)]
    return x, y, c, chips


def _remote(src, dst, send_sem, recv_sem, to):
    return pltpu.make_async_remote_copy(src_ref=src, dst_ref=dst, send_sem=send_sem, recv_sem=recv_sem, device_id=to,
                                        device_id_type=MESH)


HBM = pl.BlockSpec(memory_space=pltpu.HBM)
SEM = pl.BlockSpec(memory_space=pltpu.SEMAPHORE)
ORDERED_EFFECT = pltpu.SideEffectType.DATAFLOW_SIDE_EFFECTING


def _in_hbm(v):
    return pltpu.with_memory_space_constraint(v, pltpu.HBM)


def _row_half(ref_shape_rows, c):
    half = ref_shape_rows // 2
    return pl.ds(c * half, half)


def _ici_gather_copies(w, land, send_sems, recv_sems):
    x, y, c, chips = _place()
    me = 2 * x + y
    copies = []
    for a in range(len(w)):
        rows = _row_half(w[a].shape[0], c)
        for k, chip in enumerate(chips):
            copies.append(_remote(w[a].at[rows], land[a].at[me, rows], send_sems.at[4 * a + k], recv_sems.at[4 * a + k],
                                  (*chip, c)))
        copies.append(_remote(w[a], land[a].at[me], send_sems.at[4 * a + 3], recv_sems.at[4 * a + 3], (x, y, 1 - c)))
    return copies


def _d2d_gather_copies(w, land, send_sems, recv_sems):
    x, y, c, chips = _place()
    copies = []
    for a in range(len(w)):
        rows = _row_half(w[a].shape[0], c)
        for k, (cx, cy) in enumerate(chips):
            blk = land[a].at[2 * cx + cy, rows]
            copies.append(_remote(blk, blk, send_sems.at[3 * a + k], recv_sems.at[3 * a + k], (x, y, 1 - c)))
    return copies


def _d2d_gather_waits(w, land, send_sems, recv_sems):
    x, y, c, chips = _place()
    waits = []
    for a in range(len(w)):
        rows = _row_half(w[a].shape[0], 1 - c)
        for k, (cx, cy) in enumerate(chips):
            blk = land[a].at[2 * cx + cy, rows]
            waits.append(_remote(blk, blk, send_sems.at[3 * a + k], recv_sems.at[3 * a + k], (x, y, 1 - c)))
    return waits


def gather_start(groups, dep=None, *, name):
    sizes = [len(g) for g in groups]
    shards = [s for g in groups for s in g]
    n, ng = len(shards), len(groups)
    extra = [] if dep is None else [dep]

    def body(*refs):
        first_out = 2 * n + len(extra)
        w, land, sems = refs[:n], refs[n:2 * n], refs[first_out:first_out + 2 * ng]
        off = 0
        for gi, size in enumerate(sizes):
            for cp in _ici_gather_copies(w[off:off + size], land[off:off + size], sems[2 * gi], sems[2 * gi + 1]):
                cp.start()
            off += size

    lands = [lax.empty((NSH,) + s.shape, s.dtype) for s in shards]
    sem_shapes = tuple(pltpu.SemaphoreType.DMA((4 * size,)) for size in sizes for _ in range(2))
    res = pl.pallas_call(
        body, name=name,
        out_shape=sem_shapes + tuple(pltpu.HBM(s.shape, s.dtype) for s in shards) + tuple(pltpu.HBM(l.shape, l.dtype) for l in lands),
        in_specs=[HBM] * (2 * n) + [ANY] * len(extra), out_specs=(SEM,) * (2 * ng) + (HBM,) * (2 * n),
        input_output_aliases={i: 2 * ng + i for i in range(2 * n)},
        compiler_params=pltpu.CompilerParams(has_side_effects=ORDERED_EFFECT))(
            *[_in_hbm(s) for s in shards], *[_in_hbm(l) for l in lands], *extra)
    out, off = [], 0
    for gi, size in enumerate(sizes):
        out.append((res[2 * gi], res[2 * gi + 1], list(res[2 * ng + off:2 * ng + off + size]),
                    list(res[2 * ng + n + off:2 * ng + n + off + size])))
        off += size
    return out


def gather_wait(send_sems, recv_sems, shards, lands, after, *, name):
    n = len(shards)

    def body(*refs):
        w, land = refs[:n], refs[n:2 * n]
        send, recv = refs[2 * n:2 * n + 2]
        for cp in _ici_gather_copies(w, land, send, recv):
            cp.wait_send()
            cp.wait_recv()

    res = pl.pallas_call(
        body, name=name,
        out_shape=tuple(pltpu.HBM(s.shape, s.dtype) for s in shards) + tuple(pltpu.HBM(l.shape, l.dtype) for l in lands),
        in_specs=[HBM] * (2 * n) + [SEM, SEM] + [ANY] * len(after), out_specs=(HBM,) * (2 * n),
        input_output_aliases={i: i for i in range(2 * n)},
        compiler_params=pltpu.CompilerParams(has_side_effects=ORDERED_EFFECT))(*shards, *lands, send_sems, recv_sems, *after)
    return list(res[:n]), list(res[n:])


def gather_finish(shards, lands, *, name):
    n = len(shards)

    def body(*refs):
        w, land = refs[:n], refs[n:2 * n]
        send_sems, recv_sems = refs[3 * n:]
        d2d = _d2d_gather_copies(w, land, send_sems, recv_sems)
        for cp in d2d:
            cp.start()
        for cp in _d2d_gather_waits(w, land, send_sems, recv_sems):
            cp.wait_recv()
        for cp in d2d:
            cp.wait_send()

    return list(pl.pallas_call(
        body, name=name, out_shape=tuple(pltpu.HBM(l.shape, l.dtype) for l in lands),
        in_specs=[ANY] * (2 * n), out_specs=tuple([ANY] * n), input_output_aliases={n + i: i for i in range(n)},
        scratch_shapes=[pltpu.SemaphoreType.DMA((3 * n,)), pltpu.SemaphoreType.DMA((3 * n,))])(*shards, *lands))


def _pair_exchange_copies(g, buf, send_sems, recv_sems):
    x, y, c, _ = _place()
    copies = []
    for a in range(len(g)):
        half = g[a].shape[1] // 2
        copies.append(_remote(g[a].at[:, pl.ds((1 - c) * half, half)], buf[a], send_sems.at[a], recv_sems.at[a], (x, y, 1 - c)))
    return copies


def pair_exchange_start(grads, dep=None, *, name):
    n = len(grads)
    extra = [] if dep is None else [dep]

    def body(*refs):
        sems = refs[2 * n + len(extra):]
        for cp in _pair_exchange_copies(refs[:n], refs[n:2 * n], sems[0], sems[1]):
            cp.start()
        refs[-1][...] = jnp.zeros_like(refs[-1])

    lands = [lax.empty((NSH, g.shape[1] // 2, g.shape[2]), g.dtype) for g in grads]
    res = pl.pallas_call(
        body, name=name,
        out_shape=(pltpu.SemaphoreType.DMA((n,)), pltpu.SemaphoreType.DMA((n,)))
        + tuple(pltpu.HBM(g.shape, g.dtype) for g in grads) + tuple(pltpu.HBM(l.shape, l.dtype) for l in lands)
        + (_sds((8, 128), F32),),
        in_specs=[HBM] * (2 * n) + [ANY] * len(extra),
        out_specs=(SEM, SEM) + (HBM,) * (2 * n) + (pl.BlockSpec(memory_space=pltpu.VMEM),),
        input_output_aliases={i: 2 + i for i in range(2 * n)},
        compiler_params=pltpu.CompilerParams(has_side_effects=ORDERED_EFFECT))(
            *[_in_hbm(g) for g in grads], *[_in_hbm(l) for l in lands], *extra)
    return res[0], res[1], list(res[2:2 + n]), list(res[2 + n:2 + 2 * n]), res[-1]


def pair_exchange_wait(send_sems, recv_sems, grads, lands, after, *, name):
    n = len(grads)

    def body(*refs):
        for cp in _pair_exchange_copies(refs[:n], refs[n:2 * n], refs[2 * n], refs[2 * n + 1]):
            cp.wait_send()
            cp.wait_recv()

    res = pl.pallas_call(
        body, name=name,
        out_shape=tuple(pltpu.HBM(g.shape, g.dtype) for g in grads) + tuple(pltpu.HBM(l.shape, l.dtype) for l in lands),
        in_specs=[HBM] * (2 * n) + [SEM, SEM] + [ANY] * len(after), out_specs=(HBM,) * (2 * n),
        input_output_aliases={i: i for i in range(2 * n)},
        compiler_params=pltpu.CompilerParams(has_side_effects=ORDERED_EFFECT))(*grads, *lands, send_sems, recv_sems, *after)
    return list(res[:n]), list(res[n:])


def _chip_exchange_copies(s, buf, send_sems, recv_sems):
    x, y, c, chips = _place()
    return [_remote(s[a].at[2 * cx + cy], buf[a].at[k], send_sems.at[3 * a + k], recv_sems.at[3 * a + k], (cx, cy, c))
            for a in range(len(s)) for k, (cx, cy) in enumerate(chips)]


def exchange_start(sums, grads, *, name):
    n1, n2 = len(sums), len(grads)

    def body(*refs):
        first_out = 2 * (n1 + n2)
        chip = _chip_exchange_copies(refs[:n1], refs[n1:2 * n1], refs[first_out], refs[first_out + 1])
        pair = _pair_exchange_copies(refs[2 * n1:2 * n1 + n2], refs[2 * n1 + n2:first_out], refs[first_out + 2],
                                     refs[first_out + 3])
        for cp in chip + pair:
            cp.start()
        refs[-1][...] = jnp.zeros_like(refs[-1])

    chip_lands = [lax.empty((3,) + s.shape[1:], s.dtype) for s in sums]
    pair_lands = [lax.empty((NSH, g.shape[1] // 2, g.shape[2]), g.dtype) for g in grads]
    arrays = list(sums) + chip_lands + list(grads) + pair_lands
    res = pl.pallas_call(
        body, name=name,
        out_shape=(pltpu.SemaphoreType.DMA((3 * n1,)), pltpu.SemaphoreType.DMA((3 * n1,)), pltpu.SemaphoreType.DMA((n2,)),
                   pltpu.SemaphoreType.DMA((n2,)))
        + tuple(pltpu.HBM(a.shape, a.dtype) for a in arrays) + (_sds((8, 128), F32),),
        in_specs=[HBM] * len(arrays), out_specs=(SEM,) * 4 + (HBM,) * len(arrays) + (pl.BlockSpec(memory_space=pltpu.VMEM),),
        input_output_aliases={i: 4 + i for i in range(len(arrays))},
        compiler_params=pltpu.CompilerParams(has_side_effects=ORDERED_EFFECT))(*[_in_hbm(a) for a in arrays])
    thru = list(res[4:4 + len(arrays)])
    chip = (res[0], res[1], thru[:n1], thru[n1:2 * n1])
    pair = (res[2], res[3], thru[2 * n1:2 * n1 + n2], thru[2 * n1 + n2:])
    return chip, pair, res[-1]


def chip_exchange_start(sums, *, name):
    n = len(sums)

    def body(*refs):
        for cp in _chip_exchange_copies(refs[:n], refs[n:2 * n], refs[2 * n], refs[2 * n + 1]):
            cp.start()
        refs[-1][...] = jnp.zeros_like(refs[-1])

    lands = [lax.empty((3,) + s.shape[1:], s.dtype) for s in sums]
    res = pl.pallas_call(
        body, name=name,
        out_shape=(pltpu.SemaphoreType.DMA((3 * n,)), pltpu.SemaphoreType.DMA((3 * n,)))
        + tuple(pltpu.HBM(s.shape, s.dtype) for s in sums) + tuple(pltpu.HBM(l.shape, l.dtype) for l in lands)
        + (_sds((8, 128), F32),),
        in_specs=[HBM] * (2 * n), out_specs=(SEM, SEM) + (HBM,) * (2 * n) + (pl.BlockSpec(memory_space=pltpu.VMEM),),
        input_output_aliases={i: 2 + i for i in range(2 * n)},
        compiler_params=pltpu.CompilerParams(has_side_effects=ORDERED_EFFECT))(
            *[_in_hbm(s) for s in sums], *[_in_hbm(l) for l in lands])
    return res[0], res[1], list(res[2:2 + n]), list(res[2 + n:2 + 2 * n]), res[-1]


def chip_exchange_wait(send_sems, recv_sems, sums, lands, after, *, name):
    n = len(sums)

    def body(*refs):
        for cp in _chip_exchange_copies(refs[:n], refs[n:2 * n], refs[2 * n], refs[2 * n + 1]):
            cp.wait_send()
            cp.wait_recv()

    res = pl.pallas_call(
        body, name=name,
        out_shape=tuple(pltpu.HBM(s.shape, s.dtype) for s in sums) + tuple(pltpu.HBM(l.shape, l.dtype) for l in lands),
        in_specs=[HBM] * (2 * n) + [SEM, SEM] + [ANY] * len(after), out_specs=(HBM,) * (2 * n),
        input_output_aliases={i: i for i in range(2 * n)},
        compiler_params=pltpu.CompilerParams(has_side_effects=ORDERED_EFFECT))(*sums, *lands, send_sems, recv_sems, *after)
    return list(res[:n]), list(res[n:])


def _pair_send_copies(h, got, send_sems, recv_sems):
    x, y, c, _ = _place()
    return [_remote(h[i], got[i], send_sems.at[i], recv_sems.at[i], (x, y, 1 - c)) for i in range(len(h))]


def pair_send_start(halves, *, name):
    n = len(halves)

    def body(*refs):
        for cp in _pair_send_copies(refs[:n], refs[n:2 * n], refs[2 * n], refs[2 * n + 1]):
            cp.start()
        refs[-1][...] = jnp.zeros_like(refs[-1])

    lands = [lax.empty(h.shape, h.dtype) for h in halves]
    res = pl.pallas_call(
        body, name=name,
        out_shape=(pltpu.SemaphoreType.DMA((n,)), pltpu.SemaphoreType.DMA((n,)))
        + tuple(pltpu.HBM(h.shape, h.dtype) for h in halves) * 2 + (_sds((8, 128), F32),),
        in_specs=[HBM] * (2 * n), out_specs=(SEM, SEM) + (HBM,) * (2 * n) + (pl.BlockSpec(memory_space=pltpu.VMEM),),
        input_output_aliases={i: 2 + i for i in range(2 * n)},
        compiler_params=pltpu.CompilerParams(has_side_effects=ORDERED_EFFECT))(
            *[_in_hbm(h) for h in halves], *[_in_hbm(l) for l in lands])
    return res[0], res[1], list(res[2:2 + n]), list(res[2 + n:2 + 2 * n]), res[-1]


def pair_send_wait(send_sems, recv_sems, halves, lands, after, *, name):
    n = len(halves)

    def body(*refs):
        for cp in _pair_send_copies(refs[:n], refs[n:2 * n], refs[2 * n], refs[2 * n + 1]):
            cp.wait_send()
            cp.wait_recv()

    res = pl.pallas_call(
        body, name=name, out_shape=tuple(pltpu.HBM(h.shape, h.dtype) for h in halves) * 2,
        in_specs=[HBM] * (2 * n) + [SEM, SEM] + [ANY] * len(after), out_specs=(HBM,) * (2 * n),
        input_output_aliases={i: i for i in range(2 * n)},
        compiler_params=pltpu.CompilerParams(has_side_effects=ORDERED_EFFECT))(*halves, *lands, send_sems, recv_sems, *after)
    return list(res[:n]), list(res[n:])


def allreduce_small(v, *, name):
    rows = v.shape[0]

    def body(v_ref, o_ref, gath, send_sems, recv_sems):
        x, y, c, _ = _place()
        me = 4 * x + 2 * y + c
        gath[me] = v_ref[...]
        copies = []
        for k in range(1, 8):
            fx, fy, fc = (k >> 2) & 1, (k >> 1) & 1, k & 1
            peer = (jnp.where(fx, 1 - x, x), jnp.where(fy, 1 - y, y), jnp.where(fc, 1 - c, c))
            cp = _remote(v_ref, gath.at[me], send_sems.at[k - 1], recv_sems.at[k - 1], peer)
            cp.start()
            copies.append(cp)
        for cp in copies:
            cp.wait()
        acc = gath[0]
        for d in range(1, 8):
            acc = acc + gath[d]
        o_ref[...] = acc

    return pl.pallas_call(
        body, name=name, out_shape=_sds(v.shape, F32),
        in_specs=[pl.BlockSpec(memory_space=pltpu.VMEM)], out_specs=pl.BlockSpec(memory_space=pltpu.VMEM),
        scratch_shapes=[pltpu.VMEM((8, rows, 128), F32), pltpu.SemaphoreType.DMA((7,)), pltpu.SemaphoreType.DMA((7,))])(v)


def _same_shape_runs(arrays):
    runs = {}
    for i, a in enumerate(arrays):
        runs.setdefault(a.shape, []).append(i)
    return list(runs.values())


def _per_shape(fn, *lists):
    out = [None] * len(lists[0])
    for idx in _same_shape_runs(lists[0]):
        for i, r in zip(idx, fn(*[[l[i] for i in idx] for l in lists])):
            out[i] = r
    return out


def add_halves(gs, bufs, cidx, *, name):
    cnt = len(gs)
    _, k, n = gs[0].shape

    def body(c_ref, *refs):
        g, b, o = refs[:cnt], refs[cnt:2 * cnt], refs[2 * cnt:]
        for i in range(cnt):
            o[i][...] = (g[i][...].astype(F32) + b[i][...].astype(F32)).astype(BF16)

    blk = pl.BlockSpec((None, k // 2, n), lambda s, c: (s, 0, 0))
    mine = pl.BlockSpec((None, k // 2, n), lambda s, c: (s, c[0], 0))
    return list(pl.pallas_call(
        body, name=name, out_shape=tuple(_sds(b.shape, BF16) for b in bufs),
        grid_spec=pltpu.PrefetchScalarGridSpec(
            num_scalar_prefetch=1, grid=(NSH,), in_specs=[mine] * cnt + [blk] * cnt, out_specs=tuple([blk] * cnt)),
        compiler_params=_params(("parallel",)))(cidx, *gs, *bufs))


def add_chips(sums, bufs, sidx, *, name):
    cnt = len(sums)
    _, kh, n = sums[0].shape

    def body(s_ref, *refs):
        mine, b, o = refs[:cnt], refs[cnt:2 * cnt], refs[2 * cnt:]
        for i in range(cnt):
            o[i][...] = ((mine[i][...].astype(F32) + b[i][0].astype(F32)) + (b[i][1].astype(F32) + b[i][2].astype(F32)))

    own = pl.BlockSpec((None, kh, n), lambda i, s: (s[0], 0, 0))
    got = pl.BlockSpec((3, kh, n), lambda i, s: (0, 0, 0))
    out = pl.BlockSpec((kh, n), lambda i, s: (0, 0))
    return list(pl.pallas_call(
        body, name=name, out_shape=tuple(_sds((kh, n), F32) for _ in sums),
        grid_spec=pltpu.PrefetchScalarGridSpec(
            num_scalar_prefetch=1, grid=(1,), in_specs=[own] * cnt + [got] * cnt, out_specs=tuple([out] * cnt)),
        compiler_params=_params(("arbitrary",)))(sidx, *sums, *bufs))


PARAMS = ("ffn1_norm", "ffn1_w_gate", "ffn1_w_up", "ffn1_w_down", "mix_norm", "w_in", "b_gate", "na_q_norm", "na_k_norm",
          "na_rpb", "sw_q_norm", "sw_k_norm", "sw_sink", "t5_rel_table", "w_branch_na", "w_branch_sw", "w_out", "ffn2_norm",
          "ffn2_w_gate", "ffn2_w_up", "ffn2_w_down")
SMALL_ALL = tuple(n for n in PARAMS if n not in BIG)
TRANSPOSED = ("ffn1_w_gate", "ffn1_w_up", "w_in", "ffn2_w_gate", "ffn2_w_up")
SMALL_ROWS = 152


def _pack_small(vals):
    flat = jnp.concatenate([vals[n].reshape(-1).astype(F32) for n in SMALL_ALL] + [vals["loss"].reshape(-1)])
    return jnp.pad(flat, (0, SMALL_ROWS * 128 - flat.shape[0])).reshape(SMALL_ROWS, 128)


def _unpack_small(packed, like):
    flat, out, off = packed.reshape(-1), {}, 0
    for n in SMALL_ALL:
        size = math.prod(like[n].shape)
        out[n] = flat[off:off + size].reshape(like[n].shape)
        off += size
    out["loss"] = flat[off]
    return out


def kernel(x, ffn1_norm, ffn1_w_gate, ffn1_w_up, ffn1_w_down, mix_norm, w_in, b_gate, na_q_norm, na_k_norm, na_rpb, sw_q_norm, sw_k_norm, sw_sink, t5_rel_table, w_branch_na, w_branch_sw, w_out, ffn2_norm, ffn2_w_gate, ffn2_w_up, ffn2_w_down, loss_target, m_ffn1_norm, m_ffn1_w_gate, m_ffn1_w_up, m_ffn1_w_down, m_mix_norm, m_w_in, m_b_gate, m_na_q_norm, m_na_k_norm, m_na_rpb, m_sw_q_norm, m_sw_k_norm, m_sw_sink, m_t5_rel_table, m_w_branch_na, m_w_branch_sw, m_w_out, m_ffn2_norm, m_ffn2_w_gate, m_ffn2_w_up, m_ffn2_w_down, v_ffn1_norm, v_ffn1_w_gate, v_ffn1_w_up, v_ffn1_w_down, v_mix_norm, v_w_in, v_b_gate, v_na_q_norm, v_na_k_norm, v_na_rpb, v_sw_q_norm, v_sw_k_norm, v_sw_sink, v_t5_rel_table, v_w_branch_na, v_w_branch_sw, v_w_out, v_ffn2_norm, v_ffn2_w_gate, v_ffn2_w_up, v_ffn2_w_down):
    args = locals()
    tr = lambda n, a: jnp.transpose(a, (0, 2, 1)) if n in TRANSPOSED else a
    w = {n: tr(n, args[n]) for n in PARAMS}
    m = {n: tr(n, args["m_" + n]) for n in PARAMS}
    v = {n: tr(n, args["v_" + n]) for n in PARAMS}
    cidx = lax.axis_index("c").astype(jnp.int32).reshape(1)
    sidx = (2 * lax.axis_index("x") + lax.axis_index("y")).astype(jnp.int32).reshape(1)

    small = [{n: w[n][l] for n in SMALL} for l in range(DEPTH)]
    order = ("ffn1", "mix", "ffn2")

    keys = [(l, g) for l in range(DEPTH) for g in order]
    local = lambda l, g: [w[n][l].astype(BF16) for n in GROUPS[g]]
    first = gather_start([local(*keys[0])], name="gather_start")
    rest = gather_start([local(*key) for key in keys[1:]], first[0][2][0], name="gather_start")
    in_flight = dict(zip(keys, first + rest))
    t5b = t5_bias(w["t5_rel_table"], name="t5_bias")
    for l in range(DEPTH):
        small[l]["na_bias"] = na_bias_table(small[l]["na_rpb"], name="na_bias_table")
    early = [t5b] + [small[l]["na_bias"] for l in range(DEPTH)] + [rest[0][2][0]]

    def weights_of(l):
        def get(group, after):
            send_sems, recv_sems, thru, lands = in_flight[(l, group)]
            after = [after] + (early if (l, group) == keys[0] else [])
            thru, lands = gather_wait(send_sems, recv_sems, thru, lands, after, name="gather_wait")
            return dict(zip(GROUPS[group], gather_finish(thru, lands, name="gather_finish")))
        return get

    h0, saved0 = layer_fwd(x[0], small[0], weights_of(0), t5b)
    (dy, dy_bf, loss_row), saved1 = layer_fwd(h0, small[1], weights_of(1), t5b, target=loss_target[0])

    crossing, tokens, pending = {}, [], []

    def ship(after, then=None):
        key, send_sems, recv_sems, grads, lands = pending.pop()
        grads, from_sibling = pair_exchange_wait(send_sems, recv_sems, grads, lands, after, name="pair_exchange_wait")
        sums = _per_shape(lambda gs, bs: add_halves(gs, bs, cidx, name="add_halves"), grads, from_sibling)
        if then is None:
            send_sems, recv_sems, sums, lands, token = chip_exchange_start(sums, name="chip_exchange_start")
            crossing[key] = (send_sems, recv_sems, sums, lands)
            return token
        crossing[key], pair, token = exchange_start(sums, then[1], name="exchange_start")
        pending.append((then[0],) + pair)
        return token

    def reduce_of(l):
        def emit(group, grads):
            grads = list(grads)
            if pending:
                token = ship([grads[0]], then=((l, group), grads))
            else:
                send_sems, recv_sems, grads, lands, token = pair_exchange_start(grads, name="pair_exchange_start")
                pending.append(((l, group), send_sems, recv_sems, grads, lands))
            tokens.append(token)
            return token
        return emit

    def finish(layer, after, filled=None):
        sent = {}
        for group in order:
            send_sems, recv_sems, sums, lands = crossing[(layer, group)]
            sums, got = chip_exchange_wait(send_sems, recv_sems, sums, lands, after, name="chip_exchange_wait")
            halves = _per_shape(lambda ss, bs: add_chips(ss, bs, sidx, name="add_chips"), sums, got)
            sent[group] = pair_send_start(halves, name="pair_send_start")
            after = [sent[group][4]]
        out = {}
        for group in order:
            send_sems, recv_sems, halves, lands, _ = sent[group]
            halves, theirs = pair_send_wait(send_sems, recv_sems, halves, lands, after, name="pair_send_wait")
            names = GROUPS[group]
            res = _per_shape(
                lambda ws, ms, vs, a, b, *f: adamw_layer(ws, ms, vs, a, b, cidx, layer, list(f[0]) if f else None, name="adamw_layer"),
                *([[w[n] for n in names], [m[n] for n in names], [v[n] for n in names], halves, theirs]
                  + ([[filled[n] for n in names]] if filled is not None else [])))
            out.update(zip(names, res))
            after = [res[-1][0]]
        return out

    dy, dy_bf, small1, dt5_1 = layer_bwd(dy, dy_bf, saved1, small[1], t5b, reduce_of(1))
    grad_x, _, small0, dt5_0 = layer_bwd(dy, dy_bf, saved0, small[0], t5b, reduce_of(0), dep=tokens[-1])
    done1 = finish(1, [ship([grad_x])])

    smalls = [small0, small1]
    dt5 = t5_table_grad(dt5_0, dt5_1, name="t5_table_grad").reshape(32, 8)
    local_small = {n: jnp.stack([smalls[l][n].reshape(w[n].shape[1:]) for l in range(DEPTH)]) for n in SMALL}
    local_small["t5_rel_table"] = dt5
    local_small["loss"] = loss_row[0, 0:1]
    total = allreduce_small(_pack_small(local_small), name="allreduce_small")
    small_grads = _unpack_small(total, w)
    small_done = adamw_small([w[n] for n in SMALL_ALL], [small_grads[n] for n in SMALL_ALL], [m[n] for n in SMALL_ALL],
                             [v[n] for n in SMALL_ALL], name="adamw_small")

    grad, delta, new_m, new_v = {}, {}, {}, {}
    for n, done in finish(0, [small_done[0][0], done1[BIG[-1]][0]], filled=done1).items():
        grad[n], delta[n], new_m[n], new_v[n] = done
    for n, done in zip(SMALL_ALL, small_done):
        grad[n] = small_grads[n]
        delta[n], new_m[n], new_v[n] = done

    return (small_grads["loss"], grad_x[None], *[tr(n, grad[n]) for n in PARAMS], *[tr(n, delta[n]) for n in PARAMS],
            *[tr(n, new_m[n]) for n in PARAMS], *[tr(n, new_v[n]) for n in PARAMS])
```

```python
import math

import jax
import jax.numpy as jnp
import numpy as np
from jax import lax
from jax.experimental import pallas as pl
from jax.experimental.pallas import tpu as pltpu

F32 = jnp.float32
BF16 = jnp.bfloat16

SEQ = 2048
DM = 1024
DFF = 2816
DEPTH = 2
NSH = 4
FSH = DFF // NSH
GRID_W = 64
ROWS = SEQ // GRID_W
NA_HEADS = 8
HD = 64
NA_WR = 8
NA_WC = 16
NA_KEYS = NA_WR * GRID_W
SW_BLK = 128
SW_NB = SEQ // SW_BLK
SW_KEYS = 3 * SW_BLK
ATT_W = 2304
GATE_W = 2048
IN_W = ATT_W + GATE_W
EPS = 1e-6
NEG = -1e30
QK_SCALE = 1.0 / math.sqrt(HD)

ADAM_LR = 0.001
ADAM_B1 = 0.9
ADAM_B2 = 0.999
ADAM_EPS = 1e-08
ADAM_WD = 0.01
ADAM_STEP = 10

VMEM_LIMIT = 56 << 20
MESH = pl.DeviceIdType.MESH

NT = (((1,), (1,)), ((), ()))
TN = (((0,), (0,)), ((), ()))
NN = (((1,), (0,)), ((), ()))


def _dot(a, b, dims=NN):
    return lax.dot_general(a, b, dims, preferred_element_type=F32)


def _params(sem=None):
    return pltpu.CompilerParams(dimension_semantics=sem, vmem_limit_bytes=VMEM_LIMIT)


def _sds(shape, dtype):
    return jax.ShapeDtypeStruct(shape, dtype)


def mm(a, b, *, name, ta=False, tb=False, out_dtype=F32, add=None, scale=None, tm=512, tn=None, tk=None, exact=False,
       dep=None, b_rows=None):
    m, kd = (a.shape[1], a.shape[0]) if ta else a.shape
    if b_rows is None:
        n = b.shape[0] if tb else b.shape[1]
    else:
        n = b_rows[1] if tb else b.shape[1]
        assert tb or (b_rows[1] == kd and (tk or kd) == kd)
    tm, tn, tk = min(tm, m), min(tn or n, n), min(tk or kd, kd)
    nk = kd // tk
    dims = (((0 if ta else 1,), (1 if tb else 0,)), ((), ()))

    def body(*refs):
        a_ref, b_ref = refs[:2]
        add_ref = refs[2] if add is not None else None
        o_ref = refs[-1] if nk == 1 else refs[-2]
        if b_rows is None:
            bv = b_ref[...]
        elif tb:
            bv = b_ref[pl.ds(pl.multiple_of(b_rows[0] + pl.program_id(1) * tn, 16), tn), :]
        else:
            bv = b_ref[b_rows[0]:b_rows[0] + b_rows[1], :]
        if exact:
            part = lax.dot_general(a_ref[...], bv, dims, precision=lax.Precision.HIGHEST, preferred_element_type=F32)
        else:
            part = lax.dot_general(a_ref[...].astype(BF16), bv.astype(BF16), dims, preferred_element_type=F32)

        def finish(r):
            if scale is not None:
                r = r * scale
            if add is not None:
                r = r + add_ref[...]
            o_ref[...] = r.astype(out_dtype)

        if nk == 1:
            finish(part)
        else:
            acc, k = refs[-1], pl.program_id(2)

            @pl.when(k == 0)
            def _():
                acc[...] = part

            @pl.when(k != 0)
            def _():
                acc[...] += part

            pl.when(k == nk - 1)(lambda: finish(acc[...]))

    a_spec = pl.BlockSpec((tk, tm), lambda i, j, k: (k, i)) if ta else pl.BlockSpec((tm, tk), lambda i, j, k: (i, k))
    if b_rows is not None:
        b_spec = pl.BlockSpec(b.shape, lambda i, j, k: (0, 0), pipeline_mode=pl.Buffered(1))
    else:
        b_spec = pl.BlockSpec((tn, tk), lambda i, j, k: (j, k)) if tb else pl.BlockSpec((tk, tn), lambda i, j, k: (k, j))
    o_spec = pl.BlockSpec((tm, tn), lambda i, j, k: (i, j))
    ins, specs = [a, b], [a_spec, b_spec]
    if add is not None:
        ins.append(add)
        specs.append(o_spec)
    if dep is not None:
        ins.append(dep)
        specs.append(pl.BlockSpec(memory_space=pl.ANY))
    return pl.pallas_call(
        body, name=name, out_shape=_sds((m, n), out_dtype), grid=(m // tm, n // tn, nk), in_specs=specs,
        out_specs=o_spec, scratch_shapes=[] if nk == 1 else [pltpu.VMEM((tm, tn), F32)],
        compiler_params=_params(("parallel", "parallel", "arbitrary")))(*ins)


def _rms(x):
    return lax.rsqrt(jnp.mean(x * x, axis=-1, keepdims=True) + EPS)


def mixer_input_fwd(x, gain, w_in_t, gq_na, gk_na, gq_sw, gk_sw, *, name, tm=512):
    def body(x_ref, g_ref, w_ref, gqa_ref, gka_ref, gqs_ref, gks_ref, bd_ref, bd2_ref,
             h_ref, z_ref, zg_ref, qa_ref, ka_ref, va_ref, qs_ref, kv_ref):
        x = x_ref[...]
        h = (x * _rms(x) * g_ref[...]).astype(BF16)
        h_ref[...] = h
        z = _dot(h, w_ref[0:ATT_W, :], NT).astype(BF16)
        z_ref[...] = z
        zg_ref[...] = _dot(h, w_ref[ATT_W:IN_W, :], NT).astype(BF16)
        bd = bd_ref[...]

        def norm(v, g, bdm):
            v = v.astype(F32)
            return v * lax.rsqrt(_group_mean(v * v, bdm) + EPS) * g

        qa_ref[...] = (norm(z[:, 0:512], gqa_ref[...], bd) * QK_SCALE).astype(BF16)
        ka_ref[...] = norm(z[:, 512:1024], gka_ref[...], bd).astype(BF16)
        va_ref[...] = z[:, 1024:1536]
        qs_ref[...] = (norm(z[:, 1536:2048], gqs_ref[...], bd) * QK_SCALE).astype(BF16)
        kv_ref[:, 0:128] = norm(z[:, 2048:2176], gks_ref[...], bd2_ref[...]).astype(BF16)
        kv_ref[:, 128:256] = z[:, 2176:2304]

    s = x.shape[0]
    tile = pl.BlockSpec((tm, DM), lambda i: (i, 0))
    vec = lambda w: pl.BlockSpec((1, w), lambda i: (0, 0))
    att = pl.BlockSpec((tm, 512), lambda i: (i, 0))
    g512 = lambda g: jnp.tile(g.reshape(1, HD), (1, 8))
    q = _sds((s, 512), BF16)
    return pl.pallas_call(
        body, name=name,
        out_shape=(_sds((s, DM), BF16), _sds((s, ATT_W), BF16), _sds((s, GATE_W), BF16), q, q, q, q, _sds((s, 256), BF16)),
        grid=(s // tm,),
        in_specs=[tile, vec(DM), pl.BlockSpec((IN_W, DM), lambda i: (0, 0), pipeline_mode=pl.Buffered(1)),
                  vec(512), vec(512), vec(512), vec(128), pl.BlockSpec((512, 512), lambda i: (0, 0)),
                  pl.BlockSpec((128, 128), lambda i: (0, 0))],
        out_specs=(tile, pl.BlockSpec((tm, ATT_W), lambda i: (i, 0)), pl.BlockSpec((tm, GATE_W), lambda i: (i, 0)),
                   att, att, att, att, pl.BlockSpec((tm, 256), lambda i: (i, 0))),
        compiler_params=_params(("parallel",)))(
            x, gain, w_in_t, g512(gq_na), g512(gk_na), g512(gq_sw), jnp.tile(gk_sw.reshape(1, HD), (1, 2)),
            _block_diag(512), _block_diag(128))


def _rms_bwd_math(dh, x, gain):
    r = _rms(x)
    xh = x * r
    dgain = jnp.sum(dh * xh, axis=0, keepdims=True)
    dxn = dh * gain
    dx = r * (dxn - xh * jnp.mean(dxn * xh, axis=-1, keepdims=True))
    return dx, dgain


def mixer_input_bwd(dz, dzg, w_in_t, x, gain, dres, *, name, tm=512, dep=None):
    def body(dz_ref, dzg_ref, w_ref, x_ref, g_ref, dres_ref, *rest):
        dx_ref, dxb_ref, dg_ref = rest[-3:]

        @pl.when(pl.program_id(0) == 0)
        def _():
            dg_ref[...] = jnp.zeros_like(dg_ref)

        dh = _dot(dz_ref[...], w_ref[0:ATT_W, :]) + _dot(dzg_ref[...], w_ref[ATT_W:IN_W, :])
        dx, dg = _rms_bwd_math(dh, x_ref[...], g_ref[...])
        dx = dres_ref[...] + dx
        dx_ref[...] = dx
        dxb_ref[...] = dx.astype(BF16)
        dg_ref[...] += dg

    s = x.shape[0]
    tile = pl.BlockSpec((tm, DM), lambda i: (i, 0))
    vec = pl.BlockSpec((1, DM), lambda i: (0, 0))
    ins, specs = _with_dep(
        [dz, dzg, w_in_t, x, gain, dres],
        [pl.BlockSpec((tm, ATT_W), lambda i: (i, 0)), pl.BlockSpec((tm, GATE_W), lambda i: (i, 0)),
         pl.BlockSpec((IN_W, DM), lambda i: (0, 0), pipeline_mode=pl.Buffered(1)), tile, vec, tile], dep)
    return pl.pallas_call(
        body, name=name, out_shape=(_sds((s, DM), F32), _sds((s, DM), BF16), _sds((1, DM), F32)), grid=(s // tm,),
        in_specs=specs, out_specs=(tile, tile, vec), compiler_params=_params(("arbitrary",)))(*ins)


def mixer_input_dw(dz, dzg, h, *, name, tr=256):
    att_tiles = ATT_W // tr

    def body(dz_ref, dzg_ref, h_ref, o_ref):
        t = pl.program_id(0)
        cols = jnp.where(t < att_tiles, dz_ref[...], dzg_ref[...])
        o_ref[...] = _dot(cols, h_ref[...], TN).astype(BF16)

    s = h.shape[0]
    return pl.pallas_call(
        body, name=name, out_shape=_sds((IN_W, DM), BF16), grid=(IN_W // tr,),
        in_specs=[pl.BlockSpec((s, tr), lambda t: (0, jnp.minimum(t, att_tiles - 1))),
                  pl.BlockSpec((s, tr), lambda t: (0, jnp.maximum(t - att_tiles, 0))),
                  pl.BlockSpec((s, DM), lambda t: (0, 0))],
        out_specs=pl.BlockSpec((tr, DM), lambda t: (t, 0)), compiler_params=_params(("parallel",)))(dz, dzg, h)


def _with_dep(ins, specs, dep):
    if dep is None:
        return ins, specs
    return ins + [dep], specs + [pl.BlockSpec(memory_space=pl.ANY)]


def _resident_weight():
    return pl.BlockSpec((DFF, DM), lambda i: (0, 0), pipeline_mode=pl.Buffered(1))


def ffn_fwd(x, gain, wg, wu, wd, target=None, *, name, tm=512):
    def body(x_ref, g_ref, wg_ref, wu_ref, wd_ref, *rest):
        h_ref, gg_ref, uu_ref = rest[-3:]
        x = x_ref[...]
        h = (x * _rms(x) * g_ref[...]).astype(BF16)
        h_ref[...] = h
        gg = _dot(h, wg_ref[...], NT)
        uu = _dot(h, wu_ref[...], NT)
        gg_ref[...] = gg.astype(BF16)
        uu_ref[...] = uu.astype(BF16)
        act = (gg * jax.nn.sigmoid(gg) * uu).astype(BF16)
        y = x + 0.5 * _dot(act, wd_ref[...])
        if target is None:
            rest[0][...] = y
            return
        t_ref, dy_ref, dyb_ref, l_ref = rest[:4]

        @pl.when(pl.program_id(0) == 0)
        def _():
            l_ref[...] = jnp.zeros_like(l_ref)

        err = y - t_ref[...]
        dy = err * (1.0 / DM)
        dy_ref[...] = dy
        dyb_ref[...] = dy.astype(BF16)
        l_ref[...] += 0.5 * jnp.sum(jnp.mean(err * err, axis=-1, keepdims=True), axis=0, keepdims=True)

    s = x.shape[0]
    tile = pl.BlockSpec((tm, DM), lambda i: (i, 0))
    hid = pl.BlockSpec((tm, DFF), lambda i: (i, 0))
    w = _resident_weight()
    saved_shapes = (_sds((s, DM), BF16), _sds((s, DFF), BF16), _sds((s, DFF), BF16))
    ins, specs = [x, gain, wg, wu, wd], [tile, pl.BlockSpec((1, DM), lambda i: (0, 0)), w, w, w]
    if target is None:
        head_shapes, head_specs = (_sds((s, DM), F32),), (tile,)
    else:
        ins, specs = ins + [target], specs + [tile]
        head_shapes = (_sds((s, DM), F32), _sds((s, DM), BF16), _sds((1, 128), F32))
        head_specs = (tile, tile, pl.BlockSpec((1, 128), lambda i: (0, 0)))
    return pl.pallas_call(
        body, name=name, out_shape=head_shapes + saved_shapes, grid=(s // tm,), in_specs=specs,
        out_specs=head_specs + (tile, hid, hid),
        compiler_params=_params(("parallel",) if target is None else ("arbitrary",)))(*ins)


def ffn_bwd_tokens(dy, x, gain, gg, uu, wg, wu, wd, *, name, tm=256, dep=None):
    def body(dy_ref, x_ref, g_ref, gg_ref, uu_ref, wg_ref, wu_ref, wd_ref, *rest):
        dx_ref, dxb_ref, dgain_ref, act_ref, dg_ref, du_ref = rest[-6:]

        @pl.when(pl.program_id(0) == 0)
        def _():
            dgain_ref[...] = jnp.zeros_like(dgain_ref)

        dy = dy_ref[...]
        dact = _dot((0.5 * dy).astype(BF16), wd_ref[...], NT)
        g = gg_ref[...].astype(F32)
        u = uu_ref[...].astype(F32)
        sg = jax.nn.sigmoid(g)
        silu = g * sg
        act_ref[...] = (silu * u).astype(BF16)
        dg = (dact * u * (sg * (1.0 + g * (1.0 - sg)))).astype(BF16)
        du = (dact * silu).astype(BF16)
        dg_ref[...] = dg
        du_ref[...] = du
        dx, dgain = _rms_bwd_math(_dot(dg, wg_ref[...]) + _dot(du, wu_ref[...]), x_ref[...], g_ref[...])
        dx = dy + dx
        dx_ref[...] = dx
        dxb_ref[...] = dx.astype(BF16)
        dgain_ref[...] += dgain

    s = x.shape[0]
    tile = pl.BlockSpec((tm, DM), lambda i: (i, 0))
    vec = pl.BlockSpec((1, DM), lambda i: (0, 0))
    hid = pl.BlockSpec((tm, DFF), lambda i: (i, 0))
    hshape = _sds((s, DFF), BF16)
    w = _resident_weight()
    ins, specs = _with_dep([dy, x, gain, gg, uu, wg, wu, wd], [tile, tile, vec, hid, hid, w, w, w], dep)
    return pl.pallas_call(
        body, name=name, out_shape=(_sds((s, DM), F32), _sds((s, DM), BF16), _sds((1, DM), F32), hshape, hshape, hshape),
        grid=(s // tm,), in_specs=specs, out_specs=(tile, tile, vec, hid, hid, hid),
        compiler_params=_params(("arbitrary",)))(*ins)


def ffn_bwd_weights(h, dy, act, dg, du, *, name, tf=256):
    def body(h_ref, dy_ref, act_ref, dg_ref, du_ref, gwg_ref, gwu_ref, gwd_ref):
        h = h_ref[...]
        gwg_ref[...] = _dot(dg_ref[...], h, TN).astype(BF16)
        gwu_ref[...] = _dot(du_ref[...], h, TN).astype(BF16)
        gwd_ref[...] = (0.5 * _dot(act_ref[...], dy_ref[...], TN)).astype(BF16)

    s = h.shape[0]
    full = pl.BlockSpec((s, DM), lambda f: (0, 0))
    hid = pl.BlockSpec((s, tf), lambda f: (0, f))
    wt = pl.BlockSpec((tf, DM), lambda f: (f, 0))
    wshape = _sds((DFF, DM), BF16)
    return pl.pallas_call(
        body, name=name, out_shape=(wshape, wshape, wshape), grid=(DFF // tf,), in_specs=[full, full, hid, hid, hid],
        out_specs=(wt, wt, wt), compiler_params=_params(("parallel",)))(h, dy, act, dg, du)


def _group_mean(v, bd):
    hi = v.astype(BF16)
    lo = (v - hi.astype(F32)).astype(BF16)
    return _dot(hi, bd) + _dot(lo, bd)


def _block_diag(width):
    idx = np.arange(width) // HD
    return jnp.asarray((idx[:, None] == idx[None, :]).astype(np.float32) / HD, dtype=BF16)


def qknorm_bwd(z, dqa, dka, dva, dqs, dkv, gq_na, gk_na, gq_sw, gk_sw, *, name, tm=256):
    def body(zq_ref, zk_ref, zs_ref, zkv_ref, dqa_ref, dka_ref, dva_ref, dqs_ref, dkv_ref, gqa_ref, gka_ref, gqs_ref,
             gks_ref, bd_ref, bd2_ref, dz_ref, dgqa_ref, dgka_ref, dgqs_ref, dgks_ref):
        @pl.when(pl.program_id(0) == 0)
        def _():
            dgqa_ref[...] = jnp.zeros_like(dgqa_ref)
            dgka_ref[...] = jnp.zeros_like(dgka_ref)
            dgqs_ref[...] = jnp.zeros_like(dgqs_ref)
            dgks_ref[...] = jnp.zeros_like(dgks_ref)

        bd = bd_ref[...]

        def bwd(x, dy, g, bdm, dg_ref):
            x = x.astype(F32)
            r = lax.rsqrt(_group_mean(x * x, bdm) + EPS)
            xh = x * r
            dg_ref[...] += jnp.sum(dy * xh, axis=0, keepdims=True)
            dxn = dy * g
            return r * (dxn - xh * _group_mean(dxn * xh, bdm))

        dz_ref[:, 0:512] = bwd(zq_ref[...], dqa_ref[...] * QK_SCALE, gqa_ref[...], bd, dgqa_ref).astype(BF16)
        dz_ref[:, 512:1024] = bwd(zk_ref[...], dka_ref[...], gka_ref[...], bd, dgka_ref).astype(BF16)
        dz_ref[:, 1024:1536] = dva_ref[...].astype(BF16)
        dz_ref[:, 1536:2048] = bwd(zs_ref[...], dqs_ref[...] * QK_SCALE, gqs_ref[...], bd, dgqs_ref).astype(BF16)
        dkv = dkv_ref[...]
        dz_ref[:, 2048:2176] = bwd(zkv_ref[:, 0:128], dkv[:, 0:128], gks_ref[...], bd2_ref[...], dgks_ref).astype(BF16)
        dz_ref[:, 2176:2304] = dkv[:, 128:256].astype(BF16)

    s = z.shape[0]
    col = lambda j: pl.BlockSpec((tm, 512), lambda i, j=j: (i, j))
    t512 = pl.BlockSpec((tm, 512), lambda i: (i, 0))
    t256 = pl.BlockSpec((tm, 256), lambda i: (i, 0))
    vec = lambda w: pl.BlockSpec((1, w), lambda i: (0, 0))
    g512 = lambda g: jnp.tile(g.reshape(1, HD), (1, 8))
    return pl.pallas_call(
        body, name=name,
        out_shape=(_sds((s, ATT_W), BF16), _sds((1, 512), F32), _sds((1, 512), F32), _sds((1, 512), F32), _sds((1, 128), F32)),
        grid=(s // tm,),
        in_specs=[col(0), col(1), col(3), pl.BlockSpec((tm, 256), lambda i: (i, 8)), t512, t512, t512, t512, t256,
                  vec(512), vec(512), vec(512), vec(128), pl.BlockSpec((512, 512), lambda i: (0, 0)),
                  pl.BlockSpec((128, 128), lambda i: (0, 0))],
        out_specs=(pl.BlockSpec((tm, ATT_W), lambda i: (i, 0)), vec(512), vec(512), vec(512), vec(128)),
        compiler_params=_params(("arbitrary",)))(
            z, z, z, z, dqa, dka, dva, dqs, dkv, g512(gq_na), g512(gk_na), g512(gq_sw),
            jnp.tile(gk_sw.reshape(1, HD), (1, 2)), _block_diag(512), _block_diag(128))


def _na_row_start(r):
    return jnp.clip(r - NA_WR // 2, 0, ROWS - NA_WR)


def na_bias_table(rpb, *, name):
    t = jnp.pad(rpb, ((0, 0), (0, 2), (0, HD - (2 * NA_WC - 1))))
    pairs = jnp.concatenate([t[:, :16], t[:, 1:17]], axis=-1).reshape(NA_HEADS, 16, 1, 128)

    def body(t_ref, o_ref):
        p = pl.program_id(0)
        q = lax.broadcasted_iota(jnp.int32, (GRID_W, 128), 0)
        kc = lax.broadcasted_iota(jnp.int32, (GRID_W, 128), 1) & (GRID_W - 1)
        cs = jnp.clip(q - NA_WC // 2, 0, GRID_W - NA_WC)
        ok = (kc >= cs) & (kc < cs + NA_WC)
        for h in range(NA_HEADS):
            for pr in range(NA_WR // 2):
                x = jnp.broadcast_to(t_ref[h, 2 * pr - p + NA_WR - 1], (GRID_W, 128))
                b = pltpu.roll(x, 128 - (NA_WC - 1), 1, stride=1, stride_axis=0)
                o_ref[h, :, 128 * pr:128 * pr + 128] = jnp.where(ok, b, NEG)

    return pl.pallas_call(
        body, name=name, out_shape=_sds((NA_WR, NA_HEADS, GRID_W, NA_KEYS), F32), grid=(NA_WR,),
        in_specs=[pl.BlockSpec((NA_HEADS, 16, 1, 128), lambda p: (0, 0, 0, 0))],
        out_specs=pl.BlockSpec((None, NA_HEADS, GRID_W, NA_KEYS), lambda p: (p, 0, 0, 0)),
        compiler_params=_params(("parallel",)))(pairs)


def _lane_halves():
    lane = lax.broadcasted_iota(jnp.int32, (1, 128), 1)
    return lane < HD


def na_fwd(q, k, v, bias, *, name):
    def body(q_ref, k_ref, v_ref, b_ref, o_ref, lse_ref):
        r = pl.program_id(0)
        off = pl.multiple_of(_na_row_start(r) * GRID_W, GRID_W)
        first = _lane_halves()
        sels = [first, jnp.logical_not(first)]
        lanes = [slice(128 * j, 128 * j + 128) for j in range(NA_HEADS // 2)]
        q2s = [q_ref[:, l] for l in lanes]
        k2s = [k_ref[pl.ds(off, NA_KEYS), l] for l in lanes]
        v2s = [v_ref[pl.ds(off, NA_KEYS), l] for l in lanes]
        scores = []
        for h in range(NA_HEADS):
            j, half = divmod(h, 2)
            scores.append(_dot(jnp.where(sels[half], q2s[j], jnp.zeros_like(q2s[j])), k2s[j], NT))
        probs, lses = [], []
        for h in range(NA_HEADS):
            b = b_ref[h]
            s = jnp.where(b > 0.5 * NEG, scores[h] + b, NEG)
            m = jnp.max(s, axis=-1, keepdims=True)
            e = jnp.exp(s - m)
            l = jnp.sum(e, axis=-1, keepdims=True)
            probs.append((e / l).astype(BF16))
            lses.append(m + jnp.log(l))
        for j in range(NA_HEADS // 2):
            zero = jnp.zeros_like(v2s[j])
            o2 = (_dot(probs[2 * j], jnp.where(sels[0], v2s[j], zero))
                  + _dot(probs[2 * j + 1], jnp.where(sels[1], v2s[j], zero)))
            o_ref[:, lanes[j]] = o2.astype(BF16)
        lse_ref[...] = jnp.concatenate(lses, axis=1)

    s_tok = q.shape[0]
    full = pl.BlockSpec((s_tok, 512), lambda r: (0, 0))
    return pl.pallas_call(
        body, name=name, out_shape=(_sds((s_tok, 512), BF16), _sds((s_tok, NA_HEADS), F32)), grid=(ROWS,),
        in_specs=[pl.BlockSpec((GRID_W, 512), lambda r: (r, 0)), full, full,
                  pl.BlockSpec((None, NA_HEADS, GRID_W, NA_KEYS), lambda r: (r - _na_row_start(r), 0, 0, 0))],
        out_specs=(pl.BlockSpec((GRID_W, 512), lambda r: (r, 0)), pl.BlockSpec((GRID_W, NA_HEADS), lambda r: (r, 0))),
        compiler_params=_params(("parallel",)))(q, k, v, bias)


def na_bwd(q, k, v, o, do, lse, bias, *, name):
    def body(q_ref, k_ref, v_ref, o_ref, do_ref, lse_ref, b_ref, dq_ref, dk_ref, dv_ref, db_ref):
        r = pl.program_id(0)

        @pl.when(r == 0)
        def _():
            dk_ref[...] = jnp.zeros_like(dk_ref)
            dv_ref[...] = jnp.zeros_like(dv_ref)

        @pl.when((r <= NA_WR // 2) | (r > ROWS - NA_WR // 2))
        def _():
            db_ref[...] = jnp.zeros_like(db_ref)

        off = pl.multiple_of(_na_row_start(r) * GRID_W, GRID_W)
        first = _lane_halves()
        sels = [first, jnp.logical_not(first)]
        lanes = [slice(128 * j, 128 * j + 128) for j in range(NA_HEADS // 2)]
        q2s = [q_ref[:, l] for l in lanes]
        k2s = [k_ref[pl.ds(off, NA_KEYS), l] for l in lanes]
        v2s = [v_ref[pl.ds(off, NA_KEYS), l] for l in lanes]
        do2s = [do_ref[:, l] for l in lanes]
        prods = [do2s[j].astype(F32) * o_ref[:, lanes[j]].astype(F32) for j in range(NA_HEADS // 2)]
        lse = lse_ref[...]
        qhs, dohs, scores, dps = [], [], [], []
        for h in range(NA_HEADS):
            j, half = divmod(h, 2)
            qhs.append(jnp.where(sels[half], q2s[j], jnp.zeros_like(q2s[j])))
            dohs.append(jnp.where(sels[half], do2s[j], jnp.zeros_like(do2s[j])))
            scores.append(_dot(qhs[h], k2s[j], NT))
            dps.append(_dot(dohs[h], v2s[j], NT))
        pbs, dsbs = [], []
        for h in range(NA_HEADS):
            j, half = divmod(h, 2)
            b = b_ref[h]
            s = jnp.where(b > 0.5 * NEG, scores[h] + b, NEG)
            p = jnp.exp(s - lse[:, h:h + 1])
            delta = jnp.sum(jnp.where(sels[half], prods[j], 0.0), axis=-1, keepdims=True)
            ds = p * (dps[h] - delta)
            db_ref[h] += ds
            pbs.append(p.astype(BF16))
            dsbs.append(ds.astype(BF16))
        for j in range(NA_HEADS // 2):
            a, b = 2 * j, 2 * j + 1
            zero = jnp.zeros_like(k2s[j])
            dq_ref[:, lanes[j]] = (_dot(dsbs[a], jnp.where(sels[0], k2s[j], zero))
                                   + _dot(dsbs[b], jnp.where(sels[1], k2s[j], zero)))
            dk_ref[pl.ds(off, NA_KEYS), lanes[j]] += _dot(dsbs[a], qhs[a], TN) + _dot(dsbs[b], qhs[b], TN)
            dv_ref[pl.ds(off, NA_KEYS), lanes[j]] += _dot(pbs[a], dohs[a], TN) + _dot(pbs[b], dohs[b], TN)

    s_tok = q.shape[0]
    full = pl.BlockSpec((s_tok, 512), lambda r: (0, 0))
    row = pl.BlockSpec((GRID_W, 512), lambda r: (r, 0))
    bias_spec = pl.BlockSpec((None, NA_HEADS, GRID_W, NA_KEYS), lambda r: (r - _na_row_start(r), 0, 0, 0))
    return pl.pallas_call(
        body, name=name,
        out_shape=(_sds((s_tok, 512), F32), _sds((s_tok, 512), F32), _sds((s_tok, 512), F32),
                   _sds((NA_WR, NA_HEADS, GRID_W, NA_KEYS), F32)),
        grid=(ROWS,),
        in_specs=[row, full, full, row, row, pl.BlockSpec((GRID_W, NA_HEADS), lambda r: (r, 0)), bias_spec],
        out_specs=(row, full, full, bias_spec), compiler_params=_params(("arbitrary",)))(q, k, v, o, do, lse, bias)


def t5_bucket_map():
    rel = np.arange(SW_KEYS)[None, :] - SW_BLK - np.arange(SW_BLK)[:, None]
    nb = 16
    max_exact = nb // 2
    n = np.abs(rel)
    large = max_exact + (np.log(np.maximum(n, 1) / max_exact) / np.log(128 / max_exact) * (nb - max_exact)).astype(np.int32)
    large = np.minimum(large, nb - 1)
    return ((rel > 0) * nb + np.where(n < max_exact, n, large)).astype(np.int32)


def t5_bias(table, *, name):
    rel = np.arange(-SW_BLK, SW_BLK + 1)
    nb, max_exact = 16, 8
    n = np.abs(rel)
    large = max_exact + (np.log(np.maximum(n, 1) / max_exact) / np.log(128 / max_exact) * (nb - max_exact)).astype(np.int32)
    bucket = ((rel > 0) * nb + np.where(n < max_exact, n, np.minimum(large, nb - 1))).astype(np.int32)
    u = jnp.pad(table[jnp.asarray(bucket)].T, ((0, 0), (0, SW_KEYS - bucket.shape[0]))).reshape(8, 1, SW_KEYS)

    def body(u_ref, o_ref):
        for h in range(8):
            x = jnp.broadcast_to(u_ref[h], (SW_BLK, SW_KEYS))
            o_ref[h] = pltpu.roll(x, 0, 1, stride=1, stride_axis=0)

    return pl.pallas_call(body, name=name, out_shape=_sds((8, SW_BLK, SW_KEYS), F32), compiler_params=_params())(u)


def _sw_valid(n):
    a = lax.broadcasted_iota(jnp.int32, (SW_BLK, SW_KEYS), 0)
    j = lax.broadcasted_iota(jnp.int32, (SW_BLK, SW_KEYS), 1)
    kpos = (n - 1) * SW_BLK + j
    return (jnp.abs(j - SW_BLK - a) <= SW_BLK) & (kpos >= 0) & (kpos < SEQ)


def _dup_group(x2, g, first):
    rolled = pltpu.roll(x2, HD, 1)
    return jnp.where(first, x2, rolled) if g == 0 else jnp.where(first, rolled, x2)


def sw_fwd(q, kv, t5, sink, *, name):
    def body(q_ref, kv_ref, t5_ref, sink_ref, o_ref, lse_ref):
        n = pl.program_id(0)
        off = pl.multiple_of(n * SW_BLK, SW_BLK)
        first = _lane_halves()
        sels = [first, jnp.logical_not(first)]
        valid = _sw_valid(n)
        k2 = kv_ref[pl.ds(off, SW_KEYS), 0:128]
        v2 = kv_ref[pl.ds(off, SW_KEYS), 128:256]
        kk = [_dup_group(k2, g, first) for g in range(2)]
        vv = [_dup_group(v2, g, first) for g in range(2)]
        q2s = [q_ref[:, 128 * j:128 * j + 128] for j in range(4)]
        scores = []
        for h in range(8):
            j, half = divmod(h, 2)
            scores.append(_dot(jnp.where(sels[half], q2s[j], jnp.zeros_like(q2s[j])), kk[j // 2], NT))
        probs, lses = [], []
        for h in range(8):
            s = jnp.where(valid, scores[h] + t5_ref[h], NEG)
            snk = sink_ref[h]
            m = jnp.maximum(jnp.max(s, axis=-1, keepdims=True), snk)
            e = jnp.exp(s - m)
            den = jnp.sum(e, axis=-1, keepdims=True) + jnp.exp(snk - m)
            probs.append((e / den).astype(BF16))
            lses.append(m + jnp.log(den))
        outs = []
        for j in range(4):
            vg = vv[j // 2]
            zero = jnp.zeros_like(vg)
            outs.append(_dot(probs[2 * j], jnp.where(sels[0], vg, zero)) + _dot(probs[2 * j + 1], jnp.where(sels[1], vg, zero)))
        o_ref[...] = jnp.concatenate(outs, axis=1).astype(BF16)
        lse_ref[...] = jnp.concatenate(lses, axis=1)

    s_tok = q.shape[0]
    blk = pl.BlockSpec((SW_BLK, 512), lambda n: (n, 0))
    return pl.pallas_call(
        body, name=name, out_shape=(_sds((s_tok, 512), BF16), _sds((s_tok, 8), F32)), grid=(SW_NB,),
        in_specs=[blk, pl.BlockSpec(kv.shape, lambda n: (0, 0)), pl.BlockSpec((8, SW_BLK, SW_KEYS), lambda n: (0, 0, 0)),
                  pl.BlockSpec(memory_space=pltpu.SMEM)],
        out_specs=(blk, pl.BlockSpec((SW_BLK, 8), lambda n: (n, 0))), compiler_params=_params(("parallel",)))(q, kv, t5, sink)


def sw_bwd(q, kv, o, do, lse, t5, sink, *, name):
    def body(q_ref, kv_ref, o_ref, do_ref, lse_ref, t5_ref, sink_ref, dq_ref, dkv_ref, dt5_ref, dsink_ref):
        n = pl.program_id(0)

        @pl.when(n == 0)
        def _():
            dkv_ref[...] = jnp.zeros_like(dkv_ref)
            dt5_ref[...] = jnp.zeros_like(dt5_ref)
            dsink_ref[...] = jnp.zeros_like(dsink_ref)

        off = pl.multiple_of(n * SW_BLK, SW_BLK)
        first = _lane_halves()
        sels = [first, jnp.logical_not(first)]
        valid = _sw_valid(n)
        k2 = kv_ref[pl.ds(off, SW_KEYS), 0:128]
        v2 = kv_ref[pl.ds(off, SW_KEYS), 128:256]
        kk = [_dup_group(k2, g, first) for g in range(2)]
        vv = [_dup_group(v2, g, first) for g in range(2)]
        lanes = [slice(128 * j, 128 * j + 128) for j in range(4)]
        q2s = [q_ref[:, l] for l in lanes]
        do2s = [do_ref[:, l] for l in lanes]
        prods = [do2s[j].astype(F32) * o_ref[:, lanes[j]].astype(F32) for j in range(4)]
        lse = lse_ref[...]
        qhs, dohs, scores, dps = [], [], [], []
        for h in range(8):
            j, half = divmod(h, 2)
            qhs.append(jnp.where(sels[half], q2s[j], jnp.zeros_like(q2s[j])))
            dohs.append(jnp.where(sels[half], do2s[j], jnp.zeros_like(do2s[j])))
            scores.append(_dot(qhs[h], kk[j // 2], NT))
            dps.append(_dot(dohs[h], vv[j // 2], NT))
        pbs, dsbs, dss, dsinks = [], [], [], []
        for h in range(8):
            j, half = divmod(h, 2)
            s = jnp.where(valid, scores[h] + t5_ref[h], NEG)
            lse_h = lse[:, h:h + 1]
            p = jnp.exp(s - lse_h)
            delta = jnp.sum(jnp.where(sels[half], prods[j], 0.0), axis=-1, keepdims=True)
            ds = p * (dps[h] - delta)
            dss.append(ds)
            dsinks.append(-jnp.sum(jnp.exp(sink_ref[h] - lse_h) * delta, axis=0, keepdims=True))
            pbs.append(p.astype(BF16))
            dsbs.append(ds.astype(BF16))
        dt5_ref[...] += jnp.stack(dss)
        dsink_ref[...] += jnp.concatenate(dsinks, axis=1)
        dqs = []
        for j in range(4):
            a, b = 2 * j, 2 * j + 1
            zero = jnp.zeros_like(kk[j // 2])
            dqs.append(_dot(dsbs[a], jnp.where(sels[0], kk[j // 2], zero)) + _dot(dsbs[b], jnp.where(sels[1], kk[j // 2], zero)))
        dq_ref[...] = jnp.concatenate(dqs, axis=1)
        dk_groups, dv_groups = [], []
        for g in range(2):
            dkk = sum(_dot(dsbs[h], qhs[h], TN) for h in range(4 * g, 4 * g + 4))
            dvv = sum(_dot(pbs[h], dohs[h], TN) for h in range(4 * g, 4 * g + 4))
            dk_groups.append(dkk + pltpu.roll(dkk, HD, 1))
            dv_groups.append(dvv + pltpu.roll(dvv, HD, 1))
        dkv_ref[pl.ds(off, SW_KEYS), :] += jnp.concatenate(
            [jnp.where(first, dk_groups[0], dk_groups[1]), jnp.where(first, dv_groups[0], dv_groups[1])], axis=1)

    s_tok = q.shape[0]
    blk = pl.BlockSpec((SW_BLK, 512), lambda n: (n, 0))
    kv_spec = pl.BlockSpec(kv.shape, lambda n: (0, 0))
    t5_spec = pl.BlockSpec((8, SW_BLK, SW_KEYS), lambda n: (0, 0, 0))
    vec = pl.BlockSpec((1, 8), lambda n: (0, 0))
    return pl.pallas_call(
        body, name=name,
        out_shape=(_sds((s_tok, 512), F32), _sds(kv.shape, F32), _sds((8, SW_BLK, SW_KEYS), F32), _sds((1, 8), F32)),
        grid=(SW_NB,), in_specs=[blk, kv_spec, blk, blk, pl.BlockSpec((SW_BLK, 8), lambda n: (n, 0)), t5_spec,
                                 pl.BlockSpec(memory_space=pltpu.SMEM)],
        out_specs=(blk, kv_spec, t5_spec, vec), compiler_params=_params(("arbitrary",)))(q, kv, o, do, lse, t5, sink)


def mixer_output_fwd(o_na, o_sw, zg, bias, wa, ws, wo, res, *, name, tm=512):
    def body(ona_ref, osw_ref, z0_ref, z1_ref, b0_ref, b1_ref, wa_ref, ws_ref, wo_ref, res_ref, y_ref, pa_ref, ps_ref, m_ref):
        pa = _dot(ona_ref[...], wa_ref[...]).astype(BF16)
        ps = _dot(osw_ref[...], ws_ref[...]).astype(BF16)
        pa_ref[...] = pa
        ps_ref[...] = ps
        g0 = jax.nn.sigmoid(z0_ref[...] + b0_ref[...])
        g1 = jax.nn.sigmoid(z1_ref[...] + b1_ref[...])
        merged = (g0 * pa + g1 * ps).astype(BF16)
        m_ref[...] = merged
        y_ref[...] = res_ref[...] + _dot(merged, wo_ref[...])

    s = zg.shape[0]
    half = lambda j: pl.BlockSpec((tm, DM), lambda i, j=j: (i, j))
    bvec = lambda j: pl.BlockSpec((1, DM), lambda i, j=j: (0, j))
    att = pl.BlockSpec((tm, 512), lambda i: (i, 0))
    whole = lambda a: pl.BlockSpec(a.shape, lambda i: (0, 0), pipeline_mode=pl.Buffered(1))
    act = _sds((s, DM), BF16)
    return pl.pallas_call(
        body, name=name, out_shape=(_sds((s, DM), F32), act, act, act), grid=(s // tm,),
        in_specs=[att, att, half(0), half(1), bvec(0), bvec(1), whole(wa), whole(ws), whole(wo), half(0)],
        out_specs=(half(0),) * 4, compiler_params=_params(("parallel",)))(o_na, o_sw, zg, zg, bias, bias, wa, ws, wo, res)


def mixer_output_bwd(dy, zg, bias, pa, ps, wa, ws, wo, *, name, tm=512, dep=None):
    def body(dy_ref, z0_ref, z1_ref, b0_ref, b1_ref, pa_ref, ps_ref, wa_ref, ws_ref, wo_ref, *rest):
        dpa_ref, dps_ref, dz_ref, db_ref, dona_ref, dosw_ref = rest[-6:]

        @pl.when(pl.program_id(0) == 0)
        def _():
            db_ref[...] = jnp.zeros_like(db_ref)

        dm = _dot(dy_ref[...].astype(BF16), wo_ref[...], NT)
        g0 = jax.nn.sigmoid(z0_ref[...] + b0_ref[...])
        g1 = jax.nn.sigmoid(z1_ref[...] + b1_ref[...])
        dpa = (dm * g0).astype(BF16)
        dps = (dm * g1).astype(BF16)
        dpa_ref[...] = dpa
        dps_ref[...] = dps
        dz0 = dm * pa_ref[...] * g0 * (1.0 - g0)
        dz1 = dm * ps_ref[...] * g1 * (1.0 - g1)
        dz_ref[:, 0:DM] = dz0.astype(BF16)
        dz_ref[:, DM:2 * DM] = dz1.astype(BF16)
        db_ref[:, 0:DM] += jnp.sum(dz0, axis=0, keepdims=True)
        db_ref[:, DM:2 * DM] += jnp.sum(dz1, axis=0, keepdims=True)
        dona_ref[...] = _dot(dpa, wa_ref[...], NT).astype(BF16)
        dosw_ref[...] = _dot(dps, ws_ref[...], NT).astype(BF16)

    s = zg.shape[0]
    half = lambda j: pl.BlockSpec((tm, DM), lambda i, j=j: (i, j))
    bvec = lambda j: pl.BlockSpec((1, DM), lambda i, j=j: (0, j))
    att = pl.BlockSpec((tm, 512), lambda i: (i, 0))
    whole = lambda a: pl.BlockSpec(a.shape, lambda i: (0, 0), pipeline_mode=pl.Buffered(1))
    ins, specs = _with_dep([dy, zg, zg, bias, bias, pa, ps, wa, ws, wo],
                           [half(0), half(0), half(1), bvec(0), bvec(1), half(0), half(0), whole(wa), whole(ws), whole(wo)], dep)
    return pl.pallas_call(
        body, name=name,
        out_shape=(_sds((s, DM), BF16), _sds((s, DM), BF16), _sds((s, GATE_W), BF16), _sds((1, GATE_W), F32),
                   _sds((s, 512), BF16), _sds((s, 512), BF16)),
        grid=(s // tm,), in_specs=specs,
        out_specs=(half(0), half(0), pl.BlockSpec((tm, GATE_W), lambda i: (i, 0)), pl.BlockSpec((1, GATE_W), lambda i: (0, 0)),
                   att, att),
        compiler_params=_params(("arbitrary",)))(*ins)


def adamw_small(ws, gs, ms, vs, *, name):
    cnt = len(ws)

    def body(*refs):
        ins, outs = refs[:4 * cnt], refs[4 * cnt:]
        for i in range(cnt):
            w_ref, g_ref, m_ref, v_ref = ins[4 * i:4 * i + 4]
            d_ref, nm_ref, nv_ref = outs[3 * i:3 * i + 3]
            g = g_ref[...]
            nm = ADAM_B1 * m_ref[...] + (1.0 - ADAM_B1) * g
            nv = ADAM_B2 * v_ref[...] + (1.0 - ADAM_B2) * jnp.square(g)
            m_hat = nm / (1.0 - ADAM_B1 ** ADAM_STEP)
            v_hat = nv / (1.0 - ADAM_B2 ** ADAM_STEP)
            d_ref[...] = -ADAM_LR * (m_hat / (jnp.sqrt(v_hat) + ADAM_EPS) + ADAM_WD * w_ref[...])
            nm_ref[...] = nm
            nv_ref[...] = nv

    flat = [a for i in range(cnt) for a in (ws[i], gs[i], ms[i], vs[i])]
    res = pl.pallas_call(
        body, name=name, out_shape=tuple(_sds(ws[i].shape, F32) for i in range(cnt) for _ in range(3)),
        compiler_params=_params())(*flat)
    return [tuple(res[3 * i:3 * i + 3]) for i in range(cnt)]


def adamw_layer(ws, ms, vs, mines, theirs, cidx, layer, filled=None, *, name):
    cnt = len(ws)
    _, k, n = ws[0].shape
    nt = 2
    tk = k // 2 // nt

    def body(c_ref, *refs):
        own = pl.program_id(0) == c_ref[0]
        outs = refs[-4 * cnt:]
        for i in range(cnt):
            w_ref, m_ref, v_ref, a_ref, b_ref = refs[5 * i:5 * i + 5]
            g_ref, d_ref, nm_ref, nv_ref = outs[4 * i:4 * i + 4]
            g = jnp.where(own, a_ref[...], b_ref[...])
            g_ref[...] = g
            nm = ADAM_B1 * m_ref[...] + (1.0 - ADAM_B1) * g
            nv = ADAM_B2 * v_ref[...] + (1.0 - ADAM_B2) * jnp.square(g)
            m_hat = nm / (1.0 - ADAM_B1 ** ADAM_STEP)
            v_hat = nv / (1.0 - ADAM_B2 ** ADAM_STEP)
            d_ref[...] = -ADAM_LR * (m_hat / (jnp.sqrt(v_hat) + ADAM_EPS) + ADAM_WD * w_ref[...])
            nm_ref[...] = nm
            nv_ref[...] = nv

    full = pl.BlockSpec((None, tk, n), lambda hf, t, c: (layer, hf * nt + t, 0))
    half_mine = pl.BlockSpec((tk, n), lambda hf, t, c: (jnp.where(hf == c[0], t, 0), 0))
    half_theirs = pl.BlockSpec((tk, n), lambda hf, t, c: (jnp.where(hf != c[0], t, 0), 0))
    out = _sds(ws[0].shape, F32)
    ins, specs, aliases = [cidx], [], {}
    for i in range(cnt):
        ins += [ws[i], ms[i], vs[i], mines[i], theirs[i]]
        specs += [full, full, full, half_mine, half_theirs]
    if filled is not None:
        aliases = {len(ins) + j: j for j in range(4 * cnt)}
        ins += [a for f in filled for a in f]
        specs += [pl.BlockSpec(memory_space=pl.ANY)] * (4 * cnt)
    res = pl.pallas_call(
        body, name=name, out_shape=(out,) * (4 * cnt),
        grid_spec=pltpu.PrefetchScalarGridSpec(
            num_scalar_prefetch=1, grid=(2, nt), in_specs=specs, out_specs=(full,) * (4 * cnt)),
        input_output_aliases=aliases,
        compiler_params=_params(("arbitrary", "arbitrary")))(*ins)
    return [tuple(res[4 * i:4 * i + 4]) for i in range(cnt)]


def t5_table_grad(dt5_a, dt5_b, *, name):
    def body(a_ref, b_ref, map_ref, o_ref):
        d = a_ref[...] + b_ref[...]
        bucket = map_ref[...]
        for b in range(32):
            hit = (bucket == b)[None]
            o_ref[b] = jnp.sum(jnp.sum(jnp.where(hit, d, 0.0), axis=2), axis=1, keepdims=True)

    return pl.pallas_call(
        body, name=name, out_shape=_sds((32, 8, 1), F32), compiler_params=_params())(
            dt5_a, dt5_b, jnp.asarray(t5_bucket_map()))


def rpb_grad(dbias, *, name):
    def body(d_ref, rev_ref, o_ref):
        rev = rev_ref[...]
        for h in range(NA_HEADS):
            for pr in range(NA_WR // 2):
                d = d_ref[h, :, 128 * pr:128 * pr + 128]
                hi = d.astype(BF16)
                lo = (d - hi.astype(F32)).astype(BF16)
                flipped = _dot(rev, hi) + _dot(rev, lo)
                o_ref[h, pr] = jnp.sum(pltpu.roll(flipped, 0, 1, stride=1, stride_axis=0), axis=0, keepdims=True)

    anti = jnp.asarray(np.eye(GRID_W, dtype=np.float32)[::-1], dtype=BF16)
    e = pl.pallas_call(
        body, name=name, out_shape=_sds((NA_WR, NA_HEADS, NA_WR // 2, 1, 128), F32), grid=(NA_WR,),
        in_specs=[pl.BlockSpec((None, NA_HEADS, GRID_W, NA_KEYS), lambda p: (p, 0, 0, 0)),
                  pl.BlockSpec((GRID_W, GRID_W), lambda p: (0, 0))],
        out_specs=pl.BlockSpec((None, NA_HEADS, NA_WR // 2, 1, 128), lambda p: (p, 0, 0, 0, 0)),
        compiler_params=_params(("parallel",)))(dbias, anti)
    nci, nri = 2 * NA_WC - 1, 2 * NA_WR - 1
    e = e.reshape(NA_WR, NA_HEADS, NA_WR // 2, 128).transpose(0, 2, 1, 3).reshape(NA_WR * NA_WR // 2, NA_HEADS, 128)
    parts = jnp.concatenate([e[..., 48:48 + nci], jnp.concatenate([e[..., 112:128], e[..., 0:nci - 16]], axis=-1)], axis=0)
    p, pr = np.arange(NA_WR)[:, None], np.arange(NA_WR // 2)[None, :]
    ri = np.concatenate([(2 * pr - p + NA_WR - 1).reshape(-1), (2 * pr - p + NA_WR).reshape(-1)])
    pick = jnp.asarray((ri[None, :] == np.arange(16)[:, None]).astype(np.float32))
    out = mm(pick, parts.reshape(2 * NA_WR * NA_WR // 2, NA_HEADS * nci), name=name + "_rows", exact=True)
    return out.reshape(16, NA_HEADS, nci)[:nri].transpose(1, 0, 2)


BIG = ("ffn1_w_gate", "ffn1_w_up", "ffn1_w_down", "w_in", "w_branch_na", "w_branch_sw", "w_out",
       "ffn2_w_gate", "ffn2_w_up", "ffn2_w_down")
SMALL = ("ffn1_norm", "mix_norm", "b_gate", "na_q_norm", "na_k_norm", "na_rpb", "sw_q_norm", "sw_k_norm", "sw_sink",
         "ffn2_norm")


def _cols_to_full(w4):
    return w4.transpose(1, 0, 2).reshape(w4.shape[1], NSH * w4.shape[2])


def _full_to_cols(w):
    return w.reshape(w.shape[0], NSH, w.shape[1] // NSH).transpose(1, 0, 2)


def _mixer_weights(g):
    w_in_t = g["w_in"].reshape(IN_W, DM)
    return dict(w_in_t=w_in_t, wa=_cols_to_full(g["w_branch_na"]),
                ws=_cols_to_full(g["w_branch_sw"]), wo=g["w_out"].reshape(DM, DM))


GROUPS = {"ffn1": ("ffn1_w_gate", "ffn1_w_up", "ffn1_w_down"), "mix": ("w_in", "w_branch_na", "w_branch_sw", "w_out"),
          "ffn2": ("ffn2_w_gate", "ffn2_w_up", "ffn2_w_down")}


def layer_fwd(x, p, weights, t5b, target=None):
    row = lambda v: v.reshape(1, -1)
    stacked = lambda g: {n: a.reshape(DFF, DM) for n, a in g.items()}
    g1 = stacked(weights("ffn1", x))
    y1, h1, gg1, uu1 = ffn_fwd(x, row(p["ffn1_norm"]), g1["ffn1_w_gate"], g1["ffn1_w_up"], g1["ffn1_w_down"], name="ffn_fwd")
    w = _mixer_weights(weights("mix", y1))
    hm, z, zg, qa, ka, va, qs, kv = mixer_input_fwd(y1, row(p["mix_norm"]), w["w_in_t"], p["na_q_norm"], p["na_k_norm"],
                                                    p["sw_q_norm"], p["sw_k_norm"], name="mixer_input_fwd")
    bias = p["na_bias"]
    o_na, lse_na = na_fwd(qa, ka, va, bias, name="na_fwd")
    kvp = jnp.pad(kv, ((SW_BLK, SW_BLK), (0, 0)))
    sink = p["sw_sink"]
    o_sw, lse_sw = sw_fwd(qs, kvp, t5b, sink, name="sw_fwd")
    y2, pa, ps, merged = mixer_output_fwd(o_na, o_sw, zg, row(p["b_gate"]), w["wa"], w["ws"], w["wo"], y1,
                                          name="mixer_output_fwd")
    g2 = stacked(weights("ffn2", y2))
    *y3, h2, gg2, uu2 = ffn_fwd(y2, row(p["ffn2_norm"]), g2["ffn2_w_gate"], g2["ffn2_w_up"], g2["ffn2_w_down"], target,
                                name="ffn_fwd")
    y3 = y3[0] if target is None else tuple(y3)
    saved = dict(x=x, y1=y1, h1=h1, gg1=gg1, uu1=uu1, hm=hm, z=z, zg=zg, qa=qa, ka=ka, va=va, qs=qs, kvp=kvp, bias=bias,
                 o_na=o_na, lse_na=lse_na, o_sw=o_sw, lse_sw=lse_sw, pa=pa, ps=ps, merged=merged, y2=y2, h2=h2, gg2=gg2,
                 uu2=uu2, w=w, sink=sink, g1=g1, g2=g2)
    return y3, saved


def layer_bwd(dy3, dy3_bf, sv, p, t5b, emit, dep=None):
    w, g1, g2 = sv["w"], sv["g1"], sv["g2"]
    row = lambda v: v.reshape(1, -1)
    fold = lambda v: v.reshape(-1, HD).sum(axis=0)
    small = {}
    dy2, _, small["ffn2_norm"], act, dg, du = ffn_bwd_tokens(
        dy3, sv["y2"], row(p["ffn2_norm"]), sv["gg2"], sv["uu2"], g2["ffn2_w_gate"], g2["ffn2_w_up"], g2["ffn2_w_down"],
        name="ffn_bwd_tokens", dep=dep)
    shards = lambda gs: [g.reshape(NSH, FSH, DM) for g in gs]
    token = emit("ffn2", shards(ffn_bwd_weights(sv["h2"], dy3_bf, act, dg, du, name="ffn_bwd_weights")))
    dpa, dps, dzg, small["b_gate"], do_na, do_sw = mixer_output_bwd(
        dy2, sv["zg"], row(p["b_gate"]), sv["pa"], sv["ps"], w["wa"], w["ws"], w["wo"], name="mixer_output_bwd", dep=token)
    gw_out = mm(sv["merged"], dy2, ta=True, out_dtype=BF16, name="out_proj_dw").reshape(NSH, DM // NSH, DM)
    gw_na = _full_to_cols(mm(sv["o_na"], dpa, ta=True, out_dtype=BF16, name="branch_dw"))
    gw_sw = _full_to_cols(mm(sv["o_sw"], dps, ta=True, out_dtype=BF16, name="branch_dw"))
    dqa, dka, dva, dbias = na_bwd(sv["qa"], sv["ka"], sv["va"], sv["o_na"], do_na, sv["lse_na"], sv["bias"], name="na_bwd")
    dqs, dkvp, dt5, dsink = sw_bwd(sv["qs"], sv["kvp"], sv["o_sw"], do_sw, sv["lse_sw"], t5b, sv["sink"], name="sw_bwd")
    dkv = dkvp[SW_BLK:SW_BLK + SEQ]
    dz, dgqa, dgka, dgqs, dgks = qknorm_bwd(sv["z"], dqa, dka, dva, dqs, dkv, p["na_q_norm"], p["na_k_norm"],
                                            p["sw_q_norm"], p["sw_k_norm"], name="qknorm_bwd")
    small["na_q_norm"], small["na_k_norm"], small["sw_q_norm"], small["sw_k_norm"] = fold(dgqa), fold(dgka), fold(dgqs), fold(dgks)
    small["na_rpb"] = rpb_grad(dbias, name="rpb_grad")
    small["sw_sink"] = dsink
    gw_in = mixer_input_dw(dz, dzg, sv["hm"], name="mixer_input_dw").reshape(NSH, IN_W // NSH, DM)
    token = emit("mix", (gw_in, gw_na, gw_sw, gw_out))
    dy1, dy1_bf, small["mix_norm"] = mixer_input_bwd(dz, dzg, w["w_in_t"], sv["y1"], row(p["mix_norm"]), dy2,
                                                     name="mixer_input_bwd", dep=token)
    dx, dx_bf, small["ffn1_norm"], act, dg, du = ffn_bwd_tokens(
        dy1, sv["x"], row(p["ffn1_norm"]), sv["gg1"], sv["uu1"], g1["ffn1_w_gate"], g1["ffn1_w_up"], g1["ffn1_w_down"],
        name="ffn_bwd_tokens")
    emit("ffn1", shards(ffn_bwd_weights(sv["h1"], dy1_bf, act, dg, du, name="ffn_bwd_weights")))
    return dx, dx_bf, small, dt5


ANY = pl.BlockSpec(memory_space=pl.ANY)


def _place():
    x, y, c = lax.axis_index("x"), lax.axis_index("y"), lax.axis_index("c")
    chips = [(1 - x, y), (x, 1 - y), (1 - x, 1 - y)]
    return x, y, c, chips


def _remote(src, dst, send_sem, recv_sem, to):
    return pltpu.make_async_remote_copy(src_ref=src, dst_ref=dst, send_sem=send_sem, recv_sem=recv_sem, device_id=to,
                                        device_id_type=MESH)


HBM = pl.BlockSpec(memory_space=pltpu.HBM)
SEM = pl.BlockSpec(memory_space=pltpu.SEMAPHORE)
ORDERED_EFFECT = pltpu.SideEffectType.DATAFLOW_SIDE_EFFECTING


def _in_hbm(v):
    return pltpu.with_memory_space_constraint(v, pltpu.HBM)


def _row_half(ref_shape_rows, c):
    half = ref_shape_rows // 2
    return pl.ds(c * half, half)


def _ici_gather_copies(w, land, send_sems, recv_sems):
    x, y, c, chips = _place()
    me = 2 * x + y
    copies = []
    for a in range(len(w)):
        rows = _row_half(w[a].shape[0], c)
        for k, chip in enumerate(chips):
            copies.append(_remote(w[a].at[rows], land[a].at[me, rows], send_sems.at[4 * a + k], recv_sems.at[4 * a + k],
                                  (*chip, c)))
        copies.append(_remote(w[a], land[a].at[me], send_sems.at[4 * a + 3], recv_sems.at[4 * a + 3], (x, y, 1 - c)))
    return copies


def _d2d_gather_copies(w, land, send_sems, recv_sems):
    x, y, c, chips = _place()
    copies = []
    for a in range(len(w)):
        rows = _row_half(w[a].shape[0], c)
        for k, (cx, cy) in enumerate(chips):
            blk = land[a].at[2 * cx + cy, rows]
            copies.append(_remote(blk, blk, send_sems.at[3 * a + k], recv_sems.at[3 * a + k], (x, y, 1 - c)))
    return copies


def _d2d_gather_waits(w, land, send_sems, recv_sems):
    x, y, c, chips = _place()
    waits = []
    for a in range(len(w)):
        rows = _row_half(w[a].shape[0], 1 - c)
        for k, (cx, cy) in enumerate(chips):
            blk = land[a].at[2 * cx + cy, rows]
            waits.append(_remote(blk, blk, send_sems.at[3 * a + k], recv_sems.at[3 * a + k], (x, y, 1 - c)))
    return waits


def gather_start(groups, dep=None, *, name):
    sizes = [len(g) for g in groups]
    shards = [s for g in groups for s in g]
    n, ng = len(shards), len(groups)
    extra = [] if dep is None else [dep]

    def body(*refs):
        first_out = 2 * n + len(extra)
        w, land, sems = refs[:n], refs[n:2 * n], refs[first_out:first_out + 2 * ng]
        off = 0
        for gi, size in enumerate(sizes):
            for cp in _ici_gather_copies(w[off:off + size], land[off:off + size], sems[2 * gi], sems[2 * gi + 1]):
                cp.start()
            off += size

    lands = [lax.empty((NSH,) + s.shape, s.dtype) for s in shards]
    sem_shapes = tuple(pltpu.SemaphoreType.DMA((4 * size,)) for size in sizes for _ in range(2))
    res = pl.pallas_call(
        body, name=name,
        out_shape=sem_shapes + tuple(pltpu.HBM(s.shape, s.dtype) for s in shards) + tuple(pltpu.HBM(l.shape, l.dtype) for l in lands),
        in_specs=[HBM] * (2 * n) + [ANY] * len(extra), out_specs=(SEM,) * (2 * ng) + (HBM,) * (2 * n),
        input_output_aliases={i: 2 * ng + i for i in range(2 * n)},
        compiler_params=pltpu.CompilerParams(has_side_effects=ORDERED_EFFECT))(
            *[_in_hbm(s) for s in shards], *[_in_hbm(l) for l in lands], *extra)
    out, off = [], 0
    for gi, size in enumerate(sizes):
        out.append((res[2 * gi], res[2 * gi + 1], list(res[2 * ng + off:2 * ng + off + size]),
                    list(res[2 * ng + n + off:2 * ng + n + off + size])))
        off += size
    return out


def gather_wait(send_sems, recv_sems, shards, lands, after, *, name):
    n = len(shards)

    def body(*refs):
        w, land = refs[:n], refs[n:2 * n]
        send, recv = refs[2 * n:2 * n + 2]
        for cp in _ici_gather_copies(w, land, send, recv):
            cp.wait_send()
            cp.wait_recv()

    res = pl.pallas_call(
        body, name=name,
        out_shape=tuple(pltpu.HBM(s.shape, s.dtype) for s in shards) + tuple(pltpu.HBM(l.shape, l.dtype) for l in lands),
        in_specs=[HBM] * (2 * n) + [SEM, SEM] + [ANY] * len(after), out_specs=(HBM,) * (2 * n),
        input_output_aliases={i: i for i in range(2 * n)},
        compiler_params=pltpu.CompilerParams(has_side_effects=ORDERED_EFFECT))(*shards, *lands, send_sems, recv_sems, *after)
    return list(res[:n]), list(res[n:])


def gather_finish(shards, lands, *, name):
    n = len(shards)

    def body(*refs):
        w, land = refs[:n], refs[n:2 * n]
        send_sems, recv_sems = refs[3 * n:]
        d2d = _d2d_gather_copies(w, land, send_sems, recv_sems)
        for cp in d2d:
            cp.start()
        for cp in _d2d_gather_waits(w, land, send_sems, recv_sems):
            cp.wait_recv()
        for cp in d2d:
            cp.wait_send()

    return list(pl.pallas_call(
        body, name=name, out_shape=tuple(pltpu.HBM(l.shape, l.dtype) for l in lands),
        in_specs=[ANY] * (2 * n), out_specs=tuple([ANY] * n), input_output_aliases={n + i: i for i in range(n)},
        scratch_shapes=[pltpu.SemaphoreType.DMA((3 * n,)), pltpu.SemaphoreType.DMA((3 * n,))])(*shards, *lands))


def _pair_exchange_copies(g, buf, send_sems, recv_sems):
    x, y, c, _ = _place()
    copies = []
    for a in range(len(g)):
        half = g[a].shape[1] // 2
        copies.append(_remote(g[a].at[:, pl.ds((1 - c) * half, half)], buf[a], send_sems.at[a], recv_sems.at[a], (x, y, 1 - c)))
    return copies


def pair_exchange_start(grads, dep=None, *, name):
    n = len(grads)
    extra = [] if dep is None else [dep]

    def body(*refs):
        sems = refs[2 * n + len(extra):]
        for cp in _pair_exchange_copies(refs[:n], refs[n:2 * n], sems[0], sems[1]):
            cp.start()
        refs[-1][...] = jnp.zeros_like(refs[-1])

    lands = [lax.empty((NSH, g.shape[1] // 2, g.shape[2]), g.dtype) for g in grads]
    res = pl.pallas_call(
        body, name=name,
        out_shape=(pltpu.SemaphoreType.DMA((n,)), pltpu.SemaphoreType.DMA((n,)))
        + tuple(pltpu.HBM(g.shape, g.dtype) for g in grads) + tuple(pltpu.HBM(l.shape, l.dtype) for l in lands)
        + (_sds((8, 128), F32),),
        in_specs=[HBM] * (2 * n) + [ANY] * len(extra),
        out_specs=(SEM, SEM) + (HBM,) * (2 * n) + (pl.BlockSpec(memory_space=pltpu.VMEM),),
        input_output_aliases={i: 2 + i for i in range(2 * n)},
        compiler_params=pltpu.CompilerParams(has_side_effects=ORDERED_EFFECT))(
            *[_in_hbm(g) for g in grads], *[_in_hbm(l) for l in lands], *extra)
    return res[0], res[1], list(res[2:2 + n]), list(res[2 + n:2 + 2 * n]), res[-1]


def pair_exchange_wait(send_sems, recv_sems, grads, lands, after, *, name):
    n = len(grads)

    def body(*refs):
        for cp in _pair_exchange_copies(refs[:n], refs[n:2 * n], refs[2 * n], refs[2 * n + 1]):
            cp.wait_send()
            cp.wait_recv()

    res = pl.pallas_call(
        body, name=name,
        out_shape=tuple(pltpu.HBM(g.shape, g.dtype) for g in grads) + tuple(pltpu.HBM(l.shape, l.dtype) for l in lands),
        in_specs=[HBM] * (2 * n) + [SEM, SEM] + [ANY] * len(after), out_specs=(HBM,) * (2 * n),
        input_output_aliases={i: i for i in range(2 * n)},
        compiler_params=pltpu.CompilerParams(has_side_effects=ORDERED_EFFECT))(*grads, *lands, send_sems, recv_sems, *after)
    return list(res[:n]), list(res[n:])


def _chip_exchange_copies(s, buf, send_sems, recv_sems):
    x, y, c, chips = _place()
    return [_remote(s[a].at[2 * cx + cy], buf[a].at[k], send_sems.at[3 * a + k], recv_sems.at[3 * a + k], (cx, cy, c))
            for a in range(len(s)) for k, (cx, cy) in enumerate(chips)]


def exchange_start(sums, grads, *, name):
    n1, n2 = len(sums), len(grads)

    def body(*refs):
        first_out = 2 * (n1 + n2)
        chip = _chip_exchange_copies(refs[:n1], refs[n1:2 * n1], refs[first_out], refs[first_out + 1])
        pair = _pair_exchange_copies(refs[2 * n1:2 * n1 + n2], refs[2 * n1 + n2:first_out], refs[first_out + 2],
                                     refs[first_out + 3])
        for cp in chip + pair:
            cp.start()
        refs[-1][...] = jnp.zeros_like(refs[-1])

    chip_lands = [lax.empty((3,) + s.shape[1:], s.dtype) for s in sums]
    pair_lands = [lax.empty((NSH, g.shape[1] // 2, g.shape[2]), g.dtype) for g in grads]
    arrays = list(sums) + chip_lands + list(grads) + pair_lands
    res = pl.pallas_call(
        body, name=name,
        out_shape=(pltpu.SemaphoreType.DMA((3 * n1,)), pltpu.SemaphoreType.DMA((3 * n1,)), pltpu.SemaphoreType.DMA((n2,)),
                   pltpu.SemaphoreType.DMA((n2,)))
        + tuple(pltpu.HBM(a.shape, a.dtype) for a in arrays) + (_sds((8, 128), F32),),
        in_specs=[HBM] * len(arrays), out_specs=(SEM,) * 4 + (HBM,) * len(arrays) + (pl.BlockSpec(memory_space=pltpu.VMEM),),
        input_output_aliases={i: 4 + i for i in range(len(arrays))},
        compiler_params=pltpu.CompilerParams(has_side_effects=ORDERED_EFFECT))(*[_in_hbm(a) for a in arrays])
    thru = list(res[4:4 + len(arrays)])
    chip = (res[0], res[1], thru[:n1], thru[n1:2 * n1])
    pair = (res[2], res[3], thru[2 * n1:2 * n1 + n2], thru[2 * n1 + n2:])
    return chip, pair, res[-1]


def chip_exchange_start(sums, *, name):
    n = len(sums)

    def body(*refs):
        for cp in _chip_exchange_copies(refs[:n], refs[n:2 * n], refs[2 * n], refs[2 * n + 1]):
            cp.start()
        refs[-1][...] = jnp.zeros_like(refs[-1])

    lands = [lax.empty((3,) + s.shape[1:], s.dtype) for s in sums]
    res = pl.pallas_call(
        body, name=name,
        out_shape=(pltpu.SemaphoreType.DMA((3 * n,)), pltpu.SemaphoreType.DMA((3 * n,)))
        + tuple(pltpu.HBM(s.shape, s.dtype) for s in sums) + tuple(pltpu.HBM(l.shape, l.dtype) for l in lands)
        + (_sds((8, 128), F32),),
        in_specs=[HBM] * (2 * n), out_specs=(SEM, SEM) + (HBM,) * (2 * n) + (pl.BlockSpec(memory_space=pltpu.VMEM),),
        input_output_aliases={i: 2 + i for i in range(2 * n)},
        compiler_params=pltpu.CompilerParams(has_side_effects=ORDERED_EFFECT))(
            *[_in_hbm(s) for s in sums], *[_in_hbm(l) for l in lands])
    return res[0], res[1], list(res[2:2 + n]), list(res[2 + n:2 + 2 * n]), res[-1]


def chip_exchange_wait(send_sems, recv_sems, sums, lands, after, *, name):
    n = len(sums)

    def body(*refs):
        for cp in _chip_exchange_copies(refs[:n], refs[n:2 * n], refs[2 * n], refs[2 * n + 1]):
            cp.wait_send()
            cp.wait_recv()

    res = pl.pallas_call(
        body, name=name,
        out_shape=tuple(pltpu.HBM(s.shape, s.dtype) for s in sums) + tuple(pltpu.HBM(l.shape, l.dtype) for l in lands),
        in_specs=[HBM] * (2 * n) + [SEM, SEM] + [ANY] * len(after), out_specs=(HBM,) * (2 * n),
        input_output_aliases={i: i for i in range(2 * n)},
        compiler_params=pltpu.CompilerParams(has_side_effects=ORDERED_EFFECT))(*sums, *lands, send_sems, recv_sems, *after)
    return list(res[:n]), list(res[n:])


def _pair_send_copies(h, got, send_sems, recv_sems):
    x, y, c, _ = _place()
    return [_remote(h[i], got[i], send_sems.at[i], recv_sems.at[i], (x, y, 1 - c)) for i in range(len(h))]


def pair_send_start(halves, *, name):
    n = len(halves)

    def body(*refs):
        for cp in _pair_send_copies(refs[:n], refs[n:2 * n], refs[2 * n], refs[2 * n + 1]):
            cp.start()
        refs[-1][...] = jnp.zeros_like(refs[-1])

    lands = [lax.empty(h.shape, h.dtype) for h in halves]
    res = pl.pallas_call(
        body, name=name,
        out_shape=(pltpu.SemaphoreType.DMA((n,)), pltpu.SemaphoreType.DMA((n,)))
        + tuple(pltpu.HBM(h.shape, h.dtype) for h in halves) * 2 + (_sds((8, 128), F32),),
        in_specs=[HBM] * (2 * n), out_specs=(SEM, SEM) + (HBM,) * (2 * n) + (pl.BlockSpec(memory_space=pltpu.VMEM),),
        input_output_aliases={i: 2 + i for i in range(2 * n)},
        compiler_params=pltpu.CompilerParams(has_side_effects=ORDERED_EFFECT))(
            *[_in_hbm(h) for h in halves], *[_in_hbm(l) for l in lands])
    return res[0], res[1], list(res[2:2 + n]), list(res[2 + n:2 + 2 * n]), res[-1]


def pair_send_wait(send_sems, recv_sems, halves, lands, after, *, name):
    n = len(halves)

    def body(*refs):
        for cp in _pair_send_copies(refs[:n], refs[n:2 * n], refs[2 * n], refs[2 * n + 1]):
            cp.wait_send()
            cp.wait_recv()

    res = pl.pallas_call(
        body, name=name, out_shape=tuple(pltpu.HBM(h.shape, h.dtype) for h in halves) * 2,
        in_specs=[HBM] * (2 * n) + [SEM, SEM] + [ANY] * len(after), out_specs=(HBM,) * (2 * n),
        input_output_aliases={i: i for i in range(2 * n)},
        compiler_params=pltpu.CompilerParams(has_side_effects=ORDERED_EFFECT))(*halves, *lands, send_sems, recv_sems, *after)
    return list(res[:n]), list(res[n:])


def allreduce_small(v, *, name):
    rows = v.shape[0]

    def body(v_ref, o_ref, gath, send_sems, recv_sems):
        x, y, c, _ = _place()
        me = 4 * x + 2 * y + c
        gath[me] = v_ref[...]
        copies = []
        for k in range(1, 8):
            fx, fy, fc = (k >> 2) & 1, (k >> 1) & 1, k & 1
            peer = (jnp.where(fx, 1 - x, x), jnp.where(fy, 1 - y, y), jnp.where(fc, 1 - c, c))
            cp = _remote(v_ref, gath.at[me], send_sems.at[k - 1], recv_sems.at[k - 1], peer)
            cp.start()
            copies.append(cp)
        for cp in copies:
            cp.wait()
        acc = gath[0]
        for d in range(1, 8):
            acc = acc + gath[d]
        o_ref[...] = acc

    return pl.pallas_call(
        body, name=name, out_shape=_sds(v.shape, F32),
        in_specs=[pl.BlockSpec(memory_space=pltpu.VMEM)], out_specs=pl.BlockSpec(memory_space=pltpu.VMEM),
        scratch_shapes=[pltpu.VMEM((8, rows, 128), F32), pltpu.SemaphoreType.DMA((7,)), pltpu.SemaphoreType.DMA((7,))])(v)


def _same_shape_runs(arrays):
    runs = {}
    for i, a in enumerate(arrays):
        runs.setdefault(a.shape, []).append(i)
    return list(runs.values())


def _per_shape(fn, *lists):
    out = [None] * len(lists[0])
    for idx in _same_shape_runs(lists[0]):
        for i, r in zip(idx, fn(*[[l[i] for i in idx] for l in lists])):
            out[i] = r
    return out


def add_halves(gs, bufs, cidx, *, name):
    cnt = len(gs)
    _, k, n = gs[0].shape

    def body(c_ref, *refs):
        g, b, o = refs[:cnt], refs[cnt:2 * cnt], refs[2 * cnt:]
        for i in range(cnt):
            o[i][...] = (g[i][...].astype(F32) + b[i][...].astype(F32)).astype(BF16)

    blk = pl.BlockSpec((None, k // 2, n), lambda s, c: (s, 0, 0))
    mine = pl.BlockSpec((None, k // 2, n), lambda s, c: (s, c[0], 0))
    return list(pl.pallas_call(
        body, name=name, out_shape=tuple(_sds(b.shape, BF16) for b in bufs),
        grid_spec=pltpu.PrefetchScalarGridSpec(
            num_scalar_prefetch=1, grid=(NSH,), in_specs=[mine] * cnt + [blk] * cnt, out_specs=tuple([blk] * cnt)),
        compiler_params=_params(("parallel",)))(cidx, *gs, *bufs))


def add_chips(sums, bufs, sidx, *, name):
    cnt = len(sums)
    _, kh, n = sums[0].shape

    def body(s_ref, *refs):
        mine, b, o = refs[:cnt], refs[cnt:2 * cnt], refs[2 * cnt:]
        for i in range(cnt):
            o[i][...] = ((mine[i][...].astype(F32) + b[i][0].astype(F32)) + (b[i][1].astype(F32) + b[i][2].astype(F32)))

    own = pl.BlockSpec((None, kh, n), lambda i, s: (s[0], 0, 0))
    got = pl.BlockSpec((3, kh, n), lambda i, s: (0, 0, 0))
    out = pl.BlockSpec((kh, n), lambda i, s: (0, 0))
    return list(pl.pallas_call(
        body, name=name, out_shape=tuple(_sds((kh, n), F32) for _ in sums),
        grid_spec=pltpu.PrefetchScalarGridSpec(
            num_scalar_prefetch=1, grid=(1,), in_specs=[own] * cnt + [got] * cnt, out_specs=tuple([out] * cnt)),
        compiler_params=_params(("arbitrary",)))(sidx, *sums, *bufs))


PARAMS = ("ffn1_norm", "ffn1_w_gate", "ffn1_w_up", "ffn1_w_down", "mix_norm", "w_in", "b_gate", "na_q_norm", "na_k_norm",
          "na_rpb", "sw_q_norm", "sw_k_norm", "sw_sink", "t5_rel_table", "w_branch_na", "w_branch_sw", "w_out", "ffn2_norm",
          "ffn2_w_gate", "ffn2_w_up", "ffn2_w_down")
SMALL_ALL = tuple(n for n in PARAMS if n not in BIG)
TRANSPOSED = ("ffn1_w_gate", "ffn1_w_up", "w_in", "ffn2_w_gate", "ffn2_w_up")
SMALL_ROWS = 152


def _pack_small(vals):
    flat = jnp.concatenate([vals[n].reshape(-1).astype(F32) for n in SMALL_ALL] + [vals["loss"].reshape(-1)])
    return jnp.pad(flat, (0, SMALL_ROWS * 128 - flat.shape[0])).reshape(SMALL_ROWS, 128)


def _unpack_small(packed, like):
    flat, out, off = packed.reshape(-1), {}, 0
    for n in SMALL_ALL:
        size = math.prod(like[n].shape)
        out[n] = flat[off:off + size].reshape(like[n].shape)
        off += size
    out["loss"] = flat[off]
    return out


def kernel(x, ffn1_norm, ffn1_w_gate, ffn1_w_up, ffn1_w_down, mix_norm, w_in, b_gate, na_q_norm, na_k_norm, na_rpb, sw_q_norm, sw_k_norm, sw_sink, t5_rel_table, w_branch_na, w_branch_sw, w_out, ffn2_norm, ffn2_w_gate, ffn2_w_up, ffn2_w_down, loss_target, m_ffn1_norm, m_ffn1_w_gate, m_ffn1_w_up, m_ffn1_w_down, m_mix_norm, m_w_in, m_b_gate, m_na_q_norm, m_na_k_norm, m_na_rpb, m_sw_q_norm, m_sw_k_norm, m_sw_sink, m_t5_rel_table, m_w_branch_na, m_w_branch_sw, m_w_out, m_ffn2_norm, m_ffn2_w_gate, m_ffn2_w_up, m_ffn2_w_down, v_ffn1_norm, v_ffn1_w_gate, v_ffn1_w_up, v_ffn1_w_down, v_mix_norm, v_w_in, v_b_gate, v_na_q_norm, v_na_k_norm, v_na_rpb, v_sw_q_norm, v_sw_k_norm, v_sw_sink, v_t5_rel_table, v_w_branch_na, v_w_branch_sw, v_w_out, v_ffn2_norm, v_ffn2_w_gate, v_ffn2_w_up, v_ffn2_w_down):
    args = locals()
    tr = lambda n, a: jnp.transpose(a, (0, 2, 1)) if n in TRANSPOSED else a
    w = {n: tr(n, args[n]) for n in PARAMS}
    m = {n: tr(n, args["m_" + n]) for n in PARAMS}
    v = {n: tr(n, args["v_" + n]) for n in PARAMS}
    cidx = lax.axis_index("c").astype(jnp.int32).reshape(1)
    sidx = (2 * lax.axis_index("x") + lax.axis_index("y")).astype(jnp.int32).reshape(1)

    small = [{n: w[n][l] for n in SMALL} for l in range(DEPTH)]
    order = ("ffn1", "mix", "ffn2")

    keys = [(l, g) for l in range(DEPTH) for g in order]
    local = lambda l, g: [w[n][l].astype(BF16) for n in GROUPS[g]]
    first = gather_start([local(*keys[0])], name="gather_start")
    rest = gather_start([local(*key) for key in keys[1:]], first[0][2][0], name="gather_start")
    in_flight = dict(zip(keys, first + rest))
    t5b = t5_bias(w["t5_rel_table"], name="t5_bias")
    for l in range(DEPTH):
        small[l]["na_bias"] = na_bias_table(small[l]["na_rpb"], name="na_bias_table")
    early = [t5b] + [small[l]["na_bias"] for l in range(DEPTH)] + [rest[0][2][0]]

    def weights_of(l):
        def get(group, after):
            send_sems, recv_sems, thru, lands = in_flight[(l, group)]
            after = [after] + (early if (l, group) == keys[0] else [])
            thru, lands = gather_wait(send_sems, recv_sems, thru, lands, after, name="gather_wait")
            return dict(zip(GROUPS[group], gather_finish(thru, lands, name="gather_finish")))
        return get

    h0, saved0 = layer_fwd(x[0], small[0], weights_of(0), t5b)
    (dy, dy_bf, loss_row), saved1 = layer_fwd(h0, small[1], weights_of(1), t5b, target=loss_target[0])

    crossing, tokens, pending = {}, [], []

    def ship(after, then=None):
        key, send_sems, recv_sems, grads, lands = pending.pop()
        grads, from_sibling = pair_exchange_wait(send_sems, recv_sems, grads, lands, after, name="pair_exchange_wait")
        sums = _per_shape(lambda gs, bs: add_halves(gs, bs, cidx, name="add_halves"), grads, from_sibling)
        if then is None:
            send_sems, recv_sems, sums, lands, token = chip_exchange_start(sums, name="chip_exchange_start")
            crossing[key] = (send_sems, recv_sems, sums, lands)
            return token
        crossing[key], pair, token = exchange_start(sums, then[1], name="exchange_start")
        pending.append((then[0],) + pair)
        return token

    def reduce_of(l):
        def emit(group, grads):
            grads = list(grads)
            if pending:
                token = ship([grads[0]], then=((l, group), grads))
            else:
                send_sems, recv_sems, grads, lands, token = pair_exchange_start(grads, name="pair_exchange_start")
                pending.append(((l, group), send_sems, recv_sems, grads, lands))
            tokens.append(token)
            return token
        return emit

    def finish(layer, after, filled=None):
        sent = {}
        for group in order:
            send_sems, recv_sems, sums, lands = crossing[(layer, group)]
            sums, got = chip_exchange_wait(send_sems, recv_sems, sums, lands, after, name="chip_exchange_wait")
            halves = _per_shape(lambda ss, bs: add_chips(ss, bs, sidx, name="add_chips"), sums, got)
            sent[group] = pair_send_start(halves, name="pair_send_start")
            after = [sent[group][4]]
        out = {}
        for group in order:
            send_sems, recv_sems, halves, lands, _ = sent[group]
            halves, theirs = pair_send_wait(send_sems, recv_sems, halves, lands, after, name="pair_send_wait")
            names = GROUPS[group]
            res = _per_shape(
                lambda ws, ms, vs, a, b, *f: adamw_layer(ws, ms, vs, a, b, cidx, layer, list(f[0]) if f else None, name="adamw_layer"),
                *([[w[n] for n in names], [m[n] for n in names], [v[n] for n in names], halves, theirs]
                  + ([[filled[n] for n in names]] if filled is not None else [])))
            out.update(zip(names, res))
            after = [res[-1][0]]
        return out

    dy, dy_bf, small1, dt5_1 = layer_bwd(dy, dy_bf, saved1, small[1], t5b, reduce_of(1))
    grad_x, _, small0, dt5_0 = layer_bwd(dy, dy_bf, saved0, small[0], t5b, reduce_of(0), dep=tokens[-1])
    done1 = finish(1, [ship([grad_x])])

    smalls = [small0, small1]
    dt5 = t5_table_grad(dt5_0, dt5_1, name="t5_table_grad").reshape(32, 8)
    local_small = {n: jnp.stack([smalls[l][n].reshape(w[n].shape[1:]) for l in range(DEPTH)]) for n in SMALL}
    local_small["t5_rel_table"] = dt5
    local_small["loss"] = loss_row[0, 0:1]
    total = allreduce_small(_pack_small(local_small), name="allreduce_small")
    small_grads = _unpack_small(total, w)
    small_done = adamw_small([w[n] for n in SMALL_ALL], [small_grads[n] for n in SMALL_ALL], [m[n] for n in SMALL_ALL],
                             [v[n] for n in SMALL_ALL], name="adamw_small")

    grad, delta, new_m, new_v = {}, {}, {}, {}
    for n, done in finish(0, [small_done[0][0], done1[BIG[-1]][0]], filled=done1).items():
        grad[n], delta[n], new_m[n], new_v[n] = done
    for n, done in zip(SMALL_ALL, small_done):
        grad[n] = small_grads[n]
        delta[n], new_m[n], new_v[n] = done

    return (small_grads["loss"], grad_x[None], *[tr(n, grad[n]) for n in PARAMS], *[tr(n, delta[n]) for n in PARAMS],
            *[tr(n, new_m[n]) for n in PARAMS], *[tr(n, new_v[n]) for n in PARAMS])
```

```python
import math

import jax
import jax.numpy as jnp
import numpy as np
from jax import lax
from jax.experimental import pallas as pl
from jax.experimental.pallas import tpu as pltpu

F32 = jnp.float32
BF16 = jnp.bfloat16

SEQ = 2048
DM = 1024
DFF = 2816
DEPTH = 2
NSH = 4
FSH = DFF // NSH
GRID_W = 64
ROWS = SEQ // GRID_W
NA_HEADS = 8
HD = 64
NA_WR = 8
NA_WC = 16
NA_KEYS = NA_WR * GRID_W
SW_BLK = 128
SW_NB = SEQ // SW_BLK
SW_KEYS = 3 * SW_BLK
ATT_W = 2304
GATE_W = 2048
IN_W = ATT_W + GATE_W
EPS = 1e-6
NEG = -1e30
QK_SCALE = 1.0 / math.sqrt(HD)

ADAM_LR = 0.001
ADAM_B1 = 0.9
ADAM_B2 = 0.999
ADAM_EPS = 1e-08
ADAM_WD = 0.01
ADAM_STEP = 10

VMEM_LIMIT = 56 << 20
MESH = pl.DeviceIdType.MESH

NT = (((1,), (1,)), ((), ()))
TN = (((0,), (0,)), ((), ()))
NN = (((1,), (0,)), ((), ()))


def _dot(a, b, dims=NN):
    return lax.dot_general(a, b, dims, preferred_element_type=F32)


def _params(sem=None):
    return pltpu.CompilerParams(dimension_semantics=sem, vmem_limit_bytes=VMEM_LIMIT)


def _sds(shape, dtype):
    return jax.ShapeDtypeStruct(shape, dtype)


def mm(a, b, *, name, ta=False, tb=False, out_dtype=F32, add=None, scale=None, tm=512, tn=None, tk=None, exact=False,
       dep=None, b_rows=None):
    m, kd = (a.shape[1], a.shape[0]) if ta else a.shape
    if b_rows is None:
        n = b.shape[0] if tb else b.shape[1]
    else:
        n = b_rows[1] if tb else b.shape[1]
        assert tb or (b_rows[1] == kd and (tk or kd) == kd)
    tm, tn, tk = min(tm, m), min(tn or n, n), min(tk or kd, kd)
    nk = kd // tk
    dims = (((0 if ta else 1,), (1 if tb else 0,)), ((), ()))

    def body(*refs):
        a_ref, b_ref = refs[:2]
        add_ref = refs[2] if add is not None else None
        o_ref = refs[-1] if nk == 1 else refs[-2]
        if b_rows is None:
            bv = b_ref[...]
        elif tb:
            bv = b_ref[pl.ds(pl.multiple_of(b_rows[0] + pl.program_id(1) * tn, 16), tn), :]
        else:
            bv = b_ref[b_rows[0]:b_rows[0] + b_rows[1], :]
        if exact:
            part = lax.dot_general(a_ref[...], bv, dims, precision=lax.Precision.HIGHEST, preferred_element_type=F32)
        else:
            part = lax.dot_general(a_ref[...].astype(BF16), bv.astype(BF16), dims, preferred_element_type=F32)

        def finish(r):
            if scale is not None:
                r = r * scale
            if add is not None:
                r = r + add_ref[...]
            o_ref[...] = r.astype(out_dtype)

        if nk == 1:
            finish(part)
        else:
            acc, k = refs[-1], pl.program_id(2)

            @pl.when(k == 0)
            def _():
                acc[...] = part

            @pl.when(k != 0)
            def _():
                acc[...] += part

            pl.when(k == nk - 1)(lambda: finish(acc[...]))

    a_spec = pl.BlockSpec((tk, tm), lambda i, j, k: (k, i)) if ta else pl.BlockSpec((tm, tk), lambda i, j, k: (i, k))
    if b_rows is not None:
        b_spec = pl.BlockSpec(b.shape, lambda i, j, k: (0, 0), pipeline_mode=pl.Buffered(1))
    else:
        b_spec = pl.BlockSpec((tn, tk), lambda i, j, k: (j, k)) if tb else pl.BlockSpec((tk, tn), lambda i, j, k: (k, j))
    o_spec = pl.BlockSpec((tm, tn), lambda i, j, k: (i, j))
    ins, specs = [a, b], [a_spec, b_spec]
    if add is not None:
        ins.append(add)
        specs.append(o_spec)
    if dep is not None:
        ins.append(dep)
        specs.append(pl.BlockSpec(memory_space=pl.ANY))
    return pl.pallas_call(
        body, name=name, out_shape=_sds((m, n), out_dtype), grid=(m // tm, n // tn, nk), in_specs=specs,
        out_specs=o_spec, scratch_shapes=[] if nk == 1 else [pltpu.VMEM((tm, tn), F32)],
        compiler_params=_params(("parallel", "parallel", "arbitrary")))(*ins)


def _rms(x):
    return lax.rsqrt(jnp.mean(x * x, axis=-1, keepdims=True) + EPS)


def mixer_input_fwd(x, gain, w_in_t, gq_na, gk_na, gq_sw, gk_sw, *, name, tm=512):
    def body(x_ref, g_ref, w_ref, gqa_ref, gka_ref, gqs_ref, gks_ref, bd_ref, bd2_ref,
             h_ref, z_ref, zg_ref, qa_ref, ka_ref, va_ref, qs_ref, kv_ref):
        x = x_ref[...]
        h = (x * _rms(x) * g_ref[...]).astype(BF16)
        h_ref[...] = h
        z = _dot(h, w_ref[0:ATT_W, :], NT).astype(BF16)
        z_ref[...] = z
        zg_ref[...] = _dot(h, w_ref[ATT_W:IN_W, :], NT).astype(BF16)
        bd = bd_ref[...]

        def norm(v, g, bdm):
            v = v.astype(F32)
            return v * lax.rsqrt(_group_mean(v * v, bdm) + EPS) * g

        qa_ref[...] = (norm(z[:, 0:512], gqa_ref[...], bd) * QK_SCALE).astype(BF16)
        ka_ref[...] = norm(z[:, 512:1024], gka_ref[...], bd).astype(BF16)
        va_ref[...] = z[:, 1024:1536]
        qs_ref[...] = (norm(z[:, 1536:2048], gqs_ref[...], bd) * QK_SCALE).astype(BF16)
        kv_ref[:, 0:128] = norm(z[:, 2048:2176], gks_ref[...], bd2_ref[...]).astype(BF16)
        kv_ref[:, 128:256] = z[:, 2176:2304]

    s = x.shape[0]
    tile = pl.BlockSpec((tm, DM), lambda i: (i, 0))
    vec = lambda w: pl.BlockSpec((1, w), lambda i: (0, 0))
    att = pl.BlockSpec((tm, 512), lambda i: (i, 0))
    g512 = lambda g: jnp.tile(g.reshape(1, HD), (1, 8))
    q = _sds((s, 512), BF16)
    return pl.pallas_call(
        body, name=name,
        out_shape=(_sds((s, DM), BF16), _sds((s, ATT_W), BF16), _sds((s, GATE_W), BF16), q, q, q, q, _sds((s, 256), BF16)),
        grid=(s // tm,),
        in_specs=[tile, vec(DM), pl.BlockSpec((IN_W, DM), lambda i: (0, 0), pipeline_mode=pl.Buffered(1)),
                  vec(512), vec(512), vec(512), vec(128), pl.BlockSpec((512, 512), lambda i: (0, 0)),
                  pl.BlockSpec((128, 128), lambda i: (0, 0))],
        out_specs=(tile, pl.BlockSpec((tm, ATT_W), lambda i: (i, 0)), pl.BlockSpec((tm, GATE_W), lambda i: (i, 0)),
                   att, att, att, att, pl.BlockSpec((tm, 256), lambda i: (i, 0))),
        compiler_params=_params(("parallel",)))(
            x, gain, w_in_t, g512(gq_na), g512(gk_na), g512(gq_sw), jnp.tile(gk_sw.reshape(1, HD), (1, 2)),
            _block_diag(512), _block_diag(128))


def _rms_bwd_math(dh, x, gain):
    r = _rms(x)
    xh = x * r
    dgain = jnp.sum(dh * xh, axis=0, keepdims=True)
    dxn = dh * gain
    dx = r * (dxn - xh * jnp.mean(dxn * xh, axis=-1, keepdims=True))
    return dx, dgain


def mixer_input_bwd(dz, dzg, w_in_t, x, gain, dres, *, name, tm=512, dep=None):
    def body(dz_ref, dzg_ref, w_ref, x_ref, g_ref, dres_ref, *rest):
        dx_ref, dxb_ref, dg_ref = rest[-3:]

        @pl.when(pl.program_id(0) == 0)
        def _():
            dg_ref[...] = jnp.zeros_like(dg_ref)

        dh = _dot(dz_ref[...], w_ref[0:ATT_W, :]) + _dot(dzg_ref[...], w_ref[ATT_W:IN_W, :])
        dx, dg = _rms_bwd_math(dh, x_ref[...], g_ref[...])
        dx = dres_ref[...] + dx
        dx_ref[...] = dx
        dxb_ref[...] = dx.astype(BF16)
        dg_ref[...] += dg

    s = x.shape[0]
    tile = pl.BlockSpec((tm, DM), lambda i: (i, 0))
    vec = pl.BlockSpec((1, DM), lambda i: (0, 0))
    ins, specs = _with_dep(
        [dz, dzg, w_in_t, x, gain, dres],
        [pl.BlockSpec((tm, ATT_W), lambda i: (i, 0)), pl.BlockSpec((tm, GATE_W), lambda i: (i, 0)),
         pl.BlockSpec((IN_W, DM), lambda i: (0, 0), pipeline_mode=pl.Buffered(1)), tile, vec, tile], dep)
    return pl.pallas_call(
        body, name=name, out_shape=(_sds((s, DM), F32), _sds((s, DM), BF16), _sds((1, DM), F32)), grid=(s // tm,),
        in_specs=specs, out_specs=(tile, tile, vec), compiler_params=_params(("arbitrary",)))(*ins)


def mixer_input_dw(dz, dzg, h, *, name, tr=256):
    att_tiles = ATT_W // tr

    def body(dz_ref, dzg_ref, h_ref, o_ref):
        t = pl.program_id(0)
        cols = jnp.where(t < att_tiles, dz_ref[...], dzg_ref[...])
        o_ref[...] = _dot(cols, h_ref[...], TN).astype(BF16)

    s = h.shape[0]
    return pl.pallas_call(
        body, name=name, out_shape=_sds((IN_W, DM), BF16), grid=(IN_W // tr,),
        in_specs=[pl.BlockSpec((s, tr), lambda t: (0, jnp.minimum(t, att_tiles - 1))),
                  pl.BlockSpec((s, tr), lambda t: (0, jnp.maximum(t - att_tiles, 0))),
                  pl.BlockSpec((s, DM), lambda t: (0, 0))],
        out_specs=pl.BlockSpec((tr, DM), lambda t: (t, 0)), compiler_params=_params(("parallel",)))(dz, dzg, h)


def mixer_output_dw(merged, dy, o_na, dpa, o_sw, dps, *, name):
    def body(m_ref, dy_ref, ona_ref, dpa_ref, osw_ref, dps_ref, gwo_ref, gna_ref, gsw_ref):
        gwo_ref[...] = _dot(m_ref[...], dy_ref[...], TN).astype(BF16)
        width = DM // NSH
        for o_ref, dp_ref, out_ref in ((ona_ref, dpa_ref, gna_ref), (osw_ref, dps_ref, gsw_ref)):
            full = _dot(o_ref[...], dp_ref[...], TN).astype(BF16)
            for j in range(NSH):
                out_ref[j] = full[:, j * width:(j + 1) * width]

    branch = _sds((NSH, 512, DM // NSH), BF16)
    return pl.pallas_call(
        body, name=name, out_shape=(_sds((DM, DM), BF16), branch, branch), compiler_params=_params())(
            merged, dy, o_na, dpa, o_sw, dps)


def _with_dep(ins, specs, dep):
    if dep is None:
        return ins, specs
    return ins + [dep], specs + [pl.BlockSpec(memory_space=pl.ANY)]


def _resident_weight():
    return pl.BlockSpec((DFF, DM), lambda i: (0, 0), pipeline_mode=pl.Buffered(1))


def ffn_fwd(x, gain, wg, wu, wd, target=None, *, name, tm=512):
    def body(x_ref, g_ref, wg_ref, wu_ref, wd_ref, *rest):
        h_ref, gg_ref, uu_ref = rest[-3:]
        x = x_ref[...]
        h = (x * _rms(x) * g_ref[...]).astype(BF16)
        h_ref[...] = h
        gg = _dot(h, wg_ref[...], NT)
        uu = _dot(h, wu_ref[...], NT)
        gg_ref[...] = gg.astype(BF16)
        uu_ref[...] = uu.astype(BF16)
        act = (gg * jax.nn.sigmoid(gg) * uu).astype(BF16)
        y = x + 0.5 * _dot(act, wd_ref[...])
        if target is None:
            rest[0][...] = y
            return
        t_ref, dy_ref, dyb_ref, l_ref = rest[:4]

        @pl.when(pl.program_id(0) == 0)
        def _():
            l_ref[...] = jnp.zeros_like(l_ref)

        err = y - t_ref[...]
        dy = err * (1.0 / DM)
        dy_ref[...] = dy
        dyb_ref[...] = dy.astype(BF16)
        l_ref[...] += 0.5 * jnp.sum(jnp.mean(err * err, axis=-1, keepdims=True), axis=0, keepdims=True)

    s = x.shape[0]
    tile = pl.BlockSpec((tm, DM), lambda i: (i, 0))
    hid = pl.BlockSpec((tm, DFF), lambda i: (i, 0))
    w = _resident_weight()
    saved_shapes = (_sds((s, DM), BF16), _sds((s, DFF), BF16), _sds((s, DFF), BF16))
    ins, specs = [x, gain, wg, wu, wd], [tile, pl.BlockSpec((1, DM), lambda i: (0, 0)), w, w, w]
    if target is None:
        head_shapes, head_specs = (_sds((s, DM), F32),), (tile,)
    else:
        ins, specs = ins + [target], specs + [tile]
        head_shapes = (_sds((s, DM), F32), _sds((s, DM), BF16), _sds((1, 128), F32))
        head_specs = (tile, tile, pl.BlockSpec((1, 128), lambda i: (0, 0)))
    return pl.pallas_call(
        body, name=name, out_shape=head_shapes + saved_shapes, grid=(s // tm,), in_specs=specs,
        out_specs=head_specs + (tile, hid, hid),
        compiler_params=_params(("parallel",) if target is None else ("arbitrary",)))(*ins)


def ffn_bwd_tokens(dy, x, gain, gg, uu, wg, wu, wd, *, name, tm=256, dep=None):
    def body(dy_ref, x_ref, g_ref, gg_ref, uu_ref, wg_ref, wu_ref, wd_ref, *rest):
        dx_ref, dxb_ref, dgain_ref, act_ref, dg_ref, du_ref = rest[-6:]

        @pl.when(pl.program_id(0) == 0)
        def _():
            dgain_ref[...] = jnp.zeros_like(dgain_ref)

        dy = dy_ref[...]
        dact = _dot((0.5 * dy).astype(BF16), wd_ref[...], NT)
        g = gg_ref[...].astype(F32)
        u = uu_ref[...].astype(F32)
        sg = jax.nn.sigmoid(g)
        silu = g * sg
        act_ref[...] = (silu * u).astype(BF16)
        dg = (dact * u * (sg * (1.0 + g * (1.0 - sg)))).astype(BF16)
        du = (dact * silu).astype(BF16)
        dg_ref[...] = dg
        du_ref[...] = du
        dx, dgain = _rms_bwd_math(_dot(dg, wg_ref[...]) + _dot(du, wu_ref[...]), x_ref[...], g_ref[...])
        dx = dy + dx
        dx_ref[...] = dx
        dxb_ref[...] = dx.astype(BF16)
        dgain_ref[...] += dgain

    s = x.shape[0]
    tile = pl.BlockSpec((tm, DM), lambda i: (i, 0))
    vec = pl.BlockSpec((1, DM), lambda i: (0, 0))
    hid = pl.BlockSpec((tm, DFF), lambda i: (i, 0))
    hshape = _sds((s, DFF), BF16)
    w = _resident_weight()
    ins, specs = _with_dep([dy, x, gain, gg, uu, wg, wu, wd], [tile, tile, vec, hid, hid, w, w, w], dep)
    return pl.pallas_call(
        body, name=name, out_shape=(_sds((s, DM), F32), _sds((s, DM), BF16), _sds((1, DM), F32), hshape, hshape, hshape),
        grid=(s // tm,), in_specs=specs, out_specs=(tile, tile, vec, hid, hid, hid),
        compiler_params=_params(("arbitrary",)))(*ins)


def ffn_bwd_weights(h, dy, act, dg, du, *, name, tf=256):
    def body(h_ref, dy_ref, act_ref, dg_ref, du_ref, gwg_ref, gwu_ref, gwd_ref):
        h = h_ref[...]
        gwg_ref[...] = _dot(dg_ref[...], h, TN).astype(BF16)
        gwu_ref[...] = _dot(du_ref[...], h, TN).astype(BF16)
        gwd_ref[...] = (0.5 * _dot(act_ref[...], dy_ref[...], TN)).astype(BF16)

    s = h.shape[0]
    full = pl.BlockSpec((s, DM), lambda f: (0, 0))
    hid = pl.BlockSpec((s, tf), lambda f: (0, f))
    wt = pl.BlockSpec((tf, DM), lambda f: (f, 0))
    wshape = _sds((DFF, DM), BF16)
    return pl.pallas_call(
        body, name=name, out_shape=(wshape, wshape, wshape), grid=(DFF // tf,), in_specs=[full, full, hid, hid, hid],
        out_specs=(wt, wt, wt), compiler_params=_params(("parallel",)))(h, dy, act, dg, du)


def _group_mean(v, bd):
    hi = v.astype(BF16)
    lo = (v - hi.astype(F32)).astype(BF16)
    return _dot(hi, bd) + _dot(lo, bd)


def _block_diag(width):
    idx = np.arange(width) // HD
    return jnp.asarray((idx[:, None] == idx[None, :]).astype(np.float32) / HD, dtype=BF16)


def qknorm_bwd(z, dqa, dka, dva, dqs, dkv, gq_na, gk_na, gq_sw, gk_sw, *, name, tm=256):
    def body(zq_ref, zk_ref, zs_ref, zkv_ref, dqa_ref, dka_ref, dva_ref, dqs_ref, dkv_ref, gqa_ref, gka_ref, gqs_ref,
             gks_ref, bd_ref, bd2_ref, dz_ref, dgqa_ref, dgka_ref, dgqs_ref, dgks_ref):
        @pl.when(pl.program_id(0) == 0)
        def _():
            dgqa_ref[...] = jnp.zeros_like(dgqa_ref)
            dgka_ref[...] = jnp.zeros_like(dgka_ref)
            dgqs_ref[...] = jnp.zeros_like(dgqs_ref)
            dgks_ref[...] = jnp.zeros_like(dgks_ref)

        bd = bd_ref[...]

        def bwd(x, dy, g, bdm, dg_ref):
            x = x.astype(F32)
            r = lax.rsqrt(_group_mean(x * x, bdm) + EPS)
            xh = x * r
            dg_ref[...] += jnp.sum(dy * xh, axis=0, keepdims=True)
            dxn = dy * g
            return r * (dxn - xh * _group_mean(dxn * xh, bdm))

        dz_ref[:, 0:512] = bwd(zq_ref[...], dqa_ref[...] * QK_SCALE, gqa_ref[...], bd, dgqa_ref).astype(BF16)
        dz_ref[:, 512:1024] = bwd(zk_ref[...], dka_ref[...], gka_ref[...], bd, dgka_ref).astype(BF16)
        dz_ref[:, 1024:1536] = dva_ref[...].astype(BF16)
        dz_ref[:, 1536:2048] = bwd(zs_ref[...], dqs_ref[...] * QK_SCALE, gqs_ref[...], bd, dgqs_ref).astype(BF16)
        dkv = dkv_ref[...]
        dz_ref[:, 2048:2176] = bwd(zkv_ref[:, 0:128], dkv[:, 0:128], gks_ref[...], bd2_ref[...], dgks_ref).astype(BF16)
        dz_ref[:, 2176:2304] = dkv[:, 128:256].astype(BF16)

    s = z.shape[0]
    col = lambda j: pl.BlockSpec((tm, 512), lambda i, j=j: (i, j))
    t512 = pl.BlockSpec((tm, 512), lambda i: (i, 0))
    t256 = pl.BlockSpec((tm, 256), lambda i: (i, 0))
    vec = lambda w: pl.BlockSpec((1, w), lambda i: (0, 0))
    g512 = lambda g: jnp.tile(g.reshape(1, HD), (1, 8))
    return pl.pallas_call(
        body, name=name,
        out_shape=(_sds((s, ATT_W), BF16), _sds((1, 512), F32), _sds((1, 512), F32), _sds((1, 512), F32), _sds((1, 128), F32)),
        grid=(s // tm,),
        in_specs=[col(0), col(1), col(3), pl.BlockSpec((tm, 256), lambda i: (i, 8)), t512, t512, t512, t512, t256,
                  vec(512), vec(512), vec(512), vec(128), pl.BlockSpec((512, 512), lambda i: (0, 0)),
                  pl.BlockSpec((128, 128), lambda i: (0, 0))],
        out_specs=(pl.BlockSpec((tm, ATT_W), lambda i: (i, 0)), vec(512), vec(512), vec(512), vec(128)),
        compiler_params=_params(("arbitrary",)))(
            z, z, z, z, dqa, dka, dva, dqs, dkv, g512(gq_na), g512(gk_na), g512(gq_sw),
            jnp.tile(gk_sw.reshape(1, HD), (1, 2)), _block_diag(512), _block_diag(128))


def _na_row_start(r):
    return jnp.clip(r - NA_WR // 2, 0, ROWS - NA_WR)


def na_bias_table(rpb, *, name):
    t = jnp.pad(rpb, ((0, 0), (0, 2), (0, HD - (2 * NA_WC - 1))))
    pairs = jnp.concatenate([t[:, :16], t[:, 1:17]], axis=-1).reshape(NA_HEADS, 16, 1, 128)

    def body(t_ref, o_ref):
        p = pl.program_id(0)
        q = lax.broadcasted_iota(jnp.int32, (GRID_W, 128), 0)
        kc = lax.broadcasted_iota(jnp.int32, (GRID_W, 128), 1) & (GRID_W - 1)
        cs = jnp.clip(q - NA_WC // 2, 0, GRID_W - NA_WC)
        ok = (kc >= cs) & (kc < cs + NA_WC)
        for h in range(NA_HEADS):
            for pr in range(NA_WR // 2):
                x = jnp.broadcast_to(t_ref[h, 2 * pr - p + NA_WR - 1], (GRID_W, 128))
                b = pltpu.roll(x, 128 - (NA_WC - 1), 1, stride=1, stride_axis=0)
                o_ref[h, :, 128 * pr:128 * pr + 128] = jnp.where(ok, b, NEG)

    return pl.pallas_call(
        body, name=name, out_shape=_sds((NA_WR, NA_HEADS, GRID_W, NA_KEYS), F32), grid=(NA_WR,),
        in_specs=[pl.BlockSpec((NA_HEADS, 16, 1, 128), lambda p: (0, 0, 0, 0))],
        out_specs=pl.BlockSpec((None, NA_HEADS, GRID_W, NA_KEYS), lambda p: (p, 0, 0, 0)),
        compiler_params=_params(("parallel",)))(pairs)


def _lane_halves():
    lane = lax.broadcasted_iota(jnp.int32, (1, 128), 1)
    return lane < HD


def na_fwd(q, k, v, bias, *, name):
    def body(q_ref, k_ref, v_ref, b_ref, o_ref, lse_ref):
        r = pl.program_id(0)
        off = pl.multiple_of(_na_row_start(r) * GRID_W, GRID_W)
        first = _lane_halves()
        sels = [first, jnp.logical_not(first)]
        lanes = [slice(128 * j, 128 * j + 128) for j in range(NA_HEADS // 2)]
        q2s = [q_ref[:, l] for l in lanes]
        k2s = [k_ref[pl.ds(off, NA_KEYS), l] for l in lanes]
        v2s = [v_ref[pl.ds(off, NA_KEYS), l] for l in lanes]
        scores = []
        for h in range(NA_HEADS):
            j, half = divmod(h, 2)
            scores.append(_dot(jnp.where(sels[half], q2s[j], jnp.zeros_like(q2s[j])), k2s[j], NT))
        probs, lses = [], []
        for h in range(NA_HEADS):
            b = b_ref[h]
            s = jnp.where(b > 0.5 * NEG, scores[h] + b, NEG)
            m = jnp.max(s, axis=-1, keepdims=True)
            e = jnp.exp(s - m)
            l = jnp.sum(e, axis=-1, keepdims=True)
            probs.append((e / l).astype(BF16))
            lses.append(m + jnp.log(l))
        for j in range(NA_HEADS // 2):
            zero = jnp.zeros_like(v2s[j])
            o2 = (_dot(probs[2 * j], jnp.where(sels[0], v2s[j], zero))
                  + _dot(probs[2 * j + 1], jnp.where(sels[1], v2s[j], zero)))
            o_ref[:, lanes[j]] = o2.astype(BF16)
        lse_ref[...] = jnp.concatenate(lses, axis=1)

    s_tok = q.shape[0]
    full = pl.BlockSpec((s_tok, 512), lambda r: (0, 0))
    return pl.pallas_call(
        body, name=name, out_shape=(_sds((s_tok, 512), BF16), _sds((s_tok, NA_HEADS), F32)), grid=(ROWS,),
        in_specs=[pl.BlockSpec((GRID_W, 512), lambda r: (r, 0)), full, full,
                  pl.BlockSpec((None, NA_HEADS, GRID_W, NA_KEYS), lambda r: (r - _na_row_start(r), 0, 0, 0))],
        out_specs=(pl.BlockSpec((GRID_W, 512), lambda r: (r, 0)), pl.BlockSpec((GRID_W, NA_HEADS), lambda r: (r, 0))),
        compiler_params=_params(("parallel",)))(q, k, v, bias)


def na_bwd(q, k, v, o, do, lse, bias, *, name):
    def body(q_ref, k_ref, v_ref, o_ref, do_ref, lse_ref, b_ref, dq_ref, dk_ref, dv_ref, db_ref):
        r = pl.program_id(0)

        @pl.when(r == 0)
        def _():
            dk_ref[...] = jnp.zeros_like(dk_ref)
            dv_ref[...] = jnp.zeros_like(dv_ref)

        @pl.when((r <= NA_WR // 2) | (r > ROWS - NA_WR // 2))
        def _():
            db_ref[...] = jnp.zeros_like(db_ref)

        off = pl.multiple_of(_na_row_start(r) * GRID_W, GRID_W)
        first = _lane_halves()
        sels = [first, jnp.logical_not(first)]
        lanes = [slice(128 * j, 128 * j + 128) for j in range(NA_HEADS // 2)]
        q2s = [q_ref[:, l] for l in lanes]
        k2s = [k_ref[pl.ds(off, NA_KEYS), l] for l in lanes]
        v2s = [v_ref[pl.ds(off, NA_KEYS), l] for l in lanes]
        do2s = [do_ref[:, l] for l in lanes]
        prods = [do2s[j].astype(F32) * o_ref[:, lanes[j]].astype(F32) for j in range(NA_HEADS // 2)]
        lse = lse_ref[...]
        qhs, dohs, scores, dps = [], [], [], []
        for h in range(NA_HEADS):
            j, half = divmod(h, 2)
            qhs.append(jnp.where(sels[half], q2s[j], jnp.zeros_like(q2s[j])))
            dohs.append(jnp.where(sels[half], do2s[j], jnp.zeros_like(do2s[j])))
            scores.append(_dot(qhs[h], k2s[j], NT))
            dps.append(_dot(dohs[h], v2s[j], NT))
        pbs, dsbs = [], []
        for h in range(NA_HEADS):
            j, half = divmod(h, 2)
            b = b_ref[h]
            s = jnp.where(b > 0.5 * NEG, scores[h] + b, NEG)
            p = jnp.exp(s - lse[:, h:h + 1])
            delta = jnp.sum(jnp.where(sels[half], prods[j], 0.0), axis=-1, keepdims=True)
            ds = p * (dps[h] - delta)
            db_ref[h] += ds
            pbs.append(p.astype(BF16))
            dsbs.append(ds.astype(BF16))
        for j in range(NA_HEADS // 2):
            a, b = 2 * j, 2 * j + 1
            zero = jnp.zeros_like(k2s[j])
            dq_ref[:, lanes[j]] = (_dot(dsbs[a], jnp.where(sels[0], k2s[j], zero))
                                   + _dot(dsbs[b], jnp.where(sels[1], k2s[j], zero)))
            dk_ref[pl.ds(off, NA_KEYS), lanes[j]] += _dot(dsbs[a], qhs[a], TN) + _dot(dsbs[b], qhs[b], TN)
            dv_ref[pl.ds(off, NA_KEYS), lanes[j]] += _dot(pbs[a], dohs[a], TN) + _dot(pbs[b], dohs[b], TN)

    s_tok = q.shape[0]
    full = pl.BlockSpec((s_tok, 512), lambda r: (0, 0))
    row = pl.BlockSpec((GRID_W, 512), lambda r: (r, 0))
    bias_spec = pl.BlockSpec((None, NA_HEADS, GRID_W, NA_KEYS), lambda r: (r - _na_row_start(r), 0, 0, 0))
    return pl.pallas_call(
        body, name=name,
        out_shape=(_sds((s_tok, 512), F32), _sds((s_tok, 512), F32), _sds((s_tok, 512), F32),
                   _sds((NA_WR, NA_HEADS, GRID_W, NA_KEYS), F32)),
        grid=(ROWS,),
        in_specs=[row, full, full, row, row, pl.BlockSpec((GRID_W, NA_HEADS), lambda r: (r, 0)), bias_spec],
        out_specs=(row, full, full, bias_spec), compiler_params=_params(("arbitrary",)))(q, k, v, o, do, lse, bias)


def t5_bucket_map():
    rel = np.arange(SW_KEYS)[None, :] - SW_BLK - np.arange(SW_BLK)[:, None]
    nb = 16
    max_exact = nb // 2
    n = np.abs(rel)
    large = max_exact + (np.log(np.maximum(n, 1) / max_exact) / np.log(128 / max_exact) * (nb - max_exact)).astype(np.int32)
    large = np.minimum(large, nb - 1)
    return ((rel > 0) * nb + np.where(n < max_exact, n, large)).astype(np.int32)


def t5_bias(table, *, name):
    rel = np.arange(-SW_BLK, SW_BLK + 1)
    nb, max_exact = 16, 8
    n = np.abs(rel)
    large = max_exact + (np.log(np.maximum(n, 1) / max_exact) / np.log(128 / max_exact) * (nb - max_exact)).astype(np.int32)
    bucket = ((rel > 0) * nb + np.where(n < max_exact, n, np.minimum(large, nb - 1))).astype(np.int32)
    u = jnp.pad(table[jnp.asarray(bucket)].T, ((0, 0), (0, SW_KEYS - bucket.shape[0]))).reshape(8, 1, SW_KEYS)

    def body(u_ref, o_ref):
        for h in range(8):
            x = jnp.broadcast_to(u_ref[h], (SW_BLK, SW_KEYS))
            o_ref[h] = pltpu.roll(x, 0, 1, stride=1, stride_axis=0)

    return pl.pallas_call(body, name=name, out_shape=_sds((8, SW_BLK, SW_KEYS), F32), compiler_params=_params())(u)


def _sw_valid(n):
    a = lax.broadcasted_iota(jnp.int32, (SW_BLK, SW_KEYS), 0)
    j = lax.broadcasted_iota(jnp.int32, (SW_BLK, SW_KEYS), 1)
    kpos = (n - 1) * SW_BLK + j
    return (jnp.abs(j - SW_BLK - a) <= SW_BLK) & (kpos >= 0) & (kpos < SEQ)


def _dup_group(x2, g, first):
    rolled = pltpu.roll(x2, HD, 1)
    return jnp.where(first, x2, rolled) if g == 0 else jnp.where(first, rolled, x2)


def sw_fwd(q, kv, t5, sink, *, name):
    def body(q_ref, kv_ref, t5_ref, sink_ref, o_ref, lse_ref):
        n = pl.program_id(0)
        off = pl.multiple_of(n * SW_BLK, SW_BLK)
        first = _lane_halves()
        sels = [first, jnp.logical_not(first)]
        valid = _sw_valid(n)
        k2 = kv_ref[pl.ds(off, SW_KEYS), 0:128]
        v2 = kv_ref[pl.ds(off, SW_KEYS), 128:256]
        kk = [_dup_group(k2, g, first) for g in range(2)]
        vv = [_dup_group(v2, g, first) for g in range(2)]
        q2s = [q_ref[:, 128 * j:128 * j + 128] for j in range(4)]
        scores = []
        for h in range(8):
            j, half = divmod(h, 2)
            scores.append(_dot(jnp.where(sels[half], q2s[j], jnp.zeros_like(q2s[j])), kk[j // 2], NT))
        probs, lses = [], []
        for h in range(8):
            s = jnp.where(valid, scores[h] + t5_ref[h], NEG)
            snk = sink_ref[h]
            m = jnp.maximum(jnp.max(s, axis=-1, keepdims=True), snk)
            e = jnp.exp(s - m)
            den = jnp.sum(e, axis=-1, keepdims=True) + jnp.exp(snk - m)
            probs.append((e / den).astype(BF16))
            lses.append(m + jnp.log(den))
        outs = []
        for j in range(4):
            vg = vv[j // 2]
            zero = jnp.zeros_like(vg)
            outs.append(_dot(probs[2 * j], jnp.where(sels[0], vg, zero)) + _dot(probs[2 * j + 1], jnp.where(sels[1], vg, zero)))
        o_ref[...] = jnp.concatenate(outs, axis=1).astype(BF16)
        lse_ref[...] = jnp.concatenate(lses, axis=1)

    s_tok = q.shape[0]
    blk = pl.BlockSpec((SW_BLK, 512), lambda n: (n, 0))
    return pl.pallas_call(
        body, name=name, out_shape=(_sds((s_tok, 512), BF16), _sds((s_tok, 8), F32)), grid=(SW_NB,),
        in_specs=[blk, pl.BlockSpec(kv.shape, lambda n: (0, 0)), pl.BlockSpec((8, SW_BLK, SW_KEYS), lambda n: (0, 0, 0)),
                  pl.BlockSpec(memory_space=pltpu.SMEM)],
        out_specs=(blk, pl.BlockSpec((SW_BLK, 8), lambda n: (n, 0))), compiler_params=_params(("parallel",)))(q, kv, t5, sink)


def sw_bwd(q, kv, o, do, lse, t5, sink, *, name):
    def body(q_ref, kv_ref, o_ref, do_ref, lse_ref, t5_ref, sink_ref, dq_ref, dkv_ref, dt5_ref, dsink_ref):
        n = pl.program_id(0)

        @pl.when(n == 0)
        def _():
            dkv_ref[...] = jnp.zeros_like(dkv_ref)
            dt5_ref[...] = jnp.zeros_like(dt5_ref)
            dsink_ref[...] = jnp.zeros_like(dsink_ref)

        off = pl.multiple_of(n * SW_BLK, SW_BLK)
        first = _lane_halves()
        sels = [first, jnp.logical_not(first)]
        valid = _sw_valid(n)
        k2 = kv_ref[pl.ds(off, SW_KEYS), 0:128]
        v2 = kv_ref[pl.ds(off, SW_KEYS), 128:256]
        kk = [_dup_group(k2, g, first) for g in range(2)]
        vv = [_dup_group(v2, g, first) for g in range(2)]
        lanes = [slice(128 * j, 128 * j + 128) for j in range(4)]
        q2s = [q_ref[:, l] for l in lanes]
        do2s = [do_ref[:, l] for l in lanes]
        prods = [do2s[j].astype(F32) * o_ref[:, lanes[j]].astype(F32) for j in range(4)]
        lse = lse_ref[...]
        qhs, dohs, scores, dps = [], [], [], []
        for h in range(8):
            j, half = divmod(h, 2)
            qhs.append(jnp.where(sels[half], q2s[j], jnp.zeros_like(q2s[j])))
            dohs.append(jnp.where(sels[half], do2s[j], jnp.zeros_like(do2s[j])))
            scores.append(_dot(qhs[h], kk[j // 2], NT))
            dps.append(_dot(dohs[h], vv[j // 2], NT))
        pbs, dsbs, dss, dsinks = [], [], [], []
        for h in range(8):
            j, half = divmod(h, 2)
            s = jnp.where(valid, scores[h] + t5_ref[h], NEG)
            lse_h = lse[:, h:h + 1]
            p = jnp.exp(s - lse_h)
            delta = jnp.sum(jnp.where(sels[half], prods[j], 0.0), axis=-1, keepdims=True)
            ds = p * (dps[h] - delta)
            dss.append(ds)
            dsinks.append(-jnp.sum(jnp.exp(sink_ref[h] - lse_h) * delta, axis=0, keepdims=True))
            pbs.append(p.astype(BF16))
            dsbs.append(ds.astype(BF16))
        dt5_ref[...] += jnp.stack(dss)
        dsink_ref[...] += jnp.concatenate(dsinks, axis=1)
        dqs = []
        for j in range(4):
            a, b = 2 * j, 2 * j + 1
            zero = jnp.zeros_like(kk[j // 2])
            dqs.append(_dot(dsbs[a], jnp.where(sels[0], kk[j // 2], zero)) + _dot(dsbs[b], jnp.where(sels[1], kk[j // 2], zero)))
        dq_ref[...] = jnp.concatenate(dqs, axis=1)
        dk_groups, dv_groups = [], []
        for g in range(2):
            dkk = sum(_dot(dsbs[h], qhs[h], TN) for h in range(4 * g, 4 * g + 4))
            dvv = sum(_dot(pbs[h], dohs[h], TN) for h in range(4 * g, 4 * g + 4))
            dk_groups.append(dkk + pltpu.roll(dkk, HD, 1))
            dv_groups.append(dvv + pltpu.roll(dvv, HD, 1))
        dkv_ref[pl.ds(off, SW_KEYS), :] += jnp.concatenate(
            [jnp.where(first, dk_groups[0], dk_groups[1]), jnp.where(first, dv_groups[0], dv_groups[1])], axis=1)

    s_tok = q.shape[0]
    blk = pl.BlockSpec((SW_BLK, 512), lambda n: (n, 0))
    kv_spec = pl.BlockSpec(kv.shape, lambda n: (0, 0))
    t5_spec = pl.BlockSpec((8, SW_BLK, SW_KEYS), lambda n: (0, 0, 0))
    vec = pl.BlockSpec((1, 8), lambda n: (0, 0))
    return pl.pallas_call(
        body, name=name,
        out_shape=(_sds((s_tok, 512), F32), _sds(kv.shape, F32), _sds((8, SW_BLK, SW_KEYS), F32), _sds((1, 8), F32)),
        grid=(SW_NB,), in_specs=[blk, kv_spec, blk, blk, pl.BlockSpec((SW_BLK, 8), lambda n: (n, 0)), t5_spec,
                                 pl.BlockSpec(memory_space=pltpu.SMEM)],
        out_specs=(blk, kv_spec, t5_spec, vec), compiler_params=_params(("arbitrary",)))(q, kv, o, do, lse, t5, sink)


def mixer_output_fwd(o_na, o_sw, zg, bias, wa, ws, wo, res, *, name, tm=512):
    def body(ona_ref, osw_ref, z0_ref, z1_ref, b0_ref, b1_ref, wa_ref, ws_ref, wo_ref, res_ref, y_ref, pa_ref, ps_ref, m_ref):
        pa = _dot(ona_ref[...], wa_ref[...]).astype(BF16)
        ps = _dot(osw_ref[...], ws_ref[...]).astype(BF16)
        pa_ref[...] = pa
        ps_ref[...] = ps
        g0 = jax.nn.sigmoid(z0_ref[...] + b0_ref[...])
        g1 = jax.nn.sigmoid(z1_ref[...] + b1_ref[...])
        merged = (g0 * pa + g1 * ps).astype(BF16)
        m_ref[...] = merged
        y_ref[...] = res_ref[...] + _dot(merged, wo_ref[...])

    s = zg.shape[0]
    half = lambda j: pl.BlockSpec((tm, DM), lambda i, j=j: (i, j))
    bvec = lambda j: pl.BlockSpec((1, DM), lambda i, j=j: (0, j))
    att = pl.BlockSpec((tm, 512), lambda i: (i, 0))
    whole = lambda a: pl.BlockSpec(a.shape, lambda i: (0, 0), pipeline_mode=pl.Buffered(1))
    act = _sds((s, DM), BF16)
    return pl.pallas_call(
        body, name=name, out_shape=(_sds((s, DM), F32), act, act, act), grid=(s // tm,),
        in_specs=[att, att, half(0), half(1), bvec(0), bvec(1), whole(wa), whole(ws), whole(wo), half(0)],
        out_specs=(half(0),) * 4, compiler_params=_params(("parallel",)))(o_na, o_sw, zg, zg, bias, bias, wa, ws, wo, res)


def mixer_output_bwd(dy, zg, bias, pa, ps, wa, ws, wo, *, name, tm=512, dep=None):
    def body(dy_ref, z0_ref, z1_ref, b0_ref, b1_ref, pa_ref, ps_ref, wa_ref, ws_ref, wo_ref, *rest):
        dpa_ref, dps_ref, dz_ref, db_ref, dona_ref, dosw_ref = rest[-6:]

        @pl.when(pl.program_id(0) == 0)
        def _():
            db_ref[...] = jnp.zeros_like(db_ref)

        dm = _dot(dy_ref[...].astype(BF16), wo_ref[...], NT)
        g0 = jax.nn.sigmoid(z0_ref[...] + b0_ref[...])
        g1 = jax.nn.sigmoid(z1_ref[...] + b1_ref[...])
        dpa = (dm * g0).astype(BF16)
        dps = (dm * g1).astype(BF16)
        dpa_ref[...] = dpa
        dps_ref[...] = dps
        dz0 = dm * pa_ref[...] * g0 * (1.0 - g0)
        dz1 = dm * ps_ref[...] * g1 * (1.0 - g1)
        dz_ref[:, 0:DM] = dz0.astype(BF16)
        dz_ref[:, DM:2 * DM] = dz1.astype(BF16)
        db_ref[:, 0:DM] += jnp.sum(dz0, axis=0, keepdims=True)
        db_ref[:, DM:2 * DM] += jnp.sum(dz1, axis=0, keepdims=True)
        dona_ref[...] = _dot(dpa, wa_ref[...], NT).astype(BF16)
        dosw_ref[...] = _dot(dps, ws_ref[...], NT).astype(BF16)

    s = zg.shape[0]
    half = lambda j: pl.BlockSpec((tm, DM), lambda i, j=j: (i, j))
    bvec = lambda j: pl.BlockSpec((1, DM), lambda i, j=j: (0, j))
    att = pl.BlockSpec((tm, 512), lambda i: (i, 0))
    whole = lambda a: pl.BlockSpec(a.shape, lambda i: (0, 0), pipeline_mode=pl.Buffered(1))
    ins, specs = _with_dep([dy, zg, zg, bias, bias, pa, ps, wa, ws, wo],
                           [half(0), half(0), half(1), bvec(0), bvec(1), half(0), half(0), whole(wa), whole(ws), whole(wo)], dep)
    return pl.pallas_call(
        body, name=name,
        out_shape=(_sds((s, DM), BF16), _sds((s, DM), BF16), _sds((s, GATE_W), BF16), _sds((1, GATE_W), F32),
                   _sds((s, 512), BF16), _sds((s, 512), BF16)),
        grid=(s // tm,), in_specs=specs,
        out_specs=(half(0), half(0), pl.BlockSpec((tm, GATE_W), lambda i: (i, 0)), pl.BlockSpec((1, GATE_W), lambda i: (0, 0)),
                   att, att),
        compiler_params=_params(("arbitrary",)))(*ins)


def adamw_small(ws, gs, ms, vs, *, name):
    cnt = len(ws)

    def body(*refs):
        ins, outs = refs[:4 * cnt], refs[4 * cnt:]
        for i in range(cnt):
            w_ref, g_ref, m_ref, v_ref = ins[4 * i:4 * i + 4]
            d_ref, nm_ref, nv_ref = outs[3 * i:3 * i + 3]
            g = g_ref[...]
            nm = ADAM_B1 * m_ref[...] + (1.0 - ADAM_B1) * g
            nv = ADAM_B2 * v_ref[...] + (1.0 - ADAM_B2) * jnp.square(g)
            m_hat = nm / (1.0 - ADAM_B1 ** ADAM_STEP)
            v_hat = nv / (1.0 - ADAM_B2 ** ADAM_STEP)
            d_ref[...] = -ADAM_LR * (m_hat / (jnp.sqrt(v_hat) + ADAM_EPS) + ADAM_WD * w_ref[...])
            nm_ref[...] = nm
            nv_ref[...] = nv

    flat = [a for i in range(cnt) for a in (ws[i], gs[i], ms[i], vs[i])]
    res = pl.pallas_call(
        body, name=name, out_shape=tuple(_sds(ws[i].shape, F32) for i in range(cnt) for _ in range(3)),
        compiler_params=_params())(*flat)
    return [tuple(res[3 * i:3 * i + 3]) for i in range(cnt)]


def adamw_layer(ws, ms, vs, mines, theirs, cidx, layer, filled=None, *, name):
    cnt = len(ws)
    _, k, n = ws[0].shape
    nt = 2
    tk = k // 2 // nt

    def body(c_ref, *refs):
        own = pl.program_id(0) == c_ref[0]
        outs = refs[-4 * cnt:]
        for i in range(cnt):
            w_ref, m_ref, v_ref, a_ref, b_ref = refs[5 * i:5 * i + 5]
            g_ref, d_ref, nm_ref, nv_ref = outs[4 * i:4 * i + 4]
            g = jnp.where(own, a_ref[...], b_ref[...])
            g_ref[...] = g
            nm = ADAM_B1 * m_ref[...] + (1.0 - ADAM_B1) * g
            nv = ADAM_B2 * v_ref[...] + (1.0 - ADAM_B2) * jnp.square(g)
            m_hat = nm / (1.0 - ADAM_B1 ** ADAM_STEP)
            v_hat = nv / (1.0 - ADAM_B2 ** ADAM_STEP)
            d_ref[...] = -ADAM_LR * (m_hat / (jnp.sqrt(v_hat) + ADAM_EPS) + ADAM_WD * w_ref[...])
            nm_ref[...] = nm
            nv_ref[...] = nv

    full = pl.BlockSpec((None, tk, n), lambda hf, t, c: (layer, hf * nt + t, 0))
    half_mine = pl.BlockSpec((tk, n), lambda hf, t, c: (jnp.where(hf == c[0], t, 0), 0))
    half_theirs = pl.BlockSpec((tk, n), lambda hf, t, c: (jnp.where(hf != c[0], t, 0), 0))
    out = _sds(ws[0].shape, F32)
    ins, specs, aliases = [cidx], [], {}
    for i in range(cnt):
        ins += [ws[i], ms[i], vs[i], mines[i], theirs[i]]
        specs += [full, full, full, half_mine, half_theirs]
    if filled is not None:
        aliases = {len(ins) + j: j for j in range(4 * cnt)}
        ins += [a for f in filled for a in f]
        specs += [pl.BlockSpec(memory_space=pl.ANY)] * (4 * cnt)
    res = pl.pallas_call(
        body, name=name, out_shape=(out,) * (4 * cnt),
        grid_spec=pltpu.PrefetchScalarGridSpec(
            num_scalar_prefetch=1, grid=(2, nt), in_specs=specs, out_specs=(full,) * (4 * cnt)),
        input_output_aliases=aliases,
        compiler_params=_params(("arbitrary", "arbitrary")))(*ins)
    return [tuple(res[4 * i:4 * i + 4]) for i in range(cnt)]


def t5_table_grad(dt5_a, dt5_b, *, name):
    def body(a_ref, b_ref, map_ref, o_ref):
        d = a_ref[...] + b_ref[...]
        bucket = map_ref[...]
        for b in range(32):
            hit = (bucket == b)[None]
            o_ref[b] = jnp.sum(jnp.sum(jnp.where(hit, d, 0.0), axis=2), axis=1, keepdims=True)

    return pl.pallas_call(
        body, name=name, out_shape=_sds((32, 8, 1), F32), compiler_params=_params())(
            dt5_a, dt5_b, jnp.asarray(t5_bucket_map()))


def rpb_grad(dbias, *, name):
    def body(d_ref, rev_ref, o_ref):
        rev = rev_ref[...]
        for h in range(NA_HEADS):
            for pr in range(NA_WR // 2):
                d = d_ref[h, :, 128 * pr:128 * pr + 128]
                hi = d.astype(BF16)
                lo = (d - hi.astype(F32)).astype(BF16)
                flipped = _dot(rev, hi) + _dot(rev, lo)
                o_ref[h, pr] = jnp.sum(pltpu.roll(flipped, 0, 1, stride=1, stride_axis=0), axis=0, keepdims=True)

    anti = jnp.asarray(np.eye(GRID_W, dtype=np.float32)[::-1], dtype=BF16)
    e = pl.pallas_call(
        body, name=name, out_shape=_sds((NA_WR, NA_HEADS, NA_WR // 2, 1, 128), F32), grid=(NA_WR,),
        in_specs=[pl.BlockSpec((None, NA_HEADS, GRID_W, NA_KEYS), lambda p: (p, 0, 0, 0)),
                  pl.BlockSpec((GRID_W, GRID_W), lambda p: (0, 0))],
        out_specs=pl.BlockSpec((None, NA_HEADS, NA_WR // 2, 1, 128), lambda p: (p, 0, 0, 0, 0)),
        compiler_params=_params(("parallel",)))(dbias, anti)
    nci, nri = 2 * NA_WC - 1, 2 * NA_WR - 1
    e = e.reshape(NA_WR, NA_HEADS, NA_WR // 2, 128).transpose(0, 2, 1, 3).reshape(NA_WR * NA_WR // 2, NA_HEADS, 128)
    parts = jnp.concatenate([e[..., 48:48 + nci], jnp.concatenate([e[..., 112:128], e[..., 0:nci - 16]], axis=-1)], axis=0)
    p, pr = np.arange(NA_WR)[:, None], np.arange(NA_WR // 2)[None, :]
    ri = np.concatenate([(2 * pr - p + NA_WR - 1).reshape(-1), (2 * pr - p + NA_WR).reshape(-1)])
    pick = jnp.asarray((ri[None, :] == np.arange(16)[:, None]).astype(np.float32))
    out = mm(pick, parts.reshape(2 * NA_WR * NA_WR // 2, NA_HEADS * nci), name=name + "_rows", exact=True)
    return out.reshape(16, NA_HEADS, nci)[:nri].transpose(1, 0, 2)


BIG = ("ffn1_w_gate", "ffn1_w_up", "ffn1_w_down", "w_in", "w_branch_na", "w_branch_sw", "w_out",
       "ffn2_w_gate", "ffn2_w_up", "ffn2_w_down")
SMALL = ("ffn1_norm", "mix_norm", "b_gate", "na_q_norm", "na_k_norm", "na_rpb", "sw_q_norm", "sw_k_norm", "sw_sink",
         "ffn2_norm")


def _cols_to_full(w4):
    return w4.transpose(1, 0, 2).reshape(w4.shape[1], NSH * w4.shape[2])


def _mixer_weights(g):
    w_in_t = g["w_in"].reshape(IN_W, DM)
    return dict(w_in_t=w_in_t, wa=_cols_to_full(g["w_branch_na"]),
                ws=_cols_to_full(g["w_branch_sw"]), wo=g["w_out"].reshape(DM, DM))


GROUPS = {"ffn1": ("ffn1_w_gate", "ffn1_w_up", "ffn1_w_down"), "mix": ("w_in", "w_branch_na", "w_branch_sw", "w_out"),
          "ffn2": ("ffn2_w_gate", "ffn2_w_up", "ffn2_w_down")}


def layer_fwd(x, p, weights, t5b, target=None):
    row = lambda v: v.reshape(1, -1)
    stacked = lambda g: {n: a.reshape(DFF, DM) for n, a in g.items()}
    g1 = stacked(weights("ffn1", x))
    y1, h1, gg1, uu1 = ffn_fwd(x, row(p["ffn1_norm"]), g1["ffn1_w_gate"], g1["ffn1_w_up"], g1["ffn1_w_down"], name="ffn_fwd")
    w = _mixer_weights(weights("mix", y1))
    hm, z, zg, qa, ka, va, qs, kv = mixer_input_fwd(y1, row(p["mix_norm"]), w["w_in_t"], p["na_q_norm"], p["na_k_norm"],
                                                    p["sw_q_norm"], p["sw_k_norm"], name="mixer_input_fwd")
    bias = p["na_bias"]
    o_na, lse_na = na_fwd(qa, ka, va, bias, name="na_fwd")
    kvp = jnp.pad(kv, ((SW_BLK, SW_BLK), (0, 0)))
    sink = p["sw_sink"]
    o_sw, lse_sw = sw_fwd(qs, kvp, t5b, sink, name="sw_fwd")
    y2, pa, ps, merged = mixer_output_fwd(o_na, o_sw, zg, row(p["b_gate"]), w["wa"], w["ws"], w["wo"], y1,
                                          name="mixer_output_fwd")
    g2 = stacked(weights("ffn2", y2))
    *y3, h2, gg2, uu2 = ffn_fwd(y2, row(p["ffn2_norm"]), g2["ffn2_w_gate"], g2["ffn2_w_up"], g2["ffn2_w_down"], target,
                                name="ffn_fwd")
    y3 = y3[0] if target is None else tuple(y3)
    saved = dict(x=x, y1=y1, h1=h1, gg1=gg1, uu1=uu1, hm=hm, z=z, zg=zg, qa=qa, ka=ka, va=va, qs=qs, kvp=kvp, bias=bias,
                 o_na=o_na, lse_na=lse_na, o_sw=o_sw, lse_sw=lse_sw, pa=pa, ps=ps, merged=merged, y2=y2, h2=h2, gg2=gg2,
                 uu2=uu2, w=w, sink=sink, g1=g1, g2=g2)
    return y3, saved


def layer_bwd(dy3, dy3_bf, sv, p, t5b, emit, dep=None):
    w, g1, g2 = sv["w"], sv["g1"], sv["g2"]
    row = lambda v: v.reshape(1, -1)
    fold = lambda v: v.reshape(-1, HD).sum(axis=0)
    small = {}
    dy2, dy2_bf, small["ffn2_norm"], act, dg, du = ffn_bwd_tokens(
        dy3, sv["y2"], row(p["ffn2_norm"]), sv["gg2"], sv["uu2"], g2["ffn2_w_gate"], g2["ffn2_w_up"], g2["ffn2_w_down"],
        name="ffn_bwd_tokens", dep=dep)
    shards = lambda gs: [g.reshape(NSH, FSH, DM) for g in gs]
    token = emit("ffn2", shards(ffn_bwd_weights(sv["h2"], dy3_bf, act, dg, du, name="ffn_bwd_weights")))
    dpa, dps, dzg, small["b_gate"], do_na, do_sw = mixer_output_bwd(
        dy2, sv["zg"], row(p["b_gate"]), sv["pa"], sv["ps"], w["wa"], w["ws"], w["wo"], name="mixer_output_bwd", dep=token)
    gw_out, gw_na, gw_sw = mixer_output_dw(sv["merged"], dy2_bf, sv["o_na"], dpa, sv["o_sw"], dps, name="mixer_output_dw")
    gw_out = gw_out.reshape(NSH, DM // NSH, DM)
    dqa, dka, dva, dbias = na_bwd(sv["qa"], sv["ka"], sv["va"], sv["o_na"], do_na, sv["lse_na"], sv["bias"], name="na_bwd")
    dqs, dkvp, dt5, dsink = sw_bwd(sv["qs"], sv["kvp"], sv["o_sw"], do_sw, sv["lse_sw"], t5b, sv["sink"], name="sw_bwd")
    dkv = dkvp[SW_BLK:SW_BLK + SEQ]
    dz, dgqa, dgka, dgqs, dgks = qknorm_bwd(sv["z"], dqa, dka, dva, dqs, dkv, p["na_q_norm"], p["na_k_norm"],
                                            p["sw_q_norm"], p["sw_k_norm"], name="qknorm_bwd")
    small["na_q_norm"], small["na_k_norm"], small["sw_q_norm"], small["sw_k_norm"] = fold(dgqa), fold(dgka), fold(dgqs), fold(dgks)
    small["na_rpb"] = rpb_grad(dbias, name="rpb_grad")
    small["sw_sink"] = dsink
    gw_in = mixer_input_dw(dz, dzg, sv["hm"], name="mixer_input_dw").reshape(NSH, IN_W // NSH, DM)
    token = emit("mix", (gw_in, gw_na, gw_sw, gw_out))
    dy1, dy1_bf, small["mix_norm"] = mixer_input_bwd(dz, dzg, w["w_in_t"], sv["y1"], row(p["mix_norm"]), dy2,
                                                     name="mixer_input_bwd", dep=token)
    dx, dx_bf, small["ffn1_norm"], act, dg, du = ffn_bwd_tokens(
        dy1, sv["x"], row(p["ffn1_norm"]), sv["gg1"], sv["uu1"], g1["ffn1_w_gate"], g1["ffn1_w_up"], g1["ffn1_w_down"],
        name="ffn_bwd_tokens")
    emit("ffn1", shards(ffn_bwd_weights(sv["h1"], dy1_bf, act, dg, du, name="ffn_bwd_weights")))
    return dx, dx_bf, small, dt5


ANY = pl.BlockSpec(memory_space=pl.ANY)


def _place():
    x, y, c = lax.axis_index("x"), lax.axis_index("y"), lax.axis_index("c")
    chips = [(1 - x, y), (x, 1 - y), (1 - x, 1 - y)]
    return x, y, c, chips


def _remote(src, dst, send_sem, recv_sem, to):
    return pltpu.make_async_remote_copy(src_ref=src, dst_ref=dst, send_sem=send_sem, recv_sem=recv_sem, device_id=to,
                                        device_id_type=MESH)


HBM = pl.BlockSpec(memory_space=pltpu.HBM)
SEM = pl.BlockSpec(memory_space=pltpu.SEMAPHORE)
ORDERED_EFFECT = pltpu.SideEffectType.DATAFLOW_SIDE_EFFECTING


def _in_hbm(v):
    return pltpu.with_memory_space_constraint(v, pltpu.HBM)


def _row_half(ref_shape_rows, c):
    half = ref_shape_rows // 2
    return pl.ds(c * half, half)


def _ici_gather_copies(w, land, send_sems, recv_sems):
    x, y, c, chips = _place()
    me = 2 * x + y
    copies = []
    for a in range(len(w)):
        rows = _row_half(w[a].shape[0], c)
        for k, chip in enumerate(chips):
            copies.append(_remote(w[a].at[rows], land[a].at[me, rows], send_sems.at[4 * a + k], recv_sems.at[4 * a + k],
                                  (*chip, c)))
        copies.append(_remote(w[a], land[a].at[me], send_sems.at[4 * a + 3], recv_sems.at[4 * a + 3], (x, y, 1 - c)))
    return copies


def _d2d_gather_copies(w, land, send_sems, recv_sems):
    x, y, c, chips = _place()
    copies = []
    for a in range(len(w)):
        rows = _row_half(w[a].shape[0], c)
        for k, (cx, cy) in enumerate(chips):
            blk = land[a].at[2 * cx + cy, rows]
            copies.append(_remote(blk, blk, send_sems.at[3 * a + k], recv_sems.at[3 * a + k], (x, y, 1 - c)))
    return copies


def _d2d_gather_waits(w, land, send_sems, recv_sems):
    x, y, c, chips = _place()
    waits = []
    for a in range(len(w)):
        rows = _row_half(w[a].shape[0], 1 - c)
        for k, (cx, cy) in enumerate(chips):
            blk = land[a].at[2 * cx + cy, rows]
            waits.append(_remote(blk, blk, send_sems.at[3 * a + k], recv_sems.at[3 * a + k], (x, y, 1 - c)))
    return waits


def gather_start(groups, dep=None, *, name):
    sizes = [len(g) for g in groups]
    shards = [s for g in groups for s in g]
    n, ng = len(shards), len(groups)
    extra = [] if dep is None else [dep]

    def body(*refs):
        first_out = 2 * n + len(extra)
        w, land, sems = refs[:n], refs[n:2 * n], refs[first_out:first_out + 2 * ng]
        off = 0
        for gi, size in enumerate(sizes):
            for cp in _ici_gather_copies(w[off:off + size], land[off:off + size], sems[2 * gi], sems[2 * gi + 1]):
                cp.start()
            off += size

    lands = [lax.empty((NSH,) + s.shape, s.dtype) for s in shards]
    sem_shapes = tuple(pltpu.SemaphoreType.DMA((4 * size,)) for size in sizes for _ in range(2))
    res = pl.pallas_call(
        body, name=name,
        out_shape=sem_shapes + tuple(pltpu.HBM(s.shape, s.dtype) for s in shards) + tuple(pltpu.HBM(l.shape, l.dtype) for l in lands),
        in_specs=[HBM] * (2 * n) + [ANY] * len(extra), out_specs=(SEM,) * (2 * ng) + (HBM,) * (2 * n),
        input_output_aliases={i: 2 * ng + i for i in range(2 * n)},
        compiler_params=pltpu.CompilerParams(has_side_effects=ORDERED_EFFECT))(
            *[_in_hbm(s) for s in shards], *[_in_hbm(l) for l in lands], *extra)
    out, off = [], 0
    for gi, size in enumerate(sizes):
        out.append((res[2 * gi], res[2 * gi + 1], list(res[2 * ng + off:2 * ng + off + size]),
                    list(res[2 * ng + n + off:2 * ng + n + off + size])))
        off += size
    return out


def gather_wait(send_sems, recv_sems, shards, lands, after, *, name):
    n = len(shards)

    def body(*refs):
        w, land = refs[:n], refs[n:2 * n]
        send, recv = refs[2 * n:2 * n + 2]
        for cp in _ici_gather_copies(w, land, send, recv):
            cp.wait_send()
            cp.wait_recv()

    res = pl.pallas_call(
        body, name=name,
        out_shape=tuple(pltpu.HBM(s.shape, s.dtype) for s in shards) + tuple(pltpu.HBM(l.shape, l.dtype) for l in lands),
        in_specs=[HBM] * (2 * n) + [SEM, SEM] + [ANY] * len(after), out_specs=(HBM,) * (2 * n),
        input_output_aliases={i: i for i in range(2 * n)},
        compiler_params=pltpu.CompilerParams(has_side_effects=ORDERED_EFFECT))(*shards, *lands, send_sems, recv_sems, *after)
    return list(res[:n]), list(res[n:])


def gather_finish(shards, lands, *, name):
    n = len(shards)

    def body(*refs):
        w, land = refs[:n], refs[n:2 * n]
        send_sems, recv_sems = refs[3 * n:]
        d2d = _d2d_gather_copies(w, land, send_sems, recv_sems)
        for cp in d2d:
            cp.start()
        for cp in _d2d_gather_waits(w, land, send_sems, recv_sems):
            cp.wait_recv()
        for cp in d2d:
            cp.wait_send()

    return list(pl.pallas_call(
        body, name=name, out_shape=tuple(pltpu.HBM(l.shape, l.dtype) for l in lands),
        in_specs=[ANY] * (2 * n), out_specs=tuple([ANY] * n), input_output_aliases={n + i: i for i in range(n)},
        scratch_shapes=[pltpu.SemaphoreType.DMA((3 * n,)), pltpu.SemaphoreType.DMA((3 * n,))])(*shards, *lands))


def _pair_exchange_copies(g, buf, send_sems, recv_sems):
    x, y, c, _ = _place()
    copies = []
    for a in range(len(g)):
        half = g[a].shape[1] // 2
        copies.append(_remote(g[a].at[:, pl.ds((1 - c) * half, half)], buf[a], send_sems.at[a], recv_sems.at[a], (x, y, 1 - c)))
    return copies


def pair_exchange_start(grads, dep=None, *, name):
    n = len(grads)
    extra = [] if dep is None else [dep]

    def body(*refs):
        sems = refs[2 * n + len(extra):]
        for cp in _pair_exchange_copies(refs[:n], refs[n:2 * n], sems[0], sems[1]):
            cp.start()
        refs[-1][...] = jnp.zeros_like(refs[-1])

    lands = [lax.empty((NSH, g.shape[1] // 2, g.shape[2]), g.dtype) for g in grads]
    res = pl.pallas_call(
        body, name=name,
        out_shape=(pltpu.SemaphoreType.DMA((n,)), pltpu.SemaphoreType.DMA((n,)))
        + tuple(pltpu.HBM(g.shape, g.dtype) for g in grads) + tuple(pltpu.HBM(l.shape, l.dtype) for l in lands)
        + (_sds((8, 128), F32),),
        in_specs=[HBM] * (2 * n) + [ANY] * len(extra),
        out_specs=(SEM, SEM) + (HBM,) * (2 * n) + (pl.BlockSpec(memory_space=pltpu.VMEM),),
        input_output_aliases={i: 2 + i for i in range(2 * n)},
        compiler_params=pltpu.CompilerParams(has_side_effects=ORDERED_EFFECT))(
            *[_in_hbm(g) for g in grads], *[_in_hbm(l) for l in lands], *extra)
    return res[0], res[1], list(res[2:2 + n]), list(res[2 + n:2 + 2 * n]), res[-1]


def pair_exchange_wait(send_sems, recv_sems, grads, lands, after, *, name):
    n = len(grads)

    def body(*refs):
        for cp in _pair_exchange_copies(refs[:n], refs[n:2 * n], refs[2 * n], refs[2 * n + 1]):
            cp.wait_send()
            cp.wait_recv()

    res = pl.pallas_call(
        body, name=name,
        out_shape=tuple(pltpu.HBM(g.shape, g.dtype) for g in grads) + tuple(pltpu.HBM(l.shape, l.dtype) for l in lands),
        in_specs=[HBM] * (2 * n) + [SEM, SEM] + [ANY] * len(after), out_specs=(HBM,) * (2 * n),
        input_output_aliases={i: i for i in range(2 * n)},
        compiler_params=pltpu.CompilerParams(has_side_effects=ORDERED_EFFECT))(*grads, *lands, send_sems, recv_sems, *after)
    return list(res[:n]), list(res[n:])


def _chip_exchange_copies(s, buf, send_sems, recv_sems):
    x, y, c, chips = _place()
    return [_remote(s[a].at[2 * cx + cy], buf[a].at[k], send_sems.at[3 * a + k], recv_sems.at[3 * a + k], (cx, cy, c))
            for a in range(len(s)) for k, (cx, cy) in enumerate(chips)]


def exchange_start(sums, grads, *, name):
    n1, n2 = len(sums), len(grads)

    def body(*refs):
        first_out = 2 * (n1 + n2)
        chip = _chip_exchange_copies(refs[:n1], refs[n1:2 * n1], refs[first_out], refs[first_out + 1])
        pair = _pair_exchange_copies(refs[2 * n1:2 * n1 + n2], refs[2 * n1 + n2:first_out], refs[first_out + 2],
                                     refs[first_out + 3])
        for cp in chip + pair:
            cp.start()
        refs[-1][...] = jnp.zeros_like(refs[-1])

    chip_lands = [lax.empty((3,) + s.shape[1:], s.dtype) for s in sums]
    pair_lands = [lax.empty((NSH, g.shape[1] // 2, g.shape[2]), g.dtype) for g in grads]
    arrays = list(sums) + chip_lands + list(grads) + pair_lands
    res = pl.pallas_call(
        body, name=name,
        out_shape=(pltpu.SemaphoreType.DMA((3 * n1,)), pltpu.SemaphoreType.DMA((3 * n1,)), pltpu.SemaphoreType.DMA((n2,)),
                   pltpu.SemaphoreType.DMA((n2,)))
        + tuple(pltpu.HBM(a.shape, a.dtype) for a in arrays) + (_sds((8, 128), F32),),
        in_specs=[HBM] * len(arrays), out_specs=(SEM,) * 4 + (HBM,) * len(arrays) + (pl.BlockSpec(memory_space=pltpu.VMEM),),
        input_output_aliases={i: 4 + i for i in range(len(arrays))},
        compiler_params=pltpu.CompilerParams(has_side_effects=ORDERED_EFFECT))(*[_in_hbm(a) for a in arrays])
    thru = list(res[4:4 + len(arrays)])
    chip = (res[0], res[1], thru[:n1], thru[n1:2 * n1])
    pair = (res[2], res[3], thru[2 * n1:2 * n1 + n2], thru[2 * n1 + n2:])
    return chip, pair, res[-1]


def chip_exchange_start(sums, *, name):
    n = len(sums)

    def body(*refs):
        for cp in _chip_exchange_copies(refs[:n], refs[n:2 * n], refs[2 * n], refs[2 * n + 1]):
            cp.start()
        refs[-1][...] = jnp.zeros_like(refs[-1])

    lands = [lax.empty((3,) + s.shape[1:], s.dtype) for s in sums]
    res = pl.pallas_call(
        body, name=name,
        out_shape=(pltpu.SemaphoreType.DMA((3 * n,)), pltpu.SemaphoreType.DMA((3 * n,)))
        + tuple(pltpu.HBM(s.shape, s.dtype) for s in sums) + tuple(pltpu.HBM(l.shape, l.dtype) for l in lands)
        + (_sds((8, 128), F32),),
        in_specs=[HBM] * (2 * n), out_specs=(SEM, SEM) + (HBM,) * (2 * n) + (pl.BlockSpec(memory_space=pltpu.VMEM),),
        input_output_aliases={i: 2 + i for i in range(2 * n)},
        compiler_params=pltpu.CompilerParams(has_side_effects=ORDERED_EFFECT))(
            *[_in_hbm(s) for s in sums], *[_in_hbm(l) for l in lands])
    return res[0], res[1], list(res[2:2 + n]), list(res[2 + n:2 + 2 * n]), res[-1]


def chip_exchange_wait(send_sems, recv_sems, sums, lands, after, *, name):
    n = len(sums)

    def body(*refs):
        for cp in _chip_exchange_copies(refs[:n], refs[n:2 * n], refs[2 * n], refs[2 * n + 1]):
            cp.wait_send()
            cp.wait_recv()

    res = pl.pallas_call(
        body, name=name,
        out_shape=tuple(pltpu.HBM(s.shape, s.dtype) for s in sums) + tuple(pltpu.HBM(l.shape, l.dtype) for l in lands),
        in_specs=[HBM] * (2 * n) + [SEM, SEM] + [ANY] * len(after), out_specs=(HBM,) * (2 * n),
        input_output_aliases={i: i for i in range(2 * n)},
        compiler_params=pltpu.CompilerParams(has_side_effects=ORDERED_EFFECT))(*sums, *lands, send_sems, recv_sems, *after)
    return list(res[:n]), list(res[n:])


def _pair_send_copies(h, got, send_sems, recv_sems):
    x, y, c, _ = _place()
    return [_remote(h[i], got[i], send_sems.at[i], recv_sems.at[i], (x, y, 1 - c)) for i in range(len(h))]


def pair_send_start(halves, *, name):
    n = len(halves)

    def body(*refs):
        for cp in _pair_send_copies(refs[:n], refs[n:2 * n], refs[2 * n], refs[2 * n + 1]):
            cp.start()
        refs[-1][...] = jnp.zeros_like(refs[-1])

    lands = [lax.empty(h.shape, h.dtype) for h in halves]
    res = pl.pallas_call(
        body, name=name,
        out_shape=(pltpu.SemaphoreType.DMA((n,)), pltpu.SemaphoreType.DMA((n,)))
        + tuple(pltpu.HBM(h.shape, h.dtype) for h in halves) * 2 + (_sds((8, 128), F32),),
        in_specs=[HBM] * (2 * n), out_specs=(SEM, SEM) + (HBM,) * (2 * n) + (pl.BlockSpec(memory_space=pltpu.VMEM),),
        input_output_aliases={i: 2 + i for i in range(2 * n)},
        compiler_params=pltpu.CompilerParams(has_side_effects=ORDERED_EFFECT))(
            *[_in_hbm(h) for h in halves], *[_in_hbm(l) for l in lands])
    return res[0], res[1], list(res[2:2 + n]), list(res[2 + n:2 + 2 * n]), res[-1]


def pair_send_wait(send_sems, recv_sems, halves, lands, after, *, name):
    n = len(halves)

    def body(*refs):
        for cp in _pair_send_copies(refs[:n], refs[n:2 * n], refs[2 * n], refs[2 * n + 1]):
            cp.wait_send()
            cp.wait_recv()

    res = pl.pallas_call(
        body, name=name, out_shape=tuple(pltpu.HBM(h.shape, h.dtype) for h in halves) * 2,
        in_specs=[HBM] * (2 * n) + [SEM, SEM] + [ANY] * len(after), out_specs=(HBM,) * (2 * n),
        input_output_aliases={i: i for i in range(2 * n)},
        compiler_params=pltpu.CompilerParams(has_side_effects=ORDERED_EFFECT))(*halves, *lands, send_sems, recv_sems, *after)
    return list(res[:n]), list(res[n:])


def allreduce_small(v, *, name):
    rows = v.shape[0]

    def body(v_ref, o_ref, gath, send_sems, recv_sems):
        x, y, c, _ = _place()
        me = 4 * x + 2 * y + c
        gath[me] = v_ref[...]
        copies = []
        for k in range(1, 8):
            fx, fy, fc = (k >> 2) & 1, (k >> 1) & 1, k & 1
            peer = (jnp.where(fx, 1 - x, x), jnp.where(fy, 1 - y, y), jnp.where(fc, 1 - c, c))
            cp = _remote(v_ref, gath.at[me], send_sems.at[k - 1], recv_sems.at[k - 1], peer)
            cp.start()
            copies.append(cp)
        for cp in copies:
            cp.wait()
        acc = gath[0]
        for d in range(1, 8):
            acc = acc + gath[d]
        o_ref[...] = acc

    return pl.pallas_call(
        body, name=name, out_shape=_sds(v.shape, F32),
        in_specs=[pl.BlockSpec(memory_space=pltpu.VMEM)], out_specs=pl.BlockSpec(memory_space=pltpu.VMEM),
        scratch_shapes=[pltpu.VMEM((8, rows, 128), F32), pltpu.SemaphoreType.DMA((7,)), pltpu.SemaphoreType.DMA((7,))])(v)


def _same_shape_runs(arrays):
    runs = {}
    for i, a in enumerate(arrays):
        runs.setdefault(a.shape, []).append(i)
    return list(runs.values())


def _per_shape(fn, *lists):
    out = [None] * len(lists[0])
    for idx in _same_shape_runs(lists[0]):
        for i, r in zip(idx, fn(*[[l[i] for i in idx] for l in lists])):
            out[i] = r
    return out


def add_halves(gs, bufs, cidx, *, name):
    cnt = len(gs)
    _, k, n = gs[0].shape

    def body(c_ref, *refs):
        g, b, o = refs[:cnt], refs[cnt:2 * cnt], refs[2 * cnt:]
        for i in range(cnt):
            o[i][...] = (g[i][...].astype(F32) + b[i][...].astype(F32)).astype(BF16)

    blk = pl.BlockSpec((None, k // 2, n), lambda s, c: (s, 0, 0))
    mine = pl.BlockSpec((None, k // 2, n), lambda s, c: (s, c[0], 0))
    return list(pl.pallas_call(
        body, name=name, out_shape=tuple(_sds(b.shape, BF16) for b in bufs),
        grid_spec=pltpu.PrefetchScalarGridSpec(
            num_scalar_prefetch=1, grid=(NSH,), in_specs=[mine] * cnt + [blk] * cnt, out_specs=tuple([blk] * cnt)),
        compiler_params=_params(("parallel",)))(cidx, *gs, *bufs))


def add_chips(sums, bufs, sidx, *, name):
    cnt = len(sums)
    _, kh, n = sums[0].shape

    def body(s_ref, *refs):
        mine, b, o = refs[:cnt], refs[cnt:2 * cnt], refs[2 * cnt:]
        for i in range(cnt):
            o[i][...] = ((mine[i][...].astype(F32) + b[i][0].astype(F32)) + (b[i][1].astype(F32) + b[i][2].astype(F32)))

    own = pl.BlockSpec((None, kh, n), lambda i, s: (s[0], 0, 0))
    got = pl.BlockSpec((3, kh, n), lambda i, s: (0, 0, 0))
    out = pl.BlockSpec((kh, n), lambda i, s: (0, 0))
    return list(pl.pallas_call(
        body, name=name, out_shape=tuple(_sds((kh, n), F32) for _ in sums),
        grid_spec=pltpu.PrefetchScalarGridSpec(
            num_scalar_prefetch=1, grid=(1,), in_specs=[own] * cnt + [got] * cnt, out_specs=tuple([out] * cnt)),
        compiler_params=_params(("arbitrary",)))(sidx, *sums, *bufs))


PARAMS = ("ffn1_norm", "ffn1_w_gate", "ffn1_w_up", "ffn1_w_down", "mix_norm", "w_in", "b_gate", "na_q_norm", "na_k_norm",
          "na_rpb", "sw_q_norm", "sw_k_norm", "sw_sink", "t5_rel_table", "w_branch_na", "w_branch_sw", "w_out", "ffn2_norm",
          "ffn2_w_gate", "ffn2_w_up", "ffn2_w_down")
SMALL_ALL = tuple(n for n in PARAMS if n not in BIG)
TRANSPOSED = ("ffn1_w_gate", "ffn1_w_up", "w_in", "ffn2_w_gate", "ffn2_w_up")
SMALL_ROWS = 152


def _pack_small(vals):
    flat = jnp.concatenate([vals[n].reshape(-1).astype(F32) for n in SMALL_ALL] + [vals["loss"].reshape(-1)])
    return jnp.pad(flat, (0, SMALL_ROWS * 128 - flat.shape[0])).reshape(SMALL_ROWS, 128)


def _unpack_small(packed, like):
    flat, out, off = packed.reshape(-1), {}, 0
    for n in SMALL_ALL:
        size = math.prod(like[n].shape)
        out[n] = flat[off:off + size].reshape(like[n].shape)
        off += size
    out["loss"] = flat[off]
    return out


def kernel(x, ffn1_norm, ffn1_w_gate, ffn1_w_up, ffn1_w_down, mix_norm, w_in, b_gate, na_q_norm, na_k_norm, na_rpb, sw_q_norm, sw_k_norm, sw_sink, t5_rel_table, w_branch_na, w_branch_sw, w_out, ffn2_norm, ffn2_w_gate, ffn2_w_up, ffn2_w_down, loss_target, m_ffn1_norm, m_ffn1_w_gate, m_ffn1_w_up, m_ffn1_w_down, m_mix_norm, m_w_in, m_b_gate, m_na_q_norm, m_na_k_norm, m_na_rpb, m_sw_q_norm, m_sw_k_norm, m_sw_sink, m_t5_rel_table, m_w_branch_na, m_w_branch_sw, m_w_out, m_ffn2_norm, m_ffn2_w_gate, m_ffn2_w_up, m_ffn2_w_down, v_ffn1_norm, v_ffn1_w_gate, v_ffn1_w_up, v_ffn1_w_down, v_mix_norm, v_w_in, v_b_gate, v_na_q_norm, v_na_k_norm, v_na_rpb, v_sw_q_norm, v_sw_k_norm, v_sw_sink, v_t5_rel_table, v_w_branch_na, v_w_branch_sw, v_w_out, v_ffn2_norm, v_ffn2_w_gate, v_ffn2_w_up, v_ffn2_w_down):
    args = locals()
    tr = lambda n, a: jnp.transpose(a, (0, 2, 1)) if n in TRANSPOSED else a
    w = {n: tr(n, args[n]) for n in PARAMS}
    m = {n: tr(n, args["m_" + n]) for n in PARAMS}
    v = {n: tr(n, args["v_" + n]) for n in PARAMS}
    cidx = lax.axis_index("c").astype(jnp.int32).reshape(1)
    sidx = (2 * lax.axis_index("x") + lax.axis_index("y")).astype(jnp.int32).reshape(1)

    small = [{n: w[n][l] for n in SMALL} for l in range(DEPTH)]
    order = ("ffn1", "mix", "ffn2")

    keys = [(l, g) for l in range(DEPTH) for g in order]
    local = lambda l, g: [w[n][l].astype(BF16) for n in GROUPS[g]]
    first = gather_start([local(*keys[0])], name="gather_start")
    rest = gather_start([local(*key) for key in keys[1:]], first[0][2][0], name="gather_start")
    in_flight = dict(zip(keys, first + rest))
    t5b = t5_bias(w["t5_rel_table"], name="t5_bias")
    for l in range(DEPTH):
        small[l]["na_bias"] = na_bias_table(small[l]["na_rpb"], name="na_bias_table")
    early = [t5b] + [small[l]["na_bias"] for l in range(DEPTH)] + [rest[0][2][0]]

    def weights_of(l):
        def get(group, after):
            send_sems, recv_sems, thru, lands = in_flight[(l, group)]
            after = [after] + (early if (l, group) == keys[0] else [])
            thru, lands = gather_wait(send_sems, recv_sems, thru, lands, after, name="gather_wait")
            return dict(zip(GROUPS[group], gather_finish(thru, lands, name="gather_finish")))
        return get

    h0, saved0 = layer_fwd(x[0], small[0], weights_of(0), t5b)
    (dy, dy_bf, loss_row), saved1 = layer_fwd(h0, small[1], weights_of(1), t5b, target=loss_target[0])

    crossing, tokens, pending = {}, [], []

    def ship(after, then=None):
        key, send_sems, recv_sems, grads, lands = pending.pop()
        grads, from_sibling = pair_exchange_wait(send_sems, recv_sems, grads, lands, after, name="pair_exchange_wait")
        sums = _per_shape(lambda gs, bs: add_halves(gs, bs, cidx, name="add_halves"), grads, from_sibling)
        if then is None:
            send_sems, recv_sems, sums, lands, token = chip_exchange_start(sums, name="chip_exchange_start")
            crossing[key] = (send_sems, recv_sems, sums, lands)
            return token
        crossing[key], pair, token = exchange_start(sums, then[1], name="exchange_start")
        pending.append((then[0],) + pair)
        return token

    def reduce_of(l):
        def emit(group, grads):
            grads = list(grads)
            if pending:
                token = ship([grads[0]], then=((l, group), grads))
            else:
                send_sems, recv_sems, grads, lands, token = pair_exchange_start(grads, name="pair_exchange_start")
                pending.append(((l, group), send_sems, recv_sems, grads, lands))
            tokens.append(token)
            return token
        return emit

    def finish(layer, after, filled=None):
        sent = {}
        for group in order:
            send_sems, recv_sems, sums, lands = crossing[(layer, group)]
            sums, got = chip_exchange_wait(send_sems, recv_sems, sums, lands, after, name="chip_exchange_wait")
            halves = _per_shape(lambda ss, bs: add_chips(ss, bs, sidx, name="add_chips"), sums, got)
            sent[group] = pair_send_start(halves, name="pair_send_start")
            after = [sent[group][4]]
        out = {}
        for group in order:
            send_sems, recv_sems, halves, lands, _ = sent[group]
            halves, theirs = pair_send_wait(send_sems, recv_sems, halves, lands, after, name="pair_send_wait")
            names = GROUPS[group]
            res = _per_shape(
                lambda ws, ms, vs, a, b, *f: adamw_layer(ws, ms, vs, a, b, cidx, layer, list(f[0]) if f else None, name="adamw_layer"),
                *([[w[n] for n in names], [m[n] for n in names], [v[n] for n in names], halves, theirs]
                  + ([[filled[n] for n in names]] if filled is not None else [])))
            out.update(zip(names, res))
            after = [res[-1][0]]
        return out

    dy, dy_bf, small1, dt5_1 = layer_bwd(dy, dy_bf, saved1, small[1], t5b, reduce_of(1))
    grad_x, _, small0, dt5_0 = layer_bwd(dy, dy_bf, saved0, small[0], t5b, reduce_of(0), dep=tokens[-1])
    done1 = finish(1, [ship([grad_x])])

    smalls = [small0, small1]
    dt5 = t5_table_grad(dt5_0, dt5_1, name="t5_table_grad").reshape(32, 8)
    local_small = {n: jnp.stack([smalls[l][n].reshape(w[n].shape[1:]) for l in range(DEPTH)]) for n in SMALL}
    local_small["t5_rel_table"] = dt5
    local_small["loss"] = loss_row[0, 0:1]
    total = allreduce_small(_pack_small(local_small), name="allreduce_small")
    small_grads = _unpack_small(total, w)
    small_done = adamw_small([w[n] for n in SMALL_ALL], [small_grads[n] for n in SMALL_ALL], [m[n] for n in SMALL_ALL],
                             [v[n] for n in SMALL_ALL], name="adamw_small")

    grad, delta, new_m, new_v = {}, {}, {}, {}
    for n, done in finish(0, [small_done[0][0], done1[BIG[-1]][0]], filled=done1).items():
        grad[n], delta[n], new_m[n], new_v[n] = done
    for n, done in zip(SMALL_ALL, small_done):
        grad[n] = small_grads[n]
        delta[n], new_m[n], new_v[n] = done

    return (small_grads["loss"], grad_x[None], *[tr(n, grad[n]) for n in PARAMS], *[tr(n, delta[n]) for n in PARAMS],
            *[tr(n, new_m[n]) for n in PARAMS], *[tr(n, new_v[n]) for n in PARAMS])
```

```python
import math

import jax
import jax.numpy as jnp
import numpy as np
from jax import lax
from jax.experimental import pallas as pl
from jax.experimental.pallas import tpu as pltpu

F32 = jnp.float32
BF16 = jnp.bfloat16

SEQ = 2048
DM = 1024
DFF = 2816
DEPTH = 2
NSH = 4
FSH = DFF // NSH
GRID_W = 64
ROWS = SEQ // GRID_W
NA_HEADS = 8
HD = 64
NA_WR = 8
NA_WC = 16
NA_KEYS = NA_WR * GRID_W
SW_BLK = 128
SW_NB = SEQ // SW_BLK
SW_KEYS = 3 * SW_BLK
ATT_W = 2304
GATE_W = 2048
IN_W = ATT_W + GATE_W
EPS = 1e-6
NEG = -1e30
QK_SCALE = 1.0 / math.sqrt(HD)

ADAM_LR = 0.001
ADAM_B1 = 0.9
ADAM_B2 = 0.999
ADAM_EPS = 1e-08
ADAM_WD = 0.01
ADAM_STEP = 10

VMEM_LIMIT = 56 << 20
MESH = pl.DeviceIdType.MESH

NT = (((1,), (1,)), ((), ()))
TN = (((0,), (0,)), ((), ()))
NN = (((1,), (0,)), ((), ()))


def _dot(a, b, dims=NN):
    return lax.dot_general(a, b, dims, preferred_element_type=F32)


def _params(sem=None):
    return pltpu.CompilerParams(dimension_semantics=sem, vmem_limit_bytes=VMEM_LIMIT)


def _sds(shape, dtype):
    return jax.ShapeDtypeStruct(shape, dtype)


def mm(a, b, *, name, ta=False, tb=False, out_dtype=F32, add=None, scale=None, tm=512, tn=None, tk=None, exact=False,
       dep=None, b_rows=None):
    m, kd = (a.shape[1], a.shape[0]) if ta else a.shape
    if b_rows is None:
        n = b.shape[0] if tb else b.shape[1]
    else:
        n = b_rows[1] if tb else b.shape[1]
        assert tb or (b_rows[1] == kd and (tk or kd) == kd)
    tm, tn, tk = min(tm, m), min(tn or n, n), min(tk or kd, kd)
    nk = kd // tk
    dims = (((0 if ta else 1,), (1 if tb else 0,)), ((), ()))

    def body(*refs):
        a_ref, b_ref = refs[:2]
        add_ref = refs[2] if add is not None else None
        o_ref = refs[-1] if nk == 1 else refs[-2]
        if b_rows is None:
            bv = b_ref[...]
        elif tb:
            bv = b_ref[pl.ds(pl.multiple_of(b_rows[0] + pl.program_id(1) * tn, 16), tn), :]
        else:
            bv = b_ref[b_rows[0]:b_rows[0] + b_rows[1], :]
        if exact:
            part = lax.dot_general(a_ref[...], bv, dims, precision=lax.Precision.HIGHEST, preferred_element_type=F32)
        else:
            part = lax.dot_general(a_ref[...].astype(BF16), bv.astype(BF16), dims, preferred_element_type=F32)

        def finish(r):
            if scale is not None:
                r = r * scale
            if add is not None:
                r = r + add_ref[...]
            o_ref[...] = r.astype(out_dtype)

        if nk == 1:
            finish(part)
        else:
            acc, k = refs[-1], pl.program_id(2)

            @pl.when(k == 0)
            def _():
                acc[...] = part

            @pl.when(k != 0)
            def _():
                acc[...] += part

            pl.when(k == nk - 1)(lambda: finish(acc[...]))

    a_spec = pl.BlockSpec((tk, tm), lambda i, j, k: (k, i)) if ta else pl.BlockSpec((tm, tk), lambda i, j, k: (i, k))
    if b_rows is not None:
        b_spec = pl.BlockSpec(b.shape, lambda i, j, k: (0, 0), pipeline_mode=pl.Buffered(1))
    else:
        b_spec = pl.BlockSpec((tn, tk), lambda i, j, k: (j, k)) if tb else pl.BlockSpec((tk, tn), lambda i, j, k: (k, j))
    o_spec = pl.BlockSpec((tm, tn), lambda i, j, k: (i, j))
    ins, specs = [a, b], [a_spec, b_spec]
    if add is not None:
        ins.append(add)
        specs.append(o_spec)
    if dep is not None:
        ins.append(dep)
        specs.append(pl.BlockSpec(memory_space=pl.ANY))
    return pl.pallas_call(
        body, name=name, out_shape=_sds((m, n), out_dtype), grid=(m // tm, n // tn, nk), in_specs=specs,
        out_specs=o_spec, scratch_shapes=[] if nk == 1 else [pltpu.VMEM((tm, tn), F32)],
        compiler_params=_params(("parallel", "parallel", "arbitrary")))(*ins)


def _rms(x):
    return lax.rsqrt(jnp.mean(x * x, axis=-1, keepdims=True) + EPS)


def mixer_input_fwd(x, gain, w_in_t, gq_na, gk_na, gq_sw, gk_sw, *, name, tm=512):
    def body(x_ref, g_ref, w_ref, gqa_ref, gka_ref, gqs_ref, gks_ref, bd_ref, bd2_ref,
             h_ref, z_ref, zg_ref, qa_ref, ka_ref, va_ref, qs_ref, kv_ref):
        x = x_ref[...]
        h = (x * _rms(x) * g_ref[...]).astype(BF16)
        h_ref[...] = h
        z = _dot(h, w_ref[0:ATT_W, :], NT).astype(BF16)
        z_ref[...] = z
        zg_ref[...] = _dot(h, w_ref[ATT_W:IN_W, :], NT).astype(BF16)
        bd = bd_ref[...]

        def norm(v, g, bdm):
            v = v.astype(F32)
            return v * lax.rsqrt(_group_mean(v * v, bdm) + EPS) * g

        qa_ref[...] = (norm(z[:, 0:512], gqa_ref[...], bd) * QK_SCALE).astype(BF16)
        ka_ref[...] = norm(z[:, 512:1024], gka_ref[...], bd).astype(BF16)
        va_ref[...] = z[:, 1024:1536]
        qs_ref[...] = (norm(z[:, 1536:2048], gqs_ref[...], bd) * QK_SCALE).astype(BF16)
        kv_ref[:, 0:128] = norm(z[:, 2048:2176], gks_ref[...], bd2_ref[...]).astype(BF16)
        kv_ref[:, 128:256] = z[:, 2176:2304]

    s = x.shape[0]
    tile = pl.BlockSpec((tm, DM), lambda i: (i, 0))
    vec = lambda w: pl.BlockSpec((1, w), lambda i: (0, 0))
    att = pl.BlockSpec((tm, 512), lambda i: (i, 0))
    g512 = lambda g: jnp.tile(g.reshape(1, HD), (1, 8))
    q = _sds((s, 512), BF16)
    return pl.pallas_call(
        body, name=name,
        out_shape=(_sds((s, DM), BF16), _sds((s, ATT_W), BF16), _sds((s, GATE_W), BF16), q, q, q, q, _sds((s, 256), BF16)),
        grid=(s // tm,),
        in_specs=[tile, vec(DM), pl.BlockSpec((IN_W, DM), lambda i: (0, 0), pipeline_mode=pl.Buffered(1)),
                  vec(512), vec(512), vec(512), vec(128), pl.BlockSpec((512, 512), lambda i: (0, 0)),
                  pl.BlockSpec((128, 128), lambda i: (0, 0))],
        out_specs=(tile, pl.BlockSpec((tm, ATT_W), lambda i: (i, 0)), pl.BlockSpec((tm, GATE_W), lambda i: (i, 0)),
                   att, att, att, att, pl.BlockSpec((tm, 256), lambda i: (i, 0))),
        compiler_params=_params(("parallel",)))(
            x, gain, w_in_t, g512(gq_na), g512(gk_na), g512(gq_sw), jnp.tile(gk_sw.reshape(1, HD), (1, 2)),
            _block_diag(512), _block_diag(128))


def _rms_bwd_math(dh, x, gain):
    r = _rms(x)
    xh = x * r
    dgain = jnp.sum(dh * xh, axis=0, keepdims=True)
    dxn = dh * gain
    dx = r * (dxn - xh * jnp.mean(dxn * xh, axis=-1, keepdims=True))
    return dx, dgain


def mixer_input_bwd(dz, dzg, w_in_t, x, gain, dres, *, name, tm=512, dep=None):
    def body(dz_ref, dzg_ref, w_ref, x_ref, g_ref, dres_ref, *rest):
        dx_ref, dxb_ref, dg_ref = rest[-3:]

        @pl.when(pl.program_id(0) == 0)
        def _():
            dg_ref[...] = jnp.zeros_like(dg_ref)

        dh = _dot(dz_ref[...], w_ref[0:ATT_W, :]) + _dot(dzg_ref[...], w_ref[ATT_W:IN_W, :])
        dx, dg = _rms_bwd_math(dh, x_ref[...], g_ref[...])
        dx = dres_ref[...] + dx
        dx_ref[...] = dx
        dxb_ref[...] = dx.astype(BF16)
        dg_ref[...] += dg

    s = x.shape[0]
    tile = pl.BlockSpec((tm, DM), lambda i: (i, 0))
    vec = pl.BlockSpec((1, DM), lambda i: (0, 0))
    ins, specs = _with_dep(
        [dz, dzg, w_in_t, x, gain, dres],
        [pl.BlockSpec((tm, ATT_W), lambda i: (i, 0)), pl.BlockSpec((tm, GATE_W), lambda i: (i, 0)),
         pl.BlockSpec((IN_W, DM), lambda i: (0, 0), pipeline_mode=pl.Buffered(1)), tile, vec, tile], dep)
    return pl.pallas_call(
        body, name=name, out_shape=(_sds((s, DM), F32), _sds((s, DM), BF16), _sds((1, DM), F32)), grid=(s // tm,),
        in_specs=specs, out_specs=(tile, tile, vec), compiler_params=_params(("arbitrary",)))(*ins)


def mixer_input_dw(dz, dzg, h, *, name, tr=256):
    att_tiles = ATT_W // tr

    def body(dz_ref, dzg_ref, h_ref, o_ref):
        t = pl.program_id(0)
        cols = jnp.where(t < att_tiles, dz_ref[...], dzg_ref[...])
        o_ref[...] = _dot(cols, h_ref[...], TN).astype(BF16)

    s = h.shape[0]
    return pl.pallas_call(
        body, name=name, out_shape=_sds((IN_W, DM), BF16), grid=(IN_W // tr,),
        in_specs=[pl.BlockSpec((s, tr), lambda t: (0, jnp.minimum(t, att_tiles - 1))),
                  pl.BlockSpec((s, tr), lambda t: (0, jnp.maximum(t - att_tiles, 0))),
                  pl.BlockSpec((s, DM), lambda t: (0, 0))],
        out_specs=pl.BlockSpec((tr, DM), lambda t: (t, 0)), compiler_params=_params(("parallel",)))(dz, dzg, h)


def mixer_output_dw(merged, dy, o_na, dpa, o_sw, dps, *, name):
    def body(m_ref, dy_ref, ona_ref, dpa_ref, osw_ref, dps_ref, gwo_ref, gna_ref, gsw_ref):
        gwo_ref[...] = _dot(m_ref[...], dy_ref[...], TN).astype(BF16)
        width = DM // NSH
        for o_ref, dp_ref, out_ref in ((ona_ref, dpa_ref, gna_ref), (osw_ref, dps_ref, gsw_ref)):
            full = _dot(o_ref[...], dp_ref[...], TN).astype(BF16)
            for j in range(NSH):
                out_ref[j] = full[:, j * width:(j + 1) * width]

    branch = _sds((NSH, 512, DM // NSH), BF16)
    return pl.pallas_call(
        body, name=name, out_shape=(_sds((DM, DM), BF16), branch, branch), compiler_params=_params())(
            merged, dy, o_na, dpa, o_sw, dps)


def _with_dep(ins, specs, dep):
    if dep is None:
        return ins, specs
    return ins + [dep], specs + [pl.BlockSpec(memory_space=pl.ANY)]


def _resident_weight():
    return pl.BlockSpec((DFF, DM), lambda i: (0, 0), pipeline_mode=pl.Buffered(1))


def ffn_fwd(x, gain, wg, wu, wd, target=None, *, name, tm=512):
    def body(x_ref, g_ref, wg_ref, wu_ref, wd_ref, *rest):
        h_ref, gg_ref, uu_ref = rest[-3:]
        x = x_ref[...]
        h = (x * _rms(x) * g_ref[...]).astype(BF16)
        h_ref[...] = h
        gg = _dot(h, wg_ref[...], NT)
        uu = _dot(h, wu_ref[...], NT)
        gg_ref[...] = gg.astype(BF16)
        uu_ref[...] = uu.astype(BF16)
        act = (gg * jax.nn.sigmoid(gg) * uu).astype(BF16)
        y = x + 0.5 * _dot(act, wd_ref[...])
        if target is None:
            rest[0][...] = y
            return
        t_ref, dy_ref, dyb_ref, l_ref = rest[:4]

        @pl.when(pl.program_id(0) == 0)
        def _():
            l_ref[...] = jnp.zeros_like(l_ref)

        err = y - t_ref[...]
        dy = err * (1.0 / DM)
        dy_ref[...] = dy
        dyb_ref[...] = dy.astype(BF16)
        l_ref[...] += 0.5 * jnp.sum(jnp.mean(err * err, axis=-1, keepdims=True), axis=0, keepdims=True)

    s = x.shape[0]
    tile = pl.BlockSpec((tm, DM), lambda i: (i, 0))
    hid = pl.BlockSpec((tm, DFF), lambda i: (i, 0))
    w = _resident_weight()
    saved_shapes = (_sds((s, DM), BF16), _sds((s, DFF), BF16), _sds((s, DFF), BF16))
    ins, specs = [x, gain, wg, wu, wd], [tile, pl.BlockSpec((1, DM), lambda i: (0, 0)), w, w, w]
    if target is None:
        head_shapes, head_specs = (_sds((s, DM), F32),), (tile,)
    else:
        ins, specs = ins + [target], specs + [tile]
        head_shapes = (_sds((s, DM), F32), _sds((s, DM), BF16), _sds((1, 128), F32))
        head_specs = (tile, tile, pl.BlockSpec((1, 128), lambda i: (0, 0)))
    return pl.pallas_call(
        body, name=name, out_shape=head_shapes + saved_shapes, grid=(s // tm,), in_specs=specs,
        out_specs=head_specs + (tile, hid, hid),
        compiler_params=_params(("parallel",) if target is None else ("arbitrary",)))(*ins)


def ffn_bwd_tokens(dy, x, gain, gg, uu, wg, wu, wd, *, name, tm=256, dep=None):
    def body(dy_ref, x_ref, g_ref, gg_ref, uu_ref, wg_ref, wu_ref, wd_ref, *rest):
        dx_ref, dxb_ref, dgain_ref, act_ref, dg_ref, du_ref = rest[-6:]

        @pl.when(pl.program_id(0) == 0)
        def _():
            dgain_ref[...] = jnp.zeros_like(dgain_ref)

        dy = dy_ref[...]
        dact = _dot((0.5 * dy).astype(BF16), wd_ref[...], NT)
        g = gg_ref[...].astype(F32)
        u = uu_ref[...].astype(F32)
        sg = jax.nn.sigmoid(g)
        silu = g * sg
        act_ref[...] = (silu * u).astype(BF16)
        dg = (dact * u * (sg * (1.0 + g * (1.0 - sg)))).astype(BF16)
        du = (dact * silu).astype(BF16)
        dg_ref[...] = dg
        du_ref[...] = du
        dx, dgain = _rms_bwd_math(_dot(dg, wg_ref[...]) + _dot(du, wu_ref[...]), x_ref[...], g_ref[...])
        dx = dy + dx
        dx_ref[...] = dx
        dxb_ref[...] = dx.astype(BF16)
        dgain_ref[...] += dgain

    s = x.shape[0]
    tile = pl.BlockSpec((tm, DM), lambda i: (i, 0))
    vec = pl.BlockSpec((1, DM), lambda i: (0, 0))
    hid = pl.BlockSpec((tm, DFF), lambda i: (i, 0))
    hshape = _sds((s, DFF), BF16)
    w = _resident_weight()
    ins, specs = _with_dep([dy, x, gain, gg, uu, wg, wu, wd], [tile, tile, vec, hid, hid, w, w, w], dep)
    return pl.pallas_call(
        body, name=name, out_shape=(_sds((s, DM), F32), _sds((s, DM), BF16), _sds((1, DM), F32), hshape, hshape, hshape),
        grid=(s // tm,), in_specs=specs, out_specs=(tile, tile, vec, hid, hid, hid),
        compiler_params=_params(("arbitrary",)))(*ins)


def ffn_bwd_weights(h, dy, act, dg, du, *, name, tf=256):
    def body(h_ref, dy_ref, act_ref, dg_ref, du_ref, gwg_ref, gwu_ref, gwd_ref):
        h = h_ref[...]
        gwg_ref[...] = _dot(dg_ref[...], h, TN).astype(BF16)
        gwu_ref[...] = _dot(du_ref[...], h, TN).astype(BF16)
        gwd_ref[...] = (0.5 * _dot(act_ref[...], dy_ref[...], TN)).astype(BF16)

    s = h.shape[0]
    full = pl.BlockSpec((s, DM), lambda f: (0, 0))
    hid = pl.BlockSpec((s, tf), lambda f: (0, f))
    wt = pl.BlockSpec((tf, DM), lambda f: (f, 0))
    wshape = _sds((DFF, DM), BF16)
    return pl.pallas_call(
        body, name=name, out_shape=(wshape, wshape, wshape), grid=(DFF // tf,), in_specs=[full, full, hid, hid, hid],
        out_specs=(wt, wt, wt), compiler_params=_params(("parallel",)))(h, dy, act, dg, du)


def _group_mean(v, bd):
    hi = v.astype(BF16)
    lo = (v - hi.astype(F32)).astype(BF16)
    return _dot(hi, bd) + _dot(lo, bd)


def _block_diag(width):
    idx = np.arange(width) // HD
    return jnp.asarray((idx[:, None] == idx[None, :]).astype(np.float32) / HD, dtype=BF16)


def qknorm_bwd(z, dqa, dka, dva, dqs, dkv, gq_na, gk_na, gq_sw, gk_sw, *, name, tm=256):
    def body(zq_ref, zk_ref, zs_ref, zkv_ref, dqa_ref, dka_ref, dva_ref, dqs_ref, dkv_ref, gqa_ref, gka_ref, gqs_ref,
             gks_ref, bd_ref, bd2_ref, dz_ref, dgqa_ref, dgka_ref, dgqs_ref, dgks_ref):
        @pl.when(pl.program_id(0) == 0)
        def _():
            dgqa_ref[...] = jnp.zeros_like(dgqa_ref)
            dgka_ref[...] = jnp.zeros_like(dgka_ref)
            dgqs_ref[...] = jnp.zeros_like(dgqs_ref)
            dgks_ref[...] = jnp.zeros_like(dgks_ref)

        bd = bd_ref[...]

        def bwd(x, dy, g, bdm, dg_ref):
            x = x.astype(F32)
            r = lax.rsqrt(_group_mean(x * x, bdm) + EPS)
            xh = x * r
            dg_ref[...] += jnp.sum(dy * xh, axis=0, keepdims=True)
            dxn = dy * g
            return r * (dxn - xh * _group_mean(dxn * xh, bdm))

        dz_ref[:, 0:512] = bwd(zq_ref[...], dqa_ref[...] * QK_SCALE, gqa_ref[...], bd, dgqa_ref).astype(BF16)
        dz_ref[:, 512:1024] = bwd(zk_ref[...], dka_ref[...], gka_ref[...], bd, dgka_ref).astype(BF16)
        dz_ref[:, 1024:1536] = dva_ref[...].astype(BF16)
        dz_ref[:, 1536:2048] = bwd(zs_ref[...], dqs_ref[...] * QK_SCALE, gqs_ref[...], bd, dgqs_ref).astype(BF16)
        dkv = dkv_ref[...]
        dz_ref[:, 2048:2176] = bwd(zkv_ref[:, 0:128], dkv[:, 0:128], gks_ref[...], bd2_ref[...], dgks_ref).astype(BF16)
        dz_ref[:, 2176:2304] = dkv[:, 128:256].astype(BF16)

    s = z.shape[0]
    col = lambda j: pl.BlockSpec((tm, 512), lambda i, j=j: (i, j))
    t512 = pl.BlockSpec((tm, 512), lambda i: (i, 0))
    t256 = pl.BlockSpec((tm, 256), lambda i: (i, 0))
    vec = lambda w: pl.BlockSpec((1, w), lambda i: (0, 0))
    g512 = lambda g: jnp.tile(g.reshape(1, HD), (1, 8))
    return pl.pallas_call(
        body, name=name,
        out_shape=(_sds((s, ATT_W), BF16), _sds((1, 512), F32), _sds((1, 512), F32), _sds((1, 512), F32), _sds((1, 128), F32)),
        grid=(s // tm,),
        in_specs=[col(0), col(1), col(3), pl.BlockSpec((tm, 256), lambda i: (i, 8)), t512, t512, t512, t512, t256,
                  vec(512), vec(512), vec(512), vec(128), pl.BlockSpec((512, 512), lambda i: (0, 0)),
                  pl.BlockSpec((128, 128), lambda i: (0, 0))],
        out_specs=(pl.BlockSpec((tm, ATT_W), lambda i: (i, 0)), vec(512), vec(512), vec(512), vec(128)),
        compiler_params=_params(("arbitrary",)))(
            z, z, z, z, dqa, dka, dva, dqs, dkv, g512(gq_na), g512(gk_na), g512(gq_sw),
            jnp.tile(gk_sw.reshape(1, HD), (1, 2)), _block_diag(512), _block_diag(128))


def _na_row_start(r):
    return jnp.clip(r - NA_WR // 2, 0, ROWS - NA_WR)


def na_bias_table(rpb, *, name):
    t = jnp.pad(rpb, ((0, 0), (0, 2), (0, HD - (2 * NA_WC - 1))))
    pairs = jnp.concatenate([t[:, :16], t[:, 1:17]], axis=-1).reshape(NA_HEADS, 16, 1, 128)

    def body(t_ref, o_ref):
        p = pl.program_id(0)
        q = lax.broadcasted_iota(jnp.int32, (GRID_W, 128), 0)
        kc = lax.broadcasted_iota(jnp.int32, (GRID_W, 128), 1) & (GRID_W - 1)
        cs = jnp.clip(q - NA_WC // 2, 0, GRID_W - NA_WC)
        ok = (kc >= cs) & (kc < cs + NA_WC)
        for h in range(NA_HEADS):
            for pr in range(NA_WR // 2):
                x = jnp.broadcast_to(t_ref[h, 2 * pr - p + NA_WR - 1], (GRID_W, 128))
                b = pltpu.roll(x, 128 - (NA_WC - 1), 1, stride=1, stride_axis=0)
                o_ref[h, :, 128 * pr:128 * pr + 128] = jnp.where(ok, b, NEG)

    return pl.pallas_call(
        body, name=name, out_shape=_sds((NA_WR, NA_HEADS, GRID_W, NA_KEYS), F32), grid=(NA_WR,),
        in_specs=[pl.BlockSpec((NA_HEADS, 16, 1, 128), lambda p: (0, 0, 0, 0))],
        out_specs=pl.BlockSpec((None, NA_HEADS, GRID_W, NA_KEYS), lambda p: (p, 0, 0, 0)),
        compiler_params=_params(("parallel",)))(pairs)


def _lane_halves():
    lane = lax.broadcasted_iota(jnp.int32, (1, 128), 1)
    return lane < HD


def na_fwd(q, k, v, bias, *, name):
    def body(q_ref, k_ref, v_ref, b_ref, o_ref, lse_ref):
        r = pl.program_id(0)
        off = pl.multiple_of(_na_row_start(r) * GRID_W, GRID_W)
        first = _lane_halves()
        sels = [first, jnp.logical_not(first)]
        lanes = [slice(128 * j, 128 * j + 128) for j in range(NA_HEADS // 2)]
        q2s = [q_ref[:, l] for l in lanes]
        k2s = [k_ref[pl.ds(off, NA_KEYS), l] for l in lanes]
        v2s = [v_ref[pl.ds(off, NA_KEYS), l] for l in lanes]
        scores = []
        for h in range(NA_HEADS):
            j, half = divmod(h, 2)
            scores.append(_dot(jnp.where(sels[half], q2s[j], jnp.zeros_like(q2s[j])), k2s[j], NT))
        probs, lses = [], []
        for h in range(NA_HEADS):
            b = b_ref[h]
            s = jnp.where(b > 0.5 * NEG, scores[h] + b, NEG)
            m = jnp.max(s, axis=-1, keepdims=True)
            e = jnp.exp(s - m)
            l = jnp.sum(e, axis=-1, keepdims=True)
            probs.append((e / l).astype(BF16))
            lses.append(m + jnp.log(l))
        for j in range(NA_HEADS // 2):
            zero = jnp.zeros_like(v2s[j])
            o2 = (_dot(probs[2 * j], jnp.where(sels[0], v2s[j], zero))
                  + _dot(probs[2 * j + 1], jnp.where(sels[1], v2s[j], zero)))
            o_ref[:, lanes[j]] = o2.astype(BF16)
        lse_ref[...] = jnp.concatenate(lses, axis=1)

    s_tok = q.shape[0]
    full = pl.BlockSpec((s_tok, 512), lambda r: (0, 0))
    return pl.pallas_call(
        body, name=name, out_shape=(_sds((s_tok, 512), BF16), _sds((s_tok, NA_HEADS), F32)), grid=(ROWS,),
        in_specs=[pl.BlockSpec((GRID_W, 512), lambda r: (r, 0)), full, full,
                  pl.BlockSpec((None, NA_HEADS, GRID_W, NA_KEYS), lambda r: (r - _na_row_start(r), 0, 0, 0))],
        out_specs=(pl.BlockSpec((GRID_W, 512), lambda r: (r, 0)), pl.BlockSpec((GRID_W, NA_HEADS), lambda r: (r, 0))),
        compiler_params=_params(("parallel",)))(q, k, v, bias)


def na_bwd(q, k, v, o, do, lse, bias, *, name):
    def body(q_ref, k_ref, v_ref, o_ref, do_ref, lse_ref, b_ref, dq_ref, dk_ref, dv_ref, db_ref):
        r = pl.program_id(0)

        @pl.when(r == 0)
        def _():
            dk_ref[...] = jnp.zeros_like(dk_ref)
            dv_ref[...] = jnp.zeros_like(dv_ref)

        @pl.when((r <= NA_WR // 2) | (r > ROWS - NA_WR // 2))
        def _():
            db_ref[...] = jnp.zeros_like(db_ref)

        off = pl.multiple_of(_na_row_start(r) * GRID_W, GRID_W)
        first = _lane_halves()
        sels = [first, jnp.logical_not(first)]
        lanes = [slice(128 * j, 128 * j + 128) for j in range(NA_HEADS // 2)]
        q2s = [q_ref[:, l] for l in lanes]
        k2s = [k_ref[pl.ds(off, NA_KEYS), l] for l in lanes]
        v2s = [v_ref[pl.ds(off, NA_KEYS), l] for l in lanes]
        do2s = [do_ref[:, l] for l in lanes]
        prods = [do2s[j].astype(F32) * o_ref[:, lanes[j]].astype(F32) for j in range(NA_HEADS // 2)]
        lse = lse_ref[...]
        qhs, dohs, scores, dps = [], [], [], []
        for h in range(NA_HEADS):
            j, half = divmod(h, 2)
            qhs.append(jnp.where(sels[half], q2s[j], jnp.zeros_like(q2s[j])))
            dohs.append(jnp.where(sels[half], do2s[j], jnp.zeros_like(do2s[j])))
            scores.append(_dot(qhs[h], k2s[j], NT))
            dps.append(_dot(dohs[h], v2s[j], NT))
        pbs, dsbs = [], []
        for h in range(NA_HEADS):
            j, half = divmod(h, 2)
            b = b_ref[h]
            s = jnp.where(b > 0.5 * NEG, scores[h] + b, NEG)
            p = jnp.exp(s - lse[:, h:h + 1])
            delta = jnp.sum(jnp.where(sels[half], prods[j], 0.0), axis=-1, keepdims=True)
            ds = p * (dps[h] - delta)
            db_ref[h] += ds
            pbs.append(p.astype(BF16))
            dsbs.append(ds.astype(BF16))
        for j in range(NA_HEADS // 2):
            a, b = 2 * j, 2 * j + 1
            zero = jnp.zeros_like(k2s[j])
            dq_ref[:, lanes[j]] = (_dot(dsbs[a], jnp.where(sels[0], k2s[j], zero))
                                   + _dot(dsbs[b], jnp.where(sels[1], k2s[j], zero)))
            dk_ref[pl.ds(off, NA_KEYS), lanes[j]] += _dot(dsbs[a], qhs[a], TN) + _dot(dsbs[b], qhs[b], TN)
            dv_ref[pl.ds(off, NA_KEYS), lanes[j]] += _dot(pbs[a], dohs[a], TN) + _dot(pbs[b], dohs[b], TN)

    s_tok = q.shape[0]
    full = pl.BlockSpec((s_tok, 512), lambda r: (0, 0))
    row = pl.BlockSpec((GRID_W, 512), lambda r: (r, 0))
    bias_spec = pl.BlockSpec((None, NA_HEADS, GRID_W, NA_KEYS), lambda r: (r - _na_row_start(r), 0, 0, 0))
    return pl.pallas_call(
        body, name=name,
        out_shape=(_sds((s_tok, 512), F32), _sds((s_tok, 512), F32), _sds((s_tok, 512), F32),
                   _sds((NA_WR, NA_HEADS, GRID_W, NA_KEYS), F32)),
        grid=(ROWS,),
        in_specs=[row, full, full, row, row, pl.BlockSpec((GRID_W, NA_HEADS), lambda r: (r, 0)), bias_spec],
        out_specs=(row, full, full, bias_spec), compiler_params=_params(("arbitrary",)))(q, k, v, o, do, lse, bias)


def t5_bucket_map():
    rel = np.arange(SW_KEYS)[None, :] - SW_BLK - np.arange(SW_BLK)[:, None]
    nb = 16
    max_exact = nb // 2
    n = np.abs(rel)
    large = max_exact + (np.log(np.maximum(n, 1) / max_exact) / np.log(128 / max_exact) * (nb - max_exact)).astype(np.int32)
    large = np.minimum(large, nb - 1)
    return ((rel > 0) * nb + np.where(n < max_exact, n, large)).astype(np.int32)


def t5_bias(table, *, name):
    rel = np.arange(-SW_BLK, SW_BLK + 1)
    nb, max_exact = 16, 8
    n = np.abs(rel)
    large = max_exact + (np.log(np.maximum(n, 1) / max_exact) / np.log(128 / max_exact) * (nb - max_exact)).astype(np.int32)
    bucket = ((rel > 0) * nb + np.where(n < max_exact, n, np.minimum(large, nb - 1))).astype(np.int32)
    u = jnp.pad(table[jnp.asarray(bucket)].T, ((0, 0), (0, SW_KEYS - bucket.shape[0]))).reshape(8, 1, SW_KEYS)

    def body(u_ref, o_ref):
        for h in range(8):
            x = jnp.broadcast_to(u_ref[h], (SW_BLK, SW_KEYS))
            o_ref[h] = pltpu.roll(x, 0, 1, stride=1, stride_axis=0)

    return pl.pallas_call(body, name=name, out_shape=_sds((8, SW_BLK, SW_KEYS), F32), compiler_params=_params())(u)


def _sw_valid(n):
    a = lax.broadcasted_iota(jnp.int32, (SW_BLK, SW_KEYS), 0)
    j = lax.broadcasted_iota(jnp.int32, (SW_BLK, SW_KEYS), 1)
    kpos = (n - 1) * SW_BLK + j
    return (jnp.abs(j - SW_BLK - a) <= SW_BLK) & (kpos >= 0) & (kpos < SEQ)


def _dup_group(x2, g, first):
    rolled = pltpu.roll(x2, HD, 1)
    return jnp.where(first, x2, rolled) if g == 0 else jnp.where(first, rolled, x2)


def sw_fwd(q, kv, t5, sink, *, name):
    def body(q_ref, kv_ref, t5_ref, sink_ref, o_ref, lse_ref):
        n = pl.program_id(0)
        off = pl.multiple_of(n * SW_BLK, SW_BLK)
        first = _lane_halves()
        sels = [first, jnp.logical_not(first)]
        valid = _sw_valid(n)
        k2 = kv_ref[pl.ds(off, SW_KEYS), 0:128]
        v2 = kv_ref[pl.ds(off, SW_KEYS), 128:256]
        kk = [_dup_group(k2, g, first) for g in range(2)]
        vv = [_dup_group(v2, g, first) for g in range(2)]
        q2s = [q_ref[:, 128 * j:128 * j + 128] for j in range(4)]
        scores = []
        for h in range(8):
            j, half = divmod(h, 2)
            scores.append(_dot(jnp.where(sels[half], q2s[j], jnp.zeros_like(q2s[j])), kk[j // 2], NT))
        probs, lses = [], []
        for h in range(8):
            s = jnp.where(valid, scores[h] + t5_ref[h], NEG)
            snk = sink_ref[h]
            m = jnp.maximum(jnp.max(s, axis=-1, keepdims=True), snk)
            e = jnp.exp(s - m)
            den = jnp.sum(e, axis=-1, keepdims=True) + jnp.exp(snk - m)
            probs.append((e / den).astype(BF16))
            lses.append(m + jnp.log(den))
        outs = []
        for j in range(4):
            vg = vv[j // 2]
            zero = jnp.zeros_like(vg)
            outs.append(_dot(probs[2 * j], jnp.where(sels[0], vg, zero)) + _dot(probs[2 * j + 1], jnp.where(sels[1], vg, zero)))
        o_ref[...] = jnp.concatenate(outs, axis=1).astype(BF16)
        lse_ref[...] = jnp.concatenate(lses, axis=1)

    s_tok = q.shape[0]
    blk = pl.BlockSpec((SW_BLK, 512), lambda n: (n, 0))
    return pl.pallas_call(
        body, name=name, out_shape=(_sds((s_tok, 512), BF16), _sds((s_tok, 8), F32)), grid=(SW_NB,),
        in_specs=[blk, pl.BlockSpec(kv.shape, lambda n: (0, 0)), pl.BlockSpec((8, SW_BLK, SW_KEYS), lambda n: (0, 0, 0)),
                  pl.BlockSpec(memory_space=pltpu.SMEM)],
        out_specs=(blk, pl.BlockSpec((SW_BLK, 8), lambda n: (n, 0))), compiler_params=_params(("parallel",)))(q, kv, t5, sink)


def sw_bwd(q, kv, o, do, lse, t5, sink, *, name):
    def body(q_ref, kv_ref, o_ref, do_ref, lse_ref, t5_ref, sink_ref, dq_ref, dkv_ref, dt5_ref, dsink_ref):
        n = pl.program_id(0)

        @pl.when(n == 0)
        def _():
            dkv_ref[...] = jnp.zeros_like(dkv_ref)
            dt5_ref[...] = jnp.zeros_like(dt5_ref)
            dsink_ref[...] = jnp.zeros_like(dsink_ref)

        off = pl.multiple_of(n * SW_BLK, SW_BLK)
        first = _lane_halves()
        sels = [first, jnp.logical_not(first)]
        valid = _sw_valid(n)
        k2 = kv_ref[pl.ds(off, SW_KEYS), 0:128]
        v2 = kv_ref[pl.ds(off, SW_KEYS), 128:256]
        kk = [_dup_group(k2, g, first) for g in range(2)]
        vv = [_dup_group(v2, g, first) for g in range(2)]
        lanes = [slice(128 * j, 128 * j + 128) for j in range(4)]
        q2s = [q_ref[:, l] for l in lanes]
        do2s = [do_ref[:, l] for l in lanes]
        prods = [do2s[j].astype(F32) * o_ref[:, lanes[j]].astype(F32) for j in range(4)]
        lse = lse_ref[...]
        qhs, dohs, scores, dps = [], [], [], []
        for h in range(8):
            j, half = divmod(h, 2)
            qhs.append(jnp.where(sels[half], q2s[j], jnp.zeros_like(q2s[j])))
            dohs.append(jnp.where(sels[half], do2s[j], jnp.zeros_like(do2s[j])))
            scores.append(_dot(qhs[h], kk[j // 2], NT))
            dps.append(_dot(dohs[h], vv[j // 2], NT))
        pbs, dsbs, dss, dsinks = [], [], [], []
        for h in range(8):
            j, half = divmod(h, 2)
            s = jnp.where(valid, scores[h] + t5_ref[h], NEG)
            lse_h = lse[:, h:h + 1]
            p = jnp.exp(s - lse_h)
            delta = jnp.sum(jnp.where(sels[half], prods[j], 0.0), axis=-1, keepdims=True)
            ds = p * (dps[h] - delta)
            dss.append(ds)
            dsinks.append(-jnp.sum(jnp.exp(sink_ref[h] - lse_h) * delta, axis=0, keepdims=True))
            pbs.append(p.astype(BF16))
            dsbs.append(ds.astype(BF16))
        dt5_ref[...] += jnp.stack(dss)
        dsink_ref[...] += jnp.concatenate(dsinks, axis=1)
        dqs = []
        for j in range(4):
            a, b = 2 * j, 2 * j + 1
            zero = jnp.zeros_like(kk[j // 2])
            dqs.append(_dot(dsbs[a], jnp.where(sels[0], kk[j // 2], zero)) + _dot(dsbs[b], jnp.where(sels[1], kk[j // 2], zero)))
        dq_ref[...] = jnp.concatenate(dqs, axis=1)
        dk_groups, dv_groups = [], []
        for g in range(2):
            dkk = sum(_dot(dsbs[h], qhs[h], TN) for h in range(4 * g, 4 * g + 4))
            dvv = sum(_dot(pbs[h], dohs[h], TN) for h in range(4 * g, 4 * g + 4))
            dk_groups.append(dkk + pltpu.roll(dkk, HD, 1))
            dv_groups.append(dvv + pltpu.roll(dvv, HD, 1))
        dkv_ref[pl.ds(off, SW_KEYS), :] += jnp.concatenate(
            [jnp.where(first, dk_groups[0], dk_groups[1]), jnp.where(first, dv_groups[0], dv_groups[1])], axis=1)

    s_tok = q.shape[0]
    blk = pl.BlockSpec((SW_BLK, 512), lambda n: (n, 0))
    kv_spec = pl.BlockSpec(kv.shape, lambda n: (0, 0))
    t5_spec = pl.BlockSpec((8, SW_BLK, SW_KEYS), lambda n: (0, 0, 0))
    vec = pl.BlockSpec((1, 8), lambda n: (0, 0))
    return pl.pallas_call(
        body, name=name,
        out_shape=(_sds((s_tok, 512), F32), _sds(kv.shape, F32), _sds((8, SW_BLK, SW_KEYS), F32), _sds((1, 8), F32)),
        grid=(SW_NB,), in_specs=[blk, kv_spec, blk, blk, pl.BlockSpec((SW_BLK, 8), lambda n: (n, 0)), t5_spec,
                                 pl.BlockSpec(memory_space=pltpu.SMEM)],
        out_specs=(blk, kv_spec, t5_spec, vec), compiler_params=_params(("arbitrary",)))(q, kv, o, do, lse, t5, sink)


def mixer_output_fwd(o_na, o_sw, zg, bias, wa, ws, wo, res, *, name, tm=512):
    def body(ona_ref, osw_ref, z0_ref, z1_ref, b0_ref, b1_ref, wa_ref, ws_ref, wo_ref, res_ref, y_ref, pa_ref, ps_ref, m_ref):
        cols = lambda o, w4_ref: jnp.concatenate([_dot(o, w4_ref[j]) for j in range(NSH)], axis=1)
        pa = cols(ona_ref[...], wa_ref).astype(BF16)
        ps = cols(osw_ref[...], ws_ref).astype(BF16)
        pa_ref[...] = pa
        ps_ref[...] = ps
        g0 = jax.nn.sigmoid(z0_ref[...] + b0_ref[...])
        g1 = jax.nn.sigmoid(z1_ref[...] + b1_ref[...])
        merged = (g0 * pa + g1 * ps).astype(BF16)
        m_ref[...] = merged
        y_ref[...] = res_ref[...] + _dot(merged, wo_ref[...])

    s = zg.shape[0]
    half = lambda j: pl.BlockSpec((tm, DM), lambda i, j=j: (i, j))
    bvec = lambda j: pl.BlockSpec((1, DM), lambda i, j=j: (0, j))
    att = pl.BlockSpec((tm, 512), lambda i: (i, 0))
    whole = lambda a: pl.BlockSpec(a.shape, lambda i: (0,) * a.ndim, pipeline_mode=pl.Buffered(1))
    act = _sds((s, DM), BF16)
    return pl.pallas_call(
        body, name=name, out_shape=(_sds((s, DM), F32), act, act, act), grid=(s // tm,),
        in_specs=[att, att, half(0), half(1), bvec(0), bvec(1), whole(wa), whole(ws), whole(wo), half(0)],
        out_specs=(half(0),) * 4, compiler_params=_params(("parallel",)))(o_na, o_sw, zg, zg, bias, bias, wa, ws, wo, res)


def mixer_output_bwd(dy, zg, bias, pa, ps, wa, ws, wo, *, name, tm=512, dep=None):
    def body(dy_ref, z0_ref, z1_ref, b0_ref, b1_ref, pa_ref, ps_ref, wa_ref, ws_ref, wo_ref, *rest):
        dpa_ref, dps_ref, dz_ref, db_ref, dona_ref, dosw_ref = rest[-6:]

        @pl.when(pl.program_id(0) == 0)
        def _():
            db_ref[...] = jnp.zeros_like(db_ref)

        dm = _dot(dy_ref[...].astype(BF16), wo_ref[...], NT)
        g0 = jax.nn.sigmoid(z0_ref[...] + b0_ref[...])
        g1 = jax.nn.sigmoid(z1_ref[...] + b1_ref[...])
        dpa = (dm * g0).astype(BF16)
        dps = (dm * g1).astype(BF16)
        dpa_ref[...] = dpa
        dps_ref[...] = dps
        dz0 = dm * pa_ref[...] * g0 * (1.0 - g0)
        dz1 = dm * ps_ref[...] * g1 * (1.0 - g1)
        dz_ref[:, 0:DM] = dz0.astype(BF16)
        dz_ref[:, DM:2 * DM] = dz1.astype(BF16)
        db_ref[:, 0:DM] += jnp.sum(dz0, axis=0, keepdims=True)
        db_ref[:, DM:2 * DM] += jnp.sum(dz1, axis=0, keepdims=True)
        width = DM // NSH
        back = lambda dp, w4_ref: sum(_dot(dp[:, j * width:(j + 1) * width], w4_ref[j], NT) for j in range(NSH))
        dona_ref[...] = back(dpa, wa_ref).astype(BF16)
        dosw_ref[...] = back(dps, ws_ref).astype(BF16)

    s = zg.shape[0]
    half = lambda j: pl.BlockSpec((tm, DM), lambda i, j=j: (i, j))
    bvec = lambda j: pl.BlockSpec((1, DM), lambda i, j=j: (0, j))
    att = pl.BlockSpec((tm, 512), lambda i: (i, 0))
    whole = lambda a: pl.BlockSpec(a.shape, lambda i: (0,) * a.ndim, pipeline_mode=pl.Buffered(1))
    ins, specs = _with_dep([dy, zg, zg, bias, bias, pa, ps, wa, ws, wo],
                           [half(0), half(0), half(1), bvec(0), bvec(1), half(0), half(0), whole(wa), whole(ws), whole(wo)], dep)
    return pl.pallas_call(
        body, name=name,
        out_shape=(_sds((s, DM), BF16), _sds((s, DM), BF16), _sds((s, GATE_W), BF16), _sds((1, GATE_W), F32),
                   _sds((s, 512), BF16), _sds((s, 512), BF16)),
        grid=(s // tm,), in_specs=specs,
        out_specs=(half(0), half(0), pl.BlockSpec((tm, GATE_W), lambda i: (i, 0)), pl.BlockSpec((1, GATE_W), lambda i: (0, 0)),
                   att, att),
        compiler_params=_params(("arbitrary",)))(*ins)


def adamw_small(ws, gs, ms, vs, *, name):
    cnt = len(ws)

    def body(*refs):
        ins, outs = refs[:4 * cnt], refs[4 * cnt:]
        for i in range(cnt):
            w_ref, g_ref, m_ref, v_ref = ins[4 * i:4 * i + 4]
            d_ref, nm_ref, nv_ref = outs[3 * i:3 * i + 3]
            g = g_ref[...]
            nm = ADAM_B1 * m_ref[...] + (1.0 - ADAM_B1) * g
            nv = ADAM_B2 * v_ref[...] + (1.0 - ADAM_B2) * jnp.square(g)
            m_hat = nm / (1.0 - ADAM_B1 ** ADAM_STEP)
            v_hat = nv / (1.0 - ADAM_B2 ** ADAM_STEP)
            d_ref[...] = -ADAM_LR * (m_hat / (jnp.sqrt(v_hat) + ADAM_EPS) + ADAM_WD * w_ref[...])
            nm_ref[...] = nm
            nv_ref[...] = nv

    flat = [a for i in range(cnt) for a in (ws[i], gs[i], ms[i], vs[i])]
    res = pl.pallas_call(
        body, name=name, out_shape=tuple(_sds(ws[i].shape, F32) for i in range(cnt) for _ in range(3)),
        compiler_params=_params())(*flat)
    return [tuple(res[3 * i:3 * i + 3]) for i in range(cnt)]


def adamw_layer(ws, ms, vs, mines, theirs, cidx, layer, filled=None, *, name):
    cnt = len(ws)
    _, k, n = ws[0].shape
    nt = 2
    tk = k // 2 // nt

    def body(c_ref, *refs):
        own = pl.program_id(0) == c_ref[0]
        outs = refs[-4 * cnt:]
        for i in range(cnt):
            w_ref, m_ref, v_ref, a_ref, b_ref = refs[5 * i:5 * i + 5]
            g_ref, d_ref, nm_ref, nv_ref = outs[4 * i:4 * i + 4]
            g = jnp.where(own, a_ref[...], b_ref[...])
            g_ref[...] = g
            nm = ADAM_B1 * m_ref[...] + (1.0 - ADAM_B1) * g
            nv = ADAM_B2 * v_ref[...] + (1.0 - ADAM_B2) * jnp.square(g)
            m_hat = nm / (1.0 - ADAM_B1 ** ADAM_STEP)
            v_hat = nv / (1.0 - ADAM_B2 ** ADAM_STEP)
            d_ref[...] = -ADAM_LR * (m_hat / (jnp.sqrt(v_hat) + ADAM_EPS) + ADAM_WD * w_ref[...])
            nm_ref[...] = nm
            nv_ref[...] = nv

    full = pl.BlockSpec((None, tk, n), lambda hf, t, c: (layer, hf * nt + t, 0))
    half_mine = pl.BlockSpec((tk, n), lambda hf, t, c: (jnp.where(hf == c[0], t, 0), 0))
    half_theirs = pl.BlockSpec((tk, n), lambda hf, t, c: (jnp.where(hf != c[0], t, 0), 0))
    out = _sds(ws[0].shape, F32)
    ins, specs, aliases = [cidx], [], {}
    for i in range(cnt):
        ins += [ws[i], ms[i], vs[i], mines[i], theirs[i]]
        specs += [full, full, full, half_mine, half_theirs]
    if filled is not None:
        aliases = {len(ins) + j: j for j in range(4 * cnt)}
        ins += [a for f in filled for a in f]
        specs += [pl.BlockSpec(memory_space=pl.ANY)] * (4 * cnt)
    res = pl.pallas_call(
        body, name=name, out_shape=(out,) * (4 * cnt),
        grid_spec=pltpu.PrefetchScalarGridSpec(
            num_scalar_prefetch=1, grid=(2, nt), in_specs=specs, out_specs=(full,) * (4 * cnt)),
        input_output_aliases=aliases,
        compiler_params=_params(("arbitrary", "arbitrary")))(*ins)
    return [tuple(res[4 * i:4 * i + 4]) for i in range(cnt)]


def t5_table_grad(dt5_a, dt5_b, *, name):
    def body(a_ref, b_ref, map_ref, o_ref):
        d = a_ref[...] + b_ref[...]
        bucket = map_ref[...]
        for b in range(32):
            hit = (bucket == b)[None]
            o_ref[b] = jnp.sum(jnp.sum(jnp.where(hit, d, 0.0), axis=2), axis=1, keepdims=True)

    return pl.pallas_call(
        body, name=name, out_shape=_sds((32, 8, 1), F32), compiler_params=_params())(
            dt5_a, dt5_b, jnp.asarray(t5_bucket_map()))


def rpb_grad(dbias, *, name):
    def body(d_ref, rev_ref, o_ref):
        rev = rev_ref[...]
        for h in range(NA_HEADS):
            for pr in range(NA_WR // 2):
                d = d_ref[h, :, 128 * pr:128 * pr + 128]
                hi = d.astype(BF16)
                lo = (d - hi.astype(F32)).astype(BF16)
                flipped = _dot(rev, hi) + _dot(rev, lo)
                o_ref[h, pr] = jnp.sum(pltpu.roll(flipped, 0, 1, stride=1, stride_axis=0), axis=0, keepdims=True)

    anti = jnp.asarray(np.eye(GRID_W, dtype=np.float32)[::-1], dtype=BF16)
    e = pl.pallas_call(
        body, name=name, out_shape=_sds((NA_WR, NA_HEADS, NA_WR // 2, 1, 128), F32), grid=(NA_WR,),
        in_specs=[pl.BlockSpec((None, NA_HEADS, GRID_W, NA_KEYS), lambda p: (p, 0, 0, 0)),
                  pl.BlockSpec((GRID_W, GRID_W), lambda p: (0, 0))],
        out_specs=pl.BlockSpec((None, NA_HEADS, NA_WR // 2, 1, 128), lambda p: (p, 0, 0, 0, 0)),
        compiler_params=_params(("parallel",)))(dbias, anti)
    nci, nri = 2 * NA_WC - 1, 2 * NA_WR - 1
    e = e.reshape(NA_WR, NA_HEADS, NA_WR // 2, 128).transpose(0, 2, 1, 3).reshape(NA_WR * NA_WR // 2, NA_HEADS, 128)
    parts = jnp.concatenate([e[..., 48:48 + nci], jnp.concatenate([e[..., 112:128], e[..., 0:nci - 16]], axis=-1)], axis=0)
    p, pr = np.arange(NA_WR)[:, None], np.arange(NA_WR // 2)[None, :]
    ri = np.concatenate([(2 * pr - p + NA_WR - 1).reshape(-1), (2 * pr - p + NA_WR).reshape(-1)])
    pick = jnp.asarray((ri[None, :] == np.arange(16)[:, None]).astype(np.float32))
    out = mm(pick, parts.reshape(2 * NA_WR * NA_WR // 2, NA_HEADS * nci), name=name + "_rows", exact=True)
    return out.reshape(16, NA_HEADS, nci)[:nri].transpose(1, 0, 2)


BIG = ("ffn1_w_gate", "ffn1_w_up", "ffn1_w_down", "w_in", "w_branch_na", "w_branch_sw", "w_out",
       "ffn2_w_gate", "ffn2_w_up", "ffn2_w_down")
SMALL = ("ffn1_norm", "mix_norm", "b_gate", "na_q_norm", "na_k_norm", "na_rpb", "sw_q_norm", "sw_k_norm", "sw_sink",
         "ffn2_norm")


def _mixer_weights(g):
    w_in_t = g["w_in"].reshape(IN_W, DM)
    return dict(w_in_t=w_in_t, wa=g["w_branch_na"], ws=g["w_branch_sw"], wo=g["w_out"].reshape(DM, DM))


GROUPS = {"ffn1": ("ffn1_w_gate", "ffn1_w_up", "ffn1_w_down"), "mix": ("w_in", "w_branch_na", "w_branch_sw", "w_out"),
          "ffn2": ("ffn2_w_gate", "ffn2_w_up", "ffn2_w_down")}


def layer_fwd(x, p, weights, t5b, target=None):
    row = lambda v: v.reshape(1, -1)
    stacked = lambda g: {n: a.reshape(DFF, DM) for n, a in g.items()}
    g1 = stacked(weights("ffn1", x))
    y1, h1, gg1, uu1 = ffn_fwd(x, row(p["ffn1_norm"]), g1["ffn1_w_gate"], g1["ffn1_w_up"], g1["ffn1_w_down"], name="ffn_fwd")
    w = _mixer_weights(weights("mix", y1))
    hm, z, zg, qa, ka, va, qs, kv = mixer_input_fwd(y1, row(p["mix_norm"]), w["w_in_t"], p["na_q_norm"], p["na_k_norm"],
                                                    p["sw_q_norm"], p["sw_k_norm"], name="mixer_input_fwd")
    bias = p["na_bias"]
    o_na, lse_na = na_fwd(qa, ka, va, bias, name="na_fwd")
    kvp = jnp.pad(kv, ((SW_BLK, SW_BLK), (0, 0)))
    sink = p["sw_sink"]
    o_sw, lse_sw = sw_fwd(qs, kvp, t5b, sink, name="sw_fwd")
    y2, pa, ps, merged = mixer_output_fwd(o_na, o_sw, zg, row(p["b_gate"]), w["wa"], w["ws"], w["wo"], y1,
                                          name="mixer_output_fwd")
    g2 = stacked(weights("ffn2", y2))
    *y3, h2, gg2, uu2 = ffn_fwd(y2, row(p["ffn2_norm"]), g2["ffn2_w_gate"], g2["ffn2_w_up"], g2["ffn2_w_down"], target,
                                name="ffn_fwd")
    y3 = y3[0] if target is None else tuple(y3)
    saved = dict(x=x, y1=y1, h1=h1, gg1=gg1, uu1=uu1, hm=hm, z=z, zg=zg, qa=qa, ka=ka, va=va, qs=qs, kvp=kvp, bias=bias,
                 o_na=o_na, lse_na=lse_na, o_sw=o_sw, lse_sw=lse_sw, pa=pa, ps=ps, merged=merged, y2=y2, h2=h2, gg2=gg2,
                 uu2=uu2, w=w, sink=sink, g1=g1, g2=g2)
    return y3, saved


def layer_bwd(dy3, dy3_bf, sv, p, t5b, emit, dep=None):
    w, g1, g2 = sv["w"], sv["g1"], sv["g2"]
    row = lambda v: v.reshape(1, -1)
    fold = lambda v: v.reshape(-1, HD).sum(axis=0)
    small = {}
    dy2, dy2_bf, small["ffn2_norm"], act, dg, du = ffn_bwd_tokens(
        dy3, sv["y2"], row(p["ffn2_norm"]), sv["gg2"], sv["uu2"], g2["ffn2_w_gate"], g2["ffn2_w_up"], g2["ffn2_w_down"],
        name="ffn_bwd_tokens", dep=dep)
    shards = lambda gs: [g.reshape(NSH, FSH, DM) for g in gs]
    token = emit("ffn2", shards(ffn_bwd_weights(sv["h2"], dy3_bf, act, dg, du, name="ffn_bwd_weights")))
    dpa, dps, dzg, small["b_gate"], do_na, do_sw = mixer_output_bwd(
        dy2, sv["zg"], row(p["b_gate"]), sv["pa"], sv["ps"], w["wa"], w["ws"], w["wo"], name="mixer_output_bwd", dep=token)
    gw_out, gw_na, gw_sw = mixer_output_dw(sv["merged"], dy2_bf, sv["o_na"], dpa, sv["o_sw"], dps, name="mixer_output_dw")
    gw_out = gw_out.reshape(NSH, DM // NSH, DM)
    dqa, dka, dva, dbias = na_bwd(sv["qa"], sv["ka"], sv["va"], sv["o_na"], do_na, sv["lse_na"], sv["bias"], name="na_bwd")
    dqs, dkvp, dt5, dsink = sw_bwd(sv["qs"], sv["kvp"], sv["o_sw"], do_sw, sv["lse_sw"], t5b, sv["sink"], name="sw_bwd")
    dkv = dkvp[SW_BLK:SW_BLK + SEQ]
    dz, dgqa, dgka, dgqs, dgks = qknorm_bwd(sv["z"], dqa, dka, dva, dqs, dkv, p["na_q_norm"], p["na_k_norm"],
                                            p["sw_q_norm"], p["sw_k_norm"], name="qknorm_bwd")
    small["na_q_norm"], small["na_k_norm"], small["sw_q_norm"], small["sw_k_norm"] = fold(dgqa), fold(dgka), fold(dgqs), fold(dgks)
    small["na_rpb"] = rpb_grad(dbias, name="rpb_grad")
    small["sw_sink"] = dsink
    gw_in = mixer_input_dw(dz, dzg, sv["hm"], name="mixer_input_dw").reshape(NSH, IN_W // NSH, DM)
    token = emit("mix", (gw_in, gw_na, gw_sw, gw_out))
    dy1, dy1_bf, small["mix_norm"] = mixer_input_bwd(dz, dzg, w["w_in_t"], sv["y1"], row(p["mix_norm"]), dy2,
                                                     name="mixer_input_bwd", dep=token)
    dx, dx_bf, small["ffn1_norm"], act, dg, du = ffn_bwd_tokens(
        dy1, sv["x"], row(p["ffn1_norm"]), sv["gg1"], sv["uu1"], g1["ffn1_w_gate"], g1["ffn1_w_up"], g1["ffn1_w_down"],
        name="ffn_bwd_tokens")
    emit("ffn1", shards(ffn_bwd_weights(sv["h1"], dy1_bf, act, dg, du, name="ffn_bwd_weights")))
    return dx, dx_bf, small, dt5


ANY = pl.BlockSpec(memory_space=pl.ANY)


def _place():
    x, y, c = lax.axis_index("x"), lax.axis_index("y"), lax.axis_index("c")
    chips = [(1 - x, y), (x, 1 - y), (1 - x, 1 - y)]
    return x, y, c, chips


def _remote(src, dst, send_sem, recv_sem, to):
    return pltpu.make_async_remote_copy(src_ref=src, dst_ref=dst, send_sem=send_sem, recv_sem=recv_sem, device_id=to,
                                        device_id_type=MESH)


HBM = pl.BlockSpec(memory_space=pltpu.HBM)
SEM = pl.BlockSpec(memory_space=pltpu.SEMAPHORE)
ORDERED_EFFECT = pltpu.SideEffectType.DATAFLOW_SIDE_EFFECTING


def _in_hbm(v):
    return pltpu.with_memory_space_constraint(v, pltpu.HBM)


def _row_half(ref_shape_rows, c):
    half = ref_shape_rows // 2
    return pl.ds(c * half, half)


def _ici_gather_copies(w, land, send_sems, recv_sems):
    x, y, c, chips = _place()
    me = 2 * x + y
    copies = []
    for a in range(len(w)):
        rows = _row_half(w[a].shape[0], c)
        for k, chip in enumerate(chips):
            copies.append(_remote(w[a].at[rows], land[a].at[me, rows], send_sems.at[4 * a + k], recv_sems.at[4 * a + k],
                                  (*chip, c)))
        copies.append(_remote(w[a], land[a].at[me], send_sems.at[4 * a + 3], recv_sems.at[4 * a + 3], (x, y, 1 - c)))
    return copies


def _d2d_gather_copies(w, land, send_sems, recv_sems):
    x, y, c, chips = _place()
    copies = []
    for a in range(len(w)):
        rows = _row_half(w[a].shape[0], c)
        for k, (cx, cy) in enumerate(chips):
            blk = land[a].at[2 * cx + cy, rows]
            copies.append(_remote(blk, blk, send_sems.at[3 * a + k], recv_sems.at[3 * a + k], (x, y, 1 - c)))
    return copies


def _d2d_gather_waits(w, land, send_sems, recv_sems):
    x, y, c, chips = _place()
    waits = []
    for a in range(len(w)):
        rows = _row_half(w[a].shape[0], 1 - c)
        for k, (cx, cy) in enumerate(chips):
            blk = land[a].at[2 * cx + cy, rows]
            waits.append(_remote(blk, blk, send_sems.at[3 * a + k], recv_sems.at[3 * a + k], (x, y, 1 - c)))
    return waits


def gather_start(groups, dep=None, *, name):
    sizes = [len(g) for g in groups]
    shards = [s for g in groups for s in g]
    n, ng = len(shards), len(groups)
    extra = [] if dep is None else [dep]

    def body(*refs):
        first_out = 2 * n + len(extra)
        w, land, sems = refs[:n], refs[n:2 * n], refs[first_out:first_out + 2 * ng]
        off = 0
        for gi, size in enumerate(sizes):
            for cp in _ici_gather_copies(w[off:off + size], land[off:off + size], sems[2 * gi], sems[2 * gi + 1]):
                cp.start()
            off += size

    lands = [lax.empty((NSH,) + s.shape, s.dtype) for s in shards]
    sem_shapes = tuple(pltpu.SemaphoreType.DMA((4 * size,)) for size in sizes for _ in range(2))
    res = pl.pallas_call(
        body, name=name,
        out_shape=sem_shapes + tuple(pltpu.HBM(s.shape, s.dtype) for s in shards) + tuple(pltpu.HBM(l.shape, l.dtype) for l in lands),
        in_specs=[HBM] * (2 * n) + [ANY] * len(extra), out_specs=(SEM,) * (2 * ng) + (HBM,) * (2 * n),
        input_output_aliases={i: 2 * ng + i for i in range(2 * n)},
        compiler_params=pltpu.CompilerParams(has_side_effects=ORDERED_EFFECT))(
            *[_in_hbm(s) for s in shards], *[_in_hbm(l) for l in lands], *extra)
    out, off = [], 0
    for gi, size in enumerate(sizes):
        out.append((res[2 * gi], res[2 * gi + 1], list(res[2 * ng + off:2 * ng + off + size]),
                    list(res[2 * ng + n + off:2 * ng + n + off + size])))
        off += size
    return out


def gather_wait(send_sems, recv_sems, shards, lands, after, *, name):
    n = len(shards)

    def body(*refs):
        w, land = refs[:n], refs[n:2 * n]
        send, recv = refs[2 * n:2 * n + 2]
        for cp in _ici_gather_copies(w, land, send, recv):
            cp.wait_send()
            cp.wait_recv()

    res = pl.pallas_call(
        body, name=name,
        out_shape=tuple(pltpu.HBM(s.shape, s.dtype) for s in shards) + tuple(pltpu.HBM(l.shape, l.dtype) for l in lands),
        in_specs=[HBM] * (2 * n) + [SEM, SEM] + [ANY] * len(after), out_specs=(HBM,) * (2 * n),
        input_output_aliases={i: i for i in range(2 * n)},
        compiler_params=pltpu.CompilerParams(has_side_effects=ORDERED_EFFECT))(*shards, *lands, send_sems, recv_sems, *after)
    return list(res[:n]), list(res[n:])


def gather_finish(shards, lands, *, name):
    n = len(shards)

    def body(*refs):
        w, land = refs[:n], refs[n:2 * n]
        send_sems, recv_sems = refs[3 * n:]
        d2d = _d2d_gather_copies(w, land, send_sems, recv_sems)
        for cp in d2d:
            cp.start()
        for cp in _d2d_gather_waits(w, land, send_sems, recv_sems):
            cp.wait_recv()
        for cp in d2d:
            cp.wait_send()

    return list(pl.pallas_call(
        body, name=name, out_shape=tuple(pltpu.HBM(l.shape, l.dtype) for l in lands),
        in_specs=[ANY] * (2 * n), out_specs=tuple([ANY] * n), input_output_aliases={n + i: i for i in range(n)},
        scratch_shapes=[pltpu.SemaphoreType.DMA((3 * n,)), pltpu.SemaphoreType.DMA((3 * n,))])(*shards, *lands))


def _pair_exchange_copies(g, buf, send_sems, recv_sems):
    x, y, c, _ = _place()
    copies = []
    for a in range(len(g)):
        half = g[a].shape[1] // 2
        copies.append(_remote(g[a].at[:, pl.ds((1 - c) * half, half)], buf[a], send_sems.at[a], recv_sems.at[a], (x, y, 1 - c)))
    return copies


def pair_exchange_start(grads, dep=None, *, name):
    n = len(grads)
    extra = [] if dep is None else [dep]

    def body(*refs):
        sems = refs[2 * n + len(extra):]
        for cp in _pair_exchange_copies(refs[:n], refs[n:2 * n], sems[0], sems[1]):
            cp.start()
        refs[-1][...] = jnp.zeros_like(refs[-1])

    lands = [lax.empty((NSH, g.shape[1] // 2, g.shape[2]), g.dtype) for g in grads]
    res = pl.pallas_call(
        body, name=name,
        out_shape=(pltpu.SemaphoreType.DMA((n,)), pltpu.SemaphoreType.DMA((n,)))
        + tuple(pltpu.HBM(g.shape, g.dtype) for g in grads) + tuple(pltpu.HBM(l.shape, l.dtype) for l in lands)
        + (_sds((8, 128), F32),),
        in_specs=[HBM] * (2 * n) + [ANY] * len(extra),
        out_specs=(SEM, SEM) + (HBM,) * (2 * n) + (pl.BlockSpec(memory_space=pltpu.VMEM),),
        input_output_aliases={i: 2 + i for i in range(2 * n)},
        compiler_params=pltpu.CompilerParams(has_side_effects=ORDERED_EFFECT))(
            *[_in_hbm(g) for g in grads], *[_in_hbm(l) for l in lands], *extra)
    return res[0], res[1], list(res[2:2 + n]), list(res[2 + n:2 + 2 * n]), res[-1]


def pair_exchange_wait(send_sems, recv_sems, grads, lands, after, *, name):
    n = len(grads)

    def body(*refs):
        for cp in _pair_exchange_copies(refs[:n], refs[n:2 * n], refs[2 * n], refs[2 * n + 1]):
            cp.wait_send()
            cp.wait_recv()

    res = pl.pallas_call(
        body, name=name,
        out_shape=tuple(pltpu.HBM(g.shape, g.dtype) for g in grads) + tuple(pltpu.HBM(l.shape, l.dtype) for l in lands),
        in_specs=[HBM] * (2 * n) + [SEM, SEM] + [ANY] * len(after), out_specs=(HBM,) * (2 * n),
        input_output_aliases={i: i for i in range(2 * n)},
        compiler_params=pltpu.CompilerParams(has_side_effects=ORDERED_EFFECT))(*grads, *lands, send_sems, recv_sems, *after)
    return list(res[:n]), list(res[n:])


def _chip_exchange_copies(s, buf, send_sems, recv_sems):
    x, y, c, chips = _place()
    return [_remote(s[a].at[2 * cx + cy], buf[a].at[k], send_sems.at[3 * a + k], recv_sems.at[3 * a + k], (cx, cy, c))
            for a in range(len(s)) for k, (cx, cy) in enumerate(chips)]


def exchange_start(sums, grads, *, name):
    n1, n2 = len(sums), len(grads)

    def body(*refs):
        first_out = 2 * (n1 + n2)
        chip = _chip_exchange_copies(refs[:n1], refs[n1:2 * n1], refs[first_out], refs[first_out + 1])
        pair = _pair_exchange_copies(refs[2 * n1:2 * n1 + n2], refs[2 * n1 + n2:first_out], refs[first_out + 2],
                                     refs[first_out + 3])
        for cp in chip + pair:
            cp.start()
        refs[-1][...] = jnp.zeros_like(refs[-1])

    chip_lands = [lax.empty((3,) + s.shape[1:], s.dtype) for s in sums]
    pair_lands = [lax.empty((NSH, g.shape[1] // 2, g.shape[2]), g.dtype) for g in grads]
    arrays = list(sums) + chip_lands + list(grads) + pair_lands
    res = pl.pallas_call(
        body, name=name,
        out_shape=(pltpu.SemaphoreType.DMA((3 * n1,)), pltpu.SemaphoreType.DMA((3 * n1,)), pltpu.SemaphoreType.DMA((n2,)),
                   pltpu.SemaphoreType.DMA((n2,)))
        + tuple(pltpu.HBM(a.shape, a.dtype) for a in arrays) + (_sds((8, 128), F32),),
        in_specs=[HBM] * len(arrays), out_specs=(SEM,) * 4 + (HBM,) * len(arrays) + (pl.BlockSpec(memory_space=pltpu.VMEM),),
        input_output_aliases={i: 4 + i for i in range(len(arrays))},
        compiler_params=pltpu.CompilerParams(has_side_effects=ORDERED_EFFECT))(*[_in_hbm(a) for a in arrays])
    thru = list(res[4:4 + len(arrays)])
    chip = (res[0], res[1], thru[:n1], thru[n1:2 * n1])
    pair = (res[2], res[3], thru[2 * n1:2 * n1 + n2], thru[2 * n1 + n2:])
    return chip, pair, res[-1]


def chip_exchange_start(sums, *, name):
    n = len(sums)

    def body(*refs):
        for cp in _chip_exchange_copies(refs[:n], refs[n:2 * n], refs[2 * n], refs[2 * n + 1]):
            cp.start()
        refs[-1][...] = jnp.zeros_like(refs[-1])

    lands = [lax.empty((3,) + s.shape[1:], s.dtype) for s in sums]
    res = pl.pallas_call(
        body, name=name,
        out_shape=(pltpu.SemaphoreType.DMA((3 * n,)), pltpu.SemaphoreType.DMA((3 * n,)))
        + tuple(pltpu.HBM(s.shape, s.dtype) for s in sums) + tuple(pltpu.HBM(l.shape, l.dtype) for l in lands)
        + (_sds((8, 128), F32),),
        in_specs=[HBM] * (2 * n), out_specs=(SEM, SEM) + (HBM,) * (2 * n) + (pl.BlockSpec(memory_space=pltpu.VMEM),),
        input_output_aliases={i: 2 + i for i in range(2 * n)},
        compiler_params=pltpu.CompilerParams(has_side_effects=ORDERED_EFFECT))(
            *[_in_hbm(s) for s in sums], *[_in_hbm(l) for l in lands])
    return res[0], res[1], list(res[2:2 + n]), list(res[2 + n:2 + 2 * n]), res[-1]


def chip_exchange_wait(send_sems, recv_sems, sums, lands, after, *, name):
    n = len(sums)

    def body(*refs):
        for cp in _chip_exchange_copies(refs[:n], refs[n:2 * n], refs[2 * n], refs[2 * n + 1]):
            cp.wait_send()
            cp.wait_recv()

    res = pl.pallas_call(
        body, name=name,
        out_shape=tuple(pltpu.HBM(s.shape, s.dtype) for s in sums) + tuple(pltpu.HBM(l.shape, l.dtype) for l in lands),
        in_specs=[HBM] * (2 * n) + [SEM, SEM] + [ANY] * len(after), out_specs=(HBM,) * (2 * n),
        input_output_aliases={i: i for i in range(2 * n)},
        compiler_params=pltpu.CompilerParams(has_side_effects=ORDERED_EFFECT))(*sums, *lands, send_sems, recv_sems, *after)
    return list(res[:n]), list(res[n:])


def _pair_send_copies(h, got, send_sems, recv_sems):
    x, y, c, _ = _place()
    return [_remote(h[i], got[i], send_sems.at[i], recv_sems.at[i], (x, y, 1 - c)) for i in range(len(h))]


def pair_send_start(halves, *, name):
    n = len(halves)

    def body(*refs):
        for cp in _pair_send_copies(refs[:n], refs[n:2 * n], refs[2 * n], refs[2 * n + 1]):
            cp.start()
        refs[-1][...] = jnp.zeros_like(refs[-1])

    lands = [lax.empty(h.shape, h.dtype) for h in halves]
    res = pl.pallas_call(
        body, name=name,
        out_shape=(pltpu.SemaphoreType.DMA((n,)), pltpu.SemaphoreType.DMA((n,)))
        + tuple(pltpu.HBM(h.shape, h.dtype) for h in halves) * 2 + (_sds((8, 128), F32),),
        in_specs=[HBM] * (2 * n), out_specs=(SEM, SEM) + (HBM,) * (2 * n) + (pl.BlockSpec(memory_space=pltpu.VMEM),),
        input_output_aliases={i: 2 + i for i in range(2 * n)},
        compiler_params=pltpu.CompilerParams(has_side_effects=ORDERED_EFFECT))(
            *[_in_hbm(h) for h in halves], *[_in_hbm(l) for l in lands])
    return res[0], res[1], list(res[2:2 + n]), list(res[2 + n:2 + 2 * n]), res[-1]


def pair_send_wait(send_sems, recv_sems, halves, lands, after, *, name):
    n = len(halves)

    def body(*refs):
        for cp in _pair_send_copies(refs[:n], refs[n:2 * n], refs[2 * n], refs[2 * n + 1]):
            cp.wait_send()
            cp.wait_recv()

    res = pl.pallas_call(
        body, name=name, out_shape=tuple(pltpu.HBM(h.shape, h.dtype) for h in halves) * 2,
        in_specs=[HBM] * (2 * n) + [SEM, SEM] + [ANY] * len(after), out_specs=(HBM,) * (2 * n),
        input_output_aliases={i: i for i in range(2 * n)},
        compiler_params=pltpu.CompilerParams(has_side_effects=ORDERED_EFFECT))(*halves, *lands, send_sems, recv_sems, *after)
    return list(res[:n]), list(res[n:])


def allreduce_small(v, *, name):
    rows = v.shape[0]

    def body(v_ref, o_ref, gath, send_sems, recv_sems):
        x, y, c, _ = _place()
        me = 4 * x + 2 * y + c
        gath[me] = v_ref[...]
        copies = []
        for k in range(1, 8):
            fx, fy, fc = (k >> 2) & 1, (k >> 1) & 1, k & 1
            peer = (jnp.where(fx, 1 - x, x), jnp.where(fy, 1 - y, y), jnp.where(fc, 1 - c, c))
            cp = _remote(v_ref, gath.at[me], send_sems.at[k - 1], recv_sems.at[k - 1], peer)
            cp.start()
            copies.append(cp)
        for cp in copies:
            cp.wait()
        acc = gath[0]
        for d in range(1, 8):
            acc = acc + gath[d]
        o_ref[...] = acc

    return pl.pallas_call(
        body, name=name, out_shape=_sds(v.shape, F32),
        in_specs=[pl.BlockSpec(memory_space=pltpu.VMEM)], out_specs=pl.BlockSpec(memory_space=pltpu.VMEM),
        scratch_shapes=[pltpu.VMEM((8, rows, 128), F32), pltpu.SemaphoreType.DMA((7,)), pltpu.SemaphoreType.DMA((7,))])(v)


def _same_shape_runs(arrays):
    runs = {}
    for i, a in enumerate(arrays):
        runs.setdefault(a.shape, []).append(i)
    return list(runs.values())


def _per_shape(fn, *lists):
    out = [None] * len(lists[0])
    for idx in _same_shape_runs(lists[0]):
        for i, r in zip(idx, fn(*[[l[i] for i in idx] for l in lists])):
            out[i] = r
    return out


def add_halves(gs, bufs, cidx, *, name):
    cnt = len(gs)
    _, k, n = gs[0].shape

    def body(c_ref, *refs):
        g, b, o = refs[:cnt], refs[cnt:2 * cnt], refs[2 * cnt:]
        for i in range(cnt):
            o[i][...] = (g[i][...].astype(F32) + b[i][...].astype(F32)).astype(BF16)

    blk = pl.BlockSpec((None, k // 2, n), lambda s, c: (s, 0, 0))
    mine = pl.BlockSpec((None, k // 2, n), lambda s, c: (s, c[0], 0))
    return list(pl.pallas_call(
        body, name=name, out_shape=tuple(_sds(b.shape, BF16) for b in bufs),
        grid_spec=pltpu.PrefetchScalarGridSpec(
            num_scalar_prefetch=1, grid=(NSH,), in_specs=[mine] * cnt + [blk] * cnt, out_specs=tuple([blk] * cnt)),
        compiler_params=_params(("parallel",)))(cidx, *gs, *bufs))


def add_chips(sums, bufs, sidx, *, name):
    cnt = len(sums)
    _, kh, n = sums[0].shape

    def body(s_ref, *refs):
        mine, b, o = refs[:cnt], refs[cnt:2 * cnt], refs[2 * cnt:]
        for i in range(cnt):
            o[i][...] = ((mine[i][...].astype(F32) + b[i][0].astype(F32)) + (b[i][1].astype(F32) + b[i][2].astype(F32)))

    own = pl.BlockSpec((None, kh, n), lambda i, s: (s[0], 0, 0))
    got = pl.BlockSpec((3, kh, n), lambda i, s: (0, 0, 0))
    out = pl.BlockSpec((kh, n), lambda i, s: (0, 0))
    return list(pl.pallas_call(
        body, name=name, out_shape=tuple(_sds((kh, n), F32) for _ in sums),
        grid_spec=pltpu.PrefetchScalarGridSpec(
            num_scalar_prefetch=1, grid=(1,), in_specs=[own] * cnt + [got] * cnt, out_specs=tuple([out] * cnt)),
        compiler_params=_params(("arbitrary",)))(sidx, *sums, *bufs))


PARAMS = ("ffn1_norm", "ffn1_w_gate", "ffn1_w_up", "ffn1_w_down", "mix_norm", "w_in", "b_gate", "na_q_norm", "na_k_norm",
          "na_rpb", "sw_q_norm", "sw_k_norm", "sw_sink", "t5_rel_table", "w_branch_na", "w_branch_sw", "w_out", "ffn2_norm",
          "ffn2_w_gate", "ffn2_w_up", "ffn2_w_down")
SMALL_ALL = tuple(n for n in PARAMS if n not in BIG)
TRANSPOSED = ("ffn1_w_gate", "ffn1_w_up", "w_in", "ffn2_w_gate", "ffn2_w_up")
SMALL_ROWS = 152


def _pack_small(vals):
    flat = jnp.concatenate([vals[n].reshape(-1).astype(F32) for n in SMALL_ALL] + [vals["loss"].reshape(-1)])
    return jnp.pad(flat, (0, SMALL_ROWS * 128 - flat.shape[0])).reshape(SMALL_ROWS, 128)


def _unpack_small(packed, like):
    flat, out, off = packed.reshape(-1), {}, 0
    for n in SMALL_ALL:
        size = math.prod(like[n].shape)
        out[n] = flat[off:off + size].reshape(like[n].shape)
        off += size
    out["loss"] = flat[off]
    return out


def kernel(x, ffn1_norm, ffn1_w_gate, ffn1_w_up, ffn1_w_down, mix_norm, w_in, b_gate, na_q_norm, na_k_norm, na_rpb, sw_q_norm, sw_k_norm, sw_sink, t5_rel_table, w_branch_na, w_branch_sw, w_out, ffn2_norm, ffn2_w_gate, ffn2_w_up, ffn2_w_down, loss_target, m_ffn1_norm, m_ffn1_w_gate, m_ffn1_w_up, m_ffn1_w_down, m_mix_norm, m_w_in, m_b_gate, m_na_q_norm, m_na_k_norm, m_na_rpb, m_sw_q_norm, m_sw_k_norm, m_sw_sink, m_t5_rel_table, m_w_branch_na, m_w_branch_sw, m_w_out, m_ffn2_norm, m_ffn2_w_gate, m_ffn2_w_up, m_ffn2_w_down, v_ffn1_norm, v_ffn1_w_gate, v_ffn1_w_up, v_ffn1_w_down, v_mix_norm, v_w_in, v_b_gate, v_na_q_norm, v_na_k_norm, v_na_rpb, v_sw_q_norm, v_sw_k_norm, v_sw_sink, v_t5_rel_table, v_w_branch_na, v_w_branch_sw, v_w_out, v_ffn2_norm, v_ffn2_w_gate, v_ffn2_w_up, v_ffn2_w_down):
    args = locals()
    tr = lambda n, a: jnp.transpose(a, (0, 2, 1)) if n in TRANSPOSED else a
    w = {n: tr(n, args[n]) for n in PARAMS}
    m = {n: tr(n, args["m_" + n]) for n in PARAMS}
    v = {n: tr(n, args["v_" + n]) for n in PARAMS}
    cidx = lax.axis_index("c").astype(jnp.int32).reshape(1)
    sidx = (2 * lax.axis_index("x") + lax.axis_index("y")).astype(jnp.int32).reshape(1)

    small = [{n: w[n][l] for n in SMALL} for l in range(DEPTH)]
    order = ("ffn1", "mix", "ffn2")

    keys = [(l, g) for l in range(DEPTH) for g in order]
    local = lambda l, g: [w[n][l].astype(BF16) for n in GROUPS[g]]
    first = gather_start([local(*keys[0])], name="gather_start")
    rest = gather_start([local(*key) for key in keys[1:]], first[0][2][0], name="gather_start")
    in_flight = dict(zip(keys, first + rest))
    t5b = t5_bias(w["t5_rel_table"], name="t5_bias")
    for l in range(DEPTH):
        small[l]["na_bias"] = na_bias_table(small[l]["na_rpb"], name="na_bias_table")
    early = [t5b] + [small[l]["na_bias"] for l in range(DEPTH)] + [rest[0][2][0]]

    def weights_of(l):
        def get(group, after):
            send_sems, recv_sems, thru, lands = in_flight[(l, group)]
            after = [after] + (early if (l, group) == keys[0] else [])
            thru, lands = gather_wait(send_sems, recv_sems, thru, lands, after, name="gather_wait")
            return dict(zip(GROUPS[group], gather_finish(thru, lands, name="gather_finish")))
        return get

    h0, saved0 = layer_fwd(x[0], small[0], weights_of(0), t5b)
    (dy, dy_bf, loss_row), saved1 = layer_fwd(h0, small[1], weights_of(1), t5b, target=loss_target[0])

    crossing, tokens, pending = {}, [], []

    def ship(after, then=None):
        key, send_sems, recv_sems, grads, lands = pending.pop()
        grads, from_sibling = pair_exchange_wait(send_sems, recv_sems, grads, lands, after, name="pair_exchange_wait")
        sums = _per_shape(lambda gs, bs: add_halves(gs, bs, cidx, name="add_halves"), grads, from_sibling)
        if then is None:
            send_sems, recv_sems, sums, lands, token = chip_exchange_start(sums, name="chip_exchange_start")
            crossing[key] = (send_sems, recv_sems, sums, lands)
            return token
        crossing[key], pair, token = exchange_start(sums, then[1], name="exchange_start")
        pending.append((then[0],) + pair)
        return token

    def reduce_of(l):
        def emit(group, grads):
            grads = list(grads)
            if pending:
                token = ship([grads[0]], then=((l, group), grads))
            else:
                send_sems, recv_sems, grads, lands, token = pair_exchange_start(grads, name="pair_exchange_start")
                pending.append(((l, group), send_sems, recv_sems, grads, lands))
            tokens.append(token)
            return token
        return emit

    def finish(layer, after, filled=None):
        sent = {}
        for group in order:
            send_sems, recv_sems, sums, lands = crossing[(layer, group)]
            sums, got = chip_exchange_wait(send_sems, recv_sems, sums, lands, after, name="chip_exchange_wait")
            halves = _per_shape(lambda ss, bs: add_chips(ss, bs, sidx, name="add_chips"), sums, got)
            sent[group] = pair_send_start(halves, name="pair_send_start")
            after = [sent[group][4]]
        out = {}
        for group in order:
            send_sems, recv_sems, halves, lands, _ = sent[group]
            halves, theirs = pair_send_wait(send_sems, recv_sems, halves, lands, after, name="pair_send_wait")
            names = GROUPS[group]
            res = _per_shape(
                lambda ws, ms, vs, a, b, *f: adamw_layer(ws, ms, vs, a, b, cidx, layer, list(f[0]) if f else None, name="adamw_layer"),
                *([[w[n] for n in names], [m[n] for n in names], [v[n] for n in names], halves, theirs]
                  + ([[filled[n] for n in names]] if filled is not None else [])))
            out.update(zip(names, res))
            after = [res[-1][0]]
        return out

    dy, dy_bf, small1, dt5_1 = layer_bwd(dy, dy_bf, saved1, small[1], t5b, reduce_of(1))
    grad_x, _, small0, dt5_0 = layer_bwd(dy, dy_bf, saved0, small[0], t5b, reduce_of(0), dep=tokens[-1])
    done1 = finish(1, [ship([grad_x])])

    smalls = [small0, small1]
    dt5 = t5_table_grad(dt5_0, dt5_1, name="t5_table_grad").reshape(32, 8)
    local_small = {n: jnp.stack([smalls[l][n].reshape(w[n].shape[1:]) for l in range(DEPTH)]) for n in SMALL}
    local_small["t5_rel_table"] = dt5
    local_small["loss"] = loss_row[0, 0:1]
    total = allreduce_small(_pack_small(local_small), name="allreduce_small")
    small_grads = _unpack_small(total, w)
    small_done = adamw_small([w[n] for n in SMALL_ALL], [small_grads[n] for n in SMALL_ALL], [m[n] for n in SMALL_ALL],
                             [v[n] for n in SMALL_ALL], name="adamw_small")

    grad, delta, new_m, new_v = {}, {}, {}, {}
    for n, done in finish(0, [small_done[0][0], done1[BIG[-1]][0]], filled=done1).items():
        grad[n], delta[n], new_m[n], new_v[n] = done
    for n, done in zip(SMALL_ALL, small_done):
        grad[n] = small_grads[n]
        delta[n], new_m[n], new_v[n] = done

    return (small_grads["loss"], grad_x[None], *[tr(n, grad[n]) for n in PARAMS], *[tr(n, delta[n]) for n in PARAMS],
            *[tr(n, new_m[n]) for n in PARAMS], *[tr(n, new_v[n]) for n in PARAMS])
```

```python
import math

import jax
import jax.numpy as jnp
import numpy as np
from jax import lax
from jax.experimental import pallas as pl
from jax.experimental.pallas import tpu as pltpu

F32 = jnp.float32
BF16 = jnp.bfloat16

SEQ = 2048
DM = 1024
DFF = 2816
DEPTH = 2
NSH = 4
FSH = DFF // NSH
GRID_W = 64
ROWS = SEQ // GRID_W
NA_HEADS = 8
HD = 64
NA_WR = 8
NA_WC = 16
NA_KEYS = NA_WR * GRID_W
SW_BLK = 128
SW_NB = SEQ // SW_BLK
SW_KEYS = 3 * SW_BLK
ATT_W = 2304
GATE_W = 2048
IN_W = ATT_W + GATE_W
EPS = 1e-6
NEG = -1e30
QK_SCALE = 1.0 / math.sqrt(HD)

ADAM_LR = 0.001
ADAM_B1 = 0.9
ADAM_B2 = 0.999
ADAM_EPS = 1e-08
ADAM_WD = 0.01
ADAM_STEP = 10

VMEM_LIMIT = 56 << 20
MESH = pl.DeviceIdType.MESH

NT = (((1,), (1,)), ((), ()))
TN = (((0,), (0,)), ((), ()))
NN = (((1,), (0,)), ((), ()))


def _dot(a, b, dims=NN):
    return lax.dot_general(a, b, dims, preferred_element_type=F32)


def _params(sem=None):
    return pltpu.CompilerParams(dimension_semantics=sem, vmem_limit_bytes=VMEM_LIMIT)


def _sds(shape, dtype):
    return jax.ShapeDtypeStruct(shape, dtype)


def mm(a, b, *, name, ta=False, tb=False, out_dtype=F32, add=None, scale=None, tm=512, tn=None, tk=None, exact=False,
       dep=None, b_rows=None):
    m, kd = (a.shape[1], a.shape[0]) if ta else a.shape
    if b_rows is None:
        n = b.shape[0] if tb else b.shape[1]
    else:
        n = b_rows[1] if tb else b.shape[1]
        assert tb or (b_rows[1] == kd and (tk or kd) == kd)
    tm, tn, tk = min(tm, m), min(tn or n, n), min(tk or kd, kd)
    nk = kd // tk
    dims = (((0 if ta else 1,), (1 if tb else 0,)), ((), ()))

    def body(*refs):
        a_ref, b_ref = refs[:2]
        add_ref = refs[2] if add is not None else None
        o_ref = refs[-1] if nk == 1 else refs[-2]
        if b_rows is None:
            bv = b_ref[...]
        elif tb:
            bv = b_ref[pl.ds(pl.multiple_of(b_rows[0] + pl.program_id(1) * tn, 16), tn), :]
        else:
            bv = b_ref[b_rows[0]:b_rows[0] + b_rows[1], :]
        if exact:
            part = lax.dot_general(a_ref[...], bv, dims, precision=lax.Precision.HIGHEST, preferred_element_type=F32)
        else:
            part = lax.dot_general(a_ref[...].astype(BF16), bv.astype(BF16), dims, preferred_element_type=F32)

        def finish(r):
            if scale is not None:
                r = r * scale
            if add is not None:
                r = r + add_ref[...]
            o_ref[...] = r.astype(out_dtype)

        if nk == 1:
            finish(part)
        else:
            acc, k = refs[-1], pl.program_id(2)

            @pl.when(k == 0)
            def _():
                acc[...] = part

            @pl.when(k != 0)
            def _():
                acc[...] += part

            pl.when(k == nk - 1)(lambda: finish(acc[...]))

    a_spec = pl.BlockSpec((tk, tm), lambda i, j, k: (k, i)) if ta else pl.BlockSpec((tm, tk), lambda i, j, k: (i, k))
    if b_rows is not None:
        b_spec = pl.BlockSpec(b.shape, lambda i, j, k: (0, 0), pipeline_mode=pl.Buffered(1))
    else:
        b_spec = pl.BlockSpec((tn, tk), lambda i, j, k: (j, k)) if tb else pl.BlockSpec((tk, tn), lambda i, j, k: (k, j))
    o_spec = pl.BlockSpec((tm, tn), lambda i, j, k: (i, j))
    ins, specs = [a, b], [a_spec, b_spec]
    if add is not None:
        ins.append(add)
        specs.append(o_spec)
    if dep is not None:
        ins.append(dep)
        specs.append(pl.BlockSpec(memory_space=pl.ANY))
    return pl.pallas_call(
        body, name=name, out_shape=_sds((m, n), out_dtype), grid=(m // tm, n // tn, nk), in_specs=specs,
        out_specs=o_spec, scratch_shapes=[] if nk == 1 else [pltpu.VMEM((tm, tn), F32)],
        compiler_params=_params(("parallel", "parallel", "arbitrary")))(*ins)


def _rms(x):
    return lax.rsqrt(jnp.mean(x * x, axis=-1, keepdims=True) + EPS)


def mixer_input_fwd(x, gain, w_in_t, gq_na, gk_na, gq_sw, gk_sw, *, name, tm=512):
    def body(x_ref, g_ref, w_ref, gqa_ref, gka_ref, gqs_ref, gks_ref, bd_ref, bd2_ref,
             h_ref, z_ref, zg_ref, qa_ref, ka_ref, va_ref, qs_ref, kv_ref):
        x = x_ref[...]
        h = (x * _rms(x) * g_ref[...]).astype(BF16)
        h_ref[...] = h
        z = _dot(h, w_ref[0:ATT_W, :], NT).astype(BF16)
        z_ref[...] = z
        zg_ref[...] = _dot(h, w_ref[ATT_W:IN_W, :], NT).astype(BF16)
        bd = bd_ref[...]

        def norm(v, g, bdm):
            v = v.astype(F32)
            return v * lax.rsqrt(_group_mean(v * v, bdm) + EPS) * g

        qa_ref[...] = (norm(z[:, 0:512], gqa_ref[...], bd) * QK_SCALE).astype(BF16)
        ka_ref[...] = norm(z[:, 512:1024], gka_ref[...], bd).astype(BF16)
        va_ref[...] = z[:, 1024:1536]
        qs_ref[...] = (norm(z[:, 1536:2048], gqs_ref[...], bd) * QK_SCALE).astype(BF16)
        kv_ref[:, 0:128] = norm(z[:, 2048:2176], gks_ref[...], bd2_ref[...]).astype(BF16)
        kv_ref[:, 128:256] = z[:, 2176:2304]

    s = x.shape[0]
    tile = pl.BlockSpec((tm, DM), lambda i: (i, 0))
    vec = lambda w: pl.BlockSpec((1, w), lambda i: (0, 0))
    att = pl.BlockSpec((tm, 512), lambda i: (i, 0))
    g512 = lambda g: jnp.tile(g.reshape(1, HD), (1, 8))
    q = _sds((s, 512), BF16)
    return pl.pallas_call(
        body, name=name,
        out_shape=(_sds((s, DM), BF16), _sds((s, ATT_W), BF16), _sds((s, GATE_W), BF16), q, q, q, q, _sds((s, 256), BF16)),
        grid=(s // tm,),
        in_specs=[tile, vec(DM), pl.BlockSpec((IN_W, DM), lambda i: (0, 0), pipeline_mode=pl.Buffered(1)),
                  vec(512), vec(512), vec(512), vec(128), pl.BlockSpec((512, 512), lambda i: (0, 0)),
                  pl.BlockSpec((128, 128), lambda i: (0, 0))],
        out_specs=(tile, pl.BlockSpec((tm, ATT_W), lambda i: (i, 0)), pl.BlockSpec((tm, GATE_W), lambda i: (i, 0)),
                   att, att, att, att, pl.BlockSpec((tm, 256), lambda i: (i, 0))),
        compiler_params=_params(("parallel",)))(
            x, gain, w_in_t, g512(gq_na), g512(gk_na), g512(gq_sw), jnp.tile(gk_sw.reshape(1, HD), (1, 2)),
            _block_diag(512), _block_diag(128))


def _rms_bwd_math(dh, x, gain):
    r = _rms(x)
    xh = x * r
    dgain = jnp.sum(dh * xh, axis=0, keepdims=True)
    dxn = dh * gain
    dx = r * (dxn - xh * jnp.mean(dxn * xh, axis=-1, keepdims=True))
    return dx, dgain


def mixer_input_bwd(dz, dzg, w_in_t, x, gain, dres, *, name, tm=512, dep=None):
    def body(dz_ref, dzg_ref, w_ref, x_ref, g_ref, dres_ref, *rest):
        dx_ref, dxb_ref, dg_ref = rest[-3:]

        @pl.when(pl.program_id(0) == 0)
        def _():
            dg_ref[...] = jnp.zeros_like(dg_ref)

        dh = _dot(dz_ref[...], w_ref[0:ATT_W, :]) + _dot(dzg_ref[...], w_ref[ATT_W:IN_W, :])
        dx, dg = _rms_bwd_math(dh, x_ref[...], g_ref[...])
        dx = dres_ref[...] + dx
        dx_ref[...] = dx
        dxb_ref[...] = dx.astype(BF16)
        dg_ref[...] += dg

    s = x.shape[0]
    tile = pl.BlockSpec((tm, DM), lambda i: (i, 0))
    vec = pl.BlockSpec((1, DM), lambda i: (0, 0))
    ins, specs = _with_dep(
        [dz, dzg, w_in_t, x, gain, dres],
        [pl.BlockSpec((tm, ATT_W), lambda i: (i, 0)), pl.BlockSpec((tm, GATE_W), lambda i: (i, 0)),
         pl.BlockSpec((IN_W, DM), lambda i: (0, 0), pipeline_mode=pl.Buffered(1)), tile, vec, tile], dep)
    return pl.pallas_call(
        body, name=name, out_shape=(_sds((s, DM), F32), _sds((s, DM), BF16), _sds((1, DM), F32)), grid=(s // tm,),
        in_specs=specs, out_specs=(tile, tile, vec), compiler_params=_params(("arbitrary",)))(*ins)


def mixer_input_dw(dz, dzg, h, *, name, tr=256):
    att_tiles = ATT_W // tr

    def body(dz_ref, dzg_ref, h_ref, o_ref):
        t = pl.program_id(0)
        cols = jnp.where(t < att_tiles, dz_ref[...], dzg_ref[...])
        o_ref[...] = _dot(cols, h_ref[...], TN).astype(BF16)

    s = h.shape[0]
    return pl.pallas_call(
        body, name=name, out_shape=_sds((IN_W, DM), BF16), grid=(IN_W // tr,),
        in_specs=[pl.BlockSpec((s, tr), lambda t: (0, jnp.minimum(t, att_tiles - 1))),
                  pl.BlockSpec((s, tr), lambda t: (0, jnp.maximum(t - att_tiles, 0))),
                  pl.BlockSpec((s, DM), lambda t: (0, 0))],
        out_specs=pl.BlockSpec((tr, DM), lambda t: (t, 0)), compiler_params=_params(("parallel",)))(dz, dzg, h)


def mixer_output_dw(merged, dy, o_na, dpa, o_sw, dps, *, name):
    def body(m_ref, dy_ref, ona_ref, dpa_ref, osw_ref, dps_ref, gwo_ref, gna_ref, gsw_ref):
        gwo_ref[...] = _dot(m_ref[...], dy_ref[...], TN).astype(BF16)
        width = DM // NSH
        for o_ref, dp_ref, out_ref in ((ona_ref, dpa_ref, gna_ref), (osw_ref, dps_ref, gsw_ref)):
            full = _dot(o_ref[...], dp_ref[...], TN).astype(BF16)
            for j in range(NSH):
                out_ref[j] = full[:, j * width:(j + 1) * width]

    branch = _sds((NSH, 512, DM // NSH), BF16)
    return pl.pallas_call(
        body, name=name, out_shape=(_sds((DM, DM), BF16), branch, branch), compiler_params=_params())(
            merged, dy, o_na, dpa, o_sw, dps)


def _with_dep(ins, specs, dep):
    if dep is None:
        return ins, specs
    return ins + [dep], specs + [pl.BlockSpec(memory_space=pl.ANY)]


def _resident_weight():
    return pl.BlockSpec((DFF, DM), lambda i: (0, 0), pipeline_mode=pl.Buffered(1))


def ffn_fwd(x, gain, wg, wu, wd, target=None, *, name, tm=512):
    def body(x_ref, g_ref, wg_ref, wu_ref, wd_ref, *rest):
        h_ref, gg_ref, uu_ref = rest[-3:]
        x = x_ref[...]
        h = (x * _rms(x) * g_ref[...]).astype(BF16)
        h_ref[...] = h
        gg = _dot(h, wg_ref[...], NT)
        uu = _dot(h, wu_ref[...], NT)
        gg_ref[...] = gg.astype(BF16)
        uu_ref[...] = uu.astype(BF16)
        act = (gg * jax.nn.sigmoid(gg) * uu).astype(BF16)
        y = x + 0.5 * _dot(act, wd_ref[...])
        if target is None:
            rest[0][...] = y
            return
        t_ref, dy_ref, dyb_ref, l_ref = rest[:4]

        @pl.when(pl.program_id(0) == 0)
        def _():
            l_ref[...] = jnp.zeros_like(l_ref)

        err = y - t_ref[...]
        dy = err * (1.0 / DM)
        dy_ref[...] = dy
        dyb_ref[...] = dy.astype(BF16)
        l_ref[...] += 0.5 * jnp.sum(jnp.mean(err * err, axis=-1, keepdims=True), axis=0, keepdims=True)

    s = x.shape[0]
    tile = pl.BlockSpec((tm, DM), lambda i: (i, 0))
    hid = pl.BlockSpec((tm, DFF), lambda i: (i, 0))
    w = _resident_weight()
    saved_shapes = (_sds((s, DM), BF16), _sds((s, DFF), BF16), _sds((s, DFF), BF16))
    ins, specs = [x, gain, wg, wu, wd], [tile, pl.BlockSpec((1, DM), lambda i: (0, 0)), w, w, w]
    if target is None:
        head_shapes, head_specs = (_sds((s, DM), F32),), (tile,)
    else:
        ins, specs = ins + [target], specs + [tile]
        head_shapes = (_sds((s, DM), F32), _sds((s, DM), BF16), _sds((1, 128), F32))
        head_specs = (tile, tile, pl.BlockSpec((1, 128), lambda i: (0, 0)))
    return pl.pallas_call(
        body, name=name, out_shape=head_shapes + saved_shapes, grid=(s // tm,), in_specs=specs,
        out_specs=head_specs + (tile, hid, hid),
        compiler_params=_params(("parallel",) if target is None else ("arbitrary",)))(*ins)


def ffn_bwd_tokens(dy, x, gain, gg, uu, wg, wu, wd, *, name, tm=256, dep=None):
    def body(dy_ref, x_ref, g_ref, gg_ref, uu_ref, wg_ref, wu_ref, wd_ref, *rest):
        dx_ref, dxb_ref, dgain_ref, act_ref, dg_ref, du_ref = rest[-6:]

        @pl.when(pl.program_id(0) == 0)
        def _():
            dgain_ref[...] = jnp.zeros_like(dgain_ref)

        dy = dy_ref[...]
        dact = _dot((0.5 * dy).astype(BF16), wd_ref[...], NT)
        g = gg_ref[...].astype(F32)
        u = uu_ref[...].astype(F32)
        sg = jax.nn.sigmoid(g)
        silu = g * sg
        act_ref[...] = (silu * u).astype(BF16)
        dg = (dact * u * (sg * (1.0 + g * (1.0 - sg)))).astype(BF16)
        du = (dact * silu).astype(BF16)
        dg_ref[...] = dg
        du_ref[...] = du
        dx, dgain = _rms_bwd_math(_dot(dg, wg_ref[...]) + _dot(du, wu_ref[...]), x_ref[...], g_ref[...])
        dx = dy + dx
        dx_ref[...] = dx
        dxb_ref[...] = dx.astype(BF16)
        dgain_ref[...] += dgain

    s = x.shape[0]
    tile = pl.BlockSpec((tm, DM), lambda i: (i, 0))
    vec = pl.BlockSpec((1, DM), lambda i: (0, 0))
    hid = pl.BlockSpec((tm, DFF), lambda i: (i, 0))
    hshape = _sds((s, DFF), BF16)
    w = _resident_weight()
    ins, specs = _with_dep([dy, x, gain, gg, uu, wg, wu, wd], [tile, tile, vec, hid, hid, w, w, w], dep)
    return pl.pallas_call(
        body, name=name, out_shape=(_sds((s, DM), F32), _sds((s, DM), BF16), _sds((1, DM), F32), hshape, hshape, hshape),
        grid=(s // tm,), in_specs=specs, out_specs=(tile, tile, vec, hid, hid, hid),
        compiler_params=_params(("arbitrary",)))(*ins)


def ffn_bwd_weights(h, dy, act, dg, du, *, name, tf=256):
    def body(h_ref, dy_ref, act_ref, dg_ref, du_ref, gwg_ref, gwu_ref, gwd_ref):
        h = h_ref[...]
        gwg_ref[...] = _dot(dg_ref[...], h, TN).astype(BF16)
        gwu_ref[...] = _dot(du_ref[...], h, TN).astype(BF16)
        gwd_ref[...] = (0.5 * _dot(act_ref[...], dy_ref[...], TN)).astype(BF16)

    s = h.shape[0]
    full = pl.BlockSpec((s, DM), lambda f: (0, 0))
    hid = pl.BlockSpec((s, tf), lambda f: (0, f))
    wt = pl.BlockSpec((tf, DM), lambda f: (f, 0))
    wshape = _sds((DFF, DM), BF16)
    return pl.pallas_call(
        body, name=name, out_shape=(wshape, wshape, wshape), grid=(DFF // tf,), in_specs=[full, full, hid, hid, hid],
        out_specs=(wt, wt, wt), compiler_params=_params(("parallel",)))(h, dy, act, dg, du)


def _group_mean(v, bd):
    hi = v.astype(BF16)
    lo = (v - hi.astype(F32)).astype(BF16)
    return _dot(hi, bd) + _dot(lo, bd)


def _block_diag(width):
    idx = np.arange(width) // HD
    return jnp.asarray((idx[:, None] == idx[None, :]).astype(np.float32) / HD, dtype=BF16)


def qknorm_bwd(z, dqa, dka, dva, dqs, dkv, gq_na, gk_na, gq_sw, gk_sw, *, name, tm=256):
    def body(zq_ref, zk_ref, zs_ref, zkv_ref, dqa_ref, dka_ref, dva_ref, dqs_ref, dkv_ref, gqa_ref, gka_ref, gqs_ref,
             gks_ref, bd_ref, bd2_ref, dz_ref, dgqa_ref, dgka_ref, dgqs_ref, dgks_ref):
        @pl.when(pl.program_id(0) == 0)
        def _():
            dgqa_ref[...] = jnp.zeros_like(dgqa_ref)
            dgka_ref[...] = jnp.zeros_like(dgka_ref)
            dgqs_ref[...] = jnp.zeros_like(dgqs_ref)
            dgks_ref[...] = jnp.zeros_like(dgks_ref)

        bd = bd_ref[...]

        def bwd(x, dy, g, bdm, dg_ref):
            x = x.astype(F32)
            r = lax.rsqrt(_group_mean(x * x, bdm) + EPS)
            xh = x * r
            dg_ref[...] += jnp.sum(dy * xh, axis=0, keepdims=True)
            dxn = dy * g
            return r * (dxn - xh * _group_mean(dxn * xh, bdm))

        dz_ref[:, 0:512] = bwd(zq_ref[...], dqa_ref[...] * QK_SCALE, gqa_ref[...], bd, dgqa_ref).astype(BF16)
        dz_ref[:, 512:1024] = bwd(zk_ref[...], dka_ref[...], gka_ref[...], bd, dgka_ref).astype(BF16)
        dz_ref[:, 1024:1536] = dva_ref[...].astype(BF16)
        dz_ref[:, 1536:2048] = bwd(zs_ref[...], dqs_ref[...] * QK_SCALE, gqs_ref[...], bd, dgqs_ref).astype(BF16)
        dkv = dkv_ref[...]
        dz_ref[:, 2048:2176] = bwd(zkv_ref[:, 0:128], dkv[:, 0:128], gks_ref[...], bd2_ref[...], dgks_ref).astype(BF16)
        dz_ref[:, 2176:2304] = dkv[:, 128:256].astype(BF16)

    s = z.shape[0]
    col = lambda j: pl.BlockSpec((tm, 512), lambda i, j=j: (i, j))
    t512 = pl.BlockSpec((tm, 512), lambda i: (i, 0))
    t256 = pl.BlockSpec((tm, 256), lambda i: (i, 0))
    vec = lambda w: pl.BlockSpec((1, w), lambda i: (0, 0))
    g512 = lambda g: jnp.tile(g.reshape(1, HD), (1, 8))
    return pl.pallas_call(
        body, name=name,
        out_shape=(_sds((s, ATT_W), BF16), _sds((1, 512), F32), _sds((1, 512), F32), _sds((1, 512), F32), _sds((1, 128), F32)),
        grid=(s // tm,),
        in_specs=[col(0), col(1), col(3), pl.BlockSpec((tm, 256), lambda i: (i, 8)), t512, t512, t512, t512, t256,
                  vec(512), vec(512), vec(512), vec(128), pl.BlockSpec((512, 512), lambda i: (0, 0)),
                  pl.BlockSpec((128, 128), lambda i: (0, 0))],
        out_specs=(pl.BlockSpec((tm, ATT_W), lambda i: (i, 0)), vec(512), vec(512), vec(512), vec(128)),
        compiler_params=_params(("arbitrary",)))(
            z, z, z, z, dqa, dka, dva, dqs, dkv, g512(gq_na), g512(gk_na), g512(gq_sw),
            jnp.tile(gk_sw.reshape(1, HD), (1, 2)), _block_diag(512), _block_diag(128))


def _na_row_start(r):
    return jnp.clip(r - NA_WR // 2, 0, ROWS - NA_WR)


def na_bias_table(rpb, *, name):
    t = jnp.pad(rpb, ((0, 0), (0, 2), (0, HD - (2 * NA_WC - 1))))
    pairs = jnp.concatenate([t[:, :16], t[:, 1:17]], axis=-1).reshape(NA_HEADS, 16, 1, 128)

    def body(t_ref, o_ref):
        p = pl.program_id(0)
        q = lax.broadcasted_iota(jnp.int32, (GRID_W, 128), 0)
        kc = lax.broadcasted_iota(jnp.int32, (GRID_W, 128), 1) & (GRID_W - 1)
        cs = jnp.clip(q - NA_WC // 2, 0, GRID_W - NA_WC)
        ok = (kc >= cs) & (kc < cs + NA_WC)
        for h in range(NA_HEADS):
            for pr in range(NA_WR // 2):
                x = jnp.broadcast_to(t_ref[h, 2 * pr - p + NA_WR - 1], (GRID_W, 128))
                b = pltpu.roll(x, 128 - (NA_WC - 1), 1, stride=1, stride_axis=0)
                o_ref[h, :, 128 * pr:128 * pr + 128] = jnp.where(ok, b, NEG)

    return pl.pallas_call(
        body, name=name, out_shape=_sds((NA_WR, NA_HEADS, GRID_W, NA_KEYS), F32), grid=(NA_WR,),
        in_specs=[pl.BlockSpec((NA_HEADS, 16, 1, 128), lambda p: (0, 0, 0, 0))],
        out_specs=pl.BlockSpec((None, NA_HEADS, GRID_W, NA_KEYS), lambda p: (p, 0, 0, 0)),
        compiler_params=_params(("parallel",)))(pairs)


def _lane_halves():
    lane = lax.broadcasted_iota(jnp.int32, (1, 128), 1)
    return lane < HD


def na_fwd(q, k, v, bias, *, name):
    def body(q_ref, k_ref, v_ref, b_ref, o_ref, lse_ref):
        r = pl.program_id(0)
        off = pl.multiple_of(_na_row_start(r) * GRID_W, GRID_W)
        first = _lane_halves()
        sels = [first, jnp.logical_not(first)]
        lanes = [slice(128 * j, 128 * j + 128) for j in range(NA_HEADS // 2)]
        q2s = [q_ref[:, l] for l in lanes]
        k2s = [k_ref[pl.ds(off, NA_KEYS), l] for l in lanes]
        v2s = [v_ref[pl.ds(off, NA_KEYS), l] for l in lanes]
        scores = []
        for h in range(NA_HEADS):
            j, half = divmod(h, 2)
            scores.append(_dot(jnp.where(sels[half], q2s[j], jnp.zeros_like(q2s[j])), k2s[j], NT))
        probs, lses = [], []
        for h in range(NA_HEADS):
            b = b_ref[h]
            s = jnp.where(b > 0.5 * NEG, scores[h] + b, NEG)
            m = jnp.max(s, axis=-1, keepdims=True)
            e = jnp.exp(s - m)
            l = jnp.sum(e, axis=-1, keepdims=True)
            probs.append((e / l).astype(BF16))
            lses.append(m + jnp.log(l))
        for j in range(NA_HEADS // 2):
            zero = jnp.zeros_like(v2s[j])
            o2 = (_dot(probs[2 * j], jnp.where(sels[0], v2s[j], zero))
                  + _dot(probs[2 * j + 1], jnp.where(sels[1], v2s[j], zero)))
            o_ref[:, lanes[j]] = o2.astype(BF16)
        lse_ref[...] = jnp.concatenate(lses, axis=1)

    s_tok = q.shape[0]
    full = pl.BlockSpec((s_tok, 512), lambda r: (0, 0))
    return pl.pallas_call(
        body, name=name, out_shape=(_sds((s_tok, 512), BF16), _sds((s_tok, NA_HEADS), F32)), grid=(ROWS,),
        in_specs=[pl.BlockSpec((GRID_W, 512), lambda r: (r, 0)), full, full,
                  pl.BlockSpec((None, NA_HEADS, GRID_W, NA_KEYS), lambda r: (r - _na_row_start(r), 0, 0, 0))],
        out_specs=(pl.BlockSpec((GRID_W, 512), lambda r: (r, 0)), pl.BlockSpec((GRID_W, NA_HEADS), lambda r: (r, 0))),
        compiler_params=_params(("parallel",)))(q, k, v, bias)


def na_bwd(q, k, v, o, do, lse, bias, *, name):
    def body(q_ref, k_ref, v_ref, o_ref, do_ref, lse_ref, b_ref, dq_ref, dk_ref, dv_ref, db_ref):
        r = pl.program_id(0)

        @pl.when(r == 0)
        def _():
            dk_ref[...] = jnp.zeros_like(dk_ref)
            dv_ref[...] = jnp.zeros_like(dv_ref)

        @pl.when((r <= NA_WR // 2) | (r > ROWS - NA_WR // 2))
        def _():
            db_ref[...] = jnp.zeros_like(db_ref)

        off = pl.multiple_of(_na_row_start(r) * GRID_W, GRID_W)
        first = _lane_halves()
        sels = [first, jnp.logical_not(first)]
        lanes = [slice(128 * j, 128 * j + 128) for j in range(NA_HEADS // 2)]
        q2s = [q_ref[:, l] for l in lanes]
        k2s = [k_ref[pl.ds(off, NA_KEYS), l] for l in lanes]
        v2s = [v_ref[pl.ds(off, NA_KEYS), l] for l in lanes]
        do2s = [do_ref[:, l] for l in lanes]
        prods = [do2s[j].astype(F32) * o_ref[:, lanes[j]].astype(F32) for j in range(NA_HEADS // 2)]
        lse = lse_ref[...]
        qhs, dohs, scores, dps = [], [], [], []
        for h in range(NA_HEADS):
            j, half = divmod(h, 2)
            qhs.append(jnp.where(sels[half], q2s[j], jnp.zeros_like(q2s[j])))
            dohs.append(jnp.where(sels[half], do2s[j], jnp.zeros_like(do2s[j])))
            scores.append(_dot(qhs[h], k2s[j], NT))
            dps.append(_dot(dohs[h], v2s[j], NT))
        pbs, dsbs = [], []
        for h in range(NA_HEADS):
            j, half = divmod(h, 2)
            b = b_ref[h]
            s = jnp.where(b > 0.5 * NEG, scores[h] + b, NEG)
            p = jnp.exp(s - lse[:, h:h + 1])
            delta = jnp.sum(jnp.where(sels[half], prods[j], 0.0), axis=-1, keepdims=True)
            ds = p * (dps[h] - delta)
            db_ref[h] += ds
            pbs.append(p.astype(BF16))
            dsbs.append(ds.astype(BF16))
        for j in range(NA_HEADS // 2):
            a, b = 2 * j, 2 * j + 1
            zero = jnp.zeros_like(k2s[j])
            dq_ref[:, lanes[j]] = (_dot(dsbs[a], jnp.where(sels[0], k2s[j], zero))
                                   + _dot(dsbs[b], jnp.where(sels[1], k2s[j], zero)))
            dk_ref[pl.ds(off, NA_KEYS), lanes[j]] += _dot(dsbs[a], qhs[a], TN) + _dot(dsbs[b], qhs[b], TN)
            dv_ref[pl.ds(off, NA_KEYS), lanes[j]] += _dot(pbs[a], dohs[a], TN) + _dot(pbs[b], dohs[b], TN)

    s_tok = q.shape[0]
    full = pl.BlockSpec((s_tok, 512), lambda r: (0, 0))
    row = pl.BlockSpec((GRID_W, 512), lambda r: (r, 0))
    bias_spec = pl.BlockSpec((None, NA_HEADS, GRID_W, NA_KEYS), lambda r: (r - _na_row_start(r), 0, 0, 0))
    return pl.pallas_call(
        body, name=name,
        out_shape=(_sds((s_tok, 512), F32), _sds((s_tok, 512), F32), _sds((s_tok, 512), F32),
                   _sds((NA_WR, NA_HEADS, GRID_W, NA_KEYS), F32)),
        grid=(ROWS,),
        in_specs=[row, full, full, row, row, pl.BlockSpec((GRID_W, NA_HEADS), lambda r: (r, 0)), bias_spec],
        out_specs=(row, full, full, bias_spec), compiler_params=_params(("arbitrary",)))(q, k, v, o, do, lse, bias)


def t5_bucket_map():
    rel = np.arange(SW_KEYS)[None, :] - SW_BLK - np.arange(SW_BLK)[:, None]
    nb = 16
    max_exact = nb // 2
    n = np.abs(rel)
    large = max_exact + (np.log(np.maximum(n, 1) / max_exact) / np.log(128 / max_exact) * (nb - max_exact)).astype(np.int32)
    large = np.minimum(large, nb - 1)
    return ((rel > 0) * nb + np.where(n < max_exact, n, large)).astype(np.int32)


def t5_bias(table, *, name):
    rel = np.arange(-SW_BLK, SW_BLK + 1)
    nb, max_exact = 16, 8
    n = np.abs(rel)
    large = max_exact + (np.log(np.maximum(n, 1) / max_exact) / np.log(128 / max_exact) * (nb - max_exact)).astype(np.int32)
    bucket = ((rel > 0) * nb + np.where(n < max_exact, n, np.minimum(large, nb - 1))).astype(np.int32)
    u = jnp.pad(table[jnp.asarray(bucket)].T, ((0, 0), (0, SW_KEYS - bucket.shape[0]))).reshape(8, 1, SW_KEYS)

    def body(u_ref, o_ref):
        for h in range(8):
            x = jnp.broadcast_to(u_ref[h], (SW_BLK, SW_KEYS))
            o_ref[h] = pltpu.roll(x, 0, 1, stride=1, stride_axis=0)

    return pl.pallas_call(body, name=name, out_shape=_sds((8, SW_BLK, SW_KEYS), F32), compiler_params=_params())(u)


def _sw_valid(n):
    a = lax.broadcasted_iota(jnp.int32, (SW_BLK, SW_KEYS), 0)
    j = lax.broadcasted_iota(jnp.int32, (SW_BLK, SW_KEYS), 1)
    kpos = (n - 1) * SW_BLK + j
    return (jnp.abs(j - SW_BLK - a) <= SW_BLK) & (kpos >= 0) & (kpos < SEQ)


def _dup_group(x2, g, first):
    rolled = pltpu.roll(x2, HD, 1)
    return jnp.where(first, x2, rolled) if g == 0 else jnp.where(first, rolled, x2)


def sw_fwd(q, kv, t5, sink, *, name):
    def body(q_ref, kv_ref, t5_ref, sink_ref, o_ref, lse_ref):
        n = pl.program_id(0)
        off = pl.multiple_of(n * SW_BLK, SW_BLK)
        first = _lane_halves()
        sels = [first, jnp.logical_not(first)]
        valid = _sw_valid(n)
        k2 = kv_ref[pl.ds(off, SW_KEYS), 0:128]
        v2 = kv_ref[pl.ds(off, SW_KEYS), 128:256]
        kk = [_dup_group(k2, g, first) for g in range(2)]
        vv = [_dup_group(v2, g, first) for g in range(2)]
        q2s = [q_ref[:, 128 * j:128 * j + 128] for j in range(4)]
        scores = []
        for h in range(8):
            j, half = divmod(h, 2)
            scores.append(_dot(jnp.where(sels[half], q2s[j], jnp.zeros_like(q2s[j])), kk[j // 2], NT))
        probs, lses = [], []
        for h in range(8):
            s = jnp.where(valid, scores[h] + t5_ref[h], NEG)
            snk = sink_ref[h]
            m = jnp.maximum(jnp.max(s, axis=-1, keepdims=True), snk)
            e = jnp.exp(s - m)
            den = jnp.sum(e, axis=-1, keepdims=True) + jnp.exp(snk - m)
            probs.append((e / den).astype(BF16))
            lses.append(m + jnp.log(den))
        outs = []
        for j in range(4):
            vg = vv[j // 2]
            zero = jnp.zeros_like(vg)
            outs.append(_dot(probs[2 * j], jnp.where(sels[0], vg, zero)) + _dot(probs[2 * j + 1], jnp.where(sels[1], vg, zero)))
        o_ref[...] = jnp.concatenate(outs, axis=1).astype(BF16)
        lse_ref[...] = jnp.concatenate(lses, axis=1)

    s_tok = q.shape[0]
    blk = pl.BlockSpec((SW_BLK, 512), lambda n: (n, 0))
    return pl.pallas_call(
        body, name=name, out_shape=(_sds((s_tok, 512), BF16), _sds((s_tok, 8), F32)), grid=(SW_NB,),
        in_specs=[blk, pl.BlockSpec(kv.shape, lambda n: (0, 0)), pl.BlockSpec((8, SW_BLK, SW_KEYS), lambda n: (0, 0, 0)),
                  pl.BlockSpec(memory_space=pltpu.SMEM)],
        out_specs=(blk, pl.BlockSpec((SW_BLK, 8), lambda n: (n, 0))), compiler_params=_params(("parallel",)))(q, kv, t5, sink)


def sw_bwd(q, kv, o, do, lse, t5, sink, *, name):
    def body(q_ref, kv_ref, o_ref, do_ref, lse_ref, t5_ref, sink_ref, dq_ref, dkv_ref, dt5_ref, dsink_ref):
        n = pl.program_id(0)

        @pl.when(n == 0)
        def _():
            dkv_ref[...] = jnp.zeros_like(dkv_ref)
            dt5_ref[...] = jnp.zeros_like(dt5_ref)
            dsink_ref[...] = jnp.zeros_like(dsink_ref)

        off = pl.multiple_of(n * SW_BLK, SW_BLK)
        first = _lane_halves()
        sels = [first, jnp.logical_not(first)]
        valid = _sw_valid(n)
        k2 = kv_ref[pl.ds(off, SW_KEYS), 0:128]
        v2 = kv_ref[pl.ds(off, SW_KEYS), 128:256]
        kk = [_dup_group(k2, g, first) for g in range(2)]
        vv = [_dup_group(v2, g, first) for g in range(2)]
        lanes = [slice(128 * j, 128 * j + 128) for j in range(4)]
        q2s = [q_ref[:, l] for l in lanes]
        do2s = [do_ref[:, l] for l in lanes]
        prods = [do2s[j].astype(F32) * o_ref[:, lanes[j]].astype(F32) for j in range(4)]
        lse = lse_ref[...]
        qhs, dohs, scores, dps = [], [], [], []
        for h in range(8):
            j, half = divmod(h, 2)
            qhs.append(jnp.where(sels[half], q2s[j], jnp.zeros_like(q2s[j])))
            dohs.append(jnp.where(sels[half], do2s[j], jnp.zeros_like(do2s[j])))
            scores.append(_dot(qhs[h], kk[j // 2], NT))
            dps.append(_dot(dohs[h], vv[j // 2], NT))
        pbs, dsbs, dss, dsinks = [], [], [], []
        for h in range(8):
            j, half = divmod(h, 2)
            s = jnp.where(valid, scores[h] + t5_ref[h], NEG)
            lse_h = lse[:, h:h + 1]
            p = jnp.exp(s - lse_h)
            delta = jnp.sum(jnp.where(sels[half], prods[j], 0.0), axis=-1, keepdims=True)
            ds = p * (dps[h] - delta)
            dss.append(ds)
            dsinks.append(-jnp.sum(jnp.exp(sink_ref[h] - lse_h) * delta, axis=0, keepdims=True))
            pbs.append(p.astype(BF16))
            dsbs.append(ds.astype(BF16))
        dt5_ref[...] += jnp.stack(dss)
        dsink_ref[...] += jnp.concatenate(dsinks, axis=1)
        dqs = []
        for j in range(4):
            a, b = 2 * j, 2 * j + 1
            zero = jnp.zeros_like(kk[j // 2])
            dqs.append(_dot(dsbs[a], jnp.where(sels[0], kk[j // 2], zero)) + _dot(dsbs[b], jnp.where(sels[1], kk[j // 2], zero)))
        dq_ref[...] = jnp.concatenate(dqs, axis=1)
        dk_groups, dv_groups = [], []
        for g in range(2):
            dkk = sum(_dot(dsbs[h], qhs[h], TN) for h in range(4 * g, 4 * g + 4))
            dvv = sum(_dot(pbs[h], dohs[h], TN) for h in range(4 * g, 4 * g + 4))
            dk_groups.append(dkk + pltpu.roll(dkk, HD, 1))
            dv_groups.append(dvv + pltpu.roll(dvv, HD, 1))
        dkv_ref[pl.ds(off, SW_KEYS), :] += jnp.concatenate(
            [jnp.where(first, dk_groups[0], dk_groups[1]), jnp.where(first, dv_groups[0], dv_groups[1])], axis=1)

    s_tok = q.shape[0]
    blk = pl.BlockSpec((SW_BLK, 512), lambda n: (n, 0))
    kv_spec = pl.BlockSpec(kv.shape, lambda n: (0, 0))
    t5_spec = pl.BlockSpec((8, SW_BLK, SW_KEYS), lambda n: (0, 0, 0))
    vec = pl.BlockSpec((1, 8), lambda n: (0, 0))
    return pl.pallas_call(
        body, name=name,
        out_shape=(_sds((s_tok, 512), F32), _sds(kv.shape, F32), _sds((8, SW_BLK, SW_KEYS), F32), _sds((1, 8), F32)),
        grid=(SW_NB,), in_specs=[blk, kv_spec, blk, blk, pl.BlockSpec((SW_BLK, 8), lambda n: (n, 0)), t5_spec,
                                 pl.BlockSpec(memory_space=pltpu.SMEM)],
        out_specs=(blk, kv_spec, t5_spec, vec), compiler_params=_params(("arbitrary",)))(q, kv, o, do, lse, t5, sink)


def mixer_output_fwd(o_na, o_sw, zg, bias, wa, ws, wo, res, *, name, tm=512):
    def body(ona_ref, osw_ref, z0_ref, z1_ref, b0_ref, b1_ref, wa_ref, ws_ref, wo_ref, res_ref, y_ref, pa_ref, ps_ref, m_ref):
        cols = lambda o, w4_ref: jnp.concatenate([_dot(o, w4_ref[j]) for j in range(NSH)], axis=1)
        pa = cols(ona_ref[...], wa_ref).astype(BF16)
        ps = cols(osw_ref[...], ws_ref).astype(BF16)
        pa_ref[...] = pa
        ps_ref[...] = ps
        g0 = jax.nn.sigmoid(z0_ref[...] + b0_ref[...])
        g1 = jax.nn.sigmoid(z1_ref[...] + b1_ref[...])
        merged = (g0 * pa + g1 * ps).astype(BF16)
        m_ref[...] = merged
        y_ref[...] = res_ref[...] + _dot(merged, wo_ref[...])

    s = zg.shape[0]
    half = lambda j: pl.BlockSpec((tm, DM), lambda i, j=j: (i, j))
    bvec = lambda j: pl.BlockSpec((1, DM), lambda i, j=j: (0, j))
    att = pl.BlockSpec((tm, 512), lambda i: (i, 0))
    whole = lambda a: pl.BlockSpec(a.shape, lambda i: (0,) * a.ndim, pipeline_mode=pl.Buffered(1))
    act = _sds((s, DM), BF16)
    return pl.pallas_call(
        body, name=name, out_shape=(_sds((s, DM), F32), act, act, act), grid=(s // tm,),
        in_specs=[att, att, half(0), half(1), bvec(0), bvec(1), whole(wa), whole(ws), whole(wo), half(0)],
        out_specs=(half(0),) * 4, compiler_params=_params(("parallel",)))(o_na, o_sw, zg, zg, bias, bias, wa, ws, wo, res)


def mixer_output_bwd(dy, zg, bias, pa, ps, wa, ws, wo, *, name, tm=512, dep=None):
    def body(dy_ref, z0_ref, z1_ref, b0_ref, b1_ref, pa_ref, ps_ref, wa_ref, ws_ref, wo_ref, *rest):
        dpa_ref, dps_ref, dz_ref, db_ref, dona_ref, dosw_ref = rest[-6:]

        @pl.when(pl.program_id(0) == 0)
        def _():
            db_ref[...] = jnp.zeros_like(db_ref)

        dm = _dot(dy_ref[...].astype(BF16), wo_ref[...], NT)
        g0 = jax.nn.sigmoid(z0_ref[...] + b0_ref[...])
        g1 = jax.nn.sigmoid(z1_ref[...] + b1_ref[...])
        dpa = (dm * g0).astype(BF16)
        dps = (dm * g1).astype(BF16)
        dpa_ref[...] = dpa
        dps_ref[...] = dps
        dz0 = dm * pa_ref[...] * g0 * (1.0 - g0)
        dz1 = dm * ps_ref[...] * g1 * (1.0 - g1)
        dz_ref[:, 0:DM] = dz0.astype(BF16)
        dz_ref[:, DM:2 * DM] = dz1.astype(BF16)
        db_ref[:, 0:DM] += jnp.sum(dz0, axis=0, keepdims=True)
        db_ref[:, DM:2 * DM] += jnp.sum(dz1, axis=0, keepdims=True)
        width = DM // NSH
        back = lambda dp, w4_ref: sum(_dot(dp[:, j * width:(j + 1) * width], w4_ref[j], NT) for j in range(NSH))
        dona_ref[...] = back(dpa, wa_ref).astype(BF16)
        dosw_ref[...] = back(dps, ws_ref).astype(BF16)

    s = zg.shape[0]
    half = lambda j: pl.BlockSpec((tm, DM), lambda i, j=j: (i, j))
    bvec = lambda j: pl.BlockSpec((1, DM), lambda i, j=j: (0, j))
    att = pl.BlockSpec((tm, 512), lambda i: (i, 0))
    whole = lambda a: pl.BlockSpec(a.shape, lambda i: (0,) * a.ndim, pipeline_mode=pl.Buffered(1))
    ins, specs = _with_dep([dy, zg, zg, bias, bias, pa, ps, wa, ws, wo],
                           [half(0), half(0), half(1), bvec(0), bvec(1), half(0), half(0), whole(wa), whole(ws), whole(wo)], dep)
    return pl.pallas_call(
        body, name=name,
        out_shape=(_sds((s, DM), BF16), _sds((s, DM), BF16), _sds((s, GATE_W), BF16), _sds((1, GATE_W), F32),
                   _sds((s, 512), BF16), _sds((s, 512), BF16)),
        grid=(s // tm,), in_specs=specs,
        out_specs=(half(0), half(0), pl.BlockSpec((tm, GATE_W), lambda i: (i, 0)), pl.BlockSpec((1, GATE_W), lambda i: (0, 0)),
                   att, att),
        compiler_params=_params(("arbitrary",)))(*ins)


def adamw_small(ws, gs, ms, vs, *, name):
    cnt = len(ws)

    def body(*refs):
        ins, outs = refs[:4 * cnt], refs[4 * cnt:]
        for i in range(cnt):
            w_ref, g_ref, m_ref, v_ref = ins[4 * i:4 * i + 4]
            d_ref, nm_ref, nv_ref = outs[3 * i:3 * i + 3]
            g = g_ref[...]
            nm = ADAM_B1 * m_ref[...] + (1.0 - ADAM_B1) * g
            nv = ADAM_B2 * v_ref[...] + (1.0 - ADAM_B2) * jnp.square(g)
            m_hat = nm / (1.0 - ADAM_B1 ** ADAM_STEP)
            v_hat = nv / (1.0 - ADAM_B2 ** ADAM_STEP)
            d_ref[...] = -ADAM_LR * (m_hat / (jnp.sqrt(v_hat) + ADAM_EPS) + ADAM_WD * w_ref[...])
            nm_ref[...] = nm
            nv_ref[...] = nv

    flat = [a for i in range(cnt) for a in (ws[i], gs[i], ms[i], vs[i])]
    res = pl.pallas_call(
        body, name=name, out_shape=tuple(_sds(ws[i].shape, F32) for i in range(cnt) for _ in range(3)),
        compiler_params=_params())(*flat)
    return [tuple(res[3 * i:3 * i + 3]) for i in range(cnt)]


def adamw_layer(ws, ms, vs, mines, theirs, cidx, layer, filled=None, *, name):
    cnt = len(ws)
    _, k, n = ws[0].shape
    nt = 2
    tk = k // 2 // nt

    def body(c_ref, *refs):
        own = pl.program_id(0) == c_ref[0]
        outs = refs[-4 * cnt:]
        for i in range(cnt):
            w_ref, m_ref, v_ref, a_ref, b_ref = refs[5 * i:5 * i + 5]
            g_ref, d_ref, nm_ref, nv_ref = outs[4 * i:4 * i + 4]
            g = jnp.where(own, a_ref[...], b_ref[...])
            g_ref[...] = g
            nm = ADAM_B1 * m_ref[...] + (1.0 - ADAM_B1) * g
            nv = ADAM_B2 * v_ref[...] + (1.0 - ADAM_B2) * jnp.square(g)
            m_hat = nm / (1.0 - ADAM_B1 ** ADAM_STEP)
            v_hat = nv / (1.0 - ADAM_B2 ** ADAM_STEP)
            d_ref[...] = -ADAM_LR * (m_hat / (jnp.sqrt(v_hat) + ADAM_EPS) + ADAM_WD * w_ref[...])
            nm_ref[...] = nm
            nv_ref[...] = nv

    full = pl.BlockSpec((None, tk, n), lambda hf, t, c: (layer, hf * nt + t, 0))
    half_mine = pl.BlockSpec((tk, n), lambda hf, t, c: (jnp.where(hf == c[0], t, 0), 0))
    half_theirs = pl.BlockSpec((tk, n), lambda hf, t, c: (jnp.where(hf != c[0], t, 0), 0))
    out = _sds(ws[0].shape, F32)
    ins, specs, aliases = [cidx], [], {}
    for i in range(cnt):
        ins += [ws[i], ms[i], vs[i], mines[i], theirs[i]]
        specs += [full, full, full, half_mine, half_theirs]
    if filled is not None:
        aliases = {len(ins) + j: j for j in range(4 * cnt)}
        ins += [a for f in filled for a in f]
        specs += [pl.BlockSpec(memory_space=pl.ANY)] * (4 * cnt)
    res = pl.pallas_call(
        body, name=name, out_shape=(out,) * (4 * cnt),
        grid_spec=pltpu.PrefetchScalarGridSpec(
            num_scalar_prefetch=1, grid=(2, nt), in_specs=specs, out_specs=(full,) * (4 * cnt)),
        input_output_aliases=aliases,
        compiler_params=_params(("arbitrary", "arbitrary")))(*ins)
    return [tuple(res[4 * i:4 * i + 4]) for i in range(cnt)]


def t5_table_grad(dt5_a, dt5_b, *, name):
    def body(a_ref, b_ref, map_ref, o_ref):
        d = a_ref[...] + b_ref[...]
        bucket = map_ref[...]
        for b in range(32):
            hit = (bucket == b)[None]
            o_ref[b] = jnp.sum(jnp.sum(jnp.where(hit, d, 0.0), axis=2), axis=1, keepdims=True)

    return pl.pallas_call(
        body, name=name, out_shape=_sds((32, 8, 1), F32), compiler_params=_params())(
            dt5_a, dt5_b, jnp.asarray(t5_bucket_map()))


def rpb_grad(dbias, *, name):
    def body(d_ref, rev_ref, o_ref):
        rev = rev_ref[...]
        for h in range(NA_HEADS):
            for pr in range(NA_WR // 2):
                d = d_ref[h, :, 128 * pr:128 * pr + 128]
                hi = d.astype(BF16)
                lo = (d - hi.astype(F32)).astype(BF16)
                flipped = _dot(rev, hi) + _dot(rev, lo)
                o_ref[h, pr] = jnp.sum(pltpu.roll(flipped, 0, 1, stride=1, stride_axis=0), axis=0, keepdims=True)

    anti = jnp.asarray(np.eye(GRID_W, dtype=np.float32)[::-1], dtype=BF16)
    e = pl.pallas_call(
        body, name=name, out_shape=_sds((NA_WR, NA_HEADS, NA_WR // 2, 1, 128), F32), grid=(NA_WR,),
        in_specs=[pl.BlockSpec((None, NA_HEADS, GRID_W, NA_KEYS), lambda p: (p, 0, 0, 0)),
                  pl.BlockSpec((GRID_W, GRID_W), lambda p: (0, 0))],
        out_specs=pl.BlockSpec((None, NA_HEADS, NA_WR // 2, 1, 128), lambda p: (p, 0, 0, 0, 0)),
        compiler_params=_params(("parallel",)))(dbias, anti)
    nci, nri = 2 * NA_WC - 1, 2 * NA_WR - 1
    e = e.reshape(NA_WR, NA_HEADS, NA_WR // 2, 128).transpose(0, 2, 1, 3).reshape(NA_WR * NA_WR // 2, NA_HEADS, 128)
    parts = jnp.concatenate([e[..., 48:48 + nci], jnp.concatenate([e[..., 112:128], e[..., 0:nci - 16]], axis=-1)], axis=0)
    p, pr = np.arange(NA_WR)[:, None], np.arange(NA_WR // 2)[None, :]
    ri = np.concatenate([(2 * pr - p + NA_WR - 1).reshape(-1), (2 * pr - p + NA_WR).reshape(-1)])
    pick = jnp.asarray((ri[None, :] == np.arange(16)[:, None]).astype(np.float32))
    out = mm(pick, parts.reshape(2 * NA_WR * NA_WR // 2, NA_HEADS * nci), name=name + "_rows", exact=True)
    return out.reshape(16, NA_HEADS, nci)[:nri].transpose(1, 0, 2)


BIG = ("ffn1_w_gate", "ffn1_w_up", "ffn1_w_down", "w_in", "w_branch_na", "w_branch_sw", "w_out",
       "ffn2_w_gate", "ffn2_w_up", "ffn2_w_down")
SMALL = ("ffn1_norm", "mix_norm", "b_gate", "na_q_norm", "na_k_norm", "na_rpb", "sw_q_norm", "sw_k_norm", "sw_sink",
         "ffn2_norm")


def _mixer_weights(g):
    w_in_t = g["w_in"].reshape(IN_W, DM)
    return dict(w_in_t=w_in_t, wa=g["w_branch_na"], ws=g["w_branch_sw"], wo=g["w_out"].reshape(DM, DM))


GROUPS = {"ffn1": ("ffn1_w_gate", "ffn1_w_up", "ffn1_w_down"), "mix": ("w_in", "w_branch_na", "w_branch_sw", "w_out"),
          "ffn2": ("ffn2_w_gate", "ffn2_w_up", "ffn2_w_down")}


def layer_fwd(x, p, weights, t5b, target=None):
    row = lambda v: v.reshape(1, -1)
    stacked = lambda g: {n: a.reshape(DFF, DM) for n, a in g.items()}
    g1 = stacked(weights("ffn1", x))
    y1, h1, gg1, uu1 = ffn_fwd(x, row(p["ffn1_norm"]), g1["ffn1_w_gate"], g1["ffn1_w_up"], g1["ffn1_w_down"], name="ffn_fwd")
    w = _mixer_weights(weights("mix", y1))
    hm, z, zg, qa, ka, va, qs, kv = mixer_input_fwd(y1, row(p["mix_norm"]), w["w_in_t"], p["na_q_norm"], p["na_k_norm"],
                                                    p["sw_q_norm"], p["sw_k_norm"], name="mixer_input_fwd")
    bias = p["na_bias"]
    o_na, lse_na = na_fwd(qa, ka, va, bias, name="na_fwd")
    kvp = jnp.pad(kv, ((SW_BLK, SW_BLK), (0, 0)))
    sink = p["sw_sink"]
    o_sw, lse_sw = sw_fwd(qs, kvp, t5b, sink, name="sw_fwd")
    y2, pa, ps, merged = mixer_output_fwd(o_na, o_sw, zg, row(p["b_gate"]), w["wa"], w["ws"], w["wo"], y1,
                                          name="mixer_output_fwd")
    g2 = stacked(weights("ffn2", y2))
    *y3, h2, gg2, uu2 = ffn_fwd(y2, row(p["ffn2_norm"]), g2["ffn2_w_gate"], g2["ffn2_w_up"], g2["ffn2_w_down"], target,
                                name="ffn_fwd")
    y3 = y3[0] if target is None else tuple(y3)
    saved = dict(x=x, y1=y1, h1=h1, gg1=gg1, uu1=uu1, hm=hm, z=z, zg=zg, qa=qa, ka=ka, va=va, qs=qs, kvp=kvp, bias=bias,
                 o_na=o_na, lse_na=lse_na, o_sw=o_sw, lse_sw=lse_sw, pa=pa, ps=ps, merged=merged, y2=y2, h2=h2, gg2=gg2,
                 uu2=uu2, w=w, sink=sink, g1=g1, g2=g2)
    return y3, saved


def layer_bwd(dy3, dy3_bf, sv, p, t5b, emit, dep=None):
    w, g1, g2 = sv["w"], sv["g1"], sv["g2"]
    row = lambda v: v.reshape(1, -1)
    fold = lambda v: v.reshape(-1, HD).sum(axis=0)
    small = {}
    dy2, dy2_bf, small["ffn2_norm"], act, dg, du = ffn_bwd_tokens(
        dy3, sv["y2"], row(p["ffn2_norm"]), sv["gg2"], sv["uu2"], g2["ffn2_w_gate"], g2["ffn2_w_up"], g2["ffn2_w_down"],
        name="ffn_bwd_tokens", dep=dep)
    shards = lambda gs: [g.reshape(NSH, FSH, DM) for g in gs]
    token = emit("ffn2", shards(ffn_bwd_weights(sv["h2"], dy3_bf, act, dg, du, name="ffn_bwd_weights")))
    dpa, dps, dzg, small["b_gate"], do_na, do_sw = mixer_output_bwd(
        dy2, sv["zg"], row(p["b_gate"]), sv["pa"], sv["ps"], w["wa"], w["ws"], w["wo"], name="mixer_output_bwd", dep=token)
    gw_out, gw_na, gw_sw = mixer_output_dw(sv["merged"], dy2_bf, sv["o_na"], dpa, sv["o_sw"], dps, name="mixer_output_dw")
    gw_out = gw_out.reshape(NSH, DM // NSH, DM)
    dqa, dka, dva, dbias = na_bwd(sv["qa"], sv["ka"], sv["va"], sv["o_na"], do_na, sv["lse_na"], sv["bias"], name="na_bwd")
    dqs, dkvp, dt5, dsink = sw_bwd(sv["qs"], sv["kvp"], sv["o_sw"], do_sw, sv["lse_sw"], t5b, sv["sink"], name="sw_bwd")
    dkv = dkvp[SW_BLK:SW_BLK + SEQ]
    dz, dgqa, dgka, dgqs, dgks = qknorm_bwd(sv["z"], dqa, dka, dva, dqs, dkv, p["na_q_norm"], p["na_k_norm"],
                                            p["sw_q_norm"], p["sw_k_norm"], name="qknorm_bwd")
    small["na_q_norm"], small["na_k_norm"], small["sw_q_norm"], small["sw_k_norm"] = fold(dgqa), fold(dgka), fold(dgqs), fold(dgks)
    small["na_rpb"] = rpb_grad(dbias, name="rpb_grad")
    small["sw_sink"] = dsink
    gw_in = mixer_input_dw(dz, dzg, sv["hm"], name="mixer_input_dw").reshape(NSH, IN_W // NSH, DM)
    token = emit("mix", (gw_in, gw_na, gw_sw, gw_out))
    dy1, dy1_bf, small["mix_norm"] = mixer_input_bwd(dz, dzg, w["w_in_t"], sv["y1"], row(p["mix_norm"]), dy2,
                                                     name="mixer_input_bwd", dep=token)
    dx, dx_bf, small["ffn1_norm"], act, dg, du = ffn_bwd_tokens(
        dy1, sv["x"], row(p["ffn1_norm"]), sv["gg1"], sv["uu1"], g1["ffn1_w_gate"], g1["ffn1_w_up"], g1["ffn1_w_down"],
        name="ffn_bwd_tokens")
    emit("ffn1", shards(ffn_bwd_weights(sv["h1"], dy1_bf, act, dg, du, name="ffn_bwd_weights")))
    return dx, dx_bf, small, dt5


ANY = pl.BlockSpec(memory_space=pl.ANY)


def _place():
    x, y, c = lax.axis_index("x"), lax.axis_index("y"), lax.axis_index("c")
    chips = [(1 - x, y), (x, 1 - y), (1 - x, 1 - y)]
    return x, y, c, chips


def _remote(src, dst, send_sem, recv_sem, to):
    return pltpu.make_async_remote_copy(src_ref=src, dst_ref=dst, send_sem=send_sem, recv_sem=recv_sem, device_id=to,
                                        device_id_type=MESH)


HBM = pl.BlockSpec(memory_space=pltpu.HBM)
SEM = pl.BlockSpec(memory_space=pltpu.SEMAPHORE)
ORDERED_EFFECT = pltpu.SideEffectType.DATAFLOW_SIDE_EFFECTING


def _in_hbm(v):
    return pltpu.with_memory_space_constraint(v, pltpu.HBM)


def _row_half(ref_shape_rows, c):
    half = ref_shape_rows // 2
    return pl.ds(c * half, half)


def _ici_gather_copies(w, land, send_sems, recv_sems, whole=False):
    x, y, c, chips = _place()
    me = 2 * x + y
    copies = []
    for a in range(len(w)):
        rows = pl.ds(0, w[a].shape[0]) if whole else _row_half(w[a].shape[0], c)
        for k, chip in enumerate(chips):
            copies.append(_remote(w[a].at[rows], land[a].at[me, rows], send_sems.at[4 * a + k], recv_sems.at[4 * a + k],
                                  (*chip, c)))
        copies.append(_remote(w[a], land[a].at[me], send_sems.at[4 * a + 3], recv_sems.at[4 * a + 3], (x, y, 1 - c)))
    return copies


def _d2d_gather_copies(w, land, send_sems, recv_sems):
    x, y, c, chips = _place()
    copies = []
    for a in range(len(w)):
        rows = _row_half(w[a].shape[0], c)
        for k, (cx, cy) in enumerate(chips):
            blk = land[a].at[2 * cx + cy, rows]
            copies.append(_remote(blk, blk, send_sems.at[3 * a + k], recv_sems.at[3 * a + k], (x, y, 1 - c)))
    return copies


def _d2d_gather_waits(w, land, send_sems, recv_sems):
    x, y, c, chips = _place()
    waits = []
    for a in range(len(w)):
        rows = _row_half(w[a].shape[0], 1 - c)
        for k, (cx, cy) in enumerate(chips):
            blk = land[a].at[2 * cx + cy, rows]
            waits.append(_remote(blk, blk, send_sems.at[3 * a + k], recv_sems.at[3 * a + k], (x, y, 1 - c)))
    return waits


def gather_start(groups, dep=None, whole=(), *, name):
    sizes = [len(g) for g in groups]
    shards = [s for g in groups for s in g]
    n, ng = len(shards), len(groups)
    extra = [] if dep is None else [dep]

    def body(*refs):
        first_out = 2 * n + len(extra)
        w, land, sems = refs[:n], refs[n:2 * n], refs[first_out:first_out + 2 * ng]
        off = 0
        for gi, size in enumerate(sizes):
            for cp in _ici_gather_copies(w[off:off + size], land[off:off + size], sems[2 * gi], sems[2 * gi + 1], gi in whole):
                cp.start()
            off += size

    lands = [lax.empty((NSH,) + s.shape, s.dtype) for s in shards]
    sem_shapes = tuple(pltpu.SemaphoreType.DMA((4 * size,)) for size in sizes for _ in range(2))
    res = pl.pallas_call(
        body, name=name,
        out_shape=sem_shapes + tuple(pltpu.HBM(s.shape, s.dtype) for s in shards) + tuple(pltpu.HBM(l.shape, l.dtype) for l in lands),
        in_specs=[HBM] * (2 * n) + [ANY] * len(extra), out_specs=(SEM,) * (2 * ng) + (HBM,) * (2 * n),
        input_output_aliases={i: 2 * ng + i for i in range(2 * n)},
        compiler_params=pltpu.CompilerParams(has_side_effects=ORDERED_EFFECT))(
            *[_in_hbm(s) for s in shards], *[_in_hbm(l) for l in lands], *extra)
    out, off = [], 0
    for gi, size in enumerate(sizes):
        out.append((res[2 * gi], res[2 * gi + 1], list(res[2 * ng + off:2 * ng + off + size]),
                    list(res[2 * ng + n + off:2 * ng + n + off + size])))
        off += size
    return out


def gather_wait(send_sems, recv_sems, shards, lands, after, whole=False, *, name):
    n = len(shards)

    def body(*refs):
        w, land = refs[:n], refs[n:2 * n]
        send, recv = refs[2 * n:2 * n + 2]
        for cp in _ici_gather_copies(w, land, send, recv, whole):
            cp.wait_send()
            cp.wait_recv()

    res = pl.pallas_call(
        body, name=name,
        out_shape=tuple(pltpu.HBM(s.shape, s.dtype) for s in shards) + tuple(pltpu.HBM(l.shape, l.dtype) for l in lands),
        in_specs=[HBM] * (2 * n) + [SEM, SEM] + [ANY] * len(after), out_specs=(HBM,) * (2 * n),
        input_output_aliases={i: i for i in range(2 * n)},
        compiler_params=pltpu.CompilerParams(has_side_effects=ORDERED_EFFECT))(*shards, *lands, send_sems, recv_sems, *after)
    return list(res[:n]), list(res[n:])


def gather_finish(shards, lands, *, name):
    n = len(shards)

    def body(*refs):
        w, land = refs[:n], refs[n:2 * n]
        send_sems, recv_sems = refs[3 * n:]
        d2d = _d2d_gather_copies(w, land, send_sems, recv_sems)
        for cp in d2d:
            cp.start()
        for cp in _d2d_gather_waits(w, land, send_sems, recv_sems):
            cp.wait_recv()
        for cp in d2d:
            cp.wait_send()

    return list(pl.pallas_call(
        body, name=name, out_shape=tuple(pltpu.HBM(l.shape, l.dtype) for l in lands),
        in_specs=[ANY] * (2 * n), out_specs=tuple([ANY] * n), input_output_aliases={n + i: i for i in range(n)},
        scratch_shapes=[pltpu.SemaphoreType.DMA((3 * n,)), pltpu.SemaphoreType.DMA((3 * n,))])(*shards, *lands))


def _pair_exchange_copies(g, buf, send_sems, recv_sems):
    x, y, c, _ = _place()
    copies = []
    for a in range(len(g)):
        half = g[a].shape[1] // 2
        copies.append(_remote(g[a].at[:, pl.ds((1 - c) * half, half)], buf[a], send_sems.at[a], recv_sems.at[a], (x, y, 1 - c)))
    return copies


def pair_exchange_start(grads, dep=None, *, name):
    n = len(grads)
    extra = [] if dep is None else [dep]

    def body(*refs):
        sems = refs[2 * n + len(extra):]
        for cp in _pair_exchange_copies(refs[:n], refs[n:2 * n], sems[0], sems[1]):
            cp.start()
        refs[-1][...] = jnp.zeros_like(refs[-1])

    lands = [lax.empty((NSH, g.shape[1] // 2, g.shape[2]), g.dtype) for g in grads]
    res = pl.pallas_call(
        body, name=name,
        out_shape=(pltpu.SemaphoreType.DMA((n,)), pltpu.SemaphoreType.DMA((n,)))
        + tuple(pltpu.HBM(g.shape, g.dtype) for g in grads) + tuple(pltpu.HBM(l.shape, l.dtype) for l in lands)
        + (_sds((8, 128), F32),),
        in_specs=[HBM] * (2 * n) + [ANY] * len(extra),
        out_specs=(SEM, SEM) + (HBM,) * (2 * n) + (pl.BlockSpec(memory_space=pltpu.VMEM),),
        input_output_aliases={i: 2 + i for i in range(2 * n)},
        compiler_params=pltpu.CompilerParams(has_side_effects=ORDERED_EFFECT))(
            *[_in_hbm(g) for g in grads], *[_in_hbm(l) for l in lands], *extra)
    return res[0], res[1], list(res[2:2 + n]), list(res[2 + n:2 + 2 * n]), res[-1]


def pair_exchange_wait(send_sems, recv_sems, grads, lands, after, *, name):
    n = len(grads)

    def body(*refs):
        for cp in _pair_exchange_copies(refs[:n], refs[n:2 * n], refs[2 * n], refs[2 * n + 1]):
            cp.wait_send()
            cp.wait_recv()

    res = pl.pallas_call(
        body, name=name,
        out_shape=tuple(pltpu.HBM(g.shape, g.dtype) for g in grads) + tuple(pltpu.HBM(l.shape, l.dtype) for l in lands),
        in_specs=[HBM] * (2 * n) + [SEM, SEM] + [ANY] * len(after), out_specs=(HBM,) * (2 * n),
        input_output_aliases={i: i for i in range(2 * n)},
        compiler_params=pltpu.CompilerParams(has_side_effects=ORDERED_EFFECT))(*grads, *lands, send_sems, recv_sems, *after)
    return list(res[:n]), list(res[n:])


def _chip_exchange_copies(s, buf, send_sems, recv_sems):
    x, y, c, chips = _place()
    return [_remote(s[a].at[2 * cx + cy], buf[a].at[k], send_sems.at[3 * a + k], recv_sems.at[3 * a + k], (cx, cy, c))
            for a in range(len(s)) for k, (cx, cy) in enumerate(chips)]


def exchange_start(sums, grads, *, name):
    n1, n2 = len(sums), len(grads)

    def body(*refs):
        first_out = 2 * (n1 + n2)
        chip = _chip_exchange_copies(refs[:n1], refs[n1:2 * n1], refs[first_out], refs[first_out + 1])
        pair = _pair_exchange_copies(refs[2 * n1:2 * n1 + n2], refs[2 * n1 + n2:first_out], refs[first_out + 2],
                                     refs[first_out + 3])
        for cp in chip + pair:
            cp.start()
        refs[-1][...] = jnp.zeros_like(refs[-1])

    chip_lands = [lax.empty((3,) + s.shape[1:], s.dtype) for s in sums]
    pair_lands = [lax.empty((NSH, g.shape[1] // 2, g.shape[2]), g.dtype) for g in grads]
    arrays = list(sums) + chip_lands + list(grads) + pair_lands
    res = pl.pallas_call(
        body, name=name,
        out_shape=(pltpu.SemaphoreType.DMA((3 * n1,)), pltpu.SemaphoreType.DMA((3 * n1,)), pltpu.SemaphoreType.DMA((n2,)),
                   pltpu.SemaphoreType.DMA((n2,)))
        + tuple(pltpu.HBM(a.shape, a.dtype) for a in arrays) + (_sds((8, 128), F32),),
        in_specs=[HBM] * len(arrays), out_specs=(SEM,) * 4 + (HBM,) * len(arrays) + (pl.BlockSpec(memory_space=pltpu.VMEM),),
        input_output_aliases={i: 4 + i for i in range(len(arrays))},
        compiler_params=pltpu.CompilerParams(has_side_effects=ORDERED_EFFECT))(*[_in_hbm(a) for a in arrays])
    thru = list(res[4:4 + len(arrays)])
    chip = (res[0], res[1], thru[:n1], thru[n1:2 * n1])
    pair = (res[2], res[3], thru[2 * n1:2 * n1 + n2], thru[2 * n1 + n2:])
    return chip, pair, res[-1]


def chip_exchange_start(sums, *, name):
    n = len(sums)

    def body(*refs):
        for cp in _chip_exchange_copies(refs[:n], refs[n:2 * n], refs[2 * n], refs[2 * n + 1]):
            cp.start()
        refs[-1][...] = jnp.zeros_like(refs[-1])

    lands = [lax.empty((3,) + s.shape[1:], s.dtype) for s in sums]
    res = pl.pallas_call(
        body, name=name,
        out_shape=(pltpu.SemaphoreType.DMA((3 * n,)), pltpu.SemaphoreType.DMA((3 * n,)))
        + tuple(pltpu.HBM(s.shape, s.dtype) for s in sums) + tuple(pltpu.HBM(l.shape, l.dtype) for l in lands)
        + (_sds((8, 128), F32),),
        in_specs=[HBM] * (2 * n), out_specs=(SEM, SEM) + (HBM,) * (2 * n) + (pl.BlockSpec(memory_space=pltpu.VMEM),),
        input_output_aliases={i: 2 + i for i in range(2 * n)},
        compiler_params=pltpu.CompilerParams(has_side_effects=ORDERED_EFFECT))(
            *[_in_hbm(s) for s in sums], *[_in_hbm(l) for l in lands])
    return res[0], res[1], list(res[2:2 + n]), list(res[2 + n:2 + 2 * n]), res[-1]


def chip_exchange_wait(send_sems, recv_sems, sums, lands, after, *, name):
    n = len(sums)

    def body(*refs):
        for cp in _chip_exchange_copies(refs[:n], refs[n:2 * n], refs[2 * n], refs[2 * n + 1]):
            cp.wait_send()
            cp.wait_recv()

    res = pl.pallas_call(
        body, name=name,
        out_shape=tuple(pltpu.HBM(s.shape, s.dtype) for s in sums) + tuple(pltpu.HBM(l.shape, l.dtype) for l in lands),
        in_specs=[HBM] * (2 * n) + [SEM, SEM] + [ANY] * len(after), out_specs=(HBM,) * (2 * n),
        input_output_aliases={i: i for i in range(2 * n)},
        compiler_params=pltpu.CompilerParams(has_side_effects=ORDERED_EFFECT))(*sums, *lands, send_sems, recv_sems, *after)
    return list(res[:n]), list(res[n:])


def _pair_send_copies(h, got, send_sems, recv_sems):
    x, y, c, _ = _place()
    return [_remote(h[i], got[i], send_sems.at[i], recv_sems.at[i], (x, y, 1 - c)) for i in range(len(h))]


def pair_send_start(halves, *, name):
    n = len(halves)

    def body(*refs):
        for cp in _pair_send_copies(refs[:n], refs[n:2 * n], refs[2 * n], refs[2 * n + 1]):
            cp.start()
        refs[-1][...] = jnp.zeros_like(refs[-1])

    lands = [lax.empty(h.shape, h.dtype) for h in halves]
    res = pl.pallas_call(
        body, name=name,
        out_shape=(pltpu.SemaphoreType.DMA((n,)), pltpu.SemaphoreType.DMA((n,)))
        + tuple(pltpu.HBM(h.shape, h.dtype) for h in halves) * 2 + (_sds((8, 128), F32),),
        in_specs=[HBM] * (2 * n), out_specs=(SEM, SEM) + (HBM,) * (2 * n) + (pl.BlockSpec(memory_space=pltpu.VMEM),),
        input_output_aliases={i: 2 + i for i in range(2 * n)},
        compiler_params=pltpu.CompilerParams(has_side_effects=ORDERED_EFFECT))(
            *[_in_hbm(h) for h in halves], *[_in_hbm(l) for l in lands])
    return res[0], res[1], list(res[2:2 + n]), list(res[2 + n:2 + 2 * n]), res[-1]


def pair_send_wait(send_sems, recv_sems, halves, lands, after, *, name):
    n = len(halves)

    def body(*refs):
        for cp in _pair_send_copies(refs[:n], refs[n:2 * n], refs[2 * n], refs[2 * n + 1]):
            cp.wait_send()
            cp.wait_recv()

    res = pl.pallas_call(
        body, name=name, out_shape=tuple(pltpu.HBM(h.shape, h.dtype) for h in halves) * 2,
        in_specs=[HBM] * (2 * n) + [SEM, SEM] + [ANY] * len(after), out_specs=(HBM,) * (2 * n),
        input_output_aliases={i: i for i in range(2 * n)},
        compiler_params=pltpu.CompilerParams(has_side_effects=ORDERED_EFFECT))(*halves, *lands, send_sems, recv_sems, *after)
    return list(res[:n]), list(res[n:])


def allreduce_small(v, *, name):
    rows = v.shape[0]

    def body(v_ref, o_ref, gath, send_sems, recv_sems):
        x, y, c, _ = _place()
        me = 4 * x + 2 * y + c
        gath[me] = v_ref[...]
        copies = []
        for k in range(1, 8):
            fx, fy, fc = (k >> 2) & 1, (k >> 1) & 1, k & 1
            peer = (jnp.where(fx, 1 - x, x), jnp.where(fy, 1 - y, y), jnp.where(fc, 1 - c, c))
            cp = _remote(v_ref, gath.at[me], send_sems.at[k - 1], recv_sems.at[k - 1], peer)
            cp.start()
            copies.append(cp)
        for cp in copies:
            cp.wait()
        acc = gath[0]
        for d in range(1, 8):
            acc = acc + gath[d]
        o_ref[...] = acc

    return pl.pallas_call(
        body, name=name, out_shape=_sds(v.shape, F32),
        in_specs=[pl.BlockSpec(memory_space=pltpu.VMEM)], out_specs=pl.BlockSpec(memory_space=pltpu.VMEM),
        scratch_shapes=[pltpu.VMEM((8, rows, 128), F32), pltpu.SemaphoreType.DMA((7,)), pltpu.SemaphoreType.DMA((7,))])(v)


def _same_shape_runs(arrays):
    runs = {}
    for i, a in enumerate(arrays):
        runs.setdefault(a.shape, []).append(i)
    return list(runs.values())


def _per_shape(fn, *lists):
    out = [None] * len(lists[0])
    for idx in _same_shape_runs(lists[0]):
        for i, r in zip(idx, fn(*[[l[i] for i in idx] for l in lists])):
            out[i] = r
    return out


def add_halves(gs, bufs, cidx, *, name):
    cnt = len(gs)
    _, k, n = gs[0].shape

    def body(c_ref, *refs):
        g, b, o = refs[:cnt], refs[cnt:2 * cnt], refs[2 * cnt:]
        for i in range(cnt):
            o[i][...] = (g[i][...].astype(F32) + b[i][...].astype(F32)).astype(BF16)

    blk = pl.BlockSpec((None, k // 2, n), lambda s, c: (s, 0, 0))
    mine = pl.BlockSpec((None, k // 2, n), lambda s, c: (s, c[0], 0))
    return list(pl.pallas_call(
        body, name=name, out_shape=tuple(_sds(b.shape, BF16) for b in bufs),
        grid_spec=pltpu.PrefetchScalarGridSpec(
            num_scalar_prefetch=1, grid=(NSH,), in_specs=[mine] * cnt + [blk] * cnt, out_specs=tuple([blk] * cnt)),
        compiler_params=_params(("parallel",)))(cidx, *gs, *bufs))


def add_chips(sums, bufs, sidx, *, name):
    cnt = len(sums)
    _, kh, n = sums[0].shape

    def body(s_ref, *refs):
        mine, b, o = refs[:cnt], refs[cnt:2 * cnt], refs[2 * cnt:]
        for i in range(cnt):
            o[i][...] = ((mine[i][...].astype(F32) + b[i][0].astype(F32)) + (b[i][1].astype(F32) + b[i][2].astype(F32)))

    own = pl.BlockSpec((None, kh, n), lambda i, s: (s[0], 0, 0))
    got = pl.BlockSpec((3, kh, n), lambda i, s: (0, 0, 0))
    out = pl.BlockSpec((kh, n), lambda i, s: (0, 0))
    return list(pl.pallas_call(
        body, name=name, out_shape=tuple(_sds((kh, n), F32) for _ in sums),
        grid_spec=pltpu.PrefetchScalarGridSpec(
            num_scalar_prefetch=1, grid=(1,), in_specs=[own] * cnt + [got] * cnt, out_specs=tuple([out] * cnt)),
        compiler_params=_params(("arbitrary",)))(sidx, *sums, *bufs))


PARAMS = ("ffn1_norm", "ffn1_w_gate", "ffn1_w_up", "ffn1_w_down", "mix_norm", "w_in", "b_gate", "na_q_norm", "na_k_norm",
          "na_rpb", "sw_q_norm", "sw_k_norm", "sw_sink", "t5_rel_table", "w_branch_na", "w_branch_sw", "w_out", "ffn2_norm",
          "ffn2_w_gate", "ffn2_w_up", "ffn2_w_down")
SMALL_ALL = tuple(n for n in PARAMS if n not in BIG)
TRANSPOSED = ("ffn1_w_gate", "ffn1_w_up", "w_in", "ffn2_w_gate", "ffn2_w_up")
SMALL_ROWS = 152


def _pack_small(vals):
    flat = jnp.concatenate([vals[n].reshape(-1).astype(F32) for n in SMALL_ALL] + [vals["loss"].reshape(-1)])
    return jnp.pad(flat, (0, SMALL_ROWS * 128 - flat.shape[0])).reshape(SMALL_ROWS, 128)


def _unpack_small(packed, like):
    flat, out, off = packed.reshape(-1), {}, 0
    for n in SMALL_ALL:
        size = math.prod(like[n].shape)
        out[n] = flat[off:off + size].reshape(like[n].shape)
        off += size
    out["loss"] = flat[off]
    return out


def kernel(x, ffn1_norm, ffn1_w_gate, ffn1_w_up, ffn1_w_down, mix_norm, w_in, b_gate, na_q_norm, na_k_norm, na_rpb, sw_q_norm, sw_k_norm, sw_sink, t5_rel_table, w_branch_na, w_branch_sw, w_out, ffn2_norm, ffn2_w_gate, ffn2_w_up, ffn2_w_down, loss_target, m_ffn1_norm, m_ffn1_w_gate, m_ffn1_w_up, m_ffn1_w_down, m_mix_norm, m_w_in, m_b_gate, m_na_q_norm, m_na_k_norm, m_na_rpb, m_sw_q_norm, m_sw_k_norm, m_sw_sink, m_t5_rel_table, m_w_branch_na, m_w_branch_sw, m_w_out, m_ffn2_norm, m_ffn2_w_gate, m_ffn2_w_up, m_ffn2_w_down, v_ffn1_norm, v_ffn1_w_gate, v_ffn1_w_up, v_ffn1_w_down, v_mix_norm, v_w_in, v_b_gate, v_na_q_norm, v_na_k_norm, v_na_rpb, v_sw_q_norm, v_sw_k_norm, v_sw_sink, v_t5_rel_table, v_w_branch_na, v_w_branch_sw, v_w_out, v_ffn2_norm, v_ffn2_w_gate, v_ffn2_w_up, v_ffn2_w_down):
    args = locals()
    tr = lambda n, a: jnp.transpose(a, (0, 2, 1)) if n in TRANSPOSED else a
    w = {n: tr(n, args[n]) for n in PARAMS}
    m = {n: tr(n, args["m_" + n]) for n in PARAMS}
    v = {n: tr(n, args["v_" + n]) for n in PARAMS}
    cidx = lax.axis_index("c").astype(jnp.int32).reshape(1)
    sidx = (2 * lax.axis_index("x") + lax.axis_index("y")).astype(jnp.int32).reshape(1)

    small = [{n: w[n][l] for n in SMALL} for l in range(DEPTH)]
    order = ("ffn1", "mix", "ffn2")

    keys = [(l, g) for l in range(DEPTH) for g in order]
    local = lambda l, g: [w[n][l].astype(BF16) for n in GROUPS[g]]
    first = gather_start([local(*keys[0])], name="gather_start")
    direct = keys[-1]
    rest = gather_start([local(*key) for key in keys[1:]], first[0][2][0], whole=(len(keys) - 2,), name="gather_start")
    in_flight = dict(zip(keys, first + rest))
    t5b = t5_bias(w["t5_rel_table"], name="t5_bias")
    for l in range(DEPTH):
        small[l]["na_bias"] = na_bias_table(small[l]["na_rpb"], name="na_bias_table")
    early = [t5b] + [small[l]["na_bias"] for l in range(DEPTH)] + [rest[0][2][0]]

    def weights_of(l):
        def get(group, after):
            send_sems, recv_sems, thru, lands = in_flight[(l, group)]
            after = [after] + (early if (l, group) == keys[0] else [])
            thru, lands = gather_wait(send_sems, recv_sems, thru, lands, after, (l, group) == direct, name="gather_wait")
            if (l, group) == direct:
                return dict(zip(GROUPS[group], lands))
            return dict(zip(GROUPS[group], gather_finish(thru, lands, name="gather_finish")))
        return get

    h0, saved0 = layer_fwd(x[0], small[0], weights_of(0), t5b)
    (dy, dy_bf, loss_row), saved1 = layer_fwd(h0, small[1], weights_of(1), t5b, target=loss_target[0])

    crossing, tokens, pending = {}, [], []

    def ship(after, then=None):
        key, send_sems, recv_sems, grads, lands = pending.pop()
        grads, from_sibling = pair_exchange_wait(send_sems, recv_sems, grads, lands, after, name="pair_exchange_wait")
        sums = _per_shape(lambda gs, bs: add_halves(gs, bs, cidx, name="add_halves"), grads, from_sibling)
        if then is None:
            send_sems, recv_sems, sums, lands, token = chip_exchange_start(sums, name="chip_exchange_start")
            crossing[key] = (send_sems, recv_sems, sums, lands)
            return token
        crossing[key], pair, token = exchange_start(sums, then[1], name="exchange_start")
        pending.append((then[0],) + pair)
        return token

    def reduce_of(l):
        def emit(group, grads):
            grads = list(grads)
            if pending:
                token = ship([grads[0]], then=((l, group), grads))
            else:
                send_sems, recv_sems, grads, lands, token = pair_exchange_start(grads, name="pair_exchange_start")
                pending.append(((l, group), send_sems, recv_sems, grads, lands))
            tokens.append(token)
            return token
        return emit

    def finish(layer, after, filled=None):
        sent = {}
        for group in order:
            send_sems, recv_sems, sums, lands = crossing[(layer, group)]
            sums, got = chip_exchange_wait(send_sems, recv_sems, sums, lands, after, name="chip_exchange_wait")
            halves = _per_shape(lambda ss, bs: add_chips(ss, bs, sidx, name="add_chips"), sums, got)
            sent[group] = pair_send_start(halves, name="pair_send_start")
            after = [sent[group][4]]
        out = {}
        for group in order:
            send_sems, recv_sems, halves, lands, _ = sent[group]
            halves, theirs = pair_send_wait(send_sems, recv_sems, halves, lands, after, name="pair_send_wait")
            names = GROUPS[group]
            res = _per_shape(
                lambda ws, ms, vs, a, b, *f: adamw_layer(ws, ms, vs, a, b, cidx, layer, list(f[0]) if f else None, name="adamw_layer"),
                *([[w[n] for n in names], [m[n] for n in names], [v[n] for n in names], halves, theirs]
                  + ([[filled[n] for n in names]] if filled is not None else [])))
            out.update(zip(names, res))
            after = [res[-1][0]]
        return out

    dy, dy_bf, small1, dt5_1 = layer_bwd(dy, dy_bf, saved1, small[1], t5b, reduce_of(1))
    grad_x, _, small0, dt5_0 = layer_bwd(dy, dy_bf, saved0, small[0], t5b, reduce_of(0), dep=tokens[-1])
    done1 = finish(1, [ship([grad_x])])

    smalls = [small0, small1]
    dt5 = t5_table_grad(dt5_0, dt5_1, name="t5_table_grad").reshape(32, 8)
    local_small = {n: jnp.stack([smalls[l][n].reshape(w[n].shape[1:]) for l in range(DEPTH)]) for n in SMALL}
    local_small["t5_rel_table"] = dt5
    local_small["loss"] = loss_row[0, 0:1]
    total = allreduce_small(_pack_small(local_small), name="allreduce_small")
    small_grads = _unpack_small(total, w)
    small_done = adamw_small([w[n] for n in SMALL_ALL], [small_grads[n] for n in SMALL_ALL], [m[n] for n in SMALL_ALL],
                             [v[n] for n in SMALL_ALL], name="adamw_small")

    grad, delta, new_m, new_v = {}, {}, {}, {}
    for n, done in finish(0, [small_done[0][0], done1[BIG[-1]][0]], filled=done1).items():
        grad[n], delta[n], new_m[n], new_v[n] = done
    for n, done in zip(SMALL_ALL, small_done):
        grad[n] = small_grads[n]
        delta[n], new_m[n], new_v[n] = done

    return (small_grads["loss"], grad_x[None], *[tr(n, grad[n]) for n in PARAMS], *[tr(n, delta[n]) for n in PARAMS],
            *[tr(n, new_m[n]) for n in PARAMS], *[tr(n, new_v[n]) for n in PARAMS])
```

```python
import math

import jax
import jax.numpy as jnp
import numpy as np
from jax import lax
from jax.experimental import pallas as pl
from jax.experimental.pallas import tpu as pltpu

F32 = jnp.float32
BF16 = jnp.bfloat16

SEQ = 2048
DM = 1024
DFF = 2816
DEPTH = 2
NSH = 4
FSH = DFF // NSH
GRID_W = 64
ROWS = SEQ // GRID_W
NA_HEADS = 8
HD = 64
NA_WR = 8
NA_WC = 16
NA_KEYS = NA_WR * GRID_W
SW_BLK = 128
SW_NB = SEQ // SW_BLK
SW_KEYS = 3 * SW_BLK
ATT_W = 2304
GATE_W = 2048
IN_W = ATT_W + GATE_W
EPS = 1e-6
NEG = -1e30
QK_SCALE = 1.0 / math.sqrt(HD)

ADAM_LR = 0.001
ADAM_B1 = 0.9
ADAM_B2 = 0.999
ADAM_EPS = 1e-08
ADAM_WD = 0.01
ADAM_STEP = 10

VMEM_LIMIT = 56 << 20
MESH = pl.DeviceIdType.MESH

NT = (((1,), (1,)), ((), ()))
TN = (((0,), (0,)), ((), ()))
NN = (((1,), (0,)), ((), ()))


def _dot(a, b, dims=NN):
    return lax.dot_general(a, b, dims, preferred_element_type=F32)


def _params(sem=None):
    return pltpu.CompilerParams(dimension_semantics=sem, vmem_limit_bytes=VMEM_LIMIT)


def _sds(shape, dtype):
    return jax.ShapeDtypeStruct(shape, dtype)


def mm(a, b, *, name, ta=False, tb=False, out_dtype=F32, add=None, scale=None, tm=512, tn=None, tk=None, exact=False,
       dep=None, b_rows=None):
    m, kd = (a.shape[1], a.shape[0]) if ta else a.shape
    if b_rows is None:
        n = b.shape[0] if tb else b.shape[1]
    else:
        n = b_rows[1] if tb else b.shape[1]
        assert tb or (b_rows[1] == kd and (tk or kd) == kd)
    tm, tn, tk = min(tm, m), min(tn or n, n), min(tk or kd, kd)
    nk = kd // tk
    dims = (((0 if ta else 1,), (1 if tb else 0,)), ((), ()))

    def body(*refs):
        a_ref, b_ref = refs[:2]
        add_ref = refs[2] if add is not None else None
        o_ref = refs[-1] if nk == 1 else refs[-2]
        if b_rows is None:
            bv = b_ref[...]
        elif tb:
            bv = b_ref[pl.ds(pl.multiple_of(b_rows[0] + pl.program_id(1) * tn, 16), tn), :]
        else:
            bv = b_ref[b_rows[0]:b_rows[0] + b_rows[1], :]
        if exact:
            part = lax.dot_general(a_ref[...], bv, dims, precision=lax.Precision.HIGHEST, preferred_element_type=F32)
        else:
            part = lax.dot_general(a_ref[...].astype(BF16), bv.astype(BF16), dims, preferred_element_type=F32)

        def finish(r):
            if scale is not None:
                r = r * scale
            if add is not None:
                r = r + add_ref[...]
            o_ref[...] = r.astype(out_dtype)

        if nk == 1:
            finish(part)
        else:
            acc, k = refs[-1], pl.program_id(2)

            @pl.when(k == 0)
            def _():
                acc[...] = part

            @pl.when(k != 0)
            def _():
                acc[...] += part

            pl.when(k == nk - 1)(lambda: finish(acc[...]))

    a_spec = pl.BlockSpec((tk, tm), lambda i, j, k: (k, i)) if ta else pl.BlockSpec((tm, tk), lambda i, j, k: (i, k))
    if b_rows is not None:
        b_spec = pl.BlockSpec(b.shape, lambda i, j, k: (0, 0), pipeline_mode=pl.Buffered(1))
    else:
        b_spec = pl.BlockSpec((tn, tk), lambda i, j, k: (j, k)) if tb else pl.BlockSpec((tk, tn), lambda i, j, k: (k, j))
    o_spec = pl.BlockSpec((tm, tn), lambda i, j, k: (i, j))
    ins, specs = [a, b], [a_spec, b_spec]
    if add is not None:
        ins.append(add)
        specs.append(o_spec)
    if dep is not None:
        ins.append(dep)
        specs.append(pl.BlockSpec(memory_space=pl.ANY))
    return pl.pallas_call(
        body, name=name, out_shape=_sds((m, n), out_dtype), grid=(m // tm, n // tn, nk), in_specs=specs,
        out_specs=o_spec, scratch_shapes=[] if nk == 1 else [pltpu.VMEM((tm, tn), F32)],
        compiler_params=_params(("parallel", "parallel", "arbitrary")))(*ins)


def _rms(x):
    return lax.rsqrt(jnp.mean(x * x, axis=-1, keepdims=True) + EPS)


def mixer_input_fwd(x, gain, w_in_t, gq_na, gk_na, gq_sw, gk_sw, *, name, tm=512):
    def body(x_ref, g_ref, w_ref, gqa_ref, gka_ref, gqs_ref, gks_ref, bd_ref, bd2_ref,
             h_ref, z_ref, zg_ref, qa_ref, ka_ref, va_ref, qs_ref, kv_ref):
        x = x_ref[...]
        h = (x * _rms(x) * g_ref[...]).astype(BF16)
        h_ref[...] = h
        z = _dot(h, w_ref[0:ATT_W, :], NT).astype(BF16)
        z_ref[...] = z
        zg_ref[...] = _dot(h, w_ref[ATT_W:IN_W, :], NT).astype(BF16)
        bd = bd_ref[...]

        def norm(v, g, bdm):
            v = v.astype(F32)
            return v * lax.rsqrt(_group_mean(v * v, bdm) + EPS) * g

        qa_ref[...] = (norm(z[:, 0:512], gqa_ref[...], bd) * QK_SCALE).astype(BF16)
        ka_ref[...] = norm(z[:, 512:1024], gka_ref[...], bd).astype(BF16)
        va_ref[...] = z[:, 1024:1536]
        qs_ref[...] = (norm(z[:, 1536:2048], gqs_ref[...], bd) * QK_SCALE).astype(BF16)
        kv_ref[:, 0:128] = norm(z[:, 2048:2176], gks_ref[...], bd2_ref[...]).astype(BF16)
        kv_ref[:, 128:256] = z[:, 2176:2304]

    s = x.shape[0]
    tile = pl.BlockSpec((tm, DM), lambda i: (i, 0))
    vec = lambda w: pl.BlockSpec((1, w), lambda i: (0, 0))
    att = pl.BlockSpec((tm, 512), lambda i: (i, 0))
    g512 = lambda g: jnp.tile(g.reshape(1, HD), (1, 8))
    q = _sds((s, 512), BF16)
    return pl.pallas_call(
        body, name=name,
        out_shape=(_sds((s, DM), BF16), _sds((s, ATT_W), BF16), _sds((s, GATE_W), BF16), q, q, q, q, _sds((s, 256), BF16)),
        grid=(s // tm,),
        in_specs=[tile, vec(DM), pl.BlockSpec((IN_W, DM), lambda i: (0, 0), pipeline_mode=pl.Buffered(1)),
                  vec(512), vec(512), vec(512), vec(128), pl.BlockSpec((512, 512), lambda i: (0, 0)),
                  pl.BlockSpec((128, 128), lambda i: (0, 0))],
        out_specs=(tile, pl.BlockSpec((tm, ATT_W), lambda i: (i, 0)), pl.BlockSpec((tm, GATE_W), lambda i: (i, 0)),
                   att, att, att, att, pl.BlockSpec((tm, 256), lambda i: (i, 0))),
        compiler_params=_params(("parallel",)))(
            x, gain, w_in_t, g512(gq_na), g512(gk_na), g512(gq_sw), jnp.tile(gk_sw.reshape(1, HD), (1, 2)),
            _block_diag(512), _block_diag(128))


def _rms_bwd_math(dh, x, gain):
    r = _rms(x)
    xh = x * r
    dgain = jnp.sum(dh * xh, axis=0, keepdims=True)
    dxn = dh * gain
    dx = r * (dxn - xh * jnp.mean(dxn * xh, axis=-1, keepdims=True))
    return dx, dgain


def mixer_input_bwd(dz, dzg, w_in_t, x, gain, dres, *, name, tm=512, dep=None):
    def body(dz_ref, dzg_ref, w_ref, x_ref, g_ref, dres_ref, *rest):
        dx_ref, dxb_ref, dg_ref = rest[-3:]

        @pl.when(pl.program_id(0) == 0)
        def _():
            dg_ref[...] = jnp.zeros_like(dg_ref)

        dh = _dot(dz_ref[...], w_ref[0:ATT_W, :]) + _dot(dzg_ref[...], w_ref[ATT_W:IN_W, :])
        dx, dg = _rms_bwd_math(dh, x_ref[...], g_ref[...])
        dx = dres_ref[...] + dx
        dx_ref[...] = dx
        dxb_ref[...] = dx.astype(BF16)
        dg_ref[...] += dg

    s = x.shape[0]
    tile = pl.BlockSpec((tm, DM), lambda i: (i, 0))
    vec = pl.BlockSpec((1, DM), lambda i: (0, 0))
    ins, specs = _with_dep(
        [dz, dzg, w_in_t, x, gain, dres],
        [pl.BlockSpec((tm, ATT_W), lambda i: (i, 0)), pl.BlockSpec((tm, GATE_W), lambda i: (i, 0)),
         pl.BlockSpec((IN_W, DM), lambda i: (0, 0), pipeline_mode=pl.Buffered(1)), tile, vec, tile], dep)
    return pl.pallas_call(
        body, name=name, out_shape=(_sds((s, DM), F32), _sds((s, DM), BF16), _sds((1, DM), F32)), grid=(s // tm,),
        in_specs=specs, out_specs=(tile, tile, vec), compiler_params=_params(("arbitrary",)))(*ins)


def mixer_input_dw(dz, dzg, h, *, name, tr=256):
    att_tiles = ATT_W // tr

    def body(dz_ref, dzg_ref, h_ref, o_ref):
        t = pl.program_id(0)
        cols = jnp.where(t < att_tiles, dz_ref[...], dzg_ref[...])
        o_ref[...] = _dot(cols, h_ref[...], TN).astype(BF16)

    s = h.shape[0]
    return pl.pallas_call(
        body, name=name, out_shape=_sds((IN_W, DM), BF16), grid=(IN_W // tr,),
        in_specs=[pl.BlockSpec((s, tr), lambda t: (0, jnp.minimum(t, att_tiles - 1))),
                  pl.BlockSpec((s, tr), lambda t: (0, jnp.maximum(t - att_tiles, 0))),
                  pl.BlockSpec((s, DM), lambda t: (0, 0))],
        out_specs=pl.BlockSpec((tr, DM), lambda t: (t, 0)), compiler_params=_params(("parallel",)))(dz, dzg, h)


def mixer_output_dw(merged, dy, o_na, dpa, o_sw, dps, *, name):
    def body(m_ref, dy_ref, ona_ref, dpa_ref, osw_ref, dps_ref, gwo_ref, gna_ref, gsw_ref):
        gwo_ref[...] = _dot(m_ref[...], dy_ref[...], TN).astype(BF16)
        width = DM // NSH
        for o_ref, dp_ref, out_ref in ((ona_ref, dpa_ref, gna_ref), (osw_ref, dps_ref, gsw_ref)):
            full = _dot(o_ref[...], dp_ref[...], TN).astype(BF16)
            for j in range(NSH):
                out_ref[j] = full[:, j * width:(j + 1) * width]

    branch = _sds((NSH, 512, DM // NSH), BF16)
    return pl.pallas_call(
        body, name=name, out_shape=(_sds((DM, DM), BF16), branch, branch), compiler_params=_params())(
            merged, dy, o_na, dpa, o_sw, dps)


def _with_dep(ins, specs, dep):
    if dep is None:
        return ins, specs
    return ins + [dep], specs + [pl.BlockSpec(memory_space=pl.ANY)]


def _resident_weight():
    return pl.BlockSpec((DFF, DM), lambda i: (0, 0), pipeline_mode=pl.Buffered(1))


def ffn_fwd(x, gain, wg, wu, wd, target=None, *, name, tm=512):
    def body(x_ref, g_ref, wg_ref, wu_ref, wd_ref, *rest):
        h_ref, gg_ref, uu_ref = rest[-3:]
        x = x_ref[...]
        h = (x * _rms(x) * g_ref[...]).astype(BF16)
        h_ref[...] = h
        gg = _dot(h, wg_ref[...], NT)
        uu = _dot(h, wu_ref[...], NT)
        gg_ref[...] = gg.astype(BF16)
        uu_ref[...] = uu.astype(BF16)
        act = (gg * jax.nn.sigmoid(gg) * uu).astype(BF16)
        y = x + 0.5 * _dot(act, wd_ref[...])
        if target is None:
            rest[0][...] = y
            return
        t_ref, dy_ref, dyb_ref, l_ref = rest[:4]

        @pl.when(pl.program_id(0) == 0)
        def _():
            l_ref[...] = jnp.zeros_like(l_ref)

        err = y - t_ref[...]
        dy = err * (1.0 / DM)
        dy_ref[...] = dy
        dyb_ref[...] = dy.astype(BF16)
        l_ref[...] += 0.5 * jnp.sum(jnp.mean(err * err, axis=-1, keepdims=True), axis=0, keepdims=True)

    s = x.shape[0]
    tile = pl.BlockSpec((tm, DM), lambda i: (i, 0))
    hid = pl.BlockSpec((tm, DFF), lambda i: (i, 0))
    w = _resident_weight()
    saved_shapes = (_sds((s, DM), BF16), _sds((s, DFF), BF16), _sds((s, DFF), BF16))
    ins, specs = [x, gain, wg, wu, wd], [tile, pl.BlockSpec((1, DM), lambda i: (0, 0)), w, w, w]
    if target is None:
        head_shapes, head_specs = (_sds((s, DM), F32),), (tile,)
    else:
        ins, specs = ins + [target], specs + [tile]
        head_shapes = (_sds((s, DM), F32), _sds((s, DM), BF16), _sds((1, 128), F32))
        head_specs = (tile, tile, pl.BlockSpec((1, 128), lambda i: (0, 0)))
    return pl.pallas_call(
        body, name=name, out_shape=head_shapes + saved_shapes, grid=(s // tm,), in_specs=specs,
        out_specs=head_specs + (tile, hid, hid),
        compiler_params=_params(("parallel",) if target is None else ("arbitrary",)))(*ins)


def ffn_bwd_tokens(dy, x, gain, gg, uu, wg, wu, wd, *, name, tm=256, dep=None):
    def body(dy_ref, x_ref, g_ref, gg_ref, uu_ref, wg_ref, wu_ref, wd_ref, *rest):
        dx_ref, dxb_ref, dgain_ref, act_ref, dg_ref, du_ref = rest[-6:]

        @pl.when(pl.program_id(0) == 0)
        def _():
            dgain_ref[...] = jnp.zeros_like(dgain_ref)

        dy = dy_ref[...]
        dact = _dot((0.5 * dy).astype(BF16), wd_ref[...], NT)
        g = gg_ref[...].astype(F32)
        u = uu_ref[...].astype(F32)
        sg = jax.nn.sigmoid(g)
        silu = g * sg
        act_ref[...] = (silu * u).astype(BF16)
        dg = (dact * u * (sg * (1.0 + g * (1.0 - sg)))).astype(BF16)
        du = (dact * silu).astype(BF16)
        dg_ref[...] = dg
        du_ref[...] = du
        dx, dgain = _rms_bwd_math(_dot(dg, wg_ref[...]) + _dot(du, wu_ref[...]), x_ref[...], g_ref[...])
        dx = dy + dx
        dx_ref[...] = dx
        dxb_ref[...] = dx.astype(BF16)
        dgain_ref[...] += dgain

    s = x.shape[0]
    tile = pl.BlockSpec((tm, DM), lambda i: (i, 0))
    vec = pl.BlockSpec((1, DM), lambda i: (0, 0))
    hid = pl.BlockSpec((tm, DFF), lambda i: (i, 0))
    hshape = _sds((s, DFF), BF16)
    w = _resident_weight()
    ins, specs = _with_dep([dy, x, gain, gg, uu, wg, wu, wd], [tile, tile, vec, hid, hid, w, w, w], dep)
    return pl.pallas_call(
        body, name=name, out_shape=(_sds((s, DM), F32), _sds((s, DM), BF16), _sds((1, DM), F32), hshape, hshape, hshape),
        grid=(s // tm,), in_specs=specs, out_specs=(tile, tile, vec, hid, hid, hid),
        compiler_params=_params(("arbitrary",)))(*ins)


def ffn_bwd_weights(h, dy, act, dg, du, *, name, tf=256):
    def body(h_ref, dy_ref, act_ref, dg_ref, du_ref, gwg_ref, gwu_ref, gwd_ref):
        h = h_ref[...]
        gwg_ref[...] = _dot(dg_ref[...], h, TN).astype(BF16)
        gwu_ref[...] = _dot(du_ref[...], h, TN).astype(BF16)
        gwd_ref[...] = (0.5 * _dot(act_ref[...], dy_ref[...], TN)).astype(BF16)

    s = h.shape[0]
    full = pl.BlockSpec((s, DM), lambda f: (0, 0))
    hid = pl.BlockSpec((s, tf), lambda f: (0, f))
    wt = pl.BlockSpec((tf, DM), lambda f: (f, 0))
    wshape = _sds((DFF, DM), BF16)
    return pl.pallas_call(
        body, name=name, out_shape=(wshape, wshape, wshape), grid=(DFF // tf,), in_specs=[full, full, hid, hid, hid],
        out_specs=(wt, wt, wt), compiler_params=_params(("parallel",)))(h, dy, act, dg, du)


def _group_mean(v, bd):
    hi = v.astype(BF16)
    lo = (v - hi.astype(F32)).astype(BF16)
    return _dot(hi, bd) + _dot(lo, bd)


def _block_diag(width):
    idx = np.arange(width) // HD
    return jnp.asarray((idx[:, None] == idx[None, :]).astype(np.float32) / HD, dtype=BF16)


def qknorm_bwd(z, dqa, dka, dva, dqs, dkv, gq_na, gk_na, gq_sw, gk_sw, *, name, tm=256):
    def body(zq_ref, zk_ref, zs_ref, zkv_ref, dqa_ref, dka_ref, dva_ref, dqs_ref, dkv_ref, gqa_ref, gka_ref, gqs_ref,
             gks_ref, bd_ref, bd2_ref, dz_ref, dgqa_ref, dgka_ref, dgqs_ref, dgks_ref):
        @pl.when(pl.program_id(0) == 0)
        def _():
            dgqa_ref[...] = jnp.zeros_like(dgqa_ref)
            dgka_ref[...] = jnp.zeros_like(dgka_ref)
            dgqs_ref[...] = jnp.zeros_like(dgqs_ref)
            dgks_ref[...] = jnp.zeros_like(dgks_ref)

        bd = bd_ref[...]

        def bwd(x, dy, g, bdm, dg_ref):
            x = x.astype(F32)
            r = lax.rsqrt(_group_mean(x * x, bdm) + EPS)
            xh = x * r
            dg_ref[...] += jnp.sum(dy * xh, axis=0, keepdims=True)
            dxn = dy * g
            return r * (dxn - xh * _group_mean(dxn * xh, bdm))

        dz_ref[:, 0:512] = bwd(zq_ref[...], dqa_ref[...] * QK_SCALE, gqa_ref[...], bd, dgqa_ref).astype(BF16)
        dz_ref[:, 512:1024] = bwd(zk_ref[...], dka_ref[...], gka_ref[...], bd, dgka_ref).astype(BF16)
        dz_ref[:, 1024:1536] = dva_ref[...].astype(BF16)
        dz_ref[:, 1536:2048] = bwd(zs_ref[...], dqs_ref[...] * QK_SCALE, gqs_ref[...], bd, dgqs_ref).astype(BF16)
        dkv = dkv_ref[...]
        dz_ref[:, 2048:2176] = bwd(zkv_ref[:, 0:128], dkv[:, 0:128], gks_ref[...], bd2_ref[...], dgks_ref).astype(BF16)
        dz_ref[:, 2176:2304] = dkv[:, 128:256].astype(BF16)

    s = z.shape[0]
    col = lambda j: pl.BlockSpec((tm, 512), lambda i, j=j: (i, j))
    t512 = pl.BlockSpec((tm, 512), lambda i: (i, 0))
    t256 = pl.BlockSpec((tm, 256), lambda i: (i, 0))
    vec = lambda w: pl.BlockSpec((1, w), lambda i: (0, 0))
    g512 = lambda g: jnp.tile(g.reshape(1, HD), (1, 8))
    return pl.pallas_call(
        body, name=name,
        out_shape=(_sds((s, ATT_W), BF16), _sds((1, 512), F32), _sds((1, 512), F32), _sds((1, 512), F32), _sds((1, 128), F32)),
        grid=(s // tm,),
        in_specs=[col(0), col(1), col(3), pl.BlockSpec((tm, 256), lambda i: (i, 8)), t512, t512, t512, t512, t256,
                  vec(512), vec(512), vec(512), vec(128), pl.BlockSpec((512, 512), lambda i: (0, 0)),
                  pl.BlockSpec((128, 128), lambda i: (0, 0))],
        out_specs=(pl.BlockSpec((tm, ATT_W), lambda i: (i, 0)), vec(512), vec(512), vec(512), vec(128)),
        compiler_params=_params(("arbitrary",)))(
            z, z, z, z, dqa, dka, dva, dqs, dkv, g512(gq_na), g512(gk_na), g512(gq_sw),
            jnp.tile(gk_sw.reshape(1, HD), (1, 2)), _block_diag(512), _block_diag(128))


def _na_row_start(r):
    return jnp.clip(r - NA_WR // 2, 0, ROWS - NA_WR)


def na_bias_table(rpb, *, name):
    t = jnp.pad(rpb, ((0, 0), (0, 2), (0, HD - (2 * NA_WC - 1))))
    pairs = jnp.concatenate([t[:, :16], t[:, 1:17]], axis=-1).reshape(NA_HEADS, 16, 1, 128)

    def body(t_ref, o_ref):
        p = pl.program_id(0)
        q = lax.broadcasted_iota(jnp.int32, (GRID_W, 128), 0)
        kc = lax.broadcasted_iota(jnp.int32, (GRID_W, 128), 1) & (GRID_W - 1)
        cs = jnp.clip(q - NA_WC // 2, 0, GRID_W - NA_WC)
        ok = (kc >= cs) & (kc < cs + NA_WC)
        for h in range(NA_HEADS):
            for pr in range(NA_WR // 2):
                x = jnp.broadcast_to(t_ref[h, 2 * pr - p + NA_WR - 1], (GRID_W, 128))
                b = pltpu.roll(x, 128 - (NA_WC - 1), 1, stride=1, stride_axis=0)
                o_ref[h, :, 128 * pr:128 * pr + 128] = jnp.where(ok, b, NEG)

    return pl.pallas_call(
        body, name=name, out_shape=_sds((NA_WR, NA_HEADS, GRID_W, NA_KEYS), F32), grid=(NA_WR,),
        in_specs=[pl.BlockSpec((NA_HEADS, 16, 1, 128), lambda p: (0, 0, 0, 0))],
        out_specs=pl.BlockSpec((None, NA_HEADS, GRID_W, NA_KEYS), lambda p: (p, 0, 0, 0)),
        compiler_params=_params(("parallel",)))(pairs)


def _lane_halves():
    lane = lax.broadcasted_iota(jnp.int32, (1, 128), 1)
    return lane < HD


def na_fwd(q, k, v, bias, *, name):
    def body(q_ref, k_ref, v_ref, b_ref, o_ref, lse_ref):
        r = pl.program_id(0)
        off = pl.multiple_of(_na_row_start(r) * GRID_W, GRID_W)
        first = _lane_halves()
        sels = [first, jnp.logical_not(first)]
        lanes = [slice(128 * j, 128 * j + 128) for j in range(NA_HEADS // 2)]
        q2s = [q_ref[:, l] for l in lanes]
        k2s = [k_ref[pl.ds(off, NA_KEYS), l] for l in lanes]
        v2s = [v_ref[pl.ds(off, NA_KEYS), l] for l in lanes]
        scores = []
        for h in range(NA_HEADS):
            j, half = divmod(h, 2)
            scores.append(_dot(jnp.where(sels[half], q2s[j], jnp.zeros_like(q2s[j])), k2s[j], NT))
        probs, lses = [], []
        for h in range(NA_HEADS):
            b = b_ref[h]
            s = jnp.where(b > 0.5 * NEG, scores[h] + b, NEG)
            m = jnp.max(s, axis=-1, keepdims=True)
            e = jnp.exp(s - m)
            l = jnp.sum(e, axis=-1, keepdims=True)
            probs.append((e / l).astype(BF16))
            lses.append(m + jnp.log(l))
        for j in range(NA_HEADS // 2):
            zero = jnp.zeros_like(v2s[j])
            o2 = (_dot(probs[2 * j], jnp.where(sels[0], v2s[j], zero))
                  + _dot(probs[2 * j + 1], jnp.where(sels[1], v2s[j], zero)))
            o_ref[:, lanes[j]] = o2.astype(BF16)
        lse_ref[...] = jnp.concatenate(lses, axis=1)

    s_tok = q.shape[0]
    full = pl.BlockSpec((s_tok, 512), lambda r: (0, 0))
    return pl.pallas_call(
        body, name=name, out_shape=(_sds((s_tok, 512), BF16), _sds((s_tok, NA_HEADS), F32)), grid=(ROWS,),
        in_specs=[pl.BlockSpec((GRID_W, 512), lambda r: (r, 0)), full, full,
                  pl.BlockSpec((None, NA_HEADS, GRID_W, NA_KEYS), lambda r: (r - _na_row_start(r), 0, 0, 0))],
        out_specs=(pl.BlockSpec((GRID_W, 512), lambda r: (r, 0)), pl.BlockSpec((GRID_W, NA_HEADS), lambda r: (r, 0))),
        compiler_params=_params(("parallel",)))(q, k, v, bias)


def na_bwd(q, k, v, o, do, lse, bias, *, name):
    def body(q_ref, k_ref, v_ref, o_ref, do_ref, lse_ref, b_ref, dq_ref, dk_ref, dv_ref, db_ref):
        r = pl.program_id(0)

        @pl.when(r == 0)
        def _():
            dk_ref[...] = jnp.zeros_like(dk_ref)
            dv_ref[...] = jnp.zeros_like(dv_ref)

        @pl.when((r <= NA_WR // 2) | (r > ROWS - NA_WR // 2))
        def _():
            db_ref[...] = jnp.zeros_like(db_ref)

        off = pl.multiple_of(_na_row_start(r) * GRID_W, GRID_W)
        first = _lane_halves()
        sels = [first, jnp.logical_not(first)]
        lanes = [slice(128 * j, 128 * j + 128) for j in range(NA_HEADS // 2)]
        q2s = [q_ref[:, l] for l in lanes]
        k2s = [k_ref[pl.ds(off, NA_KEYS), l] for l in lanes]
        v2s = [v_ref[pl.ds(off, NA_KEYS), l] for l in lanes]
        do2s = [do_ref[:, l] for l in lanes]
        prods = [do2s[j].astype(F32) * o_ref[:, lanes[j]].astype(F32) for j in range(NA_HEADS // 2)]
        lse = lse_ref[...]
        qhs, dohs, scores, dps = [], [], [], []
        for h in range(NA_HEADS):
            j, half = divmod(h, 2)
            qhs.append(jnp.where(sels[half], q2s[j], jnp.zeros_like(q2s[j])))
            dohs.append(jnp.where(sels[half], do2s[j], jnp.zeros_like(do2s[j])))
            scores.append(_dot(qhs[h], k2s[j], NT))
            dps.append(_dot(dohs[h], v2s[j], NT))
        pbs, dsbs = [], []
        for h in range(NA_HEADS):
            j, half = divmod(h, 2)
            b = b_ref[h]
            s = jnp.where(b > 0.5 * NEG, scores[h] + b, NEG)
            p = jnp.exp(s - lse[:, h:h + 1])
            delta = jnp.sum(jnp.where(sels[half], prods[j], 0.0), axis=-1, keepdims=True)
            ds = p * (dps[h] - delta)
            db_ref[h] += ds
            pbs.append(p.astype(BF16))
            dsbs.append(ds.astype(BF16))
        for j in range(NA_HEADS // 2):
            a, b = 2 * j, 2 * j + 1
            zero = jnp.zeros_like(k2s[j])
            dq_ref[:, lanes[j]] = (_dot(dsbs[a], jnp.where(sels[0], k2s[j], zero))
                                   + _dot(dsbs[b], jnp.where(sels[1], k2s[j], zero)))
            dk_ref[pl.ds(off, NA_KEYS), lanes[j]] += _dot(dsbs[a], qhs[a], TN) + _dot(dsbs[b], qhs[b], TN)
            dv_ref[pl.ds(off, NA_KEYS), lanes[j]] += _dot(pbs[a], dohs[a], TN) + _dot(pbs[b], dohs[b], TN)

    s_tok = q.shape[0]
    full = pl.BlockSpec((s_tok, 512), lambda r: (0, 0))
    row = pl.BlockSpec((GRID_W, 512), lambda r: (r, 0))
    bias_spec = pl.BlockSpec((None, NA_HEADS, GRID_W, NA_KEYS), lambda r: (r - _na_row_start(r), 0, 0, 0))
    return pl.pallas_call(
        body, name=name,
        out_shape=(_sds((s_tok, 512), F32), _sds((s_tok, 512), F32), _sds((s_tok, 512), F32),
                   _sds((NA_WR, NA_HEADS, GRID_W, NA_KEYS), F32)),
        grid=(ROWS,),
        in_specs=[row, full, full, row, row, pl.BlockSpec((GRID_W, NA_HEADS), lambda r: (r, 0)), bias_spec],
        out_specs=(row, full, full, bias_spec), compiler_params=_params(("arbitrary",)))(q, k, v, o, do, lse, bias)


def t5_bucket_map():
    rel = np.arange(SW_KEYS)[None, :] - SW_BLK - np.arange(SW_BLK)[:, None]
    nb = 16
    max_exact = nb // 2
    n = np.abs(rel)
    large = max_exact + (np.log(np.maximum(n, 1) / max_exact) / np.log(128 / max_exact) * (nb - max_exact)).astype(np.int32)
    large = np.minimum(large, nb - 1)
    return ((rel > 0) * nb + np.where(n < max_exact, n, large)).astype(np.int32)


def t5_bias(table, *, name):
    rel = np.arange(-SW_BLK, SW_BLK + 1)
    nb, max_exact = 16, 8
    n = np.abs(rel)
    large = max_exact + (np.log(np.maximum(n, 1) / max_exact) / np.log(128 / max_exact) * (nb - max_exact)).astype(np.int32)
    bucket = ((rel > 0) * nb + np.where(n < max_exact, n, np.minimum(large, nb - 1))).astype(np.int32)
    u = jnp.pad(table[jnp.asarray(bucket)].T, ((0, 0), (0, SW_KEYS - bucket.shape[0]))).reshape(8, 1, SW_KEYS)

    def body(u_ref, o_ref):
        for h in range(8):
            x = jnp.broadcast_to(u_ref[h], (SW_BLK, SW_KEYS))
            o_ref[h] = pltpu.roll(x, 0, 1, stride=1, stride_axis=0)

    return pl.pallas_call(body, name=name, out_shape=_sds((8, SW_BLK, SW_KEYS), F32), compiler_params=_params())(u)


def _sw_valid(n):
    a = lax.broadcasted_iota(jnp.int32, (SW_BLK, SW_KEYS), 0)
    j = lax.broadcasted_iota(jnp.int32, (SW_BLK, SW_KEYS), 1)
    kpos = (n - 1) * SW_BLK + j
    return (jnp.abs(j - SW_BLK - a) <= SW_BLK) & (kpos >= 0) & (kpos < SEQ)


def _dup_group(x2, g, first):
    rolled = pltpu.roll(x2, HD, 1)
    return jnp.where(first, x2, rolled) if g == 0 else jnp.where(first, rolled, x2)


def sw_fwd(q, kv, t5, sink, *, name):
    def body(q_ref, kv_ref, t5_ref, sink_ref, o_ref, lse_ref):
        n = pl.program_id(0)
        off = pl.multiple_of(n * SW_BLK, SW_BLK)
        first = _lane_halves()
        sels = [first, jnp.logical_not(first)]
        valid = _sw_valid(n)
        k2 = kv_ref[pl.ds(off, SW_KEYS), 0:128]
        v2 = kv_ref[pl.ds(off, SW_KEYS), 128:256]
        kk = [_dup_group(k2, g, first) for g in range(2)]
        vv = [_dup_group(v2, g, first) for g in range(2)]
        q2s = [q_ref[:, 128 * j:128 * j + 128] for j in range(4)]
        scores = []
        for h in range(8):
            j, half = divmod(h, 2)
            scores.append(_dot(jnp.where(sels[half], q2s[j], jnp.zeros_like(q2s[j])), kk[j // 2], NT))
        probs, lses = [], []
        for h in range(8):
            s = jnp.where(valid, scores[h] + t5_ref[h], NEG)
            snk = sink_ref[h]
            m = jnp.maximum(jnp.max(s, axis=-1, keepdims=True), snk)
            e = jnp.exp(s - m)
            den = jnp.sum(e, axis=-1, keepdims=True) + jnp.exp(snk - m)
            probs.append((e / den).astype(BF16))
            lses.append(m + jnp.log(den))
        outs = []
        for j in range(4):
            vg = vv[j // 2]
            zero = jnp.zeros_like(vg)
            outs.append(_dot(probs[2 * j], jnp.where(sels[0], vg, zero)) + _dot(probs[2 * j + 1], jnp.where(sels[1], vg, zero)))
        o_ref[...] = jnp.concatenate(outs, axis=1).astype(BF16)
        lse_ref[...] = jnp.concatenate(lses, axis=1)

    s_tok = q.shape[0]
    blk = pl.BlockSpec((SW_BLK, 512), lambda n: (n, 0))
    return pl.pallas_call(
        body, name=name, out_shape=(_sds((s_tok, 512), BF16), _sds((s_tok, 8), F32)), grid=(SW_NB,),
        in_specs=[blk, pl.BlockSpec(kv.shape, lambda n: (0, 0)), pl.BlockSpec((8, SW_BLK, SW_KEYS), lambda n: (0, 0, 0)),
                  pl.BlockSpec(memory_space=pltpu.SMEM)],
        out_specs=(blk, pl.BlockSpec((SW_BLK, 8), lambda n: (n, 0))), compiler_params=_params(("parallel",)))(q, kv, t5, sink)


def sw_bwd(q, kv, o, do, lse, t5, sink, *, name):
    def body(q_ref, kv_ref, o_ref, do_ref, lse_ref, t5_ref, sink_ref, dq_ref, dkv_ref, dt5_ref, dsink_ref):
        n = pl.program_id(0)

        @pl.when(n == 0)
        def _():
            dkv_ref[...] = jnp.zeros_like(dkv_ref)
            dt5_ref[...] = jnp.zeros_like(dt5_ref)
            dsink_ref[...] = jnp.zeros_like(dsink_ref)

        off = pl.multiple_of(n * SW_BLK, SW_BLK)
        first = _lane_halves()
        sels = [first, jnp.logical_not(first)]
        valid = _sw_valid(n)
        k2 = kv_ref[pl.ds(off, SW_KEYS), 0:128]
        v2 = kv_ref[pl.ds(off, SW_KEYS), 128:256]
        kk = [_dup_group(k2, g, first) for g in range(2)]
        vv = [_dup_group(v2, g, first) for g in range(2)]
        lanes = [slice(128 * j, 128 * j + 128) for j in range(4)]
        q2s = [q_ref[:, l] for l in lanes]
        do2s = [do_ref[:, l] for l in lanes]
        prods = [do2s[j].astype(F32) * o_ref[:, lanes[j]].astype(F32) for j in range(4)]
        lse = lse_ref[...]
        qhs, dohs, scores, dps = [], [], [], []
        for h in range(8):
            j, half = divmod(h, 2)
            qhs.append(jnp.where(sels[half], q2s[j], jnp.zeros_like(q2s[j])))
            dohs.append(jnp.where(sels[half], do2s[j], jnp.zeros_like(do2s[j])))
            scores.append(_dot(qhs[h], kk[j // 2], NT))
            dps.append(_dot(dohs[h], vv[j // 2], NT))
        pbs, dsbs, dss, dsinks = [], [], [], []
        for h in range(8):
            j, half = divmod(h, 2)
            s = jnp.where(valid, scores[h] + t5_ref[h], NEG)
            lse_h = lse[:, h:h + 1]
            p = jnp.exp(s - lse_h)
            delta = jnp.sum(jnp.where(sels[half], prods[j], 0.0), axis=-1, keepdims=True)
            ds = p * (dps[h] - delta)
            dss.append(ds)
            dsinks.append(-jnp.sum(jnp.exp(sink_ref[h] - lse_h) * delta, axis=0, keepdims=True))
            pbs.append(p.astype(BF16))
            dsbs.append(ds.astype(BF16))
        dt5_ref[...] += jnp.stack(dss)
        dsink_ref[...] += jnp.concatenate(dsinks, axis=1)
        dqs = []
        for j in range(4):
            a, b = 2 * j, 2 * j + 1
            zero = jnp.zeros_like(kk[j // 2])
            dqs.append(_dot(dsbs[a], jnp.where(sels[0], kk[j // 2], zero)) + _dot(dsbs[b], jnp.where(sels[1], kk[j // 2], zero)))
        dq_ref[...] = jnp.concatenate(dqs, axis=1)
        dk_groups, dv_groups = [], []
        for g in range(2):
            dkk = sum(_dot(dsbs[h], qhs[h], TN) for h in range(4 * g, 4 * g + 4))
            dvv = sum(_dot(pbs[h], dohs[h], TN) for h in range(4 * g, 4 * g + 4))
            dk_groups.append(dkk + pltpu.roll(dkk, HD, 1))
            dv_groups.append(dvv + pltpu.roll(dvv, HD, 1))
        dkv_ref[pl.ds(off, SW_KEYS), :] += jnp.concatenate(
            [jnp.where(first, dk_groups[0], dk_groups[1]), jnp.where(first, dv_groups[0], dv_groups[1])], axis=1)

    s_tok = q.shape[0]
    blk = pl.BlockSpec((SW_BLK, 512), lambda n: (n, 0))
    kv_spec = pl.BlockSpec(kv.shape, lambda n: (0, 0))
    t5_spec = pl.BlockSpec((8, SW_BLK, SW_KEYS), lambda n: (0, 0, 0))
    vec = pl.BlockSpec((1, 8), lambda n: (0, 0))
    return pl.pallas_call(
        body, name=name,
        out_shape=(_sds((s_tok, 512), F32), _sds(kv.shape, F32), _sds((8, SW_BLK, SW_KEYS), F32), _sds((1, 8), F32)),
        grid=(SW_NB,), in_specs=[blk, kv_spec, blk, blk, pl.BlockSpec((SW_BLK, 8), lambda n: (n, 0)), t5_spec,
                                 pl.BlockSpec(memory_space=pltpu.SMEM)],
        out_specs=(blk, kv_spec, t5_spec, vec), compiler_params=_params(("arbitrary",)))(q, kv, o, do, lse, t5, sink)


def mixer_output_fwd(o_na, o_sw, zg, bias, wa, ws, wo, res, *, name, tm=512):
    def body(ona_ref, osw_ref, z0_ref, z1_ref, b0_ref, b1_ref, wa_ref, ws_ref, wo_ref, res_ref, y_ref, pa_ref, ps_ref, m_ref):
        cols = lambda o, w4_ref: jnp.concatenate([_dot(o, w4_ref[j]) for j in range(NSH)], axis=1)
        pa = cols(ona_ref[...], wa_ref).astype(BF16)
        ps = cols(osw_ref[...], ws_ref).astype(BF16)
        pa_ref[...] = pa
        ps_ref[...] = ps
        g0 = jax.nn.sigmoid(z0_ref[...] + b0_ref[...])
        g1 = jax.nn.sigmoid(z1_ref[...] + b1_ref[...])
        merged = (g0 * pa + g1 * ps).astype(BF16)
        m_ref[...] = merged
        y_ref[...] = res_ref[...] + _dot(merged, wo_ref[...])

    s = zg.shape[0]
    half = lambda j: pl.BlockSpec((tm, DM), lambda i, j=j: (i, j))
    bvec = lambda j: pl.BlockSpec((1, DM), lambda i, j=j: (0, j))
    att = pl.BlockSpec((tm, 512), lambda i: (i, 0))
    whole = lambda a: pl.BlockSpec(a.shape, lambda i: (0,) * a.ndim, pipeline_mode=pl.Buffered(1))
    act = _sds((s, DM), BF16)
    return pl.pallas_call(
        body, name=name, out_shape=(_sds((s, DM), F32), act, act, act), grid=(s // tm,),
        in_specs=[att, att, half(0), half(1), bvec(0), bvec(1), whole(wa), whole(ws), whole(wo), half(0)],
        out_specs=(half(0),) * 4, compiler_params=_params(("parallel",)))(o_na, o_sw, zg, zg, bias, bias, wa, ws, wo, res)


def mixer_output_bwd(dy, zg, bias, pa, ps, wa, ws, wo, *, name, tm=512, dep=None):
    def body(dy_ref, z0_ref, z1_ref, b0_ref, b1_ref, pa_ref, ps_ref, wa_ref, ws_ref, wo_ref, *rest):
        dpa_ref, dps_ref, dz_ref, db_ref, dona_ref, dosw_ref = rest[-6:]

        @pl.when(pl.program_id(0) == 0)
        def _():
            db_ref[...] = jnp.zeros_like(db_ref)

        dm = _dot(dy_ref[...].astype(BF16), wo_ref[...], NT)
        g0 = jax.nn.sigmoid(z0_ref[...] + b0_ref[...])
        g1 = jax.nn.sigmoid(z1_ref[...] + b1_ref[...])
        dpa = (dm * g0).astype(BF16)
        dps = (dm * g1).astype(BF16)
        dpa_ref[...] = dpa
        dps_ref[...] = dps
        dz0 = dm * pa_ref[...] * g0 * (1.0 - g0)
        dz1 = dm * ps_ref[...] * g1 * (1.0 - g1)
        dz_ref[:, 0:DM] = dz0.astype(BF16)
        dz_ref[:, DM:2 * DM] = dz1.astype(BF16)
        db_ref[:, 0:DM] += jnp.sum(dz0, axis=0, keepdims=True)
        db_ref[:, DM:2 * DM] += jnp.sum(dz1, axis=0, keepdims=True)
        width = DM // NSH
        back = lambda dp, w4_ref: sum(_dot(dp[:, j * width:(j + 1) * width], w4_ref[j], NT) for j in range(NSH))
        dona_ref[...] = back(dpa, wa_ref).astype(BF16)
        dosw_ref[...] = back(dps, ws_ref).astype(BF16)

    s = zg.shape[0]
    half = lambda j: pl.BlockSpec((tm, DM), lambda i, j=j: (i, j))
    bvec = lambda j: pl.BlockSpec((1, DM), lambda i, j=j: (0, j))
    att = pl.BlockSpec((tm, 512), lambda i: (i, 0))
    whole = lambda a: pl.BlockSpec(a.shape, lambda i: (0,) * a.ndim, pipeline_mode=pl.Buffered(1))
    ins, specs = _with_dep([dy, zg, zg, bias, bias, pa, ps, wa, ws, wo],
                           [half(0), half(0), half(1), bvec(0), bvec(1), half(0), half(0), whole(wa), whole(ws), whole(wo)], dep)
    return pl.pallas_call(
        body, name=name,
        out_shape=(_sds((s, DM), BF16), _sds((s, DM), BF16), _sds((s, GATE_W), BF16), _sds((1, GATE_W), F32),
                   _sds((s, 512), BF16), _sds((s, 512), BF16)),
        grid=(s // tm,), in_specs=specs,
        out_specs=(half(0), half(0), pl.BlockSpec((tm, GATE_W), lambda i: (i, 0)), pl.BlockSpec((1, GATE_W), lambda i: (0, 0)),
                   att, att),
        compiler_params=_params(("arbitrary",)))(*ins)


def adamw_small(ws, gs, ms, vs, *, name):
    cnt = len(ws)

    def body(*refs):
        ins, outs = refs[:4 * cnt], refs[4 * cnt:]
        for i in range(cnt):
            w_ref, g_ref, m_ref, v_ref = ins[4 * i:4 * i + 4]
            d_ref, nm_ref, nv_ref = outs[3 * i:3 * i + 3]
            g = g_ref[...]
            nm = ADAM_B1 * m_ref[...] + (1.0 - ADAM_B1) * g
            nv = ADAM_B2 * v_ref[...] + (1.0 - ADAM_B2) * jnp.square(g)
            m_hat = nm / (1.0 - ADAM_B1 ** ADAM_STEP)
            v_hat = nv / (1.0 - ADAM_B2 ** ADAM_STEP)
            d_ref[...] = -ADAM_LR * (m_hat / (jnp.sqrt(v_hat) + ADAM_EPS) + ADAM_WD * w_ref[...])
            nm_ref[...] = nm
            nv_ref[...] = nv

    flat = [a for i in range(cnt) for a in (ws[i], gs[i], ms[i], vs[i])]
    res = pl.pallas_call(
        body, name=name, out_shape=tuple(_sds(ws[i].shape, F32) for i in range(cnt) for _ in range(3)),
        compiler_params=_params())(*flat)
    return [tuple(res[3 * i:3 * i + 3]) for i in range(cnt)]


def adamw_layer(ws, ms, vs, mines, theirs, cidx, layer, filled=None, *, name):
    cnt = len(ws)
    _, k, n = ws[0].shape
    nt = 2
    tk = k // 2 // nt

    def body(c_ref, *refs):
        own = pl.program_id(0) == c_ref[0]
        outs = refs[-4 * cnt:]
        for i in range(cnt):
            w_ref, m_ref, v_ref, a_ref, b_ref = refs[5 * i:5 * i + 5]
            g_ref, d_ref, nm_ref, nv_ref = outs[4 * i:4 * i + 4]
            g = jnp.where(own, a_ref[...], b_ref[...])
            g_ref[...] = g
            nm = ADAM_B1 * m_ref[...] + (1.0 - ADAM_B1) * g
            nv = ADAM_B2 * v_ref[...] + (1.0 - ADAM_B2) * jnp.square(g)
            m_hat = nm / (1.0 - ADAM_B1 ** ADAM_STEP)
            v_hat = nv / (1.0 - ADAM_B2 ** ADAM_STEP)
            d_ref[...] = -ADAM_LR * (m_hat / (jnp.sqrt(v_hat) + ADAM_EPS) + ADAM_WD * w_ref[...])
            nm_ref[...] = nm
            nv_ref[...] = nv

    full = pl.BlockSpec((None, tk, n), lambda hf, t, c: (layer, hf * nt + t, 0))
    half_mine = pl.BlockSpec((tk, n), lambda hf, t, c: (jnp.where(hf == c[0], t, 0), 0))
    half_theirs = pl.BlockSpec((tk, n), lambda hf, t, c: (jnp.where(hf != c[0], t, 0), 0))
    out = _sds(ws[0].shape, F32)
    ins, specs, aliases = [cidx], [], {}
    for i in range(cnt):
        ins += [ws[i], ms[i], vs[i], mines[i], theirs[i]]
        specs += [full, full, full, half_mine, half_theirs]
    if filled is not None:
        aliases = {len(ins) + j: j for j in range(4 * cnt)}
        ins += [a for f in filled for a in f]
        specs += [pl.BlockSpec(memory_space=pl.ANY)] * (4 * cnt)
    res = pl.pallas_call(
        body, name=name, out_shape=(out,) * (4 * cnt),
        grid_spec=pltpu.PrefetchScalarGridSpec(
            num_scalar_prefetch=1, grid=(2, nt), in_specs=specs, out_specs=(full,) * (4 * cnt)),
        input_output_aliases=aliases,
        compiler_params=_params(("arbitrary", "arbitrary")))(*ins)
    return [tuple(res[4 * i:4 * i + 4]) for i in range(cnt)]


def t5_table_grad(dt5_a, dt5_b, *, name):
    def body(a_ref, b_ref, map_ref, o_ref):
        d = a_ref[...] + b_ref[...]
        bucket = map_ref[...]
        for b in range(32):
            hit = (bucket == b)[None]
            o_ref[b] = jnp.sum(jnp.sum(jnp.where(hit, d, 0.0), axis=2), axis=1, keepdims=True)

    return pl.pallas_call(
        body, name=name, out_shape=_sds((32, 8, 1), F32), compiler_params=_params())(
            dt5_a, dt5_b, jnp.asarray(t5_bucket_map()))


def rpb_grad(dbias, *, name):
    def body(d_ref, rev_ref, o_ref):
        rev = rev_ref[...]
        for h in range(NA_HEADS):
            for pr in range(NA_WR // 2):
                d = d_ref[h, :, 128 * pr:128 * pr + 128]
                hi = d.astype(BF16)
                lo = (d - hi.astype(F32)).astype(BF16)
                flipped = _dot(rev, hi) + _dot(rev, lo)
                o_ref[h, pr] = jnp.sum(pltpu.roll(flipped, 0, 1, stride=1, stride_axis=0), axis=0, keepdims=True)

    anti = jnp.asarray(np.eye(GRID_W, dtype=np.float32)[::-1], dtype=BF16)
    e = pl.pallas_call(
        body, name=name, out_shape=_sds((NA_WR, NA_HEADS, NA_WR // 2, 1, 128), F32), grid=(NA_WR,),
        in_specs=[pl.BlockSpec((None, NA_HEADS, GRID_W, NA_KEYS), lambda p: (p, 0, 0, 0)),
                  pl.BlockSpec((GRID_W, GRID_W), lambda p: (0, 0))],
        out_specs=pl.BlockSpec((None, NA_HEADS, NA_WR // 2, 1, 128), lambda p: (p, 0, 0, 0, 0)),
        compiler_params=_params(("parallel",)))(dbias, anti)
    nci, nri = 2 * NA_WC - 1, 2 * NA_WR - 1
    e = e.reshape(NA_WR, NA_HEADS, NA_WR // 2, 128).transpose(0, 2, 1, 3).reshape(NA_WR * NA_WR // 2, NA_HEADS, 128)
    parts = jnp.concatenate([e[..., 48:48 + nci], jnp.concatenate([e[..., 112:128], e[..., 0:nci - 16]], axis=-1)], axis=0)
    p, pr = np.arange(NA_WR)[:, None], np.arange(NA_WR // 2)[None, :]
    ri = np.concatenate([(2 * pr - p + NA_WR - 1).reshape(-1), (2 * pr - p + NA_WR).reshape(-1)])
    pick = jnp.asarray((ri[None, :] == np.arange(16)[:, None]).astype(np.float32))
    out = mm(pick, parts.reshape(2 * NA_WR * NA_WR // 2, NA_HEADS * nci), name=name + "_rows", exact=True)
    return out.reshape(16, NA_HEADS, nci)[:nri].transpose(1, 0, 2)


BIG = ("ffn1_w_gate", "ffn1_w_up", "ffn1_w_down", "w_in", "w_branch_na", "w_branch_sw", "w_out",
       "ffn2_w_gate", "ffn2_w_up", "ffn2_w_down")
SMALL = ("ffn1_norm", "mix_norm", "b_gate", "na_q_norm", "na_k_norm", "na_rpb", "sw_q_norm", "sw_k_norm", "sw_sink",
         "ffn2_norm")


def _mixer_weights(g):
    w_in_t = g["w_in"].reshape(IN_W, DM)
    return dict(w_in_t=w_in_t, wa=g["w_branch_na"], ws=g["w_branch_sw"], wo=g["w_out"].reshape(DM, DM))


GROUPS = {"ffn1": ("ffn1_w_gate", "ffn1_w_up", "ffn1_w_down"), "mix": ("w_in", "w_branch_na", "w_branch_sw", "w_out"),
          "ffn2": ("ffn2_w_gate", "ffn2_w_up", "ffn2_w_down")}


def layer_fwd(x, p, weights, t5b, target=None):
    row = lambda v: v.reshape(1, -1)
    stacked = lambda g: {n: a.reshape(DFF, DM) for n, a in g.items()}
    g1 = stacked(weights("ffn1", x))
    y1, h1, gg1, uu1 = ffn_fwd(x, row(p["ffn1_norm"]), g1["ffn1_w_gate"], g1["ffn1_w_up"], g1["ffn1_w_down"], name="ffn_fwd")
    w = _mixer_weights(weights("mix", y1))
    hm, z, zg, qa, ka, va, qs, kv = mixer_input_fwd(y1, row(p["mix_norm"]), w["w_in_t"], p["na_q_norm"], p["na_k_norm"],
                                                    p["sw_q_norm"], p["sw_k_norm"], name="mixer_input_fwd")
    bias = p["na_bias"]
    o_na, lse_na = na_fwd(qa, ka, va, bias, name="na_fwd")
    kvp = jnp.pad(kv, ((SW_BLK, SW_BLK), (0, 0)))
    sink = p["sw_sink"]
    o_sw, lse_sw = sw_fwd(qs, kvp, t5b, sink, name="sw_fwd")
    y2, pa, ps, merged = mixer_output_fwd(o_na, o_sw, zg, row(p["b_gate"]), w["wa"], w["ws"], w["wo"], y1,
                                          name="mixer_output_fwd")
    g2 = stacked(weights("ffn2", y2))
    *y3, h2, gg2, uu2 = ffn_fwd(y2, row(p["ffn2_norm"]), g2["ffn2_w_gate"], g2["ffn2_w_up"], g2["ffn2_w_down"], target,
                                name="ffn_fwd")
    y3 = y3[0] if target is None else tuple(y3)
    saved = dict(x=x, y1=y1, h1=h1, gg1=gg1, uu1=uu1, hm=hm, z=z, zg=zg, qa=qa, ka=ka, va=va, qs=qs, kvp=kvp, bias=bias,
                 o_na=o_na, lse_na=lse_na, o_sw=o_sw, lse_sw=lse_sw, pa=pa, ps=ps, merged=merged, y2=y2, h2=h2, gg2=gg2,
                 uu2=uu2, w=w, sink=sink, g1=g1, g2=g2)
    return y3, saved


def layer_bwd(dy3, dy3_bf, sv, p, t5b, emit, dep=None):
    w, g1, g2 = sv["w"], sv["g1"], sv["g2"]
    row = lambda v: v.reshape(1, -1)
    fold = lambda v: v.reshape(-1, HD).sum(axis=0)
    small = {}
    dy2, dy2_bf, small["ffn2_norm"], act, dg, du = ffn_bwd_tokens(
        dy3, sv["y2"], row(p["ffn2_norm"]), sv["gg2"], sv["uu2"], g2["ffn2_w_gate"], g2["ffn2_w_up"], g2["ffn2_w_down"],
        name="ffn_bwd_tokens", dep=dep)
    shards = lambda gs: [g.reshape(NSH, FSH, DM) for g in gs]
    token = emit("ffn2", shards(ffn_bwd_weights(sv["h2"], dy3_bf, act, dg, du, name="ffn_bwd_weights")))
    dpa, dps, dzg, small["b_gate"], do_na, do_sw = mixer_output_bwd(
        dy2, sv["zg"], row(p["b_gate"]), sv["pa"], sv["ps"], w["wa"], w["ws"], w["wo"], name="mixer_output_bwd", dep=token)
    gw_out, gw_na, gw_sw = mixer_output_dw(sv["merged"], dy2_bf, sv["o_na"], dpa, sv["o_sw"], dps, name="mixer_output_dw")
    gw_out = gw_out.reshape(NSH, DM // NSH, DM)
    dqa, dka, dva, dbias = na_bwd(sv["qa"], sv["ka"], sv["va"], sv["o_na"], do_na, sv["lse_na"], sv["bias"], name="na_bwd")
    dqs, dkvp, dt5, dsink = sw_bwd(sv["qs"], sv["kvp"], sv["o_sw"], do_sw, sv["lse_sw"], t5b, sv["sink"], name="sw_bwd")
    dkv = dkvp[SW_BLK:SW_BLK + SEQ]
    dz, dgqa, dgka, dgqs, dgks = qknorm_bwd(sv["z"], dqa, dka, dva, dqs, dkv, p["na_q_norm"], p["na_k_norm"],
                                            p["sw_q_norm"], p["sw_k_norm"], name="qknorm_bwd")
    small["na_q_norm"], small["na_k_norm"], small["sw_q_norm"], small["sw_k_norm"] = fold(dgqa), fold(dgka), fold(dgqs), fold(dgks)
    small["na_rpb"] = rpb_grad(dbias, name="rpb_grad")
    small["sw_sink"] = dsink
    gw_in = mixer_input_dw(dz, dzg, sv["hm"], name="mixer_input_dw").reshape(NSH, IN_W // NSH, DM)
    token = emit("mix", (gw_in, gw_na, gw_sw, gw_out))
    dy1, dy1_bf, small["mix_norm"] = mixer_input_bwd(dz, dzg, w["w_in_t"], sv["y1"], row(p["mix_norm"]), dy2,
                                                     name="mixer_input_bwd", dep=token)
    dx, dx_bf, small["ffn1_norm"], act, dg, du = ffn_bwd_tokens(
        dy1, sv["x"], row(p["ffn1_norm"]), sv["gg1"], sv["uu1"], g1["ffn1_w_gate"], g1["ffn1_w_up"], g1["ffn1_w_down"],
        name="ffn_bwd_tokens")
    emit("ffn1", shards(ffn_bwd_weights(sv["h1"], dy1_bf, act, dg, du, name="ffn_bwd_weights")))
    return dx, dx_bf, small, dt5


ANY = pl.BlockSpec(memory_space=pl.ANY)


def _place():
    x, y, c = lax.axis_index("x"), lax.axis_index("y"), lax.axis_index("c")
    chips = [(1 - x, y), (x, 1 - y), (1 - x, 1 - y)]
    return x, y, c, chips


def _remote(src, dst, send_sem, recv_sem, to):
    return pltpu.make_async_remote_copy(src_ref=src, dst_ref=dst, send_sem=send_sem, recv_sem=recv_sem, device_id=to,
                                        device_id_type=MESH)


HBM = pl.BlockSpec(memory_space=pltpu.HBM)
SEM = pl.BlockSpec(memory_space=pltpu.SEMAPHORE)
ORDERED_EFFECT = pltpu.SideEffectType.DATAFLOW_SIDE_EFFECTING


def _in_hbm(v):
    return pltpu.with_memory_space_constraint(v, pltpu.HBM)


def _row_half(ref_shape_rows, c):
    half = ref_shape_rows // 2
    return pl.ds(c * half, half)


def _ici_gather_copies(w, land, send_sems, recv_sems):
    x, y, c, chips = _place()
    me = 2 * x + y
    copies = []
    for a in range(len(w)):
        rows = _row_half(w[a].shape[0], c)
        for k, chip in enumerate(chips):
            copies.append(_remote(w[a].at[rows], land[a].at[me, rows], send_sems.at[4 * a + k], recv_sems.at[4 * a + k],
                                  (*chip, c)))
        copies.append(_remote(w[a], land[a].at[me], send_sems.at[4 * a + 3], recv_sems.at[4 * a + 3], (x, y, 1 - c)))
    return copies


def _d2d_gather_copies(w, land, send_sems, recv_sems):
    x, y, c, chips = _place()
    copies = []
    for a in range(len(w)):
        rows = _row_half(w[a].shape[0], c)
        for k, (cx, cy) in enumerate(chips):
            blk = land[a].at[2 * cx + cy, rows]
            copies.append(_remote(blk, blk, send_sems.at[3 * a + k], recv_sems.at[3 * a + k], (x, y, 1 - c)))
    return copies


def _d2d_gather_waits(w, land, send_sems, recv_sems):
    x, y, c, chips = _place()
    waits = []
    for a in range(len(w)):
        rows = _row_half(w[a].shape[0], 1 - c)
        for k, (cx, cy) in enumerate(chips):
            blk = land[a].at[2 * cx + cy, rows]
            waits.append(_remote(blk, blk, send_sems.at[3 * a + k], recv_sems.at[3 * a + k], (x, y, 1 - c)))
    return waits


def gather_start(groups, dep=None, *, name):
    sizes = [len(g) for g in groups]
    shards = [s for g in groups for s in g]
    n, ng = len(shards), len(groups)
    extra = [] if dep is None else [dep]

    def body(*refs):
        first_out = 2 * n + len(extra)
        w, land, sems = refs[:n], refs[n:2 * n], refs[first_out:first_out + 2 * ng]
        off = 0
        for gi, size in enumerate(sizes):
            for cp in _ici_gather_copies(w[off:off + size], land[off:off + size], sems[2 * gi], sems[2 * gi + 1]):
                cp.start()
            off += size

    lands = [lax.empty((NSH,) + s.shape, s.dtype) for s in shards]
    sem_shapes = tuple(pltpu.SemaphoreType.DMA((4 * size,)) for size in sizes for _ in range(2))
    res = pl.pallas_call(
        body, name=name,
        out_shape=sem_shapes + tuple(pltpu.HBM(s.shape, s.dtype) for s in shards) + tuple(pltpu.HBM(l.shape, l.dtype) for l in lands),
        in_specs=[HBM] * (2 * n) + [ANY] * len(extra), out_specs=(SEM,) * (2 * ng) + (HBM,) * (2 * n),
        input_output_aliases={i: 2 * ng + i for i in range(2 * n)},
        compiler_params=pltpu.CompilerParams(has_side_effects=ORDERED_EFFECT))(
            *[_in_hbm(s) for s in shards], *[_in_hbm(l) for l in lands], *extra)
    out, off = [], 0
    for gi, size in enumerate(sizes):
        out.append((res[2 * gi], res[2 * gi + 1], list(res[2 * ng + off:2 * ng + off + size]),
                    list(res[2 * ng + n + off:2 * ng + n + off + size])))
        off += size
    return out


def gather_wait(send_sems, recv_sems, shards, lands, after, *, name):
    n = len(shards)

    def body(*refs):
        w, land = refs[:n], refs[n:2 * n]
        send, recv = refs[2 * n:2 * n + 2]
        for cp in _ici_gather_copies(w, land, send, recv):
            cp.wait_send()
            cp.wait_recv()

    res = pl.pallas_call(
        body, name=name,
        out_shape=tuple(pltpu.HBM(s.shape, s.dtype) for s in shards) + tuple(pltpu.HBM(l.shape, l.dtype) for l in lands),
        in_specs=[HBM] * (2 * n) + [SEM, SEM] + [ANY] * len(after), out_specs=(HBM,) * (2 * n),
        input_output_aliases={i: i for i in range(2 * n)},
        compiler_params=pltpu.CompilerParams(has_side_effects=ORDERED_EFFECT))(*shards, *lands, send_sems, recv_sems, *after)
    return list(res[:n]), list(res[n:])


def gather_finish(shards, lands, *, name):
    n = len(shards)

    def body(*refs):
        w, land = refs[:n], refs[n:2 * n]
        send_sems, recv_sems = refs[3 * n:]
        d2d = _d2d_gather_copies(w, land, send_sems, recv_sems)
        for cp in d2d:
            cp.start()
        for cp in _d2d_gather_waits(w, land, send_sems, recv_sems):
            cp.wait_recv()
        for cp in d2d:
            cp.wait_send()

    return list(pl.pallas_call(
        body, name=name, out_shape=tuple(pltpu.HBM(l.shape, l.dtype) for l in lands),
        in_specs=[ANY] * (2 * n), out_specs=tuple([ANY] * n), input_output_aliases={n + i: i for i in range(n)},
        scratch_shapes=[pltpu.SemaphoreType.DMA((3 * n,)), pltpu.SemaphoreType.DMA((3 * n,))])(*shards, *lands))


def _pair_exchange_copies(g, buf, send_sems, recv_sems):
    x, y, c, _ = _place()
    copies = []
    for a in range(len(g)):
        half = g[a].shape[1] // 2
        copies.append(_remote(g[a].at[:, pl.ds((1 - c) * half, half)], buf[a], send_sems.at[a], recv_sems.at[a], (x, y, 1 - c)))
    return copies


def pair_exchange_start(grads, dep=None, *, name):
    n = len(grads)
    extra = [] if dep is None else [dep]

    def body(*refs):
        sems = refs[2 * n + len(extra):]
        for cp in _pair_exchange_copies(refs[:n], refs[n:2 * n], sems[0], sems[1]):
            cp.start()
        refs[-1][...] = jnp.zeros_like(refs[-1])

    lands = [lax.empty((NSH, g.shape[1] // 2, g.shape[2]), g.dtype) for g in grads]
    res = pl.pallas_call(
        body, name=name,
        out_shape=(pltpu.SemaphoreType.DMA((n,)), pltpu.SemaphoreType.DMA((n,)))
        + tuple(pltpu.HBM(g.shape, g.dtype) for g in grads) + tuple(pltpu.HBM(l.shape, l.dtype) for l in lands)
        + (_sds((8, 128), F32),),
        in_specs=[HBM] * (2 * n) + [ANY] * len(extra),
        out_specs=(SEM, SEM) + (HBM,) * (2 * n) + (pl.BlockSpec(memory_space=pltpu.VMEM),),
        input_output_aliases={i: 2 + i for i in range(2 * n)},
        compiler_params=pltpu.CompilerParams(has_side_effects=ORDERED_EFFECT))(
            *[_in_hbm(g) for g in grads], *[_in_hbm(l) for l in lands], *extra)
    return res[0], res[1], list(res[2:2 + n]), list(res[2 + n:2 + 2 * n]), res[-1]


def pair_exchange_wait(send_sems, recv_sems, grads, lands, after, *, name):
    n = len(grads)

    def body(*refs):
        for cp in _pair_exchange_copies(refs[:n], refs[n:2 * n], refs[2 * n], refs[2 * n + 1]):
            cp.wait_send()
            cp.wait_recv()

    res = pl.pallas_call(
        body, name=name,
        out_shape=tuple(pltpu.HBM(g.shape, g.dtype) for g in grads) + tuple(pltpu.HBM(l.shape, l.dtype) for l in lands),
        in_specs=[HBM] * (2 * n) + [SEM, SEM] + [ANY] * len(after), out_specs=(HBM,) * (2 * n),
        input_output_aliases={i: i for i in range(2 * n)},
        compiler_params=pltpu.CompilerParams(has_side_effects=ORDERED_EFFECT))(*grads, *lands, send_sems, recv_sems, *after)
    return list(res[:n]), list(res[n:])


def _chip_exchange_copies(s, buf, send_sems, recv_sems):
    x, y, c, chips = _place()
    return [_remote(s[a].at[2 * cx + cy], buf[a].at[k], send_sems.at[3 * a + k], recv_sems.at[3 * a + k], (cx, cy, c))
            for a in range(len(s)) for k, (cx, cy) in enumerate(chips)]


def exchange_start(sums, grads, *, name):
    n1, n2 = len(sums), len(grads)

    def body(*refs):
        first_out = 2 * (n1 + n2)
        chip = _chip_exchange_copies(refs[:n1], refs[n1:2 * n1], refs[first_out], refs[first_out + 1])
        pair = _pair_exchange_copies(refs[2 * n1:2 * n1 + n2], refs[2 * n1 + n2:first_out], refs[first_out + 2],
                                     refs[first_out + 3])
        for cp in chip + pair:
            cp.start()
        refs[-1][...] = jnp.zeros_like(refs[-1])

    chip_lands = [lax.empty((3,) + s.shape[1:], s.dtype) for s in sums]
    pair_lands = [lax.empty((NSH, g.shape[1] // 2, g.shape[2]), g.dtype) for g in grads]
    arrays = list(sums) + chip_lands + list(grads) + pair_lands
    res = pl.pallas_call(
        body, name=name,
        out_shape=(pltpu.SemaphoreType.DMA((3 * n1,)), pltpu.SemaphoreType.DMA((3 * n1,)), pltpu.SemaphoreType.DMA((n2,)),
                   pltpu.SemaphoreType.DMA((n2,)))
        + tuple(pltpu.HBM(a.shape, a.dtype) for a in arrays) + (_sds((8, 128), F32),),
        in_specs=[HBM] * len(arrays), out_specs=(SEM,) * 4 + (HBM,) * len(arrays) + (pl.BlockSpec(memory_space=pltpu.VMEM),),
        input_output_aliases={i: 4 + i for i in range(len(arrays))},
        compiler_params=pltpu.CompilerParams(has_side_effects=ORDERED_EFFECT))(*[_in_hbm(a) for a in arrays])
    thru = list(res[4:4 + len(arrays)])
    chip = (res[0], res[1], thru[:n1], thru[n1:2 * n1])
    pair = (res[2], res[3], thru[2 * n1:2 * n1 + n2], thru[2 * n1 + n2:])
    return chip, pair, res[-1]


def chip_exchange_start(sums, *, name):
    n = len(sums)

    def body(*refs):
        for cp in _chip_exchange_copies(refs[:n], refs[n:2 * n], refs[2 * n], refs[2 * n + 1]):
            cp.start()
        refs[-1][...] = jnp.zeros_like(refs[-1])

    lands = [lax.empty((3,) + s.shape[1:], s.dtype) for s in sums]
    res = pl.pallas_call(
        body, name=name,
        out_shape=(pltpu.SemaphoreType.DMA((3 * n,)), pltpu.SemaphoreType.DMA((3 * n,)))
        + tuple(pltpu.HBM(s.shape, s.dtype) for s in sums) + tuple(pltpu.HBM(l.shape, l.dtype) for l in lands)
        + (_sds((8, 128), F32),),
        in_specs=[HBM] * (2 * n), out_specs=(SEM, SEM) + (HBM,) * (2 * n) + (pl.BlockSpec(memory_space=pltpu.VMEM),),
        input_output_aliases={i: 2 + i for i in range(2 * n)},
        compiler_params=pltpu.CompilerParams(has_side_effects=ORDERED_EFFECT))(
            *[_in_hbm(s) for s in sums], *[_in_hbm(l) for l in lands])
    return res[0], res[1], list(res[2:2 + n]), list(res[2 + n:2 + 2 * n]), res[-1]


def chip_exchange_wait(send_sems, recv_sems, sums, lands, after, *, name):
    n = len(sums)

    def body(*refs):
        for cp in _chip_exchange_copies(refs[:n], refs[n:2 * n], refs[2 * n], refs[2 * n + 1]):
            cp.wait_send()
            cp.wait_recv()

    res = pl.pallas_call(
        body, name=name,
        out_shape=tuple(pltpu.HBM(s.shape, s.dtype) for s in sums) + tuple(pltpu.HBM(l.shape, l.dtype) for l in lands),
        in_specs=[HBM] * (2 * n) + [SEM, SEM] + [ANY] * len(after), out_specs=(HBM,) * (2 * n),
        input_output_aliases={i: i for i in range(2 * n)},
        compiler_params=pltpu.CompilerParams(has_side_effects=ORDERED_EFFECT))(*sums, *lands, send_sems, recv_sems, *after)
    return list(res[:n]), list(res[n:])


def _pair_send_copies(h, got, send_sems, recv_sems):
    x, y, c, _ = _place()
    return [_remote(h[i], got[i], send_sems.at[i], recv_sems.at[i], (x, y, 1 - c)) for i in range(len(h))]


def pair_send_start(halves, *, name):
    n = len(halves)

    def body(*refs):
        for cp in _pair_send_copies(refs[:n], refs[n:2 * n], refs[2 * n], refs[2 * n + 1]):
            cp.start()
        refs[-1][...] = jnp.zeros_like(refs[-1])

    lands = [lax.empty(h.shape, h.dtype) for h in halves]
    res = pl.pallas_call(
        body, name=name,
        out_shape=(pltpu.SemaphoreType.DMA((n,)), pltpu.SemaphoreType.DMA((n,)))
        + tuple(pltpu.HBM(h.shape, h.dtype) for h in halves) * 2 + (_sds((8, 128), F32),),
        in_specs=[HBM] * (2 * n), out_specs=(SEM, SEM) + (HBM,) * (2 * n) + (pl.BlockSpec(memory_space=pltpu.VMEM),),
        input_output_aliases={i: 2 + i for i in range(2 * n)},
        compiler_params=pltpu.CompilerParams(has_side_effects=ORDERED_EFFECT))(
            *[_in_hbm(h) for h in halves], *[_in_hbm(l) for l in lands])
    return res[0], res[1], list(res[2:2 + n]), list(res[2 + n:2 + 2 * n]), res[-1]


def pair_send_wait(send_sems, recv_sems, halves, lands, after, *, name):
    n = len(halves)

    def body(*refs):
        for cp in _pair_send_copies(refs[:n], refs[n:2 * n], refs[2 * n], refs[2 * n + 1]):
            cp.wait_send()
            cp.wait_recv()

    res = pl.pallas_call(
        body, name=name, out_shape=tuple(pltpu.HBM(h.shape, h.dtype) for h in halves) * 2,
        in_specs=[HBM] * (2 * n) + [SEM, SEM] + [ANY] * len(after), out_specs=(HBM,) * (2 * n),
        input_output_aliases={i: i for i in range(2 * n)},
        compiler_params=pltpu.CompilerParams(has_side_effects=ORDERED_EFFECT))(*halves, *lands, send_sems, recv_sems, *after)
    return list(res[:n]), list(res[n:])


def allreduce_small(v, *, name):
    rows = v.shape[0]

    def body(v_ref, o_ref, gath, send_sems, recv_sems):
        x, y, c, _ = _place()
        me = 4 * x + 2 * y + c
        gath[me] = v_ref[...]
        copies = []
        for k in range(1, 8):
            fx, fy, fc = (k >> 2) & 1, (k >> 1) & 1, k & 1
            peer = (jnp.where(fx, 1 - x, x), jnp.where(fy, 1 - y, y), jnp.where(fc, 1 - c, c))
            cp = _remote(v_ref, gath.at[me], send_sems.at[k - 1], recv_sems.at[k - 1], peer)
            cp.start()
            copies.append(cp)
        for cp in copies:
            cp.wait()
        acc = gath[0]
        for d in range(1, 8):
            acc = acc + gath[d]
        o_ref[...] = acc

    return pl.pallas_call(
        body, name=name, out_shape=_sds(v.shape, F32),
        in_specs=[pl.BlockSpec(memory_space=pltpu.VMEM)], out_specs=pl.BlockSpec(memory_space=pltpu.VMEM),
        scratch_shapes=[pltpu.VMEM((8, rows, 128), F32), pltpu.SemaphoreType.DMA((7,)), pltpu.SemaphoreType.DMA((7,))])(v)


def _same_shape_runs(arrays):
    runs = {}
    for i, a in enumerate(arrays):
        runs.setdefault(a.shape, []).append(i)
    return list(runs.values())


def _per_shape(fn, *lists):
    out = [None] * len(lists[0])
    for idx in _same_shape_runs(lists[0]):
        for i, r in zip(idx, fn(*[[l[i] for i in idx] for l in lists])):
            out[i] = r
    return out


def add_halves(gs, bufs, cidx, *, name):
    cnt = len(gs)
    _, k, n = gs[0].shape

    def body(c_ref, *refs):
        g, b, o = refs[:cnt], refs[cnt:2 * cnt], refs[2 * cnt:]
        for i in range(cnt):
            o[i][...] = (g[i][...].astype(F32) + b[i][...].astype(F32)).astype(BF16)

    blk = pl.BlockSpec((None, k // 2, n), lambda s, c: (s, 0, 0))
    mine = pl.BlockSpec((None, k // 2, n), lambda s, c: (s, c[0], 0))
    return list(pl.pallas_call(
        body, name=name, out_shape=tuple(_sds(b.shape, BF16) for b in bufs),
        grid_spec=pltpu.PrefetchScalarGridSpec(
            num_scalar_prefetch=1, grid=(NSH,), in_specs=[mine] * cnt + [blk] * cnt, out_specs=tuple([blk] * cnt)),
        compiler_params=_params(("parallel",)))(cidx, *gs, *bufs))


def add_chips(sums, bufs, sidx, *, name):
    cnt = len(sums)
    _, kh, n = sums[0].shape

    def body(s_ref, *refs):
        mine, b, o = refs[:cnt], refs[cnt:2 * cnt], refs[2 * cnt:]
        for i in range(cnt):
            o[i][...] = ((mine[i][...].astype(F32) + b[i][0].astype(F32)) + (b[i][1].astype(F32) + b[i][2].astype(F32)))

    own = pl.BlockSpec((None, kh, n), lambda i, s: (s[0], 0, 0))
    got = pl.BlockSpec((3, kh, n), lambda i, s: (0, 0, 0))
    out = pl.BlockSpec((kh, n), lambda i, s: (0, 0))
    return list(pl.pallas_call(
        body, name=name, out_shape=tuple(_sds((kh, n), F32) for _ in sums),
        grid_spec=pltpu.PrefetchScalarGridSpec(
            num_scalar_prefetch=1, grid=(1,), in_specs=[own] * cnt + [got] * cnt, out_specs=tuple([out] * cnt)),
        compiler_params=_params(("arbitrary",)))(sidx, *sums, *bufs))


PARAMS = ("ffn1_norm", "ffn1_w_gate", "ffn1_w_up", "ffn1_w_down", "mix_norm", "w_in", "b_gate", "na_q_norm", "na_k_norm",
          "na_rpb", "sw_q_norm", "sw_k_norm", "sw_sink", "t5_rel_table", "w_branch_na", "w_branch_sw", "w_out", "ffn2_norm",
          "ffn2_w_gate", "ffn2_w_up", "ffn2_w_down")
SMALL_ALL = tuple(n for n in PARAMS if n not in BIG)
TRANSPOSED = ("ffn1_w_gate", "ffn1_w_up", "w_in", "ffn2_w_gate", "ffn2_w_up")
SMALL_ROWS = 152


def _pack_small(vals):
    flat = jnp.concatenate([vals[n].reshape(-1).astype(F32) for n in SMALL_ALL] + [vals["loss"].reshape(-1)])
    return jnp.pad(flat, (0, SMALL_ROWS * 128 - flat.shape[0])).reshape(SMALL_ROWS, 128)


def _unpack_small(packed, like):
    flat, out, off = packed.reshape(-1), {}, 0
    for n in SMALL_ALL:
        size = math.prod(like[n].shape)
        out[n] = flat[off:off + size].reshape(like[n].shape)
        off += size
    out["loss"] = flat[off]
    return out


def kernel(x, ffn1_norm, ffn1_w_gate, ffn1_w_up, ffn1_w_down, mix_norm, w_in, b_gate, na_q_norm, na_k_norm, na_rpb, sw_q_norm, sw_k_norm, sw_sink, t5_rel_table, w_branch_na, w_branch_sw, w_out, ffn2_norm, ffn2_w_gate, ffn2_w_up, ffn2_w_down, loss_target, m_ffn1_norm, m_ffn1_w_gate, m_ffn1_w_up, m_ffn1_w_down, m_mix_norm, m_w_in, m_b_gate, m_na_q_norm, m_na_k_norm, m_na_rpb, m_sw_q_norm, m_sw_k_norm, m_sw_sink, m_t5_rel_table, m_w_branch_na, m_w_branch_sw, m_w_out, m_ffn2_norm, m_ffn2_w_gate, m_ffn2_w_up, m_ffn2_w_down, v_ffn1_norm, v_ffn1_w_gate, v_ffn1_w_up, v_ffn1_w_down, v_mix_norm, v_w_in, v_b_gate, v_na_q_norm, v_na_k_norm, v_na_rpb, v_sw_q_norm, v_sw_k_norm, v_sw_sink, v_t5_rel_table, v_w_branch_na, v_w_branch_sw, v_w_out, v_ffn2_norm, v_ffn2_w_gate, v_ffn2_w_up, v_ffn2_w_down):
    args = locals()
    tr = lambda n, a: jnp.transpose(a, (0, 2, 1)) if n in TRANSPOSED else a
    w = {n: tr(n, args[n]) for n in PARAMS}
    m = {n: tr(n, args["m_" + n]) for n in PARAMS}
    v = {n: tr(n, args["v_" + n]) for n in PARAMS}
    cidx = lax.axis_index("c").astype(jnp.int32).reshape(1)
    sidx = (2 * lax.axis_index("x") + lax.axis_index("y")).astype(jnp.int32).reshape(1)

    small = [{n: w[n][l] for n in SMALL} for l in range(DEPTH)]
    order = ("ffn1", "mix", "ffn2")

    keys = [(l, g) for l in range(DEPTH) for g in order]
    local = lambda l, g: [w[n][l].astype(BF16) for n in GROUPS[g]]
    first = gather_start([local(*keys[0])], name="gather_start")
    rest = gather_start([local(*key) for key in keys[1:3]], first[0][2][0], name="gather_start")
    in_flight = dict(zip(keys[:3], first + rest))
    t5b = t5_bias(w["t5_rel_table"], name="t5_bias")
    for l in range(DEPTH):
        small[l]["na_bias"] = na_bias_table(small[l]["na_rpb"], name="na_bias_table")
    early = [t5b] + [small[l]["na_bias"] for l in range(DEPTH)] + [rest[0][2][0]]

    def weights_of(l):
        def get(group, after):
            send_sems, recv_sems, thru, lands = in_flight[(l, group)]
            after = [after] + (early if (l, group) == keys[0] else [])
            thru, lands = gather_wait(send_sems, recv_sems, thru, lands, after, name="gather_wait")
            if (l, group) == keys[1]:
                in_flight.update(zip(keys[3:], gather_start([local(*key) for key in keys[3:]], thru[0], name="gather_start")))
            return dict(zip(GROUPS[group], gather_finish(thru, lands, name="gather_finish")))
        return get

    h0, saved0 = layer_fwd(x[0], small[0], weights_of(0), t5b)
    (dy, dy_bf, loss_row), saved1 = layer_fwd(h0, small[1], weights_of(1), t5b, target=loss_target[0])

    crossing, tokens, pending = {}, [], []

    def ship(after, then=None):
        key, send_sems, recv_sems, grads, lands = pending.pop()
        grads, from_sibling = pair_exchange_wait(send_sems, recv_sems, grads, lands, after, name="pair_exchange_wait")
        sums = _per_shape(lambda gs, bs: add_halves(gs, bs, cidx, name="add_halves"), grads, from_sibling)
        if then is None:
            send_sems, recv_sems, sums, lands, token = chip_exchange_start(sums, name="chip_exchange_start")
            crossing[key] = (send_sems, recv_sems, sums, lands)
            return token
        crossing[key], pair, token = exchange_start(sums, then[1], name="exchange_start")
        pending.append((then[0],) + pair)
        return token

    def reduce_of(l):
        def emit(group, grads):
            grads = list(grads)
            if pending:
                token = ship([grads[0]], then=((l, group), grads))
            else:
                send_sems, recv_sems, grads, lands, token = pair_exchange_start(grads, name="pair_exchange_start")
                pending.append(((l, group), send_sems, recv_sems, grads, lands))
            tokens.append(token)
            return token
        return emit

    def finish(layer, after, filled=None):
        sent = {}
        for group in order:
            send_sems, recv_sems, sums, lands = crossing[(layer, group)]
            sums, got = chip_exchange_wait(send_sems, recv_sems, sums, lands, after, name="chip_exchange_wait")
            halves = _per_shape(lambda ss, bs: add_chips(ss, bs, sidx, name="add_chips"), sums, got)
            sent[group] = pair_send_start(halves, name="pair_send_start")
            after = [sent[group][4]]
        out = {}
        for group in order:
            send_sems, recv_sems, halves, lands, _ = sent[group]
            halves, theirs = pair_send_wait(send_sems, recv_sems, halves, lands, after, name="pair_send_wait")
            names = GROUPS[group]
            res = _per_shape(
                lambda ws, ms, vs, a, b, *f: adamw_layer(ws, ms, vs, a, b, cidx, layer, list(f[0]) if f else None, name="adamw_layer"),
                *([[w[n] for n in names], [m[n] for n in names], [v[n] for n in names], halves, theirs]
                  + ([[filled[n] for n in names]] if filled is not None else [])))
            out.update(zip(names, res))
            after = [res[-1][0]]
        return out

    dy, dy_bf, small1, dt5_1 = layer_bwd(dy, dy_bf, saved1, small[1], t5b, reduce_of(1))
    grad_x, _, small0, dt5_0 = layer_bwd(dy, dy_bf, saved0, small[0], t5b, reduce_of(0), dep=tokens[-1])
    done1 = finish(1, [ship([grad_x])])

    smalls = [small0, small1]
    dt5 = t5_table_grad(dt5_0, dt5_1, name="t5_table_grad").reshape(32, 8)
    local_small = {n: jnp.stack([smalls[l][n].reshape(w[n].shape[1:]) for l in range(DEPTH)]) for n in SMALL}
    local_small["t5_rel_table"] = dt5
    local_small["loss"] = loss_row[0, 0:1]
    total = allreduce_small(_pack_small(local_small), name="allreduce_small")
    small_grads = _unpack_small(total, w)
    small_done = adamw_small([w[n] for n in SMALL_ALL], [small_grads[n] for n in SMALL_ALL], [m[n] for n in SMALL_ALL],
                             [v[n] for n in SMALL_ALL], name="adamw_small")

    grad, delta, new_m, new_v = {}, {}, {}, {}
    for n, done in finish(0, [small_done[0][0], done1[BIG[-1]][0]], filled=done1).items():
        grad[n], delta[n], new_m[n], new_v[n] = done
    for n, done in zip(SMALL_ALL, small_done):
        grad[n] = small_grads[n]
        delta[n], new_m[n], new_v[n] = done

    return (small_grads["loss"], grad_x[None], *[tr(n, grad[n]) for n in PARAMS], *[tr(n, delta[n]) for n in PARAMS],
            *[tr(n, new_m[n]) for n in PARAMS], *[tr(n, new_v[n]) for n in PARAMS])
```
